```python
import math
import jax, jax.numpy as jnp
from jax import lax
import numpy as np

D_MODEL = 2048
BATCH = 8
SEQ = 2048
DEPTH = 1

HEAD_DIM = 128
A_Q_HEADS = 8
A_KV_HEADS = 2
A_GROUP = A_Q_HEADS // A_KV_HEADS
A_HALF_WINDOW = 128
A_BLOCK = 128
B_PATTERNS = ((128, 1), (512, 4), (2048, 16))
B_N_GROUPS = len(B_PATTERNS)
B_HEADS_PER_GROUP = 4
B_HEADS = B_N_GROUPS * B_HEADS_PER_GROUP
B_BLOCK = 64
N_BUCKETS = 32
MAX_DISTANCE = 1024
N_BIAS_HEADS = A_Q_HEADS + B_HEADS
A_Q_W = A_Q_HEADS * HEAD_DIM
A_KV_W = A_KV_HEADS * HEAD_DIM
B_W = B_HEADS * HEAD_DIM
B_OUT_W = B_HEADS_PER_GROUP * HEAD_DIM
IN_PROJ_W = A_Q_W + 2 * A_KV_W + 3 * B_W + 2 * D_MODEL
D_FF = 11 * D_MODEL // 4
CONV_WIDTH = 3
PLE_DIM = 256
RMS_EPS = 1e-6
NEG_INF = -1e30

kernel_name = "hybrid_gated_window_dilated_encoder"


def rms_norm(x, g):
    xf = x.astype(jnp.float32)
    y = xf * lax.rsqrt(jnp.mean(xf * xf, axis=-1, keepdims=True) + RMS_EPS)
    return (y * g.astype(jnp.float32)).astype(x.dtype)


def t5_bucket(rel):
    half = N_BUCKETS // 2
    max_exact = half // 2
    n = jnp.abs(rel)
    side = jnp.where(rel > 0, half, 0)
    nf = jnp.maximum(n, 1).astype(jnp.float32)
    large = max_exact + (jnp.log(nf / max_exact) / math.log(MAX_DISTANCE / max_exact)
                         * (half - max_exact)).astype(jnp.int32)
    large = jnp.minimum(large, half - 1)
    return side + jnp.where(n < max_exact, n, large)


def band_bias(table, half_w, blk, dilation):
    rel = (jnp.arange(blk + 2 * half_w)[None, :] - half_w) - jnp.arange(blk)[:, None]
    b = table[t5_bucket(rel * dilation)]
    return jnp.transpose(b, (2, 0, 1))


def banded_attention(q, k, v, bias, half_w, blk, sink=None):
    n, hkv, g, L, hd = q.shape
    nb = -(-L // blk)
    lp = nb * blk
    kw = blk + 2 * half_w
    q = jnp.pad(q, ((0, 0), (0, 0), (0, 0), (0, lp - L), (0, 0)))
    pad_kv = ((0, 0), (0, 0), (half_w, lp - L + half_w), (0, 0))
    k = jnp.pad(k, pad_kv)
    v = jnp.pad(v, pad_kv)
    kidx = jnp.arange(nb)[:, None] * blk + jnp.arange(kw)[None, :]
    kb = jnp.take(k, kidx, axis=2)
    vb = jnp.take(v, kidx, axis=2)
    qb = q.reshape(n, hkv, g, nb, blk, hd)
    s = jnp.einsum('nhgbqd,nhbkd->nhgbqk', qb, kb).astype(jnp.float32) * (hd ** -0.5)
    s = s + bias[None, :, :, None].astype(jnp.float32)
    kpos = (kidx - half_w)[:, None, :]
    qpos = (jnp.arange(nb)[:, None] * blk + jnp.arange(blk)[None, :])[:, :, None]
    valid = (jnp.abs(kpos - qpos) <= half_w) & (kpos >= 0) & (kpos < L)
    s = jnp.where(valid, s, NEG_INF)
    m = jnp.max(s, axis=-1)
    if sink is not None:
        sk = sink.astype(jnp.float32)[None, :, :, None, None]
        m = jnp.maximum(m, sk)
    pexp = jnp.exp(s - m[..., None])
    denom = jnp.sum(pexp, axis=-1)
    if sink is not None:
        denom = denom + jnp.exp(sk - m)
    out = jnp.einsum('nhgbqk,nhbkd->nhgbqd', pexp.astype(v.dtype), vb).astype(jnp.float32)
    out = (out / denom[..., None]).astype(v.dtype)
    lse = m + jnp.log(denom)
    out = out.reshape(n, hkv, g, lp, hd)[:, :, :, :L]
    lse = lse.reshape(n, hkv, g, lp)[:, :, :, :L]
    return out, lse


def windowed_gqa(q, k, v, table, sink):
    b, s, _ = q.shape
    q = q.reshape(b, s, A_KV_HEADS, A_GROUP, HEAD_DIM).transpose(0, 2, 3, 1, 4)
    k = k.reshape(b, s, A_KV_HEADS, HEAD_DIM).transpose(0, 2, 1, 3)
    v = v.reshape(b, s, A_KV_HEADS, HEAD_DIM).transpose(0, 2, 1, 3)
    bias = band_bias(table[:, :A_Q_HEADS], A_HALF_WINDOW, A_BLOCK, 1).reshape(
        A_KV_HEADS, A_GROUP, A_BLOCK, A_BLOCK + 2 * A_HALF_WINDOW)
    out, _ = banded_attention(q, k, v, bias, A_HALF_WINDOW, A_BLOCK,
                              sink.reshape(A_KV_HEADS, A_GROUP))
    return out.transpose(0, 3, 1, 2, 4).reshape(b, s, A_Q_W)


def to_residue(t, dil):
    b, s, h, hd = t.shape
    return t.reshape(b, s // dil, dil, h, hd).transpose(0, 2, 3, 1, 4).reshape(b * dil, h, s // dil, hd)


def dilated_attention(q, k, v, table):
    b, s, _ = q.shape
    hg = B_HEADS_PER_GROUP
    qg = q.reshape(b, s, B_N_GROUPS, hg, HEAD_DIM)
    kg = k.reshape(b, s, B_N_GROUPS, hg, HEAD_DIM)
    vg = v.reshape(b, s, B_N_GROUPS, hg, HEAD_DIM)
    outs = []
    lses = []
    for gi, (window, dil) in enumerate(B_PATTERNS):
        half = window // (2 * dil)
        L = s // dil
        h0 = A_Q_HEADS + gi * hg
        bias = band_bias(table[:, h0:h0 + hg], half, B_BLOCK, dil)[:, None]
        o, lse = banded_attention(to_residue(qg[:, :, gi], dil)[:, :, None],
                                  to_residue(kg[:, :, gi], dil),
                                  to_residue(vg[:, :, gi], dil), bias, half, B_BLOCK)
        outs.append(o[:, :, 0].reshape(b, dil, hg, L, HEAD_DIM).transpose(0, 3, 1, 2, 4).reshape(b, s, hg, HEAD_DIM))
        lses.append(lse[:, :, 0].reshape(b, dil, hg, L).transpose(0, 3, 1, 2).reshape(b, s, hg))
    alpha = jax.nn.softmax(jnp.stack(lses, axis=0), axis=0)
    y = jnp.sum(alpha[..., None] * jnp.stack(outs, axis=0).astype(jnp.float32), axis=0)
    return y.astype(q.dtype).reshape(b, s, B_OUT_W)


def dwconv_centred(t, w, bias):
    pad = CONV_WIDTH // 2
    s = t.shape[1]
    tp = jnp.pad(t, ((0, 0), (pad, pad), (0, 0)))
    acc = tp[:, 0:s] * w[0]
    for j in range(1, CONV_WIDTH):
        acc = acc + tp[:, j:j + s] * w[j]
    return acc + bias


def _fwd_setup_inputs(seed: int = 0) -> dict:
    key = jax.random.key(seed)
    ks = jax.random.split(key, 20)
    f32 = jnp.float32

    def nrm(k, shape, scale):
        return jax.random.normal(k, shape, f32) * scale

    def gain(k, shape):
        return 1.0 + 0.05 * jax.random.normal(k, shape, f32)

    return {
        "x": nrm(ks[0], (BATCH, SEQ, D_MODEL), 1.0),
        "p": nrm(ks[1], (DEPTH, BATCH, SEQ, PLE_DIM), 1.0),
        "rel_bias_table": nrm(ks[2], (N_BUCKETS, N_BIAS_HEADS), 0.5),
        "attn_norm": gain(ks[3], (DEPTH, D_MODEL)),
        "w_in": nrm(ks[4], (DEPTH, D_MODEL, IN_PROJ_W), D_MODEL ** -0.5),
        "sink_a": nrm(ks[5], (DEPTH, A_Q_HEADS), 0.5),
        "w_branch_a": nrm(ks[6], (DEPTH, A_Q_W, D_MODEL), A_Q_W ** -0.5),
        "w_branch_b": nrm(ks[7], (DEPTH, B_OUT_W, D_MODEL), B_OUT_W ** -0.5),
        "w_out": nrm(ks[8], (DEPTH, D_MODEL, D_MODEL), D_MODEL ** -0.5),
        "ffn_norm": gain(ks[9], (DEPTH, D_MODEL)),
        "w_ffn_gate": nrm(ks[10], (DEPTH, D_MODEL, D_FF), D_MODEL ** -0.5),
        "w_ffn_up": nrm(ks[11], (DEPTH, D_MODEL, D_FF), D_MODEL ** -0.5),
        "conv_w": nrm(ks[12], (DEPTH, CONV_WIDTH, D_FF), CONV_WIDTH ** -0.5),
        "conv_b": nrm(ks[13], (DEPTH, D_FF), 0.02),
        "w_ffn_down": nrm(ks[14], (DEPTH, D_FF, D_MODEL), D_FF ** -0.5),
        "ple_norm": gain(ks[15], (DEPTH, D_MODEL)),
        "w_ple_gate": nrm(ks[16], (DEPTH, D_MODEL, D_MODEL), D_MODEL ** -0.5),
        "w_ple_proj": nrm(ks[17], (DEPTH, PLE_DIM, D_MODEL), PLE_DIM ** -0.5),
        "final_norm": gain(ks[18], (D_MODEL,)),
    }


def _fwd_reference(x, p, rel_bias_table, attn_norm, w_in, sink_a, w_branch_a, w_branch_b, w_out,
              ffn_norm, w_ffn_gate, w_ffn_up, conv_w, conv_b, w_ffn_down,
              ple_norm, w_ple_gate, w_ple_proj, final_norm):
    split_points = np.cumsum([A_Q_W, A_KV_W, A_KV_W, B_W, B_W, B_W, D_MODEL]).tolist()
    for i in range(DEPTH):
        h = rms_norm(x, attn_norm[i])
        proj = h @ w_in[i]
        qa, ka, va, qb, kb, vb, ga, gb = jnp.split(proj, split_points, axis=-1)
        ya = windowed_gqa(qa, ka, va, rel_bias_table, sink_a[i])
        yb = dilated_attention(qb, kb, vb, rel_bias_table)
        merged = jax.nn.sigmoid(ga) * (ya @ w_branch_a[i]) + jax.nn.sigmoid(gb) * (yb @ w_branch_b[i])
        x = x + merged @ w_out[i]
        hf = rms_norm(x, ffn_norm[i])
        g = dwconv_centred(hf @ w_ffn_gate[i], conv_w[i], conv_b[i])
        x = x + (jax.nn.gelu(g) * (hf @ w_ffn_up[i])) @ w_ffn_down[i]
        gate_p = jax.nn.sigmoid(rms_norm(x, ple_norm[i]) @ w_ple_gate[i])
        x = x + gate_p * (p[i] @ w_ple_proj[i])
    return rms_norm(x, final_norm)


import jax as _jax
import jax.numpy as _jnp

TWIN_FORMAT = 'train_step'
FWD_PARAMS = ['x', 'p', 'rel_bias_table', 'attn_norm', 'w_in', 'sink_a', 'w_branch_a', 'w_branch_b', 'w_out', 'ffn_norm', 'w_ffn_gate', 'w_ffn_up', 'conv_w', 'conv_b', 'w_ffn_down', 'ple_norm', 'w_ple_gate', 'w_ple_proj', 'final_norm']
TWIN_WEIGHTS = ['rel_bias_table', 'attn_norm', 'w_in', 'sink_a', 'w_branch_a', 'w_branch_b', 'w_out', 'ffn_norm', 'w_ffn_gate', 'w_ffn_up', 'conv_w', 'conv_b', 'w_ffn_down', 'ple_norm', 'w_ple_gate', 'w_ple_proj', 'final_norm']
TWIN_DIFF_INPUT = 'x'
TWIN_INPUTS = ['x', 'p', 'rel_bias_table', 'attn_norm', 'w_in', 'sink_a', 'w_branch_a', 'w_branch_b', 'w_out', 'ffn_norm', 'w_ffn_gate', 'w_ffn_up', 'conv_w', 'conv_b', 'w_ffn_down', 'ple_norm', 'w_ple_gate', 'w_ple_proj', 'final_norm', 'loss_target', 'm_rel_bias_table', 'm_attn_norm', 'm_w_in', 'm_sink_a', 'm_w_branch_a', 'm_w_branch_b', 'm_w_out', 'm_ffn_norm', 'm_w_ffn_gate', 'm_w_ffn_up', 'm_conv_w', 'm_conv_b', 'm_w_ffn_down', 'm_ple_norm', 'm_w_ple_gate', 'm_w_ple_proj', 'm_final_norm', 'v_rel_bias_table', 'v_attn_norm', 'v_w_in', 'v_sink_a', 'v_w_branch_a', 'v_w_branch_b', 'v_w_out', 'v_ffn_norm', 'v_w_ffn_gate', 'v_w_ffn_up', 'v_conv_w', 'v_conv_b', 'v_w_ffn_down', 'v_ple_norm', 'v_w_ple_gate', 'v_w_ple_proj', 'v_final_norm']
TWIN_OUTPUTS = ['loss', 'grad_x', 'grad_rel_bias_table', 'grad_attn_norm', 'grad_w_in', 'grad_sink_a', 'grad_w_branch_a', 'grad_w_branch_b', 'grad_w_out', 'grad_ffn_norm', 'grad_w_ffn_gate', 'grad_w_ffn_up', 'grad_conv_w', 'grad_conv_b', 'grad_w_ffn_down', 'grad_ple_norm', 'grad_w_ple_gate', 'grad_w_ple_proj', 'grad_final_norm', 'delta_rel_bias_table', 'delta_attn_norm', 'delta_w_in', 'delta_sink_a', 'delta_w_branch_a', 'delta_w_branch_b', 'delta_w_out', 'delta_ffn_norm', 'delta_w_ffn_gate', 'delta_w_ffn_up', 'delta_conv_w', 'delta_conv_b', 'delta_w_ffn_down', 'delta_ple_norm', 'delta_w_ple_gate', 'delta_w_ple_proj', 'delta_final_norm', 'new_m_rel_bias_table', 'new_m_attn_norm', 'new_m_w_in', 'new_m_sink_a', 'new_m_w_branch_a', 'new_m_w_branch_b', 'new_m_w_out', 'new_m_ffn_norm', 'new_m_w_ffn_gate', 'new_m_w_ffn_up', 'new_m_conv_w', 'new_m_conv_b', 'new_m_w_ffn_down', 'new_m_ple_norm', 'new_m_w_ple_gate', 'new_m_w_ple_proj', 'new_m_final_norm', 'new_v_rel_bias_table', 'new_v_attn_norm', 'new_v_w_in', 'new_v_sink_a', 'new_v_w_branch_a', 'new_v_w_branch_b', 'new_v_w_out', 'new_v_ffn_norm', 'new_v_w_ffn_gate', 'new_v_w_ffn_up', 'new_v_conv_w', 'new_v_conv_b', 'new_v_w_ffn_down', 'new_v_ple_norm', 'new_v_w_ple_gate', 'new_v_w_ple_proj', 'new_v_final_norm']
TWIN_LEAF_KINDS = {'loss': 'loss', 'grad_x': 'grad_x', 'grad_rel_bias_table': 'grad_w', 'grad_attn_norm': 'grad_w', 'grad_w_in': 'grad_w', 'grad_sink_a': 'grad_w', 'grad_w_branch_a': 'grad_w', 'grad_w_branch_b': 'grad_w', 'grad_w_out': 'grad_w', 'grad_ffn_norm': 'grad_w', 'grad_w_ffn_gate': 'grad_w', 'grad_w_ffn_up': 'grad_w', 'grad_conv_w': 'grad_w', 'grad_conv_b': 'grad_w', 'grad_w_ffn_down': 'grad_w', 'grad_ple_norm': 'grad_w', 'grad_w_ple_gate': 'grad_w', 'grad_w_ple_proj': 'grad_w', 'grad_final_norm': 'grad_w', 'delta_rel_bias_table': 'delta_w', 'delta_attn_norm': 'delta_w', 'delta_w_in': 'delta_w', 'delta_sink_a': 'delta_w', 'delta_w_branch_a': 'delta_w', 'delta_w_branch_b': 'delta_w', 'delta_w_out': 'delta_w', 'delta_ffn_norm': 'delta_w', 'delta_w_ffn_gate': 'delta_w', 'delta_w_ffn_up': 'delta_w', 'delta_conv_w': 'delta_w', 'delta_conv_b': 'delta_w', 'delta_w_ffn_down': 'delta_w', 'delta_ple_norm': 'delta_w', 'delta_w_ple_gate': 'delta_w', 'delta_w_ple_proj': 'delta_w', 'delta_final_norm': 'delta_w', 'new_m_rel_bias_table': 'new_m', 'new_m_attn_norm': 'new_m', 'new_m_w_in': 'new_m', 'new_m_sink_a': 'new_m', 'new_m_w_branch_a': 'new_m', 'new_m_w_branch_b': 'new_m', 'new_m_w_out': 'new_m', 'new_m_ffn_norm': 'new_m', 'new_m_w_ffn_gate': 'new_m', 'new_m_w_ffn_up': 'new_m', 'new_m_conv_w': 'new_m', 'new_m_conv_b': 'new_m', 'new_m_w_ffn_down': 'new_m', 'new_m_ple_norm': 'new_m', 'new_m_w_ple_gate': 'new_m', 'new_m_w_ple_proj': 'new_m', 'new_m_final_norm': 'new_m', 'new_v_rel_bias_table': 'new_v', 'new_v_attn_norm': 'new_v', 'new_v_w_in': 'new_v', 'new_v_sink_a': 'new_v', 'new_v_w_branch_a': 'new_v', 'new_v_w_branch_b': 'new_v', 'new_v_w_out': 'new_v', 'new_v_ffn_norm': 'new_v', 'new_v_w_ffn_gate': 'new_v', 'new_v_w_ffn_up': 'new_v', 'new_v_conv_w': 'new_v', 'new_v_conv_b': 'new_v', 'new_v_w_ffn_down': 'new_v', 'new_v_ple_norm': 'new_v', 'new_v_w_ple_gate': 'new_v', 'new_v_w_ple_proj': 'new_v', 'new_v_final_norm': 'new_v'}


def _forward(args):
    return _fwd_reference(*[args[k] for k in FWD_PARAMS])


def _output_shape():
    out = _jax.eval_shape(lambda: _forward(_fwd_setup_inputs(0)))
    return out.shape, out.dtype

N_MICROBATCH = 1
ADAM_LR = 0.001
ADAM_B1 = 0.9
ADAM_B2 = 0.999
ADAM_EPS = 1e-08
ADAM_WD = 0.01
ADAM_STEP = 10
PER_EXAMPLE_BATCH_AXIS = {'x': 0, 'p': 1, 'loss_target': 0}
SHARED_INPUTS = []
_WEIGHT_DTYPES = {'rel_bias_table': _jnp.float32, 'attn_norm': _jnp.float32, 'w_in': _jnp.float32, 'sink_a': _jnp.float32, 'w_branch_a': _jnp.float32, 'w_branch_b': _jnp.float32, 'w_out': _jnp.float32, 'ffn_norm': _jnp.float32, 'w_ffn_gate': _jnp.float32, 'w_ffn_up': _jnp.float32, 'conv_w': _jnp.float32, 'conv_b': _jnp.float32, 'w_ffn_down': _jnp.float32, 'ple_norm': _jnp.float32, 'w_ple_gate': _jnp.float32, 'w_ple_proj': _jnp.float32, 'final_norm': _jnp.float32}
MOMENT_SCALE = {'rel_bias_table': 9.827551e-03, 'attn_norm': 1.191415e-02, 'w_in': 5.286098e-03, 'sink_a': 3.440844e-04, 'w_branch_a': 4.661827e-03, 'w_branch_b': 4.889099e-03, 'w_out': 6.772078e-03, 'ffn_norm': 4.721020e-02, 'w_ffn_gate': 1.989115e-02, 'w_ffn_up': 1.935309e-02, 'conv_w': 1.999633e-02, 'conv_b': 1.948111e-02, 'w_ffn_down': 3.219132e-02, 'ple_norm': 1.012448e-02, 'w_ple_gate': 1.015590e-02, 'w_ple_proj': 2.650807e-02, 'final_norm': 8.011816e+00}


def _to_microbatches(a, axis):
    t = _jnp.moveaxis(a, axis, 0)
    t = t.reshape((N_MICROBATCH, t.shape[0] // N_MICROBATCH) + t.shape[1:])
    return _jnp.moveaxis(t, 1, axis + 1)


def setup_inputs(seed: int = 0) -> dict:
    inp = _fwd_setup_inputs(seed)
    key = _jax.random.fold_in(_jax.random.key(seed), 7919)
    shape, _ = _output_shape()
    out = dict(inp)
    out["loss_target"] = _jax.random.normal(_jax.random.fold_in(key, 0), shape, _jnp.float32)
    for i, name in enumerate(TWIN_WEIGHTS):
        w = inp[name].astype(_jnp.float32)
        if MOMENT_SCALE is None:
            s = _jnp.sqrt(_jnp.mean(_jnp.square(w)) + 1e-30)
        else:
            s = MOMENT_SCALE[name]
        km, kv = _jax.random.split(_jax.random.fold_in(key, i + 1))
        out[name] = w
        out["m_" + name] = s * _jax.random.normal(km, w.shape, _jnp.float32)
        out["v_" + name] = (s * s) * _jax.random.uniform(kv, w.shape, _jnp.float32, 0.5, 1.5)
    if N_MICROBATCH > 1:
        for name, axis in PER_EXAMPLE_BATCH_AXIS.items():
            out[name] = _to_microbatches(out[name], axis)
    return {'x': out['x'], 'p': out['p'], 'rel_bias_table': out['rel_bias_table'], 'attn_norm': out['attn_norm'], 'w_in': out['w_in'], 'sink_a': out['sink_a'], 'w_branch_a': out['w_branch_a'], 'w_branch_b': out['w_branch_b'], 'w_out': out['w_out'], 'ffn_norm': out['ffn_norm'], 'w_ffn_gate': out['w_ffn_gate'], 'w_ffn_up': out['w_ffn_up'], 'conv_w': out['conv_w'], 'conv_b': out['conv_b'], 'w_ffn_down': out['w_ffn_down'], 'ple_norm': out['ple_norm'], 'w_ple_gate': out['w_ple_gate'], 'w_ple_proj': out['w_ple_proj'], 'final_norm': out['final_norm'], 'loss_target': out['loss_target'], 'm_rel_bias_table': out['m_rel_bias_table'], 'm_attn_norm': out['m_attn_norm'], 'm_w_in': out['m_w_in'], 'm_sink_a': out['m_sink_a'], 'm_w_branch_a': out['m_w_branch_a'], 'm_w_branch_b': out['m_w_branch_b'], 'm_w_out': out['m_w_out'], 'm_ffn_norm': out['m_ffn_norm'], 'm_w_ffn_gate': out['m_w_ffn_gate'], 'm_w_ffn_up': out['m_w_ffn_up'], 'm_conv_w': out['m_conv_w'], 'm_conv_b': out['m_conv_b'], 'm_w_ffn_down': out['m_w_ffn_down'], 'm_ple_norm': out['m_ple_norm'], 'm_w_ple_gate': out['m_w_ple_gate'], 'm_w_ple_proj': out['m_w_ple_proj'], 'm_final_norm': out['m_final_norm'], 'v_rel_bias_table': out['v_rel_bias_table'], 'v_attn_norm': out['v_attn_norm'], 'v_w_in': out['v_w_in'], 'v_sink_a': out['v_sink_a'], 'v_w_branch_a': out['v_w_branch_a'], 'v_w_branch_b': out['v_w_branch_b'], 'v_w_out': out['v_w_out'], 'v_ffn_norm': out['v_ffn_norm'], 'v_w_ffn_gate': out['v_w_ffn_gate'], 'v_w_ffn_up': out['v_w_ffn_up'], 'v_conv_w': out['v_conv_w'], 'v_conv_b': out['v_conv_b'], 'v_w_ffn_down': out['v_w_ffn_down'], 'v_ple_norm': out['v_ple_norm'], 'v_w_ple_gate': out['v_w_ple_gate'], 'v_w_ple_proj': out['v_w_ple_proj'], 'v_final_norm': out['v_final_norm']}


def _loss(weights, diff, rest, loss_target):
    with _jax.named_scope("forward"):
        args = {**rest, TWIN_DIFF_INPUT: diff, **{k: w.astype(_WEIGHT_DTYPES[k]) for k, w in weights.items()}}
        y = _forward(args)
    with _jax.named_scope("loss_head"):
        err = _jnp.square(y.astype(_jnp.float32) - loss_target)
        return 0.5 * _jnp.sum(_jnp.mean(err, axis=-1)) if err.ndim else 0.5 * err


def _adamw(w, g, m, v):
    m = ADAM_B1 * m + (1.0 - ADAM_B1) * g
    v = ADAM_B2 * v + (1.0 - ADAM_B2) * _jnp.square(g)
    m_hat = m / (1.0 - ADAM_B1 ** ADAM_STEP)
    v_hat = v / (1.0 - ADAM_B2 ** ADAM_STEP)
    delta = -ADAM_LR * (m_hat / (_jnp.sqrt(v_hat) + ADAM_EPS) + ADAM_WD * w)
    return delta, m, v


def reference(x, p, rel_bias_table, attn_norm, w_in, sink_a, w_branch_a, w_branch_b, w_out, ffn_norm, w_ffn_gate, w_ffn_up, conv_w, conv_b, w_ffn_down, ple_norm, w_ple_gate, w_ple_proj, final_norm, loss_target, m_rel_bias_table, m_attn_norm, m_w_in, m_sink_a, m_w_branch_a, m_w_branch_b, m_w_out, m_ffn_norm, m_w_ffn_gate, m_w_ffn_up, m_conv_w, m_conv_b, m_w_ffn_down, m_ple_norm, m_w_ple_gate, m_w_ple_proj, m_final_norm, v_rel_bias_table, v_attn_norm, v_w_in, v_sink_a, v_w_branch_a, v_w_branch_b, v_w_out, v_ffn_norm, v_w_ffn_gate, v_w_ffn_up, v_conv_w, v_conv_b, v_w_ffn_down, v_ple_norm, v_w_ple_gate, v_w_ple_proj, v_final_norm):
    given = dict(x=x, p=p, rel_bias_table=rel_bias_table, attn_norm=attn_norm, w_in=w_in, sink_a=sink_a, w_branch_a=w_branch_a, w_branch_b=w_branch_b, w_out=w_out, ffn_norm=ffn_norm, w_ffn_gate=w_ffn_gate, w_ffn_up=w_ffn_up, conv_w=conv_w, conv_b=conv_b, w_ffn_down=w_ffn_down, ple_norm=ple_norm, w_ple_gate=w_ple_gate, w_ple_proj=w_ple_proj, final_norm=final_norm, loss_target=loss_target, m_rel_bias_table=m_rel_bias_table, m_attn_norm=m_attn_norm, m_w_in=m_w_in, m_sink_a=m_sink_a, m_w_branch_a=m_w_branch_a, m_w_branch_b=m_w_branch_b, m_w_out=m_w_out, m_ffn_norm=m_ffn_norm, m_w_ffn_gate=m_w_ffn_gate, m_w_ffn_up=m_w_ffn_up, m_conv_w=m_conv_w, m_conv_b=m_conv_b, m_w_ffn_down=m_w_ffn_down, m_ple_norm=m_ple_norm, m_w_ple_gate=m_w_ple_gate, m_w_ple_proj=m_w_ple_proj, m_final_norm=m_final_norm, v_rel_bias_table=v_rel_bias_table, v_attn_norm=v_attn_norm, v_w_in=v_w_in, v_sink_a=v_sink_a, v_w_branch_a=v_w_branch_a, v_w_branch_b=v_w_branch_b, v_w_out=v_w_out, v_ffn_norm=v_ffn_norm, v_w_ffn_gate=v_w_ffn_gate, v_w_ffn_up=v_w_ffn_up, v_conv_w=v_conv_w, v_conv_b=v_conv_b, v_w_ffn_down=v_w_ffn_down, v_ple_norm=v_ple_norm, v_w_ple_gate=v_w_ple_gate, v_w_ple_proj=v_w_ple_proj, v_final_norm=v_final_norm)
    weights = {n: given[n] for n in TWIN_WEIGHTS}
    shared = {n: given[n] for n in SHARED_INPUTS}
    per_example = {n: given[n] for n in ['x', 'p']}
    grad_fn = _jax.value_and_grad(_loss, argnums=(0, 1))

    def one_microbatch(ex, loss_target):
        ex = dict(ex)
        diff = ex.pop(TWIN_DIFF_INPUT)
        return grad_fn(weights, diff, {**shared, **ex}, loss_target)

    if N_MICROBATCH == 1:
        loss, (grad_w, grad_x) = one_microbatch(per_example, given["loss_target"])
    else:
        def body(carry, xs):
            loss_sum, grad_sum = carry
            l_k, (gw_k, gx_k) = one_microbatch(xs[0], xs[1])
            with _jax.named_scope("update"):
                return (loss_sum + l_k, _jax.tree.map(_jnp.add, grad_sum, gw_k)), gx_k

        init = (_jnp.zeros((), _jnp.float32), _jax.tree.map(_jnp.zeros_like, weights))
        (loss, grad_w), grad_x = _jax.lax.scan(body, init, (per_example, given["loss_target"]))
    with _jax.named_scope("update"):
        delta_w, new_m, new_v = {}, {}, {}
        for n in TWIN_WEIGHTS:
            delta_w[n], new_m[n], new_v[n] = _adamw(weights[n], grad_w[n], given["m_" + n], given["v_" + n])
    return (loss, grad_x, *[grad_w[n] for n in TWIN_WEIGHTS], *[delta_w[n] for n in TWIN_WEIGHTS],
            *[new_m[n] for n in TWIN_WEIGHTS], *[new_v[n] for n in TWIN_WEIGHTS])
```

```python
import math

import jax
import jax.numpy as jnp
from jax import lax
from jax.experimental import pallas as pl
from jax.experimental.pallas import tpu as pltpu

F32 = jnp.float32
BF16 = jnp.bfloat16
MESH = pl.DeviceIdType.MESH
N_DEV = 8

HEAD_DIM = 128
A_Q_HEADS = 8
A_KV_HEADS = 2
A_GROUP = A_Q_HEADS // A_KV_HEADS
A_BLOCK = 128
B_PATTERNS = ((128, 1), (512, 4), (2048, 16))
B_HEADS_PER_GROUP = 4
B_HEADS = len(B_PATTERNS) * B_HEADS_PER_GROUP
B_BLOCK = 64
N_BUCKETS = 32
MAX_DISTANCE = 1024
A_Q_W = A_Q_HEADS * HEAD_DIM
A_KV_W = A_KV_HEADS * HEAD_DIM
B_W = B_HEADS * HEAD_DIM
B_OUT_W = B_HEADS_PER_GROUP * HEAD_DIM
COL_QA = 0
COL_KA = COL_QA + A_Q_W
COL_VA = COL_KA + A_KV_W
COL_QB = COL_VA + A_KV_W
COL_KB = COL_QB + B_W
COL_VB = COL_KB + B_W
COL_GATES = COL_VB + B_W
RMS_EPS = 1e-6
NEG_INF = -1e30
ATTN_SCALE = HEAD_DIM ** -0.5

ADAM_LR = 0.001
ADAM_B1 = 0.9
ADAM_B2 = 0.999
ADAM_EPS = 1e-08
ADAM_WD = 0.01
ADAM_STEP = 10

GELU_C = math.sqrt(2.0 / math.pi)
GELU_A = 0.044715

V7X_VMEM_BYTES = 64 * 1024 * 1024
VMEM_CEILING = V7X_VMEM_BYTES - 8 * 1024 * 1024
LANES = 128
SUBLANES = 8


def _pick(n, cands):
    for c in cands:
        if n % c == 0:
            return c
    return n


def _nbytes(shape, dtype):
    n = 1
    for d in shape:
        if d is not None:
            n *= d
    return n * jnp.dtype(dtype).itemsize


def _params(sem, est_bytes):
    limit = int(min(VMEM_CEILING, max(32 * 1024 * 1024, 2 * est_bytes + (8 << 20))))
    return pltpu.CompilerParams(dimension_semantics=sem, vmem_limit_bytes=limit)


def _mm(name, a, b, a_bs, a_im, b_bs, b_im, out_shape, out_dtype, o_bs, o_im, grid, dims,
        res=None, r_bs=None, r_im=None):
    nk = grid[-1]
    nax = len(grid)
    has_res = res is not None
    o_tile = tuple(d for d in o_bs if d is not None)

    def body(*refs):
        if has_res:
            a_ref, b_ref, r_ref, o_ref = refs[:4]
            rest = refs[4:]
        else:
            a_ref, b_ref, o_ref = refs[:3]
            r_ref = None
            rest = refs[3:]

        def prod():
            return lax.dot_general(a_ref[...].astype(BF16), b_ref[...].astype(BF16), (dims, ((), ())),
                                   preferred_element_type=F32)

        def finish(r):
            if r_ref is not None:
                r = r + r_ref[...].astype(F32)
            o_ref[...] = r.astype(o_ref.dtype)

        if nk == 1:
            finish(prod())
        else:
            acc = rest[0]
            k = pl.program_id(nax - 1)

            @pl.when(k == 0)
            def _():
                acc[...] = prod()

            @pl.when(k > 0)
            def _():
                acc[...] += prod()

            @pl.when(k == nk - 1)
            def _():
                finish(acc[...])

    in_specs = [pl.BlockSpec(a_bs, a_im), pl.BlockSpec(b_bs, b_im)]
    args = [a, b]
    est = _nbytes(a_bs, a.dtype) + _nbytes(b_bs, b.dtype) + _nbytes(o_bs, out_dtype) + 2 * _nbytes(o_tile, F32)
    if has_res:
        in_specs.append(pl.BlockSpec(r_bs, r_im))
        args.append(res)
        est += _nbytes(r_bs, res.dtype)
    scratch = [] if nk == 1 else [pltpu.VMEM(o_tile, F32)]
    sem = ("parallel",) * (nax - 1) + ("arbitrary",)
    return pl.pallas_call(
        body, name=name, grid=grid, in_specs=in_specs, out_specs=pl.BlockSpec(o_bs, o_im),
        out_shape=jax.ShapeDtypeStruct(out_shape, out_dtype), scratch_shapes=scratch,
        compiler_params=_params(sem, est))(*args)


TM_CANDS = (1024, 512, 256, 128, 64, 32, 16, 8)
TK_CANDS = (1024, 512, 256, 128)
TN_CANDS = (1024, 512, 256, 128)


def mm_cols(name, a, wg, out_dtype, fold):
    m, k = a.shape
    nj, _, n = wg.shape
    tm, tk = _pick(m, TM_CANDS), _pick(k, TK_CANDS)
    grid = (nj, m // tm, k // tk)
    if fold:
        shape, o_bs, o_im = (m, nj * n), (tm, n), (lambda j, i, kk: (i, j))
    else:
        shape, o_bs, o_im = (nj, m, n), (None, tm, n), (lambda j, i, kk: (j, i, 0))
    return _mm(name, a, wg, (tm, tk), lambda j, i, kk: (i, kk), (None, tk, n), lambda j, i, kk: (j, kk, 0),
               shape, out_dtype, o_bs, o_im, grid, ((1,), (0,)))


def mm_plain(name, a, w, out_dtype, res=None):
    m, k = a.shape
    n = w.shape[1]
    tm, tk, tn = _pick(m, TM_CANDS), _pick(k, TK_CANDS), _pick(n, TN_CANDS)
    grid = (n // tn, m // tm, k // tk)
    return _mm(name, a, w, (tm, tk), lambda j, i, kk: (i, kk), (tk, tn), lambda j, i, kk: (kk, j),
               (m, n), out_dtype, (tm, tn), lambda j, i, kk: (i, j), grid, ((1,), (0,)),
               res, (tm, tn), lambda j, i, kk: (i, j))


def mm_jsum(name, aj, wg, out_dtype, res=None):
    nj, m, ka = aj.shape
    n = wg.shape[2]
    tm, tn = _pick(m, TM_CANDS), _pick(n, TN_CANDS)
    grid = (m // tm, n // tn, nj)
    return _mm(name, aj, wg, (None, tm, ka), lambda i, jn, j: (j, i, 0), (None, ka, tn), lambda i, jn, j: (j, 0, jn),
               (m, n), out_dtype, (tm, tn), lambda i, jn, j: (i, jn), grid, ((1,), (0,)),
               res, (tm, tn), lambda i, jn, j: (i, jn))


def mm_tn_cols(name, a, g, nj, n, out_dtype, folded):
    s, kw = a.shape
    ts, tkw = _pick(s, TK_CANDS), _pick(kw, TM_CANDS)
    grid = (nj, kw // tkw, s // ts)
    if folded:
        g_bs, g_im = (ts, n), (lambda j, i, ss: (ss, j))
    else:
        g_bs, g_im = (None, ts, n), (lambda j, i, ss: (j, ss, 0))
    return _mm(name, a, g, (ts, tkw), lambda j, i, ss: (ss, i), g_bs, g_im,
               (nj, kw, n), out_dtype, (None, tkw, n), lambda j, i, ss: (j, i, 0), grid, ((0,), (0,)))


def mm_tn_plain(name, a, g, out_dtype):
    s, kw = a.shape
    n = g.shape[1]
    ts, tkw, tn = _pick(s, TK_CANDS), _pick(kw, TM_CANDS), _pick(n, TN_CANDS)
    grid = (kw // tkw, n // tn, s // ts)
    return _mm(name, a, g, (ts, tkw), lambda i, jn, ss: (ss, i), (ts, tn), lambda i, jn, ss: (ss, jn),
               (kw, n), out_dtype, (tkw, tn), lambda i, jn, ss: (i, jn), grid, ((0,), (0,)))


def mm_tn_j(name, aj, g, out_dtype):
    nj, s, ka = aj.shape
    n = g.shape[1]
    ts, tn = _pick(s, TK_CANDS), _pick(n, TN_CANDS)
    grid = (nj, n // tn, s // ts)
    return _mm(name, aj, g, (None, ts, ka), lambda j, jn, ss: (j, ss, 0), (ts, tn), lambda j, jn, ss: (ss, jn),
               (nj, ka, n), out_dtype, (None, ka, tn), lambda j, jn, ss: (j, 0, jn), grid, ((0,), (0,)))


def mm_nt_plain(name, g, w, out_dtype):
    m, n = g.shape
    k = w.shape[0]
    tm, tn, tkk = _pick(m, TM_CANDS), _pick(n, TK_CANDS), _pick(k, TN_CANDS)
    grid = (k // tkk, m // tm, n // tn)
    return _mm(name, g, w, (tm, tn), lambda kk, i, jn: (i, jn), (tkk, tn), lambda kk, i, jn: (kk, jn),
               (m, k), out_dtype, (tm, tkk), lambda kk, i, jn: (i, kk), grid, ((1,), (1,)))


def mm_nt_j(name, g, wg, out_dtype):
    m, n = g.shape
    nj, ka, _ = wg.shape
    tm, tn = _pick(m, TM_CANDS), _pick(n, TK_CANDS)
    grid = (nj, m // tm, n // tn)
    return _mm(name, g, wg, (tm, tn), lambda j, i, jn: (i, jn), (None, ka, tn), lambda j, i, jn: (j, 0, jn),
               (nj, m, ka), out_dtype, (None, tm, ka), lambda j, i, jn: (j, i, 0), grid, ((1,), (1,)))


def mm_nt_jsum(name, g, wg, out_dtype, folded, res=None):
    nj, k, n = wg.shape
    m = g.shape[0] if folded else g.shape[1]
    tm, tkk = _pick(m, TM_CANDS), _pick(k, TN_CANDS)
    grid = (m // tm, k // tkk, nj)
    if folded:
        g_bs, g_im = (tm, n), (lambda i, kk, j: (i, j))
    else:
        g_bs, g_im = (None, tm, n), (lambda i, kk, j: (j, i, 0))
    return _mm(name, g, wg, g_bs, g_im, (None, tkk, n), lambda i, kk, j: (j, kk, 0),
               (m, k), out_dtype, (tm, tkk), lambda i, kk, j: (i, kk), grid, ((1,), (1,)),
               res, (tm, tkk), lambda i, kk, j: (i, kk))


ROW_TILE_CANDS = (256, 128, 64, 32, 16, 8)


def _rstd(x):
    return lax.rsqrt(jnp.mean(x * x, axis=-1, keepdims=True) + RMS_EPS)


def _sigmoid(t):
    return 1.0 / (1.0 + jnp.exp(-t))


def rms_fwd(name, x, gain):
    s, d = x.shape
    ts = _pick(s, ROW_TILE_CANDS)

    def body(x_ref, g_ref, h_ref):
        xv = x_ref[...]
        h_ref[...] = ((xv * _rstd(xv)) * g_ref[...]).astype(h_ref.dtype)

    return pl.pallas_call(
        body, name=name, grid=(s // ts,),
        in_specs=[pl.BlockSpec((ts, d), lambda i: (i, 0)), pl.BlockSpec((1, d), lambda i: (0, 0))],
        out_specs=pl.BlockSpec((ts, d), lambda i: (i, 0)),
        out_shape=jax.ShapeDtypeStruct((s, d), BF16),
        compiler_params=_params(("parallel",), 3 * ts * d * 4))(x, gain)


def rms_bwd(name, x, gain, dh, dres):
    s, d = x.shape
    ts = _pick(s, ROW_TILE_CANDS)

    def body(x_ref, g_ref, dh_ref, dr_ref, dx_ref, dg_ref):
        xv = x_ref[...]
        r = _rstd(xv)
        xhat = xv * r
        dhv = dh_ref[...].astype(F32)
        dxhat = dhv * g_ref[...]
        dx_ref[...] = dr_ref[...] + r * (dxhat - xhat * jnp.mean(dxhat * xhat, axis=-1, keepdims=True))
        part = jnp.sum(dhv * xhat, axis=0, keepdims=True)

        @pl.when(pl.program_id(0) == 0)
        def _():
            dg_ref[...] = part

        @pl.when(pl.program_id(0) > 0)
        def _():
            dg_ref[...] += part

    row = pl.BlockSpec((ts, d), lambda i: (i, 0))
    vec = pl.BlockSpec((1, d), lambda i: (0, 0))
    return pl.pallas_call(
        body, name=name, grid=(s // ts,), in_specs=[row, vec, row, row], out_specs=[row, vec],
        out_shape=[jax.ShapeDtypeStruct((s, d), F32), jax.ShapeDtypeStruct((1, d), F32)],
        compiler_params=_params(("arbitrary",), 6 * ts * d * 4))(x, gain, dh, dres)


def gate_merge_fwd(name, proj, ta, tb, d):
    s = proj.shape[0]
    ts = _pick(s, ROW_TILE_CANDS)
    cb = COL_GATES // d

    def body(ga_ref, gb_ref, ta_ref, tb_ref, o_ref):
        o_ref[...] = (_sigmoid(ga_ref[...]) * ta_ref[...] + _sigmoid(gb_ref[...]) * tb_ref[...]).astype(o_ref.dtype)

    row = pl.BlockSpec((ts, d), lambda i: (i, 0))
    return pl.pallas_call(
        body, name=name, grid=(s // ts,),
        in_specs=[pl.BlockSpec((ts, d), lambda i: (i, cb)), pl.BlockSpec((ts, d), lambda i: (i, cb + 1)), row, row],
        out_specs=row, out_shape=jax.ShapeDtypeStruct((s, d), BF16),
        compiler_params=_params(("parallel",), 5 * ts * d * 4))(proj, proj, ta, tb)


def gate_merge_bwd(name, dmerged, proj, ta, tb, d):
    s = proj.shape[0]
    ts = _pick(s, ROW_TILE_CANDS)
    cb = COL_GATES // d

    def body(dm_ref, ga_ref, gb_ref, ta_ref, tb_ref, dta_ref, dtb_ref, dga_ref, dgb_ref):
        dm = dm_ref[...]
        sa = _sigmoid(ga_ref[...])
        sb = _sigmoid(gb_ref[...])
        dta_ref[...] = (dm * sa).astype(dta_ref.dtype)
        dtb_ref[...] = (dm * sb).astype(dtb_ref.dtype)
        dga_ref[...] = (dm * ta_ref[...] * (sa * (1.0 - sa))).astype(dga_ref.dtype)
        dgb_ref[...] = (dm * tb_ref[...] * (sb * (1.0 - sb))).astype(dgb_ref.dtype)

    row = pl.BlockSpec((ts, d), lambda i: (i, 0))
    out = jax.ShapeDtypeStruct((s, d), BF16)
    return pl.pallas_call(
        body, name=name, grid=(s // ts,),
        in_specs=[row, pl.BlockSpec((ts, d), lambda i: (i, cb)), pl.BlockSpec((ts, d), lambda i: (i, cb + 1)), row, row],
        out_specs=[row, row, row, row], out_shape=[out, out, out, out],
        compiler_params=_params(("parallel",), 8 * ts * d * 4))(dmerged, proj, proj, ta, tb)


def tail_fwd_bwd(name, x2, lp, pp, gain, target):
    s, d = x2.shape
    ts = _pick(s, ROW_TILE_CANDS)

    def body(x2_ref, lp_ref, pp_ref, g_ref, t_ref, loss_ref, dx3_ref, dlp_ref, dpp_ref, dg_ref):
        gp = _sigmoid(lp_ref[...])
        ppv = pp_ref[...]
        x3 = x2_ref[...] + gp * ppv
        r = _rstd(x3)
        xhat = x3 * r
        gv = g_ref[...]
        err = xhat * gv - t_ref[...]
        loss = jnp.sum(err * err) * (0.5 / d)
        dy = err * (1.0 / d)
        dxhat = dy * gv
        dx3 = r * (dxhat - xhat * jnp.mean(dxhat * xhat, axis=-1, keepdims=True))
        dx3_ref[...] = dx3
        dlp_ref[...] = (dx3 * ppv * (gp * (1.0 - gp))).astype(dlp_ref.dtype)
        dpp_ref[...] = (dx3 * gp).astype(dpp_ref.dtype)
        part = jnp.sum(dy * xhat, axis=0, keepdims=True)
        lossv = jnp.full((1, LANES), loss, F32)

        @pl.when(pl.program_id(0) == 0)
        def _():
            dg_ref[...] = part
            loss_ref[...] = lossv

        @pl.when(pl.program_id(0) > 0)
        def _():
            dg_ref[...] += part
            loss_ref[...] += lossv

    row = pl.BlockSpec((ts, d), lambda i: (i, 0))
    vec = pl.BlockSpec((1, d), lambda i: (0, 0))
    return pl.pallas_call(
        body, name=name, grid=(s // ts,), in_specs=[row, row, row, vec, row],
        out_specs=[pl.BlockSpec((1, LANES), lambda i: (0, 0)), row, row, row, vec],
        out_shape=[jax.ShapeDtypeStruct((1, LANES), F32), jax.ShapeDtypeStruct((s, d), F32),
                   jax.ShapeDtypeStruct((s, d), BF16), jax.ShapeDtypeStruct((s, d), BF16),
                   jax.ShapeDtypeStruct((1, d), F32)],
        compiler_params=_params(("arbitrary",), 9 * ts * d * 4))(x2, lp, pp, gain, target)


HALO = SUBLANES


def _shift_rows(cur, prev_row, next_row):
    ts = cur.shape[0]
    rid = lax.broadcasted_iota(jnp.int32, cur.shape, 0)
    down = jnp.where(rid == 0, prev_row, pltpu.roll(cur, 1, 0))
    up = jnp.where(rid == ts - 1, next_row, pltpu.roll(cur, ts - 1, 0))
    return down, up


def _halo_specs(ts, s, nf):
    nb = ts // HALO
    last = s // HALO - 1
    cur = pl.BlockSpec((None, ts, nf), lambda j, i: (j, i, 0))
    prev = pl.BlockSpec((None, HALO, nf), lambda j, i: (j, jnp.maximum(i * nb - 1, 0), 0))
    nxt = pl.BlockSpec((None, HALO, nf), lambda j, i: (j, jnp.minimum((i + 1) * nb, last), 0))
    return cur, prev, nxt


def _halo_rows(prev_ref, next_ref, n_tiles):
    i = pl.program_id(1)
    prev_row = jnp.where(i == 0, 0.0, prev_ref[HALO - 1:HALO, :].astype(F32))
    next_row = jnp.where(i == n_tiles - 1, 0.0, next_ref[0:1, :].astype(F32))
    return prev_row, next_row


def _gelu(g):
    t = jnp.tanh(GELU_C * (g + GELU_A * (g * g * g)))
    return 0.5 * g * (1.0 + t), t


def _conv(cur, down, up, cw_ref, cb_ref):
    return down * cw_ref[0:1, :] + cur * cw_ref[1:2, :] + up * cw_ref[2:3, :] + cb_ref[...]


def ffn_mid_fwd(name, gpre, u, cw, cb):
    nj, s, nf = gpre.shape
    ts = _pick(s, (512, 256, 128, 64, 32, 16, 8))
    n_tiles = s // ts
    cur, prev, nxt = _halo_specs(ts, s, nf)

    def body(g_ref, gp_ref, gn_ref, u_ref, cw_ref, cb_ref, z_ref):
        gv = g_ref[...]
        down, up = _shift_rows(gv, *_halo_rows(gp_ref, gn_ref, n_tiles))
        act, _ = _gelu(_conv(gv, down, up, cw_ref, cb_ref))
        z_ref[...] = (act * u_ref[...]).astype(z_ref.dtype)

    return pl.pallas_call(
        body, name=name, grid=(nj, n_tiles),
        in_specs=[cur, prev, nxt, cur, pl.BlockSpec((None, SUBLANES, nf), lambda j, i: (j, 0, 0)),
                  pl.BlockSpec((None, 1, nf), lambda j, i: (j, 0, 0))],
        out_specs=cur, out_shape=jax.ShapeDtypeStruct((nj, s, nf), BF16),
        compiler_params=_params(("parallel", "parallel"), 8 * ts * nf * 4))(gpre, gpre, gpre, u, cw, cb)


def ffn_mid_bwd1(name, gpre, u, dz, cw, cb):
    nj, s, nf = gpre.shape
    ts = _pick(s, (512, 256, 128, 64, 32, 16, 8))
    n_tiles = s // ts
    cur, prev, nxt = _halo_specs(ts, s, nf)

    def body(g_ref, gp_ref, gn_ref, u_ref, dz_ref, cw_ref, cb_ref, dg_ref, du_ref, dcw_ref):
        gv = g_ref[...]
        down, up = _shift_rows(gv, *_halo_rows(gp_ref, gn_ref, n_tiles))
        gc = _conv(gv, down, up, cw_ref, cb_ref)
        act, t = _gelu(gc)
        dzv = dz_ref[...].astype(F32)
        du_ref[...] = (dzv * act).astype(du_ref.dtype)
        dact = 0.5 * (1.0 + t) + 0.5 * gc * (1.0 - t * t) * (GELU_C * (1.0 + 3.0 * GELU_A * (gc * gc)))
        dg = dzv * u_ref[...] * dact
        dg_ref[...] = dg
        rows = [jnp.sum(dg * down, axis=0, keepdims=True), jnp.sum(dg * gv, axis=0, keepdims=True),
                jnp.sum(dg * up, axis=0, keepdims=True), jnp.sum(dg, axis=0, keepdims=True)]
        part = jnp.concatenate(rows + [jnp.zeros((SUBLANES - len(rows), nf), F32)], axis=0)

        @pl.when(pl.program_id(1) == 0)
        def _():
            dcw_ref[...] = part

        @pl.when(pl.program_id(1) > 0)
        def _():
            dcw_ref[...] += part

    small = pl.BlockSpec((None, SUBLANES, nf), lambda j, i: (j, 0, 0))
    return pl.pallas_call(
        body, name=name, grid=(nj, n_tiles),
        in_specs=[cur, prev, nxt, cur, cur, small, pl.BlockSpec((None, 1, nf), lambda j, i: (j, 0, 0))],
        out_specs=[cur, cur, small],
        out_shape=[jax.ShapeDtypeStruct((nj, s, nf), F32), jax.ShapeDtypeStruct((nj, s, nf), BF16),
                   jax.ShapeDtypeStruct((nj, SUBLANES, nf), F32)],
        compiler_params=_params(("parallel", "arbitrary"), 12 * ts * nf * 4))(gpre, gpre, gpre, u, dz, cw, cb)


def ffn_mid_bwd2(name, dg, cw):
    nj, s, nf = dg.shape
    ts = _pick(s, (512, 256, 128, 64, 32, 16, 8))
    n_tiles = s // ts
    cur, prev, nxt = _halo_specs(ts, s, nf)

    def body(g_ref, gp_ref, gn_ref, cw_ref, o_ref):
        gv = g_ref[...]
        down, up = _shift_rows(gv, *_halo_rows(gp_ref, gn_ref, n_tiles))
        o_ref[...] = (up * cw_ref[0:1, :] + gv * cw_ref[1:2, :] + down * cw_ref[2:3, :]).astype(o_ref.dtype)

    return pl.pallas_call(
        body, name=name, grid=(nj, n_tiles),
        in_specs=[cur, prev, nxt, pl.BlockSpec((None, SUBLANES, nf), lambda j, i: (j, 0, 0))],
        out_specs=cur, out_shape=jax.ShapeDtypeStruct((nj, s, nf), BF16),
        compiler_params=_params(("parallel", "parallel"), 6 * ts * nf * 4))(dg, dg, dg, cw)


def _t5_bucket(rel):
    half = N_BUCKETS // 2
    max_exact = half // 2
    n = jnp.abs(rel)
    side = jnp.where(rel > 0, half, 0)
    nf = jnp.maximum(n, 1).astype(F32)
    large = max_exact + (jnp.log(nf / max_exact) / math.log(MAX_DISTANCE / max_exact)
                         * (half - max_exact)).astype(jnp.int32)
    large = jnp.minimum(large, half - 1)
    return side + jnp.where(n < max_exact, n, large)


def bucket_tile(blk, dil):
    rel = (jnp.arange(3 * blk)[None, :] - blk) - jnp.arange(blk)[:, None]
    return _t5_bucket(rel * dil).astype(jnp.int32)


def bias_build(name, table_t, bucket, h0, nh):
    blk, kw = bucket.shape

    def body(t_ref, b_ref, o_ref):
        h = pl.program_id(0)
        bv = b_ref[...]
        acc = jnp.zeros((blk, kw), F32)
        for b in range(N_BUCKETS):
            acc = jnp.where(bv == b, t_ref[h0 + h, b], acc)
        o_ref[...] = acc

    return pl.pallas_call(
        body, name=name, grid=(nh,),
        in_specs=[pl.BlockSpec(memory_space=pltpu.SMEM), pl.BlockSpec((blk, kw), lambda h: (0, 0))],
        out_specs=pl.BlockSpec((None, blk, kw), lambda h: (h, 0, 0)),
        out_shape=jax.ShapeDtypeStruct((nh, blk, kw), F32),
        compiler_params=_params(("parallel",), 4 * blk * kw * 4))(table_t, bucket)


def table_grad(name, dbias, bucket):
    nh, blk, kw = dbias.shape

    def body(d_ref, b_ref, o_ref):
        bv = b_ref[...]
        dv = d_ref[...]
        lane = lax.broadcasted_iota(jnp.int32, (SUBLANES, LANES), 1)
        acc = jnp.zeros((SUBLANES, LANES), F32)
        for b in range(N_BUCKETS):
            acc = jnp.where(lane == b, jnp.sum(jnp.where(bv == b, dv, 0.0)), acc)
        o_ref[...] = acc

    return pl.pallas_call(
        body, name=name, grid=(nh,),
        in_specs=[pl.BlockSpec((None, blk, kw), lambda h: (h, 0, 0)), pl.BlockSpec((blk, kw), lambda h: (0, 0))],
        out_specs=pl.BlockSpec((None, SUBLANES, LANES), lambda h: (h, 0, 0)),
        out_shape=jax.ShapeDtypeStruct((nh, SUBLANES, LANES), F32),
        compiler_params=_params(("parallel",), 4 * blk * kw * 4))(dbias, bucket)


def _band_specs(proj_w, seg, dil, nh, group, cq, ck, cv):
    wb = proj_w // LANES
    q = pl.BlockSpec((seg, HEAD_DIM), lambda h, r: (0, r * wb + cq // LANES + h))
    k = pl.BlockSpec((seg, HEAD_DIM), lambda h, r: (0, r * wb + ck // LANES + h // group))
    v = pl.BlockSpec((seg, HEAD_DIM), lambda h, r: (0, r * wb + cv // LANES + h // group))
    o = pl.BlockSpec((seg, HEAD_DIM), lambda h, r: (0, r * nh + h))
    return q, k, v, o


def _stage_padded(dst, src_ref, blk, seg):
    dst[0:blk, :] = jnp.zeros((blk, HEAD_DIM), dst.dtype)
    dst[blk + seg:2 * blk + seg, :] = jnp.zeros((blk, HEAD_DIM), dst.dtype)
    dst[blk:blk + seg, :] = src_ref[...].astype(dst.dtype)


def _band_valid(off, blk, seg):
    qi = lax.broadcasted_iota(jnp.int32, (blk, 3 * blk), 0)
    ci = lax.broadcasted_iota(jnp.int32, (blk, 3 * blk), 1)
    kpos = off + ci - blk
    return (jnp.abs(ci - blk - qi) <= blk) & (kpos >= 0) & (kpos < seg)


def band_attn_fwd(name, proj, bias, sink, *, blk, dil, nh, group, cq, ck, cv):
    s, w = proj.shape
    seg = s // dil
    nb = seg // blk
    view = proj.reshape(seg, dil * w)
    q_spec, k_spec, v_spec, o_spec = _band_specs(w, seg, dil, nh, group, cq, ck, cv)
    has_sink = sink is not None

    def body(*refs):
        if has_sink:
            q_ref, k_ref, v_ref, b_ref, s_ref, o_ref, l_ref, kp, vp = refs
        else:
            q_ref, k_ref, v_ref, b_ref, o_ref, l_ref, kp, vp = refs
        _stage_padded(kp, k_ref, blk, seg)
        _stage_padded(vp, v_ref, blk, seg)
        bias_v = b_ref[...]
        if has_sink:
            sk = s_ref[pl.program_id(0)]

        def step(b, carry):
            off = pl.multiple_of(b * blk, blk)
            qv = q_ref[pl.ds(off, blk), :].astype(BF16)
            kw_ = kp[pl.ds(off, 3 * blk), :]
            vw_ = vp[pl.ds(off, 3 * blk), :]
            sc = lax.dot_general(qv, kw_, (((1,), (1,)), ((), ())), preferred_element_type=F32) * ATTN_SCALE + bias_v
            sc = jnp.where(_band_valid(off, blk, seg), sc, NEG_INF)
            m = jnp.max(sc, axis=-1, keepdims=True)
            if has_sink:
                m = jnp.maximum(m, sk)
            p = jnp.exp(sc - m)
            den = jnp.sum(p, axis=-1, keepdims=True)
            if has_sink:
                den = den + jnp.exp(sk - m)
            out = lax.dot_general(p.astype(BF16), vw_, (((1,), (0,)), ((), ())), preferred_element_type=F32)
            o_ref[pl.ds(off, blk), :] = out / den
            l_ref[pl.ds(off, blk), :] = jnp.broadcast_to(m + jnp.log(den), (blk, HEAD_DIM))
            return carry

        lax.fori_loop(0, nb, step, 0)

    in_specs = [q_spec, k_spec, v_spec, pl.BlockSpec((None, blk, 3 * blk), lambda h, r: (h, 0, 0))]
    args = [view, view, view, bias]
    if has_sink:
        in_specs.append(pl.BlockSpec(memory_space=pltpu.SMEM))
        args.append(sink)
    shape = jax.ShapeDtypeStruct((seg, dil * nh * HEAD_DIM), F32)
    out, lse = pl.pallas_call(
        body, name=name, grid=(nh, dil), in_specs=in_specs, out_specs=[o_spec, o_spec], out_shape=[shape, shape],
        scratch_shapes=[pltpu.VMEM((seg + 2 * blk, HEAD_DIM), BF16), pltpu.VMEM((seg + 2 * blk, HEAD_DIM), BF16)],
        compiler_params=_params(("parallel", "parallel"), 12 * seg * HEAD_DIM * 4))(*args)
    return out.reshape(s, nh * HEAD_DIM), lse.reshape(s, nh * HEAD_DIM)


def band_attn_bwd(name, proj, bias, sink, dout, out, lse, dlse, *, blk, dil, nh, group, cq, ck, cv):
    s, w = proj.shape
    seg = s // dil
    nb = seg // blk
    nkv = nh // group
    view = proj.reshape(seg, dil * w)
    q_spec, k_spec, v_spec, o_spec = _band_specs(w, seg, dil, nh, group, cq, ck, cv)
    kv_spec = pl.BlockSpec((seg, HEAD_DIM), lambda h, r: (0, r * nkv + h // group))
    has_sink = sink is not None
    has_dl = dlse is not None
    n_in = 7 + int(has_sink) + int(has_dl)

    def body(*refs):
        ins, rest = refs[:n_in], refs[n_in:]
        q_ref, k_ref, v_ref, b_ref, do_ref, o_ref, l_ref = ins[:7]
        s_ref = ins[7] if has_sink else None
        dl_ref = ins[n_in - 1] if has_dl else None
        if has_sink:
            dq_ref, dk_ref, dv_ref, db_ref, ds_ref, kp, vp, dkp, dvp, dsa = rest
        else:
            dq_ref, dk_ref, dv_ref, db_ref, kp, vp, dkp, dvp = rest
        h = pl.program_id(0)
        r = pl.program_id(1)
        _stage_padded(kp, k_ref, blk, seg)
        _stage_padded(vp, v_ref, blk, seg)
        dkp[...] = jnp.zeros_like(dkp)
        dvp[...] = jnp.zeros_like(dvp)
        bias_v = b_ref[...]
        if has_sink:
            sk = s_ref[h]
            dsa[...] = jnp.zeros_like(dsa)

        @pl.when(r == 0)
        def _():
            db_ref[...] = jnp.zeros_like(db_ref)

        def step(b, carry):
            off = pl.multiple_of(b * blk, blk)
            rows = pl.ds(off, blk)
            win = pl.ds(off, 3 * blk)
            qv = q_ref[rows, :].astype(BF16)
            kw_ = kp[win, :]
            vw_ = vp[win, :]
            sc = lax.dot_general(qv, kw_, (((1,), (1,)), ((), ())), preferred_element_type=F32) * ATTN_SCALE + bias_v
            sc = jnp.where(_band_valid(off, blk, seg), sc, NEG_INF)
            lv = l_ref[rows, :][:, 0:1]
            p = jnp.exp(sc - lv)
            dov = do_ref[rows, :]
            delta = jnp.sum(dov * o_ref[rows, :], axis=-1, keepdims=True)
            dob = dov.astype(BF16)
            dp = lax.dot_general(dob, vw_, (((1,), (1,)), ((), ())), preferred_element_type=F32)
            t = dp - delta
            if has_dl:
                t = t + dl_ref[rows, :][:, 0:1]
            ds = p * t
            dsb = (ds * ATTN_SCALE).astype(BF16)
            dq_ref[rows, :] = lax.dot_general(dsb, kw_, (((1,), (0,)), ((), ())),
                                              preferred_element_type=F32).astype(dq_ref.dtype)
            dkp[win, :] += lax.dot_general(dsb, qv, (((0,), (0,)), ((), ())), preferred_element_type=F32)
            dvp[win, :] += lax.dot_general(p.astype(BF16), dob, (((0,), (0,)), ((), ())), preferred_element_type=F32)
            db_ref[...] += ds
            if has_sink:
                dsa[...] += jnp.exp(sk - lv) * delta
            return carry

        lax.fori_loop(0, nb, step, 0)

        if group == 1:
            dk_ref[...] = dkp[blk:blk + seg, :].astype(dk_ref.dtype)
            dv_ref[...] = dvp[blk:blk + seg, :].astype(dv_ref.dtype)
        else:
            @pl.when(h % group == 0)
            def _():
                dk_ref[...] = dkp[blk:blk + seg, :]
                dv_ref[...] = dvp[blk:blk + seg, :]

            @pl.when(h % group != 0)
            def _():
                dk_ref[...] += dkp[blk:blk + seg, :]
                dv_ref[...] += dvp[blk:blk + seg, :]
        if has_sink:
            ds_ref[...] = jnp.full((SUBLANES, LANES), -jnp.sum(dsa[...]), F32)

    b_spec = pl.BlockSpec((None, blk, 3 * blk), lambda h, r: (h, 0, 0))
    in_specs = [q_spec, k_spec, v_spec, b_spec, o_spec, o_spec, o_spec]
    args = [view, view, view, bias, dout.reshape(seg, -1), out.reshape(seg, -1), lse.reshape(seg, -1)]
    if has_sink:
        in_specs.append(pl.BlockSpec(memory_space=pltpu.SMEM))
        args.append(sink)
    if has_dl:
        in_specs.append(o_spec)
        args.append(dlse.reshape(seg, -1))
    kv_dtype = BF16 if group == 1 else F32
    out_specs = [o_spec, kv_spec, kv_spec, b_spec]
    out_shape = [jax.ShapeDtypeStruct((seg, dil * nh * HEAD_DIM), BF16),
                 jax.ShapeDtypeStruct((seg, dil * nkv * HEAD_DIM), kv_dtype),
                 jax.ShapeDtypeStruct((seg, dil * nkv * HEAD_DIM), kv_dtype),
                 jax.ShapeDtypeStruct((nh, blk, 3 * blk), F32)]
    pad = seg + 2 * blk
    scratch = [pltpu.VMEM((pad, HEAD_DIM), BF16), pltpu.VMEM((pad, HEAD_DIM), BF16),
               pltpu.VMEM((pad, HEAD_DIM), F32), pltpu.VMEM((pad, HEAD_DIM), F32)]
    if has_sink:
        out_specs.append(pl.BlockSpec((None, SUBLANES, LANES), lambda h, r: (h, 0, 0)))
        out_shape.append(jax.ShapeDtypeStruct((nh, SUBLANES, LANES), F32))
        scratch.append(pltpu.VMEM((blk, 1), F32))
    res = pl.pallas_call(
        body, name=name, grid=(nh, dil), in_specs=in_specs, out_specs=out_specs, out_shape=out_shape,
        scratch_shapes=scratch,
        compiler_params=_params(("arbitrary", "arbitrary"), 24 * seg * HEAD_DIM * 4))(*args)
    dq = res[0].reshape(s, nh * HEAD_DIM)
    dk = res[1].reshape(s, nkv * HEAD_DIM)
    dv = res[2].reshape(s, nkv * HEAD_DIM)
    return dq, dk, dv, res[3], (res[4] if has_sink else None)


def dil_merge_fwd(name, outs, lses):
    s, w = outs[0].shape
    ts = _pick(s, ROW_TILE_CANDS)
    ng = len(outs)

    def body(*refs):
        o_refs, l_refs, y_ref = refs[:ng], refs[ng:2 * ng], refs[2 * ng]
        ls = [l[...] for l in l_refs]
        mx = ls[0]
        for l in ls[1:]:
            mx = jnp.maximum(mx, l)
        es = [jnp.exp(l - mx) for l in ls]
        tot = es[0]
        for e in es[1:]:
            tot = tot + e
        acc = (es[0] / tot) * o_refs[0][...]
        for e, o in zip(es[1:], o_refs[1:]):
            acc = acc + (e / tot) * o[...]
        y_ref[...] = acc.astype(y_ref.dtype)

    row = pl.BlockSpec((ts, w), lambda i: (i, 0))
    return pl.pallas_call(
        body, name=name, grid=(s // ts,), in_specs=[row] * (2 * ng), out_specs=row,
        out_shape=jax.ShapeDtypeStruct((s, w), BF16),
        compiler_params=_params(("parallel",), 10 * ts * w * 4))(*outs, *lses)


def dil_merge_bwd(name, dy, outs, lses):
    s, w = outs[0].shape
    ts = _pick(s, ROW_TILE_CANDS)
    ng = len(outs)
    nhead = w // HEAD_DIM

    def body(*refs):
        dy_ref = refs[0]
        o_refs, l_refs = refs[1:1 + ng], refs[1 + ng:1 + 2 * ng]
        do_refs, dl_refs = refs[1 + 2 * ng:1 + 3 * ng], refs[1 + 3 * ng:1 + 4 * ng]
        for hh in range(nhead):
            cols = slice(hh * HEAD_DIM, (hh + 1) * HEAD_DIM)
            dyv = dy_ref[:, cols]
            ls = [l[:, cols] for l in l_refs]
            mx = ls[0]
            for l in ls[1:]:
                mx = jnp.maximum(mx, l)
            es = [jnp.exp(l - mx) for l in ls]
            tot = es[0]
            for e in es[1:]:
                tot = tot + e
            alphas = [e / tot for e in es]
            dal = [jnp.broadcast_to(jnp.sum(dyv * o[:, cols], axis=-1, keepdims=True), dyv.shape) for o in o_refs]
            mean = alphas[0] * dal[0]
            for a, d in zip(alphas[1:], dal[1:]):
                mean = mean + a * d
            for g in range(ng):
                do_refs[g][:, cols] = alphas[g] * dyv
                dl_refs[g][:, cols] = alphas[g] * (dal[g] - mean)

    row = pl.BlockSpec((ts, w), lambda i: (i, 0))
    shape = jax.ShapeDtypeStruct((s, w), F32)
    res = pl.pallas_call(
        body, name=name, grid=(s // ts,), in_specs=[row] * (1 + 2 * ng), out_specs=[row] * (2 * ng),
        out_shape=[shape] * (2 * ng),
        compiler_params=_params(("parallel",), 16 * ts * w * 4))(dy, *outs, *lses)
    return res[:ng], res[ng:]


def _adamw(w, g, m, v):
    m = ADAM_B1 * m + (1.0 - ADAM_B1) * g
    v = ADAM_B2 * v + (1.0 - ADAM_B2) * (g * g)
    m_hat = m / (1.0 - ADAM_B1 ** ADAM_STEP)
    v_hat = v / (1.0 - ADAM_B2 ** ADAM_STEP)
    delta = -ADAM_LR * (m_hat / (jnp.sqrt(v_hat) + ADAM_EPS) + ADAM_WD * w)
    return delta, m, v


def _row_tile(r, c, budget=1 << 20):
    if r * c * 4 <= budget or r % SUBLANES:
        return r
    for t in (1024, 512, 256, 128, 64, 32, 16, 8):
        if r % t == 0 and t * c * 4 <= budget:
            return t
    return SUBLANES


def adam_small(name, g, w, m, v):
    def body(g_ref, w_ref, m_ref, v_ref, d_ref, nm_ref, nv_ref):
        d_ref[...], nm_ref[...], nv_ref[...] = _adamw(w_ref[...], g_ref[...], m_ref[...], v_ref[...])

    shape = jax.ShapeDtypeStruct(w.shape, F32)
    return pl.pallas_call(body, name=name, out_shape=[shape, shape, shape])(g, w, m, v)


def reduce_adam(name, parts, w, m, v):
    n, r, c = parts.shape
    tr = _row_tile(r, c)

    def body(p_ref, w_ref, m_ref, v_ref, g_ref, d_ref, nm_ref, nv_ref):
        g = p_ref[0].astype(F32)
        for i in range(1, n):
            g = g + p_ref[i].astype(F32)
        g_ref[...] = g
        d_ref[...], nm_ref[...], nv_ref[...] = _adamw(w_ref[...], g, m_ref[...], v_ref[...])

    row = pl.BlockSpec((tr, c), lambda i: (i, 0))
    shape = jax.ShapeDtypeStruct((r, c), F32)
    return pl.pallas_call(
        body, name=name, grid=(r // tr,), in_specs=[pl.BlockSpec((n, tr, c), lambda i: (0, i, 0)), row, row, row],
        out_specs=[row] * 4, out_shape=[shape] * 4,
        compiler_params=_params(("parallel",), (n * 2 + 7 * 4) * tr * c))(parts, w, m, v)


def _place():
    return lax.axis_index("x"), lax.axis_index("y"), lax.axis_index("c")


def _flip(pos, bits):
    return tuple((1 - p) if b else p for p, b in zip(pos, bits))


def _index(pos):
    return 4 * pos[0] + 2 * pos[1] + pos[2]


ANY = pl.BlockSpec(memory_space=pl.ANY)


def allgather(name, shards):
    n = len(shards)

    def body(*refs):
        ins, outs = refs[:n], refs[n:2 * n]
        send_sems, recv_sems, local_sems = refs[2 * n:]
        me = _place()
        sibling = _flip(me, (0, 0, 1))
        chips = [_flip(me, (1, 0, 0)), _flip(me, (0, 1, 0)), _flip(me, (1, 1, 0))]

        def copy(a, k, block, to, src=None):
            dst = outs[a].at[_index(block)]
            return pltpu.make_async_remote_copy(
                src_ref=dst if src is None else src, dst_ref=dst, send_sem=send_sems.at[a * 7 + k],
                recv_sem=recv_sems.at[a * 7 + k], device_id=to, device_id_type=MESH)

        mine = [pltpu.make_async_copy(ins[a], outs[a].at[_index(me)], local_sems.at[a]) for a in range(n)]
        for cp in mine:
            cp.start()
        sends = []
        for a in range(n):
            first = [copy(a, 0, me, sibling, src=ins[a])]
            first += [copy(a, 1 + j, me, chip, src=ins[a]) for j, chip in enumerate(chips)]
            for cp in first:
                cp.start()
            sends += first
        for a in range(n):
            for j, chip in enumerate(chips):
                copy(a, 1 + j, chip, me).wait_recv()
                fwd = copy(a, 4 + j, chip, sibling)
                fwd.start()
                sends.append(fwd)
        for a in range(n):
            copy(a, 0, sibling, me).wait_recv()
            for j, chip in enumerate(chips):
                copy(a, 4 + j, _flip(chip, (0, 0, 1)), me).wait_recv()
        for cp in sends:
            cp.wait_send()
        for cp in mine:
            cp.wait()

    return pl.pallas_call(
        body, name=name, in_specs=[ANY] * n, out_specs=[ANY] * n,
        out_shape=[jax.ShapeDtypeStruct((N_DEV,) + tuple(sh.shape), sh.dtype) for sh in shards],
        scratch_shapes=[pltpu.SemaphoreType.DMA((7 * n,)), pltpu.SemaphoreType.DMA((7 * n,)),
                        pltpu.SemaphoreType.DMA((n,))],
    )(*shards)


REL = [(b >> 2 & 1, b >> 1 & 1, b & 1) for b in range(N_DEV)]


def exchange(name, parts):
    n = len(parts)

    def body(*refs):
        ins, outs = refs[:n], refs[n:2 * n]
        send_sems, recv_sems, local_sems = refs[2 * n:]
        me = _place()
        mine = [pltpu.make_async_copy(ins[a].at[_index(me)], outs[a].at[0], local_sems.at[a]) for a in range(n)]
        for cp in mine:
            cp.start()
        copies = []
        for a in range(n):
            for r in range(1, N_DEV):
                peer = _flip(me, REL[r])
                cp = pltpu.make_async_remote_copy(
                    src_ref=ins[a].at[_index(peer)], dst_ref=outs[a].at[r], send_sem=send_sems.at[a * 7 + r - 1],
                    recv_sem=recv_sems.at[a * 7 + r - 1], device_id=peer, device_id_type=MESH)
                cp.start()
                copies.append(cp)
        for cp in copies:
            cp.wait_recv()
        for cp in copies:
            cp.wait_send()
        for cp in mine:
            cp.wait()

    return pl.pallas_call(
        body, name=name, in_specs=[ANY] * n, out_specs=[ANY] * n,
        out_shape=[jax.ShapeDtypeStruct(p.shape, p.dtype) for p in parts],
        scratch_shapes=[pltpu.SemaphoreType.DMA((7 * n,)), pltpu.SemaphoreType.DMA((7 * n,)),
                        pltpu.SemaphoreType.DMA((n,))],
    )(*parts)


def allreduce_small(name, pack):
    rows, lanes = pack.shape

    def body(x_ref, o_ref, land, send_sems, recv_sems):
        me = _place()
        idx = _index(me)
        land[idx] = x_ref[...]
        copies = []
        for r in range(1, N_DEV):
            peer = _flip(me, REL[r])
            cp = pltpu.make_async_remote_copy(
                src_ref=x_ref, dst_ref=land.at[idx], send_sem=send_sems.at[r - 1], recv_sem=recv_sems.at[r - 1],
                device_id=peer, device_id_type=MESH)
            cp.start()
            copies.append(cp)
        for cp in copies:
            cp.wait_recv()
        for cp in copies:
            cp.wait_send()
        acc = land[0]
        for i in range(1, N_DEV):
            acc = acc + land[i]
        o_ref[...] = acc

    return pl.pallas_call(
        body, name=name, in_specs=[pl.BlockSpec(memory_space=pltpu.VMEM)],
        out_specs=pl.BlockSpec(memory_space=pltpu.VMEM), out_shape=jax.ShapeDtypeStruct((rows, lanes), F32),
        scratch_shapes=[pltpu.VMEM((N_DEV, rows, lanes), F32), pltpu.SemaphoreType.DMA((7,)),
                        pltpu.SemaphoreType.DMA((7,))],
    )(pack)


def _pad_rows(a, rows):
    return jnp.pad(a, ((0, rows - a.shape[0]), (0, 0)))


def _as_tiles(vec):
    n = vec.shape[0]
    rows = -(-n // LANES)
    rows = -(-rows // SUBLANES) * SUBLANES
    return jnp.pad(vec, (0, rows * LANES - n)).reshape(rows, LANES)


def kernel(x, p, rel_bias_table, attn_norm, w_in, sink_a, w_branch_a, w_branch_b, w_out, ffn_norm, w_ffn_gate, w_ffn_up, conv_w, conv_b, w_ffn_down, ple_norm, w_ple_gate, w_ple_proj, final_norm, loss_target, m_rel_bias_table, m_attn_norm, m_w_in, m_sink_a, m_w_branch_a, m_w_branch_b, m_w_out, m_ffn_norm, m_w_ffn_gate, m_w_ffn_up, m_conv_w, m_conv_b, m_w_ffn_down, m_ple_norm, m_w_ple_gate, m_w_ple_proj, m_final_norm, v_rel_bias_table, v_attn_norm, v_w_in, v_sink_a, v_w_branch_a, v_w_branch_b, v_w_out, v_ffn_norm, v_w_ffn_gate, v_w_ffn_up, v_conv_w, v_conv_b, v_w_ffn_down, v_ple_norm, v_w_ple_gate, v_w_ple_proj, v_final_norm):
    xs = x[0]
    s, d = xs.shape
    ps = p[0, 0]
    target = loss_target[0]
    me = 4 * lax.axis_index("x") + 2 * lax.axis_index("y") + lax.axis_index("c")

    big = dict(w_in=w_in[0], w_branch_a=w_branch_a[0], w_branch_b=w_branch_b[0], w_out=w_out[0],
               w_ffn_gate=w_ffn_gate[0], w_ffn_up=w_ffn_up[0], w_ffn_down=w_ffn_down[0],
               w_ple_gate=w_ple_gate[0], w_ple_proj=w_ple_proj[0])
    big_m = dict(w_in=m_w_in[0], w_branch_a=m_w_branch_a[0], w_branch_b=m_w_branch_b[0], w_out=m_w_out[0],
                 w_ffn_gate=m_w_ffn_gate[0], w_ffn_up=m_w_ffn_up[0], w_ffn_down=m_w_ffn_down[0],
                 w_ple_gate=m_w_ple_gate[0], w_ple_proj=m_w_ple_proj[0])
    big_v = dict(w_in=v_w_in[0], w_branch_a=v_w_branch_a[0], w_branch_b=v_w_branch_b[0], w_out=v_w_out[0],
                 w_ffn_gate=v_w_ffn_gate[0], w_ffn_up=v_w_ffn_up[0], w_ffn_down=v_w_ffn_down[0],
                 w_ple_gate=v_w_ple_gate[0], w_ple_proj=v_w_ple_proj[0])
    names = list(big)
    nf = big["w_ffn_gate"].shape[1]

    gathered = allgather("allgather_weights",
                         [big[k].astype(BF16) for k in names] + [_pad_rows(conv_w[0], SUBLANES)])
    wg = dict(zip(names, gathered[:-1]))
    cw = gathered[-1]
    cb = conv_b.reshape(N_DEV, 1, nf)
    w_out_full = wg["w_out"].reshape(d, d)
    w_pg_full = wg["w_ple_gate"].reshape(d, d)

    table_t = rel_bias_table.T
    bucket_a = bucket_tile(A_BLOCK, 1)
    bias_a = bias_build("bias_a", table_t, bucket_a, 0, A_Q_HEADS)
    buckets_b = [bucket_tile(B_BLOCK, dil) for _, dil in B_PATTERNS]
    biases_b = [bias_build(f"bias_b{g}", table_t, buckets_b[g], A_Q_HEADS + g * B_HEADS_PER_GROUP, B_HEADS_PER_GROUP)
                for g in range(len(B_PATTERNS))]

    h = rms_fwd("rms_attn", xs, attn_norm)
    proj = mm_cols("proj_in", h, wg["w_in"], F32, fold=True)
    sink = sink_a[0]
    ya, lse_a = band_attn_fwd("attn_a_fwd", proj, bias_a, sink, blk=A_BLOCK, dil=1, nh=A_Q_HEADS, group=A_GROUP,
                              cq=COL_QA, ck=COL_KA, cv=COL_VA)
    outs_b, lses_b = [], []
    for g, (_, dil) in enumerate(B_PATTERNS):
        off = g * B_OUT_W
        o, l = band_attn_fwd(f"attn_b{g}_fwd", proj, biases_b[g], None, blk=B_BLOCK, dil=dil, nh=B_HEADS_PER_GROUP,
                             group=1, cq=COL_QB + off, ck=COL_KB + off, cv=COL_VB + off)
        outs_b.append(o)
        lses_b.append(l)
    yb = dil_merge_fwd("dil_merge_fwd", outs_b, lses_b)
    ta = mm_cols("branch_a", ya, wg["w_branch_a"], F32, fold=True)
    tb = mm_cols("branch_b", yb, wg["w_branch_b"], F32, fold=True)
    merged = gate_merge_fwd("gate_merge_fwd", proj, ta, tb, d)
    x1 = mm_plain("mix_out", merged, w_out_full, F32, res=xs)

    hf = rms_fwd("rms_ffn", x1, ffn_norm)
    gpre = mm_cols("ffn_gate", hf, wg["w_ffn_gate"], F32, fold=False)
    u = mm_cols("ffn_up", hf, wg["w_ffn_up"], F32, fold=False)
    z = ffn_mid_fwd("ffn_mid_fwd", gpre, u, cw, cb)
    x2 = mm_jsum("ffn_down", z, wg["w_ffn_down"], F32, res=x1)

    hp = rms_fwd("rms_ple", x2, ple_norm)
    lp = mm_plain("ple_gate", hp, w_pg_full, F32)
    pp = mm_cols("ple_proj", ps, wg["w_ple_proj"], F32, fold=True)
    loss_part, dx3, dlp, dpp, d_final = tail_fwd_bwd("tail", x2, lp, pp, final_norm.reshape(1, d), target)

    grads = {}
    grads["w_ple_proj"] = mm_tn_cols("d_w_ple_proj", ps, dpp, N_DEV, big["w_ple_proj"].shape[1], BF16, folded=True)
    grads["w_ple_gate"] = mm_tn_plain("d_w_ple_gate", hp, dlp, BF16).reshape(N_DEV, d // N_DEV, d)
    dhp = mm_nt_plain("d_hp", dlp, w_pg_full, F32)
    dx2, d_ple = rms_bwd("rms_ple_bwd", x2, ple_norm, dhp, dx3)

    dz = mm_nt_j("d_z", dx2, wg["w_ffn_down"], BF16)
    grads["w_ffn_down"] = mm_tn_j("d_w_ffn_down", z, dx2, BF16)
    dg, du, dcw = ffn_mid_bwd1("ffn_mid_bwd1", gpre, u, dz, cw, cb)
    dgpre = ffn_mid_bwd2("ffn_mid_bwd2", dg, cw)
    grads["w_ffn_up"] = mm_tn_cols("d_w_ffn_up", hf, du, N_DEV, nf, BF16, folded=False)
    grads["w_ffn_gate"] = mm_tn_cols("d_w_ffn_gate", hf, dgpre, N_DEV, nf, BF16, folded=False)
    dhf = mm_nt_jsum("d_hf_up", du, wg["w_ffn_up"], F32, folded=False)
    dhf = mm_nt_jsum("d_hf_gate", dgpre, wg["w_ffn_gate"], F32, folded=False, res=dhf)
    dx1, d_ffn = rms_bwd("rms_ffn_bwd", x1, ffn_norm, dhf, dx2)

    dmerged = mm_nt_plain("d_merged", dx1, w_out_full, F32)
    grads["w_out"] = mm_tn_plain("d_w_out", merged, dx1, BF16).reshape(N_DEV, d // N_DEV, d)
    dta, dtb, dga, dgb = gate_merge_bwd("gate_merge_bwd", dmerged, proj, ta, tb, d)
    grads["w_branch_a"] = mm_tn_cols("d_w_branch_a", ya, dta, N_DEV, big["w_branch_a"].shape[1], BF16, folded=True)
    grads["w_branch_b"] = mm_tn_cols("d_w_branch_b", yb, dtb, N_DEV, big["w_branch_b"].shape[1], BF16, folded=True)
    dya = mm_nt_jsum("d_ya", dta, wg["w_branch_a"], F32, folded=True)
    dyb = mm_nt_jsum("d_yb", dtb, wg["w_branch_b"], F32, folded=True)
    dqa, dka, dva, dbias_a, dsink = band_attn_bwd(
        "attn_a_bwd", proj, bias_a, sink, dya, ya, lse_a, None, blk=A_BLOCK, dil=1, nh=A_Q_HEADS, group=A_GROUP,
        cq=COL_QA, ck=COL_KA, cv=COL_VA)
    douts_b, dlses_b = dil_merge_bwd("dil_merge_bwd", dyb, outs_b, lses_b)
    dq_b, dk_b, dv_b, dbias_b = [], [], [], []
    for g, (_, dil) in enumerate(B_PATTERNS):
        off = g * B_OUT_W
        dq, dk, dv, db, _ = band_attn_bwd(
            f"attn_b{g}_bwd", proj, biases_b[g], None, douts_b[g], outs_b[g], lses_b[g], dlses_b[g], blk=B_BLOCK,
            dil=dil, nh=B_HEADS_PER_GROUP, group=1, cq=COL_QB + off, ck=COL_KB + off, cv=COL_VB + off)
        dq_b.append(dq)
        dk_b.append(dk)
        dv_b.append(dv)
        dbias_b.append(db)
    dproj = jnp.concatenate([dqa, dka.astype(BF16), dva.astype(BF16)] + dq_b + dk_b + dv_b + [dga, dgb], axis=1)
    grads["w_in"] = mm_tn_cols("d_w_in", h, dproj, N_DEV, big["w_in"].shape[1], BF16, folded=True)
    dh = mm_nt_jsum("d_h", dproj, wg["w_in"], F32, folded=True)
    grad_x, d_attn = rms_bwd("rms_attn_bwd", xs, attn_norm, dh, dx1)

    dt_a = table_grad("table_grad_a", dbias_a, bucket_a)[:, 0, :N_BUCKETS]
    dt_b = [table_grad(f"table_grad_b{g}", dbias_b[g], buckets_b[g])[:, 0, :N_BUCKETS] for g in range(len(B_PATTERNS))]
    d_table_part = jnp.concatenate([dt_a] + dt_b, axis=0).T

    pieces = [
        ("loss", loss_part[0, :1]),
        ("table", d_table_part.reshape(-1)),
        ("attn_norm", d_attn.reshape(-1)),
        ("sink", dsink[:, 0, 0]),
        ("ffn_norm", d_ffn.reshape(-1)),
        ("conv_w", dcw[:, 0:3, :].reshape(-1)),
        ("conv_b", dcw[:, 3, :].reshape(-1)),
        ("ple_norm", d_ple.reshape(-1)),
        ("final_norm", d_final.reshape(-1)),
    ]
    tiles = [_as_tiles(v) for _, v in pieces]
    total = allreduce_small("allreduce_small", jnp.concatenate(tiles, axis=0))
    small = {}
    row = 0
    for (nm, v), t in zip(pieces, tiles):
        small[nm] = total[row:row + t.shape[0]].reshape(-1)[:v.shape[0]]
        row += t.shape[0]
    loss = small["loss"][0]
    g_small = dict(
        rel_bias_table=small["table"].reshape(rel_bias_table.shape),
        attn_norm=small["attn_norm"].reshape(attn_norm.shape),
        sink_a=small["sink"].reshape(sink_a.shape),
        ffn_norm=small["ffn_norm"].reshape(ffn_norm.shape),
        conv_w=lax.dynamic_index_in_dim(small["conv_w"].reshape(N_DEV, 3, nf), me, 0, keepdims=False)[None],
        conv_b=small["conv_b"].reshape(conv_b.shape),
        ple_norm=small["ple_norm"].reshape(ple_norm.shape),
        final_norm=small["final_norm"].reshape(1, d),
    )
    w_small = dict(rel_bias_table=(rel_bias_table, m_rel_bias_table, v_rel_bias_table),
                   attn_norm=(attn_norm, m_attn_norm, v_attn_norm), sink_a=(sink_a, m_sink_a, v_sink_a),
                   ffn_norm=(ffn_norm, m_ffn_norm, v_ffn_norm), conv_w=(conv_w, m_conv_w, v_conv_w),
                   conv_b=(conv_b, m_conv_b, v_conv_b), ple_norm=(ple_norm, m_ple_norm, v_ple_norm),
                   final_norm=(final_norm, m_final_norm, v_final_norm))

    recv = exchange("exchange_grads", [grads[k] for k in names])
    out_g, out_d, out_m, out_v = {}, {}, {}, {}
    for k, parts in zip(names, recv):
        g, dl, nm, nv = reduce_adam("adam_" + k, parts, big[k], big_m[k], big_v[k])
        out_g[k], out_d[k], out_m[k], out_v[k] = g[None], dl[None], nm[None], nv[None]
    for k, (wv, mv, vv) in w_small.items():
        shape = wv.shape
        two_d = (1, shape[0]) if len(shape) == 1 else ((shape[0] * shape[1], shape[2]) if len(shape) == 3 else shape)
        gk = g_small[k].reshape(two_d)
        dl, nm, nv = adam_small("adam_" + k, gk, wv.reshape(two_d), mv.reshape(two_d), vv.reshape(two_d))
        out_g[k], out_d[k], out_m[k], out_v[k] = gk.reshape(shape), dl.reshape(shape), nm.reshape(shape), nv.reshape(shape)

    order = ["rel_bias_table", "attn_norm", "w_in", "sink_a", "w_branch_a", "w_branch_b", "w_out", "ffn_norm",
             "w_ffn_gate", "w_ffn_up", "conv_w", "conv_b", "w_ffn_down", "ple_norm", "w_ple_gate", "w_ple_proj",
             "final_norm"]
    return (loss, grad_x[None], *[out_g[k] for k in order], *[out_d[k] for k in order],
            *[out_m[k] for k in order], *[out_v[k] for k in order])
```

```python
import math

import jax
import jax.numpy as jnp
from jax import lax
from jax.experimental import pallas as pl
from jax.experimental.pallas import tpu as pltpu

F32 = jnp.float32
BF16 = jnp.bfloat16
MESH = pl.DeviceIdType.MESH
N_DEV = 8

HEAD_DIM = 128
A_Q_HEADS = 8
A_KV_HEADS = 2
A_GROUP = A_Q_HEADS // A_KV_HEADS
A_BLOCK = 128
B_PATTERNS = ((128, 1), (512, 4), (2048, 16))
B_HEADS_PER_GROUP = 4
B_HEADS = len(B_PATTERNS) * B_HEADS_PER_GROUP
B_BLOCK = 64
N_BUCKETS = 32
MAX_DISTANCE = 1024
A_Q_W = A_Q_HEADS * HEAD_DIM
A_KV_W = A_KV_HEADS * HEAD_DIM
B_W = B_HEADS * HEAD_DIM
B_OUT_W = B_HEADS_PER_GROUP * HEAD_DIM
COL_QA = 0
COL_KA = COL_QA + A_Q_W
COL_VA = COL_KA + A_KV_W
COL_QB = COL_VA + A_KV_W
COL_KB = COL_QB + B_W
COL_VB = COL_KB + B_W
COL_GATES = COL_VB + B_W
RMS_EPS = 1e-6
NEG_INF = -1e30
ATTN_SCALE = HEAD_DIM ** -0.5

ADAM_LR = 0.001
ADAM_B1 = 0.9
ADAM_B2 = 0.999
ADAM_EPS = 1e-08
ADAM_WD = 0.01
ADAM_STEP = 10

GELU_C = math.sqrt(2.0 / math.pi)
GELU_A = 0.044715

V7X_VMEM_BYTES = 64 * 1024 * 1024
VMEM_CEILING = V7X_VMEM_BYTES - 8 * 1024 * 1024
LANES = 128
SUBLANES = 8


def _pick(n, cands):
    for c in cands:
        if n % c == 0:
            return c
    return n


def _nbytes(shape, dtype):
    n = 1
    for d in shape:
        if d is not None:
            n *= d
    return n * jnp.dtype(dtype).itemsize


def _params(sem, est_bytes):
    limit = int(min(VMEM_CEILING, max(32 * 1024 * 1024, 2 * est_bytes + (8 << 20))))
    return pltpu.CompilerParams(dimension_semantics=sem, vmem_limit_bytes=limit)


def _mm(name, a, b, a_bs, a_im, b_bs, b_im, out_shape, out_dtype, o_bs, o_im, grid, dims,
        res=None, r_bs=None, r_im=None):
    nk = grid[-1]
    nax = len(grid)
    has_res = res is not None
    o_tile = tuple(d for d in o_bs if d is not None)

    def body(*refs):
        if has_res:
            a_ref, b_ref, r_ref, o_ref = refs[:4]
            rest = refs[4:]
        else:
            a_ref, b_ref, o_ref = refs[:3]
            r_ref = None
            rest = refs[3:]

        def prod():
            return lax.dot_general(a_ref[...].astype(BF16), b_ref[...].astype(BF16), (dims, ((), ())),
                                   preferred_element_type=F32)

        def finish(r):
            if r_ref is not None:
                r = r + r_ref[...].astype(F32)
            o_ref[...] = r.astype(o_ref.dtype)

        if nk == 1:
            finish(prod())
        else:
            acc = rest[0]
            k = pl.program_id(nax - 1)

            @pl.when(k == 0)
            def _():
                acc[...] = prod()

            @pl.when(k > 0)
            def _():
                acc[...] += prod()

            @pl.when(k == nk - 1)
            def _():
                finish(acc[...])

    in_specs = [pl.BlockSpec(a_bs, a_im), pl.BlockSpec(b_bs, b_im)]
    args = [a, b]
    est = _nbytes(a_bs, a.dtype) + _nbytes(b_bs, b.dtype) + _nbytes(o_bs, out_dtype) + 2 * _nbytes(o_tile, F32)
    if has_res:
        in_specs.append(pl.BlockSpec(r_bs, r_im))
        args.append(res)
        est += _nbytes(r_bs, res.dtype)
    scratch = [] if nk == 1 else [pltpu.VMEM(o_tile, F32)]
    sem = ("parallel",) * (nax - 1) + ("arbitrary",)
    return pl.pallas_call(
        body, name=name, grid=grid, in_specs=in_specs, out_specs=pl.BlockSpec(o_bs, o_im),
        out_shape=jax.ShapeDtypeStruct(out_shape, out_dtype), scratch_shapes=scratch,
        compiler_params=_params(sem, est))(*args)


TM_CANDS = (1024, 512, 256, 128, 64, 32, 16, 8)
TK_CANDS = (1024, 512, 256, 128)
TN_CANDS = (1024, 512, 256, 128)


def mm_cols(name, a, wg, out_dtype, fold):
    m, k = a.shape
    nj, _, n = wg.shape
    tm, tk = _pick(m, TM_CANDS), _pick(k, TK_CANDS)
    grid = (nj, m // tm, k // tk)
    if fold:
        shape, o_bs, o_im = (m, nj * n), (tm, n), (lambda j, i, kk: (i, j))
    else:
        shape, o_bs, o_im = (nj, m, n), (None, tm, n), (lambda j, i, kk: (j, i, 0))
    return _mm(name, a, wg, (tm, tk), lambda j, i, kk: (i, kk), (None, tk, n), lambda j, i, kk: (j, kk, 0),
               shape, out_dtype, o_bs, o_im, grid, ((1,), (0,)))


def mm_plain(name, a, w, out_dtype, res=None):
    m, k = a.shape
    n = w.shape[1]
    tm, tk, tn = _pick(m, TM_CANDS), _pick(k, TK_CANDS), _pick(n, TN_CANDS)
    grid = (n // tn, m // tm, k // tk)
    return _mm(name, a, w, (tm, tk), lambda j, i, kk: (i, kk), (tk, tn), lambda j, i, kk: (kk, j),
               (m, n), out_dtype, (tm, tn), lambda j, i, kk: (i, j), grid, ((1,), (0,)),
               res, (tm, tn), lambda j, i, kk: (i, j))


def mm_jsum(name, aj, wg, out_dtype, res=None):
    nj, m, ka = aj.shape
    n = wg.shape[2]
    tm, tn = _pick(m, TM_CANDS), _pick(n, TN_CANDS)
    grid = (m // tm, n // tn, nj)
    return _mm(name, aj, wg, (None, tm, ka), lambda i, jn, j: (j, i, 0), (None, ka, tn), lambda i, jn, j: (j, 0, jn),
               (m, n), out_dtype, (tm, tn), lambda i, jn, j: (i, jn), grid, ((1,), (0,)),
               res, (tm, tn), lambda i, jn, j: (i, jn))


def mm_tn_cols(name, a, g, nj, n, out_dtype, folded):
    s, kw = a.shape
    ts, tkw = _pick(s, TK_CANDS), _pick(kw, TM_CANDS)
    grid = (nj, kw // tkw, s // ts)
    if folded:
        g_bs, g_im = (ts, n), (lambda j, i, ss: (ss, j))
    else:
        g_bs, g_im = (None, ts, n), (lambda j, i, ss: (j, ss, 0))
    return _mm(name, a, g, (ts, tkw), lambda j, i, ss: (ss, i), g_bs, g_im,
               (nj, kw, n), out_dtype, (None, tkw, n), lambda j, i, ss: (j, i, 0), grid, ((0,), (0,)))


def mm_tn_plain(name, a, g, out_dtype):
    s, kw = a.shape
    n = g.shape[1]
    ts, tkw, tn = _pick(s, TK_CANDS), _pick(kw, TM_CANDS), _pick(n, TN_CANDS)
    grid = (kw // tkw, n // tn, s // ts)
    return _mm(name, a, g, (ts, tkw), lambda i, jn, ss: (ss, i), (ts, tn), lambda i, jn, ss: (ss, jn),
               (kw, n), out_dtype, (tkw, tn), lambda i, jn, ss: (i, jn), grid, ((0,), (0,)))


def mm_tn_j(name, aj, g, out_dtype):
    nj, s, ka = aj.shape
    n = g.shape[1]
    ts, tn = _pick(s, TK_CANDS), _pick(n, TN_CANDS)
    grid = (nj, n // tn, s // ts)
    return _mm(name, aj, g, (None, ts, ka), lambda j, jn, ss: (j, ss, 0), (ts, tn), lambda j, jn, ss: (ss, jn),
               (nj, ka, n), out_dtype, (None, ka, tn), lambda j, jn, ss: (j, 0, jn), grid, ((0,), (0,)))


def mm_nt_plain(name, g, w, out_dtype):
    m, n = g.shape
    k = w.shape[0]
    tm, tn, tkk = _pick(m, TM_CANDS), _pick(n, TK_CANDS), _pick(k, TN_CANDS)
    grid = (k // tkk, m // tm, n // tn)
    return _mm(name, g, w, (tm, tn), lambda kk, i, jn: (i, jn), (tkk, tn), lambda kk, i, jn: (kk, jn),
               (m, k), out_dtype, (tm, tkk), lambda kk, i, jn: (i, kk), grid, ((1,), (1,)))


def mm_nt_j(name, g, wg, out_dtype):
    m, n = g.shape
    nj, ka, _ = wg.shape
    tm, tn = _pick(m, TM_CANDS), _pick(n, TK_CANDS)
    grid = (nj, m // tm, n // tn)
    return _mm(name, g, wg, (tm, tn), lambda j, i, jn: (i, jn), (None, ka, tn), lambda j, i, jn: (j, 0, jn),
               (nj, m, ka), out_dtype, (None, tm, ka), lambda j, i, jn: (j, i, 0), grid, ((1,), (1,)))


def mm_nt_jsum(name, g, wg, out_dtype, folded, res=None):
    nj, k, n = wg.shape
    m = g.shape[0] if folded else g.shape[1]
    tm, tkk = _pick(m, TM_CANDS), _pick(k, TN_CANDS)
    grid = (m // tm, k // tkk, nj)
    if folded:
        g_bs, g_im = (tm, n), (lambda i, kk, j: (i, j))
    else:
        g_bs, g_im = (None, tm, n), (lambda i, kk, j: (j, i, 0))
    return _mm(name, g, wg, g_bs, g_im, (None, tkk, n), lambda i, kk, j: (j, kk, 0),
               (m, k), out_dtype, (tm, tkk), lambda i, kk, j: (i, kk), grid, ((1,), (1,)),
               res, (tm, tkk), lambda i, kk, j: (i, kk))


ROW_TILE_CANDS = (256, 128, 64, 32, 16, 8)


def _rstd(x):
    return lax.rsqrt(jnp.mean(x * x, axis=-1, keepdims=True) + RMS_EPS)


def _sigmoid(t):
    return 1.0 / (1.0 + jnp.exp(-t))


def rms_fwd(name, x, gain):
    s, d = x.shape
    ts = _pick(s, ROW_TILE_CANDS)

    def body(x_ref, g_ref, h_ref):
        xv = x_ref[...]
        h_ref[...] = ((xv * _rstd(xv)) * g_ref[...]).astype(h_ref.dtype)

    return pl.pallas_call(
        body, name=name, grid=(s // ts,),
        in_specs=[pl.BlockSpec((ts, d), lambda i: (i, 0)), pl.BlockSpec((1, d), lambda i: (0, 0))],
        out_specs=pl.BlockSpec((ts, d), lambda i: (i, 0)),
        out_shape=jax.ShapeDtypeStruct((s, d), BF16),
        compiler_params=_params(("parallel",), 3 * ts * d * 4))(x, gain)


def rms_bwd(name, x, gain, dh, dres):
    s, d = x.shape
    ts = _pick(s, ROW_TILE_CANDS)

    def body(x_ref, g_ref, dh_ref, dr_ref, dx_ref, dg_ref):
        xv = x_ref[...]
        r = _rstd(xv)
        xhat = xv * r
        dhv = dh_ref[...].astype(F32)
        dxhat = dhv * g_ref[...]
        dx_ref[...] = dr_ref[...] + r * (dxhat - xhat * jnp.mean(dxhat * xhat, axis=-1, keepdims=True))
        part = jnp.sum(dhv * xhat, axis=0, keepdims=True)

        @pl.when(pl.program_id(0) == 0)
        def _():
            dg_ref[...] = part

        @pl.when(pl.program_id(0) > 0)
        def _():
            dg_ref[...] += part

    row = pl.BlockSpec((ts, d), lambda i: (i, 0))
    vec = pl.BlockSpec((1, d), lambda i: (0, 0))
    return pl.pallas_call(
        body, name=name, grid=(s // ts,), in_specs=[row, vec, row, row], out_specs=[row, vec],
        out_shape=[jax.ShapeDtypeStruct((s, d), F32), jax.ShapeDtypeStruct((1, d), F32)],
        compiler_params=_params(("arbitrary",), 6 * ts * d * 4))(x, gain, dh, dres)


def gate_merge_fwd(name, proj, ta, tb, d):
    s = proj.shape[0]
    ts = _pick(s, ROW_TILE_CANDS)
    cb = COL_GATES // d

    def body(ga_ref, gb_ref, ta_ref, tb_ref, o_ref):
        o_ref[...] = (_sigmoid(ga_ref[...]) * ta_ref[...] + _sigmoid(gb_ref[...]) * tb_ref[...]).astype(o_ref.dtype)

    row = pl.BlockSpec((ts, d), lambda i: (i, 0))
    return pl.pallas_call(
        body, name=name, grid=(s // ts,),
        in_specs=[pl.BlockSpec((ts, d), lambda i: (i, cb)), pl.BlockSpec((ts, d), lambda i: (i, cb + 1)), row, row],
        out_specs=row, out_shape=jax.ShapeDtypeStruct((s, d), BF16),
        compiler_params=_params(("parallel",), 5 * ts * d * 4))(proj, proj, ta, tb)


def gate_merge_bwd(name, dmerged, proj, ta, tb, d):
    s = proj.shape[0]
    ts = _pick(s, ROW_TILE_CANDS)
    cb = COL_GATES // d

    def body(dm_ref, ga_ref, gb_ref, ta_ref, tb_ref, dta_ref, dtb_ref, dga_ref, dgb_ref):
        dm = dm_ref[...]
        sa = _sigmoid(ga_ref[...])
        sb = _sigmoid(gb_ref[...])
        dta_ref[...] = (dm * sa).astype(dta_ref.dtype)
        dtb_ref[...] = (dm * sb).astype(dtb_ref.dtype)
        dga_ref[...] = (dm * ta_ref[...] * (sa * (1.0 - sa))).astype(dga_ref.dtype)
        dgb_ref[...] = (dm * tb_ref[...] * (sb * (1.0 - sb))).astype(dgb_ref.dtype)

    row = pl.BlockSpec((ts, d), lambda i: (i, 0))
    out = jax.ShapeDtypeStruct((s, d), BF16)
    return pl.pallas_call(
        body, name=name, grid=(s // ts,),
        in_specs=[row, pl.BlockSpec((ts, d), lambda i: (i, cb)), pl.BlockSpec((ts, d), lambda i: (i, cb + 1)), row, row],
        out_specs=[row, row, row, row], out_shape=[out, out, out, out],
        compiler_params=_params(("parallel",), 8 * ts * d * 4))(dmerged, proj, proj, ta, tb)


def tail_fwd_bwd(name, x2, lp, pp, gain, target):
    s, d = x2.shape
    ts = _pick(s, ROW_TILE_CANDS)

    def body(x2_ref, lp_ref, pp_ref, g_ref, t_ref, loss_ref, dx3_ref, dlp_ref, dpp_ref, dg_ref):
        gp = _sigmoid(lp_ref[...])
        ppv = pp_ref[...]
        x3 = x2_ref[...] + gp * ppv
        r = _rstd(x3)
        xhat = x3 * r
        gv = g_ref[...]
        err = xhat * gv - t_ref[...]
        loss = jnp.sum(err * err) * (0.5 / d)
        dy = err * (1.0 / d)
        dxhat = dy * gv
        dx3 = r * (dxhat - xhat * jnp.mean(dxhat * xhat, axis=-1, keepdims=True))
        dx3_ref[...] = dx3
        dlp_ref[...] = (dx3 * ppv * (gp * (1.0 - gp))).astype(dlp_ref.dtype)
        dpp_ref[...] = (dx3 * gp).astype(dpp_ref.dtype)
        part = jnp.sum(dy * xhat, axis=0, keepdims=True)
        lossv = jnp.full((1, LANES), loss, F32)

        @pl.when(pl.program_id(0) == 0)
        def _():
            dg_ref[...] = part
            loss_ref[...] = lossv

        @pl.when(pl.program_id(0) > 0)
        def _():
            dg_ref[...] += part
            loss_ref[...] += lossv

    row = pl.BlockSpec((ts, d), lambda i: (i, 0))
    vec = pl.BlockSpec((1, d), lambda i: (0, 0))
    return pl.pallas_call(
        body, name=name, grid=(s // ts,), in_specs=[row, row, row, vec, row],
        out_specs=[pl.BlockSpec((1, LANES), lambda i: (0, 0)), row, row, row, vec],
        out_shape=[jax.ShapeDtypeStruct((1, LANES), F32), jax.ShapeDtypeStruct((s, d), F32),
                   jax.ShapeDtypeStruct((s, d), BF16), jax.ShapeDtypeStruct((s, d), BF16),
                   jax.ShapeDtypeStruct((1, d), F32)],
        compiler_params=_params(("arbitrary",), 9 * ts * d * 4))(x2, lp, pp, gain, target)


HALO = SUBLANES


def _shift_rows(cur, prev_row, next_row):
    ts = cur.shape[0]
    rid = lax.broadcasted_iota(jnp.int32, cur.shape, 0)
    down = jnp.where(rid == 0, prev_row, pltpu.roll(cur, 1, 0))
    up = jnp.where(rid == ts - 1, next_row, pltpu.roll(cur, ts - 1, 0))
    return down, up


def _halo_specs(ts, s, nf):
    nb = ts // HALO
    last = s // HALO - 1
    cur = pl.BlockSpec((None, ts, nf), lambda j, i: (j, i, 0))
    prev = pl.BlockSpec((None, HALO, nf), lambda j, i: (j, jnp.maximum(i * nb - 1, 0), 0))
    nxt = pl.BlockSpec((None, HALO, nf), lambda j, i: (j, jnp.minimum((i + 1) * nb, last), 0))
    return cur, prev, nxt


def _halo_rows(prev_ref, next_ref, n_tiles):
    i = pl.program_id(1)
    prev_row = jnp.where(i == 0, 0.0, prev_ref[HALO - 1:HALO, :].astype(F32))
    next_row = jnp.where(i == n_tiles - 1, 0.0, next_ref[0:1, :].astype(F32))
    return prev_row, next_row


def _gelu(g):
    t = jnp.tanh(GELU_C * (g + GELU_A * (g * g * g)))
    return 0.5 * g * (1.0 + t), t


def _conv(cur, down, up, cw_ref, cb_ref):
    return down * cw_ref[0:1, :] + cur * cw_ref[1:2, :] + up * cw_ref[2:3, :] + cb_ref[...]


def ffn_mid_fwd(name, gpre, u, cw, cb):
    nj, s, nf = gpre.shape
    ts = _pick(s, (512, 256, 128, 64, 32, 16, 8))
    n_tiles = s // ts
    cur, prev, nxt = _halo_specs(ts, s, nf)

    def body(g_ref, gp_ref, gn_ref, u_ref, cw_ref, cb_ref, z_ref):
        gv = g_ref[...]
        down, up = _shift_rows(gv, *_halo_rows(gp_ref, gn_ref, n_tiles))
        act, _ = _gelu(_conv(gv, down, up, cw_ref, cb_ref))
        z_ref[...] = (act * u_ref[...]).astype(z_ref.dtype)

    return pl.pallas_call(
        body, name=name, grid=(nj, n_tiles),
        in_specs=[cur, prev, nxt, cur, pl.BlockSpec((None, SUBLANES, nf), lambda j, i: (j, 0, 0)),
                  pl.BlockSpec((None, 1, nf), lambda j, i: (j, 0, 0))],
        out_specs=cur, out_shape=jax.ShapeDtypeStruct((nj, s, nf), BF16),
        compiler_params=_params(("parallel", "parallel"), 8 * ts * nf * 4))(gpre, gpre, gpre, u, cw, cb)


def ffn_mid_bwd1(name, gpre, u, dz, cw, cb):
    nj, s, nf = gpre.shape
    ts = _pick(s, (512, 256, 128, 64, 32, 16, 8))
    n_tiles = s // ts
    cur, prev, nxt = _halo_specs(ts, s, nf)

    def body(g_ref, gp_ref, gn_ref, u_ref, dz_ref, cw_ref, cb_ref, dg_ref, du_ref, dcw_ref):
        gv = g_ref[...]
        down, up = _shift_rows(gv, *_halo_rows(gp_ref, gn_ref, n_tiles))
        gc = _conv(gv, down, up, cw_ref, cb_ref)
        act, t = _gelu(gc)
        dzv = dz_ref[...].astype(F32)
        du_ref[...] = (dzv * act).astype(du_ref.dtype)
        dact = 0.5 * (1.0 + t) + 0.5 * gc * (1.0 - t * t) * (GELU_C * (1.0 + 3.0 * GELU_A * (gc * gc)))
        dg = dzv * u_ref[...] * dact
        dg_ref[...] = dg
        rows = [jnp.sum(dg * down, axis=0, keepdims=True), jnp.sum(dg * gv, axis=0, keepdims=True),
                jnp.sum(dg * up, axis=0, keepdims=True), jnp.sum(dg, axis=0, keepdims=True)]
        part = jnp.concatenate(rows + [jnp.zeros((SUBLANES - len(rows), nf), F32)], axis=0)

        @pl.when(pl.program_id(1) == 0)
        def _():
            dcw_ref[...] = part

        @pl.when(pl.program_id(1) > 0)
        def _():
            dcw_ref[...] += part

    small = pl.BlockSpec((None, SUBLANES, nf), lambda j, i: (j, 0, 0))
    return pl.pallas_call(
        body, name=name, grid=(nj, n_tiles),
        in_specs=[cur, prev, nxt, cur, cur, small, pl.BlockSpec((None, 1, nf), lambda j, i: (j, 0, 0))],
        out_specs=[cur, cur, small],
        out_shape=[jax.ShapeDtypeStruct((nj, s, nf), F32), jax.ShapeDtypeStruct((nj, s, nf), BF16),
                   jax.ShapeDtypeStruct((nj, SUBLANES, nf), F32)],
        compiler_params=_params(("parallel", "arbitrary"), 12 * ts * nf * 4))(gpre, gpre, gpre, u, dz, cw, cb)


def ffn_mid_bwd2(name, dg, cw):
    nj, s, nf = dg.shape
    ts = _pick(s, (512, 256, 128, 64, 32, 16, 8))
    n_tiles = s // ts
    cur, prev, nxt = _halo_specs(ts, s, nf)

    def body(g_ref, gp_ref, gn_ref, cw_ref, o_ref):
        gv = g_ref[...]
        down, up = _shift_rows(gv, *_halo_rows(gp_ref, gn_ref, n_tiles))
        o_ref[...] = (up * cw_ref[0:1, :] + gv * cw_ref[1:2, :] + down * cw_ref[2:3, :]).astype(o_ref.dtype)

    return pl.pallas_call(
        body, name=name, grid=(nj, n_tiles),
        in_specs=[cur, prev, nxt, pl.BlockSpec((None, SUBLANES, nf), lambda j, i: (j, 0, 0))],
        out_specs=cur, out_shape=jax.ShapeDtypeStruct((nj, s, nf), BF16),
        compiler_params=_params(("parallel", "parallel"), 6 * ts * nf * 4))(dg, dg, dg, cw)


def _t5_bucket(rel):
    half = N_BUCKETS // 2
    max_exact = half // 2
    n = jnp.abs(rel)
    side = jnp.where(rel > 0, half, 0)
    nf = jnp.maximum(n, 1).astype(F32)
    large = max_exact + (jnp.log(nf / max_exact) / math.log(MAX_DISTANCE / max_exact)
                         * (half - max_exact)).astype(jnp.int32)
    large = jnp.minimum(large, half - 1)
    return side + jnp.where(n < max_exact, n, large)


def bucket_tile(blk, dil):
    rel = (jnp.arange(3 * blk)[None, :] - blk) - jnp.arange(blk)[:, None]
    return _t5_bucket(rel * dil).astype(jnp.int32)


def bias_build(name, table_t, bucket, h0, nh):
    blk, kw = bucket.shape

    def body(t_ref, b_ref, o_ref):
        h = pl.program_id(0)
        bv = b_ref[...]
        acc = jnp.zeros((blk, kw), F32)
        for b in range(N_BUCKETS):
            acc = jnp.where(bv == b, t_ref[h0 + h, b], acc)
        o_ref[...] = acc

    return pl.pallas_call(
        body, name=name, grid=(nh,),
        in_specs=[pl.BlockSpec(memory_space=pltpu.SMEM), pl.BlockSpec((blk, kw), lambda h: (0, 0))],
        out_specs=pl.BlockSpec((None, blk, kw), lambda h: (h, 0, 0)),
        out_shape=jax.ShapeDtypeStruct((nh, blk, kw), F32),
        compiler_params=_params(("parallel",), 4 * blk * kw * 4))(table_t, bucket)


def table_grad(name, dbias, bucket):
    nh, blk, kw = dbias.shape

    def body(d_ref, b_ref, o_ref):
        bv = b_ref[...]
        dv = d_ref[...]
        lane = lax.broadcasted_iota(jnp.int32, (SUBLANES, LANES), 1)
        acc = jnp.zeros((SUBLANES, LANES), F32)
        for b in range(N_BUCKETS):
            acc = jnp.where(lane == b, jnp.sum(jnp.where(bv == b, dv, 0.0)), acc)
        o_ref[...] = acc

    return pl.pallas_call(
        body, name=name, grid=(nh,),
        in_specs=[pl.BlockSpec((None, blk, kw), lambda h: (h, 0, 0)), pl.BlockSpec((blk, kw), lambda h: (0, 0))],
        out_specs=pl.BlockSpec((None, SUBLANES, LANES), lambda h: (h, 0, 0)),
        out_shape=jax.ShapeDtypeStruct((nh, SUBLANES, LANES), F32),
        compiler_params=_params(("parallel",), 4 * blk * kw * 4))(dbias, bucket)


def _band_specs(proj_w, seg, dil, nh, group, cq, ck, cv):
    wb = proj_w // LANES
    q = pl.BlockSpec((seg, HEAD_DIM), lambda h, r: (0, r * wb + cq // LANES + h))
    k = pl.BlockSpec((seg, HEAD_DIM), lambda h, r: (0, r * wb + ck // LANES + h // group))
    v = pl.BlockSpec((seg, HEAD_DIM), lambda h, r: (0, r * wb + cv // LANES + h // group))
    o = pl.BlockSpec((seg, HEAD_DIM), lambda h, r: (0, r * nh + h))
    return q, k, v, o


def _stage_padded(dst, src_ref, blk, seg):
    dst[0:blk, :] = jnp.zeros((blk, HEAD_DIM), dst.dtype)
    dst[blk + seg:2 * blk + seg, :] = jnp.zeros((blk, HEAD_DIM), dst.dtype)
    dst[blk:blk + seg, :] = src_ref[...].astype(dst.dtype)


def _band_valid(off, blk, seg):
    qi = lax.broadcasted_iota(jnp.int32, (blk, 3 * blk), 0)
    ci = lax.broadcasted_iota(jnp.int32, (blk, 3 * blk), 1)
    kpos = off + ci - blk
    return (jnp.abs(ci - blk - qi) <= blk) & (kpos >= 0) & (kpos < seg)


def band_attn_fwd(name, proj, bias, sink, *, blk, dil, nh, group, cq, ck, cv):
    s, w = proj.shape
    seg = s // dil
    nb = seg // blk
    view = proj.reshape(seg, dil * w)
    q_spec, k_spec, v_spec, o_spec = _band_specs(w, seg, dil, nh, group, cq, ck, cv)
    has_sink = sink is not None

    def body(*refs):
        if has_sink:
            q_ref, k_ref, v_ref, b_ref, s_ref, o_ref, l_ref, kp, vp = refs
        else:
            q_ref, k_ref, v_ref, b_ref, o_ref, l_ref, kp, vp = refs
        _stage_padded(kp, k_ref, blk, seg)
        _stage_padded(vp, v_ref, blk, seg)
        bias_v = b_ref[...]
        if has_sink:
            sk = s_ref[pl.program_id(0)]

        def step(b, carry):
            off = pl.multiple_of(b * blk, blk)
            qv = q_ref[pl.ds(off, blk), :].astype(BF16)
            kw_ = kp[pl.ds(off, 3 * blk), :]
            vw_ = vp[pl.ds(off, 3 * blk), :]
            sc = lax.dot_general(qv, kw_, (((1,), (1,)), ((), ())), preferred_element_type=F32) * ATTN_SCALE + bias_v
            sc = jnp.where(_band_valid(off, blk, seg), sc, NEG_INF)
            m = jnp.max(sc, axis=-1, keepdims=True)
            if has_sink:
                m = jnp.maximum(m, sk)
            p = jnp.exp(sc - m)
            den = jnp.sum(p, axis=-1, keepdims=True)
            if has_sink:
                den = den + jnp.exp(sk - m)
            out = lax.dot_general(p.astype(BF16), vw_, (((1,), (0,)), ((), ())), preferred_element_type=F32)
            o_ref[pl.ds(off, blk), :] = out / den
            l_ref[pl.ds(off, blk), :] = jnp.broadcast_to(m + jnp.log(den), (blk, HEAD_DIM))
            return carry

        lax.fori_loop(0, nb, step, 0)

    in_specs = [q_spec, k_spec, v_spec, pl.BlockSpec((None, blk, 3 * blk), lambda h, r: (h, 0, 0))]
    args = [view, view, view, bias]
    if has_sink:
        in_specs.append(pl.BlockSpec(memory_space=pltpu.SMEM))
        args.append(sink)
    shape = jax.ShapeDtypeStruct((seg, dil * nh * HEAD_DIM), F32)
    out, lse = pl.pallas_call(
        body, name=name, grid=(nh, dil), in_specs=in_specs, out_specs=[o_spec, o_spec], out_shape=[shape, shape],
        scratch_shapes=[pltpu.VMEM((seg + 2 * blk, HEAD_DIM), BF16), pltpu.VMEM((seg + 2 * blk, HEAD_DIM), BF16)],
        compiler_params=_params(("parallel", "parallel"), 12 * seg * HEAD_DIM * 4))(*args)
    return out.reshape(s, nh * HEAD_DIM), lse.reshape(s, nh * HEAD_DIM)


def band_attn_bwd(name, proj, bias, sink, dout, out, lse, dlse, *, blk, dil, nh, group, cq, ck, cv):
    s, w = proj.shape
    seg = s // dil
    nb = seg // blk
    nkv = nh // group
    view = proj.reshape(seg, dil * w)
    q_spec, k_spec, v_spec, o_spec = _band_specs(w, seg, dil, nh, group, cq, ck, cv)
    kv_spec = pl.BlockSpec((seg, HEAD_DIM), lambda h, r: (0, r * nkv + h // group))
    has_sink = sink is not None
    has_dl = dlse is not None
    n_in = 7 + int(has_sink) + int(has_dl)

    def body(*refs):
        ins, rest = refs[:n_in], refs[n_in:]
        q_ref, k_ref, v_ref, b_ref, do_ref, o_ref, l_ref = ins[:7]
        s_ref = ins[7] if has_sink else None
        dl_ref = ins[n_in - 1] if has_dl else None
        if has_sink:
            dq_ref, dk_ref, dv_ref, db_ref, ds_ref, kp, vp, dkp, dvp, dsa = rest
        else:
            dq_ref, dk_ref, dv_ref, db_ref, kp, vp, dkp, dvp = rest
        h = pl.program_id(0)
        r = pl.program_id(1)
        _stage_padded(kp, k_ref, blk, seg)
        _stage_padded(vp, v_ref, blk, seg)
        dkp[...] = jnp.zeros_like(dkp)
        dvp[...] = jnp.zeros_like(dvp)
        bias_v = b_ref[...]
        if has_sink:
            sk = s_ref[h]
            dsa[...] = jnp.zeros_like(dsa)

        @pl.when(r == 0)
        def _():
            db_ref[...] = jnp.zeros_like(db_ref)

        def step(b, carry):
            off = pl.multiple_of(b * blk, blk)
            rows = pl.ds(off, blk)
            win = pl.ds(off, 3 * blk)
            qv = q_ref[rows, :].astype(BF16)
            kw_ = kp[win, :]
            vw_ = vp[win, :]
            sc = lax.dot_general(qv, kw_, (((1,), (1,)), ((), ())), preferred_element_type=F32) * ATTN_SCALE + bias_v
            sc = jnp.where(_band_valid(off, blk, seg), sc, NEG_INF)
            lv = l_ref[rows, :][:, 0:1]
            p = jnp.exp(sc - lv)
            dov = do_ref[rows, :]
            delta = jnp.sum(dov * o_ref[rows, :], axis=-1, keepdims=True)
            dob = dov.astype(BF16)
            dp = lax.dot_general(dob, vw_, (((1,), (1,)), ((), ())), preferred_element_type=F32)
            t = dp - delta
            if has_dl:
                t = t + dl_ref[rows, :][:, 0:1]
            ds = p * t
            dsb = (ds * ATTN_SCALE).astype(BF16)
            dq_ref[rows, :] = lax.dot_general(dsb, kw_, (((1,), (0,)), ((), ())),
                                              preferred_element_type=F32).astype(dq_ref.dtype)
            dkp[win, :] += lax.dot_general(dsb, qv, (((0,), (0,)), ((), ())), preferred_element_type=F32)
            dvp[win, :] += lax.dot_general(p.astype(BF16), dob, (((0,), (0,)), ((), ())), preferred_element_type=F32)
            db_ref[...] += ds
            if has_sink:
                dsa[...] += jnp.exp(sk - lv) * delta
            return carry

        lax.fori_loop(0, nb, step, 0)

        if group == 1:
            dk_ref[...] = dkp[blk:blk + seg, :].astype(dk_ref.dtype)
            dv_ref[...] = dvp[blk:blk + seg, :].astype(dv_ref.dtype)
        else:
            @pl.when(h % group == 0)
            def _():
                dk_ref[...] = dkp[blk:blk + seg, :]
                dv_ref[...] = dvp[blk:blk + seg, :]

            @pl.when(h % group != 0)
            def _():
                dk_ref[...] += dkp[blk:blk + seg, :]
                dv_ref[...] += dvp[blk:blk + seg, :]
        if has_sink:
            ds_ref[...] = jnp.full((SUBLANES, LANES), -jnp.sum(dsa[...]), F32)

    b_spec = pl.BlockSpec((None, blk, 3 * blk), lambda h, r: (h, 0, 0))
    in_specs = [q_spec, k_spec, v_spec, b_spec, o_spec, o_spec, o_spec]
    args = [view, view, view, bias, dout.reshape(seg, -1), out.reshape(seg, -1), lse.reshape(seg, -1)]
    if has_sink:
        in_specs.append(pl.BlockSpec(memory_space=pltpu.SMEM))
        args.append(sink)
    if has_dl:
        in_specs.append(o_spec)
        args.append(dlse.reshape(seg, -1))
    kv_dtype = BF16 if group == 1 else F32
    out_specs = [o_spec, kv_spec, kv_spec, b_spec]
    out_shape = [jax.ShapeDtypeStruct((seg, dil * nh * HEAD_DIM), BF16),
                 jax.ShapeDtypeStruct((seg, dil * nkv * HEAD_DIM), kv_dtype),
                 jax.ShapeDtypeStruct((seg, dil * nkv * HEAD_DIM), kv_dtype),
                 jax.ShapeDtypeStruct((nh, blk, 3 * blk), F32)]
    pad = seg + 2 * blk
    scratch = [pltpu.VMEM((pad, HEAD_DIM), BF16), pltpu.VMEM((pad, HEAD_DIM), BF16),
               pltpu.VMEM((pad, HEAD_DIM), F32), pltpu.VMEM((pad, HEAD_DIM), F32)]
    if has_sink:
        out_specs.append(pl.BlockSpec((None, SUBLANES, LANES), lambda h, r: (h, 0, 0)))
        out_shape.append(jax.ShapeDtypeStruct((nh, SUBLANES, LANES), F32))
        scratch.append(pltpu.VMEM((blk, 1), F32))
    res = pl.pallas_call(
        body, name=name, grid=(nh, dil), in_specs=in_specs, out_specs=out_specs, out_shape=out_shape,
        scratch_shapes=scratch,
        compiler_params=_params(("arbitrary", "arbitrary"), 24 * seg * HEAD_DIM * 4))(*args)
    dq = res[0].reshape(s, nh * HEAD_DIM)
    dk = res[1].reshape(s, nkv * HEAD_DIM)
    dv = res[2].reshape(s, nkv * HEAD_DIM)
    return dq, dk, dv, res[3], (res[4] if has_sink else None)


def dil_merge_fwd(name, outs, lses):
    s, w = outs[0].shape
    ts = _pick(s, ROW_TILE_CANDS)
    ng = len(outs)

    def body(*refs):
        o_refs, l_refs, y_ref = refs[:ng], refs[ng:2 * ng], refs[2 * ng]
        ls = [l[...] for l in l_refs]
        mx = ls[0]
        for l in ls[1:]:
            mx = jnp.maximum(mx, l)
        es = [jnp.exp(l - mx) for l in ls]
        tot = es[0]
        for e in es[1:]:
            tot = tot + e
        acc = (es[0] / tot) * o_refs[0][...]
        for e, o in zip(es[1:], o_refs[1:]):
            acc = acc + (e / tot) * o[...]
        y_ref[...] = acc.astype(y_ref.dtype)

    row = pl.BlockSpec((ts, w), lambda i: (i, 0))
    return pl.pallas_call(
        body, name=name, grid=(s // ts,), in_specs=[row] * (2 * ng), out_specs=row,
        out_shape=jax.ShapeDtypeStruct((s, w), BF16),
        compiler_params=_params(("parallel",), 10 * ts * w * 4))(*outs, *lses)


def dil_merge_bwd(name, dy, outs, lses):
    s, w = outs[0].shape
    ts = _pick(s, ROW_TILE_CANDS)
    ng = len(outs)
    nhead = w // HEAD_DIM

    def body(*refs):
        dy_ref = refs[0]
        o_refs, l_refs = refs[1:1 + ng], refs[1 + ng:1 + 2 * ng]
        do_refs, dl_refs = refs[1 + 2 * ng:1 + 3 * ng], refs[1 + 3 * ng:1 + 4 * ng]
        for hh in range(nhead):
            cols = slice(hh * HEAD_DIM, (hh + 1) * HEAD_DIM)
            dyv = dy_ref[:, cols]
            ls = [l[:, cols] for l in l_refs]
            mx = ls[0]
            for l in ls[1:]:
                mx = jnp.maximum(mx, l)
            es = [jnp.exp(l - mx) for l in ls]
            tot = es[0]
            for e in es[1:]:
                tot = tot + e
            alphas = [e / tot for e in es]
            dal = [jnp.broadcast_to(jnp.sum(dyv * o[:, cols], axis=-1, keepdims=True), dyv.shape) for o in o_refs]
            mean = alphas[0] * dal[0]
            for a, d in zip(alphas[1:], dal[1:]):
                mean = mean + a * d
            for g in range(ng):
                do_refs[g][:, cols] = alphas[g] * dyv
                dl_refs[g][:, cols] = alphas[g] * (dal[g] - mean)

    row = pl.BlockSpec((ts, w), lambda i: (i, 0))
    shape = jax.ShapeDtypeStruct((s, w), F32)
    res = pl.pallas_call(
        body, name=name, grid=(s // ts,), in_specs=[row] * (1 + 2 * ng), out_specs=[row] * (2 * ng),
        out_shape=[shape] * (2 * ng),
        compiler_params=_params(("parallel",), 16 * ts * w * 4))(dy, *outs, *lses)
    return res[:ng], res[ng:]


def _adamw(w, g, m, v):
    m = ADAM_B1 * m + (1.0 - ADAM_B1) * g
    v = ADAM_B2 * v + (1.0 - ADAM_B2) * (g * g)
    m_hat = m / (1.0 - ADAM_B1 ** ADAM_STEP)
    v_hat = v / (1.0 - ADAM_B2 ** ADAM_STEP)
    delta = -ADAM_LR * (m_hat / (jnp.sqrt(v_hat) + ADAM_EPS) + ADAM_WD * w)
    return delta, m, v


def _row_tile(r, c, budget=1 << 20):
    if r * c * 4 <= budget or r % SUBLANES:
        return r
    for t in (1024, 512, 256, 128, 64, 32, 16, 8):
        if r % t == 0 and t * c * 4 <= budget:
            return t
    return SUBLANES


def adam_small(name, g, w, m, v):
    def body(g_ref, w_ref, m_ref, v_ref, d_ref, nm_ref, nv_ref):
        d_ref[...], nm_ref[...], nv_ref[...] = _adamw(w_ref[...], g_ref[...], m_ref[...], v_ref[...])

    shape = jax.ShapeDtypeStruct(w.shape, F32)
    return pl.pallas_call(body, name=name, out_shape=[shape, shape, shape])(g, w, m, v)


def reduce_adam(name, mine, theirs, w, m, v):
    nq, r, c = mine.shape
    tr = _row_tile(r, c)

    def body(*refs):
        parts, (w_ref, m_ref, v_ref, g_ref, d_ref, nm_ref, nv_ref) = refs[:nq], refs[nq:]
        g = parts[0][...].astype(F32)
        for p_ref in parts[1:]:
            g = g + p_ref[...].astype(F32)
        g_ref[...] = g
        d_ref[...], nm_ref[...], nv_ref[...] = _adamw(w_ref[...], g, m_ref[...], v_ref[...])

    def slot(q):
        return pl.BlockSpec((None, tr, c), lambda i: (q, i, 0))

    row = pl.BlockSpec((tr, c), lambda i: (i, 0))
    shape = jax.ShapeDtypeStruct((r, c), F32)
    return pl.pallas_call(
        body, name=name, grid=(r // tr,), in_specs=[slot(q) for q in range(nq)] + [row, row, row],
        out_specs=[row] * 4, out_shape=[shape] * 4,
        compiler_params=_params(("parallel",), (nq * 2 + 7 * 4) * tr * c))(mine, *[theirs] * (nq - 1), w, m, v)


def _place():
    return lax.axis_index("x"), lax.axis_index("y"), lax.axis_index("c")


def _flip(pos, bits):
    return tuple((1 - p) if b else p for p, b in zip(pos, bits))


def _index(pos):
    return 4 * pos[0] + 2 * pos[1] + pos[2]


ANY = pl.BlockSpec(memory_space=pl.ANY)


HBM = pl.BlockSpec(memory_space=pltpu.HBM)
SEM = pl.BlockSpec(memory_space=pltpu.SEMAPHORE)
EFFECT = pltpu.SideEffectType.DATAFLOW_SIDE_EFFECTING
TO_SIBLING = (0, 0, 1)
TO_CHIPS = [(1, 0, 0), (0, 1, 0), (1, 1, 0)]


def _in_hbm(a):
    return pltpu.with_memory_space_constraint(a, pltpu.HBM)


def _token_value(token):
    return token[0, 0]


def split_start(name, srcs, lands, plan):
    n = len(srcs)
    n_copies = len(plan((0, 0, 0)))

    def body(*refs):
        ins, lnd = refs[:n], refs[n:2 * n]
        send_sems, recv_sems = refs[2 * n], refs[2 * n + 1]
        token = refs[-1]
        me = _place()
        for k, (a, sblk, lblk, rel) in enumerate(plan(me)):
            src = ins[a] if sblk is None else ins[a].at[sblk]
            pltpu.make_async_remote_copy(
                src_ref=src, dst_ref=lnd[a].at[lblk], send_sem=send_sems.at[k], recv_sem=recv_sems.at[k],
                device_id=_flip(me, rel), device_id_type=MESH).start()
        token[...] = jnp.zeros_like(token)

    outs = pl.pallas_call(
        body, name=name,
        out_shape=(pltpu.SemaphoreType.DMA((n_copies,)), pltpu.SemaphoreType.DMA((n_copies,)),
                   *[pltpu.HBM(a.shape, a.dtype) for a in srcs], *[pltpu.HBM(a.shape, a.dtype) for a in lands],
                   jax.ShapeDtypeStruct((SUBLANES, LANES), F32)),
        in_specs=[HBM] * (2 * n),
        out_specs=(SEM, SEM, *[HBM] * (2 * n), pl.BlockSpec(memory_space=pltpu.VMEM)),
        input_output_aliases={i: 2 + i for i in range(2 * n)},
        compiler_params=pltpu.CompilerParams(has_side_effects=EFFECT),
    )(*[_in_hbm(a) for a in srcs], *[_in_hbm(a) for a in lands])
    return outs[0], outs[1], list(outs[2:2 + n]), list(outs[2 + n:2 + 2 * n]), outs[-1]


def split_wait(name, send_sems, recv_sems, srcs, lands, plan, after):
    n = len(srcs)

    def body(*refs):
        ins, lnd = refs[:n], refs[n:2 * n]
        s_sems, r_sems = refs[2 * n], refs[2 * n + 1]
        me = _place()
        for k, (a, sblk, lblk, rel) in enumerate(plan(me)):
            src = ins[a] if sblk is None else ins[a].at[sblk]
            cp = pltpu.make_async_remote_copy(
                src_ref=src, dst_ref=lnd[a].at[lblk], send_sem=s_sems.at[k], recv_sem=r_sems.at[k],
                device_id=_flip(me, rel), device_id_type=MESH)
            cp.wait_send()
            cp.wait_recv()

    outs = pl.pallas_call(
        body, name=name,
        out_shape=(*[pltpu.HBM(a.shape, a.dtype) for a in srcs], *[pltpu.HBM(a.shape, a.dtype) for a in lands]),
        in_specs=[HBM] * (2 * n) + [SEM, SEM, ANY],
        out_specs=tuple([HBM] * (2 * n)),
        input_output_aliases={i: i for i in range(2 * n)},
        compiler_params=pltpu.CompilerParams(has_side_effects=EFFECT),
    )(*srcs, *lands, send_sems, recv_sems, after)
    return list(outs[:n]), list(outs[n:])


def ag_plan(n):
    def plan(me):
        return [(a, None, _index(me), rel) for a in range(n) for rel in [TO_SIBLING] + TO_CHIPS]
    return plan


def ag_start(name, shards):
    lands = [lax.empty((N_DEV,) + tuple(sh.shape), sh.dtype) for sh in shards]
    return split_start(name, shards, lands, ag_plan(len(shards)))


def ag_finish(name, shards, lands):
    n = len(shards)

    def body(*refs):
        ins, lnd = refs[:n], refs[2 * n:3 * n]
        send_sems, recv_sems, local_sems = refs[3 * n:]
        me = _place()
        sibling = _flip(me, TO_SIBLING)
        mine =[pltpu.make_async_copy(ins[a], lnd[a].at[_index(me)], local_sems.at[a]) for a in range(n)]
        for cp in mine:
            cp.start()
        copies = []
        for a in range(n):
            for j, rel in enumerate(TO_CHIPS):
                blk = lnd[a].at[_index(_flip(me, rel))]
                there = lnd[a].at[_index(_flip(sibling, rel))]
                cp = pltpu.make_async_remote_copy(
                    src_ref=blk, dst_ref=blk, send_sem=send_sems.at[a * 3 + j], recv_sem=recv_sems.at[a * 3 + j],
                    device_id=sibling, device_id_type=MESH)
                cp.start()
                copies.append((cp, pltpu.make_async_remote_copy(
                    src_ref=blk, dst_ref=there, send_sem=send_sems.at[a * 3 + j], recv_sem=recv_sems.at[a * 3 + j],
                    device_id=sibling, device_id_type=MESH)))
        for cp, arrival in copies:
            arrival.wait_recv()
        for cp, arrival in copies:
            cp.wait_send()
        for cp in mine:
            cp.wait()

    return pl.pallas_call(
        body, name=name, in_specs=[ANY] * (2 * n), out_specs=[ANY] * n,
        out_shape=[jax.ShapeDtypeStruct(l.shape, l.dtype) for l in lands],
        input_output_aliases={n + a: a for a in range(n)},
        scratch_shapes=[pltpu.SemaphoreType.DMA((3 * n,)), pltpu.SemaphoreType.DMA((3 * n,)),
                        pltpu.SemaphoreType.DMA((n,))],
    )(*shards, *lands)


REL = [(b >> 2 & 1, b >> 1 & 1, b & 1) for b in range(N_DEV)]


CHIP_REL = [(0, 0, 0)] + TO_CHIPS
N_CHIPS = len(CHIP_REL)


def rs_pair(name, parts):
    n = len(parts)

    def body(*refs):
        ins, own, got = refs[:n], refs[n:2 * n], refs[2 * n:3 * n]
        send_sems, recv_sems, local_sems = refs[3 * n:]
        me = _place()
        sibling = _flip(me, TO_SIBLING)
        local, remote = [], []
        for a in range(n):
            for q, rel in enumerate(CHIP_REL):
                k = a * N_CHIPS + q
                cp = pltpu.make_async_copy(ins[a].at[_index(_flip(me, rel))], own[a].at[q], local_sems.at[k])
                cp.start()
                local.append(cp)
                cp = pltpu.make_async_remote_copy(
                    src_ref=ins[a].at[_index(_flip(sibling, rel))], dst_ref=got[a].at[q], send_sem=send_sems.at[k],
                    recv_sem=recv_sems.at[k], device_id=sibling, device_id_type=MESH)
                cp.start()
                remote.append(cp)
        for cp in remote:
            cp.wait_recv()
        for cp in remote:
            cp.wait_send()
        for cp in local:
            cp.wait()

    shapes = [jax.ShapeDtypeStruct((N_CHIPS,) + tuple(p.shape[1:]), p.dtype) for p in parts]
    res = pl.pallas_call(
        body, name=name, in_specs=[ANY] * n, out_specs=[ANY] * (2 * n), out_shape=shapes + shapes,
        scratch_shapes=[pltpu.SemaphoreType.DMA((N_CHIPS * n,)), pltpu.SemaphoreType.DMA((N_CHIPS * n,)),
                        pltpu.SemaphoreType.DMA((N_CHIPS * n,))],
    )(*parts)
    return list(res[:n]), list(res[n:])


def pair_add(name, own, got):
    nq, r, c = own.shape
    tr = _row_tile(r, c)

    def body(a_ref, b_ref, o_ref):
        o_ref[...] = (a_ref[...].astype(F32) + b_ref[...].astype(F32)).astype(o_ref.dtype)

    spec = pl.BlockSpec((None, tr, c), lambda q, i: (q, i, 0))
    return pl.pallas_call(
        body, name=name, grid=(nq, r // tr), in_specs=[spec, spec], out_specs=spec,
        out_shape=jax.ShapeDtypeStruct(own.shape, own.dtype),
        compiler_params=_params(("parallel", "parallel"), 6 * tr * c * 2))(own, got)


def rs_plan(n):
    def plan(me):
        return [(a, q, q, CHIP_REL[q]) for a in range(n) for q in range(1, N_CHIPS)]
    return plan


def rs_start(name, sums):
    lands = [lax.empty(t.shape, t.dtype) for t in sums]
    return split_start(name, sums, lands, rs_plan(len(sums)))


def allreduce_small(name, pack):
    rows, lanes = pack.shape

    def body(x_ref, o_ref, land, send_sems, recv_sems):
        me = _place()
        idx = _index(me)
        land[idx] = x_ref[...]
        copies = []
        for r in range(1, N_DEV):
            peer = _flip(me, REL[r])
            cp = pltpu.make_async_remote_copy(
                src_ref=x_ref, dst_ref=land.at[idx], send_sem=send_sems.at[r - 1], recv_sem=recv_sems.at[r - 1],
                device_id=peer, device_id_type=MESH)
            cp.start()
            copies.append(cp)
        for cp in copies:
            cp.wait_recv()
        for cp in copies:
            cp.wait_send()
        acc = land[0]
        for i in range(1, N_DEV):
            acc = acc + land[i]
        o_ref[...] = acc

    return pl.pallas_call(
        body, name=name, in_specs=[pl.BlockSpec(memory_space=pltpu.VMEM)],
        out_specs=pl.BlockSpec(memory_space=pltpu.VMEM), out_shape=jax.ShapeDtypeStruct((rows, lanes), F32),
        scratch_shapes=[pltpu.VMEM((N_DEV, rows, lanes), F32), pltpu.SemaphoreType.DMA((7,)),
                        pltpu.SemaphoreType.DMA((7,))],
    )(pack)


def _pad_rows(a, rows):
    return jnp.pad(a, ((0, rows - a.shape[0]), (0, 0)))


def _as_tiles(vec):
    n = vec.shape[0]
    rows = -(-n // LANES)
    rows = -(-rows // SUBLANES) * SUBLANES
    return jnp.pad(vec, (0, rows * LANES - n)).reshape(rows, LANES)


def kernel(x, p, rel_bias_table, attn_norm, w_in, sink_a, w_branch_a, w_branch_b, w_out, ffn_norm, w_ffn_gate, w_ffn_up, conv_w, conv_b, w_ffn_down, ple_norm, w_ple_gate, w_ple_proj, final_norm, loss_target, m_rel_bias_table, m_attn_norm, m_w_in, m_sink_a, m_w_branch_a, m_w_branch_b, m_w_out, m_ffn_norm, m_w_ffn_gate, m_w_ffn_up, m_conv_w, m_conv_b, m_w_ffn_down, m_ple_norm, m_w_ple_gate, m_w_ple_proj, m_final_norm, v_rel_bias_table, v_attn_norm, v_w_in, v_sink_a, v_w_branch_a, v_w_branch_b, v_w_out, v_ffn_norm, v_w_ffn_gate, v_w_ffn_up, v_conv_w, v_conv_b, v_w_ffn_down, v_ple_norm, v_w_ple_gate, v_w_ple_proj, v_final_norm):
    xs = x[0]
    s, d = xs.shape
    ps = p[0, 0]
    target = loss_target[0]
    me = 4 * lax.axis_index("x") + 2 * lax.axis_index("y") + lax.axis_index("c")

    big = dict(w_in=w_in[0], w_branch_a=w_branch_a[0], w_branch_b=w_branch_b[0], w_out=w_out[0],
               w_ffn_gate=w_ffn_gate[0], w_ffn_up=w_ffn_up[0], w_ffn_down=w_ffn_down[0],
               w_ple_gate=w_ple_gate[0], w_ple_proj=w_ple_proj[0])
    big_m = dict(w_in=m_w_in[0], w_branch_a=m_w_branch_a[0], w_branch_b=m_w_branch_b[0], w_out=m_w_out[0],
                 w_ffn_gate=m_w_ffn_gate[0], w_ffn_up=m_w_ffn_up[0], w_ffn_down=m_w_ffn_down[0],
                 w_ple_gate=m_w_ple_gate[0], w_ple_proj=m_w_ple_proj[0])
    big_v = dict(w_in=v_w_in[0], w_branch_a=v_w_branch_a[0], w_branch_b=v_w_branch_b[0], w_out=v_w_out[0],
                 w_ffn_gate=v_w_ffn_gate[0], w_ffn_up=v_w_ffn_up[0], w_ffn_down=v_w_ffn_down[0],
                 w_ple_gate=v_w_ple_gate[0], w_ple_proj=v_w_ple_proj[0])
    names = list(big)
    nf = big["w_ffn_gate"].shape[1]

    shards = {k: big[k].astype(BF16) for k in names}
    shards["conv_w"] = _pad_rows(conv_w[0], SUBLANES)
    ag_groups = [["w_in"], ["w_branch_a", "w_branch_b", "w_out"], ["w_ffn_gate", "w_ffn_up", "conv_w"],
                 ["w_ffn_down", "w_ple_gate", "w_ple_proj"]]
    ag_started = []
    tok = jnp.zeros((), F32)
    for gi, grp in enumerate(ag_groups):
        s_sems, r_sems, srcs, lands, token = ag_start(f"ag_start{gi}", [shards[k] for k in grp])
        ag_started.append((s_sems, r_sems, srcs, lands))
        tok = tok + _token_value(token)
    wg = {}

    def gather(gi, after):
        s_sems, r_sems, srcs, lands = ag_started[gi]
        srcs, lands = split_wait(f"ag_wait{gi}", s_sems, r_sems, srcs, lands, ag_plan(len(srcs)), after)
        wg.update(zip(ag_groups[gi], ag_finish(f"ag_finish{gi}", srcs, lands)))

    cb = conv_b.reshape(N_DEV, 1, nf)

    table_t = rel_bias_table.T
    bucket_a = bucket_tile(A_BLOCK, 1)
    bias_a = bias_build("bias_a", table_t, bucket_a, 0, A_Q_HEADS)
    buckets_b = [bucket_tile(B_BLOCK, dil) for _, dil in B_PATTERNS]
    biases_b = [bias_build(f"bias_b{g}", table_t, buckets_b[g], A_Q_HEADS + g * B_HEADS_PER_GROUP, B_HEADS_PER_GROUP)
                for g in range(len(B_PATTERNS))]

    h = rms_fwd("rms_attn", xs, attn_norm + tok)
    gather(0, h)
    proj = mm_cols("proj_in", h, wg["w_in"], F32, fold=True)
    sink = sink_a[0]
    ya, lse_a = band_attn_fwd("attn_a_fwd", proj, bias_a, sink, blk=A_BLOCK, dil=1, nh=A_Q_HEADS, group=A_GROUP,
                              cq=COL_QA, ck=COL_KA, cv=COL_VA)
    outs_b, lses_b = [], []
    for g, (_, dil) in enumerate(B_PATTERNS):
        off = g * B_OUT_W
        o, l = band_attn_fwd(f"attn_b{g}_fwd", proj, biases_b[g], None, blk=B_BLOCK, dil=dil, nh=B_HEADS_PER_GROUP,
                             group=1, cq=COL_QB + off, ck=COL_KB + off, cv=COL_VB + off)
        outs_b.append(o)
        lses_b.append(l)
    yb = dil_merge_fwd("dil_merge_fwd", outs_b, lses_b)
    gather(1, yb)
    w_out_full = wg["w_out"].reshape(d, d)
    ta = mm_cols("branch_a", ya, wg["w_branch_a"], F32, fold=True)
    tb = mm_cols("branch_b", yb, wg["w_branch_b"], F32, fold=True)
    merged = gate_merge_fwd("gate_merge_fwd", proj, ta, tb, d)
    x1 = mm_plain("mix_out", merged, w_out_full, F32, res=xs)

    hf = rms_fwd("rms_ffn", x1, ffn_norm)
    gather(2, hf)
    cw = wg["conv_w"]
    gpre = mm_cols("ffn_gate", hf, wg["w_ffn_gate"], F32, fold=False)
    u = mm_cols("ffn_up", hf, wg["w_ffn_up"], F32, fold=False)
    z = ffn_mid_fwd("ffn_mid_fwd", gpre, u, cw, cb)
    gather(3, z)
    w_pg_full = wg["w_ple_gate"].reshape(d, d)
    x2 = mm_jsum("ffn_down", z, wg["w_ffn_down"], F32, res=x1)

    hp = rms_fwd("rms_ple", x2, ple_norm)
    lp = mm_plain("ple_gate", hp, w_pg_full, F32)
    pp = mm_cols("ple_proj", ps, wg["w_ple_proj"], F32, fold=True)
    loss_part, dx3, dlp, dpp, d_final = tail_fwd_bwd("tail", x2, lp, pp, final_norm.reshape(1, d), target)

    grads = {}
    rs_started = []

    def scatter(tag, keys):
        own, got = rs_pair(f"rs_pair_{tag}", [grads[k] for k in keys])
        sums = [pair_add(f"pair_add_{k}", o, g) for k, o, g in zip(keys, own, got)]
        s_sems, r_sems, srcs, lands, token = rs_start(f"rs_start_{tag}", sums)
        rs_started.append((tag, keys, s_sems, r_sems, srcs, lands))
        return _token_value(token)

    grads["w_ple_proj"] = mm_tn_cols("d_w_ple_proj", ps, dpp, N_DEV, big["w_ple_proj"].shape[1], BF16, folded=True)
    grads["w_ple_gate"] = mm_tn_plain("d_w_ple_gate", hp, dlp, BF16).reshape(N_DEV, d // N_DEV, d)
    tok = scatter("ple", ["w_ple_proj", "w_ple_gate"])
    dhp = mm_nt_plain("d_hp", dlp, w_pg_full, F32)
    dx2, d_ple = rms_bwd("rms_ple_bwd", x2, ple_norm + tok, dhp, dx3)

    dz = mm_nt_j("d_z", dx2, wg["w_ffn_down"], BF16)
    grads["w_ffn_down"] = mm_tn_j("d_w_ffn_down", z, dx2, BF16)
    tok = scatter("down", ["w_ffn_down"])
    dg, du, dcw = ffn_mid_bwd1("ffn_mid_bwd1", gpre, u, dz, cw, cb + tok)
    dgpre = ffn_mid_bwd2("ffn_mid_bwd2", dg, cw)
    grads["w_ffn_up"] = mm_tn_cols("d_w_ffn_up", hf, du, N_DEV, nf, BF16, folded=False)
    grads["w_ffn_gate"] = mm_tn_cols("d_w_ffn_gate", hf, dgpre, N_DEV, nf, BF16, folded=False)
    tok = scatter("upgate", ["w_ffn_up", "w_ffn_gate"])
    dhf = mm_nt_jsum("d_hf_up", du, wg["w_ffn_up"], F32, folded=False)
    dhf = mm_nt_jsum("d_hf_gate", dgpre, wg["w_ffn_gate"], F32, folded=False, res=dhf)
    dx1, d_ffn = rms_bwd("rms_ffn_bwd", x1, ffn_norm + tok, dhf, dx2)

    dmerged = mm_nt_plain("d_merged", dx1, w_out_full, F32)
    grads["w_out"] = mm_tn_plain("d_w_out", merged, dx1, BF16).reshape(N_DEV, d // N_DEV, d)
    dta, dtb, dga, dgb = gate_merge_bwd("gate_merge_bwd", dmerged, proj, ta, tb, d)
    grads["w_branch_a"] = mm_tn_cols("d_w_branch_a", ya, dta, N_DEV, big["w_branch_a"].shape[1], BF16, folded=True)
    grads["w_branch_b"] = mm_tn_cols("d_w_branch_b", yb, dtb, N_DEV, big["w_branch_b"].shape[1], BF16, folded=True)
    tok = scatter("mix", ["w_out", "w_branch_a", "w_branch_b"])
    dya = mm_nt_jsum("d_ya", dta, wg["w_branch_a"], F32, folded=True)
    dyb = mm_nt_jsum("d_yb", dtb, wg["w_branch_b"], F32, folded=True)
    dqa, dka, dva, dbias_a, dsink = band_attn_bwd(
        "attn_a_bwd", proj, bias_a, sink + tok, dya, ya, lse_a, None, blk=A_BLOCK, dil=1, nh=A_Q_HEADS, group=A_GROUP,
        cq=COL_QA, ck=COL_KA, cv=COL_VA)
    douts_b, dlses_b = dil_merge_bwd("dil_merge_bwd", dyb, outs_b, lses_b)
    dq_b, dk_b, dv_b, dbias_b = [], [], [], []
    for g, (_, dil) in enumerate(B_PATTERNS):
        off = g * B_OUT_W
        dq, dk, dv, db, _ = band_attn_bwd(
            f"attn_b{g}_bwd", proj, biases_b[g], None, douts_b[g], outs_b[g], lses_b[g], dlses_b[g], blk=B_BLOCK,
            dil=dil, nh=B_HEADS_PER_GROUP, group=1, cq=COL_QB + off, ck=COL_KB + off, cv=COL_VB + off)
        dq_b.append(dq)
        dk_b.append(dk)
        dv_b.append(dv)
        dbias_b.append(db)
    dproj = jnp.concatenate([dqa, dka.astype(BF16), dva.astype(BF16)] + dq_b + dk_b + dv_b + [dga, dgb], axis=1)
    grads["w_in"] = mm_tn_cols("d_w_in", h, dproj, N_DEV, big["w_in"].shape[1], BF16, folded=True)
    tok = scatter("in", ["w_in"])
    dh = mm_nt_jsum("d_h", dproj, wg["w_in"], F32, folded=True)
    grad_x, d_attn = rms_bwd("rms_attn_bwd", xs, attn_norm + tok, dh, dx1)

    dt_a = table_grad("table_grad_a", dbias_a, bucket_a)[:, 0, :N_BUCKETS]
    dt_b = [table_grad(f"table_grad_b{g}", dbias_b[g], buckets_b[g])[:, 0, :N_BUCKETS] for g in range(len(B_PATTERNS))]
    d_table_part = jnp.concatenate([dt_a] + dt_b, axis=0).T

    pieces = [
        ("loss", loss_part[0, :1]),
        ("table", d_table_part.reshape(-1)),
        ("attn_norm", d_attn.reshape(-1)),
        ("sink", dsink[:, 0, 0]),
        ("ffn_norm", d_ffn.reshape(-1)),
        ("conv_w", dcw[:, 0:3, :].reshape(-1)),
        ("conv_b", dcw[:, 3, :].reshape(-1)),
        ("ple_norm", d_ple.reshape(-1)),
        ("final_norm", d_final.reshape(-1)),
    ]
    tiles = [_as_tiles(v) for _, v in pieces]
    total = allreduce_small("allreduce_small", jnp.concatenate(tiles, axis=0))
    small = {}
    row = 0
    for (nm, v), t in zip(pieces, tiles):
        small[nm] = total[row:row + t.shape[0]].reshape(-1)[:v.shape[0]]
        row += t.shape[0]
    loss = small["loss"][0]
    g_small = dict(
        rel_bias_table=small["table"].reshape(rel_bias_table.shape),
        attn_norm=small["attn_norm"].reshape(attn_norm.shape),
        sink_a=small["sink"].reshape(sink_a.shape),
        ffn_norm=small["ffn_norm"].reshape(ffn_norm.shape),
        conv_w=lax.dynamic_index_in_dim(small["conv_w"].reshape(N_DEV, 3, nf), me, 0, keepdims=False)[None],
        conv_b=small["conv_b"].reshape(conv_b.shape),
        ple_norm=small["ple_norm"].reshape(ple_norm.shape),
        final_norm=small["final_norm"].reshape(1, d),
    )
    w_small = dict(rel_bias_table=(rel_bias_table, m_rel_bias_table, v_rel_bias_table),
                   attn_norm=(attn_norm, m_attn_norm, v_attn_norm), sink_a=(sink_a, m_sink_a, v_sink_a),
                   ffn_norm=(ffn_norm, m_ffn_norm, v_ffn_norm), conv_w=(conv_w, m_conv_w, v_conv_w),
                   conv_b=(conv_b, m_conv_b, v_conv_b), ple_norm=(ple_norm, m_ple_norm, v_ple_norm),
                   final_norm=(final_norm, m_final_norm, v_final_norm))

    out_g, out_d, out_m, out_v = {}, {}, {}, {}
    for k, (wv, mv, vv) in w_small.items():
        shape = wv.shape
        two_d = (1, shape[0]) if len(shape) == 1 else ((shape[0] * shape[1], shape[2]) if len(shape) == 3 else shape)
        gk = g_small[k].reshape(two_d)
        dl, nm, nv = adam_small("adam_" + k, gk, wv.reshape(two_d), mv.reshape(two_d), vv.reshape(two_d))
        out_g[k], out_d[k], out_m[k], out_v[k] = gk.reshape(shape), dl.reshape(shape), nm.reshape(shape), nv.reshape(shape)

    after = total
    for tag, keys, s_sems, r_sems, srcs, lands in rs_started:
        srcs, lands = split_wait(f"rs_wait_{tag}", s_sems, r_sems, srcs, lands, rs_plan(len(keys)), after)
        for k, mine, theirs in zip(keys, srcs, lands):
            g, dl, nm, nv = reduce_adam("adam_" + k, mine, theirs, big[k], big_m[k], big_v[k])
            out_g[k], out_d[k], out_m[k], out_v[k] = g[None], dl[None], nm[None], nv[None]
            after = dl

    order = ["rel_bias_table", "attn_norm", "w_in", "sink_a", "w_branch_a", "w_branch_b", "w_out", "ffn_norm",
             "w_ffn_gate", "w_ffn_up", "conv_w", "conv_b", "w_ffn_down", "ple_norm", "w_ple_gate", "w_ple_proj",
             "final_norm"]
    return (loss, grad_x[None], *[out_g[k] for k in order], *[out_d[k] for k in order],
            *[out_m[k] for k in order], *[out_v[k] for k in order])
```

```python
import math

import jax
import jax.numpy as jnp
from jax import lax
from jax.experimental import pallas as pl
from jax.experimental.pallas import tpu as pltpu

F32 = jnp.float32
BF16 = jnp.bfloat16
MESH = pl.DeviceIdType.MESH
N_DEV = 8

HEAD_DIM = 128
A_Q_HEADS = 8
A_KV_HEADS = 2
A_GROUP = A_Q_HEADS // A_KV_HEADS
A_BLOCK = 128
B_PATTERNS = ((128, 1), (512, 4), (2048, 16))
B_HEADS_PER_GROUP = 4
B_HEADS = len(B_PATTERNS) * B_HEADS_PER_GROUP
B_BLOCK = 64
N_BUCKETS = 32
MAX_DISTANCE = 1024
A_Q_W = A_Q_HEADS * HEAD_DIM
A_KV_W = A_KV_HEADS * HEAD_DIM
B_W = B_HEADS * HEAD_DIM
B_OUT_W = B_HEADS_PER_GROUP * HEAD_DIM
COL_QA = 0
COL_KA = COL_QA + A_Q_W
COL_VA = COL_KA + A_KV_W
COL_QB = COL_VA + A_KV_W
COL_KB = COL_QB + B_W
COL_VB = COL_KB + B_W
COL_GATES = COL_VB + B_W
RMS_EPS = 1e-6
NEG_INF = -1e30
ATTN_SCALE = HEAD_DIM ** -0.5

ADAM_LR = 0.001
ADAM_B1 = 0.9
ADAM_B2 = 0.999
ADAM_EPS = 1e-08
ADAM_WD = 0.01
ADAM_STEP = 10

GELU_C = math.sqrt(2.0 / math.pi)
GELU_A = 0.044715

V7X_VMEM_BYTES = 64 * 1024 * 1024
VMEM_CEILING = V7X_VMEM_BYTES - 8 * 1024 * 1024
LANES = 128
SUBLANES = 8


def _pick(n, cands):
    for c in cands:
        if n % c == 0:
            return c
    return n


def _nbytes(shape, dtype):
    n = 1
    for d in shape:
        if d is not None:
            n *= d
    return n * jnp.dtype(dtype).itemsize


def _params(sem, est_bytes):
    limit = int(min(VMEM_CEILING, max(32 * 1024 * 1024, 2 * est_bytes + (8 << 20))))
    return pltpu.CompilerParams(dimension_semantics=sem, vmem_limit_bytes=limit)


def _mm(name, a, b, a_bs, a_im, b_bs, b_im, out_shape, out_dtype, o_bs, o_im, grid, dims,
        res=None, r_bs=None, r_im=None):
    nk = grid[-1]
    nax = len(grid)
    has_res = res is not None
    o_tile = tuple(d for d in o_bs if d is not None)

    def body(*refs):
        if has_res:
            a_ref, b_ref, r_ref, o_ref = refs[:4]
            rest = refs[4:]
        else:
            a_ref, b_ref, o_ref = refs[:3]
            r_ref = None
            rest = refs[3:]

        def prod():
            return lax.dot_general(a_ref[...].astype(BF16), b_ref[...].astype(BF16), (dims, ((), ())),
                                   preferred_element_type=F32)

        def finish(r):
            if r_ref is not None:
                r = r + r_ref[...].astype(F32)
            o_ref[...] = r.astype(o_ref.dtype)

        if nk == 1:
            finish(prod())
        else:
            acc = rest[0]
            k = pl.program_id(nax - 1)

            @pl.when(k == 0)
            def _():
                acc[...] = prod()

            @pl.when(k > 0)
            def _():
                acc[...] += prod()

            @pl.when(k == nk - 1)
            def _():
                finish(acc[...])

    in_specs = [pl.BlockSpec(a_bs, a_im), pl.BlockSpec(b_bs, b_im)]
    args = [a, b]
    est = _nbytes(a_bs, a.dtype) + _nbytes(b_bs, b.dtype) + _nbytes(o_bs, out_dtype) + 2 * _nbytes(o_tile, F32)
    if has_res:
        in_specs.append(pl.BlockSpec(r_bs, r_im))
        args.append(res)
        est += _nbytes(r_bs, res.dtype)
    scratch = [] if nk == 1 else [pltpu.VMEM(o_tile, F32)]
    sem = ("parallel",) * (nax - 1) + ("arbitrary",)
    return pl.pallas_call(
        body, name=name, grid=grid, in_specs=in_specs, out_specs=pl.BlockSpec(o_bs, o_im),
        out_shape=jax.ShapeDtypeStruct(out_shape, out_dtype), scratch_shapes=scratch,
        compiler_params=_params(sem, est))(*args)


TM_CANDS = (1024, 512, 256, 128, 64, 32, 16, 8)
TK_CANDS = (1024, 512, 256, 128)
TN_CANDS = (1024, 512, 256, 128)


def mm_cols(name, a, wg, out_dtype, fold):
    m, k = a.shape
    nj, _, n = wg.shape
    tm, tk = _pick(m, TM_CANDS), _pick(k, TK_CANDS)
    grid = (nj, m // tm, k // tk)
    if fold:
        shape, o_bs, o_im = (m, nj * n), (tm, n), (lambda j, i, kk: (i, j))
    else:
        shape, o_bs, o_im = (nj, m, n), (None, tm, n), (lambda j, i, kk: (j, i, 0))
    return _mm(name, a, wg, (tm, tk), lambda j, i, kk: (i, kk), (None, tk, n), lambda j, i, kk: (j, kk, 0),
               shape, out_dtype, o_bs, o_im, grid, ((1,), (0,)))


def mm_plain(name, a, w, out_dtype, res=None):
    m, k = a.shape
    n = w.shape[1]
    tm, tk, tn = _pick(m, TM_CANDS), _pick(k, TK_CANDS), _pick(n, TN_CANDS)
    grid = (n // tn, m // tm, k // tk)
    return _mm(name, a, w, (tm, tk), lambda j, i, kk: (i, kk), (tk, tn), lambda j, i, kk: (kk, j),
               (m, n), out_dtype, (tm, tn), lambda j, i, kk: (i, j), grid, ((1,), (0,)),
               res, (tm, tn), lambda j, i, kk: (i, j))


def mm_jsum(name, aj, wg, out_dtype, res=None):
    nj, m, ka = aj.shape
    n = wg.shape[2]
    tm, tn = _pick(m, TM_CANDS), _pick(n, TN_CANDS)
    grid = (m // tm, n // tn, nj)
    return _mm(name, aj, wg, (None, tm, ka), lambda i, jn, j: (j, i, 0), (None, ka, tn), lambda i, jn, j: (j, 0, jn),
               (m, n), out_dtype, (tm, tn), lambda i, jn, j: (i, jn), grid, ((1,), (0,)),
               res, (tm, tn), lambda i, jn, j: (i, jn))


def mm_tn_cols(name, a, g, nj, n, out_dtype, folded):
    s, kw = a.shape
    ts, tkw = _pick(s, TK_CANDS), _pick(kw, TM_CANDS)
    grid = (nj, kw // tkw, s // ts)
    if folded:
        g_bs, g_im = (ts, n), (lambda j, i, ss: (ss, j))
    else:
        g_bs, g_im = (None, ts, n), (lambda j, i, ss: (j, ss, 0))
    return _mm(name, a, g, (ts, tkw), lambda j, i, ss: (ss, i), g_bs, g_im,
               (nj, kw, n), out_dtype, (None, tkw, n), lambda j, i, ss: (j, i, 0), grid, ((0,), (0,)))


def mm_tn_plain(name, a, g, out_dtype):
    s, kw = a.shape
    n = g.shape[1]
    ts, tkw, tn = _pick(s, TK_CANDS), _pick(kw, TM_CANDS), _pick(n, TN_CANDS)
    grid = (kw // tkw, n // tn, s // ts)
    return _mm(name, a, g, (ts, tkw), lambda i, jn, ss: (ss, i), (ts, tn), lambda i, jn, ss: (ss, jn),
               (kw, n), out_dtype, (tkw, tn), lambda i, jn, ss: (i, jn), grid, ((0,), (0,)))


def mm_tn_j(name, aj, g, out_dtype):
    nj, s, ka = aj.shape
    n = g.shape[1]
    ts, tn = _pick(s, TK_CANDS), _pick(n, TN_CANDS)
    grid = (nj, n // tn, s // ts)
    return _mm(name, aj, g, (None, ts, ka), lambda j, jn, ss: (j, ss, 0), (ts, tn), lambda j, jn, ss: (ss, jn),
               (nj, ka, n), out_dtype, (None, ka, tn), lambda j, jn, ss: (j, 0, jn), grid, ((0,), (0,)))


def mm_nt_plain(name, g, w, out_dtype):
    m, n = g.shape
    k = w.shape[0]
    tm, tn, tkk = _pick(m, TM_CANDS), _pick(n, TK_CANDS), _pick(k, TN_CANDS)
    grid = (k // tkk, m // tm, n // tn)
    return _mm(name, g, w, (tm, tn), lambda kk, i, jn: (i, jn), (tkk, tn), lambda kk, i, jn: (kk, jn),
               (m, k), out_dtype, (tm, tkk), lambda kk, i, jn: (i, kk), grid, ((1,), (1,)))


def mm_nt_j(name, g, wg, out_dtype):
    m, n = g.shape
    nj, ka, _ = wg.shape
    tm, tn = _pick(m, TM_CANDS), _pick(n, TK_CANDS)
    grid = (nj, m // tm, n // tn)
    return _mm(name, g, wg, (tm, tn), lambda j, i, jn: (i, jn), (None, ka, tn), lambda j, i, jn: (j, 0, jn),
               (nj, m, ka), out_dtype, (None, tm, ka), lambda j, i, jn: (j, i, 0), grid, ((1,), (1,)))


def mm_nt_jsum(name, g, wg, out_dtype, folded, res=None):
    nj, k, n = wg.shape
    m = g.shape[0] if folded else g.shape[1]
    tm, tkk = _pick(m, TM_CANDS), _pick(k, TN_CANDS)
    grid = (m // tm, k // tkk, nj)
    if folded:
        g_bs, g_im = (tm, n), (lambda i, kk, j: (i, j))
    else:
        g_bs, g_im = (None, tm, n), (lambda i, kk, j: (j, i, 0))
    return _mm(name, g, wg, g_bs, g_im, (None, tkk, n), lambda i, kk, j: (j, kk, 0),
               (m, k), out_dtype, (tm, tkk), lambda i, kk, j: (i, kk), grid, ((1,), (1,)),
               res, (tm, tkk), lambda i, kk, j: (i, kk))


ROW_TILE_CANDS = (256, 128, 64, 32, 16, 8)


def _rstd(x):
    return lax.rsqrt(jnp.mean(x * x, axis=-1, keepdims=True) + RMS_EPS)


def _sigmoid(t):
    return 1.0 / (1.0 + jnp.exp(-t))


def rms_fwd(name, x, gain):
    s, d = x.shape
    ts = _pick(s, ROW_TILE_CANDS)

    def body(x_ref, g_ref, h_ref):
        xv = x_ref[...]
        h_ref[...] = ((xv * _rstd(xv)) * g_ref[...]).astype(h_ref.dtype)

    return pl.pallas_call(
        body, name=name, grid=(s // ts,),
        in_specs=[pl.BlockSpec((ts, d), lambda i: (i, 0)), pl.BlockSpec((1, d), lambda i: (0, 0))],
        out_specs=pl.BlockSpec((ts, d), lambda i: (i, 0)),
        out_shape=jax.ShapeDtypeStruct((s, d), BF16),
        compiler_params=_params(("parallel",), 3 * ts * d * 4))(x, gain)


def rms_bwd(name, x, gain, dh, dres):
    s, d = x.shape
    ts = _pick(s, ROW_TILE_CANDS)

    def body(x_ref, g_ref, dh_ref, dr_ref, dx_ref, dg_ref):
        xv = x_ref[...]
        r = _rstd(xv)
        xhat = xv * r
        dhv = dh_ref[...].astype(F32)
        dxhat = dhv * g_ref[...]
        dx_ref[...] = dr_ref[...] + r * (dxhat - xhat * jnp.mean(dxhat * xhat, axis=-1, keepdims=True))
        part = jnp.sum(dhv * xhat, axis=0, keepdims=True)

        @pl.when(pl.program_id(0) == 0)
        def _():
            dg_ref[...] = part

        @pl.when(pl.program_id(0) > 0)
        def _():
            dg_ref[...] += part

    row = pl.BlockSpec((ts, d), lambda i: (i, 0))
    vec = pl.BlockSpec((1, d), lambda i: (0, 0))
    return pl.pallas_call(
        body, name=name, grid=(s // ts,), in_specs=[row, vec, row, row], out_specs=[row, vec],
        out_shape=[jax.ShapeDtypeStruct((s, d), F32), jax.ShapeDtypeStruct((1, d), F32)],
        compiler_params=_params(("arbitrary",), 6 * ts * d * 4))(x, gain, dh, dres)


def gate_merge_fwd(name, proj, ta, tb, d):
    s = proj.shape[0]
    ts = _pick(s, ROW_TILE_CANDS)
    cb = COL_GATES // d

    def body(ga_ref, gb_ref, ta_ref, tb_ref, o_ref):
        o_ref[...] = (_sigmoid(ga_ref[...]) * ta_ref[...] + _sigmoid(gb_ref[...]) * tb_ref[...]).astype(o_ref.dtype)

    row = pl.BlockSpec((ts, d), lambda i: (i, 0))
    return pl.pallas_call(
        body, name=name, grid=(s // ts,),
        in_specs=[pl.BlockSpec((ts, d), lambda i: (i, cb)), pl.BlockSpec((ts, d), lambda i: (i, cb + 1)), row, row],
        out_specs=row, out_shape=jax.ShapeDtypeStruct((s, d), BF16),
        compiler_params=_params(("parallel",), 5 * ts * d * 4))(proj, proj, ta, tb)


def gate_merge_bwd(name, dmerged, proj, ta, tb, d):
    s = proj.shape[0]
    ts = _pick(s, ROW_TILE_CANDS)
    cb = COL_GATES // d

    def body(dm_ref, ga_ref, gb_ref, ta_ref, tb_ref, dta_ref, dtb_ref, dga_ref, dgb_ref):
        dm = dm_ref[...]
        sa = _sigmoid(ga_ref[...])
        sb = _sigmoid(gb_ref[...])
        dta_ref[...] = (dm * sa).astype(dta_ref.dtype)
        dtb_ref[...] = (dm * sb).astype(dtb_ref.dtype)
        dga_ref[...] = (dm * ta_ref[...] * (sa * (1.0 - sa))).astype(dga_ref.dtype)
        dgb_ref[...] = (dm * tb_ref[...] * (sb * (1.0 - sb))).astype(dgb_ref.dtype)

    row = pl.BlockSpec((ts, d), lambda i: (i, 0))
    out = jax.ShapeDtypeStruct((s, d), BF16)
    return pl.pallas_call(
        body, name=name, grid=(s // ts,),
        in_specs=[row, pl.BlockSpec((ts, d), lambda i: (i, cb)), pl.BlockSpec((ts, d), lambda i: (i, cb + 1)), row, row],
        out_specs=[row, row, row, row], out_shape=[out, out, out, out],
        compiler_params=_params(("parallel",), 8 * ts * d * 4))(dmerged, proj, proj, ta, tb)


def tail_fwd_bwd(name, x2, lp, pp, gain, target):
    s, d = x2.shape
    ts = _pick(s, ROW_TILE_CANDS)

    def body(x2_ref, lp_ref, pp_ref, g_ref, t_ref, loss_ref, dx3_ref, dlp_ref, dpp_ref, dg_ref):
        gp = _sigmoid(lp_ref[...])
        ppv = pp_ref[...]
        x3 = x2_ref[...] + gp * ppv
        r = _rstd(x3)
        xhat = x3 * r
        gv = g_ref[...]
        err = xhat * gv - t_ref[...]
        loss = jnp.sum(err * err) * (0.5 / d)
        dy = err * (1.0 / d)
        dxhat = dy * gv
        dx3 = r * (dxhat - xhat * jnp.mean(dxhat * xhat, axis=-1, keepdims=True))
        dx3_ref[...] = dx3
        dlp_ref[...] = (dx3 * ppv * (gp * (1.0 - gp))).astype(dlp_ref.dtype)
        dpp_ref[...] = (dx3 * gp).astype(dpp_ref.dtype)
        part = jnp.sum(dy * xhat, axis=0, keepdims=True)
        lossv = jnp.full((1, LANES), loss, F32)

        @pl.when(pl.program_id(0) == 0)
        def _():
            dg_ref[...] = part
            loss_ref[...] = lossv

        @pl.when(pl.program_id(0) > 0)
        def _():
            dg_ref[...] += part
            loss_ref[...] += lossv

    row = pl.BlockSpec((ts, d), lambda i: (i, 0))
    vec = pl.BlockSpec((1, d), lambda i: (0, 0))
    return pl.pallas_call(
        body, name=name, grid=(s // ts,), in_specs=[row, row, row, vec, row],
        out_specs=[pl.BlockSpec((1, LANES), lambda i: (0, 0)), row, row, row, vec],
        out_shape=[jax.ShapeDtypeStruct((1, LANES), F32), jax.ShapeDtypeStruct((s, d), F32),
                   jax.ShapeDtypeStruct((s, d), BF16), jax.ShapeDtypeStruct((s, d), BF16),
                   jax.ShapeDtypeStruct((1, d), F32)],
        compiler_params=_params(("arbitrary",), 9 * ts * d * 4))(x2, lp, pp, gain, target)


HALO = SUBLANES


def _shift_rows(cur, prev_row, next_row):
    ts = cur.shape[0]
    rid = lax.broadcasted_iota(jnp.int32, cur.shape, 0)
    down = jnp.where(rid == 0, prev_row, pltpu.roll(cur, 1, 0))
    up = jnp.where(rid == ts - 1, next_row, pltpu.roll(cur, ts - 1, 0))
    return down, up


def _halo_specs(ts, s, nf):
    nb = ts // HALO
    last = s // HALO - 1
    cur = pl.BlockSpec((None, ts, nf), lambda j, i: (j, i, 0))
    prev = pl.BlockSpec((None, HALO, nf), lambda j, i: (j, jnp.maximum(i * nb - 1, 0), 0))
    nxt = pl.BlockSpec((None, HALO, nf), lambda j, i: (j, jnp.minimum((i + 1) * nb, last), 0))
    return cur, prev, nxt


def _halo_rows(prev_ref, next_ref, n_tiles):
    i = pl.program_id(1)
    prev_row = jnp.where(i == 0, 0.0, prev_ref[HALO - 1:HALO, :].astype(F32))
    next_row = jnp.where(i == n_tiles - 1, 0.0, next_ref[0:1, :].astype(F32))
    return prev_row, next_row


def _gelu(g):
    t = jnp.tanh(GELU_C * (g + GELU_A * (g * g * g)))
    return 0.5 * g * (1.0 + t), t


def _conv(cur, down, up, cw_ref, cb_ref):
    return down * cw_ref[0:1, :] + cur * cw_ref[1:2, :] + up * cw_ref[2:3, :] + cb_ref[...]


def ffn_mid_fwd(name, gpre, u, cw, cb):
    nj, s, nf = gpre.shape
    ts = _pick(s, (512, 256, 128, 64, 32, 16, 8))
    n_tiles = s // ts
    cur, prev, nxt = _halo_specs(ts, s, nf)

    def body(g_ref, gp_ref, gn_ref, u_ref, cw_ref, cb_ref, z_ref):
        gv = g_ref[...]
        down, up = _shift_rows(gv, *_halo_rows(gp_ref, gn_ref, n_tiles))
        act, _ = _gelu(_conv(gv, down, up, cw_ref, cb_ref))
        z_ref[...] = (act * u_ref[...]).astype(z_ref.dtype)

    return pl.pallas_call(
        body, name=name, grid=(nj, n_tiles),
        in_specs=[cur, prev, nxt, cur, pl.BlockSpec((None, SUBLANES, nf), lambda j, i: (j, 0, 0)),
                  pl.BlockSpec((None, 1, nf), lambda j, i: (j, 0, 0))],
        out_specs=cur, out_shape=jax.ShapeDtypeStruct((nj, s, nf), BF16),
        compiler_params=_params(("parallel", "parallel"), 8 * ts * nf * 4))(gpre, gpre, gpre, u, cw, cb)


def ffn_mid_bwd1(name, gpre, u, dz, cw, cb):
    nj, s, nf = gpre.shape
    ts = _pick(s, (512, 256, 128, 64, 32, 16, 8))
    n_tiles = s // ts
    cur, prev, nxt = _halo_specs(ts, s, nf)

    def body(g_ref, gp_ref, gn_ref, u_ref, dz_ref, cw_ref, cb_ref, dg_ref, du_ref, dcw_ref):
        gv = g_ref[...]
        down, up = _shift_rows(gv, *_halo_rows(gp_ref, gn_ref, n_tiles))
        gc = _conv(gv, down, up, cw_ref, cb_ref)
        act, t = _gelu(gc)
        dzv = dz_ref[...].astype(F32)
        du_ref[...] = (dzv * act).astype(du_ref.dtype)
        dact = 0.5 * (1.0 + t) + 0.5 * gc * (1.0 - t * t) * (GELU_C * (1.0 + 3.0 * GELU_A * (gc * gc)))
        dg = dzv * u_ref[...] * dact
        dg_ref[...] = dg
        rows = [jnp.sum(dg * down, axis=0, keepdims=True), jnp.sum(dg * gv, axis=0, keepdims=True),
                jnp.sum(dg * up, axis=0, keepdims=True), jnp.sum(dg, axis=0, keepdims=True)]
        part = jnp.concatenate(rows + [jnp.zeros((SUBLANES - len(rows), nf), F32)], axis=0)

        @pl.when(pl.program_id(1) == 0)
        def _():
            dcw_ref[...] = part

        @pl.when(pl.program_id(1) > 0)
        def _():
            dcw_ref[...] += part

    small = pl.BlockSpec((None, SUBLANES, nf), lambda j, i: (j, 0, 0))
    return pl.pallas_call(
        body, name=name, grid=(nj, n_tiles),
        in_specs=[cur, prev, nxt, cur, cur, small, pl.BlockSpec((None, 1, nf), lambda j, i: (j, 0, 0))],
        out_specs=[cur, cur, small],
        out_shape=[jax.ShapeDtypeStruct((nj, s, nf), F32), jax.ShapeDtypeStruct((nj, s, nf), BF16),
                   jax.ShapeDtypeStruct((nj, SUBLANES, nf), F32)],
        compiler_params=_params(("parallel", "arbitrary"), 12 * ts * nf * 4))(gpre, gpre, gpre, u, dz, cw, cb)


def ffn_mid_bwd2(name, dg, cw):
    nj, s, nf = dg.shape
    ts = _pick(s, (512, 256, 128, 64, 32, 16, 8))
    n_tiles = s // ts
    cur, prev, nxt = _halo_specs(ts, s, nf)

    def body(g_ref, gp_ref, gn_ref, cw_ref, o_ref):
        gv = g_ref[...]
        down, up = _shift_rows(gv, *_halo_rows(gp_ref, gn_ref, n_tiles))
        o_ref[...] = (up * cw_ref[0:1, :] + gv * cw_ref[1:2, :] + down * cw_ref[2:3, :]).astype(o_ref.dtype)

    return pl.pallas_call(
        body, name=name, grid=(nj, n_tiles),
        in_specs=[cur, prev, nxt, pl.BlockSpec((None, SUBLANES, nf), lambda j, i: (j, 0, 0))],
        out_specs=cur, out_shape=jax.ShapeDtypeStruct((nj, s, nf), BF16),
        compiler_params=_params(("parallel", "parallel"), 6 * ts * nf * 4))(dg, dg, dg, cw)


def _t5_bucket(rel):
    half = N_BUCKETS // 2
    max_exact = half // 2
    n = jnp.abs(rel)
    side = jnp.where(rel > 0, half, 0)
    nf = jnp.maximum(n, 1).astype(F32)
    large = max_exact + (jnp.log(nf / max_exact) / math.log(MAX_DISTANCE / max_exact)
                         * (half - max_exact)).astype(jnp.int32)
    large = jnp.minimum(large, half - 1)
    return side + jnp.where(n < max_exact, n, large)


def bucket_tile(blk, dil):
    rel = (jnp.arange(3 * blk)[None, :] - blk) - jnp.arange(blk)[:, None]
    return _t5_bucket(rel * dil).astype(jnp.int32)


def bias_build(name, table_t, bucket, h0, nh):
    blk, kw = bucket.shape

    def body(t_ref, b_ref, o_ref):
        h = pl.program_id(0)
        bv = b_ref[...]
        acc = jnp.zeros((blk, kw), F32)
        for b in range(N_BUCKETS):
            acc = jnp.where(bv == b, t_ref[h0 + h, b], acc)
        o_ref[...] = acc

    return pl.pallas_call(
        body, name=name, grid=(nh,),
        in_specs=[pl.BlockSpec(memory_space=pltpu.SMEM), pl.BlockSpec((blk, kw), lambda h: (0, 0))],
        out_specs=pl.BlockSpec((None, blk, kw), lambda h: (h, 0, 0)),
        out_shape=jax.ShapeDtypeStruct((nh, blk, kw), F32),
        compiler_params=_params(("parallel",), 4 * blk * kw * 4))(table_t, bucket)


def table_grad(name, dbias, bucket):
    nh, blk, kw = dbias.shape

    def body(d_ref, b_ref, o_ref):
        bv = b_ref[...]
        dv = d_ref[...]
        lane = lax.broadcasted_iota(jnp.int32, (SUBLANES, LANES), 1)
        acc = jnp.zeros((SUBLANES, LANES), F32)
        for b in range(N_BUCKETS):
            acc = jnp.where(lane == b, jnp.sum(jnp.where(bv == b, dv, 0.0)), acc)
        o_ref[...] = acc

    return pl.pallas_call(
        body, name=name, grid=(nh,),
        in_specs=[pl.BlockSpec((None, blk, kw), lambda h: (h, 0, 0)), pl.BlockSpec((blk, kw), lambda h: (0, 0))],
        out_specs=pl.BlockSpec((None, SUBLANES, LANES), lambda h: (h, 0, 0)),
        out_shape=jax.ShapeDtypeStruct((nh, SUBLANES, LANES), F32),
        compiler_params=_params(("parallel",), 4 * blk * kw * 4))(dbias, bucket)


def _band_specs(proj_w, seg, dil, nh, group, cq, ck, cv):
    wb = proj_w // LANES
    q = pl.BlockSpec((seg, HEAD_DIM), lambda h, r: (0, r * wb + cq // LANES + h))
    k = pl.BlockSpec((seg, HEAD_DIM), lambda h, r: (0, r * wb + ck // LANES + h // group))
    v = pl.BlockSpec((seg, HEAD_DIM), lambda h, r: (0, r * wb + cv // LANES + h // group))
    o = pl.BlockSpec((seg, HEAD_DIM), lambda h, r: (0, r * nh + h))
    return q, k, v, o


def _stage_padded(dst, src_ref, blk, seg):
    dst[0:blk, :] = jnp.zeros((blk, HEAD_DIM), dst.dtype)
    dst[blk + seg:2 * blk + seg, :] = jnp.zeros((blk, HEAD_DIM), dst.dtype)
    dst[blk:blk + seg, :] = src_ref[...].astype(dst.dtype)


def _band_valid(off, blk, seg):
    qi = lax.broadcasted_iota(jnp.int32, (blk, 3 * blk), 0)
    ci = lax.broadcasted_iota(jnp.int32, (blk, 3 * blk), 1)
    kpos = off + ci - blk
    return (jnp.abs(ci - blk - qi) <= blk) & (kpos >= 0) & (kpos < seg)


def band_attn_fwd(name, proj, bias, sink, *, blk, dil, nh, group, cq, ck, cv):
    s, w = proj.shape
    seg = s // dil
    nb = seg // blk
    view = proj.reshape(seg, dil * w)
    q_spec, k_spec, v_spec, o_spec = _band_specs(w, seg, dil, nh, group, cq, ck, cv)
    has_sink = sink is not None

    def body(*refs):
        if has_sink:
            q_ref, k_ref, v_ref, b_ref, s_ref, o_ref, l_ref, kp, vp = refs
        else:
            q_ref, k_ref, v_ref, b_ref, o_ref, l_ref, kp, vp = refs
        _stage_padded(kp, k_ref, blk, seg)
        _stage_padded(vp, v_ref, blk, seg)
        bias_v = b_ref[...]
        if has_sink:
            sk = s_ref[pl.program_id(0)]

        def step(b, carry):
            off = pl.multiple_of(b * blk, blk)
            qv = q_ref[pl.ds(off, blk), :].astype(BF16)
            kw_ = kp[pl.ds(off, 3 * blk), :]
            vw_ = vp[pl.ds(off, 3 * blk), :]
            sc = lax.dot_general(qv, kw_, (((1,), (1,)), ((), ())), preferred_element_type=F32) * ATTN_SCALE + bias_v
            sc = jnp.where(_band_valid(off, blk, seg), sc, NEG_INF)
            m = jnp.max(sc, axis=-1, keepdims=True)
            if has_sink:
                m = jnp.maximum(m, sk)
            p = jnp.exp(sc - m)
            den = jnp.sum(p, axis=-1, keepdims=True)
            if has_sink:
                den = den + jnp.exp(sk - m)
            out = lax.dot_general(p.astype(BF16), vw_, (((1,), (0,)), ((), ())), preferred_element_type=F32)
            o_ref[pl.ds(off, blk), :] = out / den
            l_ref[pl.ds(off, blk), :] = jnp.broadcast_to(m + jnp.log(den), (blk, HEAD_DIM))
            return carry

        lax.fori_loop(0, nb, step, 0)

    in_specs = [q_spec, k_spec, v_spec, pl.BlockSpec((None, blk, 3 * blk), lambda h, r: (h, 0, 0))]
    args = [view, view, view, bias]
    if has_sink:
        in_specs.append(pl.BlockSpec(memory_space=pltpu.SMEM))
        args.append(sink)
    shape = jax.ShapeDtypeStruct((seg, dil * nh * HEAD_DIM), F32)
    out, lse = pl.pallas_call(
        body, name=name, grid=(nh, dil), in_specs=in_specs, out_specs=[o_spec, o_spec], out_shape=[shape, shape],
        scratch_shapes=[pltpu.VMEM((seg + 2 * blk, HEAD_DIM), BF16), pltpu.VMEM((seg + 2 * blk, HEAD_DIM), BF16)],
        compiler_params=_params(("parallel", "parallel"), 12 * seg * HEAD_DIM * 4))(*args)
    return out.reshape(s, nh * HEAD_DIM), lse.reshape(s, nh * HEAD_DIM)


def band_attn_bwd(name, proj, bias, sink, dout, out, lse, dlse, *, blk, dil, nh, group, cq, ck, cv):
    s, w = proj.shape
    seg = s // dil
    nb = seg // blk
    nkv = nh // group
    view = proj.reshape(seg, dil * w)
    q_spec, k_spec, v_spec, o_spec = _band_specs(w, seg, dil, nh, group, cq, ck, cv)
    kv_spec = pl.BlockSpec((seg, HEAD_DIM), lambda h, r: (0, r * nkv + h // group))
    has_sink = sink is not None
    has_dl = dlse is not None
    n_in = 7 + int(has_sink) + int(has_dl)

    def body(*refs):
        ins, rest = refs[:n_in], refs[n_in:]
        q_ref, k_ref, v_ref, b_ref, do_ref, o_ref, l_ref = ins[:7]
        s_ref = ins[7] if has_sink else None
        dl_ref = ins[n_in - 1] if has_dl else None
        if has_sink:
            dq_ref, dk_ref, dv_ref, db_ref, ds_ref, kp, vp, dkp, dvp, dsa = rest
        else:
            dq_ref, dk_ref, dv_ref, db_ref, kp, vp, dkp, dvp = rest
        h = pl.program_id(0)
        r = pl.program_id(1)
        _stage_padded(kp, k_ref, blk, seg)
        _stage_padded(vp, v_ref, blk, seg)
        dkp[...] = jnp.zeros_like(dkp)
        dvp[...] = jnp.zeros_like(dvp)
        bias_v = b_ref[...]
        if has_sink:
            sk = s_ref[h]
            dsa[...] = jnp.zeros_like(dsa)

        @pl.when(r == 0)
        def _():
            db_ref[...] = jnp.zeros_like(db_ref)

        def step(b, carry):
            off = pl.multiple_of(b * blk, blk)
            rows = pl.ds(off, blk)
            win = pl.ds(off, 3 * blk)
            qv = q_ref[rows, :].astype(BF16)
            kw_ = kp[win, :]
            vw_ = vp[win, :]
            sc = lax.dot_general(qv, kw_, (((1,), (1,)), ((), ())), preferred_element_type=F32) * ATTN_SCALE + bias_v
            sc = jnp.where(_band_valid(off, blk, seg), sc, NEG_INF)
            lv = l_ref[rows, :][:, 0:1]
            p = jnp.exp(sc - lv)
            dov = do_ref[rows, :]
            delta = jnp.sum(dov * o_ref[rows, :], axis=-1, keepdims=True)
            dob = dov.astype(BF16)
            dp = lax.dot_general(dob, vw_, (((1,), (1,)), ((), ())), preferred_element_type=F32)
            t = dp - delta
            if has_dl:
                t = t + dl_ref[rows, :][:, 0:1]
            ds = p * t
            dsb = (ds * ATTN_SCALE).astype(BF16)
            dq_ref[rows, :] = lax.dot_general(dsb, kw_, (((1,), (0,)), ((), ())),
                                              preferred_element_type=F32).astype(dq_ref.dtype)
            dkp[win, :] += lax.dot_general(dsb, qv, (((0,), (0,)), ((), ())), preferred_element_type=F32)
            dvp[win, :] += lax.dot_general(p.astype(BF16), dob, (((0,), (0,)), ((), ())), preferred_element_type=F32)
            db_ref[...] += ds
            if has_sink:
                dsa[...] += jnp.exp(sk - lv) * delta
            return carry

        lax.fori_loop(0, nb, step, 0)

        if group == 1:
            dk_ref[...] = dkp[blk:blk + seg, :].astype(dk_ref.dtype)
            dv_ref[...] = dvp[blk:blk + seg, :].astype(dv_ref.dtype)
        else:
            @pl.when(h % group == 0)
            def _():
                dk_ref[...] = dkp[blk:blk + seg, :]
                dv_ref[...] = dvp[blk:blk + seg, :]

            @pl.when(h % group != 0)
            def _():
                dk_ref[...] += dkp[blk:blk + seg, :]
                dv_ref[...] += dvp[blk:blk + seg, :]
        if has_sink:
            ds_ref[...] = jnp.full((SUBLANES, LANES), -jnp.sum(dsa[...]), F32)

    b_spec = pl.BlockSpec((None, blk, 3 * blk), lambda h, r: (h, 0, 0))
    in_specs = [q_spec, k_spec, v_spec, b_spec, o_spec, o_spec, o_spec]
    args = [view, view, view, bias, dout.reshape(seg, -1), out.reshape(seg, -1), lse.reshape(seg, -1)]
    if has_sink:
        in_specs.append(pl.BlockSpec(memory_space=pltpu.SMEM))
        args.append(sink)
    if has_dl:
        in_specs.append(o_spec)
        args.append(dlse.reshape(seg, -1))
    kv_dtype = BF16 if group == 1 else F32
    out_specs = [o_spec, kv_spec, kv_spec, b_spec]
    out_shape = [jax.ShapeDtypeStruct((seg, dil * nh * HEAD_DIM), BF16),
                 jax.ShapeDtypeStruct((seg, dil * nkv * HEAD_DIM), kv_dtype),
                 jax.ShapeDtypeStruct((seg, dil * nkv * HEAD_DIM), kv_dtype),
                 jax.ShapeDtypeStruct((nh, blk, 3 * blk), F32)]
    pad = seg + 2 * blk
    scratch = [pltpu.VMEM((pad, HEAD_DIM), BF16), pltpu.VMEM((pad, HEAD_DIM), BF16),
               pltpu.VMEM((pad, HEAD_DIM), F32), pltpu.VMEM((pad, HEAD_DIM), F32)]
    if has_sink:
        out_specs.append(pl.BlockSpec((None, SUBLANES, LANES), lambda h, r: (h, 0, 0)))
        out_shape.append(jax.ShapeDtypeStruct((nh, SUBLANES, LANES), F32))
        scratch.append(pltpu.VMEM((blk, 1), F32))
    res = pl.pallas_call(
        body, name=name, grid=(nh, dil), in_specs=in_specs, out_specs=out_specs, out_shape=out_shape,
        scratch_shapes=scratch,
        compiler_params=_params(("arbitrary", "arbitrary"), 24 * seg * HEAD_DIM * 4))(*args)
    dq = res[0].reshape(s, nh * HEAD_DIM)
    dk = res[1].reshape(s, nkv * HEAD_DIM)
    dv = res[2].reshape(s, nkv * HEAD_DIM)
    return dq, dk, dv, res[3], (res[4] if has_sink else None)


def dil_merge_fwd(name, outs, lses):
    s, w = outs[0].shape
    ts = _pick(s, ROW_TILE_CANDS)
    ng = len(outs)

    def body(*refs):
        o_refs, l_refs, y_ref = refs[:ng], refs[ng:2 * ng], refs[2 * ng]
        ls = [l[...] for l in l_refs]
        mx = ls[0]
        for l in ls[1:]:
            mx = jnp.maximum(mx, l)
        es = [jnp.exp(l - mx) for l in ls]
        tot = es[0]
        for e in es[1:]:
            tot = tot + e
        acc = (es[0] / tot) * o_refs[0][...]
        for e, o in zip(es[1:], o_refs[1:]):
            acc = acc + (e / tot) * o[...]
        y_ref[...] = acc.astype(y_ref.dtype)

    row = pl.BlockSpec((ts, w), lambda i: (i, 0))
    return pl.pallas_call(
        body, name=name, grid=(s // ts,), in_specs=[row] * (2 * ng), out_specs=row,
        out_shape=jax.ShapeDtypeStruct((s, w), BF16),
        compiler_params=_params(("parallel",), 10 * ts * w * 4))(*outs, *lses)


def dil_merge_bwd(name, dy, outs, lses):
    s, w = outs[0].shape
    ts = _pick(s, ROW_TILE_CANDS)
    ng = len(outs)
    nhead = w // HEAD_DIM

    def body(*refs):
        dy_ref = refs[0]
        o_refs, l_refs = refs[1:1 + ng], refs[1 + ng:1 + 2 * ng]
        do_refs, dl_refs = refs[1 + 2 * ng:1 + 3 * ng], refs[1 + 3 * ng:1 + 4 * ng]
        for hh in range(nhead):
            cols = slice(hh * HEAD_DIM, (hh + 1) * HEAD_DIM)
            dyv = dy_ref[:, cols]
            ls = [l[:, cols] for l in l_refs]
            mx = ls[0]
            for l in ls[1:]:
                mx = jnp.maximum(mx, l)
            es = [jnp.exp(l - mx) for l in ls]
            tot = es[0]
            for e in es[1:]:
                tot = tot + e
            alphas = [e / tot for e in es]
            dal = [jnp.broadcast_to(jnp.sum(dyv * o[:, cols], axis=-1, keepdims=True), dyv.shape) for o in o_refs]
            mean = alphas[0] * dal[0]
            for a, d in zip(alphas[1:], dal[1:]):
                mean = mean + a * d
            for g in range(ng):
                do_refs[g][:, cols] = alphas[g] * dyv
                dl_refs[g][:, cols] = alphas[g] * (dal[g] - mean)

    row = pl.BlockSpec((ts, w), lambda i: (i, 0))
    shape = jax.ShapeDtypeStruct((s, w), F32)
    res = pl.pallas_call(
        body, name=name, grid=(s // ts,), in_specs=[row] * (1 + 2 * ng), out_specs=[row] * (2 * ng),
        out_shape=[shape] * (2 * ng),
        compiler_params=_params(("parallel",), 16 * ts * w * 4))(dy, *outs, *lses)
    return res[:ng], res[ng:]


def _adamw(w, g, m, v):
    m = ADAM_B1 * m + (1.0 - ADAM_B1) * g
    v = ADAM_B2 * v + (1.0 - ADAM_B2) * (g * g)
    m_hat = m / (1.0 - ADAM_B1 ** ADAM_STEP)
    v_hat = v / (1.0 - ADAM_B2 ** ADAM_STEP)
    delta = -ADAM_LR * (m_hat / (jnp.sqrt(v_hat) + ADAM_EPS) + ADAM_WD * w)
    return delta, m, v


def _row_tile(r, c, budget=1 << 20):
    if r * c * 4 <= budget or r % SUBLANES:
        return r
    for t in (1024, 512, 256, 128, 64, 32, 16, 8):
        if r % t == 0 and t * c * 4 <= budget:
            return t
    return SUBLANES


def adam_small(name, g, w, m, v):
    def body(g_ref, w_ref, m_ref, v_ref, d_ref, nm_ref, nv_ref):
        d_ref[...], nm_ref[...], nv_ref[...] = _adamw(w_ref[...], g_ref[...], m_ref[...], v_ref[...])

    shape = jax.ShapeDtypeStruct(w.shape, F32)
    return pl.pallas_call(body, name=name, out_shape=[shape, shape, shape])(g, w, m, v)


def reduce_adam(name, mine, theirs, w, m, v):
    nq, r, c = mine.shape
    tr = _row_tile(r, c)

    def body(*refs):
        parts, (w_ref, m_ref, v_ref, g_ref, d_ref, nm_ref, nv_ref) = refs[:nq], refs[nq:]
        g = parts[0][...].astype(F32)
        for p_ref in parts[1:]:
            g = g + p_ref[...].astype(F32)
        g_ref[...] = g
        d_ref[...], nm_ref[...], nv_ref[...] = _adamw(w_ref[...], g, m_ref[...], v_ref[...])

    def slot(q):
        return pl.BlockSpec((None, tr, c), lambda i: (q, i, 0))

    row = pl.BlockSpec((tr, c), lambda i: (i, 0))
    shape = jax.ShapeDtypeStruct((r, c), F32)
    return pl.pallas_call(
        body, name=name, grid=(r // tr,), in_specs=[slot(q) for q in range(nq)] + [row, row, row],
        out_specs=[row] * 4, out_shape=[shape] * 4,
        compiler_params=_params(("parallel",), (nq * 2 + 7 * 4) * tr * c))(mine, *[theirs] * (nq - 1), w, m, v)


def _place():
    return lax.axis_index("x"), lax.axis_index("y"), lax.axis_index("c")


def _flip(pos, bits):
    return tuple((1 - p) if b else p for p, b in zip(pos, bits))


def _index(pos):
    return 4 * pos[0] + 2 * pos[1] + pos[2]


ANY = pl.BlockSpec(memory_space=pl.ANY)


HBM = pl.BlockSpec(memory_space=pltpu.HBM)
SEM = pl.BlockSpec(memory_space=pltpu.SEMAPHORE)
EFFECT = pltpu.SideEffectType.DATAFLOW_SIDE_EFFECTING
TO_SIBLING = (0, 0, 1)
TO_CHIPS = [(1, 0, 0), (0, 1, 0), (1, 1, 0)]


def _in_hbm(a):
    return pltpu.with_memory_space_constraint(a, pltpu.HBM)


def _token_value(token):
    return token[0, 0]


def split_start(name, srcs, lands, plan):
    n = len(srcs)
    n_copies = len(plan((0, 0, 0)))

    def body(*refs):
        ins, lnd = refs[:n], refs[n:2 * n]
        send_sems, recv_sems = refs[2 * n], refs[2 * n + 1]
        token = refs[-1]
        me = _place()
        for k, (a, sblk, lblk, rel) in enumerate(plan(me)):
            src = ins[a] if sblk is None else ins[a].at[sblk]
            pltpu.make_async_remote_copy(
                src_ref=src, dst_ref=lnd[a].at[lblk], send_sem=send_sems.at[k], recv_sem=recv_sems.at[k],
                device_id=_flip(me, rel), device_id_type=MESH).start()
        token[...] = jnp.zeros_like(token)

    outs = pl.pallas_call(
        body, name=name,
        out_shape=(pltpu.SemaphoreType.DMA((n_copies,)), pltpu.SemaphoreType.DMA((n_copies,)),
                   *[pltpu.HBM(a.shape, a.dtype) for a in srcs], *[pltpu.HBM(a.shape, a.dtype) for a in lands],
                   jax.ShapeDtypeStruct((SUBLANES, LANES), F32)),
        in_specs=[HBM] * (2 * n),
        out_specs=(SEM, SEM, *[HBM] * (2 * n), pl.BlockSpec(memory_space=pltpu.VMEM)),
        input_output_aliases={i: 2 + i for i in range(2 * n)},
        compiler_params=pltpu.CompilerParams(has_side_effects=EFFECT),
    )(*[_in_hbm(a) for a in srcs], *[_in_hbm(a) for a in lands])
    return outs[0], outs[1], list(outs[2:2 + n]), list(outs[2 + n:2 + 2 * n]), outs[-1]


def split_wait(name, send_sems, recv_sems, srcs, lands, plan, after):
    n = len(srcs)

    def body(*refs):
        ins, lnd = refs[:n], refs[n:2 * n]
        s_sems, r_sems = refs[2 * n], refs[2 * n + 1]
        me = _place()
        for k, (a, sblk, lblk, rel) in enumerate(plan(me)):
            src = ins[a] if sblk is None else ins[a].at[sblk]
            cp = pltpu.make_async_remote_copy(
                src_ref=src, dst_ref=lnd[a].at[lblk], send_sem=s_sems.at[k], recv_sem=r_sems.at[k],
                device_id=_flip(me, rel), device_id_type=MESH)
            cp.wait_send()
            cp.wait_recv()

    outs = pl.pallas_call(
        body, name=name,
        out_shape=(*[pltpu.HBM(a.shape, a.dtype) for a in srcs], *[pltpu.HBM(a.shape, a.dtype) for a in lands]),
        in_specs=[HBM] * (2 * n) + [SEM, SEM, ANY],
        out_specs=tuple([HBM] * (2 * n)),
        input_output_aliases={i: i for i in range(2 * n)},
        compiler_params=pltpu.CompilerParams(has_side_effects=EFFECT),
    )(*srcs, *lands, send_sems, recv_sems, after)
    return list(outs[:n]), list(outs[n:])


def ag_plan(n):
    def plan(me):
        return [(a, None, _index(me), rel) for a in range(n) for rel in [TO_SIBLING] + TO_CHIPS]
    return plan


def ag_start(name, shards):
    lands = [lax.empty((N_DEV,) + tuple(sh.shape), sh.dtype) for sh in shards]
    return split_start(name, shards, lands, ag_plan(len(shards)))


def ag_finish(name, lands):
    n = len(lands)

    def body(*refs):
        lnd = refs[n:2 * n]
        send_sems, recv_sems = refs[2 * n:]
        me = _place()
        sibling = _flip(me, TO_SIBLING)
        copies = []
        for a in range(n):
            for j, rel in enumerate(TO_CHIPS):
                blk = lnd[a].at[_index(_flip(me, rel))]
                there = lnd[a].at[_index(_flip(sibling, rel))]
                cp = pltpu.make_async_remote_copy(
                    src_ref=blk, dst_ref=blk, send_sem=send_sems.at[a * 3 + j], recv_sem=recv_sems.at[a * 3 + j],
                    device_id=sibling, device_id_type=MESH)
                cp.start()
                copies.append((cp, pltpu.make_async_remote_copy(
                    src_ref=blk, dst_ref=there, send_sem=send_sems.at[a * 3 + j], recv_sem=recv_sems.at[a * 3 + j],
                    device_id=sibling, device_id_type=MESH)))
        for cp, arrival in copies:
            arrival.wait_recv()
        for cp, arrival in copies:
            cp.wait_send()

    return pl.pallas_call(
        body, name=name, in_specs=[ANY] * n, out_specs=[ANY] * n,
        out_shape=[jax.ShapeDtypeStruct(l.shape, l.dtype) for l in lands],
        input_output_aliases={a: a for a in range(n)},
        scratch_shapes=[pltpu.SemaphoreType.DMA((3 * n,)), pltpu.SemaphoreType.DMA((3 * n,))],
    )(*lands)


REL = [(b >> 2 & 1, b >> 1 & 1, b & 1) for b in range(N_DEV)]


CHIP_REL = [(0, 0, 0)] + TO_CHIPS
N_CHIPS = len(CHIP_REL)


def rs_pair(name, parts):
    n = len(parts)

    def body(*refs):
        ins, got = refs[:n], refs[n:2 * n]
        send_sems, recv_sems = refs[2 * n:]
        me = _place()
        sibling = _flip(me, TO_SIBLING)
        remote = []
        for a in range(n):
            for q, rel in enumerate(CHIP_REL):
                k = a * N_CHIPS + q
                cp = pltpu.make_async_remote_copy(
                    src_ref=ins[a].at[_index(_flip(sibling, rel))], dst_ref=got[a].at[q], send_sem=send_sems.at[k],
                    recv_sem=recv_sems.at[k], device_id=sibling, device_id_type=MESH)
                cp.start()
                remote.append(cp)
        for cp in remote:
            cp.wait_recv()
        for cp in remote:
            cp.wait_send()

    shapes = [jax.ShapeDtypeStruct((N_CHIPS,) + tuple(p.shape[1:]), p.dtype) for p in parts]
    res = pl.pallas_call(
        body, name=name, in_specs=[ANY] * n, out_specs=[ANY] * n, out_shape=shapes,
        scratch_shapes=[pltpu.SemaphoreType.DMA((N_CHIPS * n,)), pltpu.SemaphoreType.DMA((N_CHIPS * n,))],
    )(*parts)
    return list(res)


def own_blocks():
    me = _place()
    return jnp.stack([_index(_flip(me, rel)) for rel in CHIP_REL]).astype(jnp.int32)


def pair_add(name, blocks, parts, got):
    nq, r, c = got.shape
    tr = _row_tile(r, c)

    def body(blk_ref, a_ref, b_ref, o_ref):
        o_ref[...] = (a_ref[...].astype(F32) + b_ref[...].astype(F32)).astype(o_ref.dtype)

    spec = pl.BlockSpec((None, tr, c), lambda q, i, blk: (q, i, 0))
    return pl.pallas_call(
        body, name=name,
        grid_spec=pltpu.PrefetchScalarGridSpec(
            num_scalar_prefetch=1, grid=(nq, r // tr),
            in_specs=[pl.BlockSpec((None, tr, c), lambda q, i, blk: (blk[q], i, 0)), spec], out_specs=spec),
        out_shape=jax.ShapeDtypeStruct(got.shape, got.dtype),
        compiler_params=_params(("arbitrary", "arbitrary"), 6 * tr * c * 2))(blocks, parts, got)


def rs_plan(n):
    def plan(me):
        return [(a, q, q, CHIP_REL[q]) for a in range(n) for q in range(1, N_CHIPS)]
    return plan


def rs_start(name, sums):
    lands = [lax.empty(t.shape, t.dtype) for t in sums]
    return split_start(name, sums, lands, rs_plan(len(sums)))


def allreduce_small(name, pack, after):
    rows, lanes = pack.shape

    def body(x_ref, after_ref, o_ref, land, send_sems, recv_sems):
        me = _place()
        idx = _index(me)
        land[idx] = x_ref[...]
        copies = []
        for r in range(1, N_DEV):
            peer = _flip(me, REL[r])
            cp = pltpu.make_async_remote_copy(
                src_ref=x_ref, dst_ref=land.at[idx], send_sem=send_sems.at[r - 1], recv_sem=recv_sems.at[r - 1],
                device_id=peer, device_id_type=MESH)
            cp.start()
            copies.append(cp)
        for cp in copies:
            cp.wait_recv()
        for cp in copies:
            cp.wait_send()
        acc = land[0]
        for i in range(1, N_DEV):
            acc = acc + land[i]
        o_ref[...] = acc

    return pl.pallas_call(
        body, name=name, in_specs=[pl.BlockSpec(memory_space=pltpu.VMEM), ANY],
        out_specs=pl.BlockSpec(memory_space=pltpu.VMEM), out_shape=jax.ShapeDtypeStruct((rows, lanes), F32),
        scratch_shapes=[pltpu.VMEM((N_DEV, rows, lanes), F32), pltpu.SemaphoreType.DMA((7,)),
                        pltpu.SemaphoreType.DMA((7,))],
    )(pack, after)


def _pad_rows(a, rows):
    return jnp.pad(a, ((0, rows - a.shape[0]), (0, 0)))


def _as_tiles(vec):
    n = vec.shape[0]
    rows = -(-n // LANES)
    rows = -(-rows // SUBLANES) * SUBLANES
    return jnp.pad(vec, (0, rows * LANES - n)).reshape(rows, LANES)


def kernel(x, p, rel_bias_table, attn_norm, w_in, sink_a, w_branch_a, w_branch_b, w_out, ffn_norm, w_ffn_gate, w_ffn_up, conv_w, conv_b, w_ffn_down, ple_norm, w_ple_gate, w_ple_proj, final_norm, loss_target, m_rel_bias_table, m_attn_norm, m_w_in, m_sink_a, m_w_branch_a, m_w_branch_b, m_w_out, m_ffn_norm, m_w_ffn_gate, m_w_ffn_up, m_conv_w, m_conv_b, m_w_ffn_down, m_ple_norm, m_w_ple_gate, m_w_ple_proj, m_final_norm, v_rel_bias_table, v_attn_norm, v_w_in, v_sink_a, v_w_branch_a, v_w_branch_b, v_w_out, v_ffn_norm, v_w_ffn_gate, v_w_ffn_up, v_conv_w, v_conv_b, v_w_ffn_down, v_ple_norm, v_w_ple_gate, v_w_ple_proj, v_final_norm):
    xs = x[0]
    s, d = xs.shape
    ps = p[0, 0]
    target = loss_target[0]
    me = 4 * lax.axis_index("x") + 2 * lax.axis_index("y") + lax.axis_index("c")

    big = dict(w_in=w_in[0], w_branch_a=w_branch_a[0], w_branch_b=w_branch_b[0], w_out=w_out[0],
               w_ffn_gate=w_ffn_gate[0], w_ffn_up=w_ffn_up[0], w_ffn_down=w_ffn_down[0],
               w_ple_gate=w_ple_gate[0], w_ple_proj=w_ple_proj[0])
    big_m = dict(w_in=m_w_in[0], w_branch_a=m_w_branch_a[0], w_branch_b=m_w_branch_b[0], w_out=m_w_out[0],
                 w_ffn_gate=m_w_ffn_gate[0], w_ffn_up=m_w_ffn_up[0], w_ffn_down=m_w_ffn_down[0],
                 w_ple_gate=m_w_ple_gate[0], w_ple_proj=m_w_ple_proj[0])
    big_v = dict(w_in=v_w_in[0], w_branch_a=v_w_branch_a[0], w_branch_b=v_w_branch_b[0], w_out=v_w_out[0],
                 w_ffn_gate=v_w_ffn_gate[0], w_ffn_up=v_w_ffn_up[0], w_ffn_down=v_w_ffn_down[0],
                 w_ple_gate=v_w_ple_gate[0], w_ple_proj=v_w_ple_proj[0])
    names = list(big)
    nf = big["w_ffn_gate"].shape[1]

    shards = {k: big[k].astype(BF16) for k in names}
    shards["conv_w"] = _pad_rows(conv_w[0], SUBLANES)
    ag_groups = [["w_in"], ["w_branch_a", "w_branch_b", "w_out"], ["w_ffn_gate", "w_ffn_up", "conv_w"],
                 ["w_ffn_down", "w_ple_gate", "w_ple_proj"]]
    ag_started = []
    tok = jnp.zeros((), F32)
    for gi, grp in enumerate(ag_groups):
        s_sems, r_sems, srcs, lands, token = ag_start(f"ag_start{gi}", [shards[k] for k in grp])
        ag_started.append((s_sems, r_sems, srcs, lands))
        tok = tok + _token_value(token)
    wg = {}

    def gather(gi, after):
        s_sems, r_sems, srcs, lands = ag_started[gi]
        srcs, lands = split_wait(f"ag_wait{gi}", s_sems, r_sems, srcs, lands, ag_plan(len(srcs)), after)
        lands = ag_finish(f"ag_finish{gi}", lands)
        wg.update({k: lax.dynamic_update_index_in_dim(l, sh, me, 0) for k, l, sh in zip(ag_groups[gi], lands, srcs)})

    cb = conv_b.reshape(N_DEV, 1, nf)

    table_t = rel_bias_table.T
    bucket_a = bucket_tile(A_BLOCK, 1)
    bias_a = bias_build("bias_a", table_t, bucket_a, 0, A_Q_HEADS)
    buckets_b = [bucket_tile(B_BLOCK, dil) for _, dil in B_PATTERNS]
    biases_b = [bias_build(f"bias_b{g}", table_t, buckets_b[g], A_Q_HEADS + g * B_HEADS_PER_GROUP, B_HEADS_PER_GROUP)
                for g in range(len(B_PATTERNS))]

    h = rms_fwd("rms_attn", xs, attn_norm + tok)
    gather(0, h)
    proj = mm_cols("proj_in", h, wg["w_in"], F32, fold=True)
    sink = sink_a[0]
    ya, lse_a = band_attn_fwd("attn_a_fwd", proj, bias_a, sink, blk=A_BLOCK, dil=1, nh=A_Q_HEADS, group=A_GROUP,
                              cq=COL_QA, ck=COL_KA, cv=COL_VA)
    outs_b, lses_b = [], []
    for g, (_, dil) in enumerate(B_PATTERNS):
        off = g * B_OUT_W
        o, l = band_attn_fwd(f"attn_b{g}_fwd", proj, biases_b[g], None, blk=B_BLOCK, dil=dil, nh=B_HEADS_PER_GROUP,
                             group=1, cq=COL_QB + off, ck=COL_KB + off, cv=COL_VB + off)
        outs_b.append(o)
        lses_b.append(l)
    yb = dil_merge_fwd("dil_merge_fwd", outs_b, lses_b)
    gather(1, yb)
    w_out_full = wg["w_out"].reshape(d, d)
    ta = mm_cols("branch_a", ya, wg["w_branch_a"], F32, fold=True)
    tb = mm_cols("branch_b", yb, wg["w_branch_b"], F32, fold=True)
    merged = gate_merge_fwd("gate_merge_fwd", proj, ta, tb, d)
    x1 = mm_plain("mix_out", merged, w_out_full, F32, res=xs)

    hf = rms_fwd("rms_ffn", x1, ffn_norm)
    gather(2, hf)
    cw = wg["conv_w"]
    gpre = mm_cols("ffn_gate", hf, wg["w_ffn_gate"], F32, fold=False)
    u = mm_cols("ffn_up", hf, wg["w_ffn_up"], F32, fold=False)
    z = ffn_mid_fwd("ffn_mid_fwd", gpre, u, cw, cb)
    gather(3, z)
    w_pg_full = wg["w_ple_gate"].reshape(d, d)
    x2 = mm_jsum("ffn_down", z, wg["w_ffn_down"], F32, res=x1)

    hp = rms_fwd("rms_ple", x2, ple_norm)
    lp = mm_plain("ple_gate", hp, w_pg_full, F32)
    pp = mm_cols("ple_proj", ps, wg["w_ple_proj"], F32, fold=True)
    loss_part, dx3, dlp, dpp, d_final = tail_fwd_bwd("tail", x2, lp, pp, final_norm.reshape(1, d), target)

    grads = {}
    rs_started = []
    blocks = own_blocks()

    def scatter(tag, keys):
        got = rs_pair(f"rs_pair_{tag}", [grads[k] for k in keys])
        sums = [pair_add(f"pair_add_{k}", blocks, grads[k], g) for k, g in zip(keys, got)]
        s_sems, r_sems, srcs, lands, token = rs_start(f"rs_start_{tag}", sums)
        rs_started.append((tag, keys, s_sems, r_sems, srcs, lands))
        return _token_value(token)

    grads["w_ple_proj"] = mm_tn_cols("d_w_ple_proj", ps, dpp, N_DEV, big["w_ple_proj"].shape[1], BF16, folded=True)
    grads["w_ple_gate"] = mm_tn_plain("d_w_ple_gate", hp, dlp, BF16).reshape(N_DEV, d // N_DEV, d)
    tok = scatter("ple", ["w_ple_proj", "w_ple_gate"])
    dhp = mm_nt_plain("d_hp", dlp, w_pg_full, F32)
    dx2, d_ple = rms_bwd("rms_ple_bwd", x2, ple_norm + tok, dhp, dx3)

    dz = mm_nt_j("d_z", dx2, wg["w_ffn_down"], BF16)
    grads["w_ffn_down"] = mm_tn_j("d_w_ffn_down", z, dx2, BF16)
    tok = scatter("down", ["w_ffn_down"])
    dg, du, dcw = ffn_mid_bwd1("ffn_mid_bwd1", gpre, u, dz, cw, cb + tok)
    dgpre = ffn_mid_bwd2("ffn_mid_bwd2", dg, cw)
    grads["w_ffn_up"] = mm_tn_cols("d_w_ffn_up", hf, du, N_DEV, nf, BF16, folded=False)
    grads["w_ffn_gate"] = mm_tn_cols("d_w_ffn_gate", hf, dgpre, N_DEV, nf, BF16, folded=False)
    tok = scatter("upgate", ["w_ffn_up", "w_ffn_gate"])
    dhf = mm_nt_jsum("d_hf_up", du, wg["w_ffn_up"], F32, folded=False)
    dhf = mm_nt_jsum("d_hf_gate", dgpre, wg["w_ffn_gate"], F32, folded=False, res=dhf)
    dx1, d_ffn = rms_bwd("rms_ffn_bwd", x1, ffn_norm + tok, dhf, dx2)

    dmerged = mm_nt_plain("d_merged", dx1, w_out_full, F32)
    grads["w_out"] = mm_tn_plain("d_w_out", merged, dx1, BF16).reshape(N_DEV, d // N_DEV, d)
    dta, dtb, dga, dgb = gate_merge_bwd("gate_merge_bwd", dmerged, proj, ta, tb, d)
    grads["w_branch_a"] = mm_tn_cols("d_w_branch_a", ya, dta, N_DEV, big["w_branch_a"].shape[1], BF16, folded=True)
    grads["w_branch_b"] = mm_tn_cols("d_w_branch_b", yb, dtb, N_DEV, big["w_branch_b"].shape[1], BF16, folded=True)
    tok = scatter("mix", ["w_out", "w_branch_a", "w_branch_b"])
    dya = mm_nt_jsum("d_ya", dta, wg["w_branch_a"], F32, folded=True)
    dyb = mm_nt_jsum("d_yb", dtb, wg["w_branch_b"], F32, folded=True)
    dqa, dka, dva, dbias_a, dsink = band_attn_bwd(
        "attn_a_bwd", proj, bias_a, sink + tok, dya, ya, lse_a, None, blk=A_BLOCK, dil=1, nh=A_Q_HEADS, group=A_GROUP,
        cq=COL_QA, ck=COL_KA, cv=COL_VA)
    douts_b, dlses_b = dil_merge_bwd("dil_merge_bwd", dyb, outs_b, lses_b)
    dq_b, dk_b, dv_b, dbias_b = [], [], [], []
    for g, (_, dil) in enumerate(B_PATTERNS):
        off = g * B_OUT_W
        dq, dk, dv, db, _ = band_attn_bwd(
            f"attn_b{g}_bwd", proj, biases_b[g], None, douts_b[g], outs_b[g], lses_b[g], dlses_b[g], blk=B_BLOCK,
            dil=dil, nh=B_HEADS_PER_GROUP, group=1, cq=COL_QB + off, ck=COL_KB + off, cv=COL_VB + off)
        dq_b.append(dq)
        dk_b.append(dk)
        dv_b.append(dv)
        dbias_b.append(db)
    dproj = jnp.concatenate([dqa, dka.astype(BF16), dva.astype(BF16)] + dq_b + dk_b + dv_b + [dga, dgb], axis=1)
    grads["w_in"] = mm_tn_cols("d_w_in", h, dproj, N_DEV, big["w_in"].shape[1], BF16, folded=True)
    tok = scatter("in", ["w_in"])
    dh = mm_nt_jsum("d_h", dproj, wg["w_in"], F32, folded=True)
    grad_x, d_attn = rms_bwd("rms_attn_bwd", xs, attn_norm + tok, dh, dx1)

    dt_a = table_grad("table_grad_a", dbias_a, bucket_a)[:, 0, :N_BUCKETS]
    dt_b = [table_grad(f"table_grad_b{g}", dbias_b[g], buckets_b[g])[:, 0, :N_BUCKETS] for g in range(len(B_PATTERNS))]
    d_table_part = jnp.concatenate([dt_a] + dt_b, axis=0).T

    pieces = [
        ("loss", loss_part[0, :1]),
        ("table", d_table_part.reshape(-1)),
        ("attn_norm", d_attn.reshape(-1)),
        ("sink", dsink[:, 0, 0]),
        ("ffn_norm", d_ffn.reshape(-1)),
        ("conv_w", dcw[:, 0:3, :].reshape(-1)),
        ("conv_b", dcw[:, 3, :].reshape(-1)),
        ("ple_norm", d_ple.reshape(-1)),
        ("final_norm", d_final.reshape(-1)),
    ]
    tiles = [_as_tiles(v) for _, v in pieces]
    pack = jnp.concatenate(tiles, axis=0)

    out_g, out_d, out_m, out_v = {}, {}, {}, {}

    def finish(group, after):
        tag, keys, s_sems, r_sems, srcs, lands = group
        srcs, lands = split_wait(f"rs_wait_{tag}", s_sems, r_sems, srcs, lands, rs_plan(len(keys)), after)
        for k, mine, theirs in zip(keys, srcs, lands):
            g, dl, nm, nv = reduce_adam("adam_" + k, mine, theirs, big[k], big_m[k], big_v[k])
            out_g[k], out_d[k], out_m[k], out_v[k] = g[None], dl[None], nm[None], nv[None]
            after = dl
        return after

    after = pack
    for group in rs_started[:-1]:
        after = finish(group, after)
    total = allreduce_small("allreduce_small", pack, after)
    finish(rs_started[-1], total)
    small = {}
    row = 0
    for (nm, v), t in zip(pieces, tiles):
        small[nm] = total[row:row + t.shape[0]].reshape(-1)[:v.shape[0]]
        row += t.shape[0]
    loss = small["loss"][0]
    g_small = dict(
        rel_bias_table=small["table"].reshape(rel_bias_table.shape),
        attn_norm=small["attn_norm"].reshape(attn_norm.shape),
        sink_a=small["sink"].reshape(sink_a.shape),
        ffn_norm=small["ffn_norm"].reshape(ffn_norm.shape),
        conv_w=lax.dynamic_index_in_dim(small["conv_w"].reshape(N_DEV, 3, nf), me, 0, keepdims=False)[None],
        conv_b=small["conv_b"].reshape(conv_b.shape),
        ple_norm=small["ple_norm"].reshape(ple_norm.shape),
        final_norm=small["final_norm"].reshape(1, d),
    )
    w_small = dict(rel_bias_table=(rel_bias_table, m_rel_bias_table, v_rel_bias_table),
                   attn_norm=(attn_norm, m_attn_norm, v_attn_norm), sink_a=(sink_a, m_sink_a, v_sink_a),
                   ffn_norm=(ffn_norm, m_ffn_norm, v_ffn_norm), conv_w=(conv_w, m_conv_w, v_conv_w),
                   conv_b=(conv_b, m_conv_b, v_conv_b), ple_norm=(ple_norm, m_ple_norm, v_ple_norm),
                   final_norm=(final_norm, m_final_norm, v_final_norm))

    for k, (wv, mv, vv) in w_small.items():
        shape = wv.shape
        two_d = (1, shape[0]) if len(shape) == 1 else ((shape[0] * shape[1], shape[2]) if len(shape) == 3 else shape)
        gk = g_small[k].reshape(two_d)
        dl, nm, nv = adam_small("adam_" + k, gk, wv.reshape(two_d), mv.reshape(two_d), vv.reshape(two_d))
        out_g[k], out_d[k], out_m[k], out_v[k] = gk.reshape(shape), dl.reshape(shape), nm.reshape(shape), nv.reshape(shape)

    order = ["rel_bias_table", "attn_norm", "w_in", "sink_a", "w_branch_a", "w_branch_b", "w_out", "ffn_norm",
             "w_ffn_gate", "w_ffn_up", "conv_w", "conv_b", "w_ffn_down", "ple_norm", "w_ple_gate", "w_ple_proj",
             "final_norm"]
    return (loss, grad_x[None], *[out_g[k] for k in order], *[out_d[k] for k in order],
            *[out_m[k] for k in order], *[out_v[k] for k in order])
```

```python
import math

import jax
import jax.numpy as jnp
from jax import lax
from jax.experimental import pallas as pl
from jax.experimental.pallas import tpu as pltpu

F32 = jnp.float32
BF16 = jnp.bfloat16
MESH = pl.DeviceIdType.MESH
N_DEV = 8

HEAD_DIM = 128
A_Q_HEADS = 8
A_KV_HEADS = 2
A_GROUP = A_Q_HEADS // A_KV_HEADS
A_BLOCK = 128
B_PATTERNS = ((128, 1), (512, 4), (2048, 16))
B_HEADS_PER_GROUP = 4
B_HEADS = len(B_PATTERNS) * B_HEADS_PER_GROUP
B_BLOCK = 64
N_BUCKETS = 32
MAX_DISTANCE = 1024
A_Q_W = A_Q_HEADS * HEAD_DIM
A_KV_W = A_KV_HEADS * HEAD_DIM
B_W = B_HEADS * HEAD_DIM
B_OUT_W = B_HEADS_PER_GROUP * HEAD_DIM
COL_QA = 0
COL_KA = COL_QA + A_Q_W
COL_VA = COL_KA + A_KV_W
COL_QB = COL_VA + A_KV_W
COL_KB = COL_QB + B_W
COL_VB = COL_KB + B_W
COL_GATES = COL_VB + B_W
RMS_EPS = 1e-6
NEG_INF = -1e30
ATTN_SCALE = HEAD_DIM ** -0.5
ATTN_Q_ROWS = 256
ATTN_CHAINS = 2

ADAM_LR = 0.001
ADAM_B1 = 0.9
ADAM_B2 = 0.999
ADAM_EPS = 1e-08
ADAM_WD = 0.01
ADAM_STEP = 10

GELU_C = math.sqrt(2.0 / math.pi)
GELU_A = 0.044715

V7X_VMEM_BYTES = 64 * 1024 * 1024
VMEM_CEILING = V7X_VMEM_BYTES - 8 * 1024 * 1024
LANES = 128
SUBLANES = 8


def _pick(n, cands):
    for c in cands:
        if n % c == 0:
            return c
    return n


def _nbytes(shape, dtype):
    n = 1
    for d in shape:
        if d is not None:
            n *= d
    return n * jnp.dtype(dtype).itemsize


def _params(sem, est_bytes):
    limit = int(min(VMEM_CEILING, max(32 * 1024 * 1024, 2 * est_bytes + (8 << 20))))
    return pltpu.CompilerParams(dimension_semantics=sem, vmem_limit_bytes=limit)


def _mm(name, a, b, a_bs, a_im, b_bs, b_im, out_shape, out_dtype, o_bs, o_im, grid, dims,
        res=None, r_bs=None, r_im=None):
    nk = grid[-1]
    nax = len(grid)
    has_res = res is not None
    o_tile = tuple(d for d in o_bs if d is not None)

    def body(*refs):
        if has_res:
            a_ref, b_ref, r_ref, o_ref = refs[:4]
            rest = refs[4:]
        else:
            a_ref, b_ref, o_ref = refs[:3]
            r_ref = None
            rest = refs[3:]

        def prod():
            return lax.dot_general(a_ref[...].astype(BF16), b_ref[...].astype(BF16), (dims, ((), ())),
                                   preferred_element_type=F32)

        def finish(r):
            if r_ref is not None:
                r = r + r_ref[...].astype(F32)
            o_ref[...] = r.astype(o_ref.dtype)

        if nk == 1:
            finish(prod())
        else:
            acc = rest[0]
            k = pl.program_id(nax - 1)

            @pl.when(k == 0)
            def _():
                acc[...] = prod()

            @pl.when(k > 0)
            def _():
                acc[...] += prod()

            @pl.when(k == nk - 1)
            def _():
                finish(acc[...])

    in_specs = [pl.BlockSpec(a_bs, a_im), pl.BlockSpec(b_bs, b_im)]
    args = [a, b]
    est = _nbytes(a_bs, a.dtype) + _nbytes(b_bs, b.dtype) + _nbytes(o_bs, out_dtype) + 2 * _nbytes(o_tile, F32)
    if has_res:
        in_specs.append(pl.BlockSpec(r_bs, r_im))
        args.append(res)
        est += _nbytes(r_bs, res.dtype)
    scratch = [] if nk == 1 else [pltpu.VMEM(o_tile, F32)]
    sem = ("parallel",) * (nax - 1) + ("arbitrary",)
    return pl.pallas_call(
        body, name=name, grid=grid, in_specs=in_specs, out_specs=pl.BlockSpec(o_bs, o_im),
        out_shape=jax.ShapeDtypeStruct(out_shape, out_dtype), scratch_shapes=scratch,
        compiler_params=_params(sem, est))(*args)


TM_CANDS = (1024, 512, 256, 128, 64, 32, 16, 8)
TK_CANDS = (1024, 512, 256, 128)
TN_CANDS = (1024, 512, 256, 128)


def mm_cols(name, a, wg, out_dtype, fold):
    m, k = a.shape
    nj, _, n = wg.shape
    tm, tk = _pick(m, TM_CANDS), _pick(k, TK_CANDS)
    grid = (nj, m // tm, k // tk)
    if fold:
        shape, o_bs, o_im = (m, nj * n), (tm, n), (lambda j, i, kk: (i, j))
    else:
        shape, o_bs, o_im = (nj, m, n), (None, tm, n), (lambda j, i, kk: (j, i, 0))
    return _mm(name, a, wg, (tm, tk), lambda j, i, kk: (i, kk), (None, tk, n), lambda j, i, kk: (j, kk, 0),
               shape, out_dtype, o_bs, o_im, grid, ((1,), (0,)))


def mm_plain(name, a, w, out_dtype, res=None):
    m, k = a.shape
    n = w.shape[1]
    tm, tk, tn = _pick(m, TM_CANDS), _pick(k, TK_CANDS), _pick(n, TN_CANDS)
    grid = (n // tn, m // tm, k // tk)
    return _mm(name, a, w, (tm, tk), lambda j, i, kk: (i, kk), (tk, tn), lambda j, i, kk: (kk, j),
               (m, n), out_dtype, (tm, tn), lambda j, i, kk: (i, j), grid, ((1,), (0,)),
               res, (tm, tn), lambda j, i, kk: (i, j))


def mm_jsum(name, aj, wg, out_dtype, res=None):
    nj, m, ka = aj.shape
    n = wg.shape[2]
    tm, tn = _pick(m, TM_CANDS), _pick(n, TN_CANDS)
    grid = (m // tm, n // tn, nj)
    return _mm(name, aj, wg, (None, tm, ka), lambda i, jn, j: (j, i, 0), (None, ka, tn), lambda i, jn, j: (j, 0, jn),
               (m, n), out_dtype, (tm, tn), lambda i, jn, j: (i, jn), grid, ((1,), (0,)),
               res, (tm, tn), lambda i, jn, j: (i, jn))


def mm_tn_cols(name, a, g, nj, n, out_dtype, folded):
    s, kw = a.shape
    ts, tkw = _pick(s, TK_CANDS), _pick(kw, TM_CANDS)
    grid = (nj, kw // tkw, s // ts)
    if folded:
        g_bs, g_im = (ts, n), (lambda j, i, ss: (ss, j))
    else:
        g_bs, g_im = (None, ts, n), (lambda j, i, ss: (j, ss, 0))
    return _mm(name, a, g, (ts, tkw), lambda j, i, ss: (ss, i), g_bs, g_im,
               (nj, kw, n), out_dtype, (None, tkw, n), lambda j, i, ss: (j, i, 0), grid, ((0,), (0,)))


def mm_tn_plain(name, a, g, out_dtype):
    s, kw = a.shape
    n = g.shape[1]
    ts, tkw, tn = _pick(s, TK_CANDS), _pick(kw, TM_CANDS), _pick(n, TN_CANDS)
    grid = (kw // tkw, n // tn, s // ts)
    return _mm(name, a, g, (ts, tkw), lambda i, jn, ss: (ss, i), (ts, tn), lambda i, jn, ss: (ss, jn),
               (kw, n), out_dtype, (tkw, tn), lambda i, jn, ss: (i, jn), grid, ((0,), (0,)))


def mm_tn_j(name, aj, g, out_dtype):
    nj, s, ka = aj.shape
    n = g.shape[1]
    ts, tn = _pick(s, TK_CANDS), _pick(n, TN_CANDS)
    grid = (nj, n // tn, s // ts)
    return _mm(name, aj, g, (None, ts, ka), lambda j, jn, ss: (j, ss, 0), (ts, tn), lambda j, jn, ss: (ss, jn),
               (nj, ka, n), out_dtype, (None, ka, tn), lambda j, jn, ss: (j, 0, jn), grid, ((0,), (0,)))


def mm_nt_plain(name, g, w, out_dtype):
    m, n = g.shape
    k = w.shape[0]
    tm, tn, tkk = _pick(m, TM_CANDS), _pick(n, TK_CANDS), _pick(k, TN_CANDS)
    grid = (k // tkk, m // tm, n // tn)
    return _mm(name, g, w, (tm, tn), lambda kk, i, jn: (i, jn), (tkk, tn), lambda kk, i, jn: (kk, jn),
               (m, k), out_dtype, (tm, tkk), lambda kk, i, jn: (i, kk), grid, ((1,), (1,)))


def mm_nt_j(name, g, wg, out_dtype):
    m, n = g.shape
    nj, ka, _ = wg.shape
    tm, tn = _pick(m, TM_CANDS), _pick(n, TK_CANDS)
    grid = (nj, m // tm, n // tn)
    return _mm(name, g, wg, (tm, tn), lambda j, i, jn: (i, jn), (None, ka, tn), lambda j, i, jn: (j, 0, jn),
               (nj, m, ka), out_dtype, (None, tm, ka), lambda j, i, jn: (j, i, 0), grid, ((1,), (1,)))


def mm_nt_jsum(name, g, wg, out_dtype, folded, res=None):
    nj, k, n = wg.shape
    m = g.shape[0] if folded else g.shape[1]
    tm, tkk = _pick(m, TM_CANDS), _pick(k, TN_CANDS)
    grid = (m // tm, k // tkk, nj)
    if folded:
        g_bs, g_im = (tm, n), (lambda i, kk, j: (i, j))
    else:
        g_bs, g_im = (None, tm, n), (lambda i, kk, j: (j, i, 0))
    return _mm(name, g, wg, g_bs, g_im, (None, tkk, n), lambda i, kk, j: (j, kk, 0),
               (m, k), out_dtype, (tm, tkk), lambda i, kk, j: (i, kk), grid, ((1,), (1,)),
               res, (tm, tkk), lambda i, kk, j: (i, kk))


ROW_TILE_CANDS = (256, 128, 64, 32, 16, 8)


def _rstd(x):
    return lax.rsqrt(jnp.mean(x * x, axis=-1, keepdims=True) + RMS_EPS)


def _sigmoid(t):
    return 1.0 / (1.0 + jnp.exp(-t))


def rms_fwd(name, x, gain):
    s, d = x.shape
    ts = _pick(s, ROW_TILE_CANDS)

    def body(x_ref, g_ref, h_ref):
        xv = x_ref[...]
        h_ref[...] = ((xv * _rstd(xv)) * g_ref[...]).astype(h_ref.dtype)

    return pl.pallas_call(
        body, name=name, grid=(s // ts,),
        in_specs=[pl.BlockSpec((ts, d), lambda i: (i, 0)), pl.BlockSpec((1, d), lambda i: (0, 0))],
        out_specs=pl.BlockSpec((ts, d), lambda i: (i, 0)),
        out_shape=jax.ShapeDtypeStruct((s, d), BF16),
        compiler_params=_params(("parallel",), 3 * ts * d * 4))(x, gain)


def rms_bwd(name, x, gain, dh, dres):
    s, d = x.shape
    ts = _pick(s, ROW_TILE_CANDS)

    def body(x_ref, g_ref, dh_ref, dr_ref, dx_ref, dg_ref):
        xv = x_ref[...]
        r = _rstd(xv)
        xhat = xv * r
        dhv = dh_ref[...].astype(F32)
        dxhat = dhv * g_ref[...]
        dx_ref[...] = dr_ref[...] + r * (dxhat - xhat * jnp.mean(dxhat * xhat, axis=-1, keepdims=True))
        part = jnp.sum(dhv * xhat, axis=0, keepdims=True)

        @pl.when(pl.program_id(0) == 0)
        def _():
            dg_ref[...] = part

        @pl.when(pl.program_id(0) > 0)
        def _():
            dg_ref[...] += part

    row = pl.BlockSpec((ts, d), lambda i: (i, 0))
    vec = pl.BlockSpec((1, d), lambda i: (0, 0))
    return pl.pallas_call(
        body, name=name, grid=(s // ts,), in_specs=[row, vec, row, row], out_specs=[row, vec],
        out_shape=[jax.ShapeDtypeStruct((s, d), F32), jax.ShapeDtypeStruct((1, d), F32)],
        compiler_params=_params(("arbitrary",), 6 * ts * d * 4))(x, gain, dh, dres)


def gate_merge_fwd(name, proj, ta, tb, d):
    s = proj.shape[0]
    ts = _pick(s, ROW_TILE_CANDS)
    cb = COL_GATES // d

    def body(ga_ref, gb_ref, ta_ref, tb_ref, o_ref):
        o_ref[...] = (_sigmoid(ga_ref[...]) * ta_ref[...] + _sigmoid(gb_ref[...]) * tb_ref[...]).astype(o_ref.dtype)

    row = pl.BlockSpec((ts, d), lambda i: (i, 0))
    return pl.pallas_call(
        body, name=name, grid=(s // ts,),
        in_specs=[pl.BlockSpec((ts, d), lambda i: (i, cb)), pl.BlockSpec((ts, d), lambda i: (i, cb + 1)), row, row],
        out_specs=row, out_shape=jax.ShapeDtypeStruct((s, d), BF16),
        compiler_params=_params(("parallel",), 5 * ts * d * 4))(proj, proj, ta, tb)


def gate_merge_bwd(name, dmerged, proj, ta, tb, d):
    s = proj.shape[0]
    ts = _pick(s, ROW_TILE_CANDS)
    cb = COL_GATES // d

    def body(dm_ref, ga_ref, gb_ref, ta_ref, tb_ref, dta_ref, dtb_ref, dga_ref, dgb_ref):
        dm = dm_ref[...]
        sa = _sigmoid(ga_ref[...])
        sb = _sigmoid(gb_ref[...])
        dta_ref[...] = (dm * sa).astype(dta_ref.dtype)
        dtb_ref[...] = (dm * sb).astype(dtb_ref.dtype)
        dga_ref[...] = (dm * ta_ref[...] * (sa * (1.0 - sa))).astype(dga_ref.dtype)
        dgb_ref[...] = (dm * tb_ref[...] * (sb * (1.0 - sb))).astype(dgb_ref.dtype)

    row = pl.BlockSpec((ts, d), lambda i: (i, 0))
    out = jax.ShapeDtypeStruct((s, d), BF16)
    return pl.pallas_call(
        body, name=name, grid=(s // ts,),
        in_specs=[row, pl.BlockSpec((ts, d), lambda i: (i, cb)), pl.BlockSpec((ts, d), lambda i: (i, cb + 1)), row, row],
        out_specs=[row, row, row, row], out_shape=[out, out, out, out],
        compiler_params=_params(("parallel",), 8 * ts * d * 4))(dmerged, proj, proj, ta, tb)


def tail_fwd_bwd(name, x2, lp, pp, gain, target):
    s, d = x2.shape
    ts = _pick(s, ROW_TILE_CANDS)

    def body(x2_ref, lp_ref, pp_ref, g_ref, t_ref, loss_ref, dx3_ref, dlp_ref, dpp_ref, dg_ref):
        gp = _sigmoid(lp_ref[...])
        ppv = pp_ref[...]
        x3 = x2_ref[...] + gp * ppv
        r = _rstd(x3)
        xhat = x3 * r
        gv = g_ref[...]
        err = xhat * gv - t_ref[...]
        loss = jnp.sum(err * err) * (0.5 / d)
        dy = err * (1.0 / d)
        dxhat = dy * gv
        dx3 = r * (dxhat - xhat * jnp.mean(dxhat * xhat, axis=-1, keepdims=True))
        dx3_ref[...] = dx3
        dlp_ref[...] = (dx3 * ppv * (gp * (1.0 - gp))).astype(dlp_ref.dtype)
        dpp_ref[...] = (dx3 * gp).astype(dpp_ref.dtype)
        part = jnp.sum(dy * xhat, axis=0, keepdims=True)
        lossv = jnp.full((1, LANES), loss, F32)

        @pl.when(pl.program_id(0) == 0)
        def _():
            dg_ref[...] = part
            loss_ref[...] = lossv

        @pl.when(pl.program_id(0) > 0)
        def _():
            dg_ref[...] += part
            loss_ref[...] += lossv

    row = pl.BlockSpec((ts, d), lambda i: (i, 0))
    vec = pl.BlockSpec((1, d), lambda i: (0, 0))
    return pl.pallas_call(
        body, name=name, grid=(s // ts,), in_specs=[row, row, row, vec, row],
        out_specs=[pl.BlockSpec((1, LANES), lambda i: (0, 0)), row, row, row, vec],
        out_shape=[jax.ShapeDtypeStruct((1, LANES), F32), jax.ShapeDtypeStruct((s, d), F32),
                   jax.ShapeDtypeStruct((s, d), BF16), jax.ShapeDtypeStruct((s, d), BF16),
                   jax.ShapeDtypeStruct((1, d), F32)],
        compiler_params=_params(("arbitrary",), 9 * ts * d * 4))(x2, lp, pp, gain, target)


HALO = SUBLANES


def _shift_rows(cur, prev_row, next_row):
    ts = cur.shape[0]
    rid = lax.broadcasted_iota(jnp.int32, cur.shape, 0)
    down = jnp.where(rid == 0, prev_row, pltpu.roll(cur, 1, 0))
    up = jnp.where(rid == ts - 1, next_row, pltpu.roll(cur, ts - 1, 0))
    return down, up


def _halo_specs(ts, s, nf):
    nb = ts // HALO
    last = s // HALO - 1
    cur = pl.BlockSpec((None, ts, nf), lambda j, i: (j, i, 0))
    prev = pl.BlockSpec((None, HALO, nf), lambda j, i: (j, jnp.maximum(i * nb - 1, 0), 0))
    nxt = pl.BlockSpec((None, HALO, nf), lambda j, i: (j, jnp.minimum((i + 1) * nb, last), 0))
    return cur, prev, nxt


def _halo_rows(prev_ref, next_ref, n_tiles):
    i = pl.program_id(1)
    prev_row = jnp.where(i == 0, 0.0, prev_ref[HALO - 1:HALO, :].astype(F32))
    next_row = jnp.where(i == n_tiles - 1, 0.0, next_ref[0:1, :].astype(F32))
    return prev_row, next_row


def _gelu(g):
    t = jnp.tanh(GELU_C * (g + GELU_A * (g * g * g)))
    return 0.5 * g * (1.0 + t), t


def _conv(cur, down, up, cw_ref, cb_ref):
    return down * cw_ref[0:1, :] + cur * cw_ref[1:2, :] + up * cw_ref[2:3, :] + cb_ref[...]


def ffn_mid_fwd(name, gpre, u, cw, cb):
    nj, s, nf = gpre.shape
    ts = _pick(s, (512, 256, 128, 64, 32, 16, 8))
    n_tiles = s // ts
    cur, prev, nxt = _halo_specs(ts, s, nf)

    def body(g_ref, gp_ref, gn_ref, u_ref, cw_ref, cb_ref, z_ref):
        gv = g_ref[...]
        down, up = _shift_rows(gv, *_halo_rows(gp_ref, gn_ref, n_tiles))
        act, _ = _gelu(_conv(gv, down, up, cw_ref, cb_ref))
        z_ref[...] = (act * u_ref[...]).astype(z_ref.dtype)

    return pl.pallas_call(
        body, name=name, grid=(nj, n_tiles),
        in_specs=[cur, prev, nxt, cur, pl.BlockSpec((None, SUBLANES, nf), lambda j, i: (j, 0, 0)),
                  pl.BlockSpec((None, 1, nf), lambda j, i: (j, 0, 0))],
        out_specs=cur, out_shape=jax.ShapeDtypeStruct((nj, s, nf), BF16),
        compiler_params=_params(("parallel", "parallel"), 8 * ts * nf * 4))(gpre, gpre, gpre, u, cw, cb)


def ffn_mid_bwd1(name, gpre, u, dz, cw, cb):
    nj, s, nf = gpre.shape
    ts = _pick(s, (512, 256, 128, 64, 32, 16, 8))
    n_tiles = s // ts
    cur, prev, nxt = _halo_specs(ts, s, nf)

    def body(g_ref, gp_ref, gn_ref, u_ref, dz_ref, cw_ref, cb_ref, dg_ref, du_ref, dcw_ref):
        gv = g_ref[...]
        down, up = _shift_rows(gv, *_halo_rows(gp_ref, gn_ref, n_tiles))
        gc = _conv(gv, down, up, cw_ref, cb_ref)
        act, t = _gelu(gc)
        dzv = dz_ref[...].astype(F32)
        du_ref[...] = (dzv * act).astype(du_ref.dtype)
        dact = 0.5 * (1.0 + t) + 0.5 * gc * (1.0 - t * t) * (GELU_C * (1.0 + 3.0 * GELU_A * (gc * gc)))
        dg = dzv * u_ref[...] * dact
        dg_ref[...] = dg
        rows = [jnp.sum(dg * down, axis=0, keepdims=True), jnp.sum(dg * gv, axis=0, keepdims=True),
                jnp.sum(dg * up, axis=0, keepdims=True), jnp.sum(dg, axis=0, keepdims=True)]
        part = jnp.concatenate(rows + [jnp.zeros((SUBLANES - len(rows), nf), F32)], axis=0)

        @pl.when(pl.program_id(1) == 0)
        def _():
            dcw_ref[...] = part

        @pl.when(pl.program_id(1) > 0)
        def _():
            dcw_ref[...] += part

    small = pl.BlockSpec((None, SUBLANES, nf), lambda j, i: (j, 0, 0))
    return pl.pallas_call(
        body, name=name, grid=(nj, n_tiles),
        in_specs=[cur, prev, nxt, cur, cur, small, pl.BlockSpec((None, 1, nf), lambda j, i: (j, 0, 0))],
        out_specs=[cur, cur, small],
        out_shape=[jax.ShapeDtypeStruct((nj, s, nf), F32), jax.ShapeDtypeStruct((nj, s, nf), BF16),
                   jax.ShapeDtypeStruct((nj, SUBLANES, nf), F32)],
        compiler_params=_params(("parallel", "arbitrary"), 12 * ts * nf * 4))(gpre, gpre, gpre, u, dz, cw, cb)


def ffn_mid_bwd2(name, dg, cw):
    nj, s, nf = dg.shape
    ts = _pick(s, (512, 256, 128, 64, 32, 16, 8))
    n_tiles = s // ts
    cur, prev, nxt = _halo_specs(ts, s, nf)

    def body(g_ref, gp_ref, gn_ref, cw_ref, o_ref):
        gv = g_ref[...]
        down, up = _shift_rows(gv, *_halo_rows(gp_ref, gn_ref, n_tiles))
        o_ref[...] = (up * cw_ref[0:1, :] + gv * cw_ref[1:2, :] + down * cw_ref[2:3, :]).astype(o_ref.dtype)

    return pl.pallas_call(
        body, name=name, grid=(nj, n_tiles),
        in_specs=[cur, prev, nxt, pl.BlockSpec((None, SUBLANES, nf), lambda j, i: (j, 0, 0))],
        out_specs=cur, out_shape=jax.ShapeDtypeStruct((nj, s, nf), BF16),
        compiler_params=_params(("parallel", "parallel"), 6 * ts * nf * 4))(dg, dg, dg, cw)


def _t5_bucket(rel):
    half = N_BUCKETS // 2
    max_exact = half // 2
    n = jnp.abs(rel)
    side = jnp.where(rel > 0, half, 0)
    nf = jnp.maximum(n, 1).astype(F32)
    large = max_exact + (jnp.log(nf / max_exact) / math.log(MAX_DISTANCE / max_exact)
                         * (half - max_exact)).astype(jnp.int32)
    large = jnp.minimum(large, half - 1)
    return side + jnp.where(n < max_exact, n, large)


def bucket_tile(rows, half, dil):
    rel = (jnp.arange(rows + 2 * half)[None, :] - half) - jnp.arange(rows)[:, None]
    return _t5_bucket(rel * dil).astype(jnp.int32)


def bias_build(name, table_t, bucket, h0, nh, half):
    blk, kw = bucket.shape

    def body(t_ref, b_ref, o_ref):
        h = pl.program_id(0)
        bv = b_ref[...]
        acc = jnp.zeros((blk, kw), F32)
        for b in range(N_BUCKETS):
            acc = jnp.where(bv == b, t_ref[h0 + h, b], acc)
        qi = lax.broadcasted_iota(jnp.int32, (blk, kw), 0)
        ci = lax.broadcasted_iota(jnp.int32, (blk, kw), 1)
        o_ref[...] = jnp.where(jnp.abs(ci - half - qi) <= half, acc, NEG_INF)

    return pl.pallas_call(
        body, name=name, grid=(nh,),
        in_specs=[pl.BlockSpec(memory_space=pltpu.SMEM), pl.BlockSpec((blk, kw), lambda h: (0, 0))],
        out_specs=pl.BlockSpec((None, blk, kw), lambda h: (h, 0, 0)),
        out_shape=jax.ShapeDtypeStruct((nh, blk, kw), F32),
        compiler_params=_params(("parallel",), 4 * blk * kw * 4))(table_t, bucket)


def table_grad(name, dbias, bucket):
    nh, blk, kw = dbias.shape

    def body(d_ref, b_ref, o_ref):
        bv = b_ref[...]
        dv = d_ref[...]
        lane = lax.broadcasted_iota(jnp.int32, (SUBLANES, LANES), 1)
        acc = jnp.zeros((SUBLANES, LANES), F32)
        for b in range(N_BUCKETS):
            acc = jnp.where(lane == b, jnp.sum(jnp.where(bv == b, dv, 0.0)), acc)
        o_ref[...] = acc

    return pl.pallas_call(
        body, name=name, grid=(nh,),
        in_specs=[pl.BlockSpec((None, blk, kw), lambda h: (h, 0, 0)), pl.BlockSpec((blk, kw), lambda h: (0, 0))],
        out_specs=pl.BlockSpec((None, SUBLANES, LANES), lambda h: (h, 0, 0)),
        out_shape=jax.ShapeDtypeStruct((nh, SUBLANES, LANES), F32),
        compiler_params=_params(("parallel",), 4 * blk * kw * 4))(dbias, bucket)


class _Band:
    def __init__(self, s, half, q_rows, n_chains, dil):
        self.s, self.half, self.dil, self.n_chains = s, half, dil, n_chains
        self.seg = s // dil
        self.q_rows = min(q_rows, self.seg)
        self.win = self.q_rows + 2 * half
        self.pad = self.seg + 2 * half
        self.nsb = self.seg // self.q_rows
        self.n_items = dil * self.nsb
        assert self.n_items % n_chains == 0 and self.seg % self.q_rows == 0
        self.staged = dil > 1

    def rows_of(self, r):
        return pl.ds(r, self.seg, stride=self.dil) if self.dil > 1 else slice(None)

    def stage_kv(self, dst, src_ref):
        zeros = jnp.zeros((self.half, HEAD_DIM), dst.dtype)
        for r in range(self.dil):
            base = r * self.pad
            dst[base:base + self.half, :] = zeros
            dst[base + self.half + self.seg:base + self.pad, :] = zeros
            dst[base + self.half:base + self.half + self.seg, :] = src_ref[self.rows_of(r), :].astype(dst.dtype)

    def stage(self, dst, src_ref):
        for r in range(self.dil):
            dst[r * self.seg:(r + 1) * self.seg, :] = src_ref[self.rows_of(r), :].astype(dst.dtype)

    def unstage(self, dst_ref, src, add=False):
        for r in range(self.dil):
            val = src[r * self.seg:(r + 1) * self.seg, :].astype(dst_ref.dtype)
            if add:
                val = val + dst_ref[self.rows_of(r), :]
            dst_ref[self.rows_of(r), :] = val

    def offsets(self, item):
        r, sb = item // self.nsb, item % self.nsb
        qoff = pl.multiple_of(r * self.seg + sb * self.q_rows, self.q_rows)
        koff = pl.multiple_of(r * self.pad + sb * self.q_rows, B_BLOCK)
        kpos = sb * self.q_rows - self.half + lax.broadcasted_iota(jnp.int32, (1, self.win), 1)
        edge = jnp.where((kpos >= 0) & (kpos < self.seg), 0.0, NEG_INF)
        return qoff, koff, edge


def band_attn_fwd(name, proj, bias, sink, *, half, q_rows, n_chains, dil, nh, group, cq, ck, cv):
    s, w = proj.shape
    g = _Band(s, half, q_rows, n_chains, dil)
    has_sink = sink is not None

    def body(*refs):
        q_ref, k_ref, v_ref, b_ref = refs[:4]
        s_ref = refs[4] if has_sink else None
        o_ref, l_ref, ks, vs = refs[4 + has_sink:8 + has_sink]
        qs, os_, ls = refs[8 + has_sink:] if g.staged else (None, o_ref, l_ref)
        g.stage_kv(ks, k_ref)
        g.stage_kv(vs, v_ref)
        if g.staged:
            g.stage(qs, q_ref)
        bias_v = b_ref[...]
        sk = s_ref[pl.program_id(0)] if has_sink else None

        def chain(item):
            qoff, koff, edge = g.offsets(item)
            rows = pl.ds(qoff, g.q_rows)
            qv = qs[rows, :] if g.staged else q_ref[rows, :].astype(BF16)
            kw_ = ks[pl.ds(koff, g.win), :]
            vw_ = vs[pl.ds(koff, g.win), :]
            sc = lax.dot_general(qv, kw_, (((1,), (1,)), ((), ())), preferred_element_type=F32) * ATTN_SCALE
            sc = sc + bias_v + edge
            m = jnp.max(sc, axis=-1, keepdims=True)
            if has_sink:
                m = jnp.maximum(m, sk)
            p = jnp.exp(sc - m)
            den = jnp.sum(p, axis=-1, keepdims=True)
            if has_sink:
                den = den + jnp.exp(sk - m)
            out = lax.dot_general(p.astype(BF16), vw_, (((1,), (0,)), ((), ())), preferred_element_type=F32)
            return rows, out / den, jnp.broadcast_to(m + jnp.log(den), (g.q_rows, HEAD_DIM))

        def step(i, carry):
            for rows, out, lse in [chain(i * n_chains + u) for u in range(n_chains)]:
                os_[rows, :] = out
                ls[rows, :] = lse
            return carry

        lax.fori_loop(0, g.n_items // n_chains, step, 0)
        if g.staged:
            g.unstage(o_ref, os_)
            g.unstage(l_ref, ls)

    def col(c0, per):
        return pl.BlockSpec((s, HEAD_DIM), lambda h: (0, c0 // LANES + h // per))

    in_specs = [col(cq, 1), col(ck, group), col(cv, group),
                pl.BlockSpec((None, g.q_rows, g.win), lambda h: (h, 0, 0))]
    args = [proj, proj, proj, bias]
    if has_sink:
        in_specs.append(pl.BlockSpec(memory_space=pltpu.SMEM))
        args.append(sink)
    shape = jax.ShapeDtypeStruct((s, nh * HEAD_DIM), F32)
    scratch = [pltpu.VMEM((dil * g.pad, HEAD_DIM), BF16), pltpu.VMEM((dil * g.pad, HEAD_DIM), BF16)]
    if g.staged:
        scratch += [pltpu.VMEM((s, HEAD_DIM), BF16), pltpu.VMEM((s, HEAD_DIM), F32), pltpu.VMEM((s, HEAD_DIM), F32)]
    return pl.pallas_call(
        body, name=name, grid=(nh,), in_specs=in_specs, out_specs=[col(0, 1), col(0, 1)], out_shape=[shape, shape],
        scratch_shapes=scratch, compiler_params=_params(("parallel",), 16 * s * HEAD_DIM * 4))(*args)


def band_attn_bwd(name, proj, bias, sink, dout, out, lse, dlse, *, half, q_rows, n_chains, dil, nh, group, cq, ck, cv):
    s, w = proj.shape
    g = _Band(s, half, q_rows, n_chains, dil)
    nkv = nh // group
    has_sink = sink is not None
    has_dl = dlse is not None
    n_in = 7 + int(has_sink) + int(has_dl)
    n_out = 4 + int(has_sink)

    def body(*refs):
        ins, outs, scr = refs[:n_in], refs[n_in:n_in + n_out], refs[n_in + n_out:]
        q_ref, k_ref, v_ref, b_ref, do_ref, o_ref, l_ref = ins[:7]
        s_ref = ins[7] if has_sink else None
        dl_ref = ins[n_in - 1] if has_dl else None
        dq_ref, dk_ref, dv_ref, db_ref = outs[:4]
        ks, vs, dks, dvs = scr[:4]
        scr = list(scr[4:])
        dsa = scr.pop(0) if has_sink else None
        if g.staged:
            qs, dos, os_, ls, dqs = scr[:5]
            dls = scr[5] if has_dl else None
            g.stage(qs, q_ref)
            g.stage(dos, do_ref)
            g.stage(os_, o_ref)
            g.stage(ls, l_ref)
            if has_dl:
                g.stage(dls, dl_ref)
        else:
            qs, dos, os_, ls, dqs, dls = None, do_ref, o_ref, l_ref, dq_ref, dl_ref
        h = pl.program_id(0)
        g.stage_kv(ks, k_ref)
        g.stage_kv(vs, v_ref)
        dks[...] = jnp.zeros_like(dks)
        dvs[...] = jnp.zeros_like(dvs)
        db_ref[...] = jnp.zeros_like(db_ref)
        bias_v = b_ref[...]
        if has_sink:
            sk = s_ref[h]
            dsa[...] = jnp.zeros_like(dsa)

        def chain(item):
            qoff, koff, edge = g.offsets(item)
            rows = pl.ds(qoff, g.q_rows)
            win = pl.ds(koff, g.win)
            qv = qs[rows, :] if g.staged else q_ref[rows, :].astype(BF16)
            kw_ = ks[win, :]
            vw_ = vs[win, :]
            sc = lax.dot_general(qv, kw_, (((1,), (1,)), ((), ())), preferred_element_type=F32) * ATTN_SCALE
            lv = ls[rows, :][:, 0:1]
            p = jnp.exp(sc + bias_v + edge - lv)
            dov = dos[rows, :]
            delta = jnp.sum(dov * os_[rows, :], axis=-1, keepdims=True)
            dob = dov.astype(BF16)
            dp = lax.dot_general(dob, vw_, (((1,), (1,)), ((), ())), preferred_element_type=F32)
            t = dp - delta
            if has_dl:
                t = t + dls[rows, :][:, 0:1]
            ds = p * t
            dsb = (ds * ATTN_SCALE).astype(BF16)
            dq = lax.dot_general(dsb, kw_, (((1,), (0,)), ((), ())), preferred_element_type=F32)
            dkc = lax.dot_general(dsb, qv, (((0,), (0,)), ((), ())), preferred_element_type=F32)
            dvc = lax.dot_general(p.astype(BF16), dob, (((0,), (0,)), ((), ())), preferred_element_type=F32)
            dsk = jnp.exp(sk - lv) * delta if has_sink else None
            return rows, win, dq, dkc, dvc, ds, dsk

        def step(i, carry):
            res = [chain(i * n_chains + u) for u in range(n_chains)]
            ds_sum = res[0][5]
            for rr in res[1:]:
                ds_sum = ds_sum + rr[5]
            db_ref[...] += ds_sum
            for rows, win, dq, dkc, dvc, ds, dsk in res:
                dqs[rows, :] = dq
                dks[win, :] += dkc
                dvs[win, :] += dvc
                if has_sink:
                    dsa[...] += dsk
            return carry

        lax.fori_loop(0, g.n_items // n_chains, step, 0)

        if g.staged:
            g.unstage(dq_ref, dqs)

        def emit_kv(add):
            for r in range(dil):
                lo = r * g.pad + half
                for dst_ref, src in ((dk_ref, dks), (dv_ref, dvs)):
                    val = src[lo:lo + g.seg, :]
                    if add:
                        val = val + dst_ref[g.rows_of(r), :]
                    dst_ref[g.rows_of(r), :] = val

        if group == 1:
            emit_kv(False)
        else:
            @pl.when(h % group == 0)
            def _():
                emit_kv(False)

            @pl.when(h % group != 0)
            def _():
                emit_kv(True)
        if has_sink:
            outs[4][...] = jnp.full((SUBLANES, LANES), -jnp.sum(dsa[...]), F32)

    def col(c0, per):
        return pl.BlockSpec((s, HEAD_DIM), lambda h: (0, c0 // LANES + h // per))

    b_spec = pl.BlockSpec((None, g.q_rows, g.win), lambda h: (h, 0, 0))
    in_specs = [col(cq, 1), col(ck, group), col(cv, group), b_spec, col(0, 1), col(0, 1), col(0, 1)]
    args = [proj, proj, proj, bias, dout, out, lse]
    if has_sink:
        in_specs.append(pl.BlockSpec(memory_space=pltpu.SMEM))
        args.append(sink)
    if has_dl:
        in_specs.append(col(0, 1))
        args.append(dlse)
    out_specs = [col(0, 1), col(0, group), col(0, group), b_spec]
    out_shape = [jax.ShapeDtypeStruct((s, nh * HEAD_DIM), F32), jax.ShapeDtypeStruct((s, nkv * HEAD_DIM), F32),
                 jax.ShapeDtypeStruct((s, nkv * HEAD_DIM), F32), jax.ShapeDtypeStruct((nh, g.q_rows, g.win), F32)]
    scratch = [pltpu.VMEM((dil * g.pad, HEAD_DIM), BF16), pltpu.VMEM((dil * g.pad, HEAD_DIM), BF16),
               pltpu.VMEM((dil * g.pad, HEAD_DIM), F32), pltpu.VMEM((dil * g.pad, HEAD_DIM), F32)]
    if has_sink:
        out_specs.append(pl.BlockSpec((None, SUBLANES, LANES), lambda h: (h, 0, 0)))
        out_shape.append(jax.ShapeDtypeStruct((nh, SUBLANES, LANES), F32))
        scratch.append(pltpu.VMEM((g.q_rows, 1), F32))
    if g.staged:
        scratch += [pltpu.VMEM((s, HEAD_DIM), BF16)] + [pltpu.VMEM((s, HEAD_DIM), F32)] * (4 + int(has_dl))
    res = pl.pallas_call(
        body, name=name, grid=(nh,), in_specs=in_specs, out_specs=out_specs, out_shape=out_shape,
        scratch_shapes=scratch, compiler_params=_params(("arbitrary",), 28 * s * HEAD_DIM * 4))(*args)
    return res[0], res[1], res[2], res[3], (res[4] if has_sink else None)


def dil_merge_fwd(name, outs, lses):
    s, w = outs[0].shape
    ts = _pick(s, ROW_TILE_CANDS)
    ng = len(outs)

    def body(*refs):
        o_refs, l_refs, y_ref = refs[:ng], refs[ng:2 * ng], refs[2 * ng]
        ls = [l[...] for l in l_refs]
        mx = ls[0]
        for l in ls[1:]:
            mx = jnp.maximum(mx, l)
        es = [jnp.exp(l - mx) for l in ls]
        tot = es[0]
        for e in es[1:]:
            tot = tot + e
        acc = (es[0] / tot) * o_refs[0][...]
        for e, o in zip(es[1:], o_refs[1:]):
            acc = acc + (e / tot) * o[...]
        y_ref[...] = acc.astype(y_ref.dtype)

    row = pl.BlockSpec((ts, w), lambda i: (i, 0))
    return pl.pallas_call(
        body, name=name, grid=(s // ts,), in_specs=[row] * (2 * ng), out_specs=row,
        out_shape=jax.ShapeDtypeStruct((s, w), BF16),
        compiler_params=_params(("parallel",), 10 * ts * w * 4))(*outs, *lses)


def dil_merge_bwd(name, dy, outs, lses):
    s, w = outs[0].shape
    ts = _pick(s, ROW_TILE_CANDS)
    ng = len(outs)
    nhead = w // HEAD_DIM

    def body(*refs):
        dy_ref = refs[0]
        o_refs, l_refs = refs[1:1 + ng], refs[1 + ng:1 + 2 * ng]
        do_refs, dl_refs = refs[1 + 2 * ng:1 + 3 * ng], refs[1 + 3 * ng:1 + 4 * ng]
        for hh in range(nhead):
            cols = slice(hh * HEAD_DIM, (hh + 1) * HEAD_DIM)
            dyv = dy_ref[:, cols]
            ls = [l[:, cols] for l in l_refs]
            mx = ls[0]
            for l in ls[1:]:
                mx = jnp.maximum(mx, l)
            es = [jnp.exp(l - mx) for l in ls]
            tot = es[0]
            for e in es[1:]:
                tot = tot + e
            alphas = [e / tot for e in es]
            dal = [jnp.broadcast_to(jnp.sum(dyv * o[:, cols], axis=-1, keepdims=True), dyv.shape) for o in o_refs]
            mean = alphas[0] * dal[0]
            for a, d in zip(alphas[1:], dal[1:]):
                mean = mean + a * d
            for g in range(ng):
                do_refs[g][:, cols] = alphas[g] * dyv
                dl_refs[g][:, cols] = alphas[g] * (dal[g] - mean)

    row = pl.BlockSpec((ts, w), lambda i: (i, 0))
    shape = jax.ShapeDtypeStruct((s, w), F32)
    res = pl.pallas_call(
        body, name=name, grid=(s // ts,), in_specs=[row] * (1 + 2 * ng), out_specs=[row] * (2 * ng),
        out_shape=[shape] * (2 * ng),
        compiler_params=_params(("parallel",), 16 * ts * w * 4))(dy, *outs, *lses)
    return res[:ng], res[ng:]


def _adamw(w, g, m, v):
    m = ADAM_B1 * m + (1.0 - ADAM_B1) * g
    v = ADAM_B2 * v + (1.0 - ADAM_B2) * (g * g)
    m_hat = m / (1.0 - ADAM_B1 ** ADAM_STEP)
    v_hat = v / (1.0 - ADAM_B2 ** ADAM_STEP)
    delta = -ADAM_LR * (m_hat / (jnp.sqrt(v_hat) + ADAM_EPS) + ADAM_WD * w)
    return delta, m, v


def _row_tile(r, c, budget=1 << 20):
    if r * c * 4 <= budget or r % SUBLANES:
        return r
    for t in (1024, 512, 256, 128, 64, 32, 16, 8):
        if r % t == 0 and t * c * 4 <= budget:
            return t
    return SUBLANES


def adam_small(name, g, w, m, v):
    def body(g_ref, w_ref, m_ref, v_ref, d_ref, nm_ref, nv_ref):
        d_ref[...], nm_ref[...], nv_ref[...] = _adamw(w_ref[...], g_ref[...], m_ref[...], v_ref[...])

    shape = jax.ShapeDtypeStruct(w.shape, F32)
    return pl.pallas_call(body, name=name, out_shape=[shape, shape, shape])(g, w, m, v)


def reduce_adam(name, mine, theirs, w, m, v):
    nq, r, c = mine.shape
    tr = _row_tile(r, c)

    def body(*refs):
        parts, (w_ref, m_ref, v_ref, g_ref, d_ref, nm_ref, nv_ref) = refs[:nq], refs[nq:]
        g = parts[0][...].astype(F32)
        for p_ref in parts[1:]:
            g = g + p_ref[...].astype(F32)
        g_ref[...] = g
        d_ref[...], nm_ref[...], nv_ref[...] = _adamw(w_ref[...], g, m_ref[...], v_ref[...])

    def slot(q):
        return pl.BlockSpec((None, tr, c), lambda i: (q, i, 0))

    row = pl.BlockSpec((tr, c), lambda i: (i, 0))
    shape = jax.ShapeDtypeStruct((r, c), F32)
    return pl.pallas_call(
        body, name=name, grid=(r // tr,), in_specs=[slot(q) for q in range(nq)] + [row, row, row],
        out_specs=[row] * 4, out_shape=[shape] * 4,
        compiler_params=_params(("parallel",), (nq * 2 + 7 * 4) * tr * c))(mine, *[theirs] * (nq - 1), w, m, v)


def _place():
    return lax.axis_index("x"), lax.axis_index("y"), lax.axis_index("c")


def _flip(pos, bits):
    return tuple((1 - p) if b else p for p, b in zip(pos, bits))


def _index(pos):
    return 4 * pos[0] + 2 * pos[1] + pos[2]


ANY = pl.BlockSpec(memory_space=pl.ANY)


HBM = pl.BlockSpec(memory_space=pltpu.HBM)
SEM = pl.BlockSpec(memory_space=pltpu.SEMAPHORE)
EFFECT = pltpu.SideEffectType.DATAFLOW_SIDE_EFFECTING
TO_SIBLING = (0, 0, 1)
TO_CHIPS = [(1, 0, 0), (0, 1, 0), (1, 1, 0)]


def _in_hbm(a):
    return pltpu.with_memory_space_constraint(a, pltpu.HBM)


def _token_value(token):
    return token[0, 0]


def split_start(name, srcs, lands, plan, after):
    n = len(srcs)
    n_copies = len(plan((0, 0, 0)))

    def body(*refs):
        ins, lnd = refs[:n], refs[n:2 * n]
        send_sems, recv_sems = refs[2 * n + 1], refs[2 * n + 2]
        token = refs[-1]
        me = _place()
        for k, (a, sblk, lblk, rel) in enumerate(plan(me)):
            src = ins[a] if sblk is None else ins[a].at[sblk]
            pltpu.make_async_remote_copy(
                src_ref=src, dst_ref=lnd[a].at[lblk], send_sem=send_sems.at[k], recv_sem=recv_sems.at[k],
                device_id=_flip(me, rel), device_id_type=MESH).start()
        token[...] = jnp.zeros_like(token)

    outs = pl.pallas_call(
        body, name=name,
        out_shape=(pltpu.SemaphoreType.DMA((n_copies,)), pltpu.SemaphoreType.DMA((n_copies,)),
                   *[pltpu.HBM(a.shape, a.dtype) for a in srcs], *[pltpu.HBM(a.shape, a.dtype) for a in lands],
                   jax.ShapeDtypeStruct((SUBLANES, LANES), F32)),
        in_specs=[HBM] * (2 * n) + [ANY],
        out_specs=(SEM, SEM, *[HBM] * (2 * n), pl.BlockSpec(memory_space=pltpu.VMEM)),
        input_output_aliases={i: 2 + i for i in range(2 * n)},
        compiler_params=pltpu.CompilerParams(has_side_effects=EFFECT),
    )(*[_in_hbm(a) for a in srcs], *[_in_hbm(a) for a in lands], after)
    return outs[0], outs[1], list(outs[2:2 + n]), list(outs[2 + n:2 + 2 * n]), outs[-1]


def split_wait(name, send_sems, recv_sems, srcs, lands, plan, after):
    n = len(srcs)

    def body(*refs):
        ins, lnd = refs[:n], refs[n:2 * n]
        s_sems, r_sems = refs[2 * n], refs[2 * n + 1]
        me = _place()
        for k, (a, sblk, lblk, rel) in enumerate(plan(me)):
            src = ins[a] if sblk is None else ins[a].at[sblk]
            cp = pltpu.make_async_remote_copy(
                src_ref=src, dst_ref=lnd[a].at[lblk], send_sem=s_sems.at[k], recv_sem=r_sems.at[k],
                device_id=_flip(me, rel), device_id_type=MESH)
            cp.wait_send()
            cp.wait_recv()

    outs = pl.pallas_call(
        body, name=name,
        out_shape=(*[pltpu.HBM(a.shape, a.dtype) for a in srcs], *[pltpu.HBM(a.shape, a.dtype) for a in lands]),
        in_specs=[HBM] * (2 * n) + [SEM, SEM, ANY],
        out_specs=tuple([HBM] * (2 * n)),
        input_output_aliases={i: i for i in range(2 * n)},
        compiler_params=pltpu.CompilerParams(has_side_effects=EFFECT),
    )(*srcs, *lands, send_sems, recv_sems, after)
    return list(outs[:n]), list(outs[n:])


def ag_plan(n):
    def plan(me):
        return [(a, None, _index(me), rel) for a in range(n) for rel in [TO_SIBLING] + TO_CHIPS]
    return plan


def ag_start(name, shards, after):
    lands = [lax.empty((N_DEV,) + tuple(sh.shape), sh.dtype) for sh in shards]
    return split_start(name, shards, lands, ag_plan(len(shards)), after)


def ag_finish(name, lands):
    n = len(lands)

    def body(*refs):
        lnd = refs[n:2 * n]
        send_sems, recv_sems = refs[2 * n:]
        me = _place()
        sibling = _flip(me, TO_SIBLING)
        copies = []
        for a in range(n):
            for j, rel in enumerate(TO_CHIPS):
                blk = lnd[a].at[_index(_flip(me, rel))]
                there = lnd[a].at[_index(_flip(sibling, rel))]
                cp = pltpu.make_async_remote_copy(
                    src_ref=blk, dst_ref=blk, send_sem=send_sems.at[a * 3 + j], recv_sem=recv_sems.at[a * 3 + j],
                    device_id=sibling, device_id_type=MESH)
                cp.start()
                copies.append((cp, pltpu.make_async_remote_copy(
                    src_ref=blk, dst_ref=there, send_sem=send_sems.at[a * 3 + j], recv_sem=recv_sems.at[a * 3 + j],
                    device_id=sibling, device_id_type=MESH)))
        for cp, arrival in copies:
            arrival.wait_recv()
        for cp, arrival in copies:
            cp.wait_send()

    return pl.pallas_call(
        body, name=name, in_specs=[ANY] * n, out_specs=[ANY] * n,
        out_shape=[jax.ShapeDtypeStruct(l.shape, l.dtype) for l in lands],
        input_output_aliases={a: a for a in range(n)},
        scratch_shapes=[pltpu.SemaphoreType.DMA((3 * n,)), pltpu.SemaphoreType.DMA((3 * n,))],
    )(*lands)


REL = [(b >> 2 & 1, b >> 1 & 1, b & 1) for b in range(N_DEV)]


CHIP_REL = [(0, 0, 0)] + TO_CHIPS
N_CHIPS = len(CHIP_REL)


def rs_pair(name, parts):
    n = len(parts)

    def body(*refs):
        ins, got = refs[:n], refs[n:2 * n]
        send_sems, recv_sems = refs[2 * n:]
        me = _place()
        sibling = _flip(me, TO_SIBLING)
        remote = []
        for a in range(n):
            for q, rel in enumerate(CHIP_REL):
                k = a * N_CHIPS + q
                cp = pltpu.make_async_remote_copy(
                    src_ref=ins[a].at[_index(_flip(sibling, rel))], dst_ref=got[a].at[q], send_sem=send_sems.at[k],
                    recv_sem=recv_sems.at[k], device_id=sibling, device_id_type=MESH)
                cp.start()
                remote.append(cp)
        for cp in remote:
            cp.wait_recv()
        for cp in remote:
            cp.wait_send()

    shapes = [jax.ShapeDtypeStruct((N_CHIPS,) + tuple(p.shape[1:]), p.dtype) for p in parts]
    res = pl.pallas_call(
        body, name=name, in_specs=[ANY] * n, out_specs=[ANY] * n, out_shape=shapes,
        scratch_shapes=[pltpu.SemaphoreType.DMA((N_CHIPS * n,)), pltpu.SemaphoreType.DMA((N_CHIPS * n,))],
    )(*parts)
    return list(res)


def own_blocks():
    me = _place()
    return jnp.stack([_index(_flip(me, rel)) for rel in CHIP_REL]).astype(jnp.int32)


def pair_add(name, blocks, parts, got):
    nq, r, c = got.shape
    tr = _row_tile(r, c)

    def body(blk_ref, a_ref, b_ref, o_ref):
        o_ref[...] = (a_ref[...].astype(F32) + b_ref[...].astype(F32)).astype(o_ref.dtype)

    spec = pl.BlockSpec((None, tr, c), lambda q, i, blk: (q, i, 0))
    return pl.pallas_call(
        body, name=name,
        grid_spec=pltpu.PrefetchScalarGridSpec(
            num_scalar_prefetch=1, grid=(nq, r // tr),
            in_specs=[pl.BlockSpec((None, tr, c), lambda q, i, blk: (blk[q], i, 0)), spec], out_specs=spec),
        out_shape=jax.ShapeDtypeStruct(got.shape, got.dtype),
        compiler_params=_params(("arbitrary", "arbitrary"), 6 * tr * c * 2))(blocks, parts, got)


def rs_plan(n):
    def plan(me):
        return [(a, q, q, CHIP_REL[q]) for a in range(n) for q in range(1, N_CHIPS)]
    return plan


def rs_start(name, sums):
    lands = [lax.empty(t.shape, t.dtype) for t in sums]
    return split_start(name, sums, lands, rs_plan(len(sums)), sums[0])


def allreduce_small(name, pack, after):
    rows, lanes = pack.shape

    def body(x_ref, after_ref, o_ref, land, send_sems, recv_sems):
        me = _place()
        idx = _index(me)
        land[idx] = x_ref[...]
        copies = []
        for r in range(1, N_DEV):
            peer = _flip(me, REL[r])
            cp = pltpu.make_async_remote_copy(
                src_ref=x_ref, dst_ref=land.at[idx], send_sem=send_sems.at[r - 1], recv_sem=recv_sems.at[r - 1],
                device_id=peer, device_id_type=MESH)
            cp.start()
            copies.append(cp)
        for cp in copies:
            cp.wait_recv()
        for cp in copies:
            cp.wait_send()
        acc = land[0]
        for i in range(1, N_DEV):
            acc = acc + land[i]
        o_ref[...] = acc

    return pl.pallas_call(
        body, name=name, in_specs=[pl.BlockSpec(memory_space=pltpu.VMEM), ANY],
        out_specs=pl.BlockSpec(memory_space=pltpu.VMEM), out_shape=jax.ShapeDtypeStruct((rows, lanes), F32),
        scratch_shapes=[pltpu.VMEM((N_DEV, rows, lanes), F32), pltpu.SemaphoreType.DMA((7,)),
                        pltpu.SemaphoreType.DMA((7,))],
    )(pack, after)


def _pad_rows(a, rows):
    return jnp.pad(a, ((0, rows - a.shape[0]), (0, 0)))


def _as_tiles(vec):
    n = vec.shape[0]
    rows = -(-n // LANES)
    rows = -(-rows // SUBLANES) * SUBLANES
    return jnp.pad(vec, (0, rows * LANES - n)).reshape(rows, LANES)


def kernel(x, p, rel_bias_table, attn_norm, w_in, sink_a, w_branch_a, w_branch_b, w_out, ffn_norm, w_ffn_gate, w_ffn_up, conv_w, conv_b, w_ffn_down, ple_norm, w_ple_gate, w_ple_proj, final_norm, loss_target, m_rel_bias_table, m_attn_norm, m_w_in, m_sink_a, m_w_branch_a, m_w_branch_b, m_w_out, m_ffn_norm, m_w_ffn_gate, m_w_ffn_up, m_conv_w, m_conv_b, m_w_ffn_down, m_ple_norm, m_w_ple_gate, m_w_ple_proj, m_final_norm, v_rel_bias_table, v_attn_norm, v_w_in, v_sink_a, v_w_branch_a, v_w_branch_b, v_w_out, v_ffn_norm, v_w_ffn_gate, v_w_ffn_up, v_conv_w, v_conv_b, v_w_ffn_down, v_ple_norm, v_w_ple_gate, v_w_ple_proj, v_final_norm):
    xs = x[0]
    s, d = xs.shape
    ps = p[0, 0]
    target = loss_target[0]
    me = 4 * lax.axis_index("x") + 2 * lax.axis_index("y") + lax.axis_index("c")

    big = dict(w_in=w_in[0], w_branch_a=w_branch_a[0], w_branch_b=w_branch_b[0], w_out=w_out[0],
               w_ffn_gate=w_ffn_gate[0], w_ffn_up=w_ffn_up[0], w_ffn_down=w_ffn_down[0],
               w_ple_gate=w_ple_gate[0], w_ple_proj=w_ple_proj[0])
    big_m = dict(w_in=m_w_in[0], w_branch_a=m_w_branch_a[0], w_branch_b=m_w_branch_b[0], w_out=m_w_out[0],
                 w_ffn_gate=m_w_ffn_gate[0], w_ffn_up=m_w_ffn_up[0], w_ffn_down=m_w_ffn_down[0],
                 w_ple_gate=m_w_ple_gate[0], w_ple_proj=m_w_ple_proj[0])
    big_v = dict(w_in=v_w_in[0], w_branch_a=v_w_branch_a[0], w_branch_b=v_w_branch_b[0], w_out=v_w_out[0],
                 w_ffn_gate=v_w_ffn_gate[0], w_ffn_up=v_w_ffn_up[0], w_ffn_down=v_w_ffn_down[0],
                 w_ple_gate=v_w_ple_gate[0], w_ple_proj=v_w_ple_proj[0])
    names = list(big)
    nf = big["w_ffn_gate"].shape[1]

    shards = {k: big[k].astype(BF16) for k in names}
    shards["conv_w"] = _pad_rows(conv_w[0], SUBLANES)
    ag_groups = [["w_in"], ["w_branch_a", "w_branch_b", "w_out"], ["w_ffn_gate", "w_ffn_up", "conv_w"],
                 ["w_ffn_down", "w_ple_gate", "w_ple_proj"]]
    ag_started = {}
    wg = {}

    def start_gather(gi, after):
        s_sems, r_sems, srcs, lands, token = ag_start(f"ag_start{gi}", [shards[k] for k in ag_groups[gi]], after)
        ag_started[gi] = (s_sems, r_sems, srcs, lands)
        return _token_value(token)

    def gather(gi, after):
        s_sems, r_sems, srcs, lands = ag_started[gi]
        srcs, lands = split_wait(f"ag_wait{gi}", s_sems, r_sems, srcs, lands, ag_plan(len(srcs)), after)
        lands = ag_finish(f"ag_finish{gi}", lands)
        wg.update({k: lax.dynamic_update_index_in_dim(l, sh, me, 0) for k, l, sh in zip(ag_groups[gi], lands, srcs)})

    cb = conv_b.reshape(N_DEV, 1, nf)

    table_t = rel_bias_table.T
    geo_a = dict(half=A_BLOCK, q_rows=ATTN_Q_ROWS, n_chains=ATTN_CHAINS, dil=1, nh=A_Q_HEADS, group=A_GROUP,
                 cq=COL_QA, ck=COL_KA, cv=COL_VA)
    geo_b = [dict(half=B_BLOCK, q_rows=min(ATTN_Q_ROWS, s // dil), n_chains=ATTN_CHAINS, dil=dil,
                  nh=B_HEADS_PER_GROUP, group=1, cq=COL_QB + g * B_OUT_W, ck=COL_KB + g * B_OUT_W,
                  cv=COL_VB + g * B_OUT_W) for g, (_, dil) in enumerate(B_PATTERNS)]
    bucket_a = bucket_tile(geo_a["q_rows"], A_BLOCK, 1)
    bias_a = bias_build("bias_a", table_t, bucket_a, 0, A_Q_HEADS, A_BLOCK)
    buckets_b = [bucket_tile(gb["q_rows"], B_BLOCK, gb["dil"]) for gb in geo_b]
    biases_b = [bias_build(f"bias_b{g}", table_t, buckets_b[g], A_Q_HEADS + g * B_HEADS_PER_GROUP, B_HEADS_PER_GROUP,
                           B_BLOCK) for g in range(len(B_PATTERNS))]

    tok = start_gather(0, xs)
    h = rms_fwd("rms_attn", xs, attn_norm + tok)
    gather(0, h)
    tok = start_gather(1, wg["w_in"]) + start_gather(2, wg["w_in"])
    proj = mm_cols("proj_in", h, wg["w_in"], F32, fold=True)
    sink = sink_a[0] + tok
    ya, lse_a = band_attn_fwd("attn_a_fwd", proj, bias_a, sink, **geo_a)
    outs_b, lses_b = [], []
    for g in range(len(B_PATTERNS)):
        o, l = band_attn_fwd(f"attn_b{g}_fwd", proj, biases_b[g], None, **geo_b[g])
        outs_b.append(o)
        lses_b.append(l)
    yb = dil_merge_fwd("dil_merge_fwd", outs_b, lses_b)
    gather(1, yb)
    tok = start_gather(3, wg["w_out"])
    w_out_full = wg["w_out"].reshape(d, d)
    ta = mm_cols("branch_a", ya, wg["w_branch_a"], F32, fold=True)
    tb = mm_cols("branch_b", yb, wg["w_branch_b"], F32, fold=True)
    merged = gate_merge_fwd("gate_merge_fwd", proj, ta, tb, d)
    x1 = mm_plain("mix_out", merged, w_out_full, F32, res=xs)

    hf = rms_fwd("rms_ffn", x1, ffn_norm + tok)
    gather(2, hf)
    cw = wg["conv_w"]
    gpre = mm_cols("ffn_gate", hf, wg["w_ffn_gate"], F32, fold=False)
    u = mm_cols("ffn_up", hf, wg["w_ffn_up"], F32, fold=False)
    z = ffn_mid_fwd("ffn_mid_fwd", gpre, u, cw, cb)
    gather(3, z)
    w_pg_full = wg["w_ple_gate"].reshape(d, d)
    x2 = mm_jsum("ffn_down", z, wg["w_ffn_down"], F32, res=x1)

    hp = rms_fwd("rms_ple", x2, ple_norm)
    lp = mm_plain("ple_gate", hp, w_pg_full, F32)
    pp = mm_cols("ple_proj", ps, wg["w_ple_proj"], F32, fold=True)
    loss_part, dx3, dlp, dpp, d_final = tail_fwd_bwd("tail", x2, lp, pp, final_norm.reshape(1, d), target)

    grads = {}
    rs_started = []
    blocks = own_blocks()

    def scatter(tag, keys):
        got = rs_pair(f"rs_pair_{tag}", [grads[k] for k in keys])
        sums = [pair_add(f"pair_add_{k}", blocks, grads[k], g) for k, g in zip(keys, got)]
        s_sems, r_sems, srcs, lands, token = rs_start(f"rs_start_{tag}", sums)
        rs_started.append((tag, keys, s_sems, r_sems, srcs, lands))
        return _token_value(token)

    grads["w_ple_proj"] = mm_tn_cols("d_w_ple_proj", ps, dpp, N_DEV, big["w_ple_proj"].shape[1], BF16, folded=True)
    grads["w_ple_gate"] = mm_tn_plain("d_w_ple_gate", hp, dlp, BF16).reshape(N_DEV, d // N_DEV, d)
    tok = scatter("ple", ["w_ple_proj", "w_ple_gate"])
    dhp = mm_nt_plain("d_hp", dlp, w_pg_full, F32)
    dx2, d_ple = rms_bwd("rms_ple_bwd", x2, ple_norm + tok, dhp, dx3)

    dz = mm_nt_j("d_z", dx2, wg["w_ffn_down"], BF16)
    grads["w_ffn_down"] = mm_tn_j("d_w_ffn_down", z, dx2, BF16)
    tok = scatter("down", ["w_ffn_down"])
    dg, du, dcw = ffn_mid_bwd1("ffn_mid_bwd1", gpre, u, dz, cw, cb + tok)
    dgpre = ffn_mid_bwd2("ffn_mid_bwd2", dg, cw)
    grads["w_ffn_up"] = mm_tn_cols("d_w_ffn_up", hf, du, N_DEV, nf, BF16, folded=False)
    grads["w_ffn_gate"] = mm_tn_cols("d_w_ffn_gate", hf, dgpre, N_DEV, nf, BF16, folded=False)
    tok = scatter("upgate", ["w_ffn_up", "w_ffn_gate"])
    dhf = mm_nt_jsum("d_hf_up", du, wg["w_ffn_up"], F32, folded=False)
    dhf = mm_nt_jsum("d_hf_gate", dgpre, wg["w_ffn_gate"], F32, folded=False, res=dhf)
    dx1, d_ffn = rms_bwd("rms_ffn_bwd", x1, ffn_norm + tok, dhf, dx2)

    dmerged = mm_nt_plain("d_merged", dx1, w_out_full, F32)
    grads["w_out"] = mm_tn_plain("d_w_out", merged, dx1, BF16).reshape(N_DEV, d // N_DEV, d)
    dta, dtb, dga, dgb = gate_merge_bwd("gate_merge_bwd", dmerged, proj, ta, tb, d)
    grads["w_branch_a"] = mm_tn_cols("d_w_branch_a", ya, dta, N_DEV, big["w_branch_a"].shape[1], BF16, folded=True)
    grads["w_branch_b"] = mm_tn_cols("d_w_branch_b", yb, dtb, N_DEV, big["w_branch_b"].shape[1], BF16, folded=True)
    tok = scatter("mix", ["w_out", "w_branch_a", "w_branch_b"])
    dya = mm_nt_jsum("d_ya", dta, wg["w_branch_a"], F32, folded=True)
    dyb = mm_nt_jsum("d_yb", dtb, wg["w_branch_b"], F32, folded=True)
    dqa, dka, dva, dbias_a, dsink = band_attn_bwd("attn_a_bwd", proj, bias_a, sink + tok, dya, ya, lse_a, None, **geo_a)
    douts_b, dlses_b = dil_merge_bwd("dil_merge_bwd", dyb, outs_b, lses_b)
    dq_b, dk_b, dv_b, dbias_b = [], [], [], []
    for g in range(len(B_PATTERNS)):
        dq, dk, dv, db, _ = band_attn_bwd(f"attn_b{g}_bwd", proj, biases_b[g], None, douts_b[g], outs_b[g], lses_b[g],
                                          dlses_b[g], **geo_b[g])
        dq_b.append(dq)
        dk_b.append(dk)
        dv_b.append(dv)
        dbias_b.append(db)
    dproj = jnp.concatenate([t.astype(BF16) for t in [dqa, dka, dva] + dq_b + dk_b + dv_b + [dga, dgb]], axis=1)
    grads["w_in"] = mm_tn_cols("d_w_in", h, dproj, N_DEV, big["w_in"].shape[1], BF16, folded=True)
    tok = scatter("in", ["w_in"])
    dh = mm_nt_jsum("d_h", dproj, wg["w_in"], F32, folded=True)
    grad_x, d_attn = rms_bwd("rms_attn_bwd", xs, attn_norm + tok, dh, dx1)

    dt_a = table_grad("table_grad_a", dbias_a, bucket_a)[:, 0, :N_BUCKETS]
    dt_b = [table_grad(f"table_grad_b{g}", dbias_b[g], buckets_b[g])[:, 0, :N_BUCKETS] for g in range(len(B_PATTERNS))]
    d_table_part = jnp.concatenate([dt_a] + dt_b, axis=0).T

    pieces = [
        ("loss", loss_part[0, :1]),
        ("table", d_table_part.reshape(-1)),
        ("attn_norm", d_attn.reshape(-1)),
        ("sink", dsink[:, 0, 0]),
        ("ffn_norm", d_ffn.reshape(-1)),
        ("conv_w", dcw[:, 0:3, :].reshape(-1)),
        ("conv_b", dcw[:, 3, :].reshape(-1)),
        ("ple_norm", d_ple.reshape(-1)),
        ("final_norm", d_final.reshape(-1)),
    ]
    tiles = [_as_tiles(v) for _, v in pieces]
    pack = jnp.concatenate(tiles, axis=0)

    out_g, out_d, out_m, out_v = {}, {}, {}, {}

    def finish(group, after):
        tag, keys, s_sems, r_sems, srcs, lands = group
        srcs, lands = split_wait(f"rs_wait_{tag}", s_sems, r_sems, srcs, lands, rs_plan(len(keys)), after)
        for k, mine, theirs in zip(keys, srcs, lands):
            g, dl, nm, nv = reduce_adam("adam_" + k, mine, theirs, big[k], big_m[k], big_v[k])
            out_g[k], out_d[k], out_m[k], out_v[k] = g[None], dl[None], nm[None], nv[None]
            after = dl
        return after

    after = pack
    for group in rs_started[:-1]:
        after = finish(group, after)
    total = allreduce_small("allreduce_small", pack, after)
    finish(rs_started[-1], total)
    small = {}
    row = 0
    for (nm, v), t in zip(pieces, tiles):
        small[nm] = total[row:row + t.shape[0]].reshape(-1)[:v.shape[0]]
        row += t.shape[0]
    loss = small["loss"][0]
    g_small = dict(
        rel_bias_table=small["table"].reshape(rel_bias_table.shape),
        attn_norm=small["attn_norm"].reshape(attn_norm.shape),
        sink_a=small["sink"].reshape(sink_a.shape),
        ffn_norm=small["ffn_norm"].reshape(ffn_norm.shape),
        conv_w=lax.dynamic_index_in_dim(small["conv_w"].reshape(N_DEV, 3, nf), me, 0, keepdims=False)[None],
        conv_b=small["conv_b"].reshape(conv_b.shape),
        ple_norm=small["ple_norm"].reshape(ple_norm.shape),
        final_norm=small["final_norm"].reshape(1, d),
    )
    w_small = dict(rel_bias_table=(rel_bias_table, m_rel_bias_table, v_rel_bias_table),
                   attn_norm=(attn_norm, m_attn_norm, v_attn_norm), sink_a=(sink_a, m_sink_a, v_sink_a),
                   ffn_norm=(ffn_norm, m_ffn_norm, v_ffn_norm), conv_w=(conv_w, m_conv_w, v_conv_w),
                   conv_b=(conv_b, m_conv_b, v_conv_b), ple_norm=(ple_norm, m_ple_norm, v_ple_norm),
                   final_norm=(final_norm, m_final_norm, v_final_norm))

    for k, (wv, mv, vv) in w_small.items():
        shape = wv.shape
        two_d = (1, shape[0]) if len(shape) == 1 else ((shape[0] * shape[1], shape[2]) if len(shape) == 3 else shape)
        gk = g_small[k].reshape(two_d)
        dl, nm, nv = adam_small("adam_" + k, gk, wv.reshape(two_d), mv.reshape(two_d), vv.reshape(two_d))
        out_g[k], out_d[k], out_m[k], out_v[k] = gk.reshape(shape), dl.reshape(shape), nm.reshape(shape), nv.reshape(shape)

    order = ["rel_bias_table", "attn_norm", "w_in", "sink_a", "w_branch_a", "w_branch_b", "w_out", "ffn_norm",
             "w_ffn_gate", "w_ffn_up", "conv_w", "conv_b", "w_ffn_down", "ple_norm", "w_ple_gate", "w_ple_proj",
             "final_norm"]
    return (loss, grad_x[None], *[out_g[k] for k in order], *[out_d[k] for k in order],
            *[out_m[k] for k in order], *[out_v[k] for k in order])
```

```python
import math

import jax
import jax.numpy as jnp
from jax import lax
from jax.experimental import pallas as pl
from jax.experimental.pallas import tpu as pltpu

F32 = jnp.float32
BF16 = jnp.bfloat16
MESH = pl.DeviceIdType.MESH
N_DEV = 8

HEAD_DIM = 128
A_Q_HEADS = 8
A_KV_HEADS = 2
A_GROUP = A_Q_HEADS // A_KV_HEADS
A_BLOCK = 128
B_PATTERNS = ((128, 1), (512, 4), (2048, 16))
B_HEADS_PER_GROUP = 4
B_HEADS = len(B_PATTERNS) * B_HEADS_PER_GROUP
B_BLOCK = 64
N_BUCKETS = 32
MAX_DISTANCE = 1024
A_Q_W = A_Q_HEADS * HEAD_DIM
A_KV_W = A_KV_HEADS * HEAD_DIM
B_W = B_HEADS * HEAD_DIM
B_OUT_W = B_HEADS_PER_GROUP * HEAD_DIM
COL_QA = 0
COL_KA = COL_QA + A_Q_W
COL_VA = COL_KA + A_KV_W
COL_QB = COL_VA + A_KV_W
COL_KB = COL_QB + B_W
COL_VB = COL_KB + B_W
COL_GATES = COL_VB + B_W
RMS_EPS = 1e-6
NEG_INF = -1e30
ATTN_SCALE = HEAD_DIM ** -0.5
ATTN_Q_ROWS = 256
ATTN_CHAINS = 2

ADAM_LR = 0.001
ADAM_B1 = 0.9
ADAM_B2 = 0.999
ADAM_EPS = 1e-08
ADAM_WD = 0.01
ADAM_STEP = 10

GELU_C = math.sqrt(2.0 / math.pi)
GELU_A = 0.044715

V7X_VMEM_BYTES = 64 * 1024 * 1024
VMEM_CEILING = V7X_VMEM_BYTES - 8 * 1024 * 1024
LANES = 128
SUBLANES = 8


def _pick(n, cands):
    for c in cands:
        if n % c == 0:
            return c
    return n


def _nbytes(shape, dtype):
    n = 1
    for d in shape:
        if d is not None:
            n *= d
    return n * jnp.dtype(dtype).itemsize


def _params(sem, est_bytes):
    limit = int(min(VMEM_CEILING, max(32 * 1024 * 1024, 2 * est_bytes + (8 << 20))))
    return pltpu.CompilerParams(dimension_semantics=sem, vmem_limit_bytes=limit)


def _mm(name, a, b, a_bs, a_im, b_bs, b_im, out_shape, out_dtype, o_bs, o_im, grid, dims,
        res=None, r_bs=None, r_im=None):
    nk = grid[-1]
    nax = len(grid)
    has_res = res is not None
    o_tile = tuple(d for d in o_bs if d is not None)

    def body(*refs):
        if has_res:
            a_ref, b_ref, r_ref, o_ref = refs[:4]
            rest = refs[4:]
        else:
            a_ref, b_ref, o_ref = refs[:3]
            r_ref = None
            rest = refs[3:]

        def prod():
            return lax.dot_general(a_ref[...].astype(BF16), b_ref[...].astype(BF16), (dims, ((), ())),
                                   preferred_element_type=F32)

        def finish(r):
            if r_ref is not None:
                r = r + r_ref[...].astype(F32)
            o_ref[...] = r.astype(o_ref.dtype)

        if nk == 1:
            finish(prod())
        else:
            acc = rest[0]
            k = pl.program_id(nax - 1)

            @pl.when(k == 0)
            def _():
                acc[...] = prod()

            @pl.when(k > 0)
            def _():
                acc[...] += prod()

            @pl.when(k == nk - 1)
            def _():
                finish(acc[...])

    in_specs = [pl.BlockSpec(a_bs, a_im), pl.BlockSpec(b_bs, b_im)]
    args = [a, b]
    est = _nbytes(a_bs, a.dtype) + _nbytes(b_bs, b.dtype) + _nbytes(o_bs, out_dtype) + 2 * _nbytes(o_tile, F32)
    if has_res:
        in_specs.append(pl.BlockSpec(r_bs, r_im))
        args.append(res)
        est += _nbytes(r_bs, res.dtype)
    scratch = [] if nk == 1 else [pltpu.VMEM(o_tile, F32)]
    sem = ("parallel",) * (nax - 1) + ("arbitrary",)
    return pl.pallas_call(
        body, name=name, grid=grid, in_specs=in_specs, out_specs=pl.BlockSpec(o_bs, o_im),
        out_shape=jax.ShapeDtypeStruct(out_shape, out_dtype), scratch_shapes=scratch,
        compiler_params=_params(sem, est))(*[_in_hbm(a) for a in args])


TM_CANDS = (1024, 512, 256, 128, 64, 32, 16, 8)
TK_CANDS = (1024, 512, 256, 128)
TN_CANDS = (1024, 512, 256, 128)


def mm_cols(name, a, wg, out_dtype, fold):
    m, k = a.shape
    nj, _, n = wg.shape
    tm, tk = _pick(m, TM_CANDS), _pick(k, TK_CANDS)
    grid = (nj, m // tm, k // tk)
    if fold:
        shape, o_bs, o_im = (m, nj * n), (tm, n), (lambda j, i, kk: (i, j))
    else:
        shape, o_bs, o_im = (nj, m, n), (None, tm, n), (lambda j, i, kk: (j, i, 0))
    return _mm(name, a, wg, (tm, tk), lambda j, i, kk: (i, kk), (None, tk, n), lambda j, i, kk: (j, kk, 0),
               shape, out_dtype, o_bs, o_im, grid, ((1,), (0,)))


def mm_plain(name, a, w, out_dtype, res=None):
    m, k = a.shape
    n = w.shape[1]
    tm, tk, tn = _pick(m, TM_CANDS), _pick(k, TK_CANDS), _pick(n, TN_CANDS)
    grid = (n // tn, m // tm, k // tk)
    return _mm(name, a, w, (tm, tk), lambda j, i, kk: (i, kk), (tk, tn), lambda j, i, kk: (kk, j),
               (m, n), out_dtype, (tm, tn), lambda j, i, kk: (i, j), grid, ((1,), (0,)),
               res, (tm, tn), lambda j, i, kk: (i, j))


def mm_jsum(name, aj, wg, out_dtype, res=None):
    nj, m, ka = aj.shape
    n = wg.shape[2]
    tm, tn = _pick(m, TM_CANDS), _pick(n, TN_CANDS)
    grid = (m // tm, n // tn, nj)
    return _mm(name, aj, wg, (None, tm, ka), lambda i, jn, j: (j, i, 0), (None, ka, tn), lambda i, jn, j: (j, 0, jn),
               (m, n), out_dtype, (tm, tn), lambda i, jn, j: (i, jn), grid, ((1,), (0,)),
               res, (tm, tn), lambda i, jn, j: (i, jn))


def mm_tn_cols(name, a, g, nj, n, out_dtype, folded):
    s, kw = a.shape
    ts, tkw = _pick(s, TK_CANDS), _pick(kw, TM_CANDS)
    grid = (nj, kw // tkw, s // ts)
    if folded:
        g_bs, g_im = (ts, n), (lambda j, i, ss: (ss, j))
    else:
        g_bs, g_im = (None, ts, n), (lambda j, i, ss: (j, ss, 0))
    return _mm(name, a, g, (ts, tkw), lambda j, i, ss: (ss, i), g_bs, g_im,
               (nj, kw, n), out_dtype, (None, tkw, n), lambda j, i, ss: (j, i, 0), grid, ((0,), (0,)))


def mm_tn_plain(name, a, g, out_dtype):
    s, kw = a.shape
    n = g.shape[1]
    ts, tkw, tn = _pick(s, TK_CANDS), _pick(kw, TM_CANDS), _pick(n, TN_CANDS)
    grid = (kw // tkw, n // tn, s // ts)
    return _mm(name, a, g, (ts, tkw), lambda i, jn, ss: (ss, i), (ts, tn), lambda i, jn, ss: (ss, jn),
               (kw, n), out_dtype, (tkw, tn), lambda i, jn, ss: (i, jn), grid, ((0,), (0,)))


def mm_tn_j(name, aj, g, out_dtype):
    nj, s, ka = aj.shape
    n = g.shape[1]
    ts, tn = _pick(s, TK_CANDS), _pick(n, TN_CANDS)
    grid = (nj, n // tn, s // ts)
    return _mm(name, aj, g, (None, ts, ka), lambda j, jn, ss: (j, ss, 0), (ts, tn), lambda j, jn, ss: (ss, jn),
               (nj, ka, n), out_dtype, (None, ka, tn), lambda j, jn, ss: (j, 0, jn), grid, ((0,), (0,)))


def mm_nt_plain(name, g, w, out_dtype):
    m, n = g.shape
    k = w.shape[0]
    tm, tn, tkk = _pick(m, TM_CANDS), _pick(n, TK_CANDS), _pick(k, TN_CANDS)
    grid = (k // tkk, m // tm, n // tn)
    return _mm(name, g, w, (tm, tn), lambda kk, i, jn: (i, jn), (tkk, tn), lambda kk, i, jn: (kk, jn),
               (m, k), out_dtype, (tm, tkk), lambda kk, i, jn: (i, kk), grid, ((1,), (1,)))


def mm_nt_j(name, g, wg, out_dtype):
    m, n = g.shape
    nj, ka, _ = wg.shape
    tm, tn = _pick(m, TM_CANDS), _pick(n, TK_CANDS)
    grid = (nj, m // tm, n // tn)
    return _mm(name, g, wg, (tm, tn), lambda j, i, jn: (i, jn), (None, ka, tn), lambda j, i, jn: (j, 0, jn),
               (nj, m, ka), out_dtype, (None, tm, ka), lambda j, i, jn: (j, i, 0), grid, ((1,), (1,)))


def mm_nt_jsum(name, g, wg, out_dtype, folded, res=None):
    nj, k, n = wg.shape
    m = g.shape[0] if folded else g.shape[1]
    tm, tkk = _pick(m, TM_CANDS), _pick(k, TN_CANDS)
    grid = (m // tm, k // tkk, nj)
    if folded:
        g_bs, g_im = (tm, n), (lambda i, kk, j: (i, j))
    else:
        g_bs, g_im = (None, tm, n), (lambda i, kk, j: (j, i, 0))
    return _mm(name, g, wg, g_bs, g_im, (None, tkk, n), lambda i, kk, j: (j, kk, 0),
               (m, k), out_dtype, (tm, tkk), lambda i, kk, j: (i, kk), grid, ((1,), (1,)),
               res, (tm, tkk), lambda i, kk, j: (i, kk))


ROW_TILE_CANDS = (256, 128, 64, 32, 16, 8)


def _rstd(x):
    return lax.rsqrt(jnp.mean(x * x, axis=-1, keepdims=True) + RMS_EPS)


def _sigmoid(t):
    return 1.0 / (1.0 + jnp.exp(-t))


def rms_fwd(name, x, gain):
    s, d = x.shape
    ts = _pick(s, ROW_TILE_CANDS)

    def body(x_ref, g_ref, h_ref):
        xv = x_ref[...]
        h_ref[...] = ((xv * _rstd(xv)) * g_ref[...]).astype(h_ref.dtype)

    return pl.pallas_call(
        body, name=name, grid=(s // ts,),
        in_specs=[pl.BlockSpec((ts, d), lambda i: (i, 0)), pl.BlockSpec((1, d), lambda i: (0, 0))],
        out_specs=pl.BlockSpec((ts, d), lambda i: (i, 0)),
        out_shape=jax.ShapeDtypeStruct((s, d), BF16),
        compiler_params=_params(("parallel",), 3 * ts * d * 4))(_in_hbm(x), gain)


def rms_bwd(name, x, gain, dh, dres):
    s, d = x.shape
    ts = _pick(s, ROW_TILE_CANDS)

    def body(x_ref, g_ref, dh_ref, dr_ref, dx_ref, dg_ref):
        xv = x_ref[...]
        r = _rstd(xv)
        xhat = xv * r
        dhv = dh_ref[...].astype(F32)
        dxhat = dhv * g_ref[...]
        dx_ref[...] = dr_ref[...] + r * (dxhat - xhat * jnp.mean(dxhat * xhat, axis=-1, keepdims=True))
        part = jnp.sum(dhv * xhat, axis=0, keepdims=True)

        @pl.when(pl.program_id(0) == 0)
        def _():
            dg_ref[...] = part

        @pl.when(pl.program_id(0) > 0)
        def _():
            dg_ref[...] += part

    row = pl.BlockSpec((ts, d), lambda i: (i, 0))
    vec = pl.BlockSpec((1, d), lambda i: (0, 0))
    return pl.pallas_call(
        body, name=name, grid=(s // ts,), in_specs=[row, vec, row, row], out_specs=[row, vec],
        out_shape=[jax.ShapeDtypeStruct((s, d), F32), jax.ShapeDtypeStruct((1, d), F32)],
        compiler_params=_params(("arbitrary",), 6 * ts * d * 4))(_in_hbm(x), gain, _in_hbm(dh), _in_hbm(dres))


def gate_merge_fwd(name, proj, ta, tb, d):
    s = proj.shape[0]
    ts = _pick(s, ROW_TILE_CANDS)
    cb = COL_GATES // d

    def body(ga_ref, gb_ref, ta_ref, tb_ref, o_ref):
        o_ref[...] = (_sigmoid(ga_ref[...]) * ta_ref[...] + _sigmoid(gb_ref[...]) * tb_ref[...]).astype(o_ref.dtype)

    row = pl.BlockSpec((ts, d), lambda i: (i, 0))
    return pl.pallas_call(
        body, name=name, grid=(s // ts,),
        in_specs=[pl.BlockSpec((ts, d), lambda i: (i, cb)), pl.BlockSpec((ts, d), lambda i: (i, cb + 1)), row, row],
        out_specs=row, out_shape=jax.ShapeDtypeStruct((s, d), BF16),
        compiler_params=_params(("parallel",), 5 * ts * d * 4))(*[_in_hbm(a) for a in (proj, proj, ta, tb)])


def gate_merge_bwd(name, dmerged, proj, ta, tb, d):
    s = proj.shape[0]
    ts = _pick(s, ROW_TILE_CANDS)
    cb = COL_GATES // d

    def body(dm_ref, ga_ref, gb_ref, ta_ref, tb_ref, dta_ref, dtb_ref, dga_ref, dgb_ref):
        dm = dm_ref[...]
        sa = _sigmoid(ga_ref[...])
        sb = _sigmoid(gb_ref[...])
        dta_ref[...] = (dm * sa).astype(dta_ref.dtype)
        dtb_ref[...] = (dm * sb).astype(dtb_ref.dtype)
        dga_ref[...] = (dm * ta_ref[...] * (sa * (1.0 - sa))).astype(dga_ref.dtype)
        dgb_ref[...] = (dm * tb_ref[...] * (sb * (1.0 - sb))).astype(dgb_ref.dtype)

    row = pl.BlockSpec((ts, d), lambda i: (i, 0))
    out = jax.ShapeDtypeStruct((s, d), BF16)
    return pl.pallas_call(
        body, name=name, grid=(s // ts,),
        in_specs=[row, pl.BlockSpec((ts, d), lambda i: (i, cb)), pl.BlockSpec((ts, d), lambda i: (i, cb + 1)), row, row],
        out_specs=[row, row, row, row], out_shape=[out, out, out, out],
        compiler_params=_params(("parallel",), 8 * ts * d * 4))(*[_in_hbm(a) for a in (dmerged, proj, proj, ta, tb)])


def tail_fwd_bwd(name, x2, lp, pp, gain, target):
    s, d = x2.shape
    ts = _pick(s, ROW_TILE_CANDS)

    def body(x2_ref, lp_ref, pp_ref, g_ref, t_ref, loss_ref, dx3_ref, dlp_ref, dpp_ref, dg_ref):
        gp = _sigmoid(lp_ref[...])
        ppv = pp_ref[...]
        x3 = x2_ref[...] + gp * ppv
        r = _rstd(x3)
        xhat = x3 * r
        gv = g_ref[...]
        err = xhat * gv - t_ref[...]
        loss = jnp.sum(err * err) * (0.5 / d)
        dy = err * (1.0 / d)
        dxhat = dy * gv
        dx3 = r * (dxhat - xhat * jnp.mean(dxhat * xhat, axis=-1, keepdims=True))
        dx3_ref[...] = dx3
        dlp_ref[...] = (dx3 * ppv * (gp * (1.0 - gp))).astype(dlp_ref.dtype)
        dpp_ref[...] = (dx3 * gp).astype(dpp_ref.dtype)
        part = jnp.sum(dy * xhat, axis=0, keepdims=True)
        lossv = jnp.full((1, LANES), loss, F32)

        @pl.when(pl.program_id(0) == 0)
        def _():
            dg_ref[...] = part
            loss_ref[...] = lossv

        @pl.when(pl.program_id(0) > 0)
        def _():
            dg_ref[...] += part
            loss_ref[...] += lossv

    row = pl.BlockSpec((ts, d), lambda i: (i, 0))
    vec = pl.BlockSpec((1, d), lambda i: (0, 0))
    return pl.pallas_call(
        body, name=name, grid=(s // ts,), in_specs=[row, row, row, vec, row],
        out_specs=[pl.BlockSpec((1, LANES), lambda i: (0, 0)), row, row, row, vec],
        out_shape=[jax.ShapeDtypeStruct((1, LANES), F32), jax.ShapeDtypeStruct((s, d), F32),
                   jax.ShapeDtypeStruct((s, d), BF16), jax.ShapeDtypeStruct((s, d), BF16),
                   jax.ShapeDtypeStruct((1, d), F32)],
        compiler_params=_params(("arbitrary",), 9 * ts * d * 4))(_in_hbm(x2), _in_hbm(lp), _in_hbm(pp), gain, _in_hbm(target))


HALO = SUBLANES


def _shift_rows(cur, prev_row, next_row):
    ts = cur.shape[0]
    rid = lax.broadcasted_iota(jnp.int32, cur.shape, 0)
    down = jnp.where(rid == 0, prev_row, pltpu.roll(cur, 1, 0))
    up = jnp.where(rid == ts - 1, next_row, pltpu.roll(cur, ts - 1, 0))
    return down, up


def _halo_specs(ts, s, nf):
    nb = ts // HALO
    last = s // HALO - 1
    cur = pl.BlockSpec((None, ts, nf), lambda j, i: (j, i, 0))
    prev = pl.BlockSpec((None, HALO, nf), lambda j, i: (j, jnp.maximum(i * nb - 1, 0), 0))
    nxt = pl.BlockSpec((None, HALO, nf), lambda j, i: (j, jnp.minimum((i + 1) * nb, last), 0))
    return cur, prev, nxt


def _halo_rows(prev_ref, next_ref, n_tiles):
    i = pl.program_id(1)
    prev_row = jnp.where(i == 0, 0.0, prev_ref[HALO - 1:HALO, :].astype(F32))
    next_row = jnp.where(i == n_tiles - 1, 0.0, next_ref[0:1, :].astype(F32))
    return prev_row, next_row


def _gelu(g):
    t = jnp.tanh(GELU_C * (g + GELU_A * (g * g * g)))
    return 0.5 * g * (1.0 + t), t


def _conv(cur, down, up, cw_ref, cb_ref):
    return down * cw_ref[0:1, :] + cur * cw_ref[1:2, :] + up * cw_ref[2:3, :] + cb_ref[...]


def ffn_mid_fwd(name, gpre, u, cw, cb):
    nj, s, nf = gpre.shape
    ts = _pick(s, (512, 256, 128, 64, 32, 16, 8))
    n_tiles = s // ts
    cur, prev, nxt = _halo_specs(ts, s, nf)

    def body(g_ref, gp_ref, gn_ref, u_ref, cw_ref, cb_ref, z_ref):
        gv = g_ref[...]
        down, up = _shift_rows(gv, *_halo_rows(gp_ref, gn_ref, n_tiles))
        act, _ = _gelu(_conv(gv, down, up, cw_ref, cb_ref))
        z_ref[...] = (act * u_ref[...]).astype(z_ref.dtype)

    return pl.pallas_call(
        body, name=name, grid=(nj, n_tiles),
        in_specs=[cur, prev, nxt, cur, pl.BlockSpec((None, SUBLANES, nf), lambda j, i: (j, 0, 0)),
                  pl.BlockSpec((None, 1, nf), lambda j, i: (j, 0, 0))],
        out_specs=cur, out_shape=jax.ShapeDtypeStruct((nj, s, nf), BF16),
        compiler_params=_params(("parallel", "parallel"), 8 * ts * nf * 4))(*[_in_hbm(a) for a in (gpre, gpre, gpre, u)], cw, cb)


def ffn_mid_bwd1(name, gpre, u, dz, cw, cb):
    nj, s, nf = gpre.shape
    ts = _pick(s, (512, 256, 128, 64, 32, 16, 8))
    n_tiles = s // ts
    cur, prev, nxt = _halo_specs(ts, s, nf)

    def body(g_ref, gp_ref, gn_ref, u_ref, dz_ref, cw_ref, cb_ref, dg_ref, du_ref, dcw_ref):
        gv = g_ref[...]
        down, up = _shift_rows(gv, *_halo_rows(gp_ref, gn_ref, n_tiles))
        gc = _conv(gv, down, up, cw_ref, cb_ref)
        act, t = _gelu(gc)
        dzv = dz_ref[...].astype(F32)
        du_ref[...] = (dzv * act).astype(du_ref.dtype)
        dact = 0.5 * (1.0 + t) + 0.5 * gc * (1.0 - t * t) * (GELU_C * (1.0 + 3.0 * GELU_A * (gc * gc)))
        dg = dzv * u_ref[...] * dact
        dg_ref[...] = dg
        rows = [jnp.sum(dg * down, axis=0, keepdims=True), jnp.sum(dg * gv, axis=0, keepdims=True),
                jnp.sum(dg * up, axis=0, keepdims=True), jnp.sum(dg, axis=0, keepdims=True)]
        part = jnp.concatenate(rows + [jnp.zeros((SUBLANES - len(rows), nf), F32)], axis=0)

        @pl.when(pl.program_id(1) == 0)
        def _():
            dcw_ref[...] = part

        @pl.when(pl.program_id(1) > 0)
        def _():
            dcw_ref[...] += part

    small = pl.BlockSpec((None, SUBLANES, nf), lambda j, i: (j, 0, 0))
    return pl.pallas_call(
        body, name=name, grid=(nj, n_tiles),
        in_specs=[cur, prev, nxt, cur, cur, small, pl.BlockSpec((None, 1, nf), lambda j, i: (j, 0, 0))],
        out_specs=[cur, cur, small],
        out_shape=[jax.ShapeDtypeStruct((nj, s, nf), F32), jax.ShapeDtypeStruct((nj, s, nf), BF16),
                   jax.ShapeDtypeStruct((nj, SUBLANES, nf), F32)],
        compiler_params=_params(("parallel", "arbitrary"), 12 * ts * nf * 4))(*[_in_hbm(a) for a in (gpre, gpre, gpre, u, dz)], cw, cb)


def ffn_mid_bwd2(name, dg, cw):
    nj, s, nf = dg.shape
    ts = _pick(s, (512, 256, 128, 64, 32, 16, 8))
    n_tiles = s // ts
    cur, prev, nxt = _halo_specs(ts, s, nf)

    def body(g_ref, gp_ref, gn_ref, cw_ref, o_ref):
        gv = g_ref[...]
        down, up = _shift_rows(gv, *_halo_rows(gp_ref, gn_ref, n_tiles))
        o_ref[...] = (up * cw_ref[0:1, :] + gv * cw_ref[1:2, :] + down * cw_ref[2:3, :]).astype(o_ref.dtype)

    return pl.pallas_call(
        body, name=name, grid=(nj, n_tiles),
        in_specs=[cur, prev, nxt, pl.BlockSpec((None, SUBLANES, nf), lambda j, i: (j, 0, 0))],
        out_specs=cur, out_shape=jax.ShapeDtypeStruct((nj, s, nf), BF16),
        compiler_params=_params(("parallel", "parallel"), 6 * ts * nf * 4))(*[_in_hbm(a) for a in (dg, dg, dg)], cw)


def _t5_bucket(rel):
    half = N_BUCKETS // 2
    max_exact = half // 2
    n = jnp.abs(rel)
    side = jnp.where(rel > 0, half, 0)
    nf = jnp.maximum(n, 1).astype(F32)
    large = max_exact + (jnp.log(nf / max_exact) / math.log(MAX_DISTANCE / max_exact)
                         * (half - max_exact)).astype(jnp.int32)
    large = jnp.minimum(large, half - 1)
    return side + jnp.where(n < max_exact, n, large)


def bucket_tile(rows, half, dil):
    rel = (jnp.arange(rows + 2 * half)[None, :] - half) - jnp.arange(rows)[:, None]
    return _t5_bucket(rel * dil).astype(jnp.int32)


def bias_build(name, table_t, bucket, h0, nh, half):
    blk, kw = bucket.shape

    def body(t_ref, b_ref, o_ref):
        h = pl.program_id(0)
        bv = b_ref[...]
        acc = jnp.zeros((blk, kw), F32)
        for b in range(N_BUCKETS):
            acc = jnp.where(bv == b, t_ref[h0 + h, b], acc)
        qi = lax.broadcasted_iota(jnp.int32, (blk, kw), 0)
        ci = lax.broadcasted_iota(jnp.int32, (blk, kw), 1)
        o_ref[...] = jnp.where(jnp.abs(ci - half - qi) <= half, acc, NEG_INF)

    return pl.pallas_call(
        body, name=name, grid=(nh,),
        in_specs=[pl.BlockSpec(memory_space=pltpu.SMEM), pl.BlockSpec((blk, kw), lambda h: (0, 0))],
        out_specs=pl.BlockSpec((None, blk, kw), lambda h: (h, 0, 0)),
        out_shape=jax.ShapeDtypeStruct((nh, blk, kw), F32),
        compiler_params=_params(("parallel",), 4 * blk * kw * 4))(table_t, bucket)


def table_grad(name, dbias, bucket):
    nh, blk, kw = dbias.shape

    def body(d_ref, b_ref, o_ref):
        bv = b_ref[...]
        dv = d_ref[...]
        lane = lax.broadcasted_iota(jnp.int32, (SUBLANES, LANES), 1)
        acc = jnp.zeros((SUBLANES, LANES), F32)
        for b in range(N_BUCKETS):
            acc = jnp.where(lane == b, jnp.sum(jnp.where(bv == b, dv, 0.0)), acc)
        o_ref[...] = acc

    return pl.pallas_call(
        body, name=name, grid=(nh,),
        in_specs=[pl.BlockSpec((None, blk, kw), lambda h: (h, 0, 0)), pl.BlockSpec((blk, kw), lambda h: (0, 0))],
        out_specs=pl.BlockSpec((None, SUBLANES, LANES), lambda h: (h, 0, 0)),
        out_shape=jax.ShapeDtypeStruct((nh, SUBLANES, LANES), F32),
        compiler_params=_params(("parallel",), 4 * blk * kw * 4))(dbias, bucket)


class _Band:
    def __init__(self, s, half, q_rows, n_chains, dil):
        self.s, self.half, self.dil, self.n_chains = s, half, dil, n_chains
        self.seg = s // dil
        self.q_rows = min(q_rows, self.seg)
        self.win = self.q_rows + 2 * half
        self.pad = self.seg + 2 * half
        self.nsb = self.seg // self.q_rows
        self.n_items = dil * self.nsb
        assert self.n_items % n_chains == 0 and self.seg % self.q_rows == 0
        self.staged = dil > 1

    def rows_of(self, r):
        return pl.ds(r, self.seg, stride=self.dil) if self.dil > 1 else slice(None)

    def stage_kv(self, dst, src_ref):
        zeros = jnp.zeros((self.half, HEAD_DIM), dst.dtype)
        for r in range(self.dil):
            base = r * self.pad
            dst[base:base + self.half, :] = zeros
            dst[base + self.half + self.seg:base + self.pad, :] = zeros
            dst[base + self.half:base + self.half + self.seg, :] = src_ref[self.rows_of(r), :].astype(dst.dtype)

    def stage(self, dst, src_ref):
        for r in range(self.dil):
            dst[r * self.seg:(r + 1) * self.seg, :] = src_ref[self.rows_of(r), :].astype(dst.dtype)

    def unstage(self, dst_ref, src, add=False):
        for r in range(self.dil):
            val = src[r * self.seg:(r + 1) * self.seg, :].astype(dst_ref.dtype)
            if add:
                val = val + dst_ref[self.rows_of(r), :]
            dst_ref[self.rows_of(r), :] = val

    def offsets(self, item):
        r, sb = item // self.nsb, item % self.nsb
        qoff = pl.multiple_of(r * self.seg + sb * self.q_rows, self.q_rows)
        koff = pl.multiple_of(r * self.pad + sb * self.q_rows, B_BLOCK)
        kpos = sb * self.q_rows - self.half + lax.broadcasted_iota(jnp.int32, (1, self.win), 1)
        edge = jnp.where((kpos >= 0) & (kpos < self.seg), 0.0, NEG_INF)
        return qoff, koff, edge


def band_attn_fwd(name, proj, bias, sink, *, half, q_rows, n_chains, dil, nh, group, cq, ck, cv):
    s, w = proj.shape
    g = _Band(s, half, q_rows, n_chains, dil)
    has_sink = sink is not None

    def body(*refs):
        q_ref, k_ref, v_ref, b_ref = refs[:4]
        s_ref = refs[4] if has_sink else None
        o_ref, l_ref, ks, vs = refs[4 + has_sink:8 + has_sink]
        qs, os_, ls = refs[8 + has_sink:] if g.staged else (None, o_ref, l_ref)
        g.stage_kv(ks, k_ref)
        g.stage_kv(vs, v_ref)
        if g.staged:
            g.stage(qs, q_ref)
        bias_v = b_ref[...]
        sk = s_ref[pl.program_id(0)] if has_sink else None

        def chain(item):
            qoff, koff, edge = g.offsets(item)
            rows = pl.ds(qoff, g.q_rows)
            qv = qs[rows, :] if g.staged else q_ref[rows, :].astype(BF16)
            kw_ = ks[pl.ds(koff, g.win), :]
            vw_ = vs[pl.ds(koff, g.win), :]
            sc = lax.dot_general(qv, kw_, (((1,), (1,)), ((), ())), preferred_element_type=F32) * ATTN_SCALE
            sc = sc + bias_v + edge
            m = jnp.max(sc, axis=-1, keepdims=True)
            if has_sink:
                m = jnp.maximum(m, sk)
            p = jnp.exp(sc - m)
            den = jnp.sum(p, axis=-1, keepdims=True)
            if has_sink:
                den = den + jnp.exp(sk - m)
            out = lax.dot_general(p.astype(BF16), vw_, (((1,), (0,)), ((), ())), preferred_element_type=F32)
            return rows, out / den, jnp.broadcast_to(m + jnp.log(den), (g.q_rows, HEAD_DIM))

        def step(i, carry):
            for rows, out, lse in [chain(i * n_chains + u) for u in range(n_chains)]:
                os_[rows, :] = out
                ls[rows, :] = lse
            return carry

        lax.fori_loop(0, g.n_items // n_chains, step, 0)
        if g.staged:
            g.unstage(o_ref, os_)
            g.unstage(l_ref, ls)

    def col(c0, per):
        return pl.BlockSpec((s, HEAD_DIM), lambda h: (0, c0 // LANES + h // per))

    in_specs = [col(cq, 1), col(ck, group), col(cv, group),
                pl.BlockSpec((None, g.q_rows, g.win), lambda h: (h, 0, 0))]
    args = [proj, proj, proj, bias]
    if has_sink:
        in_specs.append(pl.BlockSpec(memory_space=pltpu.SMEM))
        args.append(sink)
    shape = jax.ShapeDtypeStruct((s, nh * HEAD_DIM), F32)
    scratch = [pltpu.VMEM((dil * g.pad, HEAD_DIM), BF16), pltpu.VMEM((dil * g.pad, HEAD_DIM), BF16)]
    if g.staged:
        scratch += [pltpu.VMEM((s, HEAD_DIM), BF16), pltpu.VMEM((s, HEAD_DIM), F32), pltpu.VMEM((s, HEAD_DIM), F32)]
    return pl.pallas_call(
        body, name=name, grid=(nh,), in_specs=in_specs, out_specs=[col(0, 1), col(0, 1)], out_shape=[shape, shape],
        scratch_shapes=scratch, compiler_params=_params(("parallel",), 16 * s * HEAD_DIM * 4))(*[_in_hbm(a) if a.ndim > 1 else a for a in args])


def band_attn_bwd(name, proj, bias, sink, dout, out, lse, dlse, *, half, q_rows, n_chains, dil, nh, group, cq, ck, cv):
    s, w = proj.shape
    g = _Band(s, half, q_rows, n_chains, dil)
    nkv = nh // group
    has_sink = sink is not None
    has_dl = dlse is not None
    n_in = 7 + int(has_sink) + int(has_dl)
    n_out = 4 + int(has_sink)

    def body(*refs):
        ins, outs, scr = refs[:n_in], refs[n_in:n_in + n_out], refs[n_in + n_out:]
        q_ref, k_ref, v_ref, b_ref, do_ref, o_ref, l_ref = ins[:7]
        s_ref = ins[7] if has_sink else None
        dl_ref = ins[n_in - 1] if has_dl else None
        dq_ref, dk_ref, dv_ref, db_ref = outs[:4]
        ks, vs, dks, dvs = scr[:4]
        scr = list(scr[4:])
        dsa = scr.pop(0) if has_sink else None
        if g.staged:
            qs, dos, os_, ls, dqs = scr[:5]
            dls = scr[5] if has_dl else None
            g.stage(qs, q_ref)
            g.stage(dos, do_ref)
            g.stage(os_, o_ref)
            g.stage(ls, l_ref)
            if has_dl:
                g.stage(dls, dl_ref)
        else:
            qs, dos, os_, ls, dqs, dls = None, do_ref, o_ref, l_ref, dq_ref, dl_ref
        h = pl.program_id(0)
        g.stage_kv(ks, k_ref)
        g.stage_kv(vs, v_ref)
        dks[...] = jnp.zeros_like(dks)
        dvs[...] = jnp.zeros_like(dvs)
        db_ref[...] = jnp.zeros_like(db_ref)
        bias_v = b_ref[...]
        if has_sink:
            sk = s_ref[h]
            dsa[...] = jnp.zeros_like(dsa)

        def chain(item):
            qoff, koff, edge = g.offsets(item)
            rows = pl.ds(qoff, g.q_rows)
            win = pl.ds(koff, g.win)
            qv = qs[rows, :] if g.staged else q_ref[rows, :].astype(BF16)
            kw_ = ks[win, :]
            vw_ = vs[win, :]
            sc = lax.dot_general(qv, kw_, (((1,), (1,)), ((), ())), preferred_element_type=F32) * ATTN_SCALE
            lv = ls[rows, :][:, 0:1]
            p = jnp.exp(sc + bias_v + edge - lv)
            dov = dos[rows, :]
            delta = jnp.sum(dov * os_[rows, :], axis=-1, keepdims=True)
            dob = dov.astype(BF16)
            dp = lax.dot_general(dob, vw_, (((1,), (1,)), ((), ())), preferred_element_type=F32)
            t = dp - delta
            if has_dl:
                t = t + dls[rows, :][:, 0:1]
            ds = p * t
            dsb = (ds * ATTN_SCALE).astype(BF16)
            dq = lax.dot_general(dsb, kw_, (((1,), (0,)), ((), ())), preferred_element_type=F32)
            dkc = lax.dot_general(dsb, qv, (((0,), (0,)), ((), ())), preferred_element_type=F32)
            dvc = lax.dot_general(p.astype(BF16), dob, (((0,), (0,)), ((), ())), preferred_element_type=F32)
            dsk = jnp.exp(sk - lv) * delta if has_sink else None
            return rows, win, dq, dkc, dvc, ds, dsk

        def step(i, carry):
            res = [chain(i * n_chains + u) for u in range(n_chains)]
            ds_sum = res[0][5]
            for rr in res[1:]:
                ds_sum = ds_sum + rr[5]
            db_ref[...] += ds_sum
            for rows, win, dq, dkc, dvc, ds, dsk in res:
                dqs[rows, :] = dq
                dks[win, :] += dkc
                dvs[win, :] += dvc
                if has_sink:
                    dsa[...] += dsk
            return carry

        lax.fori_loop(0, g.n_items // n_chains, step, 0)

        if g.staged:
            g.unstage(dq_ref, dqs)

        def emit_kv(add):
            for r in range(dil):
                lo = r * g.pad + half
                for dst_ref, src in ((dk_ref, dks), (dv_ref, dvs)):
                    val = src[lo:lo + g.seg, :]
                    if add:
                        val = val + dst_ref[g.rows_of(r), :]
                    dst_ref[g.rows_of(r), :] = val

        if group == 1:
            emit_kv(False)
        else:
            @pl.when(h % group == 0)
            def _():
                emit_kv(False)

            @pl.when(h % group != 0)
            def _():
                emit_kv(True)
        if has_sink:
            outs[4][...] = jnp.full((SUBLANES, LANES), -jnp.sum(dsa[...]), F32)

    def col(c0, per):
        return pl.BlockSpec((s, HEAD_DIM), lambda h: (0, c0 // LANES + h // per))

    b_spec = pl.BlockSpec((None, g.q_rows, g.win), lambda h: (h, 0, 0))
    in_specs = [col(cq, 1), col(ck, group), col(cv, group), b_spec, col(0, 1), col(0, 1), col(0, 1)]
    args = [proj, proj, proj, bias, dout, out, lse]
    if has_sink:
        in_specs.append(pl.BlockSpec(memory_space=pltpu.SMEM))
        args.append(sink)
    if has_dl:
        in_specs.append(col(0, 1))
        args.append(dlse)
    out_specs = [col(0, 1), col(0, group), col(0, group), b_spec]
    out_shape = [jax.ShapeDtypeStruct((s, nh * HEAD_DIM), F32), jax.ShapeDtypeStruct((s, nkv * HEAD_DIM), F32),
                 jax.ShapeDtypeStruct((s, nkv * HEAD_DIM), F32), jax.ShapeDtypeStruct((nh, g.q_rows, g.win), F32)]
    scratch = [pltpu.VMEM((dil * g.pad, HEAD_DIM), BF16), pltpu.VMEM((dil * g.pad, HEAD_DIM), BF16),
               pltpu.VMEM((dil * g.pad, HEAD_DIM), F32), pltpu.VMEM((dil * g.pad, HEAD_DIM), F32)]
    if has_sink:
        out_specs.append(pl.BlockSpec((None, SUBLANES, LANES), lambda h: (h, 0, 0)))
        out_shape.append(jax.ShapeDtypeStruct((nh, SUBLANES, LANES), F32))
        scratch.append(pltpu.VMEM((g.q_rows, 1), F32))
    if g.staged:
        scratch += [pltpu.VMEM((s, HEAD_DIM), BF16)] + [pltpu.VMEM((s, HEAD_DIM), F32)] * (4 + int(has_dl))
    res = pl.pallas_call(
        body, name=name, grid=(nh,), in_specs=in_specs, out_specs=out_specs, out_shape=out_shape,
        scratch_shapes=scratch, compiler_params=_params(("arbitrary",), 28 * s * HEAD_DIM * 4))(*[_in_hbm(a) if a.ndim > 1 else a for a in args])
    return res[0], res[1], res[2], res[3], (res[4] if has_sink else None)


def dil_merge_fwd(name, outs, lses):
    s, w = outs[0].shape
    ts = _pick(s, ROW_TILE_CANDS)
    ng = len(outs)

    def body(*refs):
        o_refs, l_refs, y_ref = refs[:ng], refs[ng:2 * ng], refs[2 * ng]
        ls = [l[...] for l in l_refs]
        mx = ls[0]
        for l in ls[1:]:
            mx = jnp.maximum(mx, l)
        es = [jnp.exp(l - mx) for l in ls]
        tot = es[0]
        for e in es[1:]:
            tot = tot + e
        acc = (es[0] / tot) * o_refs[0][...]
        for e, o in zip(es[1:], o_refs[1:]):
            acc = acc + (e / tot) * o[...]
        y_ref[...] = acc.astype(y_ref.dtype)

    row = pl.BlockSpec((ts, w), lambda i: (i, 0))
    return pl.pallas_call(
        body, name=name, grid=(s // ts,), in_specs=[row] * (2 * ng), out_specs=row,
        out_shape=jax.ShapeDtypeStruct((s, w), BF16),
        compiler_params=_params(("parallel",), 10 * ts * w * 4))(*[_in_hbm(a) for a in (*outs, *lses)])


def dil_merge_bwd(name, dy, outs, lses):
    s, w = outs[0].shape
    ts = _pick(s, ROW_TILE_CANDS)
    ng = len(outs)
    nhead = w // HEAD_DIM

    def body(*refs):
        dy_ref = refs[0]
        o_refs, l_refs = refs[1:1 + ng], refs[1 + ng:1 + 2 * ng]
        do_refs, dl_refs = refs[1 + 2 * ng:1 + 3 * ng], refs[1 + 3 * ng:1 + 4 * ng]
        for hh in range(nhead):
            cols = slice(hh * HEAD_DIM, (hh + 1) * HEAD_DIM)
            dyv = dy_ref[:, cols]
            ls = [l[:, cols] for l in l_refs]
            mx = ls[0]
            for l in ls[1:]:
                mx = jnp.maximum(mx, l)
            es = [jnp.exp(l - mx) for l in ls]
            tot = es[0]
            for e in es[1:]:
                tot = tot + e
            alphas = [e / tot for e in es]
            dal = [jnp.broadcast_to(jnp.sum(dyv * o[:, cols], axis=-1, keepdims=True), dyv.shape) for o in o_refs]
            mean = alphas[0] * dal[0]
            for a, d in zip(alphas[1:], dal[1:]):
                mean = mean + a * d
            for g in range(ng):
                do_refs[g][:, cols] = alphas[g] * dyv
                dl_refs[g][:, cols] = alphas[g] * (dal[g] - mean)

    row = pl.BlockSpec((ts, w), lambda i: (i, 0))
    shape = jax.ShapeDtypeStruct((s, w), F32)
    res = pl.pallas_call(
        body, name=name, grid=(s // ts,), in_specs=[row] * (1 + 2 * ng), out_specs=[row] * (2 * ng),
        out_shape=[shape] * (2 * ng),
        compiler_params=_params(("parallel",), 16 * ts * w * 4))(*[_in_hbm(a) for a in (dy, *outs, *lses)])
    return res[:ng], res[ng:]


def _adamw(w, g, m, v):
    m = ADAM_B1 * m + (1.0 - ADAM_B1) * g
    v = ADAM_B2 * v + (1.0 - ADAM_B2) * (g * g)
    m_hat = m / (1.0 - ADAM_B1 ** ADAM_STEP)
    v_hat = v / (1.0 - ADAM_B2 ** ADAM_STEP)
    delta = -ADAM_LR * (m_hat / (jnp.sqrt(v_hat) + ADAM_EPS) + ADAM_WD * w)
    return delta, m, v


def _row_tile(r, c, budget=1 << 20):
    if r * c * 4 <= budget or r % SUBLANES:
        return r
    for t in (1024, 512, 256, 128, 64, 32, 16, 8):
        if r % t == 0 and t * c * 4 <= budget:
            return t
    return SUBLANES


def adam_small(name, g, w, m, v):
    def body(g_ref, w_ref, m_ref, v_ref, d_ref, nm_ref, nv_ref):
        d_ref[...], nm_ref[...], nv_ref[...] = _adamw(w_ref[...], g_ref[...], m_ref[...], v_ref[...])

    shape = jax.ShapeDtypeStruct(w.shape, F32)
    return pl.pallas_call(body, name=name, out_shape=[shape, shape, shape])(g, w, m, v)


def reduce_adam(name, mine, theirs, w, m, v):
    nq, r, c = mine.shape
    tr = _row_tile(r, c)

    def body(*refs):
        parts, (w_ref, m_ref, v_ref, g_ref, d_ref, nm_ref, nv_ref) = refs[:nq], refs[nq:]
        g = parts[0][...].astype(F32)
        for p_ref in parts[1:]:
            g = g + p_ref[...].astype(F32)
        g_ref[...] = g
        d_ref[...], nm_ref[...], nv_ref[...] = _adamw(w_ref[...], g, m_ref[...], v_ref[...])

    def slot(q):
        return pl.BlockSpec((None, tr, c), lambda i: (q, i, 0))

    row = pl.BlockSpec((tr, c), lambda i: (i, 0))
    shape = jax.ShapeDtypeStruct((r, c), F32)
    return pl.pallas_call(
        body, name=name, grid=(r // tr,), in_specs=[slot(q) for q in range(nq)] + [row, row, row],
        out_specs=[row] * 4, out_shape=[shape] * 4,
        compiler_params=_params(("parallel",), (nq * 2 + 7 * 4) * tr * c))(*[_in_hbm(a) for a in (mine, *[theirs] * (nq - 1), w, m, v)])


def _place():
    return lax.axis_index("x"), lax.axis_index("y"), lax.axis_index("c")


def _flip(pos, bits):
    return tuple((1 - p) if b else p for p, b in zip(pos, bits))


def _index(pos):
    return 4 * pos[0] + 2 * pos[1] + pos[2]


ANY = pl.BlockSpec(memory_space=pl.ANY)


HBM = pl.BlockSpec(memory_space=pltpu.HBM)
SEM = pl.BlockSpec(memory_space=pltpu.SEMAPHORE)
EFFECT = pltpu.SideEffectType.DATAFLOW_SIDE_EFFECTING
TO_SIBLING = (0, 0, 1)
TO_CHIPS = [(1, 0, 0), (0, 1, 0), (1, 1, 0)]


def _in_hbm(a):
    return pltpu.with_memory_space_constraint(a, pltpu.HBM)


def _token_value(token):
    return token[0, 0]


def split_start(name, srcs, lands, plan, after):
    n = len(srcs)
    n_copies = len(plan((0, 0, 0)))

    def body(*refs):
        ins, lnd = refs[:n], refs[n:2 * n]
        send_sems, recv_sems = refs[2 * n + 1], refs[2 * n + 2]
        token = refs[-1]
        me = _place()
        for k, (a, sblk, lblk, rel) in enumerate(plan(me)):
            src = ins[a] if sblk is None else ins[a].at[sblk]
            pltpu.make_async_remote_copy(
                src_ref=src, dst_ref=lnd[a].at[lblk], send_sem=send_sems.at[k], recv_sem=recv_sems.at[k],
                device_id=_flip(me, rel), device_id_type=MESH).start()
        token[...] = jnp.zeros_like(token)

    outs = pl.pallas_call(
        body, name=name,
        out_shape=(pltpu.SemaphoreType.DMA((n_copies,)), pltpu.SemaphoreType.DMA((n_copies,)),
                   *[pltpu.HBM(a.shape, a.dtype) for a in srcs], *[pltpu.HBM(a.shape, a.dtype) for a in lands],
                   jax.ShapeDtypeStruct((SUBLANES, LANES), F32)),
        in_specs=[HBM] * (2 * n) + [ANY],
        out_specs=(SEM, SEM, *[HBM] * (2 * n), pl.BlockSpec(memory_space=pltpu.VMEM)),
        input_output_aliases={i: 2 + i for i in range(2 * n)},
        compiler_params=pltpu.CompilerParams(has_side_effects=EFFECT),
    )(*[_in_hbm(a) for a in srcs], *[_in_hbm(a) for a in lands], after)
    return outs[0], outs[1], list(outs[2:2 + n]), list(outs[2 + n:2 + 2 * n]), outs[-1]


def split_wait(name, send_sems, recv_sems, srcs, lands, plan, after):
    n = len(srcs)

    def body(*refs):
        ins, lnd = refs[:n], refs[n:2 * n]
        s_sems, r_sems = refs[2 * n], refs[2 * n + 1]
        me = _place()
        for k, (a, sblk, lblk, rel) in enumerate(plan(me)):
            src = ins[a] if sblk is None else ins[a].at[sblk]
            cp = pltpu.make_async_remote_copy(
                src_ref=src, dst_ref=lnd[a].at[lblk], send_sem=s_sems.at[k], recv_sem=r_sems.at[k],
                device_id=_flip(me, rel), device_id_type=MESH)
            cp.wait_send()
            cp.wait_recv()

    outs = pl.pallas_call(
        body, name=name,
        out_shape=(*[pltpu.HBM(a.shape, a.dtype) for a in srcs], *[pltpu.HBM(a.shape, a.dtype) for a in lands]),
        in_specs=[HBM] * (2 * n) + [SEM, SEM, ANY],
        out_specs=tuple([HBM] * (2 * n)),
        input_output_aliases={i: i for i in range(2 * n)},
        compiler_params=pltpu.CompilerParams(has_side_effects=EFFECT),
    )(*srcs, *lands, send_sems, recv_sems, after)
    return list(outs[:n]), list(outs[n:])


def ag_plan(n):
    def plan(me):
        return [(a, None, _index(me), rel) for a in range(n) for rel in [TO_SIBLING] + TO_CHIPS]
    return plan


def ag_start(name, shards, after):
    lands = [lax.empty((N_DEV,) + tuple(sh.shape), sh.dtype) for sh in shards]
    return split_start(name, shards, lands, ag_plan(len(shards)), after)


def ag_finish(name, lands):
    n = len(lands)

    def body(*refs):
        lnd = refs[n:2 * n]
        send_sems, recv_sems = refs[2 * n:]
        me = _place()
        sibling = _flip(me, TO_SIBLING)
        copies = []
        for a in range(n):
            for j, rel in enumerate(TO_CHIPS):
                blk = lnd[a].at[_index(_flip(me, rel))]
                there = lnd[a].at[_index(_flip(sibling, rel))]
                cp = pltpu.make_async_remote_copy(
                    src_ref=blk, dst_ref=blk, send_sem=send_sems.at[a * 3 + j], recv_sem=recv_sems.at[a * 3 + j],
                    device_id=sibling, device_id_type=MESH)
                cp.start()
                copies.append((cp, pltpu.make_async_remote_copy(
                    src_ref=blk, dst_ref=there, send_sem=send_sems.at[a * 3 + j], recv_sem=recv_sems.at[a * 3 + j],
                    device_id=sibling, device_id_type=MESH)))
        for cp, arrival in copies:
            arrival.wait_recv()
        for cp, arrival in copies:
            cp.wait_send()

    return pl.pallas_call(
        body, name=name, in_specs=[ANY] * n, out_specs=[ANY] * n,
        out_shape=[jax.ShapeDtypeStruct(l.shape, l.dtype) for l in lands],
        input_output_aliases={a: a for a in range(n)},
        scratch_shapes=[pltpu.SemaphoreType.DMA((3 * n,)), pltpu.SemaphoreType.DMA((3 * n,))],
    )(*lands)


REL = [(b >> 2 & 1, b >> 1 & 1, b & 1) for b in range(N_DEV)]


CHIP_REL = [(0, 0, 0)] + TO_CHIPS
N_CHIPS = len(CHIP_REL)


def rs_pair(name, parts):
    n = len(parts)

    def body(*refs):
        ins, got = refs[:n], refs[n:2 * n]
        send_sems, recv_sems = refs[2 * n:]
        me = _place()
        sibling = _flip(me, TO_SIBLING)
        remote = []
        for a in range(n):
            for q, rel in enumerate(CHIP_REL):
                k = a * N_CHIPS + q
                cp = pltpu.make_async_remote_copy(
                    src_ref=ins[a].at[_index(_flip(sibling, rel))], dst_ref=got[a].at[q], send_sem=send_sems.at[k],
                    recv_sem=recv_sems.at[k], device_id=sibling, device_id_type=MESH)
                cp.start()
                remote.append(cp)
        for cp in remote:
            cp.wait_recv()
        for cp in remote:
            cp.wait_send()

    shapes = [jax.ShapeDtypeStruct((N_CHIPS,) + tuple(p.shape[1:]), p.dtype) for p in parts]
    res = pl.pallas_call(
        body, name=name, in_specs=[ANY] * n, out_specs=[ANY] * n, out_shape=shapes,
        scratch_shapes=[pltpu.SemaphoreType.DMA((N_CHIPS * n,)), pltpu.SemaphoreType.DMA((N_CHIPS * n,))],
    )(*parts)
    return list(res)


def own_blocks():
    me = _place()
    return jnp.stack([_index(_flip(me, rel)) for rel in CHIP_REL]).astype(jnp.int32)


def pair_add(name, blocks, parts, got):
    nq, r, c = got.shape
    tr = _row_tile(r, c)

    def body(blk_ref, a_ref, b_ref, o_ref):
        o_ref[...] = (a_ref[...].astype(F32) + b_ref[...].astype(F32)).astype(o_ref.dtype)

    spec = pl.BlockSpec((None, tr, c), lambda q, i, blk: (q, i, 0))
    return pl.pallas_call(
        body, name=name,
        grid_spec=pltpu.PrefetchScalarGridSpec(
            num_scalar_prefetch=1, grid=(nq, r // tr),
            in_specs=[pl.BlockSpec((None, tr, c), lambda q, i, blk: (blk[q], i, 0)), spec], out_specs=spec),
        out_shape=jax.ShapeDtypeStruct(got.shape, got.dtype),
        compiler_params=_params(("arbitrary", "arbitrary"), 6 * tr * c * 2))(blocks, _in_hbm(parts), _in_hbm(got))


def rs_plan(n):
    def plan(me):
        return [(a, q, q, CHIP_REL[q]) for a in range(n) for q in range(1, N_CHIPS)]
    return plan


def rs_start(name, sums, after):
    lands = [lax.empty(t.shape, t.dtype) for t in sums]
    return split_start(name, sums, lands, rs_plan(len(sums)), after)


def allreduce_small(name, pack, after):
    rows, lanes = pack.shape

    def body(x_ref, after_ref, o_ref, land, send_sems, recv_sems):
        me = _place()
        idx = _index(me)
        land[idx] = x_ref[...]
        copies = []
        for r in range(1, N_DEV):
            peer = _flip(me, REL[r])
            cp = pltpu.make_async_remote_copy(
                src_ref=x_ref, dst_ref=land.at[idx], send_sem=send_sems.at[r - 1], recv_sem=recv_sems.at[r - 1],
                device_id=peer, device_id_type=MESH)
            cp.start()
            copies.append(cp)
        for cp in copies:
            cp.wait_recv()
        for cp in copies:
            cp.wait_send()
        acc = land[0]
        for i in range(1, N_DEV):
            acc = acc + land[i]
        o_ref[...] = acc

    return pl.pallas_call(
        body, name=name, in_specs=[pl.BlockSpec(memory_space=pltpu.VMEM), ANY],
        out_specs=pl.BlockSpec(memory_space=pltpu.VMEM), out_shape=jax.ShapeDtypeStruct((rows, lanes), F32),
        scratch_shapes=[pltpu.VMEM((N_DEV, rows, lanes), F32), pltpu.SemaphoreType.DMA((7,)),
                        pltpu.SemaphoreType.DMA((7,))],
    )(pack, after)


def _pad_rows(a, rows):
    return jnp.pad(a, ((0, rows - a.shape[0]), (0, 0)))


def _as_tiles(vec):
    n = vec.shape[0]
    rows = -(-n // LANES)
    rows = -(-rows // SUBLANES) * SUBLANES
    return jnp.pad(vec, (0, rows * LANES - n)).reshape(rows, LANES)


def kernel(x, p, rel_bias_table, attn_norm, w_in, sink_a, w_branch_a, w_branch_b, w_out, ffn_norm, w_ffn_gate, w_ffn_up, conv_w, conv_b, w_ffn_down, ple_norm, w_ple_gate, w_ple_proj, final_norm, loss_target, m_rel_bias_table, m_attn_norm, m_w_in, m_sink_a, m_w_branch_a, m_w_branch_b, m_w_out, m_ffn_norm, m_w_ffn_gate, m_w_ffn_up, m_conv_w, m_conv_b, m_w_ffn_down, m_ple_norm, m_w_ple_gate, m_w_ple_proj, m_final_norm, v_rel_bias_table, v_attn_norm, v_w_in, v_sink_a, v_w_branch_a, v_w_branch_b, v_w_out, v_ffn_norm, v_w_ffn_gate, v_w_ffn_up, v_conv_w, v_conv_b, v_w_ffn_down, v_ple_norm, v_w_ple_gate, v_w_ple_proj, v_final_norm):
    xs = x[0]
    s, d = xs.shape
    ps = p[0, 0]
    target = loss_target[0]
    me = 4 * lax.axis_index("x") + 2 * lax.axis_index("y") + lax.axis_index("c")

    big = dict(w_in=w_in[0], w_branch_a=w_branch_a[0], w_branch_b=w_branch_b[0], w_out=w_out[0],
               w_ffn_gate=w_ffn_gate[0], w_ffn_up=w_ffn_up[0], w_ffn_down=w_ffn_down[0],
               w_ple_gate=w_ple_gate[0], w_ple_proj=w_ple_proj[0])
    big_m = dict(w_in=m_w_in[0], w_branch_a=m_w_branch_a[0], w_branch_b=m_w_branch_b[0], w_out=m_w_out[0],
                 w_ffn_gate=m_w_ffn_gate[0], w_ffn_up=m_w_ffn_up[0], w_ffn_down=m_w_ffn_down[0],
                 w_ple_gate=m_w_ple_gate[0], w_ple_proj=m_w_ple_proj[0])
    big_v = dict(w_in=v_w_in[0], w_branch_a=v_w_branch_a[0], w_branch_b=v_w_branch_b[0], w_out=v_w_out[0],
                 w_ffn_gate=v_w_ffn_gate[0], w_ffn_up=v_w_ffn_up[0], w_ffn_down=v_w_ffn_down[0],
                 w_ple_gate=v_w_ple_gate[0], w_ple_proj=v_w_ple_proj[0])
    names = list(big)
    nf = big["w_ffn_gate"].shape[1]

    shards = {k: big[k].astype(BF16) for k in names}
    shards["conv_w"] = _pad_rows(conv_w[0], SUBLANES)
    ag_groups = [["w_in"], ["w_branch_a", "w_branch_b", "w_out"], ["w_ffn_gate", "w_ffn_up", "conv_w"],
                 ["w_ffn_down", "w_ple_gate", "w_ple_proj"]]
    ag_started = {}
    wg = {}

    def start_gather(gi, after):
        s_sems, r_sems, srcs, lands, token = ag_start(f"ag_start{gi}", [shards[k] for k in ag_groups[gi]], after)
        ag_started[gi] = (s_sems, r_sems, srcs, lands)
        return _token_value(token)

    def gather(gi, after):
        s_sems, r_sems, srcs, lands = ag_started[gi]
        srcs, lands = split_wait(f"ag_wait{gi}", s_sems, r_sems, srcs, lands, ag_plan(len(srcs)), after)
        lands = ag_finish(f"ag_finish{gi}", lands)
        wg.update({k: lax.dynamic_update_index_in_dim(l, sh, me, 0) for k, l, sh in zip(ag_groups[gi], lands, srcs)})

    cb = conv_b.reshape(N_DEV, 1, nf)

    table_t = rel_bias_table.T
    geo_a = dict(half=A_BLOCK, q_rows=ATTN_Q_ROWS, n_chains=ATTN_CHAINS, dil=1, nh=A_Q_HEADS, group=A_GROUP,
                 cq=COL_QA, ck=COL_KA, cv=COL_VA)
    geo_b = [dict(half=B_BLOCK, q_rows=min(ATTN_Q_ROWS, s // dil), n_chains=ATTN_CHAINS, dil=dil,
                  nh=B_HEADS_PER_GROUP, group=1, cq=COL_QB + g * B_OUT_W, ck=COL_KB + g * B_OUT_W,
                  cv=COL_VB + g * B_OUT_W) for g, (_, dil) in enumerate(B_PATTERNS)]
    bucket_a = bucket_tile(geo_a["q_rows"], A_BLOCK, 1)
    bias_a = bias_build("bias_a", table_t, bucket_a, 0, A_Q_HEADS, A_BLOCK)
    buckets_b = [bucket_tile(gb["q_rows"], B_BLOCK, gb["dil"]) for gb in geo_b]
    biases_b = [bias_build(f"bias_b{g}", table_t, buckets_b[g], A_Q_HEADS + g * B_HEADS_PER_GROUP, B_HEADS_PER_GROUP,
                           B_BLOCK) for g in range(len(B_PATTERNS))]

    tok = start_gather(0, xs) + start_gather(1, xs)
    h = rms_fwd("rms_attn", xs, attn_norm + tok)
    gather(0, h)
    tok = start_gather(2, wg["w_in"])
    proj = mm_cols("proj_in", h, wg["w_in"], F32, fold=True)
    sink = sink_a[0] + tok
    ya, lse_a = band_attn_fwd("attn_a_fwd", proj, bias_a, sink, **geo_a)
    outs_b, lses_b = [], []
    for g in range(len(B_PATTERNS)):
        o, l = band_attn_fwd(f"attn_b{g}_fwd", proj, biases_b[g], None, **geo_b[g])
        outs_b.append(o)
        lses_b.append(l)
    yb = dil_merge_fwd("dil_merge_fwd", outs_b, lses_b)
    gather(1, yb)
    tok = start_gather(3, wg["w_out"])
    w_out_full = wg["w_out"].reshape(d, d)
    ta = mm_cols("branch_a", ya, wg["w_branch_a"], F32, fold=True)
    tb = mm_cols("branch_b", yb, wg["w_branch_b"], F32, fold=True)
    merged = gate_merge_fwd("gate_merge_fwd", proj, ta, tb, d)
    x1 = mm_plain("mix_out", merged, w_out_full, F32, res=xs)

    hf = rms_fwd("rms_ffn", x1, ffn_norm + tok)
    gather(2, hf)
    cw = wg["conv_w"]
    gpre = mm_cols("ffn_gate", hf, wg["w_ffn_gate"], F32, fold=False)
    u = mm_cols("ffn_up", hf, wg["w_ffn_up"], F32, fold=False)
    z = ffn_mid_fwd("ffn_mid_fwd", gpre, u, cw, cb)
    gather(3, z)
    w_pg_full = wg["w_ple_gate"].reshape(d, d)
    x2 = mm_jsum("ffn_down", z, wg["w_ffn_down"], F32, res=x1)

    hp = rms_fwd("rms_ple", x2, ple_norm)
    lp = mm_plain("ple_gate", hp, w_pg_full, F32)
    pp = mm_cols("ple_proj", ps, wg["w_ple_proj"], F32, fold=True)
    loss_part, dx3, dlp, dpp, d_final = tail_fwd_bwd("tail", x2, lp, pp, final_norm.reshape(1, d), target)

    grads = {}
    rs_started = []
    blocks = own_blocks()

    def scatter(tag, keys):
        got = rs_pair(f"rs_pair_{tag}", [grads[k] for k in keys])
        sums = [pair_add(f"pair_add_{k}", blocks, grads[k], g) for k, g in zip(keys, got)]
        s_sems, r_sems, srcs, lands, token = rs_start(f"rs_start_{tag}", sums, blocks)
        rs_started.append((tag, keys, s_sems, r_sems, srcs, lands))
        return _token_value(token)

    grads["w_ple_proj"] = mm_tn_cols("d_w_ple_proj", ps, dpp, N_DEV, big["w_ple_proj"].shape[1], BF16, folded=True)
    grads["w_ple_gate"] = mm_tn_plain("d_w_ple_gate", hp, dlp, BF16).reshape(N_DEV, d // N_DEV, d)
    tok = scatter("ple", ["w_ple_proj", "w_ple_gate"])
    dhp = mm_nt_plain("d_hp", dlp, w_pg_full, F32)
    dx2, d_ple = rms_bwd("rms_ple_bwd", x2, ple_norm + tok, dhp, dx3)

    dz = mm_nt_j("d_z", dx2, wg["w_ffn_down"], BF16)
    grads["w_ffn_down"] = mm_tn_j("d_w_ffn_down", z, dx2, BF16)
    tok = scatter("down", ["w_ffn_down"])
    dg, du, dcw = ffn_mid_bwd1("ffn_mid_bwd1", gpre, u, dz, cw, cb + tok)
    dgpre = ffn_mid_bwd2("ffn_mid_bwd2", dg, cw)
    grads["w_ffn_up"] = mm_tn_cols("d_w_ffn_up", hf, du, N_DEV, nf, BF16, folded=False)
    grads["w_ffn_gate"] = mm_tn_cols("d_w_ffn_gate", hf, dgpre, N_DEV, nf, BF16, folded=False)
    tok = scatter("upgate", ["w_ffn_up", "w_ffn_gate"])
    dhf = mm_nt_jsum("d_hf_up", du, wg["w_ffn_up"], F32, folded=False)
    dhf = mm_nt_jsum("d_hf_gate", dgpre, wg["w_ffn_gate"], F32, folded=False, res=dhf)
    dx1, d_ffn = rms_bwd("rms_ffn_bwd", x1, ffn_norm + tok, dhf, dx2)

    dmerged = mm_nt_plain("d_merged", dx1, w_out_full, F32)
    grads["w_out"] = mm_tn_plain("d_w_out", merged, dx1, BF16).reshape(N_DEV, d // N_DEV, d)
    dta, dtb, dga, dgb = gate_merge_bwd("gate_merge_bwd", dmerged, proj, ta, tb, d)
    grads["w_branch_a"] = mm_tn_cols("d_w_branch_a", ya, dta, N_DEV, big["w_branch_a"].shape[1], BF16, folded=True)
    grads["w_branch_b"] = mm_tn_cols("d_w_branch_b", yb, dtb, N_DEV, big["w_branch_b"].shape[1], BF16, folded=True)
    tok = scatter("mix", ["w_out", "w_branch_a", "w_branch_b"])
    dya = mm_nt_jsum("d_ya", dta, wg["w_branch_a"], F32, folded=True)
    dyb = mm_nt_jsum("d_yb", dtb, wg["w_branch_b"], F32, folded=True)
    dqa, dka, dva, dbias_a, dsink = band_attn_bwd("attn_a_bwd", proj, bias_a, sink + tok, dya, ya, lse_a, None, **geo_a)
    douts_b, dlses_b = dil_merge_bwd("dil_merge_bwd", dyb, outs_b, lses_b)
    dq_b, dk_b, dv_b, dbias_b = [], [], [], []
    for g in range(len(B_PATTERNS)):
        dq, dk, dv, db, _ = band_attn_bwd(f"attn_b{g}_bwd", proj, biases_b[g], None, douts_b[g], outs_b[g], lses_b[g],
                                          dlses_b[g], **geo_b[g])
        dq_b.append(dq)
        dk_b.append(dk)
        dv_b.append(dv)
        dbias_b.append(db)
    dproj = jnp.concatenate([t.astype(BF16) for t in [dqa, dka, dva] + dq_b + dk_b + dv_b + [dga, dgb]], axis=1)
    grads["w_in"] = mm_tn_cols("d_w_in", h, dproj, N_DEV, big["w_in"].shape[1], BF16, folded=True)
    tok = scatter("in", ["w_in"])
    dh = mm_nt_jsum("d_h", dproj, wg["w_in"], F32, folded=True)
    grad_x, d_attn = rms_bwd("rms_attn_bwd", xs, attn_norm + tok, dh, dx1)

    dt_a = table_grad("table_grad_a", dbias_a, bucket_a)[:, 0, :N_BUCKETS]
    dt_b = [table_grad(f"table_grad_b{g}", dbias_b[g], buckets_b[g])[:, 0, :N_BUCKETS] for g in range(len(B_PATTERNS))]
    d_table_part = jnp.concatenate([dt_a] + dt_b, axis=0).T

    pieces = [
        ("loss", loss_part[0, :1]),
        ("table", d_table_part.reshape(-1)),
        ("attn_norm", d_attn.reshape(-1)),
        ("sink", dsink[:, 0, 0]),
        ("ffn_norm", d_ffn.reshape(-1)),
        ("conv_w", dcw[:, 0:3, :].reshape(-1)),
        ("conv_b", dcw[:, 3, :].reshape(-1)),
        ("ple_norm", d_ple.reshape(-1)),
        ("final_norm", d_final.reshape(-1)),
    ]
    tiles = [_as_tiles(v) for _, v in pieces]
    pack = jnp.concatenate(tiles, axis=0)

    out_g, out_d, out_m, out_v = {}, {}, {}, {}

    def finish(group, after):
        tag, keys, s_sems, r_sems, srcs, lands = group
        srcs, lands = split_wait(f"rs_wait_{tag}", s_sems, r_sems, srcs, lands, rs_plan(len(keys)), after)
        for k, mine, theirs in zip(keys, srcs, lands):
            g, dl, nm, nv = reduce_adam("adam_" + k, mine, theirs, big[k], big_m[k], big_v[k])
            out_g[k], out_d[k], out_m[k], out_v[k] = g[None], dl[None], nm[None], nv[None]
            after = dl
        return after

    after = pack
    for group in rs_started[:-1]:
        after = finish(group, after)
    total = allreduce_small("allreduce_small", pack, after)
    finish(rs_started[-1], total)
    small = {}
    row = 0
    for (nm, v), t in zip(pieces, tiles):
        small[nm] = total[row:row + t.shape[0]].reshape(-1)[:v.shape[0]]
        row += t.shape[0]
    loss = small["loss"][0]
    g_small = dict(
        rel_bias_table=small["table"].reshape(rel_bias_table.shape),
        attn_norm=small["attn_norm"].reshape(attn_norm.shape),
        sink_a=small["sink"].reshape(sink_a.shape),
        ffn_norm=small["ffn_norm"].reshape(ffn_norm.shape),
        conv_w=lax.dynamic_index_in_dim(small["conv_w"].reshape(N_DEV, 3, nf), me, 0, keepdims=False)[None],
        conv_b=small["conv_b"].reshape(conv_b.shape),
        ple_norm=small["ple_norm"].reshape(ple_norm.shape),
        final_norm=small["final_norm"].reshape(1, d),
    )
    w_small = dict(rel_bias_table=(rel_bias_table, m_rel_bias_table, v_rel_bias_table),
                   attn_norm=(attn_norm, m_attn_norm, v_attn_norm), sink_a=(sink_a, m_sink_a, v_sink_a),
                   ffn_norm=(ffn_norm, m_ffn_norm, v_ffn_norm), conv_w=(conv_w, m_conv_w, v_conv_w),
                   conv_b=(conv_b, m_conv_b, v_conv_b), ple_norm=(ple_norm, m_ple_norm, v_ple_norm),
                   final_norm=(final_norm, m_final_norm, v_final_norm))

    for k, (wv, mv, vv) in w_small.items():
        shape = wv.shape
        two_d = (1, shape[0]) if len(shape) == 1 else ((shape[0] * shape[1], shape[2]) if len(shape) == 3 else shape)
        gk = g_small[k].reshape(two_d)
        dl, nm, nv = adam_small("adam_" + k, gk, wv.reshape(two_d), mv.reshape(two_d), vv.reshape(two_d))
        out_g[k], out_d[k], out_m[k], out_v[k] = gk.reshape(shape), dl.reshape(shape), nm.reshape(shape), nv.reshape(shape)

    order = ["rel_bias_table", "attn_norm", "w_in", "sink_a", "w_branch_a", "w_branch_b", "w_out", "ffn_norm",
             "w_ffn_gate", "w_ffn_up", "conv_w", "conv_b", "w_ffn_down", "ple_norm", "w_ple_gate", "w_ple_proj",
             "final_norm"]
    return (loss, grad_x[None], *[out_g[k] for k in order], *[out_d[k] for k in order],
            *[out_m[k] for k in order], *[out_v[k] for k in order])
```

```python
import math

import jax
import jax.numpy as jnp
from jax import lax
from jax.experimental import pallas as pl
from jax.experimental.pallas import tpu as pltpu

F32 = jnp.float32
BF16 = jnp.bfloat16
MESH = pl.DeviceIdType.MESH
N_DEV = 8

HEAD_DIM = 128
A_Q_HEADS = 8
A_KV_HEADS = 2
A_GROUP = A_Q_HEADS // A_KV_HEADS
A_BLOCK = 128
B_PATTERNS = ((128, 1), (512, 4), (2048, 16))
B_HEADS_PER_GROUP = 4
B_HEADS = len(B_PATTERNS) * B_HEADS_PER_GROUP
B_BLOCK = 64
N_BUCKETS = 32
MAX_DISTANCE = 1024
A_Q_W = A_Q_HEADS * HEAD_DIM
A_KV_W = A_KV_HEADS * HEAD_DIM
B_W = B_HEADS * HEAD_DIM
B_OUT_W = B_HEADS_PER_GROUP * HEAD_DIM
COL_QA = 0
COL_KA = COL_QA + A_Q_W
COL_VA = COL_KA + A_KV_W
COL_QB = COL_VA + A_KV_W
COL_KB = COL_QB + B_W
COL_VB = COL_KB + B_W
COL_GATES = COL_VB + B_W
RMS_EPS = 1e-6
NEG_INF = -1e30
ATTN_SCALE = HEAD_DIM ** -0.5
ATTN_Q_ROWS = 256
ATTN_CHAINS = 2

ADAM_LR = 0.001
ADAM_B1 = 0.9
ADAM_B2 = 0.999
ADAM_EPS = 1e-08
ADAM_WD = 0.01
ADAM_STEP = 10

GELU_C = math.sqrt(2.0 / math.pi)
GELU_A = 0.044715

V7X_VMEM_BYTES = 64 * 1024 * 1024
VMEM_CEILING = V7X_VMEM_BYTES - 8 * 1024 * 1024
LANES = 128
SUBLANES = 8


def _pick(n, cands):
    for c in cands:
        if n % c == 0:
            return c
    return n


def _nbytes(shape, dtype):
    n = 1
    for d in shape:
        if d is not None:
            n *= d
    return n * jnp.dtype(dtype).itemsize


def _params(sem, est_bytes):
    limit = int(min(VMEM_CEILING, max(32 * 1024 * 1024, 2 * est_bytes + (8 << 20))))
    return pltpu.CompilerParams(dimension_semantics=sem, vmem_limit_bytes=limit)


def _mm(name, a, b, a_bs, a_im, b_bs, b_im, out_shape, out_dtype, o_bs, o_im, grid, dims,
        res=None, r_bs=None, r_im=None):
    nk = grid[-1]
    nax = len(grid)
    has_res = res is not None
    o_tile = tuple(d for d in o_bs if d is not None)

    def body(*refs):
        if has_res:
            a_ref, b_ref, r_ref, o_ref = refs[:4]
            rest = refs[4:]
        else:
            a_ref, b_ref, o_ref = refs[:3]
            r_ref = None
            rest = refs[3:]

        def prod():
            return lax.dot_general(a_ref[...].astype(BF16), b_ref[...].astype(BF16), (dims, ((), ())),
                                   preferred_element_type=F32)

        def finish(r):
            if r_ref is not None:
                r = r + r_ref[...].astype(F32)
            o_ref[...] = r.astype(o_ref.dtype)

        if nk == 1:
            finish(prod())
        else:
            acc = rest[0]
            k = pl.program_id(nax - 1)

            @pl.when(k == 0)
            def _():
                acc[...] = prod()

            @pl.when(k > 0)
            def _():
                acc[...] += prod()

            @pl.when(k == nk - 1)
            def _():
                finish(acc[...])

    in_specs = [pl.BlockSpec(a_bs, a_im), pl.BlockSpec(b_bs, b_im)]
    args = [a, b]
    est = _nbytes(a_bs, a.dtype) + _nbytes(b_bs, b.dtype) + _nbytes(o_bs, out_dtype) + 2 * _nbytes(o_tile, F32)
    if has_res:
        in_specs.append(pl.BlockSpec(r_bs, r_im))
        args.append(res)
        est += _nbytes(r_bs, res.dtype)
    scratch = [] if nk == 1 else [pltpu.VMEM(o_tile, F32)]
    sem = ("parallel",) * (nax - 1) + ("arbitrary",)
    return pl.pallas_call(
        body, name=name, grid=grid, in_specs=in_specs, out_specs=pl.BlockSpec(o_bs, o_im),
        out_shape=jax.ShapeDtypeStruct(out_shape, out_dtype), scratch_shapes=scratch,
        compiler_params=_params(sem, est))(*args)


TM_CANDS = (1024, 512, 256, 128, 64, 32, 16, 8)
TK_CANDS = (1024, 512, 256, 128)
TN_CANDS = (1024, 512, 256, 128)


def mm_cols(name, a, wg, out_dtype, fold):
    m, k = a.shape
    nj, _, n = wg.shape
    tm, tk = _pick(m, TM_CANDS), _pick(k, TK_CANDS)
    grid = (nj, m // tm, k // tk)
    if fold:
        shape, o_bs, o_im = (m, nj * n), (tm, n), (lambda j, i, kk: (i, j))
    else:
        shape, o_bs, o_im = (nj, m, n), (None, tm, n), (lambda j, i, kk: (j, i, 0))
    return _mm(name, a, wg, (tm, tk), lambda j, i, kk: (i, kk), (None, tk, n), lambda j, i, kk: (j, kk, 0),
               shape, out_dtype, o_bs, o_im, grid, ((1,), (0,)))


def mm_plain(name, a, w, out_dtype, res=None):
    m, k = a.shape
    n = w.shape[1]
    tm, tk, tn = _pick(m, TM_CANDS), _pick(k, TK_CANDS), _pick(n, TN_CANDS)
    grid = (n // tn, m // tm, k // tk)
    return _mm(name, a, w, (tm, tk), lambda j, i, kk: (i, kk), (tk, tn), lambda j, i, kk: (kk, j),
               (m, n), out_dtype, (tm, tn), lambda j, i, kk: (i, j), grid, ((1,), (0,)),
               res, (tm, tn), lambda j, i, kk: (i, j))


def mm_jsum(name, aj, wg, out_dtype, res=None):
    nj, m, ka = aj.shape
    n = wg.shape[2]
    tm, tn = _pick(m, TM_CANDS), _pick(n, TN_CANDS)
    grid = (m // tm, n // tn, nj)
    return _mm(name, aj, wg, (None, tm, ka), lambda i, jn, j: (j, i, 0), (None, ka, tn), lambda i, jn, j: (j, 0, jn),
               (m, n), out_dtype, (tm, tn), lambda i, jn, j: (i, jn), grid, ((1,), (0,)),
               res, (tm, tn), lambda i, jn, j: (i, jn))


def mm_tn_cols(name, a, g, nj, n, out_dtype, folded):
    s, kw = a.shape
    ts, tkw = _pick(s, TK_CANDS), _pick(kw, TM_CANDS)
    grid = (nj, kw // tkw, s // ts)
    if folded:
        g_bs, g_im = (ts, n), (lambda j, i, ss: (ss, j))
    else:
        g_bs, g_im = (None, ts, n), (lambda j, i, ss: (j, ss, 0))
    return _mm(name, a, g, (ts, tkw), lambda j, i, ss: (ss, i), g_bs, g_im,
               (nj, kw, n), out_dtype, (None, tkw, n), lambda j, i, ss: (j, i, 0), grid, ((0,), (0,)))


def mm_tn_plain(name, a, g, out_dtype):
    s, kw = a.shape
    n = g.shape[1]
    ts, tkw, tn = _pick(s, TK_CANDS), _pick(kw, TM_CANDS), _pick(n, TN_CANDS)
    grid = (kw // tkw, n // tn, s // ts)
    return _mm(name, a, g, (ts, tkw), lambda i, jn, ss: (ss, i), (ts, tn), lambda i, jn, ss: (ss, jn),
               (kw, n), out_dtype, (tkw, tn), lambda i, jn, ss: (i, jn), grid, ((0,), (0,)))


def mm_tn_j(name, aj, g, out_dtype):
    nj, s, ka = aj.shape
    n = g.shape[1]
    ts, tn = _pick(s, TK_CANDS), _pick(n, TN_CANDS)
    grid = (nj, n // tn, s // ts)
    return _mm(name, aj, g, (None, ts, ka), lambda j, jn, ss: (j, ss, 0), (ts, tn), lambda j, jn, ss: (ss, jn),
               (nj, ka, n), out_dtype, (None, ka, tn), lambda j, jn, ss: (j, 0, jn), grid, ((0,), (0,)))


def mm_nt_plain(name, g, w, out_dtype):
    m, n = g.shape
    k = w.shape[0]
    tm, tn, tkk = _pick(m, TM_CANDS), _pick(n, TK_CANDS), _pick(k, TN_CANDS)
    grid = (k // tkk, m // tm, n // tn)
    return _mm(name, g, w, (tm, tn), lambda kk, i, jn: (i, jn), (tkk, tn), lambda kk, i, jn: (kk, jn),
               (m, k), out_dtype, (tm, tkk), lambda kk, i, jn: (i, kk), grid, ((1,), (1,)))


def mm_nt_j(name, g, wg, out_dtype):
    m, n = g.shape
    nj, ka, _ = wg.shape
    tm, tn = _pick(m, TM_CANDS), _pick(n, TK_CANDS)
    grid = (nj, m // tm, n // tn)
    return _mm(name, g, wg, (tm, tn), lambda j, i, jn: (i, jn), (None, ka, tn), lambda j, i, jn: (j, 0, jn),
               (nj, m, ka), out_dtype, (None, tm, ka), lambda j, i, jn: (j, i, 0), grid, ((1,), (1,)))


def mm_nt_jsum(name, g, wg, out_dtype, folded, res=None):
    nj, k, n = wg.shape
    m = g.shape[0] if folded else g.shape[1]
    tm, tkk = _pick(m, TM_CANDS), _pick(k, TN_CANDS)
    grid = (m // tm, k // tkk, nj)
    if folded:
        g_bs, g_im = (tm, n), (lambda i, kk, j: (i, j))
    else:
        g_bs, g_im = (None, tm, n), (lambda i, kk, j: (j, i, 0))
    return _mm(name, g, wg, g_bs, g_im, (None, tkk, n), lambda i, kk, j: (j, kk, 0),
               (m, k), out_dtype, (tm, tkk), lambda i, kk, j: (i, kk), grid, ((1,), (1,)),
               res, (tm, tkk), lambda i, kk, j: (i, kk))


ROW_TILE_CANDS = (256, 128, 64, 32, 16, 8)


def _rstd(x):
    return lax.rsqrt(jnp.mean(x * x, axis=-1, keepdims=True) + RMS_EPS)


def _sigmoid(t):
    return 1.0 / (1.0 + jnp.exp(-t))


def rms_fwd(name, x, gain):
    s, d = x.shape
    ts = _pick(s, ROW_TILE_CANDS)

    def body(x_ref, g_ref, h_ref):
        xv = x_ref[...]
        h_ref[...] = ((xv * _rstd(xv)) * g_ref[...]).astype(h_ref.dtype)

    return pl.pallas_call(
        body, name=name, grid=(s // ts,),
        in_specs=[pl.BlockSpec((ts, d), lambda i: (i, 0)), pl.BlockSpec((1, d), lambda i: (0, 0))],
        out_specs=pl.BlockSpec((ts, d), lambda i: (i, 0)),
        out_shape=jax.ShapeDtypeStruct((s, d), BF16),
        compiler_params=_params(("parallel",), 3 * ts * d * 4))(x, gain)


def rms_bwd(name, x, gain, dh, dres):
    s, d = x.shape
    ts = _pick(s, ROW_TILE_CANDS)

    def body(x_ref, g_ref, dh_ref, dr_ref, dx_ref, dg_ref):
        xv = x_ref[...]
        r = _rstd(xv)
        xhat = xv * r
        dhv = dh_ref[...].astype(F32)
        dxhat = dhv * g_ref[...]
        dx_ref[...] = dr_ref[...] + r * (dxhat - xhat * jnp.mean(dxhat * xhat, axis=-1, keepdims=True))
        part = jnp.sum(dhv * xhat, axis=0, keepdims=True)

        @pl.when(pl.program_id(0) == 0)
        def _():
            dg_ref[...] = part

        @pl.when(pl.program_id(0) > 0)
        def _():
            dg_ref[...] += part

    row = pl.BlockSpec((ts, d), lambda i: (i, 0))
    vec = pl.BlockSpec((1, d), lambda i: (0, 0))
    return pl.pallas_call(
        body, name=name, grid=(s // ts,), in_specs=[row, vec, row, row], out_specs=[row, vec],
        out_shape=[jax.ShapeDtypeStruct((s, d), F32), jax.ShapeDtypeStruct((1, d), F32)],
        compiler_params=_params(("arbitrary",), 6 * ts * d * 4))(x, gain, dh, dres)


def gate_merge_fwd(name, proj, ta, tb, d):
    s = proj.shape[0]
    ts = _pick(s, ROW_TILE_CANDS)
    cb = COL_GATES // d

    def body(ga_ref, gb_ref, ta_ref, tb_ref, o_ref):
        o_ref[...] = (_sigmoid(ga_ref[...]) * ta_ref[...] + _sigmoid(gb_ref[...]) * tb_ref[...]).astype(o_ref.dtype)

    row = pl.BlockSpec((ts, d), lambda i: (i, 0))
    return pl.pallas_call(
        body, name=name, grid=(s // ts,),
        in_specs=[pl.BlockSpec((ts, d), lambda i: (i, cb)), pl.BlockSpec((ts, d), lambda i: (i, cb + 1)), row, row],
        out_specs=row, out_shape=jax.ShapeDtypeStruct((s, d), BF16),
        compiler_params=_params(("parallel",), 5 * ts * d * 4))(proj, proj, ta, tb)


def gate_merge_bwd(name, dmerged, proj, ta, tb, d):
    s = proj.shape[0]
    ts = _pick(s, ROW_TILE_CANDS)
    cb = COL_GATES // d

    def body(dm_ref, ga_ref, gb_ref, ta_ref, tb_ref, dta_ref, dtb_ref, dga_ref, dgb_ref):
        dm = dm_ref[...]
        sa = _sigmoid(ga_ref[...])
        sb = _sigmoid(gb_ref[...])
        dta_ref[...] = (dm * sa).astype(dta_ref.dtype)
        dtb_ref[...] = (dm * sb).astype(dtb_ref.dtype)
        dga_ref[...] = (dm * ta_ref[...] * (sa * (1.0 - sa))).astype(dga_ref.dtype)
        dgb_ref[...] = (dm * tb_ref[...] * (sb * (1.0 - sb))).astype(dgb_ref.dtype)

    row = pl.BlockSpec((ts, d), lambda i: (i, 0))
    out = jax.ShapeDtypeStruct((s, d), BF16)
    return pl.pallas_call(
        body, name=name, grid=(s // ts,),
        in_specs=[row, pl.BlockSpec((ts, d), lambda i: (i, cb)), pl.BlockSpec((ts, d), lambda i: (i, cb + 1)), row, row],
        out_specs=[row, row, row, row], out_shape=[out, out, out, out],
        compiler_params=_params(("parallel",), 8 * ts * d * 4))(dmerged, proj, proj, ta, tb)


def tail_fwd_bwd(name, x2, lp, pp, gain, target):
    s, d = x2.shape
    ts = _pick(s, ROW_TILE_CANDS)

    def body(x2_ref, lp_ref, pp_ref, g_ref, t_ref, loss_ref, dx3_ref, dlp_ref, dpp_ref, dg_ref):
        gp = _sigmoid(lp_ref[...])
        ppv = pp_ref[...]
        x3 = x2_ref[...] + gp * ppv
        r = _rstd(x3)
        xhat = x3 * r
        gv = g_ref[...]
        err = xhat * gv - t_ref[...]
        loss = jnp.sum(err * err) * (0.5 / d)
        dy = err * (1.0 / d)
        dxhat = dy * gv
        dx3 = r * (dxhat - xhat * jnp.mean(dxhat * xhat, axis=-1, keepdims=True))
        dx3_ref[...] = dx3
        dlp_ref[...] = (dx3 * ppv * (gp * (1.0 - gp))).astype(dlp_ref.dtype)
        dpp_ref[...] = (dx3 * gp).astype(dpp_ref.dtype)
        part = jnp.sum(dy * xhat, axis=0, keepdims=True)
        lossv = jnp.full((1, LANES), loss, F32)

        @pl.when(pl.program_id(0) == 0)
        def _():
            dg_ref[...] = part
            loss_ref[...] = lossv

        @pl.when(pl.program_id(0) > 0)
        def _():
            dg_ref[...] += part
            loss_ref[...] += lossv

    row = pl.BlockSpec((ts, d), lambda i: (i, 0))
    vec = pl.BlockSpec((1, d), lambda i: (0, 0))
    return pl.pallas_call(
        body, name=name, grid=(s // ts,), in_specs=[row, row, row, vec, row],
        out_specs=[pl.BlockSpec((1, LANES), lambda i: (0, 0)), row, row, row, vec],
        out_shape=[jax.ShapeDtypeStruct((1, LANES), F32), jax.ShapeDtypeStruct((s, d), F32),
                   jax.ShapeDtypeStruct((s, d), BF16), jax.ShapeDtypeStruct((s, d), BF16),
                   jax.ShapeDtypeStruct((1, d), F32)],
        compiler_params=_params(("arbitrary",), 9 * ts * d * 4))(x2, lp, pp, gain, target)


HALO = SUBLANES


def _shift_rows(cur, prev_row, next_row):
    ts = cur.shape[0]
    rid = lax.broadcasted_iota(jnp.int32, cur.shape, 0)
    down = jnp.where(rid == 0, prev_row, pltpu.roll(cur, 1, 0))
    up = jnp.where(rid == ts - 1, next_row, pltpu.roll(cur, ts - 1, 0))
    return down, up


def _halo_specs(ts, s, nf):
    nb = ts // HALO
    last = s // HALO - 1
    cur = pl.BlockSpec((None, ts, nf), lambda j, i: (j, i, 0))
    prev = pl.BlockSpec((None, HALO, nf), lambda j, i: (j, jnp.maximum(i * nb - 1, 0), 0))
    nxt = pl.BlockSpec((None, HALO, nf), lambda j, i: (j, jnp.minimum((i + 1) * nb, last), 0))
    return cur, prev, nxt


def _halo_rows(prev_ref, next_ref, n_tiles):
    i = pl.program_id(1)
    prev_row = jnp.where(i == 0, 0.0, prev_ref[HALO - 1:HALO, :].astype(F32))
    next_row = jnp.where(i == n_tiles - 1, 0.0, next_ref[0:1, :].astype(F32))
    return prev_row, next_row


def _gelu(g):
    t = jnp.tanh(GELU_C * (g + GELU_A * (g * g * g)))
    return 0.5 * g * (1.0 + t), t


def _conv(cur, down, up, cw_ref, cb_ref):
    return down * cw_ref[0:1, :] + cur * cw_ref[1:2, :] + up * cw_ref[2:3, :] + cb_ref[...]


def ffn_mid_fwd(name, gpre, u, cw, cb):
    nj, s, nf = gpre.shape
    ts = _pick(s, (512, 256, 128, 64, 32, 16, 8))
    n_tiles = s // ts
    cur, prev, nxt = _halo_specs(ts, s, nf)

    def body(g_ref, gp_ref, gn_ref, u_ref, cw_ref, cb_ref, z_ref):
        gv = g_ref[...]
        down, up = _shift_rows(gv, *_halo_rows(gp_ref, gn_ref, n_tiles))
        act, _ = _gelu(_conv(gv, down, up, cw_ref, cb_ref))
        z_ref[...] = (act * u_ref[...]).astype(z_ref.dtype)

    return pl.pallas_call(
        body, name=name, grid=(nj, n_tiles),
        in_specs=[cur, prev, nxt, cur, pl.BlockSpec((None, SUBLANES, nf), lambda j, i: (j, 0, 0)),
                  pl.BlockSpec((None, 1, nf), lambda j, i: (j, 0, 0))],
        out_specs=cur, out_shape=jax.ShapeDtypeStruct((nj, s, nf), BF16),
        compiler_params=_params(("parallel", "parallel"), 8 * ts * nf * 4))(gpre, gpre, gpre, u, cw, cb)


def ffn_mid_bwd1(name, gpre, u, dz, cw, cb):
    nj, s, nf = gpre.shape
    ts = _pick(s, (512, 256, 128, 64, 32, 16, 8))
    n_tiles = s // ts
    cur, prev, nxt = _halo_specs(ts, s, nf)

    def body(g_ref, gp_ref, gn_ref, u_ref, dz_ref, cw_ref, cb_ref, dg_ref, du_ref, dcw_ref):
        gv = g_ref[...]
        down, up = _shift_rows(gv, *_halo_rows(gp_ref, gn_ref, n_tiles))
        gc = _conv(gv, down, up, cw_ref, cb_ref)
        act, t = _gelu(gc)
        dzv = dz_ref[...].astype(F32)
        du_ref[...] = (dzv * act).astype(du_ref.dtype)
        dact = 0.5 * (1.0 + t) + 0.5 * gc * (1.0 - t * t) * (GELU_C * (1.0 + 3.0 * GELU_A * (gc * gc)))
        dg = dzv * u_ref[...] * dact
        dg_ref[...] = dg
        rows = [jnp.sum(dg * down, axis=0, keepdims=True), jnp.sum(dg * gv, axis=0, keepdims=True),
                jnp.sum(dg * up, axis=0, keepdims=True), jnp.sum(dg, axis=0, keepdims=True)]
        part = jnp.concatenate(rows + [jnp.zeros((SUBLANES - len(rows), nf), F32)], axis=0)

        @pl.when(pl.program_id(1) == 0)
        def _():
            dcw_ref[...] = part

        @pl.when(pl.program_id(1) > 0)
        def _():
            dcw_ref[...] += part

    small = pl.BlockSpec((None, SUBLANES, nf), lambda j, i: (j, 0, 0))
    return pl.pallas_call(
        body, name=name, grid=(nj, n_tiles),
        in_specs=[cur, prev, nxt, cur, cur, small, pl.BlockSpec((None, 1, nf), lambda j, i: (j, 0, 0))],
        out_specs=[cur, cur, small],
        out_shape=[jax.ShapeDtypeStruct((nj, s, nf), F32), jax.ShapeDtypeStruct((nj, s, nf), BF16),
                   jax.ShapeDtypeStruct((nj, SUBLANES, nf), F32)],
        compiler_params=_params(("parallel", "arbitrary"), 12 * ts * nf * 4))(gpre, gpre, gpre, u, dz, cw, cb)


def ffn_mid_bwd2(name, dg, cw):
    nj, s, nf = dg.shape
    ts = _pick(s, (512, 256, 128, 64, 32, 16, 8))
    n_tiles = s // ts
    cur, prev, nxt = _halo_specs(ts, s, nf)

    def body(g_ref, gp_ref, gn_ref, cw_ref, o_ref):
        gv = g_ref[...]
        down, up = _shift_rows(gv, *_halo_rows(gp_ref, gn_ref, n_tiles))
        o_ref[...] = (up * cw_ref[0:1, :] + gv * cw_ref[1:2, :] + down * cw_ref[2:3, :]).astype(o_ref.dtype)

    return pl.pallas_call(
        body, name=name, grid=(nj, n_tiles),
        in_specs=[cur, prev, nxt, pl.BlockSpec((None, SUBLANES, nf), lambda j, i: (j, 0, 0))],
        out_specs=cur, out_shape=jax.ShapeDtypeStruct((nj, s, nf), BF16),
        compiler_params=_params(("parallel", "parallel"), 6 * ts * nf * 4))(dg, dg, dg, cw)


def _t5_bucket(rel):
    half = N_BUCKETS // 2
    max_exact = half // 2
    n = jnp.abs(rel)
    side = jnp.where(rel > 0, half, 0)
    nf = jnp.maximum(n, 1).astype(F32)
    large = max_exact + (jnp.log(nf / max_exact) / math.log(MAX_DISTANCE / max_exact)
                         * (half - max_exact)).astype(jnp.int32)
    large = jnp.minimum(large, half - 1)
    return side + jnp.where(n < max_exact, n, large)


def bucket_tile(rows, half, dil):
    rel = (jnp.arange(rows + 2 * half)[None, :] - half) - jnp.arange(rows)[:, None]
    return _t5_bucket(rel * dil).astype(jnp.int32)


def bias_build(name, table_t, bucket, h0, nh, half):
    blk, kw = bucket.shape

    def body(t_ref, b_ref, o_ref):
        h = pl.program_id(0)
        bv = b_ref[...]
        acc = jnp.zeros((blk, kw), F32)
        for b in range(N_BUCKETS):
            acc = jnp.where(bv == b, t_ref[h0 + h, b], acc)
        qi = lax.broadcasted_iota(jnp.int32, (blk, kw), 0)
        ci = lax.broadcasted_iota(jnp.int32, (blk, kw), 1)
        o_ref[...] = jnp.where(jnp.abs(ci - half - qi) <= half, acc, NEG_INF)

    return pl.pallas_call(
        body, name=name, grid=(nh,),
        in_specs=[pl.BlockSpec(memory_space=pltpu.SMEM), pl.BlockSpec((blk, kw), lambda h: (0, 0))],
        out_specs=pl.BlockSpec((None, blk, kw), lambda h: (h, 0, 0)),
        out_shape=jax.ShapeDtypeStruct((nh, blk, kw), F32),
        compiler_params=_params(("parallel",), 4 * blk * kw * 4))(table_t, bucket)


def table_grad(name, dbias, bucket):
    nh, blk, kw = dbias.shape

    def body(d_ref, b_ref, o_ref):
        bv = b_ref[...]
        dv = d_ref[...]
        lane = lax.broadcasted_iota(jnp.int32, (SUBLANES, LANES), 1)
        acc = jnp.zeros((SUBLANES, LANES), F32)
        for b in range(N_BUCKETS):
            acc = jnp.where(lane == b, jnp.sum(jnp.where(bv == b, dv, 0.0)), acc)
        o_ref[...] = acc

    return pl.pallas_call(
        body, name=name, grid=(nh,),
        in_specs=[pl.BlockSpec((None, blk, kw), lambda h: (h, 0, 0)), pl.BlockSpec((blk, kw), lambda h: (0, 0))],
        out_specs=pl.BlockSpec((None, SUBLANES, LANES), lambda h: (h, 0, 0)),
        out_shape=jax.ShapeDtypeStruct((nh, SUBLANES, LANES), F32),
        compiler_params=_params(("parallel",), 4 * blk * kw * 4))(dbias, bucket)


class _Band:
    def __init__(self, s, half, q_rows, n_chains, dil):
        self.s, self.half, self.dil, self.n_chains = s, half, dil, n_chains
        self.seg = s // dil
        self.q_rows = min(q_rows, self.seg)
        self.win = self.q_rows + 2 * half
        self.pad = self.seg + 2 * half
        self.nsb = self.seg // self.q_rows
        self.n_items = dil * self.nsb
        assert self.n_items % n_chains == 0 and self.seg % self.q_rows == 0
        self.staged = dil > 1

    def rows_of(self, r):
        return pl.ds(r, self.seg, stride=self.dil) if self.dil > 1 else slice(None)

    def stage_kv(self, dst, src_ref):
        zeros = jnp.zeros((self.half, HEAD_DIM), dst.dtype)
        for r in range(self.dil):
            base = r * self.pad
            dst[base:base + self.half, :] = zeros
            dst[base + self.half + self.seg:base + self.pad, :] = zeros
            dst[base + self.half:base + self.half + self.seg, :] = src_ref[self.rows_of(r), :].astype(dst.dtype)

    def stage(self, dst, src_ref):
        for r in range(self.dil):
            dst[r * self.seg:(r + 1) * self.seg, :] = src_ref[self.rows_of(r), :].astype(dst.dtype)

    def unstage(self, dst_ref, src, add=False):
        for r in range(self.dil):
            val = src[r * self.seg:(r + 1) * self.seg, :].astype(dst_ref.dtype)
            if add:
                val = val + dst_ref[self.rows_of(r), :]
            dst_ref[self.rows_of(r), :] = val

    def offsets(self, item):
        r, sb = item // self.nsb, item % self.nsb
        qoff = pl.multiple_of(r * self.seg + sb * self.q_rows, self.q_rows)
        koff = pl.multiple_of(r * self.pad + sb * self.q_rows, B_BLOCK)
        kpos = sb * self.q_rows - self.half + lax.broadcasted_iota(jnp.int32, (1, self.win), 1)
        edge = jnp.where((kpos >= 0) & (kpos < self.seg), 0.0, NEG_INF)
        return qoff, koff, edge


def band_attn_fwd(name, proj, bias, sink, *, half, q_rows, n_chains, dil, nh, group, cq, ck, cv):
    s, w = proj.shape
    g = _Band(s, half, q_rows, n_chains, dil)
    has_sink = sink is not None

    def body(*refs):
        q_ref, k_ref, v_ref, b_ref = refs[:4]
        s_ref = refs[4] if has_sink else None
        o_ref, l_ref, ks, vs = refs[4 + has_sink:8 + has_sink]
        qs, os_, ls = refs[8 + has_sink:] if g.staged else (None, o_ref, l_ref)
        g.stage_kv(ks, k_ref)
        g.stage_kv(vs, v_ref)
        if g.staged:
            g.stage(qs, q_ref)
        bias_v = b_ref[...]
        sk = s_ref[pl.program_id(0)] if has_sink else None

        def chain(item):
            qoff, koff, edge = g.offsets(item)
            rows = pl.ds(qoff, g.q_rows)
            qv = qs[rows, :] if g.staged else q_ref[rows, :].astype(BF16)
            kw_ = ks[pl.ds(koff, g.win), :]
            vw_ = vs[pl.ds(koff, g.win), :]
            sc = lax.dot_general(qv, kw_, (((1,), (1,)), ((), ())), preferred_element_type=F32) * ATTN_SCALE
            sc = sc + bias_v + edge
            m = jnp.max(sc, axis=-1, keepdims=True)
            if has_sink:
                m = jnp.maximum(m, sk)
            p = jnp.exp(sc - m)
            den = jnp.sum(p, axis=-1, keepdims=True)
            if has_sink:
                den = den + jnp.exp(sk - m)
            out = lax.dot_general(p.astype(BF16), vw_, (((1,), (0,)), ((), ())), preferred_element_type=F32)
            return rows, out / den, jnp.broadcast_to(m + jnp.log(den), (g.q_rows, HEAD_DIM))

        def step(i, carry):
            for rows, out, lse in [chain(i * n_chains + u) for u in range(n_chains)]:
                os_[rows, :] = out
                ls[rows, :] = lse
            return carry

        lax.fori_loop(0, g.n_items // n_chains, step, 0)
        if g.staged:
            g.unstage(o_ref, os_)
            g.unstage(l_ref, ls)

    def col(c0, per):
        return pl.BlockSpec((s, HEAD_DIM), lambda h: (0, c0 // LANES + h // per))

    in_specs = [col(cq, 1), col(ck, group), col(cv, group),
                pl.BlockSpec((None, g.q_rows, g.win), lambda h: (h, 0, 0))]
    args = [proj, proj, proj, bias]
    if has_sink:
        in_specs.append(pl.BlockSpec(memory_space=pltpu.SMEM))
        args.append(sink)
    shape = jax.ShapeDtypeStruct((s, nh * HEAD_DIM), F32)
    scratch = [pltpu.VMEM((dil * g.pad, HEAD_DIM), BF16), pltpu.VMEM((dil * g.pad, HEAD_DIM), BF16)]
    if g.staged:
        scratch += [pltpu.VMEM((s, HEAD_DIM), BF16), pltpu.VMEM((s, HEAD_DIM), F32), pltpu.VMEM((s, HEAD_DIM), F32)]
    return pl.pallas_call(
        body, name=name, grid=(nh,), in_specs=in_specs, out_specs=[col(0, 1), col(0, 1)], out_shape=[shape, shape],
        scratch_shapes=scratch, compiler_params=_params(("parallel",), 16 * s * HEAD_DIM * 4))(*args)


def band_attn_bwd(name, proj, bias, sink, dout, out, lse, dlse, *, half, q_rows, n_chains, dil, nh, group, cq, ck, cv):
    s, w = proj.shape
    g = _Band(s, half, q_rows, n_chains, dil)
    nkv = nh // group
    has_sink = sink is not None
    has_dl = dlse is not None
    n_in = 7 + int(has_sink) + int(has_dl)
    n_out = 4 + int(has_sink)

    def body(*refs):
        ins, outs, scr = refs[:n_in], refs[n_in:n_in + n_out], refs[n_in + n_out:]
        q_ref, k_ref, v_ref, b_ref, do_ref, o_ref, l_ref = ins[:7]
        s_ref = ins[7] if has_sink else None
        dl_ref = ins[n_in - 1] if has_dl else None
        dq_ref, dk_ref, dv_ref, db_ref = outs[:4]
        ks, vs, dks, dvs = scr[:4]
        scr = list(scr[4:])
        dsa = scr.pop(0) if has_sink else None
        if g.staged:
            qs, dos, os_, ls, dqs = scr[:5]
            dls = scr[5] if has_dl else None
            g.stage(qs, q_ref)
            g.stage(dos, do_ref)
            g.stage(os_, o_ref)
            g.stage(ls, l_ref)
            if has_dl:
                g.stage(dls, dl_ref)
        else:
            qs, dos, os_, ls, dqs, dls = None, do_ref, o_ref, l_ref, dq_ref, dl_ref
        h = pl.program_id(0)
        g.stage_kv(ks, k_ref)
        g.stage_kv(vs, v_ref)
        dks[...] = jnp.zeros_like(dks)
        dvs[...] = jnp.zeros_like(dvs)
        db_ref[...] = jnp.zeros_like(db_ref)
        bias_v = b_ref[...]
        if has_sink:
            sk = s_ref[h]
            dsa[...] = jnp.zeros_like(dsa)

        def chain(item):
            qoff, koff, edge = g.offsets(item)
            rows = pl.ds(qoff, g.q_rows)
            win = pl.ds(koff, g.win)
            qv = qs[rows, :] if g.staged else q_ref[rows, :].astype(BF16)
            kw_ = ks[win, :]
            vw_ = vs[win, :]
            sc = lax.dot_general(qv, kw_, (((1,), (1,)), ((), ())), preferred_element_type=F32) * ATTN_SCALE
            lv = ls[rows, :][:, 0:1]
            p = jnp.exp(sc + bias_v + edge - lv)
            dov = dos[rows, :]
            delta = jnp.sum(dov * os_[rows, :], axis=-1, keepdims=True)
            dob = dov.astype(BF16)
            dp = lax.dot_general(dob, vw_, (((1,), (1,)), ((), ())), preferred_element_type=F32)
            t = dp - delta
            if has_dl:
                t = t + dls[rows, :][:, 0:1]
            ds = p * t
            dsb = (ds * ATTN_SCALE).astype(BF16)
            dq = lax.dot_general(dsb, kw_, (((1,), (0,)), ((), ())), preferred_element_type=F32)
            dkc = lax.dot_general(dsb, qv, (((0,), (0,)), ((), ())), preferred_element_type=F32)
            dvc = lax.dot_general(p.astype(BF16), dob, (((0,), (0,)), ((), ())), preferred_element_type=F32)
            dsk = jnp.exp(sk - lv) * delta if has_sink else None
            return rows, win, dq, dkc, dvc, ds, dsk

        def step(i, carry):
            res = [chain(i * n_chains + u) for u in range(n_chains)]
            ds_sum = res[0][5]
            for rr in res[1:]:
                ds_sum = ds_sum + rr[5]
            db_ref[...] += ds_sum
            for rows, win, dq, dkc, dvc, ds, dsk in res:
                dqs[rows, :] = dq
                dks[win, :] += dkc
                dvs[win, :] += dvc
                if has_sink:
                    dsa[...] += dsk
            return carry

        lax.fori_loop(0, g.n_items // n_chains, step, 0)

        if g.staged:
            g.unstage(dq_ref, dqs)

        def emit_kv(add):
            for r in range(dil):
                lo = r * g.pad + half
                for dst_ref, src in ((dk_ref, dks), (dv_ref, dvs)):
                    val = src[lo:lo + g.seg, :]
                    if add:
                        val = val + dst_ref[g.rows_of(r), :]
                    dst_ref[g.rows_of(r), :] = val

        if group == 1:
            emit_kv(False)
        else:
            @pl.when(h % group == 0)
            def _():
                emit_kv(False)

            @pl.when(h % group != 0)
            def _():
                emit_kv(True)
        if has_sink:
            outs[4][...] = jnp.full((SUBLANES, LANES), -jnp.sum(dsa[...]), F32)

    def col(c0, per):
        return pl.BlockSpec((s, HEAD_DIM), lambda h: (0, c0 // LANES + h // per))

    b_spec = pl.BlockSpec((None, g.q_rows, g.win), lambda h: (h, 0, 0))
    in_specs = [col(cq, 1), col(ck, group), col(cv, group), b_spec, col(0, 1), col(0, 1), col(0, 1)]
    args = [proj, proj, proj, bias, dout, out, lse]
    if has_sink:
        in_specs.append(pl.BlockSpec(memory_space=pltpu.SMEM))
        args.append(sink)
    if has_dl:
        in_specs.append(col(0, 1))
        args.append(dlse)
    out_specs = [col(0, 1), col(0, group), col(0, group), b_spec]
    out_shape = [jax.ShapeDtypeStruct((s, nh * HEAD_DIM), F32), jax.ShapeDtypeStruct((s, nkv * HEAD_DIM), F32),
                 jax.ShapeDtypeStruct((s, nkv * HEAD_DIM), F32), jax.ShapeDtypeStruct((nh, g.q_rows, g.win), F32)]
    scratch = [pltpu.VMEM((dil * g.pad, HEAD_DIM), BF16), pltpu.VMEM((dil * g.pad, HEAD_DIM), BF16),
               pltpu.VMEM((dil * g.pad, HEAD_DIM), F32), pltpu.VMEM((dil * g.pad, HEAD_DIM), F32)]
    if has_sink:
        out_specs.append(pl.BlockSpec((None, SUBLANES, LANES), lambda h: (h, 0, 0)))
        out_shape.append(jax.ShapeDtypeStruct((nh, SUBLANES, LANES), F32))
        scratch.append(pltpu.VMEM((g.q_rows, 1), F32))
    if g.staged:
        scratch += [pltpu.VMEM((s, HEAD_DIM), BF16)] + [pltpu.VMEM((s, HEAD_DIM), F32)] * (4 + int(has_dl))
    res = pl.pallas_call(
        body, name=name, grid=(nh,), in_specs=in_specs, out_specs=out_specs, out_shape=out_shape,
        scratch_shapes=scratch, compiler_params=_params(("arbitrary",), 28 * s * HEAD_DIM * 4))(*args)
    return res[0], res[1], res[2], res[3], (res[4] if has_sink else None)


def dil_merge_fwd(name, outs, lses):
    s, w = outs[0].shape
    ts = _pick(s, ROW_TILE_CANDS)
    ng = len(outs)

    def body(*refs):
        o_refs, l_refs, y_ref = refs[:ng], refs[ng:2 * ng], refs[2 * ng]
        ls = [l[...] for l in l_refs]
        mx = ls[0]
        for l in ls[1:]:
            mx = jnp.maximum(mx, l)
        es = [jnp.exp(l - mx) for l in ls]
        tot = es[0]
        for e in es[1:]:
            tot = tot + e
        acc = (es[0] / tot) * o_refs[0][...]
        for e, o in zip(es[1:], o_refs[1:]):
            acc = acc + (e / tot) * o[...]
        y_ref[...] = acc.astype(y_ref.dtype)

    row = pl.BlockSpec((ts, w), lambda i: (i, 0))
    return pl.pallas_call(
        body, name=name, grid=(s // ts,), in_specs=[row] * (2 * ng), out_specs=row,
        out_shape=jax.ShapeDtypeStruct((s, w), BF16),
        compiler_params=_params(("parallel",), 10 * ts * w * 4))(*outs, *lses)


def dil_merge_bwd(name, dy, outs, lses):
    s, w = outs[0].shape
    ts = _pick(s, ROW_TILE_CANDS)
    ng = len(outs)
    nhead = w // HEAD_DIM

    def body(*refs):
        dy_ref = refs[0]
        o_refs, l_refs = refs[1:1 + ng], refs[1 + ng:1 + 2 * ng]
        do_refs, dl_refs = refs[1 + 2 * ng:1 + 3 * ng], refs[1 + 3 * ng:1 + 4 * ng]
        for hh in range(nhead):
            cols = slice(hh * HEAD_DIM, (hh + 1) * HEAD_DIM)
            dyv = dy_ref[:, cols]
            ls = [l[:, cols] for l in l_refs]
            mx = ls[0]
            for l in ls[1:]:
                mx = jnp.maximum(mx, l)
            es = [jnp.exp(l - mx) for l in ls]
            tot = es[0]
            for e in es[1:]:
                tot = tot + e
            alphas = [e / tot for e in es]
            dal = [jnp.broadcast_to(jnp.sum(dyv * o[:, cols], axis=-1, keepdims=True), dyv.shape) for o in o_refs]
            mean = alphas[0] * dal[0]
            for a, d in zip(alphas[1:], dal[1:]):
                mean = mean + a * d
            for g in range(ng):
                do_refs[g][:, cols] = alphas[g] * dyv
                dl_refs[g][:, cols] = alphas[g] * (dal[g] - mean)

    row = pl.BlockSpec((ts, w), lambda i: (i, 0))
    shape = jax.ShapeDtypeStruct((s, w), F32)
    res = pl.pallas_call(
        body, name=name, grid=(s // ts,), in_specs=[row] * (1 + 2 * ng), out_specs=[row] * (2 * ng),
        out_shape=[shape] * (2 * ng),
        compiler_params=_params(("parallel",), 16 * ts * w * 4))(dy, *outs, *lses)
    return res[:ng], res[ng:]


def _adamw(w, g, m, v):
    m = ADAM_B1 * m + (1.0 - ADAM_B1) * g
    v = ADAM_B2 * v + (1.0 - ADAM_B2) * (g * g)
    m_hat = m / (1.0 - ADAM_B1 ** ADAM_STEP)
    v_hat = v / (1.0 - ADAM_B2 ** ADAM_STEP)
    delta = -ADAM_LR * (m_hat / (jnp.sqrt(v_hat) + ADAM_EPS) + ADAM_WD * w)
    return delta, m, v


def _row_tile(r, c, budget=1 << 20):
    if r * c * 4 <= budget or r % SUBLANES:
        return r
    for t in (1024, 512, 256, 128, 64, 32, 16, 8):
        if r % t == 0 and t * c * 4 <= budget:
            return t
    return SUBLANES


def adam_small(name, g, w, m, v):
    def body(g_ref, w_ref, m_ref, v_ref, d_ref, nm_ref, nv_ref):
        d_ref[...], nm_ref[...], nv_ref[...] = _adamw(w_ref[...], g_ref[...], m_ref[...], v_ref[...])

    shape = jax.ShapeDtypeStruct(w.shape, F32)
    return pl.pallas_call(body, name=name, out_shape=[shape, shape, shape])(g, w, m, v)


def reduce_adam(name, mine, theirs, w, m, v):
    nq, r, c = mine.shape
    tr = _row_tile(r, c)

    def body(*refs):
        parts, (w_ref, m_ref, v_ref, g_ref, d_ref, nm_ref, nv_ref) = refs[:nq], refs[nq:]
        g = parts[0][...].astype(F32)
        for p_ref in parts[1:]:
            g = g + p_ref[...].astype(F32)
        g_ref[...] = g
        d_ref[...], nm_ref[...], nv_ref[...] = _adamw(w_ref[...], g, m_ref[...], v_ref[...])

    def slot(q):
        return pl.BlockSpec((None, tr, c), lambda i: (q, i, 0))

    row = pl.BlockSpec((tr, c), lambda i: (i, 0))
    shape = jax.ShapeDtypeStruct((r, c), F32)
    return pl.pallas_call(
        body, name=name, grid=(r // tr,), in_specs=[slot(q) for q in range(nq)] + [row, row, row],
        out_specs=[row] * 4, out_shape=[shape] * 4,
        compiler_params=_params(("parallel",), (nq * 2 + 7 * 4) * tr * c))(mine, *[theirs] * (nq - 1), w, m, v)


def _place():
    return lax.axis_index("x"), lax.axis_index("y"), lax.axis_index("c")


def _flip(pos, bits):
    return tuple((1 - p) if b else p for p, b in zip(pos, bits))


def _index(pos):
    return 4 * pos[0] + 2 * pos[1] + pos[2]


ANY = pl.BlockSpec(memory_space=pl.ANY)


HBM = pl.BlockSpec(memory_space=pltpu.HBM)
SEM = pl.BlockSpec(memory_space=pltpu.SEMAPHORE)
EFFECT = pltpu.SideEffectType.DATAFLOW_SIDE_EFFECTING
TO_SIBLING = (0, 0, 1)
TO_CHIPS = [(1, 0, 0), (0, 1, 0), (1, 1, 0)]


def _in_hbm(a):
    return pltpu.with_memory_space_constraint(a, pltpu.HBM)


def _token_value(token):
    return token[0, 0]


def _when(pred, fn):
    if pred is True:
        fn()
    elif pred is not False:
        pl.when(pred)(fn)


def _plan_copy(k, entry, ins, lnd, send_sems, recv_sems):
    a, src_a, sblk, lblk, to, send_if, recv_if = entry
    src = lnd[a] if src_a is None else ins[src_a]
    return pltpu.make_async_remote_copy(
        src_ref=src.at[sblk], dst_ref=lnd[a].at[lblk], send_sem=send_sems.at[k], recv_sem=recv_sems.at[k],
        device_id=to, device_id_type=MESH), send_if, recv_if


def split_start(name, srcs, lands, plan, after):
    ns, nl = len(srcs), len(lands)
    n_copies = len(plan((0, 0, 0)))

    def body(*refs):
        ins, lnd = refs[:ns], refs[ns:ns + nl]
        send_sems, recv_sems = refs[ns + nl + 1], refs[ns + nl + 2]
        token = refs[-1]
        for k, entry in enumerate(plan(_place())):
            cp, send_if, _ = _plan_copy(k, entry, ins, lnd, send_sems, recv_sems)
            _when(send_if, cp.start)
        token[...] = jnp.zeros_like(token)

    outs = pl.pallas_call(
        body, name=name,
        out_shape=(pltpu.SemaphoreType.DMA((n_copies,)), pltpu.SemaphoreType.DMA((n_copies,)),
                   *[pltpu.HBM(a.shape, a.dtype) for a in srcs], *[pltpu.HBM(a.shape, a.dtype) for a in lands],
                   jax.ShapeDtypeStruct((SUBLANES, LANES), F32)),
        in_specs=[HBM] * (ns + nl) + [ANY],
        out_specs=(SEM, SEM, *[HBM] * (ns + nl), pl.BlockSpec(memory_space=pltpu.VMEM)),
        input_output_aliases={i: 2 + i for i in range(ns + nl)},
        compiler_params=pltpu.CompilerParams(has_side_effects=EFFECT),
    )(*[_in_hbm(a) for a in srcs], *[_in_hbm(a) for a in lands], after)
    return outs[0], outs[1], list(outs[2:2 + ns]), list(outs[2 + ns:2 + ns + nl]), outs[-1]


def split_wait(name, send_sems, recv_sems, srcs, lands, plan, after):
    ns, nl = len(srcs), len(lands)

    def body(*refs):
        ins, lnd = refs[:ns], refs[ns:ns + nl]
        s_sems, r_sems = refs[ns + nl], refs[ns + nl + 1]
        for k, entry in enumerate(plan(_place())):
            cp, send_if, recv_if = _plan_copy(k, entry, ins, lnd, s_sems, r_sems)
            _when(send_if, cp.wait_send)
            _when(recv_if, cp.wait_recv)

    outs = pl.pallas_call(
        body, name=name,
        out_shape=(*[pltpu.HBM(a.shape, a.dtype) for a in srcs], *[pltpu.HBM(a.shape, a.dtype) for a in lands]),
        in_specs=[HBM] * (ns + nl) + [SEM, SEM, ANY],
        out_specs=tuple([HBM] * (ns + nl)),
        input_output_aliases={i: i for i in range(ns + nl)},
        compiler_params=pltpu.CompilerParams(has_side_effects=EFFECT),
    )(*srcs, *lands, send_sems, recv_sems, after)
    return list(outs[:ns]), list(outs[ns:])


NORTH = 1


def ag_plan(n):
    def plan(me):
        x, y, c = me
        entries = []
        for a in range(n):
            for t in (NORTH, 1 - NORTH):
                blk = _index((x, y, t))
                for rel in TO_CHIPS:
                    entries.append((a, None, blk, blk, _flip((x, y, t), rel), c == NORTH, c == t))
        return entries
    return plan


def ag_pair(name, lands):
    n = len(lands)

    def body(*refs):
        lnd = refs[n:2 * n]
        send_sems, recv_sems = refs[2 * n:]
        me = _place()
        sibling = _flip(me, TO_SIBLING)
        copies = []
        for a in range(n):
            mine, theirs = lnd[a].at[_index(me)], lnd[a].at[_index(sibling)]
            cp = pltpu.make_async_remote_copy(src_ref=mine, dst_ref=mine, send_sem=send_sems.at[a],
                                              recv_sem=recv_sems.at[a], device_id=sibling, device_id_type=MESH)
            cp.start()
            copies.append((cp, pltpu.make_async_remote_copy(
                src_ref=mine, dst_ref=theirs, send_sem=send_sems.at[a], recv_sem=recv_sems.at[a], device_id=sibling,
                device_id_type=MESH)))
        for cp, arrival in copies:
            arrival.wait_recv()
        for cp, arrival in copies:
            cp.wait_send()

    return pl.pallas_call(
        body, name=name, in_specs=[ANY] * n, out_specs=[ANY] * n,
        out_shape=[jax.ShapeDtypeStruct(l.shape, l.dtype) for l in lands],
        input_output_aliases={a: a for a in range(n)},
        scratch_shapes=[pltpu.SemaphoreType.DMA((n,)), pltpu.SemaphoreType.DMA((n,))],
    )(*lands)


def ag_start(name, lands, after):
    return split_start(name, [], lands, ag_plan(len(lands)), after)


def ag_finish(name, lands):
    n = len(lands)

    def body(*refs):
        lnd = refs[n:2 * n]
        send_sems, recv_sems = refs[2 * n:]
        me = _place()
        sibling = _flip(me, TO_SIBLING)
        copies = []
        for a in range(n):
            for j, rel in enumerate(TO_CHIPS):
                blk = lnd[a].at[_index(_flip(me, rel))]
                there = lnd[a].at[_index(_flip(sibling, rel))]
                cp = pltpu.make_async_remote_copy(
                    src_ref=blk, dst_ref=blk, send_sem=send_sems.at[a * 3 + j], recv_sem=recv_sems.at[a * 3 + j],
                    device_id=sibling, device_id_type=MESH)
                cp.start()
                copies.append((cp, pltpu.make_async_remote_copy(
                    src_ref=blk, dst_ref=there, send_sem=send_sems.at[a * 3 + j], recv_sem=recv_sems.at[a * 3 + j],
                    device_id=sibling, device_id_type=MESH)))
        for cp, arrival in copies:
            arrival.wait_recv()
        for cp, arrival in copies:
            cp.wait_send()

    return pl.pallas_call(
        body, name=name, in_specs=[ANY] * n, out_specs=[ANY] * n,
        out_shape=[jax.ShapeDtypeStruct(l.shape, l.dtype) for l in lands],
        input_output_aliases={a: a for a in range(n)},
        scratch_shapes=[pltpu.SemaphoreType.DMA((3 * n,)), pltpu.SemaphoreType.DMA((3 * n,))],
    )(*lands)


REL = [(b >> 2 & 1, b >> 1 & 1, b & 1) for b in range(N_DEV)]


CHIP_REL = [(0, 0, 0)] + TO_CHIPS
N_CHIPS = len(CHIP_REL)


def rs_pair(name, parts):
    n = len(parts)

    def body(*refs):
        ins, got = refs[:n], refs[n:2 * n]
        send_sems, recv_sems = refs[2 * n:]
        me = _place()
        sibling = _flip(me, TO_SIBLING)
        remote = []
        for a in range(n):
            for q, rel in enumerate(CHIP_REL):
                k = a * N_CHIPS + q
                cp = pltpu.make_async_remote_copy(
                    src_ref=ins[a].at[_index(_flip(sibling, rel))], dst_ref=got[a].at[q], send_sem=send_sems.at[k],
                    recv_sem=recv_sems.at[k], device_id=sibling, device_id_type=MESH)
                cp.start()
                remote.append(cp)
        for cp in remote:
            cp.wait_recv()
        for cp in remote:
            cp.wait_send()

    shapes = [jax.ShapeDtypeStruct((N_CHIPS,) + tuple(p.shape[1:]), p.dtype) for p in parts]
    res = pl.pallas_call(
        body, name=name, in_specs=[ANY] * n, out_specs=[ANY] * n, out_shape=shapes,
        scratch_shapes=[pltpu.SemaphoreType.DMA((N_CHIPS * n,)), pltpu.SemaphoreType.DMA((N_CHIPS * n,))],
    )(*parts)
    return list(res)


def own_blocks():
    me = _place()
    return jnp.stack([_index(_flip(me, rel)) for rel in CHIP_REL]).astype(jnp.int32)


def pair_add(name, blocks, parts, got):
    nq, r, c = got.shape
    tr = _row_tile(r, c)

    def body(blk_ref, a_ref, b_ref, o_ref):
        o_ref[...] = (a_ref[...].astype(F32) + b_ref[...].astype(F32)).astype(o_ref.dtype)

    spec = pl.BlockSpec((None, tr, c), lambda q, i, blk: (q, i, 0))
    return pl.pallas_call(
        body, name=name,
        grid_spec=pltpu.PrefetchScalarGridSpec(
            num_scalar_prefetch=1, grid=(nq, r // tr),
            in_specs=[pl.BlockSpec((None, tr, c), lambda q, i, blk: (blk[q], i, 0)), spec], out_specs=spec),
        out_shape=jax.ShapeDtypeStruct(got.shape, got.dtype),
        compiler_params=_params(("arbitrary", "arbitrary"), 6 * tr * c * 2))(blocks, parts, got)


def rs_plan(n):
    def plan(me):
        return [(a, a, q, q, _flip(me, CHIP_REL[q]), True, True) for a in range(n) for q in range(1, N_CHIPS)]
    return plan


def rs_start(name, sums, after):
    lands = [lax.empty(t.shape, t.dtype) for t in sums]
    return split_start(name, sums, lands, rs_plan(len(sums)), after)


def allreduce_small(name, pack, after):
    rows, lanes = pack.shape

    def body(x_ref, after_ref, o_ref, land, send_sems, recv_sems):
        me = _place()
        idx = _index(me)
        land[idx] = x_ref[...]
        copies = []
        for r in range(1, N_DEV):
            peer = _flip(me, REL[r])
            cp = pltpu.make_async_remote_copy(
                src_ref=x_ref, dst_ref=land.at[idx], send_sem=send_sems.at[r - 1], recv_sem=recv_sems.at[r - 1],
                device_id=peer, device_id_type=MESH)
            cp.start()
            copies.append(cp)
        for cp in copies:
            cp.wait_recv()
        for cp in copies:
            cp.wait_send()
        acc = land[0]
        for i in range(1, N_DEV):
            acc = acc + land[i]
        o_ref[...] = acc

    return pl.pallas_call(
        body, name=name, in_specs=[pl.BlockSpec(memory_space=pltpu.VMEM), ANY],
        out_specs=pl.BlockSpec(memory_space=pltpu.VMEM), out_shape=jax.ShapeDtypeStruct((rows, lanes), F32),
        scratch_shapes=[pltpu.VMEM((N_DEV, rows, lanes), F32), pltpu.SemaphoreType.DMA((7,)),
                        pltpu.SemaphoreType.DMA((7,))],
    )(pack, after)


def _pad_rows(a, rows):
    return jnp.pad(a, ((0, rows - a.shape[0]), (0, 0)))


def _as_tiles(vec):
    n = vec.shape[0]
    rows = -(-n // LANES)
    rows = -(-rows // SUBLANES) * SUBLANES
    return jnp.pad(vec, (0, rows * LANES - n)).reshape(rows, LANES)


def kernel(x, p, rel_bias_table, attn_norm, w_in, sink_a, w_branch_a, w_branch_b, w_out, ffn_norm, w_ffn_gate, w_ffn_up, conv_w, conv_b, w_ffn_down, ple_norm, w_ple_gate, w_ple_proj, final_norm, loss_target, m_rel_bias_table, m_attn_norm, m_w_in, m_sink_a, m_w_branch_a, m_w_branch_b, m_w_out, m_ffn_norm, m_w_ffn_gate, m_w_ffn_up, m_conv_w, m_conv_b, m_w_ffn_down, m_ple_norm, m_w_ple_gate, m_w_ple_proj, m_final_norm, v_rel_bias_table, v_attn_norm, v_w_in, v_sink_a, v_w_branch_a, v_w_branch_b, v_w_out, v_ffn_norm, v_w_ffn_gate, v_w_ffn_up, v_conv_w, v_conv_b, v_w_ffn_down, v_ple_norm, v_w_ple_gate, v_w_ple_proj, v_final_norm):
    xs = x[0]
    s, d = xs.shape
    ps = p[0, 0]
    target = loss_target[0]
    me = 4 * lax.axis_index("x") + 2 * lax.axis_index("y") + lax.axis_index("c")

    big = dict(w_in=w_in[0], w_branch_a=w_branch_a[0], w_branch_b=w_branch_b[0], w_out=w_out[0],
               w_ffn_gate=w_ffn_gate[0], w_ffn_up=w_ffn_up[0], w_ffn_down=w_ffn_down[0],
               w_ple_gate=w_ple_gate[0], w_ple_proj=w_ple_proj[0])
    big_m = dict(w_in=m_w_in[0], w_branch_a=m_w_branch_a[0], w_branch_b=m_w_branch_b[0], w_out=m_w_out[0],
                 w_ffn_gate=m_w_ffn_gate[0], w_ffn_up=m_w_ffn_up[0], w_ffn_down=m_w_ffn_down[0],
                 w_ple_gate=m_w_ple_gate[0], w_ple_proj=m_w_ple_proj[0])
    big_v = dict(w_in=v_w_in[0], w_branch_a=v_w_branch_a[0], w_branch_b=v_w_branch_b[0], w_out=v_w_out[0],
                 w_ffn_gate=v_w_ffn_gate[0], w_ffn_up=v_w_ffn_up[0], w_ffn_down=v_w_ffn_down[0],
                 w_ple_gate=v_w_ple_gate[0], w_ple_proj=v_w_ple_proj[0])
    names = list(big)
    nf = big["w_ffn_gate"].shape[1]

    shards = {k: big[k].astype(BF16) for k in names}
    shards["conv_w"] = _pad_rows(conv_w[0], SUBLANES)
    ag_groups = [["w_in"], ["w_branch_a", "w_branch_b", "w_out"], ["w_ffn_gate", "w_ffn_up", "conv_w"],
                 ["w_ffn_down", "w_ple_gate", "w_ple_proj"]]
    ag_started = {}
    wg = {}

    def start_gather(gi, after):
        lands = [lax.dynamic_update_index_in_dim(lax.empty((N_DEV,) + shards[k].shape, shards[k].dtype), shards[k],
                                                 me, 0) for k in ag_groups[gi]]
        lands = ag_pair(f"ag_pair{gi}", lands)
        s_sems, r_sems, _, lands, token = ag_start(f"ag_start{gi}", lands, after)
        ag_started[gi] = (s_sems, r_sems, lands)
        return _token_value(token)

    def gather(gi, after):
        s_sems, r_sems, lands = ag_started[gi]
        _, lands = split_wait(f"ag_wait{gi}", s_sems, r_sems, [], lands, ag_plan(len(lands)), after)
        wg.update(zip(ag_groups[gi], ag_finish(f"ag_finish{gi}", lands)))

    cb = conv_b.reshape(N_DEV, 1, nf)

    table_t = rel_bias_table.T
    geo_a = dict(half=A_BLOCK, q_rows=ATTN_Q_ROWS, n_chains=ATTN_CHAINS, dil=1, nh=A_Q_HEADS, group=A_GROUP,
                 cq=COL_QA, ck=COL_KA, cv=COL_VA)
    geo_b = [dict(half=B_BLOCK, q_rows=min(ATTN_Q_ROWS, s // dil), n_chains=ATTN_CHAINS, dil=dil,
                  nh=B_HEADS_PER_GROUP, group=1, cq=COL_QB + g * B_OUT_W, ck=COL_KB + g * B_OUT_W,
                  cv=COL_VB + g * B_OUT_W) for g, (_, dil) in enumerate(B_PATTERNS)]
    bucket_a = bucket_tile(geo_a["q_rows"], A_BLOCK, 1)
    bias_a = bias_build("bias_a", table_t, bucket_a, 0, A_Q_HEADS, A_BLOCK)
    buckets_b = [bucket_tile(gb["q_rows"], B_BLOCK, gb["dil"]) for gb in geo_b]
    biases_b = [bias_build(f"bias_b{g}", table_t, buckets_b[g], A_Q_HEADS + g * B_HEADS_PER_GROUP, B_HEADS_PER_GROUP,
                           B_BLOCK) for g in range(len(B_PATTERNS))]

    tok = start_gather(0, xs) + start_gather(1, xs)
    h = rms_fwd("rms_attn", xs, attn_norm + tok)
    gather(0, h)
    tok = start_gather(2, wg["w_in"])
    proj = mm_cols("proj_in", h, wg["w_in"], F32, fold=True)
    sink = sink_a[0] + tok
    ya, lse_a = band_attn_fwd("attn_a_fwd", proj, bias_a, sink, **geo_a)
    outs_b, lses_b = [], []
    for g in range(len(B_PATTERNS)):
        o, l = band_attn_fwd(f"attn_b{g}_fwd", proj, biases_b[g], None, **geo_b[g])
        outs_b.append(o)
        lses_b.append(l)
    yb = dil_merge_fwd("dil_merge_fwd", outs_b, lses_b)
    gather(1, yb)
    tok = start_gather(3, wg["w_out"])
    w_out_full = wg["w_out"].reshape(d, d)
    ta = mm_cols("branch_a", ya, wg["w_branch_a"], F32, fold=True)
    tb = mm_cols("branch_b", yb, wg["w_branch_b"], F32, fold=True)
    merged = gate_merge_fwd("gate_merge_fwd", proj, ta, tb, d)
    x1 = mm_plain("mix_out", merged, w_out_full, F32, res=xs)

    hf = rms_fwd("rms_ffn", x1, ffn_norm + tok)
    gather(2, hf)
    cw = wg["conv_w"]
    gpre = mm_cols("ffn_gate", hf, wg["w_ffn_gate"], F32, fold=False)
    u = mm_cols("ffn_up", hf, wg["w_ffn_up"], F32, fold=False)
    z = ffn_mid_fwd("ffn_mid_fwd", gpre, u, cw, cb)
    gather(3, z)
    w_pg_full = wg["w_ple_gate"].reshape(d, d)
    x2 = mm_jsum("ffn_down", z, wg["w_ffn_down"], F32, res=x1)

    hp = rms_fwd("rms_ple", x2, ple_norm)
    lp = mm_plain("ple_gate", hp, w_pg_full, F32)
    pp = mm_cols("ple_proj", ps, wg["w_ple_proj"], F32, fold=True)
    loss_part, dx3, dlp, dpp, d_final = tail_fwd_bwd("tail", x2, lp, pp, final_norm.reshape(1, d), target)

    grads = {}
    rs_started = []
    blocks = own_blocks()

    def scatter(tag, keys):
        got = rs_pair(f"rs_pair_{tag}", [grads[k] for k in keys])
        sums = [pair_add(f"pair_add_{k}", blocks, grads[k], g) for k, g in zip(keys, got)]
        s_sems, r_sems, srcs, lands, token = rs_start(f"rs_start_{tag}", sums, blocks)
        rs_started.append((tag, keys, s_sems, r_sems, srcs, lands))
        return _token_value(token)

    grads["w_ple_proj"] = mm_tn_cols("d_w_ple_proj", ps, dpp, N_DEV, big["w_ple_proj"].shape[1], BF16, folded=True)
    grads["w_ple_gate"] = mm_tn_plain("d_w_ple_gate", hp, dlp, BF16).reshape(N_DEV, d // N_DEV, d)
    tok = scatter("ple", ["w_ple_proj", "w_ple_gate"])
    dhp = mm_nt_plain("d_hp", dlp, w_pg_full, F32)
    dx2, d_ple = rms_bwd("rms_ple_bwd", x2, ple_norm + tok, dhp, dx3)

    dz = mm_nt_j("d_z", dx2, wg["w_ffn_down"], BF16)
    grads["w_ffn_down"] = mm_tn_j("d_w_ffn_down", z, dx2, BF16)
    tok = scatter("down", ["w_ffn_down"])
    dg, du, dcw = ffn_mid_bwd1("ffn_mid_bwd1", gpre, u, dz, cw, cb + tok)
    dgpre = ffn_mid_bwd2("ffn_mid_bwd2", dg, cw)
    grads["w_ffn_up"] = mm_tn_cols("d_w_ffn_up", hf, du, N_DEV, nf, BF16, folded=False)
    grads["w_ffn_gate"] = mm_tn_cols("d_w_ffn_gate", hf, dgpre, N_DEV, nf, BF16, folded=False)
    tok = scatter("upgate", ["w_ffn_up", "w_ffn_gate"])
    dhf = mm_nt_jsum("d_hf_up", du, wg["w_ffn_up"], F32, folded=False)
    dhf = mm_nt_jsum("d_hf_gate", dgpre, wg["w_ffn_gate"], F32, folded=False, res=dhf)
    dx1, d_ffn = rms_bwd("rms_ffn_bwd", x1, ffn_norm + tok, dhf, dx2)

    dmerged = mm_nt_plain("d_merged", dx1, w_out_full, F32)
    grads["w_out"] = mm_tn_plain("d_w_out", merged, dx1, BF16).reshape(N_DEV, d // N_DEV, d)
    dta, dtb, dga, dgb = gate_merge_bwd("gate_merge_bwd", dmerged, proj, ta, tb, d)
    grads["w_branch_a"] = mm_tn_cols("d_w_branch_a", ya, dta, N_DEV, big["w_branch_a"].shape[1], BF16, folded=True)
    grads["w_branch_b"] = mm_tn_cols("d_w_branch_b", yb, dtb, N_DEV, big["w_branch_b"].shape[1], BF16, folded=True)
    tok = scatter("mix", ["w_out", "w_branch_a", "w_branch_b"])
    dya = mm_nt_jsum("d_ya", dta, wg["w_branch_a"], F32, folded=True)
    dyb = mm_nt_jsum("d_yb", dtb, wg["w_branch_b"], F32, folded=True)
    dqa, dka, dva, dbias_a, dsink = band_attn_bwd("attn_a_bwd", proj, bias_a, sink + tok, dya, ya, lse_a, None, **geo_a)
    douts_b, dlses_b = dil_merge_bwd("dil_merge_bwd", dyb, outs_b, lses_b)
    dq_b, dk_b, dv_b, dbias_b = [], [], [], []
    for g in range(len(B_PATTERNS)):
        dq, dk, dv, db, _ = band_attn_bwd(f"attn_b{g}_bwd", proj, biases_b[g], None, douts_b[g], outs_b[g], lses_b[g],
                                          dlses_b[g], **geo_b[g])
        dq_b.append(dq)
        dk_b.append(dk)
        dv_b.append(dv)
        dbias_b.append(db)
    dproj = jnp.concatenate([t.astype(BF16) for t in [dqa, dka, dva] + dq_b + dk_b + dv_b + [dga, dgb]], axis=1)
    grads["w_in"] = mm_tn_cols("d_w_in", h, dproj, N_DEV, big["w_in"].shape[1], BF16, folded=True)
    tok = scatter("in", ["w_in"])
    dh = mm_nt_jsum("d_h", dproj, wg["w_in"], F32, folded=True)
    grad_x, d_attn = rms_bwd("rms_attn_bwd", xs, attn_norm + tok, dh, dx1)

    dt_a = table_grad("table_grad_a", dbias_a, bucket_a)[:, 0, :N_BUCKETS]
    dt_b = [table_grad(f"table_grad_b{g}", dbias_b[g], buckets_b[g])[:, 0, :N_BUCKETS] for g in range(len(B_PATTERNS))]
    d_table_part = jnp.concatenate([dt_a] + dt_b, axis=0).T

    pieces = [
        ("loss", loss_part[0, :1]),
        ("table", d_table_part.reshape(-1)),
        ("attn_norm", d_attn.reshape(-1)),
        ("sink", dsink[:, 0, 0]),
        ("ffn_norm", d_ffn.reshape(-1)),
        ("conv_w", dcw[:, 0:3, :].reshape(-1)),
        ("conv_b", dcw[:, 3, :].reshape(-1)),
        ("ple_norm", d_ple.reshape(-1)),
        ("final_norm", d_final.reshape(-1)),
    ]
    tiles = [_as_tiles(v) for _, v in pieces]
    pack = jnp.concatenate(tiles, axis=0)

    out_g, out_d, out_m, out_v = {}, {}, {}, {}

    def finish(group, after):
        tag, keys, s_sems, r_sems, srcs, lands = group
        srcs, lands = split_wait(f"rs_wait_{tag}", s_sems, r_sems, srcs, lands, rs_plan(len(keys)), after)
        for k, mine, theirs in zip(keys, srcs, lands):
            g, dl, nm, nv = reduce_adam("adam_" + k, mine, theirs, big[k], big_m[k], big_v[k])
            out_g[k], out_d[k], out_m[k], out_v[k] = g[None], dl[None], nm[None], nv[None]
            after = dl
        return after

    after = pack
    for group in rs_started[:-1]:
        after = finish(group, after)
    total = allreduce_small("allreduce_small", pack, after)
    finish(rs_started[-1], total)
    small = {}
    row = 0
    for (nm, v), t in zip(pieces, tiles):
        small[nm] = total[row:row + t.shape[0]].reshape(-1)[:v.shape[0]]
        row += t.shape[0]
    loss = small["loss"][0]
    g_small = dict(
        rel_bias_table=small["table"].reshape(rel_bias_table.shape),
        attn_norm=small["attn_norm"].reshape(attn_norm.shape),
        sink_a=small["sink"].reshape(sink_a.shape),
        ffn_norm=small["ffn_norm"].reshape(ffn_norm.shape),
        conv_w=lax.dynamic_index_in_dim(small["conv_w"].reshape(N_DEV, 3, nf), me, 0, keepdims=False)[None],
        conv_b=small["conv_b"].reshape(conv_b.shape),
        ple_norm=small["ple_norm"].reshape(ple_norm.shape),
        final_norm=small["final_norm"].reshape(1, d),
    )
    w_small = dict(rel_bias_table=(rel_bias_table, m_rel_bias_table, v_rel_bias_table),
                   attn_norm=(attn_norm, m_attn_norm, v_attn_norm), sink_a=(sink_a, m_sink_a, v_sink_a),
                   ffn_norm=(ffn_norm, m_ffn_norm, v_ffn_norm), conv_w=(conv_w, m_conv_w, v_conv_w),
                   conv_b=(conv_b, m_conv_b, v_conv_b), ple_norm=(ple_norm, m_ple_norm, v_ple_norm),
                   final_norm=(final_norm, m_final_norm, v_final_norm))

    for k, (wv, mv, vv) in w_small.items():
        shape = wv.shape
        two_d = (1, shape[0]) if len(shape) == 1 else ((shape[0] * shape[1], shape[2]) if len(shape) == 3 else shape)
        gk = g_small[k].reshape(two_d)
        dl, nm, nv = adam_small("adam_" + k, gk, wv.reshape(two_d), mv.reshape(two_d), vv.reshape(two_d))
        out_g[k], out_d[k], out_m[k], out_v[k] = gk.reshape(shape), dl.reshape(shape), nm.reshape(shape), nv.reshape(shape)

    order = ["rel_bias_table", "attn_norm", "w_in", "sink_a", "w_branch_a", "w_branch_b", "w_out", "ffn_norm",
             "w_ffn_gate", "w_ffn_up", "conv_w", "conv_b", "w_ffn_down", "ple_norm", "w_ple_gate", "w_ple_proj",
             "final_norm"]
    return (loss, grad_x[None], *[out_g[k] for k in order], *[out_d[k] for k in order],
            *[out_m[k] for k in order], *[out_v[k] for k in order])
```

```python
import math

import jax
import jax.numpy as jnp
from jax import lax
from jax.experimental import pallas as pl
from jax.experimental.pallas import tpu as pltpu

F32 = jnp.float32
BF16 = jnp.bfloat16
MESH = pl.DeviceIdType.MESH
N_DEV = 8

HEAD_DIM = 128
A_Q_HEADS = 8
A_KV_HEADS = 2
A_GROUP = A_Q_HEADS // A_KV_HEADS
A_BLOCK = 128
B_PATTERNS = ((128, 1), (512, 4), (2048, 16))
B_HEADS_PER_GROUP = 4
B_HEADS = len(B_PATTERNS) * B_HEADS_PER_GROUP
B_BLOCK = 64
N_BUCKETS = 32
MAX_DISTANCE = 1024
A_Q_W = A_Q_HEADS * HEAD_DIM
A_KV_W = A_KV_HEADS * HEAD_DIM
B_W = B_HEADS * HEAD_DIM
B_OUT_W = B_HEADS_PER_GROUP * HEAD_DIM
COL_QA = 0
COL_KA = COL_QA + A_Q_W
COL_VA = COL_KA + A_KV_W
COL_QB = COL_VA + A_KV_W
COL_KB = COL_QB + B_W
COL_VB = COL_KB + B_W
COL_GATES = COL_VB + B_W
RMS_EPS = 1e-6
NEG_INF = -1e30
ATTN_SCALE = HEAD_DIM ** -0.5
ATTN_Q_ROWS = 256
ATTN_CHAINS = 2

ADAM_LR = 0.001
ADAM_B1 = 0.9
ADAM_B2 = 0.999
ADAM_EPS = 1e-08
ADAM_WD = 0.01
ADAM_STEP = 10

GELU_C = math.sqrt(2.0 / math.pi)
GELU_A = 0.044715

V7X_VMEM_BYTES = 64 * 1024 * 1024
VMEM_CEILING = V7X_VMEM_BYTES - 8 * 1024 * 1024
LANES = 128
SUBLANES = 8


def _pick(n, cands):
    for c in cands:
        if n % c == 0:
            return c
    return n


def _nbytes(shape, dtype):
    n = 1
    for d in shape:
        if d is not None:
            n *= d
    return n * jnp.dtype(dtype).itemsize


def _params(sem, est_bytes):
    limit = int(min(VMEM_CEILING, max(32 * 1024 * 1024, 2 * est_bytes + (8 << 20))))
    return pltpu.CompilerParams(dimension_semantics=sem, vmem_limit_bytes=limit)


def _mm(name, a, b, a_bs, a_im, b_bs, b_im, out_shape, out_dtype, o_bs, o_im, grid, dims,
        res=None, r_bs=None, r_im=None, after=None):
    nk = grid[-1]
    nax = len(grid)
    has_res = res is not None
    has_after = after is not None
    o_tile = tuple(d for d in o_bs if d is not None)

    def body(*refs):
        a_ref, b_ref = refs[:2]
        r_ref = refs[2] if has_res else None
        n_in = 2 + has_res + has_after
        o_ref = refs[n_in]
        rest = refs[n_in + 1:]

        def prod():
            return lax.dot_general(a_ref[...].astype(BF16), b_ref[...].astype(BF16), (dims, ((), ())),
                                   preferred_element_type=F32)

        def finish(r):
            if r_ref is not None:
                r = r + r_ref[...].astype(F32)
            o_ref[...] = r.astype(o_ref.dtype)

        if nk == 1:
            finish(prod())
        else:
            acc = rest[0]
            k = pl.program_id(nax - 1)

            @pl.when(k == 0)
            def _():
                acc[...] = prod()

            @pl.when(k > 0)
            def _():
                acc[...] += prod()

            @pl.when(k == nk - 1)
            def _():
                finish(acc[...])

    in_specs = [pl.BlockSpec(a_bs, a_im), pl.BlockSpec(b_bs, b_im)]
    args = [a, b]
    est = _nbytes(a_bs, a.dtype) + _nbytes(b_bs, b.dtype) + _nbytes(o_bs, out_dtype) + 2 * _nbytes(o_tile, F32)
    if has_res:
        in_specs.append(pl.BlockSpec(r_bs, r_im))
        args.append(res)
        est += _nbytes(r_bs, res.dtype)
    if has_after:
        in_specs.append(pl.BlockSpec(memory_space=pl.ANY))
        args.append(after)
    scratch = [] if nk == 1 else [pltpu.VMEM(o_tile, F32)]
    sem = ("parallel",) * (nax - 1) + ("arbitrary",)
    return pl.pallas_call(
        body, name=name, grid=grid, in_specs=in_specs, out_specs=pl.BlockSpec(o_bs, o_im),
        out_shape=jax.ShapeDtypeStruct(out_shape, out_dtype), scratch_shapes=scratch,
        compiler_params=_params(sem, est))(*args)


TM_CANDS = (1024, 512, 256, 128, 64, 32, 16, 8)
TK_CANDS = (1024, 512, 256, 128)
TN_CANDS = (1024, 512, 256, 128)


def mm_cols(name, a, wg, out_dtype, fold, after=None):
    m, k = a.shape
    nj, _, n = wg.shape
    tm, tk = _pick(m, TM_CANDS), _pick(k, TK_CANDS)
    grid = (nj, m // tm, k // tk)
    if fold:
        shape, o_bs, o_im = (m, nj * n), (tm, n), (lambda j, i, kk: (i, j))
    else:
        shape, o_bs, o_im = (nj, m, n), (None, tm, n), (lambda j, i, kk: (j, i, 0))
    return _mm(name, a, wg, (tm, tk), lambda j, i, kk: (i, kk), (None, tk, n), lambda j, i, kk: (j, kk, 0),
               shape, out_dtype, o_bs, o_im, grid, ((1,), (0,)), after=after)


def mm_plain(name, a, w, out_dtype, res=None):
    m, k = a.shape
    n = w.shape[1]
    tm, tk, tn = _pick(m, TM_CANDS), _pick(k, TK_CANDS), _pick(n, TN_CANDS)
    grid = (n // tn, m // tm, k // tk)
    return _mm(name, a, w, (tm, tk), lambda j, i, kk: (i, kk), (tk, tn), lambda j, i, kk: (kk, j),
               (m, n), out_dtype, (tm, tn), lambda j, i, kk: (i, j), grid, ((1,), (0,)),
               res, (tm, tn), lambda j, i, kk: (i, j))


def mm_jsum(name, aj, wg, out_dtype, res=None):
    nj, m, ka = aj.shape
    n = wg.shape[2]
    tm, tn = _pick(m, TM_CANDS), _pick(n, TN_CANDS)
    grid = (m // tm, n // tn, nj)
    return _mm(name, aj, wg, (None, tm, ka), lambda i, jn, j: (j, i, 0), (None, ka, tn), lambda i, jn, j: (j, 0, jn),
               (m, n), out_dtype, (tm, tn), lambda i, jn, j: (i, jn), grid, ((1,), (0,)),
               res, (tm, tn), lambda i, jn, j: (i, jn))


def mm_tn_cols(name, a, g, nj, n, out_dtype, folded):
    s, kw = a.shape
    ts, tkw = _pick(s, TK_CANDS), _pick(kw, TM_CANDS)
    grid = (nj, kw // tkw, s // ts)
    if folded:
        g_bs, g_im = (ts, n), (lambda j, i, ss: (ss, j))
    else:
        g_bs, g_im = (None, ts, n), (lambda j, i, ss: (j, ss, 0))
    return _mm(name, a, g, (ts, tkw), lambda j, i, ss: (ss, i), g_bs, g_im,
               (nj, kw, n), out_dtype, (None, tkw, n), lambda j, i, ss: (j, i, 0), grid, ((0,), (0,)))


def mm_tn_plain(name, a, g, out_dtype):
    s, kw = a.shape
    n = g.shape[1]
    ts, tkw, tn = _pick(s, TK_CANDS), _pick(kw, TM_CANDS), _pick(n, TN_CANDS)
    grid = (kw // tkw, n // tn, s // ts)
    return _mm(name, a, g, (ts, tkw), lambda i, jn, ss: (ss, i), (ts, tn), lambda i, jn, ss: (ss, jn),
               (kw, n), out_dtype, (tkw, tn), lambda i, jn, ss: (i, jn), grid, ((0,), (0,)))


def mm_tn_j(name, aj, g, out_dtype):
    nj, s, ka = aj.shape
    n = g.shape[1]
    ts, tn = _pick(s, TK_CANDS), _pick(n, TN_CANDS)
    grid = (nj, n // tn, s // ts)
    return _mm(name, aj, g, (None, ts, ka), lambda j, jn, ss: (j, ss, 0), (ts, tn), lambda j, jn, ss: (ss, jn),
               (nj, ka, n), out_dtype, (None, ka, tn), lambda j, jn, ss: (j, 0, jn), grid, ((0,), (0,)))


def mm_nt_plain(name, g, w, out_dtype):
    m, n = g.shape
    k = w.shape[0]
    tm, tn, tkk = _pick(m, TM_CANDS), _pick(n, TK_CANDS), _pick(k, TN_CANDS)
    grid = (k // tkk, m // tm, n // tn)
    return _mm(name, g, w, (tm, tn), lambda kk, i, jn: (i, jn), (tkk, tn), lambda kk, i, jn: (kk, jn),
               (m, k), out_dtype, (tm, tkk), lambda kk, i, jn: (i, kk), grid, ((1,), (1,)))


def mm_nt_j(name, g, wg, out_dtype):
    m, n = g.shape
    nj, ka, _ = wg.shape
    tm, tn = _pick(m, TM_CANDS), _pick(n, TK_CANDS)
    grid = (nj, m // tm, n // tn)
    return _mm(name, g, wg, (tm, tn), lambda j, i, jn: (i, jn), (None, ka, tn), lambda j, i, jn: (j, 0, jn),
               (nj, m, ka), out_dtype, (None, tm, ka), lambda j, i, jn: (j, i, 0), grid, ((1,), (1,)))


def mm_nt_jsum(name, g, wg, out_dtype, folded, res=None):
    nj, k, n = wg.shape
    m = g.shape[0] if folded else g.shape[1]
    tm, tkk = _pick(m, TM_CANDS), _pick(k, TN_CANDS)
    grid = (m // tm, k // tkk, nj)
    if folded:
        g_bs, g_im = (tm, n), (lambda i, kk, j: (i, j))
    else:
        g_bs, g_im = (None, tm, n), (lambda i, kk, j: (j, i, 0))
    return _mm(name, g, wg, g_bs, g_im, (None, tkk, n), lambda i, kk, j: (j, kk, 0),
               (m, k), out_dtype, (tm, tkk), lambda i, kk, j: (i, kk), grid, ((1,), (1,)),
               res, (tm, tkk), lambda i, kk, j: (i, kk))


ROW_TILE_CANDS = (256, 128, 64, 32, 16, 8)


def _rstd(x):
    return lax.rsqrt(jnp.mean(x * x, axis=-1, keepdims=True) + RMS_EPS)


def _sigmoid(t):
    return 1.0 / (1.0 + jnp.exp(-t))


def rms_fwd(name, x, gain):
    s, d = x.shape
    ts = _pick(s, ROW_TILE_CANDS)

    def body(x_ref, g_ref, h_ref):
        xv = x_ref[...]
        h_ref[...] = ((xv * _rstd(xv)) * g_ref[...]).astype(h_ref.dtype)

    return pl.pallas_call(
        body, name=name, grid=(s // ts,),
        in_specs=[pl.BlockSpec((ts, d), lambda i: (i, 0)), pl.BlockSpec((1, d), lambda i: (0, 0))],
        out_specs=pl.BlockSpec((ts, d), lambda i: (i, 0)),
        out_shape=jax.ShapeDtypeStruct((s, d), BF16),
        compiler_params=_params(("parallel",), 3 * ts * d * 4))(x, gain)


def rms_bwd(name, x, gain, dh, dres):
    s, d = x.shape
    ts = _pick(s, ROW_TILE_CANDS)

    def body(x_ref, g_ref, dh_ref, dr_ref, dx_ref, dg_ref):
        xv = x_ref[...]
        r = _rstd(xv)
        xhat = xv * r
        dhv = dh_ref[...].astype(F32)
        dxhat = dhv * g_ref[...]
        dx_ref[...] = dr_ref[...] + r * (dxhat - xhat * jnp.mean(dxhat * xhat, axis=-1, keepdims=True))
        part = jnp.sum(dhv * xhat, axis=0, keepdims=True)

        @pl.when(pl.program_id(0) == 0)
        def _():
            dg_ref[...] = part

        @pl.when(pl.program_id(0) > 0)
        def _():
            dg_ref[...] += part

    row = pl.BlockSpec((ts, d), lambda i: (i, 0))
    vec = pl.BlockSpec((1, d), lambda i: (0, 0))
    return pl.pallas_call(
        body, name=name, grid=(s // ts,), in_specs=[row, vec, row, row], out_specs=[row, vec],
        out_shape=[jax.ShapeDtypeStruct((s, d), F32), jax.ShapeDtypeStruct((1, d), F32)],
        compiler_params=_params(("arbitrary",), 6 * ts * d * 4))(x, gain, dh, dres)


def gate_merge_fwd(name, proj, ta, tb, d):
    s = proj.shape[0]
    ts = _pick(s, ROW_TILE_CANDS)
    cb = COL_GATES // d

    def body(ga_ref, gb_ref, ta_ref, tb_ref, o_ref):
        o_ref[...] = (_sigmoid(ga_ref[...]) * ta_ref[...] + _sigmoid(gb_ref[...]) * tb_ref[...]).astype(o_ref.dtype)

    row = pl.BlockSpec((ts, d), lambda i: (i, 0))
    return pl.pallas_call(
        body, name=name, grid=(s // ts,),
        in_specs=[pl.BlockSpec((ts, d), lambda i: (i, cb)), pl.BlockSpec((ts, d), lambda i: (i, cb + 1)), row, row],
        out_specs=row, out_shape=jax.ShapeDtypeStruct((s, d), BF16),
        compiler_params=_params(("parallel",), 5 * ts * d * 4))(proj, proj, ta, tb)


def gate_merge_bwd(name, dmerged, proj, ta, tb, d):
    s = proj.shape[0]
    ts = _pick(s, ROW_TILE_CANDS)
    cb = COL_GATES // d

    def body(dm_ref, ga_ref, gb_ref, ta_ref, tb_ref, dta_ref, dtb_ref, dga_ref, dgb_ref):
        dm = dm_ref[...]
        sa = _sigmoid(ga_ref[...])
        sb = _sigmoid(gb_ref[...])
        dta_ref[...] = (dm * sa).astype(dta_ref.dtype)
        dtb_ref[...] = (dm * sb).astype(dtb_ref.dtype)
        dga_ref[...] = (dm * ta_ref[...] * (sa * (1.0 - sa))).astype(dga_ref.dtype)
        dgb_ref[...] = (dm * tb_ref[...] * (sb * (1.0 - sb))).astype(dgb_ref.dtype)

    row = pl.BlockSpec((ts, d), lambda i: (i, 0))
    out = jax.ShapeDtypeStruct((s, d), BF16)
    return pl.pallas_call(
        body, name=name, grid=(s // ts,),
        in_specs=[row, pl.BlockSpec((ts, d), lambda i: (i, cb)), pl.BlockSpec((ts, d), lambda i: (i, cb + 1)), row, row],
        out_specs=[row, row, row, row], out_shape=[out, out, out, out],
        compiler_params=_params(("parallel",), 8 * ts * d * 4))(dmerged, proj, proj, ta, tb)


def tail_fwd_bwd(name, x2, lp, pp, gain, target):
    s, d = x2.shape
    ts = _pick(s, ROW_TILE_CANDS)

    def body(x2_ref, lp_ref, pp_ref, g_ref, t_ref, loss_ref, dx3_ref, dlp_ref, dpp_ref, dg_ref):
        gp = _sigmoid(lp_ref[...])
        ppv = pp_ref[...]
        x3 = x2_ref[...] + gp * ppv
        r = _rstd(x3)
        xhat = x3 * r
        gv = g_ref[...]
        err = xhat * gv - t_ref[...]
        loss = jnp.sum(err * err) * (0.5 / d)
        dy = err * (1.0 / d)
        dxhat = dy * gv
        dx3 = r * (dxhat - xhat * jnp.mean(dxhat * xhat, axis=-1, keepdims=True))
        dx3_ref[...] = dx3
        dlp_ref[...] = (dx3 * ppv * (gp * (1.0 - gp))).astype(dlp_ref.dtype)
        dpp_ref[...] = (dx3 * gp).astype(dpp_ref.dtype)
        part = jnp.sum(dy * xhat, axis=0, keepdims=True)
        lossv = jnp.full((1, LANES), loss, F32)

        @pl.when(pl.program_id(0) == 0)
        def _():
            dg_ref[...] = part
            loss_ref[...] = lossv

        @pl.when(pl.program_id(0) > 0)
        def _():
            dg_ref[...] += part
            loss_ref[...] += lossv

    row = pl.BlockSpec((ts, d), lambda i: (i, 0))
    vec = pl.BlockSpec((1, d), lambda i: (0, 0))
    return pl.pallas_call(
        body, name=name, grid=(s // ts,), in_specs=[row, row, row, vec, row],
        out_specs=[pl.BlockSpec((1, LANES), lambda i: (0, 0)), row, row, row, vec],
        out_shape=[jax.ShapeDtypeStruct((1, LANES), F32), jax.ShapeDtypeStruct((s, d), F32),
                   jax.ShapeDtypeStruct((s, d), BF16), jax.ShapeDtypeStruct((s, d), BF16),
                   jax.ShapeDtypeStruct((1, d), F32)],
        compiler_params=_params(("arbitrary",), 9 * ts * d * 4))(x2, lp, pp, gain, target)


HALO = SUBLANES


def _shift_rows(cur, prev_row, next_row):
    ts = cur.shape[0]
    rid = lax.broadcasted_iota(jnp.int32, cur.shape, 0)
    down = jnp.where(rid == 0, prev_row, pltpu.roll(cur, 1, 0))
    up = jnp.where(rid == ts - 1, next_row, pltpu.roll(cur, ts - 1, 0))
    return down, up


def _halo_specs(ts, s, nf):
    nb = ts // HALO
    last = s // HALO - 1
    cur = pl.BlockSpec((None, ts, nf), lambda j, i: (j, i, 0))
    prev = pl.BlockSpec((None, HALO, nf), lambda j, i: (j, jnp.maximum(i * nb - 1, 0), 0))
    nxt = pl.BlockSpec((None, HALO, nf), lambda j, i: (j, jnp.minimum((i + 1) * nb, last), 0))
    return cur, prev, nxt


def _halo_rows(prev_ref, next_ref, n_tiles):
    i = pl.program_id(1)
    prev_row = jnp.where(i == 0, 0.0, prev_ref[HALO - 1:HALO, :].astype(F32))
    next_row = jnp.where(i == n_tiles - 1, 0.0, next_ref[0:1, :].astype(F32))
    return prev_row, next_row


def _gelu(g):
    t = jnp.tanh(GELU_C * (g + GELU_A * (g * g * g)))
    return 0.5 * g * (1.0 + t), t


def _conv(cur, down, up, cw_ref, cb_ref):
    return down * cw_ref[0:1, :] + cur * cw_ref[1:2, :] + up * cw_ref[2:3, :] + cb_ref[...]


def ffn_mid_fwd(name, gpre, u, cw, cb):
    nj, s, nf = gpre.shape
    ts = _pick(s, (512, 256, 128, 64, 32, 16, 8))
    n_tiles = s // ts
    cur, prev, nxt = _halo_specs(ts, s, nf)

    def body(g_ref, gp_ref, gn_ref, u_ref, cw_ref, cb_ref, z_ref):
        gv = g_ref[...]
        down, up = _shift_rows(gv, *_halo_rows(gp_ref, gn_ref, n_tiles))
        act, _ = _gelu(_conv(gv, down, up, cw_ref, cb_ref))
        z_ref[...] = (act * u_ref[...]).astype(z_ref.dtype)

    return pl.pallas_call(
        body, name=name, grid=(nj, n_tiles),
        in_specs=[cur, prev, nxt, cur, pl.BlockSpec((None, SUBLANES, nf), lambda j, i: (j, 0, 0)),
                  pl.BlockSpec((None, 1, nf), lambda j, i: (j, 0, 0))],
        out_specs=cur, out_shape=jax.ShapeDtypeStruct((nj, s, nf), BF16),
        compiler_params=_params(("parallel", "parallel"), 8 * ts * nf * 4))(gpre, gpre, gpre, u, cw, cb)


def ffn_mid_bwd1(name, gpre, u, dz, cw, cb):
    nj, s, nf = gpre.shape
    ts = _pick(s, (512, 256, 128, 64, 32, 16, 8))
    n_tiles = s // ts
    cur, prev, nxt = _halo_specs(ts, s, nf)

    def body(g_ref, gp_ref, gn_ref, u_ref, dz_ref, cw_ref, cb_ref, dg_ref, du_ref, dcw_ref):
        gv = g_ref[...]
        down, up = _shift_rows(gv, *_halo_rows(gp_ref, gn_ref, n_tiles))
        gc = _conv(gv, down, up, cw_ref, cb_ref)
        act, t = _gelu(gc)
        dzv = dz_ref[...].astype(F32)
        du_ref[...] = (dzv * act).astype(du_ref.dtype)
        dact = 0.5 * (1.0 + t) + 0.5 * gc * (1.0 - t * t) * (GELU_C * (1.0 + 3.0 * GELU_A * (gc * gc)))
        dg = dzv * u_ref[...] * dact
        dg_ref[...] = dg
        rows = [jnp.sum(dg * down, axis=0, keepdims=True), jnp.sum(dg * gv, axis=0, keepdims=True),
                jnp.sum(dg * up, axis=0, keepdims=True), jnp.sum(dg, axis=0, keepdims=True)]
        part = jnp.concatenate(rows + [jnp.zeros((SUBLANES - len(rows), nf), F32)], axis=0)

        @pl.when(pl.program_id(1) == 0)
        def _():
            dcw_ref[...] = part

        @pl.when(pl.program_id(1) > 0)
        def _():
            dcw_ref[...] += part

    small = pl.BlockSpec((None, SUBLANES, nf), lambda j, i: (j, 0, 0))
    return pl.pallas_call(
        body, name=name, grid=(nj, n_tiles),
        in_specs=[cur, prev, nxt, cur, cur, small, pl.BlockSpec((None, 1, nf), lambda j, i: (j, 0, 0))],
        out_specs=[cur, cur, small],
        out_shape=[jax.ShapeDtypeStruct((nj, s, nf), F32), jax.ShapeDtypeStruct((nj, s, nf), BF16),
                   jax.ShapeDtypeStruct((nj, SUBLANES, nf), F32)],
        compiler_params=_params(("parallel", "arbitrary"), 12 * ts * nf * 4))(gpre, gpre, gpre, u, dz, cw, cb)


def ffn_mid_bwd2(name, dg, cw):
    nj, s, nf = dg.shape
    ts = _pick(s, (512, 256, 128, 64, 32, 16, 8))
    n_tiles = s // ts
    cur, prev, nxt = _halo_specs(ts, s, nf)

    def body(g_ref, gp_ref, gn_ref, cw_ref, o_ref):
        gv = g_ref[...]
        down, up = _shift_rows(gv, *_halo_rows(gp_ref, gn_ref, n_tiles))
        o_ref[...] = (up * cw_ref[0:1, :] + gv * cw_ref[1:2, :] + down * cw_ref[2:3, :]).astype(o_ref.dtype)

    return pl.pallas_call(
        body, name=name, grid=(nj, n_tiles),
        in_specs=[cur, prev, nxt, pl.BlockSpec((None, SUBLANES, nf), lambda j, i: (j, 0, 0))],
        out_specs=cur, out_shape=jax.ShapeDtypeStruct((nj, s, nf), BF16),
        compiler_params=_params(("parallel", "parallel"), 6 * ts * nf * 4))(dg, dg, dg, cw)


def _t5_bucket(rel):
    half = N_BUCKETS // 2
    max_exact = half // 2
    n = jnp.abs(rel)
    side = jnp.where(rel > 0, half, 0)
    nf = jnp.maximum(n, 1).astype(F32)
    large = max_exact + (jnp.log(nf / max_exact) / math.log(MAX_DISTANCE / max_exact)
                         * (half - max_exact)).astype(jnp.int32)
    large = jnp.minimum(large, half - 1)
    return side + jnp.where(n < max_exact, n, large)


def bucket_tile(rows, half, dil):
    rel = (jnp.arange(rows + 2 * half)[None, :] - half) - jnp.arange(rows)[:, None]
    return _t5_bucket(rel * dil).astype(jnp.int32)


def bias_build(name, table_t, bucket, h0, nh, half):
    blk, kw = bucket.shape

    def body(t_ref, b_ref, o_ref):
        h = pl.program_id(0)
        bv = b_ref[...]
        acc = jnp.zeros((blk, kw), F32)
        for b in range(N_BUCKETS):
            acc = jnp.where(bv == b, t_ref[h0 + h, b], acc)
        qi = lax.broadcasted_iota(jnp.int32, (blk, kw), 0)
        ci = lax.broadcasted_iota(jnp.int32, (blk, kw), 1)
        o_ref[...] = jnp.where(jnp.abs(ci - half - qi) <= half, acc, NEG_INF)

    return pl.pallas_call(
        body, name=name, grid=(nh,),
        in_specs=[pl.BlockSpec(memory_space=pltpu.SMEM), pl.BlockSpec((blk, kw), lambda h: (0, 0))],
        out_specs=pl.BlockSpec((None, blk, kw), lambda h: (h, 0, 0)),
        out_shape=jax.ShapeDtypeStruct((nh, blk, kw), F32),
        compiler_params=_params(("parallel",), 4 * blk * kw * 4))(table_t, bucket)


def table_grad(name, dbias, bucket):
    nh, blk, kw = dbias.shape

    def body(d_ref, b_ref, o_ref):
        bv = b_ref[...]
        dv = d_ref[...]
        lane = lax.broadcasted_iota(jnp.int32, (SUBLANES, LANES), 1)
        acc = jnp.zeros((SUBLANES, LANES), F32)
        for b in range(N_BUCKETS):
            acc = jnp.where(lane == b, jnp.sum(jnp.where(bv == b, dv, 0.0)), acc)
        o_ref[...] = acc

    return pl.pallas_call(
        body, name=name, grid=(nh,),
        in_specs=[pl.BlockSpec((None, blk, kw), lambda h: (h, 0, 0)), pl.BlockSpec((blk, kw), lambda h: (0, 0))],
        out_specs=pl.BlockSpec((None, SUBLANES, LANES), lambda h: (h, 0, 0)),
        out_shape=jax.ShapeDtypeStruct((nh, SUBLANES, LANES), F32),
        compiler_params=_params(("parallel",), 4 * blk * kw * 4))(dbias, bucket)


class _Band:
    def __init__(self, s, half, q_rows, n_chains, dil):
        self.s, self.half, self.dil, self.n_chains = s, half, dil, n_chains
        self.seg = s // dil
        self.q_rows = min(q_rows, self.seg)
        self.win = self.q_rows + 2 * half
        self.pad = self.seg + 2 * half
        self.nsb = self.seg // self.q_rows
        self.n_items = dil * self.nsb
        assert self.n_items % n_chains == 0 and self.seg % self.q_rows == 0
        self.staged = dil > 1

    def rows_of(self, r):
        return pl.ds(r, self.seg, stride=self.dil) if self.dil > 1 else slice(None)

    def stage_kv(self, dst, src_ref):
        zeros = jnp.zeros((self.half, HEAD_DIM), dst.dtype)
        for r in range(self.dil):
            base = r * self.pad
            dst[base:base + self.half, :] = zeros
            dst[base + self.half + self.seg:base + self.pad, :] = zeros
            dst[base + self.half:base + self.half + self.seg, :] = src_ref[self.rows_of(r), :].astype(dst.dtype)

    def stage(self, dst, src_ref):
        for r in range(self.dil):
            dst[r * self.seg:(r + 1) * self.seg, :] = src_ref[self.rows_of(r), :].astype(dst.dtype)

    def unstage(self, dst_ref, src, add=False):
        for r in range(self.dil):
            val = src[r * self.seg:(r + 1) * self.seg, :].astype(dst_ref.dtype)
            if add:
                val = val + dst_ref[self.rows_of(r), :]
            dst_ref[self.rows_of(r), :] = val

    def offsets(self, item):
        r, sb = item // self.nsb, item % self.nsb
        qoff = pl.multiple_of(r * self.seg + sb * self.q_rows, self.q_rows)
        koff = pl.multiple_of(r * self.pad + sb * self.q_rows, B_BLOCK)
        kpos = sb * self.q_rows - self.half + lax.broadcasted_iota(jnp.int32, (1, self.win), 1)
        edge = jnp.where((kpos >= 0) & (kpos < self.seg), 0.0, NEG_INF)
        return qoff, koff, edge


def band_attn_fwd(name, proj, bias, sink, *, half, q_rows, n_chains, dil, nh, group, cq, ck, cv):
    s, w = proj.shape
    g = _Band(s, half, q_rows, n_chains, dil)
    has_sink = sink is not None

    def body(*refs):
        q_ref, k_ref, v_ref, b_ref = refs[:4]
        s_ref = refs[4] if has_sink else None
        o_ref, l_ref, ks, vs = refs[4 + has_sink:8 + has_sink]
        qs, os_, ls = refs[8 + has_sink:] if g.staged else (None, o_ref, l_ref)
        g.stage_kv(ks, k_ref)
        g.stage_kv(vs, v_ref)
        if g.staged:
            g.stage(qs, q_ref)
        bias_v = b_ref[...]
        sk = s_ref[pl.program_id(0)] if has_sink else None

        def chain(item):
            qoff, koff, edge = g.offsets(item)
            rows = pl.ds(qoff, g.q_rows)
            qv = qs[rows, :] if g.staged else q_ref[rows, :].astype(BF16)
            kw_ = ks[pl.ds(koff, g.win), :]
            vw_ = vs[pl.ds(koff, g.win), :]
            sc = lax.dot_general(qv, kw_, (((1,), (1,)), ((), ())), preferred_element_type=F32) * ATTN_SCALE
            sc = sc + bias_v + edge
            m = jnp.max(sc, axis=-1, keepdims=True)
            if has_sink:
                m = jnp.maximum(m, sk)
            p = jnp.exp(sc - m)
            den = jnp.sum(p, axis=-1, keepdims=True)
            if has_sink:
                den = den + jnp.exp(sk - m)
            out = lax.dot_general(p.astype(BF16), vw_, (((1,), (0,)), ((), ())), preferred_element_type=F32)
            return rows, out / den, jnp.broadcast_to(m + jnp.log(den), (g.q_rows, HEAD_DIM))

        def step(i, carry):
            for rows, out, lse in [chain(i * n_chains + u) for u in range(n_chains)]:
                os_[rows, :] = out
                ls[rows, :] = lse
            return carry

        lax.fori_loop(0, g.n_items // n_chains, step, 0)
        if g.staged:
            g.unstage(o_ref, os_)
            g.unstage(l_ref, ls)

    def col(c0, per):
        return pl.BlockSpec((s, HEAD_DIM), lambda h: (0, c0 // LANES + h // per))

    in_specs = [col(cq, 1), col(ck, group), col(cv, group),
                pl.BlockSpec((None, g.q_rows, g.win), lambda h: (h, 0, 0))]
    args = [proj, proj, proj, bias]
    if has_sink:
        in_specs.append(pl.BlockSpec(memory_space=pltpu.SMEM))
        args.append(sink)
    shape = jax.ShapeDtypeStruct((s, nh * HEAD_DIM), F32)
    scratch = [pltpu.VMEM((dil * g.pad, HEAD_DIM), BF16), pltpu.VMEM((dil * g.pad, HEAD_DIM), BF16)]
    if g.staged:
        scratch += [pltpu.VMEM((s, HEAD_DIM), BF16), pltpu.VMEM((s, HEAD_DIM), F32), pltpu.VMEM((s, HEAD_DIM), F32)]
    return pl.pallas_call(
        body, name=name, grid=(nh,), in_specs=in_specs, out_specs=[col(0, 1), col(0, 1)], out_shape=[shape, shape],
        scratch_shapes=scratch, compiler_params=_params(("parallel",), 16 * s * HEAD_DIM * 4))(*args)


def band_attn_bwd(name, proj, bias, sink, dout, out, lse, dlse, *, half, q_rows, n_chains, dil, nh, group, cq, ck, cv):
    s, w = proj.shape
    g = _Band(s, half, q_rows, n_chains, dil)
    nkv = nh // group
    has_sink = sink is not None
    has_dl = dlse is not None
    n_in = 7 + int(has_sink) + int(has_dl)
    n_out = 4 + int(has_sink)

    def body(*refs):
        ins, outs, scr = refs[:n_in], refs[n_in:n_in + n_out], refs[n_in + n_out:]
        q_ref, k_ref, v_ref, b_ref, do_ref, o_ref, l_ref = ins[:7]
        s_ref = ins[7] if has_sink else None
        dl_ref = ins[n_in - 1] if has_dl else None
        dq_ref, dk_ref, dv_ref, db_ref = outs[:4]
        ks, vs, dks, dvs = scr[:4]
        scr = list(scr[4:])
        dsa = scr.pop(0) if has_sink else None
        if g.staged:
            qs, dos, os_, ls, dqs = scr[:5]
            dls = scr[5] if has_dl else None
            g.stage(qs, q_ref)
            g.stage(dos, do_ref)
            g.stage(os_, o_ref)
            g.stage(ls, l_ref)
            if has_dl:
                g.stage(dls, dl_ref)
        else:
            qs, dos, os_, ls, dqs, dls = None, do_ref, o_ref, l_ref, dq_ref, dl_ref
        h = pl.program_id(0)
        g.stage_kv(ks, k_ref)
        g.stage_kv(vs, v_ref)
        dks[...] = jnp.zeros_like(dks)
        dvs[...] = jnp.zeros_like(dvs)
        db_ref[...] = jnp.zeros_like(db_ref)
        bias_v = b_ref[...]
        if has_sink:
            sk = s_ref[h]
            dsa[...] = jnp.zeros_like(dsa)

        def chain(item):
            qoff, koff, edge = g.offsets(item)
            rows = pl.ds(qoff, g.q_rows)
            win = pl.ds(koff, g.win)
            qv = qs[rows, :] if g.staged else q_ref[rows, :].astype(BF16)
            kw_ = ks[win, :]
            vw_ = vs[win, :]
            sc = lax.dot_general(qv, kw_, (((1,), (1,)), ((), ())), preferred_element_type=F32) * ATTN_SCALE
            lv = ls[rows, :][:, 0:1]
            p = jnp.exp(sc + bias_v + edge - lv)
            dov = dos[rows, :]
            delta = jnp.sum(dov * os_[rows, :], axis=-1, keepdims=True)
            dob = dov.astype(BF16)
            dp = lax.dot_general(dob, vw_, (((1,), (1,)), ((), ())), preferred_element_type=F32)
            t = dp - delta
            if has_dl:
                t = t + dls[rows, :][:, 0:1]
            ds = p * t
            dsb = (ds * ATTN_SCALE).astype(BF16)
            dq = lax.dot_general(dsb, kw_, (((1,), (0,)), ((), ())), preferred_element_type=F32)
            dkc = lax.dot_general(dsb, qv, (((0,), (0,)), ((), ())), preferred_element_type=F32)
            dvc = lax.dot_general(p.astype(BF16), dob, (((0,), (0,)), ((), ())), preferred_element_type=F32)
            dsk = jnp.exp(sk - lv) * delta if has_sink else None
            return rows, win, dq, dkc, dvc, ds, dsk

        def step(i, carry):
            res = [chain(i * n_chains + u) for u in range(n_chains)]
            ds_sum = res[0][5]
            for rr in res[1:]:
                ds_sum = ds_sum + rr[5]
            db_ref[...] += ds_sum
            for rows, win, dq, dkc, dvc, ds, dsk in res:
                dqs[rows, :] = dq
                dks[win, :] += dkc
                dvs[win, :] += dvc
                if has_sink:
                    dsa[...] += dsk
            return carry

        lax.fori_loop(0, g.n_items // n_chains, step, 0)

        if g.staged:
            g.unstage(dq_ref, dqs)

        def emit_kv(add):
            for r in range(dil):
                lo = r * g.pad + half
                for dst_ref, src in ((dk_ref, dks), (dv_ref, dvs)):
                    val = src[lo:lo + g.seg, :]
                    if add:
                        val = val + dst_ref[g.rows_of(r), :]
                    dst_ref[g.rows_of(r), :] = val

        if group == 1:
            emit_kv(False)
        else:
            @pl.when(h % group == 0)
            def _():
                emit_kv(False)

            @pl.when(h % group != 0)
            def _():
                emit_kv(True)
        if has_sink:
            outs[4][...] = jnp.full((SUBLANES, LANES), -jnp.sum(dsa[...]), F32)

    def col(c0, per):
        return pl.BlockSpec((s, HEAD_DIM), lambda h: (0, c0 // LANES + h // per))

    b_spec = pl.BlockSpec((None, g.q_rows, g.win), lambda h: (h, 0, 0))
    in_specs = [col(cq, 1), col(ck, group), col(cv, group), b_spec, col(0, 1), col(0, 1), col(0, 1)]
    args = [proj, proj, proj, bias, dout, out, lse]
    if has_sink:
        in_specs.append(pl.BlockSpec(memory_space=pltpu.SMEM))
        args.append(sink)
    if has_dl:
        in_specs.append(col(0, 1))
        args.append(dlse)
    out_specs = [col(0, 1), col(0, group), col(0, group), b_spec]
    out_shape = [jax.ShapeDtypeStruct((s, nh * HEAD_DIM), F32), jax.ShapeDtypeStruct((s, nkv * HEAD_DIM), F32),
                 jax.ShapeDtypeStruct((s, nkv * HEAD_DIM), F32), jax.ShapeDtypeStruct((nh, g.q_rows, g.win), F32)]
    scratch = [pltpu.VMEM((dil * g.pad, HEAD_DIM), BF16), pltpu.VMEM((dil * g.pad, HEAD_DIM), BF16),
               pltpu.VMEM((dil * g.pad, HEAD_DIM), F32), pltpu.VMEM((dil * g.pad, HEAD_DIM), F32)]
    if has_sink:
        out_specs.append(pl.BlockSpec((None, SUBLANES, LANES), lambda h: (h, 0, 0)))
        out_shape.append(jax.ShapeDtypeStruct((nh, SUBLANES, LANES), F32))
        scratch.append(pltpu.VMEM((g.q_rows, 1), F32))
    if g.staged:
        scratch += [pltpu.VMEM((s, HEAD_DIM), BF16)] + [pltpu.VMEM((s, HEAD_DIM), F32)] * (4 + int(has_dl))
    res = pl.pallas_call(
        body, name=name, grid=(nh,), in_specs=in_specs, out_specs=out_specs, out_shape=out_shape,
        scratch_shapes=scratch, compiler_params=_params(("arbitrary",), 28 * s * HEAD_DIM * 4))(*args)
    return res[0], res[1], res[2], res[3], (res[4] if has_sink else None)


def dil_merge_fwd(name, outs, lses):
    s, w = outs[0].shape
    ts = _pick(s, ROW_TILE_CANDS)
    ng = len(outs)

    def body(*refs):
        o_refs, l_refs, y_ref = refs[:ng], refs[ng:2 * ng], refs[2 * ng]
        ls = [l[...] for l in l_refs]
        mx = ls[0]
        for l in ls[1:]:
            mx = jnp.maximum(mx, l)
        es = [jnp.exp(l - mx) for l in ls]
        tot = es[0]
        for e in es[1:]:
            tot = tot + e
        acc = (es[0] / tot) * o_refs[0][...]
        for e, o in zip(es[1:], o_refs[1:]):
            acc = acc + (e / tot) * o[...]
        y_ref[...] = acc.astype(y_ref.dtype)

    row = pl.BlockSpec((ts, w), lambda i: (i, 0))
    return pl.pallas_call(
        body, name=name, grid=(s // ts,), in_specs=[row] * (2 * ng), out_specs=row,
        out_shape=jax.ShapeDtypeStruct((s, w), BF16),
        compiler_params=_params(("parallel",), 10 * ts * w * 4))(*outs, *lses)


def dil_merge_bwd(name, dy, outs, lses):
    s, w = outs[0].shape
    ts = _pick(s, ROW_TILE_CANDS)
    ng = len(outs)
    nhead = w // HEAD_DIM

    def body(*refs):
        dy_ref = refs[0]
        o_refs, l_refs = refs[1:1 + ng], refs[1 + ng:1 + 2 * ng]
        do_refs, dl_refs = refs[1 + 2 * ng:1 + 3 * ng], refs[1 + 3 * ng:1 + 4 * ng]
        for hh in range(nhead):
            cols = slice(hh * HEAD_DIM, (hh + 1) * HEAD_DIM)
            dyv = dy_ref[:, cols]
            ls = [l[:, cols] for l in l_refs]
            mx = ls[0]
            for l in ls[1:]:
                mx = jnp.maximum(mx, l)
            es = [jnp.exp(l - mx) for l in ls]
            tot = es[0]
            for e in es[1:]:
                tot = tot + e
            alphas = [e / tot for e in es]
            dal = [jnp.broadcast_to(jnp.sum(dyv * o[:, cols], axis=-1, keepdims=True), dyv.shape) for o in o_refs]
            mean = alphas[0] * dal[0]
            for a, d in zip(alphas[1:], dal[1:]):
                mean = mean + a * d
            for g in range(ng):
                do_refs[g][:, cols] = alphas[g] * dyv
                dl_refs[g][:, cols] = alphas[g] * (dal[g] - mean)

    row = pl.BlockSpec((ts, w), lambda i: (i, 0))
    shape = jax.ShapeDtypeStruct((s, w), F32)
    res = pl.pallas_call(
        body, name=name, grid=(s // ts,), in_specs=[row] * (1 + 2 * ng), out_specs=[row] * (2 * ng),
        out_shape=[shape] * (2 * ng),
        compiler_params=_params(("parallel",), 16 * ts * w * 4))(dy, *outs, *lses)
    return res[:ng], res[ng:]


def _adamw(w, g, m, v):
    m = ADAM_B1 * m + (1.0 - ADAM_B1) * g
    v = ADAM_B2 * v + (1.0 - ADAM_B2) * (g * g)
    m_hat = m / (1.0 - ADAM_B1 ** ADAM_STEP)
    v_hat = v / (1.0 - ADAM_B2 ** ADAM_STEP)
    delta = -ADAM_LR * (m_hat / (jnp.sqrt(v_hat) + ADAM_EPS) + ADAM_WD * w)
    return delta, m, v


def _row_tile(r, c, budget=1 << 20):
    if r * c * 4 <= budget or r % SUBLANES:
        return r
    for t in (1024, 512, 256, 128, 64, 32, 16, 8):
        if r % t == 0 and t * c * 4 <= budget:
            return t
    return SUBLANES


def adam_small(name, g, w, m, v):
    def body(g_ref, w_ref, m_ref, v_ref, d_ref, nm_ref, nv_ref):
        d_ref[...], nm_ref[...], nv_ref[...] = _adamw(w_ref[...], g_ref[...], m_ref[...], v_ref[...])

    shape = jax.ShapeDtypeStruct(w.shape, F32)
    return pl.pallas_call(body, name=name, out_shape=[shape, shape, shape])(g, w, m, v)


def reduce_adam(name, mine, theirs, w, m, v):
    nq, r, c = mine.shape
    tr = _row_tile(r, c)

    def body(*refs):
        parts, (w_ref, m_ref, v_ref, g_ref, d_ref, nm_ref, nv_ref) = refs[:nq], refs[nq:]
        g = parts[0][...].astype(F32)
        for p_ref in parts[1:]:
            g = g + p_ref[...].astype(F32)
        g_ref[...] = g
        d_ref[...], nm_ref[...], nv_ref[...] = _adamw(w_ref[...], g, m_ref[...], v_ref[...])

    def slot(q):
        return pl.BlockSpec((None, tr, c), lambda i: (q, i, 0))

    row = pl.BlockSpec((tr, c), lambda i: (i, 0))
    shape = jax.ShapeDtypeStruct((r, c), F32)
    return pl.pallas_call(
        body, name=name, grid=(r // tr,), in_specs=[slot(q) for q in range(nq)] + [row, row, row],
        out_specs=[row] * 4, out_shape=[shape] * 4,
        compiler_params=_params(("parallel",), (nq * 2 + 7 * 4) * tr * c))(mine, *[theirs] * (nq - 1), w, m, v)


def _place():
    return lax.axis_index("x"), lax.axis_index("y"), lax.axis_index("c")


def _flip(pos, bits):
    return tuple((1 - p) if b else p for p, b in zip(pos, bits))


def _index(pos):
    return 4 * pos[0] + 2 * pos[1] + pos[2]


ANY = pl.BlockSpec(memory_space=pl.ANY)


HBM = pl.BlockSpec(memory_space=pltpu.HBM)
SEM = pl.BlockSpec(memory_space=pltpu.SEMAPHORE)
EFFECT = pltpu.SideEffectType.DATAFLOW_SIDE_EFFECTING
TO_SIBLING = (0, 0, 1)
TO_CHIPS = [(1, 0, 0), (0, 1, 0), (1, 1, 0)]


def _in_hbm(a):
    return pltpu.with_memory_space_constraint(a, pltpu.HBM)


def _token_value(token):
    return token[0, 0]


def _when(pred, fn):
    if pred is True:
        fn()
    elif pred is not False:
        pl.when(pred)(fn)


def _plan_copy(k, entry, ins, lnd, send_sems, recv_sems):
    a, src_a, sblk, lblk, to, send_if, recv_if = entry
    src = lnd[a] if src_a is None else ins[src_a]
    return pltpu.make_async_remote_copy(
        src_ref=src.at[sblk], dst_ref=lnd[a].at[lblk], send_sem=send_sems.at[k], recv_sem=recv_sems.at[k],
        device_id=to, device_id_type=MESH), send_if, recv_if


def split_start(name, srcs, lands, plan, after):
    ns, nl = len(srcs), len(lands)
    n_copies = len(plan((0, 0, 0)))

    def body(*refs):
        ins, lnd = refs[:ns], refs[ns:ns + nl]
        send_sems, recv_sems = refs[ns + nl + 1], refs[ns + nl + 2]
        token = refs[-1]
        for k, entry in enumerate(plan(_place())):
            cp, send_if, _ = _plan_copy(k, entry, ins, lnd, send_sems, recv_sems)
            _when(send_if, cp.start)
        token[...] = jnp.zeros_like(token)

    outs = pl.pallas_call(
        body, name=name,
        out_shape=(pltpu.SemaphoreType.DMA((n_copies,)), pltpu.SemaphoreType.DMA((n_copies,)),
                   *[pltpu.HBM(a.shape, a.dtype) for a in srcs], *[pltpu.HBM(a.shape, a.dtype) for a in lands],
                   jax.ShapeDtypeStruct((SUBLANES, LANES), F32)),
        in_specs=[HBM] * (ns + nl) + [ANY],
        out_specs=(SEM, SEM, *[HBM] * (ns + nl), pl.BlockSpec(memory_space=pltpu.VMEM)),
        input_output_aliases={i: 2 + i for i in range(ns + nl)},
        compiler_params=pltpu.CompilerParams(has_side_effects=EFFECT),
    )(*[_in_hbm(a) for a in srcs], *[_in_hbm(a) for a in lands], after)
    return outs[0], outs[1], list(outs[2:2 + ns]), list(outs[2 + ns:2 + ns + nl]), outs[-1]


def split_wait(name, send_sems, recv_sems, srcs, lands, plan, after):
    ns, nl = len(srcs), len(lands)

    def body(*refs):
        ins, lnd = refs[:ns], refs[ns:ns + nl]
        s_sems, r_sems = refs[ns + nl], refs[ns + nl + 1]
        for k, entry in enumerate(plan(_place())):
            cp, send_if, recv_if = _plan_copy(k, entry, ins, lnd, s_sems, r_sems)
            _when(send_if, cp.wait_send)
            _when(recv_if, cp.wait_recv)

    outs = pl.pallas_call(
        body, name=name,
        out_shape=(*[pltpu.HBM(a.shape, a.dtype) for a in srcs], *[pltpu.HBM(a.shape, a.dtype) for a in lands]),
        in_specs=[HBM] * (ns + nl) + [SEM, SEM, ANY],
        out_specs=tuple([HBM] * (ns + nl)),
        input_output_aliases={i: i for i in range(ns + nl)},
        compiler_params=pltpu.CompilerParams(has_side_effects=EFFECT),
    )(*srcs, *lands, send_sems, recv_sems, after)
    return list(outs[:ns]), list(outs[ns:])


NORTH = 1


def ag_plan(n):
    def plan(me):
        x, y, c = me
        entries = []
        for a in range(n):
            for t in (NORTH, 1 - NORTH):
                blk = _index((x, y, t))
                for rel in TO_CHIPS:
                    entries.append((a, None, blk, blk, _flip((x, y, t), rel), c == NORTH, c == t))
        return entries
    return plan


def ag_pair(name, lands, after):
    n = len(lands)

    def body(*refs):
        lnd = refs[n + 1:2 * n + 1]
        send_sems, recv_sems = refs[2 * n + 1:]
        me = _place()
        sibling = _flip(me, TO_SIBLING)
        copies = []
        for a in range(n):
            mine, theirs = lnd[a].at[_index(me)], lnd[a].at[_index(sibling)]
            cp = pltpu.make_async_remote_copy(src_ref=mine, dst_ref=mine, send_sem=send_sems.at[a],
                                              recv_sem=recv_sems.at[a], device_id=sibling, device_id_type=MESH)
            cp.start()
            copies.append((cp, pltpu.make_async_remote_copy(
                src_ref=mine, dst_ref=theirs, send_sem=send_sems.at[a], recv_sem=recv_sems.at[a], device_id=sibling,
                device_id_type=MESH)))
        for cp, arrival in copies:
            arrival.wait_recv()
        for cp, arrival in copies:
            cp.wait_send()

    return pl.pallas_call(
        body, name=name, in_specs=[ANY] * (n + 1), out_specs=[ANY] * n,
        out_shape=[jax.ShapeDtypeStruct(l.shape, l.dtype) for l in lands],
        input_output_aliases={a: a for a in range(n)},
        scratch_shapes=[pltpu.SemaphoreType.DMA((n,)), pltpu.SemaphoreType.DMA((n,))],
    )(*lands, after)


def ag_start(name, lands, after):
    return split_start(name, [], lands, ag_plan(len(lands)), after)


def ag_finish(name, lands):
    n = len(lands)

    def body(*refs):
        lnd = refs[n:2 * n]
        send_sems, recv_sems = refs[2 * n:]
        me = _place()
        sibling = _flip(me, TO_SIBLING)
        copies = []
        for a in range(n):
            for j, rel in enumerate(TO_CHIPS):
                blk = lnd[a].at[_index(_flip(me, rel))]
                there = lnd[a].at[_index(_flip(sibling, rel))]
                cp = pltpu.make_async_remote_copy(
                    src_ref=blk, dst_ref=blk, send_sem=send_sems.at[a * 3 + j], recv_sem=recv_sems.at[a * 3 + j],
                    device_id=sibling, device_id_type=MESH)
                cp.start()
                copies.append((cp, pltpu.make_async_remote_copy(
                    src_ref=blk, dst_ref=there, send_sem=send_sems.at[a * 3 + j], recv_sem=recv_sems.at[a * 3 + j],
                    device_id=sibling, device_id_type=MESH)))
        for cp, arrival in copies:
            arrival.wait_recv()
        for cp, arrival in copies:
            cp.wait_send()

    return pl.pallas_call(
        body, name=name, in_specs=[ANY] * n, out_specs=[ANY] * n,
        out_shape=[jax.ShapeDtypeStruct(l.shape, l.dtype) for l in lands],
        input_output_aliases={a: a for a in range(n)},
        scratch_shapes=[pltpu.SemaphoreType.DMA((3 * n,)), pltpu.SemaphoreType.DMA((3 * n,))],
    )(*lands)


REL = [(b >> 2 & 1, b >> 1 & 1, b & 1) for b in range(N_DEV)]


CHIP_REL = [(0, 0, 0)] + TO_CHIPS
N_CHIPS = len(CHIP_REL)


def rs_pair(name, parts):
    n = len(parts)

    def body(*refs):
        ins, got = refs[:n], refs[n:2 * n]
        send_sems, recv_sems = refs[2 * n:]
        me = _place()
        sibling = _flip(me, TO_SIBLING)
        remote = []
        for a in range(n):
            for q, rel in enumerate(CHIP_REL):
                k = a * N_CHIPS + q
                cp = pltpu.make_async_remote_copy(
                    src_ref=ins[a].at[_index(_flip(sibling, rel))], dst_ref=got[a].at[q], send_sem=send_sems.at[k],
                    recv_sem=recv_sems.at[k], device_id=sibling, device_id_type=MESH)
                cp.start()
                remote.append(cp)
        for cp in remote:
            cp.wait_recv()
        for cp in remote:
            cp.wait_send()

    shapes = [jax.ShapeDtypeStruct((N_CHIPS,) + tuple(p.shape[1:]), p.dtype) for p in parts]
    res = pl.pallas_call(
        body, name=name, in_specs=[ANY] * n, out_specs=[ANY] * n, out_shape=shapes,
        scratch_shapes=[pltpu.SemaphoreType.DMA((N_CHIPS * n,)), pltpu.SemaphoreType.DMA((N_CHIPS * n,))],
    )(*parts)
    return list(res)


def own_blocks():
    me = _place()
    return jnp.stack([_index(_flip(me, rel)) for rel in CHIP_REL]).astype(jnp.int32)


def pair_add(name, blocks, parts, got):
    nq, r, c = got.shape
    tr = _row_tile(r, c)

    def body(blk_ref, a_ref, b_ref, o_ref):
        o_ref[...] = (a_ref[...].astype(F32) + b_ref[...].astype(F32)).astype(o_ref.dtype)

    spec = pl.BlockSpec((None, tr, c), lambda q, i, blk: (q, i, 0))
    return pl.pallas_call(
        body, name=name,
        grid_spec=pltpu.PrefetchScalarGridSpec(
            num_scalar_prefetch=1, grid=(nq, r // tr),
            in_specs=[pl.BlockSpec((None, tr, c), lambda q, i, blk: (blk[q], i, 0)), spec], out_specs=spec),
        out_shape=jax.ShapeDtypeStruct(got.shape, got.dtype),
        compiler_params=_params(("arbitrary", "arbitrary"), 6 * tr * c * 2))(blocks, parts, got)


def rs_plan(n):
    def plan(me):
        return [(a, a, q, q, _flip(me, CHIP_REL[q]), True, True) for a in range(n) for q in range(1, N_CHIPS)]
    return plan


def rs_start(name, sums, after):
    lands = [lax.empty(t.shape, t.dtype) for t in sums]
    return split_start(name, sums, lands, rs_plan(len(sums)), after)


def allreduce_small(name, pack, after):
    rows, lanes = pack.shape

    def body(x_ref, after_ref, o_ref, land, send_sems, recv_sems):
        me = _place()
        idx = _index(me)
        land[idx] = x_ref[...]
        copies = []
        for r in range(1, N_DEV):
            peer = _flip(me, REL[r])
            cp = pltpu.make_async_remote_copy(
                src_ref=x_ref, dst_ref=land.at[idx], send_sem=send_sems.at[r - 1], recv_sem=recv_sems.at[r - 1],
                device_id=peer, device_id_type=MESH)
            cp.start()
            copies.append(cp)
        for cp in copies:
            cp.wait_recv()
        for cp in copies:
            cp.wait_send()
        acc = land[0]
        for i in range(1, N_DEV):
            acc = acc + land[i]
        o_ref[...] = acc

    return pl.pallas_call(
        body, name=name, in_specs=[pl.BlockSpec(memory_space=pltpu.VMEM), ANY],
        out_specs=pl.BlockSpec(memory_space=pltpu.VMEM), out_shape=jax.ShapeDtypeStruct((rows, lanes), F32),
        scratch_shapes=[pltpu.VMEM((N_DEV, rows, lanes), F32), pltpu.SemaphoreType.DMA((7,)),
                        pltpu.SemaphoreType.DMA((7,))],
    )(pack, after)


def _pad_rows(a, rows):
    return jnp.pad(a, ((0, rows - a.shape[0]), (0, 0)))


def _as_tiles(vec):
    n = vec.shape[0]
    rows = -(-n // LANES)
    rows = -(-rows // SUBLANES) * SUBLANES
    return jnp.pad(vec, (0, rows * LANES - n)).reshape(rows, LANES)


def kernel(x, p, rel_bias_table, attn_norm, w_in, sink_a, w_branch_a, w_branch_b, w_out, ffn_norm, w_ffn_gate, w_ffn_up, conv_w, conv_b, w_ffn_down, ple_norm, w_ple_gate, w_ple_proj, final_norm, loss_target, m_rel_bias_table, m_attn_norm, m_w_in, m_sink_a, m_w_branch_a, m_w_branch_b, m_w_out, m_ffn_norm, m_w_ffn_gate, m_w_ffn_up, m_conv_w, m_conv_b, m_w_ffn_down, m_ple_norm, m_w_ple_gate, m_w_ple_proj, m_final_norm, v_rel_bias_table, v_attn_norm, v_w_in, v_sink_a, v_w_branch_a, v_w_branch_b, v_w_out, v_ffn_norm, v_w_ffn_gate, v_w_ffn_up, v_conv_w, v_conv_b, v_w_ffn_down, v_ple_norm, v_w_ple_gate, v_w_ple_proj, v_final_norm):
    xs = x[0]
    s, d = xs.shape
    ps = p[0, 0]
    target = loss_target[0]
    me = 4 * lax.axis_index("x") + 2 * lax.axis_index("y") + lax.axis_index("c")

    big = dict(w_in=w_in[0], w_branch_a=w_branch_a[0], w_branch_b=w_branch_b[0], w_out=w_out[0],
               w_ffn_gate=w_ffn_gate[0], w_ffn_up=w_ffn_up[0], w_ffn_down=w_ffn_down[0],
               w_ple_gate=w_ple_gate[0], w_ple_proj=w_ple_proj[0])
    big_m = dict(w_in=m_w_in[0], w_branch_a=m_w_branch_a[0], w_branch_b=m_w_branch_b[0], w_out=m_w_out[0],
                 w_ffn_gate=m_w_ffn_gate[0], w_ffn_up=m_w_ffn_up[0], w_ffn_down=m_w_ffn_down[0],
                 w_ple_gate=m_w_ple_gate[0], w_ple_proj=m_w_ple_proj[0])
    big_v = dict(w_in=v_w_in[0], w_branch_a=v_w_branch_a[0], w_branch_b=v_w_branch_b[0], w_out=v_w_out[0],
                 w_ffn_gate=v_w_ffn_gate[0], w_ffn_up=v_w_ffn_up[0], w_ffn_down=v_w_ffn_down[0],
                 w_ple_gate=v_w_ple_gate[0], w_ple_proj=v_w_ple_proj[0])
    names = list(big)
    nf = big["w_ffn_gate"].shape[1]

    shards = {k: big[k].astype(BF16) for k in names}
    shards["conv_w"] = _pad_rows(conv_w[0], SUBLANES)
    ag_groups = [["w_in"], ["w_branch_a", "w_branch_b", "w_out"], ["w_ffn_gate", "w_ffn_up", "conv_w"],
                 ["w_ffn_down", "w_ple_gate", "w_ple_proj"]]
    ag_started = {}
    wg = {}

    def start_gather(gi, after):
        lands = [lax.dynamic_update_index_in_dim(lax.empty((N_DEV,) + shards[k].shape, shards[k].dtype), shards[k],
                                                 me, 0) for k in ag_groups[gi]]
        lands = ag_pair(f"ag_pair{gi}", lands, after)
        s_sems, r_sems, _, lands, token = ag_start(f"ag_start{gi}", lands, after)
        ag_started[gi] = (s_sems, r_sems, lands)
        return token

    def gather(gi, after):
        s_sems, r_sems, lands = ag_started[gi]
        _, lands = split_wait(f"ag_wait{gi}", s_sems, r_sems, [], lands, ag_plan(len(lands)), after)
        wg.update(zip(ag_groups[gi], ag_finish(f"ag_finish{gi}", lands)))

    cb = conv_b.reshape(N_DEV, 1, nf)

    table_t = rel_bias_table.T
    geo_a = dict(half=A_BLOCK, q_rows=ATTN_Q_ROWS, n_chains=ATTN_CHAINS, dil=1, nh=A_Q_HEADS, group=A_GROUP,
                 cq=COL_QA, ck=COL_KA, cv=COL_VA)
    geo_b = [dict(half=B_BLOCK, q_rows=min(ATTN_Q_ROWS, s // dil), n_chains=ATTN_CHAINS, dil=dil,
                  nh=B_HEADS_PER_GROUP, group=1, cq=COL_QB + g * B_OUT_W, ck=COL_KB + g * B_OUT_W,
                  cv=COL_VB + g * B_OUT_W) for g, (_, dil) in enumerate(B_PATTERNS)]
    bucket_a = bucket_tile(geo_a["q_rows"], A_BLOCK, 1)
    bias_a = bias_build("bias_a", table_t, bucket_a, 0, A_Q_HEADS, A_BLOCK)
    buckets_b = [bucket_tile(gb["q_rows"], B_BLOCK, gb["dil"]) for gb in geo_b]
    biases_b = [bias_build(f"bias_b{g}", table_t, buckets_b[g], A_Q_HEADS + g * B_HEADS_PER_GROUP, B_HEADS_PER_GROUP,
                           B_BLOCK) for g in range(len(B_PATTERNS))]

    token = start_gather(1, start_gather(0, xs))
    h = rms_fwd("rms_attn", xs, attn_norm + _token_value(token))
    gather(0, h)
    token = start_gather(2, wg["w_in"])
    proj = mm_cols("proj_in", h, wg["w_in"], F32, fold=True, after=token)
    sink = sink_a[0]
    ya, lse_a = band_attn_fwd("attn_a_fwd", proj, bias_a, sink, **geo_a)
    outs_b, lses_b = [], []
    for g in range(len(B_PATTERNS)):
        o, l = band_attn_fwd(f"attn_b{g}_fwd", proj, biases_b[g], None, **geo_b[g])
        outs_b.append(o)
        lses_b.append(l)
    yb = dil_merge_fwd("dil_merge_fwd", outs_b, lses_b)
    gather(1, yb)
    token = start_gather(3, wg["w_out"])
    w_out_full = wg["w_out"].reshape(d, d)
    ta = mm_cols("branch_a", ya, wg["w_branch_a"], F32, fold=True, after=token)
    tb = mm_cols("branch_b", yb, wg["w_branch_b"], F32, fold=True)
    merged = gate_merge_fwd("gate_merge_fwd", proj, ta, tb, d)
    x1 = mm_plain("mix_out", merged, w_out_full, F32, res=xs)

    hf = rms_fwd("rms_ffn", x1, ffn_norm)
    gather(2, hf)
    cw = wg["conv_w"]
    gpre = mm_cols("ffn_gate", hf, wg["w_ffn_gate"], F32, fold=False)
    u = mm_cols("ffn_up", hf, wg["w_ffn_up"], F32, fold=False)
    z = ffn_mid_fwd("ffn_mid_fwd", gpre, u, cw, cb)
    gather(3, z)
    w_pg_full = wg["w_ple_gate"].reshape(d, d)
    x2 = mm_jsum("ffn_down", z, wg["w_ffn_down"], F32, res=x1)

    hp = rms_fwd("rms_ple", x2, ple_norm)
    lp = mm_plain("ple_gate", hp, w_pg_full, F32)
    pp = mm_cols("ple_proj", ps, wg["w_ple_proj"], F32, fold=True)
    loss_part, dx3, dlp, dpp, d_final = tail_fwd_bwd("tail", x2, lp, pp, final_norm.reshape(1, d), target)

    grads = {}
    rs_started = []
    blocks = own_blocks()

    def scatter(tag, keys):
        got = rs_pair(f"rs_pair_{tag}", [grads[k] for k in keys])
        sums = [pair_add(f"pair_add_{k}", blocks, grads[k], g) for k, g in zip(keys, got)]
        s_sems, r_sems, srcs, lands, token = rs_start(f"rs_start_{tag}", sums, blocks)
        rs_started.append((tag, keys, s_sems, r_sems, srcs, lands))
        return _token_value(token)

    grads["w_ple_proj"] = mm_tn_cols("d_w_ple_proj", ps, dpp, N_DEV, big["w_ple_proj"].shape[1], BF16, folded=True)
    grads["w_ple_gate"] = mm_tn_plain("d_w_ple_gate", hp, dlp, BF16).reshape(N_DEV, d // N_DEV, d)
    tok = scatter("ple", ["w_ple_proj", "w_ple_gate"])
    dhp = mm_nt_plain("d_hp", dlp, w_pg_full, F32)
    dx2, d_ple = rms_bwd("rms_ple_bwd", x2, ple_norm + tok, dhp, dx3)

    dz = mm_nt_j("d_z", dx2, wg["w_ffn_down"], BF16)
    grads["w_ffn_down"] = mm_tn_j("d_w_ffn_down", z, dx2, BF16)
    tok = scatter("down", ["w_ffn_down"])
    dg, du, dcw = ffn_mid_bwd1("ffn_mid_bwd1", gpre, u, dz, cw, cb + tok)
    dgpre = ffn_mid_bwd2("ffn_mid_bwd2", dg, cw)
    grads["w_ffn_up"] = mm_tn_cols("d_w_ffn_up", hf, du, N_DEV, nf, BF16, folded=False)
    grads["w_ffn_gate"] = mm_tn_cols("d_w_ffn_gate", hf, dgpre, N_DEV, nf, BF16, folded=False)
    tok = scatter("upgate", ["w_ffn_up", "w_ffn_gate"])
    dhf = mm_nt_jsum("d_hf_up", du, wg["w_ffn_up"], F32, folded=False)
    dhf = mm_nt_jsum("d_hf_gate", dgpre, wg["w_ffn_gate"], F32, folded=False, res=dhf)
    dx1, d_ffn = rms_bwd("rms_ffn_bwd", x1, ffn_norm + tok, dhf, dx2)

    dmerged = mm_nt_plain("d_merged", dx1, w_out_full, F32)
    grads["w_out"] = mm_tn_plain("d_w_out", merged, dx1, BF16).reshape(N_DEV, d // N_DEV, d)
    dta, dtb, dga, dgb = gate_merge_bwd("gate_merge_bwd", dmerged, proj, ta, tb, d)
    grads["w_branch_a"] = mm_tn_cols("d_w_branch_a", ya, dta, N_DEV, big["w_branch_a"].shape[1], BF16, folded=True)
    grads["w_branch_b"] = mm_tn_cols("d_w_branch_b", yb, dtb, N_DEV, big["w_branch_b"].shape[1], BF16, folded=True)
    tok = scatter("mix", ["w_out", "w_branch_a", "w_branch_b"])
    dya = mm_nt_jsum("d_ya", dta, wg["w_branch_a"], F32, folded=True)
    dyb = mm_nt_jsum("d_yb", dtb, wg["w_branch_b"], F32, folded=True)
    dqa, dka, dva, dbias_a, dsink = band_attn_bwd("attn_a_bwd", proj, bias_a, sink + tok, dya, ya, lse_a, None, **geo_a)
    douts_b, dlses_b = dil_merge_bwd("dil_merge_bwd", dyb, outs_b, lses_b)
    dq_b, dk_b, dv_b, dbias_b = [], [], [], []
    for g in range(len(B_PATTERNS)):
        dq, dk, dv, db, _ = band_attn_bwd(f"attn_b{g}_bwd", proj, biases_b[g], None, douts_b[g], outs_b[g], lses_b[g],
                                          dlses_b[g], **geo_b[g])
        dq_b.append(dq)
        dk_b.append(dk)
        dv_b.append(dv)
        dbias_b.append(db)
    dproj = jnp.concatenate([t.astype(BF16) for t in [dqa, dka, dva] + dq_b + dk_b + dv_b + [dga, dgb]], axis=1)
    grads["w_in"] = mm_tn_cols("d_w_in", h, dproj, N_DEV, big["w_in"].shape[1], BF16, folded=True)
    tok = scatter("in", ["w_in"])
    dh = mm_nt_jsum("d_h", dproj, wg["w_in"], F32, folded=True)
    grad_x, d_attn = rms_bwd("rms_attn_bwd", xs, attn_norm + tok, dh, dx1)

    dt_a = table_grad("table_grad_a", dbias_a, bucket_a)[:, 0, :N_BUCKETS]
    dt_b = [table_grad(f"table_grad_b{g}", dbias_b[g], buckets_b[g])[:, 0, :N_BUCKETS] for g in range(len(B_PATTERNS))]
    d_table_part = jnp.concatenate([dt_a] + dt_b, axis=0).T

    pieces = [
        ("loss", loss_part[0, :1]),
        ("table", d_table_part.reshape(-1)),
        ("attn_norm", d_attn.reshape(-1)),
        ("sink", dsink[:, 0, 0]),
        ("ffn_norm", d_ffn.reshape(-1)),
        ("conv_w", dcw[:, 0:3, :].reshape(-1)),
        ("conv_b", dcw[:, 3, :].reshape(-1)),
        ("ple_norm", d_ple.reshape(-1)),
        ("final_norm", d_final.reshape(-1)),
    ]
    tiles = [_as_tiles(v) for _, v in pieces]
    pack = jnp.concatenate(tiles, axis=0)

    out_g, out_d, out_m, out_v = {}, {}, {}, {}

    def finish(group, after):
        tag, keys, s_sems, r_sems, srcs, lands = group
        srcs, lands = split_wait(f"rs_wait_{tag}", s_sems, r_sems, srcs, lands, rs_plan(len(keys)), after)
        for k, mine, theirs in zip(keys, srcs, lands):
            g, dl, nm, nv = reduce_adam("adam_" + k, mine, theirs, big[k], big_m[k], big_v[k])
            out_g[k], out_d[k], out_m[k], out_v[k] = g[None], dl[None], nm[None], nv[None]
            after = dl
        return after

    after = pack
    for group in rs_started[:-1]:
        after = finish(group, after)
    total = allreduce_small("allreduce_small", pack, after)
    finish(rs_started[-1], total)
    small = {}
    row = 0
    for (nm, v), t in zip(pieces, tiles):
        small[nm] = total[row:row + t.shape[0]].reshape(-1)[:v.shape[0]]
        row += t.shape[0]
    loss = small["loss"][0]
    g_small = dict(
        rel_bias_table=small["table"].reshape(rel_bias_table.shape),
        attn_norm=small["attn_norm"].reshape(attn_norm.shape),
        sink_a=small["sink"].reshape(sink_a.shape),
        ffn_norm=small["ffn_norm"].reshape(ffn_norm.shape),
        conv_w=lax.dynamic_index_in_dim(small["conv_w"].reshape(N_DEV, 3, nf), me, 0, keepdims=False)[None],
        conv_b=small["conv_b"].reshape(conv_b.shape),
        ple_norm=small["ple_norm"].reshape(ple_norm.shape),
        final_norm=small["final_norm"].reshape(1, d),
    )
    w_small = dict(rel_bias_table=(rel_bias_table, m_rel_bias_table, v_rel_bias_table),
                   attn_norm=(attn_norm, m_attn_norm, v_attn_norm), sink_a=(sink_a, m_sink_a, v_sink_a),
                   ffn_norm=(ffn_norm, m_ffn_norm, v_ffn_norm), conv_w=(conv_w, m_conv_w, v_conv_w),
                   conv_b=(conv_b, m_conv_b, v_conv_b), ple_norm=(ple_norm, m_ple_norm, v_ple_norm),
                   final_norm=(final_norm, m_final_norm, v_final_norm))

    for k, (wv, mv, vv) in w_small.items():
        shape = wv.shape
        two_d = (1, shape[0]) if len(shape) == 1 else ((shape[0] * shape[1], shape[2]) if len(shape) == 3 else shape)
        gk = g_small[k].reshape(two_d)
        dl, nm, nv = adam_small("adam_" + k, gk, wv.reshape(two_d), mv.reshape(two_d), vv.reshape(two_d))
        out_g[k], out_d[k], out_m[k], out_v[k] = gk.reshape(shape), dl.reshape(shape), nm.reshape(shape), nv.reshape(shape)

    order = ["rel_bias_table", "attn_norm", "w_in", "sink_a", "w_branch_a", "w_branch_b", "w_out", "ffn_norm",
             "w_ffn_gate", "w_ffn_up", "conv_w", "conv_b", "w_ffn_down", "ple_norm", "w_ple_gate", "w_ple_proj",
             "final_norm"]
    return (loss, grad_x[None], *[out_g[k] for k in order], *[out_d[k] for k in order],
            *[out_m[k] for k in order], *[out_v[k] for k in order])
```

```python
import math

import jax
import jax.numpy as jnp
from jax import lax
from jax.experimental import pallas as pl
from jax.experimental.pallas import tpu as pltpu

F32 = jnp.float32
BF16 = jnp.bfloat16
MESH = pl.DeviceIdType.MESH
N_DEV = 8

HEAD_DIM = 128
A_Q_HEADS = 8
A_KV_HEADS = 2
A_GROUP = A_Q_HEADS // A_KV_HEADS
A_BLOCK = 128
B_PATTERNS = ((128, 1), (512, 4), (2048, 16))
B_HEADS_PER_GROUP = 4
B_HEADS = len(B_PATTERNS) * B_HEADS_PER_GROUP
B_BLOCK = 64
N_BUCKETS = 32
MAX_DISTANCE = 1024
A_Q_W = A_Q_HEADS * HEAD_DIM
A_KV_W = A_KV_HEADS * HEAD_DIM
B_W = B_HEADS * HEAD_DIM
B_OUT_W = B_HEADS_PER_GROUP * HEAD_DIM
COL_QA = 0
COL_KA = COL_QA + A_Q_W
COL_VA = COL_KA + A_KV_W
COL_QB = COL_VA + A_KV_W
COL_KB = COL_QB + B_W
COL_VB = COL_KB + B_W
COL_GATES = COL_VB + B_W
RMS_EPS = 1e-6
NEG_INF = -1e30
ATTN_SCALE = HEAD_DIM ** -0.5
ATTN_Q_ROWS = 256
ATTN_CHAINS = 2

ADAM_LR = 0.001
ADAM_B1 = 0.9
ADAM_B2 = 0.999
ADAM_EPS = 1e-08
ADAM_WD = 0.01
ADAM_STEP = 10

GELU_C = math.sqrt(2.0 / math.pi)
GELU_A = 0.044715

V7X_VMEM_BYTES = 64 * 1024 * 1024
VMEM_CEILING = V7X_VMEM_BYTES - 8 * 1024 * 1024
LANES = 128
SUBLANES = 8


def _pick(n, cands):
    for c in cands:
        if n % c == 0:
            return c
    return n


def _nbytes(shape, dtype):
    n = 1
    for d in shape:
        if d is not None:
            n *= d
    return n * jnp.dtype(dtype).itemsize


def _params(sem, est_bytes):
    limit = int(min(VMEM_CEILING, max(32 * 1024 * 1024, 2 * est_bytes + (8 << 20))))
    return pltpu.CompilerParams(dimension_semantics=sem, vmem_limit_bytes=limit)


def _mm(name, a, b, a_bs, a_im, b_bs, b_im, out_shape, out_dtype, o_bs, o_im, grid, dims,
        res=None, r_bs=None, r_im=None, after=None):
    nk = grid[-1]
    nax = len(grid)
    has_res = res is not None
    has_after = after is not None
    o_tile = tuple(d for d in o_bs if d is not None)

    def body(*refs):
        a_ref, b_ref = refs[:2]
        r_ref = refs[2] if has_res else None
        n_in = 2 + has_res + has_after
        o_ref = refs[n_in]
        rest = refs[n_in + 1:]

        def prod():
            return lax.dot_general(a_ref[...].astype(BF16), b_ref[...].astype(BF16), (dims, ((), ())),
                                   preferred_element_type=F32)

        def finish(r):
            if r_ref is not None:
                r = r + r_ref[...].astype(F32)
            o_ref[...] = r.astype(o_ref.dtype)

        if nk == 1:
            finish(prod())
        else:
            acc = rest[0]
            k = pl.program_id(nax - 1)

            @pl.when(k == 0)
            def _():
                acc[...] = prod()

            @pl.when(k > 0)
            def _():
                acc[...] += prod()

            @pl.when(k == nk - 1)
            def _():
                finish(acc[...])

    in_specs = [pl.BlockSpec(a_bs, a_im), pl.BlockSpec(b_bs, b_im)]
    args = [a, b]
    est = _nbytes(a_bs, a.dtype) + _nbytes(b_bs, b.dtype) + _nbytes(o_bs, out_dtype) + 2 * _nbytes(o_tile, F32)
    if has_res:
        in_specs.append(pl.BlockSpec(r_bs, r_im))
        args.append(res)
        est += _nbytes(r_bs, res.dtype)
    if has_after:
        in_specs.append(pl.BlockSpec(memory_space=pl.ANY))
        args.append(after)
    scratch = [] if nk == 1 else [pltpu.VMEM(o_tile, F32)]
    sem = ("parallel",) * (nax - 1) + ("arbitrary",)
    return pl.pallas_call(
        body, name=name, grid=grid, in_specs=in_specs, out_specs=pl.BlockSpec(o_bs, o_im),
        out_shape=jax.ShapeDtypeStruct(out_shape, out_dtype), scratch_shapes=scratch,
        compiler_params=_params(sem, est))(*args)


TM_CANDS = (1024, 512, 256, 128, 64, 32, 16, 8)
TK_CANDS = (1024, 512, 256, 128)
TN_CANDS = (1024, 512, 256, 128)


def mm_cols(name, a, wg, out_dtype, fold, after=None):
    m, k = a.shape
    nj, _, n = wg.shape
    tm, tk = _pick(m, TM_CANDS), _pick(k, TK_CANDS)
    grid = (nj, m // tm, k // tk)
    if fold:
        shape, o_bs, o_im = (m, nj * n), (tm, n), (lambda j, i, kk: (i, j))
    else:
        shape, o_bs, o_im = (nj, m, n), (None, tm, n), (lambda j, i, kk: (j, i, 0))
    return _mm(name, a, wg, (tm, tk), lambda j, i, kk: (i, kk), (None, tk, n), lambda j, i, kk: (j, kk, 0),
               shape, out_dtype, o_bs, o_im, grid, ((1,), (0,)), after=after)


def mm_plain(name, a, w, out_dtype, res=None):
    m, k = a.shape
    n = w.shape[1]
    tm, tk, tn = _pick(m, TM_CANDS), _pick(k, TK_CANDS), _pick(n, TN_CANDS)
    grid = (n // tn, m // tm, k // tk)
    return _mm(name, a, w, (tm, tk), lambda j, i, kk: (i, kk), (tk, tn), lambda j, i, kk: (kk, j),
               (m, n), out_dtype, (tm, tn), lambda j, i, kk: (i, j), grid, ((1,), (0,)),
               res, (tm, tn), lambda j, i, kk: (i, j))


def mm_jsum(name, aj, wg, out_dtype, res=None):
    nj, m, ka = aj.shape
    n = wg.shape[2]
    tm, tn = _pick(m, TM_CANDS), _pick(n, TN_CANDS)
    grid = (m // tm, n // tn, nj)
    return _mm(name, aj, wg, (None, tm, ka), lambda i, jn, j: (j, i, 0), (None, ka, tn), lambda i, jn, j: (j, 0, jn),
               (m, n), out_dtype, (tm, tn), lambda i, jn, j: (i, jn), grid, ((1,), (0,)),
               res, (tm, tn), lambda i, jn, j: (i, jn))


def mm_tn_cols(name, a, g, nj, n, out_dtype, folded, after=None):
    s, kw = a.shape
    ts, tkw = _pick(s, TK_CANDS), _pick(kw, TM_CANDS)
    grid = (nj, kw // tkw, s // ts)
    if folded:
        g_bs, g_im = (ts, n), (lambda j, i, ss: (ss, j))
    else:
        g_bs, g_im = (None, ts, n), (lambda j, i, ss: (j, ss, 0))
    return _mm(name, a, g, (ts, tkw), lambda j, i, ss: (ss, i), g_bs, g_im,
               (nj, kw, n), out_dtype, (None, tkw, n), lambda j, i, ss: (j, i, 0), grid, ((0,), (0,)), after=after)


def mm_tn_plain(name, a, g, out_dtype):
    s, kw = a.shape
    n = g.shape[1]
    ts, tkw, tn = _pick(s, TK_CANDS), _pick(kw, TM_CANDS), _pick(n, TN_CANDS)
    grid = (kw // tkw, n // tn, s // ts)
    return _mm(name, a, g, (ts, tkw), lambda i, jn, ss: (ss, i), (ts, tn), lambda i, jn, ss: (ss, jn),
               (kw, n), out_dtype, (tkw, tn), lambda i, jn, ss: (i, jn), grid, ((0,), (0,)))


def mm_tn_j(name, aj, g, out_dtype):
    nj, s, ka = aj.shape
    n = g.shape[1]
    ts, tn = _pick(s, TK_CANDS), _pick(n, TN_CANDS)
    grid = (nj, n // tn, s // ts)
    return _mm(name, aj, g, (None, ts, ka), lambda j, jn, ss: (j, ss, 0), (ts, tn), lambda j, jn, ss: (ss, jn),
               (nj, ka, n), out_dtype, (None, ka, tn), lambda j, jn, ss: (j, 0, jn), grid, ((0,), (0,)))


def mm_nt_plain(name, g, w, out_dtype):
    m, n = g.shape
    k = w.shape[0]
    tm, tn, tkk = _pick(m, TM_CANDS), _pick(n, TK_CANDS), _pick(k, TN_CANDS)
    grid = (k // tkk, m // tm, n // tn)
    return _mm(name, g, w, (tm, tn), lambda kk, i, jn: (i, jn), (tkk, tn), lambda kk, i, jn: (kk, jn),
               (m, k), out_dtype, (tm, tkk), lambda kk, i, jn: (i, kk), grid, ((1,), (1,)))


def mm_nt_j(name, g, wg, out_dtype):
    m, n = g.shape
    nj, ka, _ = wg.shape
    tm, tn = _pick(m, TM_CANDS), _pick(n, TK_CANDS)
    grid = (nj, m // tm, n // tn)
    return _mm(name, g, wg, (tm, tn), lambda j, i, jn: (i, jn), (None, ka, tn), lambda j, i, jn: (j, 0, jn),
               (nj, m, ka), out_dtype, (None, tm, ka), lambda j, i, jn: (j, i, 0), grid, ((1,), (1,)))


def mm_nt_jsum(name, g, wg, out_dtype, folded, res=None, after=None):
    nj, k, n = wg.shape
    m = g.shape[0] if folded else g.shape[1]
    tm, tkk = _pick(m, TM_CANDS), _pick(k, TN_CANDS)
    grid = (m // tm, k // tkk, nj)
    if folded:
        g_bs, g_im = (tm, n), (lambda i, kk, j: (i, j))
    else:
        g_bs, g_im = (None, tm, n), (lambda i, kk, j: (j, i, 0))
    return _mm(name, g, wg, g_bs, g_im, (None, tkk, n), lambda i, kk, j: (j, kk, 0),
               (m, k), out_dtype, (tm, tkk), lambda i, kk, j: (i, kk), grid, ((1,), (1,)),
               res, (tm, tkk), lambda i, kk, j: (i, kk), after=after)


ROW_TILE_CANDS = (256, 128, 64, 32, 16, 8)


def _rstd(x):
    return lax.rsqrt(jnp.mean(x * x, axis=-1, keepdims=True) + RMS_EPS)


def _sigmoid(t):
    return 1.0 / (1.0 + jnp.exp(-t))


def rms_fwd(name, x, gain):
    s, d = x.shape
    ts = _pick(s, ROW_TILE_CANDS)

    def body(x_ref, g_ref, h_ref):
        xv = x_ref[...]
        h_ref[...] = ((xv * _rstd(xv)) * g_ref[...]).astype(h_ref.dtype)

    return pl.pallas_call(
        body, name=name, grid=(s // ts,),
        in_specs=[pl.BlockSpec((ts, d), lambda i: (i, 0)), pl.BlockSpec((1, d), lambda i: (0, 0))],
        out_specs=pl.BlockSpec((ts, d), lambda i: (i, 0)),
        out_shape=jax.ShapeDtypeStruct((s, d), BF16),
        compiler_params=_params(("parallel",), 3 * ts * d * 4))(x, gain)


def rms_bwd(name, x, gain, dh, dres):
    s, d = x.shape
    ts = _pick(s, ROW_TILE_CANDS)

    def body(x_ref, g_ref, dh_ref, dr_ref, dx_ref, dg_ref):
        xv = x_ref[...]
        r = _rstd(xv)
        xhat = xv * r
        dhv = dh_ref[...].astype(F32)
        dxhat = dhv * g_ref[...]
        dx_ref[...] = dr_ref[...] + r * (dxhat - xhat * jnp.mean(dxhat * xhat, axis=-1, keepdims=True))
        part = jnp.sum(dhv * xhat, axis=0, keepdims=True)

        @pl.when(pl.program_id(0) == 0)
        def _():
            dg_ref[...] = part

        @pl.when(pl.program_id(0) > 0)
        def _():
            dg_ref[...] += part

    row = pl.BlockSpec((ts, d), lambda i: (i, 0))
    vec = pl.BlockSpec((1, d), lambda i: (0, 0))
    return pl.pallas_call(
        body, name=name, grid=(s // ts,), in_specs=[row, vec, row, row], out_specs=[row, vec],
        out_shape=[jax.ShapeDtypeStruct((s, d), F32), jax.ShapeDtypeStruct((1, d), F32)],
        compiler_params=_params(("arbitrary",), 6 * ts * d * 4))(x, gain, dh, dres)


def gate_merge_fwd(name, proj, ta, tb, d):
    s = proj.shape[0]
    ts = _pick(s, ROW_TILE_CANDS)
    cb = COL_GATES // d

    def body(ga_ref, gb_ref, ta_ref, tb_ref, o_ref):
        o_ref[...] = (_sigmoid(ga_ref[...]) * ta_ref[...] + _sigmoid(gb_ref[...]) * tb_ref[...]).astype(o_ref.dtype)

    row = pl.BlockSpec((ts, d), lambda i: (i, 0))
    return pl.pallas_call(
        body, name=name, grid=(s // ts,),
        in_specs=[pl.BlockSpec((ts, d), lambda i: (i, cb)), pl.BlockSpec((ts, d), lambda i: (i, cb + 1)), row, row],
        out_specs=row, out_shape=jax.ShapeDtypeStruct((s, d), BF16),
        compiler_params=_params(("parallel",), 5 * ts * d * 4))(proj, proj, ta, tb)


def gate_merge_bwd(name, dmerged, proj, ta, tb, d):
    s = proj.shape[0]
    ts = _pick(s, ROW_TILE_CANDS)
    cb = COL_GATES // d

    def body(dm_ref, ga_ref, gb_ref, ta_ref, tb_ref, dta_ref, dtb_ref, dga_ref, dgb_ref):
        dm = dm_ref[...]
        sa = _sigmoid(ga_ref[...])
        sb = _sigmoid(gb_ref[...])
        dta_ref[...] = (dm * sa).astype(dta_ref.dtype)
        dtb_ref[...] = (dm * sb).astype(dtb_ref.dtype)
        dga_ref[...] = (dm * ta_ref[...] * (sa * (1.0 - sa))).astype(dga_ref.dtype)
        dgb_ref[...] = (dm * tb_ref[...] * (sb * (1.0 - sb))).astype(dgb_ref.dtype)

    row = pl.BlockSpec((ts, d), lambda i: (i, 0))
    out = jax.ShapeDtypeStruct((s, d), BF16)
    return pl.pallas_call(
        body, name=name, grid=(s // ts,),
        in_specs=[row, pl.BlockSpec((ts, d), lambda i: (i, cb)), pl.BlockSpec((ts, d), lambda i: (i, cb + 1)), row, row],
        out_specs=[row, row, row, row], out_shape=[out, out, out, out],
        compiler_params=_params(("parallel",), 8 * ts * d * 4))(dmerged, proj, proj, ta, tb)


def tail_fwd_bwd(name, x2, lp, pp, gain, target):
    s, d = x2.shape
    ts = _pick(s, ROW_TILE_CANDS)

    def body(x2_ref, lp_ref, pp_ref, g_ref, t_ref, loss_ref, dx3_ref, dlp_ref, dpp_ref, dg_ref):
        gp = _sigmoid(lp_ref[...])
        ppv = pp_ref[...]
        x3 = x2_ref[...] + gp * ppv
        r = _rstd(x3)
        xhat = x3 * r
        gv = g_ref[...]
        err = xhat * gv - t_ref[...]
        loss = jnp.sum(err * err) * (0.5 / d)
        dy = err * (1.0 / d)
        dxhat = dy * gv
        dx3 = r * (dxhat - xhat * jnp.mean(dxhat * xhat, axis=-1, keepdims=True))
        dx3_ref[...] = dx3
        dlp_ref[...] = (dx3 * ppv * (gp * (1.0 - gp))).astype(dlp_ref.dtype)
        dpp_ref[...] = (dx3 * gp).astype(dpp_ref.dtype)
        part = jnp.sum(dy * xhat, axis=0, keepdims=True)
        lossv = jnp.full((1, LANES), loss, F32)

        @pl.when(pl.program_id(0) == 0)
        def _():
            dg_ref[...] = part
            loss_ref[...] = lossv

        @pl.when(pl.program_id(0) > 0)
        def _():
            dg_ref[...] += part
            loss_ref[...] += lossv

    row = pl.BlockSpec((ts, d), lambda i: (i, 0))
    vec = pl.BlockSpec((1, d), lambda i: (0, 0))
    return pl.pallas_call(
        body, name=name, grid=(s // ts,), in_specs=[row, row, row, vec, row],
        out_specs=[pl.BlockSpec((1, LANES), lambda i: (0, 0)), row, row, row, vec],
        out_shape=[jax.ShapeDtypeStruct((1, LANES), F32), jax.ShapeDtypeStruct((s, d), F32),
                   jax.ShapeDtypeStruct((s, d), BF16), jax.ShapeDtypeStruct((s, d), BF16),
                   jax.ShapeDtypeStruct((1, d), F32)],
        compiler_params=_params(("arbitrary",), 9 * ts * d * 4))(x2, lp, pp, gain, target)


HALO = SUBLANES


def _shift_rows(cur, prev_row, next_row):
    ts = cur.shape[0]
    rid = lax.broadcasted_iota(jnp.int32, cur.shape, 0)
    down = jnp.where(rid == 0, prev_row, pltpu.roll(cur, 1, 0))
    up = jnp.where(rid == ts - 1, next_row, pltpu.roll(cur, ts - 1, 0))
    return down, up


def _halo_specs(ts, s, nf):
    nb = ts // HALO
    last = s // HALO - 1
    cur = pl.BlockSpec((None, ts, nf), lambda j, i: (j, i, 0))
    prev = pl.BlockSpec((None, HALO, nf), lambda j, i: (j, jnp.maximum(i * nb - 1, 0), 0))
    nxt = pl.BlockSpec((None, HALO, nf), lambda j, i: (j, jnp.minimum((i + 1) * nb, last), 0))
    return cur, prev, nxt


def _halo_rows(prev_ref, next_ref, n_tiles):
    i = pl.program_id(1)
    prev_row = jnp.where(i == 0, 0.0, prev_ref[HALO - 1:HALO, :].astype(F32))
    next_row = jnp.where(i == n_tiles - 1, 0.0, next_ref[0:1, :].astype(F32))
    return prev_row, next_row


def _gelu(g):
    t = jnp.tanh(GELU_C * (g + GELU_A * (g * g * g)))
    return 0.5 * g * (1.0 + t), t


def _conv(cur, down, up, cw_ref, cb_ref):
    return down * cw_ref[0:1, :] + cur * cw_ref[1:2, :] + up * cw_ref[2:3, :] + cb_ref[...]


def ffn_mid_fwd(name, gpre, u, cw, cb):
    nj, s, nf = gpre.shape
    ts = _pick(s, (512, 256, 128, 64, 32, 16, 8))
    n_tiles = s // ts
    cur, prev, nxt = _halo_specs(ts, s, nf)

    def body(g_ref, gp_ref, gn_ref, u_ref, cw_ref, cb_ref, z_ref):
        gv = g_ref[...]
        down, up = _shift_rows(gv, *_halo_rows(gp_ref, gn_ref, n_tiles))
        act, _ = _gelu(_conv(gv, down, up, cw_ref, cb_ref))
        z_ref[...] = (act * u_ref[...]).astype(z_ref.dtype)

    return pl.pallas_call(
        body, name=name, grid=(nj, n_tiles),
        in_specs=[cur, prev, nxt, cur, pl.BlockSpec((None, SUBLANES, nf), lambda j, i: (j, 0, 0)),
                  pl.BlockSpec((None, 1, nf), lambda j, i: (j, 0, 0))],
        out_specs=cur, out_shape=jax.ShapeDtypeStruct((nj, s, nf), BF16),
        compiler_params=_params(("parallel", "parallel"), 8 * ts * nf * 4))(gpre, gpre, gpre, u, cw, cb)


def ffn_mid_bwd1(name, gpre, u, dz, cw, cb):
    nj, s, nf = gpre.shape
    ts = _pick(s, (512, 256, 128, 64, 32, 16, 8))
    n_tiles = s // ts
    cur, prev, nxt = _halo_specs(ts, s, nf)

    def body(g_ref, gp_ref, gn_ref, u_ref, dz_ref, cw_ref, cb_ref, dg_ref, du_ref, dcw_ref):
        gv = g_ref[...]
        down, up = _shift_rows(gv, *_halo_rows(gp_ref, gn_ref, n_tiles))
        gc = _conv(gv, down, up, cw_ref, cb_ref)
        act, t = _gelu(gc)
        dzv = dz_ref[...].astype(F32)
        du_ref[...] = (dzv * act).astype(du_ref.dtype)
        dact = 0.5 * (1.0 + t) + 0.5 * gc * (1.0 - t * t) * (GELU_C * (1.0 + 3.0 * GELU_A * (gc * gc)))
        dg = dzv * u_ref[...] * dact
        dg_ref[...] = dg
        rows = [jnp.sum(dg * down, axis=0, keepdims=True), jnp.sum(dg * gv, axis=0, keepdims=True),
                jnp.sum(dg * up, axis=0, keepdims=True), jnp.sum(dg, axis=0, keepdims=True)]
        part = jnp.concatenate(rows + [jnp.zeros((SUBLANES - len(rows), nf), F32)], axis=0)

        @pl.when(pl.program_id(1) == 0)
        def _():
            dcw_ref[...] = part

        @pl.when(pl.program_id(1) > 0)
        def _():
            dcw_ref[...] += part

    small = pl.BlockSpec((None, SUBLANES, nf), lambda j, i: (j, 0, 0))
    return pl.pallas_call(
        body, name=name, grid=(nj, n_tiles),
        in_specs=[cur, prev, nxt, cur, cur, small, pl.BlockSpec((None, 1, nf), lambda j, i: (j, 0, 0))],
        out_specs=[cur, cur, small],
        out_shape=[jax.ShapeDtypeStruct((nj, s, nf), F32), jax.ShapeDtypeStruct((nj, s, nf), BF16),
                   jax.ShapeDtypeStruct((nj, SUBLANES, nf), F32)],
        compiler_params=_params(("parallel", "arbitrary"), 12 * ts * nf * 4))(gpre, gpre, gpre, u, dz, cw, cb)


def ffn_mid_bwd2(name, dg, cw):
    nj, s, nf = dg.shape
    ts = _pick(s, (512, 256, 128, 64, 32, 16, 8))
    n_tiles = s // ts
    cur, prev, nxt = _halo_specs(ts, s, nf)

    def body(g_ref, gp_ref, gn_ref, cw_ref, o_ref):
        gv = g_ref[...]
        down, up = _shift_rows(gv, *_halo_rows(gp_ref, gn_ref, n_tiles))
        o_ref[...] = (up * cw_ref[0:1, :] + gv * cw_ref[1:2, :] + down * cw_ref[2:3, :]).astype(o_ref.dtype)

    return pl.pallas_call(
        body, name=name, grid=(nj, n_tiles),
        in_specs=[cur, prev, nxt, pl.BlockSpec((None, SUBLANES, nf), lambda j, i: (j, 0, 0))],
        out_specs=cur, out_shape=jax.ShapeDtypeStruct((nj, s, nf), BF16),
        compiler_params=_params(("parallel", "parallel"), 6 * ts * nf * 4))(dg, dg, dg, cw)


def _t5_bucket(rel):
    half = N_BUCKETS // 2
    max_exact = half // 2
    n = jnp.abs(rel)
    side = jnp.where(rel > 0, half, 0)
    nf = jnp.maximum(n, 1).astype(F32)
    large = max_exact + (jnp.log(nf / max_exact) / math.log(MAX_DISTANCE / max_exact)
                         * (half - max_exact)).astype(jnp.int32)
    large = jnp.minimum(large, half - 1)
    return side + jnp.where(n < max_exact, n, large)


def bucket_tile(rows, half, dil):
    rel = (jnp.arange(rows + 2 * half)[None, :] - half) - jnp.arange(rows)[:, None]
    return _t5_bucket(rel * dil).astype(jnp.int32)


def bias_build(name, table_t, bucket, h0, nh, half):
    blk, kw = bucket.shape

    def body(t_ref, b_ref, o_ref):
        h = pl.program_id(0)
        bv = b_ref[...]
        acc = jnp.zeros((blk, kw), F32)
        for b in range(N_BUCKETS):
            acc = jnp.where(bv == b, t_ref[h0 + h, b], acc)
        qi = lax.broadcasted_iota(jnp.int32, (blk, kw), 0)
        ci = lax.broadcasted_iota(jnp.int32, (blk, kw), 1)
        o_ref[...] = jnp.where(jnp.abs(ci - half - qi) <= half, acc, NEG_INF)

    return pl.pallas_call(
        body, name=name, grid=(nh,),
        in_specs=[pl.BlockSpec(memory_space=pltpu.SMEM), pl.BlockSpec((blk, kw), lambda h: (0, 0))],
        out_specs=pl.BlockSpec((None, blk, kw), lambda h: (h, 0, 0)),
        out_shape=jax.ShapeDtypeStruct((nh, blk, kw), F32),
        compiler_params=_params(("parallel",), 4 * blk * kw * 4))(table_t, bucket)


def table_grad(name, dbias, bucket):
    nh, blk, kw = dbias.shape

    def body(d_ref, b_ref, o_ref):
        bv = b_ref[...]
        dv = d_ref[...]
        lane = lax.broadcasted_iota(jnp.int32, (SUBLANES, LANES), 1)
        acc = jnp.zeros((SUBLANES, LANES), F32)
        for b in range(N_BUCKETS):
            acc = jnp.where(lane == b, jnp.sum(jnp.where(bv == b, dv, 0.0)), acc)
        o_ref[...] = acc

    return pl.pallas_call(
        body, name=name, grid=(nh,),
        in_specs=[pl.BlockSpec((None, blk, kw), lambda h: (h, 0, 0)), pl.BlockSpec((blk, kw), lambda h: (0, 0))],
        out_specs=pl.BlockSpec((None, SUBLANES, LANES), lambda h: (h, 0, 0)),
        out_shape=jax.ShapeDtypeStruct((nh, SUBLANES, LANES), F32),
        compiler_params=_params(("parallel",), 4 * blk * kw * 4))(dbias, bucket)


class _Band:
    def __init__(self, s, half, q_rows, n_chains, dil):
        self.s, self.half, self.dil, self.n_chains = s, half, dil, n_chains
        self.seg = s // dil
        self.q_rows = min(q_rows, self.seg)
        self.win = self.q_rows + 2 * half
        self.pad = self.seg + 2 * half
        self.nsb = self.seg // self.q_rows
        self.n_items = dil * self.nsb
        assert self.n_items % n_chains == 0 and self.seg % self.q_rows == 0
        self.staged = dil > 1

    def rows_of(self, r):
        return pl.ds(r, self.seg, stride=self.dil) if self.dil > 1 else slice(None)

    def stage_kv(self, dst, src_ref):
        zeros = jnp.zeros((self.half, HEAD_DIM), dst.dtype)
        for r in range(self.dil):
            base = r * self.pad
            dst[base:base + self.half, :] = zeros
            dst[base + self.half + self.seg:base + self.pad, :] = zeros
            dst[base + self.half:base + self.half + self.seg, :] = src_ref[self.rows_of(r), :].astype(dst.dtype)

    def stage(self, dst, src_ref):
        for r in range(self.dil):
            dst[r * self.seg:(r + 1) * self.seg, :] = src_ref[self.rows_of(r), :].astype(dst.dtype)

    def unstage(self, dst_ref, src, add=False):
        for r in range(self.dil):
            val = src[r * self.seg:(r + 1) * self.seg, :].astype(dst_ref.dtype)
            if add:
                val = val + dst_ref[self.rows_of(r), :]
            dst_ref[self.rows_of(r), :] = val

    def offsets(self, item):
        r, sb = item // self.nsb, item % self.nsb
        qoff = pl.multiple_of(r * self.seg + sb * self.q_rows, self.q_rows)
        koff = pl.multiple_of(r * self.pad + sb * self.q_rows, B_BLOCK)
        kpos = sb * self.q_rows - self.half + lax.broadcasted_iota(jnp.int32, (1, self.win), 1)
        edge = jnp.where((kpos >= 0) & (kpos < self.seg), 0.0, NEG_INF)
        return qoff, koff, edge


def band_attn_fwd(name, proj, bias, sink, *, half, q_rows, n_chains, dil, nh, group, cq, ck, cv):
    s, w = proj.shape
    g = _Band(s, half, q_rows, n_chains, dil)
    has_sink = sink is not None

    def body(*refs):
        q_ref, k_ref, v_ref, b_ref = refs[:4]
        s_ref = refs[4] if has_sink else None
        o_ref, l_ref, ks, vs = refs[4 + has_sink:8 + has_sink]
        qs, os_, ls = refs[8 + has_sink:] if g.staged else (None, o_ref, l_ref)
        g.stage_kv(ks, k_ref)
        g.stage_kv(vs, v_ref)
        if g.staged:
            g.stage(qs, q_ref)
        bias_v = b_ref[...]
        sk = s_ref[pl.program_id(0)] if has_sink else None

        def chain(item):
            qoff, koff, edge = g.offsets(item)
            rows = pl.ds(qoff, g.q_rows)
            qv = qs[rows, :] if g.staged else q_ref[rows, :].astype(BF16)
            kw_ = ks[pl.ds(koff, g.win), :]
            vw_ = vs[pl.ds(koff, g.win), :]
            sc = lax.dot_general(qv, kw_, (((1,), (1,)), ((), ())), preferred_element_type=F32) * ATTN_SCALE
            sc = sc + bias_v + edge
            m = jnp.max(sc, axis=-1, keepdims=True)
            if has_sink:
                m = jnp.maximum(m, sk)
            p = jnp.exp(sc - m)
            den = jnp.sum(p, axis=-1, keepdims=True)
            if has_sink:
                den = den + jnp.exp(sk - m)
            out = lax.dot_general(p.astype(BF16), vw_, (((1,), (0,)), ((), ())), preferred_element_type=F32)
            return rows, out / den, jnp.broadcast_to(m + jnp.log(den), (g.q_rows, HEAD_DIM))

        def step(i, carry):
            for rows, out, lse in [chain(i * n_chains + u) for u in range(n_chains)]:
                os_[rows, :] = out
                ls[rows, :] = lse
            return carry

        lax.fori_loop(0, g.n_items // n_chains, step, 0)
        if g.staged:
            g.unstage(o_ref, os_)
            g.unstage(l_ref, ls)

    def col(c0, per):
        return pl.BlockSpec((s, HEAD_DIM), lambda h: (0, c0 // LANES + h // per))

    in_specs = [col(cq, 1), col(ck, group), col(cv, group),
                pl.BlockSpec((None, g.q_rows, g.win), lambda h: (h, 0, 0))]
    args = [proj, proj, proj, bias]
    if has_sink:
        in_specs.append(pl.BlockSpec(memory_space=pltpu.SMEM))
        args.append(sink)
    shape = jax.ShapeDtypeStruct((s, nh * HEAD_DIM), F32)
    scratch = [pltpu.VMEM((dil * g.pad, HEAD_DIM), BF16), pltpu.VMEM((dil * g.pad, HEAD_DIM), BF16)]
    if g.staged:
        scratch += [pltpu.VMEM((s, HEAD_DIM), BF16), pltpu.VMEM((s, HEAD_DIM), F32), pltpu.VMEM((s, HEAD_DIM), F32)]
    return pl.pallas_call(
        body, name=name, grid=(nh,), in_specs=in_specs, out_specs=[col(0, 1), col(0, 1)], out_shape=[shape, shape],
        scratch_shapes=scratch, compiler_params=_params(("parallel",), 16 * s * HEAD_DIM * 4))(*args)


def band_attn_bwd(name, proj, bias, sink, dout, out, lse, dlse, *, half, q_rows, n_chains, dil, nh, group, cq, ck, cv):
    s, w = proj.shape
    g = _Band(s, half, q_rows, n_chains, dil)
    nkv = nh // group
    has_sink = sink is not None
    has_dl = dlse is not None
    n_in = 7 + int(has_sink) + int(has_dl)
    n_out = 4 + int(has_sink)

    def body(*refs):
        ins, outs, scr = refs[:n_in], refs[n_in:n_in + n_out], refs[n_in + n_out:]
        q_ref, k_ref, v_ref, b_ref, do_ref, o_ref, l_ref = ins[:7]
        s_ref = ins[7] if has_sink else None
        dl_ref = ins[n_in - 1] if has_dl else None
        dq_ref, dk_ref, dv_ref, db_ref = outs[:4]
        ks, vs, dks, dvs = scr[:4]
        scr = list(scr[4:])
        dsa = scr.pop(0) if has_sink else None
        if g.staged:
            qs, dos, os_, ls, dqs = scr[:5]
            dls = scr[5] if has_dl else None
            g.stage(qs, q_ref)
            g.stage(dos, do_ref)
            g.stage(os_, o_ref)
            g.stage(ls, l_ref)
            if has_dl:
                g.stage(dls, dl_ref)
        else:
            qs, dos, os_, ls, dqs, dls = None, do_ref, o_ref, l_ref, dq_ref, dl_ref
        h = pl.program_id(0)
        g.stage_kv(ks, k_ref)
        g.stage_kv(vs, v_ref)
        dks[...] = jnp.zeros_like(dks)
        dvs[...] = jnp.zeros_like(dvs)
        db_ref[...] = jnp.zeros_like(db_ref)
        bias_v = b_ref[...]
        if has_sink:
            sk = s_ref[h]
            dsa[...] = jnp.zeros_like(dsa)

        def chain(item):
            qoff, koff, edge = g.offsets(item)
            rows = pl.ds(qoff, g.q_rows)
            win = pl.ds(koff, g.win)
            qv = qs[rows, :] if g.staged else q_ref[rows, :].astype(BF16)
            kw_ = ks[win, :]
            vw_ = vs[win, :]
            sc = lax.dot_general(qv, kw_, (((1,), (1,)), ((), ())), preferred_element_type=F32) * ATTN_SCALE
            lv = ls[rows, :][:, 0:1]
            p = jnp.exp(sc + bias_v + edge - lv)
            dov = dos[rows, :]
            delta = jnp.sum(dov * os_[rows, :], axis=-1, keepdims=True)
            dob = dov.astype(BF16)
            dp = lax.dot_general(dob, vw_, (((1,), (1,)), ((), ())), preferred_element_type=F32)
            t = dp - delta
            if has_dl:
                t = t + dls[rows, :][:, 0:1]
            ds = p * t
            dsb = (ds * ATTN_SCALE).astype(BF16)
            dq = lax.dot_general(dsb, kw_, (((1,), (0,)), ((), ())), preferred_element_type=F32)
            dkc = lax.dot_general(dsb, qv, (((0,), (0,)), ((), ())), preferred_element_type=F32)
            dvc = lax.dot_general(p.astype(BF16), dob, (((0,), (0,)), ((), ())), preferred_element_type=F32)
            dsk = jnp.exp(sk - lv) * delta if has_sink else None
            return rows, win, dq, dkc, dvc, ds, dsk

        def step(i, carry):
            res = [chain(i * n_chains + u) for u in range(n_chains)]
            ds_sum = res[0][5]
            for rr in res[1:]:
                ds_sum = ds_sum + rr[5]
            db_ref[...] += ds_sum
            for rows, win, dq, dkc, dvc, ds, dsk in res:
                dqs[rows, :] = dq
                dks[win, :] += dkc
                dvs[win, :] += dvc
                if has_sink:
                    dsa[...] += dsk
            return carry

        lax.fori_loop(0, g.n_items // n_chains, step, 0)

        if g.staged:
            g.unstage(dq_ref, dqs)

        def emit_kv(add):
            for r in range(dil):
                lo = r * g.pad + half
                for dst_ref, src in ((dk_ref, dks), (dv_ref, dvs)):
                    val = src[lo:lo + g.seg, :]
                    if add:
                        val = val + dst_ref[g.rows_of(r), :]
                    dst_ref[g.rows_of(r), :] = val

        if group == 1:
            emit_kv(False)
        else:
            @pl.when(h % group == 0)
            def _():
                emit_kv(False)

            @pl.when(h % group != 0)
            def _():
                emit_kv(True)
        if has_sink:
            outs[4][...] = jnp.full((SUBLANES, LANES), -jnp.sum(dsa[...]), F32)

    def col(c0, per):
        return pl.BlockSpec((s, HEAD_DIM), lambda h: (0, c0 // LANES + h // per))

    b_spec = pl.BlockSpec((None, g.q_rows, g.win), lambda h: (h, 0, 0))
    in_specs = [col(cq, 1), col(ck, group), col(cv, group), b_spec, col(0, 1), col(0, 1), col(0, 1)]
    args = [proj, proj, proj, bias, dout, out, lse]
    if has_sink:
        in_specs.append(pl.BlockSpec(memory_space=pltpu.SMEM))
        args.append(sink)
    if has_dl:
        in_specs.append(col(0, 1))
        args.append(dlse)
    out_specs = [col(0, 1), col(0, group), col(0, group), b_spec]
    out_shape = [jax.ShapeDtypeStruct((s, nh * HEAD_DIM), F32), jax.ShapeDtypeStruct((s, nkv * HEAD_DIM), F32),
                 jax.ShapeDtypeStruct((s, nkv * HEAD_DIM), F32), jax.ShapeDtypeStruct((nh, g.q_rows, g.win), F32)]
    scratch = [pltpu.VMEM((dil * g.pad, HEAD_DIM), BF16), pltpu.VMEM((dil * g.pad, HEAD_DIM), BF16),
               pltpu.VMEM((dil * g.pad, HEAD_DIM), F32), pltpu.VMEM((dil * g.pad, HEAD_DIM), F32)]
    if has_sink:
        out_specs.append(pl.BlockSpec((None, SUBLANES, LANES), lambda h: (h, 0, 0)))
        out_shape.append(jax.ShapeDtypeStruct((nh, SUBLANES, LANES), F32))
        scratch.append(pltpu.VMEM((g.q_rows, 1), F32))
    if g.staged:
        scratch += [pltpu.VMEM((s, HEAD_DIM), BF16)] + [pltpu.VMEM((s, HEAD_DIM), F32)] * (4 + int(has_dl))
    res = pl.pallas_call(
        body, name=name, grid=(nh,), in_specs=in_specs, out_specs=out_specs, out_shape=out_shape,
        scratch_shapes=scratch, compiler_params=_params(("arbitrary",), 28 * s * HEAD_DIM * 4))(*args)
    return res[0], res[1], res[2], res[3], (res[4] if has_sink else None)


def dil_merge_fwd(name, outs, lses):
    s, w = outs[0].shape
    ts = _pick(s, ROW_TILE_CANDS)
    ng = len(outs)

    def body(*refs):
        o_refs, l_refs, y_ref = refs[:ng], refs[ng:2 * ng], refs[2 * ng]
        ls = [l[...] for l in l_refs]
        mx = ls[0]
        for l in ls[1:]:
            mx = jnp.maximum(mx, l)
        es = [jnp.exp(l - mx) for l in ls]
        tot = es[0]
        for e in es[1:]:
            tot = tot + e
        acc = (es[0] / tot) * o_refs[0][...]
        for e, o in zip(es[1:], o_refs[1:]):
            acc = acc + (e / tot) * o[...]
        y_ref[...] = acc.astype(y_ref.dtype)

    row = pl.BlockSpec((ts, w), lambda i: (i, 0))
    return pl.pallas_call(
        body, name=name, grid=(s // ts,), in_specs=[row] * (2 * ng), out_specs=row,
        out_shape=jax.ShapeDtypeStruct((s, w), BF16),
        compiler_params=_params(("parallel",), 10 * ts * w * 4))(*outs, *lses)


def dil_merge_bwd(name, dy, outs, lses):
    s, w = outs[0].shape
    ts = _pick(s, ROW_TILE_CANDS)
    ng = len(outs)
    nhead = w // HEAD_DIM

    def body(*refs):
        dy_ref = refs[0]
        o_refs, l_refs = refs[1:1 + ng], refs[1 + ng:1 + 2 * ng]
        do_refs, dl_refs = refs[1 + 2 * ng:1 + 3 * ng], refs[1 + 3 * ng:1 + 4 * ng]
        for hh in range(nhead):
            cols = slice(hh * HEAD_DIM, (hh + 1) * HEAD_DIM)
            dyv = dy_ref[:, cols]
            ls = [l[:, cols] for l in l_refs]
            mx = ls[0]
            for l in ls[1:]:
                mx = jnp.maximum(mx, l)
            es = [jnp.exp(l - mx) for l in ls]
            tot = es[0]
            for e in es[1:]:
                tot = tot + e
            alphas = [e / tot for e in es]
            dal = [jnp.broadcast_to(jnp.sum(dyv * o[:, cols], axis=-1, keepdims=True), dyv.shape) for o in o_refs]
            mean = alphas[0] * dal[0]
            for a, d in zip(alphas[1:], dal[1:]):
                mean = mean + a * d
            for g in range(ng):
                do_refs[g][:, cols] = alphas[g] * dyv
                dl_refs[g][:, cols] = alphas[g] * (dal[g] - mean)

    row = pl.BlockSpec((ts, w), lambda i: (i, 0))
    shape = jax.ShapeDtypeStruct((s, w), F32)
    res = pl.pallas_call(
        body, name=name, grid=(s // ts,), in_specs=[row] * (1 + 2 * ng), out_specs=[row] * (2 * ng),
        out_shape=[shape] * (2 * ng),
        compiler_params=_params(("parallel",), 16 * ts * w * 4))(dy, *outs, *lses)
    return res[:ng], res[ng:]


def _adamw(w, g, m, v):
    m = ADAM_B1 * m + (1.0 - ADAM_B1) * g
    v = ADAM_B2 * v + (1.0 - ADAM_B2) * (g * g)
    m_hat = m / (1.0 - ADAM_B1 ** ADAM_STEP)
    v_hat = v / (1.0 - ADAM_B2 ** ADAM_STEP)
    delta = -ADAM_LR * (m_hat / (jnp.sqrt(v_hat) + ADAM_EPS) + ADAM_WD * w)
    return delta, m, v


def _row_tile(r, c, budget=1 << 20):
    if r * c * 4 <= budget or r % SUBLANES:
        return r
    for t in (1024, 512, 256, 128, 64, 32, 16, 8):
        if r % t == 0 and t * c * 4 <= budget:
            return t
    return SUBLANES


def adam_small(name, g, w, m, v):
    def body(g_ref, w_ref, m_ref, v_ref, d_ref, nm_ref, nv_ref):
        d_ref[...], nm_ref[...], nv_ref[...] = _adamw(w_ref[...], g_ref[...], m_ref[...], v_ref[...])

    shape = jax.ShapeDtypeStruct(w.shape, F32)
    return pl.pallas_call(body, name=name, out_shape=[shape, shape, shape])(g, w, m, v)


def reduce_adam(name, mine, theirs, w, m, v):
    nq, r, c = mine.shape
    tr = _row_tile(r, c)

    def body(*refs):
        parts, (w_ref, m_ref, v_ref, g_ref, d_ref, nm_ref, nv_ref) = refs[:nq], refs[nq:]
        g = parts[0][...].astype(F32)
        for p_ref in parts[1:]:
            g = g + p_ref[...].astype(F32)
        g_ref[...] = g
        d_ref[...], nm_ref[...], nv_ref[...] = _adamw(w_ref[...], g, m_ref[...], v_ref[...])

    def slot(q):
        return pl.BlockSpec((None, tr, c), lambda i: (q, i, 0))

    row = pl.BlockSpec((tr, c), lambda i: (i, 0))
    shape = jax.ShapeDtypeStruct((r, c), F32)
    return pl.pallas_call(
        body, name=name, grid=(r // tr,), in_specs=[slot(q) for q in range(nq)] + [row, row, row],
        out_specs=[row] * 4, out_shape=[shape] * 4,
        compiler_params=_params(("parallel",), (nq * 2 + 7 * 4) * tr * c))(mine, *[theirs] * (nq - 1), w, m, v)


def _place():
    return lax.axis_index("x"), lax.axis_index("y"), lax.axis_index("c")


def _flip(pos, bits):
    return tuple((1 - p) if b else p for p, b in zip(pos, bits))


def _index(pos):
    return 4 * pos[0] + 2 * pos[1] + pos[2]


ANY = pl.BlockSpec(memory_space=pl.ANY)


HBM = pl.BlockSpec(memory_space=pltpu.HBM)
SEM = pl.BlockSpec(memory_space=pltpu.SEMAPHORE)
EFFECT = pltpu.SideEffectType.DATAFLOW_SIDE_EFFECTING
TO_SIBLING = (0, 0, 1)
TO_CHIPS = [(1, 0, 0), (0, 1, 0), (1, 1, 0)]


def _in_hbm(a):
    return pltpu.with_memory_space_constraint(a, pltpu.HBM)


def _token_value(token):
    return token[0, 0]


def _when(pred, fn):
    if pred is True:
        fn()
    elif pred is not False:
        pl.when(pred)(fn)


def _plan_copy(k, entry, ins, lnd, send_sems, recv_sems):
    a, src_a, sblk, lblk, to, send_if, recv_if = entry
    src = lnd[a] if src_a is None else ins[src_a]
    return pltpu.make_async_remote_copy(
        src_ref=src.at[sblk], dst_ref=lnd[a].at[lblk], send_sem=send_sems.at[k], recv_sem=recv_sems.at[k],
        device_id=to, device_id_type=MESH), send_if, recv_if


def split_start(name, srcs, lands, plan, after):
    ns, nl = len(srcs), len(lands)
    n_copies = len(plan((0, 0, 0)))

    def body(*refs):
        ins, lnd = refs[:ns], refs[ns:ns + nl]
        send_sems, recv_sems = refs[ns + nl + 1], refs[ns + nl + 2]
        token = refs[-1]
        for k, entry in enumerate(plan(_place())):
            cp, send_if, _ = _plan_copy(k, entry, ins, lnd, send_sems, recv_sems)
            _when(send_if, cp.start)
        token[...] = jnp.zeros_like(token)

    outs = pl.pallas_call(
        body, name=name,
        out_shape=(pltpu.SemaphoreType.DMA((n_copies,)), pltpu.SemaphoreType.DMA((n_copies,)),
                   *[pltpu.HBM(a.shape, a.dtype) for a in srcs], *[pltpu.HBM(a.shape, a.dtype) for a in lands],
                   jax.ShapeDtypeStruct((SUBLANES, LANES), F32)),
        in_specs=[HBM] * (ns + nl) + [ANY],
        out_specs=(SEM, SEM, *[HBM] * (ns + nl), pl.BlockSpec(memory_space=pltpu.VMEM)),
        input_output_aliases={i: 2 + i for i in range(ns + nl)},
        compiler_params=pltpu.CompilerParams(has_side_effects=EFFECT),
    )(*[_in_hbm(a) for a in srcs], *[_in_hbm(a) for a in lands], after)
    return outs[0], outs[1], list(outs[2:2 + ns]), list(outs[2 + ns:2 + ns + nl]), outs[-1]


def split_wait(name, send_sems, recv_sems, srcs, lands, plan, after):
    ns, nl = len(srcs), len(lands)

    def body(*refs):
        ins, lnd = refs[:ns], refs[ns:ns + nl]
        s_sems, r_sems = refs[ns + nl], refs[ns + nl + 1]
        for k, entry in enumerate(plan(_place())):
            cp, send_if, recv_if = _plan_copy(k, entry, ins, lnd, s_sems, r_sems)
            _when(send_if, cp.wait_send)
            _when(recv_if, cp.wait_recv)

    outs = pl.pallas_call(
        body, name=name,
        out_shape=(*[pltpu.HBM(a.shape, a.dtype) for a in srcs], *[pltpu.HBM(a.shape, a.dtype) for a in lands]),
        in_specs=[HBM] * (ns + nl) + [SEM, SEM, ANY],
        out_specs=tuple([HBM] * (ns + nl)),
        input_output_aliases={i: i for i in range(ns + nl)},
        compiler_params=pltpu.CompilerParams(has_side_effects=EFFECT),
    )(*srcs, *lands, send_sems, recv_sems, after)
    return list(outs[:ns]), list(outs[ns:])


NORTH = 1


def ag_plan(n):
    def plan(me):
        x, y, c = me
        entries = []
        for a in range(n):
            for t in (NORTH, 1 - NORTH):
                blk = _index((x, y, t))
                for rel in TO_CHIPS:
                    entries.append((a, None, blk, blk, _flip((x, y, t), rel), c == NORTH, c == t))
        return entries
    return plan


def ag_pair(name, lands, after):
    n = len(lands)

    def body(*refs):
        lnd = refs[n + 1:2 * n + 1]
        token = refs[2 * n + 1]
        send_sems, recv_sems = refs[2 * n + 2:]
        token[...] = jnp.zeros_like(token)
        me = _place()
        sibling = _flip(me, TO_SIBLING)
        copies = []
        for a in range(n):
            mine, theirs = lnd[a].at[_index(me)], lnd[a].at[_index(sibling)]
            cp = pltpu.make_async_remote_copy(src_ref=mine, dst_ref=mine, send_sem=send_sems.at[a],
                                              recv_sem=recv_sems.at[a], device_id=sibling, device_id_type=MESH)
            cp.start()
            copies.append((cp, pltpu.make_async_remote_copy(
                src_ref=mine, dst_ref=theirs, send_sem=send_sems.at[a], recv_sem=recv_sems.at[a], device_id=sibling,
                device_id_type=MESH)))
        for cp, arrival in copies:
            arrival.wait_recv()
        for cp, arrival in copies:
            cp.wait_send()

    outs = pl.pallas_call(
        body, name=name, in_specs=[ANY] * (n + 1), out_specs=[ANY] * n + [pl.BlockSpec(memory_space=pltpu.VMEM)],
        out_shape=[jax.ShapeDtypeStruct(l.shape, l.dtype) for l in lands]
        + [jax.ShapeDtypeStruct((SUBLANES, LANES), F32)],
        input_output_aliases={a: a for a in range(n)},
        scratch_shapes=[pltpu.SemaphoreType.DMA((n,)), pltpu.SemaphoreType.DMA((n,))],
    )(*lands, after)
    return list(outs[:n]), outs[n]


def ag_start(name, lands, after):
    return split_start(name, [], lands, ag_plan(len(lands)), after)


def pass_plan(n):
    def plan(me):
        sibling = _flip(me, TO_SIBLING)
        return [(a, None, _index(_flip(me, rel)), _index(_flip(me, rel)), sibling, True, True)
                for a in range(n) for rel in TO_CHIPS]
    return plan


def ag_finish(name, lands):
    n = len(lands)

    def body(*refs):
        lnd = refs[n:2 * n]
        send_sems, recv_sems = refs[2 * n:]
        me = _place()
        sibling = _flip(me, TO_SIBLING)
        copies = []
        for a in range(n):
            for j, rel in enumerate(TO_CHIPS):
                blk = lnd[a].at[_index(_flip(me, rel))]
                there = lnd[a].at[_index(_flip(sibling, rel))]
                cp = pltpu.make_async_remote_copy(
                    src_ref=blk, dst_ref=blk, send_sem=send_sems.at[a * 3 + j], recv_sem=recv_sems.at[a * 3 + j],
                    device_id=sibling, device_id_type=MESH)
                cp.start()
                copies.append((cp, pltpu.make_async_remote_copy(
                    src_ref=blk, dst_ref=there, send_sem=send_sems.at[a * 3 + j], recv_sem=recv_sems.at[a * 3 + j],
                    device_id=sibling, device_id_type=MESH)))
        for cp, arrival in copies:
            arrival.wait_recv()
        for cp, arrival in copies:
            cp.wait_send()

    return pl.pallas_call(
        body, name=name, in_specs=[ANY] * n, out_specs=[ANY] * n,
        out_shape=[jax.ShapeDtypeStruct(l.shape, l.dtype) for l in lands],
        input_output_aliases={a: a for a in range(n)},
        scratch_shapes=[pltpu.SemaphoreType.DMA((3 * n,)), pltpu.SemaphoreType.DMA((3 * n,))],
    )(*lands)


REL = [(b >> 2 & 1, b >> 1 & 1, b & 1) for b in range(N_DEV)]


CHIP_REL = [(0, 0, 0)] + TO_CHIPS
N_CHIPS = len(CHIP_REL)


def rs_pair(name, parts):
    n = len(parts)

    def body(*refs):
        ins, got = refs[:n], refs[n:2 * n]
        send_sems, recv_sems = refs[2 * n:]
        me = _place()
        sibling = _flip(me, TO_SIBLING)
        remote = []
        for a in range(n):
            for q, rel in enumerate(CHIP_REL):
                k = a * N_CHIPS + q
                cp = pltpu.make_async_remote_copy(
                    src_ref=ins[a].at[_index(_flip(sibling, rel))], dst_ref=got[a].at[q], send_sem=send_sems.at[k],
                    recv_sem=recv_sems.at[k], device_id=sibling, device_id_type=MESH)
                cp.start()
                remote.append(cp)
        for cp in remote:
            cp.wait_recv()
        for cp in remote:
            cp.wait_send()

    shapes = [jax.ShapeDtypeStruct((N_CHIPS,) + tuple(p.shape[1:]), p.dtype) for p in parts]
    res = pl.pallas_call(
        body, name=name, in_specs=[ANY] * n, out_specs=[ANY] * n, out_shape=shapes,
        scratch_shapes=[pltpu.SemaphoreType.DMA((N_CHIPS * n,)), pltpu.SemaphoreType.DMA((N_CHIPS * n,))],
    )(*parts)
    return list(res)


def own_blocks():
    me = _place()
    return jnp.stack([_index(_flip(me, rel)) for rel in CHIP_REL]).astype(jnp.int32)


def pair_add(name, blocks, parts, got):
    nq, r, c = got.shape
    tr = _row_tile(r, c)

    def body(blk_ref, a_ref, b_ref, o_ref):
        o_ref[...] = (a_ref[...].astype(F32) + b_ref[...].astype(F32)).astype(o_ref.dtype)

    spec = pl.BlockSpec((None, tr, c), lambda q, i, blk: (q, i, 0))
    return pl.pallas_call(
        body, name=name,
        grid_spec=pltpu.PrefetchScalarGridSpec(
            num_scalar_prefetch=1, grid=(nq, r // tr),
            in_specs=[pl.BlockSpec((None, tr, c), lambda q, i, blk: (blk[q], i, 0)), spec], out_specs=spec),
        out_shape=jax.ShapeDtypeStruct(got.shape, got.dtype),
        compiler_params=_params(("arbitrary", "arbitrary"), 6 * tr * c * 2))(blocks, parts, got)


def rs_pair_plan(n):
    def plan(me):
        sibling = _flip(me, TO_SIBLING)
        return [(a, a, _index(_flip(sibling, rel)), q, sibling, True, True)
                for a in range(n) for q, rel in enumerate(CHIP_REL)]
    return plan


def rs_plan(n):
    def plan(me):
        return [(a, a, q, q, _flip(me, CHIP_REL[q]), True, True) for a in range(n) for q in range(1, N_CHIPS)]
    return plan


def rs_start(name, sums, after):
    lands = [lax.empty(t.shape, t.dtype) for t in sums]
    return split_start(name, sums, lands, rs_plan(len(sums)), after)


def allreduce_small(name, pack, after):
    rows, lanes = pack.shape

    def body(x_ref, after_ref, o_ref, land, send_sems, recv_sems):
        me = _place()
        idx = _index(me)
        land[idx] = x_ref[...]
        copies = []
        for r in range(1, N_DEV):
            peer = _flip(me, REL[r])
            cp = pltpu.make_async_remote_copy(
                src_ref=x_ref, dst_ref=land.at[idx], send_sem=send_sems.at[r - 1], recv_sem=recv_sems.at[r - 1],
                device_id=peer, device_id_type=MESH)
            cp.start()
            copies.append(cp)
        for cp in copies:
            cp.wait_recv()
        for cp in copies:
            cp.wait_send()
        acc = land[0]
        for i in range(1, N_DEV):
            acc = acc + land[i]
        o_ref[...] = acc

    return pl.pallas_call(
        body, name=name, in_specs=[pl.BlockSpec(memory_space=pltpu.VMEM), ANY],
        out_specs=pl.BlockSpec(memory_space=pltpu.VMEM), out_shape=jax.ShapeDtypeStruct((rows, lanes), F32),
        scratch_shapes=[pltpu.VMEM((N_DEV, rows, lanes), F32), pltpu.SemaphoreType.DMA((7,)),
                        pltpu.SemaphoreType.DMA((7,))],
    )(pack, after)


def _pad_rows(a, rows):
    return jnp.pad(a, ((0, rows - a.shape[0]), (0, 0)))


def _as_tiles(vec):
    n = vec.shape[0]
    rows = -(-n // LANES)
    rows = -(-rows // SUBLANES) * SUBLANES
    return jnp.pad(vec, (0, rows * LANES - n)).reshape(rows, LANES)


def kernel(x, p, rel_bias_table, attn_norm, w_in, sink_a, w_branch_a, w_branch_b, w_out, ffn_norm, w_ffn_gate, w_ffn_up, conv_w, conv_b, w_ffn_down, ple_norm, w_ple_gate, w_ple_proj, final_norm, loss_target, m_rel_bias_table, m_attn_norm, m_w_in, m_sink_a, m_w_branch_a, m_w_branch_b, m_w_out, m_ffn_norm, m_w_ffn_gate, m_w_ffn_up, m_conv_w, m_conv_b, m_w_ffn_down, m_ple_norm, m_w_ple_gate, m_w_ple_proj, m_final_norm, v_rel_bias_table, v_attn_norm, v_w_in, v_sink_a, v_w_branch_a, v_w_branch_b, v_w_out, v_ffn_norm, v_w_ffn_gate, v_w_ffn_up, v_conv_w, v_conv_b, v_w_ffn_down, v_ple_norm, v_w_ple_gate, v_w_ple_proj, v_final_norm):
    xs = x[0]
    s, d = xs.shape
    ps = p[0, 0]
    target = loss_target[0]
    me = 4 * lax.axis_index("x") + 2 * lax.axis_index("y") + lax.axis_index("c")

    big = dict(w_in=w_in[0], w_branch_a=w_branch_a[0], w_branch_b=w_branch_b[0], w_out=w_out[0],
               w_ffn_gate=w_ffn_gate[0], w_ffn_up=w_ffn_up[0], w_ffn_down=w_ffn_down[0],
               w_ple_gate=w_ple_gate[0], w_ple_proj=w_ple_proj[0])
    big_m = dict(w_in=m_w_in[0], w_branch_a=m_w_branch_a[0], w_branch_b=m_w_branch_b[0], w_out=m_w_out[0],
                 w_ffn_gate=m_w_ffn_gate[0], w_ffn_up=m_w_ffn_up[0], w_ffn_down=m_w_ffn_down[0],
                 w_ple_gate=m_w_ple_gate[0], w_ple_proj=m_w_ple_proj[0])
    big_v = dict(w_in=v_w_in[0], w_branch_a=v_w_branch_a[0], w_branch_b=v_w_branch_b[0], w_out=v_w_out[0],
                 w_ffn_gate=v_w_ffn_gate[0], w_ffn_up=v_w_ffn_up[0], w_ffn_down=v_w_ffn_down[0],
                 w_ple_gate=v_w_ple_gate[0], w_ple_proj=v_w_ple_proj[0])
    names = list(big)
    nf = big["w_ffn_gate"].shape[1]

    shards = {k: big[k].astype(BF16) for k in names}
    shards["conv_w"] = _pad_rows(conv_w[0], SUBLANES)
    ag_groups = [["w_in"], ["w_branch_a", "w_branch_b", "w_out"], ["w_ffn_gate", "w_ffn_up", "conv_w"],
                 ["w_ffn_down", "w_ple_gate", "w_ple_proj"]]
    ag_started = {}
    wg = {}

    ag_paired, ag_passing = {}, {}

    def pair(gi, after):
        lands = [lax.dynamic_update_index_in_dim(lax.empty((N_DEV,) + shards[k].shape, shards[k].dtype), shards[k],
                                                 me, 0) for k in ag_groups[gi]]
        ag_paired[gi], token = ag_pair(f"ag_pair{gi}", lands, after)
        return token

    def start(gi, after):
        s_sems, r_sems, _, lands, token = ag_start(f"ag_start{gi}", ag_paired[gi], after)
        ag_started[gi] = (s_sems, r_sems, lands)
        return token

    def landed(gi, after):
        s_sems, r_sems, lands = ag_started[gi]
        return split_wait(f"ag_wait{gi}", s_sems, r_sems, [], lands, ag_plan(len(lands)), after)[1]

    def pass_on(gi, lands, after):
        s_sems, r_sems, _, lands, token = split_start(f"ag_pass{gi}", [], lands, pass_plan(len(lands)), after)
        ag_passing[gi] = (s_sems, r_sems, lands)
        return token

    def ready(gi, after):
        s_sems, r_sems, lands = ag_passing[gi]
        lands = split_wait(f"ag_ready{gi}", s_sems, r_sems, [], lands, pass_plan(len(lands)), after)[1]
        wg.update(zip(ag_groups[gi], lands))

    cb = conv_b.reshape(N_DEV, 1, nf)

    table_t = rel_bias_table.T
    geo_a = dict(half=A_BLOCK, q_rows=ATTN_Q_ROWS, n_chains=ATTN_CHAINS, dil=1, nh=A_Q_HEADS, group=A_GROUP,
                 cq=COL_QA, ck=COL_KA, cv=COL_VA)
    geo_b = [dict(half=B_BLOCK, q_rows=min(ATTN_Q_ROWS, s // dil), n_chains=ATTN_CHAINS, dil=dil,
                  nh=B_HEADS_PER_GROUP, group=1, cq=COL_QB + g * B_OUT_W, ck=COL_KB + g * B_OUT_W,
                  cv=COL_VB + g * B_OUT_W) for g, (_, dil) in enumerate(B_PATTERNS)]
    bucket_a = bucket_tile(geo_a["q_rows"], A_BLOCK, 1)
    bias_a = bias_build("bias_a", table_t, bucket_a, 0, A_Q_HEADS, A_BLOCK)
    buckets_b = [bucket_tile(gb["q_rows"], B_BLOCK, gb["dil"]) for gb in geo_b]
    biases_b = [bias_build(f"bias_b{g}", table_t, buckets_b[g], A_Q_HEADS + g * B_HEADS_PER_GROUP, B_HEADS_PER_GROUP,
                           B_BLOCK) for g in range(len(B_PATTERNS))]

    token = start(1, pair(1, start(0, pair(0, xs))))
    h = rms_fwd("rms_attn", xs, attn_norm + _token_value(token))
    lands0 = landed(0, pair(3, pair(2, h)))
    wg["w_in"] = ag_finish("ag_finish0", lands0)[0]
    token = start(2, wg["w_in"])
    proj = mm_cols("proj_in", h, wg["w_in"], F32, fold=True, after=token)
    lands1 = landed(1, proj)
    token = start(3, pass_on(1, lands1, proj))
    sink = sink_a[0] + _token_value(token)
    ya, lse_a = band_attn_fwd("attn_a_fwd", proj, bias_a, sink, **geo_a)
    outs_b, lses_b = [], []
    for g in range(len(B_PATTERNS)):
        o, l = band_attn_fwd(f"attn_b{g}_fwd", proj, biases_b[g], None, **geo_b[g])
        outs_b.append(o)
        lses_b.append(l)
    yb = dil_merge_fwd("dil_merge_fwd", outs_b, lses_b)
    ready(1, yb)
    token = pass_on(2, landed(2, yb), yb)
    w_out_full = wg["w_out"].reshape(d, d)
    ta = mm_cols("branch_a", ya, wg["w_branch_a"], F32, fold=True, after=token)
    tb = mm_cols("branch_b", yb, wg["w_branch_b"], F32, fold=True)
    merged = gate_merge_fwd("gate_merge_fwd", proj, ta, tb, d)
    x1 = mm_plain("mix_out", merged, w_out_full, F32, res=xs)

    hf = rms_fwd("rms_ffn", x1, ffn_norm)
    ready(2, hf)
    token = pass_on(3, landed(3, hf), hf)
    cw = wg["conv_w"]
    gpre = mm_cols("ffn_gate", hf, wg["w_ffn_gate"], F32, fold=False, after=token)
    u = mm_cols("ffn_up", hf, wg["w_ffn_up"], F32, fold=False)
    z = ffn_mid_fwd("ffn_mid_fwd", gpre, u, cw, cb)
    ready(3, z)
    w_pg_full = wg["w_ple_gate"].reshape(d, d)
    x2 = mm_jsum("ffn_down", z, wg["w_ffn_down"], F32, res=x1)

    hp = rms_fwd("rms_ple", x2, ple_norm)
    lp = mm_plain("ple_gate", hp, w_pg_full, F32)
    pp = mm_cols("ple_proj", ps, wg["w_ple_proj"], F32, fold=True)
    loss_part, dx3, dlp, dpp, d_final = tail_fwd_bwd("tail", x2, lp, pp, final_norm.reshape(1, d), target)

    grads = {}
    rs_started = []
    blocks = own_blocks()

    exchanging = []

    def exchange(tag, keys):
        parts = [grads[k] for k in keys]
        lands = [lax.empty((N_CHIPS,) + tuple(p.shape[1:]), p.dtype) for p in parts]
        s_sems, r_sems, parts, lands, token = split_start(f"rs_pair_{tag}", parts, lands, rs_pair_plan(len(keys)), blocks)
        exchanging.append((tag, keys, s_sems, r_sems, parts, lands))
        return _token_value(token)

    def send(after):
        tag, keys, s_sems, r_sems, parts, lands = exchanging.pop(0)
        parts, got = split_wait(f"rs_paired_{tag}", s_sems, r_sems, parts, lands, rs_pair_plan(len(keys)), after)
        return send_sums(tag, keys, parts, got)

    def send_sums(tag, keys, parts, got):
        sums = [pair_add(f"pair_add_{k}", blocks, p, g) for k, p, g in zip(keys, parts, got)]
        s_sems, r_sems, srcs, lands, token = rs_start(f"rs_start_{tag}", sums, blocks)
        rs_started.append((tag, keys, s_sems, r_sems, srcs, lands))
        return token

    grads["w_ple_proj"] = mm_tn_cols("d_w_ple_proj", ps, dpp, N_DEV, big["w_ple_proj"].shape[1], BF16, folded=True)
    grads["w_ple_gate"] = mm_tn_plain("d_w_ple_gate", hp, dlp, BF16).reshape(N_DEV, d // N_DEV, d)
    tok = exchange("ple", ["w_ple_proj", "w_ple_gate"])
    dhp = mm_nt_plain("d_hp", dlp, w_pg_full, F32)
    dx2, d_ple = rms_bwd("rms_ple_bwd", x2, ple_norm + tok, dhp, dx3)

    dz = mm_nt_j("d_z", dx2, wg["w_ffn_down"], BF16)
    grads["w_ffn_down"] = mm_tn_j("d_w_ffn_down", z, dx2, BF16)
    tok = _token_value(send(dz)) + exchange("down", ["w_ffn_down"])
    dg, du, dcw = ffn_mid_bwd1("ffn_mid_bwd1", gpre, u, dz, cw, cb + tok)
    dgpre = ffn_mid_bwd2("ffn_mid_bwd2", dg, cw)
    grads["w_ffn_up"] = mm_tn_cols("d_w_ffn_up", hf, du, N_DEV, nf, BF16, folded=False)
    grads["w_ffn_gate"] = mm_tn_cols("d_w_ffn_gate", hf, dgpre, N_DEV, nf, BF16, folded=False)
    dhf = mm_nt_jsum("d_hf_up", du, wg["w_ffn_up"], F32, folded=False)
    dhf = mm_nt_jsum("d_hf_gate", dgpre, wg["w_ffn_gate"], F32, folded=False, res=dhf)
    tok = _token_value(send(dhf)) + exchange("upgate", ["w_ffn_up", "w_ffn_gate"])
    dx1, d_ffn = rms_bwd("rms_ffn_bwd", x1, ffn_norm + tok, dhf, dx2)

    dmerged = mm_nt_plain("d_merged", dx1, w_out_full, F32)
    grads["w_out"] = mm_tn_plain("d_w_out", merged, dx1, BF16).reshape(N_DEV, d // N_DEV, d)
    dta, dtb, dga, dgb = gate_merge_bwd("gate_merge_bwd", dmerged, proj, ta, tb, d)
    grads["w_branch_a"] = mm_tn_cols("d_w_branch_a", ya, dta, N_DEV, big["w_branch_a"].shape[1], BF16, folded=True)
    grads["w_branch_b"] = mm_tn_cols("d_w_branch_b", yb, dtb, N_DEV, big["w_branch_b"].shape[1], BF16, folded=True)
    dya = mm_nt_jsum("d_ya", dta, wg["w_branch_a"], F32, folded=True)
    dyb = mm_nt_jsum("d_yb", dtb, wg["w_branch_b"], F32, folded=True)
    tok = _token_value(send(dyb)) + exchange("mix", ["w_out", "w_branch_a", "w_branch_b"])
    dqa, dka, dva, dbias_a, dsink = band_attn_bwd("attn_a_bwd", proj, bias_a, sink + tok, dya, ya, lse_a, None, **geo_a)
    douts_b, dlses_b = dil_merge_bwd("dil_merge_bwd", dyb, outs_b, lses_b)
    dq_b, dk_b, dv_b, dbias_b = [], [], [], []
    for g in range(len(B_PATTERNS)):
        dq, dk, dv, db, _ = band_attn_bwd(f"attn_b{g}_bwd", proj, biases_b[g], None, douts_b[g], outs_b[g], lses_b[g],
                                          dlses_b[g], **geo_b[g])
        dq_b.append(dq)
        dk_b.append(dk)
        dv_b.append(dv)
        dbias_b.append(db)
    dproj = jnp.concatenate([t.astype(BF16) for t in [dqa, dka, dva] + dq_b + dk_b + dv_b + [dga, dgb]], axis=1)
    token = send(dproj)
    grads["w_in"] = mm_tn_cols("d_w_in", h, dproj, N_DEV, big["w_in"].shape[1], BF16, folded=True, after=token)
    token = send_sums("in", ["w_in"], [grads["w_in"]], rs_pair("rs_pair_in", [grads["w_in"]]))
    dh = mm_nt_jsum("d_h", dproj, wg["w_in"], F32, folded=True, after=token)
    grad_x, d_attn = rms_bwd("rms_attn_bwd", xs, attn_norm, dh, dx1)

    dt_a = table_grad("table_grad_a", dbias_a, bucket_a)[:, 0, :N_BUCKETS]
    dt_b = [table_grad(f"table_grad_b{g}", dbias_b[g], buckets_b[g])[:, 0, :N_BUCKETS] for g in range(len(B_PATTERNS))]
    d_table_part = jnp.concatenate([dt_a] + dt_b, axis=0).T

    pieces = [
        ("loss", loss_part[0, :1]),
        ("table", d_table_part.reshape(-1)),
        ("attn_norm", d_attn.reshape(-1)),
        ("sink", dsink[:, 0, 0]),
        ("ffn_norm", d_ffn.reshape(-1)),
        ("conv_w", dcw[:, 0:3, :].reshape(-1)),
        ("conv_b", dcw[:, 3, :].reshape(-1)),
        ("ple_norm", d_ple.reshape(-1)),
        ("final_norm", d_final.reshape(-1)),
    ]
    tiles = [_as_tiles(v) for _, v in pieces]
    pack = jnp.concatenate(tiles, axis=0)

    out_g, out_d, out_m, out_v = {}, {}, {}, {}

    def finish(group, after):
        tag, keys, s_sems, r_sems, srcs, lands = group
        srcs, lands = split_wait(f"rs_wait_{tag}", s_sems, r_sems, srcs, lands, rs_plan(len(keys)), after)
        for k, mine, theirs in zip(keys, srcs, lands):
            g, dl, nm, nv = reduce_adam("adam_" + k, mine, theirs, big[k], big_m[k], big_v[k])
            out_g[k], out_d[k], out_m[k], out_v[k] = g[None], dl[None], nm[None], nv[None]
            after = dl
        return after

    after = pack
    for group in rs_started[:-1]:
        after = finish(group, after)
    total = allreduce_small("allreduce_small", pack, after)
    finish(rs_started[-1], total)
    small = {}
    row = 0
    for (nm, v), t in zip(pieces, tiles):
        small[nm] = total[row:row + t.shape[0]].reshape(-1)[:v.shape[0]]
        row += t.shape[0]
    loss = small["loss"][0]
    g_small = dict(
        rel_bias_table=small["table"].reshape(rel_bias_table.shape),
        attn_norm=small["attn_norm"].reshape(attn_norm.shape),
        sink_a=small["sink"].reshape(sink_a.shape),
        ffn_norm=small["ffn_norm"].reshape(ffn_norm.shape),
        conv_w=lax.dynamic_index_in_dim(small["conv_w"].reshape(N_DEV, 3, nf), me, 0, keepdims=False)[None],
        conv_b=small["conv_b"].reshape(conv_b.shape),
        ple_norm=small["ple_norm"].reshape(ple_norm.shape),
        final_norm=small["final_norm"].reshape(1, d),
    )
    w_small = dict(rel_bias_table=(rel_bias_table, m_rel_bias_table, v_rel_bias_table),
                   attn_norm=(attn_norm, m_attn_norm, v_attn_norm), sink_a=(sink_a, m_sink_a, v_sink_a),
                   ffn_norm=(ffn_norm, m_ffn_norm, v_ffn_norm), conv_w=(conv_w, m_conv_w, v_conv_w),
                   conv_b=(conv_b, m_conv_b, v_conv_b), ple_norm=(ple_norm, m_ple_norm, v_ple_norm),
                   final_norm=(final_norm, m_final_norm, v_final_norm))

    for k, (wv, mv, vv) in w_small.items():
        shape = wv.shape
        two_d = (1, shape[0]) if len(shape) == 1 else ((shape[0] * shape[1], shape[2]) if len(shape) == 3 else shape)
        gk = g_small[k].reshape(two_d)
        dl, nm, nv = adam_small("adam_" + k, gk, wv.reshape(two_d), mv.reshape(two_d), vv.reshape(two_d))
        out_g[k], out_d[k], out_m[k], out_v[k] = gk.reshape(shape), dl.reshape(shape), nm.reshape(shape), nv.reshape(shape)

    order = ["rel_bias_table", "attn_norm", "w_in", "sink_a", "w_branch_a", "w_branch_b", "w_out", "ffn_norm",
             "w_ffn_gate", "w_ffn_up", "conv_w", "conv_b", "w_ffn_down", "ple_norm", "w_ple_gate", "w_ple_proj",
             "final_norm"]
    return (loss, grad_x[None], *[out_g[k] for k in order], *[out_d[k] for k in order],
            *[out_m[k] for k in order], *[out_v[k] for k in order])
```

```python
import math

import jax
import jax.numpy as jnp
from jax import lax
from jax.experimental import pallas as pl
from jax.experimental.pallas import tpu as pltpu

F32 = jnp.float32
BF16 = jnp.bfloat16
MESH = pl.DeviceIdType.MESH
N_DEV = 8

HEAD_DIM = 128
A_Q_HEADS = 8
A_KV_HEADS = 2
A_GROUP = A_Q_HEADS // A_KV_HEADS
A_BLOCK = 128
B_PATTERNS = ((128, 1), (512, 4), (2048, 16))
B_HEADS_PER_GROUP = 4
B_HEADS = len(B_PATTERNS) * B_HEADS_PER_GROUP
B_BLOCK = 64
N_BUCKETS = 32
MAX_DISTANCE = 1024
A_Q_W = A_Q_HEADS * HEAD_DIM
A_KV_W = A_KV_HEADS * HEAD_DIM
B_W = B_HEADS * HEAD_DIM
B_OUT_W = B_HEADS_PER_GROUP * HEAD_DIM
COL_QA = 0
COL_KA = COL_QA + A_Q_W
COL_VA = COL_KA + A_KV_W
COL_QB = COL_VA + A_KV_W
COL_KB = COL_QB + B_W
COL_VB = COL_KB + B_W
COL_GATES = COL_VB + B_W
RMS_EPS = 1e-6
NEG_INF = -1e30
ATTN_SCALE = HEAD_DIM ** -0.5
ATTN_Q_ROWS = 256
ATTN_CHAINS = 2

ADAM_LR = 0.001
ADAM_B1 = 0.9
ADAM_B2 = 0.999
ADAM_EPS = 1e-08
ADAM_WD = 0.01
ADAM_STEP = 10

GELU_C = math.sqrt(2.0 / math.pi)
GELU_A = 0.044715

V7X_VMEM_BYTES = 64 * 1024 * 1024
VMEM_CEILING = V7X_VMEM_BYTES - 8 * 1024 * 1024
LANES = 128
SUBLANES = 8


def _pick(n, cands):
    for c in cands:
        if n % c == 0:
            return c
    return n


def _nbytes(shape, dtype):
    n = 1
    for d in shape:
        if d is not None:
            n *= d
    return n * jnp.dtype(dtype).itemsize


def _params(sem, est_bytes):
    limit = int(min(VMEM_CEILING, max(32 * 1024 * 1024, 2 * est_bytes + (8 << 20))))
    return pltpu.CompilerParams(dimension_semantics=sem, vmem_limit_bytes=limit)


def _mm(name, a, b, a_bs, a_im, b_bs, b_im, out_shape, out_dtype, o_bs, o_im, grid, dims,
        res=None, r_bs=None, r_im=None, after=None):
    nk = grid[-1]
    nax = len(grid)
    has_res = res is not None
    has_after = after is not None
    o_tile = tuple(d for d in o_bs if d is not None)

    def body(*refs):
        a_ref, b_ref = refs[:2]
        r_ref = refs[2] if has_res else None
        n_in = 2 + has_res + has_after
        o_ref = refs[n_in]
        rest = refs[n_in + 1:]

        def prod():
            return lax.dot_general(a_ref[...].astype(BF16), b_ref[...].astype(BF16), (dims, ((), ())),
                                   preferred_element_type=F32)

        def finish(r):
            if r_ref is not None:
                r = r + r_ref[...].astype(F32)
            o_ref[...] = r.astype(o_ref.dtype)

        if nk == 1:
            finish(prod())
        else:
            acc = rest[0]
            k = pl.program_id(nax - 1)

            @pl.when(k == 0)
            def _():
                acc[...] = prod()

            @pl.when(k > 0)
            def _():
                acc[...] += prod()

            @pl.when(k == nk - 1)
            def _():
                finish(acc[...])

    in_specs = [pl.BlockSpec(a_bs, a_im), pl.BlockSpec(b_bs, b_im)]
    args = [a, b]
    est = _nbytes(a_bs, a.dtype) + _nbytes(b_bs, b.dtype) + _nbytes(o_bs, out_dtype) + 2 * _nbytes(o_tile, F32)
    if has_res:
        in_specs.append(pl.BlockSpec(r_bs, r_im))
        args.append(res)
        est += _nbytes(r_bs, res.dtype)
    if has_after:
        in_specs.append(pl.BlockSpec(memory_space=pl.ANY))
        args.append(after)
    scratch = [] if nk == 1 else [pltpu.VMEM(o_tile, F32)]
    sem = ("parallel",) * (nax - 1) + ("arbitrary",)
    return pl.pallas_call(
        body, name=name, grid=grid, in_specs=in_specs, out_specs=pl.BlockSpec(o_bs, o_im),
        out_shape=jax.ShapeDtypeStruct(out_shape, out_dtype), scratch_shapes=scratch,
        compiler_params=_params(sem, est))(*args)


TM_CANDS = (1024, 512, 256, 128, 64, 32, 16, 8)
TM_WIDE_CANDS = (2048,) + TM_CANDS
TK_CANDS = (1024, 512, 256, 128)
TN_CANDS = (1024, 512, 256, 128)


def mm_cols(name, a, wg, out_dtype, fold, after=None):
    m, k = a.shape
    nj, _, n = wg.shape
    tm, tk = _pick(m, TM_WIDE_CANDS), _pick(k, TK_CANDS)
    grid = (nj, m // tm, k // tk)
    if fold:
        shape, o_bs, o_im = (m, nj * n), (tm, n), (lambda j, i, kk: (i, j))
    else:
        shape, o_bs, o_im = (nj, m, n), (None, tm, n), (lambda j, i, kk: (j, i, 0))
    return _mm(name, a, wg, (tm, tk), lambda j, i, kk: (i, kk), (None, tk, n), lambda j, i, kk: (j, kk, 0),
               shape, out_dtype, o_bs, o_im, grid, ((1,), (0,)), after=after)


def mm_plain(name, a, w, out_dtype, res=None):
    m, k = a.shape
    n = w.shape[1]
    tm, tk, tn = _pick(m, TM_CANDS), _pick(k, TK_CANDS), _pick(n, TN_CANDS)
    grid = (n // tn, m // tm, k // tk)
    return _mm(name, a, w, (tm, tk), lambda j, i, kk: (i, kk), (tk, tn), lambda j, i, kk: (kk, j),
               (m, n), out_dtype, (tm, tn), lambda j, i, kk: (i, j), grid, ((1,), (0,)),
               res, (tm, tn), lambda j, i, kk: (i, j))


def mm_jsum(name, aj, wg, out_dtype, res=None):
    nj, m, ka = aj.shape
    n = wg.shape[2]
    tm, tn = _pick(m, TM_CANDS), _pick(n, TN_CANDS)
    grid = (m // tm, n // tn, nj)
    return _mm(name, aj, wg, (None, tm, ka), lambda i, jn, j: (j, i, 0), (None, ka, tn), lambda i, jn, j: (j, 0, jn),
               (m, n), out_dtype, (tm, tn), lambda i, jn, j: (i, jn), grid, ((1,), (0,)),
               res, (tm, tn), lambda i, jn, j: (i, jn))


def mm_tn_cols(name, a, g, nj, n, out_dtype, folded, after=None):
    s, kw = a.shape
    ts, tkw = _pick(s, TK_CANDS), _pick(kw, TM_CANDS)
    grid = (nj, kw // tkw, s // ts)
    if folded:
        g_bs, g_im = (ts, n), (lambda j, i, ss: (ss, j))
    else:
        g_bs, g_im = (None, ts, n), (lambda j, i, ss: (j, ss, 0))
    return _mm(name, a, g, (ts, tkw), lambda j, i, ss: (ss, i), g_bs, g_im,
               (nj, kw, n), out_dtype, (None, tkw, n), lambda j, i, ss: (j, i, 0), grid, ((0,), (0,)), after=after)


def mm_tn_plain(name, a, g, out_dtype):
    s, kw = a.shape
    n = g.shape[1]
    ts, tkw, tn = _pick(s, TK_CANDS), _pick(kw, TM_CANDS), _pick(n, TN_CANDS)
    grid = (kw // tkw, n // tn, s // ts)
    return _mm(name, a, g, (ts, tkw), lambda i, jn, ss: (ss, i), (ts, tn), lambda i, jn, ss: (ss, jn),
               (kw, n), out_dtype, (tkw, tn), lambda i, jn, ss: (i, jn), grid, ((0,), (0,)))


def mm_tn_j(name, aj, g, out_dtype):
    nj, s, ka = aj.shape
    n = g.shape[1]
    ts, tn = _pick(s, TK_CANDS), _pick(n, TN_CANDS)
    grid = (nj, n // tn, s // ts)
    return _mm(name, aj, g, (None, ts, ka), lambda j, jn, ss: (j, ss, 0), (ts, tn), lambda j, jn, ss: (ss, jn),
               (nj, ka, n), out_dtype, (None, ka, tn), lambda j, jn, ss: (j, 0, jn), grid, ((0,), (0,)))


def mm_nt_plain(name, g, w, out_dtype):
    m, n = g.shape
    k = w.shape[0]
    tm, tn, tkk = _pick(m, TM_CANDS), _pick(n, TK_CANDS), _pick(k, TN_CANDS)
    grid = (k // tkk, m // tm, n // tn)
    return _mm(name, g, w, (tm, tn), lambda kk, i, jn: (i, jn), (tkk, tn), lambda kk, i, jn: (kk, jn),
               (m, k), out_dtype, (tm, tkk), lambda kk, i, jn: (i, kk), grid, ((1,), (1,)))


def mm_nt_j(name, g, wg, out_dtype):
    m, n = g.shape
    nj, ka, _ = wg.shape
    tm, tn = _pick(m, TM_CANDS), _pick(n, TK_CANDS)
    grid = (nj, m // tm, n // tn)
    return _mm(name, g, wg, (tm, tn), lambda j, i, jn: (i, jn), (None, ka, tn), lambda j, i, jn: (j, 0, jn),
               (nj, m, ka), out_dtype, (None, tm, ka), lambda j, i, jn: (j, i, 0), grid, ((1,), (1,)))


def mm_nt_jsum(name, g, wg, out_dtype, folded, res=None, after=None):
    nj, k, n = wg.shape
    m = g.shape[0] if folded else g.shape[1]
    tm, tkk = _pick(m, TM_CANDS if res is not None else TM_WIDE_CANDS), _pick(k, TN_CANDS)
    grid = (m // tm, k // tkk, nj)
    if folded:
        g_bs, g_im = (tm, n), (lambda i, kk, j: (i, j))
    else:
        g_bs, g_im = (None, tm, n), (lambda i, kk, j: (j, i, 0))
    return _mm(name, g, wg, g_bs, g_im, (None, tkk, n), lambda i, kk, j: (j, kk, 0),
               (m, k), out_dtype, (tm, tkk), lambda i, kk, j: (i, kk), grid, ((1,), (1,)),
               res, (tm, tkk), lambda i, kk, j: (i, kk), after=after)


ROW_TILE_CANDS = (256, 128, 64, 32, 16, 8)


def _rstd(x):
    return lax.rsqrt(jnp.mean(x * x, axis=-1, keepdims=True) + RMS_EPS)


def _sigmoid(t):
    return 1.0 / (1.0 + jnp.exp(-t))


def rms_fwd(name, x, gain):
    s, d = x.shape
    ts = _pick(s, ROW_TILE_CANDS)

    def body(x_ref, g_ref, h_ref):
        xv = x_ref[...]
        h_ref[...] = ((xv * _rstd(xv)) * g_ref[...]).astype(h_ref.dtype)

    return pl.pallas_call(
        body, name=name, grid=(s // ts,),
        in_specs=[pl.BlockSpec((ts, d), lambda i: (i, 0)), pl.BlockSpec((1, d), lambda i: (0, 0))],
        out_specs=pl.BlockSpec((ts, d), lambda i: (i, 0)),
        out_shape=jax.ShapeDtypeStruct((s, d), BF16),
        compiler_params=_params(("parallel",), 3 * ts * d * 4))(x, gain)


def rms_bwd(name, x, gain, dh, dres):
    s, d = x.shape
    ts = _pick(s, ROW_TILE_CANDS)

    def body(x_ref, g_ref, dh_ref, dr_ref, dx_ref, dg_ref):
        xv = x_ref[...]
        r = _rstd(xv)
        xhat = xv * r
        dhv = dh_ref[...].astype(F32)
        dxhat = dhv * g_ref[...]
        dx_ref[...] = dr_ref[...] + r * (dxhat - xhat * jnp.mean(dxhat * xhat, axis=-1, keepdims=True))
        part = jnp.sum(dhv * xhat, axis=0, keepdims=True)

        @pl.when(pl.program_id(0) == 0)
        def _():
            dg_ref[...] = part

        @pl.when(pl.program_id(0) > 0)
        def _():
            dg_ref[...] += part

    row = pl.BlockSpec((ts, d), lambda i: (i, 0))
    vec = pl.BlockSpec((1, d), lambda i: (0, 0))
    return pl.pallas_call(
        body, name=name, grid=(s // ts,), in_specs=[row, vec, row, row], out_specs=[row, vec],
        out_shape=[jax.ShapeDtypeStruct((s, d), F32), jax.ShapeDtypeStruct((1, d), F32)],
        compiler_params=_params(("arbitrary",), 6 * ts * d * 4))(x, gain, dh, dres)


def gate_merge_fwd(name, proj, ta, tb, d):
    s = proj.shape[0]
    ts = _pick(s, ROW_TILE_CANDS)
    cb = COL_GATES // d

    def body(ga_ref, gb_ref, ta_ref, tb_ref, o_ref):
        o_ref[...] = (_sigmoid(ga_ref[...]) * ta_ref[...] + _sigmoid(gb_ref[...]) * tb_ref[...]).astype(o_ref.dtype)

    row = pl.BlockSpec((ts, d), lambda i: (i, 0))
    return pl.pallas_call(
        body, name=name, grid=(s // ts,),
        in_specs=[pl.BlockSpec((ts, d), lambda i: (i, cb)), pl.BlockSpec((ts, d), lambda i: (i, cb + 1)), row, row],
        out_specs=row, out_shape=jax.ShapeDtypeStruct((s, d), BF16),
        compiler_params=_params(("parallel",), 5 * ts * d * 4))(proj, proj, ta, tb)


def gate_merge_bwd(name, dmerged, proj, ta, tb, d):
    s = proj.shape[0]
    ts = _pick(s, ROW_TILE_CANDS)
    cb = COL_GATES // d

    def body(dm_ref, ga_ref, gb_ref, ta_ref, tb_ref, dta_ref, dtb_ref, dga_ref, dgb_ref):
        dm = dm_ref[...]
        sa = _sigmoid(ga_ref[...])
        sb = _sigmoid(gb_ref[...])
        dta_ref[...] = (dm * sa).astype(dta_ref.dtype)
        dtb_ref[...] = (dm * sb).astype(dtb_ref.dtype)
        dga_ref[...] = (dm * ta_ref[...] * (sa * (1.0 - sa))).astype(dga_ref.dtype)
        dgb_ref[...] = (dm * tb_ref[...] * (sb * (1.0 - sb))).astype(dgb_ref.dtype)

    row = pl.BlockSpec((ts, d), lambda i: (i, 0))
    out = jax.ShapeDtypeStruct((s, d), BF16)
    return pl.pallas_call(
        body, name=name, grid=(s // ts,),
        in_specs=[row, pl.BlockSpec((ts, d), lambda i: (i, cb)), pl.BlockSpec((ts, d), lambda i: (i, cb + 1)), row, row],
        out_specs=[row, row, row, row], out_shape=[out, out, out, out],
        compiler_params=_params(("parallel",), 8 * ts * d * 4))(dmerged, proj, proj, ta, tb)


def tail_fwd_bwd(name, x2, lp, pp, gain, target):
    s, d = x2.shape
    ts = _pick(s, ROW_TILE_CANDS)

    def body(x2_ref, lp_ref, pp_ref, g_ref, t_ref, loss_ref, dx3_ref, dlp_ref, dpp_ref, dg_ref):
        gp = _sigmoid(lp_ref[...])
        ppv = pp_ref[...]
        x3 = x2_ref[...] + gp * ppv
        r = _rstd(x3)
        xhat = x3 * r
        gv = g_ref[...]
        err = xhat * gv - t_ref[...]
        loss = jnp.sum(err * err) * (0.5 / d)
        dy = err * (1.0 / d)
        dxhat = dy * gv
        dx3 = r * (dxhat - xhat * jnp.mean(dxhat * xhat, axis=-1, keepdims=True))
        dx3_ref[...] = dx3
        dlp_ref[...] = (dx3 * ppv * (gp * (1.0 - gp))).astype(dlp_ref.dtype)
        dpp_ref[...] = (dx3 * gp).astype(dpp_ref.dtype)
        part = jnp.sum(dy * xhat, axis=0, keepdims=True)
        lossv = jnp.full((1, LANES), loss, F32)

        @pl.when(pl.program_id(0) == 0)
        def _():
            dg_ref[...] = part
            loss_ref[...] = lossv

        @pl.when(pl.program_id(0) > 0)
        def _():
            dg_ref[...] += part
            loss_ref[...] += lossv

    row = pl.BlockSpec((ts, d), lambda i: (i, 0))
    vec = pl.BlockSpec((1, d), lambda i: (0, 0))
    return pl.pallas_call(
        body, name=name, grid=(s // ts,), in_specs=[row, row, row, vec, row],
        out_specs=[pl.BlockSpec((1, LANES), lambda i: (0, 0)), row, row, row, vec],
        out_shape=[jax.ShapeDtypeStruct((1, LANES), F32), jax.ShapeDtypeStruct((s, d), F32),
                   jax.ShapeDtypeStruct((s, d), BF16), jax.ShapeDtypeStruct((s, d), BF16),
                   jax.ShapeDtypeStruct((1, d), F32)],
        compiler_params=_params(("arbitrary",), 9 * ts * d * 4))(x2, lp, pp, gain, target)


HALO = SUBLANES


def _shift_rows(cur, prev_row, next_row):
    ts = cur.shape[0]
    rid = lax.broadcasted_iota(jnp.int32, cur.shape, 0)
    down = jnp.where(rid == 0, prev_row, pltpu.roll(cur, 1, 0))
    up = jnp.where(rid == ts - 1, next_row, pltpu.roll(cur, ts - 1, 0))
    return down, up


def _halo_specs(ts, s, nf):
    nb = ts // HALO
    last = s // HALO - 1
    cur = pl.BlockSpec((None, ts, nf), lambda j, i: (j, i, 0))
    prev = pl.BlockSpec((None, HALO, nf), lambda j, i: (j, jnp.maximum(i * nb - 1, 0), 0))
    nxt = pl.BlockSpec((None, HALO, nf), lambda j, i: (j, jnp.minimum((i + 1) * nb, last), 0))
    return cur, prev, nxt


def _halo_rows(prev_ref, next_ref, n_tiles):
    i = pl.program_id(1)
    prev_row = jnp.where(i == 0, 0.0, prev_ref[HALO - 1:HALO, :].astype(F32))
    next_row = jnp.where(i == n_tiles - 1, 0.0, next_ref[0:1, :].astype(F32))
    return prev_row, next_row


def _gelu(g):
    t = jnp.tanh(GELU_C * (g + GELU_A * (g * g * g)))
    return 0.5 * g * (1.0 + t), t


def _conv(cur, down, up, cw_ref, cb_ref):
    return down * cw_ref[0:1, :] + cur * cw_ref[1:2, :] + up * cw_ref[2:3, :] + cb_ref[...]


def ffn_mid_fwd(name, gpre, u, cw, cb):
    nj, s, nf = gpre.shape
    ts = _pick(s, (512, 256, 128, 64, 32, 16, 8))
    n_tiles = s // ts
    cur, prev, nxt = _halo_specs(ts, s, nf)

    def body(g_ref, gp_ref, gn_ref, u_ref, cw_ref, cb_ref, z_ref):
        gv = g_ref[...]
        down, up = _shift_rows(gv, *_halo_rows(gp_ref, gn_ref, n_tiles))
        act, _ = _gelu(_conv(gv, down, up, cw_ref, cb_ref))
        z_ref[...] = (act * u_ref[...]).astype(z_ref.dtype)

    return pl.pallas_call(
        body, name=name, grid=(nj, n_tiles),
        in_specs=[cur, prev, nxt, cur, pl.BlockSpec((None, SUBLANES, nf), lambda j, i: (j, 0, 0)),
                  pl.BlockSpec((None, 1, nf), lambda j, i: (j, 0, 0))],
        out_specs=cur, out_shape=jax.ShapeDtypeStruct((nj, s, nf), BF16),
        compiler_params=_params(("parallel", "parallel"), 8 * ts * nf * 4))(gpre, gpre, gpre, u, cw, cb)


def ffn_mid_bwd1(name, gpre, u, dz, cw, cb):
    nj, s, nf = gpre.shape
    ts = _pick(s, (512, 256, 128, 64, 32, 16, 8))
    n_tiles = s // ts
    cur, prev, nxt = _halo_specs(ts, s, nf)

    def body(g_ref, gp_ref, gn_ref, u_ref, dz_ref, cw_ref, cb_ref, dg_ref, du_ref, dcw_ref):
        gv = g_ref[...]
        down, up = _shift_rows(gv, *_halo_rows(gp_ref, gn_ref, n_tiles))
        gc = _conv(gv, down, up, cw_ref, cb_ref)
        act, t = _gelu(gc)
        dzv = dz_ref[...].astype(F32)
        du_ref[...] = (dzv * act).astype(du_ref.dtype)
        dact = 0.5 * (1.0 + t) + 0.5 * gc * (1.0 - t * t) * (GELU_C * (1.0 + 3.0 * GELU_A * (gc * gc)))
        dg = dzv * u_ref[...] * dact
        dg_ref[...] = dg
        rows = [jnp.sum(dg * down, axis=0, keepdims=True), jnp.sum(dg * gv, axis=0, keepdims=True),
                jnp.sum(dg * up, axis=0, keepdims=True), jnp.sum(dg, axis=0, keepdims=True)]
        part = jnp.concatenate(rows + [jnp.zeros((SUBLANES - len(rows), nf), F32)], axis=0)

        @pl.when(pl.program_id(1) == 0)
        def _():
            dcw_ref[...] = part

        @pl.when(pl.program_id(1) > 0)
        def _():
            dcw_ref[...] += part

    small = pl.BlockSpec((None, SUBLANES, nf), lambda j, i: (j, 0, 0))
    return pl.pallas_call(
        body, name=name, grid=(nj, n_tiles),
        in_specs=[cur, prev, nxt, cur, cur, small, pl.BlockSpec((None, 1, nf), lambda j, i: (j, 0, 0))],
        out_specs=[cur, cur, small],
        out_shape=[jax.ShapeDtypeStruct((nj, s, nf), F32), jax.ShapeDtypeStruct((nj, s, nf), BF16),
                   jax.ShapeDtypeStruct((nj, SUBLANES, nf), F32)],
        compiler_params=_params(("parallel", "arbitrary"), 12 * ts * nf * 4))(gpre, gpre, gpre, u, dz, cw, cb)


def ffn_mid_bwd2(name, dg, cw):
    nj, s, nf = dg.shape
    ts = _pick(s, (512, 256, 128, 64, 32, 16, 8))
    n_tiles = s // ts
    cur, prev, nxt = _halo_specs(ts, s, nf)

    def body(g_ref, gp_ref, gn_ref, cw_ref, o_ref):
        gv = g_ref[...]
        down, up = _shift_rows(gv, *_halo_rows(gp_ref, gn_ref, n_tiles))
        o_ref[...] = (up * cw_ref[0:1, :] + gv * cw_ref[1:2, :] + down * cw_ref[2:3, :]).astype(o_ref.dtype)

    return pl.pallas_call(
        body, name=name, grid=(nj, n_tiles),
        in_specs=[cur, prev, nxt, pl.BlockSpec((None, SUBLANES, nf), lambda j, i: (j, 0, 0))],
        out_specs=cur, out_shape=jax.ShapeDtypeStruct((nj, s, nf), BF16),
        compiler_params=_params(("parallel", "parallel"), 6 * ts * nf * 4))(dg, dg, dg, cw)


def _t5_bucket(rel):
    half = N_BUCKETS // 2
    max_exact = half // 2
    n = jnp.abs(rel)
    side = jnp.where(rel > 0, half, 0)
    nf = jnp.maximum(n, 1).astype(F32)
    large = max_exact + (jnp.log(nf / max_exact) / math.log(MAX_DISTANCE / max_exact)
                         * (half - max_exact)).astype(jnp.int32)
    large = jnp.minimum(large, half - 1)
    return side + jnp.where(n < max_exact, n, large)


def bucket_tile(rows, half, dil):
    rel = (jnp.arange(rows + 2 * half)[None, :] - half) - jnp.arange(rows)[:, None]
    return _t5_bucket(rel * dil).astype(jnp.int32)


def bias_build(name, table_t, bucket, h0, nh, half):
    blk, kw = bucket.shape

    def body(t_ref, b_ref, o_ref):
        h = pl.program_id(0)
        bv = b_ref[...]
        acc = jnp.zeros((blk, kw), F32)
        for b in range(N_BUCKETS):
            acc = jnp.where(bv == b, t_ref[h0 + h, b], acc)
        qi = lax.broadcasted_iota(jnp.int32, (blk, kw), 0)
        ci = lax.broadcasted_iota(jnp.int32, (blk, kw), 1)
        o_ref[...] = jnp.where(jnp.abs(ci - half - qi) <= half, acc, NEG_INF)

    return pl.pallas_call(
        body, name=name, grid=(nh,),
        in_specs=[pl.BlockSpec(memory_space=pltpu.SMEM), pl.BlockSpec((blk, kw), lambda h: (0, 0))],
        out_specs=pl.BlockSpec((None, blk, kw), lambda h: (h, 0, 0)),
        out_shape=jax.ShapeDtypeStruct((nh, blk, kw), F32),
        compiler_params=_params(("parallel",), 4 * blk * kw * 4))(table_t, bucket)


def table_grad(name, dbias, bucket):
    nh, blk, kw = dbias.shape

    def body(d_ref, b_ref, o_ref):
        bv = b_ref[...]
        dv = d_ref[...]
        lane = lax.broadcasted_iota(jnp.int32, (SUBLANES, LANES), 1)
        acc = jnp.zeros((SUBLANES, LANES), F32)
        for b in range(N_BUCKETS):
            acc = jnp.where(lane == b, jnp.sum(jnp.where(bv == b, dv, 0.0)), acc)
        o_ref[...] = acc

    return pl.pallas_call(
        body, name=name, grid=(nh,),
        in_specs=[pl.BlockSpec((None, blk, kw), lambda h: (h, 0, 0)), pl.BlockSpec((blk, kw), lambda h: (0, 0))],
        out_specs=pl.BlockSpec((None, SUBLANES, LANES), lambda h: (h, 0, 0)),
        out_shape=jax.ShapeDtypeStruct((nh, SUBLANES, LANES), F32),
        compiler_params=_params(("parallel",), 4 * blk * kw * 4))(dbias, bucket)


class _Band:
    def __init__(self, s, half, q_rows, n_chains, dil):
        self.s, self.half, self.dil, self.n_chains = s, half, dil, n_chains
        self.seg = s // dil
        self.q_rows = min(q_rows, self.seg)
        self.win = self.q_rows + 2 * half
        self.pad = self.seg + 2 * half
        self.nsb = self.seg // self.q_rows
        self.n_items = dil * self.nsb
        assert self.n_items % n_chains == 0 and self.seg % self.q_rows == 0
        self.staged = dil > 1

    def rows_of(self, r):
        return pl.ds(r, self.seg, stride=self.dil) if self.dil > 1 else slice(None)

    def stage_kv(self, dst, src_ref):
        zeros = jnp.zeros((self.half, HEAD_DIM), dst.dtype)
        for r in range(self.dil):
            base = r * self.pad
            dst[base:base + self.half, :] = zeros
            dst[base + self.half + self.seg:base + self.pad, :] = zeros
            dst[base + self.half:base + self.half + self.seg, :] = src_ref[self.rows_of(r), :].astype(dst.dtype)

    def stage(self, dst, src_ref):
        for r in range(self.dil):
            dst[r * self.seg:(r + 1) * self.seg, :] = src_ref[self.rows_of(r), :].astype(dst.dtype)

    def unstage(self, dst_ref, src, add=False):
        for r in range(self.dil):
            val = src[r * self.seg:(r + 1) * self.seg, :].astype(dst_ref.dtype)
            if add:
                val = val + dst_ref[self.rows_of(r), :]
            dst_ref[self.rows_of(r), :] = val

    def offsets(self, item):
        r, sb = item // self.nsb, item % self.nsb
        qoff = pl.multiple_of(r * self.seg + sb * self.q_rows, self.q_rows)
        koff = pl.multiple_of(r * self.pad + sb * self.q_rows, B_BLOCK)
        kpos = sb * self.q_rows - self.half + lax.broadcasted_iota(jnp.int32, (1, self.win), 1)
        edge = jnp.where((kpos >= 0) & (kpos < self.seg), 0.0, NEG_INF)
        return qoff, koff, edge


def band_attn_fwd(name, proj, bias, sink, *, half, q_rows, n_chains, dil, nh, group, cq, ck, cv):
    s, w = proj.shape
    g = _Band(s, half, q_rows, n_chains, dil)
    has_sink = sink is not None

    def body(*refs):
        q_ref, k_ref, v_ref, b_ref = refs[:4]
        s_ref = refs[4] if has_sink else None
        o_ref, l_ref, ks, vs = refs[4 + has_sink:8 + has_sink]
        qs, os_, ls = refs[8 + has_sink:] if g.staged else (None, o_ref, l_ref)
        g.stage_kv(ks, k_ref)
        g.stage_kv(vs, v_ref)
        if g.staged:
            g.stage(qs, q_ref)
        bias_v = b_ref[...]
        sk = s_ref[pl.program_id(0)] if has_sink else None

        def chain(item):
            qoff, koff, edge = g.offsets(item)
            rows = pl.ds(qoff, g.q_rows)
            qv = qs[rows, :] if g.staged else q_ref[rows, :].astype(BF16)
            kw_ = ks[pl.ds(koff, g.win), :]
            vw_ = vs[pl.ds(koff, g.win), :]
            sc = lax.dot_general(qv, kw_, (((1,), (1,)), ((), ())), preferred_element_type=F32) * ATTN_SCALE
            sc = sc + bias_v + edge
            m = jnp.max(sc, axis=-1, keepdims=True)
            if has_sink:
                m = jnp.maximum(m, sk)
            p = jnp.exp(sc - m)
            den = jnp.sum(p, axis=-1, keepdims=True)
            if has_sink:
                den = den + jnp.exp(sk - m)
            out = lax.dot_general(p.astype(BF16), vw_, (((1,), (0,)), ((), ())), preferred_element_type=F32)
            return rows, out / den, jnp.broadcast_to(m + jnp.log(den), (g.q_rows, HEAD_DIM))

        def step(i, carry):
            for rows, out, lse in [chain(i * n_chains + u) for u in range(n_chains)]:
                os_[rows, :] = out
                ls[rows, :] = lse
            return carry

        lax.fori_loop(0, g.n_items // n_chains, step, 0)
        if g.staged:
            g.unstage(o_ref, os_)
            g.unstage(l_ref, ls)

    def col(c0, per):
        return pl.BlockSpec((s, HEAD_DIM), lambda h: (0, c0 // LANES + h // per))

    in_specs = [col(cq, 1), col(ck, group), col(cv, group),
                pl.BlockSpec((None, g.q_rows, g.win), lambda h: (h, 0, 0))]
    args = [proj, proj, proj, bias]
    if has_sink:
        in_specs.append(pl.BlockSpec(memory_space=pltpu.SMEM))
        args.append(sink)
    shape = jax.ShapeDtypeStruct((s, nh * HEAD_DIM), F32)
    scratch = [pltpu.VMEM((dil * g.pad, HEAD_DIM), BF16), pltpu.VMEM((dil * g.pad, HEAD_DIM), BF16)]
    if g.staged:
        scratch += [pltpu.VMEM((s, HEAD_DIM), BF16), pltpu.VMEM((s, HEAD_DIM), F32), pltpu.VMEM((s, HEAD_DIM), F32)]
    return pl.pallas_call(
        body, name=name, grid=(nh,), in_specs=in_specs, out_specs=[col(0, 1), col(0, 1)], out_shape=[shape, shape],
        scratch_shapes=scratch, compiler_params=_params(("parallel",), 16 * s * HEAD_DIM * 4))(*args)


def band_attn_bwd(name, proj, bias, sink, dout, out, lse, dlse, *, half, q_rows, n_chains, dil, nh, group, cq, ck, cv):
    s, w = proj.shape
    g = _Band(s, half, q_rows, n_chains, dil)
    nkv = nh // group
    has_sink = sink is not None
    has_dl = dlse is not None
    n_in = 7 + int(has_sink) + int(has_dl)
    n_out = 4 + int(has_sink)

    def body(*refs):
        ins, outs, scr = refs[:n_in], refs[n_in:n_in + n_out], refs[n_in + n_out:]
        q_ref, k_ref, v_ref, b_ref, do_ref, o_ref, l_ref = ins[:7]
        s_ref = ins[7] if has_sink else None
        dl_ref = ins[n_in - 1] if has_dl else None
        dq_ref, dk_ref, dv_ref, db_ref = outs[:4]
        ks, vs, dks, dvs = scr[:4]
        scr = list(scr[4:])
        dsa = scr.pop(0) if has_sink else None
        if g.staged:
            qs, dos, os_, ls, dqs = scr[:5]
            dls = scr[5] if has_dl else None
            g.stage(qs, q_ref)
            g.stage(dos, do_ref)
            g.stage(os_, o_ref)
            g.stage(ls, l_ref)
            if has_dl:
                g.stage(dls, dl_ref)
        else:
            qs, dos, os_, ls, dqs, dls = None, do_ref, o_ref, l_ref, dq_ref, dl_ref
        h = pl.program_id(0)
        g.stage_kv(ks, k_ref)
        g.stage_kv(vs, v_ref)
        dks[...] = jnp.zeros_like(dks)
        dvs[...] = jnp.zeros_like(dvs)
        db_ref[...] = jnp.zeros_like(db_ref)
        bias_v = b_ref[...]
        if has_sink:
            sk = s_ref[h]
            dsa[...] = jnp.zeros_like(dsa)

        def chain(item):
            qoff, koff, edge = g.offsets(item)
            rows = pl.ds(qoff, g.q_rows)
            win = pl.ds(koff, g.win)
            qv = qs[rows, :] if g.staged else q_ref[rows, :].astype(BF16)
            kw_ = ks[win, :]
            vw_ = vs[win, :]
            sc = lax.dot_general(qv, kw_, (((1,), (1,)), ((), ())), preferred_element_type=F32) * ATTN_SCALE
            lv = ls[rows, :][:, 0:1]
            p = jnp.exp(sc + bias_v + edge - lv)
            dov = dos[rows, :]
            delta = jnp.sum(dov * os_[rows, :], axis=-1, keepdims=True)
            dob = dov.astype(BF16)
            dp = lax.dot_general(dob, vw_, (((1,), (1,)), ((), ())), preferred_element_type=F32)
            t = dp - delta
            if has_dl:
                t = t + dls[rows, :][:, 0:1]
            ds = p * t
            dsb = (ds * ATTN_SCALE).astype(BF16)
            dq = lax.dot_general(dsb, kw_, (((1,), (0,)), ((), ())), preferred_element_type=F32)
            dkc = lax.dot_general(dsb, qv, (((0,), (0,)), ((), ())), preferred_element_type=F32)
            dvc = lax.dot_general(p.astype(BF16), dob, (((0,), (0,)), ((), ())), preferred_element_type=F32)
            dsk = jnp.exp(sk - lv) * delta if has_sink else None
            return rows, win, dq, dkc, dvc, ds, dsk

        def step(i, carry):
            res = [chain(i * n_chains + u) for u in range(n_chains)]
            ds_sum = res[0][5]
            for rr in res[1:]:
                ds_sum = ds_sum + rr[5]
            db_ref[...] += ds_sum
            for rows, win, dq, dkc, dvc, ds, dsk in res:
                dqs[rows, :] = dq
                dks[win, :] += dkc
                dvs[win, :] += dvc
                if has_sink:
                    dsa[...] += dsk
            return carry

        lax.fori_loop(0, g.n_items // n_chains, step, 0)

        if g.staged:
            g.unstage(dq_ref, dqs)

        def emit_kv(add):
            for r in range(dil):
                lo = r * g.pad + half
                for dst_ref, src in ((dk_ref, dks), (dv_ref, dvs)):
                    val = src[lo:lo + g.seg, :]
                    if add:
                        val = val + dst_ref[g.rows_of(r), :]
                    dst_ref[g.rows_of(r), :] = val

        if group == 1:
            emit_kv(False)
        else:
            @pl.when(h % group == 0)
            def _():
                emit_kv(False)

            @pl.when(h % group != 0)
            def _():
                emit_kv(True)
        if has_sink:
            outs[4][...] = jnp.full((SUBLANES, LANES), -jnp.sum(dsa[...]), F32)

    def col(c0, per):
        return pl.BlockSpec((s, HEAD_DIM), lambda h: (0, c0 // LANES + h // per))

    b_spec = pl.BlockSpec((None, g.q_rows, g.win), lambda h: (h, 0, 0))
    in_specs = [col(cq, 1), col(ck, group), col(cv, group), b_spec, col(0, 1), col(0, 1), col(0, 1)]
    args = [proj, proj, proj, bias, dout, out, lse]
    if has_sink:
        in_specs.append(pl.BlockSpec(memory_space=pltpu.SMEM))
        args.append(sink)
    if has_dl:
        in_specs.append(col(0, 1))
        args.append(dlse)
    out_specs = [col(0, 1), col(0, group), col(0, group), b_spec]
    out_shape = [jax.ShapeDtypeStruct((s, nh * HEAD_DIM), F32), jax.ShapeDtypeStruct((s, nkv * HEAD_DIM), F32),
                 jax.ShapeDtypeStruct((s, nkv * HEAD_DIM), F32), jax.ShapeDtypeStruct((nh, g.q_rows, g.win), F32)]
    scratch = [pltpu.VMEM((dil * g.pad, HEAD_DIM), BF16), pltpu.VMEM((dil * g.pad, HEAD_DIM), BF16),
               pltpu.VMEM((dil * g.pad, HEAD_DIM), F32), pltpu.VMEM((dil * g.pad, HEAD_DIM), F32)]
    if has_sink:
        out_specs.append(pl.BlockSpec((None, SUBLANES, LANES), lambda h: (h, 0, 0)))
        out_shape.append(jax.ShapeDtypeStruct((nh, SUBLANES, LANES), F32))
        scratch.append(pltpu.VMEM((g.q_rows, 1), F32))
    if g.staged:
        scratch += [pltpu.VMEM((s, HEAD_DIM), BF16)] + [pltpu.VMEM((s, HEAD_DIM), F32)] * (4 + int(has_dl))
    res = pl.pallas_call(
        body, name=name, grid=(nh,), in_specs=in_specs, out_specs=out_specs, out_shape=out_shape,
        scratch_shapes=scratch, compiler_params=_params(("arbitrary",), 28 * s * HEAD_DIM * 4))(*args)
    return res[0], res[1], res[2], res[3], (res[4] if has_sink else None)


def dil_merge_fwd(name, outs, lses):
    s, w = outs[0].shape
    ts = _pick(s, ROW_TILE_CANDS)
    ng = len(outs)

    def body(*refs):
        o_refs, l_refs, y_ref = refs[:ng], refs[ng:2 * ng], refs[2 * ng]
        ls = [l[...] for l in l_refs]
        mx = ls[0]
        for l in ls[1:]:
            mx = jnp.maximum(mx, l)
        es = [jnp.exp(l - mx) for l in ls]
        tot = es[0]
        for e in es[1:]:
            tot = tot + e
        acc = (es[0] / tot) * o_refs[0][...]
        for e, o in zip(es[1:], o_refs[1:]):
            acc = acc + (e / tot) * o[...]
        y_ref[...] = acc.astype(y_ref.dtype)

    row = pl.BlockSpec((ts, w), lambda i: (i, 0))
    return pl.pallas_call(
        body, name=name, grid=(s // ts,), in_specs=[row] * (2 * ng), out_specs=row,
        out_shape=jax.ShapeDtypeStruct((s, w), BF16),
        compiler_params=_params(("parallel",), 10 * ts * w * 4))(*outs, *lses)


def dil_merge_bwd(name, dy, outs, lses):
    s, w = outs[0].shape
    ts = _pick(s, ROW_TILE_CANDS)
    ng = len(outs)
    nhead = w // HEAD_DIM

    def body(*refs):
        dy_ref = refs[0]
        o_refs, l_refs = refs[1:1 + ng], refs[1 + ng:1 + 2 * ng]
        do_refs, dl_refs = refs[1 + 2 * ng:1 + 3 * ng], refs[1 + 3 * ng:1 + 4 * ng]
        for hh in range(nhead):
            cols = slice(hh * HEAD_DIM, (hh + 1) * HEAD_DIM)
            dyv = dy_ref[:, cols]
            ls = [l[:, cols] for l in l_refs]
            mx = ls[0]
            for l in ls[1:]:
                mx = jnp.maximum(mx, l)
            es = [jnp.exp(l - mx) for l in ls]
            tot = es[0]
            for e in es[1:]:
                tot = tot + e
            alphas = [e / tot for e in es]
            dal = [jnp.broadcast_to(jnp.sum(dyv * o[:, cols], axis=-1, keepdims=True), dyv.shape) for o in o_refs]
            mean = alphas[0] * dal[0]
            for a, d in zip(alphas[1:], dal[1:]):
                mean = mean + a * d
            for g in range(ng):
                do_refs[g][:, cols] = alphas[g] * dyv
                dl_refs[g][:, cols] = alphas[g] * (dal[g] - mean)

    row = pl.BlockSpec((ts, w), lambda i: (i, 0))
    shape = jax.ShapeDtypeStruct((s, w), F32)
    res = pl.pallas_call(
        body, name=name, grid=(s // ts,), in_specs=[row] * (1 + 2 * ng), out_specs=[row] * (2 * ng),
        out_shape=[shape] * (2 * ng),
        compiler_params=_params(("parallel",), 16 * ts * w * 4))(dy, *outs, *lses)
    return res[:ng], res[ng:]


def _adamw(w, g, m, v):
    m = ADAM_B1 * m + (1.0 - ADAM_B1) * g
    v = ADAM_B2 * v + (1.0 - ADAM_B2) * (g * g)
    m_hat = m / (1.0 - ADAM_B1 ** ADAM_STEP)
    v_hat = v / (1.0 - ADAM_B2 ** ADAM_STEP)
    delta = -ADAM_LR * (m_hat / (jnp.sqrt(v_hat) + ADAM_EPS) + ADAM_WD * w)
    return delta, m, v


def _row_tile(r, c, budget=1 << 20):
    if r * c * 4 <= budget or r % SUBLANES:
        return r
    for t in (1024, 512, 256, 128, 64, 32, 16, 8):
        if r % t == 0 and t * c * 4 <= budget:
            return t
    return SUBLANES


def adam_small(name, g, w, m, v):
    def body(g_ref, w_ref, m_ref, v_ref, d_ref, nm_ref, nv_ref):
        d_ref[...], nm_ref[...], nv_ref[...] = _adamw(w_ref[...], g_ref[...], m_ref[...], v_ref[...])

    shape = jax.ShapeDtypeStruct(w.shape, F32)
    return pl.pallas_call(body, name=name, out_shape=[shape, shape, shape])(g, w, m, v)


def reduce_adam(name, mine, theirs, w, m, v):
    nq, r, c = mine.shape
    tr = _row_tile(r, c)

    def body(*refs):
        parts, (w_ref, m_ref, v_ref, g_ref, d_ref, nm_ref, nv_ref) = refs[:nq], refs[nq:]
        g = parts[0][...].astype(F32)
        for p_ref in parts[1:]:
            g = g + p_ref[...].astype(F32)
        g_ref[...] = g
        d_ref[...], nm_ref[...], nv_ref[...] = _adamw(w_ref[...], g, m_ref[...], v_ref[...])

    def slot(q):
        return pl.BlockSpec((None, tr, c), lambda i: (q, i, 0))

    row = pl.BlockSpec((tr, c), lambda i: (i, 0))
    shape = jax.ShapeDtypeStruct((r, c), F32)
    return pl.pallas_call(
        body, name=name, grid=(r // tr,), in_specs=[slot(q) for q in range(nq)] + [row, row, row],
        out_specs=[row] * 4, out_shape=[shape] * 4,
        compiler_params=_params(("parallel",), (nq * 2 + 7 * 4) * tr * c))(mine, *[theirs] * (nq - 1), w, m, v)


def _place():
    return lax.axis_index("x"), lax.axis_index("y"), lax.axis_index("c")


def _flip(pos, bits):
    return tuple((1 - p) if b else p for p, b in zip(pos, bits))


def _index(pos):
    return 4 * pos[0] + 2 * pos[1] + pos[2]


ANY = pl.BlockSpec(memory_space=pl.ANY)


HBM = pl.BlockSpec(memory_space=pltpu.HBM)
SEM = pl.BlockSpec(memory_space=pltpu.SEMAPHORE)
EFFECT = pltpu.SideEffectType.DATAFLOW_SIDE_EFFECTING
TO_SIBLING = (0, 0, 1)
TO_CHIPS = [(1, 0, 0), (0, 1, 0), (1, 1, 0)]


def _in_hbm(a):
    return pltpu.with_memory_space_constraint(a, pltpu.HBM)


def _token_value(token):
    return token[0, 0]


def _when(pred, fn):
    if pred is True:
        fn()
    elif pred is not False:
        pl.when(pred)(fn)


def _plan_copy(k, entry, ins, lnd, send_sems, recv_sems):
    a, src_a, sblk, lblk, to, send_if, recv_if = entry
    src = lnd[a] if src_a is None else ins[src_a]
    return pltpu.make_async_remote_copy(
        src_ref=src.at[sblk], dst_ref=lnd[a].at[lblk], send_sem=send_sems.at[k], recv_sem=recv_sems.at[k],
        device_id=to, device_id_type=MESH), send_if, recv_if


def split_start(name, srcs, lands, plan, after):
    ns, nl = len(srcs), len(lands)
    n_copies = len(plan((0, 0, 0)))

    def body(*refs):
        ins, lnd = refs[:ns], refs[ns:ns + nl]
        send_sems, recv_sems = refs[ns + nl + 1], refs[ns + nl + 2]
        token = refs[-1]
        for k, entry in enumerate(plan(_place())):
            cp, send_if, _ = _plan_copy(k, entry, ins, lnd, send_sems, recv_sems)
            _when(send_if, cp.start)
        token[...] = jnp.zeros_like(token)

    outs = pl.pallas_call(
        body, name=name,
        out_shape=(pltpu.SemaphoreType.DMA((n_copies,)), pltpu.SemaphoreType.DMA((n_copies,)),
                   *[pltpu.HBM(a.shape, a.dtype) for a in srcs], *[pltpu.HBM(a.shape, a.dtype) for a in lands],
                   jax.ShapeDtypeStruct((SUBLANES, LANES), F32)),
        in_specs=[HBM] * (ns + nl) + [ANY],
        out_specs=(SEM, SEM, *[HBM] * (ns + nl), pl.BlockSpec(memory_space=pltpu.VMEM)),
        input_output_aliases={i: 2 + i for i in range(ns + nl)},
        compiler_params=pltpu.CompilerParams(has_side_effects=EFFECT),
    )(*[_in_hbm(a) for a in srcs], *[_in_hbm(a) for a in lands], after)
    return outs[0], outs[1], list(outs[2:2 + ns]), list(outs[2 + ns:2 + ns + nl]), outs[-1]


def split_wait(name, send_sems, recv_sems, srcs, lands, plan, after):
    ns, nl = len(srcs), len(lands)

    def body(*refs):
        ins, lnd = refs[:ns], refs[ns:ns + nl]
        s_sems, r_sems = refs[ns + nl], refs[ns + nl + 1]
        for k, entry in enumerate(plan(_place())):
            cp, send_if, recv_if = _plan_copy(k, entry, ins, lnd, s_sems, r_sems)
            _when(send_if, cp.wait_send)
            _when(recv_if, cp.wait_recv)

    outs = pl.pallas_call(
        body, name=name,
        out_shape=(*[pltpu.HBM(a.shape, a.dtype) for a in srcs], *[pltpu.HBM(a.shape, a.dtype) for a in lands]),
        in_specs=[HBM] * (ns + nl) + [SEM, SEM, ANY],
        out_specs=tuple([HBM] * (ns + nl)),
        input_output_aliases={i: i for i in range(ns + nl)},
        compiler_params=pltpu.CompilerParams(has_side_effects=EFFECT),
    )(*srcs, *lands, send_sems, recv_sems, after)
    return list(outs[:ns]), list(outs[ns:])


NORTH = 1


def ag_plan(n):
    def plan(me):
        x, y, c = me
        entries = []
        for a in range(n):
            for t in (NORTH, 1 - NORTH):
                blk = _index((x, y, t))
                for rel in TO_CHIPS:
                    entries.append((a, None, blk, blk, _flip((x, y, t), rel), c == NORTH, c == t))
        return entries
    return plan


def ag_pair(name, lands, after):
    n = len(lands)

    def body(*refs):
        lnd = refs[n + 1:2 * n + 1]
        token = refs[2 * n + 1]
        send_sems, recv_sems = refs[2 * n + 2:]
        token[...] = jnp.zeros_like(token)
        me = _place()
        sibling = _flip(me, TO_SIBLING)
        copies = []
        for a in range(n):
            mine, theirs = lnd[a].at[_index(me)], lnd[a].at[_index(sibling)]
            cp = pltpu.make_async_remote_copy(src_ref=mine, dst_ref=mine, send_sem=send_sems.at[a],
                                              recv_sem=recv_sems.at[a], device_id=sibling, device_id_type=MESH)
            cp.start()
            copies.append((cp, pltpu.make_async_remote_copy(
                src_ref=mine, dst_ref=theirs, send_sem=send_sems.at[a], recv_sem=recv_sems.at[a], device_id=sibling,
                device_id_type=MESH)))
        for cp, arrival in copies:
            arrival.wait_recv()
        for cp, arrival in copies:
            cp.wait_send()

    outs = pl.pallas_call(
        body, name=name, in_specs=[ANY] * (n + 1), out_specs=[ANY] * n + [pl.BlockSpec(memory_space=pltpu.VMEM)],
        out_shape=[jax.ShapeDtypeStruct(l.shape, l.dtype) for l in lands]
        + [jax.ShapeDtypeStruct((SUBLANES, LANES), F32)],
        input_output_aliases={a: a for a in range(n)},
        scratch_shapes=[pltpu.SemaphoreType.DMA((n,)), pltpu.SemaphoreType.DMA((n,))],
    )(*lands, after)
    return list(outs[:n]), outs[n]


def ag_start(name, lands, after):
    return split_start(name, [], lands, ag_plan(len(lands)), after)


def pass_plan(n):
    def plan(me):
        sibling = _flip(me, TO_SIBLING)
        return [(a, None, _index(_flip(me, rel)), _index(_flip(me, rel)), sibling, True, True)
                for a in range(n) for rel in TO_CHIPS]
    return plan


def ag_finish(name, lands):
    n = len(lands)

    def body(*refs):
        lnd = refs[n:2 * n]
        send_sems, recv_sems = refs[2 * n:]
        me = _place()
        sibling = _flip(me, TO_SIBLING)
        copies = []
        for a in range(n):
            for j, rel in enumerate(TO_CHIPS):
                blk = lnd[a].at[_index(_flip(me, rel))]
                there = lnd[a].at[_index(_flip(sibling, rel))]
                cp = pltpu.make_async_remote_copy(
                    src_ref=blk, dst_ref=blk, send_sem=send_sems.at[a * 3 + j], recv_sem=recv_sems.at[a * 3 + j],
                    device_id=sibling, device_id_type=MESH)
                cp.start()
                copies.append((cp, pltpu.make_async_remote_copy(
                    src_ref=blk, dst_ref=there, send_sem=send_sems.at[a * 3 + j], recv_sem=recv_sems.at[a * 3 + j],
                    device_id=sibling, device_id_type=MESH)))
        for cp, arrival in copies:
            arrival.wait_recv()
        for cp, arrival in copies:
            cp.wait_send()

    return pl.pallas_call(
        body, name=name, in_specs=[ANY] * n, out_specs=[ANY] * n,
        out_shape=[jax.ShapeDtypeStruct(l.shape, l.dtype) for l in lands],
        input_output_aliases={a: a for a in range(n)},
        scratch_shapes=[pltpu.SemaphoreType.DMA((3 * n,)), pltpu.SemaphoreType.DMA((3 * n,))],
    )(*lands)


REL = [(b >> 2 & 1, b >> 1 & 1, b & 1) for b in range(N_DEV)]


CHIP_REL = [(0, 0, 0)] + TO_CHIPS
N_CHIPS = len(CHIP_REL)


def rs_pair(name, parts):
    n = len(parts)

    def body(*refs):
        ins, got = refs[:n], refs[n:2 * n]
        send_sems, recv_sems = refs[2 * n:]
        me = _place()
        sibling = _flip(me, TO_SIBLING)
        remote = []
        for a in range(n):
            for q, rel in enumerate(CHIP_REL):
                k = a * N_CHIPS + q
                cp = pltpu.make_async_remote_copy(
                    src_ref=ins[a].at[_index(_flip(sibling, rel))], dst_ref=got[a].at[q], send_sem=send_sems.at[k],
                    recv_sem=recv_sems.at[k], device_id=sibling, device_id_type=MESH)
                cp.start()
                remote.append(cp)
        for cp in remote:
            cp.wait_recv()
        for cp in remote:
            cp.wait_send()

    shapes = [jax.ShapeDtypeStruct((N_CHIPS,) + tuple(p.shape[1:]), p.dtype) for p in parts]
    res = pl.pallas_call(
        body, name=name, in_specs=[ANY] * n, out_specs=[ANY] * n, out_shape=shapes,
        scratch_shapes=[pltpu.SemaphoreType.DMA((N_CHIPS * n,)), pltpu.SemaphoreType.DMA((N_CHIPS * n,))],
    )(*parts)
    return list(res)


def own_blocks():
    me = _place()
    return jnp.stack([_index(_flip(me, rel)) for rel in CHIP_REL]).astype(jnp.int32)


def pair_add(name, blocks, parts, got):
    nq, r, c = got.shape
    tr = _row_tile(r, c)

    def body(blk_ref, a_ref, b_ref, o_ref):
        o_ref[...] = (a_ref[...].astype(F32) + b_ref[...].astype(F32)).astype(o_ref.dtype)

    spec = pl.BlockSpec((None, tr, c), lambda q, i, blk: (q, i, 0))
    return pl.pallas_call(
        body, name=name,
        grid_spec=pltpu.PrefetchScalarGridSpec(
            num_scalar_prefetch=1, grid=(nq, r // tr),
            in_specs=[pl.BlockSpec((None, tr, c), lambda q, i, blk: (blk[q], i, 0)), spec], out_specs=spec),
        out_shape=jax.ShapeDtypeStruct(got.shape, got.dtype),
        compiler_params=_params(("arbitrary", "arbitrary"), 6 * tr * c * 2))(blocks, parts, got)


def rs_pair_plan(n):
    def plan(me):
        sibling = _flip(me, TO_SIBLING)
        return [(a, a, _index(_flip(sibling, rel)), q, sibling, True, True)
                for a in range(n) for q, rel in enumerate(CHIP_REL)]
    return plan


def rs_plan(n):
    def plan(me):
        return [(a, a, q, q, _flip(me, CHIP_REL[q]), True, True) for a in range(n) for q in range(1, N_CHIPS)]
    return plan


def rs_start(name, sums, after):
    lands = [lax.empty(t.shape, t.dtype) for t in sums]
    return split_start(name, sums, lands, rs_plan(len(sums)), after)


def allreduce_small(name, pack, after):
    rows, lanes = pack.shape

    def body(x_ref, after_ref, o_ref, land, send_sems, recv_sems):
        me = _place()
        idx = _index(me)
        land[idx] = x_ref[...]
        copies = []
        for r in range(1, N_DEV):
            peer = _flip(me, REL[r])
            cp = pltpu.make_async_remote_copy(
                src_ref=x_ref, dst_ref=land.at[idx], send_sem=send_sems.at[r - 1], recv_sem=recv_sems.at[r - 1],
                device_id=peer, device_id_type=MESH)
            cp.start()
            copies.append(cp)
        for cp in copies:
            cp.wait_recv()
        for cp in copies:
            cp.wait_send()
        acc = land[0]
        for i in range(1, N_DEV):
            acc = acc + land[i]
        o_ref[...] = acc

    return pl.pallas_call(
        body, name=name, in_specs=[pl.BlockSpec(memory_space=pltpu.VMEM), ANY],
        out_specs=pl.BlockSpec(memory_space=pltpu.VMEM), out_shape=jax.ShapeDtypeStruct((rows, lanes), F32),
        scratch_shapes=[pltpu.VMEM((N_DEV, rows, lanes), F32), pltpu.SemaphoreType.DMA((7,)),
                        pltpu.SemaphoreType.DMA((7,))],
    )(pack, after)


def _pad_rows(a, rows):
    return jnp.pad(a, ((0, rows - a.shape[0]), (0, 0)))


def _as_tiles(vec):
    n = vec.shape[0]
    rows = -(-n // LANES)
    rows = -(-rows // SUBLANES) * SUBLANES
    return jnp.pad(vec, (0, rows * LANES - n)).reshape(rows, LANES)


def kernel(x, p, rel_bias_table, attn_norm, w_in, sink_a, w_branch_a, w_branch_b, w_out, ffn_norm, w_ffn_gate, w_ffn_up, conv_w, conv_b, w_ffn_down, ple_norm, w_ple_gate, w_ple_proj, final_norm, loss_target, m_rel_bias_table, m_attn_norm, m_w_in, m_sink_a, m_w_branch_a, m_w_branch_b, m_w_out, m_ffn_norm, m_w_ffn_gate, m_w_ffn_up, m_conv_w, m_conv_b, m_w_ffn_down, m_ple_norm, m_w_ple_gate, m_w_ple_proj, m_final_norm, v_rel_bias_table, v_attn_norm, v_w_in, v_sink_a, v_w_branch_a, v_w_branch_b, v_w_out, v_ffn_norm, v_w_ffn_gate, v_w_ffn_up, v_conv_w, v_conv_b, v_w_ffn_down, v_ple_norm, v_w_ple_gate, v_w_ple_proj, v_final_norm):
    xs = x[0]
    s, d = xs.shape
    ps = p[0, 0]
    target = loss_target[0]
    me = 4 * lax.axis_index("x") + 2 * lax.axis_index("y") + lax.axis_index("c")

    big = dict(w_in=w_in[0], w_branch_a=w_branch_a[0], w_branch_b=w_branch_b[0], w_out=w_out[0],
               w_ffn_gate=w_ffn_gate[0], w_ffn_up=w_ffn_up[0], w_ffn_down=w_ffn_down[0],
               w_ple_gate=w_ple_gate[0], w_ple_proj=w_ple_proj[0])
    big_m = dict(w_in=m_w_in[0], w_branch_a=m_w_branch_a[0], w_branch_b=m_w_branch_b[0], w_out=m_w_out[0],
                 w_ffn_gate=m_w_ffn_gate[0], w_ffn_up=m_w_ffn_up[0], w_ffn_down=m_w_ffn_down[0],
                 w_ple_gate=m_w_ple_gate[0], w_ple_proj=m_w_ple_proj[0])
    big_v = dict(w_in=v_w_in[0], w_branch_a=v_w_branch_a[0], w_branch_b=v_w_branch_b[0], w_out=v_w_out[0],
                 w_ffn_gate=v_w_ffn_gate[0], w_ffn_up=v_w_ffn_up[0], w_ffn_down=v_w_ffn_down[0],
                 w_ple_gate=v_w_ple_gate[0], w_ple_proj=v_w_ple_proj[0])
    names = list(big)
    nf = big["w_ffn_gate"].shape[1]

    shards = {k: big[k].astype(BF16) for k in names}
    shards["conv_w"] = _pad_rows(conv_w[0], SUBLANES)
    flipped = ("w_ffn_gate", "w_ffn_up")
    for k in flipped:
        big[k], big_m[k], big_v[k] = big[k].T, big_m[k].T, big_v[k].T
    ag_groups = [["w_in"], ["w_branch_a", "w_branch_b", "w_out"], ["w_ffn_gate", "w_ffn_up", "conv_w"],
                 ["w_ffn_down", "w_ple_gate", "w_ple_proj"]]
    ag_started = {}
    wg = {}

    ag_paired, ag_passing = {}, {}

    def pair(gi, after):
        lands = [lax.dynamic_update_index_in_dim(lax.empty((N_DEV,) + shards[k].shape, shards[k].dtype), shards[k],
                                                 me, 0) for k in ag_groups[gi]]
        ag_paired[gi], token = ag_pair(f"ag_pair{gi}", lands, after)
        return token

    def start(gi, after):
        s_sems, r_sems, _, lands, token = ag_start(f"ag_start{gi}", ag_paired[gi], after)
        ag_started[gi] = (s_sems, r_sems, lands)
        return token

    def landed(gi, after):
        s_sems, r_sems, lands = ag_started[gi]
        return split_wait(f"ag_wait{gi}", s_sems, r_sems, [], lands, ag_plan(len(lands)), after)[1]

    def pass_on(gi, lands, after):
        s_sems, r_sems, _, lands, token = split_start(f"ag_pass{gi}", [], lands, pass_plan(len(lands)), after)
        ag_passing[gi] = (s_sems, r_sems, lands)
        return token

    def ready(gi, after):
        s_sems, r_sems, lands = ag_passing[gi]
        lands = split_wait(f"ag_ready{gi}", s_sems, r_sems, [], lands, pass_plan(len(lands)), after)[1]
        wg.update(zip(ag_groups[gi], lands))

    cb = conv_b.reshape(N_DEV, 1, nf)

    table_t = rel_bias_table.T
    geo_a = dict(half=A_BLOCK, q_rows=ATTN_Q_ROWS, n_chains=ATTN_CHAINS, dil=1, nh=A_Q_HEADS, group=A_GROUP,
                 cq=COL_QA, ck=COL_KA, cv=COL_VA)
    geo_b = [dict(half=B_BLOCK, q_rows=min(ATTN_Q_ROWS, s // dil), n_chains=ATTN_CHAINS, dil=dil,
                  nh=B_HEADS_PER_GROUP, group=1, cq=COL_QB + g * B_OUT_W, ck=COL_KB + g * B_OUT_W,
                  cv=COL_VB + g * B_OUT_W) for g, (_, dil) in enumerate(B_PATTERNS)]
    bucket_a = bucket_tile(geo_a["q_rows"], A_BLOCK, 1)
    bias_a = bias_build("bias_a", table_t, bucket_a, 0, A_Q_HEADS, A_BLOCK)
    buckets_b = [bucket_tile(gb["q_rows"], B_BLOCK, gb["dil"]) for gb in geo_b]
    biases_b = [bias_build(f"bias_b{g}", table_t, buckets_b[g], A_Q_HEADS + g * B_HEADS_PER_GROUP, B_HEADS_PER_GROUP,
                           B_BLOCK) for g in range(len(B_PATTERNS))]

    token = start(1, pair(1, start(0, pair(0, xs))))
    h = rms_fwd("rms_attn", xs, attn_norm + _token_value(token))
    lands0 = landed(0, pair(3, pair(2, h)))
    wg["w_in"] = ag_finish("ag_finish0", lands0)[0]
    token = start(2, wg["w_in"])
    proj = mm_cols("proj_in", h, wg["w_in"], F32, fold=True, after=token)
    lands1 = landed(1, proj)
    token = start(3, pass_on(1, lands1, proj))
    sink = sink_a[0] + _token_value(token)
    ya, lse_a = band_attn_fwd("attn_a_fwd", proj, bias_a, sink, **geo_a)
    outs_b, lses_b = [], []
    for g in range(len(B_PATTERNS)):
        o, l = band_attn_fwd(f"attn_b{g}_fwd", proj, biases_b[g], None, **geo_b[g])
        outs_b.append(o)
        lses_b.append(l)
    yb = dil_merge_fwd("dil_merge_fwd", outs_b, lses_b)
    ready(1, yb)
    token = pass_on(2, landed(2, yb), yb)
    w_out_full = wg["w_out"].reshape(d, d)
    ta = mm_cols("branch_a", ya, wg["w_branch_a"], F32, fold=True, after=token)
    tb = mm_cols("branch_b", yb, wg["w_branch_b"], F32, fold=True)
    merged = gate_merge_fwd("gate_merge_fwd", proj, ta, tb, d)
    x1 = mm_plain("mix_out", merged, w_out_full, F32, res=xs)

    hf = rms_fwd("rms_ffn", x1, ffn_norm)
    ready(2, hf)
    token = pass_on(3, landed(3, hf), hf)
    cw = wg["conv_w"]
    gpre = mm_cols("ffn_gate", hf, wg["w_ffn_gate"], F32, fold=False, after=token)
    u = mm_cols("ffn_up", hf, wg["w_ffn_up"], F32, fold=False)
    z = ffn_mid_fwd("ffn_mid_fwd", gpre, u, cw, cb)
    ready(3, z)
    w_pg_full = wg["w_ple_gate"].reshape(d, d)
    x2 = mm_jsum("ffn_down", z, wg["w_ffn_down"], F32, res=x1)

    hp = rms_fwd("rms_ple", x2, ple_norm)
    lp = mm_plain("ple_gate", hp, w_pg_full, F32)
    pp = mm_cols("ple_proj", ps, wg["w_ple_proj"], F32, fold=True)
    loss_part, dx3, dlp, dpp, d_final = tail_fwd_bwd("tail", x2, lp, pp, final_norm.reshape(1, d), target)

    grads = {}
    rs_started = []
    blocks = own_blocks()

    exchanging = []

    def exchange(tag, keys):
        parts = [grads[k] for k in keys]
        lands = [lax.empty((N_CHIPS,) + tuple(p.shape[1:]), p.dtype) for p in parts]
        s_sems, r_sems, parts, lands, token = split_start(f"rs_pair_{tag}", parts, lands, rs_pair_plan(len(keys)), blocks)
        exchanging.append((tag, keys, s_sems, r_sems, parts, lands))
        return _token_value(token)

    def send(after):
        tag, keys, s_sems, r_sems, parts, lands = exchanging.pop(0)
        parts, got = split_wait(f"rs_paired_{tag}", s_sems, r_sems, parts, lands, rs_pair_plan(len(keys)), after)
        return send_sums(tag, keys, parts, got)

    def send_sums(tag, keys, parts, got):
        sums = [pair_add(f"pair_add_{k}", blocks, p, g) for k, p, g in zip(keys, parts, got)]
        s_sems, r_sems, srcs, lands, token = rs_start(f"rs_start_{tag}", sums, blocks)
        rs_started.append((tag, keys, s_sems, r_sems, srcs, lands))
        return token

    grads["w_ple_proj"] = mm_tn_cols("d_w_ple_proj", ps, dpp, N_DEV, big["w_ple_proj"].shape[1], BF16, folded=True)
    grads["w_ple_gate"] = mm_tn_plain("d_w_ple_gate", hp, dlp, BF16).reshape(N_DEV, d // N_DEV, d)
    tok = exchange("ple", ["w_ple_proj", "w_ple_gate"])
    dhp = mm_nt_plain("d_hp", dlp, w_pg_full, F32)
    dx2, d_ple = rms_bwd("rms_ple_bwd", x2, ple_norm + tok, dhp, dx3)

    dz = mm_nt_j("d_z", dx2, wg["w_ffn_down"], BF16)
    grads["w_ffn_down"] = mm_tn_j("d_w_ffn_down", z, dx2, BF16)
    tok = _token_value(send(dz)) + exchange("down", ["w_ffn_down"])
    dg, du, dcw = ffn_mid_bwd1("ffn_mid_bwd1", gpre, u, dz, cw, cb + tok)
    dgpre = ffn_mid_bwd2("ffn_mid_bwd2", dg, cw)
    grads["w_ffn_up"] = mm_tn_j("d_w_ffn_up", du, hf, BF16)
    grads["w_ffn_gate"] = mm_tn_j("d_w_ffn_gate", dgpre, hf, BF16)
    dhf = mm_nt_jsum("d_hf_up", du, wg["w_ffn_up"], F32, folded=False)
    dhf = mm_nt_jsum("d_hf_gate", dgpre, wg["w_ffn_gate"], F32, folded=False, res=dhf)
    tok = _token_value(send(dhf)) + exchange("upgate", ["w_ffn_up", "w_ffn_gate"])
    dx1, d_ffn = rms_bwd("rms_ffn_bwd", x1, ffn_norm + tok, dhf, dx2)

    dmerged = mm_nt_plain("d_merged", dx1, w_out_full, F32)
    grads["w_out"] = mm_tn_plain("d_w_out", merged, dx1, BF16).reshape(N_DEV, d // N_DEV, d)
    dta, dtb, dga, dgb = gate_merge_bwd("gate_merge_bwd", dmerged, proj, ta, tb, d)
    grads["w_branch_a"] = mm_tn_cols("d_w_branch_a", ya, dta, N_DEV, big["w_branch_a"].shape[1], BF16, folded=True)
    grads["w_branch_b"] = mm_tn_cols("d_w_branch_b", yb, dtb, N_DEV, big["w_branch_b"].shape[1], BF16, folded=True)
    dya = mm_nt_jsum("d_ya", dta, wg["w_branch_a"], F32, folded=True)
    dyb = mm_nt_jsum("d_yb", dtb, wg["w_branch_b"], F32, folded=True)
    tok = _token_value(send(dyb)) + exchange("mix", ["w_out", "w_branch_a", "w_branch_b"])
    dqa, dka, dva, dbias_a, dsink = band_attn_bwd("attn_a_bwd", proj, bias_a, sink + tok, dya, ya, lse_a, None, **geo_a)
    douts_b, dlses_b = dil_merge_bwd("dil_merge_bwd", dyb, outs_b, lses_b)
    dq_b, dk_b, dv_b, dbias_b = [], [], [], []
    for g in range(len(B_PATTERNS)):
        dq, dk, dv, db, _ = band_attn_bwd(f"attn_b{g}_bwd", proj, biases_b[g], None, douts_b[g], outs_b[g], lses_b[g],
                                          dlses_b[g], **geo_b[g])
        dq_b.append(dq)
        dk_b.append(dk)
        dv_b.append(dv)
        dbias_b.append(db)
    dproj = jnp.concatenate([t.astype(BF16) for t in [dqa, dka, dva] + dq_b + dk_b + dv_b + [dga, dgb]], axis=1)
    token = send(dproj)
    grads["w_in"] = mm_tn_cols("d_w_in", h, dproj, N_DEV, big["w_in"].shape[1], BF16, folded=True, after=token)
    token = send_sums("in", ["w_in"], [grads["w_in"]], rs_pair("rs_pair_in", [grads["w_in"]]))
    dh = mm_nt_jsum("d_h", dproj, wg["w_in"], F32, folded=True, after=token)
    grad_x, d_attn = rms_bwd("rms_attn_bwd", xs, attn_norm, dh, dx1)

    dt_a = table_grad("table_grad_a", dbias_a, bucket_a)[:, 0, :N_BUCKETS]
    dt_b = [table_grad(f"table_grad_b{g}", dbias_b[g], buckets_b[g])[:, 0, :N_BUCKETS] for g in range(len(B_PATTERNS))]
    d_table_part = jnp.concatenate([dt_a] + dt_b, axis=0).T

    pieces = [
        ("loss", loss_part[0, :1]),
        ("table", d_table_part.reshape(-1)),
        ("attn_norm", d_attn.reshape(-1)),
        ("sink", dsink[:, 0, 0]),
        ("ffn_norm", d_ffn.reshape(-1)),
        ("conv_w", dcw[:, 0:3, :].reshape(-1)),
        ("conv_b", dcw[:, 3, :].reshape(-1)),
        ("ple_norm", d_ple.reshape(-1)),
        ("final_norm", d_final.reshape(-1)),
    ]
    tiles = [_as_tiles(v) for _, v in pieces]
    pack = jnp.concatenate(tiles, axis=0)

    out_g, out_d, out_m, out_v = {}, {}, {}, {}

    def finish(group, after):
        tag, keys, s_sems, r_sems, srcs, lands = group
        srcs, lands = split_wait(f"rs_wait_{tag}", s_sems, r_sems, srcs, lands, rs_plan(len(keys)), after)
        for k, mine, theirs in zip(keys, srcs, lands):
            res = reduce_adam("adam_" + k, mine, theirs, big[k], big_m[k], big_v[k])
            after = res[1]
            out_g[k], out_d[k], out_m[k], out_v[k] = [(t.T if k in flipped else t)[None] for t in res]
        return after

    after = pack
    for group in rs_started[:-1]:
        after = finish(group, after)
    total = allreduce_small("allreduce_small", pack, after)
    finish(rs_started[-1], total)
    small = {}
    row = 0
    for (nm, v), t in zip(pieces, tiles):
        small[nm] = total[row:row + t.shape[0]].reshape(-1)[:v.shape[0]]
        row += t.shape[0]
    loss = small["loss"][0]
    g_small = dict(
        rel_bias_table=small["table"].reshape(rel_bias_table.shape),
        attn_norm=small["attn_norm"].reshape(attn_norm.shape),
        sink_a=small["sink"].reshape(sink_a.shape),
        ffn_norm=small["ffn_norm"].reshape(ffn_norm.shape),
        conv_w=lax.dynamic_index_in_dim(small["conv_w"].reshape(N_DEV, 3, nf), me, 0, keepdims=False)[None],
        conv_b=small["conv_b"].reshape(conv_b.shape),
        ple_norm=small["ple_norm"].reshape(ple_norm.shape),
        final_norm=small["final_norm"].reshape(1, d),
    )
    w_small = dict(rel_bias_table=(rel_bias_table, m_rel_bias_table, v_rel_bias_table),
                   attn_norm=(attn_norm, m_attn_norm, v_attn_norm), sink_a=(sink_a, m_sink_a, v_sink_a),
                   ffn_norm=(ffn_norm, m_ffn_norm, v_ffn_norm), conv_w=(conv_w, m_conv_w, v_conv_w),
                   conv_b=(conv_b, m_conv_b, v_conv_b), ple_norm=(ple_norm, m_ple_norm, v_ple_norm),
                   final_norm=(final_norm, m_final_norm, v_final_norm))

    for k, (wv, mv, vv) in w_small.items():
        shape = wv.shape
        two_d = (1, shape[0]) if len(shape) == 1 else ((shape[0] * shape[1], shape[2]) if len(shape) == 3 else shape)
        gk = g_small[k].reshape(two_d)
        dl, nm, nv = adam_small("adam_" + k, gk, wv.reshape(two_d), mv.reshape(two_d), vv.reshape(two_d))
        out_g[k], out_d[k], out_m[k], out_v[k] = gk.reshape(shape), dl.reshape(shape), nm.reshape(shape), nv.reshape(shape)

    order = ["rel_bias_table", "attn_norm", "w_in", "sink_a", "w_branch_a", "w_branch_b", "w_out", "ffn_norm",
             "w_ffn_gate", "w_ffn_up", "conv_w", "conv_b", "w_ffn_down", "ple_norm", "w_ple_gate", "w_ple_proj",
             "final_norm"]
    return (loss, grad_x[None], *[out_g[k] for k in order], *[out_d[k] for k in order],
            *[out_m[k] for k in order], *[out_v[k] for k in order])
```

```python
import math

import jax
import jax.numpy as jnp
from jax import lax
from jax.experimental import pallas as pl
from jax.experimental.pallas import tpu as pltpu

F32 = jnp.float32
BF16 = jnp.bfloat16
MESH = pl.DeviceIdType.MESH
N_DEV = 8

HEAD_DIM = 128
A_Q_HEADS = 8
A_KV_HEADS = 2
A_GROUP = A_Q_HEADS // A_KV_HEADS
A_BLOCK = 128
B_PATTERNS = ((128, 1), (512, 4), (2048, 16))
B_HEADS_PER_GROUP = 4
B_HEADS = len(B_PATTERNS) * B_HEADS_PER_GROUP
B_BLOCK = 64
N_BUCKETS = 32
MAX_DISTANCE = 1024
A_Q_W = A_Q_HEADS * HEAD_DIM
A_KV_W = A_KV_HEADS * HEAD_DIM
B_W = B_HEADS * HEAD_DIM
B_OUT_W = B_HEADS_PER_GROUP * HEAD_DIM
COL_QA = 0
COL_KA = COL_QA + A_Q_W
COL_VA = COL_KA + A_KV_W
COL_QB = COL_VA + A_KV_W
COL_KB = COL_QB + B_W
COL_VB = COL_KB + B_W
COL_GATES = COL_VB + B_W
RMS_EPS = 1e-6
NEG_INF = -1e30
ATTN_SCALE = HEAD_DIM ** -0.5
ATTN_Q_ROWS = 256
ATTN_CHAINS = 2

ADAM_LR = 0.001
ADAM_B1 = 0.9
ADAM_B2 = 0.999
ADAM_EPS = 1e-08
ADAM_WD = 0.01
ADAM_STEP = 10

GELU_C = math.sqrt(2.0 / math.pi)
GELU_A = 0.044715

V7X_VMEM_BYTES = 64 * 1024 * 1024
VMEM_CEILING = V7X_VMEM_BYTES - 8 * 1024 * 1024
LANES = 128
SUBLANES = 8


def _pick(n, cands):
    for c in cands:
        if n % c == 0:
            return c
    return n


def _nbytes(shape, dtype):
    n = 1
    for d in shape:
        if d is not None:
            n *= d
    return n * jnp.dtype(dtype).itemsize


def _params(sem, est_bytes):
    limit = int(min(VMEM_CEILING, max(32 * 1024 * 1024, 2 * est_bytes + (8 << 20))))
    return pltpu.CompilerParams(dimension_semantics=sem, vmem_limit_bytes=limit)


def _mm(name, a, b, a_bs, a_im, b_bs, b_im, out_shape, out_dtype, o_bs, o_im, grid, dims,
        res=None, r_bs=None, r_im=None, after=None):
    nk = grid[-1]
    nax = len(grid)
    has_res = res is not None
    has_after = after is not None
    o_tile = tuple(d for d in o_bs if d is not None)

    def body(*refs):
        a_ref, b_ref = refs[:2]
        r_ref = refs[2] if has_res else None
        n_in = 2 + has_res + has_after
        o_ref = refs[n_in]
        rest = refs[n_in + 1:]

        def prod():
            return lax.dot_general(a_ref[...].astype(BF16), b_ref[...].astype(BF16), (dims, ((), ())),
                                   preferred_element_type=F32)

        def finish(r):
            if r_ref is not None:
                r = r + r_ref[...].astype(F32)
            o_ref[...] = r.astype(o_ref.dtype)

        if nk == 1:
            finish(prod())
        else:
            acc = rest[0]
            k = pl.program_id(nax - 1)

            @pl.when(k == 0)
            def _():
                acc[...] = prod()

            @pl.when(k > 0)
            def _():
                acc[...] += prod()

            @pl.when(k == nk - 1)
            def _():
                finish(acc[...])

    in_specs = [pl.BlockSpec(a_bs, a_im), pl.BlockSpec(b_bs, b_im)]
    args = [a, b]
    est = _nbytes(a_bs, a.dtype) + _nbytes(b_bs, b.dtype) + _nbytes(o_bs, out_dtype) + 2 * _nbytes(o_tile, F32)
    if has_res:
        in_specs.append(pl.BlockSpec(r_bs, r_im))
        args.append(res)
        est += _nbytes(r_bs, res.dtype)
    if has_after:
        in_specs.append(pl.BlockSpec(memory_space=pl.ANY))
        args.append(after)
    scratch = [] if nk == 1 else [pltpu.VMEM(o_tile, F32)]
    sem = ("parallel",) * (nax - 1) + ("arbitrary",)
    return pl.pallas_call(
        body, name=name, grid=grid, in_specs=in_specs, out_specs=pl.BlockSpec(o_bs, o_im),
        out_shape=pltpu.HBM(out_shape, out_dtype), scratch_shapes=scratch,
        compiler_params=_params(sem, est))(*args)


TM_CANDS = (1024, 512, 256, 128, 64, 32, 16, 8)
TM_WIDE_CANDS = (2048,) + TM_CANDS
TK_CANDS = (1024, 512, 256, 128)
TN_CANDS = (1024, 512, 256, 128)


def mm_cols(name, a, wg, out_dtype, fold, after=None):
    m, k = a.shape
    nj, _, n = wg.shape
    tm, tk = _pick(m, TM_WIDE_CANDS), _pick(k, TK_CANDS)
    grid = (nj, m // tm, k // tk)
    if fold:
        shape, o_bs, o_im = (m, nj * n), (tm, n), (lambda j, i, kk: (i, j))
    else:
        shape, o_bs, o_im = (nj, m, n), (None, tm, n), (lambda j, i, kk: (j, i, 0))
    return _mm(name, a, wg, (tm, tk), lambda j, i, kk: (i, kk), (None, tk, n), lambda j, i, kk: (j, kk, 0),
               shape, out_dtype, o_bs, o_im, grid, ((1,), (0,)), after=after)


def mm_plain(name, a, w, out_dtype, res=None):
    m, k = a.shape
    n = w.shape[1]
    tm, tk, tn = _pick(m, TM_CANDS), _pick(k, TK_CANDS), _pick(n, TN_CANDS)
    grid = (n // tn, m // tm, k // tk)
    return _mm(name, a, w, (tm, tk), lambda j, i, kk: (i, kk), (tk, tn), lambda j, i, kk: (kk, j),
               (m, n), out_dtype, (tm, tn), lambda j, i, kk: (i, j), grid, ((1,), (0,)),
               res, (tm, tn), lambda j, i, kk: (i, j))


def mm_jsum(name, aj, wg, out_dtype, res=None):
    nj, m, ka = aj.shape
    n = wg.shape[2]
    tm, tn = _pick(m, TM_CANDS), _pick(n, TN_CANDS)
    grid = (m // tm, n // tn, nj)
    return _mm(name, aj, wg, (None, tm, ka), lambda i, jn, j: (j, i, 0), (None, ka, tn), lambda i, jn, j: (j, 0, jn),
               (m, n), out_dtype, (tm, tn), lambda i, jn, j: (i, jn), grid, ((1,), (0,)),
               res, (tm, tn), lambda i, jn, j: (i, jn))


def mm_tn_cols(name, a, g, nj, n, out_dtype, folded, after=None):
    s, kw = a.shape
    ts, tkw = _pick(s, TK_CANDS), _pick(kw, TM_CANDS)
    grid = (nj, kw // tkw, s // ts)
    if folded:
        g_bs, g_im = (ts, n), (lambda j, i, ss: (ss, j))
    else:
        g_bs, g_im = (None, ts, n), (lambda j, i, ss: (j, ss, 0))
    return _mm(name, a, g, (ts, tkw), lambda j, i, ss: (ss, i), g_bs, g_im,
               (nj, kw, n), out_dtype, (None, tkw, n), lambda j, i, ss: (j, i, 0), grid, ((0,), (0,)), after=after)


def mm_tn_plain(name, a, g, out_dtype):
    s, kw = a.shape
    n = g.shape[1]
    ts, tkw, tn = _pick(s, TK_CANDS), _pick(kw, TM_CANDS), _pick(n, TN_CANDS)
    grid = (kw // tkw, n // tn, s // ts)
    return _mm(name, a, g, (ts, tkw), lambda i, jn, ss: (ss, i), (ts, tn), lambda i, jn, ss: (ss, jn),
               (kw, n), out_dtype, (tkw, tn), lambda i, jn, ss: (i, jn), grid, ((0,), (0,)))


def mm_tn_j(name, aj, g, out_dtype):
    nj, s, ka = aj.shape
    n = g.shape[1]
    ts, tn = _pick(s, TK_CANDS), _pick(n, TN_CANDS)
    grid = (nj, n // tn, s // ts)
    return _mm(name, aj, g, (None, ts, ka), lambda j, jn, ss: (j, ss, 0), (ts, tn), lambda j, jn, ss: (ss, jn),
               (nj, ka, n), out_dtype, (None, ka, tn), lambda j, jn, ss: (j, 0, jn), grid, ((0,), (0,)))


def mm_nt_plain(name, g, w, out_dtype):
    m, n = g.shape
    k = w.shape[0]
    tm, tn, tkk = _pick(m, TM_CANDS), _pick(n, TK_CANDS), _pick(k, TN_CANDS)
    grid = (k // tkk, m // tm, n // tn)
    return _mm(name, g, w, (tm, tn), lambda kk, i, jn: (i, jn), (tkk, tn), lambda kk, i, jn: (kk, jn),
               (m, k), out_dtype, (tm, tkk), lambda kk, i, jn: (i, kk), grid, ((1,), (1,)))


def mm_nt_j(name, g, wg, out_dtype):
    m, n = g.shape
    nj, ka, _ = wg.shape
    tm, tn = _pick(m, TM_CANDS), _pick(n, TK_CANDS)
    grid = (nj, m // tm, n // tn)
    return _mm(name, g, wg, (tm, tn), lambda j, i, jn: (i, jn), (None, ka, tn), lambda j, i, jn: (j, 0, jn),
               (nj, m, ka), out_dtype, (None, tm, ka), lambda j, i, jn: (j, i, 0), grid, ((1,), (1,)))


def mm_nt_jsum(name, g, wg, out_dtype, folded, res=None, after=None):
    nj, k, n = wg.shape
    m = g.shape[0] if folded else g.shape[1]
    tm, tkk = _pick(m, TM_CANDS if res is not None else TM_WIDE_CANDS), _pick(k, TN_CANDS)
    grid = (m // tm, k // tkk, nj)
    if folded:
        g_bs, g_im = (tm, n), (lambda i, kk, j: (i, j))
    else:
        g_bs, g_im = (None, tm, n), (lambda i, kk, j: (j, i, 0))
    return _mm(name, g, wg, g_bs, g_im, (None, tkk, n), lambda i, kk, j: (j, kk, 0),
               (m, k), out_dtype, (tm, tkk), lambda i, kk, j: (i, kk), grid, ((1,), (1,)),
               res, (tm, tkk), lambda i, kk, j: (i, kk), after=after)


ROW_TILE_CANDS = (256, 128, 64, 32, 16, 8)


def _rstd(x):
    return lax.rsqrt(jnp.mean(x * x, axis=-1, keepdims=True) + RMS_EPS)


def _sigmoid(t):
    return 1.0 / (1.0 + jnp.exp(-t))


def rms_fwd(name, x, gain):
    s, d = x.shape
    ts = _pick(s, ROW_TILE_CANDS)

    def body(x_ref, g_ref, h_ref):
        xv = x_ref[...]
        h_ref[...] = ((xv * _rstd(xv)) * g_ref[...]).astype(h_ref.dtype)

    return pl.pallas_call(
        body, name=name, grid=(s // ts,),
        in_specs=[pl.BlockSpec((ts, d), lambda i: (i, 0)), pl.BlockSpec((1, d), lambda i: (0, 0))],
        out_specs=pl.BlockSpec((ts, d), lambda i: (i, 0)),
        out_shape=jax.ShapeDtypeStruct((s, d), BF16),
        compiler_params=_params(("parallel",), 3 * ts * d * 4))(x, gain)


def rms_bwd(name, x, gain, dh, dres):
    s, d = x.shape
    ts = _pick(s, ROW_TILE_CANDS)

    def body(x_ref, g_ref, dh_ref, dr_ref, dx_ref, dg_ref):
        xv = x_ref[...]
        r = _rstd(xv)
        xhat = xv * r
        dhv = dh_ref[...].astype(F32)
        dxhat = dhv * g_ref[...]
        dx_ref[...] = dr_ref[...] + r * (dxhat - xhat * jnp.mean(dxhat * xhat, axis=-1, keepdims=True))
        part = jnp.sum(dhv * xhat, axis=0, keepdims=True)

        @pl.when(pl.program_id(0) == 0)
        def _():
            dg_ref[...] = part

        @pl.when(pl.program_id(0) > 0)
        def _():
            dg_ref[...] += part

    row = pl.BlockSpec((ts, d), lambda i: (i, 0))
    vec = pl.BlockSpec((1, d), lambda i: (0, 0))
    return pl.pallas_call(
        body, name=name, grid=(s // ts,), in_specs=[row, vec, row, row], out_specs=[row, vec],
        out_shape=[jax.ShapeDtypeStruct((s, d), F32), jax.ShapeDtypeStruct((1, d), F32)],
        compiler_params=_params(("arbitrary",), 6 * ts * d * 4))(x, gain, dh, dres)


def gate_merge_fwd(name, proj, ta, tb, d):
    s = proj.shape[0]
    ts = _pick(s, ROW_TILE_CANDS)
    cb = COL_GATES // d

    def body(ga_ref, gb_ref, ta_ref, tb_ref, o_ref):
        o_ref[...] = (_sigmoid(ga_ref[...]) * ta_ref[...] + _sigmoid(gb_ref[...]) * tb_ref[...]).astype(o_ref.dtype)

    row = pl.BlockSpec((ts, d), lambda i: (i, 0))
    return pl.pallas_call(
        body, name=name, grid=(s // ts,),
        in_specs=[pl.BlockSpec((ts, d), lambda i: (i, cb)), pl.BlockSpec((ts, d), lambda i: (i, cb + 1)), row, row],
        out_specs=row, out_shape=jax.ShapeDtypeStruct((s, d), BF16),
        compiler_params=_params(("parallel",), 5 * ts * d * 4))(proj, proj, ta, tb)


def gate_merge_bwd(name, dmerged, proj, ta, tb, d):
    s = proj.shape[0]
    ts = _pick(s, ROW_TILE_CANDS)
    cb = COL_GATES // d

    def body(dm_ref, ga_ref, gb_ref, ta_ref, tb_ref, dta_ref, dtb_ref, dga_ref, dgb_ref):
        dm = dm_ref[...]
        sa = _sigmoid(ga_ref[...])
        sb = _sigmoid(gb_ref[...])
        dta_ref[...] = (dm * sa).astype(dta_ref.dtype)
        dtb_ref[...] = (dm * sb).astype(dtb_ref.dtype)
        dga_ref[...] = (dm * ta_ref[...] * (sa * (1.0 - sa))).astype(dga_ref.dtype)
        dgb_ref[...] = (dm * tb_ref[...] * (sb * (1.0 - sb))).astype(dgb_ref.dtype)

    row = pl.BlockSpec((ts, d), lambda i: (i, 0))
    out = jax.ShapeDtypeStruct((s, d), BF16)
    return pl.pallas_call(
        body, name=name, grid=(s // ts,),
        in_specs=[row, pl.BlockSpec((ts, d), lambda i: (i, cb)), pl.BlockSpec((ts, d), lambda i: (i, cb + 1)), row, row],
        out_specs=[row, row, row, row], out_shape=[out, out, out, out],
        compiler_params=_params(("parallel",), 8 * ts * d * 4))(dmerged, proj, proj, ta, tb)


def tail_fwd_bwd(name, x2, lp, pp, gain, target):
    s, d = x2.shape
    ts = _pick(s, ROW_TILE_CANDS)

    def body(x2_ref, lp_ref, pp_ref, g_ref, t_ref, loss_ref, dx3_ref, dlp_ref, dpp_ref, dg_ref):
        gp = _sigmoid(lp_ref[...])
        ppv = pp_ref[...]
        x3 = x2_ref[...] + gp * ppv
        r = _rstd(x3)
        xhat = x3 * r
        gv = g_ref[...]
        err = xhat * gv - t_ref[...]
        loss = jnp.sum(err * err) * (0.5 / d)
        dy = err * (1.0 / d)
        dxhat = dy * gv
        dx3 = r * (dxhat - xhat * jnp.mean(dxhat * xhat, axis=-1, keepdims=True))
        dx3_ref[...] = dx3
        dlp_ref[...] = (dx3 * ppv * (gp * (1.0 - gp))).astype(dlp_ref.dtype)
        dpp_ref[...] = (dx3 * gp).astype(dpp_ref.dtype)
        part = jnp.sum(dy * xhat, axis=0, keepdims=True)
        lossv = jnp.full((1, LANES), loss, F32)

        @pl.when(pl.program_id(0) == 0)
        def _():
            dg_ref[...] = part
            loss_ref[...] = lossv

        @pl.when(pl.program_id(0) > 0)
        def _():
            dg_ref[...] += part
            loss_ref[...] += lossv

    row = pl.BlockSpec((ts, d), lambda i: (i, 0))
    vec = pl.BlockSpec((1, d), lambda i: (0, 0))
    return pl.pallas_call(
        body, name=name, grid=(s // ts,), in_specs=[row, row, row, vec, row],
        out_specs=[pl.BlockSpec((1, LANES), lambda i: (0, 0)), row, row, row, vec],
        out_shape=[jax.ShapeDtypeStruct((1, LANES), F32), jax.ShapeDtypeStruct((s, d), F32),
                   jax.ShapeDtypeStruct((s, d), BF16), jax.ShapeDtypeStruct((s, d), BF16),
                   jax.ShapeDtypeStruct((1, d), F32)],
        compiler_params=_params(("arbitrary",), 9 * ts * d * 4))(x2, lp, pp, gain, target)


HALO = SUBLANES


def _shift_rows(cur, prev_row, next_row):
    ts = cur.shape[0]
    rid = lax.broadcasted_iota(jnp.int32, cur.shape, 0)
    down = jnp.where(rid == 0, prev_row, pltpu.roll(cur, 1, 0))
    up = jnp.where(rid == ts - 1, next_row, pltpu.roll(cur, ts - 1, 0))
    return down, up


def _halo_specs(ts, s, nf):
    nb = ts // HALO
    last = s // HALO - 1
    cur = pl.BlockSpec((None, ts, nf), lambda j, i: (j, i, 0))
    prev = pl.BlockSpec((None, HALO, nf), lambda j, i: (j, jnp.maximum(i * nb - 1, 0), 0))
    nxt = pl.BlockSpec((None, HALO, nf), lambda j, i: (j, jnp.minimum((i + 1) * nb, last), 0))
    return cur, prev, nxt


def _halo_rows(prev_ref, next_ref, n_tiles):
    i = pl.program_id(1)
    prev_row = jnp.where(i == 0, 0.0, prev_ref[HALO - 1:HALO, :].astype(F32))
    next_row = jnp.where(i == n_tiles - 1, 0.0, next_ref[0:1, :].astype(F32))
    return prev_row, next_row


def _gelu(g):
    t = jnp.tanh(GELU_C * (g + GELU_A * (g * g * g)))
    return 0.5 * g * (1.0 + t), t


def _conv(cur, down, up, cw_ref, cb_ref):
    return down * cw_ref[0:1, :] + cur * cw_ref[1:2, :] + up * cw_ref[2:3, :] + cb_ref[...]


def ffn_mid_fwd(name, gpre, u, cw, cb):
    nj, s, nf = gpre.shape
    ts = _pick(s, (512, 256, 128, 64, 32, 16, 8))
    n_tiles = s // ts
    cur, prev, nxt = _halo_specs(ts, s, nf)

    def body(g_ref, gp_ref, gn_ref, u_ref, cw_ref, cb_ref, z_ref):
        gv = g_ref[...]
        down, up = _shift_rows(gv, *_halo_rows(gp_ref, gn_ref, n_tiles))
        act, _ = _gelu(_conv(gv, down, up, cw_ref, cb_ref))
        z_ref[...] = (act * u_ref[...]).astype(z_ref.dtype)

    return pl.pallas_call(
        body, name=name, grid=(nj, n_tiles),
        in_specs=[cur, prev, nxt, cur, pl.BlockSpec((None, SUBLANES, nf), lambda j, i: (j, 0, 0)),
                  pl.BlockSpec((None, 1, nf), lambda j, i: (j, 0, 0))],
        out_specs=cur, out_shape=jax.ShapeDtypeStruct((nj, s, nf), BF16),
        compiler_params=_params(("parallel", "parallel"), 8 * ts * nf * 4))(gpre, gpre, gpre, u, cw, cb)


def ffn_mid_bwd1(name, gpre, u, dz, cw, cb):
    nj, s, nf = gpre.shape
    ts = _pick(s, (512, 256, 128, 64, 32, 16, 8))
    n_tiles = s // ts
    cur, prev, nxt = _halo_specs(ts, s, nf)

    def body(g_ref, gp_ref, gn_ref, u_ref, dz_ref, cw_ref, cb_ref, dg_ref, du_ref, dcw_ref):
        gv = g_ref[...]
        down, up = _shift_rows(gv, *_halo_rows(gp_ref, gn_ref, n_tiles))
        gc = _conv(gv, down, up, cw_ref, cb_ref)
        act, t = _gelu(gc)
        dzv = dz_ref[...].astype(F32)
        du_ref[...] = (dzv * act).astype(du_ref.dtype)
        dact = 0.5 * (1.0 + t) + 0.5 * gc * (1.0 - t * t) * (GELU_C * (1.0 + 3.0 * GELU_A * (gc * gc)))
        dg = dzv * u_ref[...] * dact
        dg_ref[...] = dg
        rows = [jnp.sum(dg * down, axis=0, keepdims=True), jnp.sum(dg * gv, axis=0, keepdims=True),
                jnp.sum(dg * up, axis=0, keepdims=True), jnp.sum(dg, axis=0, keepdims=True)]
        part = jnp.concatenate(rows + [jnp.zeros((SUBLANES - len(rows), nf), F32)], axis=0)

        @pl.when(pl.program_id(1) == 0)
        def _():
            dcw_ref[...] = part

        @pl.when(pl.program_id(1) > 0)
        def _():
            dcw_ref[...] += part

    small = pl.BlockSpec((None, SUBLANES, nf), lambda j, i: (j, 0, 0))
    return pl.pallas_call(
        body, name=name, grid=(nj, n_tiles),
        in_specs=[cur, prev, nxt, cur, cur, small, pl.BlockSpec((None, 1, nf), lambda j, i: (j, 0, 0))],
        out_specs=[cur, cur, small],
        out_shape=[jax.ShapeDtypeStruct((nj, s, nf), F32), jax.ShapeDtypeStruct((nj, s, nf), BF16),
                   jax.ShapeDtypeStruct((nj, SUBLANES, nf), F32)],
        compiler_params=_params(("parallel", "arbitrary"), 12 * ts * nf * 4))(gpre, gpre, gpre, u, dz, cw, cb)


def ffn_mid_bwd2(name, dg, cw):
    nj, s, nf = dg.shape
    ts = _pick(s, (512, 256, 128, 64, 32, 16, 8))
    n_tiles = s // ts
    cur, prev, nxt = _halo_specs(ts, s, nf)

    def body(g_ref, gp_ref, gn_ref, cw_ref, o_ref):
        gv = g_ref[...]
        down, up = _shift_rows(gv, *_halo_rows(gp_ref, gn_ref, n_tiles))
        o_ref[...] = (up * cw_ref[0:1, :] + gv * cw_ref[1:2, :] + down * cw_ref[2:3, :]).astype(o_ref.dtype)

    return pl.pallas_call(
        body, name=name, grid=(nj, n_tiles),
        in_specs=[cur, prev, nxt, pl.BlockSpec((None, SUBLANES, nf), lambda j, i: (j, 0, 0))],
        out_specs=cur, out_shape=jax.ShapeDtypeStruct((nj, s, nf), BF16),
        compiler_params=_params(("parallel", "parallel"), 6 * ts * nf * 4))(dg, dg, dg, cw)


def _t5_bucket(rel):
    half = N_BUCKETS // 2
    max_exact = half // 2
    n = jnp.abs(rel)
    side = jnp.where(rel > 0, half, 0)
    nf = jnp.maximum(n, 1).astype(F32)
    large = max_exact + (jnp.log(nf / max_exact) / math.log(MAX_DISTANCE / max_exact)
                         * (half - max_exact)).astype(jnp.int32)
    large = jnp.minimum(large, half - 1)
    return side + jnp.where(n < max_exact, n, large)


def bucket_tile(rows, half, dil):
    rel = (jnp.arange(rows + 2 * half)[None, :] - half) - jnp.arange(rows)[:, None]
    return _t5_bucket(rel * dil).astype(jnp.int32)


def bias_build(name, table_t, bucket, h0, nh, half):
    blk, kw = bucket.shape

    def body(t_ref, b_ref, o_ref):
        h = pl.program_id(0)
        bv = b_ref[...]
        acc = jnp.zeros((blk, kw), F32)
        for b in range(N_BUCKETS):
            acc = jnp.where(bv == b, t_ref[h0 + h, b], acc)
        qi = lax.broadcasted_iota(jnp.int32, (blk, kw), 0)
        ci = lax.broadcasted_iota(jnp.int32, (blk, kw), 1)
        o_ref[...] = jnp.where(jnp.abs(ci - half - qi) <= half, acc, NEG_INF)

    return pl.pallas_call(
        body, name=name, grid=(nh,),
        in_specs=[pl.BlockSpec(memory_space=pltpu.SMEM), pl.BlockSpec((blk, kw), lambda h: (0, 0))],
        out_specs=pl.BlockSpec((None, blk, kw), lambda h: (h, 0, 0)),
        out_shape=jax.ShapeDtypeStruct((nh, blk, kw), F32),
        compiler_params=_params(("parallel",), 4 * blk * kw * 4))(table_t, bucket)


def table_grad(name, dbias, bucket):
    nh, blk, kw = dbias.shape

    def body(d_ref, b_ref, o_ref):
        bv = b_ref[...]
        dv = d_ref[...]
        lane = lax.broadcasted_iota(jnp.int32, (SUBLANES, LANES), 1)
        acc = jnp.zeros((SUBLANES, LANES), F32)
        for b in range(N_BUCKETS):
            acc = jnp.where(lane == b, jnp.sum(jnp.where(bv == b, dv, 0.0)), acc)
        o_ref[...] = acc

    return pl.pallas_call(
        body, name=name, grid=(nh,),
        in_specs=[pl.BlockSpec((None, blk, kw), lambda h: (h, 0, 0)), pl.BlockSpec((blk, kw), lambda h: (0, 0))],
        out_specs=pl.BlockSpec((None, SUBLANES, LANES), lambda h: (h, 0, 0)),
        out_shape=jax.ShapeDtypeStruct((nh, SUBLANES, LANES), F32),
        compiler_params=_params(("parallel",), 4 * blk * kw * 4))(dbias, bucket)


class _Band:
    def __init__(self, s, half, q_rows, n_chains, dil):
        self.s, self.half, self.dil, self.n_chains = s, half, dil, n_chains
        self.seg = s // dil
        self.q_rows = min(q_rows, self.seg)
        self.win = self.q_rows + 2 * half
        self.pad = self.seg + 2 * half
        self.nsb = self.seg // self.q_rows
        self.n_items = dil * self.nsb
        assert self.n_items % n_chains == 0 and self.seg % self.q_rows == 0
        self.staged = dil > 1

    def rows_of(self, r):
        return pl.ds(r, self.seg, stride=self.dil) if self.dil > 1 else slice(None)

    def stage_kv(self, dst, src_ref):
        zeros = jnp.zeros((self.half, HEAD_DIM), dst.dtype)
        for r in range(self.dil):
            base = r * self.pad
            dst[base:base + self.half, :] = zeros
            dst[base + self.half + self.seg:base + self.pad, :] = zeros
            dst[base + self.half:base + self.half + self.seg, :] = src_ref[self.rows_of(r), :].astype(dst.dtype)

    def stage(self, dst, src_ref):
        for r in range(self.dil):
            dst[r * self.seg:(r + 1) * self.seg, :] = src_ref[self.rows_of(r), :].astype(dst.dtype)

    def unstage(self, dst_ref, src, add=False):
        for r in range(self.dil):
            val = src[r * self.seg:(r + 1) * self.seg, :].astype(dst_ref.dtype)
            if add:
                val = val + dst_ref[self.rows_of(r), :]
            dst_ref[self.rows_of(r), :] = val

    def offsets(self, item):
        r, sb = item // self.nsb, item % self.nsb
        qoff = pl.multiple_of(r * self.seg + sb * self.q_rows, self.q_rows)
        koff = pl.multiple_of(r * self.pad + sb * self.q_rows, B_BLOCK)
        kpos = sb * self.q_rows - self.half + lax.broadcasted_iota(jnp.int32, (1, self.win), 1)
        edge = jnp.where((kpos >= 0) & (kpos < self.seg), 0.0, NEG_INF)
        return qoff, koff, edge


def band_attn_fwd(name, proj, bias, sink, *, half, q_rows, n_chains, dil, nh, group, cq, ck, cv):
    s, w = proj.shape
    g = _Band(s, half, q_rows, n_chains, dil)
    has_sink = sink is not None

    def body(*refs):
        q_ref, k_ref, v_ref, b_ref = refs[:4]
        s_ref = refs[4] if has_sink else None
        o_ref, l_ref, ks, vs = refs[4 + has_sink:8 + has_sink]
        qs, os_, ls = refs[8 + has_sink:] if g.staged else (None, o_ref, l_ref)
        g.stage_kv(ks, k_ref)
        g.stage_kv(vs, v_ref)
        if g.staged:
            g.stage(qs, q_ref)
        bias_v = b_ref[...]
        sk = s_ref[pl.program_id(0)] if has_sink else None

        def chain(item):
            qoff, koff, edge = g.offsets(item)
            rows = pl.ds(qoff, g.q_rows)
            qv = qs[rows, :] if g.staged else q_ref[rows, :].astype(BF16)
            kw_ = ks[pl.ds(koff, g.win), :]
            vw_ = vs[pl.ds(koff, g.win), :]
            sc = lax.dot_general(qv, kw_, (((1,), (1,)), ((), ())), preferred_element_type=F32) * ATTN_SCALE
            sc = sc + bias_v + edge
            m = jnp.max(sc, axis=-1, keepdims=True)
            if has_sink:
                m = jnp.maximum(m, sk)
            p = jnp.exp(sc - m)
            den = jnp.sum(p, axis=-1, keepdims=True)
            if has_sink:
                den = den + jnp.exp(sk - m)
            out = lax.dot_general(p.astype(BF16), vw_, (((1,), (0,)), ((), ())), preferred_element_type=F32)
            return rows, out / den, jnp.broadcast_to(m + jnp.log(den), (g.q_rows, HEAD_DIM))

        def step(i, carry):
            for rows, out, lse in [chain(i * n_chains + u) for u in range(n_chains)]:
                os_[rows, :] = out
                ls[rows, :] = lse
            return carry

        lax.fori_loop(0, g.n_items // n_chains, step, 0)
        if g.staged:
            g.unstage(o_ref, os_)
            g.unstage(l_ref, ls)

    def col(c0, per):
        return pl.BlockSpec((s, HEAD_DIM), lambda h: (0, c0 // LANES + h // per))

    in_specs = [col(cq, 1), col(ck, group), col(cv, group),
                pl.BlockSpec((None, g.q_rows, g.win), lambda h: (h, 0, 0))]
    args = [proj, proj, proj, bias]
    if has_sink:
        in_specs.append(pl.BlockSpec(memory_space=pltpu.SMEM))
        args.append(sink)
    shape = jax.ShapeDtypeStruct((s, nh * HEAD_DIM), F32)
    scratch = [pltpu.VMEM((dil * g.pad, HEAD_DIM), BF16), pltpu.VMEM((dil * g.pad, HEAD_DIM), BF16)]
    if g.staged:
        scratch += [pltpu.VMEM((s, HEAD_DIM), BF16), pltpu.VMEM((s, HEAD_DIM), F32), pltpu.VMEM((s, HEAD_DIM), F32)]
    return pl.pallas_call(
        body, name=name, grid=(nh,), in_specs=in_specs, out_specs=[col(0, 1), col(0, 1)], out_shape=[shape, shape],
        scratch_shapes=scratch, compiler_params=_params(("parallel",), 16 * s * HEAD_DIM * 4))(*args)


def band_attn_bwd(name, proj, bias, sink, dout, out, lse, dlse, *, half, q_rows, n_chains, dil, nh, group, cq, ck, cv):
    s, w = proj.shape
    g = _Band(s, half, q_rows, n_chains, dil)
    nkv = nh // group
    has_sink = sink is not None
    has_dl = dlse is not None
    n_in = 7 + int(has_sink) + int(has_dl)
    n_out = 4 + int(has_sink)

    def body(*refs):
        ins, outs, scr = refs[:n_in], refs[n_in:n_in + n_out], refs[n_in + n_out:]
        q_ref, k_ref, v_ref, b_ref, do_ref, o_ref, l_ref = ins[:7]
        s_ref = ins[7] if has_sink else None
        dl_ref = ins[n_in - 1] if has_dl else None
        dq_ref, dk_ref, dv_ref, db_ref = outs[:4]
        ks, vs, dks, dvs = scr[:4]
        scr = list(scr[4:])
        dsa = scr.pop(0) if has_sink else None
        if g.staged:
            qs, dos, os_, ls, dqs = scr[:5]
            dls = scr[5] if has_dl else None
            g.stage(qs, q_ref)
            g.stage(dos, do_ref)
            g.stage(os_, o_ref)
            g.stage(ls, l_ref)
            if has_dl:
                g.stage(dls, dl_ref)
        else:
            qs, dos, os_, ls, dqs, dls = None, do_ref, o_ref, l_ref, dq_ref, dl_ref
        h = pl.program_id(0)
        g.stage_kv(ks, k_ref)
        g.stage_kv(vs, v_ref)
        dks[...] = jnp.zeros_like(dks)
        dvs[...] = jnp.zeros_like(dvs)
        db_ref[...] = jnp.zeros_like(db_ref)
        bias_v = b_ref[...]
        if has_sink:
            sk = s_ref[h]
            dsa[...] = jnp.zeros_like(dsa)

        def chain(item):
            qoff, koff, edge = g.offsets(item)
            rows = pl.ds(qoff, g.q_rows)
            win = pl.ds(koff, g.win)
            qv = qs[rows, :] if g.staged else q_ref[rows, :].astype(BF16)
            kw_ = ks[win, :]
            vw_ = vs[win, :]
            sc = lax.dot_general(qv, kw_, (((1,), (1,)), ((), ())), preferred_element_type=F32) * ATTN_SCALE
            lv = ls[rows, :][:, 0:1]
            p = jnp.exp(sc + bias_v + edge - lv)
            dov = dos[rows, :]
            delta = jnp.sum(dov * os_[rows, :], axis=-1, keepdims=True)
            dob = dov.astype(BF16)
            dp = lax.dot_general(dob, vw_, (((1,), (1,)), ((), ())), preferred_element_type=F32)
            t = dp - delta
            if has_dl:
                t = t + dls[rows, :][:, 0:1]
            ds = p * t
            dsb = (ds * ATTN_SCALE).astype(BF16)
            dq = lax.dot_general(dsb, kw_, (((1,), (0,)), ((), ())), preferred_element_type=F32)
            dkc = lax.dot_general(dsb, qv, (((0,), (0,)), ((), ())), preferred_element_type=F32)
            dvc = lax.dot_general(p.astype(BF16), dob, (((0,), (0,)), ((), ())), preferred_element_type=F32)
            dsk = jnp.exp(sk - lv) * delta if has_sink else None
            return rows, win, dq, dkc, dvc, ds, dsk

        def step(i, carry):
            res = [chain(i * n_chains + u) for u in range(n_chains)]
            ds_sum = res[0][5]
            for rr in res[1:]:
                ds_sum = ds_sum + rr[5]
            db_ref[...] += ds_sum
            for rows, win, dq, dkc, dvc, ds, dsk in res:
                dqs[rows, :] = dq
                dks[win, :] += dkc
                dvs[win, :] += dvc
                if has_sink:
                    dsa[...] += dsk
            return carry

        lax.fori_loop(0, g.n_items // n_chains, step, 0)

        if g.staged:
            g.unstage(dq_ref, dqs)

        def emit_kv(add):
            for r in range(dil):
                lo = r * g.pad + half
                for dst_ref, src in ((dk_ref, dks), (dv_ref, dvs)):
                    val = src[lo:lo + g.seg, :]
                    if add:
                        val = val + dst_ref[g.rows_of(r), :]
                    dst_ref[g.rows_of(r), :] = val

        if group == 1:
            emit_kv(False)
        else:
            @pl.when(h % group == 0)
            def _():
                emit_kv(False)

            @pl.when(h % group != 0)
            def _():
                emit_kv(True)
        if has_sink:
            outs[4][...] = jnp.full((SUBLANES, LANES), -jnp.sum(dsa[...]), F32)

    def col(c0, per):
        return pl.BlockSpec((s, HEAD_DIM), lambda h: (0, c0 // LANES + h // per))

    b_spec = pl.BlockSpec((None, g.q_rows, g.win), lambda h: (h, 0, 0))
    in_specs = [col(cq, 1), col(ck, group), col(cv, group), b_spec, col(0, 1), col(0, 1), col(0, 1)]
    args = [proj, proj, proj, bias, dout, out, lse]
    if has_sink:
        in_specs.append(pl.BlockSpec(memory_space=pltpu.SMEM))
        args.append(sink)
    if has_dl:
        in_specs.append(col(0, 1))
        args.append(dlse)
    out_specs = [col(0, 1), col(0, group), col(0, group), b_spec]
    out_shape = [jax.ShapeDtypeStruct((s, nh * HEAD_DIM), F32), jax.ShapeDtypeStruct((s, nkv * HEAD_DIM), F32),
                 jax.ShapeDtypeStruct((s, nkv * HEAD_DIM), F32), jax.ShapeDtypeStruct((nh, g.q_rows, g.win), F32)]
    scratch = [pltpu.VMEM((dil * g.pad, HEAD_DIM), BF16), pltpu.VMEM((dil * g.pad, HEAD_DIM), BF16),
               pltpu.VMEM((dil * g.pad, HEAD_DIM), F32), pltpu.VMEM((dil * g.pad, HEAD_DIM), F32)]
    if has_sink:
        out_specs.append(pl.BlockSpec((None, SUBLANES, LANES), lambda h: (h, 0, 0)))
        out_shape.append(jax.ShapeDtypeStruct((nh, SUBLANES, LANES), F32))
        scratch.append(pltpu.VMEM((g.q_rows, 1), F32))
    if g.staged:
        scratch += [pltpu.VMEM((s, HEAD_DIM), BF16)] + [pltpu.VMEM((s, HEAD_DIM), F32)] * (4 + int(has_dl))
    res = pl.pallas_call(
        body, name=name, grid=(nh,), in_specs=in_specs, out_specs=out_specs, out_shape=out_shape,
        scratch_shapes=scratch, compiler_params=_params(("arbitrary",), 28 * s * HEAD_DIM * 4))(*args)
    return res[0], res[1], res[2], res[3], (res[4] if has_sink else None)


def dil_merge_fwd(name, outs, lses):
    s, w = outs[0].shape
    ts = _pick(s, ROW_TILE_CANDS)
    ng = len(outs)

    def body(*refs):
        o_refs, l_refs, y_ref = refs[:ng], refs[ng:2 * ng], refs[2 * ng]
        ls = [l[...] for l in l_refs]
        mx = ls[0]
        for l in ls[1:]:
            mx = jnp.maximum(mx, l)
        es = [jnp.exp(l - mx) for l in ls]
        tot = es[0]
        for e in es[1:]:
            tot = tot + e
        acc = (es[0] / tot) * o_refs[0][...]
        for e, o in zip(es[1:], o_refs[1:]):
            acc = acc + (e / tot) * o[...]
        y_ref[...] = acc.astype(y_ref.dtype)

    row = pl.BlockSpec((ts, w), lambda i: (i, 0))
    return pl.pallas_call(
        body, name=name, grid=(s // ts,), in_specs=[row] * (2 * ng), out_specs=row,
        out_shape=jax.ShapeDtypeStruct((s, w), BF16),
        compiler_params=_params(("parallel",), 10 * ts * w * 4))(*outs, *lses)


def dil_merge_bwd(name, dy, outs, lses):
    s, w = outs[0].shape
    ts = _pick(s, ROW_TILE_CANDS)
    ng = len(outs)
    nhead = w // HEAD_DIM

    def body(*refs):
        dy_ref = refs[0]
        o_refs, l_refs = refs[1:1 + ng], refs[1 + ng:1 + 2 * ng]
        do_refs, dl_refs = refs[1 + 2 * ng:1 + 3 * ng], refs[1 + 3 * ng:1 + 4 * ng]
        for hh in range(nhead):
            cols = slice(hh * HEAD_DIM, (hh + 1) * HEAD_DIM)
            dyv = dy_ref[:, cols]
            ls = [l[:, cols] for l in l_refs]
            mx = ls[0]
            for l in ls[1:]:
                mx = jnp.maximum(mx, l)
            es = [jnp.exp(l - mx) for l in ls]
            tot = es[0]
            for e in es[1:]:
                tot = tot + e
            alphas = [e / tot for e in es]
            dal = [jnp.broadcast_to(jnp.sum(dyv * o[:, cols], axis=-1, keepdims=True), dyv.shape) for o in o_refs]
            mean = alphas[0] * dal[0]
            for a, d in zip(alphas[1:], dal[1:]):
                mean = mean + a * d
            for g in range(ng):
                do_refs[g][:, cols] = alphas[g] * dyv
                dl_refs[g][:, cols] = alphas[g] * (dal[g] - mean)

    row = pl.BlockSpec((ts, w), lambda i: (i, 0))
    shape = jax.ShapeDtypeStruct((s, w), F32)
    res = pl.pallas_call(
        body, name=name, grid=(s // ts,), in_specs=[row] * (1 + 2 * ng), out_specs=[row] * (2 * ng),
        out_shape=[shape] * (2 * ng),
        compiler_params=_params(("parallel",), 16 * ts * w * 4))(dy, *outs, *lses)
    return res[:ng], res[ng:]


def _adamw(w, g, m, v):
    m = ADAM_B1 * m + (1.0 - ADAM_B1) * g
    v = ADAM_B2 * v + (1.0 - ADAM_B2) * (g * g)
    m_hat = m / (1.0 - ADAM_B1 ** ADAM_STEP)
    v_hat = v / (1.0 - ADAM_B2 ** ADAM_STEP)
    delta = -ADAM_LR * (m_hat / (jnp.sqrt(v_hat) + ADAM_EPS) + ADAM_WD * w)
    return delta, m, v


def _row_tile(r, c, budget=1 << 20):
    if r * c * 4 <= budget or r % SUBLANES:
        return r
    for t in (1024, 512, 256, 128, 64, 32, 16, 8):
        if r % t == 0 and t * c * 4 <= budget:
            return t
    return SUBLANES


def adam_small(name, g, w, m, v):
    def body(g_ref, w_ref, m_ref, v_ref, d_ref, nm_ref, nv_ref):
        d_ref[...], nm_ref[...], nv_ref[...] = _adamw(w_ref[...], g_ref[...], m_ref[...], v_ref[...])

    shape = jax.ShapeDtypeStruct(w.shape, F32)
    return pl.pallas_call(body, name=name, out_shape=[shape, shape, shape])(g, w, m, v)


def reduce_adam(name, mine, theirs, w, m, v):
    nq, r, c = mine.shape
    tr = _row_tile(r, c)

    def body(*refs):
        parts, (w_ref, m_ref, v_ref, g_ref, d_ref, nm_ref, nv_ref) = refs[:nq], refs[nq:]
        g = parts[0][...].astype(F32)
        for p_ref in parts[1:]:
            g = g + p_ref[...].astype(F32)
        g_ref[...] = g
        d_ref[...], nm_ref[...], nv_ref[...] = _adamw(w_ref[...], g, m_ref[...], v_ref[...])

    def slot(q):
        return pl.BlockSpec((None, tr, c), lambda i: (q, i, 0))

    row = pl.BlockSpec((tr, c), lambda i: (i, 0))
    shape = jax.ShapeDtypeStruct((r, c), F32)
    return pl.pallas_call(
        body, name=name, grid=(r // tr,), in_specs=[slot(q) for q in range(nq)] + [row, row, row],
        out_specs=[row] * 4, out_shape=[shape] * 4,
        compiler_params=_params(("parallel",), (nq * 2 + 7 * 4) * tr * c))(mine, *[theirs] * (nq - 1), w, m, v)


def _place():
    return lax.axis_index("x"), lax.axis_index("y"), lax.axis_index("c")


def _flip(pos, bits):
    return tuple((1 - p) if b else p for p, b in zip(pos, bits))


def _index(pos):
    return 4 * pos[0] + 2 * pos[1] + pos[2]


ANY = pl.BlockSpec(memory_space=pl.ANY)


HBM = pl.BlockSpec(memory_space=pltpu.HBM)
SEM = pl.BlockSpec(memory_space=pltpu.SEMAPHORE)
EFFECT = pltpu.SideEffectType.DATAFLOW_SIDE_EFFECTING
TO_SIBLING = (0, 0, 1)
TO_CHIPS = [(1, 0, 0), (0, 1, 0), (1, 1, 0)]


def _in_hbm(a):
    return pltpu.with_memory_space_constraint(a, pltpu.HBM)


def _token_value(token):
    return token[0, 0]


def _when(pred, fn):
    if pred is True:
        fn()
    elif pred is not False:
        pl.when(pred)(fn)


def _plan_copy(k, entry, ins, lnd, send_sems, recv_sems):
    a, src_a, sblk, lblk, to, send_if, recv_if = entry
    src = lnd[a] if src_a is None else ins[src_a]
    return pltpu.make_async_remote_copy(
        src_ref=src.at[sblk], dst_ref=lnd[a].at[lblk], send_sem=send_sems.at[k], recv_sem=recv_sems.at[k],
        device_id=to, device_id_type=MESH), send_if, recv_if


def split_start(name, srcs, lands, plan, after):
    ns, nl = len(srcs), len(lands)
    n_copies = len(plan((0, 0, 0)))

    def body(*refs):
        ins, lnd = refs[:ns], refs[ns:ns + nl]
        send_sems, recv_sems = refs[ns + nl + 1], refs[ns + nl + 2]
        token = refs[-1]
        for k, entry in enumerate(plan(_place())):
            cp, send_if, _ = _plan_copy(k, entry, ins, lnd, send_sems, recv_sems)
            _when(send_if, cp.start)
        token[...] = jnp.zeros_like(token)

    outs = pl.pallas_call(
        body, name=name,
        out_shape=(pltpu.SemaphoreType.DMA((n_copies,)), pltpu.SemaphoreType.DMA((n_copies,)),
                   *[pltpu.HBM(a.shape, a.dtype) for a in srcs], *[pltpu.HBM(a.shape, a.dtype) for a in lands],
                   jax.ShapeDtypeStruct((SUBLANES, LANES), F32)),
        in_specs=[HBM] * (ns + nl) + [ANY],
        out_specs=(SEM, SEM, *[HBM] * (ns + nl), pl.BlockSpec(memory_space=pltpu.VMEM)),
        input_output_aliases={i: 2 + i for i in range(ns + nl)},
        compiler_params=pltpu.CompilerParams(has_side_effects=EFFECT),
    )(*[_in_hbm(a) for a in srcs], *[_in_hbm(a) for a in lands], after)
    return outs[0], outs[1], list(outs[2:2 + ns]), list(outs[2 + ns:2 + ns + nl]), outs[-1]


def split_wait(name, send_sems, recv_sems, srcs, lands, plan, after):
    ns, nl = len(srcs), len(lands)

    def body(*refs):
        ins, lnd = refs[:ns], refs[ns:ns + nl]
        s_sems, r_sems = refs[ns + nl], refs[ns + nl + 1]
        for k, entry in enumerate(plan(_place())):
            cp, send_if, recv_if = _plan_copy(k, entry, ins, lnd, s_sems, r_sems)
            _when(send_if, cp.wait_send)
            _when(recv_if, cp.wait_recv)
        refs[-1][...] = jnp.zeros((SUBLANES, LANES), F32)

    outs = pl.pallas_call(
        body, name=name,
        out_shape=(*[pltpu.HBM(a.shape, a.dtype) for a in srcs], *[pltpu.HBM(a.shape, a.dtype) for a in lands],
                   jax.ShapeDtypeStruct((SUBLANES, LANES), F32)),
        in_specs=[HBM] * (ns + nl) + [SEM, SEM, ANY],
        out_specs=(*[HBM] * (ns + nl), pl.BlockSpec(memory_space=pltpu.VMEM)),
        input_output_aliases={i: i for i in range(ns + nl)},
        compiler_params=pltpu.CompilerParams(has_side_effects=EFFECT),
    )(*srcs, *lands, send_sems, recv_sems, after)
    return list(outs[:ns]), list(outs[ns:ns + nl]), outs[-1]


NORTH = 1


def ag_plan(n):
    def plan(me):
        x, y, c = me
        entries = []
        for a in range(n):
            for t in (NORTH, 1 - NORTH):
                blk = _index((x, y, t))
                for rel in TO_CHIPS:
                    entries.append((a, None, blk, blk, _flip((x, y, t), rel), c == NORTH, c == t))
        return entries
    return plan


def ag_pair(name, lands, after):
    n = len(lands)

    def body(*refs):
        lnd = refs[n + 1:2 * n + 1]
        token = refs[2 * n + 1]
        send_sems, recv_sems = refs[2 * n + 2:]
        token[...] = jnp.zeros_like(token)
        me = _place()
        sibling = _flip(me, TO_SIBLING)
        copies = []
        for a in range(n):
            mine, theirs = lnd[a].at[_index(me)], lnd[a].at[_index(sibling)]
            cp = pltpu.make_async_remote_copy(src_ref=mine, dst_ref=mine, send_sem=send_sems.at[a],
                                              recv_sem=recv_sems.at[a], device_id=sibling, device_id_type=MESH)
            cp.start()
            copies.append((cp, pltpu.make_async_remote_copy(
                src_ref=mine, dst_ref=theirs, send_sem=send_sems.at[a], recv_sem=recv_sems.at[a], device_id=sibling,
                device_id_type=MESH)))
        for cp, arrival in copies:
            arrival.wait_recv()
        for cp, arrival in copies:
            cp.wait_send()

    outs = pl.pallas_call(
        body, name=name, in_specs=[ANY] * (n + 1), out_specs=[ANY] * n + [pl.BlockSpec(memory_space=pltpu.VMEM)],
        out_shape=[jax.ShapeDtypeStruct(l.shape, l.dtype) for l in lands]
        + [jax.ShapeDtypeStruct((SUBLANES, LANES), F32)],
        input_output_aliases={a: a for a in range(n)},
        scratch_shapes=[pltpu.SemaphoreType.DMA((n,)), pltpu.SemaphoreType.DMA((n,))],
    )(*lands, after)
    return list(outs[:n]), outs[n]


def ag_start(name, lands, after):
    return split_start(name, [], lands, ag_plan(len(lands)), after)


def pass_plan(n):
    def plan(me):
        sibling = _flip(me, TO_SIBLING)
        return [(a, None, _index(_flip(me, rel)), _index(_flip(me, rel)), sibling, True, True)
                for a in range(n) for rel in TO_CHIPS]
    return plan


def ag_finish(name, lands):
    n = len(lands)

    def body(*refs):
        lnd = refs[n:2 * n]
        send_sems, recv_sems = refs[2 * n:]
        me = _place()
        sibling = _flip(me, TO_SIBLING)
        copies = []
        for a in range(n):
            for j, rel in enumerate(TO_CHIPS):
                blk = lnd[a].at[_index(_flip(me, rel))]
                there = lnd[a].at[_index(_flip(sibling, rel))]
                cp = pltpu.make_async_remote_copy(
                    src_ref=blk, dst_ref=blk, send_sem=send_sems.at[a * 3 + j], recv_sem=recv_sems.at[a * 3 + j],
                    device_id=sibling, device_id_type=MESH)
                cp.start()
                copies.append((cp, pltpu.make_async_remote_copy(
                    src_ref=blk, dst_ref=there, send_sem=send_sems.at[a * 3 + j], recv_sem=recv_sems.at[a * 3 + j],
                    device_id=sibling, device_id_type=MESH)))
        for cp, arrival in copies:
            arrival.wait_recv()
        for cp, arrival in copies:
            cp.wait_send()

    return pl.pallas_call(
        body, name=name, in_specs=[ANY] * n, out_specs=[ANY] * n,
        out_shape=[jax.ShapeDtypeStruct(l.shape, l.dtype) for l in lands],
        input_output_aliases={a: a for a in range(n)},
        scratch_shapes=[pltpu.SemaphoreType.DMA((3 * n,)), pltpu.SemaphoreType.DMA((3 * n,))],
    )(*lands)


REL = [(b >> 2 & 1, b >> 1 & 1, b & 1) for b in range(N_DEV)]


CHIP_REL = [(0, 0, 0)] + TO_CHIPS
N_CHIPS = len(CHIP_REL)


def rs_pair(name, parts):
    n = len(parts)

    def body(*refs):
        ins, got = refs[:n], refs[n:2 * n]
        send_sems, recv_sems = refs[2 * n:]
        me = _place()
        sibling = _flip(me, TO_SIBLING)
        remote = []
        for a in range(n):
            for q, rel in enumerate(CHIP_REL):
                k = a * N_CHIPS + q
                cp = pltpu.make_async_remote_copy(
                    src_ref=ins[a].at[_index(_flip(sibling, rel))], dst_ref=got[a].at[q], send_sem=send_sems.at[k],
                    recv_sem=recv_sems.at[k], device_id=sibling, device_id_type=MESH)
                cp.start()
                remote.append(cp)
        for cp in remote:
            cp.wait_recv()
        for cp in remote:
            cp.wait_send()

    shapes = [jax.ShapeDtypeStruct((N_CHIPS,) + tuple(p.shape[1:]), p.dtype) for p in parts]
    res = pl.pallas_call(
        body, name=name, in_specs=[ANY] * n, out_specs=[ANY] * n, out_shape=shapes,
        scratch_shapes=[pltpu.SemaphoreType.DMA((N_CHIPS * n,)), pltpu.SemaphoreType.DMA((N_CHIPS * n,))],
    )(*parts)
    return list(res)


def own_blocks():
    me = _place()
    return jnp.stack([_index(_flip(me, rel)) for rel in CHIP_REL]).astype(jnp.int32)


def pair_add(name, blocks, parts, got):
    nq, r, c = got.shape
    tr = _row_tile(r, c, budget=6 << 20)

    def body(blk_ref, a_ref, b_ref, o_ref):
        o_ref[...] = (a_ref[...].astype(F32) + b_ref[...].astype(F32)).astype(o_ref.dtype)

    spec = pl.BlockSpec((None, tr, c), lambda q, i, blk: (q, i, 0))
    return pl.pallas_call(
        body, name=name,
        grid_spec=pltpu.PrefetchScalarGridSpec(
            num_scalar_prefetch=1, grid=(nq, r // tr),
            in_specs=[pl.BlockSpec((None, tr, c), lambda q, i, blk: (blk[q], i, 0)), spec], out_specs=spec),
        out_shape=pltpu.HBM(got.shape, got.dtype),
        compiler_params=_params(("arbitrary", "arbitrary"), 6 * tr * c * 2))(blocks, parts, got)


def rs_pair_plan(n):
    def plan(me):
        sibling = _flip(me, TO_SIBLING)
        return [(a, a, _index(_flip(sibling, rel)), q, sibling, True, True)
                for a in range(n) for q, rel in enumerate(CHIP_REL)]
    return plan


def rs_plan(n):
    def plan(me):
        return [(a, a, q, q, _flip(me, CHIP_REL[q]), True, True) for a in range(n) for q in range(1, N_CHIPS)]
    return plan


def rs_start(name, sums, after):
    lands = [lax.empty(t.shape, t.dtype) for t in sums]
    return split_start(name, sums, lands, rs_plan(len(sums)), after)


def allreduce_small(name, pack, after):
    rows, lanes = pack.shape

    def body(x_ref, after_ref, o_ref, land, send_sems, recv_sems):
        me = _place()
        idx = _index(me)
        land[idx] = x_ref[...]
        copies = []
        for r in range(1, N_DEV):
            peer = _flip(me, REL[r])
            cp = pltpu.make_async_remote_copy(
                src_ref=x_ref, dst_ref=land.at[idx], send_sem=send_sems.at[r - 1], recv_sem=recv_sems.at[r - 1],
                device_id=peer, device_id_type=MESH)
            cp.start()
            copies.append(cp)
        for cp in copies:
            cp.wait_recv()
        for cp in copies:
            cp.wait_send()
        acc = land[0]
        for i in range(1, N_DEV):
            acc = acc + land[i]
        o_ref[...] = acc

    return pl.pallas_call(
        body, name=name, in_specs=[pl.BlockSpec(memory_space=pltpu.VMEM), ANY],
        out_specs=pl.BlockSpec(memory_space=pltpu.VMEM), out_shape=jax.ShapeDtypeStruct((rows, lanes), F32),
        scratch_shapes=[pltpu.VMEM((N_DEV, rows, lanes), F32), pltpu.SemaphoreType.DMA((7,)),
                        pltpu.SemaphoreType.DMA((7,))],
    )(pack, after)


def _pad_rows(a, rows):
    return jnp.pad(a, ((0, rows - a.shape[0]), (0, 0)))


def _as_tiles(vec):
    n = vec.shape[0]
    rows = -(-n // LANES)
    rows = -(-rows // SUBLANES) * SUBLANES
    return jnp.pad(vec, (0, rows * LANES - n)).reshape(rows, LANES)


def kernel(x, p, rel_bias_table, attn_norm, w_in, sink_a, w_branch_a, w_branch_b, w_out, ffn_norm, w_ffn_gate, w_ffn_up, conv_w, conv_b, w_ffn_down, ple_norm, w_ple_gate, w_ple_proj, final_norm, loss_target, m_rel_bias_table, m_attn_norm, m_w_in, m_sink_a, m_w_branch_a, m_w_branch_b, m_w_out, m_ffn_norm, m_w_ffn_gate, m_w_ffn_up, m_conv_w, m_conv_b, m_w_ffn_down, m_ple_norm, m_w_ple_gate, m_w_ple_proj, m_final_norm, v_rel_bias_table, v_attn_norm, v_w_in, v_sink_a, v_w_branch_a, v_w_branch_b, v_w_out, v_ffn_norm, v_w_ffn_gate, v_w_ffn_up, v_conv_w, v_conv_b, v_w_ffn_down, v_ple_norm, v_w_ple_gate, v_w_ple_proj, v_final_norm):
    xs = x[0]
    s, d = xs.shape
    ps = p[0, 0]
    target = loss_target[0]
    me = 4 * lax.axis_index("x") + 2 * lax.axis_index("y") + lax.axis_index("c")

    big = dict(w_in=w_in[0], w_branch_a=w_branch_a[0], w_branch_b=w_branch_b[0], w_out=w_out[0],
               w_ffn_gate=w_ffn_gate[0], w_ffn_up=w_ffn_up[0], w_ffn_down=w_ffn_down[0],
               w_ple_gate=w_ple_gate[0], w_ple_proj=w_ple_proj[0])
    big_m = dict(w_in=m_w_in[0], w_branch_a=m_w_branch_a[0], w_branch_b=m_w_branch_b[0], w_out=m_w_out[0],
                 w_ffn_gate=m_w_ffn_gate[0], w_ffn_up=m_w_ffn_up[0], w_ffn_down=m_w_ffn_down[0],
                 w_ple_gate=m_w_ple_gate[0], w_ple_proj=m_w_ple_proj[0])
    big_v = dict(w_in=v_w_in[0], w_branch_a=v_w_branch_a[0], w_branch_b=v_w_branch_b[0], w_out=v_w_out[0],
                 w_ffn_gate=v_w_ffn_gate[0], w_ffn_up=v_w_ffn_up[0], w_ffn_down=v_w_ffn_down[0],
                 w_ple_gate=v_w_ple_gate[0], w_ple_proj=v_w_ple_proj[0])
    names = list(big)
    nf = big["w_ffn_gate"].shape[1]

    shards = {k: big[k].astype(BF16) for k in names}
    shards["conv_w"] = _pad_rows(conv_w[0], SUBLANES)
    flipped = ("w_ffn_gate", "w_ffn_up")
    for k in flipped:
        big[k], big_m[k], big_v[k] = big[k].T, big_m[k].T, big_v[k].T
    ag_groups = [["w_in"], ["w_branch_a", "w_branch_b", "w_out"], ["w_ffn_gate", "w_ffn_up", "conv_w"],
                 ["w_ffn_down", "w_ple_gate", "w_ple_proj"]]
    ag_started = {}
    wg = {}

    ag_paired, ag_passing = {}, {}

    def pair(gi, after):
        lands = [lax.dynamic_update_index_in_dim(lax.empty((N_DEV,) + shards[k].shape, shards[k].dtype), shards[k],
                                                 me, 0) for k in ag_groups[gi]]
        ag_paired[gi], token = ag_pair(f"ag_pair{gi}", lands, after)
        return token

    def start(gi, after):
        s_sems, r_sems, _, lands, token = ag_start(f"ag_start{gi}", ag_paired[gi], after)
        ag_started[gi] = (s_sems, r_sems, lands)
        return token

    def landed(gi, after):
        s_sems, r_sems, lands = ag_started[gi]
        return split_wait(f"ag_wait{gi}", s_sems, r_sems, [], lands, ag_plan(len(lands)), after)[1:]

    def pass_on(gi, lands, after):
        s_sems, r_sems, _, lands, token = split_start(f"ag_pass{gi}", [], lands, pass_plan(len(lands)), after)
        ag_passing[gi] = (s_sems, r_sems, lands)
        return token

    def ready(gi, after):
        s_sems, r_sems, lands = ag_passing[gi]
        lands = split_wait(f"ag_ready{gi}", s_sems, r_sems, [], lands, pass_plan(len(lands)), after)[1]
        wg.update(zip(ag_groups[gi], lands))

    cb = conv_b.reshape(N_DEV, 1, nf)

    table_t = rel_bias_table.T
    geo_a = dict(half=A_BLOCK, q_rows=ATTN_Q_ROWS, n_chains=ATTN_CHAINS, dil=1, nh=A_Q_HEADS, group=A_GROUP,
                 cq=COL_QA, ck=COL_KA, cv=COL_VA)
    geo_b = [dict(half=B_BLOCK, q_rows=min(ATTN_Q_ROWS, s // dil), n_chains=ATTN_CHAINS, dil=dil,
                  nh=B_HEADS_PER_GROUP, group=1, cq=COL_QB + g * B_OUT_W, ck=COL_KB + g * B_OUT_W,
                  cv=COL_VB + g * B_OUT_W) for g, (_, dil) in enumerate(B_PATTERNS)]
    bucket_a = bucket_tile(geo_a["q_rows"], A_BLOCK, 1)
    bias_a = bias_build("bias_a", table_t, bucket_a, 0, A_Q_HEADS, A_BLOCK)
    buckets_b = [bucket_tile(gb["q_rows"], B_BLOCK, gb["dil"]) for gb in geo_b]
    biases_b = [bias_build(f"bias_b{g}", table_t, buckets_b[g], A_Q_HEADS + g * B_HEADS_PER_GROUP, B_HEADS_PER_GROUP,
                           B_BLOCK) for g in range(len(B_PATTERNS))]

    token = start(1, pair(1, start(0, pair(0, xs))))
    h = rms_fwd("rms_attn", xs, attn_norm + _token_value(token))
    lands0, token = landed(0, pair(3, pair(2, h)))
    token = start(2, token)
    wg["w_in"] = ag_finish("ag_finish0", lands0)[0]
    proj = mm_cols("proj_in", h, wg["w_in"], F32, fold=True, after=token)
    token = start(3, pass_on(1, landed(1, proj)[0], proj))
    sink = sink_a[0] + _token_value(token)
    ya, lse_a = band_attn_fwd("attn_a_fwd", proj, bias_a, sink, **geo_a)
    outs_b, lses_b = [], []
    for g in range(len(B_PATTERNS)):
        o, l = band_attn_fwd(f"attn_b{g}_fwd", proj, biases_b[g], None, **geo_b[g])
        outs_b.append(o)
        lses_b.append(l)
    yb = dil_merge_fwd("dil_merge_fwd", outs_b, lses_b)
    ready(1, yb)
    token = pass_on(2, landed(2, yb)[0], yb)
    w_out_full = wg["w_out"].reshape(d, d)
    ta = mm_cols("branch_a", ya, wg["w_branch_a"], F32, fold=True, after=token)
    tb = mm_cols("branch_b", yb, wg["w_branch_b"], F32, fold=True)
    merged = gate_merge_fwd("gate_merge_fwd", proj, ta, tb, d)
    x1 = mm_plain("mix_out", merged, w_out_full, F32, res=xs)

    hf = rms_fwd("rms_ffn", x1, ffn_norm)
    ready(2, hf)
    token = pass_on(3, landed(3, hf)[0], hf)
    cw = wg["conv_w"]
    gpre = mm_cols("ffn_gate", hf, wg["w_ffn_gate"], F32, fold=False, after=token)
    u = mm_cols("ffn_up", hf, wg["w_ffn_up"], F32, fold=False)
    z = ffn_mid_fwd("ffn_mid_fwd", gpre, u, cw, cb)
    ready(3, z)
    w_pg_full = wg["w_ple_gate"].reshape(d, d)
    x2 = mm_jsum("ffn_down", z, wg["w_ffn_down"], F32, res=x1)

    hp = rms_fwd("rms_ple", x2, ple_norm)
    lp = mm_plain("ple_gate", hp, w_pg_full, F32)
    pp = mm_cols("ple_proj", ps, wg["w_ple_proj"], F32, fold=True)
    loss_part, dx3, dlp, dpp, d_final = tail_fwd_bwd("tail", x2, lp, pp, final_norm.reshape(1, d), target)

    grads = {}
    rs_started = []
    blocks = own_blocks()

    exchanging = []

    def exchange(tag, keys):
        parts = [grads[k] for k in keys]
        lands = [lax.empty((N_CHIPS,) + tuple(p.shape[1:]), p.dtype) for p in parts]
        s_sems, r_sems, parts, lands, token = split_start(f"rs_pair_{tag}", parts, lands, rs_pair_plan(len(keys)), blocks)
        exchanging.append((tag, keys, s_sems, r_sems, parts, lands))
        return _token_value(token)

    def send(after):
        tag, keys, s_sems, r_sems, parts, lands = exchanging.pop(0)
        parts, got, _ = split_wait(f"rs_paired_{tag}", s_sems, r_sems, parts, lands, rs_pair_plan(len(keys)), after)
        return send_sums(tag, keys, parts, got)

    def send_sums(tag, keys, parts, got):
        sums = [pair_add(f"pair_add_{k}", blocks, p, g) for k, p, g in zip(keys, parts, got)]
        s_sems, r_sems, srcs, lands, token = rs_start(f"rs_start_{tag}", sums, blocks)
        rs_started.append((tag, keys, s_sems, r_sems, srcs, lands))
        return token

    grads["w_ple_proj"] = mm_tn_cols("d_w_ple_proj", ps, dpp, N_DEV, big["w_ple_proj"].shape[1], BF16, folded=True)
    grads["w_ple_gate"] = mm_tn_plain("d_w_ple_gate", hp, dlp, BF16).reshape(N_DEV, d // N_DEV, d)
    tok = exchange("ple", ["w_ple_proj", "w_ple_gate"])
    dhp = mm_nt_plain("d_hp", dlp, w_pg_full, F32)
    dx2, d_ple = rms_bwd("rms_ple_bwd", x2, ple_norm + tok, dhp, dx3)

    dz = mm_nt_j("d_z", dx2, wg["w_ffn_down"], BF16)
    grads["w_ffn_down"] = mm_tn_j("d_w_ffn_down", z, dx2, BF16)
    tok = _token_value(send(dz)) + exchange("down", ["w_ffn_down"])
    dg, du, dcw = ffn_mid_bwd1("ffn_mid_bwd1", gpre, u, dz, cw, cb + tok)
    dgpre = ffn_mid_bwd2("ffn_mid_bwd2", dg, cw)
    grads["w_ffn_up"] = mm_tn_j("d_w_ffn_up", du, hf, BF16)
    grads["w_ffn_gate"] = mm_tn_j("d_w_ffn_gate", dgpre, hf, BF16)
    dhf = mm_nt_jsum("d_hf_up", du, wg["w_ffn_up"], F32, folded=False)
    dhf = mm_nt_jsum("d_hf_gate", dgpre, wg["w_ffn_gate"], F32, folded=False, res=dhf)
    tok = _token_value(send(dhf)) + exchange("upgate", ["w_ffn_up", "w_ffn_gate"])
    dx1, d_ffn = rms_bwd("rms_ffn_bwd", x1, ffn_norm + tok, dhf, dx2)

    dmerged = mm_nt_plain("d_merged", dx1, w_out_full, F32)
    grads["w_out"] = mm_tn_plain("d_w_out", merged, dx1, BF16).reshape(N_DEV, d // N_DEV, d)
    dta, dtb, dga, dgb = gate_merge_bwd("gate_merge_bwd", dmerged, proj, ta, tb, d)
    grads["w_branch_a"] = mm_tn_cols("d_w_branch_a", ya, dta, N_DEV, big["w_branch_a"].shape[1], BF16, folded=True)
    grads["w_branch_b"] = mm_tn_cols("d_w_branch_b", yb, dtb, N_DEV, big["w_branch_b"].shape[1], BF16, folded=True)
    dya = mm_nt_jsum("d_ya", dta, wg["w_branch_a"], F32, folded=True)
    dyb = mm_nt_jsum("d_yb", dtb, wg["w_branch_b"], F32, folded=True)
    tok = _token_value(send(dyb)) + exchange("mix", ["w_out", "w_branch_a", "w_branch_b"])
    dqa, dka, dva, dbias_a, dsink = band_attn_bwd("attn_a_bwd", proj, bias_a, sink + tok, dya, ya, lse_a, None, **geo_a)
    douts_b, dlses_b = dil_merge_bwd("dil_merge_bwd", dyb, outs_b, lses_b)
    dq_b, dk_b, dv_b, dbias_b = [], [], [], []
    for g in range(len(B_PATTERNS)):
        dq, dk, dv, db, _ = band_attn_bwd(f"attn_b{g}_bwd", proj, biases_b[g], None, douts_b[g], outs_b[g], lses_b[g],
                                          dlses_b[g], **geo_b[g])
        dq_b.append(dq)
        dk_b.append(dk)
        dv_b.append(dv)
        dbias_b.append(db)
    dproj = jnp.concatenate([t.astype(BF16) for t in [dqa, dka, dva] + dq_b + dk_b + dv_b + [dga, dgb]], axis=1)
    token = send(dproj)
    grads["w_in"] = mm_tn_cols("d_w_in", h, dproj, N_DEV, big["w_in"].shape[1], BF16, folded=True, after=token)
    token = send_sums("in", ["w_in"], [grads["w_in"]], rs_pair("rs_pair_in", [grads["w_in"]]))
    dh = mm_nt_jsum("d_h", dproj, wg["w_in"], F32, folded=True, after=token)
    grad_x, d_attn = rms_bwd("rms_attn_bwd", xs, attn_norm, dh, dx1)

    dt_a = table_grad("table_grad_a", dbias_a, bucket_a)[:, 0, :N_BUCKETS]
    dt_b = [table_grad(f"table_grad_b{g}", dbias_b[g], buckets_b[g])[:, 0, :N_BUCKETS] for g in range(len(B_PATTERNS))]
    d_table_part = jnp.concatenate([dt_a] + dt_b, axis=0).T

    pieces = [
        ("loss", loss_part[0, :1]),
        ("table", d_table_part.reshape(-1)),
        ("attn_norm", d_attn.reshape(-1)),
        ("sink", dsink[:, 0, 0]),
        ("ffn_norm", d_ffn.reshape(-1)),
        ("conv_w", dcw[:, 0:3, :].reshape(-1)),
        ("conv_b", dcw[:, 3, :].reshape(-1)),
        ("ple_norm", d_ple.reshape(-1)),
        ("final_norm", d_final.reshape(-1)),
    ]
    tiles = [_as_tiles(v) for _, v in pieces]
    pack = jnp.concatenate(tiles, axis=0)

    out_g, out_d, out_m, out_v = {}, {}, {}, {}

    def finish(group, after):
        tag, keys, s_sems, r_sems, srcs, lands = group
        srcs, lands, _ = split_wait(f"rs_wait_{tag}", s_sems, r_sems, srcs, lands, rs_plan(len(keys)), after)
        for k, mine, theirs in zip(keys, srcs, lands):
            res = reduce_adam("adam_" + k, mine, theirs, big[k], big_m[k], big_v[k])
            after = res[1]
            out_g[k], out_d[k], out_m[k], out_v[k] = [(t.T if k in flipped else t)[None] for t in res]
        return after

    after = pack
    for group in rs_started[:-1]:
        after = finish(group, after)
    total = allreduce_small("allreduce_small", pack, after)
    finish(rs_started[-1], total)
    small = {}
    row = 0
    for (nm, v), t in zip(pieces, tiles):
        small[nm] = total[row:row + t.shape[0]].reshape(-1)[:v.shape[0]]
        row += t.shape[0]
    loss = small["loss"][0]
    g_small = dict(
        rel_bias_table=small["table"].reshape(rel_bias_table.shape),
        attn_norm=small["attn_norm"].reshape(attn_norm.shape),
        sink_a=small["sink"].reshape(sink_a.shape),
        ffn_norm=small["ffn_norm"].reshape(ffn_norm.shape),
        conv_w=lax.dynamic_index_in_dim(small["conv_w"].reshape(N_DEV, 3, nf), me, 0, keepdims=False)[None],
        conv_b=small["conv_b"].reshape(conv_b.shape),
        ple_norm=small["ple_norm"].reshape(ple_norm.shape),
        final_norm=small["final_norm"].reshape(1, d),
    )
    w_small = dict(rel_bias_table=(rel_bias_table, m_rel_bias_table, v_rel_bias_table),
                   attn_norm=(attn_norm, m_attn_norm, v_attn_norm), sink_a=(sink_a, m_sink_a, v_sink_a),
                   ffn_norm=(ffn_norm, m_ffn_norm, v_ffn_norm), conv_w=(conv_w, m_conv_w, v_conv_w),
                   conv_b=(conv_b, m_conv_b, v_conv_b), ple_norm=(ple_norm, m_ple_norm, v_ple_norm),
                   final_norm=(final_norm, m_final_norm, v_final_norm))

    for k, (wv, mv, vv) in w_small.items():
        shape = wv.shape
        two_d = (1, shape[0]) if len(shape) == 1 else ((shape[0] * shape[1], shape[2]) if len(shape) == 3 else shape)
        gk = g_small[k].reshape(two_d)
        dl, nm, nv = adam_small("adam_" + k, gk, wv.reshape(two_d), mv.reshape(two_d), vv.reshape(two_d))
        out_g[k], out_d[k], out_m[k], out_v[k] = gk.reshape(shape), dl.reshape(shape), nm.reshape(shape), nv.reshape(shape)

    order = ["rel_bias_table", "attn_norm", "w_in", "sink_a", "w_branch_a", "w_branch_b", "w_out", "ffn_norm",
             "w_ffn_gate", "w_ffn_up", "conv_w", "conv_b", "w_ffn_down", "ple_norm", "w_ple_gate", "w_ple_proj",
             "final_norm"]
    return (loss, grad_x[None], *[out_g[k] for k in order], *[out_d[k] for k in order],
            *[out_m[k] for k in order], *[out_v[k] for k in order])
```

```python
import math

import jax
import jax.numpy as jnp
from jax import lax
from jax.experimental import pallas as pl
from jax.experimental.pallas import tpu as pltpu

F32 = jnp.float32
BF16 = jnp.bfloat16
MESH = pl.DeviceIdType.MESH
N_DEV = 8

HEAD_DIM = 128
A_Q_HEADS = 8
A_KV_HEADS = 2
A_GROUP = A_Q_HEADS // A_KV_HEADS
A_BLOCK = 128
B_PATTERNS = ((128, 1), (512, 4), (2048, 16))
B_HEADS_PER_GROUP = 4
B_HEADS = len(B_PATTERNS) * B_HEADS_PER_GROUP
B_BLOCK = 64
N_BUCKETS = 32
MAX_DISTANCE = 1024
A_Q_W = A_Q_HEADS * HEAD_DIM
A_KV_W = A_KV_HEADS * HEAD_DIM
B_W = B_HEADS * HEAD_DIM
B_OUT_W = B_HEADS_PER_GROUP * HEAD_DIM
COL_QA = 0
COL_KA = COL_QA + A_Q_W
COL_VA = COL_KA + A_KV_W
COL_QB = COL_VA + A_KV_W
COL_KB = COL_QB + B_W
COL_VB = COL_KB + B_W
COL_GATES = COL_VB + B_W
RMS_EPS = 1e-6
NEG_INF = -1e30
ATTN_SCALE = HEAD_DIM ** -0.5
ATTN_Q_ROWS = 256
ATTN_CHAINS = 2

ADAM_LR = 0.001
ADAM_B1 = 0.9
ADAM_B2 = 0.999
ADAM_EPS = 1e-08
ADAM_WD = 0.01
ADAM_STEP = 10

GELU_C = math.sqrt(2.0 / math.pi)
GELU_A = 0.044715

V7X_VMEM_BYTES = 64 * 1024 * 1024
VMEM_CEILING = V7X_VMEM_BYTES - 8 * 1024 * 1024
LANES = 128
SUBLANES = 8


def _pick(n, cands):
    for c in cands:
        if n % c == 0:
            return c
    return n


def _nbytes(shape, dtype):
    n = 1
    for d in shape:
        if d is not None:
            n *= d
    return n * jnp.dtype(dtype).itemsize


def _params(sem, est_bytes):
    limit = int(min(VMEM_CEILING, max(32 * 1024 * 1024, 2 * est_bytes + (8 << 20))))
    return pltpu.CompilerParams(dimension_semantics=sem, vmem_limit_bytes=limit)


def _mm(name, a, b, a_bs, a_im, b_bs, b_im, out_shape, out_dtype, o_bs, o_im, grid, dims,
        res=None, r_bs=None, r_im=None, after=None):
    nk = grid[-1]
    nax = len(grid)
    has_res = res is not None
    has_after = after is not None
    o_tile = tuple(d for d in o_bs if d is not None)

    def body(*refs):
        a_ref, b_ref = refs[:2]
        r_ref = refs[2] if has_res else None
        n_in = 2 + has_res + has_after
        o_ref = refs[n_in]
        rest = refs[n_in + 1:]

        def prod():
            return lax.dot_general(a_ref[...].astype(BF16), b_ref[...].astype(BF16), (dims, ((), ())),
                                   preferred_element_type=F32)

        def finish(r):
            if r_ref is not None:
                r = r + r_ref[...].astype(F32)
            o_ref[...] = r.astype(o_ref.dtype)

        if nk == 1:
            finish(prod())
        else:
            acc = rest[0]
            k = pl.program_id(nax - 1)

            @pl.when(k == 0)
            def _():
                acc[...] = prod()

            @pl.when(k > 0)
            def _():
                acc[...] += prod()

            @pl.when(k == nk - 1)
            def _():
                finish(acc[...])

    in_specs = [pl.BlockSpec(a_bs, a_im), pl.BlockSpec(b_bs, b_im)]
    args = [a, b]
    est = _nbytes(a_bs, a.dtype) + _nbytes(b_bs, b.dtype) + _nbytes(o_bs, out_dtype) + 2 * _nbytes(o_tile, F32)
    if has_res:
        in_specs.append(pl.BlockSpec(r_bs, r_im))
        args.append(res)
        est += _nbytes(r_bs, res.dtype)
    if has_after:
        in_specs.append(pl.BlockSpec(memory_space=pl.ANY))
        args.append(after)
    scratch = [] if nk == 1 else [pltpu.VMEM(o_tile, F32)]
    sem = ("parallel",) * (nax - 1) + ("arbitrary",)
    return pl.pallas_call(
        body, name=name, grid=grid, in_specs=in_specs, out_specs=pl.BlockSpec(o_bs, o_im),
        out_shape=pltpu.HBM(out_shape, out_dtype), scratch_shapes=scratch,
        compiler_params=_params(sem, est))(*args)


TM_CANDS = (1024, 512, 256, 128, 64, 32, 16, 8)
TM_WIDE_CANDS = (2048,) + TM_CANDS
TK_CANDS = (1024, 512, 256, 128)
TN_CANDS = (1024, 512, 256, 128)


def mm_cols(name, a, wg, out_dtype, fold, after=None):
    m, k = a.shape
    nj, _, n = wg.shape
    tm, tk = _pick(m, TM_WIDE_CANDS), _pick(k, TK_CANDS)
    grid = (nj, m // tm, k // tk)
    if fold:
        shape, o_bs, o_im = (m, nj * n), (tm, n), (lambda j, i, kk: (i, j))
    else:
        shape, o_bs, o_im = (nj, m, n), (None, tm, n), (lambda j, i, kk: (j, i, 0))
    return _mm(name, a, wg, (tm, tk), lambda j, i, kk: (i, kk), (None, tk, n), lambda j, i, kk: (j, kk, 0),
               shape, out_dtype, o_bs, o_im, grid, ((1,), (0,)), after=after)


def mm_plain(name, a, w, out_dtype, res=None):
    m, k = a.shape
    n = w.shape[1]
    tm, tk, tn = _pick(m, TM_CANDS), _pick(k, TK_CANDS), _pick(n, TN_CANDS)
    grid = (n // tn, m // tm, k // tk)
    return _mm(name, a, w, (tm, tk), lambda j, i, kk: (i, kk), (tk, tn), lambda j, i, kk: (kk, j),
               (m, n), out_dtype, (tm, tn), lambda j, i, kk: (i, j), grid, ((1,), (0,)),
               res, (tm, tn), lambda j, i, kk: (i, j))


def mm_jsum(name, aj, wg, out_dtype, res=None):
    nj, m, ka = aj.shape
    n = wg.shape[2]
    tm, tn = _pick(m, TM_CANDS), _pick(n, TN_CANDS)
    grid = (m // tm, n // tn, nj)
    return _mm(name, aj, wg, (None, tm, ka), lambda i, jn, j: (j, i, 0), (None, ka, tn), lambda i, jn, j: (j, 0, jn),
               (m, n), out_dtype, (tm, tn), lambda i, jn, j: (i, jn), grid, ((1,), (0,)),
               res, (tm, tn), lambda i, jn, j: (i, jn))


def mm_tn_cols(name, a, g, nj, n, out_dtype, folded, after=None):
    s, kw = a.shape
    ts, tkw = _pick(s, TK_CANDS), _pick(kw, TM_CANDS)
    grid = (nj, kw // tkw, s // ts)
    if folded:
        g_bs, g_im = (ts, n), (lambda j, i, ss: (ss, j))
    else:
        g_bs, g_im = (None, ts, n), (lambda j, i, ss: (j, ss, 0))
    return _mm(name, a, g, (ts, tkw), lambda j, i, ss: (ss, i), g_bs, g_im,
               (nj, kw, n), out_dtype, (None, tkw, n), lambda j, i, ss: (j, i, 0), grid, ((0,), (0,)), after=after)


def mm_tn_plain(name, a, g, out_dtype):
    s, kw = a.shape
    n = g.shape[1]
    ts, tkw, tn = _pick(s, TK_CANDS), _pick(kw, TM_CANDS), _pick(n, TN_CANDS)
    grid = (kw // tkw, n // tn, s // ts)
    return _mm(name, a, g, (ts, tkw), lambda i, jn, ss: (ss, i), (ts, tn), lambda i, jn, ss: (ss, jn),
               (kw, n), out_dtype, (tkw, tn), lambda i, jn, ss: (i, jn), grid, ((0,), (0,)))


def mm_tn_j(name, aj, g, out_dtype):
    nj, s, ka = aj.shape
    n = g.shape[1]
    ts, tn = _pick(s, TK_CANDS), _pick(n, TN_CANDS)
    grid = (nj, n // tn, s // ts)
    return _mm(name, aj, g, (None, ts, ka), lambda j, jn, ss: (j, ss, 0), (ts, tn), lambda j, jn, ss: (ss, jn),
               (nj, ka, n), out_dtype, (None, ka, tn), lambda j, jn, ss: (j, 0, jn), grid, ((0,), (0,)))


def mm_nt_plain(name, g, w, out_dtype):
    m, n = g.shape
    k = w.shape[0]
    tm, tn, tkk = _pick(m, TM_CANDS), _pick(n, TK_CANDS), _pick(k, TN_CANDS)
    grid = (k // tkk, m // tm, n // tn)
    return _mm(name, g, w, (tm, tn), lambda kk, i, jn: (i, jn), (tkk, tn), lambda kk, i, jn: (kk, jn),
               (m, k), out_dtype, (tm, tkk), lambda kk, i, jn: (i, kk), grid, ((1,), (1,)))


def mm_nt_j(name, g, wg, out_dtype):
    m, n = g.shape
    nj, ka, _ = wg.shape
    tm, tn = _pick(m, TM_CANDS), _pick(n, TK_CANDS)
    grid = (nj, m // tm, n // tn)
    return _mm(name, g, wg, (tm, tn), lambda j, i, jn: (i, jn), (None, ka, tn), lambda j, i, jn: (j, 0, jn),
               (nj, m, ka), out_dtype, (None, tm, ka), lambda j, i, jn: (j, i, 0), grid, ((1,), (1,)))


def mm_nt_jsum(name, g, wg, out_dtype, folded, res=None, after=None):
    nj, k, n = wg.shape
    m = g.shape[0] if folded else g.shape[1]
    tm, tkk = _pick(m, TM_CANDS if res is not None else TM_WIDE_CANDS), _pick(k, TN_CANDS)
    grid = (m // tm, k // tkk, nj)
    if folded:
        g_bs, g_im = (tm, n), (lambda i, kk, j: (i, j))
    else:
        g_bs, g_im = (None, tm, n), (lambda i, kk, j: (j, i, 0))
    return _mm(name, g, wg, g_bs, g_im, (None, tkk, n), lambda i, kk, j: (j, kk, 0),
               (m, k), out_dtype, (tm, tkk), lambda i, kk, j: (i, kk), grid, ((1,), (1,)),
               res, (tm, tkk), lambda i, kk, j: (i, kk), after=after)


ROW_TILE_CANDS = (256, 128, 64, 32, 16, 8)


def _rstd(x):
    return lax.rsqrt(jnp.mean(x * x, axis=-1, keepdims=True) + RMS_EPS)


def _sigmoid(t):
    return 1.0 / (1.0 + jnp.exp(-t))


def rms_fwd(name, x, gain):
    s, d = x.shape
    ts = _pick(s, ROW_TILE_CANDS)

    def body(x_ref, g_ref, h_ref):
        xv = x_ref[...]
        h_ref[...] = ((xv * _rstd(xv)) * g_ref[...]).astype(h_ref.dtype)

    return pl.pallas_call(
        body, name=name, grid=(s // ts,),
        in_specs=[pl.BlockSpec((ts, d), lambda i: (i, 0)), pl.BlockSpec((1, d), lambda i: (0, 0))],
        out_specs=pl.BlockSpec((ts, d), lambda i: (i, 0)),
        out_shape=pltpu.HBM((s, d), BF16),
        compiler_params=_params(("parallel",), 3 * ts * d * 4))(x, gain)


def rms_bwd(name, x, gain, dh, dres):
    s, d = x.shape
    ts = _pick(s, ROW_TILE_CANDS)

    def body(x_ref, g_ref, dh_ref, dr_ref, dx_ref, dg_ref):
        xv = x_ref[...]
        r = _rstd(xv)
        xhat = xv * r
        dhv = dh_ref[...].astype(F32)
        dxhat = dhv * g_ref[...]
        dx_ref[...] = dr_ref[...] + r * (dxhat - xhat * jnp.mean(dxhat * xhat, axis=-1, keepdims=True))
        part = jnp.sum(dhv * xhat, axis=0, keepdims=True)

        @pl.when(pl.program_id(0) == 0)
        def _():
            dg_ref[...] = part

        @pl.when(pl.program_id(0) > 0)
        def _():
            dg_ref[...] += part

    row = pl.BlockSpec((ts, d), lambda i: (i, 0))
    vec = pl.BlockSpec((1, d), lambda i: (0, 0))
    return pl.pallas_call(
        body, name=name, grid=(s // ts,), in_specs=[row, vec, row, row], out_specs=[row, vec],
        out_shape=[pltpu.HBM((s, d), F32), jax.ShapeDtypeStruct((1, d), F32)],
        compiler_params=_params(("arbitrary",), 6 * ts * d * 4))(x, gain, dh, dres)


def gate_merge_fwd(name, proj, ta, tb, d):
    s = proj.shape[0]
    ts = _pick(s, ROW_TILE_CANDS)
    cb = COL_GATES // d

    def body(ga_ref, gb_ref, ta_ref, tb_ref, o_ref):
        o_ref[...] = (_sigmoid(ga_ref[...]) * ta_ref[...] + _sigmoid(gb_ref[...]) * tb_ref[...]).astype(o_ref.dtype)

    row = pl.BlockSpec((ts, d), lambda i: (i, 0))
    return pl.pallas_call(
        body, name=name, grid=(s // ts,),
        in_specs=[pl.BlockSpec((ts, d), lambda i: (i, cb)), pl.BlockSpec((ts, d), lambda i: (i, cb + 1)), row, row],
        out_specs=row, out_shape=pltpu.HBM((s, d), BF16),
        compiler_params=_params(("parallel",), 5 * ts * d * 4))(proj, proj, ta, tb)


def gate_merge_bwd(name, dmerged, proj, ta, tb, d):
    s = proj.shape[0]
    ts = _pick(s, ROW_TILE_CANDS)
    cb = COL_GATES // d

    def body(dm_ref, ga_ref, gb_ref, ta_ref, tb_ref, dta_ref, dtb_ref, dga_ref, dgb_ref):
        dm = dm_ref[...]
        sa = _sigmoid(ga_ref[...])
        sb = _sigmoid(gb_ref[...])
        dta_ref[...] = (dm * sa).astype(dta_ref.dtype)
        dtb_ref[...] = (dm * sb).astype(dtb_ref.dtype)
        dga_ref[...] = (dm * ta_ref[...] * (sa * (1.0 - sa))).astype(dga_ref.dtype)
        dgb_ref[...] = (dm * tb_ref[...] * (sb * (1.0 - sb))).astype(dgb_ref.dtype)

    row = pl.BlockSpec((ts, d), lambda i: (i, 0))
    out = pltpu.HBM((s, d), BF16)
    return pl.pallas_call(
        body, name=name, grid=(s // ts,),
        in_specs=[row, pl.BlockSpec((ts, d), lambda i: (i, cb)), pl.BlockSpec((ts, d), lambda i: (i, cb + 1)), row, row],
        out_specs=[row, row, row, row], out_shape=[out, out, out, out],
        compiler_params=_params(("parallel",), 8 * ts * d * 4))(dmerged, proj, proj, ta, tb)


def tail_fwd_bwd(name, x2, lp, pp, gain, target):
    s, d = x2.shape
    ts = _pick(s, ROW_TILE_CANDS)

    def body(x2_ref, lp_ref, pp_ref, g_ref, t_ref, loss_ref, dx3_ref, dlp_ref, dpp_ref, dg_ref):
        gp = _sigmoid(lp_ref[...])
        ppv = pp_ref[...]
        x3 = x2_ref[...] + gp * ppv
        r = _rstd(x3)
        xhat = x3 * r
        gv = g_ref[...]
        err = xhat * gv - t_ref[...]
        loss = jnp.sum(err * err) * (0.5 / d)
        dy = err * (1.0 / d)
        dxhat = dy * gv
        dx3 = r * (dxhat - xhat * jnp.mean(dxhat * xhat, axis=-1, keepdims=True))
        dx3_ref[...] = dx3
        dlp_ref[...] = (dx3 * ppv * (gp * (1.0 - gp))).astype(dlp_ref.dtype)
        dpp_ref[...] = (dx3 * gp).astype(dpp_ref.dtype)
        part = jnp.sum(dy * xhat, axis=0, keepdims=True)
        lossv = jnp.full((1, LANES), loss, F32)

        @pl.when(pl.program_id(0) == 0)
        def _():
            dg_ref[...] = part
            loss_ref[...] = lossv

        @pl.when(pl.program_id(0) > 0)
        def _():
            dg_ref[...] += part
            loss_ref[...] += lossv

    row = pl.BlockSpec((ts, d), lambda i: (i, 0))
    vec = pl.BlockSpec((1, d), lambda i: (0, 0))
    return pl.pallas_call(
        body, name=name, grid=(s // ts,), in_specs=[row, row, row, vec, row],
        out_specs=[pl.BlockSpec((1, LANES), lambda i: (0, 0)), row, row, row, vec],
        out_shape=[jax.ShapeDtypeStruct((1, LANES), F32), pltpu.HBM((s, d), F32),
                   pltpu.HBM((s, d), BF16), pltpu.HBM((s, d), BF16),
                   jax.ShapeDtypeStruct((1, d), F32)],
        compiler_params=_params(("arbitrary",), 9 * ts * d * 4))(x2, lp, pp, gain, target)


HALO = SUBLANES


def _shift_rows(cur, prev_row, next_row):
    ts = cur.shape[0]
    rid = lax.broadcasted_iota(jnp.int32, cur.shape, 0)
    down = jnp.where(rid == 0, prev_row, pltpu.roll(cur, 1, 0))
    up = jnp.where(rid == ts - 1, next_row, pltpu.roll(cur, ts - 1, 0))
    return down, up


def _halo_specs(ts, s, nf):
    nb = ts // HALO
    last = s // HALO - 1
    cur = pl.BlockSpec((None, ts, nf), lambda j, i: (j, i, 0))
    prev = pl.BlockSpec((None, HALO, nf), lambda j, i: (j, jnp.maximum(i * nb - 1, 0), 0))
    nxt = pl.BlockSpec((None, HALO, nf), lambda j, i: (j, jnp.minimum((i + 1) * nb, last), 0))
    return cur, prev, nxt


def _halo_rows(prev_ref, next_ref, n_tiles):
    i = pl.program_id(1)
    prev_row = jnp.where(i == 0, 0.0, prev_ref[HALO - 1:HALO, :].astype(F32))
    next_row = jnp.where(i == n_tiles - 1, 0.0, next_ref[0:1, :].astype(F32))
    return prev_row, next_row


def _gelu(g):
    t = jnp.tanh(GELU_C * (g + GELU_A * (g * g * g)))
    return 0.5 * g * (1.0 + t), t


def _conv(cur, down, up, cw_ref, cb_ref):
    return down * cw_ref[0:1, :] + cur * cw_ref[1:2, :] + up * cw_ref[2:3, :] + cb_ref[...]


def ffn_mid_fwd(name, gpre, u, cw, cb):
    nj, s, nf = gpre.shape
    ts = _pick(s, (512, 256, 128, 64, 32, 16, 8))
    n_tiles = s // ts
    cur, prev, nxt = _halo_specs(ts, s, nf)

    def body(g_ref, gp_ref, gn_ref, u_ref, cw_ref, cb_ref, z_ref):
        gv = g_ref[...]
        down, up = _shift_rows(gv, *_halo_rows(gp_ref, gn_ref, n_tiles))
        act, _ = _gelu(_conv(gv, down, up, cw_ref, cb_ref))
        z_ref[...] = (act * u_ref[...]).astype(z_ref.dtype)

    return pl.pallas_call(
        body, name=name, grid=(nj, n_tiles),
        in_specs=[cur, prev, nxt, cur, pl.BlockSpec((None, SUBLANES, nf), lambda j, i: (j, 0, 0)),
                  pl.BlockSpec((None, 1, nf), lambda j, i: (j, 0, 0))],
        out_specs=cur, out_shape=pltpu.HBM((nj, s, nf), BF16),
        compiler_params=_params(("parallel", "parallel"), 8 * ts * nf * 4))(gpre, gpre, gpre, u, cw, cb)


def ffn_mid_bwd1(name, gpre, u, dz, cw, cb):
    nj, s, nf = gpre.shape
    ts = _pick(s, (512, 256, 128, 64, 32, 16, 8))
    n_tiles = s // ts
    cur, prev, nxt = _halo_specs(ts, s, nf)

    def body(g_ref, gp_ref, gn_ref, u_ref, dz_ref, cw_ref, cb_ref, dg_ref, du_ref, dcw_ref):
        gv = g_ref[...]
        down, up = _shift_rows(gv, *_halo_rows(gp_ref, gn_ref, n_tiles))
        gc = _conv(gv, down, up, cw_ref, cb_ref)
        act, t = _gelu(gc)
        dzv = dz_ref[...].astype(F32)
        du_ref[...] = (dzv * act).astype(du_ref.dtype)
        dact = 0.5 * (1.0 + t) + 0.5 * gc * (1.0 - t * t) * (GELU_C * (1.0 + 3.0 * GELU_A * (gc * gc)))
        dg = dzv * u_ref[...] * dact
        dg_ref[...] = dg
        rows = [jnp.sum(dg * down, axis=0, keepdims=True), jnp.sum(dg * gv, axis=0, keepdims=True),
                jnp.sum(dg * up, axis=0, keepdims=True), jnp.sum(dg, axis=0, keepdims=True)]
        part = jnp.concatenate(rows + [jnp.zeros((SUBLANES - len(rows), nf), F32)], axis=0)

        @pl.when(pl.program_id(1) == 0)
        def _():
            dcw_ref[...] = part

        @pl.when(pl.program_id(1) > 0)
        def _():
            dcw_ref[...] += part

    small = pl.BlockSpec((None, SUBLANES, nf), lambda j, i: (j, 0, 0))
    return pl.pallas_call(
        body, name=name, grid=(nj, n_tiles),
        in_specs=[cur, prev, nxt, cur, cur, small, pl.BlockSpec((None, 1, nf), lambda j, i: (j, 0, 0))],
        out_specs=[cur, cur, small],
        out_shape=[pltpu.HBM((nj, s, nf), F32), pltpu.HBM((nj, s, nf), BF16),
                   jax.ShapeDtypeStruct((nj, SUBLANES, nf), F32)],
        compiler_params=_params(("parallel", "arbitrary"), 12 * ts * nf * 4))(gpre, gpre, gpre, u, dz, cw, cb)


def ffn_mid_bwd2(name, dg, cw):
    nj, s, nf = dg.shape
    ts = _pick(s, (512, 256, 128, 64, 32, 16, 8))
    n_tiles = s // ts
    cur, prev, nxt = _halo_specs(ts, s, nf)

    def body(g_ref, gp_ref, gn_ref, cw_ref, o_ref):
        gv = g_ref[...]
        down, up = _shift_rows(gv, *_halo_rows(gp_ref, gn_ref, n_tiles))
        o_ref[...] = (up * cw_ref[0:1, :] + gv * cw_ref[1:2, :] + down * cw_ref[2:3, :]).astype(o_ref.dtype)

    return pl.pallas_call(
        body, name=name, grid=(nj, n_tiles),
        in_specs=[cur, prev, nxt, pl.BlockSpec((None, SUBLANES, nf), lambda j, i: (j, 0, 0))],
        out_specs=cur, out_shape=pltpu.HBM((nj, s, nf), BF16),
        compiler_params=_params(("parallel", "parallel"), 6 * ts * nf * 4))(dg, dg, dg, cw)


def _t5_bucket(rel):
    half = N_BUCKETS // 2
    max_exact = half // 2
    n = jnp.abs(rel)
    side = jnp.where(rel > 0, half, 0)
    nf = jnp.maximum(n, 1).astype(F32)
    large = max_exact + (jnp.log(nf / max_exact) / math.log(MAX_DISTANCE / max_exact)
                         * (half - max_exact)).astype(jnp.int32)
    large = jnp.minimum(large, half - 1)
    return side + jnp.where(n < max_exact, n, large)


def bucket_tile(rows, half, dil):
    rel = (jnp.arange(rows + 2 * half)[None, :] - half) - jnp.arange(rows)[:, None]
    return _t5_bucket(rel * dil).astype(jnp.int32)


def bias_build(name, table_t, bucket, h0, nh, half):
    blk, kw = bucket.shape

    def body(t_ref, b_ref, o_ref):
        h = pl.program_id(0)
        bv = b_ref[...]
        acc = jnp.zeros((blk, kw), F32)
        for b in range(N_BUCKETS):
            acc = jnp.where(bv == b, t_ref[h0 + h, b], acc)
        qi = lax.broadcasted_iota(jnp.int32, (blk, kw), 0)
        ci = lax.broadcasted_iota(jnp.int32, (blk, kw), 1)
        o_ref[...] = jnp.where(jnp.abs(ci - half - qi) <= half, acc, NEG_INF)

    return pl.pallas_call(
        body, name=name, grid=(nh,),
        in_specs=[pl.BlockSpec(memory_space=pltpu.SMEM), pl.BlockSpec((blk, kw), lambda h: (0, 0))],
        out_specs=pl.BlockSpec((None, blk, kw), lambda h: (h, 0, 0)),
        out_shape=jax.ShapeDtypeStruct((nh, blk, kw), F32),
        compiler_params=_params(("parallel",), 4 * blk * kw * 4))(table_t, bucket)


def table_grad(name, dbias, bucket):
    nh, blk, kw = dbias.shape

    def body(d_ref, b_ref, o_ref):
        bv = b_ref[...]
        dv = d_ref[...]
        lane = lax.broadcasted_iota(jnp.int32, (SUBLANES, LANES), 1)
        acc = jnp.zeros((SUBLANES, LANES), F32)
        for b in range(N_BUCKETS):
            acc = jnp.where(lane == b, jnp.sum(jnp.where(bv == b, dv, 0.0)), acc)
        o_ref[...] = acc

    return pl.pallas_call(
        body, name=name, grid=(nh,),
        in_specs=[pl.BlockSpec((None, blk, kw), lambda h: (h, 0, 0)), pl.BlockSpec((blk, kw), lambda h: (0, 0))],
        out_specs=pl.BlockSpec((None, SUBLANES, LANES), lambda h: (h, 0, 0)),
        out_shape=jax.ShapeDtypeStruct((nh, SUBLANES, LANES), F32),
        compiler_params=_params(("parallel",), 4 * blk * kw * 4))(dbias, bucket)


class _Band:
    def __init__(self, s, half, q_rows, n_chains, dil):
        self.s, self.half, self.dil, self.n_chains = s, half, dil, n_chains
        self.seg = s // dil
        self.q_rows = min(q_rows, self.seg)
        self.win = self.q_rows + 2 * half
        self.pad = self.seg + 2 * half
        self.nsb = self.seg // self.q_rows
        self.n_items = dil * self.nsb
        assert self.n_items % n_chains == 0 and self.seg % self.q_rows == 0
        self.staged = dil > 1

    def rows_of(self, r):
        return pl.ds(r, self.seg, stride=self.dil) if self.dil > 1 else slice(None)

    def stage_kv(self, dst, src_ref):
        zeros = jnp.zeros((self.half, HEAD_DIM), dst.dtype)
        for r in range(self.dil):
            base = r * self.pad
            dst[base:base + self.half, :] = zeros
            dst[base + self.half + self.seg:base + self.pad, :] = zeros
            dst[base + self.half:base + self.half + self.seg, :] = src_ref[self.rows_of(r), :].astype(dst.dtype)

    def stage(self, dst, src_ref):
        for r in range(self.dil):
            dst[r * self.seg:(r + 1) * self.seg, :] = src_ref[self.rows_of(r), :].astype(dst.dtype)

    def unstage(self, dst_ref, src, add=False):
        for r in range(self.dil):
            val = src[r * self.seg:(r + 1) * self.seg, :].astype(dst_ref.dtype)
            if add:
                val = val + dst_ref[self.rows_of(r), :]
            dst_ref[self.rows_of(r), :] = val

    def offsets(self, item):
        r, sb = item // self.nsb, item % self.nsb
        qoff = pl.multiple_of(r * self.seg + sb * self.q_rows, self.q_rows)
        koff = pl.multiple_of(r * self.pad + sb * self.q_rows, B_BLOCK)
        kpos = sb * self.q_rows - self.half + lax.broadcasted_iota(jnp.int32, (1, self.win), 1)
        edge = jnp.where((kpos >= 0) & (kpos < self.seg), 0.0, NEG_INF)
        return qoff, koff, edge


def band_attn_fwd(name, proj, bias, sink, *, half, q_rows, n_chains, dil, nh, group, cq, ck, cv):
    s, w = proj.shape
    g = _Band(s, half, q_rows, n_chains, dil)
    has_sink = sink is not None

    def body(*refs):
        q_ref, k_ref, v_ref, b_ref = refs[:4]
        s_ref = refs[4] if has_sink else None
        o_ref, l_ref, ks, vs = refs[4 + has_sink:8 + has_sink]
        qs, os_, ls = refs[8 + has_sink:] if g.staged else (None, o_ref, l_ref)
        g.stage_kv(ks, k_ref)
        g.stage_kv(vs, v_ref)
        if g.staged:
            g.stage(qs, q_ref)
        bias_v = b_ref[...]
        sk = s_ref[pl.program_id(0)] if has_sink else None

        def chain(item):
            qoff, koff, edge = g.offsets(item)
            rows = pl.ds(qoff, g.q_rows)
            qv = qs[rows, :] if g.staged else q_ref[rows, :].astype(BF16)
            kw_ = ks[pl.ds(koff, g.win), :]
            vw_ = vs[pl.ds(koff, g.win), :]
            sc = lax.dot_general(qv, kw_, (((1,), (1,)), ((), ())), preferred_element_type=F32) * ATTN_SCALE
            sc = sc + bias_v + edge
            m = jnp.max(sc, axis=-1, keepdims=True)
            if has_sink:
                m = jnp.maximum(m, sk)
            p = jnp.exp(sc - m)
            den = jnp.sum(p, axis=-1, keepdims=True)
            if has_sink:
                den = den + jnp.exp(sk - m)
            out = lax.dot_general(p.astype(BF16), vw_, (((1,), (0,)), ((), ())), preferred_element_type=F32)
            return rows, out / den, jnp.broadcast_to(m + jnp.log(den), (g.q_rows, HEAD_DIM))

        def step(i, carry):
            for rows, out, lse in [chain(i * n_chains + u) for u in range(n_chains)]:
                os_[rows, :] = out
                ls[rows, :] = lse
            return carry

        lax.fori_loop(0, g.n_items // n_chains, step, 0)
        if g.staged:
            g.unstage(o_ref, os_)
            g.unstage(l_ref, ls)

    def col(c0, per):
        return pl.BlockSpec((s, HEAD_DIM), lambda h: (0, c0 // LANES + h // per))

    in_specs = [col(cq, 1), col(ck, group), col(cv, group),
                pl.BlockSpec((None, g.q_rows, g.win), lambda h: (h, 0, 0))]
    args = [proj, proj, proj, bias]
    if has_sink:
        in_specs.append(pl.BlockSpec(memory_space=pltpu.SMEM))
        args.append(sink)
    shape = pltpu.HBM((s, nh * HEAD_DIM), F32)
    scratch = [pltpu.VMEM((dil * g.pad, HEAD_DIM), BF16), pltpu.VMEM((dil * g.pad, HEAD_DIM), BF16)]
    if g.staged:
        scratch += [pltpu.VMEM((s, HEAD_DIM), BF16), pltpu.VMEM((s, HEAD_DIM), F32), pltpu.VMEM((s, HEAD_DIM), F32)]
    return pl.pallas_call(
        body, name=name, grid=(nh,), in_specs=in_specs, out_specs=[col(0, 1), col(0, 1)], out_shape=[shape, shape],
        scratch_shapes=scratch, compiler_params=_params(("parallel",), 16 * s * HEAD_DIM * 4))(*args)


def band_attn_bwd(name, proj, bias, sink, dout, out, lse, dlse, *, half, q_rows, n_chains, dil, nh, group, cq, ck, cv):
    s, w = proj.shape
    g = _Band(s, half, q_rows, n_chains, dil)
    nkv = nh // group
    has_sink = sink is not None
    has_dl = dlse is not None
    n_in = 7 + int(has_sink) + int(has_dl)
    n_out = 4 + int(has_sink)

    def body(*refs):
        ins, outs, scr = refs[:n_in], refs[n_in:n_in + n_out], refs[n_in + n_out:]
        q_ref, k_ref, v_ref, b_ref, do_ref, o_ref, l_ref = ins[:7]
        s_ref = ins[7] if has_sink else None
        dl_ref = ins[n_in - 1] if has_dl else None
        dq_ref, dk_ref, dv_ref, db_ref = outs[:4]
        ks, vs, dks, dvs = scr[:4]
        scr = list(scr[4:])
        dsa = scr.pop(0) if has_sink else None
        if g.staged:
            qs, dos, os_, ls, dqs = scr[:5]
            dls = scr[5] if has_dl else None
            g.stage(qs, q_ref)
            g.stage(dos, do_ref)
            g.stage(os_, o_ref)
            g.stage(ls, l_ref)
            if has_dl:
                g.stage(dls, dl_ref)
        else:
            qs, dos, os_, ls, dqs, dls = None, do_ref, o_ref, l_ref, dq_ref, dl_ref
        h = pl.program_id(0)
        g.stage_kv(ks, k_ref)
        g.stage_kv(vs, v_ref)
        dks[...] = jnp.zeros_like(dks)
        dvs[...] = jnp.zeros_like(dvs)
        db_ref[...] = jnp.zeros_like(db_ref)
        bias_v = b_ref[...]
        if has_sink:
            sk = s_ref[h]
            dsa[...] = jnp.zeros_like(dsa)

        def chain(item):
            qoff, koff, edge = g.offsets(item)
            rows = pl.ds(qoff, g.q_rows)
            win = pl.ds(koff, g.win)
            qv = qs[rows, :] if g.staged else q_ref[rows, :].astype(BF16)
            kw_ = ks[win, :]
            vw_ = vs[win, :]
            sc = lax.dot_general(qv, kw_, (((1,), (1,)), ((), ())), preferred_element_type=F32) * ATTN_SCALE
            lv = ls[rows, :][:, 0:1]
            p = jnp.exp(sc + bias_v + edge - lv)
            dov = dos[rows, :]
            delta = jnp.sum(dov * os_[rows, :], axis=-1, keepdims=True)
            dob = dov.astype(BF16)
            dp = lax.dot_general(dob, vw_, (((1,), (1,)), ((), ())), preferred_element_type=F32)
            t = dp - delta
            if has_dl:
                t = t + dls[rows, :][:, 0:1]
            ds = p * t
            dsb = (ds * ATTN_SCALE).astype(BF16)
            dq = lax.dot_general(dsb, kw_, (((1,), (0,)), ((), ())), preferred_element_type=F32)
            dkc = lax.dot_general(dsb, qv, (((0,), (0,)), ((), ())), preferred_element_type=F32)
            dvc = lax.dot_general(p.astype(BF16), dob, (((0,), (0,)), ((), ())), preferred_element_type=F32)
            dsk = jnp.exp(sk - lv) * delta if has_sink else None
            return rows, win, dq, dkc, dvc, ds, dsk

        def step(i, carry):
            res = [chain(i * n_chains + u) for u in range(n_chains)]
            ds_sum = res[0][5]
            for rr in res[1:]:
                ds_sum = ds_sum + rr[5]
            db_ref[...] += ds_sum
            for rows, win, dq, dkc, dvc, ds, dsk in res:
                dqs[rows, :] = dq
                dks[win, :] += dkc
                dvs[win, :] += dvc
                if has_sink:
                    dsa[...] += dsk
            return carry

        lax.fori_loop(0, g.n_items // n_chains, step, 0)

        if g.staged:
            g.unstage(dq_ref, dqs)

        def emit_kv(add):
            for r in range(dil):
                lo = r * g.pad + half
                for dst_ref, src in ((dk_ref, dks), (dv_ref, dvs)):
                    val = src[lo:lo + g.seg, :]
                    if add:
                        val = val + dst_ref[g.rows_of(r), :]
                    dst_ref[g.rows_of(r), :] = val

        if group == 1:
            emit_kv(False)
        else:
            @pl.when(h % group == 0)
            def _():
                emit_kv(False)

            @pl.when(h % group != 0)
            def _():
                emit_kv(True)
        if has_sink:
            outs[4][...] = jnp.full((SUBLANES, LANES), -jnp.sum(dsa[...]), F32)

    def col(c0, per):
        return pl.BlockSpec((s, HEAD_DIM), lambda h: (0, c0 // LANES + h // per))

    b_spec = pl.BlockSpec((None, g.q_rows, g.win), lambda h: (h, 0, 0))
    in_specs = [col(cq, 1), col(ck, group), col(cv, group), b_spec, col(0, 1), col(0, 1), col(0, 1)]
    args = [proj, proj, proj, bias, dout, out, lse]
    if has_sink:
        in_specs.append(pl.BlockSpec(memory_space=pltpu.SMEM))
        args.append(sink)
    if has_dl:
        in_specs.append(col(0, 1))
        args.append(dlse)
    out_specs = [col(0, 1), col(0, group), col(0, group), b_spec]
    out_shape = [pltpu.HBM((s, nh * HEAD_DIM), F32), pltpu.HBM((s, nkv * HEAD_DIM), F32),
                 pltpu.HBM((s, nkv * HEAD_DIM), F32), jax.ShapeDtypeStruct((nh, g.q_rows, g.win), F32)]
    scratch = [pltpu.VMEM((dil * g.pad, HEAD_DIM), BF16), pltpu.VMEM((dil * g.pad, HEAD_DIM), BF16),
               pltpu.VMEM((dil * g.pad, HEAD_DIM), F32), pltpu.VMEM((dil * g.pad, HEAD_DIM), F32)]
    if has_sink:
        out_specs.append(pl.BlockSpec((None, SUBLANES, LANES), lambda h: (h, 0, 0)))
        out_shape.append(jax.ShapeDtypeStruct((nh, SUBLANES, LANES), F32))
        scratch.append(pltpu.VMEM((g.q_rows, 1), F32))
    if g.staged:
        scratch += [pltpu.VMEM((s, HEAD_DIM), BF16)] + [pltpu.VMEM((s, HEAD_DIM), F32)] * (4 + int(has_dl))
    res = pl.pallas_call(
        body, name=name, grid=(nh,), in_specs=in_specs, out_specs=out_specs, out_shape=out_shape,
        scratch_shapes=scratch, compiler_params=_params(("arbitrary",), 28 * s * HEAD_DIM * 4))(*args)
    return res[0], res[1], res[2], res[3], (res[4] if has_sink else None)


def dil_merge_fwd(name, outs, lses):
    s, w = outs[0].shape
    ts = _pick(s, ROW_TILE_CANDS)
    ng = len(outs)

    def body(*refs):
        o_refs, l_refs, y_ref = refs[:ng], refs[ng:2 * ng], refs[2 * ng]
        ls = [l[...] for l in l_refs]
        mx = ls[0]
        for l in ls[1:]:
            mx = jnp.maximum(mx, l)
        es = [jnp.exp(l - mx) for l in ls]
        tot = es[0]
        for e in es[1:]:
            tot = tot + e
        acc = (es[0] / tot) * o_refs[0][...]
        for e, o in zip(es[1:], o_refs[1:]):
            acc = acc + (e / tot) * o[...]
        y_ref[...] = acc.astype(y_ref.dtype)

    row = pl.BlockSpec((ts, w), lambda i: (i, 0))
    return pl.pallas_call(
        body, name=name, grid=(s // ts,), in_specs=[row] * (2 * ng), out_specs=row,
        out_shape=pltpu.HBM((s, w), BF16),
        compiler_params=_params(("parallel",), 10 * ts * w * 4))(*outs, *lses)


def dil_merge_bwd(name, dy, outs, lses):
    s, w = outs[0].shape
    ts = _pick(s, ROW_TILE_CANDS)
    ng = len(outs)
    nhead = w // HEAD_DIM

    def body(*refs):
        dy_ref = refs[0]
        o_refs, l_refs = refs[1:1 + ng], refs[1 + ng:1 + 2 * ng]
        do_refs, dl_refs = refs[1 + 2 * ng:1 + 3 * ng], refs[1 + 3 * ng:1 + 4 * ng]
        for hh in range(nhead):
            cols = slice(hh * HEAD_DIM, (hh + 1) * HEAD_DIM)
            dyv = dy_ref[:, cols]
            ls = [l[:, cols] for l in l_refs]
            mx = ls[0]
            for l in ls[1:]:
                mx = jnp.maximum(mx, l)
            es = [jnp.exp(l - mx) for l in ls]
            tot = es[0]
            for e in es[1:]:
                tot = tot + e
            alphas = [e / tot for e in es]
            dal = [jnp.broadcast_to(jnp.sum(dyv * o[:, cols], axis=-1, keepdims=True), dyv.shape) for o in o_refs]
            mean = alphas[0] * dal[0]
            for a, d in zip(alphas[1:], dal[1:]):
                mean = mean + a * d
            for g in range(ng):
                do_refs[g][:, cols] = alphas[g] * dyv
                dl_refs[g][:, cols] = alphas[g] * (dal[g] - mean)

    row = pl.BlockSpec((ts, w), lambda i: (i, 0))
    shape = pltpu.HBM((s, w), F32)
    res = pl.pallas_call(
        body, name=name, grid=(s // ts,), in_specs=[row] * (1 + 2 * ng), out_specs=[row] * (2 * ng),
        out_shape=[shape] * (2 * ng),
        compiler_params=_params(("parallel",), 16 * ts * w * 4))(dy, *outs, *lses)
    return res[:ng], res[ng:]


def _adamw(w, g, m, v):
    m = ADAM_B1 * m + (1.0 - ADAM_B1) * g
    v = ADAM_B2 * v + (1.0 - ADAM_B2) * (g * g)
    m_hat = m / (1.0 - ADAM_B1 ** ADAM_STEP)
    v_hat = v / (1.0 - ADAM_B2 ** ADAM_STEP)
    delta = -ADAM_LR * (m_hat / (jnp.sqrt(v_hat) + ADAM_EPS) + ADAM_WD * w)
    return delta, m, v


def _row_tile(r, c, budget=1 << 20):
    if r * c * 4 <= budget or r % SUBLANES:
        return r
    for t in (1024, 512, 256, 128, 64, 32, 16, 8):
        if r % t == 0 and t * c * 4 <= budget:
            return t
    return SUBLANES


def adam_small(name, g, w, m, v):
    def body(g_ref, w_ref, m_ref, v_ref, d_ref, nm_ref, nv_ref):
        d_ref[...], nm_ref[...], nv_ref[...] = _adamw(w_ref[...], g_ref[...], m_ref[...], v_ref[...])

    shape = jax.ShapeDtypeStruct(w.shape, F32)
    return pl.pallas_call(body, name=name, out_shape=[shape, shape, shape])(g, w, m, v)


def reduce_adam(name, mine, theirs, w, m, v):
    nq, r, c = mine.shape
    tr = _row_tile(r, c)

    def body(*refs):
        parts, (w_ref, m_ref, v_ref, g_ref, d_ref, nm_ref, nv_ref) = refs[:nq], refs[nq:]
        g = parts[0][...].astype(F32)
        for p_ref in parts[1:]:
            g = g + p_ref[...].astype(F32)
        g_ref[...] = g
        d_ref[...], nm_ref[...], nv_ref[...] = _adamw(w_ref[...], g, m_ref[...], v_ref[...])

    def slot(q):
        return pl.BlockSpec((None, tr, c), lambda i: (q, i, 0))

    row = pl.BlockSpec((tr, c), lambda i: (i, 0))
    shape = jax.ShapeDtypeStruct((r, c), F32)
    return pl.pallas_call(
        body, name=name, grid=(r // tr,), in_specs=[slot(q) for q in range(nq)] + [row, row, row],
        out_specs=[row] * 4, out_shape=[shape] * 4,
        compiler_params=_params(("parallel",), (nq * 2 + 7 * 4) * tr * c))(mine, *[theirs] * (nq - 1), w, m, v)


def _place():
    return lax.axis_index("x"), lax.axis_index("y"), lax.axis_index("c")


def _flip(pos, bits):
    return tuple((1 - p) if b else p for p, b in zip(pos, bits))


def _index(pos):
    return 4 * pos[0] + 2 * pos[1] + pos[2]


ANY = pl.BlockSpec(memory_space=pl.ANY)


HBM = pl.BlockSpec(memory_space=pltpu.HBM)
SEM = pl.BlockSpec(memory_space=pltpu.SEMAPHORE)
EFFECT = pltpu.SideEffectType.DATAFLOW_SIDE_EFFECTING
TO_SIBLING = (0, 0, 1)
TO_CHIPS = [(1, 0, 0), (0, 1, 0), (1, 1, 0)]


def _in_hbm(a):
    return pltpu.with_memory_space_constraint(a, pltpu.HBM)


def _token_value(token):
    return token[0, 0]


def _when(pred, fn):
    if pred is True:
        fn()
    elif pred is not False:
        pl.when(pred)(fn)


def _plan_copy(k, entry, ins, lnd, send_sems, recv_sems):
    a, src_a, sblk, lblk, to, send_if, recv_if = entry
    src = lnd[a] if src_a is None else ins[src_a]
    return pltpu.make_async_remote_copy(
        src_ref=src.at[sblk], dst_ref=lnd[a].at[lblk], send_sem=send_sems.at[k], recv_sem=recv_sems.at[k],
        device_id=to, device_id_type=MESH), send_if, recv_if


def split_start(name, srcs, lands, plan, after):
    ns, nl = len(srcs), len(lands)
    n_copies = len(plan((0, 0, 0)))

    def body(*refs):
        ins, lnd = refs[:ns], refs[ns:ns + nl]
        send_sems, recv_sems = refs[ns + nl + 1], refs[ns + nl + 2]
        token = refs[-1]
        for k, entry in enumerate(plan(_place())):
            cp, send_if, _ = _plan_copy(k, entry, ins, lnd, send_sems, recv_sems)
            _when(send_if, cp.start)
        token[...] = jnp.zeros_like(token)

    outs = pl.pallas_call(
        body, name=name,
        out_shape=(pltpu.SemaphoreType.DMA((n_copies,)), pltpu.SemaphoreType.DMA((n_copies,)),
                   *[pltpu.HBM(a.shape, a.dtype) for a in srcs], *[pltpu.HBM(a.shape, a.dtype) for a in lands],
                   jax.ShapeDtypeStruct((SUBLANES, LANES), F32)),
        in_specs=[HBM] * (ns + nl) + [ANY],
        out_specs=(SEM, SEM, *[HBM] * (ns + nl), pl.BlockSpec(memory_space=pltpu.VMEM)),
        input_output_aliases={i: 2 + i for i in range(ns + nl)},
        compiler_params=pltpu.CompilerParams(has_side_effects=EFFECT),
    )(*[_in_hbm(a) for a in srcs], *[_in_hbm(a) for a in lands], after)
    return outs[0], outs[1], list(outs[2:2 + ns]), list(outs[2 + ns:2 + ns + nl]), outs[-1]


def split_wait(name, send_sems, recv_sems, srcs, lands, plan, after):
    ns, nl = len(srcs), len(lands)

    def body(*refs):
        ins, lnd = refs[:ns], refs[ns:ns + nl]
        s_sems, r_sems = refs[ns + nl], refs[ns + nl + 1]
        for k, entry in enumerate(plan(_place())):
            cp, send_if, recv_if = _plan_copy(k, entry, ins, lnd, s_sems, r_sems)
            _when(send_if, cp.wait_send)
            _when(recv_if, cp.wait_recv)
        refs[-1][...] = jnp.zeros((SUBLANES, LANES), F32)

    outs = pl.pallas_call(
        body, name=name,
        out_shape=(*[pltpu.HBM(a.shape, a.dtype) for a in srcs], *[pltpu.HBM(a.shape, a.dtype) for a in lands],
                   jax.ShapeDtypeStruct((SUBLANES, LANES), F32)),
        in_specs=[HBM] * (ns + nl) + [SEM, SEM, ANY],
        out_specs=(*[HBM] * (ns + nl), pl.BlockSpec(memory_space=pltpu.VMEM)),
        input_output_aliases={i: i for i in range(ns + nl)},
        compiler_params=pltpu.CompilerParams(has_side_effects=EFFECT),
    )(*srcs, *lands, send_sems, recv_sems, after)
    return list(outs[:ns]), list(outs[ns:ns + nl]), outs[-1]


NORTH = 1


def ag_plan(n):
    def plan(me):
        x, y, c = me
        entries = []
        for a in range(n):
            for t in (NORTH, 1 - NORTH):
                blk = _index((x, y, t))
                for rel in TO_CHIPS:
                    entries.append((a, None, blk, blk, _flip((x, y, t), rel), c == NORTH, c == t))
        return entries
    return plan


def ag_pair(name, lands, after):
    n = len(lands)

    def body(*refs):
        lnd = refs[n + 1:2 * n + 1]
        token = refs[2 * n + 1]
        send_sems, recv_sems = refs[2 * n + 2:]
        token[...] = jnp.zeros_like(token)
        me = _place()
        sibling = _flip(me, TO_SIBLING)
        copies = []
        for a in range(n):
            mine, theirs = lnd[a].at[_index(me)], lnd[a].at[_index(sibling)]
            cp = pltpu.make_async_remote_copy(src_ref=mine, dst_ref=mine, send_sem=send_sems.at[a],
                                              recv_sem=recv_sems.at[a], device_id=sibling, device_id_type=MESH)
            cp.start()
            copies.append((cp, pltpu.make_async_remote_copy(
                src_ref=mine, dst_ref=theirs, send_sem=send_sems.at[a], recv_sem=recv_sems.at[a], device_id=sibling,
                device_id_type=MESH)))
        for cp, arrival in copies:
            arrival.wait_recv()
        for cp, arrival in copies:
            cp.wait_send()

    outs = pl.pallas_call(
        body, name=name, in_specs=[ANY] * (n + 1), out_specs=[ANY] * n + [pl.BlockSpec(memory_space=pltpu.VMEM)],
        out_shape=[jax.ShapeDtypeStruct(l.shape, l.dtype) for l in lands]
        + [jax.ShapeDtypeStruct((SUBLANES, LANES), F32)],
        input_output_aliases={a: a for a in range(n)},
        scratch_shapes=[pltpu.SemaphoreType.DMA((n,)), pltpu.SemaphoreType.DMA((n,))],
    )(*lands, after)
    return list(outs[:n]), outs[n]


def ag_start(name, lands, after):
    return split_start(name, [], lands, ag_plan(len(lands)), after)


def pass_plan(n):
    def plan(me):
        sibling = _flip(me, TO_SIBLING)
        return [(a, None, _index(_flip(me, rel)), _index(_flip(me, rel)), sibling, True, True)
                for a in range(n) for rel in TO_CHIPS]
    return plan


def ag_finish(name, lands):
    n = len(lands)

    def body(*refs):
        lnd = refs[n:2 * n]
        send_sems, recv_sems = refs[2 * n:]
        me = _place()
        sibling = _flip(me, TO_SIBLING)
        copies = []
        for a in range(n):
            for j, rel in enumerate(TO_CHIPS):
                blk = lnd[a].at[_index(_flip(me, rel))]
                there = lnd[a].at[_index(_flip(sibling, rel))]
                cp = pltpu.make_async_remote_copy(
                    src_ref=blk, dst_ref=blk, send_sem=send_sems.at[a * 3 + j], recv_sem=recv_sems.at[a * 3 + j],
                    device_id=sibling, device_id_type=MESH)
                cp.start()
                copies.append((cp, pltpu.make_async_remote_copy(
                    src_ref=blk, dst_ref=there, send_sem=send_sems.at[a * 3 + j], recv_sem=recv_sems.at[a * 3 + j],
                    device_id=sibling, device_id_type=MESH)))
        for cp, arrival in copies:
            arrival.wait_recv()
        for cp, arrival in copies:
            cp.wait_send()

    return pl.pallas_call(
        body, name=name, in_specs=[ANY] * n, out_specs=[ANY] * n,
        out_shape=[jax.ShapeDtypeStruct(l.shape, l.dtype) for l in lands],
        input_output_aliases={a: a for a in range(n)},
        scratch_shapes=[pltpu.SemaphoreType.DMA((3 * n,)), pltpu.SemaphoreType.DMA((3 * n,))],
    )(*lands)


REL = [(b >> 2 & 1, b >> 1 & 1, b & 1) for b in range(N_DEV)]


CHIP_REL = [(0, 0, 0)] + TO_CHIPS
N_CHIPS = len(CHIP_REL)


def rs_pair(name, parts):
    n = len(parts)

    def body(*refs):
        ins, got = refs[:n], refs[n:2 * n]
        send_sems, recv_sems = refs[2 * n:]
        me = _place()
        sibling = _flip(me, TO_SIBLING)
        remote = []
        for a in range(n):
            for q, rel in enumerate(CHIP_REL):
                k = a * N_CHIPS + q
                cp = pltpu.make_async_remote_copy(
                    src_ref=ins[a].at[_index(_flip(sibling, rel))], dst_ref=got[a].at[q], send_sem=send_sems.at[k],
                    recv_sem=recv_sems.at[k], device_id=sibling, device_id_type=MESH)
                cp.start()
                remote.append(cp)
        for cp in remote:
            cp.wait_recv()
        for cp in remote:
            cp.wait_send()

    shapes = [jax.ShapeDtypeStruct((N_CHIPS,) + tuple(p.shape[1:]), p.dtype) for p in parts]
    res = pl.pallas_call(
        body, name=name, in_specs=[ANY] * n, out_specs=[ANY] * n, out_shape=shapes,
        scratch_shapes=[pltpu.SemaphoreType.DMA((N_CHIPS * n,)), pltpu.SemaphoreType.DMA((N_CHIPS * n,))],
    )(*parts)
    return list(res)


def own_blocks():
    me = _place()
    return jnp.stack([_index(_flip(me, rel)) for rel in CHIP_REL]).astype(jnp.int32)


def pair_add(name, blocks, parts, got):
    nq, r, c = got.shape
    tr = _row_tile(r, c, budget=6 << 20)

    def body(blk_ref, a_ref, b_ref, o_ref):
        o_ref[...] = (a_ref[...].astype(F32) + b_ref[...].astype(F32)).astype(o_ref.dtype)

    spec = pl.BlockSpec((None, tr, c), lambda q, i, blk: (q, i, 0))
    return pl.pallas_call(
        body, name=name,
        grid_spec=pltpu.PrefetchScalarGridSpec(
            num_scalar_prefetch=1, grid=(nq, r // tr),
            in_specs=[pl.BlockSpec((None, tr, c), lambda q, i, blk: (blk[q], i, 0)), spec], out_specs=spec),
        out_shape=pltpu.HBM(got.shape, got.dtype),
        compiler_params=_params(("arbitrary", "arbitrary"), 6 * tr * c * 2))(blocks, parts, got)


def rs_pair_plan(n):
    def plan(me):
        sibling = _flip(me, TO_SIBLING)
        return [(a, a, _index(_flip(sibling, rel)), q, sibling, True, True)
                for a in range(n) for q, rel in enumerate(CHIP_REL)]
    return plan


def rs_plan(n):
    def plan(me):
        return [(a, a, q, q, _flip(me, CHIP_REL[q]), True, True) for a in range(n) for q in range(1, N_CHIPS)]
    return plan


def rs_start(name, sums, after):
    lands = [lax.empty(t.shape, t.dtype) for t in sums]
    return split_start(name, sums, lands, rs_plan(len(sums)), after)


def allreduce_small(name, pack, after):
    rows, lanes = pack.shape

    def body(x_ref, after_ref, o_ref, land, send_sems, recv_sems):
        me = _place()
        idx = _index(me)
        land[idx] = x_ref[...]
        copies = []
        for r in range(1, N_DEV):
            peer = _flip(me, REL[r])
            cp = pltpu.make_async_remote_copy(
                src_ref=x_ref, dst_ref=land.at[idx], send_sem=send_sems.at[r - 1], recv_sem=recv_sems.at[r - 1],
                device_id=peer, device_id_type=MESH)
            cp.start()
            copies.append(cp)
        for cp in copies:
            cp.wait_recv()
        for cp in copies:
            cp.wait_send()
        acc = land[0]
        for i in range(1, N_DEV):
            acc = acc + land[i]
        o_ref[...] = acc

    return pl.pallas_call(
        body, name=name, in_specs=[pl.BlockSpec(memory_space=pltpu.VMEM), ANY],
        out_specs=pl.BlockSpec(memory_space=pltpu.VMEM), out_shape=jax.ShapeDtypeStruct((rows, lanes), F32),
        scratch_shapes=[pltpu.VMEM((N_DEV, rows, lanes), F32), pltpu.SemaphoreType.DMA((7,)),
                        pltpu.SemaphoreType.DMA((7,))],
    )(pack, after)


def _pad_rows(a, rows):
    return jnp.pad(a, ((0, rows - a.shape[0]), (0, 0)))


def _as_tiles(vec):
    n = vec.shape[0]
    rows = -(-n // LANES)
    rows = -(-rows // SUBLANES) * SUBLANES
    return jnp.pad(vec, (0, rows * LANES - n)).reshape(rows, LANES)


def kernel(x, p, rel_bias_table, attn_norm, w_in, sink_a, w_branch_a, w_branch_b, w_out, ffn_norm, w_ffn_gate, w_ffn_up, conv_w, conv_b, w_ffn_down, ple_norm, w_ple_gate, w_ple_proj, final_norm, loss_target, m_rel_bias_table, m_attn_norm, m_w_in, m_sink_a, m_w_branch_a, m_w_branch_b, m_w_out, m_ffn_norm, m_w_ffn_gate, m_w_ffn_up, m_conv_w, m_conv_b, m_w_ffn_down, m_ple_norm, m_w_ple_gate, m_w_ple_proj, m_final_norm, v_rel_bias_table, v_attn_norm, v_w_in, v_sink_a, v_w_branch_a, v_w_branch_b, v_w_out, v_ffn_norm, v_w_ffn_gate, v_w_ffn_up, v_conv_w, v_conv_b, v_w_ffn_down, v_ple_norm, v_w_ple_gate, v_w_ple_proj, v_final_norm):
    xs = x[0]
    s, d = xs.shape
    ps = p[0, 0]
    target = loss_target[0]
    me = 4 * lax.axis_index("x") + 2 * lax.axis_index("y") + lax.axis_index("c")

    big = dict(w_in=w_in[0], w_branch_a=w_branch_a[0], w_branch_b=w_branch_b[0], w_out=w_out[0],
               w_ffn_gate=w_ffn_gate[0], w_ffn_up=w_ffn_up[0], w_ffn_down=w_ffn_down[0],
               w_ple_gate=w_ple_gate[0], w_ple_proj=w_ple_proj[0])
    big_m = dict(w_in=m_w_in[0], w_branch_a=m_w_branch_a[0], w_branch_b=m_w_branch_b[0], w_out=m_w_out[0],
                 w_ffn_gate=m_w_ffn_gate[0], w_ffn_up=m_w_ffn_up[0], w_ffn_down=m_w_ffn_down[0],
                 w_ple_gate=m_w_ple_gate[0], w_ple_proj=m_w_ple_proj[0])
    big_v = dict(w_in=v_w_in[0], w_branch_a=v_w_branch_a[0], w_branch_b=v_w_branch_b[0], w_out=v_w_out[0],
                 w_ffn_gate=v_w_ffn_gate[0], w_ffn_up=v_w_ffn_up[0], w_ffn_down=v_w_ffn_down[0],
                 w_ple_gate=v_w_ple_gate[0], w_ple_proj=v_w_ple_proj[0])
    names = list(big)
    nf = big["w_ffn_gate"].shape[1]

    shards = {k: big[k].astype(BF16) for k in names}
    shards["conv_w"] = _pad_rows(conv_w[0], SUBLANES)
    flipped = ("w_ffn_gate", "w_ffn_up")
    for k in flipped:
        big[k], big_m[k], big_v[k] = big[k].T, big_m[k].T, big_v[k].T
    ag_groups = [["w_in"], ["w_branch_a", "w_branch_b", "w_out"], ["w_ffn_gate", "conv_w"], ["w_ffn_up"],
                 ["w_ffn_down", "w_ple_gate", "w_ple_proj"]]
    ag_started = {}
    wg = {}

    ag_paired, ag_passing = {}, {}

    def pair(gi, after):
        lands = [lax.dynamic_update_index_in_dim(lax.empty((N_DEV,) + shards[k].shape, shards[k].dtype), shards[k],
                                                 me, 0) for k in ag_groups[gi]]
        ag_paired[gi], token = ag_pair(f"ag_pair{gi}", lands, after)
        return token

    def start(gi, after):
        s_sems, r_sems, _, lands, token = ag_start(f"ag_start{gi}", ag_paired[gi], after)
        ag_started[gi] = (s_sems, r_sems, lands)
        return token

    def landed(gi, after):
        s_sems, r_sems, lands = ag_started[gi]
        return split_wait(f"ag_wait{gi}", s_sems, r_sems, [], lands, ag_plan(len(lands)), after)[1:]

    def pass_on(gi, lands, after):
        s_sems, r_sems, _, lands, token = split_start(f"ag_pass{gi}", [], lands, pass_plan(len(lands)), after)
        ag_passing[gi] = (s_sems, r_sems, lands)
        return token

    def ready(gi, after):
        s_sems, r_sems, lands = ag_passing[gi]
        lands = split_wait(f"ag_ready{gi}", s_sems, r_sems, [], lands, pass_plan(len(lands)), after)[1]
        wg.update(zip(ag_groups[gi], lands))

    cb = conv_b.reshape(N_DEV, 1, nf)

    table_t = rel_bias_table.T
    geo_a = dict(half=A_BLOCK, q_rows=ATTN_Q_ROWS, n_chains=ATTN_CHAINS, dil=1, nh=A_Q_HEADS, group=A_GROUP,
                 cq=COL_QA, ck=COL_KA, cv=COL_VA)
    geo_b = [dict(half=B_BLOCK, q_rows=min(ATTN_Q_ROWS, s // dil), n_chains=ATTN_CHAINS, dil=dil,
                  nh=B_HEADS_PER_GROUP, group=1, cq=COL_QB + g * B_OUT_W, ck=COL_KB + g * B_OUT_W,
                  cv=COL_VB + g * B_OUT_W) for g, (_, dil) in enumerate(B_PATTERNS)]
    bucket_a = bucket_tile(geo_a["q_rows"], A_BLOCK, 1)
    bias_a = bias_build("bias_a", table_t, bucket_a, 0, A_Q_HEADS, A_BLOCK)
    buckets_b = [bucket_tile(gb["q_rows"], B_BLOCK, gb["dil"]) for gb in geo_b]
    biases_b = [bias_build(f"bias_b{g}", table_t, buckets_b[g], A_Q_HEADS + g * B_HEADS_PER_GROUP, B_HEADS_PER_GROUP,
                           B_BLOCK) for g in range(len(B_PATTERNS))]

    token = start(1, pair(1, start(0, pair(0, xs))))
    h = rms_fwd("rms_attn", xs, attn_norm + _token_value(token))
    lands0, token = landed(0, pair(4, pair(3, pair(2, h))))
    token = start(3, start(2, token))
    wg["w_in"] = ag_finish("ag_finish0", lands0)[0]
    proj = mm_cols("proj_in", h, wg["w_in"], F32, fold=True, after=token)
    token = start(4, pass_on(1, landed(1, proj)[0], proj))
    sink = sink_a[0] + _token_value(token)
    ya, lse_a = band_attn_fwd("attn_a_fwd", proj, bias_a, sink, **geo_a)
    outs_b, lses_b = [], []
    for g in range(len(B_PATTERNS)):
        o, l = band_attn_fwd(f"attn_b{g}_fwd", proj, biases_b[g], None, **geo_b[g])
        outs_b.append(o)
        lses_b.append(l)
    yb = dil_merge_fwd("dil_merge_fwd", outs_b, lses_b)
    ready(1, yb)
    token = pass_on(2, landed(2, yb)[0], yb)
    w_out_full = wg["w_out"].reshape(d, d)
    ta = mm_cols("branch_a", ya, wg["w_branch_a"], F32, fold=True, after=token)
    tb = mm_cols("branch_b", yb, wg["w_branch_b"], F32, fold=True)
    merged = gate_merge_fwd("gate_merge_fwd", proj, ta, tb, d)
    x1 = mm_plain("mix_out", merged, w_out_full, F32, res=xs)

    hf = rms_fwd("rms_ffn", x1, ffn_norm)
    ready(2, hf)
    token = pass_on(3, landed(3, hf)[0], hf)
    cw = wg["conv_w"]
    gpre = mm_cols("ffn_gate", hf, wg["w_ffn_gate"], F32, fold=False, after=token)
    ready(3, gpre)
    token = pass_on(4, landed(4, gpre)[0], gpre)
    u = mm_cols("ffn_up", hf, wg["w_ffn_up"], F32, fold=False, after=token)
    z = ffn_mid_fwd("ffn_mid_fwd", gpre, u, cw, cb)
    ready(4, z)
    w_pg_full = wg["w_ple_gate"].reshape(d, d)
    x2 = mm_jsum("ffn_down", z, wg["w_ffn_down"], F32, res=x1)

    hp = rms_fwd("rms_ple", x2, ple_norm)
    lp = mm_plain("ple_gate", hp, w_pg_full, F32)
    pp = mm_cols("ple_proj", ps, wg["w_ple_proj"], F32, fold=True)
    loss_part, dx3, dlp, dpp, d_final = tail_fwd_bwd("tail", x2, lp, pp, final_norm.reshape(1, d), target)

    grads = {}
    rs_started = []
    blocks = own_blocks()

    exchanging = []

    def exchange(tag, keys):
        parts = [grads[k] for k in keys]
        lands = [lax.empty((N_CHIPS,) + tuple(p.shape[1:]), p.dtype) for p in parts]
        s_sems, r_sems, parts, lands, token = split_start(f"rs_pair_{tag}", parts, lands, rs_pair_plan(len(keys)), blocks)
        exchanging.append((tag, keys, s_sems, r_sems, parts, lands))
        return _token_value(token)

    def send(after):
        tag, keys, s_sems, r_sems, parts, lands = exchanging.pop(0)
        parts, got, _ = split_wait(f"rs_paired_{tag}", s_sems, r_sems, parts, lands, rs_pair_plan(len(keys)), after)
        return send_sums(tag, keys, parts, got)

    def send_sums(tag, keys, parts, got):
        sums = [pair_add(f"pair_add_{k}", blocks, p, g) for k, p, g in zip(keys, parts, got)]
        s_sems, r_sems, srcs, lands, token = rs_start(f"rs_start_{tag}", sums, blocks)
        rs_started.append((tag, keys, s_sems, r_sems, srcs, lands))
        return token

    grads["w_ple_proj"] = mm_tn_cols("d_w_ple_proj", ps, dpp, N_DEV, big["w_ple_proj"].shape[1], BF16, folded=True)
    grads["w_ple_gate"] = mm_tn_plain("d_w_ple_gate", hp, dlp, BF16).reshape(N_DEV, d // N_DEV, d)
    tok = exchange("ple", ["w_ple_proj", "w_ple_gate"])
    dhp = mm_nt_plain("d_hp", dlp, w_pg_full, F32)
    dx2, d_ple = rms_bwd("rms_ple_bwd", x2, ple_norm + tok, dhp, dx3)

    dz = mm_nt_j("d_z", dx2, wg["w_ffn_down"], BF16)
    grads["w_ffn_down"] = mm_tn_j("d_w_ffn_down", z, dx2, BF16)
    tok = _token_value(send(dz)) + exchange("down", ["w_ffn_down"])
    dg, du, dcw = ffn_mid_bwd1("ffn_mid_bwd1", gpre, u, dz, cw, cb + tok)
    dgpre = ffn_mid_bwd2("ffn_mid_bwd2", dg, cw)
    grads["w_ffn_up"] = mm_tn_j("d_w_ffn_up", du, hf, BF16)
    grads["w_ffn_gate"] = mm_tn_j("d_w_ffn_gate", dgpre, hf, BF16)
    dhf = mm_nt_jsum("d_hf_up", du, wg["w_ffn_up"], F32, folded=False)
    dhf = mm_nt_jsum("d_hf_gate", dgpre, wg["w_ffn_gate"], F32, folded=False, res=dhf)
    tok = _token_value(send(dhf)) + exchange("upgate", ["w_ffn_up", "w_ffn_gate"])
    dx1, d_ffn = rms_bwd("rms_ffn_bwd", x1, ffn_norm + tok, dhf, dx2)

    dmerged = mm_nt_plain("d_merged", dx1, w_out_full, F32)
    grads["w_out"] = mm_tn_plain("d_w_out", merged, dx1, BF16).reshape(N_DEV, d // N_DEV, d)
    dta, dtb, dga, dgb = gate_merge_bwd("gate_merge_bwd", dmerged, proj, ta, tb, d)
    grads["w_branch_a"] = mm_tn_cols("d_w_branch_a", ya, dta, N_DEV, big["w_branch_a"].shape[1], BF16, folded=True)
    grads["w_branch_b"] = mm_tn_cols("d_w_branch_b", yb, dtb, N_DEV, big["w_branch_b"].shape[1], BF16, folded=True)
    dya = mm_nt_jsum("d_ya", dta, wg["w_branch_a"], F32, folded=True)
    dyb = mm_nt_jsum("d_yb", dtb, wg["w_branch_b"], F32, folded=True)
    tok = _token_value(send(dyb)) + exchange("mix", ["w_out", "w_branch_a", "w_branch_b"])
    dqa, dka, dva, dbias_a, dsink = band_attn_bwd("attn_a_bwd", proj, bias_a, sink + tok, dya, ya, lse_a, None, **geo_a)
    douts_b, dlses_b = dil_merge_bwd("dil_merge_bwd", dyb, outs_b, lses_b)
    dq_b, dk_b, dv_b, dbias_b = [], [], [], []
    for g in range(len(B_PATTERNS)):
        dq, dk, dv, db, _ = band_attn_bwd(f"attn_b{g}_bwd", proj, biases_b[g], None, douts_b[g], outs_b[g], lses_b[g],
                                          dlses_b[g], **geo_b[g])
        dq_b.append(dq)
        dk_b.append(dk)
        dv_b.append(dv)
        dbias_b.append(db)
    dproj = jnp.concatenate([t.astype(BF16) for t in [dqa, dka, dva] + dq_b + dk_b + dv_b + [dga, dgb]], axis=1)
    token = send(dproj)
    grads["w_in"] = mm_tn_cols("d_w_in", h, dproj, N_DEV, big["w_in"].shape[1], BF16, folded=True, after=token)
    token = send_sums("in", ["w_in"], [grads["w_in"]], rs_pair("rs_pair_in", [grads["w_in"]]))
    dh = mm_nt_jsum("d_h", dproj, wg["w_in"], F32, folded=True, after=token)
    grad_x, d_attn = rms_bwd("rms_attn_bwd", xs, attn_norm, dh, dx1)

    dt_a = table_grad("table_grad_a", dbias_a, bucket_a)[:, 0, :N_BUCKETS]
    dt_b = [table_grad(f"table_grad_b{g}", dbias_b[g], buckets_b[g])[:, 0, :N_BUCKETS] for g in range(len(B_PATTERNS))]
    d_table_part = jnp.concatenate([dt_a] + dt_b, axis=0).T

    pieces = [
        ("loss", loss_part[0, :1]),
        ("table", d_table_part.reshape(-1)),
        ("attn_norm", d_attn.reshape(-1)),
        ("sink", dsink[:, 0, 0]),
        ("ffn_norm", d_ffn.reshape(-1)),
        ("conv_w", dcw[:, 0:3, :].reshape(-1)),
        ("conv_b", dcw[:, 3, :].reshape(-1)),
        ("ple_norm", d_ple.reshape(-1)),
        ("final_norm", d_final.reshape(-1)),
    ]
    tiles = [_as_tiles(v) for _, v in pieces]
    pack = jnp.concatenate(tiles, axis=0)

    out_g, out_d, out_m, out_v = {}, {}, {}, {}

    def finish(group, after):
        tag, keys, s_sems, r_sems, srcs, lands = group
        srcs, lands, _ = split_wait(f"rs_wait_{tag}", s_sems, r_sems, srcs, lands, rs_plan(len(keys)), after)
        for k, mine, theirs in zip(keys, srcs, lands):
            res = reduce_adam("adam_" + k, mine, theirs, big[k], big_m[k], big_v[k])
            after = res[1]
            out_g[k], out_d[k], out_m[k], out_v[k] = [(t.T if k in flipped else t)[None] for t in res]
        return after

    after = pack
    for group in rs_started[:-1]:
        after = finish(group, after)
    total = allreduce_small("allreduce_small", pack, after)
    finish(rs_started[-1], total)
    small = {}
    row = 0
    for (nm, v), t in zip(pieces, tiles):
        small[nm] = total[row:row + t.shape[0]].reshape(-1)[:v.shape[0]]
        row += t.shape[0]
    loss = small["loss"][0]
    g_small = dict(
        rel_bias_table=small["table"].reshape(rel_bias_table.shape),
        attn_norm=small["attn_norm"].reshape(attn_norm.shape),
        sink_a=small["sink"].reshape(sink_a.shape),
        ffn_norm=small["ffn_norm"].reshape(ffn_norm.shape),
        conv_w=lax.dynamic_index_in_dim(small["conv_w"].reshape(N_DEV, 3, nf), me, 0, keepdims=False)[None],
        conv_b=small["conv_b"].reshape(conv_b.shape),
        ple_norm=small["ple_norm"].reshape(ple_norm.shape),
        final_norm=small["final_norm"].reshape(1, d),
    )
    w_small = dict(rel_bias_table=(rel_bias_table, m_rel_bias_table, v_rel_bias_table),
                   attn_norm=(attn_norm, m_attn_norm, v_attn_norm), sink_a=(sink_a, m_sink_a, v_sink_a),
                   ffn_norm=(ffn_norm, m_ffn_norm, v_ffn_norm), conv_w=(conv_w, m_conv_w, v_conv_w),
                   conv_b=(conv_b, m_conv_b, v_conv_b), ple_norm=(ple_norm, m_ple_norm, v_ple_norm),
                   final_norm=(final_norm, m_final_norm, v_final_norm))

    for k, (wv, mv, vv) in w_small.items():
        shape = wv.shape
        two_d = (1, shape[0]) if len(shape) == 1 else ((shape[0] * shape[1], shape[2]) if len(shape) == 3 else shape)
        gk = g_small[k].reshape(two_d)
        dl, nm, nv = adam_small("adam_" + k, gk, wv.reshape(two_d), mv.reshape(two_d), vv.reshape(two_d))
        out_g[k], out_d[k], out_m[k], out_v[k] = gk.reshape(shape), dl.reshape(shape), nm.reshape(shape), nv.reshape(shape)

    order = ["rel_bias_table", "attn_norm", "w_in", "sink_a", "w_branch_a", "w_branch_b", "w_out", "ffn_norm",
             "w_ffn_gate", "w_ffn_up", "conv_w", "conv_b", "w_ffn_down", "ple_norm", "w_ple_gate", "w_ple_proj",
             "final_norm"]
    return (loss, grad_x[None], *[out_g[k] for k in order], *[out_d[k] for k in order],
            *[out_m[k] for k in order], *[out_v[k] for k in order])
```

```python
import math

import jax
import jax.numpy as jnp
from jax import lax
from jax.experimental import pallas as pl
from jax.experimental.pallas import tpu as pltpu

F32 = jnp.float32
BF16 = jnp.bfloat16
MESH = pl.DeviceIdType.MESH
N_DEV = 8

HEAD_DIM = 128
A_Q_HEADS = 8
A_KV_HEADS = 2
A_GROUP = A_Q_HEADS // A_KV_HEADS
A_BLOCK = 128
B_PATTERNS = ((128, 1), (512, 4), (2048, 16))
B_HEADS_PER_GROUP = 4
B_HEADS = len(B_PATTERNS) * B_HEADS_PER_GROUP
B_BLOCK = 64
N_BUCKETS = 32
MAX_DISTANCE = 1024
A_Q_W = A_Q_HEADS * HEAD_DIM
A_KV_W = A_KV_HEADS * HEAD_DIM
B_W = B_HEADS * HEAD_DIM
B_OUT_W = B_HEADS_PER_GROUP * HEAD_DIM
COL_QA = 0
COL_KA = COL_QA + A_Q_W
COL_VA = COL_KA + A_KV_W
COL_QB = COL_VA + A_KV_W
COL_KB = COL_QB + B_W
COL_VB = COL_KB + B_W
COL_GATES = COL_VB + B_W
RMS_EPS = 1e-6
NEG_INF = -1e30
ATTN_SCALE = HEAD_DIM ** -0.5
ATTN_Q_ROWS = 256
ATTN_CHAINS = 4

ADAM_LR = 0.001
ADAM_B1 = 0.9
ADAM_B2 = 0.999
ADAM_EPS = 1e-08
ADAM_WD = 0.01
ADAM_STEP = 10

GELU_C = math.sqrt(2.0 / math.pi)
GELU_A = 0.044715

V7X_VMEM_BYTES = 64 * 1024 * 1024
VMEM_CEILING = V7X_VMEM_BYTES - 8 * 1024 * 1024
LANES = 128
SUBLANES = 8


def _pick(n, cands):
    for c in cands:
        if n % c == 0:
            return c
    return n


def _nbytes(shape, dtype):
    n = 1
    for d in shape:
        if d is not None:
            n *= d
    return n * jnp.dtype(dtype).itemsize


def _params(sem, est_bytes):
    limit = int(min(VMEM_CEILING, max(32 * 1024 * 1024, 2 * est_bytes + (8 << 20))))
    return pltpu.CompilerParams(dimension_semantics=sem, vmem_limit_bytes=limit)


def _mm(name, a, b, a_bs, a_im, b_bs, b_im, out_shape, out_dtype, o_bs, o_im, grid, dims,
        res=None, r_bs=None, r_im=None, after=None):
    nk = grid[-1]
    nax = len(grid)
    has_res = res is not None
    has_after = after is not None
    o_tile = tuple(d for d in o_bs if d is not None)

    def body(*refs):
        a_ref, b_ref = refs[:2]
        r_ref = refs[2] if has_res else None
        n_in = 2 + has_res + has_after
        o_ref = refs[n_in]
        rest = refs[n_in + 1:]

        def prod():
            return lax.dot_general(a_ref[...].astype(BF16), b_ref[...].astype(BF16), (dims, ((), ())),
                                   preferred_element_type=F32)

        def finish(r):
            if r_ref is not None:
                r = r + r_ref[...].astype(F32)
            o_ref[...] = r.astype(o_ref.dtype)

        if nk == 1:
            finish(prod())
        else:
            acc = rest[0]
            k = pl.program_id(nax - 1)

            @pl.when(k == 0)
            def _():
                acc[...] = prod()

            @pl.when(k > 0)
            def _():
                acc[...] += prod()

            @pl.when(k == nk - 1)
            def _():
                finish(acc[...])

    in_specs = [pl.BlockSpec(a_bs, a_im), pl.BlockSpec(b_bs, b_im)]
    args = [a, b]
    est = _nbytes(a_bs, a.dtype) + _nbytes(b_bs, b.dtype) + _nbytes(o_bs, out_dtype) + 2 * _nbytes(o_tile, F32)
    if has_res:
        in_specs.append(pl.BlockSpec(r_bs, r_im))
        args.append(res)
        est += _nbytes(r_bs, res.dtype)
    if has_after:
        in_specs.append(pl.BlockSpec(memory_space=pl.ANY))
        args.append(after)
    scratch = [] if nk == 1 else [pltpu.VMEM(o_tile, F32)]
    sem = ("parallel",) * (nax - 1) + ("arbitrary",)
    return pl.pallas_call(
        body, name=name, grid=grid, in_specs=in_specs, out_specs=pl.BlockSpec(o_bs, o_im),
        out_shape=pltpu.HBM(out_shape, out_dtype), scratch_shapes=scratch,
        compiler_params=_params(sem, est))(*args)


TM_CANDS = (1024, 512, 256, 128, 64, 32, 16, 8)
TM_WIDE_CANDS = (2048,) + TM_CANDS
TK_CANDS = (1024, 512, 256, 128)
TN_CANDS = (1024, 512, 256, 128)


def mm_cols(name, a, wg, out_dtype, fold, after=None):
    m, k = a.shape
    nj, _, n = wg.shape
    tm, tk = _pick(m, TM_WIDE_CANDS), _pick(k, TK_CANDS)
    grid = (nj, m // tm, k // tk)
    if fold:
        shape, o_bs, o_im = (m, nj * n), (tm, n), (lambda j, i, kk: (i, j))
    else:
        shape, o_bs, o_im = (nj, m, n), (None, tm, n), (lambda j, i, kk: (j, i, 0))
    return _mm(name, a, wg, (tm, tk), lambda j, i, kk: (i, kk), (None, tk, n), lambda j, i, kk: (j, kk, 0),
               shape, out_dtype, o_bs, o_im, grid, ((1,), (0,)), after=after)


def mm_plain(name, a, w, out_dtype, res=None):
    m, k = a.shape
    n = w.shape[1]
    tm, tk, tn = _pick(m, TM_CANDS), _pick(k, TK_CANDS), _pick(n, TN_CANDS)
    grid = (n // tn, m // tm, k // tk)
    return _mm(name, a, w, (tm, tk), lambda j, i, kk: (i, kk), (tk, tn), lambda j, i, kk: (kk, j),
               (m, n), out_dtype, (tm, tn), lambda j, i, kk: (i, j), grid, ((1,), (0,)),
               res, (tm, tn), lambda j, i, kk: (i, j))


def mm_jsum(name, aj, wg, out_dtype, res=None):
    nj, m, ka = aj.shape
    n = wg.shape[2]
    tm, tn = _pick(m, TM_CANDS), _pick(n, TN_CANDS)
    grid = (m // tm, n // tn, nj)
    return _mm(name, aj, wg, (None, tm, ka), lambda i, jn, j: (j, i, 0), (None, ka, tn), lambda i, jn, j: (j, 0, jn),
               (m, n), out_dtype, (tm, tn), lambda i, jn, j: (i, jn), grid, ((1,), (0,)),
               res, (tm, tn), lambda i, jn, j: (i, jn))


def mm_tn_cols(name, a, g, nj, n, out_dtype, folded, after=None):
    s, kw = a.shape
    ts, tkw = _pick(s, TK_CANDS), _pick(kw, TM_CANDS)
    grid = (nj, kw // tkw, s // ts)
    if folded:
        g_bs, g_im = (ts, n), (lambda j, i, ss: (ss, j))
    else:
        g_bs, g_im = (None, ts, n), (lambda j, i, ss: (j, ss, 0))
    return _mm(name, a, g, (ts, tkw), lambda j, i, ss: (ss, i), g_bs, g_im,
               (nj, kw, n), out_dtype, (None, tkw, n), lambda j, i, ss: (j, i, 0), grid, ((0,), (0,)), after=after)


def mm_tn_plain(name, a, g, out_dtype):
    s, kw = a.shape
    n = g.shape[1]
    ts, tkw, tn = _pick(s, TK_CANDS), _pick(kw, TM_CANDS), _pick(n, TN_CANDS)
    grid = (kw // tkw, n // tn, s // ts)
    return _mm(name, a, g, (ts, tkw), lambda i, jn, ss: (ss, i), (ts, tn), lambda i, jn, ss: (ss, jn),
               (kw, n), out_dtype, (tkw, tn), lambda i, jn, ss: (i, jn), grid, ((0,), (0,)))


def mm_tn_j(name, aj, g, out_dtype):
    nj, s, ka = aj.shape
    n = g.shape[1]
    ts, tn = _pick(s, TK_CANDS), _pick(n, TN_CANDS)
    grid = (nj, n // tn, s // ts)
    return _mm(name, aj, g, (None, ts, ka), lambda j, jn, ss: (j, ss, 0), (ts, tn), lambda j, jn, ss: (ss, jn),
               (nj, ka, n), out_dtype, (None, ka, tn), lambda j, jn, ss: (j, 0, jn), grid, ((0,), (0,)))


def mm_nt_plain(name, g, w, out_dtype):
    m, n = g.shape
    k = w.shape[0]
    tm, tn, tkk = _pick(m, TM_CANDS), _pick(n, TK_CANDS), _pick(k, TN_CANDS)
    grid = (k // tkk, m // tm, n // tn)
    return _mm(name, g, w, (tm, tn), lambda kk, i, jn: (i, jn), (tkk, tn), lambda kk, i, jn: (kk, jn),
               (m, k), out_dtype, (tm, tkk), lambda kk, i, jn: (i, kk), grid, ((1,), (1,)))


def mm_nt_j(name, g, wg, out_dtype):
    m, n = g.shape
    nj, ka, _ = wg.shape
    tm, tn = _pick(m, TM_CANDS), _pick(n, TK_CANDS)
    grid = (nj, m // tm, n // tn)
    return _mm(name, g, wg, (tm, tn), lambda j, i, jn: (i, jn), (None, ka, tn), lambda j, i, jn: (j, 0, jn),
               (nj, m, ka), out_dtype, (None, tm, ka), lambda j, i, jn: (j, i, 0), grid, ((1,), (1,)))


def mm_nt_jsum(name, g, wg, out_dtype, folded, res=None, after=None):
    nj, k, n = wg.shape
    m = g.shape[0] if folded else g.shape[1]
    tm, tkk = _pick(m, TM_CANDS if res is not None else TM_WIDE_CANDS), _pick(k, TN_CANDS)
    grid = (m // tm, k // tkk, nj)
    if folded:
        g_bs, g_im = (tm, n), (lambda i, kk, j: (i, j))
    else:
        g_bs, g_im = (None, tm, n), (lambda i, kk, j: (j, i, 0))
    return _mm(name, g, wg, g_bs, g_im, (None, tkk, n), lambda i, kk, j: (j, kk, 0),
               (m, k), out_dtype, (tm, tkk), lambda i, kk, j: (i, kk), grid, ((1,), (1,)),
               res, (tm, tkk), lambda i, kk, j: (i, kk), after=after)


ROW_TILE_CANDS = (256, 128, 64, 32, 16, 8)


def _rstd(x):
    return lax.rsqrt(jnp.mean(x * x, axis=-1, keepdims=True) + RMS_EPS)


def _sigmoid(t):
    return 1.0 / (1.0 + jnp.exp(-t))


def rms_fwd(name, x, gain):
    s, d = x.shape
    ts = _pick(s, ROW_TILE_CANDS)

    def body(x_ref, g_ref, h_ref):
        xv = x_ref[...]
        h_ref[...] = ((xv * _rstd(xv)) * g_ref[...]).astype(h_ref.dtype)

    return pl.pallas_call(
        body, name=name, grid=(s // ts,),
        in_specs=[pl.BlockSpec((ts, d), lambda i: (i, 0)), pl.BlockSpec((1, d), lambda i: (0, 0))],
        out_specs=pl.BlockSpec((ts, d), lambda i: (i, 0)),
        out_shape=pltpu.HBM((s, d), BF16),
        compiler_params=_params(("parallel",), 3 * ts * d * 4))(x, gain)


def rms_bwd(name, x, gain, dh, dres):
    s, d = x.shape
    ts = _pick(s, ROW_TILE_CANDS)

    def body(x_ref, g_ref, dh_ref, dr_ref, dx_ref, dg_ref):
        xv = x_ref[...]
        r = _rstd(xv)
        xhat = xv * r
        dhv = dh_ref[...].astype(F32)
        dxhat = dhv * g_ref[...]
        dx_ref[...] = dr_ref[...] + r * (dxhat - xhat * jnp.mean(dxhat * xhat, axis=-1, keepdims=True))
        part = jnp.sum(dhv * xhat, axis=0, keepdims=True)

        @pl.when(pl.program_id(0) == 0)
        def _():
            dg_ref[...] = part

        @pl.when(pl.program_id(0) > 0)
        def _():
            dg_ref[...] += part

    row = pl.BlockSpec((ts, d), lambda i: (i, 0))
    vec = pl.BlockSpec((1, d), lambda i: (0, 0))
    return pl.pallas_call(
        body, name=name, grid=(s // ts,), in_specs=[row, vec, row, row], out_specs=[row, vec],
        out_shape=[pltpu.HBM((s, d), F32), jax.ShapeDtypeStruct((1, d), F32)],
        compiler_params=_params(("arbitrary",), 6 * ts * d * 4))(x, gain, dh, dres)


def gate_merge_fwd(name, proj, ta, tb, d):
    s = proj.shape[0]
    ts = _pick(s, ROW_TILE_CANDS)
    cb = COL_GATES // d

    def body(ga_ref, gb_ref, ta_ref, tb_ref, o_ref):
        o_ref[...] = (_sigmoid(ga_ref[...]) * ta_ref[...] + _sigmoid(gb_ref[...]) * tb_ref[...]).astype(o_ref.dtype)

    row = pl.BlockSpec((ts, d), lambda i: (i, 0))
    return pl.pallas_call(
        body, name=name, grid=(s // ts,),
        in_specs=[pl.BlockSpec((ts, d), lambda i: (i, cb)), pl.BlockSpec((ts, d), lambda i: (i, cb + 1)), row, row],
        out_specs=row, out_shape=pltpu.HBM((s, d), BF16),
        compiler_params=_params(("parallel",), 5 * ts * d * 4))(proj, proj, ta, tb)


def gate_merge_bwd(name, dmerged, proj, ta, tb, d):
    s = proj.shape[0]
    ts = _pick(s, ROW_TILE_CANDS)
    cb = COL_GATES // d

    def body(dm_ref, ga_ref, gb_ref, ta_ref, tb_ref, dta_ref, dtb_ref, dga_ref, dgb_ref):
        dm = dm_ref[...]
        sa = _sigmoid(ga_ref[...])
        sb = _sigmoid(gb_ref[...])
        dta_ref[...] = (dm * sa).astype(dta_ref.dtype)
        dtb_ref[...] = (dm * sb).astype(dtb_ref.dtype)
        dga_ref[...] = (dm * ta_ref[...] * (sa * (1.0 - sa))).astype(dga_ref.dtype)
        dgb_ref[...] = (dm * tb_ref[...] * (sb * (1.0 - sb))).astype(dgb_ref.dtype)

    row = pl.BlockSpec((ts, d), lambda i: (i, 0))
    out = pltpu.HBM((s, d), BF16)
    return pl.pallas_call(
        body, name=name, grid=(s // ts,),
        in_specs=[row, pl.BlockSpec((ts, d), lambda i: (i, cb)), pl.BlockSpec((ts, d), lambda i: (i, cb + 1)), row, row],
        out_specs=[row, row, row, row], out_shape=[out, out, out, out],
        compiler_params=_params(("parallel",), 8 * ts * d * 4))(dmerged, proj, proj, ta, tb)


def tail_fwd_bwd(name, x2, lp, pp, gain, target):
    s, d = x2.shape
    ts = _pick(s, ROW_TILE_CANDS)

    def body(x2_ref, lp_ref, pp_ref, g_ref, t_ref, loss_ref, dx3_ref, dlp_ref, dpp_ref, dg_ref):
        gp = _sigmoid(lp_ref[...])
        ppv = pp_ref[...]
        x3 = x2_ref[...] + gp * ppv
        r = _rstd(x3)
        xhat = x3 * r
        gv = g_ref[...]
        err = xhat * gv - t_ref[...]
        loss = jnp.sum(err * err) * (0.5 / d)
        dy = err * (1.0 / d)
        dxhat = dy * gv
        dx3 = r * (dxhat - xhat * jnp.mean(dxhat * xhat, axis=-1, keepdims=True))
        dx3_ref[...] = dx3
        dlp_ref[...] = (dx3 * ppv * (gp * (1.0 - gp))).astype(dlp_ref.dtype)
        dpp_ref[...] = (dx3 * gp).astype(dpp_ref.dtype)
        part = jnp.sum(dy * xhat, axis=0, keepdims=True)
        lossv = jnp.full((1, LANES), loss, F32)

        @pl.when(pl.program_id(0) == 0)
        def _():
            dg_ref[...] = part
            loss_ref[...] = lossv

        @pl.when(pl.program_id(0) > 0)
        def _():
            dg_ref[...] += part
            loss_ref[...] += lossv

    row = pl.BlockSpec((ts, d), lambda i: (i, 0))
    vec = pl.BlockSpec((1, d), lambda i: (0, 0))
    return pl.pallas_call(
        body, name=name, grid=(s // ts,), in_specs=[row, row, row, vec, row],
        out_specs=[pl.BlockSpec((1, LANES), lambda i: (0, 0)), row, row, row, vec],
        out_shape=[jax.ShapeDtypeStruct((1, LANES), F32), pltpu.HBM((s, d), F32),
                   pltpu.HBM((s, d), BF16), pltpu.HBM((s, d), BF16),
                   jax.ShapeDtypeStruct((1, d), F32)],
        compiler_params=_params(("arbitrary",), 9 * ts * d * 4))(x2, lp, pp, gain, target)


HALO = SUBLANES


def _shift_rows(cur, prev_row, next_row):
    ts = cur.shape[0]
    rid = lax.broadcasted_iota(jnp.int32, cur.shape, 0)
    down = jnp.where(rid == 0, prev_row, pltpu.roll(cur, 1, 0))
    up = jnp.where(rid == ts - 1, next_row, pltpu.roll(cur, ts - 1, 0))
    return down, up


def _halo_specs(ts, s, nf):
    nb = ts // HALO
    last = s // HALO - 1
    cur = pl.BlockSpec((None, ts, nf), lambda j, i: (j, i, 0))
    prev = pl.BlockSpec((None, HALO, nf), lambda j, i: (j, jnp.maximum(i * nb - 1, 0), 0))
    nxt = pl.BlockSpec((None, HALO, nf), lambda j, i: (j, jnp.minimum((i + 1) * nb, last), 0))
    return cur, prev, nxt


def _halo_rows(prev_ref, next_ref, n_tiles):
    i = pl.program_id(1)
    prev_row = jnp.where(i == 0, 0.0, prev_ref[HALO - 1:HALO, :].astype(F32))
    next_row = jnp.where(i == n_tiles - 1, 0.0, next_ref[0:1, :].astype(F32))
    return prev_row, next_row


def _gelu(g):
    t = jnp.tanh(GELU_C * (g + GELU_A * (g * g * g)))
    return 0.5 * g * (1.0 + t), t


def _conv(cur, down, up, cw_ref, cb_ref):
    return down * cw_ref[0:1, :] + cur * cw_ref[1:2, :] + up * cw_ref[2:3, :] + cb_ref[...]


def ffn_mid_fwd(name, gpre, u, cw, cb):
    nj, s, nf = gpre.shape
    ts = _pick(s, (512, 256, 128, 64, 32, 16, 8))
    n_tiles = s // ts
    cur, prev, nxt = _halo_specs(ts, s, nf)

    def body(g_ref, gp_ref, gn_ref, u_ref, cw_ref, cb_ref, z_ref):
        gv = g_ref[...]
        down, up = _shift_rows(gv, *_halo_rows(gp_ref, gn_ref, n_tiles))
        act, _ = _gelu(_conv(gv, down, up, cw_ref, cb_ref))
        z_ref[...] = (act * u_ref[...]).astype(z_ref.dtype)

    return pl.pallas_call(
        body, name=name, grid=(nj, n_tiles),
        in_specs=[cur, prev, nxt, cur, pl.BlockSpec((None, SUBLANES, nf), lambda j, i: (j, 0, 0)),
                  pl.BlockSpec((None, 1, nf), lambda j, i: (j, 0, 0))],
        out_specs=cur, out_shape=pltpu.HBM((nj, s, nf), BF16),
        compiler_params=_params(("parallel", "parallel"), 8 * ts * nf * 4))(gpre, gpre, gpre, u, cw, cb)


def ffn_mid_bwd1(name, gpre, u, dz, cw, cb):
    nj, s, nf = gpre.shape
    ts = _pick(s, (512, 256, 128, 64, 32, 16, 8))
    n_tiles = s // ts
    cur, prev, nxt = _halo_specs(ts, s, nf)

    def body(g_ref, gp_ref, gn_ref, u_ref, dz_ref, cw_ref, cb_ref, dg_ref, du_ref, dcw_ref):
        gv = g_ref[...]
        down, up = _shift_rows(gv, *_halo_rows(gp_ref, gn_ref, n_tiles))
        gc = _conv(gv, down, up, cw_ref, cb_ref)
        act, t = _gelu(gc)
        dzv = dz_ref[...].astype(F32)
        du_ref[...] = (dzv * act).astype(du_ref.dtype)
        dact = 0.5 * (1.0 + t) + 0.5 * gc * (1.0 - t * t) * (GELU_C * (1.0 + 3.0 * GELU_A * (gc * gc)))
        dg = dzv * u_ref[...] * dact
        dg_ref[...] = dg
        rows = [jnp.sum(dg * down, axis=0, keepdims=True), jnp.sum(dg * gv, axis=0, keepdims=True),
                jnp.sum(dg * up, axis=0, keepdims=True), jnp.sum(dg, axis=0, keepdims=True)]
        part = jnp.concatenate(rows + [jnp.zeros((SUBLANES - len(rows), nf), F32)], axis=0)

        @pl.when(pl.program_id(1) == 0)
        def _():
            dcw_ref[...] = part

        @pl.when(pl.program_id(1) > 0)
        def _():
            dcw_ref[...] += part

    small = pl.BlockSpec((None, SUBLANES, nf), lambda j, i: (j, 0, 0))
    return pl.pallas_call(
        body, name=name, grid=(nj, n_tiles),
        in_specs=[cur, prev, nxt, cur, cur, small, pl.BlockSpec((None, 1, nf), lambda j, i: (j, 0, 0))],
        out_specs=[cur, cur, small],
        out_shape=[pltpu.HBM((nj, s, nf), F32), pltpu.HBM((nj, s, nf), BF16),
                   jax.ShapeDtypeStruct((nj, SUBLANES, nf), F32)],
        compiler_params=_params(("parallel", "arbitrary"), 12 * ts * nf * 4))(gpre, gpre, gpre, u, dz, cw, cb)


def ffn_mid_bwd2(name, dg, cw):
    nj, s, nf = dg.shape
    ts = _pick(s, (512, 256, 128, 64, 32, 16, 8))
    n_tiles = s // ts
    cur, prev, nxt = _halo_specs(ts, s, nf)

    def body(g_ref, gp_ref, gn_ref, cw_ref, o_ref):
        gv = g_ref[...]
        down, up = _shift_rows(gv, *_halo_rows(gp_ref, gn_ref, n_tiles))
        o_ref[...] = (up * cw_ref[0:1, :] + gv * cw_ref[1:2, :] + down * cw_ref[2:3, :]).astype(o_ref.dtype)

    return pl.pallas_call(
        body, name=name, grid=(nj, n_tiles),
        in_specs=[cur, prev, nxt, pl.BlockSpec((None, SUBLANES, nf), lambda j, i: (j, 0, 0))],
        out_specs=cur, out_shape=pltpu.HBM((nj, s, nf), BF16),
        compiler_params=_params(("parallel", "parallel"), 6 * ts * nf * 4))(dg, dg, dg, cw)


def _t5_bucket(rel):
    half = N_BUCKETS // 2
    max_exact = half // 2
    n = jnp.abs(rel)
    side = jnp.where(rel > 0, half, 0)
    nf = jnp.maximum(n, 1).astype(F32)
    large = max_exact + (jnp.log(nf / max_exact) / math.log(MAX_DISTANCE / max_exact)
                         * (half - max_exact)).astype(jnp.int32)
    large = jnp.minimum(large, half - 1)
    return side + jnp.where(n < max_exact, n, large)


def bucket_tile(rows, half, dil):
    rel = (jnp.arange(rows + 2 * half)[None, :] - half) - jnp.arange(rows)[:, None]
    return _t5_bucket(rel * dil).astype(jnp.int32)


def bias_build(name, table_t, bucket, h0, nh, half):
    blk, kw = bucket.shape

    def body(t_ref, b_ref, o_ref):
        h = pl.program_id(0)
        bv = b_ref[...]
        acc = jnp.zeros((blk, kw), F32)
        for b in range(N_BUCKETS):
            acc = jnp.where(bv == b, t_ref[h0 + h, b], acc)
        qi = lax.broadcasted_iota(jnp.int32, (blk, kw), 0)
        ci = lax.broadcasted_iota(jnp.int32, (blk, kw), 1)
        o_ref[...] = jnp.where(jnp.abs(ci - half - qi) <= half, acc, NEG_INF)

    return pl.pallas_call(
        body, name=name, grid=(nh,),
        in_specs=[pl.BlockSpec(memory_space=pltpu.SMEM), pl.BlockSpec((blk, kw), lambda h: (0, 0))],
        out_specs=pl.BlockSpec((None, blk, kw), lambda h: (h, 0, 0)),
        out_shape=jax.ShapeDtypeStruct((nh, blk, kw), F32),
        compiler_params=_params(("parallel",), 4 * blk * kw * 4))(table_t, bucket)


def table_grad(name, dbias, bucket):
    nh, blk, kw = dbias.shape

    def body(d_ref, b_ref, o_ref):
        bv = b_ref[...]
        dv = d_ref[...]
        lane = lax.broadcasted_iota(jnp.int32, (SUBLANES, LANES), 1)
        acc = jnp.zeros((SUBLANES, LANES), F32)
        for b in range(N_BUCKETS):
            acc = jnp.where(lane == b, jnp.sum(jnp.where(bv == b, dv, 0.0)), acc)
        o_ref[...] = acc

    return pl.pallas_call(
        body, name=name, grid=(nh,),
        in_specs=[pl.BlockSpec((None, blk, kw), lambda h: (h, 0, 0)), pl.BlockSpec((blk, kw), lambda h: (0, 0))],
        out_specs=pl.BlockSpec((None, SUBLANES, LANES), lambda h: (h, 0, 0)),
        out_shape=jax.ShapeDtypeStruct((nh, SUBLANES, LANES), F32),
        compiler_params=_params(("parallel",), 4 * blk * kw * 4))(dbias, bucket)


class _Band:
    def __init__(self, s, half, q_rows, n_chains, dil):
        self.s, self.half, self.dil, self.n_chains = s, half, dil, n_chains
        self.seg = s // dil
        self.q_rows = min(q_rows, self.seg)
        self.win = self.q_rows + 2 * half
        self.pad = self.seg + 2 * half
        self.nsb = self.seg // self.q_rows
        self.n_items = dil * self.nsb
        assert self.n_items % n_chains == 0 and self.seg % self.q_rows == 0
        self.staged = dil > 1

    def rows_of(self, r):
        return pl.ds(r, self.seg, stride=self.dil) if self.dil > 1 else slice(None)

    def stage_kv(self, dst, src_ref):
        zeros = jnp.zeros((self.half, HEAD_DIM), dst.dtype)
        for r in range(self.dil):
            base = r * self.pad
            dst[base:base + self.half, :] = zeros
            dst[base + self.half + self.seg:base + self.pad, :] = zeros
            dst[base + self.half:base + self.half + self.seg, :] = src_ref[self.rows_of(r), :].astype(dst.dtype)

    def stage(self, dst, src_ref):
        for r in range(self.dil):
            dst[r * self.seg:(r + 1) * self.seg, :] = src_ref[self.rows_of(r), :].astype(dst.dtype)

    def unstage(self, dst_ref, src, add=False):
        for r in range(self.dil):
            val = src[r * self.seg:(r + 1) * self.seg, :].astype(dst_ref.dtype)
            if add:
                val = val + dst_ref[self.rows_of(r), :]
            dst_ref[self.rows_of(r), :] = val

    def offsets(self, item):
        r, sb = item // self.nsb, item % self.nsb
        qoff = pl.multiple_of(r * self.seg + sb * self.q_rows, self.q_rows)
        koff = pl.multiple_of(r * self.pad + sb * self.q_rows, B_BLOCK)
        kpos = sb * self.q_rows - self.half + lax.broadcasted_iota(jnp.int32, (1, self.win), 1)
        edge = jnp.where((kpos >= 0) & (kpos < self.seg), 0.0, NEG_INF)
        return qoff, koff, edge


def band_attn_fwd(name, proj, bias, sink, *, half, q_rows, n_chains, dil, nh, group, cq, ck, cv):
    s, w = proj.shape
    g = _Band(s, half, q_rows, n_chains, dil)
    has_sink = sink is not None

    def body(*refs):
        q_ref, k_ref, v_ref, b_ref = refs[:4]
        s_ref = refs[4] if has_sink else None
        o_ref, l_ref, ks, vs = refs[4 + has_sink:8 + has_sink]
        qs, os_, ls = refs[8 + has_sink:] if g.staged else (None, o_ref, l_ref)
        g.stage_kv(ks, k_ref)
        g.stage_kv(vs, v_ref)
        if g.staged:
            g.stage(qs, q_ref)
        bias_v = b_ref[...]
        sk = s_ref[pl.program_id(0)] if has_sink else None

        def chain(item):
            qoff, koff, edge = g.offsets(item)
            rows = pl.ds(qoff, g.q_rows)
            qv = qs[rows, :] if g.staged else q_ref[rows, :].astype(BF16)
            kw_ = ks[pl.ds(koff, g.win), :]
            vw_ = vs[pl.ds(koff, g.win), :]
            sc = lax.dot_general(qv, kw_, (((1,), (1,)), ((), ())), preferred_element_type=F32) * ATTN_SCALE
            sc = sc + bias_v + edge
            m = jnp.max(sc, axis=-1, keepdims=True)
            if has_sink:
                m = jnp.maximum(m, sk)
            p = jnp.exp(sc - m)
            den = jnp.sum(p, axis=-1, keepdims=True)
            if has_sink:
                den = den + jnp.exp(sk - m)
            out = lax.dot_general(p.astype(BF16), vw_, (((1,), (0,)), ((), ())), preferred_element_type=F32)
            return rows, out / den, jnp.broadcast_to(m + jnp.log(den), (g.q_rows, HEAD_DIM))

        def step(i, carry):
            for rows, out, lse in [chain(i * n_chains + u) for u in range(n_chains)]:
                os_[rows, :] = out
                ls[rows, :] = lse
            return carry

        lax.fori_loop(0, g.n_items // n_chains, step, 0)
        if g.staged:
            g.unstage(o_ref, os_)
            g.unstage(l_ref, ls)

    def col(c0, per):
        return pl.BlockSpec((s, HEAD_DIM), lambda h: (0, c0 // LANES + h // per))

    in_specs = [col(cq, 1), col(ck, group), col(cv, group),
                pl.BlockSpec((None, g.q_rows, g.win), lambda h: (h, 0, 0))]
    args = [proj, proj, proj, bias]
    if has_sink:
        in_specs.append(pl.BlockSpec(memory_space=pltpu.SMEM))
        args.append(sink)
    shape = pltpu.HBM((s, nh * HEAD_DIM), F32)
    scratch = [pltpu.VMEM((dil * g.pad, HEAD_DIM), BF16), pltpu.VMEM((dil * g.pad, HEAD_DIM), BF16)]
    if g.staged:
        scratch += [pltpu.VMEM((s, HEAD_DIM), BF16), pltpu.VMEM((s, HEAD_DIM), F32), pltpu.VMEM((s, HEAD_DIM), F32)]
    return pl.pallas_call(
        body, name=name, grid=(nh,), in_specs=in_specs, out_specs=[col(0, 1), col(0, 1)], out_shape=[shape, shape],
        scratch_shapes=scratch, compiler_params=_params(("parallel",), 16 * s * HEAD_DIM * 4))(*args)


def band_attn_bwd(name, proj, bias, sink, dout, out, lse, dlse, *, half, q_rows, n_chains, dil, nh, group, cq, ck, cv):
    s, w = proj.shape
    g = _Band(s, half, q_rows, n_chains, dil)
    nkv = nh // group
    has_sink = sink is not None
    has_dl = dlse is not None
    n_in = 7 + int(has_sink) + int(has_dl)
    n_out = 4 + int(has_sink)

    def body(*refs):
        ins, outs, scr = refs[:n_in], refs[n_in:n_in + n_out], refs[n_in + n_out:]
        q_ref, k_ref, v_ref, b_ref, do_ref, o_ref, l_ref = ins[:7]
        s_ref = ins[7] if has_sink else None
        dl_ref = ins[n_in - 1] if has_dl else None
        dq_ref, dk_ref, dv_ref, db_ref = outs[:4]
        ks, vs, dks, dvs = scr[:4]
        scr = list(scr[4:])
        dsa = scr.pop(0) if has_sink else None
        if g.staged:
            qs, dos, os_, ls, dqs = scr[:5]
            dls = scr[5] if has_dl else None
            g.stage(qs, q_ref)
            g.stage(dos, do_ref)
            g.stage(os_, o_ref)
            g.stage(ls, l_ref)
            if has_dl:
                g.stage(dls, dl_ref)
        else:
            qs, dos, os_, ls, dqs, dls = None, do_ref, o_ref, l_ref, dq_ref, dl_ref
        h = pl.program_id(0)
        g.stage_kv(ks, k_ref)
        g.stage_kv(vs, v_ref)
        dks[...] = jnp.zeros_like(dks)
        dvs[...] = jnp.zeros_like(dvs)
        db_ref[...] = jnp.zeros_like(db_ref)
        bias_v = b_ref[...]
        if has_sink:
            sk = s_ref[h]
            dsa[...] = jnp.zeros_like(dsa)

        def chain(item):
            qoff, koff, edge = g.offsets(item)
            rows = pl.ds(qoff, g.q_rows)
            win = pl.ds(koff, g.win)
            qv = qs[rows, :] if g.staged else q_ref[rows, :].astype(BF16)
            kw_ = ks[win, :]
            vw_ = vs[win, :]
            sc = lax.dot_general(qv, kw_, (((1,), (1,)), ((), ())), preferred_element_type=F32) * ATTN_SCALE
            lv = ls[rows, :][:, 0:1]
            p = jnp.exp(sc + bias_v + edge - lv)
            dov = dos[rows, :]
            delta = jnp.sum(dov * os_[rows, :], axis=-1, keepdims=True)
            dob = dov.astype(BF16)
            dp = lax.dot_general(dob, vw_, (((1,), (1,)), ((), ())), preferred_element_type=F32)
            t = dp - delta
            if has_dl:
                t = t + dls[rows, :][:, 0:1]
            ds = p * t
            dsb = (ds * ATTN_SCALE).astype(BF16)
            dq = lax.dot_general(dsb, kw_, (((1,), (0,)), ((), ())), preferred_element_type=F32)
            dkc = lax.dot_general(dsb, qv, (((0,), (0,)), ((), ())), preferred_element_type=F32)
            dvc = lax.dot_general(p.astype(BF16), dob, (((0,), (0,)), ((), ())), preferred_element_type=F32)
            dsk = jnp.exp(sk - lv) * delta if has_sink else None
            return rows, win, dq, dkc, dvc, ds, dsk

        def step(i, carry):
            res = [chain(i * n_chains + u) for u in range(n_chains)]
            ds_sum = res[0][5]
            for rr in res[1:]:
                ds_sum = ds_sum + rr[5]
            db_ref[...] += ds_sum
            for rows, win, dq, dkc, dvc, ds, dsk in res:
                dqs[rows, :] = dq
                dks[win, :] += dkc
                dvs[win, :] += dvc
                if has_sink:
                    dsa[...] += dsk
            return carry

        lax.fori_loop(0, g.n_items // n_chains, step, 0)

        if g.staged:
            g.unstage(dq_ref, dqs)

        def emit_kv(add):
            for r in range(dil):
                lo = r * g.pad + half
                for dst_ref, src in ((dk_ref, dks), (dv_ref, dvs)):
                    val = src[lo:lo + g.seg, :]
                    if add:
                        val = val + dst_ref[g.rows_of(r), :]
                    dst_ref[g.rows_of(r), :] = val

        if group == 1:
            emit_kv(False)
        else:
            @pl.when(h % group == 0)
            def _():
                emit_kv(False)

            @pl.when(h % group != 0)
            def _():
                emit_kv(True)
        if has_sink:
            outs[4][...] = jnp.full((SUBLANES, LANES), -jnp.sum(dsa[...]), F32)

    def col(c0, per):
        return pl.BlockSpec((s, HEAD_DIM), lambda h: (0, c0 // LANES + h // per))

    b_spec = pl.BlockSpec((None, g.q_rows, g.win), lambda h: (h, 0, 0))
    in_specs = [col(cq, 1), col(ck, group), col(cv, group), b_spec, col(0, 1), col(0, 1), col(0, 1)]
    args = [proj, proj, proj, bias, dout, out, lse]
    if has_sink:
        in_specs.append(pl.BlockSpec(memory_space=pltpu.SMEM))
        args.append(sink)
    if has_dl:
        in_specs.append(col(0, 1))
        args.append(dlse)
    out_specs = [col(0, 1), col(0, group), col(0, group), b_spec]
    out_shape = [pltpu.HBM((s, nh * HEAD_DIM), F32), pltpu.HBM((s, nkv * HEAD_DIM), F32),
                 pltpu.HBM((s, nkv * HEAD_DIM), F32), jax.ShapeDtypeStruct((nh, g.q_rows, g.win), F32)]
    scratch = [pltpu.VMEM((dil * g.pad, HEAD_DIM), BF16), pltpu.VMEM((dil * g.pad, HEAD_DIM), BF16),
               pltpu.VMEM((dil * g.pad, HEAD_DIM), F32), pltpu.VMEM((dil * g.pad, HEAD_DIM), F32)]
    if has_sink:
        out_specs.append(pl.BlockSpec((None, SUBLANES, LANES), lambda h: (h, 0, 0)))
        out_shape.append(jax.ShapeDtypeStruct((nh, SUBLANES, LANES), F32))
        scratch.append(pltpu.VMEM((g.q_rows, 1), F32))
    if g.staged:
        scratch += [pltpu.VMEM((s, HEAD_DIM), BF16)] + [pltpu.VMEM((s, HEAD_DIM), F32)] * (4 + int(has_dl))
    res = pl.pallas_call(
        body, name=name, grid=(nh,), in_specs=in_specs, out_specs=out_specs, out_shape=out_shape,
        scratch_shapes=scratch, compiler_params=_params(("arbitrary",), 28 * s * HEAD_DIM * 4))(*args)
    return res[0], res[1], res[2], res[3], (res[4] if has_sink else None)


def dil_merge_fwd(name, outs, lses):
    s, w = outs[0].shape
    ts = _pick(s, ROW_TILE_CANDS)
    ng = len(outs)

    def body(*refs):
        o_refs, l_refs, y_ref = refs[:ng], refs[ng:2 * ng], refs[2 * ng]
        ls = [l[...] for l in l_refs]
        mx = ls[0]
        for l in ls[1:]:
            mx = jnp.maximum(mx, l)
        es = [jnp.exp(l - mx) for l in ls]
        tot = es[0]
        for e in es[1:]:
            tot = tot + e
        acc = (es[0] / tot) * o_refs[0][...]
        for e, o in zip(es[1:], o_refs[1:]):
            acc = acc + (e / tot) * o[...]
        y_ref[...] = acc.astype(y_ref.dtype)

    row = pl.BlockSpec((ts, w), lambda i: (i, 0))
    return pl.pallas_call(
        body, name=name, grid=(s // ts,), in_specs=[row] * (2 * ng), out_specs=row,
        out_shape=pltpu.HBM((s, w), BF16),
        compiler_params=_params(("parallel",), 10 * ts * w * 4))(*outs, *lses)


def dil_merge_bwd(name, dy, outs, lses):
    s, w = outs[0].shape
    ts = _pick(s, ROW_TILE_CANDS)
    ng = len(outs)
    nhead = w // HEAD_DIM

    def body(*refs):
        dy_ref = refs[0]
        o_refs, l_refs = refs[1:1 + ng], refs[1 + ng:1 + 2 * ng]
        do_refs, dl_refs = refs[1 + 2 * ng:1 + 3 * ng], refs[1 + 3 * ng:1 + 4 * ng]
        for hh in range(nhead):
            cols = slice(hh * HEAD_DIM, (hh + 1) * HEAD_DIM)
            dyv = dy_ref[:, cols]
            ls = [l[:, cols] for l in l_refs]
            mx = ls[0]
            for l in ls[1:]:
                mx = jnp.maximum(mx, l)
            es = [jnp.exp(l - mx) for l in ls]
            tot = es[0]
            for e in es[1:]:
                tot = tot + e
            alphas = [e / tot for e in es]
            dal = [jnp.broadcast_to(jnp.sum(dyv * o[:, cols], axis=-1, keepdims=True), dyv.shape) for o in o_refs]
            mean = alphas[0] * dal[0]
            for a, d in zip(alphas[1:], dal[1:]):
                mean = mean + a * d
            for g in range(ng):
                do_refs[g][:, cols] = alphas[g] * dyv
                dl_refs[g][:, cols] = alphas[g] * (dal[g] - mean)

    row = pl.BlockSpec((ts, w), lambda i: (i, 0))
    shape = pltpu.HBM((s, w), F32)
    res = pl.pallas_call(
        body, name=name, grid=(s // ts,), in_specs=[row] * (1 + 2 * ng), out_specs=[row] * (2 * ng),
        out_shape=[shape] * (2 * ng),
        compiler_params=_params(("parallel",), 16 * ts * w * 4))(dy, *outs, *lses)
    return res[:ng], res[ng:]


def _adamw(w, g, m, v):
    m = ADAM_B1 * m + (1.0 - ADAM_B1) * g
    v = ADAM_B2 * v + (1.0 - ADAM_B2) * (g * g)
    m_hat = m / (1.0 - ADAM_B1 ** ADAM_STEP)
    v_hat = v / (1.0 - ADAM_B2 ** ADAM_STEP)
    delta = -ADAM_LR * (m_hat / (jnp.sqrt(v_hat) + ADAM_EPS) + ADAM_WD * w)
    return delta, m, v


def _row_tile(r, c, budget=1 << 20):
    if r * c * 4 <= budget or r % SUBLANES:
        return r
    for t in (1024, 512, 256, 128, 64, 32, 16, 8):
        if r % t == 0 and t * c * 4 <= budget:
            return t
    return SUBLANES


def adam_small(name, g, w, m, v):
    def body(g_ref, w_ref, m_ref, v_ref, d_ref, nm_ref, nv_ref):
        d_ref[...], nm_ref[...], nv_ref[...] = _adamw(w_ref[...], g_ref[...], m_ref[...], v_ref[...])

    shape = jax.ShapeDtypeStruct(w.shape, F32)
    return pl.pallas_call(body, name=name, out_shape=[shape, shape, shape])(g, w, m, v)


def reduce_adam(name, mine, theirs, w, m, v):
    nq, r, c = mine.shape
    tr = _row_tile(r, c)

    def body(*refs):
        parts, (w_ref, m_ref, v_ref, g_ref, d_ref, nm_ref, nv_ref) = refs[:nq], refs[nq:]
        g = parts[0][...].astype(F32)
        for p_ref in parts[1:]:
            g = g + p_ref[...].astype(F32)
        g_ref[...] = g
        d_ref[...], nm_ref[...], nv_ref[...] = _adamw(w_ref[...], g, m_ref[...], v_ref[...])

    def slot(q):
        return pl.BlockSpec((None, tr, c), lambda i: (q, i, 0))

    row = pl.BlockSpec((tr, c), lambda i: (i, 0))
    shape = jax.ShapeDtypeStruct((r, c), F32)
    return pl.pallas_call(
        body, name=name, grid=(r // tr,), in_specs=[slot(q) for q in range(nq)] + [row, row, row],
        out_specs=[row] * 4, out_shape=[shape] * 4,
        compiler_params=_params(("parallel",), (nq * 2 + 7 * 4) * tr * c))(mine, *[theirs] * (nq - 1), w, m, v)


def _place():
    return lax.axis_index("x"), lax.axis_index("y"), lax.axis_index("c")


def _flip(pos, bits):
    return tuple((1 - p) if b else p for p, b in zip(pos, bits))


def _index(pos):
    return 4 * pos[0] + 2 * pos[1] + pos[2]


ANY = pl.BlockSpec(memory_space=pl.ANY)


HBM = pl.BlockSpec(memory_space=pltpu.HBM)
SEM = pl.BlockSpec(memory_space=pltpu.SEMAPHORE)
EFFECT = pltpu.SideEffectType.DATAFLOW_SIDE_EFFECTING
TO_SIBLING = (0, 0, 1)
TO_CHIPS = [(1, 0, 0), (0, 1, 0), (1, 1, 0)]


def _in_hbm(a):
    return pltpu.with_memory_space_constraint(a, pltpu.HBM)


def _token_value(token):
    return token[0, 0]


def _when(pred, fn):
    if pred is True:
        fn()
    elif pred is not False:
        pl.when(pred)(fn)


def _plan_copy(k, entry, ins, lnd, send_sems, recv_sems):
    a, src_a, sblk, lblk, to, send_if, recv_if = entry
    src = lnd[a] if src_a is None else ins[src_a]
    return pltpu.make_async_remote_copy(
        src_ref=src.at[sblk], dst_ref=lnd[a].at[lblk], send_sem=send_sems.at[k], recv_sem=recv_sems.at[k],
        device_id=to, device_id_type=MESH), send_if, recv_if


def split_start(name, srcs, lands, plan, after):
    ns, nl = len(srcs), len(lands)
    n_copies = len(plan((0, 0, 0)))

    def body(*refs):
        ins, lnd = refs[:ns], refs[ns:ns + nl]
        send_sems, recv_sems = refs[ns + nl + 1], refs[ns + nl + 2]
        token = refs[-1]
        for k, entry in enumerate(plan(_place())):
            cp, send_if, _ = _plan_copy(k, entry, ins, lnd, send_sems, recv_sems)
            _when(send_if, cp.start)
        token[...] = jnp.zeros_like(token)

    outs = pl.pallas_call(
        body, name=name,
        out_shape=(pltpu.SemaphoreType.DMA((n_copies,)), pltpu.SemaphoreType.DMA((n_copies,)),
                   *[pltpu.HBM(a.shape, a.dtype) for a in srcs], *[pltpu.HBM(a.shape, a.dtype) for a in lands],
                   jax.ShapeDtypeStruct((SUBLANES, LANES), F32)),
        in_specs=[HBM] * (ns + nl) + [ANY],
        out_specs=(SEM, SEM, *[HBM] * (ns + nl), pl.BlockSpec(memory_space=pltpu.VMEM)),
        input_output_aliases={i: 2 + i for i in range(ns + nl)},
        compiler_params=pltpu.CompilerParams(has_side_effects=EFFECT),
    )(*[_in_hbm(a) for a in srcs], *[_in_hbm(a) for a in lands], after)
    return outs[0], outs[1], list(outs[2:2 + ns]), list(outs[2 + ns:2 + ns + nl]), outs[-1]


def split_wait(name, send_sems, recv_sems, srcs, lands, plan, after):
    ns, nl = len(srcs), len(lands)

    def body(*refs):
        ins, lnd = refs[:ns], refs[ns:ns + nl]
        s_sems, r_sems = refs[ns + nl], refs[ns + nl + 1]
        for k, entry in enumerate(plan(_place())):
            cp, send_if, recv_if = _plan_copy(k, entry, ins, lnd, s_sems, r_sems)
            _when(send_if, cp.wait_send)
            _when(recv_if, cp.wait_recv)
        refs[-1][...] = jnp.zeros((SUBLANES, LANES), F32)

    outs = pl.pallas_call(
        body, name=name,
        out_shape=(*[pltpu.HBM(a.shape, a.dtype) for a in srcs], *[pltpu.HBM(a.shape, a.dtype) for a in lands],
                   jax.ShapeDtypeStruct((SUBLANES, LANES), F32)),
        in_specs=[HBM] * (ns + nl) + [SEM, SEM, ANY],
        out_specs=(*[HBM] * (ns + nl), pl.BlockSpec(memory_space=pltpu.VMEM)),
        input_output_aliases={i: i for i in range(ns + nl)},
        compiler_params=pltpu.CompilerParams(has_side_effects=EFFECT),
    )(*srcs, *lands, send_sems, recv_sems, after)
    return list(outs[:ns]), list(outs[ns:ns + nl]), outs[-1]


NORTH = 1


def ag_plan(n):
    def plan(me):
        x, y, c = me
        entries = []
        for a in range(n):
            for t in (NORTH, 1 - NORTH):
                blk = _index((x, y, t))
                for rel in TO_CHIPS:
                    entries.append((a, None, blk, blk, _flip((x, y, t), rel), c == NORTH, c == t))
        return entries
    return plan


def ag_pair(name, lands, after):
    n = len(lands)

    def body(*refs):
        lnd = refs[n + 1:2 * n + 1]
        token = refs[2 * n + 1]
        send_sems, recv_sems = refs[2 * n + 2:]
        token[...] = jnp.zeros_like(token)
        me = _place()
        sibling = _flip(me, TO_SIBLING)
        copies = []
        for a in range(n):
            mine, theirs = lnd[a].at[_index(me)], lnd[a].at[_index(sibling)]
            cp = pltpu.make_async_remote_copy(src_ref=mine, dst_ref=mine, send_sem=send_sems.at[a],
                                              recv_sem=recv_sems.at[a], device_id=sibling, device_id_type=MESH)
            cp.start()
            copies.append((cp, pltpu.make_async_remote_copy(
                src_ref=mine, dst_ref=theirs, send_sem=send_sems.at[a], recv_sem=recv_sems.at[a], device_id=sibling,
                device_id_type=MESH)))
        for cp, arrival in copies:
            arrival.wait_recv()
        for cp, arrival in copies:
            cp.wait_send()

    outs = pl.pallas_call(
        body, name=name, in_specs=[ANY] * (n + 1), out_specs=[ANY] * n + [pl.BlockSpec(memory_space=pltpu.VMEM)],
        out_shape=[jax.ShapeDtypeStruct(l.shape, l.dtype) for l in lands]
        + [jax.ShapeDtypeStruct((SUBLANES, LANES), F32)],
        input_output_aliases={a: a for a in range(n)},
        scratch_shapes=[pltpu.SemaphoreType.DMA((n,)), pltpu.SemaphoreType.DMA((n,))],
    )(*lands, after)
    return list(outs[:n]), outs[n]


def ag_start(name, lands, after):
    return split_start(name, [], lands, ag_plan(len(lands)), after)


def pass_plan(n):
    def plan(me):
        sibling = _flip(me, TO_SIBLING)
        return [(a, None, _index(_flip(me, rel)), _index(_flip(me, rel)), sibling, True, True)
                for a in range(n) for rel in TO_CHIPS]
    return plan


def ag_finish(name, lands):
    n = len(lands)

    def body(*refs):
        lnd = refs[n:2 * n]
        send_sems, recv_sems = refs[2 * n:]
        me = _place()
        sibling = _flip(me, TO_SIBLING)
        copies = []
        for a in range(n):
            for j, rel in enumerate(TO_CHIPS):
                blk = lnd[a].at[_index(_flip(me, rel))]
                there = lnd[a].at[_index(_flip(sibling, rel))]
                cp = pltpu.make_async_remote_copy(
                    src_ref=blk, dst_ref=blk, send_sem=send_sems.at[a * 3 + j], recv_sem=recv_sems.at[a * 3 + j],
                    device_id=sibling, device_id_type=MESH)
                cp.start()
                copies.append((cp, pltpu.make_async_remote_copy(
                    src_ref=blk, dst_ref=there, send_sem=send_sems.at[a * 3 + j], recv_sem=recv_sems.at[a * 3 + j],
                    device_id=sibling, device_id_type=MESH)))
        for cp, arrival in copies:
            arrival.wait_recv()
        for cp, arrival in copies:
            cp.wait_send()

    return pl.pallas_call(
        body, name=name, in_specs=[ANY] * n, out_specs=[ANY] * n,
        out_shape=[jax.ShapeDtypeStruct(l.shape, l.dtype) for l in lands],
        input_output_aliases={a: a for a in range(n)},
        scratch_shapes=[pltpu.SemaphoreType.DMA((3 * n,)), pltpu.SemaphoreType.DMA((3 * n,))],
    )(*lands)


REL = [(b >> 2 & 1, b >> 1 & 1, b & 1) for b in range(N_DEV)]


CHIP_REL = [(0, 0, 0)] + TO_CHIPS
N_CHIPS = len(CHIP_REL)


def rs_pair(name, parts):
    n = len(parts)

    def body(*refs):
        ins, got = refs[:n], refs[n:2 * n]
        send_sems, recv_sems = refs[2 * n:]
        me = _place()
        sibling = _flip(me, TO_SIBLING)
        remote = []
        for a in range(n):
            for q, rel in enumerate(CHIP_REL):
                k = a * N_CHIPS + q
                cp = pltpu.make_async_remote_copy(
                    src_ref=ins[a].at[_index(_flip(sibling, rel))], dst_ref=got[a].at[q], send_sem=send_sems.at[k],
                    recv_sem=recv_sems.at[k], device_id=sibling, device_id_type=MESH)
                cp.start()
                remote.append(cp)
        for cp in remote:
            cp.wait_recv()
        for cp in remote:
            cp.wait_send()

    shapes = [jax.ShapeDtypeStruct((N_CHIPS,) + tuple(p.shape[1:]), p.dtype) for p in parts]
    res = pl.pallas_call(
        body, name=name, in_specs=[ANY] * n, out_specs=[ANY] * n, out_shape=shapes,
        scratch_shapes=[pltpu.SemaphoreType.DMA((N_CHIPS * n,)), pltpu.SemaphoreType.DMA((N_CHIPS * n,))],
    )(*parts)
    return list(res)


def own_blocks():
    me = _place()
    return jnp.stack([_index(_flip(me, rel)) for rel in CHIP_REL]).astype(jnp.int32)


def pair_add(name, blocks, parts, got):
    nq, r, c = got.shape
    tr = _row_tile(r, c, budget=6 << 20)

    def body(blk_ref, a_ref, b_ref, o_ref):
        o_ref[...] = (a_ref[...].astype(F32) + b_ref[...].astype(F32)).astype(o_ref.dtype)

    spec = pl.BlockSpec((None, tr, c), lambda q, i, blk: (q, i, 0))
    return pl.pallas_call(
        body, name=name,
        grid_spec=pltpu.PrefetchScalarGridSpec(
            num_scalar_prefetch=1, grid=(nq, r // tr),
            in_specs=[pl.BlockSpec((None, tr, c), lambda q, i, blk: (blk[q], i, 0)), spec], out_specs=spec),
        out_shape=pltpu.HBM(got.shape, got.dtype),
        compiler_params=_params(("arbitrary", "arbitrary"), 6 * tr * c * 2))(blocks, parts, got)


def rs_pair_plan(n):
    def plan(me):
        sibling = _flip(me, TO_SIBLING)
        return [(a, a, _index(_flip(sibling, rel)), q, sibling, True, True)
                for a in range(n) for q, rel in enumerate(CHIP_REL)]
    return plan


def rs_plan(n):
    def plan(me):
        return [(a, a, q, q, _flip(me, CHIP_REL[q]), True, True) for a in range(n) for q in range(1, N_CHIPS)]
    return plan


def rs_start(name, sums, after):
    lands = [lax.empty(t.shape, t.dtype) for t in sums]
    return split_start(name, sums, lands, rs_plan(len(sums)), after)


def allreduce_small(name, pack, after):
    rows, lanes = pack.shape

    def body(x_ref, after_ref, o_ref, land, send_sems, recv_sems):
        me = _place()
        idx = _index(me)
        land[idx] = x_ref[...]
        copies = []
        for r in range(1, N_DEV):
            peer = _flip(me, REL[r])
            cp = pltpu.make_async_remote_copy(
                src_ref=x_ref, dst_ref=land.at[idx], send_sem=send_sems.at[r - 1], recv_sem=recv_sems.at[r - 1],
                device_id=peer, device_id_type=MESH)
            cp.start()
            copies.append(cp)
        for cp in copies:
            cp.wait_recv()
        for cp in copies:
            cp.wait_send()
        acc = land[0]
        for i in range(1, N_DEV):
            acc = acc + land[i]
        o_ref[...] = acc

    return pl.pallas_call(
        body, name=name, in_specs=[pl.BlockSpec(memory_space=pltpu.VMEM), ANY],
        out_specs=pl.BlockSpec(memory_space=pltpu.VMEM), out_shape=jax.ShapeDtypeStruct((rows, lanes), F32),
        scratch_shapes=[pltpu.VMEM((N_DEV, rows, lanes), F32), pltpu.SemaphoreType.DMA((7,)),
                        pltpu.SemaphoreType.DMA((7,))],
    )(pack, after)


def _pad_rows(a, rows):
    return jnp.pad(a, ((0, rows - a.shape[0]), (0, 0)))


def _as_tiles(vec):
    n = vec.shape[0]
    rows = -(-n // LANES)
    rows = -(-rows // SUBLANES) * SUBLANES
    return jnp.pad(vec, (0, rows * LANES - n)).reshape(rows, LANES)


def kernel(x, p, rel_bias_table, attn_norm, w_in, sink_a, w_branch_a, w_branch_b, w_out, ffn_norm, w_ffn_gate, w_ffn_up, conv_w, conv_b, w_ffn_down, ple_norm, w_ple_gate, w_ple_proj, final_norm, loss_target, m_rel_bias_table, m_attn_norm, m_w_in, m_sink_a, m_w_branch_a, m_w_branch_b, m_w_out, m_ffn_norm, m_w_ffn_gate, m_w_ffn_up, m_conv_w, m_conv_b, m_w_ffn_down, m_ple_norm, m_w_ple_gate, m_w_ple_proj, m_final_norm, v_rel_bias_table, v_attn_norm, v_w_in, v_sink_a, v_w_branch_a, v_w_branch_b, v_w_out, v_ffn_norm, v_w_ffn_gate, v_w_ffn_up, v_conv_w, v_conv_b, v_w_ffn_down, v_ple_norm, v_w_ple_gate, v_w_ple_proj, v_final_norm):
    xs = x[0]
    s, d = xs.shape
    ps = p[0, 0]
    target = loss_target[0]
    me = 4 * lax.axis_index("x") + 2 * lax.axis_index("y") + lax.axis_index("c")

    big = dict(w_in=w_in[0], w_branch_a=w_branch_a[0], w_branch_b=w_branch_b[0], w_out=w_out[0],
               w_ffn_gate=w_ffn_gate[0], w_ffn_up=w_ffn_up[0], w_ffn_down=w_ffn_down[0],
               w_ple_gate=w_ple_gate[0], w_ple_proj=w_ple_proj[0])
    big_m = dict(w_in=m_w_in[0], w_branch_a=m_w_branch_a[0], w_branch_b=m_w_branch_b[0], w_out=m_w_out[0],
                 w_ffn_gate=m_w_ffn_gate[0], w_ffn_up=m_w_ffn_up[0], w_ffn_down=m_w_ffn_down[0],
                 w_ple_gate=m_w_ple_gate[0], w_ple_proj=m_w_ple_proj[0])
    big_v = dict(w_in=v_w_in[0], w_branch_a=v_w_branch_a[0], w_branch_b=v_w_branch_b[0], w_out=v_w_out[0],
                 w_ffn_gate=v_w_ffn_gate[0], w_ffn_up=v_w_ffn_up[0], w_ffn_down=v_w_ffn_down[0],
                 w_ple_gate=v_w_ple_gate[0], w_ple_proj=v_w_ple_proj[0])
    names = list(big)
    nf = big["w_ffn_gate"].shape[1]

    shards = {k: big[k].astype(BF16) for k in names}
    shards["conv_w"] = _pad_rows(conv_w[0], SUBLANES)
    flipped = ("w_ffn_gate", "w_ffn_up")
    for k in flipped:
        big[k], big_m[k], big_v[k] = big[k].T, big_m[k].T, big_v[k].T
    ag_groups = [["w_in"], ["w_branch_a", "w_branch_b", "w_out"], ["w_ffn_gate", "conv_w"], ["w_ffn_up"],
                 ["w_ffn_down", "w_ple_gate", "w_ple_proj"]]
    ag_started = {}
    wg = {}

    ag_paired, ag_passing = {}, {}

    def pair(gi, after):
        lands = [lax.dynamic_update_index_in_dim(lax.empty((N_DEV,) + shards[k].shape, shards[k].dtype), shards[k],
                                                 me, 0) for k in ag_groups[gi]]
        ag_paired[gi], token = ag_pair(f"ag_pair{gi}", lands, after)
        return token

    def start(gi, after):
        s_sems, r_sems, _, lands, token = ag_start(f"ag_start{gi}", ag_paired[gi], after)
        ag_started[gi] = (s_sems, r_sems, lands)
        return token

    def landed(gi, after):
        s_sems, r_sems, lands = ag_started[gi]
        return split_wait(f"ag_wait{gi}", s_sems, r_sems, [], lands, ag_plan(len(lands)), after)[1:]

    def pass_on(gi, lands, after):
        s_sems, r_sems, _, lands, token = split_start(f"ag_pass{gi}", [], lands, pass_plan(len(lands)), after)
        ag_passing[gi] = (s_sems, r_sems, lands)
        return token

    def ready(gi, after):
        s_sems, r_sems, lands = ag_passing[gi]
        lands = split_wait(f"ag_ready{gi}", s_sems, r_sems, [], lands, pass_plan(len(lands)), after)[1]
        wg.update(zip(ag_groups[gi], lands))

    cb = conv_b.reshape(N_DEV, 1, nf)

    table_t = rel_bias_table.T
    geo_a = dict(half=A_BLOCK, q_rows=ATTN_Q_ROWS, n_chains=ATTN_CHAINS, dil=1, nh=A_Q_HEADS, group=A_GROUP,
                 cq=COL_QA, ck=COL_KA, cv=COL_VA)
    geo_b = [dict(half=B_BLOCK, q_rows=min(ATTN_Q_ROWS, s // dil), n_chains=ATTN_CHAINS, dil=dil,
                  nh=B_HEADS_PER_GROUP, group=1, cq=COL_QB + g * B_OUT_W, ck=COL_KB + g * B_OUT_W,
                  cv=COL_VB + g * B_OUT_W) for g, (_, dil) in enumerate(B_PATTERNS)]
    bucket_a = bucket_tile(geo_a["q_rows"], A_BLOCK, 1)
    bias_a = bias_build("bias_a", table_t, bucket_a, 0, A_Q_HEADS, A_BLOCK)
    buckets_b = [bucket_tile(gb["q_rows"], B_BLOCK, gb["dil"]) for gb in geo_b]
    biases_b = [bias_build(f"bias_b{g}", table_t, buckets_b[g], A_Q_HEADS + g * B_HEADS_PER_GROUP, B_HEADS_PER_GROUP,
                           B_BLOCK) for g in range(len(B_PATTERNS))]

    token = start(0, pair(0, xs))
    h = rms_fwd("rms_attn", xs, attn_norm + _token_value(token))
    lands0, token = landed(0, pair(4, pair(3, pair(2, pair(1, h)))))
    token = start(4, start(3, start(2, start(1, token))))
    wg["w_in"] = ag_finish("ag_finish0", lands0)[0]
    proj = mm_cols("proj_in", h, wg["w_in"], F32, fold=True, after=token)
    token = pass_on(1, landed(1, proj)[0], proj)
    sink = sink_a[0] + _token_value(token)
    ya, lse_a = band_attn_fwd("attn_a_fwd", proj, bias_a, sink, **geo_a)
    outs_b, lses_b = [], []
    for g in range(len(B_PATTERNS)):
        o, l = band_attn_fwd(f"attn_b{g}_fwd", proj, biases_b[g], None, **geo_b[g])
        outs_b.append(o)
        lses_b.append(l)
    yb = dil_merge_fwd("dil_merge_fwd", outs_b, lses_b)
    ready(1, yb)
    token = pass_on(2, landed(2, yb)[0], yb)
    w_out_full = wg["w_out"].reshape(d, d)
    ta = mm_cols("branch_a", ya, wg["w_branch_a"], F32, fold=True, after=token)
    tb = mm_cols("branch_b", yb, wg["w_branch_b"], F32, fold=True)
    merged = gate_merge_fwd("gate_merge_fwd", proj, ta, tb, d)
    x1 = mm_plain("mix_out", merged, w_out_full, F32, res=xs)

    hf = rms_fwd("rms_ffn", x1, ffn_norm)
    ready(2, hf)
    token = pass_on(3, landed(3, hf)[0], hf)
    cw = wg["conv_w"]
    gpre = mm_cols("ffn_gate", hf, wg["w_ffn_gate"], F32, fold=False, after=token)
    ready(3, gpre)
    token = pass_on(4, landed(4, gpre)[0], gpre)
    u = mm_cols("ffn_up", hf, wg["w_ffn_up"], F32, fold=False, after=token)
    z = ffn_mid_fwd("ffn_mid_fwd", gpre, u, cw, cb)
    ready(4, z)
    w_pg_full = wg["w_ple_gate"].reshape(d, d)
    x2 = mm_jsum("ffn_down", z, wg["w_ffn_down"], F32, res=x1)

    hp = rms_fwd("rms_ple", x2, ple_norm)
    lp = mm_plain("ple_gate", hp, w_pg_full, F32)
    pp = mm_cols("ple_proj", ps, wg["w_ple_proj"], F32, fold=True)
    loss_part, dx3, dlp, dpp, d_final = tail_fwd_bwd("tail", x2, lp, pp, final_norm.reshape(1, d), target)

    grads = {}
    rs_started = []
    blocks = own_blocks()

    exchanging = []

    def exchange(tag, keys):
        parts = [grads[k] for k in keys]
        lands = [lax.empty((N_CHIPS,) + tuple(p.shape[1:]), p.dtype) for p in parts]
        s_sems, r_sems, parts, lands, token = split_start(f"rs_pair_{tag}", parts, lands, rs_pair_plan(len(keys)), blocks)
        exchanging.append((tag, keys, s_sems, r_sems, parts, lands))
        return _token_value(token)

    def send(after):
        tag, keys, s_sems, r_sems, parts, lands = exchanging.pop(0)
        parts, got, _ = split_wait(f"rs_paired_{tag}", s_sems, r_sems, parts, lands, rs_pair_plan(len(keys)), after)
        return send_sums(tag, keys, parts, got)

    def send_sums(tag, keys, parts, got):
        sums = [pair_add(f"pair_add_{k}", blocks, p, g) for k, p, g in zip(keys, parts, got)]
        s_sems, r_sems, srcs, lands, token = rs_start(f"rs_start_{tag}", sums, blocks)
        rs_started.append((tag, keys, s_sems, r_sems, srcs, lands))
        return token

    grads["w_ple_proj"] = mm_tn_cols("d_w_ple_proj", ps, dpp, N_DEV, big["w_ple_proj"].shape[1], BF16, folded=True)
    grads["w_ple_gate"] = mm_tn_plain("d_w_ple_gate", hp, dlp, BF16).reshape(N_DEV, d // N_DEV, d)
    tok = exchange("ple", ["w_ple_proj", "w_ple_gate"])
    dhp = mm_nt_plain("d_hp", dlp, w_pg_full, F32)
    dx2, d_ple = rms_bwd("rms_ple_bwd", x2, ple_norm + tok, dhp, dx3)

    dz = mm_nt_j("d_z", dx2, wg["w_ffn_down"], BF16)
    grads["w_ffn_down"] = mm_tn_j("d_w_ffn_down", z, dx2, BF16)
    tok = _token_value(send(dz)) + exchange("down", ["w_ffn_down"])
    dg, du, dcw = ffn_mid_bwd1("ffn_mid_bwd1", gpre, u, dz, cw, cb + tok)
    dgpre = ffn_mid_bwd2("ffn_mid_bwd2", dg, cw)
    grads["w_ffn_up"] = mm_tn_j("d_w_ffn_up", du, hf, BF16)
    grads["w_ffn_gate"] = mm_tn_j("d_w_ffn_gate", dgpre, hf, BF16)
    dhf = mm_nt_jsum("d_hf_up", du, wg["w_ffn_up"], F32, folded=False)
    dhf = mm_nt_jsum("d_hf_gate", dgpre, wg["w_ffn_gate"], F32, folded=False, res=dhf)
    tok = _token_value(send(dhf)) + exchange("upgate", ["w_ffn_up", "w_ffn_gate"])
    dx1, d_ffn = rms_bwd("rms_ffn_bwd", x1, ffn_norm + tok, dhf, dx2)

    dmerged = mm_nt_plain("d_merged", dx1, w_out_full, F32)
    grads["w_out"] = mm_tn_plain("d_w_out", merged, dx1, BF16).reshape(N_DEV, d // N_DEV, d)
    dta, dtb, dga, dgb = gate_merge_bwd("gate_merge_bwd", dmerged, proj, ta, tb, d)
    grads["w_branch_a"] = mm_tn_cols("d_w_branch_a", ya, dta, N_DEV, big["w_branch_a"].shape[1], BF16, folded=True)
    grads["w_branch_b"] = mm_tn_cols("d_w_branch_b", yb, dtb, N_DEV, big["w_branch_b"].shape[1], BF16, folded=True)
    dya = mm_nt_jsum("d_ya", dta, wg["w_branch_a"], F32, folded=True)
    dyb = mm_nt_jsum("d_yb", dtb, wg["w_branch_b"], F32, folded=True)
    tok = _token_value(send(dyb)) + exchange("mix", ["w_out", "w_branch_a", "w_branch_b"])
    dqa, dka, dva, dbias_a, dsink = band_attn_bwd("attn_a_bwd", proj, bias_a, sink + tok, dya, ya, lse_a, None, **geo_a)
    douts_b, dlses_b = dil_merge_bwd("dil_merge_bwd", dyb, outs_b, lses_b)
    dq_b, dk_b, dv_b, dbias_b = [], [], [], []
    for g in range(len(B_PATTERNS)):
        dq, dk, dv, db, _ = band_attn_bwd(f"attn_b{g}_bwd", proj, biases_b[g], None, douts_b[g], outs_b[g], lses_b[g],
                                          dlses_b[g], **geo_b[g])
        dq_b.append(dq)
        dk_b.append(dk)
        dv_b.append(dv)
        dbias_b.append(db)
    dproj = jnp.concatenate([t.astype(BF16) for t in [dqa, dka, dva] + dq_b + dk_b + dv_b + [dga, dgb]], axis=1)
    token = send(dproj)
    grads["w_in"] = mm_tn_cols("d_w_in", h, dproj, N_DEV, big["w_in"].shape[1], BF16, folded=True, after=token)
    token = send_sums("in", ["w_in"], [grads["w_in"]], rs_pair("rs_pair_in", [grads["w_in"]]))
    dh = mm_nt_jsum("d_h", dproj, wg["w_in"], F32, folded=True, after=token)
    grad_x, d_attn = rms_bwd("rms_attn_bwd", xs, attn_norm, dh, dx1)

    dt_a = table_grad("table_grad_a", dbias_a, bucket_a)[:, 0, :N_BUCKETS]
    dt_b = [table_grad(f"table_grad_b{g}", dbias_b[g], buckets_b[g])[:, 0, :N_BUCKETS] for g in range(len(B_PATTERNS))]
    d_table_part = jnp.concatenate([dt_a] + dt_b, axis=0).T

    pieces = [
        ("loss", loss_part[0, :1]),
        ("table", d_table_part.reshape(-1)),
        ("attn_norm", d_attn.reshape(-1)),
        ("sink", dsink[:, 0, 0]),
        ("ffn_norm", d_ffn.reshape(-1)),
        ("conv_w", dcw[:, 0:3, :].reshape(-1)),
        ("conv_b", dcw[:, 3, :].reshape(-1)),
        ("ple_norm", d_ple.reshape(-1)),
        ("final_norm", d_final.reshape(-1)),
    ]
    tiles = [_as_tiles(v) for _, v in pieces]
    pack = jnp.concatenate(tiles, axis=0)

    out_g, out_d, out_m, out_v = {}, {}, {}, {}

    def finish(group, after):
        tag, keys, s_sems, r_sems, srcs, lands = group
        srcs, lands, _ = split_wait(f"rs_wait_{tag}", s_sems, r_sems, srcs, lands, rs_plan(len(keys)), after)
        for k, mine, theirs in zip(keys, srcs, lands):
            res = reduce_adam("adam_" + k, mine, theirs, big[k], big_m[k], big_v[k])
            after = res[1]
            out_g[k], out_d[k], out_m[k], out_v[k] = [(t.T if k in flipped else t)[None] for t in res]
        return after

    after = pack
    for group in rs_started[:-1]:
        after = finish(group, after)
    total = allreduce_small("allreduce_small", pack, after)
    finish(rs_started[-1], total)
    small = {}
    row = 0
    for (nm, v), t in zip(pieces, tiles):
        small[nm] = total[row:row + t.shape[0]].reshape(-1)[:v.shape[0]]
        row += t.shape[0]
    loss = small["loss"][0]
    g_small = dict(
        rel_bias_table=small["table"].reshape(rel_bias_table.shape),
        attn_norm=small["attn_norm"].reshape(attn_norm.shape),
        sink_a=small["sink"].reshape(sink_a.shape),
        ffn_norm=small["ffn_norm"].reshape(ffn_norm.shape),
        conv_w=lax.dynamic_index_in_dim(small["conv_w"].reshape(N_DEV, 3, nf), me, 0, keepdims=False)[None],
        conv_b=small["conv_b"].reshape(conv_b.shape),
        ple_norm=small["ple_norm"].reshape(ple_norm.shape),
        final_norm=small["final_norm"].reshape(1, d),
    )
    w_small = dict(rel_bias_table=(rel_bias_table, m_rel_bias_table, v_rel_bias_table),
                   attn_norm=(attn_norm, m_attn_norm, v_attn_norm), sink_a=(sink_a, m_sink_a, v_sink_a),
                   ffn_norm=(ffn_norm, m_ffn_norm, v_ffn_norm), conv_w=(conv_w, m_conv_w, v_conv_w),
                   conv_b=(conv_b, m_conv_b, v_conv_b), ple_norm=(ple_norm, m_ple_norm, v_ple_norm),
                   final_norm=(final_norm, m_final_norm, v_final_norm))

    for k, (wv, mv, vv) in w_small.items():
        shape = wv.shape
        two_d = (1, shape[0]) if len(shape) == 1 else ((shape[0] * shape[1], shape[2]) if len(shape) == 3 else shape)
        gk = g_small[k].reshape(two_d)
        dl, nm, nv = adam_small("adam_" + k, gk, wv.reshape(two_d), mv.reshape(two_d), vv.reshape(two_d))
        out_g[k], out_d[k], out_m[k], out_v[k] = gk.reshape(shape), dl.reshape(shape), nm.reshape(shape), nv.reshape(shape)

    order = ["rel_bias_table", "attn_norm", "w_in", "sink_a", "w_branch_a", "w_branch_b", "w_out", "ffn_norm",
             "w_ffn_gate", "w_ffn_up", "conv_w", "conv_b", "w_ffn_down", "ple_norm", "w_ple_gate", "w_ple_proj",
             "final_norm"]
    return (loss, grad_x[None], *[out_g[k] for k in order], *[out_d[k] for k in order],
            *[out_m[k] for k in order], *[out_v[k] for k in order])
```

```python
import math

import jax
import jax.numpy as jnp
from jax import lax
from jax.experimental import pallas as pl
from jax.experimental.pallas import tpu as pltpu

F32 = jnp.float32
BF16 = jnp.bfloat16
MESH = pl.DeviceIdType.MESH
N_DEV = 8

HEAD_DIM = 128
A_Q_HEADS = 8
A_KV_HEADS = 2
A_GROUP = A_Q_HEADS // A_KV_HEADS
A_BLOCK = 128
B_PATTERNS = ((128, 1), (512, 4), (2048, 16))
B_HEADS_PER_GROUP = 4
B_HEADS = len(B_PATTERNS) * B_HEADS_PER_GROUP
B_BLOCK = 64
N_BUCKETS = 32
MAX_DISTANCE = 1024
A_Q_W = A_Q_HEADS * HEAD_DIM
A_KV_W = A_KV_HEADS * HEAD_DIM
B_W = B_HEADS * HEAD_DIM
B_OUT_W = B_HEADS_PER_GROUP * HEAD_DIM
COL_QA = 0
COL_KA = COL_QA + A_Q_W
COL_VA = COL_KA + A_KV_W
COL_QB = COL_VA + A_KV_W
COL_KB = COL_QB + B_W
COL_VB = COL_KB + B_W
COL_GATES = COL_VB + B_W
RMS_EPS = 1e-6
NEG_INF = -1e30
ATTN_SCALE = HEAD_DIM ** -0.5
ATTN_Q_ROWS = 256
ATTN_CHAINS = 4

ADAM_LR = 0.001
ADAM_B1 = 0.9
ADAM_B2 = 0.999
ADAM_EPS = 1e-08
ADAM_WD = 0.01
ADAM_STEP = 10

GELU_C = math.sqrt(2.0 / math.pi)
GELU_A = 0.044715

V7X_VMEM_BYTES = 64 * 1024 * 1024
VMEM_CEILING = V7X_VMEM_BYTES - 8 * 1024 * 1024
LANES = 128
SUBLANES = 8


def _pick(n, cands):
    for c in cands:
        if n % c == 0:
            return c
    return n


def _nbytes(shape, dtype):
    n = 1
    for d in shape:
        if d is not None:
            n *= d
    return n * jnp.dtype(dtype).itemsize


def _params(sem, est_bytes):
    limit = int(min(VMEM_CEILING, max(32 * 1024 * 1024, 2 * est_bytes + (8 << 20))))
    return pltpu.CompilerParams(dimension_semantics=sem, vmem_limit_bytes=limit)


def _mm(name, a, b, a_bs, a_im, b_bs, b_im, out_shape, out_dtype, o_bs, o_im, grid, dims,
        res=None, r_bs=None, r_im=None, after=None):
    nk = grid[-1]
    nax = len(grid)
    has_res = res is not None
    has_after = after is not None
    o_tile = tuple(d for d in o_bs if d is not None)

    def body(*refs):
        a_ref, b_ref = refs[:2]
        r_ref = refs[2] if has_res else None
        n_in = 2 + has_res + has_after
        o_ref = refs[n_in]
        rest = refs[n_in + 1:]

        def prod():
            return lax.dot_general(a_ref[...].astype(BF16), b_ref[...].astype(BF16), (dims, ((), ())),
                                   preferred_element_type=F32)

        def finish(r):
            if r_ref is not None:
                r = r + r_ref[...].astype(F32)
            o_ref[...] = r.astype(o_ref.dtype)

        if nk == 1:
            finish(prod())
        else:
            acc = rest[0]
            k = pl.program_id(nax - 1)

            @pl.when(k == 0)
            def _():
                acc[...] = prod()

            @pl.when(k > 0)
            def _():
                acc[...] += prod()

            @pl.when(k == nk - 1)
            def _():
                finish(acc[...])

    in_specs = [pl.BlockSpec(a_bs, a_im), pl.BlockSpec(b_bs, b_im)]
    args = [a, b]
    est = _nbytes(a_bs, a.dtype) + _nbytes(b_bs, b.dtype) + _nbytes(o_bs, out_dtype) + 2 * _nbytes(o_tile, F32)
    if has_res:
        in_specs.append(pl.BlockSpec(r_bs, r_im))
        args.append(res)
        est += _nbytes(r_bs, res.dtype)
    if has_after:
        in_specs.append(pl.BlockSpec(memory_space=pl.ANY))
        args.append(after)
    scratch = [] if nk == 1 else [pltpu.VMEM(o_tile, F32)]
    sem = ("parallel",) * (nax - 1) + ("arbitrary",)
    return pl.pallas_call(
        body, name=name, grid=grid, in_specs=in_specs, out_specs=pl.BlockSpec(o_bs, o_im),
        out_shape=pltpu.HBM(out_shape, out_dtype), scratch_shapes=scratch,
        compiler_params=_params(sem, est))(*args)


TM_CANDS = (1024, 512, 256, 128, 64, 32, 16, 8)
TM_WIDE_CANDS = (2048,) + TM_CANDS
TK_CANDS = (1024, 512, 256, 128)
TN_CANDS = (1024, 512, 256, 128)


def mm_cols(name, a, wg, out_dtype, fold, after=None):
    m, k = a.shape
    nj, _, n = wg.shape
    tm, tk = _pick(m, TM_WIDE_CANDS), _pick(k, TK_CANDS)
    grid = (nj, m // tm, k // tk)
    if fold:
        shape, o_bs, o_im = (m, nj * n), (tm, n), (lambda j, i, kk: (i, j))
    else:
        shape, o_bs, o_im = (nj, m, n), (None, tm, n), (lambda j, i, kk: (j, i, 0))
    return _mm(name, a, wg, (tm, tk), lambda j, i, kk: (i, kk), (None, tk, n), lambda j, i, kk: (j, kk, 0),
               shape, out_dtype, o_bs, o_im, grid, ((1,), (0,)), after=after)


def mm_plain(name, a, w, out_dtype, res=None):
    m, k = a.shape
    n = w.shape[1]
    tm, tk, tn = _pick(m, TM_CANDS), _pick(k, TK_CANDS), _pick(n, TN_CANDS)
    grid = (n // tn, m // tm, k // tk)
    return _mm(name, a, w, (tm, tk), lambda j, i, kk: (i, kk), (tk, tn), lambda j, i, kk: (kk, j),
               (m, n), out_dtype, (tm, tn), lambda j, i, kk: (i, j), grid, ((1,), (0,)),
               res, (tm, tn), lambda j, i, kk: (i, j))


def mm_jsum(name, aj, wg, out_dtype, res=None):
    nj, m, ka = aj.shape
    n = wg.shape[2]
    tm, tn = _pick(m, TM_CANDS), _pick(n, TN_CANDS)
    grid = (m // tm, n // tn, nj)
    return _mm(name, aj, wg, (None, tm, ka), lambda i, jn, j: (j, i, 0), (None, ka, tn), lambda i, jn, j: (j, 0, jn),
               (m, n), out_dtype, (tm, tn), lambda i, jn, j: (i, jn), grid, ((1,), (0,)),
               res, (tm, tn), lambda i, jn, j: (i, jn))


def mm_tn_cols(name, a, g, nj, n, out_dtype, folded, after=None):
    s, kw = a.shape
    ts, tkw = _pick(s, TK_CANDS), _pick(kw, TM_CANDS)
    grid = (nj, kw // tkw, s // ts)
    if folded:
        g_bs, g_im = (ts, n), (lambda j, i, ss: (ss, j))
    else:
        g_bs, g_im = (None, ts, n), (lambda j, i, ss: (j, ss, 0))
    return _mm(name, a, g, (ts, tkw), lambda j, i, ss: (ss, i), g_bs, g_im,
               (nj, kw, n), out_dtype, (None, tkw, n), lambda j, i, ss: (j, i, 0), grid, ((0,), (0,)), after=after)


def mm_tn_plain(name, a, g, out_dtype):
    s, kw = a.shape
    n = g.shape[1]
    ts, tkw, tn = _pick(s, TK_CANDS), _pick(kw, TM_CANDS), _pick(n, TN_CANDS)
    grid = (kw // tkw, n // tn, s // ts)
    return _mm(name, a, g, (ts, tkw), lambda i, jn, ss: (ss, i), (ts, tn), lambda i, jn, ss: (ss, jn),
               (kw, n), out_dtype, (tkw, tn), lambda i, jn, ss: (i, jn), grid, ((0,), (0,)))


def mm_tn_j(name, aj, g, out_dtype):
    nj, s, ka = aj.shape
    n = g.shape[1]
    ts, tn = _pick(s, TK_CANDS), _pick(n, TN_CANDS)
    grid = (nj, n // tn, s // ts)
    return _mm(name, aj, g, (None, ts, ka), lambda j, jn, ss: (j, ss, 0), (ts, tn), lambda j, jn, ss: (ss, jn),
               (nj, ka, n), out_dtype, (None, ka, tn), lambda j, jn, ss: (j, 0, jn), grid, ((0,), (0,)))


def mm_nt_plain(name, g, w, out_dtype):
    m, n = g.shape
    k = w.shape[0]
    tm, tn, tkk = _pick(m, TM_CANDS), _pick(n, TK_CANDS), _pick(k, TN_CANDS)
    grid = (k // tkk, m // tm, n // tn)
    return _mm(name, g, w, (tm, tn), lambda kk, i, jn: (i, jn), (tkk, tn), lambda kk, i, jn: (kk, jn),
               (m, k), out_dtype, (tm, tkk), lambda kk, i, jn: (i, kk), grid, ((1,), (1,)))


def mm_nt_j(name, g, wg, out_dtype):
    m, n = g.shape
    nj, ka, _ = wg.shape
    tm, tn = _pick(m, TM_CANDS), _pick(n, TK_CANDS)
    grid = (nj, m // tm, n // tn)
    return _mm(name, g, wg, (tm, tn), lambda j, i, jn: (i, jn), (None, ka, tn), lambda j, i, jn: (j, 0, jn),
               (nj, m, ka), out_dtype, (None, tm, ka), lambda j, i, jn: (j, i, 0), grid, ((1,), (1,)))


def mm_nt_jsum(name, g, wg, out_dtype, folded, res=None, after=None):
    nj, k, n = wg.shape
    m = g.shape[0] if folded else g.shape[1]
    tm, tkk = _pick(m, TM_CANDS if res is not None else TM_WIDE_CANDS), _pick(k, TN_CANDS)
    grid = (m // tm, k // tkk, nj)
    if folded:
        g_bs, g_im = (tm, n), (lambda i, kk, j: (i, j))
    else:
        g_bs, g_im = (None, tm, n), (lambda i, kk, j: (j, i, 0))
    return _mm(name, g, wg, g_bs, g_im, (None, tkk, n), lambda i, kk, j: (j, kk, 0),
               (m, k), out_dtype, (tm, tkk), lambda i, kk, j: (i, kk), grid, ((1,), (1,)),
               res, (tm, tkk), lambda i, kk, j: (i, kk), after=after)


ROW_TILE_CANDS = (256, 128, 64, 32, 16, 8)


def _rstd(x):
    return lax.rsqrt(jnp.mean(x * x, axis=-1, keepdims=True) + RMS_EPS)


def _sigmoid(t):
    return 1.0 / (1.0 + jnp.exp(-t))


def rms_fwd(name, x, gain):
    s, d = x.shape
    ts = _pick(s, ROW_TILE_CANDS)

    def body(x_ref, g_ref, h_ref):
        xv = x_ref[...]
        h_ref[...] = ((xv * _rstd(xv)) * g_ref[...]).astype(h_ref.dtype)

    return pl.pallas_call(
        body, name=name, grid=(s // ts,),
        in_specs=[pl.BlockSpec((ts, d), lambda i: (i, 0)), pl.BlockSpec((1, d), lambda i: (0, 0))],
        out_specs=pl.BlockSpec((ts, d), lambda i: (i, 0)),
        out_shape=pltpu.HBM((s, d), BF16),
        compiler_params=_params(("parallel",), 3 * ts * d * 4))(x, gain)


def rms_bwd(name, x, gain, dh, dres, bf16_copy):
    s, d = x.shape
    ts = _pick(s, ROW_TILE_CANDS)

    def body(x_ref, g_ref, dh_ref, dr_ref, dx_ref, *rest):
        dxb_ref, dg_ref = rest if bf16_copy else (None, rest[0])
        xv = x_ref[...]
        r = _rstd(xv)
        xhat = xv * r
        dhv = dh_ref[...].astype(F32)
        dxhat = dhv * g_ref[...]
        dx = dr_ref[...] + r * (dxhat - xhat * jnp.mean(dxhat * xhat, axis=-1, keepdims=True))
        dx_ref[...] = dx
        if bf16_copy:
            dxb_ref[...] = dx.astype(dxb_ref.dtype)
        part = jnp.sum(dhv * xhat, axis=0, keepdims=True)

        @pl.when(pl.program_id(0) == 0)
        def _():
            dg_ref[...] = part

        @pl.when(pl.program_id(0) > 0)
        def _():
            dg_ref[...] += part

    row = pl.BlockSpec((ts, d), lambda i: (i, 0))
    vec = pl.BlockSpec((1, d), lambda i: (0, 0))
    copy_spec, copy_shape = ([row], [pltpu.HBM((s, d), BF16)]) if bf16_copy else ([], [])
    res = pl.pallas_call(
        body, name=name, grid=(s // ts,), in_specs=[row, vec, row, row], out_specs=[row] + copy_spec + [vec],
        out_shape=[pltpu.HBM((s, d), F32)] + copy_shape + [jax.ShapeDtypeStruct((1, d), F32)],
        compiler_params=_params(("arbitrary",), 7 * ts * d * 4))(x, gain, dh, dres)
    return (res[0], res[1], res[2]) if bf16_copy else (res[0], None, res[1])


def gate_merge_fwd(name, proj, ta, tb, d):
    s = proj.shape[0]
    ts = _pick(s, ROW_TILE_CANDS)
    cb = COL_GATES // d

    def body(ga_ref, gb_ref, ta_ref, tb_ref, o_ref):
        o_ref[...] = (_sigmoid(ga_ref[...]) * ta_ref[...] + _sigmoid(gb_ref[...]) * tb_ref[...]).astype(o_ref.dtype)

    row = pl.BlockSpec((ts, d), lambda i: (i, 0))
    return pl.pallas_call(
        body, name=name, grid=(s // ts,),
        in_specs=[pl.BlockSpec((ts, d), lambda i: (i, cb)), pl.BlockSpec((ts, d), lambda i: (i, cb + 1)), row, row],
        out_specs=row, out_shape=pltpu.HBM((s, d), BF16),
        compiler_params=_params(("parallel",), 5 * ts * d * 4))(proj, proj, ta, tb)


def gate_merge_bwd(name, dmerged, proj, ta, tb, d):
    s = proj.shape[0]
    ts = _pick(s, ROW_TILE_CANDS)
    cb = COL_GATES // d

    def body(dm_ref, ga_ref, gb_ref, ta_ref, tb_ref, dta_ref, dtb_ref, dga_ref, dgb_ref):
        dm = dm_ref[...]
        sa = _sigmoid(ga_ref[...])
        sb = _sigmoid(gb_ref[...])
        dta_ref[...] = (dm * sa).astype(dta_ref.dtype)
        dtb_ref[...] = (dm * sb).astype(dtb_ref.dtype)
        dga_ref[...] = (dm * ta_ref[...] * (sa * (1.0 - sa))).astype(dga_ref.dtype)
        dgb_ref[...] = (dm * tb_ref[...] * (sb * (1.0 - sb))).astype(dgb_ref.dtype)

    row = pl.BlockSpec((ts, d), lambda i: (i, 0))
    out = pltpu.HBM((s, d), BF16)
    return pl.pallas_call(
        body, name=name, grid=(s // ts,),
        in_specs=[row, pl.BlockSpec((ts, d), lambda i: (i, cb)), pl.BlockSpec((ts, d), lambda i: (i, cb + 1)), row, row],
        out_specs=[row, row, row, row], out_shape=[out, out, out, out],
        compiler_params=_params(("parallel",), 8 * ts * d * 4))(dmerged, proj, proj, ta, tb)


def tail_fwd_bwd(name, x2, lp, pp, gain, target):
    s, d = x2.shape
    ts = _pick(s, ROW_TILE_CANDS)

    def body(x2_ref, lp_ref, pp_ref, g_ref, t_ref, loss_ref, dx3_ref, dlp_ref, dpp_ref, dg_ref):
        gp = _sigmoid(lp_ref[...])
        ppv = pp_ref[...]
        x3 = x2_ref[...] + gp * ppv
        r = _rstd(x3)
        xhat = x3 * r
        gv = g_ref[...]
        err = xhat * gv - t_ref[...]
        loss = jnp.sum(err * err) * (0.5 / d)
        dy = err * (1.0 / d)
        dxhat = dy * gv
        dx3 = r * (dxhat - xhat * jnp.mean(dxhat * xhat, axis=-1, keepdims=True))
        dx3_ref[...] = dx3
        dlp_ref[...] = (dx3 * ppv * (gp * (1.0 - gp))).astype(dlp_ref.dtype)
        dpp_ref[...] = (dx3 * gp).astype(dpp_ref.dtype)
        part = jnp.sum(dy * xhat, axis=0, keepdims=True)
        lossv = jnp.full((1, LANES), loss, F32)

        @pl.when(pl.program_id(0) == 0)
        def _():
            dg_ref[...] = part
            loss_ref[...] = lossv

        @pl.when(pl.program_id(0) > 0)
        def _():
            dg_ref[...] += part
            loss_ref[...] += lossv

    row = pl.BlockSpec((ts, d), lambda i: (i, 0))
    vec = pl.BlockSpec((1, d), lambda i: (0, 0))
    return pl.pallas_call(
        body, name=name, grid=(s // ts,), in_specs=[row, row, row, vec, row],
        out_specs=[pl.BlockSpec((1, LANES), lambda i: (0, 0)), row, row, row, vec],
        out_shape=[jax.ShapeDtypeStruct((1, LANES), F32), pltpu.HBM((s, d), F32),
                   pltpu.HBM((s, d), BF16), pltpu.HBM((s, d), BF16),
                   jax.ShapeDtypeStruct((1, d), F32)],
        compiler_params=_params(("arbitrary",), 9 * ts * d * 4))(x2, lp, pp, gain, target)


HALO = SUBLANES
BF16_ROWS = 2 * SUBLANES


def _shift_rows(cur, prev_row, next_row):
    ts = cur.shape[0]
    rid = lax.broadcasted_iota(jnp.int32, cur.shape, 0)
    down = jnp.where(rid == 0, prev_row, pltpu.roll(cur, 1, 0))
    up = jnp.where(rid == ts - 1, next_row, pltpu.roll(cur, ts - 1, 0))
    return down, up


def _halo_specs(ts, s, nf, halo=HALO):
    nb = ts // halo
    last = s // halo - 1
    cur = pl.BlockSpec((None, ts, nf), lambda j, i: (j, i, 0))
    prev = pl.BlockSpec((None, halo, nf), lambda j, i: (j, jnp.maximum(i * nb - 1, 0), 0))
    nxt = pl.BlockSpec((None, halo, nf), lambda j, i: (j, jnp.minimum((i + 1) * nb, last), 0))
    return cur, prev, nxt


def _halo_rows(prev_ref, next_ref, n_tiles):
    i = pl.program_id(1)
    prev_row = jnp.where(i == 0, 0.0, prev_ref[HALO - 1:HALO, :].astype(F32))
    next_row = jnp.where(i == n_tiles - 1, 0.0, next_ref[0:1, :].astype(F32))
    return prev_row, next_row


def _gelu(g):
    t = jnp.tanh(GELU_C * (g + GELU_A * (g * g * g)))
    return 0.5 * g * (1.0 + t), t


def _conv(cur, down, up, cw_ref, cb_ref):
    return down * cw_ref[0:1, :] + cur * cw_ref[1:2, :] + up * cw_ref[2:3, :] + cb_ref[...]


def ffn_mid_fwd(name, gpre, u, cw, cb):
    nj, s, nf = gpre.shape
    ts = _pick(s, (512, 256, 128, 64, 32, 16, 8))
    n_tiles = s // ts
    cur, prev, nxt = _halo_specs(ts, s, nf)

    def body(g_ref, gp_ref, gn_ref, u_ref, cw_ref, cb_ref, z_ref):
        gv = g_ref[...]
        down, up = _shift_rows(gv, *_halo_rows(gp_ref, gn_ref, n_tiles))
        act, _ = _gelu(_conv(gv, down, up, cw_ref, cb_ref))
        z_ref[...] = (act * u_ref[...]).astype(z_ref.dtype)

    return pl.pallas_call(
        body, name=name, grid=(nj, n_tiles),
        in_specs=[cur, prev, nxt, cur, pl.BlockSpec((None, SUBLANES, nf), lambda j, i: (j, 0, 0)),
                  pl.BlockSpec((None, 1, nf), lambda j, i: (j, 0, 0))],
        out_specs=cur, out_shape=pltpu.HBM((nj, s, nf), BF16),
        compiler_params=_params(("parallel", "parallel"), 8 * ts * nf * 4))(gpre, gpre, gpre, u, cw, cb)


def _gelu_grad(g, t):
    return 0.5 * (1.0 + t) + 0.5 * g * (1.0 - t * t) * (GELU_C * (1.0 + 3.0 * GELU_A * (g * g)))


def ffn_mid_bwd(name, gpre, u, dz, cw, cb):
    nj, s, nf = gpre.shape
    ts = _pick(s, (512, 256, 128, 64, 32, 16, 8))
    n_tiles = s // ts
    cur, prev, nxt = _halo_specs(ts, s, nf)

    def body(g_ref, gp_ref, gn_ref, u_ref, up_ref, un_ref, dz_ref, dzp_ref, dzn_ref, cw_ref, cb_ref,
             du_ref, dgp_ref, dcw_ref):
        i = pl.program_id(1)
        w0, w1, w2, bias = cw_ref[0:1, :], cw_ref[1:2, :], cw_ref[2:3, :], cb_ref[...]
        gv = g_ref[...]
        down, up = _shift_rows(gv, *_halo_rows(gp_ref, gn_ref, n_tiles))
        gc = down * w0 + gv * w1 + up * w2 + bias
        act, t = _gelu(gc)
        dzv = dz_ref[...].astype(F32)
        du_ref[...] = (dzv * act).astype(du_ref.dtype)
        dg = dzv * u_ref[...] * _gelu_grad(gc, t)

        def edge_dg(g_before, g_at, g_after, u_at, dz_at):
            ge = g_before.astype(F32) * w0 + g_at.astype(F32) * w1 + g_after.astype(F32) * w2 + bias
            return dz_at.astype(F32) * u_at.astype(F32) * _gelu_grad(ge, _gelu(ge)[1])

        dz_before = dzp_ref[...].astype(F32)[BF16_ROWS - 1:BF16_ROWS, :]
        dz_after = dzn_ref[...].astype(F32)[0:1, :]
        dg_prev = jnp.where(i == 0, 0.0, edge_dg(gp_ref[HALO - 2:HALO - 1, :], gp_ref[HALO - 1:HALO, :], gv[0:1, :],
                                                 up_ref[HALO - 1:HALO, :], dz_before))
        dg_next = jnp.where(i == n_tiles - 1, 0.0, edge_dg(gv[ts - 1:ts, :], gn_ref[0:1, :], gn_ref[1:2, :],
                                                           un_ref[0:1, :], dz_after))
        dg_down, dg_up = _shift_rows(dg, dg_prev, dg_next)
        dgp_ref[...] = (dg_up * w0 + dg * w1 + dg_down * w2).astype(dgp_ref.dtype)
        rows = [jnp.sum(dg * down, axis=0, keepdims=True), jnp.sum(dg * gv, axis=0, keepdims=True),
                jnp.sum(dg * up, axis=0, keepdims=True), jnp.sum(dg, axis=0, keepdims=True)]
        part = jnp.concatenate(rows + [jnp.zeros((SUBLANES - len(rows), nf), F32)], axis=0)

        @pl.when(i == 0)
        def _():
            dcw_ref[...] = part

        @pl.when(i > 0)
        def _():
            dcw_ref[...] += part

    small = pl.BlockSpec((None, SUBLANES, nf), lambda j, i: (j, 0, 0))
    return pl.pallas_call(
        body, name=name, grid=(nj, n_tiles),
        in_specs=[cur, prev, nxt] * 2 + list(_halo_specs(ts, s, nf, BF16_ROWS))
        + [small, pl.BlockSpec((None, 1, nf), lambda j, i: (j, 0, 0))],
        out_specs=[cur, cur, small],
        out_shape=[pltpu.HBM((nj, s, nf), BF16), pltpu.HBM((nj, s, nf), BF16),
                   jax.ShapeDtypeStruct((nj, SUBLANES, nf), F32)],
        compiler_params=_params(("parallel", "arbitrary"), 14 * ts * nf * 4))(
            gpre, gpre, gpre, u, u, u, dz, dz, dz, cw, cb)


def _t5_bucket(rel):
    half = N_BUCKETS // 2
    max_exact = half // 2
    n = jnp.abs(rel)
    side = jnp.where(rel > 0, half, 0)
    nf = jnp.maximum(n, 1).astype(F32)
    large = max_exact + (jnp.log(nf / max_exact) / math.log(MAX_DISTANCE / max_exact)
                         * (half - max_exact)).astype(jnp.int32)
    large = jnp.minimum(large, half - 1)
    return side + jnp.where(n < max_exact, n, large)


def bucket_tile(rows, half, dil):
    rel = (jnp.arange(rows + 2 * half)[None, :] - half) - jnp.arange(rows)[:, None]
    return _t5_bucket(rel * dil).astype(jnp.int32)


def bias_build(name, table_t, bucket, h0, nh, half):
    blk, kw = bucket.shape

    def body(t_ref, b_ref, o_ref):
        h = pl.program_id(0)
        bv = b_ref[...]
        acc = jnp.zeros((blk, kw), F32)
        for b in range(N_BUCKETS):
            acc = jnp.where(bv == b, t_ref[h0 + h, b], acc)
        qi = lax.broadcasted_iota(jnp.int32, (blk, kw), 0)
        ci = lax.broadcasted_iota(jnp.int32, (blk, kw), 1)
        o_ref[...] = jnp.where(jnp.abs(ci - half - qi) <= half, acc, NEG_INF)

    return pl.pallas_call(
        body, name=name, grid=(nh,),
        in_specs=[pl.BlockSpec(memory_space=pltpu.SMEM), pl.BlockSpec((blk, kw), lambda h: (0, 0))],
        out_specs=pl.BlockSpec((None, blk, kw), lambda h: (h, 0, 0)),
        out_shape=jax.ShapeDtypeStruct((nh, blk, kw), F32),
        compiler_params=_params(("parallel",), 4 * blk * kw * 4))(table_t, bucket)


def table_grad(name, dbias, bucket):
    nh, blk, kw = dbias.shape

    def body(d_ref, b_ref, o_ref):
        bv = b_ref[...]
        dv = d_ref[...]
        lane = lax.broadcasted_iota(jnp.int32, (SUBLANES, LANES), 1)
        acc = jnp.zeros((SUBLANES, LANES), F32)
        for b in range(N_BUCKETS):
            acc = jnp.where(lane == b, jnp.sum(jnp.where(bv == b, dv, 0.0)), acc)
        o_ref[...] = acc

    return pl.pallas_call(
        body, name=name, grid=(nh,),
        in_specs=[pl.BlockSpec((None, blk, kw), lambda h: (h, 0, 0)), pl.BlockSpec((blk, kw), lambda h: (0, 0))],
        out_specs=pl.BlockSpec((None, SUBLANES, LANES), lambda h: (h, 0, 0)),
        out_shape=jax.ShapeDtypeStruct((nh, SUBLANES, LANES), F32),
        compiler_params=_params(("parallel",), 4 * blk * kw * 4))(dbias, bucket)


class _Band:
    def __init__(self, s, half, q_rows, n_chains, dil):
        self.s, self.half, self.dil, self.n_chains = s, half, dil, n_chains
        self.seg = s // dil
        self.q_rows = min(q_rows, self.seg)
        self.win = self.q_rows + 2 * half
        self.pad = self.seg + 2 * half
        self.nsb = self.seg // self.q_rows
        self.n_items = dil * self.nsb
        assert self.n_items % n_chains == 0 and self.seg % self.q_rows == 0
        self.staged = dil > 1

    def rows_of(self, r):
        return pl.ds(r, self.seg, stride=self.dil) if self.dil > 1 else slice(None)

    def stage_kv(self, dst, src_ref):
        zeros = jnp.zeros((self.half, HEAD_DIM), dst.dtype)
        for r in range(self.dil):
            base = r * self.pad
            dst[base:base + self.half, :] = zeros
            dst[base + self.half + self.seg:base + self.pad, :] = zeros
            dst[base + self.half:base + self.half + self.seg, :] = src_ref[self.rows_of(r), :].astype(dst.dtype)

    def stage(self, dst, src_ref):
        for r in range(self.dil):
            dst[r * self.seg:(r + 1) * self.seg, :] = src_ref[self.rows_of(r), :].astype(dst.dtype)

    def unstage(self, dst_ref, src, add=False):
        for r in range(self.dil):
            val = src[r * self.seg:(r + 1) * self.seg, :].astype(dst_ref.dtype)
            if add:
                val = val + dst_ref[self.rows_of(r), :]
            dst_ref[self.rows_of(r), :] = val

    def offsets(self, item):
        r, sb = item // self.nsb, item % self.nsb
        qoff = pl.multiple_of(r * self.seg + sb * self.q_rows, self.q_rows)
        koff = pl.multiple_of(r * self.pad + sb * self.q_rows, B_BLOCK)
        kpos = sb * self.q_rows - self.half + lax.broadcasted_iota(jnp.int32, (1, self.win), 1)
        edge = jnp.where((kpos >= 0) & (kpos < self.seg), 0.0, NEG_INF)
        return qoff, koff, edge


def band_attn_fwd(name, proj, bias, sink, *, half, q_rows, n_chains, dil, nh, group, cq, ck, cv):
    s, w = proj.shape
    g = _Band(s, half, q_rows, n_chains, dil)
    has_sink = sink is not None

    def body(*refs):
        q_ref, k_ref, v_ref, b_ref = refs[:4]
        s_ref = refs[4] if has_sink else None
        o_ref, l_ref, ks, vs = refs[4 + has_sink:8 + has_sink]
        qs, os_, ls = refs[8 + has_sink:] if g.staged else (None, o_ref, l_ref)
        g.stage_kv(ks, k_ref)
        g.stage_kv(vs, v_ref)
        if g.staged:
            g.stage(qs, q_ref)
        bias_v = b_ref[...]
        sk = s_ref[pl.program_id(0)] if has_sink else None

        def chain(item):
            qoff, koff, edge = g.offsets(item)
            rows = pl.ds(qoff, g.q_rows)
            qv = qs[rows, :] if g.staged else q_ref[rows, :].astype(BF16)
            kw_ = ks[pl.ds(koff, g.win), :]
            vw_ = vs[pl.ds(koff, g.win), :]
            sc = lax.dot_general(qv, kw_, (((1,), (1,)), ((), ())), preferred_element_type=F32) * ATTN_SCALE
            sc = sc + bias_v + edge
            m = jnp.max(sc, axis=-1, keepdims=True)
            if has_sink:
                m = jnp.maximum(m, sk)
            p = jnp.exp(sc - m)
            den = jnp.sum(p, axis=-1, keepdims=True)
            if has_sink:
                den = den + jnp.exp(sk - m)
            out = lax.dot_general(p.astype(BF16), vw_, (((1,), (0,)), ((), ())), preferred_element_type=F32)
            return rows, out / den, jnp.broadcast_to(m + jnp.log(den), (g.q_rows, HEAD_DIM))

        def step(i, carry):
            for rows, out, lse in [chain(i * n_chains + u) for u in range(n_chains)]:
                os_[rows, :] = out
                ls[rows, :] = lse
            return carry

        lax.fori_loop(0, g.n_items // n_chains, step, 0)
        if g.staged:
            g.unstage(o_ref, os_)
            g.unstage(l_ref, ls)

    def col(c0, per):
        return pl.BlockSpec((s, HEAD_DIM), lambda h: (0, c0 // LANES + h // per))

    in_specs = [col(cq, 1), col(ck, group), col(cv, group),
                pl.BlockSpec((None, g.q_rows, g.win), lambda h: (h, 0, 0))]
    args = [proj, proj, proj, bias]
    if has_sink:
        in_specs.append(pl.BlockSpec(memory_space=pltpu.SMEM))
        args.append(sink)
    shape = pltpu.HBM((s, nh * HEAD_DIM), F32)
    scratch = [pltpu.VMEM((dil * g.pad, HEAD_DIM), BF16), pltpu.VMEM((dil * g.pad, HEAD_DIM), BF16)]
    if g.staged:
        scratch += [pltpu.VMEM((s, HEAD_DIM), BF16), pltpu.VMEM((s, HEAD_DIM), F32), pltpu.VMEM((s, HEAD_DIM), F32)]
    return pl.pallas_call(
        body, name=name, grid=(nh,), in_specs=in_specs, out_specs=[col(0, 1), col(0, 1)], out_shape=[shape, shape],
        scratch_shapes=scratch, compiler_params=_params(("parallel",), 16 * s * HEAD_DIM * 4))(*args)


def band_attn_bwd(name, proj, bias, sink, dout, out, lse, dlse, *, half, q_rows, n_chains, dil, nh, group, cq, ck, cv):
    s, w = proj.shape
    g = _Band(s, half, q_rows, n_chains, dil)
    nkv = nh // group
    has_sink = sink is not None
    has_dl = dlse is not None
    n_in = 7 + int(has_sink) + int(has_dl)
    n_out = 4 + int(has_sink)

    def body(*refs):
        ins, outs, scr = refs[:n_in], refs[n_in:n_in + n_out], refs[n_in + n_out:]
        q_ref, k_ref, v_ref, b_ref, do_ref, o_ref, l_ref = ins[:7]
        s_ref = ins[7] if has_sink else None
        dl_ref = ins[n_in - 1] if has_dl else None
        dq_ref, dk_ref, dv_ref, db_ref = outs[:4]
        ks, vs, dks, dvs = scr[:4]
        scr = list(scr[4:])
        dsa = scr.pop(0) if has_sink else None
        if g.staged:
            qs, dos, os_, ls, dqs = scr[:5]
            dls = scr[5] if has_dl else None
            g.stage(qs, q_ref)
            g.stage(dos, do_ref)
            g.stage(os_, o_ref)
            g.stage(ls, l_ref)
            if has_dl:
                g.stage(dls, dl_ref)
        else:
            qs, dos, os_, ls, dqs, dls = None, do_ref, o_ref, l_ref, dq_ref, dl_ref
        h = pl.program_id(0)
        g.stage_kv(ks, k_ref)
        g.stage_kv(vs, v_ref)
        dks[...] = jnp.zeros_like(dks)
        dvs[...] = jnp.zeros_like(dvs)
        db_ref[...] = jnp.zeros_like(db_ref)
        bias_v = b_ref[...]
        if has_sink:
            sk = s_ref[h]
            dsa[...] = jnp.zeros_like(dsa)

        def chain(item):
            qoff, koff, edge = g.offsets(item)
            rows = pl.ds(qoff, g.q_rows)
            win = pl.ds(koff, g.win)
            qv = qs[rows, :] if g.staged else q_ref[rows, :].astype(BF16)
            kw_ = ks[win, :]
            vw_ = vs[win, :]
            sc = lax.dot_general(qv, kw_, (((1,), (1,)), ((), ())), preferred_element_type=F32) * ATTN_SCALE
            lv = ls[rows, :][:, 0:1]
            p = jnp.exp(sc + bias_v + edge - lv)
            dov = dos[rows, :]
            delta = jnp.sum(dov * os_[rows, :], axis=-1, keepdims=True)
            dob = dov.astype(BF16)
            dp = lax.dot_general(dob, vw_, (((1,), (1,)), ((), ())), preferred_element_type=F32)
            t = dp - delta
            if has_dl:
                t = t + dls[rows, :][:, 0:1]
            ds = p * t
            dsb = (ds * ATTN_SCALE).astype(BF16)
            dq = lax.dot_general(dsb, kw_, (((1,), (0,)), ((), ())), preferred_element_type=F32)
            dkc = lax.dot_general(dsb, qv, (((0,), (0,)), ((), ())), preferred_element_type=F32)
            dvc = lax.dot_general(p.astype(BF16), dob, (((0,), (0,)), ((), ())), preferred_element_type=F32)
            dsk = jnp.exp(sk - lv) * delta if has_sink else None
            return rows, win, dq, dkc, dvc, ds, dsk

        def step(i, carry):
            res = [chain(i * n_chains + u) for u in range(n_chains)]
            ds_sum = res[0][5]
            for rr in res[1:]:
                ds_sum = ds_sum + rr[5]
            db_ref[...] += ds_sum
            for rows, win, dq, dkc, dvc, ds, dsk in res:
                dqs[rows, :] = dq
                dks[win, :] += dkc
                dvs[win, :] += dvc
                if has_sink:
                    dsa[...] += dsk
            return carry

        lax.fori_loop(0, g.n_items // n_chains, step, 0)

        if g.staged:
            g.unstage(dq_ref, dqs)

        def emit_kv(add):
            for r in range(dil):
                lo = r * g.pad + half
                for dst_ref, src in ((dk_ref, dks), (dv_ref, dvs)):
                    val = src[lo:lo + g.seg, :]
                    if add:
                        val = val + dst_ref[g.rows_of(r), :]
                    dst_ref[g.rows_of(r), :] = val

        if group == 1:
            emit_kv(False)
        else:
            @pl.when(h % group == 0)
            def _():
                emit_kv(False)

            @pl.when(h % group != 0)
            def _():
                emit_kv(True)
        if has_sink:
            outs[4][...] = jnp.full((SUBLANES, LANES), -jnp.sum(dsa[...]), F32)

    def col(c0, per):
        return pl.BlockSpec((s, HEAD_DIM), lambda h: (0, c0 // LANES + h // per))

    b_spec = pl.BlockSpec((None, g.q_rows, g.win), lambda h: (h, 0, 0))
    in_specs = [col(cq, 1), col(ck, group), col(cv, group), b_spec, col(0, 1), col(0, 1), col(0, 1)]
    args = [proj, proj, proj, bias, dout, out, lse]
    if has_sink:
        in_specs.append(pl.BlockSpec(memory_space=pltpu.SMEM))
        args.append(sink)
    if has_dl:
        in_specs.append(col(0, 1))
        args.append(dlse)
    out_specs = [col(0, 1), col(0, group), col(0, group), b_spec]
    out_shape = [pltpu.HBM((s, nh * HEAD_DIM), F32), pltpu.HBM((s, nkv * HEAD_DIM), F32),
                 pltpu.HBM((s, nkv * HEAD_DIM), F32), jax.ShapeDtypeStruct((nh, g.q_rows, g.win), F32)]
    scratch = [pltpu.VMEM((dil * g.pad, HEAD_DIM), BF16), pltpu.VMEM((dil * g.pad, HEAD_DIM), BF16),
               pltpu.VMEM((dil * g.pad, HEAD_DIM), F32), pltpu.VMEM((dil * g.pad, HEAD_DIM), F32)]
    if has_sink:
        out_specs.append(pl.BlockSpec((None, SUBLANES, LANES), lambda h: (h, 0, 0)))
        out_shape.append(jax.ShapeDtypeStruct((nh, SUBLANES, LANES), F32))
        scratch.append(pltpu.VMEM((g.q_rows, 1), F32))
    if g.staged:
        scratch += [pltpu.VMEM((s, HEAD_DIM), BF16)] + [pltpu.VMEM((s, HEAD_DIM), F32)] * (4 + int(has_dl))
    res = pl.pallas_call(
        body, name=name, grid=(nh,), in_specs=in_specs, out_specs=out_specs, out_shape=out_shape,
        scratch_shapes=scratch, compiler_params=_params(("arbitrary",), 28 * s * HEAD_DIM * 4))(*args)
    return res[0], res[1], res[2], res[3], (res[4] if has_sink else None)


def dil_merge_fwd(name, outs, lses):
    s, w = outs[0].shape
    ts = _pick(s, ROW_TILE_CANDS)
    ng = len(outs)

    def body(*refs):
        o_refs, l_refs, y_ref = refs[:ng], refs[ng:2 * ng], refs[2 * ng]
        ls = [l[...] for l in l_refs]
        mx = ls[0]
        for l in ls[1:]:
            mx = jnp.maximum(mx, l)
        es = [jnp.exp(l - mx) for l in ls]
        tot = es[0]
        for e in es[1:]:
            tot = tot + e
        acc = (es[0] / tot) * o_refs[0][...]
        for e, o in zip(es[1:], o_refs[1:]):
            acc = acc + (e / tot) * o[...]
        y_ref[...] = acc.astype(y_ref.dtype)

    row = pl.BlockSpec((ts, w), lambda i: (i, 0))
    return pl.pallas_call(
        body, name=name, grid=(s // ts,), in_specs=[row] * (2 * ng), out_specs=row,
        out_shape=pltpu.HBM((s, w), BF16),
        compiler_params=_params(("parallel",), 10 * ts * w * 4))(*outs, *lses)


def dil_merge_bwd(name, dy, outs, lses):
    s, w = outs[0].shape
    ts = _pick(s, ROW_TILE_CANDS)
    ng = len(outs)
    nhead = w // HEAD_DIM

    def body(*refs):
        dy_ref = refs[0]
        o_refs, l_refs = refs[1:1 + ng], refs[1 + ng:1 + 2 * ng]
        do_refs, dl_refs = refs[1 + 2 * ng:1 + 3 * ng], refs[1 + 3 * ng:1 + 4 * ng]
        for hh in range(nhead):
            cols = slice(hh * HEAD_DIM, (hh + 1) * HEAD_DIM)
            dyv = dy_ref[:, cols]
            ls = [l[:, cols] for l in l_refs]
            mx = ls[0]
            for l in ls[1:]:
                mx = jnp.maximum(mx, l)
            es = [jnp.exp(l - mx) for l in ls]
            tot = es[0]
            for e in es[1:]:
                tot = tot + e
            alphas = [e / tot for e in es]
            dal = [jnp.broadcast_to(jnp.sum(dyv * o[:, cols], axis=-1, keepdims=True), dyv.shape) for o in o_refs]
            mean = alphas[0] * dal[0]
            for a, d in zip(alphas[1:], dal[1:]):
                mean = mean + a * d
            for g in range(ng):
                do_refs[g][:, cols] = alphas[g] * dyv
                dl_refs[g][:, cols] = alphas[g] * (dal[g] - mean)

    row = pl.BlockSpec((ts, w), lambda i: (i, 0))
    shape = pltpu.HBM((s, w), F32)
    res = pl.pallas_call(
        body, name=name, grid=(s // ts,), in_specs=[row] * (1 + 2 * ng), out_specs=[row] * (2 * ng),
        out_shape=[shape] * (2 * ng),
        compiler_params=_params(("parallel",), 16 * ts * w * 4))(dy, *outs, *lses)
    return res[:ng], res[ng:]


def _adamw(w, g, m, v):
    m = ADAM_B1 * m + (1.0 - ADAM_B1) * g
    v = ADAM_B2 * v + (1.0 - ADAM_B2) * (g * g)
    m_hat = m / (1.0 - ADAM_B1 ** ADAM_STEP)
    v_hat = v / (1.0 - ADAM_B2 ** ADAM_STEP)
    delta = -ADAM_LR * (m_hat / (jnp.sqrt(v_hat) + ADAM_EPS) + ADAM_WD * w)
    return delta, m, v


def _row_tile(r, c, budget=1 << 20):
    if r * c * 4 <= budget or r % SUBLANES:
        return r
    for t in (1024, 512, 256, 128, 64, 32, 16, 8):
        if r % t == 0 and t * c * 4 <= budget:
            return t
    return SUBLANES


def adam_small(name, g, w, m, v):
    def body(g_ref, w_ref, m_ref, v_ref, d_ref, nm_ref, nv_ref):
        d_ref[...], nm_ref[...], nv_ref[...] = _adamw(w_ref[...], g_ref[...], m_ref[...], v_ref[...])

    shape = jax.ShapeDtypeStruct(w.shape, F32)
    return pl.pallas_call(body, name=name, out_shape=[shape, shape, shape])(g, w, m, v)


def reduce_adam(name, mine, theirs, w, m, v):
    nq, r, c = mine.shape
    tr = _row_tile(r, c)

    def body(*refs):
        parts, (w_ref, m_ref, v_ref, g_ref, d_ref, nm_ref, nv_ref) = refs[:nq], refs[nq:]
        g = parts[0][...].astype(F32)
        for p_ref in parts[1:]:
            g = g + p_ref[...].astype(F32)
        g_ref[...] = g
        d_ref[...], nm_ref[...], nv_ref[...] = _adamw(w_ref[...], g, m_ref[...], v_ref[...])

    def slot(q):
        return pl.BlockSpec((None, tr, c), lambda i: (q, i, 0))

    row = pl.BlockSpec((tr, c), lambda i: (i, 0))
    shape = jax.ShapeDtypeStruct((r, c), F32)
    return pl.pallas_call(
        body, name=name, grid=(r // tr,), in_specs=[slot(q) for q in range(nq)] + [row, row, row],
        out_specs=[row] * 4, out_shape=[shape] * 4,
        compiler_params=_params(("parallel",), (nq * 2 + 7 * 4) * tr * c))(mine, *[theirs] * (nq - 1), w, m, v)


def _place():
    return lax.axis_index("x"), lax.axis_index("y"), lax.axis_index("c")


def _flip(pos, bits):
    return tuple((1 - p) if b else p for p, b in zip(pos, bits))


def _index(pos):
    return 4 * pos[0] + 2 * pos[1] + pos[2]


ANY = pl.BlockSpec(memory_space=pl.ANY)


HBM = pl.BlockSpec(memory_space=pltpu.HBM)
SEM = pl.BlockSpec(memory_space=pltpu.SEMAPHORE)
EFFECT = pltpu.SideEffectType.DATAFLOW_SIDE_EFFECTING
TO_SIBLING = (0, 0, 1)
TO_CHIPS = [(1, 0, 0), (0, 1, 0), (1, 1, 0)]


def _in_hbm(a):
    return pltpu.with_memory_space_constraint(a, pltpu.HBM)


def _token_value(token):
    return token[0, 0]


def _when(pred, fn):
    if pred is True:
        fn()
    elif pred is not False:
        pl.when(pred)(fn)


def _plan_copy(k, entry, ins, lnd, send_sems, recv_sems):
    a, src_a, sblk, lblk, to, send_if, recv_if = entry
    src = lnd[a] if src_a is None else ins[src_a]
    return pltpu.make_async_remote_copy(
        src_ref=src.at[sblk], dst_ref=lnd[a].at[lblk], send_sem=send_sems.at[k], recv_sem=recv_sems.at[k],
        device_id=to, device_id_type=MESH), send_if, recv_if


def split_start(name, srcs, lands, plan, after):
    ns, nl = len(srcs), len(lands)
    n_copies = len(plan((0, 0, 0)))

    def body(*refs):
        ins, lnd = refs[:ns], refs[ns:ns + nl]
        send_sems, recv_sems = refs[ns + nl + 1], refs[ns + nl + 2]
        token = refs[-1]
        for k, entry in enumerate(plan(_place())):
            cp, send_if, _ = _plan_copy(k, entry, ins, lnd, send_sems, recv_sems)
            _when(send_if, cp.start)
        token[...] = jnp.zeros_like(token)

    outs = pl.pallas_call(
        body, name=name,
        out_shape=(pltpu.SemaphoreType.DMA((n_copies,)), pltpu.SemaphoreType.DMA((n_copies,)),
                   *[pltpu.HBM(a.shape, a.dtype) for a in srcs], *[pltpu.HBM(a.shape, a.dtype) for a in lands],
                   jax.ShapeDtypeStruct((SUBLANES, LANES), F32)),
        in_specs=[HBM] * (ns + nl) + [ANY],
        out_specs=(SEM, SEM, *[HBM] * (ns + nl), pl.BlockSpec(memory_space=pltpu.VMEM)),
        input_output_aliases={i: 2 + i for i in range(ns + nl)},
        compiler_params=pltpu.CompilerParams(has_side_effects=EFFECT),
    )(*[_in_hbm(a) for a in srcs], *[_in_hbm(a) for a in lands], after)
    return outs[0], outs[1], list(outs[2:2 + ns]), list(outs[2 + ns:2 + ns + nl]), outs[-1]


def split_wait(name, send_sems, recv_sems, srcs, lands, plan, after):
    ns, nl = len(srcs), len(lands)

    def body(*refs):
        ins, lnd = refs[:ns], refs[ns:ns + nl]
        s_sems, r_sems = refs[ns + nl], refs[ns + nl + 1]
        for k, entry in enumerate(plan(_place())):
            cp, send_if, recv_if = _plan_copy(k, entry, ins, lnd, s_sems, r_sems)
            _when(send_if, cp.wait_send)
            _when(recv_if, cp.wait_recv)
        refs[-1][...] = jnp.zeros((SUBLANES, LANES), F32)

    outs = pl.pallas_call(
        body, name=name,
        out_shape=(*[pltpu.HBM(a.shape, a.dtype) for a in srcs], *[pltpu.HBM(a.shape, a.dtype) for a in lands],
                   jax.ShapeDtypeStruct((SUBLANES, LANES), F32)),
        in_specs=[HBM] * (ns + nl) + [SEM, SEM, ANY],
        out_specs=(*[HBM] * (ns + nl), pl.BlockSpec(memory_space=pltpu.VMEM)),
        input_output_aliases={i: i for i in range(ns + nl)},
        compiler_params=pltpu.CompilerParams(has_side_effects=EFFECT),
    )(*srcs, *lands, send_sems, recv_sems, after)
    return list(outs[:ns]), list(outs[ns:ns + nl]), outs[-1]


NORTH = 1


def ag_plan(n):
    def plan(me):
        x, y, c = me
        entries = []
        for a in range(n):
            for t in (NORTH, 1 - NORTH):
                blk = _index((x, y, t))
                for rel in TO_CHIPS:
                    entries.append((a, None, blk, blk, _flip((x, y, t), rel), c == NORTH, c == t))
        return entries
    return plan


def ag_pair(name, lands, after):
    n = len(lands)

    def body(*refs):
        lnd = refs[n + 1:2 * n + 1]
        token = refs[2 * n + 1]
        send_sems, recv_sems = refs[2 * n + 2:]
        token[...] = jnp.zeros_like(token)
        me = _place()
        sibling = _flip(me, TO_SIBLING)
        copies = []
        for a in range(n):
            mine, theirs = lnd[a].at[_index(me)], lnd[a].at[_index(sibling)]
            cp = pltpu.make_async_remote_copy(src_ref=mine, dst_ref=mine, send_sem=send_sems.at[a],
                                              recv_sem=recv_sems.at[a], device_id=sibling, device_id_type=MESH)
            cp.start()
            copies.append((cp, pltpu.make_async_remote_copy(
                src_ref=mine, dst_ref=theirs, send_sem=send_sems.at[a], recv_sem=recv_sems.at[a], device_id=sibling,
                device_id_type=MESH)))
        for cp, arrival in copies:
            arrival.wait_recv()
        for cp, arrival in copies:
            cp.wait_send()

    outs = pl.pallas_call(
        body, name=name, in_specs=[ANY] * (n + 1), out_specs=[ANY] * n + [pl.BlockSpec(memory_space=pltpu.VMEM)],
        out_shape=[jax.ShapeDtypeStruct(l.shape, l.dtype) for l in lands]
        + [jax.ShapeDtypeStruct((SUBLANES, LANES), F32)],
        input_output_aliases={a: a for a in range(n)},
        scratch_shapes=[pltpu.SemaphoreType.DMA((n,)), pltpu.SemaphoreType.DMA((n,))],
    )(*lands, after)
    return list(outs[:n]), outs[n]


def ag_start(name, lands, after):
    return split_start(name, [], lands, ag_plan(len(lands)), after)


def pass_plan(n):
    def plan(me):
        sibling = _flip(me, TO_SIBLING)
        return [(a, None, _index(_flip(me, rel)), _index(_flip(me, rel)), sibling, True, True)
                for a in range(n) for rel in TO_CHIPS]
    return plan


def ag_finish(name, lands):
    n = len(lands)

    def body(*refs):
        lnd = refs[n:2 * n]
        send_sems, recv_sems = refs[2 * n:]
        me = _place()
        sibling = _flip(me, TO_SIBLING)
        copies = []
        for a in range(n):
            for j, rel in enumerate(TO_CHIPS):
                blk = lnd[a].at[_index(_flip(me, rel))]
                there = lnd[a].at[_index(_flip(sibling, rel))]
                cp = pltpu.make_async_remote_copy(
                    src_ref=blk, dst_ref=blk, send_sem=send_sems.at[a * 3 + j], recv_sem=recv_sems.at[a * 3 + j],
                    device_id=sibling, device_id_type=MESH)
                cp.start()
                copies.append((cp, pltpu.make_async_remote_copy(
                    src_ref=blk, dst_ref=there, send_sem=send_sems.at[a * 3 + j], recv_sem=recv_sems.at[a * 3 + j],
                    device_id=sibling, device_id_type=MESH)))
        for cp, arrival in copies:
            arrival.wait_recv()
        for cp, arrival in copies:
            cp.wait_send()

    return pl.pallas_call(
        body, name=name, in_specs=[ANY] * n, out_specs=[ANY] * n,
        out_shape=[jax.ShapeDtypeStruct(l.shape, l.dtype) for l in lands],
        input_output_aliases={a: a for a in range(n)},
        scratch_shapes=[pltpu.SemaphoreType.DMA((3 * n,)), pltpu.SemaphoreType.DMA((3 * n,))],
    )(*lands)


REL = [(b >> 2 & 1, b >> 1 & 1, b & 1) for b in range(N_DEV)]


CHIP_REL = [(0, 0, 0)] + TO_CHIPS
N_CHIPS = len(CHIP_REL)


def rs_pair(name, parts):
    n = len(parts)

    def body(*refs):
        ins, got = refs[:n], refs[n:2 * n]
        send_sems, recv_sems = refs[2 * n:]
        me = _place()
        sibling = _flip(me, TO_SIBLING)
        remote = []
        for a in range(n):
            for q, rel in enumerate(CHIP_REL):
                k = a * N_CHIPS + q
                cp = pltpu.make_async_remote_copy(
                    src_ref=ins[a].at[_index(_flip(sibling, rel))], dst_ref=got[a].at[q], send_sem=send_sems.at[k],
                    recv_sem=recv_sems.at[k], device_id=sibling, device_id_type=MESH)
                cp.start()
                remote.append(cp)
        for cp in remote:
            cp.wait_recv()
        for cp in remote:
            cp.wait_send()

    shapes = [jax.ShapeDtypeStruct((N_CHIPS,) + tuple(p.shape[1:]), p.dtype) for p in parts]
    res = pl.pallas_call(
        body, name=name, in_specs=[ANY] * n, out_specs=[ANY] * n, out_shape=shapes,
        scratch_shapes=[pltpu.SemaphoreType.DMA((N_CHIPS * n,)), pltpu.SemaphoreType.DMA((N_CHIPS * n,))],
    )(*parts)
    return list(res)


def own_blocks():
    me = _place()
    return jnp.stack([_index(_flip(me, rel)) for rel in CHIP_REL]).astype(jnp.int32)


def pair_add(name, blocks, parts, got):
    nq, r, c = got.shape
    tr = _row_tile(r, c, budget=6 << 20)

    def body(blk_ref, a_ref, b_ref, o_ref):
        o_ref[...] = (a_ref[...].astype(F32) + b_ref[...].astype(F32)).astype(o_ref.dtype)

    spec = pl.BlockSpec((None, tr, c), lambda q, i, blk: (q, i, 0))
    return pl.pallas_call(
        body, name=name,
        grid_spec=pltpu.PrefetchScalarGridSpec(
            num_scalar_prefetch=1, grid=(nq, r // tr),
            in_specs=[pl.BlockSpec((None, tr, c), lambda q, i, blk: (blk[q], i, 0)), spec], out_specs=spec),
        out_shape=pltpu.HBM(got.shape, got.dtype),
        compiler_params=_params(("arbitrary", "arbitrary"), 6 * tr * c * 2))(blocks, parts, got)


def rs_pair_plan(n):
    def plan(me):
        sibling = _flip(me, TO_SIBLING)
        return [(a, a, _index(_flip(sibling, rel)), q, sibling, True, True)
                for a in range(n) for q, rel in enumerate(CHIP_REL)]
    return plan


def rs_plan(n):
    def plan(me):
        return [(a, a, q, q, _flip(me, CHIP_REL[q]), True, True) for a in range(n) for q in range(1, N_CHIPS)]
    return plan


def rs_start(name, sums, after):
    lands = [lax.empty(t.shape, t.dtype) for t in sums]
    return split_start(name, sums, lands, rs_plan(len(sums)), after)


def allreduce_small(name, pack, after):
    rows, lanes = pack.shape

    def body(x_ref, after_ref, o_ref, land, send_sems, recv_sems):
        me = _place()
        idx = _index(me)
        land[idx] = x_ref[...]
        copies = []
        for r in range(1, N_DEV):
            peer = _flip(me, REL[r])
            cp = pltpu.make_async_remote_copy(
                src_ref=x_ref, dst_ref=land.at[idx], send_sem=send_sems.at[r - 1], recv_sem=recv_sems.at[r - 1],
                device_id=peer, device_id_type=MESH)
            cp.start()
            copies.append(cp)
        for cp in copies:
            cp.wait_recv()
        for cp in copies:
            cp.wait_send()
        acc = land[0]
        for i in range(1, N_DEV):
            acc = acc + land[i]
        o_ref[...] = acc

    return pl.pallas_call(
        body, name=name, in_specs=[pl.BlockSpec(memory_space=pltpu.VMEM), ANY],
        out_specs=pl.BlockSpec(memory_space=pltpu.VMEM), out_shape=jax.ShapeDtypeStruct((rows, lanes), F32),
        scratch_shapes=[pltpu.VMEM((N_DEV, rows, lanes), F32), pltpu.SemaphoreType.DMA((7,)),
                        pltpu.SemaphoreType.DMA((7,))],
    )(pack, after)


def _pad_rows(a, rows):
    return jnp.pad(a, ((0, rows - a.shape[0]), (0, 0)))


def _as_tiles(vec):
    n = vec.shape[0]
    rows = -(-n // LANES)
    rows = -(-rows // SUBLANES) * SUBLANES
    return jnp.pad(vec, (0, rows * LANES - n)).reshape(rows, LANES)


def kernel(x, p, rel_bias_table, attn_norm, w_in, sink_a, w_branch_a, w_branch_b, w_out, ffn_norm, w_ffn_gate, w_ffn_up, conv_w, conv_b, w_ffn_down, ple_norm, w_ple_gate, w_ple_proj, final_norm, loss_target, m_rel_bias_table, m_attn_norm, m_w_in, m_sink_a, m_w_branch_a, m_w_branch_b, m_w_out, m_ffn_norm, m_w_ffn_gate, m_w_ffn_up, m_conv_w, m_conv_b, m_w_ffn_down, m_ple_norm, m_w_ple_gate, m_w_ple_proj, m_final_norm, v_rel_bias_table, v_attn_norm, v_w_in, v_sink_a, v_w_branch_a, v_w_branch_b, v_w_out, v_ffn_norm, v_w_ffn_gate, v_w_ffn_up, v_conv_w, v_conv_b, v_w_ffn_down, v_ple_norm, v_w_ple_gate, v_w_ple_proj, v_final_norm):
    xs = x[0]
    s, d = xs.shape
    ps = p[0, 0]
    target = loss_target[0]
    me = 4 * lax.axis_index("x") + 2 * lax.axis_index("y") + lax.axis_index("c")

    big = dict(w_in=w_in[0], w_branch_a=w_branch_a[0], w_branch_b=w_branch_b[0], w_out=w_out[0],
               w_ffn_gate=w_ffn_gate[0], w_ffn_up=w_ffn_up[0], w_ffn_down=w_ffn_down[0],
               w_ple_gate=w_ple_gate[0], w_ple_proj=w_ple_proj[0])
    big_m = dict(w_in=m_w_in[0], w_branch_a=m_w_branch_a[0], w_branch_b=m_w_branch_b[0], w_out=m_w_out[0],
                 w_ffn_gate=m_w_ffn_gate[0], w_ffn_up=m_w_ffn_up[0], w_ffn_down=m_w_ffn_down[0],
                 w_ple_gate=m_w_ple_gate[0], w_ple_proj=m_w_ple_proj[0])
    big_v = dict(w_in=v_w_in[0], w_branch_a=v_w_branch_a[0], w_branch_b=v_w_branch_b[0], w_out=v_w_out[0],
                 w_ffn_gate=v_w_ffn_gate[0], w_ffn_up=v_w_ffn_up[0], w_ffn_down=v_w_ffn_down[0],
                 w_ple_gate=v_w_ple_gate[0], w_ple_proj=v_w_ple_proj[0])
    names = list(big)
    nf = big["w_ffn_gate"].shape[1]

    shards = {k: big[k].astype(BF16) for k in names}
    shards["conv_w"] = _pad_rows(conv_w[0], SUBLANES)
    flipped = ("w_ffn_gate", "w_ffn_up")
    for k in flipped:
        big[k], big_m[k], big_v[k] = big[k].T, big_m[k].T, big_v[k].T
    ag_groups = [["w_in"], ["w_branch_a", "w_branch_b", "w_out"], ["w_ffn_gate", "conv_w"], ["w_ffn_up"],
                 ["w_ffn_down", "w_ple_gate", "w_ple_proj"]]
    ag_started = {}
    wg = {}

    ag_paired, ag_passing = {}, {}

    def pair(gi, after):
        lands = [lax.dynamic_update_index_in_dim(lax.empty((N_DEV,) + shards[k].shape, shards[k].dtype), shards[k],
                                                 me, 0) for k in ag_groups[gi]]
        ag_paired[gi], token = ag_pair(f"ag_pair{gi}", lands, after)
        return token

    def start(gi, after):
        s_sems, r_sems, _, lands, token = ag_start(f"ag_start{gi}", ag_paired[gi], after)
        ag_started[gi] = (s_sems, r_sems, lands)
        return token

    def landed(gi, after):
        s_sems, r_sems, lands = ag_started[gi]
        return split_wait(f"ag_wait{gi}", s_sems, r_sems, [], lands, ag_plan(len(lands)), after)[1:]

    def pass_on(gi, lands, after):
        s_sems, r_sems, _, lands, token = split_start(f"ag_pass{gi}", [], lands, pass_plan(len(lands)), after)
        ag_passing[gi] = (s_sems, r_sems, lands)
        return token

    def ready(gi, after):
        s_sems, r_sems, lands = ag_passing[gi]
        lands = split_wait(f"ag_ready{gi}", s_sems, r_sems, [], lands, pass_plan(len(lands)), after)[1]
        wg.update(zip(ag_groups[gi], lands))

    cb = conv_b.reshape(N_DEV, 1, nf)

    table_t = rel_bias_table.T
    geo_a = dict(half=A_BLOCK, q_rows=ATTN_Q_ROWS, n_chains=ATTN_CHAINS, dil=1, nh=A_Q_HEADS, group=A_GROUP,
                 cq=COL_QA, ck=COL_KA, cv=COL_VA)
    geo_b = [dict(half=B_BLOCK, q_rows=min(ATTN_Q_ROWS, s // dil), n_chains=ATTN_CHAINS, dil=dil,
                  nh=B_HEADS_PER_GROUP, group=1, cq=COL_QB + g * B_OUT_W, ck=COL_KB + g * B_OUT_W,
                  cv=COL_VB + g * B_OUT_W) for g, (_, dil) in enumerate(B_PATTERNS)]
    bucket_a = bucket_tile(geo_a["q_rows"], A_BLOCK, 1)
    bias_a = bias_build("bias_a", table_t, bucket_a, 0, A_Q_HEADS, A_BLOCK)
    buckets_b = [bucket_tile(gb["q_rows"], B_BLOCK, gb["dil"]) for gb in geo_b]
    biases_b = [bias_build(f"bias_b{g}", table_t, buckets_b[g], A_Q_HEADS + g * B_HEADS_PER_GROUP, B_HEADS_PER_GROUP,
                           B_BLOCK) for g in range(len(B_PATTERNS))]

    token = start(0, pair(0, xs))
    h = rms_fwd("rms_attn", xs, attn_norm + _token_value(token))
    lands0, token = landed(0, pair(4, pair(3, pair(2, pair(1, h)))))
    token = start(4, start(3, start(2, start(1, token))))
    wg["w_in"] = ag_finish("ag_finish0", lands0)[0]
    proj = mm_cols("proj_in", h, wg["w_in"], F32, fold=True, after=token)
    token = pass_on(1, landed(1, proj)[0], proj)
    sink = sink_a[0] + _token_value(token)
    ya, lse_a = band_attn_fwd("attn_a_fwd", proj, bias_a, sink, **geo_a)
    outs_b, lses_b = [], []
    for g in range(len(B_PATTERNS)):
        o, l = band_attn_fwd(f"attn_b{g}_fwd", proj, biases_b[g], None, **geo_b[g])
        outs_b.append(o)
        lses_b.append(l)
    yb = dil_merge_fwd("dil_merge_fwd", outs_b, lses_b)
    ready(1, yb)
    token = pass_on(2, landed(2, yb)[0], yb)
    w_out_full = wg["w_out"].reshape(d, d)
    ta = mm_cols("branch_a", ya, wg["w_branch_a"], F32, fold=True, after=token)
    tb = mm_cols("branch_b", yb, wg["w_branch_b"], F32, fold=True)
    merged = gate_merge_fwd("gate_merge_fwd", proj, ta, tb, d)
    x1 = mm_plain("mix_out", merged, w_out_full, F32, res=xs)

    hf = rms_fwd("rms_ffn", x1, ffn_norm)
    ready(2, hf)
    token = pass_on(3, landed(3, hf)[0], hf)
    cw = wg["conv_w"]
    gpre = mm_cols("ffn_gate", hf, wg["w_ffn_gate"], F32, fold=False, after=token)
    ready(3, gpre)
    token = pass_on(4, landed(4, gpre)[0], gpre)
    u = mm_cols("ffn_up", hf, wg["w_ffn_up"], F32, fold=False, after=token)
    z = ffn_mid_fwd("ffn_mid_fwd", gpre, u, cw, cb)
    ready(4, z)
    w_pg_full = wg["w_ple_gate"].reshape(d, d)
    x2 = mm_jsum("ffn_down", z, wg["w_ffn_down"], F32, res=x1)

    hp = rms_fwd("rms_ple", x2, ple_norm)
    lp = mm_plain("ple_gate", hp, w_pg_full, F32)
    pp = mm_cols("ple_proj", ps, wg["w_ple_proj"], F32, fold=True)
    loss_part, dx3, dlp, dpp, d_final = tail_fwd_bwd("tail", x2, lp, pp, final_norm.reshape(1, d), target)

    grads = {}
    rs_started = []
    blocks = own_blocks()

    exchanging = []

    def exchange(tag, keys):
        parts = [grads[k] for k in keys]
        lands = [lax.empty((N_CHIPS,) + tuple(p.shape[1:]), p.dtype) for p in parts]
        s_sems, r_sems, parts, lands, token = split_start(f"rs_pair_{tag}", parts, lands, rs_pair_plan(len(keys)), blocks)
        exchanging.append((tag, keys, s_sems, r_sems, parts, lands))
        return _token_value(token)

    def send(after):
        tag, keys, s_sems, r_sems, parts, lands = exchanging.pop(0)
        parts, got, _ = split_wait(f"rs_paired_{tag}", s_sems, r_sems, parts, lands, rs_pair_plan(len(keys)), after)
        return send_sums(tag, keys, parts, got)

    def send_sums(tag, keys, parts, got):
        sums = [pair_add(f"pair_add_{k}", blocks, p, g) for k, p, g in zip(keys, parts, got)]
        s_sems, r_sems, srcs, lands, token = rs_start(f"rs_start_{tag}", sums, blocks)
        rs_started.append((tag, keys, s_sems, r_sems, srcs, lands))
        return token

    grads["w_ple_proj"] = mm_tn_cols("d_w_ple_proj", ps, dpp, N_DEV, big["w_ple_proj"].shape[1], BF16, folded=True)
    grads["w_ple_gate"] = mm_tn_plain("d_w_ple_gate", hp, dlp, BF16).reshape(N_DEV, d // N_DEV, d)
    tok = exchange("ple", ["w_ple_proj", "w_ple_gate"])
    dhp = mm_nt_plain("d_hp", dlp, w_pg_full, F32)
    dx2, dx2_b, d_ple = rms_bwd("rms_ple_bwd", x2, ple_norm + tok, dhp, dx3, True)

    dz = mm_nt_j("d_z", dx2_b, wg["w_ffn_down"], BF16)
    grads["w_ffn_down"] = mm_tn_j("d_w_ffn_down", z, dx2_b, BF16)
    tok = _token_value(send(dz)) + exchange("down", ["w_ffn_down"])
    du, dgpre, dcw = ffn_mid_bwd("ffn_mid_bwd", gpre, u, dz, cw, cb + tok)
    grads["w_ffn_up"] = mm_tn_j("d_w_ffn_up", du, hf, BF16)
    grads["w_ffn_gate"] = mm_tn_j("d_w_ffn_gate", dgpre, hf, BF16)
    dhf = mm_nt_jsum("d_hf_up", du, wg["w_ffn_up"], F32, folded=False)
    dhf = mm_nt_jsum("d_hf_gate", dgpre, wg["w_ffn_gate"], F32, folded=False, res=dhf)
    tok = _token_value(send(dhf)) + exchange("upgate", ["w_ffn_up", "w_ffn_gate"])
    dx1, dx1_b, d_ffn = rms_bwd("rms_ffn_bwd", x1, ffn_norm + tok, dhf, dx2, True)

    dmerged = mm_nt_plain("d_merged", dx1_b, w_out_full, F32)
    grads["w_out"] = mm_tn_plain("d_w_out", merged, dx1_b, BF16).reshape(N_DEV, d // N_DEV, d)
    dta, dtb, dga, dgb = gate_merge_bwd("gate_merge_bwd", dmerged, proj, ta, tb, d)
    grads["w_branch_a"] = mm_tn_cols("d_w_branch_a", ya, dta, N_DEV, big["w_branch_a"].shape[1], BF16, folded=True)
    grads["w_branch_b"] = mm_tn_cols("d_w_branch_b", yb, dtb, N_DEV, big["w_branch_b"].shape[1], BF16, folded=True)
    dya = mm_nt_jsum("d_ya", dta, wg["w_branch_a"], F32, folded=True)
    dyb = mm_nt_jsum("d_yb", dtb, wg["w_branch_b"], F32, folded=True)
    tok = _token_value(send(dyb)) + exchange("mix", ["w_out", "w_branch_a", "w_branch_b"])
    dqa, dka, dva, dbias_a, dsink = band_attn_bwd("attn_a_bwd", proj, bias_a, sink + tok, dya, ya, lse_a, None, **geo_a)
    douts_b, dlses_b = dil_merge_bwd("dil_merge_bwd", dyb, outs_b, lses_b)
    dq_b, dk_b, dv_b, dbias_b = [], [], [], []
    for g in range(len(B_PATTERNS)):
        dq, dk, dv, db, _ = band_attn_bwd(f"attn_b{g}_bwd", proj, biases_b[g], None, douts_b[g], outs_b[g], lses_b[g],
                                          dlses_b[g], **geo_b[g])
        dq_b.append(dq)
        dk_b.append(dk)
        dv_b.append(dv)
        dbias_b.append(db)
    dproj = jnp.concatenate([t.astype(BF16) for t in [dqa, dka, dva] + dq_b + dk_b + dv_b + [dga, dgb]], axis=1)
    token = send(dproj)
    grads["w_in"] = mm_tn_cols("d_w_in", h, dproj, N_DEV, big["w_in"].shape[1], BF16, folded=True, after=token)
    token = send_sums("in", ["w_in"], [grads["w_in"]], rs_pair("rs_pair_in", [grads["w_in"]]))
    dh = mm_nt_jsum("d_h", dproj, wg["w_in"], F32, folded=True, after=token)
    grad_x, _, d_attn = rms_bwd("rms_attn_bwd", xs, attn_norm, dh, dx1, False)

    dt_a = table_grad("table_grad_a", dbias_a, bucket_a)[:, 0, :N_BUCKETS]
    dt_b = [table_grad(f"table_grad_b{g}", dbias_b[g], buckets_b[g])[:, 0, :N_BUCKETS] for g in range(len(B_PATTERNS))]
    d_table_part = jnp.concatenate([dt_a] + dt_b, axis=0).T

    pieces = [
        ("loss", loss_part[0, :1]),
        ("table", d_table_part.reshape(-1)),
        ("attn_norm", d_attn.reshape(-1)),
        ("sink", dsink[:, 0, 0]),
        ("ffn_norm", d_ffn.reshape(-1)),
        ("conv_w", dcw[:, 0:3, :].reshape(-1)),
        ("conv_b", dcw[:, 3, :].reshape(-1)),
        ("ple_norm", d_ple.reshape(-1)),
        ("final_norm", d_final.reshape(-1)),
    ]
    tiles = [_as_tiles(v) for _, v in pieces]
    pack = jnp.concatenate(tiles, axis=0)

    out_g, out_d, out_m, out_v = {}, {}, {}, {}

    def finish(group, after):
        tag, keys, s_sems, r_sems, srcs, lands = group
        srcs, lands, _ = split_wait(f"rs_wait_{tag}", s_sems, r_sems, srcs, lands, rs_plan(len(keys)), after)
        for k, mine, theirs in zip(keys, srcs, lands):
            res = reduce_adam("adam_" + k, mine, theirs, big[k], big_m[k], big_v[k])
            after = res[1]
            out_g[k], out_d[k], out_m[k], out_v[k] = [(t.T if k in flipped else t)[None] for t in res]
        return after

    after = pack
    for group in rs_started[:-1]:
        after = finish(group, after)
    total = allreduce_small("allreduce_small", pack, after)
    finish(rs_started[-1], total)
    small = {}
    row = 0
    for (nm, v), t in zip(pieces, tiles):
        small[nm] = total[row:row + t.shape[0]].reshape(-1)[:v.shape[0]]
        row += t.shape[0]
    loss = small["loss"][0]
    g_small = dict(
        rel_bias_table=small["table"].reshape(rel_bias_table.shape),
        attn_norm=small["attn_norm"].reshape(attn_norm.shape),
        sink_a=small["sink"].reshape(sink_a.shape),
        ffn_norm=small["ffn_norm"].reshape(ffn_norm.shape),
        conv_w=lax.dynamic_index_in_dim(small["conv_w"].reshape(N_DEV, 3, nf), me, 0, keepdims=False)[None],
        conv_b=small["conv_b"].reshape(conv_b.shape),
        ple_norm=small["ple_norm"].reshape(ple_norm.shape),
        final_norm=small["final_norm"].reshape(1, d),
    )
    w_small = dict(rel_bias_table=(rel_bias_table, m_rel_bias_table, v_rel_bias_table),
                   attn_norm=(attn_norm, m_attn_norm, v_attn_norm), sink_a=(sink_a, m_sink_a, v_sink_a),
                   ffn_norm=(ffn_norm, m_ffn_norm, v_ffn_norm), conv_w=(conv_w, m_conv_w, v_conv_w),
                   conv_b=(conv_b, m_conv_b, v_conv_b), ple_norm=(ple_norm, m_ple_norm, v_ple_norm),
                   final_norm=(final_norm, m_final_norm, v_final_norm))

    for k, (wv, mv, vv) in w_small.items():
        shape = wv.shape
        two_d = (1, shape[0]) if len(shape) == 1 else ((shape[0] * shape[1], shape[2]) if len(shape) == 3 else shape)
        gk = g_small[k].reshape(two_d)
        dl, nm, nv = adam_small("adam_" + k, gk, wv.reshape(two_d), mv.reshape(two_d), vv.reshape(two_d))
        out_g[k], out_d[k], out_m[k], out_v[k] = gk.reshape(shape), dl.reshape(shape), nm.reshape(shape), nv.reshape(shape)

    order = ["rel_bias_table", "attn_norm", "w_in", "sink_a", "w_branch_a", "w_branch_b", "w_out", "ffn_norm",
             "w_ffn_gate", "w_ffn_up", "conv_w", "conv_b", "w_ffn_down", "ple_norm", "w_ple_gate", "w_ple_proj",
             "final_norm"]
    return (loss, grad_x[None], *[out_g[k] for k in order], *[out_d[k] for k in order],
            *[out_m[k] for k in order], *[out_v[k] for k in order])
```

```python
import math

import jax
import jax.numpy as jnp
from jax import lax
from jax.experimental import pallas as pl
from jax.experimental.pallas import tpu as pltpu

F32 = jnp.float32
BF16 = jnp.bfloat16
MESH = pl.DeviceIdType.MESH
N_DEV = 8

HEAD_DIM = 128
A_Q_HEADS = 8
A_KV_HEADS = 2
A_GROUP = A_Q_HEADS // A_KV_HEADS
A_BLOCK = 128
B_PATTERNS = ((128, 1), (512, 4), (2048, 16))
B_HEADS_PER_GROUP = 4
B_HEADS = len(B_PATTERNS) * B_HEADS_PER_GROUP
B_BLOCK = 64
N_BUCKETS = 32
MAX_DISTANCE = 1024
A_Q_W = A_Q_HEADS * HEAD_DIM
A_KV_W = A_KV_HEADS * HEAD_DIM
B_W = B_HEADS * HEAD_DIM
B_OUT_W = B_HEADS_PER_GROUP * HEAD_DIM
COL_QA = 0
COL_KA = COL_QA + A_Q_W
COL_VA = COL_KA + A_KV_W
COL_QB = COL_VA + A_KV_W
COL_KB = COL_QB + B_W
COL_VB = COL_KB + B_W
COL_GATES = COL_VB + B_W
RMS_EPS = 1e-6
NEG_INF = -1e30
ATTN_SCALE = HEAD_DIM ** -0.5
ATTN_Q_ROWS = 256
ATTN_CHAINS = 4

ADAM_LR = 0.001
ADAM_B1 = 0.9
ADAM_B2 = 0.999
ADAM_EPS = 1e-08
ADAM_WD = 0.01
ADAM_STEP = 10

GELU_C = math.sqrt(2.0 / math.pi)
GELU_A = 0.044715

V7X_VMEM_BYTES = 64 * 1024 * 1024
VMEM_CEILING = V7X_VMEM_BYTES - 8 * 1024 * 1024
LANES = 128
SUBLANES = 8


def _pick(n, cands):
    for c in cands:
        if n % c == 0:
            return c
    return n


def _nbytes(shape, dtype):
    n = 1
    for d in shape:
        if d is not None:
            n *= d
    return n * jnp.dtype(dtype).itemsize


def _params(sem, est_bytes):
    limit = int(min(VMEM_CEILING, max(32 * 1024 * 1024, 2 * est_bytes + (8 << 20))))
    return pltpu.CompilerParams(dimension_semantics=sem, vmem_limit_bytes=limit)


def _mm(name, a, b, a_bs, a_im, b_bs, b_im, out_shape, out_dtype, o_bs, o_im, grid, dims,
        res=None, r_bs=None, r_im=None, after=None):
    nk = grid[-1]
    nax = len(grid)
    has_res = res is not None
    has_after = after is not None
    o_tile = tuple(d for d in o_bs if d is not None)

    def body(*refs):
        a_ref, b_ref = refs[:2]
        r_ref = refs[2] if has_res else None
        n_in = 2 + has_res + has_after
        o_ref = refs[n_in]
        rest = refs[n_in + 1:]

        def prod():
            return lax.dot_general(a_ref[...].astype(BF16), b_ref[...].astype(BF16), (dims, ((), ())),
                                   preferred_element_type=F32)

        def finish(r):
            if r_ref is not None:
                r = r + r_ref[...].astype(F32)
            o_ref[...] = r.astype(o_ref.dtype)

        if nk == 1:
            finish(prod())
        else:
            acc = rest[0]
            k = pl.program_id(nax - 1)

            @pl.when(k == 0)
            def _():
                acc[...] = prod()

            @pl.when(k > 0)
            def _():
                acc[...] += prod()

            @pl.when(k == nk - 1)
            def _():
                finish(acc[...])

    in_specs = [pl.BlockSpec(a_bs, a_im), pl.BlockSpec(b_bs, b_im)]
    args = [a, b]
    est = _nbytes(a_bs, a.dtype) + _nbytes(b_bs, b.dtype) + _nbytes(o_bs, out_dtype) + 2 * _nbytes(o_tile, F32)
    if has_res:
        in_specs.append(pl.BlockSpec(r_bs, r_im))
        args.append(res)
        est += _nbytes(r_bs, res.dtype)
    if has_after:
        in_specs.append(pl.BlockSpec(memory_space=pl.ANY))
        args.append(after)
    scratch = [] if nk == 1 else [pltpu.VMEM(o_tile, F32)]
    sem = ("parallel",) * (nax - 1) + ("arbitrary",)
    return pl.pallas_call(
        body, name=name, grid=grid, in_specs=in_specs, out_specs=pl.BlockSpec(o_bs, o_im),
        out_shape=pltpu.HBM(out_shape, out_dtype), scratch_shapes=scratch,
        compiler_params=_params(sem, est))(*args)


TM_CANDS = (1024, 512, 256, 128, 64, 32, 16, 8)
TM_WIDE_CANDS = (2048,) + TM_CANDS
TK_CANDS = (1024, 512, 256, 128)
TN_CANDS = (1024, 512, 256, 128)


def mm_cols(name, a, wg, out_dtype, fold, after=None):
    m, k = a.shape
    nj, _, n = wg.shape
    tm, tk = _pick(m, TM_WIDE_CANDS), _pick(k, TK_CANDS)
    grid = (nj, m // tm, k // tk)
    if fold:
        shape, o_bs, o_im = (m, nj * n), (tm, n), (lambda j, i, kk: (i, j))
    else:
        shape, o_bs, o_im = (nj, m, n), (None, tm, n), (lambda j, i, kk: (j, i, 0))
    return _mm(name, a, wg, (tm, tk), lambda j, i, kk: (i, kk), (None, tk, n), lambda j, i, kk: (j, kk, 0),
               shape, out_dtype, o_bs, o_im, grid, ((1,), (0,)), after=after)


def mm_plain(name, a, w, out_dtype, res=None):
    m, k = a.shape
    n = w.shape[1]
    tm, tk, tn = _pick(m, TM_CANDS), _pick(k, TK_CANDS), _pick(n, TN_CANDS)
    grid = (n // tn, m // tm, k // tk)
    return _mm(name, a, w, (tm, tk), lambda j, i, kk: (i, kk), (tk, tn), lambda j, i, kk: (kk, j),
               (m, n), out_dtype, (tm, tn), lambda j, i, kk: (i, j), grid, ((1,), (0,)),
               res, (tm, tn), lambda j, i, kk: (i, j))


def mm_jsum(name, aj, wg, out_dtype, res=None, after=None):
    nj, m, ka = aj.shape
    n = wg.shape[2]
    tm, tn = _pick(m, TM_CANDS), _pick(n, TN_CANDS)
    grid = (m // tm, n // tn, nj)
    return _mm(name, aj, wg, (None, tm, ka), lambda i, jn, j: (j, i, 0), (None, ka, tn), lambda i, jn, j: (j, 0, jn),
               (m, n), out_dtype, (tm, tn), lambda i, jn, j: (i, jn), grid, ((1,), (0,)),
               res, (tm, tn), lambda i, jn, j: (i, jn), after=after)


def mm_tn_cols(name, a, g, nj, n, out_dtype, folded, after=None):
    s, kw = a.shape
    ts, tkw = _pick(s, TK_CANDS), _pick(kw, TM_CANDS)
    grid = (nj, kw // tkw, s // ts)
    if folded:
        g_bs, g_im = (ts, n), (lambda j, i, ss: (ss, j))
    else:
        g_bs, g_im = (None, ts, n), (lambda j, i, ss: (j, ss, 0))
    return _mm(name, a, g, (ts, tkw), lambda j, i, ss: (ss, i), g_bs, g_im,
               (nj, kw, n), out_dtype, (None, tkw, n), lambda j, i, ss: (j, i, 0), grid, ((0,), (0,)), after=after)


def mm_tn_plain(name, a, g, out_dtype):
    s, kw = a.shape
    n = g.shape[1]
    ts, tkw, tn = _pick(s, TK_CANDS), _pick(kw, TM_CANDS), _pick(n, TN_CANDS)
    grid = (kw // tkw, n // tn, s // ts)
    return _mm(name, a, g, (ts, tkw), lambda i, jn, ss: (ss, i), (ts, tn), lambda i, jn, ss: (ss, jn),
               (kw, n), out_dtype, (tkw, tn), lambda i, jn, ss: (i, jn), grid, ((0,), (0,)))


def mm_tn_j(name, aj, g, out_dtype):
    nj, s, ka = aj.shape
    n = g.shape[1]
    ts, tn = _pick(s, TK_CANDS), _pick(n, TN_CANDS)
    grid = (nj, n // tn, s // ts)
    return _mm(name, aj, g, (None, ts, ka), lambda j, jn, ss: (j, ss, 0), (ts, tn), lambda j, jn, ss: (ss, jn),
               (nj, ka, n), out_dtype, (None, ka, tn), lambda j, jn, ss: (j, 0, jn), grid, ((0,), (0,)))


def mm_nt_plain(name, g, w, out_dtype):
    m, n = g.shape
    k = w.shape[0]
    tm, tn, tkk = _pick(m, TM_CANDS), _pick(n, TK_CANDS), _pick(k, TN_CANDS)
    grid = (k // tkk, m // tm, n // tn)
    return _mm(name, g, w, (tm, tn), lambda kk, i, jn: (i, jn), (tkk, tn), lambda kk, i, jn: (kk, jn),
               (m, k), out_dtype, (tm, tkk), lambda kk, i, jn: (i, kk), grid, ((1,), (1,)))


def mm_nt_j(name, g, wg, out_dtype):
    m, n = g.shape
    nj, ka, _ = wg.shape
    tm, tn = _pick(m, TM_CANDS), _pick(n, TK_CANDS)
    grid = (nj, m // tm, n // tn)
    return _mm(name, g, wg, (tm, tn), lambda j, i, jn: (i, jn), (None, ka, tn), lambda j, i, jn: (j, 0, jn),
               (nj, m, ka), out_dtype, (None, tm, ka), lambda j, i, jn: (j, i, 0), grid, ((1,), (1,)))


def mm_nt_jsum(name, g, wg, out_dtype, folded, res=None, after=None):
    nj, k, n = wg.shape
    m = g.shape[0] if folded else g.shape[1]
    tm, tkk = _pick(m, TM_CANDS if res is not None else TM_WIDE_CANDS), _pick(k, TN_CANDS)
    grid = (m // tm, k // tkk, nj)
    if folded:
        g_bs, g_im = (tm, n), (lambda i, kk, j: (i, j))
    else:
        g_bs, g_im = (None, tm, n), (lambda i, kk, j: (j, i, 0))
    return _mm(name, g, wg, g_bs, g_im, (None, tkk, n), lambda i, kk, j: (j, kk, 0),
               (m, k), out_dtype, (tm, tkk), lambda i, kk, j: (i, kk), grid, ((1,), (1,)),
               res, (tm, tkk), lambda i, kk, j: (i, kk), after=after)


ROW_TILE_CANDS = (256, 128, 64, 32, 16, 8)


def _rstd(x):
    return lax.rsqrt(jnp.mean(x * x, axis=-1, keepdims=True) + RMS_EPS)


def _sigmoid(t):
    return 1.0 / (1.0 + jnp.exp(-t))


def rms_fwd(name, x, gain):
    s, d = x.shape
    ts = _pick(s, ROW_TILE_CANDS)

    def body(x_ref, g_ref, h_ref):
        xv = x_ref[...]
        h_ref[...] = ((xv * _rstd(xv)) * g_ref[...]).astype(h_ref.dtype)

    return pl.pallas_call(
        body, name=name, grid=(s // ts,),
        in_specs=[pl.BlockSpec((ts, d), lambda i: (i, 0)), pl.BlockSpec((1, d), lambda i: (0, 0))],
        out_specs=pl.BlockSpec((ts, d), lambda i: (i, 0)),
        out_shape=pltpu.HBM((s, d), BF16),
        compiler_params=_params(("parallel",), 3 * ts * d * 4))(x, gain)


def rms_bwd(name, x, gain, dh, dres, bf16_copy):
    s, d = x.shape
    ts = _pick(s, ROW_TILE_CANDS)

    def body(x_ref, g_ref, dh_ref, dr_ref, dx_ref, *rest):
        dxb_ref, dg_ref = rest if bf16_copy else (None, rest[0])
        xv = x_ref[...]
        r = _rstd(xv)
        xhat = xv * r
        dhv = dh_ref[...].astype(F32)
        dxhat = dhv * g_ref[...]
        dx = dr_ref[...] + r * (dxhat - xhat * jnp.mean(dxhat * xhat, axis=-1, keepdims=True))
        dx_ref[...] = dx
        if bf16_copy:
            dxb_ref[...] = dx.astype(dxb_ref.dtype)
        part = jnp.sum(dhv * xhat, axis=0, keepdims=True)

        @pl.when(pl.program_id(0) == 0)
        def _():
            dg_ref[...] = part

        @pl.when(pl.program_id(0) > 0)
        def _():
            dg_ref[...] += part

    row = pl.BlockSpec((ts, d), lambda i: (i, 0))
    vec = pl.BlockSpec((1, d), lambda i: (0, 0))
    copy_spec, copy_shape = ([row], [pltpu.HBM((s, d), BF16)]) if bf16_copy else ([], [])
    res = pl.pallas_call(
        body, name=name, grid=(s // ts,), in_specs=[row, vec, row, row], out_specs=[row] + copy_spec + [vec],
        out_shape=[pltpu.HBM((s, d), F32)] + copy_shape + [jax.ShapeDtypeStruct((1, d), F32)],
        compiler_params=_params(("arbitrary",), 7 * ts * d * 4))(x, gain, dh, dres)
    return (res[0], res[1], res[2]) if bf16_copy else (res[0], None, res[1])


def gate_merge_fwd(name, proj, ta, tb, d):
    s = proj.shape[0]
    ts = _pick(s, ROW_TILE_CANDS)
    cb = COL_GATES // d

    def body(ga_ref, gb_ref, ta_ref, tb_ref, o_ref):
        o_ref[...] = (_sigmoid(ga_ref[...]) * ta_ref[...] + _sigmoid(gb_ref[...]) * tb_ref[...]).astype(o_ref.dtype)

    row = pl.BlockSpec((ts, d), lambda i: (i, 0))
    return pl.pallas_call(
        body, name=name, grid=(s // ts,),
        in_specs=[pl.BlockSpec((ts, d), lambda i: (i, cb)), pl.BlockSpec((ts, d), lambda i: (i, cb + 1)), row, row],
        out_specs=row, out_shape=pltpu.HBM((s, d), BF16),
        compiler_params=_params(("parallel",), 5 * ts * d * 4))(proj, proj, ta, tb)


def gate_merge_bwd(name, dmerged, proj, ta, tb, d):
    s = proj.shape[0]
    ts = _pick(s, ROW_TILE_CANDS)
    cb = COL_GATES // d

    def body(dm_ref, ga_ref, gb_ref, ta_ref, tb_ref, dta_ref, dtb_ref, dga_ref, dgb_ref):
        dm = dm_ref[...]
        sa = _sigmoid(ga_ref[...])
        sb = _sigmoid(gb_ref[...])
        dta_ref[...] = (dm * sa).astype(dta_ref.dtype)
        dtb_ref[...] = (dm * sb).astype(dtb_ref.dtype)
        dga_ref[...] = (dm * ta_ref[...] * (sa * (1.0 - sa))).astype(dga_ref.dtype)
        dgb_ref[...] = (dm * tb_ref[...] * (sb * (1.0 - sb))).astype(dgb_ref.dtype)

    row = pl.BlockSpec((ts, d), lambda i: (i, 0))
    out = pltpu.HBM((s, d), BF16)
    return pl.pallas_call(
        body, name=name, grid=(s // ts,),
        in_specs=[row, pl.BlockSpec((ts, d), lambda i: (i, cb)), pl.BlockSpec((ts, d), lambda i: (i, cb + 1)), row, row],
        out_specs=[row, row, row, row], out_shape=[out, out, out, out],
        compiler_params=_params(("parallel",), 8 * ts * d * 4))(dmerged, proj, proj, ta, tb)


def tail_fwd_bwd(name, x2, lp, pp, gain, target):
    s, d = x2.shape
    ts = _pick(s, ROW_TILE_CANDS)

    def body(x2_ref, lp_ref, pp_ref, g_ref, t_ref, loss_ref, dx3_ref, dlp_ref, dpp_ref, dg_ref):
        gp = _sigmoid(lp_ref[...])
        ppv = pp_ref[...]
        x3 = x2_ref[...] + gp * ppv
        r = _rstd(x3)
        xhat = x3 * r
        gv = g_ref[...]
        err = xhat * gv - t_ref[...]
        loss = jnp.sum(err * err) * (0.5 / d)
        dy = err * (1.0 / d)
        dxhat = dy * gv
        dx3 = r * (dxhat - xhat * jnp.mean(dxhat * xhat, axis=-1, keepdims=True))
        dx3_ref[...] = dx3
        dlp_ref[...] = (dx3 * ppv * (gp * (1.0 - gp))).astype(dlp_ref.dtype)
        dpp_ref[...] = (dx3 * gp).astype(dpp_ref.dtype)
        part = jnp.sum(dy * xhat, axis=0, keepdims=True)
        lossv = jnp.full((1, LANES), loss, F32)

        @pl.when(pl.program_id(0) == 0)
        def _():
            dg_ref[...] = part
            loss_ref[...] = lossv

        @pl.when(pl.program_id(0) > 0)
        def _():
            dg_ref[...] += part
            loss_ref[...] += lossv

    row = pl.BlockSpec((ts, d), lambda i: (i, 0))
    vec = pl.BlockSpec((1, d), lambda i: (0, 0))
    return pl.pallas_call(
        body, name=name, grid=(s // ts,), in_specs=[row, row, row, vec, row],
        out_specs=[pl.BlockSpec((1, LANES), lambda i: (0, 0)), row, row, row, vec],
        out_shape=[jax.ShapeDtypeStruct((1, LANES), F32), pltpu.HBM((s, d), F32),
                   pltpu.HBM((s, d), BF16), pltpu.HBM((s, d), BF16),
                   jax.ShapeDtypeStruct((1, d), F32)],
        compiler_params=_params(("arbitrary",), 9 * ts * d * 4))(x2, lp, pp, gain, target)


HALO = SUBLANES
BF16_ROWS = 2 * SUBLANES


def _shift_rows(cur, prev_row, next_row):
    ts = cur.shape[0]
    rid = lax.broadcasted_iota(jnp.int32, cur.shape, 0)
    down = jnp.where(rid == 0, prev_row, pltpu.roll(cur, 1, 0))
    up = jnp.where(rid == ts - 1, next_row, pltpu.roll(cur, ts - 1, 0))
    return down, up


def _halo_specs(ts, s, nf, halo=HALO):
    nb = ts // halo
    last = s // halo - 1
    cur = pl.BlockSpec((None, ts, nf), lambda j, i: (j, i, 0))
    prev = pl.BlockSpec((None, halo, nf), lambda j, i: (j, jnp.maximum(i * nb - 1, 0), 0))
    nxt = pl.BlockSpec((None, halo, nf), lambda j, i: (j, jnp.minimum((i + 1) * nb, last), 0))
    return cur, prev, nxt


def _halo_rows(prev_ref, next_ref, n_tiles):
    i = pl.program_id(1)
    prev_row = jnp.where(i == 0, 0.0, prev_ref[HALO - 1:HALO, :].astype(F32))
    next_row = jnp.where(i == n_tiles - 1, 0.0, next_ref[0:1, :].astype(F32))
    return prev_row, next_row


def _gelu(g):
    t = jnp.tanh(GELU_C * (g + GELU_A * (g * g * g)))
    return 0.5 * g * (1.0 + t), t


def _conv(cur, down, up, cw_ref, cb_ref):
    return down * cw_ref[0:1, :] + cur * cw_ref[1:2, :] + up * cw_ref[2:3, :] + cb_ref[...]


def ffn_mid_fwd(name, gpre, u, cw, cb):
    nj, s, nf = gpre.shape
    ts = _pick(s, (512, 256, 128, 64, 32, 16, 8))
    n_tiles = s // ts
    cur, prev, nxt = _halo_specs(ts, s, nf)

    def body(g_ref, gp_ref, gn_ref, u_ref, cw_ref, cb_ref, z_ref):
        gv = g_ref[...]
        down, up = _shift_rows(gv, *_halo_rows(gp_ref, gn_ref, n_tiles))
        act, _ = _gelu(_conv(gv, down, up, cw_ref, cb_ref))
        z_ref[...] = (act * u_ref[...]).astype(z_ref.dtype)

    return pl.pallas_call(
        body, name=name, grid=(nj, n_tiles),
        in_specs=[cur, prev, nxt, cur, pl.BlockSpec((None, SUBLANES, nf), lambda j, i: (j, 0, 0)),
                  pl.BlockSpec((None, 1, nf), lambda j, i: (j, 0, 0))],
        out_specs=cur, out_shape=pltpu.HBM((nj, s, nf), BF16),
        compiler_params=_params(("parallel", "parallel"), 8 * ts * nf * 4))(gpre, gpre, gpre, u, cw, cb)


def _gelu_grad(g, t):
    return 0.5 * (1.0 + t) + 0.5 * g * (1.0 - t * t) * (GELU_C * (1.0 + 3.0 * GELU_A * (g * g)))


def ffn_mid_bwd(name, gpre, u, dz, cw, cb):
    nj, s, nf = gpre.shape
    ts = _pick(s, (512, 256, 128, 64, 32, 16, 8))
    n_tiles = s // ts
    cur, prev, nxt = _halo_specs(ts, s, nf)

    def body(g_ref, gp_ref, gn_ref, u_ref, up_ref, un_ref, dz_ref, dzp_ref, dzn_ref, cw_ref, cb_ref,
             du_ref, dgp_ref, dcw_ref):
        i = pl.program_id(1)
        w0, w1, w2, bias = cw_ref[0:1, :], cw_ref[1:2, :], cw_ref[2:3, :], cb_ref[...]
        gv = g_ref[...]
        down, up = _shift_rows(gv, *_halo_rows(gp_ref, gn_ref, n_tiles))
        gc = down * w0 + gv * w1 + up * w2 + bias
        act, t = _gelu(gc)
        dzv = dz_ref[...].astype(F32)
        du_ref[...] = (dzv * act).astype(du_ref.dtype)
        dg = dzv * u_ref[...] * _gelu_grad(gc, t)

        def edge_dg(g_before, g_at, g_after, u_at, dz_at):
            ge = g_before.astype(F32) * w0 + g_at.astype(F32) * w1 + g_after.astype(F32) * w2 + bias
            return dz_at.astype(F32) * u_at.astype(F32) * _gelu_grad(ge, _gelu(ge)[1])

        dz_before = dzp_ref[...].astype(F32)[BF16_ROWS - 1:BF16_ROWS, :]
        dz_after = dzn_ref[...].astype(F32)[0:1, :]
        dg_prev = jnp.where(i == 0, 0.0, edge_dg(gp_ref[HALO - 2:HALO - 1, :], gp_ref[HALO - 1:HALO, :], gv[0:1, :],
                                                 up_ref[HALO - 1:HALO, :], dz_before))
        dg_next = jnp.where(i == n_tiles - 1, 0.0, edge_dg(gv[ts - 1:ts, :], gn_ref[0:1, :], gn_ref[1:2, :],
                                                           un_ref[0:1, :], dz_after))
        dg_down, dg_up = _shift_rows(dg, dg_prev, dg_next)
        dgp_ref[...] = (dg_up * w0 + dg * w1 + dg_down * w2).astype(dgp_ref.dtype)
        rows = [jnp.sum(dg * down, axis=0, keepdims=True), jnp.sum(dg * gv, axis=0, keepdims=True),
                jnp.sum(dg * up, axis=0, keepdims=True), jnp.sum(dg, axis=0, keepdims=True)]
        part = jnp.concatenate(rows + [jnp.zeros((SUBLANES - len(rows), nf), F32)], axis=0)

        @pl.when(i == 0)
        def _():
            dcw_ref[...] = part

        @pl.when(i > 0)
        def _():
            dcw_ref[...] += part

    small = pl.BlockSpec((None, SUBLANES, nf), lambda j, i: (j, 0, 0))
    return pl.pallas_call(
        body, name=name, grid=(nj, n_tiles),
        in_specs=[cur, prev, nxt] * 2 + list(_halo_specs(ts, s, nf, BF16_ROWS))
        + [small, pl.BlockSpec((None, 1, nf), lambda j, i: (j, 0, 0))],
        out_specs=[cur, cur, small],
        out_shape=[pltpu.HBM((nj, s, nf), BF16), pltpu.HBM((nj, s, nf), BF16),
                   jax.ShapeDtypeStruct((nj, SUBLANES, nf), F32)],
        compiler_params=_params(("parallel", "arbitrary"), 14 * ts * nf * 4))(
            gpre, gpre, gpre, u, u, u, dz, dz, dz, cw, cb)


def _t5_bucket(rel):
    half = N_BUCKETS // 2
    max_exact = half // 2
    n = jnp.abs(rel)
    side = jnp.where(rel > 0, half, 0)
    nf = jnp.maximum(n, 1).astype(F32)
    large = max_exact + (jnp.log(nf / max_exact) / math.log(MAX_DISTANCE / max_exact)
                         * (half - max_exact)).astype(jnp.int32)
    large = jnp.minimum(large, half - 1)
    return side + jnp.where(n < max_exact, n, large)


def bucket_tile(rows, half, dil):
    rel = (jnp.arange(rows + 2 * half)[None, :] - half) - jnp.arange(rows)[:, None]
    return _t5_bucket(rel * dil).astype(jnp.int32)


def bias_build(name, table_t, bucket, h0, nh, half):
    blk, kw = bucket.shape

    def body(t_ref, b_ref, o_ref):
        h = pl.program_id(0)
        bv = b_ref[...]
        acc = jnp.zeros((blk, kw), F32)
        for b in range(N_BUCKETS):
            acc = jnp.where(bv == b, t_ref[h0 + h, b], acc)
        qi = lax.broadcasted_iota(jnp.int32, (blk, kw), 0)
        ci = lax.broadcasted_iota(jnp.int32, (blk, kw), 1)
        o_ref[...] = jnp.where(jnp.abs(ci - half - qi) <= half, acc, NEG_INF)

    return pl.pallas_call(
        body, name=name, grid=(nh,),
        in_specs=[pl.BlockSpec(memory_space=pltpu.SMEM), pl.BlockSpec((blk, kw), lambda h: (0, 0))],
        out_specs=pl.BlockSpec((None, blk, kw), lambda h: (h, 0, 0)),
        out_shape=jax.ShapeDtypeStruct((nh, blk, kw), F32),
        compiler_params=_params(("parallel",), 4 * blk * kw * 4))(table_t, bucket)


def table_grad(name, dbias, bucket):
    nh, blk, kw = dbias.shape

    def body(d_ref, b_ref, o_ref):
        bv = b_ref[...]
        dv = d_ref[...]
        lane = lax.broadcasted_iota(jnp.int32, (SUBLANES, LANES), 1)
        acc = jnp.zeros((SUBLANES, LANES), F32)
        for b in range(N_BUCKETS):
            acc = jnp.where(lane == b, jnp.sum(jnp.where(bv == b, dv, 0.0)), acc)
        o_ref[...] = acc

    return pl.pallas_call(
        body, name=name, grid=(nh,),
        in_specs=[pl.BlockSpec((None, blk, kw), lambda h: (h, 0, 0)), pl.BlockSpec((blk, kw), lambda h: (0, 0))],
        out_specs=pl.BlockSpec((None, SUBLANES, LANES), lambda h: (h, 0, 0)),
        out_shape=jax.ShapeDtypeStruct((nh, SUBLANES, LANES), F32),
        compiler_params=_params(("parallel",), 4 * blk * kw * 4))(dbias, bucket)


class _Band:
    def __init__(self, s, half, q_rows, n_chains, dil):
        self.s, self.half, self.dil, self.n_chains = s, half, dil, n_chains
        self.seg = s // dil
        self.q_rows = min(q_rows, self.seg)
        self.win = self.q_rows + 2 * half
        self.pad = self.seg + 2 * half
        self.nsb = self.seg // self.q_rows
        self.n_items = dil * self.nsb
        assert self.n_items % n_chains == 0 and self.seg % self.q_rows == 0
        self.staged = dil > 1

    def rows_of(self, r):
        return pl.ds(r, self.seg, stride=self.dil) if self.dil > 1 else slice(None)

    def stage_kv(self, dst, src_ref):
        zeros = jnp.zeros((self.half, HEAD_DIM), dst.dtype)
        for r in range(self.dil):
            base = r * self.pad
            dst[base:base + self.half, :] = zeros
            dst[base + self.half + self.seg:base + self.pad, :] = zeros
            dst[base + self.half:base + self.half + self.seg, :] = src_ref[self.rows_of(r), :].astype(dst.dtype)

    def stage(self, dst, src_ref):
        for r in range(self.dil):
            dst[r * self.seg:(r + 1) * self.seg, :] = src_ref[self.rows_of(r), :].astype(dst.dtype)

    def unstage(self, dst_ref, src, add=False):
        for r in range(self.dil):
            val = src[r * self.seg:(r + 1) * self.seg, :].astype(dst_ref.dtype)
            if add:
                val = val + dst_ref[self.rows_of(r), :]
            dst_ref[self.rows_of(r), :] = val

    def offsets(self, item):
        r, sb = item // self.nsb, item % self.nsb
        qoff = pl.multiple_of(r * self.seg + sb * self.q_rows, self.q_rows)
        koff = pl.multiple_of(r * self.pad + sb * self.q_rows, B_BLOCK)
        kpos = sb * self.q_rows - self.half + lax.broadcasted_iota(jnp.int32, (1, self.win), 1)
        edge = jnp.where((kpos >= 0) & (kpos < self.seg), 0.0, NEG_INF)
        return qoff, koff, edge


def band_attn_fwd(name, proj, bias, sink, *, half, q_rows, n_chains, dil, nh, group, cq, ck, cv):
    s, w = proj.shape
    g = _Band(s, half, q_rows, n_chains, dil)
    has_sink = sink is not None

    def body(*refs):
        q_ref, k_ref, v_ref, b_ref = refs[:4]
        s_ref = refs[4] if has_sink else None
        o_ref, l_ref, ks, vs = refs[4 + has_sink:8 + has_sink]
        qs, os_, ls = refs[8 + has_sink:] if g.staged else (None, o_ref, l_ref)
        g.stage_kv(ks, k_ref)
        g.stage_kv(vs, v_ref)
        if g.staged:
            g.stage(qs, q_ref)
        bias_v = b_ref[...]
        sk = s_ref[pl.program_id(0)] if has_sink else None

        def chain(item):
            qoff, koff, edge = g.offsets(item)
            rows = pl.ds(qoff, g.q_rows)
            qv = qs[rows, :] if g.staged else q_ref[rows, :].astype(BF16)
            kw_ = ks[pl.ds(koff, g.win), :]
            vw_ = vs[pl.ds(koff, g.win), :]
            sc = lax.dot_general(qv, kw_, (((1,), (1,)), ((), ())), preferred_element_type=F32) * ATTN_SCALE
            sc = sc + bias_v + edge
            m = jnp.max(sc, axis=-1, keepdims=True)
            if has_sink:
                m = jnp.maximum(m, sk)
            p = jnp.exp(sc - m)
            den = jnp.sum(p, axis=-1, keepdims=True)
            if has_sink:
                den = den + jnp.exp(sk - m)
            out = lax.dot_general(p.astype(BF16), vw_, (((1,), (0,)), ((), ())), preferred_element_type=F32)
            return rows, out / den, jnp.broadcast_to(m + jnp.log(den), (g.q_rows, HEAD_DIM))

        def step(i, carry):
            for rows, out, lse in [chain(i * n_chains + u) for u in range(n_chains)]:
                os_[rows, :] = out
                ls[rows, :] = lse
            return carry

        lax.fori_loop(0, g.n_items // n_chains, step, 0)
        if g.staged:
            g.unstage(o_ref, os_)
            g.unstage(l_ref, ls)

    def col(c0, per):
        return pl.BlockSpec((s, HEAD_DIM), lambda h: (0, c0 // LANES + h // per))

    in_specs = [col(cq, 1), col(ck, group), col(cv, group),
                pl.BlockSpec((None, g.q_rows, g.win), lambda h: (h, 0, 0))]
    args = [proj, proj, proj, bias]
    if has_sink:
        in_specs.append(pl.BlockSpec(memory_space=pltpu.SMEM))
        args.append(sink)
    shape = pltpu.HBM((s, nh * HEAD_DIM), F32)
    scratch = [pltpu.VMEM((dil * g.pad, HEAD_DIM), BF16), pltpu.VMEM((dil * g.pad, HEAD_DIM), BF16)]
    if g.staged:
        scratch += [pltpu.VMEM((s, HEAD_DIM), BF16), pltpu.VMEM((s, HEAD_DIM), F32), pltpu.VMEM((s, HEAD_DIM), F32)]
    return pl.pallas_call(
        body, name=name, grid=(nh,), in_specs=in_specs, out_specs=[col(0, 1), col(0, 1)], out_shape=[shape, shape],
        scratch_shapes=scratch, compiler_params=_params(("parallel",), 16 * s * HEAD_DIM * 4))(*args)


def band_attn_bwd(name, proj, bias, sink, dout, out, lse, dlse, *, half, q_rows, n_chains, dil, nh, group, cq, ck, cv):
    s, w = proj.shape
    g = _Band(s, half, q_rows, n_chains, dil)
    nkv = nh // group
    has_sink = sink is not None
    has_dl = dlse is not None
    n_in = 7 + int(has_sink) + int(has_dl)
    n_out = 4 + int(has_sink)

    def body(*refs):
        ins, outs, scr = refs[:n_in], refs[n_in:n_in + n_out], refs[n_in + n_out:]
        q_ref, k_ref, v_ref, b_ref, do_ref, o_ref, l_ref = ins[:7]
        s_ref = ins[7] if has_sink else None
        dl_ref = ins[n_in - 1] if has_dl else None
        dq_ref, dk_ref, dv_ref, db_ref = outs[:4]
        ks, vs, dks, dvs = scr[:4]
        scr = list(scr[4:])
        dsa = scr.pop(0) if has_sink else None
        if g.staged:
            qs, dos, os_, ls, dqs = scr[:5]
            dls = scr[5] if has_dl else None
            g.stage(qs, q_ref)
            g.stage(dos, do_ref)
            g.stage(os_, o_ref)
            g.stage(ls, l_ref)
            if has_dl:
                g.stage(dls, dl_ref)
        else:
            qs, dos, os_, ls, dqs, dls = None, do_ref, o_ref, l_ref, dq_ref, dl_ref
        h = pl.program_id(0)
        g.stage_kv(ks, k_ref)
        g.stage_kv(vs, v_ref)
        dks[...] = jnp.zeros_like(dks)
        dvs[...] = jnp.zeros_like(dvs)
        db_ref[...] = jnp.zeros_like(db_ref)
        bias_v = b_ref[...]
        if has_sink:
            sk = s_ref[h]
            dsa[...] = jnp.zeros_like(dsa)

        def chain(item):
            qoff, koff, edge = g.offsets(item)
            rows = pl.ds(qoff, g.q_rows)
            win = pl.ds(koff, g.win)
            qv = qs[rows, :] if g.staged else q_ref[rows, :].astype(BF16)
            kw_ = ks[win, :]
            vw_ = vs[win, :]
            sc = lax.dot_general(qv, kw_, (((1,), (1,)), ((), ())), preferred_element_type=F32) * ATTN_SCALE
            lv = ls[rows, :][:, 0:1]
            p = jnp.exp(sc + bias_v + edge - lv)
            dov = dos[rows, :]
            delta = jnp.sum(dov * os_[rows, :], axis=-1, keepdims=True)
            dob = dov.astype(BF16)
            dp = lax.dot_general(dob, vw_, (((1,), (1,)), ((), ())), preferred_element_type=F32)
            t = dp - delta
            if has_dl:
                t = t + dls[rows, :][:, 0:1]
            ds = p * t
            dsb = (ds * ATTN_SCALE).astype(BF16)
            dq = lax.dot_general(dsb, kw_, (((1,), (0,)), ((), ())), preferred_element_type=F32)
            dkc = lax.dot_general(dsb, qv, (((0,), (0,)), ((), ())), preferred_element_type=F32)
            dvc = lax.dot_general(p.astype(BF16), dob, (((0,), (0,)), ((), ())), preferred_element_type=F32)
            dsk = jnp.exp(sk - lv) * delta if has_sink else None
            return rows, win, dq, dkc, dvc, ds, dsk

        def step(i, carry):
            res = [chain(i * n_chains + u) for u in range(n_chains)]
            ds_sum = res[0][5]
            for rr in res[1:]:
                ds_sum = ds_sum + rr[5]
            db_ref[...] += ds_sum
            for rows, win, dq, dkc, dvc, ds, dsk in res:
                dqs[rows, :] = dq
                dks[win, :] += dkc
                dvs[win, :] += dvc
                if has_sink:
                    dsa[...] += dsk
            return carry

        lax.fori_loop(0, g.n_items // n_chains, step, 0)

        if g.staged:
            g.unstage(dq_ref, dqs)

        def emit_kv(add):
            for r in range(dil):
                lo = r * g.pad + half
                for dst_ref, src in ((dk_ref, dks), (dv_ref, dvs)):
                    val = src[lo:lo + g.seg, :]
                    if add:
                        val = val + dst_ref[g.rows_of(r), :]
                    dst_ref[g.rows_of(r), :] = val

        if group == 1:
            emit_kv(False)
        else:
            @pl.when(h % group == 0)
            def _():
                emit_kv(False)

            @pl.when(h % group != 0)
            def _():
                emit_kv(True)
        if has_sink:
            outs[4][...] = jnp.full((SUBLANES, LANES), -jnp.sum(dsa[...]), F32)

    def col(c0, per):
        return pl.BlockSpec((s, HEAD_DIM), lambda h: (0, c0 // LANES + h // per))

    b_spec = pl.BlockSpec((None, g.q_rows, g.win), lambda h: (h, 0, 0))
    in_specs = [col(cq, 1), col(ck, group), col(cv, group), b_spec, col(0, 1), col(0, 1), col(0, 1)]
    args = [proj, proj, proj, bias, dout, out, lse]
    if has_sink:
        in_specs.append(pl.BlockSpec(memory_space=pltpu.SMEM))
        args.append(sink)
    if has_dl:
        in_specs.append(col(0, 1))
        args.append(dlse)
    out_specs = [col(0, 1), col(0, group), col(0, group), b_spec]
    out_shape = [pltpu.HBM((s, nh * HEAD_DIM), F32), pltpu.HBM((s, nkv * HEAD_DIM), F32),
                 pltpu.HBM((s, nkv * HEAD_DIM), F32), jax.ShapeDtypeStruct((nh, g.q_rows, g.win), F32)]
    scratch = [pltpu.VMEM((dil * g.pad, HEAD_DIM), BF16), pltpu.VMEM((dil * g.pad, HEAD_DIM), BF16),
               pltpu.VMEM((dil * g.pad, HEAD_DIM), F32), pltpu.VMEM((dil * g.pad, HEAD_DIM), F32)]
    if has_sink:
        out_specs.append(pl.BlockSpec((None, SUBLANES, LANES), lambda h: (h, 0, 0)))
        out_shape.append(jax.ShapeDtypeStruct((nh, SUBLANES, LANES), F32))
        scratch.append(pltpu.VMEM((g.q_rows, 1), F32))
    if g.staged:
        scratch += [pltpu.VMEM((s, HEAD_DIM), BF16)] + [pltpu.VMEM((s, HEAD_DIM), F32)] * (4 + int(has_dl))
    res = pl.pallas_call(
        body, name=name, grid=(nh,), in_specs=in_specs, out_specs=out_specs, out_shape=out_shape,
        scratch_shapes=scratch, compiler_params=_params(("arbitrary",), 28 * s * HEAD_DIM * 4))(*args)
    return res[0], res[1], res[2], res[3], (res[4] if has_sink else None)


def dil_merge_fwd(name, outs, lses):
    s, w = outs[0].shape
    ts = _pick(s, ROW_TILE_CANDS)
    ng = len(outs)

    def body(*refs):
        o_refs, l_refs, y_ref = refs[:ng], refs[ng:2 * ng], refs[2 * ng]
        ls = [l[...] for l in l_refs]
        mx = ls[0]
        for l in ls[1:]:
            mx = jnp.maximum(mx, l)
        es = [jnp.exp(l - mx) for l in ls]
        tot = es[0]
        for e in es[1:]:
            tot = tot + e
        acc = (es[0] / tot) * o_refs[0][...]
        for e, o in zip(es[1:], o_refs[1:]):
            acc = acc + (e / tot) * o[...]
        y_ref[...] = acc.astype(y_ref.dtype)

    row = pl.BlockSpec((ts, w), lambda i: (i, 0))
    return pl.pallas_call(
        body, name=name, grid=(s // ts,), in_specs=[row] * (2 * ng), out_specs=row,
        out_shape=pltpu.HBM((s, w), BF16),
        compiler_params=_params(("parallel",), 10 * ts * w * 4))(*outs, *lses)


def dil_merge_bwd(name, dy, outs, lses):
    s, w = outs[0].shape
    ts = _pick(s, ROW_TILE_CANDS)
    ng = len(outs)
    nhead = w // HEAD_DIM

    def body(*refs):
        dy_ref = refs[0]
        o_refs, l_refs = refs[1:1 + ng], refs[1 + ng:1 + 2 * ng]
        do_refs, dl_refs = refs[1 + 2 * ng:1 + 3 * ng], refs[1 + 3 * ng:1 + 4 * ng]
        for hh in range(nhead):
            cols = slice(hh * HEAD_DIM, (hh + 1) * HEAD_DIM)
            dyv = dy_ref[:, cols]
            ls = [l[:, cols] for l in l_refs]
            mx = ls[0]
            for l in ls[1:]:
                mx = jnp.maximum(mx, l)
            es = [jnp.exp(l - mx) for l in ls]
            tot = es[0]
            for e in es[1:]:
                tot = tot + e
            alphas = [e / tot for e in es]
            dal = [jnp.broadcast_to(jnp.sum(dyv * o[:, cols], axis=-1, keepdims=True), dyv.shape) for o in o_refs]
            mean = alphas[0] * dal[0]
            for a, d in zip(alphas[1:], dal[1:]):
                mean = mean + a * d
            for g in range(ng):
                do_refs[g][:, cols] = alphas[g] * dyv
                dl_refs[g][:, cols] = alphas[g] * (dal[g] - mean)

    row = pl.BlockSpec((ts, w), lambda i: (i, 0))
    shape = pltpu.HBM((s, w), F32)
    res = pl.pallas_call(
        body, name=name, grid=(s // ts,), in_specs=[row] * (1 + 2 * ng), out_specs=[row] * (2 * ng),
        out_shape=[shape] * (2 * ng),
        compiler_params=_params(("parallel",), 16 * ts * w * 4))(dy, *outs, *lses)
    return res[:ng], res[ng:]


def _adamw(w, g, m, v):
    m = ADAM_B1 * m + (1.0 - ADAM_B1) * g
    v = ADAM_B2 * v + (1.0 - ADAM_B2) * (g * g)
    m_hat = m / (1.0 - ADAM_B1 ** ADAM_STEP)
    v_hat = v / (1.0 - ADAM_B2 ** ADAM_STEP)
    delta = -ADAM_LR * (m_hat / (jnp.sqrt(v_hat) + ADAM_EPS) + ADAM_WD * w)
    return delta, m, v


def _row_tile(r, c, budget=1 << 20):
    if r * c * 4 <= budget or r % SUBLANES:
        return r
    for t in (1024, 512, 256, 128, 64, 32, 16, 8):
        if r % t == 0 and t * c * 4 <= budget:
            return t
    return SUBLANES


def adam_small(name, g, w, m, v):
    def body(g_ref, w_ref, m_ref, v_ref, d_ref, nm_ref, nv_ref):
        d_ref[...], nm_ref[...], nv_ref[...] = _adamw(w_ref[...], g_ref[...], m_ref[...], v_ref[...])

    shape = jax.ShapeDtypeStruct(w.shape, F32)
    return pl.pallas_call(body, name=name, out_shape=[shape, shape, shape])(g, w, m, v)


def reduce_adam(name, mine, theirs, w, m, v):
    nq, r, c = mine.shape
    tr = _row_tile(r, c)

    def body(*refs):
        parts, (w_ref, m_ref, v_ref, g_ref, d_ref, nm_ref, nv_ref) = refs[:nq], refs[nq:]
        g = parts[0][...].astype(F32)
        for p_ref in parts[1:]:
            g = g + p_ref[...].astype(F32)
        g_ref[...] = g
        d_ref[...], nm_ref[...], nv_ref[...] = _adamw(w_ref[...], g, m_ref[...], v_ref[...])

    def slot(q):
        return pl.BlockSpec((None, tr, c), lambda i: (q, i, 0))

    row = pl.BlockSpec((tr, c), lambda i: (i, 0))
    shape = jax.ShapeDtypeStruct((r, c), F32)
    return pl.pallas_call(
        body, name=name, grid=(r // tr,), in_specs=[slot(q) for q in range(nq)] + [row, row, row],
        out_specs=[row] * 4, out_shape=[shape] * 4,
        compiler_params=_params(("parallel",), (nq * 2 + 7 * 4) * tr * c))(mine, *[theirs] * (nq - 1), w, m, v)


def _place():
    return lax.axis_index("x"), lax.axis_index("y"), lax.axis_index("c")


def _flip(pos, bits):
    return tuple((1 - p) if b else p for p, b in zip(pos, bits))


def _index(pos):
    return 4 * pos[0] + 2 * pos[1] + pos[2]


ANY = pl.BlockSpec(memory_space=pl.ANY)


HBM = pl.BlockSpec(memory_space=pltpu.HBM)
SEM = pl.BlockSpec(memory_space=pltpu.SEMAPHORE)
EFFECT = pltpu.SideEffectType.DATAFLOW_SIDE_EFFECTING
TO_SIBLING = (0, 0, 1)
TO_CHIPS = [(1, 0, 0), (0, 1, 0), (1, 1, 0)]


def _in_hbm(a):
    return pltpu.with_memory_space_constraint(a, pltpu.HBM)


def _token_value(token):
    return token[0, 0]


def _when(pred, fn):
    if pred is True:
        fn()
    elif pred is not False:
        pl.when(pred)(fn)


def _plan_copy(k, entry, ins, lnd, send_sems, recv_sems):
    a, src_a, sblk, lblk, to, send_if, recv_if = entry
    src = lnd[a] if src_a is None else ins[src_a]
    return pltpu.make_async_remote_copy(
        src_ref=src.at[sblk], dst_ref=lnd[a].at[lblk], send_sem=send_sems.at[k], recv_sem=recv_sems.at[k],
        device_id=to, device_id_type=MESH), send_if, recv_if


def split_start(name, srcs, lands, plan, after):
    ns, nl = len(srcs), len(lands)
    n_copies = len(plan((0, 0, 0)))

    def body(*refs):
        ins, lnd = refs[:ns], refs[ns:ns + nl]
        send_sems, recv_sems = refs[ns + nl + 1], refs[ns + nl + 2]
        token = refs[-1]
        for k, entry in enumerate(plan(_place())):
            cp, send_if, _ = _plan_copy(k, entry, ins, lnd, send_sems, recv_sems)
            _when(send_if, cp.start)
        token[...] = jnp.zeros_like(token)

    outs = pl.pallas_call(
        body, name=name,
        out_shape=(pltpu.SemaphoreType.DMA((n_copies,)), pltpu.SemaphoreType.DMA((n_copies,)),
                   *[pltpu.HBM(a.shape, a.dtype) for a in srcs], *[pltpu.HBM(a.shape, a.dtype) for a in lands],
                   jax.ShapeDtypeStruct((SUBLANES, LANES), F32)),
        in_specs=[HBM] * (ns + nl) + [ANY],
        out_specs=(SEM, SEM, *[HBM] * (ns + nl), pl.BlockSpec(memory_space=pltpu.VMEM)),
        input_output_aliases={i: 2 + i for i in range(ns + nl)},
        compiler_params=pltpu.CompilerParams(has_side_effects=EFFECT),
    )(*[_in_hbm(a) for a in srcs], *[_in_hbm(a) for a in lands], after)
    return outs[0], outs[1], list(outs[2:2 + ns]), list(outs[2 + ns:2 + ns + nl]), outs[-1]


def split_wait(name, send_sems, recv_sems, srcs, lands, plan, after):
    ns, nl = len(srcs), len(lands)

    def body(*refs):
        ins, lnd = refs[:ns], refs[ns:ns + nl]
        s_sems, r_sems = refs[ns + nl], refs[ns + nl + 1]
        for k, entry in enumerate(plan(_place())):
            cp, send_if, recv_if = _plan_copy(k, entry, ins, lnd, s_sems, r_sems)
            _when(send_if, cp.wait_send)
            _when(recv_if, cp.wait_recv)
        refs[-1][...] = jnp.zeros((SUBLANES, LANES), F32)

    outs = pl.pallas_call(
        body, name=name,
        out_shape=(*[pltpu.HBM(a.shape, a.dtype) for a in srcs], *[pltpu.HBM(a.shape, a.dtype) for a in lands],
                   jax.ShapeDtypeStruct((SUBLANES, LANES), F32)),
        in_specs=[HBM] * (ns + nl) + [SEM, SEM, ANY],
        out_specs=(*[HBM] * (ns + nl), pl.BlockSpec(memory_space=pltpu.VMEM)),
        input_output_aliases={i: i for i in range(ns + nl)},
        compiler_params=pltpu.CompilerParams(has_side_effects=EFFECT),
    )(*srcs, *lands, send_sems, recv_sems, after)
    return list(outs[:ns]), list(outs[ns:ns + nl]), outs[-1]


NORTH = 1


def ag_plan(n, rels=TO_CHIPS):
    def plan(me):
        x, y, c = me
        entries = []
        for a in range(n):
            for t in (NORTH, 1 - NORTH):
                blk = _index((x, y, t))
                for rel in rels:
                    entries.append((a, None, blk, blk, _flip((x, y, t), rel), c == NORTH, c == t))
        return entries
    return plan


TO_X, TO_Y = TO_CHIPS[0], TO_CHIPS[1]


def relay_plan(n):
    def plan(me):
        x, y, c = me
        entries = []
        for a in range(n):
            for t, came, goes in ((NORTH, TO_X, TO_Y), (1 - NORTH, TO_Y, TO_X)):
                blk = _index(_flip((x, y, t), came))
                entries.append((a, None, blk, blk, _flip((x, y, t), goes), c == t, c == t))
        return entries
    return plan


def ag_pair(name, lands, after):
    n = len(lands)

    def body(*refs):
        lnd = refs[n + 1:2 * n + 1]
        token = refs[2 * n + 1]
        send_sems, recv_sems = refs[2 * n + 2:]
        token[...] = jnp.zeros_like(token)
        me = _place()
        sibling = _flip(me, TO_SIBLING)
        copies = []
        for a in range(n):
            mine, theirs = lnd[a].at[_index(me)], lnd[a].at[_index(sibling)]
            cp = pltpu.make_async_remote_copy(src_ref=mine, dst_ref=mine, send_sem=send_sems.at[a],
                                              recv_sem=recv_sems.at[a], device_id=sibling, device_id_type=MESH)
            cp.start()
            copies.append((cp, pltpu.make_async_remote_copy(
                src_ref=mine, dst_ref=theirs, send_sem=send_sems.at[a], recv_sem=recv_sems.at[a], device_id=sibling,
                device_id_type=MESH)))
        for cp, arrival in copies:
            arrival.wait_recv()
        for cp, arrival in copies:
            cp.wait_send()

    outs = pl.pallas_call(
        body, name=name, in_specs=[ANY] * (n + 1), out_specs=[ANY] * n + [pl.BlockSpec(memory_space=pltpu.VMEM)],
        out_shape=[jax.ShapeDtypeStruct(l.shape, l.dtype) for l in lands]
        + [jax.ShapeDtypeStruct((SUBLANES, LANES), F32)],
        input_output_aliases={a: a for a in range(n)},
        scratch_shapes=[pltpu.SemaphoreType.DMA((n,)), pltpu.SemaphoreType.DMA((n,))],
    )(*lands, after)
    return list(outs[:n]), outs[n]


def pass_plan(n):
    def plan(me):
        sibling = _flip(me, TO_SIBLING)
        return [(a, None, _index(_flip(me, rel)), _index(_flip(me, rel)), sibling, True, True)
                for a in range(n) for rel in TO_CHIPS]
    return plan


def ag_finish(name, lands):
    n = len(lands)

    def body(*refs):
        lnd = refs[n:2 * n]
        send_sems, recv_sems = refs[2 * n:]
        me = _place()
        sibling = _flip(me, TO_SIBLING)
        copies = []
        for a in range(n):
            for j, rel in enumerate(TO_CHIPS):
                blk = lnd[a].at[_index(_flip(me, rel))]
                there = lnd[a].at[_index(_flip(sibling, rel))]
                cp = pltpu.make_async_remote_copy(
                    src_ref=blk, dst_ref=blk, send_sem=send_sems.at[a * 3 + j], recv_sem=recv_sems.at[a * 3 + j],
                    device_id=sibling, device_id_type=MESH)
                cp.start()
                copies.append((cp, pltpu.make_async_remote_copy(
                    src_ref=blk, dst_ref=there, send_sem=send_sems.at[a * 3 + j], recv_sem=recv_sems.at[a * 3 + j],
                    device_id=sibling, device_id_type=MESH)))
        for cp, arrival in copies:
            arrival.wait_recv()
        for cp, arrival in copies:
            cp.wait_send()

    return pl.pallas_call(
        body, name=name, in_specs=[ANY] * n, out_specs=[ANY] * n,
        out_shape=[jax.ShapeDtypeStruct(l.shape, l.dtype) for l in lands],
        input_output_aliases={a: a for a in range(n)},
        scratch_shapes=[pltpu.SemaphoreType.DMA((3 * n,)), pltpu.SemaphoreType.DMA((3 * n,))],
    )(*lands)


REL = [(b >> 2 & 1, b >> 1 & 1, b & 1) for b in range(N_DEV)]


CHIP_REL = [(0, 0, 0)] + TO_CHIPS
N_CHIPS = len(CHIP_REL)


def rs_pair(name, parts):
    n = len(parts)

    def body(*refs):
        ins, got = refs[:n], refs[n:2 * n]
        send_sems, recv_sems = refs[2 * n:]
        me = _place()
        sibling = _flip(me, TO_SIBLING)
        remote = []
        for a in range(n):
            for q, rel in enumerate(CHIP_REL):
                k = a * N_CHIPS + q
                cp = pltpu.make_async_remote_copy(
                    src_ref=ins[a].at[_index(_flip(sibling, rel))], dst_ref=got[a].at[q], send_sem=send_sems.at[k],
                    recv_sem=recv_sems.at[k], device_id=sibling, device_id_type=MESH)
                cp.start()
                remote.append(cp)
        for cp in remote:
            cp.wait_recv()
        for cp in remote:
            cp.wait_send()

    shapes = [jax.ShapeDtypeStruct((N_CHIPS,) + tuple(p.shape[1:]), p.dtype) for p in parts]
    res = pl.pallas_call(
        body, name=name, in_specs=[ANY] * n, out_specs=[ANY] * n, out_shape=shapes,
        scratch_shapes=[pltpu.SemaphoreType.DMA((N_CHIPS * n,)), pltpu.SemaphoreType.DMA((N_CHIPS * n,))],
    )(*parts)
    return list(res)


def own_blocks():
    me = _place()
    return jnp.stack([_index(_flip(me, rel)) for rel in CHIP_REL]).astype(jnp.int32)


def pair_add(name, blocks, parts, got):
    nq, r, c = got.shape
    tr = _row_tile(r, c, budget=6 << 20)

    def body(blk_ref, a_ref, b_ref, o_ref):
        o_ref[...] = (a_ref[...].astype(F32) + b_ref[...].astype(F32)).astype(o_ref.dtype)

    spec = pl.BlockSpec((None, tr, c), lambda q, i, blk: (q, i, 0))
    return pl.pallas_call(
        body, name=name,
        grid_spec=pltpu.PrefetchScalarGridSpec(
            num_scalar_prefetch=1, grid=(nq, r // tr),
            in_specs=[pl.BlockSpec((None, tr, c), lambda q, i, blk: (blk[q], i, 0)), spec], out_specs=spec),
        out_shape=pltpu.HBM(got.shape, got.dtype),
        compiler_params=_params(("arbitrary", "arbitrary"), 6 * tr * c * 2))(blocks, parts, got)


def rs_pair_plan(n):
    def plan(me):
        sibling = _flip(me, TO_SIBLING)
        return [(a, a, _index(_flip(sibling, rel)), q, sibling, True, True)
                for a in range(n) for q, rel in enumerate(CHIP_REL)]
    return plan


def rs_plan(n):
    def plan(me):
        return [(a, a, q, q, _flip(me, CHIP_REL[q]), True, True) for a in range(n) for q in range(1, N_CHIPS)]
    return plan


def rs_start(name, sums, after):
    lands = [lax.empty(t.shape, t.dtype) for t in sums]
    return split_start(name, sums, lands, rs_plan(len(sums)), after)


def allreduce_small(name, pack, after):
    rows, lanes = pack.shape

    def body(x_ref, after_ref, o_ref, land, send_sems, recv_sems):
        me = _place()
        idx = _index(me)
        land[idx] = x_ref[...]
        copies = []
        for r in range(1, N_DEV):
            peer = _flip(me, REL[r])
            cp = pltpu.make_async_remote_copy(
                src_ref=x_ref, dst_ref=land.at[idx], send_sem=send_sems.at[r - 1], recv_sem=recv_sems.at[r - 1],
                device_id=peer, device_id_type=MESH)
            cp.start()
            copies.append(cp)
        for cp in copies:
            cp.wait_recv()
        for cp in copies:
            cp.wait_send()
        acc = land[0]
        for i in range(1, N_DEV):
            acc = acc + land[i]
        o_ref[...] = acc

    return pl.pallas_call(
        body, name=name, in_specs=[pl.BlockSpec(memory_space=pltpu.VMEM), ANY],
        out_specs=pl.BlockSpec(memory_space=pltpu.VMEM), out_shape=jax.ShapeDtypeStruct((rows, lanes), F32),
        scratch_shapes=[pltpu.VMEM((N_DEV, rows, lanes), F32), pltpu.SemaphoreType.DMA((7,)),
                        pltpu.SemaphoreType.DMA((7,))],
    )(pack, after)


def _pad_rows(a, rows):
    return jnp.pad(a, ((0, rows - a.shape[0]), (0, 0)))


def _as_tiles(vec):
    n = vec.shape[0]
    rows = -(-n // LANES)
    rows = -(-rows // SUBLANES) * SUBLANES
    return jnp.pad(vec, (0, rows * LANES - n)).reshape(rows, LANES)


def kernel(x, p, rel_bias_table, attn_norm, w_in, sink_a, w_branch_a, w_branch_b, w_out, ffn_norm, w_ffn_gate, w_ffn_up, conv_w, conv_b, w_ffn_down, ple_norm, w_ple_gate, w_ple_proj, final_norm, loss_target, m_rel_bias_table, m_attn_norm, m_w_in, m_sink_a, m_w_branch_a, m_w_branch_b, m_w_out, m_ffn_norm, m_w_ffn_gate, m_w_ffn_up, m_conv_w, m_conv_b, m_w_ffn_down, m_ple_norm, m_w_ple_gate, m_w_ple_proj, m_final_norm, v_rel_bias_table, v_attn_norm, v_w_in, v_sink_a, v_w_branch_a, v_w_branch_b, v_w_out, v_ffn_norm, v_w_ffn_gate, v_w_ffn_up, v_conv_w, v_conv_b, v_w_ffn_down, v_ple_norm, v_w_ple_gate, v_w_ple_proj, v_final_norm):
    xs = x[0]
    s, d = xs.shape
    ps = p[0, 0]
    target = loss_target[0]
    me = 4 * lax.axis_index("x") + 2 * lax.axis_index("y") + lax.axis_index("c")

    big = dict(w_in=w_in[0], w_branch_a=w_branch_a[0], w_branch_b=w_branch_b[0], w_out=w_out[0],
               w_ffn_gate=w_ffn_gate[0], w_ffn_up=w_ffn_up[0], w_ffn_down=w_ffn_down[0],
               w_ple_gate=w_ple_gate[0], w_ple_proj=w_ple_proj[0])
    big_m = dict(w_in=m_w_in[0], w_branch_a=m_w_branch_a[0], w_branch_b=m_w_branch_b[0], w_out=m_w_out[0],
                 w_ffn_gate=m_w_ffn_gate[0], w_ffn_up=m_w_ffn_up[0], w_ffn_down=m_w_ffn_down[0],
                 w_ple_gate=m_w_ple_gate[0], w_ple_proj=m_w_ple_proj[0])
    big_v = dict(w_in=v_w_in[0], w_branch_a=v_w_branch_a[0], w_branch_b=v_w_branch_b[0], w_out=v_w_out[0],
                 w_ffn_gate=v_w_ffn_gate[0], w_ffn_up=v_w_ffn_up[0], w_ffn_down=v_w_ffn_down[0],
                 w_ple_gate=v_w_ple_gate[0], w_ple_proj=v_w_ple_proj[0])
    names = list(big)
    nf = big["w_ffn_gate"].shape[1]

    shards = {k: big[k].astype(BF16) for k in names}
    shards["conv_w"] = _pad_rows(conv_w[0], SUBLANES)
    flipped = ("w_ffn_gate", "w_ffn_up")
    for k in flipped:
        big[k], big_m[k], big_v[k] = big[k].T, big_m[k].T, big_v[k].T
    ag_groups = [["w_in"], ["w_branch_a", "w_branch_b", "w_out"], ["w_ffn_gate", "conv_w"], ["w_ffn_up"],
                 ["w_ffn_down"], ["w_ple_gate", "w_ple_proj"]]
    ag_started = {}
    wg = {}

    ag_paired, ag_passing = {}, {}

    def pair(gi, after):
        lands = [lax.dynamic_update_index_in_dim(lax.empty((N_DEV,) + shards[k].shape, shards[k].dtype), shards[k],
                                                 me, 0) for k in ag_groups[gi]]
        ag_paired[gi], token = ag_pair(f"ag_pair{gi}", lands, after)
        return token

    def copies(gi):
        return ag_plan(len(ag_groups[gi]), [TO_X, TO_Y] if gi == 0 else TO_CHIPS)

    def start(gi, after):
        s_sems, r_sems, _, lands, token = split_start(f"ag_start{gi}", [], ag_paired[gi], copies(gi), after)
        ag_started[gi] = (s_sems, r_sems, lands)
        return token

    def landed(gi, after):
        s_sems, r_sems, lands = ag_started[gi]
        return split_wait(f"ag_wait{gi}", s_sems, r_sems, [], lands, copies(gi), after)[1:]

    def relayed(lands, after):
        plan = relay_plan(len(lands))
        s_sems, r_sems, _, lands, token = split_start("ag_relay0", [], lands, plan, after)
        return split_wait("ag_relayed0", s_sems, r_sems, [], lands, plan, token)[1:]

    def pass_on(gi, lands, after):
        s_sems, r_sems, _, lands, token = split_start(f"ag_pass{gi}", [], lands, pass_plan(len(lands)), after)
        ag_passing[gi] = (s_sems, r_sems, lands)
        return token

    def ready(gi, after):
        s_sems, r_sems, lands = ag_passing[gi]
        lands = split_wait(f"ag_ready{gi}", s_sems, r_sems, [], lands, pass_plan(len(lands)), after)[1]
        wg.update(zip(ag_groups[gi], lands))

    cb = conv_b.reshape(N_DEV, 1, nf)

    table_t = rel_bias_table.T
    geo_a = dict(half=A_BLOCK, q_rows=ATTN_Q_ROWS, n_chains=ATTN_CHAINS, dil=1, nh=A_Q_HEADS, group=A_GROUP,
                 cq=COL_QA, ck=COL_KA, cv=COL_VA)
    geo_b = [dict(half=B_BLOCK, q_rows=min(ATTN_Q_ROWS, s // dil), n_chains=ATTN_CHAINS, dil=dil,
                  nh=B_HEADS_PER_GROUP, group=1, cq=COL_QB + g * B_OUT_W, ck=COL_KB + g * B_OUT_W,
                  cv=COL_VB + g * B_OUT_W) for g, (_, dil) in enumerate(B_PATTERNS)]
    bucket_a = bucket_tile(geo_a["q_rows"], A_BLOCK, 1)
    bias_a = bias_build("bias_a", table_t, bucket_a, 0, A_Q_HEADS, A_BLOCK)
    buckets_b = [bucket_tile(gb["q_rows"], B_BLOCK, gb["dil"]) for gb in geo_b]
    biases_b = [bias_build(f"bias_b{g}", table_t, buckets_b[g], A_Q_HEADS + g * B_HEADS_PER_GROUP, B_HEADS_PER_GROUP,
                           B_BLOCK) for g in range(len(B_PATTERNS))]

    token = start(0, pair(0, xs))
    h = rms_fwd("rms_attn", xs, attn_norm + _token_value(token))
    lands0, token = landed(0, pair(5, pair(4, pair(3, pair(2, pair(1, h))))))
    lands0, token = relayed(lands0, token)
    token = start(5, start(4, start(3, start(2, start(1, token)))))
    wg["w_in"] = ag_finish("ag_finish0", lands0)[0]
    proj = mm_cols("proj_in", h, wg["w_in"], F32, fold=True, after=token)
    token = pass_on(1, landed(1, proj)[0], proj)
    sink = sink_a[0] + _token_value(token)
    ya, lse_a = band_attn_fwd("attn_a_fwd", proj, bias_a, sink, **geo_a)
    outs_b, lses_b = [], []
    for g in range(len(B_PATTERNS)):
        o, l = band_attn_fwd(f"attn_b{g}_fwd", proj, biases_b[g], None, **geo_b[g])
        outs_b.append(o)
        lses_b.append(l)
    yb = dil_merge_fwd("dil_merge_fwd", outs_b, lses_b)
    ready(1, yb)
    token = pass_on(2, landed(2, yb)[0], yb)
    w_out_full = wg["w_out"].reshape(d, d)
    ta = mm_cols("branch_a", ya, wg["w_branch_a"], F32, fold=True, after=token)
    tb = mm_cols("branch_b", yb, wg["w_branch_b"], F32, fold=True)
    merged = gate_merge_fwd("gate_merge_fwd", proj, ta, tb, d)
    x1 = mm_plain("mix_out", merged, w_out_full, F32, res=xs)

    hf = rms_fwd("rms_ffn", x1, ffn_norm)
    ready(2, hf)
    token = pass_on(3, landed(3, hf)[0], hf)
    cw = wg["conv_w"]
    gpre = mm_cols("ffn_gate", hf, wg["w_ffn_gate"], F32, fold=False, after=token)
    ready(3, gpre)
    token = pass_on(4, landed(4, gpre)[0], gpre)
    u = mm_cols("ffn_up", hf, wg["w_ffn_up"], F32, fold=False, after=token)
    z = ffn_mid_fwd("ffn_mid_fwd", gpre, u, cw, cb)
    ready(4, z)
    token = pass_on(5, landed(5, z)[0], z)
    x2 = mm_jsum("ffn_down", z, wg["w_ffn_down"], F32, res=x1, after=token)

    hp = rms_fwd("rms_ple", x2, ple_norm)
    ready(5, hp)
    w_pg_full = wg["w_ple_gate"].reshape(d, d)
    lp = mm_plain("ple_gate", hp, w_pg_full, F32)
    pp = mm_cols("ple_proj", ps, wg["w_ple_proj"], F32, fold=True)
    loss_part, dx3, dlp, dpp, d_final = tail_fwd_bwd("tail", x2, lp, pp, final_norm.reshape(1, d), target)

    grads = {}
    rs_started = []
    blocks = own_blocks()

    exchanging = []

    def exchange(tag, keys):
        parts = [grads[k] for k in keys]
        lands = [lax.empty((N_CHIPS,) + tuple(p.shape[1:]), p.dtype) for p in parts]
        s_sems, r_sems, parts, lands, token = split_start(f"rs_pair_{tag}", parts, lands, rs_pair_plan(len(keys)), blocks)
        exchanging.append((tag, keys, s_sems, r_sems, parts, lands))
        return _token_value(token)

    def send(after):
        tag, keys, s_sems, r_sems, parts, lands = exchanging.pop(0)
        parts, got, _ = split_wait(f"rs_paired_{tag}", s_sems, r_sems, parts, lands, rs_pair_plan(len(keys)), after)
        return send_sums(tag, keys, parts, got)

    def send_sums(tag, keys, parts, got):
        sums = [pair_add(f"pair_add_{k}", blocks, p, g) for k, p, g in zip(keys, parts, got)]
        s_sems, r_sems, srcs, lands, token = rs_start(f"rs_start_{tag}", sums, blocks)
        rs_started.append((tag, keys, s_sems, r_sems, srcs, lands))
        return token

    grads["w_ple_proj"] = mm_tn_cols("d_w_ple_proj", ps, dpp, N_DEV, big["w_ple_proj"].shape[1], BF16, folded=True)
    grads["w_ple_gate"] = mm_tn_plain("d_w_ple_gate", hp, dlp, BF16).reshape(N_DEV, d // N_DEV, d)
    tok = exchange("ple", ["w_ple_proj", "w_ple_gate"])
    dhp = mm_nt_plain("d_hp", dlp, w_pg_full, F32)
    dx2, dx2_b, d_ple = rms_bwd("rms_ple_bwd", x2, ple_norm + tok, dhp, dx3, True)

    dz = mm_nt_j("d_z", dx2_b, wg["w_ffn_down"], BF16)
    grads["w_ffn_down"] = mm_tn_j("d_w_ffn_down", z, dx2_b, BF16)
    tok = _token_value(send(dz)) + exchange("down", ["w_ffn_down"])
    du, dgpre, dcw = ffn_mid_bwd("ffn_mid_bwd", gpre, u, dz, cw, cb + tok)
    grads["w_ffn_up"] = mm_tn_j("d_w_ffn_up", du, hf, BF16)
    grads["w_ffn_gate"] = mm_tn_j("d_w_ffn_gate", dgpre, hf, BF16)
    dhf = mm_nt_jsum("d_hf_up", du, wg["w_ffn_up"], F32, folded=False)
    dhf = mm_nt_jsum("d_hf_gate", dgpre, wg["w_ffn_gate"], F32, folded=False, res=dhf)
    tok = _token_value(send(dhf)) + exchange("upgate", ["w_ffn_up", "w_ffn_gate"])
    dx1, dx1_b, d_ffn = rms_bwd("rms_ffn_bwd", x1, ffn_norm + tok, dhf, dx2, True)

    dmerged = mm_nt_plain("d_merged", dx1_b, w_out_full, F32)
    grads["w_out"] = mm_tn_plain("d_w_out", merged, dx1_b, BF16).reshape(N_DEV, d // N_DEV, d)
    dta, dtb, dga, dgb = gate_merge_bwd("gate_merge_bwd", dmerged, proj, ta, tb, d)
    grads["w_branch_a"] = mm_tn_cols("d_w_branch_a", ya, dta, N_DEV, big["w_branch_a"].shape[1], BF16, folded=True)
    grads["w_branch_b"] = mm_tn_cols("d_w_branch_b", yb, dtb, N_DEV, big["w_branch_b"].shape[1], BF16, folded=True)
    dya = mm_nt_jsum("d_ya", dta, wg["w_branch_a"], F32, folded=True)
    dyb = mm_nt_jsum("d_yb", dtb, wg["w_branch_b"], F32, folded=True)
    tok = _token_value(send(dyb)) + exchange("mix", ["w_out", "w_branch_a", "w_branch_b"])
    dqa, dka, dva, dbias_a, dsink = band_attn_bwd("attn_a_bwd", proj, bias_a, sink + tok, dya, ya, lse_a, None, **geo_a)
    douts_b, dlses_b = dil_merge_bwd("dil_merge_bwd", dyb, outs_b, lses_b)
    dq_b, dk_b, dv_b, dbias_b = [], [], [], []
    for g in range(len(B_PATTERNS)):
        dq, dk, dv, db, _ = band_attn_bwd(f"attn_b{g}_bwd", proj, biases_b[g], None, douts_b[g], outs_b[g], lses_b[g],
                                          dlses_b[g], **geo_b[g])
        dq_b.append(dq)
        dk_b.append(dk)
        dv_b.append(dv)
        dbias_b.append(db)
    dproj = jnp.concatenate([t.astype(BF16) for t in [dqa, dka, dva] + dq_b + dk_b + dv_b + [dga, dgb]], axis=1)
    token = send(dproj)
    grads["w_in"] = mm_tn_cols("d_w_in", h, dproj, N_DEV, big["w_in"].shape[1], BF16, folded=True, after=token)
    token = send_sums("in", ["w_in"], [grads["w_in"]], rs_pair("rs_pair_in", [grads["w_in"]]))
    dh = mm_nt_jsum("d_h", dproj, wg["w_in"], F32, folded=True, after=token)
    grad_x, _, d_attn = rms_bwd("rms_attn_bwd", xs, attn_norm, dh, dx1, False)

    dt_a = table_grad("table_grad_a", dbias_a, bucket_a)[:, 0, :N_BUCKETS]
    dt_b = [table_grad(f"table_grad_b{g}", dbias_b[g], buckets_b[g])[:, 0, :N_BUCKETS] for g in range(len(B_PATTERNS))]
    d_table_part = jnp.concatenate([dt_a] + dt_b, axis=0).T

    pieces = [
        ("loss", loss_part[0, :1]),
        ("table", d_table_part.reshape(-1)),
        ("attn_norm", d_attn.reshape(-1)),
        ("sink", dsink[:, 0, 0]),
        ("ffn_norm", d_ffn.reshape(-1)),
        ("conv_w", dcw[:, 0:3, :].reshape(-1)),
        ("conv_b", dcw[:, 3, :].reshape(-1)),
        ("ple_norm", d_ple.reshape(-1)),
        ("final_norm", d_final.reshape(-1)),
    ]
    tiles = [_as_tiles(v) for _, v in pieces]
    pack = jnp.concatenate(tiles, axis=0)

    out_g, out_d, out_m, out_v = {}, {}, {}, {}

    def finish(group, after):
        tag, keys, s_sems, r_sems, srcs, lands = group
        srcs, lands, _ = split_wait(f"rs_wait_{tag}", s_sems, r_sems, srcs, lands, rs_plan(len(keys)), after)
        for k, mine, theirs in zip(keys, srcs, lands):
            res = reduce_adam("adam_" + k, mine, theirs, big[k], big_m[k], big_v[k])
            after = res[1]
            out_g[k], out_d[k], out_m[k], out_v[k] = [(t.T if k in flipped else t)[None] for t in res]
        return after

    after = pack
    for group in rs_started[:-1]:
        after = finish(group, after)
    total = allreduce_small("allreduce_small", pack, after)
    finish(rs_started[-1], total)
    small = {}
    row = 0
    for (nm, v), t in zip(pieces, tiles):
        small[nm] = total[row:row + t.shape[0]].reshape(-1)[:v.shape[0]]
        row += t.shape[0]
    loss = small["loss"][0]
    g_small = dict(
        rel_bias_table=small["table"].reshape(rel_bias_table.shape),
        attn_norm=small["attn_norm"].reshape(attn_norm.shape),
        sink_a=small["sink"].reshape(sink_a.shape),
        ffn_norm=small["ffn_norm"].reshape(ffn_norm.shape),
        conv_w=lax.dynamic_index_in_dim(small["conv_w"].reshape(N_DEV, 3, nf), me, 0, keepdims=False)[None],
        conv_b=small["conv_b"].reshape(conv_b.shape),
        ple_norm=small["ple_norm"].reshape(ple_norm.shape),
        final_norm=small["final_norm"].reshape(1, d),
    )
    w_small = dict(rel_bias_table=(rel_bias_table, m_rel_bias_table, v_rel_bias_table),
                   attn_norm=(attn_norm, m_attn_norm, v_attn_norm), sink_a=(sink_a, m_sink_a, v_sink_a),
                   ffn_norm=(ffn_norm, m_ffn_norm, v_ffn_norm), conv_w=(conv_w, m_conv_w, v_conv_w),
                   conv_b=(conv_b, m_conv_b, v_conv_b), ple_norm=(ple_norm, m_ple_norm, v_ple_norm),
                   final_norm=(final_norm, m_final_norm, v_final_norm))

    for k, (wv, mv, vv) in w_small.items():
        shape = wv.shape
        two_d = (1, shape[0]) if len(shape) == 1 else ((shape[0] * shape[1], shape[2]) if len(shape) == 3 else shape)
        gk = g_small[k].reshape(two_d)
        dl, nm, nv = adam_small("adam_" + k, gk, wv.reshape(two_d), mv.reshape(two_d), vv.reshape(two_d))
        out_g[k], out_d[k], out_m[k], out_v[k] = gk.reshape(shape), dl.reshape(shape), nm.reshape(shape), nv.reshape(shape)

    order = ["rel_bias_table", "attn_norm", "w_in", "sink_a", "w_branch_a", "w_branch_b", "w_out", "ffn_norm",
             "w_ffn_gate", "w_ffn_up", "conv_w", "conv_b", "w_ffn_down", "ple_norm", "w_ple_gate", "w_ple_proj",
             "final_norm"]
    return (loss, grad_x[None], *[out_g[k] for k in order], *[out_d[k] for k in order],
            *[out_m[k] for k in order], *[out_v[k] for k in order])
```

```python
import math

import jax
import jax.numpy as jnp
from jax import lax
from jax.experimental import pallas as pl
from jax.experimental.pallas import tpu as pltpu

F32 = jnp.float32
BF16 = jnp.bfloat16
MESH = pl.DeviceIdType.MESH
N_DEV = 8

HEAD_DIM = 128
A_Q_HEADS = 8
A_KV_HEADS = 2
A_GROUP = A_Q_HEADS // A_KV_HEADS
A_BLOCK = 128
B_PATTERNS = ((128, 1), (512, 4), (2048, 16))
B_HEADS_PER_GROUP = 4
B_HEADS = len(B_PATTERNS) * B_HEADS_PER_GROUP
B_BLOCK = 64
N_BUCKETS = 32
MAX_DISTANCE = 1024
A_Q_W = A_Q_HEADS * HEAD_DIM
A_KV_W = A_KV_HEADS * HEAD_DIM
B_W = B_HEADS * HEAD_DIM
B_OUT_W = B_HEADS_PER_GROUP * HEAD_DIM
COL_QA = 0
COL_KA = COL_QA + A_Q_W
COL_VA = COL_KA + A_KV_W
COL_QB = COL_VA + A_KV_W
COL_KB = COL_QB + B_W
COL_VB = COL_KB + B_W
COL_GATES = COL_VB + B_W
RMS_EPS = 1e-6
NEG_INF = -1e30
ATTN_SCALE = HEAD_DIM ** -0.5
ATTN_Q_ROWS = 256
ATTN_CHAINS = 4

ADAM_LR = 0.001
ADAM_B1 = 0.9
ADAM_B2 = 0.999
ADAM_EPS = 1e-08
ADAM_WD = 0.01
ADAM_STEP = 10

GELU_C = math.sqrt(2.0 / math.pi)
GELU_A = 0.044715

V7X_VMEM_BYTES = 64 * 1024 * 1024
VMEM_CEILING = V7X_VMEM_BYTES - 8 * 1024 * 1024
LANES = 128
SUBLANES = 8


def _pick(n, cands):
    for c in cands:
        if n % c == 0:
            return c
    return n


def _nbytes(shape, dtype):
    n = 1
    for d in shape:
        if d is not None:
            n *= d
    return n * jnp.dtype(dtype).itemsize


def _params(sem, est_bytes):
    limit = int(min(VMEM_CEILING, max(32 * 1024 * 1024, 2 * est_bytes + (8 << 20))))
    return pltpu.CompilerParams(dimension_semantics=sem, vmem_limit_bytes=limit)


def _mm(name, a, b, a_bs, a_im, b_bs, b_im, out_shape, out_dtype, o_bs, o_im, grid, dims,
        res=None, r_bs=None, r_im=None, after=None):
    nk = grid[-1]
    nax = len(grid)
    has_res = res is not None
    has_after = after is not None
    o_tile = tuple(d for d in o_bs if d is not None)

    def body(*refs):
        a_ref, b_ref = refs[:2]
        r_ref = refs[2] if has_res else None
        n_in = 2 + has_res + has_after
        o_ref = refs[n_in]
        rest = refs[n_in + 1:]

        def prod():
            return lax.dot_general(a_ref[...].astype(BF16), b_ref[...].astype(BF16), (dims, ((), ())),
                                   preferred_element_type=F32)

        def finish(r):
            if r_ref is not None:
                r = r + r_ref[...].astype(F32)
            o_ref[...] = r.astype(o_ref.dtype)

        if nk == 1:
            finish(prod())
        else:
            acc = rest[0]
            k = pl.program_id(nax - 1)

            @pl.when(k == 0)
            def _():
                acc[...] = prod()

            @pl.when(k > 0)
            def _():
                acc[...] += prod()

            @pl.when(k == nk - 1)
            def _():
                finish(acc[...])

    in_specs = [pl.BlockSpec(a_bs, a_im), pl.BlockSpec(b_bs, b_im)]
    args = [a, b]
    est = _nbytes(a_bs, a.dtype) + _nbytes(b_bs, b.dtype) + _nbytes(o_bs, out_dtype) + 2 * _nbytes(o_tile, F32)
    if has_res:
        in_specs.append(pl.BlockSpec(r_bs, r_im))
        args.append(res)
        est += _nbytes(r_bs, res.dtype)
    if has_after:
        in_specs.append(pl.BlockSpec(memory_space=pl.ANY))
        args.append(after)
    scratch = [] if nk == 1 else [pltpu.VMEM(o_tile, F32)]
    sem = ("parallel",) * (nax - 1) + ("arbitrary",)
    return pl.pallas_call(
        body, name=name, grid=grid, in_specs=in_specs, out_specs=pl.BlockSpec(o_bs, o_im),
        out_shape=pltpu.HBM(out_shape, out_dtype), scratch_shapes=scratch,
        compiler_params=_params(sem, est))(*args)


TM_CANDS = (1024, 512, 256, 128, 64, 32, 16, 8)
TM_WIDE_CANDS = (2048,) + TM_CANDS
TK_CANDS = (1024, 512, 256, 128)
TN_CANDS = (1024, 512, 256, 128)


def mm_cols(name, a, wg, out_dtype, fold, after=None):
    m, k = a.shape
    nj, _, n = wg.shape
    tm, tk = _pick(m, TM_WIDE_CANDS), _pick(k, TK_CANDS)
    grid = (nj, m // tm, k // tk)
    if fold:
        shape, o_bs, o_im = (m, nj * n), (tm, n), (lambda j, i, kk: (i, j))
    else:
        shape, o_bs, o_im = (nj, m, n), (None, tm, n), (lambda j, i, kk: (j, i, 0))
    return _mm(name, a, wg, (tm, tk), lambda j, i, kk: (i, kk), (None, tk, n), lambda j, i, kk: (j, kk, 0),
               shape, out_dtype, o_bs, o_im, grid, ((1,), (0,)), after=after)


def mm_plain(name, a, w, out_dtype, res=None):
    m, k = a.shape
    n = w.shape[1]
    tm, tk, tn = _pick(m, TM_CANDS), _pick(k, TK_CANDS), _pick(n, TN_CANDS)
    grid = (n // tn, m // tm, k // tk)
    return _mm(name, a, w, (tm, tk), lambda j, i, kk: (i, kk), (tk, tn), lambda j, i, kk: (kk, j),
               (m, n), out_dtype, (tm, tn), lambda j, i, kk: (i, j), grid, ((1,), (0,)),
               res, (tm, tn), lambda j, i, kk: (i, j))


def mm_jsum(name, aj, wg, out_dtype, res=None, after=None):
    nj, m, ka = aj.shape
    n = wg.shape[2]
    tm, tn = _pick(m, TM_CANDS), _pick(n, TN_CANDS)
    grid = (m // tm, n // tn, nj)
    return _mm(name, aj, wg, (None, tm, ka), lambda i, jn, j: (j, i, 0), (None, ka, tn), lambda i, jn, j: (j, 0, jn),
               (m, n), out_dtype, (tm, tn), lambda i, jn, j: (i, jn), grid, ((1,), (0,)),
               res, (tm, tn), lambda i, jn, j: (i, jn), after=after)


def mm_tn_cols(name, a, g, nj, n, out_dtype, folded, after=None):
    s, kw = a.shape
    ts, tkw = _pick(s, TK_CANDS), _pick(kw, TM_CANDS)
    grid = (nj, kw // tkw, s // ts)
    if folded:
        g_bs, g_im = (ts, n), (lambda j, i, ss: (ss, j))
    else:
        g_bs, g_im = (None, ts, n), (lambda j, i, ss: (j, ss, 0))
    return _mm(name, a, g, (ts, tkw), lambda j, i, ss: (ss, i), g_bs, g_im,
               (nj, kw, n), out_dtype, (None, tkw, n), lambda j, i, ss: (j, i, 0), grid, ((0,), (0,)), after=after)


def mm_tn_plain(name, a, g, out_dtype):
    s, kw = a.shape
    n = g.shape[1]
    ts, tkw, tn = _pick(s, TK_CANDS), _pick(kw, TM_CANDS), _pick(n, TN_CANDS)
    grid = (kw // tkw, n // tn, s // ts)
    return _mm(name, a, g, (ts, tkw), lambda i, jn, ss: (ss, i), (ts, tn), lambda i, jn, ss: (ss, jn),
               (kw, n), out_dtype, (tkw, tn), lambda i, jn, ss: (i, jn), grid, ((0,), (0,)))


def mm_tn_j(name, aj, g, out_dtype):
    nj, s, ka = aj.shape
    n = g.shape[1]
    ts, tn = _pick(s, TK_CANDS), _pick(n, TN_CANDS)
    grid = (nj, n // tn, s // ts)
    return _mm(name, aj, g, (None, ts, ka), lambda j, jn, ss: (j, ss, 0), (ts, tn), lambda j, jn, ss: (ss, jn),
               (nj, ka, n), out_dtype, (None, ka, tn), lambda j, jn, ss: (j, 0, jn), grid, ((0,), (0,)))


def mm_nt_plain(name, g, w, out_dtype):
    m, n = g.shape
    k = w.shape[0]
    tm, tn, tkk = _pick(m, TM_CANDS), _pick(n, TK_CANDS), _pick(k, TN_CANDS)
    grid = (k // tkk, m // tm, n // tn)
    return _mm(name, g, w, (tm, tn), lambda kk, i, jn: (i, jn), (tkk, tn), lambda kk, i, jn: (kk, jn),
               (m, k), out_dtype, (tm, tkk), lambda kk, i, jn: (i, kk), grid, ((1,), (1,)))


def mm_nt_j(name, g, wg, out_dtype):
    m, n = g.shape
    nj, ka, _ = wg.shape
    tm, tn = _pick(m, TM_CANDS), _pick(n, TK_CANDS)
    grid = (nj, m // tm, n // tn)
    return _mm(name, g, wg, (tm, tn), lambda j, i, jn: (i, jn), (None, ka, tn), lambda j, i, jn: (j, 0, jn),
               (nj, m, ka), out_dtype, (None, tm, ka), lambda j, i, jn: (j, i, 0), grid, ((1,), (1,)))


def mm_nt_jsum(name, g, wg, out_dtype, folded, res=None, after=None):
    nj, k, n = wg.shape
    m = g.shape[0] if folded else g.shape[1]
    tm, tkk = _pick(m, TM_CANDS if res is not None else TM_WIDE_CANDS), _pick(k, TN_CANDS)
    grid = (m // tm, k // tkk, nj)
    if folded:
        g_bs, g_im = (tm, n), (lambda i, kk, j: (i, j))
    else:
        g_bs, g_im = (None, tm, n), (lambda i, kk, j: (j, i, 0))
    return _mm(name, g, wg, g_bs, g_im, (None, tkk, n), lambda i, kk, j: (j, kk, 0),
               (m, k), out_dtype, (tm, tkk), lambda i, kk, j: (i, kk), grid, ((1,), (1,)),
               res, (tm, tkk), lambda i, kk, j: (i, kk), after=after)


ROW_TILE_CANDS = (256, 128, 64, 32, 16, 8)


def _rstd(x):
    return lax.rsqrt(jnp.mean(x * x, axis=-1, keepdims=True) + RMS_EPS)


def _sigmoid(t):
    return 1.0 / (1.0 + jnp.exp(-t))


def rms_fwd(name, x, gain):
    s, d = x.shape
    ts = _pick(s, ROW_TILE_CANDS)

    def body(x_ref, g_ref, h_ref):
        xv = x_ref[...]
        h_ref[...] = ((xv * _rstd(xv)) * g_ref[...]).astype(h_ref.dtype)

    return pl.pallas_call(
        body, name=name, grid=(s // ts,),
        in_specs=[pl.BlockSpec((ts, d), lambda i: (i, 0)), pl.BlockSpec((1, d), lambda i: (0, 0))],
        out_specs=pl.BlockSpec((ts, d), lambda i: (i, 0)),
        out_shape=pltpu.HBM((s, d), BF16),
        compiler_params=_params(("parallel",), 3 * ts * d * 4))(x, gain)


def rms_bwd(name, x, gain, dh, dres, bf16_copy):
    s, d = x.shape
    ts = _pick(s, ROW_TILE_CANDS)

    def body(x_ref, g_ref, dh_ref, dr_ref, dx_ref, *rest):
        dxb_ref, dg_ref = rest if bf16_copy else (None, rest[0])
        xv = x_ref[...]
        r = _rstd(xv)
        xhat = xv * r
        dhv = dh_ref[...].astype(F32)
        dxhat = dhv * g_ref[...]
        dx = dr_ref[...] + r * (dxhat - xhat * jnp.mean(dxhat * xhat, axis=-1, keepdims=True))
        dx_ref[...] = dx
        if bf16_copy:
            dxb_ref[...] = dx.astype(dxb_ref.dtype)
        part = jnp.sum(dhv * xhat, axis=0, keepdims=True)

        @pl.when(pl.program_id(0) == 0)
        def _():
            dg_ref[...] = part

        @pl.when(pl.program_id(0) > 0)
        def _():
            dg_ref[...] += part

    row = pl.BlockSpec((ts, d), lambda i: (i, 0))
    vec = pl.BlockSpec((1, d), lambda i: (0, 0))
    copy_spec, copy_shape = ([row], [pltpu.HBM((s, d), BF16)]) if bf16_copy else ([], [])
    res = pl.pallas_call(
        body, name=name, grid=(s // ts,), in_specs=[row, vec, row, row], out_specs=[row] + copy_spec + [vec],
        out_shape=[pltpu.HBM((s, d), F32)] + copy_shape + [jax.ShapeDtypeStruct((1, d), F32)],
        compiler_params=_params(("arbitrary",), 7 * ts * d * 4))(x, gain, dh, dres)
    return (res[0], res[1], res[2]) if bf16_copy else (res[0], None, res[1])


def gate_merge_fwd(name, proj, ta, tb, d):
    s = proj.shape[0]
    ts = _pick(s, ROW_TILE_CANDS)
    cb = COL_GATES // d

    def body(ga_ref, gb_ref, ta_ref, tb_ref, o_ref):
        o_ref[...] = (_sigmoid(ga_ref[...]) * ta_ref[...] + _sigmoid(gb_ref[...]) * tb_ref[...]).astype(o_ref.dtype)

    row = pl.BlockSpec((ts, d), lambda i: (i, 0))
    return pl.pallas_call(
        body, name=name, grid=(s // ts,),
        in_specs=[pl.BlockSpec((ts, d), lambda i: (i, cb)), pl.BlockSpec((ts, d), lambda i: (i, cb + 1)), row, row],
        out_specs=row, out_shape=pltpu.HBM((s, d), BF16),
        compiler_params=_params(("parallel",), 5 * ts * d * 4))(proj, proj, ta, tb)


def gate_merge_bwd(name, dmerged, proj, ta, tb, d):
    s = proj.shape[0]
    ts = _pick(s, ROW_TILE_CANDS)
    cb = COL_GATES // d

    def body(dm_ref, ga_ref, gb_ref, ta_ref, tb_ref, dta_ref, dtb_ref, dga_ref, dgb_ref):
        dm = dm_ref[...]
        sa = _sigmoid(ga_ref[...])
        sb = _sigmoid(gb_ref[...])
        dta_ref[...] = (dm * sa).astype(dta_ref.dtype)
        dtb_ref[...] = (dm * sb).astype(dtb_ref.dtype)
        dga_ref[...] = (dm * ta_ref[...] * (sa * (1.0 - sa))).astype(dga_ref.dtype)
        dgb_ref[...] = (dm * tb_ref[...] * (sb * (1.0 - sb))).astype(dgb_ref.dtype)

    row = pl.BlockSpec((ts, d), lambda i: (i, 0))
    out = pltpu.HBM((s, d), BF16)
    return pl.pallas_call(
        body, name=name, grid=(s // ts,),
        in_specs=[row, pl.BlockSpec((ts, d), lambda i: (i, cb)), pl.BlockSpec((ts, d), lambda i: (i, cb + 1)), row, row],
        out_specs=[row, row, row, row], out_shape=[out, out, out, out],
        compiler_params=_params(("parallel",), 8 * ts * d * 4))(dmerged, proj, proj, ta, tb)


def tail_fwd_bwd(name, x2, lp, pp, gain, target):
    s, d = x2.shape
    ts = _pick(s, ROW_TILE_CANDS)

    def body(x2_ref, lp_ref, pp_ref, g_ref, t_ref, loss_ref, dx3_ref, dlp_ref, dpp_ref, dg_ref):
        gp = _sigmoid(lp_ref[...])
        ppv = pp_ref[...]
        x3 = x2_ref[...] + gp * ppv
        r = _rstd(x3)
        xhat = x3 * r
        gv = g_ref[...]
        err = xhat * gv - t_ref[...]
        loss = jnp.sum(err * err) * (0.5 / d)
        dy = err * (1.0 / d)
        dxhat = dy * gv
        dx3 = r * (dxhat - xhat * jnp.mean(dxhat * xhat, axis=-1, keepdims=True))
        dx3_ref[...] = dx3
        dlp_ref[...] = (dx3 * ppv * (gp * (1.0 - gp))).astype(dlp_ref.dtype)
        dpp_ref[...] = (dx3 * gp).astype(dpp_ref.dtype)
        part = jnp.sum(dy * xhat, axis=0, keepdims=True)
        lossv = jnp.full((1, LANES), loss, F32)

        @pl.when(pl.program_id(0) == 0)
        def _():
            dg_ref[...] = part
            loss_ref[...] = lossv

        @pl.when(pl.program_id(0) > 0)
        def _():
            dg_ref[...] += part
            loss_ref[...] += lossv

    row = pl.BlockSpec((ts, d), lambda i: (i, 0))
    vec = pl.BlockSpec((1, d), lambda i: (0, 0))
    return pl.pallas_call(
        body, name=name, grid=(s // ts,), in_specs=[row, row, row, vec, row],
        out_specs=[pl.BlockSpec((1, LANES), lambda i: (0, 0)), row, row, row, vec],
        out_shape=[jax.ShapeDtypeStruct((1, LANES), F32), pltpu.HBM((s, d), F32),
                   pltpu.HBM((s, d), BF16), pltpu.HBM((s, d), BF16),
                   jax.ShapeDtypeStruct((1, d), F32)],
        compiler_params=_params(("arbitrary",), 9 * ts * d * 4))(x2, lp, pp, gain, target)


HALO = SUBLANES
BF16_ROWS = 2 * SUBLANES


def _shift_rows(cur, prev_row, next_row):
    ts = cur.shape[0]
    rid = lax.broadcasted_iota(jnp.int32, cur.shape, 0)
    down = jnp.where(rid == 0, prev_row, pltpu.roll(cur, 1, 0))
    up = jnp.where(rid == ts - 1, next_row, pltpu.roll(cur, ts - 1, 0))
    return down, up


def _halo_specs(ts, s, nf, halo=HALO):
    nb = ts // halo
    last = s // halo - 1
    cur = pl.BlockSpec((None, ts, nf), lambda j, i: (j, i, 0))
    prev = pl.BlockSpec((None, halo, nf), lambda j, i: (j, jnp.maximum(i * nb - 1, 0), 0))
    nxt = pl.BlockSpec((None, halo, nf), lambda j, i: (j, jnp.minimum((i + 1) * nb, last), 0))
    return cur, prev, nxt


def _halo_rows(prev_ref, next_ref, n_tiles):
    i = pl.program_id(1)
    prev_row = jnp.where(i == 0, 0.0, prev_ref[HALO - 1:HALO, :].astype(F32))
    next_row = jnp.where(i == n_tiles - 1, 0.0, next_ref[0:1, :].astype(F32))
    return prev_row, next_row


def _gelu(g):
    t = jnp.tanh(GELU_C * (g + GELU_A * (g * g * g)))
    return 0.5 * g * (1.0 + t), t


def _conv(cur, down, up, cw_ref, cb_ref):
    return down * cw_ref[0:1, :] + cur * cw_ref[1:2, :] + up * cw_ref[2:3, :] + cb_ref[...]


def ffn_mid_fwd(name, gpre, u, cw, cb):
    nj, s, nf = gpre.shape
    ts = _pick(s, (512, 256, 128, 64, 32, 16, 8))
    n_tiles = s // ts
    cur, prev, nxt = _halo_specs(ts, s, nf)

    def body(g_ref, gp_ref, gn_ref, u_ref, cw_ref, cb_ref, z_ref):
        gv = g_ref[...]
        down, up = _shift_rows(gv, *_halo_rows(gp_ref, gn_ref, n_tiles))
        act, _ = _gelu(_conv(gv, down, up, cw_ref, cb_ref))
        z_ref[...] = (act * u_ref[...]).astype(z_ref.dtype)

    return pl.pallas_call(
        body, name=name, grid=(nj, n_tiles),
        in_specs=[cur, prev, nxt, cur, pl.BlockSpec((None, SUBLANES, nf), lambda j, i: (j, 0, 0)),
                  pl.BlockSpec((None, 1, nf), lambda j, i: (j, 0, 0))],
        out_specs=cur, out_shape=pltpu.HBM((nj, s, nf), BF16),
        compiler_params=_params(("parallel", "parallel"), 8 * ts * nf * 4))(gpre, gpre, gpre, u, cw, cb)


def _gelu_grad(g, t):
    return 0.5 * (1.0 + t) + 0.5 * g * (1.0 - t * t) * (GELU_C * (1.0 + 3.0 * GELU_A * (g * g)))


def ffn_mid_bwd(name, gpre, u, dz, cw, cb):
    nj, s, nf = gpre.shape
    ts = _pick(s, (512, 256, 128, 64, 32, 16, 8))
    n_tiles = s // ts
    cur, prev, nxt = _halo_specs(ts, s, nf)

    def body(g_ref, gp_ref, gn_ref, u_ref, up_ref, un_ref, dz_ref, dzp_ref, dzn_ref, cw_ref, cb_ref,
             du_ref, dgp_ref, dcw_ref):
        i = pl.program_id(1)
        w0, w1, w2, bias = cw_ref[0:1, :], cw_ref[1:2, :], cw_ref[2:3, :], cb_ref[...]
        gv = g_ref[...]
        down, up = _shift_rows(gv, *_halo_rows(gp_ref, gn_ref, n_tiles))
        gc = down * w0 + gv * w1 + up * w2 + bias
        act, t = _gelu(gc)
        dzv = dz_ref[...].astype(F32)
        du_ref[...] = (dzv * act).astype(du_ref.dtype)
        dg = dzv * u_ref[...] * _gelu_grad(gc, t)

        def edge_dg(g_before, g_at, g_after, u_at, dz_at):
            ge = g_before.astype(F32) * w0 + g_at.astype(F32) * w1 + g_after.astype(F32) * w2 + bias
            return dz_at.astype(F32) * u_at.astype(F32) * _gelu_grad(ge, _gelu(ge)[1])

        dz_before = dzp_ref[...].astype(F32)[BF16_ROWS - 1:BF16_ROWS, :]
        dz_after = dzn_ref[...].astype(F32)[0:1, :]
        dg_prev = jnp.where(i == 0, 0.0, edge_dg(gp_ref[HALO - 2:HALO - 1, :], gp_ref[HALO - 1:HALO, :], gv[0:1, :],
                                                 up_ref[HALO - 1:HALO, :], dz_before))
        dg_next = jnp.where(i == n_tiles - 1, 0.0, edge_dg(gv[ts - 1:ts, :], gn_ref[0:1, :], gn_ref[1:2, :],
                                                           un_ref[0:1, :], dz_after))
        dg_down, dg_up = _shift_rows(dg, dg_prev, dg_next)
        dgp_ref[...] = (dg_up * w0 + dg * w1 + dg_down * w2).astype(dgp_ref.dtype)
        rows = [jnp.sum(dg * down, axis=0, keepdims=True), jnp.sum(dg * gv, axis=0, keepdims=True),
                jnp.sum(dg * up, axis=0, keepdims=True), jnp.sum(dg, axis=0, keepdims=True)]
        part = jnp.concatenate(rows + [jnp.zeros((SUBLANES - len(rows), nf), F32)], axis=0)

        @pl.when(i == 0)
        def _():
            dcw_ref[...] = part

        @pl.when(i > 0)
        def _():
            dcw_ref[...] += part

    small = pl.BlockSpec((None, SUBLANES, nf), lambda j, i: (j, 0, 0))
    return pl.pallas_call(
        body, name=name, grid=(nj, n_tiles),
        in_specs=[cur, prev, nxt] * 2 + list(_halo_specs(ts, s, nf, BF16_ROWS))
        + [small, pl.BlockSpec((None, 1, nf), lambda j, i: (j, 0, 0))],
        out_specs=[cur, cur, small],
        out_shape=[pltpu.HBM((nj, s, nf), BF16), pltpu.HBM((nj, s, nf), BF16),
                   jax.ShapeDtypeStruct((nj, SUBLANES, nf), F32)],
        compiler_params=_params(("parallel", "arbitrary"), 14 * ts * nf * 4))(
            gpre, gpre, gpre, u, u, u, dz, dz, dz, cw, cb)


def _t5_bucket(rel):
    half = N_BUCKETS // 2
    max_exact = half // 2
    n = jnp.abs(rel)
    side = jnp.where(rel > 0, half, 0)
    nf = jnp.maximum(n, 1).astype(F32)
    large = max_exact + (jnp.log(nf / max_exact) / math.log(MAX_DISTANCE / max_exact)
                         * (half - max_exact)).astype(jnp.int32)
    large = jnp.minimum(large, half - 1)
    return side + jnp.where(n < max_exact, n, large)


def bucket_tile(rows, half, dil):
    rel = (jnp.arange(rows + 2 * half)[None, :] - half) - jnp.arange(rows)[:, None]
    return _t5_bucket(rel * dil).astype(jnp.int32)


def bias_build(name, table_t, bucket, h0, nh, half):
    blk, kw = bucket.shape

    def body(t_ref, b_ref, o_ref):
        h = pl.program_id(0)
        bv = b_ref[...]
        acc = jnp.zeros((blk, kw), F32)
        for b in range(N_BUCKETS):
            acc = jnp.where(bv == b, t_ref[h0 + h, b], acc)
        qi = lax.broadcasted_iota(jnp.int32, (blk, kw), 0)
        ci = lax.broadcasted_iota(jnp.int32, (blk, kw), 1)
        o_ref[...] = jnp.where(jnp.abs(ci - half - qi) <= half, acc, NEG_INF)

    return pl.pallas_call(
        body, name=name, grid=(nh,),
        in_specs=[pl.BlockSpec(memory_space=pltpu.SMEM), pl.BlockSpec((blk, kw), lambda h: (0, 0))],
        out_specs=pl.BlockSpec((None, blk, kw), lambda h: (h, 0, 0)),
        out_shape=jax.ShapeDtypeStruct((nh, blk, kw), F32),
        compiler_params=_params(("parallel",), 4 * blk * kw * 4))(table_t, bucket)


def table_grad(name, dbias, bucket):
    nh, blk, kw = dbias.shape

    def body(d_ref, b_ref, o_ref):
        bv = b_ref[...]
        dv = d_ref[...]
        lane = lax.broadcasted_iota(jnp.int32, (SUBLANES, LANES), 1)
        acc = jnp.zeros((SUBLANES, LANES), F32)
        for b in range(N_BUCKETS):
            acc = jnp.where(lane == b, jnp.sum(jnp.where(bv == b, dv, 0.0)), acc)
        o_ref[...] = acc

    return pl.pallas_call(
        body, name=name, grid=(nh,),
        in_specs=[pl.BlockSpec((None, blk, kw), lambda h: (h, 0, 0)), pl.BlockSpec((blk, kw), lambda h: (0, 0))],
        out_specs=pl.BlockSpec((None, SUBLANES, LANES), lambda h: (h, 0, 0)),
        out_shape=jax.ShapeDtypeStruct((nh, SUBLANES, LANES), F32),
        compiler_params=_params(("parallel",), 4 * blk * kw * 4))(dbias, bucket)


class _Band:
    def __init__(self, s, half, q_rows, n_chains, dil):
        self.s, self.half, self.dil, self.n_chains = s, half, dil, n_chains
        self.seg = s // dil
        self.q_rows = min(q_rows, self.seg)
        self.win = self.q_rows + 2 * half
        self.pad = self.seg + 2 * half
        self.nsb = self.seg // self.q_rows
        self.n_items = dil * self.nsb
        assert self.n_items % n_chains == 0 and self.seg % self.q_rows == 0
        self.staged = dil > 1

    def rows_of(self, r):
        return pl.ds(r, self.seg, stride=self.dil) if self.dil > 1 else slice(None)

    def stage_kv(self, dst, src_ref):
        zeros = jnp.zeros((self.half, HEAD_DIM), dst.dtype)
        for r in range(self.dil):
            base = r * self.pad
            dst[base:base + self.half, :] = zeros
            dst[base + self.half + self.seg:base + self.pad, :] = zeros
            dst[base + self.half:base + self.half + self.seg, :] = src_ref[self.rows_of(r), :].astype(dst.dtype)

    def stage(self, dst, src_ref):
        for r in range(self.dil):
            dst[r * self.seg:(r + 1) * self.seg, :] = src_ref[self.rows_of(r), :].astype(dst.dtype)

    def unstage(self, dst_ref, src, add=False):
        for r in range(self.dil):
            val = src[r * self.seg:(r + 1) * self.seg, :].astype(dst_ref.dtype)
            if add:
                val = val + dst_ref[self.rows_of(r), :]
            dst_ref[self.rows_of(r), :] = val

    def offsets(self, item):
        r, sb = item // self.nsb, item % self.nsb
        qoff = pl.multiple_of(r * self.seg + sb * self.q_rows, self.q_rows)
        koff = pl.multiple_of(r * self.pad + sb * self.q_rows, B_BLOCK)
        kpos = sb * self.q_rows - self.half + lax.broadcasted_iota(jnp.int32, (1, self.win), 1)
        edge = jnp.where((kpos >= 0) & (kpos < self.seg), 0.0, NEG_INF)
        return qoff, koff, edge


def band_attn_fwd(name, proj, bias, sink, *, half, q_rows, n_chains, dil, nh, group, cq, ck, cv):
    s, w = proj.shape
    g = _Band(s, half, q_rows, n_chains, dil)
    has_sink = sink is not None

    def body(*refs):
        q_ref, k_ref, v_ref, b_ref = refs[:4]
        s_ref = refs[4] if has_sink else None
        o_ref, l_ref, ks, vs = refs[4 + has_sink:8 + has_sink]
        qs, os_, ls = refs[8 + has_sink:] if g.staged else (None, o_ref, l_ref)
        g.stage_kv(ks, k_ref)
        g.stage_kv(vs, v_ref)
        if g.staged:
            g.stage(qs, q_ref)
        bias_v = b_ref[...]
        sk = s_ref[pl.program_id(0)] if has_sink else None

        def chain(item):
            qoff, koff, edge = g.offsets(item)
            rows = pl.ds(qoff, g.q_rows)
            qv = qs[rows, :] if g.staged else q_ref[rows, :].astype(BF16)
            kw_ = ks[pl.ds(koff, g.win), :]
            vw_ = vs[pl.ds(koff, g.win), :]
            sc = lax.dot_general(qv, kw_, (((1,), (1,)), ((), ())), preferred_element_type=F32) * ATTN_SCALE
            sc = sc + bias_v + edge
            m = jnp.max(sc, axis=-1, keepdims=True)
            if has_sink:
                m = jnp.maximum(m, sk)
            p = jnp.exp(sc - m)
            den = jnp.sum(p, axis=-1, keepdims=True)
            if has_sink:
                den = den + jnp.exp(sk - m)
            out = lax.dot_general(p.astype(BF16), vw_, (((1,), (0,)), ((), ())), preferred_element_type=F32)
            return rows, out / den, jnp.broadcast_to(m + jnp.log(den), (g.q_rows, HEAD_DIM))

        def step(i, carry):
            for rows, out, lse in [chain(i * n_chains + u) for u in range(n_chains)]:
                os_[rows, :] = out
                ls[rows, :] = lse
            return carry

        lax.fori_loop(0, g.n_items // n_chains, step, 0)
        if g.staged:
            g.unstage(o_ref, os_)
            g.unstage(l_ref, ls)

    def col(c0, per):
        return pl.BlockSpec((s, HEAD_DIM), lambda h: (0, c0 // LANES + h // per))

    in_specs = [col(cq, 1), col(ck, group), col(cv, group),
                pl.BlockSpec((None, g.q_rows, g.win), lambda h: (h, 0, 0))]
    args = [proj, proj, proj, bias]
    if has_sink:
        in_specs.append(pl.BlockSpec(memory_space=pltpu.SMEM))
        args.append(sink)
    shape = pltpu.HBM((s, nh * HEAD_DIM), F32)
    scratch = [pltpu.VMEM((dil * g.pad, HEAD_DIM), BF16), pltpu.VMEM((dil * g.pad, HEAD_DIM), BF16)]
    if g.staged:
        scratch += [pltpu.VMEM((s, HEAD_DIM), BF16), pltpu.VMEM((s, HEAD_DIM), F32), pltpu.VMEM((s, HEAD_DIM), F32)]
    return pl.pallas_call(
        body, name=name, grid=(nh,), in_specs=in_specs, out_specs=[col(0, 1), col(0, 1)], out_shape=[shape, shape],
        scratch_shapes=scratch, compiler_params=_params(("parallel",), 16 * s * HEAD_DIM * 4))(*args)


def band_attn_bwd(name, proj, bias, sink, dout, out, lse, dlse, *, half, q_rows, n_chains, dil, nh, group, cq, ck, cv):
    s, w = proj.shape
    g = _Band(s, half, q_rows, n_chains, dil)
    nkv = nh // group
    has_sink = sink is not None
    has_dl = dlse is not None
    n_in = 7 + int(has_sink) + int(has_dl)
    n_out = 4 + int(has_sink)

    def body(*refs):
        ins, outs, scr = refs[:n_in], refs[n_in:n_in + n_out], refs[n_in + n_out:]
        q_ref, k_ref, v_ref, b_ref, do_ref, o_ref, l_ref = ins[:7]
        s_ref = ins[7] if has_sink else None
        dl_ref = ins[n_in - 1] if has_dl else None
        dq_ref, dk_ref, dv_ref, db_ref = outs[:4]
        ks, vs, dks, dvs = scr[:4]
        scr = list(scr[4:])
        dsa = scr.pop(0) if has_sink else None
        if g.staged:
            qs, dos, os_, ls, dqs = scr[:5]
            dls = scr[5] if has_dl else None
            g.stage(qs, q_ref)
            g.stage(dos, do_ref)
            g.stage(os_, o_ref)
            g.stage(ls, l_ref)
            if has_dl:
                g.stage(dls, dl_ref)
        else:
            qs, dos, os_, ls, dqs, dls = None, do_ref, o_ref, l_ref, dq_ref, dl_ref
        h = pl.program_id(0)
        g.stage_kv(ks, k_ref)
        g.stage_kv(vs, v_ref)
        dks[...] = jnp.zeros_like(dks)
        dvs[...] = jnp.zeros_like(dvs)
        db_ref[...] = jnp.zeros_like(db_ref)
        bias_v = b_ref[...]
        if has_sink:
            sk = s_ref[h]
            dsa[...] = jnp.zeros_like(dsa)

        def chain(item):
            qoff, koff, edge = g.offsets(item)
            rows = pl.ds(qoff, g.q_rows)
            win = pl.ds(koff, g.win)
            qv = qs[rows, :] if g.staged else q_ref[rows, :].astype(BF16)
            kw_ = ks[win, :]
            vw_ = vs[win, :]
            sc = lax.dot_general(qv, kw_, (((1,), (1,)), ((), ())), preferred_element_type=F32) * ATTN_SCALE
            lv = ls[rows, :][:, 0:1]
            p = jnp.exp(sc + bias_v + edge - lv)
            dov = dos[rows, :]
            delta = jnp.sum(dov * os_[rows, :], axis=-1, keepdims=True)
            dob = dov.astype(BF16)
            dp = lax.dot_general(dob, vw_, (((1,), (1,)), ((), ())), preferred_element_type=F32)
            t = dp - delta
            if has_dl:
                t = t + dls[rows, :][:, 0:1]
            ds = p * t
            dsb = (ds * ATTN_SCALE).astype(BF16)
            dq = lax.dot_general(dsb, kw_, (((1,), (0,)), ((), ())), preferred_element_type=F32)
            dkc = lax.dot_general(dsb, qv, (((0,), (0,)), ((), ())), preferred_element_type=F32)
            dvc = lax.dot_general(p.astype(BF16), dob, (((0,), (0,)), ((), ())), preferred_element_type=F32)
            dsk = jnp.exp(sk - lv) * delta if has_sink else None
            return rows, win, dq, dkc, dvc, ds, dsk

        def step(i, carry):
            res = [chain(i * n_chains + u) for u in range(n_chains)]
            ds_sum = res[0][5]
            for rr in res[1:]:
                ds_sum = ds_sum + rr[5]
            db_ref[...] += ds_sum
            for rows, win, dq, dkc, dvc, ds, dsk in res:
                dqs[rows, :] = dq
                dks[win, :] += dkc
                dvs[win, :] += dvc
                if has_sink:
                    dsa[...] += dsk
            return carry

        lax.fori_loop(0, g.n_items // n_chains, step, 0)

        if g.staged:
            g.unstage(dq_ref, dqs)

        def emit_kv(add):
            for r in range(dil):
                lo = r * g.pad + half
                for dst_ref, src in ((dk_ref, dks), (dv_ref, dvs)):
                    val = src[lo:lo + g.seg, :]
                    if add:
                        val = val + dst_ref[g.rows_of(r), :]
                    dst_ref[g.rows_of(r), :] = val

        if group == 1:
            emit_kv(False)
        else:
            @pl.when(h % group == 0)
            def _():
                emit_kv(False)

            @pl.when(h % group != 0)
            def _():
                emit_kv(True)
        if has_sink:
            outs[4][...] = jnp.full((SUBLANES, LANES), -jnp.sum(dsa[...]), F32)

    def col(c0, per):
        return pl.BlockSpec((s, HEAD_DIM), lambda h: (0, c0 // LANES + h // per))

    b_spec = pl.BlockSpec((None, g.q_rows, g.win), lambda h: (h, 0, 0))
    in_specs = [col(cq, 1), col(ck, group), col(cv, group), b_spec, col(0, 1), col(0, 1), col(0, 1)]
    args = [proj, proj, proj, bias, dout, out, lse]
    if has_sink:
        in_specs.append(pl.BlockSpec(memory_space=pltpu.SMEM))
        args.append(sink)
    if has_dl:
        in_specs.append(col(0, 1))
        args.append(dlse)
    out_specs = [col(0, 1), col(0, group), col(0, group), b_spec]
    out_shape = [pltpu.HBM((s, nh * HEAD_DIM), F32), pltpu.HBM((s, nkv * HEAD_DIM), F32),
                 pltpu.HBM((s, nkv * HEAD_DIM), F32), jax.ShapeDtypeStruct((nh, g.q_rows, g.win), F32)]
    scratch = [pltpu.VMEM((dil * g.pad, HEAD_DIM), BF16), pltpu.VMEM((dil * g.pad, HEAD_DIM), BF16),
               pltpu.VMEM((dil * g.pad, HEAD_DIM), F32), pltpu.VMEM((dil * g.pad, HEAD_DIM), F32)]
    if has_sink:
        out_specs.append(pl.BlockSpec((None, SUBLANES, LANES), lambda h: (h, 0, 0)))
        out_shape.append(jax.ShapeDtypeStruct((nh, SUBLANES, LANES), F32))
        scratch.append(pltpu.VMEM((g.q_rows, 1), F32))
    if g.staged:
        scratch += [pltpu.VMEM((s, HEAD_DIM), BF16)] + [pltpu.VMEM((s, HEAD_DIM), F32)] * (4 + int(has_dl))
    res = pl.pallas_call(
        body, name=name, grid=(nh,), in_specs=in_specs, out_specs=out_specs, out_shape=out_shape,
        scratch_shapes=scratch, compiler_params=_params(("arbitrary",), 28 * s * HEAD_DIM * 4))(*args)
    return res[0], res[1], res[2], res[3], (res[4] if has_sink else None)


def dil_merge_fwd(name, outs, lses):
    s, w = outs[0].shape
    ts = _pick(s, ROW_TILE_CANDS)
    ng = len(outs)

    def body(*refs):
        o_refs, l_refs, y_ref = refs[:ng], refs[ng:2 * ng], refs[2 * ng]
        ls = [l[...] for l in l_refs]
        mx = ls[0]
        for l in ls[1:]:
            mx = jnp.maximum(mx, l)
        es = [jnp.exp(l - mx) for l in ls]
        tot = es[0]
        for e in es[1:]:
            tot = tot + e
        acc = (es[0] / tot) * o_refs[0][...]
        for e, o in zip(es[1:], o_refs[1:]):
            acc = acc + (e / tot) * o[...]
        y_ref[...] = acc.astype(y_ref.dtype)

    row = pl.BlockSpec((ts, w), lambda i: (i, 0))
    return pl.pallas_call(
        body, name=name, grid=(s // ts,), in_specs=[row] * (2 * ng), out_specs=row,
        out_shape=pltpu.HBM((s, w), BF16),
        compiler_params=_params(("parallel",), 10 * ts * w * 4))(*outs, *lses)


def dil_merge_bwd(name, dy, outs, lses):
    s, w = outs[0].shape
    ts = _pick(s, ROW_TILE_CANDS)
    ng = len(outs)
    nhead = w // HEAD_DIM

    def body(*refs):
        dy_ref = refs[0]
        o_refs, l_refs = refs[1:1 + ng], refs[1 + ng:1 + 2 * ng]
        do_refs, dl_refs = refs[1 + 2 * ng:1 + 3 * ng], refs[1 + 3 * ng:1 + 4 * ng]
        for hh in range(nhead):
            cols = slice(hh * HEAD_DIM, (hh + 1) * HEAD_DIM)
            dyv = dy_ref[:, cols]
            ls = [l[:, cols] for l in l_refs]
            mx = ls[0]
            for l in ls[1:]:
                mx = jnp.maximum(mx, l)
            es = [jnp.exp(l - mx) for l in ls]
            tot = es[0]
            for e in es[1:]:
                tot = tot + e
            alphas = [e / tot for e in es]
            dal = [jnp.broadcast_to(jnp.sum(dyv * o[:, cols], axis=-1, keepdims=True), dyv.shape) for o in o_refs]
            mean = alphas[0] * dal[0]
            for a, d in zip(alphas[1:], dal[1:]):
                mean = mean + a * d
            for g in range(ng):
                do_refs[g][:, cols] = alphas[g] * dyv
                dl_refs[g][:, cols] = alphas[g] * (dal[g] - mean)

    row = pl.BlockSpec((ts, w), lambda i: (i, 0))
    shape = pltpu.HBM((s, w), F32)
    res = pl.pallas_call(
        body, name=name, grid=(s // ts,), in_specs=[row] * (1 + 2 * ng), out_specs=[row] * (2 * ng),
        out_shape=[shape] * (2 * ng),
        compiler_params=_params(("parallel",), 16 * ts * w * 4))(dy, *outs, *lses)
    return res[:ng], res[ng:]


def _adamw(w, g, m, v):
    m = ADAM_B1 * m + (1.0 - ADAM_B1) * g
    v = ADAM_B2 * v + (1.0 - ADAM_B2) * (g * g)
    m_hat = m / (1.0 - ADAM_B1 ** ADAM_STEP)
    v_hat = v / (1.0 - ADAM_B2 ** ADAM_STEP)
    delta = -ADAM_LR * (m_hat / (jnp.sqrt(v_hat) + ADAM_EPS) + ADAM_WD * w)
    return delta, m, v


def _row_tile(r, c, budget=1 << 20):
    if r * c * 4 <= budget or r % SUBLANES:
        return r
    for t in (1024, 512, 256, 128, 64, 32, 16, 8):
        if r % t == 0 and t * c * 4 <= budget:
            return t
    return SUBLANES


def adam_small(name, g, w, m, v):
    def body(g_ref, w_ref, m_ref, v_ref, d_ref, nm_ref, nv_ref):
        d_ref[...], nm_ref[...], nv_ref[...] = _adamw(w_ref[...], g_ref[...], m_ref[...], v_ref[...])

    shape = jax.ShapeDtypeStruct(w.shape, F32)
    return pl.pallas_call(body, name=name, out_shape=[shape, shape, shape])(g, w, m, v)


def reduce_adam(name, mine, theirs, w, m, v):
    nq, r, c = mine.shape
    tr = _row_tile(r, c)

    def body(*refs):
        parts, (w_ref, m_ref, v_ref, g_ref, d_ref, nm_ref, nv_ref) = refs[:nq], refs[nq:]
        g = parts[0][...].astype(F32)
        for p_ref in parts[1:]:
            g = g + p_ref[...].astype(F32)
        g_ref[...] = g
        d_ref[...], nm_ref[...], nv_ref[...] = _adamw(w_ref[...], g, m_ref[...], v_ref[...])

    def slot(q):
        return pl.BlockSpec((None, tr, c), lambda i: (q, i, 0))

    row = pl.BlockSpec((tr, c), lambda i: (i, 0))
    shape = jax.ShapeDtypeStruct((r, c), F32)
    return pl.pallas_call(
        body, name=name, grid=(r // tr,), in_specs=[slot(q) for q in range(nq)] + [row, row, row],
        out_specs=[row] * 4, out_shape=[shape] * 4,
        compiler_params=_params(("parallel",), (nq * 2 + 7 * 4) * tr * c))(mine, *[theirs] * (nq - 1), *[_in_hbm(t) for t in (w, m, v)])


def _place():
    return lax.axis_index("x"), lax.axis_index("y"), lax.axis_index("c")


def _flip(pos, bits):
    return tuple((1 - p) if b else p for p, b in zip(pos, bits))


def _index(pos):
    return 4 * pos[0] + 2 * pos[1] + pos[2]


ANY = pl.BlockSpec(memory_space=pl.ANY)


HBM = pl.BlockSpec(memory_space=pltpu.HBM)
SEM = pl.BlockSpec(memory_space=pltpu.SEMAPHORE)
EFFECT = pltpu.SideEffectType.DATAFLOW_SIDE_EFFECTING
TO_SIBLING = (0, 0, 1)
TO_CHIPS = [(1, 0, 0), (0, 1, 0), (1, 1, 0)]


def _in_hbm(a):
    return pltpu.with_memory_space_constraint(a, pltpu.HBM)


def _token_value(token):
    return token[0, 0]


def _when(pred, fn):
    if pred is True:
        fn()
    elif pred is not False:
        pl.when(pred)(fn)


def _plan_copy(k, entry, ins, lnd, send_sems, recv_sems):
    a, src_a, sblk, lblk, to, send_if, recv_if = entry
    src = lnd[a] if src_a is None else ins[src_a]
    return pltpu.make_async_remote_copy(
        src_ref=src.at[sblk], dst_ref=lnd[a].at[lblk], send_sem=send_sems.at[k], recv_sem=recv_sems.at[k],
        device_id=to, device_id_type=MESH), send_if, recv_if


def split_start(name, srcs, lands, plan, after):
    ns, nl = len(srcs), len(lands)
    n_copies = len(plan((0, 0, 0)))

    def body(*refs):
        ins, lnd = refs[:ns], refs[ns:ns + nl]
        send_sems, recv_sems = refs[ns + nl + 1], refs[ns + nl + 2]
        token = refs[-1]
        for k, entry in enumerate(plan(_place())):
            cp, send_if, _ = _plan_copy(k, entry, ins, lnd, send_sems, recv_sems)
            _when(send_if, cp.start)
        token[...] = jnp.zeros_like(token)

    outs = pl.pallas_call(
        body, name=name,
        out_shape=(pltpu.SemaphoreType.DMA((n_copies,)), pltpu.SemaphoreType.DMA((n_copies,)),
                   *[pltpu.HBM(a.shape, a.dtype) for a in srcs], *[pltpu.HBM(a.shape, a.dtype) for a in lands],
                   jax.ShapeDtypeStruct((SUBLANES, LANES), F32)),
        in_specs=[HBM] * (ns + nl) + [ANY],
        out_specs=(SEM, SEM, *[HBM] * (ns + nl), pl.BlockSpec(memory_space=pltpu.VMEM)),
        input_output_aliases={i: 2 + i for i in range(ns + nl)},
        compiler_params=pltpu.CompilerParams(has_side_effects=EFFECT),
    )(*[_in_hbm(a) for a in srcs], *[_in_hbm(a) for a in lands], after)
    return outs[0], outs[1], list(outs[2:2 + ns]), list(outs[2 + ns:2 + ns + nl]), outs[-1]


def split_wait(name, send_sems, recv_sems, srcs, lands, plan, after):
    ns, nl = len(srcs), len(lands)

    def body(*refs):
        ins, lnd = refs[:ns], refs[ns:ns + nl]
        s_sems, r_sems = refs[ns + nl], refs[ns + nl + 1]
        for k, entry in enumerate(plan(_place())):
            cp, send_if, recv_if = _plan_copy(k, entry, ins, lnd, s_sems, r_sems)
            _when(send_if, cp.wait_send)
            _when(recv_if, cp.wait_recv)
        refs[-1][...] = jnp.zeros((SUBLANES, LANES), F32)

    outs = pl.pallas_call(
        body, name=name,
        out_shape=(*[pltpu.HBM(a.shape, a.dtype) for a in srcs], *[pltpu.HBM(a.shape, a.dtype) for a in lands],
                   jax.ShapeDtypeStruct((SUBLANES, LANES), F32)),
        in_specs=[HBM] * (ns + nl) + [SEM, SEM, ANY],
        out_specs=(*[HBM] * (ns + nl), pl.BlockSpec(memory_space=pltpu.VMEM)),
        input_output_aliases={i: i for i in range(ns + nl)},
        compiler_params=pltpu.CompilerParams(has_side_effects=EFFECT),
    )(*srcs, *lands, send_sems, recv_sems, after)
    return list(outs[:ns]), list(outs[ns:ns + nl]), outs[-1]


NORTH = 1


def ag_plan(n, rels=TO_CHIPS):
    def plan(me):
        x, y, c = me
        entries = []
        for a in range(n):
            for t in (NORTH, 1 - NORTH):
                blk = _index((x, y, t))
                for rel in rels:
                    entries.append((a, None, blk, blk, _flip((x, y, t), rel), c == NORTH, c == t))
        return entries
    return plan


TO_X, TO_Y = TO_CHIPS[0], TO_CHIPS[1]


def relay_plan(n):
    def plan(me):
        x, y, c = me
        entries = []
        for a in range(n):
            for t, came, goes in ((NORTH, TO_X, TO_Y), (1 - NORTH, TO_Y, TO_X)):
                blk = _index(_flip((x, y, t), came))
                entries.append((a, None, blk, blk, _flip((x, y, t), goes), c == t, c == t))
        return entries
    return plan


def ag_pair(name, lands, after):
    n = len(lands)

    def body(*refs):
        lnd = refs[n + 1:2 * n + 1]
        token = refs[2 * n + 1]
        send_sems, recv_sems = refs[2 * n + 2:]
        token[...] = jnp.zeros_like(token)
        me = _place()
        sibling = _flip(me, TO_SIBLING)
        copies = []
        for a in range(n):
            mine, theirs = lnd[a].at[_index(me)], lnd[a].at[_index(sibling)]
            cp = pltpu.make_async_remote_copy(src_ref=mine, dst_ref=mine, send_sem=send_sems.at[a],
                                              recv_sem=recv_sems.at[a], device_id=sibling, device_id_type=MESH)
            cp.start()
            copies.append((cp, pltpu.make_async_remote_copy(
                src_ref=mine, dst_ref=theirs, send_sem=send_sems.at[a], recv_sem=recv_sems.at[a], device_id=sibling,
                device_id_type=MESH)))
        for cp, arrival in copies:
            arrival.wait_recv()
        for cp, arrival in copies:
            cp.wait_send()

    outs = pl.pallas_call(
        body, name=name, in_specs=[ANY] * (n + 1), out_specs=[ANY] * n + [pl.BlockSpec(memory_space=pltpu.VMEM)],
        out_shape=[jax.ShapeDtypeStruct(l.shape, l.dtype) for l in lands]
        + [jax.ShapeDtypeStruct((SUBLANES, LANES), F32)],
        input_output_aliases={a: a for a in range(n)},
        scratch_shapes=[pltpu.SemaphoreType.DMA((n,)), pltpu.SemaphoreType.DMA((n,))],
    )(*lands, after)
    return list(outs[:n]), outs[n]


def pass_plan(n):
    def plan(me):
        sibling = _flip(me, TO_SIBLING)
        return [(a, None, _index(_flip(me, rel)), _index(_flip(me, rel)), sibling, True, True)
                for a in range(n) for rel in TO_CHIPS]
    return plan


def ag_finish(name, lands):
    n = len(lands)

    def body(*refs):
        lnd = refs[n:2 * n]
        send_sems, recv_sems = refs[2 * n:]
        me = _place()
        sibling = _flip(me, TO_SIBLING)
        copies = []
        for a in range(n):
            for j, rel in enumerate(TO_CHIPS):
                blk = lnd[a].at[_index(_flip(me, rel))]
                there = lnd[a].at[_index(_flip(sibling, rel))]
                cp = pltpu.make_async_remote_copy(
                    src_ref=blk, dst_ref=blk, send_sem=send_sems.at[a * 3 + j], recv_sem=recv_sems.at[a * 3 + j],
                    device_id=sibling, device_id_type=MESH)
                cp.start()
                copies.append((cp, pltpu.make_async_remote_copy(
                    src_ref=blk, dst_ref=there, send_sem=send_sems.at[a * 3 + j], recv_sem=recv_sems.at[a * 3 + j],
                    device_id=sibling, device_id_type=MESH)))
        for cp, arrival in copies:
            arrival.wait_recv()
        for cp, arrival in copies:
            cp.wait_send()

    return pl.pallas_call(
        body, name=name, in_specs=[ANY] * n, out_specs=[ANY] * n,
        out_shape=[jax.ShapeDtypeStruct(l.shape, l.dtype) for l in lands],
        input_output_aliases={a: a for a in range(n)},
        scratch_shapes=[pltpu.SemaphoreType.DMA((3 * n,)), pltpu.SemaphoreType.DMA((3 * n,))],
    )(*lands)


REL = [(b >> 2 & 1, b >> 1 & 1, b & 1) for b in range(N_DEV)]


CHIP_REL = [(0, 0, 0)] + TO_CHIPS
N_CHIPS = len(CHIP_REL)


def rs_pair(name, parts):
    n = len(parts)

    def body(*refs):
        ins, got = refs[:n], refs[n:2 * n]
        send_sems, recv_sems = refs[2 * n:]
        me = _place()
        sibling = _flip(me, TO_SIBLING)
        remote = []
        for a in range(n):
            for q, rel in enumerate(CHIP_REL):
                k = a * N_CHIPS + q
                cp = pltpu.make_async_remote_copy(
                    src_ref=ins[a].at[_index(_flip(sibling, rel))], dst_ref=got[a].at[q], send_sem=send_sems.at[k],
                    recv_sem=recv_sems.at[k], device_id=sibling, device_id_type=MESH)
                cp.start()
                remote.append(cp)
        for cp in remote:
            cp.wait_recv()
        for cp in remote:
            cp.wait_send()

    shapes = [jax.ShapeDtypeStruct((N_CHIPS,) + tuple(p.shape[1:]), p.dtype) for p in parts]
    res = pl.pallas_call(
        body, name=name, in_specs=[ANY] * n, out_specs=[ANY] * n, out_shape=shapes,
        scratch_shapes=[pltpu.SemaphoreType.DMA((N_CHIPS * n,)), pltpu.SemaphoreType.DMA((N_CHIPS * n,))],
    )(*parts)
    return list(res)


def own_blocks():
    me = _place()
    return jnp.stack([_index(_flip(me, rel)) for rel in CHIP_REL]).astype(jnp.int32)


def pair_add(name, blocks, parts, got):
    nq, r, c = got.shape
    tr = _row_tile(r, c, budget=6 << 20)

    def body(blk_ref, a_ref, b_ref, o_ref):
        o_ref[...] = (a_ref[...].astype(F32) + b_ref[...].astype(F32)).astype(o_ref.dtype)

    spec = pl.BlockSpec((None, tr, c), lambda q, i, blk: (q, i, 0))
    return pl.pallas_call(
        body, name=name,
        grid_spec=pltpu.PrefetchScalarGridSpec(
            num_scalar_prefetch=1, grid=(nq, r // tr),
            in_specs=[pl.BlockSpec((None, tr, c), lambda q, i, blk: (blk[q], i, 0)), spec], out_specs=spec),
        out_shape=pltpu.HBM(got.shape, got.dtype),
        compiler_params=_params(("arbitrary", "arbitrary"), 6 * tr * c * 2))(blocks, parts, got)


def rs_pair_plan(n):
    def plan(me):
        sibling = _flip(me, TO_SIBLING)
        return [(a, a, _index(_flip(sibling, rel)), q, sibling, True, True)
                for a in range(n) for q, rel in enumerate(CHIP_REL)]
    return plan


def rs_plan(n):
    def plan(me):
        return [(a, a, q, q, _flip(me, CHIP_REL[q]), True, True) for a in range(n) for q in range(1, N_CHIPS)]
    return plan


def rs_start(name, sums, after):
    lands = [lax.empty(t.shape, t.dtype) for t in sums]
    return split_start(name, sums, lands, rs_plan(len(sums)), after)


def allreduce_small(name, pack, after):
    rows, lanes = pack.shape

    def body(x_ref, after_ref, o_ref, land, send_sems, recv_sems):
        me = _place()
        idx = _index(me)
        land[idx] = x_ref[...]
        copies = []
        for r in range(1, N_DEV):
            peer = _flip(me, REL[r])
            cp = pltpu.make_async_remote_copy(
                src_ref=x_ref, dst_ref=land.at[idx], send_sem=send_sems.at[r - 1], recv_sem=recv_sems.at[r - 1],
                device_id=peer, device_id_type=MESH)
            cp.start()
            copies.append(cp)
        for cp in copies:
            cp.wait_recv()
        for cp in copies:
            cp.wait_send()
        acc = land[0]
        for i in range(1, N_DEV):
            acc = acc + land[i]
        o_ref[...] = acc

    return pl.pallas_call(
        body, name=name, in_specs=[pl.BlockSpec(memory_space=pltpu.VMEM), ANY],
        out_specs=pl.BlockSpec(memory_space=pltpu.VMEM), out_shape=jax.ShapeDtypeStruct((rows, lanes), F32),
        scratch_shapes=[pltpu.VMEM((N_DEV, rows, lanes), F32), pltpu.SemaphoreType.DMA((7,)),
                        pltpu.SemaphoreType.DMA((7,))],
    )(pack, after)


def _pad_rows(a, rows):
    return jnp.pad(a, ((0, rows - a.shape[0]), (0, 0)))


def _as_tiles(vec):
    n = vec.shape[0]
    rows = -(-n // LANES)
    rows = -(-rows // SUBLANES) * SUBLANES
    return jnp.pad(vec, (0, rows * LANES - n)).reshape(rows, LANES)


def kernel(x, p, rel_bias_table, attn_norm, w_in, sink_a, w_branch_a, w_branch_b, w_out, ffn_norm, w_ffn_gate, w_ffn_up, conv_w, conv_b, w_ffn_down, ple_norm, w_ple_gate, w_ple_proj, final_norm, loss_target, m_rel_bias_table, m_attn_norm, m_w_in, m_sink_a, m_w_branch_a, m_w_branch_b, m_w_out, m_ffn_norm, m_w_ffn_gate, m_w_ffn_up, m_conv_w, m_conv_b, m_w_ffn_down, m_ple_norm, m_w_ple_gate, m_w_ple_proj, m_final_norm, v_rel_bias_table, v_attn_norm, v_w_in, v_sink_a, v_w_branch_a, v_w_branch_b, v_w_out, v_ffn_norm, v_w_ffn_gate, v_w_ffn_up, v_conv_w, v_conv_b, v_w_ffn_down, v_ple_norm, v_w_ple_gate, v_w_ple_proj, v_final_norm):
    xs = x[0]
    s, d = xs.shape
    ps = p[0, 0]
    target = loss_target[0]
    me = 4 * lax.axis_index("x") + 2 * lax.axis_index("y") + lax.axis_index("c")

    big = dict(w_in=w_in[0], w_branch_a=w_branch_a[0], w_branch_b=w_branch_b[0], w_out=w_out[0],
               w_ffn_gate=w_ffn_gate[0], w_ffn_up=w_ffn_up[0], w_ffn_down=w_ffn_down[0],
               w_ple_gate=w_ple_gate[0], w_ple_proj=w_ple_proj[0])
    big_m = dict(w_in=m_w_in[0], w_branch_a=m_w_branch_a[0], w_branch_b=m_w_branch_b[0], w_out=m_w_out[0],
                 w_ffn_gate=m_w_ffn_gate[0], w_ffn_up=m_w_ffn_up[0], w_ffn_down=m_w_ffn_down[0],
                 w_ple_gate=m_w_ple_gate[0], w_ple_proj=m_w_ple_proj[0])
    big_v = dict(w_in=v_w_in[0], w_branch_a=v_w_branch_a[0], w_branch_b=v_w_branch_b[0], w_out=v_w_out[0],
                 w_ffn_gate=v_w_ffn_gate[0], w_ffn_up=v_w_ffn_up[0], w_ffn_down=v_w_ffn_down[0],
                 w_ple_gate=v_w_ple_gate[0], w_ple_proj=v_w_ple_proj[0])
    names = list(big)
    nf = big["w_ffn_gate"].shape[1]

    shards = {k: big[k].astype(BF16) for k in names}
    shards["conv_w"] = _pad_rows(conv_w[0], SUBLANES)
    flipped = ("w_ffn_gate", "w_ffn_up")
    for k in flipped:
        big[k], big_m[k], big_v[k] = big[k].T, big_m[k].T, big_v[k].T
    ag_groups = [["w_in"], ["w_branch_a", "w_branch_b", "w_out"], ["w_ffn_gate", "conv_w"], ["w_ffn_up"],
                 ["w_ffn_down"], ["w_ple_gate", "w_ple_proj"]]
    ag_started = {}
    wg = {}

    ag_paired, ag_passing = {}, {}

    def pair(gi, after):
        lands = [lax.dynamic_update_index_in_dim(lax.empty((N_DEV,) + shards[k].shape, shards[k].dtype), shards[k],
                                                 me, 0) for k in ag_groups[gi]]
        ag_paired[gi], token = ag_pair(f"ag_pair{gi}", lands, after)
        return token

    def copies(gi):
        return ag_plan(len(ag_groups[gi]), [TO_X, TO_Y] if gi == 0 else TO_CHIPS)

    def start(gi, after):
        s_sems, r_sems, _, lands, token = split_start(f"ag_start{gi}", [], ag_paired[gi], copies(gi), after)
        ag_started[gi] = (s_sems, r_sems, lands)
        return token

    def landed(gi, after):
        s_sems, r_sems, lands = ag_started[gi]
        return split_wait(f"ag_wait{gi}", s_sems, r_sems, [], lands, copies(gi), after)[1:]

    def relayed(lands, after, meanwhile):
        plan = relay_plan(len(lands))
        s_sems, r_sems, _, lands, token = split_start("ag_relay0", [], lands, plan, after)
        return split_wait("ag_relayed0", s_sems, r_sems, [], lands, plan, meanwhile + _token_value(token))[1:]

    def pass_on(gi, lands, after):
        s_sems, r_sems, _, lands, token = split_start(f"ag_pass{gi}", [], lands, pass_plan(len(lands)), after)
        ag_passing[gi] = (s_sems, r_sems, lands)
        return token

    def ready(gi, after):
        s_sems, r_sems, lands = ag_passing[gi]
        lands = split_wait(f"ag_ready{gi}", s_sems, r_sems, [], lands, pass_plan(len(lands)), after)[1]
        wg.update(zip(ag_groups[gi], lands))

    cb = conv_b.reshape(N_DEV, 1, nf)

    table_t = rel_bias_table.T
    geo_a = dict(half=A_BLOCK, q_rows=ATTN_Q_ROWS, n_chains=ATTN_CHAINS, dil=1, nh=A_Q_HEADS, group=A_GROUP,
                 cq=COL_QA, ck=COL_KA, cv=COL_VA)
    geo_b = [dict(half=B_BLOCK, q_rows=min(ATTN_Q_ROWS, s // dil), n_chains=ATTN_CHAINS, dil=dil,
                  nh=B_HEADS_PER_GROUP, group=1, cq=COL_QB + g * B_OUT_W, ck=COL_KB + g * B_OUT_W,
                  cv=COL_VB + g * B_OUT_W) for g, (_, dil) in enumerate(B_PATTERNS)]
    bucket_a = bucket_tile(geo_a["q_rows"], A_BLOCK, 1)
    bias_a = bias_build("bias_a", table_t, bucket_a, 0, A_Q_HEADS, A_BLOCK)
    buckets_b = [bucket_tile(gb["q_rows"], B_BLOCK, gb["dil"]) for gb in geo_b]
    biases_b = [bias_build(f"bias_b{g}", table_t, buckets_b[g], A_Q_HEADS + g * B_HEADS_PER_GROUP, B_HEADS_PER_GROUP,
                           B_BLOCK) for g in range(len(B_PATTERNS))]

    token = start(0, pair(0, xs))
    h = rms_fwd("rms_attn", xs, attn_norm + _token_value(token))
    lands0, token = landed(0, pair(5, pair(4, pair(3, pair(2, pair(1, h))))))
    bias_corner = bias_a[0, :1, :1] + sum(b[0, :1, :1] for b in biases_b)
    lands0, token = relayed(lands0, token, bias_corner)
    token = start(5, start(4, start(3, start(2, start(1, token)))))
    wg["w_in"] = ag_finish("ag_finish0", lands0)[0]
    proj = mm_cols("proj_in", h, wg["w_in"], F32, fold=True, after=token)
    token = pass_on(1, landed(1, proj)[0], proj)
    sink = sink_a[0] + _token_value(token)
    ya, lse_a = band_attn_fwd("attn_a_fwd", proj, bias_a, sink, **geo_a)
    outs_b, lses_b = [], []
    for g in range(len(B_PATTERNS)):
        o, l = band_attn_fwd(f"attn_b{g}_fwd", proj, biases_b[g], None, **geo_b[g])
        outs_b.append(o)
        lses_b.append(l)
    yb = dil_merge_fwd("dil_merge_fwd", outs_b, lses_b)
    ready(1, yb)
    token = pass_on(2, landed(2, yb)[0], yb)
    w_out_full = wg["w_out"].reshape(d, d)
    ta = mm_cols("branch_a", ya, wg["w_branch_a"], F32, fold=True, after=token)
    tb = mm_cols("branch_b", yb, wg["w_branch_b"], F32, fold=True)
    merged = gate_merge_fwd("gate_merge_fwd", proj, ta, tb, d)
    x1 = mm_plain("mix_out", merged, w_out_full, F32, res=xs)

    hf = rms_fwd("rms_ffn", x1, ffn_norm)
    ready(2, hf)
    token = pass_on(3, landed(3, hf)[0], hf)
    cw = wg["conv_w"]
    gpre = mm_cols("ffn_gate", hf, wg["w_ffn_gate"], F32, fold=False, after=token)
    ready(3, gpre)
    token = pass_on(4, landed(4, gpre)[0], gpre)
    u = mm_cols("ffn_up", hf, wg["w_ffn_up"], F32, fold=False, after=token)
    z = ffn_mid_fwd("ffn_mid_fwd", gpre, u, cw, cb)
    ready(4, z)
    token = pass_on(5, landed(5, z)[0], z)
    x2 = mm_jsum("ffn_down", z, wg["w_ffn_down"], F32, res=x1, after=token)

    hp = rms_fwd("rms_ple", x2, ple_norm)
    ready(5, hp)
    w_pg_full = wg["w_ple_gate"].reshape(d, d)
    lp = mm_plain("ple_gate", hp, w_pg_full, F32)
    pp = mm_cols("ple_proj", ps, wg["w_ple_proj"], F32, fold=True)
    loss_part, dx3, dlp, dpp, d_final = tail_fwd_bwd("tail", x2, lp, pp, final_norm.reshape(1, d), target)

    grads = {}
    rs_started = []
    blocks = own_blocks()

    exchanging = []

    def exchange(tag, keys):
        parts = [grads[k] for k in keys]
        lands = [lax.empty((N_CHIPS,) + tuple(p.shape[1:]), p.dtype) for p in parts]
        s_sems, r_sems, parts, lands, token = split_start(f"rs_pair_{tag}", parts, lands, rs_pair_plan(len(keys)), blocks)
        exchanging.append((tag, keys, s_sems, r_sems, parts, lands))
        return _token_value(token)

    def send(after):
        tag, keys, s_sems, r_sems, parts, lands = exchanging.pop(0)
        parts, got, _ = split_wait(f"rs_paired_{tag}", s_sems, r_sems, parts, lands, rs_pair_plan(len(keys)), after)
        return send_sums(tag, keys, parts, got)

    def send_sums(tag, keys, parts, got):
        sums = [pair_add(f"pair_add_{k}", blocks, p, g) for k, p, g in zip(keys, parts, got)]
        s_sems, r_sems, srcs, lands, token = rs_start(f"rs_start_{tag}", sums, blocks)
        rs_started.append((tag, keys, s_sems, r_sems, srcs, lands))
        return token

    grads["w_ple_proj"] = mm_tn_cols("d_w_ple_proj", ps, dpp, N_DEV, big["w_ple_proj"].shape[1], BF16, folded=True)
    grads["w_ple_gate"] = mm_tn_plain("d_w_ple_gate", hp, dlp, BF16).reshape(N_DEV, d // N_DEV, d)
    tok = exchange("ple", ["w_ple_proj", "w_ple_gate"])
    dhp = mm_nt_plain("d_hp", dlp, w_pg_full, F32)
    dx2, dx2_b, d_ple = rms_bwd("rms_ple_bwd", x2, ple_norm + tok, dhp, dx3, True)

    dz = mm_nt_j("d_z", dx2_b, wg["w_ffn_down"], BF16)
    grads["w_ffn_down"] = mm_tn_j("d_w_ffn_down", z, dx2_b, BF16)
    tok = _token_value(send(dz)) + exchange("down", ["w_ffn_down"])
    du, dgpre, dcw = ffn_mid_bwd("ffn_mid_bwd", gpre, u, dz, cw, cb + tok)
    grads["w_ffn_up"] = mm_tn_j("d_w_ffn_up", du, hf, BF16)
    grads["w_ffn_gate"] = mm_tn_j("d_w_ffn_gate", dgpre, hf, BF16)
    dhf = mm_nt_jsum("d_hf_up", du, wg["w_ffn_up"], F32, folded=False)
    dhf = mm_nt_jsum("d_hf_gate", dgpre, wg["w_ffn_gate"], F32, folded=False, res=dhf)
    tok = _token_value(send(dhf)) + exchange("upgate", ["w_ffn_up", "w_ffn_gate"])
    dx1, dx1_b, d_ffn = rms_bwd("rms_ffn_bwd", x1, ffn_norm + tok, dhf, dx2, True)

    dmerged = mm_nt_plain("d_merged", dx1_b, w_out_full, F32)
    grads["w_out"] = mm_tn_plain("d_w_out", merged, dx1_b, BF16).reshape(N_DEV, d // N_DEV, d)
    dta, dtb, dga, dgb = gate_merge_bwd("gate_merge_bwd", dmerged, proj, ta, tb, d)
    grads["w_branch_a"] = mm_tn_cols("d_w_branch_a", ya, dta, N_DEV, big["w_branch_a"].shape[1], BF16, folded=True)
    grads["w_branch_b"] = mm_tn_cols("d_w_branch_b", yb, dtb, N_DEV, big["w_branch_b"].shape[1], BF16, folded=True)
    dya = mm_nt_jsum("d_ya", dta, wg["w_branch_a"], F32, folded=True)
    dyb = mm_nt_jsum("d_yb", dtb, wg["w_branch_b"], F32, folded=True)
    tok = _token_value(send(dyb)) + exchange("mix", ["w_out", "w_branch_a", "w_branch_b"])
    dqa, dka, dva, dbias_a, dsink = band_attn_bwd("attn_a_bwd", proj, bias_a, sink + tok, dya, ya, lse_a, None, **geo_a)
    douts_b, dlses_b = dil_merge_bwd("dil_merge_bwd", dyb, outs_b, lses_b)
    dq_b, dk_b, dv_b, dbias_b = [], [], [], []
    for g in range(len(B_PATTERNS)):
        dq, dk, dv, db, _ = band_attn_bwd(f"attn_b{g}_bwd", proj, biases_b[g], None, douts_b[g], outs_b[g], lses_b[g],
                                          dlses_b[g], **geo_b[g])
        dq_b.append(dq)
        dk_b.append(dk)
        dv_b.append(dv)
        dbias_b.append(db)
    dproj = jnp.concatenate([t.astype(BF16) for t in [dqa, dka, dva] + dq_b + dk_b + dv_b + [dga, dgb]], axis=1)
    token = send(dproj)
    grads["w_in"] = mm_tn_cols("d_w_in", h, dproj, N_DEV, big["w_in"].shape[1], BF16, folded=True, after=token)
    token = send_sums("in", ["w_in"], [grads["w_in"]], rs_pair("rs_pair_in", [grads["w_in"]]))
    dh = mm_nt_jsum("d_h", dproj, wg["w_in"], F32, folded=True, after=token)
    grad_x, _, d_attn = rms_bwd("rms_attn_bwd", xs, attn_norm, dh, dx1, False)

    dt_a = table_grad("table_grad_a", dbias_a, bucket_a)[:, 0, :N_BUCKETS]
    dt_b = [table_grad(f"table_grad_b{g}", dbias_b[g], buckets_b[g])[:, 0, :N_BUCKETS] for g in range(len(B_PATTERNS))]
    d_table_part = jnp.concatenate([dt_a] + dt_b, axis=0).T

    pieces = [
        ("loss", loss_part[0, :1]),
        ("table", d_table_part.reshape(-1)),
        ("attn_norm", d_attn.reshape(-1)),
        ("sink", dsink[:, 0, 0]),
        ("ffn_norm", d_ffn.reshape(-1)),
        ("conv_w", dcw[:, 0:3, :].reshape(-1)),
        ("conv_b", dcw[:, 3, :].reshape(-1)),
        ("ple_norm", d_ple.reshape(-1)),
        ("final_norm", d_final.reshape(-1)),
    ]
    tiles = [_as_tiles(v) for _, v in pieces]
    pack = jnp.concatenate(tiles, axis=0)

    out_g, out_d, out_m, out_v = {}, {}, {}, {}

    def finish(group, after):
        tag, keys, s_sems, r_sems, srcs, lands = group
        srcs, lands, _ = split_wait(f"rs_wait_{tag}", s_sems, r_sems, srcs, lands, rs_plan(len(keys)), after)
        for k, mine, theirs in zip(keys, srcs, lands):
            res = reduce_adam("adam_" + k, mine, theirs, big[k], big_m[k], big_v[k])
            after = res[1]
            out_g[k], out_d[k], out_m[k], out_v[k] = [(t.T if k in flipped else t)[None] for t in res]
        return after

    after = pack
    for group in rs_started[:-1]:
        after = finish(group, after)
    total = allreduce_small("allreduce_small", pack, after)
    finish(rs_started[-1], total)
    small = {}
    row = 0
    for (nm, v), t in zip(pieces, tiles):
        small[nm] = total[row:row + t.shape[0]].reshape(-1)[:v.shape[0]]
        row += t.shape[0]
    loss = small["loss"][0]
    g_small = dict(
        rel_bias_table=small["table"].reshape(rel_bias_table.shape),
        attn_norm=small["attn_norm"].reshape(attn_norm.shape),
        sink_a=small["sink"].reshape(sink_a.shape),
        ffn_norm=small["ffn_norm"].reshape(ffn_norm.shape),
        conv_w=lax.dynamic_index_in_dim(small["conv_w"].reshape(N_DEV, 3, nf), me, 0, keepdims=False)[None],
        conv_b=small["conv_b"].reshape(conv_b.shape),
        ple_norm=small["ple_norm"].reshape(ple_norm.shape),
        final_norm=small["final_norm"].reshape(1, d),
    )
    w_small = dict(rel_bias_table=(rel_bias_table, m_rel_bias_table, v_rel_bias_table),
                   attn_norm=(attn_norm, m_attn_norm, v_attn_norm), sink_a=(sink_a, m_sink_a, v_sink_a),
                   ffn_norm=(ffn_norm, m_ffn_norm, v_ffn_norm), conv_w=(conv_w, m_conv_w, v_conv_w),
                   conv_b=(conv_b, m_conv_b, v_conv_b), ple_norm=(ple_norm, m_ple_norm, v_ple_norm),
                   final_norm=(final_norm, m_final_norm, v_final_norm))

    for k, (wv, mv, vv) in w_small.items():
        shape = wv.shape
        two_d = (1, shape[0]) if len(shape) == 1 else ((shape[0] * shape[1], shape[2]) if len(shape) == 3 else shape)
        gk = g_small[k].reshape(two_d)
        dl, nm, nv = adam_small("adam_" + k, gk, wv.reshape(two_d), mv.reshape(two_d), vv.reshape(two_d))
        out_g[k], out_d[k], out_m[k], out_v[k] = gk.reshape(shape), dl.reshape(shape), nm.reshape(shape), nv.reshape(shape)

    order = ["rel_bias_table", "attn_norm", "w_in", "sink_a", "w_branch_a", "w_branch_b", "w_out", "ffn_norm",
             "w_ffn_gate", "w_ffn_up", "conv_w", "conv_b", "w_ffn_down", "ple_norm", "w_ple_gate", "w_ple_proj",
             "final_norm"]
    return (loss, grad_x[None], *[out_g[k] for k in order], *[out_d[k] for k in order],
            *[out_m[k] for k in order], *[out_v[k] for k in order])
```

```python
import math

import jax
import jax.numpy as jnp
from jax import lax
from jax.experimental import pallas as pl
from jax.experimental.pallas import tpu as pltpu

F32 = jnp.float32
BF16 = jnp.bfloat16
MESH = pl.DeviceIdType.MESH
N_DEV = 8

HEAD_DIM = 128
A_Q_HEADS = 8
A_KV_HEADS = 2
A_GROUP = A_Q_HEADS // A_KV_HEADS
A_BLOCK = 128
B_PATTERNS = ((128, 1), (512, 4), (2048, 16))
B_HEADS_PER_GROUP = 4
B_HEADS = len(B_PATTERNS) * B_HEADS_PER_GROUP
B_BLOCK = 64
N_BUCKETS = 32
MAX_DISTANCE = 1024
A_Q_W = A_Q_HEADS * HEAD_DIM
A_KV_W = A_KV_HEADS * HEAD_DIM
B_W = B_HEADS * HEAD_DIM
B_OUT_W = B_HEADS_PER_GROUP * HEAD_DIM
COL_QA = 0
COL_KA = COL_QA + A_Q_W
COL_VA = COL_KA + A_KV_W
COL_QB = COL_VA + A_KV_W
COL_KB = COL_QB + B_W
COL_VB = COL_KB + B_W
COL_GATES = COL_VB + B_W
RMS_EPS = 1e-6
NEG_INF = -1e30
ATTN_SCALE = HEAD_DIM ** -0.5
ATTN_Q_ROWS = 256
ATTN_CHAINS = 4

ADAM_LR = 0.001
ADAM_B1 = 0.9
ADAM_B2 = 0.999
ADAM_EPS = 1e-08
ADAM_WD = 0.01
ADAM_STEP = 10

GELU_C = math.sqrt(2.0 / math.pi)
GELU_A = 0.044715

V7X_VMEM_BYTES = 64 * 1024 * 1024
VMEM_CEILING = V7X_VMEM_BYTES - 8 * 1024 * 1024
LANES = 128
SUBLANES = 8


def _pick(n, cands):
    for c in cands:
        if n % c == 0:
            return c
    return n


def _nbytes(shape, dtype):
    n = 1
    for d in shape:
        if d is not None:
            n *= d
    return n * jnp.dtype(dtype).itemsize


def _params(sem, est_bytes):
    limit = int(min(VMEM_CEILING, max(32 * 1024 * 1024, 2 * est_bytes + (8 << 20))))
    return pltpu.CompilerParams(dimension_semantics=sem, vmem_limit_bytes=limit)


def _mm(name, a, b, a_bs, a_im, b_bs, b_im, out_shape, out_dtype, o_bs, o_im, grid, dims,
        res=None, r_bs=None, r_im=None, after=None):
    nk = grid[-1]
    nax = len(grid)
    has_res = res is not None
    has_after = after is not None
    o_tile = tuple(d for d in o_bs if d is not None)

    def body(*refs):
        a_ref, b_ref = refs[:2]
        r_ref = refs[2] if has_res else None
        n_in = 2 + has_res + has_after
        o_ref = refs[n_in]
        rest = refs[n_in + 1:]

        def prod():
            return lax.dot_general(a_ref[...].astype(BF16), b_ref[...].astype(BF16), (dims, ((), ())),
                                   preferred_element_type=F32)

        def finish(r):
            if r_ref is not None:
                r = r + r_ref[...].astype(F32)
            o_ref[...] = r.astype(o_ref.dtype)

        if nk == 1:
            finish(prod())
        else:
            acc = rest[0]
            k = pl.program_id(nax - 1)

            @pl.when(k == 0)
            def _():
                acc[...] = prod()

            @pl.when(k > 0)
            def _():
                acc[...] += prod()

            @pl.when(k == nk - 1)
            def _():
                finish(acc[...])

    in_specs = [pl.BlockSpec(a_bs, a_im), pl.BlockSpec(b_bs, b_im)]
    args = [a, b]
    est = _nbytes(a_bs, a.dtype) + _nbytes(b_bs, b.dtype) + _nbytes(o_bs, out_dtype) + 2 * _nbytes(o_tile, F32)
    if has_res:
        in_specs.append(pl.BlockSpec(r_bs, r_im))
        args.append(res)
        est += _nbytes(r_bs, res.dtype)
    if has_after:
        in_specs.append(pl.BlockSpec(memory_space=pl.ANY))
        args.append(after)
    scratch = [] if nk == 1 else [pltpu.VMEM(o_tile, F32)]
    sem = ("parallel",) * (nax - 1) + ("arbitrary",)
    return pl.pallas_call(
        body, name=name, grid=grid, in_specs=in_specs, out_specs=pl.BlockSpec(o_bs, o_im),
        out_shape=pltpu.HBM(out_shape, out_dtype), scratch_shapes=scratch,
        compiler_params=_params(sem, est))(*args)


TM_CANDS = (1024, 512, 256, 128, 64, 32, 16, 8)
TM_WIDE_CANDS = (2048,) + TM_CANDS
TK_CANDS = (1024, 512, 256, 128)
TN_CANDS = (1024, 512, 256, 128)


def mm_cols(name, a, wg, out_dtype, fold, after=None):
    m, k = a.shape
    nj, _, n = wg.shape
    tm, tk = _pick(m, TM_WIDE_CANDS), _pick(k, TK_CANDS)
    grid = (nj, m // tm, k // tk)
    if fold:
        shape, o_bs, o_im = (m, nj * n), (tm, n), (lambda j, i, kk: (i, j))
    else:
        shape, o_bs, o_im = (nj, m, n), (None, tm, n), (lambda j, i, kk: (j, i, 0))
    return _mm(name, a, wg, (tm, tk), lambda j, i, kk: (i, kk), (None, tk, n), lambda j, i, kk: (j, kk, 0),
               shape, out_dtype, o_bs, o_im, grid, ((1,), (0,)), after=after)


def mm_plain(name, a, w, out_dtype, res=None):
    m, k = a.shape
    n = w.shape[1]
    tm, tk, tn = _pick(m, TM_CANDS if res is not None else TM_WIDE_CANDS), _pick(k, TK_CANDS), _pick(n, TN_CANDS)
    grid = (n // tn, m // tm, k // tk)
    return _mm(name, a, w, (tm, tk), lambda j, i, kk: (i, kk), (tk, tn), lambda j, i, kk: (kk, j),
               (m, n), out_dtype, (tm, tn), lambda j, i, kk: (i, j), grid, ((1,), (0,)),
               res, (tm, tn), lambda j, i, kk: (i, j))


def mm_jsum(name, aj, wg, out_dtype, res=None, after=None):
    nj, m, ka = aj.shape
    n = wg.shape[2]
    tm, tn = _pick(m, TM_CANDS), _pick(n, TN_CANDS)
    grid = (m // tm, n // tn, nj)
    return _mm(name, aj, wg, (None, tm, ka), lambda i, jn, j: (j, i, 0), (None, ka, tn), lambda i, jn, j: (j, 0, jn),
               (m, n), out_dtype, (tm, tn), lambda i, jn, j: (i, jn), grid, ((1,), (0,)),
               res, (tm, tn), lambda i, jn, j: (i, jn), after=after)


def mm_tn_cols(name, a, g, nj, n, out_dtype, folded, after=None):
    s, kw = a.shape
    ts, tkw = _pick(s, TK_CANDS), _pick(kw, TM_WIDE_CANDS)
    grid = (nj, kw // tkw, s // ts)
    if folded:
        g_bs, g_im = (ts, n), (lambda j, i, ss: (ss, j))
    else:
        g_bs, g_im = (None, ts, n), (lambda j, i, ss: (j, ss, 0))
    return _mm(name, a, g, (ts, tkw), lambda j, i, ss: (ss, i), g_bs, g_im,
               (nj, kw, n), out_dtype, (None, tkw, n), lambda j, i, ss: (j, i, 0), grid, ((0,), (0,)), after=after)


def mm_tn_plain(name, a, g, out_dtype):
    s, kw = a.shape
    n = g.shape[1]
    ts, tkw, tn = _pick(s, TK_CANDS), _pick(kw, TM_WIDE_CANDS), _pick(n, TN_CANDS)
    grid = (kw // tkw, n // tn, s // ts)
    return _mm(name, a, g, (ts, tkw), lambda i, jn, ss: (ss, i), (ts, tn), lambda i, jn, ss: (ss, jn),
               (kw, n), out_dtype, (tkw, tn), lambda i, jn, ss: (i, jn), grid, ((0,), (0,)))


def mm_tn_j(name, aj, g, out_dtype):
    nj, s, ka = aj.shape
    n = g.shape[1]
    ts, tn = _pick(s, TK_CANDS), _pick(n, TM_WIDE_CANDS)
    grid = (nj, n // tn, s // ts)
    return _mm(name, aj, g, (None, ts, ka), lambda j, jn, ss: (j, ss, 0), (ts, tn), lambda j, jn, ss: (ss, jn),
               (nj, ka, n), out_dtype, (None, ka, tn), lambda j, jn, ss: (j, 0, jn), grid, ((0,), (0,)))


def mm_nt_plain(name, g, w, out_dtype):
    m, n = g.shape
    k = w.shape[0]
    tm, tn, tkk = _pick(m, TM_WIDE_CANDS), _pick(n, TK_CANDS), _pick(k, TN_CANDS)
    grid = (k // tkk, m // tm, n // tn)
    return _mm(name, g, w, (tm, tn), lambda kk, i, jn: (i, jn), (tkk, tn), lambda kk, i, jn: (kk, jn),
               (m, k), out_dtype, (tm, tkk), lambda kk, i, jn: (i, kk), grid, ((1,), (1,)))


def mm_nt_j(name, g, wg, out_dtype):
    m, n = g.shape
    nj, ka, _ = wg.shape
    tm, tn = _pick(m, TM_WIDE_CANDS), _pick(n, TK_CANDS)
    grid = (nj, m // tm, n // tn)
    return _mm(name, g, wg, (tm, tn), lambda j, i, jn: (i, jn), (None, ka, tn), lambda j, i, jn: (j, 0, jn),
               (nj, m, ka), out_dtype, (None, tm, ka), lambda j, i, jn: (j, i, 0), grid, ((1,), (1,)))


def mm_nt_jsum(name, g, wg, out_dtype, folded, res=None, after=None):
    nj, k, n = wg.shape
    m = g.shape[0] if folded else g.shape[1]
    tm, tkk = _pick(m, TM_CANDS if res is not None else TM_WIDE_CANDS), _pick(k, TN_CANDS)
    grid = (m // tm, k // tkk, nj)
    if folded:
        g_bs, g_im = (tm, n), (lambda i, kk, j: (i, j))
    else:
        g_bs, g_im = (None, tm, n), (lambda i, kk, j: (j, i, 0))
    return _mm(name, g, wg, g_bs, g_im, (None, tkk, n), lambda i, kk, j: (j, kk, 0),
               (m, k), out_dtype, (tm, tkk), lambda i, kk, j: (i, kk), grid, ((1,), (1,)),
               res, (tm, tkk), lambda i, kk, j: (i, kk), after=after)


ROW_TILE_CANDS = (256, 128, 64, 32, 16, 8)


def _rstd(x):
    return lax.rsqrt(jnp.mean(x * x, axis=-1, keepdims=True) + RMS_EPS)


def _sigmoid(t):
    return 1.0 / (1.0 + jnp.exp(-t))


def rms_fwd(name, x, gain):
    s, d = x.shape
    ts = _pick(s, ROW_TILE_CANDS)

    def body(x_ref, g_ref, h_ref):
        xv = x_ref[...]
        h_ref[...] = ((xv * _rstd(xv)) * g_ref[...]).astype(h_ref.dtype)

    return pl.pallas_call(
        body, name=name, grid=(s // ts,),
        in_specs=[pl.BlockSpec((ts, d), lambda i: (i, 0)), pl.BlockSpec((1, d), lambda i: (0, 0))],
        out_specs=pl.BlockSpec((ts, d), lambda i: (i, 0)),
        out_shape=pltpu.HBM((s, d), BF16),
        compiler_params=_params(("parallel",), 3 * ts * d * 4))(x, gain)


def rms_bwd(name, x, gain, dh, dres, bf16_copy):
    s, d = x.shape
    ts = _pick(s, ROW_TILE_CANDS)

    def body(x_ref, g_ref, dh_ref, dr_ref, dx_ref, *rest):
        dxb_ref, dg_ref = rest if bf16_copy else (None, rest[0])
        xv = x_ref[...]
        r = _rstd(xv)
        xhat = xv * r
        dhv = dh_ref[...].astype(F32)
        dxhat = dhv * g_ref[...]
        dx = dr_ref[...] + r * (dxhat - xhat * jnp.mean(dxhat * xhat, axis=-1, keepdims=True))
        dx_ref[...] = dx
        if bf16_copy:
            dxb_ref[...] = dx.astype(dxb_ref.dtype)
        part = jnp.sum(dhv * xhat, axis=0, keepdims=True)

        @pl.when(pl.program_id(0) == 0)
        def _():
            dg_ref[...] = part

        @pl.when(pl.program_id(0) > 0)
        def _():
            dg_ref[...] += part

    row = pl.BlockSpec((ts, d), lambda i: (i, 0))
    vec = pl.BlockSpec((1, d), lambda i: (0, 0))
    copy_spec, copy_shape = ([row], [pltpu.HBM((s, d), BF16)]) if bf16_copy else ([], [])
    res = pl.pallas_call(
        body, name=name, grid=(s // ts,), in_specs=[row, vec, row, row], out_specs=[row] + copy_spec + [vec],
        out_shape=[pltpu.HBM((s, d), F32)] + copy_shape + [jax.ShapeDtypeStruct((1, d), F32)],
        compiler_params=_params(("arbitrary",), 7 * ts * d * 4))(x, gain, dh, dres)
    return (res[0], res[1], res[2]) if bf16_copy else (res[0], None, res[1])


def gate_merge_fwd(name, proj, ta, tb, d):
    s = proj.shape[0]
    ts = _pick(s, ROW_TILE_CANDS)
    cb = COL_GATES // d

    def body(ga_ref, gb_ref, ta_ref, tb_ref, o_ref):
        o_ref[...] = (_sigmoid(ga_ref[...]) * ta_ref[...] + _sigmoid(gb_ref[...]) * tb_ref[...]).astype(o_ref.dtype)

    row = pl.BlockSpec((ts, d), lambda i: (i, 0))
    return pl.pallas_call(
        body, name=name, grid=(s // ts,),
        in_specs=[pl.BlockSpec((ts, d), lambda i: (i, cb)), pl.BlockSpec((ts, d), lambda i: (i, cb + 1)), row, row],
        out_specs=row, out_shape=pltpu.HBM((s, d), BF16),
        compiler_params=_params(("parallel",), 5 * ts * d * 4))(proj, proj, ta, tb)


def gate_merge_bwd(name, dmerged, proj, ta, tb, d):
    s = proj.shape[0]
    ts = _pick(s, ROW_TILE_CANDS)
    cb = COL_GATES // d

    def body(dm_ref, ga_ref, gb_ref, ta_ref, tb_ref, dta_ref, dtb_ref, dga_ref, dgb_ref):
        dm = dm_ref[...]
        sa = _sigmoid(ga_ref[...])
        sb = _sigmoid(gb_ref[...])
        dta_ref[...] = (dm * sa).astype(dta_ref.dtype)
        dtb_ref[...] = (dm * sb).astype(dtb_ref.dtype)
        dga_ref[...] = (dm * ta_ref[...] * (sa * (1.0 - sa))).astype(dga_ref.dtype)
        dgb_ref[...] = (dm * tb_ref[...] * (sb * (1.0 - sb))).astype(dgb_ref.dtype)

    row = pl.BlockSpec((ts, d), lambda i: (i, 0))
    out = pltpu.HBM((s, d), BF16)
    return pl.pallas_call(
        body, name=name, grid=(s // ts,),
        in_specs=[row, pl.BlockSpec((ts, d), lambda i: (i, cb)), pl.BlockSpec((ts, d), lambda i: (i, cb + 1)), row, row],
        out_specs=[row, row, row, row], out_shape=[out, out, out, out],
        compiler_params=_params(("parallel",), 8 * ts * d * 4))(dmerged, proj, proj, ta, tb)


def tail_fwd_bwd(name, x2, lp, pp, gain, target):
    s, d = x2.shape
    ts = _pick(s, ROW_TILE_CANDS)

    def body(x2_ref, lp_ref, pp_ref, g_ref, t_ref, loss_ref, dx3_ref, dlp_ref, dpp_ref, dg_ref):
        gp = _sigmoid(lp_ref[...])
        ppv = pp_ref[...]
        x3 = x2_ref[...] + gp * ppv
        r = _rstd(x3)
        xhat = x3 * r
        gv = g_ref[...]
        err = xhat * gv - t_ref[...]
        loss = jnp.sum(err * err) * (0.5 / d)
        dy = err * (1.0 / d)
        dxhat = dy * gv
        dx3 = r * (dxhat - xhat * jnp.mean(dxhat * xhat, axis=-1, keepdims=True))
        dx3_ref[...] = dx3
        dlp_ref[...] = (dx3 * ppv * (gp * (1.0 - gp))).astype(dlp_ref.dtype)
        dpp_ref[...] = (dx3 * gp).astype(dpp_ref.dtype)
        part = jnp.sum(dy * xhat, axis=0, keepdims=True)
        lossv = jnp.full((1, LANES), loss, F32)

        @pl.when(pl.program_id(0) == 0)
        def _():
            dg_ref[...] = part
            loss_ref[...] = lossv

        @pl.when(pl.program_id(0) > 0)
        def _():
            dg_ref[...] += part
            loss_ref[...] += lossv

    row = pl.BlockSpec((ts, d), lambda i: (i, 0))
    vec = pl.BlockSpec((1, d), lambda i: (0, 0))
    return pl.pallas_call(
        body, name=name, grid=(s // ts,), in_specs=[row, row, row, vec, row],
        out_specs=[pl.BlockSpec((1, LANES), lambda i: (0, 0)), row, row, row, vec],
        out_shape=[jax.ShapeDtypeStruct((1, LANES), F32), pltpu.HBM((s, d), F32),
                   pltpu.HBM((s, d), BF16), pltpu.HBM((s, d), BF16),
                   jax.ShapeDtypeStruct((1, d), F32)],
        compiler_params=_params(("arbitrary",), 9 * ts * d * 4))(x2, lp, pp, gain, target)


HALO = SUBLANES
BF16_ROWS = 2 * SUBLANES


def _shift_rows(cur, prev_row, next_row):
    ts = cur.shape[0]
    rid = lax.broadcasted_iota(jnp.int32, cur.shape, 0)
    down = jnp.where(rid == 0, prev_row, pltpu.roll(cur, 1, 0))
    up = jnp.where(rid == ts - 1, next_row, pltpu.roll(cur, ts - 1, 0))
    return down, up


def _halo_specs(ts, s, nf, halo=HALO):
    nb = ts // halo
    last = s // halo - 1
    cur = pl.BlockSpec((None, ts, nf), lambda j, i: (j, i, 0))
    prev = pl.BlockSpec((None, halo, nf), lambda j, i: (j, jnp.maximum(i * nb - 1, 0), 0))
    nxt = pl.BlockSpec((None, halo, nf), lambda j, i: (j, jnp.minimum((i + 1) * nb, last), 0))
    return cur, prev, nxt


def _halo_rows(prev_ref, next_ref, n_tiles):
    i = pl.program_id(1)
    prev_row = jnp.where(i == 0, 0.0, prev_ref[HALO - 1:HALO, :].astype(F32))
    next_row = jnp.where(i == n_tiles - 1, 0.0, next_ref[0:1, :].astype(F32))
    return prev_row, next_row


def _gelu(g):
    t = jnp.tanh(GELU_C * (g + GELU_A * (g * g * g)))
    return 0.5 * g * (1.0 + t), t


def _conv(cur, down, up, cw_ref, cb_ref):
    return down * cw_ref[0:1, :] + cur * cw_ref[1:2, :] + up * cw_ref[2:3, :] + cb_ref[...]


def ffn_mid_fwd(name, gpre, u, cw, cb):
    nj, s, nf = gpre.shape
    ts = _pick(s, (512, 256, 128, 64, 32, 16, 8))
    n_tiles = s // ts
    cur, prev, nxt = _halo_specs(ts, s, nf)

    def body(g_ref, gp_ref, gn_ref, u_ref, cw_ref, cb_ref, z_ref):
        gv = g_ref[...]
        down, up = _shift_rows(gv, *_halo_rows(gp_ref, gn_ref, n_tiles))
        act, _ = _gelu(_conv(gv, down, up, cw_ref, cb_ref))
        z_ref[...] = (act * u_ref[...]).astype(z_ref.dtype)

    return pl.pallas_call(
        body, name=name, grid=(nj, n_tiles),
        in_specs=[cur, prev, nxt, cur, pl.BlockSpec((None, SUBLANES, nf), lambda j, i: (j, 0, 0)),
                  pl.BlockSpec((None, 1, nf), lambda j, i: (j, 0, 0))],
        out_specs=cur, out_shape=pltpu.HBM((nj, s, nf), BF16),
        compiler_params=_params(("parallel", "parallel"), 8 * ts * nf * 4))(gpre, gpre, gpre, u, cw, cb)


def _gelu_grad(g, t):
    return 0.5 * (1.0 + t) + 0.5 * g * (1.0 - t * t) * (GELU_C * (1.0 + 3.0 * GELU_A * (g * g)))


def ffn_mid_bwd(name, gpre, u, dz, cw, cb):
    nj, s, nf = gpre.shape
    ts = _pick(s, (512, 256, 128, 64, 32, 16, 8))
    n_tiles = s // ts
    cur, prev, nxt = _halo_specs(ts, s, nf)

    def body(g_ref, gp_ref, gn_ref, u_ref, up_ref, un_ref, dz_ref, dzp_ref, dzn_ref, cw_ref, cb_ref,
             du_ref, dgp_ref, dcw_ref):
        i = pl.program_id(1)
        w0, w1, w2, bias = cw_ref[0:1, :], cw_ref[1:2, :], cw_ref[2:3, :], cb_ref[...]
        gv = g_ref[...]
        down, up = _shift_rows(gv, *_halo_rows(gp_ref, gn_ref, n_tiles))
        gc = down * w0 + gv * w1 + up * w2 + bias
        act, t = _gelu(gc)
        dzv = dz_ref[...].astype(F32)
        du_ref[...] = (dzv * act).astype(du_ref.dtype)
        dg = dzv * u_ref[...] * _gelu_grad(gc, t)

        def edge_dg(g_before, g_at, g_after, u_at, dz_at):
            ge = g_before.astype(F32) * w0 + g_at.astype(F32) * w1 + g_after.astype(F32) * w2 + bias
            return dz_at.astype(F32) * u_at.astype(F32) * _gelu_grad(ge, _gelu(ge)[1])

        dz_before = dzp_ref[...].astype(F32)[BF16_ROWS - 1:BF16_ROWS, :]
        dz_after = dzn_ref[...].astype(F32)[0:1, :]
        dg_prev = jnp.where(i == 0, 0.0, edge_dg(gp_ref[HALO - 2:HALO - 1, :], gp_ref[HALO - 1:HALO, :], gv[0:1, :],
                                                 up_ref[HALO - 1:HALO, :], dz_before))
        dg_next = jnp.where(i == n_tiles - 1, 0.0, edge_dg(gv[ts - 1:ts, :], gn_ref[0:1, :], gn_ref[1:2, :],
                                                           un_ref[0:1, :], dz_after))
        dg_down, dg_up = _shift_rows(dg, dg_prev, dg_next)
        dgp_ref[...] = (dg_up * w0 + dg * w1 + dg_down * w2).astype(dgp_ref.dtype)
        rows = [jnp.sum(dg * down, axis=0, keepdims=True), jnp.sum(dg * gv, axis=0, keepdims=True),
                jnp.sum(dg * up, axis=0, keepdims=True), jnp.sum(dg, axis=0, keepdims=True)]
        part = jnp.concatenate(rows + [jnp.zeros((SUBLANES - len(rows), nf), F32)], axis=0)

        @pl.when(i == 0)
        def _():
            dcw_ref[...] = part

        @pl.when(i > 0)
        def _():
            dcw_ref[...] += part

    small = pl.BlockSpec((None, SUBLANES, nf), lambda j, i: (j, 0, 0))
    return pl.pallas_call(
        body, name=name, grid=(nj, n_tiles),
        in_specs=[cur, prev, nxt] * 2 + list(_halo_specs(ts, s, nf, BF16_ROWS))
        + [small, pl.BlockSpec((None, 1, nf), lambda j, i: (j, 0, 0))],
        out_specs=[cur, cur, small],
        out_shape=[pltpu.HBM((nj, s, nf), BF16), pltpu.HBM((nj, s, nf), BF16),
                   jax.ShapeDtypeStruct((nj, SUBLANES, nf), F32)],
        compiler_params=_params(("parallel", "arbitrary"), 14 * ts * nf * 4))(
            gpre, gpre, gpre, u, u, u, dz, dz, dz, cw, cb)


def _t5_bucket(rel):
    half = N_BUCKETS // 2
    max_exact = half // 2
    n = jnp.abs(rel)
    side = jnp.where(rel > 0, half, 0)
    nf = jnp.maximum(n, 1).astype(F32)
    large = max_exact + (jnp.log(nf / max_exact) / math.log(MAX_DISTANCE / max_exact)
                         * (half - max_exact)).astype(jnp.int32)
    large = jnp.minimum(large, half - 1)
    return side + jnp.where(n < max_exact, n, large)


def bucket_tile(rows, half, dil):
    rel = (jnp.arange(rows + 2 * half)[None, :] - half) - jnp.arange(rows)[:, None]
    return _t5_bucket(rel * dil).astype(jnp.int32)


def bias_build(name, table_t, bucket, h0, nh, half):
    blk, kw = bucket.shape

    def body(t_ref, b_ref, o_ref):
        h = pl.program_id(0)
        bv = b_ref[...]
        acc = jnp.zeros((blk, kw), F32)
        for b in range(N_BUCKETS):
            acc = jnp.where(bv == b, t_ref[h0 + h, b], acc)
        qi = lax.broadcasted_iota(jnp.int32, (blk, kw), 0)
        ci = lax.broadcasted_iota(jnp.int32, (blk, kw), 1)
        o_ref[...] = jnp.where(jnp.abs(ci - half - qi) <= half, acc, NEG_INF)

    return pl.pallas_call(
        body, name=name, grid=(nh,),
        in_specs=[pl.BlockSpec(memory_space=pltpu.SMEM), pl.BlockSpec((blk, kw), lambda h: (0, 0))],
        out_specs=pl.BlockSpec((None, blk, kw), lambda h: (h, 0, 0)),
        out_shape=jax.ShapeDtypeStruct((nh, blk, kw), F32),
        compiler_params=_params(("parallel",), 4 * blk * kw * 4))(table_t, bucket)


def table_grad(name, dbias, bucket):
    nh, blk, kw = dbias.shape

    def body(d_ref, b_ref, o_ref):
        bv = b_ref[...]
        dv = d_ref[...]
        lane = lax.broadcasted_iota(jnp.int32, (SUBLANES, LANES), 1)
        acc = jnp.zeros((SUBLANES, LANES), F32)
        for b in range(N_BUCKETS):
            acc = jnp.where(lane == b, jnp.sum(jnp.where(bv == b, dv, 0.0)), acc)
        o_ref[...] = acc

    return pl.pallas_call(
        body, name=name, grid=(nh,),
        in_specs=[pl.BlockSpec((None, blk, kw), lambda h: (h, 0, 0)), pl.BlockSpec((blk, kw), lambda h: (0, 0))],
        out_specs=pl.BlockSpec((None, SUBLANES, LANES), lambda h: (h, 0, 0)),
        out_shape=jax.ShapeDtypeStruct((nh, SUBLANES, LANES), F32),
        compiler_params=_params(("parallel",), 4 * blk * kw * 4))(dbias, bucket)


class _Band:
    def __init__(self, s, half, q_rows, n_chains, dil):
        self.s, self.half, self.dil, self.n_chains = s, half, dil, n_chains
        self.seg = s // dil
        self.q_rows = min(q_rows, self.seg)
        self.win = self.q_rows + 2 * half
        self.pad = self.seg + 2 * half
        self.nsb = self.seg // self.q_rows
        self.n_items = dil * self.nsb
        assert self.n_items % n_chains == 0 and self.seg % self.q_rows == 0
        self.staged = dil > 1

    def rows_of(self, r):
        return pl.ds(r, self.seg, stride=self.dil) if self.dil > 1 else slice(None)

    def stage_kv(self, dst, src_ref):
        zeros = jnp.zeros((self.half, HEAD_DIM), dst.dtype)
        for r in range(self.dil):
            base = r * self.pad
            dst[base:base + self.half, :] = zeros
            dst[base + self.half + self.seg:base + self.pad, :] = zeros
            dst[base + self.half:base + self.half + self.seg, :] = src_ref[self.rows_of(r), :].astype(dst.dtype)

    def stage(self, dst, src_ref):
        for r in range(self.dil):
            dst[r * self.seg:(r + 1) * self.seg, :] = src_ref[self.rows_of(r), :].astype(dst.dtype)

    def unstage(self, dst_ref, src, add=False):
        for r in range(self.dil):
            val = src[r * self.seg:(r + 1) * self.seg, :].astype(dst_ref.dtype)
            if add:
                val = val + dst_ref[self.rows_of(r), :]
            dst_ref[self.rows_of(r), :] = val

    def offsets(self, item):
        r, sb = item // self.nsb, item % self.nsb
        qoff = pl.multiple_of(r * self.seg + sb * self.q_rows, self.q_rows)
        koff = pl.multiple_of(r * self.pad + sb * self.q_rows, B_BLOCK)
        kpos = sb * self.q_rows - self.half + lax.broadcasted_iota(jnp.int32, (1, self.win), 1)
        edge = jnp.where((kpos >= 0) & (kpos < self.seg), 0.0, NEG_INF)
        return qoff, koff, edge


def band_attn_fwd(name, proj, bias, sink, *, half, q_rows, n_chains, dil, nh, group, cq, ck, cv):
    s, w = proj.shape
    g = _Band(s, half, q_rows, n_chains, dil)
    has_sink = sink is not None

    def body(*refs):
        q_ref, k_ref, v_ref, b_ref = refs[:4]
        s_ref = refs[4] if has_sink else None
        o_ref, l_ref, ks, vs = refs[4 + has_sink:8 + has_sink]
        qs, os_, ls = refs[8 + has_sink:] if g.staged else (None, o_ref, l_ref)
        g.stage_kv(ks, k_ref)
        g.stage_kv(vs, v_ref)
        if g.staged:
            g.stage(qs, q_ref)
        bias_v = b_ref[...]
        sk = s_ref[pl.program_id(0)] if has_sink else None

        def chain(item):
            qoff, koff, edge = g.offsets(item)
            rows = pl.ds(qoff, g.q_rows)
            qv = qs[rows, :] if g.staged else q_ref[rows, :].astype(BF16)
            kw_ = ks[pl.ds(koff, g.win), :]
            vw_ = vs[pl.ds(koff, g.win), :]
            sc = lax.dot_general(qv, kw_, (((1,), (1,)), ((), ())), preferred_element_type=F32) * ATTN_SCALE
            sc = sc + bias_v + edge
            m = jnp.max(sc, axis=-1, keepdims=True)
            if has_sink:
                m = jnp.maximum(m, sk)
            p = jnp.exp(sc - m)
            den = jnp.sum(p, axis=-1, keepdims=True)
            if has_sink:
                den = den + jnp.exp(sk - m)
            out = lax.dot_general(p.astype(BF16), vw_, (((1,), (0,)), ((), ())), preferred_element_type=F32)
            return rows, out / den, jnp.broadcast_to(m + jnp.log(den), (g.q_rows, HEAD_DIM))

        def step(i, carry):
            for rows, out, lse in [chain(i * n_chains + u) for u in range(n_chains)]:
                os_[rows, :] = out
                ls[rows, :] = lse
            return carry

        lax.fori_loop(0, g.n_items // n_chains, step, 0)
        if g.staged:
            g.unstage(o_ref, os_)
            g.unstage(l_ref, ls)

    def col(c0, per):
        return pl.BlockSpec((s, HEAD_DIM), lambda h: (0, c0 // LANES + h // per))

    in_specs = [col(cq, 1), col(ck, group), col(cv, group),
                pl.BlockSpec((None, g.q_rows, g.win), lambda h: (h, 0, 0))]
    args = [proj, proj, proj, bias]
    if has_sink:
        in_specs.append(pl.BlockSpec(memory_space=pltpu.SMEM))
        args.append(sink)
    shape = pltpu.HBM((s, nh * HEAD_DIM), F32)
    scratch = [pltpu.VMEM((dil * g.pad, HEAD_DIM), BF16), pltpu.VMEM((dil * g.pad, HEAD_DIM), BF16)]
    if g.staged:
        scratch += [pltpu.VMEM((s, HEAD_DIM), BF16), pltpu.VMEM((s, HEAD_DIM), F32), pltpu.VMEM((s, HEAD_DIM), F32)]
    return pl.pallas_call(
        body, name=name, grid=(nh,), in_specs=in_specs, out_specs=[col(0, 1), col(0, 1)], out_shape=[shape, shape],
        scratch_shapes=scratch, compiler_params=_params(("parallel",), 16 * s * HEAD_DIM * 4))(*args)


def band_attn_bwd(name, proj, bias, sink, dout, out, lse, dlse, *, half, q_rows, n_chains, dil, nh, group, cq, ck, cv):
    s, w = proj.shape
    g = _Band(s, half, q_rows, n_chains, dil)
    nkv = nh // group
    has_sink = sink is not None
    has_dl = dlse is not None
    n_in = 7 + int(has_sink) + int(has_dl)
    n_out = 4 + int(has_sink)

    def body(*refs):
        ins, outs, scr = refs[:n_in], refs[n_in:n_in + n_out], refs[n_in + n_out:]
        q_ref, k_ref, v_ref, b_ref, do_ref, o_ref, l_ref = ins[:7]
        s_ref = ins[7] if has_sink else None
        dl_ref = ins[n_in - 1] if has_dl else None
        dq_ref, dk_ref, dv_ref, db_ref = outs[:4]
        ks, vs, dks, dvs = scr[:4]
        scr = list(scr[4:])
        dsa = scr.pop(0) if has_sink else None
        if g.staged:
            qs, dos, os_, ls, dqs = scr[:5]
            dls = scr[5] if has_dl else None
            g.stage(qs, q_ref)
            g.stage(dos, do_ref)
            g.stage(os_, o_ref)
            g.stage(ls, l_ref)
            if has_dl:
                g.stage(dls, dl_ref)
        else:
            qs, dos, os_, ls, dqs, dls = None, do_ref, o_ref, l_ref, dq_ref, dl_ref
        h = pl.program_id(0)
        g.stage_kv(ks, k_ref)
        g.stage_kv(vs, v_ref)
        dks[...] = jnp.zeros_like(dks)
        dvs[...] = jnp.zeros_like(dvs)
        db_ref[...] = jnp.zeros_like(db_ref)
        bias_v = b_ref[...]
        if has_sink:
            sk = s_ref[h]
            dsa[...] = jnp.zeros_like(dsa)

        def chain(item):
            qoff, koff, edge = g.offsets(item)
            rows = pl.ds(qoff, g.q_rows)
            win = pl.ds(koff, g.win)
            qv = qs[rows, :] if g.staged else q_ref[rows, :].astype(BF16)
            kw_ = ks[win, :]
            vw_ = vs[win, :]
            sc = lax.dot_general(qv, kw_, (((1,), (1,)), ((), ())), preferred_element_type=F32) * ATTN_SCALE
            lv = ls[rows, :][:, 0:1]
            p = jnp.exp(sc + bias_v + edge - lv)
            dov = dos[rows, :]
            delta = jnp.sum(dov * os_[rows, :], axis=-1, keepdims=True)
            dob = dov.astype(BF16)
            dp = lax.dot_general(dob, vw_, (((1,), (1,)), ((), ())), preferred_element_type=F32)
            t = dp - delta
            if has_dl:
                t = t + dls[rows, :][:, 0:1]
            ds = p * t
            dsb = (ds * ATTN_SCALE).astype(BF16)
            dq = lax.dot_general(dsb, kw_, (((1,), (0,)), ((), ())), preferred_element_type=F32)
            dkc = lax.dot_general(dsb, qv, (((0,), (0,)), ((), ())), preferred_element_type=F32)
            dvc = lax.dot_general(p.astype(BF16), dob, (((0,), (0,)), ((), ())), preferred_element_type=F32)
            dsk = jnp.exp(sk - lv) * delta if has_sink else None
            return rows, win, dq, dkc, dvc, ds, dsk

        def step(i, carry):
            res = [chain(i * n_chains + u) for u in range(n_chains)]
            ds_sum = res[0][5]
            for rr in res[1:]:
                ds_sum = ds_sum + rr[5]
            db_ref[...] += ds_sum
            for rows, win, dq, dkc, dvc, ds, dsk in res:
                dqs[rows, :] = dq
                dks[win, :] += dkc
                dvs[win, :] += dvc
                if has_sink:
                    dsa[...] += dsk
            return carry

        lax.fori_loop(0, g.n_items // n_chains, step, 0)

        if g.staged:
            g.unstage(dq_ref, dqs)

        def emit_kv(add):
            for r in range(dil):
                lo = r * g.pad + half
                for dst_ref, src in ((dk_ref, dks), (dv_ref, dvs)):
                    val = src[lo:lo + g.seg, :]
                    if add:
                        val = val + dst_ref[g.rows_of(r), :]
                    dst_ref[g.rows_of(r), :] = val

        if group == 1:
            emit_kv(False)
        else:
            @pl.when(h % group == 0)
            def _():
                emit_kv(False)

            @pl.when(h % group != 0)
            def _():
                emit_kv(True)
        if has_sink:
            outs[4][...] = jnp.full((SUBLANES, LANES), -jnp.sum(dsa[...]), F32)

    def col(c0, per):
        return pl.BlockSpec((s, HEAD_DIM), lambda h: (0, c0 // LANES + h // per))

    b_spec = pl.BlockSpec((None, g.q_rows, g.win), lambda h: (h, 0, 0))
    in_specs = [col(cq, 1), col(ck, group), col(cv, group), b_spec, col(0, 1), col(0, 1), col(0, 1)]
    args = [proj, proj, proj, bias, dout, out, lse]
    if has_sink:
        in_specs.append(pl.BlockSpec(memory_space=pltpu.SMEM))
        args.append(sink)
    if has_dl:
        in_specs.append(col(0, 1))
        args.append(dlse)
    out_specs = [col(0, 1), col(0, group), col(0, group), b_spec]
    out_shape = [pltpu.HBM((s, nh * HEAD_DIM), F32), pltpu.HBM((s, nkv * HEAD_DIM), F32),
                 pltpu.HBM((s, nkv * HEAD_DIM), F32), jax.ShapeDtypeStruct((nh, g.q_rows, g.win), F32)]
    scratch = [pltpu.VMEM((dil * g.pad, HEAD_DIM), BF16), pltpu.VMEM((dil * g.pad, HEAD_DIM), BF16),
               pltpu.VMEM((dil * g.pad, HEAD_DIM), F32), pltpu.VMEM((dil * g.pad, HEAD_DIM), F32)]
    if has_sink:
        out_specs.append(pl.BlockSpec((None, SUBLANES, LANES), lambda h: (h, 0, 0)))
        out_shape.append(jax.ShapeDtypeStruct((nh, SUBLANES, LANES), F32))
        scratch.append(pltpu.VMEM((g.q_rows, 1), F32))
    if g.staged:
        scratch += [pltpu.VMEM((s, HEAD_DIM), BF16)] + [pltpu.VMEM((s, HEAD_DIM), F32)] * (4 + int(has_dl))
    res = pl.pallas_call(
        body, name=name, grid=(nh,), in_specs=in_specs, out_specs=out_specs, out_shape=out_shape,
        scratch_shapes=scratch, compiler_params=_params(("arbitrary",), 28 * s * HEAD_DIM * 4))(*args)
    return res[0], res[1], res[2], res[3], (res[4] if has_sink else None)


def dil_merge_fwd(name, outs, lses):
    s, w = outs[0].shape
    ts = _pick(s, ROW_TILE_CANDS)
    ng = len(outs)

    def body(*refs):
        o_refs, l_refs, y_ref = refs[:ng], refs[ng:2 * ng], refs[2 * ng]
        ls = [l[...] for l in l_refs]
        mx = ls[0]
        for l in ls[1:]:
            mx = jnp.maximum(mx, l)
        es = [jnp.exp(l - mx) for l in ls]
        tot = es[0]
        for e in es[1:]:
            tot = tot + e
        acc = (es[0] / tot) * o_refs[0][...]
        for e, o in zip(es[1:], o_refs[1:]):
            acc = acc + (e / tot) * o[...]
        y_ref[...] = acc.astype(y_ref.dtype)

    row = pl.BlockSpec((ts, w), lambda i: (i, 0))
    return pl.pallas_call(
        body, name=name, grid=(s // ts,), in_specs=[row] * (2 * ng), out_specs=row,
        out_shape=pltpu.HBM((s, w), BF16),
        compiler_params=_params(("parallel",), 10 * ts * w * 4))(*outs, *lses)


def dil_merge_bwd(name, dy, outs, lses):
    s, w = outs[0].shape
    ts = _pick(s, ROW_TILE_CANDS)
    ng = len(outs)
    nhead = w // HEAD_DIM

    def body(*refs):
        dy_ref = refs[0]
        o_refs, l_refs = refs[1:1 + ng], refs[1 + ng:1 + 2 * ng]
        do_refs, dl_refs = refs[1 + 2 * ng:1 + 3 * ng], refs[1 + 3 * ng:1 + 4 * ng]
        for hh in range(nhead):
            cols = slice(hh * HEAD_DIM, (hh + 1) * HEAD_DIM)
            dyv = dy_ref[:, cols]
            ls = [l[:, cols] for l in l_refs]
            mx = ls[0]
            for l in ls[1:]:
                mx = jnp.maximum(mx, l)
            es = [jnp.exp(l - mx) for l in ls]
            tot = es[0]
            for e in es[1:]:
                tot = tot + e
            alphas = [e / tot for e in es]
            dal = [jnp.broadcast_to(jnp.sum(dyv * o[:, cols], axis=-1, keepdims=True), dyv.shape) for o in o_refs]
            mean = alphas[0] * dal[0]
            for a, d in zip(alphas[1:], dal[1:]):
                mean = mean + a * d
            for g in range(ng):
                do_refs[g][:, cols] = alphas[g] * dyv
                dl_refs[g][:, cols] = alphas[g] * (dal[g] - mean)

    row = pl.BlockSpec((ts, w), lambda i: (i, 0))
    shape = pltpu.HBM((s, w), F32)
    res = pl.pallas_call(
        body, name=name, grid=(s // ts,), in_specs=[row] * (1 + 2 * ng), out_specs=[row] * (2 * ng),
        out_shape=[shape] * (2 * ng),
        compiler_params=_params(("parallel",), 16 * ts * w * 4))(dy, *outs, *lses)
    return res[:ng], res[ng:]


def _adamw(w, g, m, v):
    m = ADAM_B1 * m + (1.0 - ADAM_B1) * g
    v = ADAM_B2 * v + (1.0 - ADAM_B2) * (g * g)
    m_hat = m / (1.0 - ADAM_B1 ** ADAM_STEP)
    v_hat = v / (1.0 - ADAM_B2 ** ADAM_STEP)
    delta = -ADAM_LR * (m_hat / (jnp.sqrt(v_hat) + ADAM_EPS) + ADAM_WD * w)
    return delta, m, v


def _row_tile(r, c, budget=1 << 20):
    if r * c * 4 <= budget or r % SUBLANES:
        return r
    for t in (1024, 512, 256, 128, 64, 32, 16, 8):
        if r % t == 0 and t * c * 4 <= budget:
            return t
    return SUBLANES


def adam_small(name, g, w, m, v):
    def body(g_ref, w_ref, m_ref, v_ref, d_ref, nm_ref, nv_ref):
        d_ref[...], nm_ref[...], nv_ref[...] = _adamw(w_ref[...], g_ref[...], m_ref[...], v_ref[...])

    shape = jax.ShapeDtypeStruct(w.shape, F32)
    return pl.pallas_call(body, name=name, out_shape=[shape, shape, shape])(g, w, m, v)


def reduce_adam(name, mine, theirs, w, m, v):
    nq, r, c = mine.shape
    tr = _row_tile(r, c)

    def body(*refs):
        parts, (w_ref, m_ref, v_ref, g_ref, d_ref, nm_ref, nv_ref) = refs[:nq], refs[nq:]
        g = parts[0][...].astype(F32)
        for p_ref in parts[1:]:
            g = g + p_ref[...].astype(F32)
        g_ref[...] = g
        d_ref[...], nm_ref[...], nv_ref[...] = _adamw(w_ref[...], g, m_ref[...], v_ref[...])

    def slot(q):
        return pl.BlockSpec((None, tr, c), lambda i: (q, i, 0))

    row = pl.BlockSpec((tr, c), lambda i: (i, 0))
    shape = jax.ShapeDtypeStruct((r, c), F32)
    return pl.pallas_call(
        body, name=name, grid=(r // tr,), in_specs=[slot(q) for q in range(nq)] + [row, row, row],
        out_specs=[row] * 4, out_shape=[shape] * 4,
        compiler_params=_params(("parallel",), (nq * 2 + 7 * 4) * tr * c))(mine, *[theirs] * (nq - 1), *[_in_hbm(t) for t in (w, m, v)])


def _place():
    return lax.axis_index("x"), lax.axis_index("y"), lax.axis_index("c")


def _flip(pos, bits):
    return tuple((1 - p) if b else p for p, b in zip(pos, bits))


def _index(pos):
    return 4 * pos[0] + 2 * pos[1] + pos[2]


ANY = pl.BlockSpec(memory_space=pl.ANY)


HBM = pl.BlockSpec(memory_space=pltpu.HBM)
SEM = pl.BlockSpec(memory_space=pltpu.SEMAPHORE)
EFFECT = pltpu.SideEffectType.DATAFLOW_SIDE_EFFECTING
TO_SIBLING = (0, 0, 1)
TO_CHIPS = [(1, 0, 0), (0, 1, 0), (1, 1, 0)]


def _in_hbm(a):
    return pltpu.with_memory_space_constraint(a, pltpu.HBM)


def _token_value(token):
    return token[0, 0]


def _when(pred, fn):
    if pred is True:
        fn()
    elif pred is not False:
        pl.when(pred)(fn)


def _plan_copy(k, entry, ins, lnd, send_sems, recv_sems):
    a, src_a, sblk, lblk, to, send_if, recv_if = entry
    src = lnd[a] if src_a is None else ins[src_a]
    return pltpu.make_async_remote_copy(
        src_ref=src.at[sblk], dst_ref=lnd[a].at[lblk], send_sem=send_sems.at[k], recv_sem=recv_sems.at[k],
        device_id=to, device_id_type=MESH), send_if, recv_if


def split_start(name, srcs, lands, plan, after):
    ns, nl = len(srcs), len(lands)
    n_copies = len(plan((0, 0, 0)))

    def body(*refs):
        ins, lnd = refs[:ns], refs[ns:ns + nl]
        send_sems, recv_sems = refs[ns + nl + 1], refs[ns + nl + 2]
        token = refs[-1]
        for k, entry in enumerate(plan(_place())):
            cp, send_if, _ = _plan_copy(k, entry, ins, lnd, send_sems, recv_sems)
            _when(send_if, cp.start)
        token[...] = jnp.zeros_like(token)

    outs = pl.pallas_call(
        body, name=name,
        out_shape=(pltpu.SemaphoreType.DMA((n_copies,)), pltpu.SemaphoreType.DMA((n_copies,)),
                   *[pltpu.HBM(a.shape, a.dtype) for a in srcs], *[pltpu.HBM(a.shape, a.dtype) for a in lands],
                   jax.ShapeDtypeStruct((SUBLANES, LANES), F32)),
        in_specs=[HBM] * (ns + nl) + [ANY],
        out_specs=(SEM, SEM, *[HBM] * (ns + nl), pl.BlockSpec(memory_space=pltpu.VMEM)),
        input_output_aliases={i: 2 + i for i in range(ns + nl)},
        compiler_params=pltpu.CompilerParams(has_side_effects=EFFECT),
    )(*[_in_hbm(a) for a in srcs], *[_in_hbm(a) for a in lands], after)
    return outs[0], outs[1], list(outs[2:2 + ns]), list(outs[2 + ns:2 + ns + nl]), outs[-1]


def split_wait(name, send_sems, recv_sems, srcs, lands, plan, after):
    ns, nl = len(srcs), len(lands)

    def body(*refs):
        ins, lnd = refs[:ns], refs[ns:ns + nl]
        s_sems, r_sems = refs[ns + nl], refs[ns + nl + 1]
        for k, entry in enumerate(plan(_place())):
            cp, send_if, recv_if = _plan_copy(k, entry, ins, lnd, s_sems, r_sems)
            _when(send_if, cp.wait_send)
            _when(recv_if, cp.wait_recv)
        refs[-1][...] = jnp.zeros((SUBLANES, LANES), F32)

    outs = pl.pallas_call(
        body, name=name,
        out_shape=(*[pltpu.HBM(a.shape, a.dtype) for a in srcs], *[pltpu.HBM(a.shape, a.dtype) for a in lands],
                   jax.ShapeDtypeStruct((SUBLANES, LANES), F32)),
        in_specs=[HBM] * (ns + nl) + [SEM, SEM, ANY],
        out_specs=(*[HBM] * (ns + nl), pl.BlockSpec(memory_space=pltpu.VMEM)),
        input_output_aliases={i: i for i in range(ns + nl)},
        compiler_params=pltpu.CompilerParams(has_side_effects=EFFECT),
    )(*srcs, *lands, send_sems, recv_sems, after)
    return list(outs[:ns]), list(outs[ns:ns + nl]), outs[-1]


NORTH = 1


def ag_plan(n, rels=TO_CHIPS):
    def plan(me):
        x, y, c = me
        entries = []
        for a in range(n):
            for t in (NORTH, 1 - NORTH):
                blk = _index((x, y, t))
                for rel in rels:
                    entries.append((a, None, blk, blk, _flip((x, y, t), rel), c == NORTH, c == t))
        return entries
    return plan


TO_X, TO_Y = TO_CHIPS[0], TO_CHIPS[1]


def relay_plan(n):
    def plan(me):
        x, y, c = me
        entries = []
        for a in range(n):
            for t, came, goes in ((NORTH, TO_X, TO_Y), (1 - NORTH, TO_Y, TO_X)):
                blk = _index(_flip((x, y, t), came))
                entries.append((a, None, blk, blk, _flip((x, y, t), goes), c == t, c == t))
        return entries
    return plan


def ag_pair(name, lands, after):
    n = len(lands)

    def body(*refs):
        lnd = refs[n + 1:2 * n + 1]
        token = refs[2 * n + 1]
        send_sems, recv_sems = refs[2 * n + 2:]
        token[...] = jnp.zeros_like(token)
        me = _place()
        sibling = _flip(me, TO_SIBLING)
        copies = []
        for a in range(n):
            mine, theirs = lnd[a].at[_index(me)], lnd[a].at[_index(sibling)]
            cp = pltpu.make_async_remote_copy(src_ref=mine, dst_ref=mine, send_sem=send_sems.at[a],
                                              recv_sem=recv_sems.at[a], device_id=sibling, device_id_type=MESH)
            cp.start()
            copies.append((cp, pltpu.make_async_remote_copy(
                src_ref=mine, dst_ref=theirs, send_sem=send_sems.at[a], recv_sem=recv_sems.at[a], device_id=sibling,
                device_id_type=MESH)))
        for cp, arrival in copies:
            arrival.wait_recv()
        for cp, arrival in copies:
            cp.wait_send()

    outs = pl.pallas_call(
        body, name=name, in_specs=[ANY] * (n + 1), out_specs=[ANY] * n + [pl.BlockSpec(memory_space=pltpu.VMEM)],
        out_shape=[jax.ShapeDtypeStruct(l.shape, l.dtype) for l in lands]
        + [jax.ShapeDtypeStruct((SUBLANES, LANES), F32)],
        input_output_aliases={a: a for a in range(n)},
        scratch_shapes=[pltpu.SemaphoreType.DMA((n,)), pltpu.SemaphoreType.DMA((n,))],
    )(*lands, after)
    return list(outs[:n]), outs[n]


def pass_plan(n):
    def plan(me):
        sibling = _flip(me, TO_SIBLING)
        return [(a, None, _index(_flip(me, rel)), _index(_flip(me, rel)), sibling, True, True)
                for a in range(n) for rel in TO_CHIPS]
    return plan


def ag_finish(name, lands):
    n = len(lands)

    def body(*refs):
        lnd = refs[n:2 * n]
        send_sems, recv_sems = refs[2 * n:]
        me = _place()
        sibling = _flip(me, TO_SIBLING)
        copies = []
        for a in range(n):
            for j, rel in enumerate(TO_CHIPS):
                blk = lnd[a].at[_index(_flip(me, rel))]
                there = lnd[a].at[_index(_flip(sibling, rel))]
                cp = pltpu.make_async_remote_copy(
                    src_ref=blk, dst_ref=blk, send_sem=send_sems.at[a * 3 + j], recv_sem=recv_sems.at[a * 3 + j],
                    device_id=sibling, device_id_type=MESH)
                cp.start()
                copies.append((cp, pltpu.make_async_remote_copy(
                    src_ref=blk, dst_ref=there, send_sem=send_sems.at[a * 3 + j], recv_sem=recv_sems.at[a * 3 + j],
                    device_id=sibling, device_id_type=MESH)))
        for cp, arrival in copies:
            arrival.wait_recv()
        for cp, arrival in copies:
            cp.wait_send()

    return pl.pallas_call(
        body, name=name, in_specs=[ANY] * n, out_specs=[ANY] * n,
        out_shape=[jax.ShapeDtypeStruct(l.shape, l.dtype) for l in lands],
        input_output_aliases={a: a for a in range(n)},
        scratch_shapes=[pltpu.SemaphoreType.DMA((3 * n,)), pltpu.SemaphoreType.DMA((3 * n,))],
    )(*lands)


REL = [(b >> 2 & 1, b >> 1 & 1, b & 1) for b in range(N_DEV)]


CHIP_REL = [(0, 0, 0)] + TO_CHIPS
N_CHIPS = len(CHIP_REL)


def rs_pair(name, parts):
    n = len(parts)

    def body(*refs):
        ins, got = refs[:n], refs[n:2 * n]
        send_sems, recv_sems = refs[2 * n:]
        me = _place()
        sibling = _flip(me, TO_SIBLING)
        remote = []
        for a in range(n):
            for q, rel in enumerate(CHIP_REL):
                k = a * N_CHIPS + q
                cp = pltpu.make_async_remote_copy(
                    src_ref=ins[a].at[_index(_flip(sibling, rel))], dst_ref=got[a].at[q], send_sem=send_sems.at[k],
                    recv_sem=recv_sems.at[k], device_id=sibling, device_id_type=MESH)
                cp.start()
                remote.append(cp)
        for cp in remote:
            cp.wait_recv()
        for cp in remote:
            cp.wait_send()

    shapes = [jax.ShapeDtypeStruct((N_CHIPS,) + tuple(p.shape[1:]), p.dtype) for p in parts]
    res = pl.pallas_call(
        body, name=name, in_specs=[ANY] * n, out_specs=[ANY] * n, out_shape=shapes,
        scratch_shapes=[pltpu.SemaphoreType.DMA((N_CHIPS * n,)), pltpu.SemaphoreType.DMA((N_CHIPS * n,))],
    )(*parts)
    return list(res)


def own_blocks():
    me = _place()
    return jnp.stack([_index(_flip(me, rel)) for rel in CHIP_REL]).astype(jnp.int32)


def pair_add(name, blocks, parts, got):
    nq, r, c = got.shape
    tr = _row_tile(r, c, budget=6 << 20)

    def body(blk_ref, a_ref, b_ref, o_ref):
        o_ref[...] = (a_ref[...].astype(F32) + b_ref[...].astype(F32)).astype(o_ref.dtype)

    spec = pl.BlockSpec((None, tr, c), lambda q, i, blk: (q, i, 0))
    return pl.pallas_call(
        body, name=name,
        grid_spec=pltpu.PrefetchScalarGridSpec(
            num_scalar_prefetch=1, grid=(nq, r // tr),
            in_specs=[pl.BlockSpec((None, tr, c), lambda q, i, blk: (blk[q], i, 0)), spec], out_specs=spec),
        out_shape=pltpu.HBM(got.shape, got.dtype),
        compiler_params=_params(("arbitrary", "arbitrary"), 6 * tr * c * 2))(blocks, parts, got)


def rs_pair_plan(n):
    def plan(me):
        sibling = _flip(me, TO_SIBLING)
        return [(a, a, _index(_flip(sibling, rel)), q, sibling, True, True)
                for a in range(n) for q, rel in enumerate(CHIP_REL)]
    return plan


def rs_plan(n):
    def plan(me):
        return [(a, a, q, q, _flip(me, CHIP_REL[q]), True, True) for a in range(n) for q in range(1, N_CHIPS)]
    return plan


def rs_start(name, sums, after):
    lands = [lax.empty(t.shape, t.dtype) for t in sums]
    return split_start(name, sums, lands, rs_plan(len(sums)), after)


def allreduce_small(name, pack, after):
    rows, lanes = pack.shape

    def body(x_ref, after_ref, o_ref, land, send_sems, recv_sems):
        me = _place()
        idx = _index(me)
        land[idx] = x_ref[...]
        copies = []
        for r in range(1, N_DEV):
            peer = _flip(me, REL[r])
            cp = pltpu.make_async_remote_copy(
                src_ref=x_ref, dst_ref=land.at[idx], send_sem=send_sems.at[r - 1], recv_sem=recv_sems.at[r - 1],
                device_id=peer, device_id_type=MESH)
            cp.start()
            copies.append(cp)
        for cp in copies:
            cp.wait_recv()
        for cp in copies:
            cp.wait_send()
        acc = land[0]
        for i in range(1, N_DEV):
            acc = acc + land[i]
        o_ref[...] = acc

    return pl.pallas_call(
        body, name=name, in_specs=[pl.BlockSpec(memory_space=pltpu.VMEM), ANY],
        out_specs=pl.BlockSpec(memory_space=pltpu.VMEM), out_shape=jax.ShapeDtypeStruct((rows, lanes), F32),
        scratch_shapes=[pltpu.VMEM((N_DEV, rows, lanes), F32), pltpu.SemaphoreType.DMA((7,)),
                        pltpu.SemaphoreType.DMA((7,))],
    )(pack, after)


def _pad_rows(a, rows):
    return jnp.pad(a, ((0, rows - a.shape[0]), (0, 0)))


def _as_tiles(vec):
    n = vec.shape[0]
    rows = -(-n // LANES)
    rows = -(-rows // SUBLANES) * SUBLANES
    return jnp.pad(vec, (0, rows * LANES - n)).reshape(rows, LANES)


def kernel(x, p, rel_bias_table, attn_norm, w_in, sink_a, w_branch_a, w_branch_b, w_out, ffn_norm, w_ffn_gate, w_ffn_up, conv_w, conv_b, w_ffn_down, ple_norm, w_ple_gate, w_ple_proj, final_norm, loss_target, m_rel_bias_table, m_attn_norm, m_w_in, m_sink_a, m_w_branch_a, m_w_branch_b, m_w_out, m_ffn_norm, m_w_ffn_gate, m_w_ffn_up, m_conv_w, m_conv_b, m_w_ffn_down, m_ple_norm, m_w_ple_gate, m_w_ple_proj, m_final_norm, v_rel_bias_table, v_attn_norm, v_w_in, v_sink_a, v_w_branch_a, v_w_branch_b, v_w_out, v_ffn_norm, v_w_ffn_gate, v_w_ffn_up, v_conv_w, v_conv_b, v_w_ffn_down, v_ple_norm, v_w_ple_gate, v_w_ple_proj, v_final_norm):
    xs = x[0]
    s, d = xs.shape
    ps = p[0, 0]
    target = loss_target[0]
    me = 4 * lax.axis_index("x") + 2 * lax.axis_index("y") + lax.axis_index("c")

    big = dict(w_in=w_in[0], w_branch_a=w_branch_a[0], w_branch_b=w_branch_b[0], w_out=w_out[0],
               w_ffn_gate=w_ffn_gate[0], w_ffn_up=w_ffn_up[0], w_ffn_down=w_ffn_down[0],
               w_ple_gate=w_ple_gate[0], w_ple_proj=w_ple_proj[0])
    big_m = dict(w_in=m_w_in[0], w_branch_a=m_w_branch_a[0], w_branch_b=m_w_branch_b[0], w_out=m_w_out[0],
                 w_ffn_gate=m_w_ffn_gate[0], w_ffn_up=m_w_ffn_up[0], w_ffn_down=m_w_ffn_down[0],
                 w_ple_gate=m_w_ple_gate[0], w_ple_proj=m_w_ple_proj[0])
    big_v = dict(w_in=v_w_in[0], w_branch_a=v_w_branch_a[0], w_branch_b=v_w_branch_b[0], w_out=v_w_out[0],
                 w_ffn_gate=v_w_ffn_gate[0], w_ffn_up=v_w_ffn_up[0], w_ffn_down=v_w_ffn_down[0],
                 w_ple_gate=v_w_ple_gate[0], w_ple_proj=v_w_ple_proj[0])
    names = list(big)
    nf = big["w_ffn_gate"].shape[1]

    shards = {k: big[k].astype(BF16) for k in names}
    shards["conv_w"] = _pad_rows(conv_w[0], SUBLANES)
    flipped = ("w_ffn_gate", "w_ffn_up")
    for k in flipped:
        big[k], big_m[k], big_v[k] = big[k].T, big_m[k].T, big_v[k].T
    ag_groups = [["w_in"], ["w_branch_a", "w_branch_b", "w_out"], ["w_ffn_gate", "conv_w"], ["w_ffn_up"],
                 ["w_ffn_down"], ["w_ple_gate", "w_ple_proj"]]
    ag_started = {}
    wg = {}

    ag_paired, ag_passing = {}, {}

    def pair(gi, after):
        lands = [lax.dynamic_update_index_in_dim(lax.empty((N_DEV,) + shards[k].shape, shards[k].dtype), shards[k],
                                                 me, 0) for k in ag_groups[gi]]
        ag_paired[gi], token = ag_pair(f"ag_pair{gi}", lands, after)
        return token

    def copies(gi):
        return ag_plan(len(ag_groups[gi]), [TO_X, TO_Y] if gi == 0 else TO_CHIPS)

    def start(gi, after):
        s_sems, r_sems, _, lands, token = split_start(f"ag_start{gi}", [], ag_paired[gi], copies(gi), after)
        ag_started[gi] = (s_sems, r_sems, lands)
        return token

    def landed(gi, after):
        s_sems, r_sems, lands = ag_started[gi]
        return split_wait(f"ag_wait{gi}", s_sems, r_sems, [], lands, copies(gi), after)[1:]

    def relayed(lands, after, meanwhile):
        plan = relay_plan(len(lands))
        s_sems, r_sems, _, lands, token = split_start("ag_relay0", [], lands, plan, after)
        return split_wait("ag_relayed0", s_sems, r_sems, [], lands, plan, meanwhile + _token_value(token))[1:]

    def pass_on(gi, lands, after):
        s_sems, r_sems, _, lands, token = split_start(f"ag_pass{gi}", [], lands, pass_plan(len(lands)), after)
        ag_passing[gi] = (s_sems, r_sems, lands)
        return token

    def ready(gi, after):
        s_sems, r_sems, lands = ag_passing[gi]
        lands = split_wait(f"ag_ready{gi}", s_sems, r_sems, [], lands, pass_plan(len(lands)), after)[1]
        wg.update(zip(ag_groups[gi], lands))

    cb = conv_b.reshape(N_DEV, 1, nf)

    table_t = rel_bias_table.T
    geo_a = dict(half=A_BLOCK, q_rows=ATTN_Q_ROWS, n_chains=ATTN_CHAINS, dil=1, nh=A_Q_HEADS, group=A_GROUP,
                 cq=COL_QA, ck=COL_KA, cv=COL_VA)
    geo_b = [dict(half=B_BLOCK, q_rows=min(ATTN_Q_ROWS, s // dil), n_chains=ATTN_CHAINS, dil=dil,
                  nh=B_HEADS_PER_GROUP, group=1, cq=COL_QB + g * B_OUT_W, ck=COL_KB + g * B_OUT_W,
                  cv=COL_VB + g * B_OUT_W) for g, (_, dil) in enumerate(B_PATTERNS)]
    bucket_a = bucket_tile(geo_a["q_rows"], A_BLOCK, 1)
    bias_a = bias_build("bias_a", table_t, bucket_a, 0, A_Q_HEADS, A_BLOCK)
    buckets_b = [bucket_tile(gb["q_rows"], B_BLOCK, gb["dil"]) for gb in geo_b]
    biases_b = [bias_build(f"bias_b{g}", table_t, buckets_b[g], A_Q_HEADS + g * B_HEADS_PER_GROUP, B_HEADS_PER_GROUP,
                           B_BLOCK) for g in range(len(B_PATTERNS))]

    token = start(0, pair(0, xs))
    h = rms_fwd("rms_attn", xs, attn_norm + _token_value(token))
    lands0, token = landed(0, pair(5, pair(4, pair(3, pair(2, pair(1, h))))))
    bias_corner = bias_a[0, :1, :1] + sum(b[0, :1, :1] for b in biases_b)
    lands0, token = relayed(lands0, token, bias_corner)
    token = start(5, start(4, start(3, start(2, start(1, token)))))
    wg["w_in"] = ag_finish("ag_finish0", lands0)[0]
    proj = mm_cols("proj_in", h, wg["w_in"], F32, fold=True, after=token)
    token = pass_on(1, landed(1, proj)[0], proj)
    sink = sink_a[0] + _token_value(token)
    ya, lse_a = band_attn_fwd("attn_a_fwd", proj, bias_a, sink, **geo_a)
    outs_b, lses_b = [], []
    for g in range(len(B_PATTERNS)):
        o, l = band_attn_fwd(f"attn_b{g}_fwd", proj, biases_b[g], None, **geo_b[g])
        outs_b.append(o)
        lses_b.append(l)
    yb = dil_merge_fwd("dil_merge_fwd", outs_b, lses_b)
    ready(1, yb)
    token = pass_on(2, landed(2, yb)[0], yb)
    w_out_full = wg["w_out"].reshape(d, d)
    ta = mm_cols("branch_a", ya, wg["w_branch_a"], F32, fold=True, after=token)
    tb = mm_cols("branch_b", yb, wg["w_branch_b"], F32, fold=True)
    merged = gate_merge_fwd("gate_merge_fwd", proj, ta, tb, d)
    x1 = mm_plain("mix_out", merged, w_out_full, F32, res=xs)

    hf = rms_fwd("rms_ffn", x1, ffn_norm)
    ready(2, hf)
    token = pass_on(3, landed(3, hf)[0], hf)
    cw = wg["conv_w"]
    gpre = mm_cols("ffn_gate", hf, wg["w_ffn_gate"], F32, fold=False, after=token)
    ready(3, gpre)
    token = pass_on(4, landed(4, gpre)[0], gpre)
    u = mm_cols("ffn_up", hf, wg["w_ffn_up"], F32, fold=False, after=token)
    z = ffn_mid_fwd("ffn_mid_fwd", gpre, u, cw, cb)
    ready(4, z)
    token = pass_on(5, landed(5, z)[0], z)
    x2 = mm_jsum("ffn_down", z, wg["w_ffn_down"], F32, res=x1, after=token)

    hp = rms_fwd("rms_ple", x2, ple_norm)
    ready(5, hp)
    w_pg_full = wg["w_ple_gate"].reshape(d, d)
    lp = mm_plain("ple_gate", hp, w_pg_full, F32)
    pp = mm_cols("ple_proj", ps, wg["w_ple_proj"], F32, fold=True)
    loss_part, dx3, dlp, dpp, d_final = tail_fwd_bwd("tail", x2, lp, pp, final_norm.reshape(1, d), target)

    grads = {}
    rs_started = []
    blocks = own_blocks()

    exchanging = []

    def exchange(tag, keys):
        parts = [grads[k] for k in keys]
        lands = [lax.empty((N_CHIPS,) + tuple(p.shape[1:]), p.dtype) for p in parts]
        s_sems, r_sems, parts, lands, token = split_start(f"rs_pair_{tag}", parts, lands, rs_pair_plan(len(keys)), blocks)
        exchanging.append((tag, keys, s_sems, r_sems, parts, lands))
        return _token_value(token)

    def send(after):
        tag, keys, s_sems, r_sems, parts, lands = exchanging.pop(0)
        parts, got, _ = split_wait(f"rs_paired_{tag}", s_sems, r_sems, parts, lands, rs_pair_plan(len(keys)), after)
        return send_sums(tag, keys, parts, got)

    def send_sums(tag, keys, parts, got):
        sums = [pair_add(f"pair_add_{k}", blocks, p, g) for k, p, g in zip(keys, parts, got)]
        s_sems, r_sems, srcs, lands, token = rs_start(f"rs_start_{tag}", sums, blocks)
        rs_started.append((tag, keys, s_sems, r_sems, srcs, lands))
        return token

    grads["w_ple_proj"] = mm_tn_cols("d_w_ple_proj", ps, dpp, N_DEV, big["w_ple_proj"].shape[1], BF16, folded=True)
    grads["w_ple_gate"] = mm_tn_plain("d_w_ple_gate", hp, dlp, BF16).reshape(N_DEV, d // N_DEV, d)
    tok = exchange("ple", ["w_ple_proj", "w_ple_gate"])
    dhp = mm_nt_plain("d_hp", dlp, w_pg_full, F32)
    dx2, dx2_b, d_ple = rms_bwd("rms_ple_bwd", x2, ple_norm + tok, dhp, dx3, True)

    dz = mm_nt_j("d_z", dx2_b, wg["w_ffn_down"], BF16)
    grads["w_ffn_down"] = mm_tn_j("d_w_ffn_down", z, dx2_b, BF16)
    tok = _token_value(send(dz)) + exchange("down", ["w_ffn_down"])
    du, dgpre, dcw = ffn_mid_bwd("ffn_mid_bwd", gpre, u, dz, cw, cb + tok)
    grads["w_ffn_up"] = mm_tn_j("d_w_ffn_up", du, hf, BF16)
    grads["w_ffn_gate"] = mm_tn_j("d_w_ffn_gate", dgpre, hf, BF16)
    dhf = mm_nt_jsum("d_hf_up", du, wg["w_ffn_up"], F32, folded=False)
    dhf = mm_nt_jsum("d_hf_gate", dgpre, wg["w_ffn_gate"], F32, folded=False, res=dhf)
    tok = _token_value(send(dhf)) + exchange("upgate", ["w_ffn_up", "w_ffn_gate"])
    dx1, dx1_b, d_ffn = rms_bwd("rms_ffn_bwd", x1, ffn_norm + tok, dhf, dx2, True)

    dmerged = mm_nt_plain("d_merged", dx1_b, w_out_full, F32)
    grads["w_out"] = mm_tn_plain("d_w_out", merged, dx1_b, BF16).reshape(N_DEV, d // N_DEV, d)
    dta, dtb, dga, dgb = gate_merge_bwd("gate_merge_bwd", dmerged, proj, ta, tb, d)
    grads["w_branch_a"] = mm_tn_cols("d_w_branch_a", ya, dta, N_DEV, big["w_branch_a"].shape[1], BF16, folded=True)
    grads["w_branch_b"] = mm_tn_cols("d_w_branch_b", yb, dtb, N_DEV, big["w_branch_b"].shape[1], BF16, folded=True)
    dya = mm_nt_jsum("d_ya", dta, wg["w_branch_a"], F32, folded=True)
    dyb = mm_nt_jsum("d_yb", dtb, wg["w_branch_b"], F32, folded=True)
    tok = _token_value(send(dyb)) + exchange("mix", ["w_out", "w_branch_a", "w_branch_b"])
    dqa, dka, dva, dbias_a, dsink = band_attn_bwd("attn_a_bwd", proj, bias_a, sink + tok, dya, ya, lse_a, None, **geo_a)
    douts_b, dlses_b = dil_merge_bwd("dil_merge_bwd", dyb, outs_b, lses_b)
    dq_b, dk_b, dv_b, dbias_b = [], [], [], []
    for g in range(len(B_PATTERNS)):
        dq, dk, dv, db, _ = band_attn_bwd(f"attn_b{g}_bwd", proj, biases_b[g], None, douts_b[g], outs_b[g], lses_b[g],
                                          dlses_b[g], **geo_b[g])
        dq_b.append(dq)
        dk_b.append(dk)
        dv_b.append(dv)
        dbias_b.append(db)
    dproj = jnp.concatenate([t.astype(BF16) for t in [dqa, dka, dva] + dq_b + dk_b + dv_b + [dga, dgb]], axis=1)
    token = send(dproj)
    grads["w_in"] = mm_tn_cols("d_w_in", h, dproj, N_DEV, big["w_in"].shape[1], BF16, folded=True, after=token)
    token = send_sums("in", ["w_in"], [grads["w_in"]], rs_pair("rs_pair_in", [grads["w_in"]]))
    dh = mm_nt_jsum("d_h", dproj, wg["w_in"], F32, folded=True, after=token)
    grad_x, _, d_attn = rms_bwd("rms_attn_bwd", xs, attn_norm, dh, dx1, False)

    dt_a = table_grad("table_grad_a", dbias_a, bucket_a)[:, 0, :N_BUCKETS]
    dt_b = [table_grad(f"table_grad_b{g}", dbias_b[g], buckets_b[g])[:, 0, :N_BUCKETS] for g in range(len(B_PATTERNS))]
    d_table_part = jnp.concatenate([dt_a] + dt_b, axis=0).T

    pieces = [
        ("loss", loss_part[0, :1]),
        ("table", d_table_part.reshape(-1)),
        ("attn_norm", d_attn.reshape(-1)),
        ("sink", dsink[:, 0, 0]),
        ("ffn_norm", d_ffn.reshape(-1)),
        ("conv_w", dcw[:, 0:3, :].reshape(-1)),
        ("conv_b", dcw[:, 3, :].reshape(-1)),
        ("ple_norm", d_ple.reshape(-1)),
        ("final_norm", d_final.reshape(-1)),
    ]
    tiles = [_as_tiles(v) for _, v in pieces]
    pack = jnp.concatenate(tiles, axis=0)

    out_g, out_d, out_m, out_v = {}, {}, {}, {}

    def finish(group, after):
        tag, keys, s_sems, r_sems, srcs, lands = group
        srcs, lands, _ = split_wait(f"rs_wait_{tag}", s_sems, r_sems, srcs, lands, rs_plan(len(keys)), after)
        for k, mine, theirs in zip(keys, srcs, lands):
            res = reduce_adam("adam_" + k, mine, theirs, big[k], big_m[k], big_v[k])
            after = res[1]
            out_g[k], out_d[k], out_m[k], out_v[k] = [(t.T if k in flipped else t)[None] for t in res]
        return after

    after = pack
    for group in rs_started[:-1]:
        after = finish(group, after)
    total = allreduce_small("allreduce_small", pack, after)
    finish(rs_started[-1], total)
    small = {}
    row = 0
    for (nm, v), t in zip(pieces, tiles):
        small[nm] = total[row:row + t.shape[0]].reshape(-1)[:v.shape[0]]
        row += t.shape[0]
    loss = small["loss"][0]
    g_small = dict(
        rel_bias_table=small["table"].reshape(rel_bias_table.shape),
        attn_norm=small["attn_norm"].reshape(attn_norm.shape),
        sink_a=small["sink"].reshape(sink_a.shape),
        ffn_norm=small["ffn_norm"].reshape(ffn_norm.shape),
        conv_w=lax.dynamic_index_in_dim(small["conv_w"].reshape(N_DEV, 3, nf), me, 0, keepdims=False)[None],
        conv_b=small["conv_b"].reshape(conv_b.shape),
        ple_norm=small["ple_norm"].reshape(ple_norm.shape),
        final_norm=small["final_norm"].reshape(1, d),
    )
    w_small = dict(rel_bias_table=(rel_bias_table, m_rel_bias_table, v_rel_bias_table),
                   attn_norm=(attn_norm, m_attn_norm, v_attn_norm), sink_a=(sink_a, m_sink_a, v_sink_a),
                   ffn_norm=(ffn_norm, m_ffn_norm, v_ffn_norm), conv_w=(conv_w, m_conv_w, v_conv_w),
                   conv_b=(conv_b, m_conv_b, v_conv_b), ple_norm=(ple_norm, m_ple_norm, v_ple_norm),
                   final_norm=(final_norm, m_final_norm, v_final_norm))

    for k, (wv, mv, vv) in w_small.items():
        shape = wv.shape
        two_d = (1, shape[0]) if len(shape) == 1 else ((shape[0] * shape[1], shape[2]) if len(shape) == 3 else shape)
        gk = g_small[k].reshape(two_d)
        dl, nm, nv = adam_small("adam_" + k, gk, wv.reshape(two_d), mv.reshape(two_d), vv.reshape(two_d))
        out_g[k], out_d[k], out_m[k], out_v[k] = gk.reshape(shape), dl.reshape(shape), nm.reshape(shape), nv.reshape(shape)

    order = ["rel_bias_table", "attn_norm", "w_in", "sink_a", "w_branch_a", "w_branch_b", "w_out", "ffn_norm",
             "w_ffn_gate", "w_ffn_up", "conv_w", "conv_b", "w_ffn_down", "ple_norm", "w_ple_gate", "w_ple_proj",
             "final_norm"]
    return (loss, grad_x[None], *[out_g[k] for k in order], *[out_d[k] for k in order],
            *[out_m[k] for k in order], *[out_v[k] for k in order])
```

```python
import math

import jax
import jax.numpy as jnp
from jax import lax
from jax.experimental import pallas as pl
from jax.experimental.pallas import tpu as pltpu

F32 = jnp.float32
BF16 = jnp.bfloat16
MESH = pl.DeviceIdType.MESH
N_DEV = 8

HEAD_DIM = 128
A_Q_HEADS = 8
A_KV_HEADS = 2
A_GROUP = A_Q_HEADS // A_KV_HEADS
A_BLOCK = 128
B_PATTERNS = ((128, 1), (512, 4), (2048, 16))
B_HEADS_PER_GROUP = 4
B_HEADS = len(B_PATTERNS) * B_HEADS_PER_GROUP
B_BLOCK = 64
N_BUCKETS = 32
MAX_DISTANCE = 1024
A_Q_W = A_Q_HEADS * HEAD_DIM
A_KV_W = A_KV_HEADS * HEAD_DIM
B_W = B_HEADS * HEAD_DIM
B_OUT_W = B_HEADS_PER_GROUP * HEAD_DIM
COL_QA = 0
COL_KA = COL_QA + A_Q_W
COL_VA = COL_KA + A_KV_W
COL_QB = COL_VA + A_KV_W
COL_KB = COL_QB + B_W
COL_VB = COL_KB + B_W
COL_GATES = COL_VB + B_W
RMS_EPS = 1e-6
NEG_INF = -1e30
ATTN_SCALE = HEAD_DIM ** -0.5
ATTN_Q_ROWS = 256
ATTN_CHAINS = 4

ADAM_LR = 0.001
ADAM_B1 = 0.9
ADAM_B2 = 0.999
ADAM_EPS = 1e-08
ADAM_WD = 0.01
ADAM_STEP = 10

GELU_C = math.sqrt(2.0 / math.pi)
GELU_A = 0.044715

V7X_VMEM_BYTES = 64 * 1024 * 1024
VMEM_CEILING = V7X_VMEM_BYTES - 8 * 1024 * 1024
LANES = 128
SUBLANES = 8


def _pick(n, cands):
    for c in cands:
        if n % c == 0:
            return c
    return n


def _nbytes(shape, dtype):
    n = 1
    for d in shape:
        if d is not None:
            n *= d
    return n * jnp.dtype(dtype).itemsize


def _params(sem, est_bytes):
    limit = int(min(VMEM_CEILING, max(32 * 1024 * 1024, 2 * est_bytes + (8 << 20))))
    return pltpu.CompilerParams(dimension_semantics=sem, vmem_limit_bytes=limit)


def _mm(name, a, b, a_bs, a_im, b_bs, b_im, out_shape, out_dtype, o_bs, o_im, grid, dims,
        res=None, r_bs=None, r_im=None, after=None):
    nk = grid[-1]
    nax = len(grid)
    has_res = res is not None
    has_after = after is not None
    o_tile = tuple(d for d in o_bs if d is not None)

    def body(*refs):
        a_ref, b_ref = refs[:2]
        r_ref = refs[2] if has_res else None
        n_in = 2 + has_res + has_after
        o_ref = refs[n_in]
        rest = refs[n_in + 1:]

        def prod():
            return lax.dot_general(a_ref[...].astype(BF16), b_ref[...].astype(BF16), (dims, ((), ())),
                                   preferred_element_type=F32)

        def finish(r):
            if r_ref is not None:
                r = r + r_ref[...].astype(F32)
            o_ref[...] = r.astype(o_ref.dtype)

        if nk == 1:
            finish(prod())
        else:
            acc = rest[0]
            k = pl.program_id(nax - 1)

            @pl.when(k == 0)
            def _():
                acc[...] = prod()

            @pl.when(k > 0)
            def _():
                acc[...] += prod()

            @pl.when(k == nk - 1)
            def _():
                finish(acc[...])

    in_specs = [pl.BlockSpec(a_bs, a_im), pl.BlockSpec(b_bs, b_im)]
    args = [a, b]
    est = _nbytes(a_bs, a.dtype) + _nbytes(b_bs, b.dtype) + _nbytes(o_bs, out_dtype) + 2 * _nbytes(o_tile, F32)
    if has_res:
        in_specs.append(pl.BlockSpec(r_bs, r_im))
        args.append(res)
        est += _nbytes(r_bs, res.dtype)
    if has_after:
        in_specs.append(pl.BlockSpec(memory_space=pl.ANY))
        args.append(after)
    scratch = [] if nk == 1 else [pltpu.VMEM(o_tile, F32)]
    sem = ("parallel",) * (nax - 1) + ("arbitrary",)
    return pl.pallas_call(
        body, name=name, grid=grid, in_specs=in_specs, out_specs=pl.BlockSpec(o_bs, o_im),
        out_shape=pltpu.HBM(out_shape, out_dtype), scratch_shapes=scratch,
        compiler_params=_params(sem, est))(*args)


TM_CANDS = (1024, 512, 256, 128, 64, 32, 16, 8)
TM_WIDE_CANDS = (2048,) + TM_CANDS
MM_WHOLE_K_BYTES = 36 * 1024 * 1024
TK_CANDS = (1024, 512, 256, 128)
TN_CANDS = (1024, 512, 256, 128)


def mm_cols(name, a, wg, out_dtype, fold, after=None):
    m, k = a.shape
    nj, _, n = wg.shape
    tm, tk = _pick(m, TM_WIDE_CANDS), _pick(k, TK_CANDS)
    whole_k = 2 * (_nbytes((tm, k), a.dtype) + _nbytes((k, n), wg.dtype) + _nbytes((tm, n), out_dtype))
    if whole_k <= MM_WHOLE_K_BYTES:
        tk = k
    grid = (nj, m // tm, k // tk)
    if fold:
        shape, o_bs, o_im = (m, nj * n), (tm, n), (lambda j, i, kk: (i, j))
    else:
        shape, o_bs, o_im = (nj, m, n), (None, tm, n), (lambda j, i, kk: (j, i, 0))
    return _mm(name, a, wg, (tm, tk), lambda j, i, kk: (i, kk), (None, tk, n), lambda j, i, kk: (j, kk, 0),
               shape, out_dtype, o_bs, o_im, grid, ((1,), (0,)), after=after)


def mm_plain(name, a, w, out_dtype, res=None):
    m, k = a.shape
    n = w.shape[1]
    tm, tk, tn = _pick(m, TM_CANDS if res is not None else TM_WIDE_CANDS), _pick(k, TK_CANDS), _pick(n, TN_CANDS)
    grid = (n // tn, m // tm, k // tk)
    return _mm(name, a, w, (tm, tk), lambda j, i, kk: (i, kk), (tk, tn), lambda j, i, kk: (kk, j),
               (m, n), out_dtype, (tm, tn), lambda j, i, kk: (i, j), grid, ((1,), (0,)),
               res, (tm, tn), lambda j, i, kk: (i, j))


def mm_jsum(name, aj, wg, out_dtype, res=None, after=None):
    nj, m, ka = aj.shape
    n = wg.shape[2]
    tm, tn = _pick(m, TM_WIDE_CANDS), _pick(n, TN_CANDS if res is None else TN_CANDS[1:])
    grid = (m // tm, n // tn, nj)
    return _mm(name, aj, wg, (None, tm, ka), lambda i, jn, j: (j, i, 0), (None, ka, tn), lambda i, jn, j: (j, 0, jn),
               (m, n), out_dtype, (tm, tn), lambda i, jn, j: (i, jn), grid, ((1,), (0,)),
               res, (tm, tn), lambda i, jn, j: (i, jn), after=after)


def mm_tn_cols(name, a, g, nj, n, out_dtype, folded, after=None):
    s, kw = a.shape
    ts, tkw = _pick(s, TK_CANDS), _pick(kw, TM_WIDE_CANDS)
    grid = (nj, kw // tkw, s // ts)
    if folded:
        g_bs, g_im = (ts, n), (lambda j, i, ss: (ss, j))
    else:
        g_bs, g_im = (None, ts, n), (lambda j, i, ss: (j, ss, 0))
    return _mm(name, a, g, (ts, tkw), lambda j, i, ss: (ss, i), g_bs, g_im,
               (nj, kw, n), out_dtype, (None, tkw, n), lambda j, i, ss: (j, i, 0), grid, ((0,), (0,)), after=after)


def mm_tn_plain(name, a, g, out_dtype):
    s, kw = a.shape
    n = g.shape[1]
    ts, tkw, tn = _pick(s, TK_CANDS), _pick(kw, TM_WIDE_CANDS), _pick(n, TN_CANDS)
    grid = (kw // tkw, n // tn, s // ts)
    return _mm(name, a, g, (ts, tkw), lambda i, jn, ss: (ss, i), (ts, tn), lambda i, jn, ss: (ss, jn),
               (kw, n), out_dtype, (tkw, tn), lambda i, jn, ss: (i, jn), grid, ((0,), (0,)))


def mm_tn_j(name, aj, g, out_dtype):
    nj, s, ka = aj.shape
    n = g.shape[1]
    ts, tn = _pick(s, TK_CANDS), _pick(n, TM_WIDE_CANDS)
    grid = (nj, n // tn, s // ts)
    return _mm(name, aj, g, (None, ts, ka), lambda j, jn, ss: (j, ss, 0), (ts, tn), lambda j, jn, ss: (ss, jn),
               (nj, ka, n), out_dtype, (None, ka, tn), lambda j, jn, ss: (j, 0, jn), grid, ((0,), (0,)))


def mm_nt_plain(name, g, w, out_dtype):
    m, n = g.shape
    k = w.shape[0]
    tm, tn, tkk = _pick(m, TM_WIDE_CANDS), _pick(n, TK_CANDS), _pick(k, TN_CANDS)
    grid = (k // tkk, m // tm, n // tn)
    return _mm(name, g, w, (tm, tn), lambda kk, i, jn: (i, jn), (tkk, tn), lambda kk, i, jn: (kk, jn),
               (m, k), out_dtype, (tm, tkk), lambda kk, i, jn: (i, kk), grid, ((1,), (1,)))


def mm_nt_j(name, g, wg, out_dtype):
    m, n = g.shape
    nj, ka, _ = wg.shape
    tm, tn = _pick(m, TM_WIDE_CANDS), _pick(n, TK_CANDS)
    grid = (nj, m // tm, n // tn)
    return _mm(name, g, wg, (tm, tn), lambda j, i, jn: (i, jn), (None, ka, tn), lambda j, i, jn: (j, 0, jn),
               (nj, m, ka), out_dtype, (None, tm, ka), lambda j, i, jn: (j, i, 0), grid, ((1,), (1,)))


def mm_nt_jsum(name, g, wg, out_dtype, folded, res=None, after=None):
    nj, k, n = wg.shape
    m = g.shape[0] if folded else g.shape[1]
    tm, tkk = _pick(m, TM_WIDE_CANDS), _pick(k, TN_CANDS if res is None else TN_CANDS[1:])
    grid = (m // tm, k // tkk, nj)
    if folded:
        g_bs, g_im = (tm, n), (lambda i, kk, j: (i, j))
    else:
        g_bs, g_im = (None, tm, n), (lambda i, kk, j: (j, i, 0))
    return _mm(name, g, wg, g_bs, g_im, (None, tkk, n), lambda i, kk, j: (j, kk, 0),
               (m, k), out_dtype, (tm, tkk), lambda i, kk, j: (i, kk), grid, ((1,), (1,)),
               res, (tm, tkk), lambda i, kk, j: (i, kk), after=after)


ROW_TILE_CANDS = (256, 128, 64, 32, 16, 8)


def _rstd(x):
    return lax.rsqrt(jnp.mean(x * x, axis=-1, keepdims=True) + RMS_EPS)


def _sigmoid(t):
    return 1.0 / (1.0 + jnp.exp(-t))


def rms_fwd(name, x, gain):
    s, d = x.shape
    ts = _pick(s, ROW_TILE_CANDS)

    def body(x_ref, g_ref, h_ref):
        xv = x_ref[...]
        h_ref[...] = ((xv * _rstd(xv)) * g_ref[...]).astype(h_ref.dtype)

    return pl.pallas_call(
        body, name=name, grid=(s // ts,),
        in_specs=[pl.BlockSpec((ts, d), lambda i: (i, 0)), pl.BlockSpec((1, d), lambda i: (0, 0))],
        out_specs=pl.BlockSpec((ts, d), lambda i: (i, 0)),
        out_shape=pltpu.HBM((s, d), BF16),
        compiler_params=_params(("parallel",), 3 * ts * d * 4))(x, gain)


def rms_bwd(name, x, gain, dh, dres, bf16_copy):
    s, d = x.shape
    ts = _pick(s, ROW_TILE_CANDS)

    def body(x_ref, g_ref, dh_ref, dr_ref, dx_ref, *rest):
        dxb_ref, dg_ref = rest if bf16_copy else (None, rest[0])
        xv = x_ref[...]
        r = _rstd(xv)
        xhat = xv * r
        dhv = dh_ref[...].astype(F32)
        dxhat = dhv * g_ref[...]
        dx = dr_ref[...] + r * (dxhat - xhat * jnp.mean(dxhat * xhat, axis=-1, keepdims=True))
        dx_ref[...] = dx
        if bf16_copy:
            dxb_ref[...] = dx.astype(dxb_ref.dtype)
        part = jnp.sum(dhv * xhat, axis=0, keepdims=True)

        @pl.when(pl.program_id(0) == 0)
        def _():
            dg_ref[...] = part

        @pl.when(pl.program_id(0) > 0)
        def _():
            dg_ref[...] += part

    row = pl.BlockSpec((ts, d), lambda i: (i, 0))
    vec = pl.BlockSpec((1, d), lambda i: (0, 0))
    copy_spec, copy_shape = ([row], [pltpu.HBM((s, d), BF16)]) if bf16_copy else ([], [])
    res = pl.pallas_call(
        body, name=name, grid=(s // ts,), in_specs=[row, vec, row, row], out_specs=[row] + copy_spec + [vec],
        out_shape=[pltpu.HBM((s, d), F32)] + copy_shape + [jax.ShapeDtypeStruct((1, d), F32)],
        compiler_params=_params(("arbitrary",), 7 * ts * d * 4))(x, gain, dh, dres)
    return (res[0], res[1], res[2]) if bf16_copy else (res[0], None, res[1])


def gate_merge_fwd(name, proj, ta, tb, d):
    s = proj.shape[0]
    ts = _pick(s, ROW_TILE_CANDS)
    cb = COL_GATES // d

    def body(ga_ref, gb_ref, ta_ref, tb_ref, o_ref):
        o_ref[...] = (_sigmoid(ga_ref[...]) * ta_ref[...] + _sigmoid(gb_ref[...]) * tb_ref[...]).astype(o_ref.dtype)

    row = pl.BlockSpec((ts, d), lambda i: (i, 0))
    return pl.pallas_call(
        body, name=name, grid=(s // ts,),
        in_specs=[pl.BlockSpec((ts, d), lambda i: (i, cb)), pl.BlockSpec((ts, d), lambda i: (i, cb + 1)), row, row],
        out_specs=row, out_shape=pltpu.HBM((s, d), BF16),
        compiler_params=_params(("parallel",), 5 * ts * d * 4))(proj, proj, ta, tb)


def gate_merge_bwd(name, dmerged, proj, ta, tb, d):
    s = proj.shape[0]
    ts = _pick(s, ROW_TILE_CANDS)
    cb = COL_GATES // d

    def body(dm_ref, ga_ref, gb_ref, ta_ref, tb_ref, dta_ref, dtb_ref, dga_ref, dgb_ref):
        dm = dm_ref[...]
        sa = _sigmoid(ga_ref[...])
        sb = _sigmoid(gb_ref[...])
        dta_ref[...] = (dm * sa).astype(dta_ref.dtype)
        dtb_ref[...] = (dm * sb).astype(dtb_ref.dtype)
        dga_ref[...] = (dm * ta_ref[...] * (sa * (1.0 - sa))).astype(dga_ref.dtype)
        dgb_ref[...] = (dm * tb_ref[...] * (sb * (1.0 - sb))).astype(dgb_ref.dtype)

    row = pl.BlockSpec((ts, d), lambda i: (i, 0))
    out = pltpu.HBM((s, d), BF16)
    return pl.pallas_call(
        body, name=name, grid=(s // ts,),
        in_specs=[row, pl.BlockSpec((ts, d), lambda i: (i, cb)), pl.BlockSpec((ts, d), lambda i: (i, cb + 1)), row, row],
        out_specs=[row, row, row, row], out_shape=[out, out, out, out],
        compiler_params=_params(("parallel",), 8 * ts * d * 4))(dmerged, proj, proj, ta, tb)


def tail_fwd_bwd(name, x2, lp, pp, gain, target):
    s, d = x2.shape
    ts = _pick(s, ROW_TILE_CANDS)

    def body(x2_ref, lp_ref, pp_ref, g_ref, t_ref, loss_ref, dx3_ref, dlp_ref, dpp_ref, dg_ref):
        gp = _sigmoid(lp_ref[...])
        ppv = pp_ref[...]
        x3 = x2_ref[...] + gp * ppv
        r = _rstd(x3)
        xhat = x3 * r
        gv = g_ref[...]
        err = xhat * gv - t_ref[...]
        loss = jnp.sum(err * err) * (0.5 / d)
        dy = err * (1.0 / d)
        dxhat = dy * gv
        dx3 = r * (dxhat - xhat * jnp.mean(dxhat * xhat, axis=-1, keepdims=True))
        dx3_ref[...] = dx3
        dlp_ref[...] = (dx3 * ppv * (gp * (1.0 - gp))).astype(dlp_ref.dtype)
        dpp_ref[...] = (dx3 * gp).astype(dpp_ref.dtype)
        part = jnp.sum(dy * xhat, axis=0, keepdims=True)
        lossv = jnp.full((1, LANES), loss, F32)

        @pl.when(pl.program_id(0) == 0)
        def _():
            dg_ref[...] = part
            loss_ref[...] = lossv

        @pl.when(pl.program_id(0) > 0)
        def _():
            dg_ref[...] += part
            loss_ref[...] += lossv

    row = pl.BlockSpec((ts, d), lambda i: (i, 0))
    vec = pl.BlockSpec((1, d), lambda i: (0, 0))
    return pl.pallas_call(
        body, name=name, grid=(s // ts,), in_specs=[row, row, row, vec, row],
        out_specs=[pl.BlockSpec((1, LANES), lambda i: (0, 0)), row, row, row, vec],
        out_shape=[jax.ShapeDtypeStruct((1, LANES), F32), pltpu.HBM((s, d), F32),
                   pltpu.HBM((s, d), BF16), pltpu.HBM((s, d), BF16),
                   jax.ShapeDtypeStruct((1, d), F32)],
        compiler_params=_params(("arbitrary",), 9 * ts * d * 4))(x2, lp, pp, gain, target)


HALO = SUBLANES
BF16_ROWS = 2 * SUBLANES


def _shift_rows(cur, prev_row, next_row):
    ts = cur.shape[0]
    rid = lax.broadcasted_iota(jnp.int32, cur.shape, 0)
    down = jnp.where(rid == 0, prev_row, pltpu.roll(cur, 1, 0))
    up = jnp.where(rid == ts - 1, next_row, pltpu.roll(cur, ts - 1, 0))
    return down, up


def _halo_specs(ts, s, nf, halo=HALO):
    nb = ts // halo
    last = s // halo - 1
    cur = pl.BlockSpec((None, ts, nf), lambda j, i: (j, i, 0))
    prev = pl.BlockSpec((None, halo, nf), lambda j, i: (j, jnp.maximum(i * nb - 1, 0), 0))
    nxt = pl.BlockSpec((None, halo, nf), lambda j, i: (j, jnp.minimum((i + 1) * nb, last), 0))
    return cur, prev, nxt


def _halo_rows(prev_ref, next_ref, n_tiles):
    i = pl.program_id(1)
    prev_row = jnp.where(i == 0, 0.0, prev_ref[HALO - 1:HALO, :].astype(F32))
    next_row = jnp.where(i == n_tiles - 1, 0.0, next_ref[0:1, :].astype(F32))
    return prev_row, next_row


def _gelu(g):
    t = jnp.tanh(GELU_C * (g + GELU_A * (g * g * g)))
    return 0.5 * g * (1.0 + t), t


def _conv(cur, down, up, cw_ref, cb_ref):
    return down * cw_ref[0:1, :] + cur * cw_ref[1:2, :] + up * cw_ref[2:3, :] + cb_ref[...]


def ffn_mid_fwd(name, gpre, u, cw, cb):
    nj, s, nf = gpre.shape
    ts = _pick(s, (512, 256, 128, 64, 32, 16, 8))
    n_tiles = s // ts
    cur, prev, nxt = _halo_specs(ts, s, nf)

    def body(g_ref, gp_ref, gn_ref, u_ref, cw_ref, cb_ref, z_ref):
        gv = g_ref[...]
        down, up = _shift_rows(gv, *_halo_rows(gp_ref, gn_ref, n_tiles))
        act, _ = _gelu(_conv(gv, down, up, cw_ref, cb_ref))
        z_ref[...] = (act * u_ref[...]).astype(z_ref.dtype)

    return pl.pallas_call(
        body, name=name, grid=(nj, n_tiles),
        in_specs=[cur, prev, nxt, cur, pl.BlockSpec((None, SUBLANES, nf), lambda j, i: (j, 0, 0)),
                  pl.BlockSpec((None, 1, nf), lambda j, i: (j, 0, 0))],
        out_specs=cur, out_shape=pltpu.HBM((nj, s, nf), BF16),
        compiler_params=_params(("parallel", "parallel"), 8 * ts * nf * 4))(gpre, gpre, gpre, u, cw, cb)


def _gelu_grad(g, t):
    return 0.5 * (1.0 + t) + 0.5 * g * (1.0 - t * t) * (GELU_C * (1.0 + 3.0 * GELU_A * (g * g)))


def ffn_mid_bwd(name, gpre, u, dz, cw, cb):
    nj, s, nf = gpre.shape
    ts = _pick(s, (512, 256, 128, 64, 32, 16, 8))
    n_tiles = s // ts
    cur, prev, nxt = _halo_specs(ts, s, nf)

    def body(g_ref, gp_ref, gn_ref, u_ref, up_ref, un_ref, dz_ref, dzp_ref, dzn_ref, cw_ref, cb_ref,
             du_ref, dgp_ref, dcw_ref):
        i = pl.program_id(1)
        w0, w1, w2, bias = cw_ref[0:1, :], cw_ref[1:2, :], cw_ref[2:3, :], cb_ref[...]
        gv = g_ref[...]
        down, up = _shift_rows(gv, *_halo_rows(gp_ref, gn_ref, n_tiles))
        gc = down * w0 + gv * w1 + up * w2 + bias
        act, t = _gelu(gc)
        dzv = dz_ref[...].astype(F32)
        du_ref[...] = (dzv * act).astype(du_ref.dtype)
        dg = dzv * u_ref[...] * _gelu_grad(gc, t)

        def edge_dg(g_before, g_at, g_after, u_at, dz_at):
            ge = g_before.astype(F32) * w0 + g_at.astype(F32) * w1 + g_after.astype(F32) * w2 + bias
            return dz_at.astype(F32) * u_at.astype(F32) * _gelu_grad(ge, _gelu(ge)[1])

        dz_before = dzp_ref[...].astype(F32)[BF16_ROWS - 1:BF16_ROWS, :]
        dz_after = dzn_ref[...].astype(F32)[0:1, :]
        dg_prev = jnp.where(i == 0, 0.0, edge_dg(gp_ref[HALO - 2:HALO - 1, :], gp_ref[HALO - 1:HALO, :], gv[0:1, :],
                                                 up_ref[HALO - 1:HALO, :], dz_before))
        dg_next = jnp.where(i == n_tiles - 1, 0.0, edge_dg(gv[ts - 1:ts, :], gn_ref[0:1, :], gn_ref[1:2, :],
                                                           un_ref[0:1, :], dz_after))
        dg_down, dg_up = _shift_rows(dg, dg_prev, dg_next)
        dgp_ref[...] = (dg_up * w0 + dg * w1 + dg_down * w2).astype(dgp_ref.dtype)
        rows = [jnp.sum(dg * down, axis=0, keepdims=True), jnp.sum(dg * gv, axis=0, keepdims=True),
                jnp.sum(dg * up, axis=0, keepdims=True), jnp.sum(dg, axis=0, keepdims=True)]
        part = jnp.concatenate(rows + [jnp.zeros((SUBLANES - len(rows), nf), F32)], axis=0)

        @pl.when(i == 0)
        def _():
            dcw_ref[...] = part

        @pl.when(i > 0)
        def _():
            dcw_ref[...] += part

    small = pl.BlockSpec((None, SUBLANES, nf), lambda j, i: (j, 0, 0))
    return pl.pallas_call(
        body, name=name, grid=(nj, n_tiles),
        in_specs=[cur, prev, nxt] * 2 + list(_halo_specs(ts, s, nf, BF16_ROWS))
        + [small, pl.BlockSpec((None, 1, nf), lambda j, i: (j, 0, 0))],
        out_specs=[cur, cur, small],
        out_shape=[pltpu.HBM((nj, s, nf), BF16), pltpu.HBM((nj, s, nf), BF16),
                   jax.ShapeDtypeStruct((nj, SUBLANES, nf), F32)],
        compiler_params=_params(("parallel", "arbitrary"), 14 * ts * nf * 4))(
            gpre, gpre, gpre, u, u, u, dz, dz, dz, cw, cb)


def _t5_bucket(rel):
    half = N_BUCKETS // 2
    max_exact = half // 2
    n = jnp.abs(rel)
    side = jnp.where(rel > 0, half, 0)
    nf = jnp.maximum(n, 1).astype(F32)
    large = max_exact + (jnp.log(nf / max_exact) / math.log(MAX_DISTANCE / max_exact)
                         * (half - max_exact)).astype(jnp.int32)
    large = jnp.minimum(large, half - 1)
    return side + jnp.where(n < max_exact, n, large)


def bucket_tile(rows, half, dil):
    rel = (jnp.arange(rows + 2 * half)[None, :] - half) - jnp.arange(rows)[:, None]
    return _t5_bucket(rel * dil).astype(jnp.int32)


def bias_build(name, table_t, bucket, h0, nh, half):
    blk, kw = bucket.shape

    def body(t_ref, b_ref, o_ref):
        h = pl.program_id(0)
        bv = b_ref[...]
        acc = jnp.zeros((blk, kw), F32)
        for b in range(N_BUCKETS):
            acc = jnp.where(bv == b, t_ref[h0 + h, b], acc)
        qi = lax.broadcasted_iota(jnp.int32, (blk, kw), 0)
        ci = lax.broadcasted_iota(jnp.int32, (blk, kw), 1)
        o_ref[...] = jnp.where(jnp.abs(ci - half - qi) <= half, acc, NEG_INF)

    return pl.pallas_call(
        body, name=name, grid=(nh,),
        in_specs=[pl.BlockSpec(memory_space=pltpu.SMEM), pl.BlockSpec((blk, kw), lambda h: (0, 0))],
        out_specs=pl.BlockSpec((None, blk, kw), lambda h: (h, 0, 0)),
        out_shape=jax.ShapeDtypeStruct((nh, blk, kw), F32),
        compiler_params=_params(("parallel",), 4 * blk * kw * 4))(table_t, bucket)


def table_grad(name, dbias, bucket):
    nh, blk, kw = dbias.shape

    def body(d_ref, b_ref, o_ref):
        bv = b_ref[...]
        dv = d_ref[...]
        lane = lax.broadcasted_iota(jnp.int32, (SUBLANES, LANES), 1)
        acc = jnp.zeros((SUBLANES, LANES), F32)
        for b in range(N_BUCKETS):
            acc = jnp.where(lane == b, jnp.sum(jnp.where(bv == b, dv, 0.0)), acc)
        o_ref[...] = acc

    return pl.pallas_call(
        body, name=name, grid=(nh,),
        in_specs=[pl.BlockSpec((None, blk, kw), lambda h: (h, 0, 0)), pl.BlockSpec((blk, kw), lambda h: (0, 0))],
        out_specs=pl.BlockSpec((None, SUBLANES, LANES), lambda h: (h, 0, 0)),
        out_shape=jax.ShapeDtypeStruct((nh, SUBLANES, LANES), F32),
        compiler_params=_params(("parallel",), 4 * blk * kw * 4))(dbias, bucket)


class _Band:
    def __init__(self, s, half, q_rows, n_chains, dil):
        self.s, self.half, self.dil, self.n_chains = s, half, dil, n_chains
        self.seg = s // dil
        self.q_rows = min(q_rows, self.seg)
        self.win = self.q_rows + 2 * half
        self.pad = self.seg + 2 * half
        self.nsb = self.seg // self.q_rows
        self.n_items = dil * self.nsb
        assert self.n_items % n_chains == 0 and self.seg % self.q_rows == 0
        self.staged = dil > 1

    def rows_of(self, r):
        return pl.ds(r, self.seg, stride=self.dil) if self.dil > 1 else slice(None)

    def stage_kv(self, dst, src_ref):
        zeros = jnp.zeros((self.half, HEAD_DIM), dst.dtype)
        for r in range(self.dil):
            base = r * self.pad
            dst[base:base + self.half, :] = zeros
            dst[base + self.half + self.seg:base + self.pad, :] = zeros
            dst[base + self.half:base + self.half + self.seg, :] = src_ref[self.rows_of(r), :].astype(dst.dtype)

    def stage(self, dst, src_ref):
        for r in range(self.dil):
            dst[r * self.seg:(r + 1) * self.seg, :] = src_ref[self.rows_of(r), :].astype(dst.dtype)

    def unstage(self, dst_ref, src, add=False):
        for r in range(self.dil):
            val = src[r * self.seg:(r + 1) * self.seg, :].astype(dst_ref.dtype)
            if add:
                val = val + dst_ref[self.rows_of(r), :]
            dst_ref[self.rows_of(r), :] = val

    def offsets(self, item):
        r, sb = item // self.nsb, item % self.nsb
        qoff = pl.multiple_of(r * self.seg + sb * self.q_rows, self.q_rows)
        koff = pl.multiple_of(r * self.pad + sb * self.q_rows, B_BLOCK)
        kpos = sb * self.q_rows - self.half + lax.broadcasted_iota(jnp.int32, (1, self.win), 1)
        edge = jnp.where((kpos >= 0) & (kpos < self.seg), 0.0, NEG_INF)
        return qoff, koff, edge


def band_attn_fwd(name, proj, bias, sink, *, half, q_rows, n_chains, dil, nh, group, cq, ck, cv):
    s, w = proj.shape
    g = _Band(s, half, q_rows, n_chains, dil)
    has_sink = sink is not None

    def body(*refs):
        q_ref, k_ref, v_ref, b_ref = refs[:4]
        s_ref = refs[4] if has_sink else None
        o_ref, l_ref, ks, vs = refs[4 + has_sink:8 + has_sink]
        qs, os_, ls = refs[8 + has_sink:] if g.staged else (None, o_ref, l_ref)
        g.stage_kv(ks, k_ref)
        g.stage_kv(vs, v_ref)
        if g.staged:
            g.stage(qs, q_ref)
        bias_v = b_ref[...]
        sk = s_ref[pl.program_id(0)] if has_sink else None

        def chain(item):
            qoff, koff, edge = g.offsets(item)
            rows = pl.ds(qoff, g.q_rows)
            qv = qs[rows, :] if g.staged else q_ref[rows, :].astype(BF16)
            kw_ = ks[pl.ds(koff, g.win), :]
            vw_ = vs[pl.ds(koff, g.win), :]
            sc = lax.dot_general(qv, kw_, (((1,), (1,)), ((), ())), preferred_element_type=F32) * ATTN_SCALE
            sc = sc + bias_v + edge
            m = jnp.max(sc, axis=-1, keepdims=True)
            if has_sink:
                m = jnp.maximum(m, sk)
            p = jnp.exp(sc - m)
            den = jnp.sum(p, axis=-1, keepdims=True)
            if has_sink:
                den = den + jnp.exp(sk - m)
            out = lax.dot_general(p.astype(BF16), vw_, (((1,), (0,)), ((), ())), preferred_element_type=F32)
            return rows, out / den, jnp.broadcast_to(m + jnp.log(den), (g.q_rows, HEAD_DIM))

        def step(i, carry):
            for rows, out, lse in [chain(i * n_chains + u) for u in range(n_chains)]:
                os_[rows, :] = out
                ls[rows, :] = lse
            return carry

        lax.fori_loop(0, g.n_items // n_chains, step, 0)
        if g.staged:
            g.unstage(o_ref, os_)
            g.unstage(l_ref, ls)

    def col(c0, per):
        return pl.BlockSpec((s, HEAD_DIM), lambda h: (0, c0 // LANES + h // per))

    in_specs = [col(cq, 1), col(ck, group), col(cv, group),
                pl.BlockSpec((None, g.q_rows, g.win), lambda h: (h, 0, 0))]
    args = [proj, proj, proj, bias]
    if has_sink:
        in_specs.append(pl.BlockSpec(memory_space=pltpu.SMEM))
        args.append(sink)
    shape = pltpu.HBM((s, nh * HEAD_DIM), F32)
    scratch = [pltpu.VMEM((dil * g.pad, HEAD_DIM), BF16), pltpu.VMEM((dil * g.pad, HEAD_DIM), BF16)]
    if g.staged:
        scratch += [pltpu.VMEM((s, HEAD_DIM), BF16), pltpu.VMEM((s, HEAD_DIM), F32), pltpu.VMEM((s, HEAD_DIM), F32)]
    return pl.pallas_call(
        body, name=name, grid=(nh,), in_specs=in_specs, out_specs=[col(0, 1), col(0, 1)], out_shape=[shape, shape],
        scratch_shapes=scratch, compiler_params=_params(("parallel",), 16 * s * HEAD_DIM * 4))(*args)


def band_attn_bwd(name, proj, bias, sink, dout, out, lse, dlse, *, half, q_rows, n_chains, dil, nh, group, cq, ck, cv):
    s, w = proj.shape
    g = _Band(s, half, q_rows, n_chains, dil)
    nkv = nh // group
    has_sink = sink is not None
    has_dl = dlse is not None
    n_in = 7 + int(has_sink) + int(has_dl)
    n_out = 4 + int(has_sink)

    def body(*refs):
        ins, outs, scr = refs[:n_in], refs[n_in:n_in + n_out], refs[n_in + n_out:]
        q_ref, k_ref, v_ref, b_ref, do_ref, o_ref, l_ref = ins[:7]
        s_ref = ins[7] if has_sink else None
        dl_ref = ins[n_in - 1] if has_dl else None
        dq_ref, dk_ref, dv_ref, db_ref = outs[:4]
        ks, vs, dks, dvs = scr[:4]
        scr = list(scr[4:])
        dsa = scr.pop(0) if has_sink else None
        if g.staged:
            qs, dos, os_, ls, dqs = scr[:5]
            dls = scr[5] if has_dl else None
            g.stage(qs, q_ref)
            g.stage(dos, do_ref)
            g.stage(os_, o_ref)
            g.stage(ls, l_ref)
            if has_dl:
                g.stage(dls, dl_ref)
        else:
            qs, dos, os_, ls, dqs, dls = None, do_ref, o_ref, l_ref, dq_ref, dl_ref
        h = pl.program_id(0)
        g.stage_kv(ks, k_ref)
        g.stage_kv(vs, v_ref)
        dks[...] = jnp.zeros_like(dks)
        dvs[...] = jnp.zeros_like(dvs)
        db_ref[...] = jnp.zeros_like(db_ref)
        bias_v = b_ref[...]
        if has_sink:
            sk = s_ref[h]
            dsa[...] = jnp.zeros_like(dsa)

        def chain(item):
            qoff, koff, edge = g.offsets(item)
            rows = pl.ds(qoff, g.q_rows)
            win = pl.ds(koff, g.win)
            qv = qs[rows, :] if g.staged else q_ref[rows, :].astype(BF16)
            kw_ = ks[win, :]
            vw_ = vs[win, :]
            sc = lax.dot_general(qv, kw_, (((1,), (1,)), ((), ())), preferred_element_type=F32) * ATTN_SCALE
            lv = ls[rows, :][:, 0:1]
            p = jnp.exp(sc + bias_v + edge - lv)
            dov = dos[rows, :]
            delta = jnp.sum(dov * os_[rows, :], axis=-1, keepdims=True)
            dob = dov.astype(BF16)
            dp = lax.dot_general(dob, vw_, (((1,), (1,)), ((), ())), preferred_element_type=F32)
            t = dp - delta
            if has_dl:
                t = t + dls[rows, :][:, 0:1]
            ds = p * t
            dsb = (ds * ATTN_SCALE).astype(BF16)
            dq = lax.dot_general(dsb, kw_, (((1,), (0,)), ((), ())), preferred_element_type=F32)
            dkc = lax.dot_general(dsb, qv, (((0,), (0,)), ((), ())), preferred_element_type=F32)
            dvc = lax.dot_general(p.astype(BF16), dob, (((0,), (0,)), ((), ())), preferred_element_type=F32)
            dsk = jnp.exp(sk - lv) * delta if has_sink else None
            return rows, win, dq, dkc, dvc, ds, dsk

        def step(i, carry):
            res = [chain(i * n_chains + u) for u in range(n_chains)]
            ds_sum = res[0][5]
            for rr in res[1:]:
                ds_sum = ds_sum + rr[5]
            db_ref[...] += ds_sum
            for rows, win, dq, dkc, dvc, ds, dsk in res:
                dqs[rows, :] = dq
                dks[win, :] += dkc
                dvs[win, :] += dvc
                if has_sink:
                    dsa[...] += dsk
            return carry

        lax.fori_loop(0, g.n_items // n_chains, step, 0)

        if g.staged:
            g.unstage(dq_ref, dqs)

        def emit_kv(add):
            for r in range(dil):
                lo = r * g.pad + half
                for dst_ref, src in ((dk_ref, dks), (dv_ref, dvs)):
                    val = src[lo:lo + g.seg, :]
                    if add:
                        val = val + dst_ref[g.rows_of(r), :]
                    dst_ref[g.rows_of(r), :] = val

        if group == 1:
            emit_kv(False)
        else:
            @pl.when(h % group == 0)
            def _():
                emit_kv(False)

            @pl.when(h % group != 0)
            def _():
                emit_kv(True)
        if has_sink:
            outs[4][...] = jnp.full((SUBLANES, LANES), -jnp.sum(dsa[...]), F32)

    def col(c0, per):
        return pl.BlockSpec((s, HEAD_DIM), lambda h: (0, c0 // LANES + h // per))

    b_spec = pl.BlockSpec((None, g.q_rows, g.win), lambda h: (h, 0, 0))
    in_specs = [col(cq, 1), col(ck, group), col(cv, group), b_spec, col(0, 1), col(0, 1), col(0, 1)]
    args = [proj, proj, proj, bias, dout, out, lse]
    if has_sink:
        in_specs.append(pl.BlockSpec(memory_space=pltpu.SMEM))
        args.append(sink)
    if has_dl:
        in_specs.append(col(0, 1))
        args.append(dlse)
    out_specs = [col(0, 1), col(0, group), col(0, group), b_spec]
    out_shape = [pltpu.HBM((s, nh * HEAD_DIM), F32), pltpu.HBM((s, nkv * HEAD_DIM), F32),
                 pltpu.HBM((s, nkv * HEAD_DIM), F32), jax.ShapeDtypeStruct((nh, g.q_rows, g.win), F32)]
    scratch = [pltpu.VMEM((dil * g.pad, HEAD_DIM), BF16), pltpu.VMEM((dil * g.pad, HEAD_DIM), BF16),
               pltpu.VMEM((dil * g.pad, HEAD_DIM), F32), pltpu.VMEM((dil * g.pad, HEAD_DIM), F32)]
    if has_sink:
        out_specs.append(pl.BlockSpec((None, SUBLANES, LANES), lambda h: (h, 0, 0)))
        out_shape.append(jax.ShapeDtypeStruct((nh, SUBLANES, LANES), F32))
        scratch.append(pltpu.VMEM((g.q_rows, 1), F32))
    if g.staged:
        scratch += [pltpu.VMEM((s, HEAD_DIM), BF16)] + [pltpu.VMEM((s, HEAD_DIM), F32)] * (4 + int(has_dl))
    res = pl.pallas_call(
        body, name=name, grid=(nh,), in_specs=in_specs, out_specs=out_specs, out_shape=out_shape,
        scratch_shapes=scratch, compiler_params=_params(("arbitrary",), 28 * s * HEAD_DIM * 4))(*args)
    return res[0], res[1], res[2], res[3], (res[4] if has_sink else None)


def dil_merge_fwd(name, outs, lses):
    s, w = outs[0].shape
    ts = _pick(s, ROW_TILE_CANDS)
    ng = len(outs)

    def body(*refs):
        o_refs, l_refs, y_ref = refs[:ng], refs[ng:2 * ng], refs[2 * ng]
        ls = [l[...] for l in l_refs]
        mx = ls[0]
        for l in ls[1:]:
            mx = jnp.maximum(mx, l)
        es = [jnp.exp(l - mx) for l in ls]
        tot = es[0]
        for e in es[1:]:
            tot = tot + e
        acc = (es[0] / tot) * o_refs[0][...]
        for e, o in zip(es[1:], o_refs[1:]):
            acc = acc + (e / tot) * o[...]
        y_ref[...] = acc.astype(y_ref.dtype)

    row = pl.BlockSpec((ts, w), lambda i: (i, 0))
    return pl.pallas_call(
        body, name=name, grid=(s // ts,), in_specs=[row] * (2 * ng), out_specs=row,
        out_shape=pltpu.HBM((s, w), BF16),
        compiler_params=_params(("parallel",), 10 * ts * w * 4))(*outs, *lses)


def dil_merge_bwd(name, dy, outs, lses):
    s, w = outs[0].shape
    ts = _pick(s, ROW_TILE_CANDS)
    ng = len(outs)
    nhead = w // HEAD_DIM

    def body(*refs):
        dy_ref = refs[0]
        o_refs, l_refs = refs[1:1 + ng], refs[1 + ng:1 + 2 * ng]
        do_refs, dl_refs = refs[1 + 2 * ng:1 + 3 * ng], refs[1 + 3 * ng:1 + 4 * ng]
        for hh in range(nhead):
            cols = slice(hh * HEAD_DIM, (hh + 1) * HEAD_DIM)
            dyv = dy_ref[:, cols]
            ls = [l[:, cols] for l in l_refs]
            mx = ls[0]
            for l in ls[1:]:
                mx = jnp.maximum(mx, l)
            es = [jnp.exp(l - mx) for l in ls]
            tot = es[0]
            for e in es[1:]:
                tot = tot + e
            alphas = [e / tot for e in es]
            dal = [jnp.broadcast_to(jnp.sum(dyv * o[:, cols], axis=-1, keepdims=True), dyv.shape) for o in o_refs]
            mean = alphas[0] * dal[0]
            for a, d in zip(alphas[1:], dal[1:]):
                mean = mean + a * d
            for g in range(ng):
                do_refs[g][:, cols] = alphas[g] * dyv
                dl_refs[g][:, cols] = alphas[g] * (dal[g] - mean)

    row = pl.BlockSpec((ts, w), lambda i: (i, 0))
    shape = pltpu.HBM((s, w), F32)
    res = pl.pallas_call(
        body, name=name, grid=(s // ts,), in_specs=[row] * (1 + 2 * ng), out_specs=[row] * (2 * ng),
        out_shape=[shape] * (2 * ng),
        compiler_params=_params(("parallel",), 16 * ts * w * 4))(dy, *outs, *lses)
    return res[:ng], res[ng:]


def _adamw(w, g, m, v):
    m = ADAM_B1 * m + (1.0 - ADAM_B1) * g
    v = ADAM_B2 * v + (1.0 - ADAM_B2) * (g * g)
    m_hat = m / (1.0 - ADAM_B1 ** ADAM_STEP)
    v_hat = v / (1.0 - ADAM_B2 ** ADAM_STEP)
    delta = -ADAM_LR * (m_hat / (jnp.sqrt(v_hat) + ADAM_EPS) + ADAM_WD * w)
    return delta, m, v


def _row_tile(r, c, budget=1 << 20):
    if r * c * 4 <= budget or r % SUBLANES:
        return r
    for t in (1024, 512, 256, 128, 64, 32, 16, 8):
        if r % t == 0 and t * c * 4 <= budget:
            return t
    return SUBLANES


def adam_small(name, g, w, m, v):
    def body(g_ref, w_ref, m_ref, v_ref, d_ref, nm_ref, nv_ref):
        d_ref[...], nm_ref[...], nv_ref[...] = _adamw(w_ref[...], g_ref[...], m_ref[...], v_ref[...])

    shape = jax.ShapeDtypeStruct(w.shape, F32)
    return pl.pallas_call(body, name=name, out_shape=[shape, shape, shape])(g, w, m, v)


def reduce_adam(name, mine, theirs, w, m, v):
    nq, r, c = mine.shape
    tr = _row_tile(r, c)

    def body(*refs):
        parts, (w_ref, m_ref, v_ref, g_ref, d_ref, nm_ref, nv_ref) = refs[:nq], refs[nq:]
        g = parts[0][...].astype(F32)
        for p_ref in parts[1:]:
            g = g + p_ref[...].astype(F32)
        g_ref[...] = g
        d_ref[...], nm_ref[...], nv_ref[...] = _adamw(w_ref[...], g, m_ref[...], v_ref[...])

    def slot(q):
        return pl.BlockSpec((None, tr, c), lambda i: (q, i, 0))

    row = pl.BlockSpec((tr, c), lambda i: (i, 0))
    shape = jax.ShapeDtypeStruct((r, c), F32)
    return pl.pallas_call(
        body, name=name, grid=(r // tr,), in_specs=[slot(q) for q in range(nq)] + [row, row, row],
        out_specs=[row] * 4, out_shape=[shape] * 4,
        compiler_params=_params(("parallel",), (nq * 2 + 7 * 4) * tr * c))(mine, *[theirs] * (nq - 1), *[_in_hbm(t) for t in (w, m, v)])


def _place():
    return lax.axis_index("x"), lax.axis_index("y"), lax.axis_index("c")


def _flip(pos, bits):
    return tuple((1 - p) if b else p for p, b in zip(pos, bits))


def _index(pos):
    return 4 * pos[0] + 2 * pos[1] + pos[2]


ANY = pl.BlockSpec(memory_space=pl.ANY)


HBM = pl.BlockSpec(memory_space=pltpu.HBM)
SEM = pl.BlockSpec(memory_space=pltpu.SEMAPHORE)
EFFECT = pltpu.SideEffectType.DATAFLOW_SIDE_EFFECTING
TO_SIBLING = (0, 0, 1)
TO_CHIPS = [(1, 0, 0), (0, 1, 0), (1, 1, 0)]


def _in_hbm(a):
    return pltpu.with_memory_space_constraint(a, pltpu.HBM)


def _token_value(token):
    return token[0, 0]


def _when(pred, fn):
    if pred is True:
        fn()
    elif pred is not False:
        pl.when(pred)(fn)


def _plan_copy(k, entry, ins, lnd, send_sems, recv_sems):
    a, src_a, sblk, lblk, to, send_if, recv_if = entry
    src = lnd[a] if src_a is None else ins[src_a]
    return pltpu.make_async_remote_copy(
        src_ref=src.at[sblk], dst_ref=lnd[a].at[lblk], send_sem=send_sems.at[k], recv_sem=recv_sems.at[k],
        device_id=to, device_id_type=MESH), send_if, recv_if


def split_start(name, srcs, lands, plan, after):
    ns, nl = len(srcs), len(lands)
    n_copies = len(plan((0, 0, 0)))

    def body(*refs):
        ins, lnd = refs[:ns], refs[ns:ns + nl]
        send_sems, recv_sems = refs[ns + nl + 1], refs[ns + nl + 2]
        token = refs[-1]
        for k, entry in enumerate(plan(_place())):
            cp, send_if, _ = _plan_copy(k, entry, ins, lnd, send_sems, recv_sems)
            _when(send_if, cp.start)
        token[...] = jnp.zeros_like(token)

    outs = pl.pallas_call(
        body, name=name,
        out_shape=(pltpu.SemaphoreType.DMA((n_copies,)), pltpu.SemaphoreType.DMA((n_copies,)),
                   *[pltpu.HBM(a.shape, a.dtype) for a in srcs], *[pltpu.HBM(a.shape, a.dtype) for a in lands],
                   jax.ShapeDtypeStruct((SUBLANES, LANES), F32)),
        in_specs=[HBM] * (ns + nl) + [ANY],
        out_specs=(SEM, SEM, *[HBM] * (ns + nl), pl.BlockSpec(memory_space=pltpu.VMEM)),
        input_output_aliases={i: 2 + i for i in range(ns + nl)},
        compiler_params=pltpu.CompilerParams(has_side_effects=EFFECT),
    )(*[_in_hbm(a) for a in srcs], *[_in_hbm(a) for a in lands], after)
    return outs[0], outs[1], list(outs[2:2 + ns]), list(outs[2 + ns:2 + ns + nl]), outs[-1]


def split_wait(name, send_sems, recv_sems, srcs, lands, plan, after):
    ns, nl = len(srcs), len(lands)

    def body(*refs):
        ins, lnd = refs[:ns], refs[ns:ns + nl]
        s_sems, r_sems = refs[ns + nl], refs[ns + nl + 1]
        for k, entry in enumerate(plan(_place())):
            cp, send_if, recv_if = _plan_copy(k, entry, ins, lnd, s_sems, r_sems)
            _when(send_if, cp.wait_send)
            _when(recv_if, cp.wait_recv)
        refs[-1][...] = jnp.zeros((SUBLANES, LANES), F32)

    outs = pl.pallas_call(
        body, name=name,
        out_shape=(*[pltpu.HBM(a.shape, a.dtype) for a in srcs], *[pltpu.HBM(a.shape, a.dtype) for a in lands],
                   jax.ShapeDtypeStruct((SUBLANES, LANES), F32)),
        in_specs=[HBM] * (ns + nl) + [SEM, SEM, ANY],
        out_specs=(*[HBM] * (ns + nl), pl.BlockSpec(memory_space=pltpu.VMEM)),
        input_output_aliases={i: i for i in range(ns + nl)},
        compiler_params=pltpu.CompilerParams(has_side_effects=EFFECT),
    )(*srcs, *lands, send_sems, recv_sems, after)
    return list(outs[:ns]), list(outs[ns:ns + nl]), outs[-1]


NORTH = 1


def ag_plan(n, rels=TO_CHIPS):
    def plan(me):
        x, y, c = me
        entries = []
        for a in range(n):
            for t in (NORTH, 1 - NORTH):
                blk = _index((x, y, t))
                for rel in rels:
                    entries.append((a, None, blk, blk, _flip((x, y, t), rel), c == NORTH, c == t))
        return entries
    return plan


TO_X, TO_Y = TO_CHIPS[0], TO_CHIPS[1]


def relay_plan(n):
    def plan(me):
        x, y, c = me
        entries = []
        for a in range(n):
            for t, came, goes in ((NORTH, TO_X, TO_Y), (1 - NORTH, TO_Y, TO_X)):
                blk = _index(_flip((x, y, t), came))
                entries.append((a, None, blk, blk, _flip((x, y, t), goes), c == t, c == t))
        return entries
    return plan


def ag_pair(name, lands, after):
    n = len(lands)

    def body(*refs):
        lnd = refs[n + 1:2 * n + 1]
        token = refs[2 * n + 1]
        send_sems, recv_sems = refs[2 * n + 2:]
        token[...] = jnp.zeros_like(token)
        me = _place()
        sibling = _flip(me, TO_SIBLING)
        copies = []
        for a in range(n):
            mine, theirs = lnd[a].at[_index(me)], lnd[a].at[_index(sibling)]
            cp = pltpu.make_async_remote_copy(src_ref=mine, dst_ref=mine, send_sem=send_sems.at[a],
                                              recv_sem=recv_sems.at[a], device_id=sibling, device_id_type=MESH)
            cp.start()
            copies.append((cp, pltpu.make_async_remote_copy(
                src_ref=mine, dst_ref=theirs, send_sem=send_sems.at[a], recv_sem=recv_sems.at[a], device_id=sibling,
                device_id_type=MESH)))
        for cp, arrival in copies:
            arrival.wait_recv()
        for cp, arrival in copies:
            cp.wait_send()

    outs = pl.pallas_call(
        body, name=name, in_specs=[ANY] * (n + 1), out_specs=[ANY] * n + [pl.BlockSpec(memory_space=pltpu.VMEM)],
        out_shape=[jax.ShapeDtypeStruct(l.shape, l.dtype) for l in lands]
        + [jax.ShapeDtypeStruct((SUBLANES, LANES), F32)],
        input_output_aliases={a: a for a in range(n)},
        scratch_shapes=[pltpu.SemaphoreType.DMA((n,)), pltpu.SemaphoreType.DMA((n,))],
    )(*lands, after)
    return list(outs[:n]), outs[n]


def pass_plan(n):
    def plan(me):
        sibling = _flip(me, TO_SIBLING)
        return [(a, None, _index(_flip(me, rel)), _index(_flip(me, rel)), sibling, True, True)
                for a in range(n) for rel in TO_CHIPS]
    return plan


def ag_finish(name, lands):
    n = len(lands)

    def body(*refs):
        lnd = refs[n:2 * n]
        send_sems, recv_sems = refs[2 * n:]
        me = _place()
        sibling = _flip(me, TO_SIBLING)
        copies = []
        for a in range(n):
            for j, rel in enumerate(TO_CHIPS):
                blk = lnd[a].at[_index(_flip(me, rel))]
                there = lnd[a].at[_index(_flip(sibling, rel))]
                cp = pltpu.make_async_remote_copy(
                    src_ref=blk, dst_ref=blk, send_sem=send_sems.at[a * 3 + j], recv_sem=recv_sems.at[a * 3 + j],
                    device_id=sibling, device_id_type=MESH)
                cp.start()
                copies.append((cp, pltpu.make_async_remote_copy(
                    src_ref=blk, dst_ref=there, send_sem=send_sems.at[a * 3 + j], recv_sem=recv_sems.at[a * 3 + j],
                    device_id=sibling, device_id_type=MESH)))
        for cp, arrival in copies:
            arrival.wait_recv()
        for cp, arrival in copies:
            cp.wait_send()

    return pl.pallas_call(
        body, name=name, in_specs=[ANY] * n, out_specs=[ANY] * n,
        out_shape=[jax.ShapeDtypeStruct(l.shape, l.dtype) for l in lands],
        input_output_aliases={a: a for a in range(n)},
        scratch_shapes=[pltpu.SemaphoreType.DMA((3 * n,)), pltpu.SemaphoreType.DMA((3 * n,))],
    )(*lands)


REL = [(b >> 2 & 1, b >> 1 & 1, b & 1) for b in range(N_DEV)]


CHIP_REL = [(0, 0, 0)] + TO_CHIPS
N_CHIPS = len(CHIP_REL)


def rs_pair(name, parts):
    n = len(parts)

    def body(*refs):
        ins, got = refs[:n], refs[n:2 * n]
        send_sems, recv_sems = refs[2 * n:]
        me = _place()
        sibling = _flip(me, TO_SIBLING)
        remote = []
        for a in range(n):
            for q, rel in enumerate(CHIP_REL):
                k = a * N_CHIPS + q
                cp = pltpu.make_async_remote_copy(
                    src_ref=ins[a].at[_index(_flip(sibling, rel))], dst_ref=got[a].at[q], send_sem=send_sems.at[k],
                    recv_sem=recv_sems.at[k], device_id=sibling, device_id_type=MESH)
                cp.start()
                remote.append(cp)
        for cp in remote:
            cp.wait_recv()
        for cp in remote:
            cp.wait_send()

    shapes = [jax.ShapeDtypeStruct((N_CHIPS,) + tuple(p.shape[1:]), p.dtype) for p in parts]
    res = pl.pallas_call(
        body, name=name, in_specs=[ANY] * n, out_specs=[ANY] * n, out_shape=shapes,
        scratch_shapes=[pltpu.SemaphoreType.DMA((N_CHIPS * n,)), pltpu.SemaphoreType.DMA((N_CHIPS * n,))],
    )(*parts)
    return list(res)


def own_blocks():
    me = _place()
    return jnp.stack([_index(_flip(me, rel)) for rel in CHIP_REL]).astype(jnp.int32)


def pair_add(name, blocks, parts, got):
    nq, r, c = got.shape
    tr = _row_tile(r, c, budget=6 << 20)

    def body(blk_ref, a_ref, b_ref, o_ref):
        o_ref[...] = (a_ref[...].astype(F32) + b_ref[...].astype(F32)).astype(o_ref.dtype)

    spec = pl.BlockSpec((None, tr, c), lambda q, i, blk: (q, i, 0))
    return pl.pallas_call(
        body, name=name,
        grid_spec=pltpu.PrefetchScalarGridSpec(
            num_scalar_prefetch=1, grid=(nq, r // tr),
            in_specs=[pl.BlockSpec((None, tr, c), lambda q, i, blk: (blk[q], i, 0)), spec], out_specs=spec),
        out_shape=pltpu.HBM(got.shape, got.dtype),
        compiler_params=_params(("arbitrary", "arbitrary"), 6 * tr * c * 2))(blocks, parts, got)


def rs_pair_plan(n):
    def plan(me):
        sibling = _flip(me, TO_SIBLING)
        return [(a, a, _index(_flip(sibling, rel)), q, sibling, True, True)
                for a in range(n) for q, rel in enumerate(CHIP_REL)]
    return plan


def rs_plan(n):
    def plan(me):
        return [(a, a, q, q, _flip(me, CHIP_REL[q]), True, True) for a in range(n) for q in range(1, N_CHIPS)]
    return plan


def rs_start(name, sums, after):
    lands = [lax.empty(t.shape, t.dtype) for t in sums]
    return split_start(name, sums, lands, rs_plan(len(sums)), after)


def allreduce_small(name, pack, after):
    rows, lanes = pack.shape

    def body(x_ref, after_ref, o_ref, land, send_sems, recv_sems):
        me = _place()
        idx = _index(me)
        land[idx] = x_ref[...]
        copies = []
        for r in range(1, N_DEV):
            peer = _flip(me, REL[r])
            cp = pltpu.make_async_remote_copy(
                src_ref=x_ref, dst_ref=land.at[idx], send_sem=send_sems.at[r - 1], recv_sem=recv_sems.at[r - 1],
                device_id=peer, device_id_type=MESH)
            cp.start()
            copies.append(cp)
        for cp in copies:
            cp.wait_recv()
        for cp in copies:
            cp.wait_send()
        acc = land[0]
        for i in range(1, N_DEV):
            acc = acc + land[i]
        o_ref[...] = acc

    return pl.pallas_call(
        body, name=name, in_specs=[pl.BlockSpec(memory_space=pltpu.VMEM), ANY],
        out_specs=pl.BlockSpec(memory_space=pltpu.VMEM), out_shape=jax.ShapeDtypeStruct((rows, lanes), F32),
        scratch_shapes=[pltpu.VMEM((N_DEV, rows, lanes), F32), pltpu.SemaphoreType.DMA((7,)),
                        pltpu.SemaphoreType.DMA((7,))],
    )(pack, after)


def _pad_rows(a, rows):
    return jnp.pad(a, ((0, rows - a.shape[0]), (0, 0)))


def _as_tiles(vec):
    n = vec.shape[0]
    rows = -(-n // LANES)
    rows = -(-rows // SUBLANES) * SUBLANES
    return jnp.pad(vec, (0, rows * LANES - n)).reshape(rows, LANES)


def kernel(x, p, rel_bias_table, attn_norm, w_in, sink_a, w_branch_a, w_branch_b, w_out, ffn_norm, w_ffn_gate, w_ffn_up, conv_w, conv_b, w_ffn_down, ple_norm, w_ple_gate, w_ple_proj, final_norm, loss_target, m_rel_bias_table, m_attn_norm, m_w_in, m_sink_a, m_w_branch_a, m_w_branch_b, m_w_out, m_ffn_norm, m_w_ffn_gate, m_w_ffn_up, m_conv_w, m_conv_b, m_w_ffn_down, m_ple_norm, m_w_ple_gate, m_w_ple_proj, m_final_norm, v_rel_bias_table, v_attn_norm, v_w_in, v_sink_a, v_w_branch_a, v_w_branch_b, v_w_out, v_ffn_norm, v_w_ffn_gate, v_w_ffn_up, v_conv_w, v_conv_b, v_w_ffn_down, v_ple_norm, v_w_ple_gate, v_w_ple_proj, v_final_norm):
    xs = x[0]
    s, d = xs.shape
    ps = p[0, 0]
    target = loss_target[0]
    me = 4 * lax.axis_index("x") + 2 * lax.axis_index("y") + lax.axis_index("c")

    big = dict(w_in=w_in[0], w_branch_a=w_branch_a[0], w_branch_b=w_branch_b[0], w_out=w_out[0],
               w_ffn_gate=w_ffn_gate[0], w_ffn_up=w_ffn_up[0], w_ffn_down=w_ffn_down[0],
               w_ple_gate=w_ple_gate[0], w_ple_proj=w_ple_proj[0])
    big_m = dict(w_in=m_w_in[0], w_branch_a=m_w_branch_a[0], w_branch_b=m_w_branch_b[0], w_out=m_w_out[0],
                 w_ffn_gate=m_w_ffn_gate[0], w_ffn_up=m_w_ffn_up[0], w_ffn_down=m_w_ffn_down[0],
                 w_ple_gate=m_w_ple_gate[0], w_ple_proj=m_w_ple_proj[0])
    big_v = dict(w_in=v_w_in[0], w_branch_a=v_w_branch_a[0], w_branch_b=v_w_branch_b[0], w_out=v_w_out[0],
                 w_ffn_gate=v_w_ffn_gate[0], w_ffn_up=v_w_ffn_up[0], w_ffn_down=v_w_ffn_down[0],
                 w_ple_gate=v_w_ple_gate[0], w_ple_proj=v_w_ple_proj[0])
    names = list(big)
    nf = big["w_ffn_gate"].shape[1]

    shards = {k: big[k].astype(BF16) for k in names}
    shards["conv_w"] = _pad_rows(conv_w[0], SUBLANES)
    flipped = ("w_ffn_gate", "w_ffn_up")
    for k in flipped:
        big[k], big_m[k], big_v[k] = big[k].T, big_m[k].T, big_v[k].T
    ag_groups = [["w_in"], ["w_branch_a", "w_branch_b", "w_out"], ["w_ffn_gate", "conv_w"], ["w_ffn_up"],
                 ["w_ffn_down"], ["w_ple_gate", "w_ple_proj"]]
    ag_started = {}
    wg = {}

    ag_paired, ag_passing = {}, {}

    def pair(gi, after):
        lands = [lax.dynamic_update_index_in_dim(lax.empty((N_DEV,) + shards[k].shape, shards[k].dtype), shards[k],
                                                 me, 0) for k in ag_groups[gi]]
        ag_paired[gi], token = ag_pair(f"ag_pair{gi}", lands, after)
        return token

    def copies(gi):
        return ag_plan(len(ag_groups[gi]), [TO_X, TO_Y] if gi == 0 else TO_CHIPS)

    def start(gi, after):
        s_sems, r_sems, _, lands, token = split_start(f"ag_start{gi}", [], ag_paired[gi], copies(gi), after)
        ag_started[gi] = (s_sems, r_sems, lands)
        return token

    def landed(gi, after):
        s_sems, r_sems, lands = ag_started[gi]
        return split_wait(f"ag_wait{gi}", s_sems, r_sems, [], lands, copies(gi), after)[1:]

    def relayed(lands, after, meanwhile):
        plan = relay_plan(len(lands))
        s_sems, r_sems, _, lands, token = split_start("ag_relay0", [], lands, plan, after)
        return split_wait("ag_relayed0", s_sems, r_sems, [], lands, plan, meanwhile + _token_value(token))[1:]

    def pass_on(gi, lands, after):
        s_sems, r_sems, _, lands, token = split_start(f"ag_pass{gi}", [], lands, pass_plan(len(lands)), after)
        ag_passing[gi] = (s_sems, r_sems, lands)
        return token

    def ready(gi, after):
        s_sems, r_sems, lands = ag_passing[gi]
        lands = split_wait(f"ag_ready{gi}", s_sems, r_sems, [], lands, pass_plan(len(lands)), after)[1]
        wg.update(zip(ag_groups[gi], lands))

    cb = conv_b.reshape(N_DEV, 1, nf)

    table_t = rel_bias_table.T
    geo_a = dict(half=A_BLOCK, q_rows=ATTN_Q_ROWS, n_chains=ATTN_CHAINS, dil=1, nh=A_Q_HEADS, group=A_GROUP,
                 cq=COL_QA, ck=COL_KA, cv=COL_VA)
    geo_b = [dict(half=B_BLOCK, q_rows=min(ATTN_Q_ROWS, s // dil), n_chains=ATTN_CHAINS, dil=dil,
                  nh=B_HEADS_PER_GROUP, group=1, cq=COL_QB + g * B_OUT_W, ck=COL_KB + g * B_OUT_W,
                  cv=COL_VB + g * B_OUT_W) for g, (_, dil) in enumerate(B_PATTERNS)]
    bucket_a = bucket_tile(geo_a["q_rows"], A_BLOCK, 1)
    bias_a = bias_build("bias_a", table_t, bucket_a, 0, A_Q_HEADS, A_BLOCK)
    buckets_b = [bucket_tile(gb["q_rows"], B_BLOCK, gb["dil"]) for gb in geo_b]
    biases_b = [bias_build(f"bias_b{g}", table_t, buckets_b[g], A_Q_HEADS + g * B_HEADS_PER_GROUP, B_HEADS_PER_GROUP,
                           B_BLOCK) for g in range(len(B_PATTERNS))]

    token = start(0, pair(0, xs))
    h = rms_fwd("rms_attn", xs, attn_norm + _token_value(token))
    lands0, token = landed(0, pair(5, pair(4, pair(3, pair(2, pair(1, h))))))
    bias_corner = bias_a[0, :1, :1] + sum(b[0, :1, :1] for b in biases_b)
    lands0, token = relayed(lands0, token, bias_corner)
    token = start(5, start(4, start(3, start(2, start(1, token)))))
    wg["w_in"] = ag_finish("ag_finish0", lands0)[0]
    proj = mm_cols("proj_in", h, wg["w_in"], F32, fold=True, after=token)
    token = pass_on(1, landed(1, proj)[0], proj)
    sink = sink_a[0] + _token_value(token)
    ya, lse_a = band_attn_fwd("attn_a_fwd", proj, bias_a, sink, **geo_a)
    outs_b, lses_b = [], []
    for g in range(len(B_PATTERNS)):
        o, l = band_attn_fwd(f"attn_b{g}_fwd", proj, biases_b[g], None, **geo_b[g])
        outs_b.append(o)
        lses_b.append(l)
    yb = dil_merge_fwd("dil_merge_fwd", outs_b, lses_b)
    ready(1, yb)
    token = pass_on(2, landed(2, yb)[0], yb)
    w_out_full = wg["w_out"].reshape(d, d)
    ta = mm_cols("branch_a", ya, wg["w_branch_a"], F32, fold=True, after=token)
    tb = mm_cols("branch_b", yb, wg["w_branch_b"], F32, fold=True)
    merged = gate_merge_fwd("gate_merge_fwd", proj, ta, tb, d)
    x1 = mm_plain("mix_out", merged, w_out_full, F32, res=xs)

    hf = rms_fwd("rms_ffn", x1, ffn_norm)
    ready(2, hf)
    token = pass_on(3, landed(3, hf)[0], hf)
    cw = wg["conv_w"]
    gpre = mm_cols("ffn_gate", hf, wg["w_ffn_gate"], F32, fold=False, after=token)
    ready(3, gpre)
    token = pass_on(4, landed(4, gpre)[0], gpre)
    u = mm_cols("ffn_up", hf, wg["w_ffn_up"], F32, fold=False, after=token)
    z = ffn_mid_fwd("ffn_mid_fwd", gpre, u, cw, cb)
    ready(4, z)
    token = pass_on(5, landed(5, z)[0], z)
    x2 = mm_jsum("ffn_down", z, wg["w_ffn_down"], F32, res=x1, after=token)

    hp = rms_fwd("rms_ple", x2, ple_norm)
    ready(5, hp)
    w_pg_full = wg["w_ple_gate"].reshape(d, d)
    lp = mm_plain("ple_gate", hp, w_pg_full, F32)
    pp = mm_cols("ple_proj", ps, wg["w_ple_proj"], F32, fold=True)
    loss_part, dx3, dlp, dpp, d_final = tail_fwd_bwd("tail", x2, lp, pp, final_norm.reshape(1, d), target)

    grads = {}
    rs_started = []
    blocks = own_blocks()

    exchanging = []

    def exchange(tag, keys):
        parts = [grads[k] for k in keys]
        lands = [lax.empty((N_CHIPS,) + tuple(p.shape[1:]), p.dtype) for p in parts]
        s_sems, r_sems, parts, lands, token = split_start(f"rs_pair_{tag}", parts, lands, rs_pair_plan(len(keys)), blocks)
        exchanging.append((tag, keys, s_sems, r_sems, parts, lands))
        return _token_value(token)

    def send(after):
        tag, keys, s_sems, r_sems, parts, lands = exchanging.pop(0)
        parts, got, _ = split_wait(f"rs_paired_{tag}", s_sems, r_sems, parts, lands, rs_pair_plan(len(keys)), after)
        return send_sums(tag, keys, parts, got)

    def send_sums(tag, keys, parts, got):
        sums = [pair_add(f"pair_add_{k}", blocks, p, g) for k, p, g in zip(keys, parts, got)]
        s_sems, r_sems, srcs, lands, token = rs_start(f"rs_start_{tag}", sums, blocks)
        rs_started.append((tag, keys, s_sems, r_sems, srcs, lands))
        return token

    grads["w_ple_proj"] = mm_tn_cols("d_w_ple_proj", ps, dpp, N_DEV, big["w_ple_proj"].shape[1], BF16, folded=True)
    grads["w_ple_gate"] = mm_tn_plain("d_w_ple_gate", hp, dlp, BF16).reshape(N_DEV, d // N_DEV, d)
    tok = exchange("ple", ["w_ple_proj", "w_ple_gate"])
    dhp = mm_nt_plain("d_hp", dlp, w_pg_full, F32)
    dx2, dx2_b, d_ple = rms_bwd("rms_ple_bwd", x2, ple_norm + tok, dhp, dx3, True)

    dz = mm_nt_j("d_z", dx2_b, wg["w_ffn_down"], BF16)
    grads["w_ffn_down"] = mm_tn_j("d_w_ffn_down", z, dx2_b, BF16)
    tok = _token_value(send(dz)) + exchange("down", ["w_ffn_down"])
    du, dgpre, dcw = ffn_mid_bwd("ffn_mid_bwd", gpre, u, dz, cw, cb + tok)
    grads["w_ffn_up"] = mm_tn_j("d_w_ffn_up", du, hf, BF16)
    grads["w_ffn_gate"] = mm_tn_j("d_w_ffn_gate", dgpre, hf, BF16)
    dhf = mm_nt_jsum("d_hf_up", du, wg["w_ffn_up"], F32, folded=False)
    dhf = mm_nt_jsum("d_hf_gate", dgpre, wg["w_ffn_gate"], F32, folded=False, res=dhf)
    tok = _token_value(send(dhf)) + exchange("upgate", ["w_ffn_up", "w_ffn_gate"])
    dx1, dx1_b, d_ffn = rms_bwd("rms_ffn_bwd", x1, ffn_norm + tok, dhf, dx2, True)

    dmerged = mm_nt_plain("d_merged", dx1_b, w_out_full, F32)
    grads["w_out"] = mm_tn_plain("d_w_out", merged, dx1_b, BF16).reshape(N_DEV, d // N_DEV, d)
    dta, dtb, dga, dgb = gate_merge_bwd("gate_merge_bwd", dmerged, proj, ta, tb, d)
    grads["w_branch_a"] = mm_tn_cols("d_w_branch_a", ya, dta, N_DEV, big["w_branch_a"].shape[1], BF16, folded=True)
    grads["w_branch_b"] = mm_tn_cols("d_w_branch_b", yb, dtb, N_DEV, big["w_branch_b"].shape[1], BF16, folded=True)
    dya = mm_nt_jsum("d_ya", dta, wg["w_branch_a"], F32, folded=True)
    dyb = mm_nt_jsum("d_yb", dtb, wg["w_branch_b"], F32, folded=True)
    tok = _token_value(send(dyb)) + exchange("mix", ["w_out", "w_branch_a", "w_branch_b"])
    dqa, dka, dva, dbias_a, dsink = band_attn_bwd("attn_a_bwd", proj, bias_a, sink + tok, dya, ya, lse_a, None, **geo_a)
    douts_b, dlses_b = dil_merge_bwd("dil_merge_bwd", dyb, outs_b, lses_b)
    dq_b, dk_b, dv_b, dbias_b = [], [], [], []
    for g in range(len(B_PATTERNS)):
        dq, dk, dv, db, _ = band_attn_bwd(f"attn_b{g}_bwd", proj, biases_b[g], None, douts_b[g], outs_b[g], lses_b[g],
                                          dlses_b[g], **geo_b[g])
        dq_b.append(dq)
        dk_b.append(dk)
        dv_b.append(dv)
        dbias_b.append(db)
    dproj = jnp.concatenate([t.astype(BF16) for t in [dqa, dka, dva] + dq_b + dk_b + dv_b + [dga, dgb]], axis=1)
    token = send(dproj)
    grads["w_in"] = mm_tn_cols("d_w_in", h, dproj, N_DEV, big["w_in"].shape[1], BF16, folded=True, after=token)
    token = send_sums("in", ["w_in"], [grads["w_in"]], rs_pair("rs_pair_in", [grads["w_in"]]))
    dh = mm_nt_jsum("d_h", dproj, wg["w_in"], F32, folded=True, after=token)
    grad_x, _, d_attn = rms_bwd("rms_attn_bwd", xs, attn_norm, dh, dx1, False)

    dt_a = table_grad("table_grad_a", dbias_a, bucket_a)[:, 0, :N_BUCKETS]
    dt_b = [table_grad(f"table_grad_b{g}", dbias_b[g], buckets_b[g])[:, 0, :N_BUCKETS] for g in range(len(B_PATTERNS))]
    d_table_part = jnp.concatenate([dt_a] + dt_b, axis=0).T

    pieces = [
        ("loss", loss_part[0, :1]),
        ("table", d_table_part.reshape(-1)),
        ("attn_norm", d_attn.reshape(-1)),
        ("sink", dsink[:, 0, 0]),
        ("ffn_norm", d_ffn.reshape(-1)),
        ("conv_w", dcw[:, 0:3, :].reshape(-1)),
        ("conv_b", dcw[:, 3, :].reshape(-1)),
        ("ple_norm", d_ple.reshape(-1)),
        ("final_norm", d_final.reshape(-1)),
    ]
    tiles = [_as_tiles(v) for _, v in pieces]
    pack = jnp.concatenate(tiles, axis=0)

    out_g, out_d, out_m, out_v = {}, {}, {}, {}

    def finish(group, after):
        tag, keys, s_sems, r_sems, srcs, lands = group
        srcs, lands, _ = split_wait(f"rs_wait_{tag}", s_sems, r_sems, srcs, lands, rs_plan(len(keys)), after)
        for k, mine, theirs in zip(keys, srcs, lands):
            res = reduce_adam("adam_" + k, mine, theirs, big[k], big_m[k], big_v[k])
            after = res[1]
            out_g[k], out_d[k], out_m[k], out_v[k] = [(t.T if k in flipped else t)[None] for t in res]
        return after

    after = pack
    for group in rs_started[:-1]:
        after = finish(group, after)
    total = allreduce_small("allreduce_small", pack, after)
    finish(rs_started[-1], total)
    small = {}
    row = 0
    for (nm, v), t in zip(pieces, tiles):
        small[nm] = total[row:row + t.shape[0]].reshape(-1)[:v.shape[0]]
        row += t.shape[0]
    loss = small["loss"][0]
    g_small = dict(
        rel_bias_table=small["table"].reshape(rel_bias_table.shape),
        attn_norm=small["attn_norm"].reshape(attn_norm.shape),
        sink_a=small["sink"].reshape(sink_a.shape),
        ffn_norm=small["ffn_norm"].reshape(ffn_norm.shape),
        conv_w=lax.dynamic_index_in_dim(small["conv_w"].reshape(N_DEV, 3, nf), me, 0, keepdims=False)[None],
        conv_b=small["conv_b"].reshape(conv_b.shape),
        ple_norm=small["ple_norm"].reshape(ple_norm.shape),
        final_norm=small["final_norm"].reshape(1, d),
    )
    w_small = dict(rel_bias_table=(rel_bias_table, m_rel_bias_table, v_rel_bias_table),
                   attn_norm=(attn_norm, m_attn_norm, v_attn_norm), sink_a=(sink_a, m_sink_a, v_sink_a),
                   ffn_norm=(ffn_norm, m_ffn_norm, v_ffn_norm), conv_w=(conv_w, m_conv_w, v_conv_w),
                   conv_b=(conv_b, m_conv_b, v_conv_b), ple_norm=(ple_norm, m_ple_norm, v_ple_norm),
                   final_norm=(final_norm, m_final_norm, v_final_norm))

    for k, (wv, mv, vv) in w_small.items():
        shape = wv.shape
        two_d = (1, shape[0]) if len(shape) == 1 else ((shape[0] * shape[1], shape[2]) if len(shape) == 3 else shape)
        gk = g_small[k].reshape(two_d)
        dl, nm, nv = adam_small("adam_" + k, gk, wv.reshape(two_d), mv.reshape(two_d), vv.reshape(two_d))
        out_g[k], out_d[k], out_m[k], out_v[k] = gk.reshape(shape), dl.reshape(shape), nm.reshape(shape), nv.reshape(shape)

    order = ["rel_bias_table", "attn_norm", "w_in", "sink_a", "w_branch_a", "w_branch_b", "w_out", "ffn_norm",
             "w_ffn_gate", "w_ffn_up", "conv_w", "conv_b", "w_ffn_down", "ple_norm", "w_ple_gate", "w_ple_proj",
             "final_norm"]
    return (loss, grad_x[None], *[out_g[k] for k in order], *[out_d[k] for k in order],
            *[out_m[k] for k in order], *[out_v[k] for k in order])
```

```python
import math

import jax
import jax.numpy as jnp
from jax import lax
from jax.experimental import pallas as pl
from jax.experimental.pallas import tpu as pltpu

F32 = jnp.float32
BF16 = jnp.bfloat16
MESH = pl.DeviceIdType.MESH
N_DEV = 8

HEAD_DIM = 128
A_Q_HEADS = 8
A_KV_HEADS = 2
A_GROUP = A_Q_HEADS // A_KV_HEADS
A_BLOCK = 128
B_PATTERNS = ((128, 1), (512, 4), (2048, 16))
B_HEADS_PER_GROUP = 4
B_HEADS = len(B_PATTERNS) * B_HEADS_PER_GROUP
B_BLOCK = 64
N_BUCKETS = 32
MAX_DISTANCE = 1024
A_Q_W = A_Q_HEADS * HEAD_DIM
A_KV_W = A_KV_HEADS * HEAD_DIM
B_W = B_HEADS * HEAD_DIM
B_OUT_W = B_HEADS_PER_GROUP * HEAD_DIM
COL_QA = 0
COL_KA = COL_QA + A_Q_W
COL_VA = COL_KA + A_KV_W
COL_QB = COL_VA + A_KV_W
COL_KB = COL_QB + B_W
COL_VB = COL_KB + B_W
COL_GATES = COL_VB + B_W
RMS_EPS = 1e-6
NEG_INF = -1e30
ATTN_SCALE = HEAD_DIM ** -0.5
ATTN_Q_ROWS = 256
ATTN_CHAINS = 4

ADAM_LR = 0.001
ADAM_B1 = 0.9
ADAM_B2 = 0.999
ADAM_EPS = 1e-08
ADAM_WD = 0.01
ADAM_STEP = 10

GELU_C = math.sqrt(2.0 / math.pi)
GELU_A = 0.044715

V7X_VMEM_BYTES = 64 * 1024 * 1024
VMEM_CEILING = V7X_VMEM_BYTES - 8 * 1024 * 1024
LANES = 128
SUBLANES = 8


def _pick(n, cands):
    for c in cands:
        if n % c == 0:
            return c
    return n


def _nbytes(shape, dtype):
    n = 1
    for d in shape:
        if d is not None:
            n *= d
    return n * jnp.dtype(dtype).itemsize


def _params(sem, est_bytes):
    limit = int(min(VMEM_CEILING, max(32 * 1024 * 1024, 2 * est_bytes + (8 << 20))))
    return pltpu.CompilerParams(dimension_semantics=sem, vmem_limit_bytes=limit)


def _mm(name, a, b, a_bs, a_im, b_bs, b_im, out_shape, out_dtype, o_bs, o_im, grid, dims,
        res=None, r_bs=None, r_im=None, after=None):
    nk = grid[-1]
    nax = len(grid)
    has_res = res is not None
    has_after = after is not None
    o_tile = tuple(d for d in o_bs if d is not None)

    def body(*refs):
        a_ref, b_ref = refs[:2]
        r_ref = refs[2] if has_res else None
        n_in = 2 + has_res + has_after
        o_ref = refs[n_in]
        rest = refs[n_in + 1:]

        def prod():
            return lax.dot_general(a_ref[...].astype(BF16), b_ref[...].astype(BF16), (dims, ((), ())),
                                   preferred_element_type=F32)

        def finish(r):
            if r_ref is not None:
                r = r + r_ref[...].astype(F32)
            o_ref[...] = r.astype(o_ref.dtype)

        if nk == 1:
            finish(prod())
        else:
            acc = rest[0]
            k = pl.program_id(nax - 1)

            @pl.when(k == 0)
            def _():
                acc[...] = prod()

            @pl.when(k > 0)
            def _():
                acc[...] += prod()

            @pl.when(k == nk - 1)
            def _():
                finish(acc[...])

    in_specs = [pl.BlockSpec(a_bs, a_im), pl.BlockSpec(b_bs, b_im)]
    args = [a, b]
    est = _nbytes(a_bs, a.dtype) + _nbytes(b_bs, b.dtype) + _nbytes(o_bs, out_dtype) + 2 * _nbytes(o_tile, F32)
    if has_res:
        in_specs.append(pl.BlockSpec(r_bs, r_im))
        args.append(res)
        est += _nbytes(r_bs, res.dtype)
    if has_after:
        in_specs.append(pl.BlockSpec(memory_space=pl.ANY))
        args.append(after)
    scratch = [] if nk == 1 else [pltpu.VMEM(o_tile, F32)]
    sem = ("parallel",) * (nax - 1) + ("arbitrary",)
    return pl.pallas_call(
        body, name=name, grid=grid, in_specs=in_specs, out_specs=pl.BlockSpec(o_bs, o_im),
        out_shape=pltpu.HBM(out_shape, out_dtype), scratch_shapes=scratch,
        compiler_params=_params(sem, est))(*args)


TM_CANDS = (1024, 512, 256, 128, 64, 32, 16, 8)
TM_WIDE_CANDS = (2048,) + TM_CANDS
MM_WHOLE_K_BYTES = 36 * 1024 * 1024


def _whole(k, tile_bytes):
    return k if 2 * tile_bytes(k) <= MM_WHOLE_K_BYTES else _pick(k, TK_CANDS)
TK_CANDS = (1024, 512, 256, 128)
TN_CANDS = (1024, 512, 256, 128)


def mm_cols(name, a, wg, out_dtype, fold, after=None):
    m, k = a.shape
    nj, _, n = wg.shape
    tm = _pick(m, TM_WIDE_CANDS)
    tk = _whole(k, lambda t: _nbytes((tm, t), a.dtype) + _nbytes((t, n), wg.dtype) + _nbytes((tm, n), out_dtype))
    grid = (nj, m // tm, k // tk)
    if fold:
        shape, o_bs, o_im = (m, nj * n), (tm, n), (lambda j, i, kk: (i, j))
    else:
        shape, o_bs, o_im = (nj, m, n), (None, tm, n), (lambda j, i, kk: (j, i, 0))
    return _mm(name, a, wg, (tm, tk), lambda j, i, kk: (i, kk), (None, tk, n), lambda j, i, kk: (j, kk, 0),
               shape, out_dtype, o_bs, o_im, grid, ((1,), (0,)), after=after)


def mm_plain(name, a, w, out_dtype, res=None):
    m, k = a.shape
    n = w.shape[1]
    tm, tk, tn = _pick(m, TM_CANDS if res is not None else TM_WIDE_CANDS), _pick(k, TK_CANDS), _pick(n, TN_CANDS)
    if res is None:
        tk = _whole(k, lambda t: _nbytes((tm, t), a.dtype) + _nbytes((t, tn), w.dtype) + _nbytes((tm, tn), out_dtype))
    grid = (n // tn, m // tm, k // tk)
    return _mm(name, a, w, (tm, tk), lambda j, i, kk: (i, kk), (tk, tn), lambda j, i, kk: (kk, j),
               (m, n), out_dtype, (tm, tn), lambda j, i, kk: (i, j), grid, ((1,), (0,)),
               res, (tm, tn), lambda j, i, kk: (i, j))


def mm_jsum(name, aj, wg, out_dtype, res=None, after=None):
    nj, m, ka = aj.shape
    n = wg.shape[2]
    tm, tn = _pick(m, TM_CANDS if res is not None else TM_WIDE_CANDS), _pick(n, TN_CANDS)
    grid = (m // tm, n // tn, nj)
    return _mm(name, aj, wg, (None, tm, ka), lambda i, jn, j: (j, i, 0), (None, ka, tn), lambda i, jn, j: (j, 0, jn),
               (m, n), out_dtype, (tm, tn), lambda i, jn, j: (i, jn), grid, ((1,), (0,)),
               res, (tm, tn), lambda i, jn, j: (i, jn), after=after)


def mm_tn_cols(name, a, g, nj, n, out_dtype, folded, after=None):
    s, kw = a.shape
    tkw = _pick(kw, TM_WIDE_CANDS)
    ts = _whole(s, lambda t: _nbytes((t, tkw), a.dtype) + _nbytes((t, n), g.dtype) + _nbytes((tkw, n), out_dtype))
    grid = (nj, kw // tkw, s // ts)
    if folded:
        g_bs, g_im = (ts, n), (lambda j, i, ss: (ss, j))
    else:
        g_bs, g_im = (None, ts, n), (lambda j, i, ss: (j, ss, 0))
    return _mm(name, a, g, (ts, tkw), lambda j, i, ss: (ss, i), g_bs, g_im,
               (nj, kw, n), out_dtype, (None, tkw, n), lambda j, i, ss: (j, i, 0), grid, ((0,), (0,)), after=after)


def mm_tn_plain(name, a, g, out_dtype):
    s, kw = a.shape
    n = g.shape[1]
    tkw, tn = _pick(kw, TM_WIDE_CANDS), _pick(n, TN_CANDS)
    ts = _whole(s, lambda t: _nbytes((t, tkw), a.dtype) + _nbytes((t, tn), g.dtype) + _nbytes((tkw, tn), out_dtype))
    grid = (kw // tkw, n // tn, s // ts)
    return _mm(name, a, g, (ts, tkw), lambda i, jn, ss: (ss, i), (ts, tn), lambda i, jn, ss: (ss, jn),
               (kw, n), out_dtype, (tkw, tn), lambda i, jn, ss: (i, jn), grid, ((0,), (0,)))


def mm_tn_j(name, aj, g, out_dtype):
    nj, s, ka = aj.shape
    n = g.shape[1]
    tn = _pick(n, TM_WIDE_CANDS)
    ts = _whole(s, lambda t: _nbytes((t, ka), aj.dtype) + _nbytes((t, tn), g.dtype) + _nbytes((ka, tn), out_dtype))
    grid = (nj, n // tn, s // ts)
    return _mm(name, aj, g, (None, ts, ka), lambda j, jn, ss: (j, ss, 0), (ts, tn), lambda j, jn, ss: (ss, jn),
               (nj, ka, n), out_dtype, (None, ka, tn), lambda j, jn, ss: (j, 0, jn), grid, ((0,), (0,)))


def mm_nt_plain(name, g, w, out_dtype):
    m, n = g.shape
    k = w.shape[0]
    tm, tkk = _pick(m, TM_WIDE_CANDS), _pick(k, TN_CANDS)
    tn = _whole(n, lambda t: _nbytes((tm, t), g.dtype) + _nbytes((tkk, t), w.dtype) + _nbytes((tm, tkk), out_dtype))
    grid = (k // tkk, m // tm, n // tn)
    return _mm(name, g, w, (tm, tn), lambda kk, i, jn: (i, jn), (tkk, tn), lambda kk, i, jn: (kk, jn),
               (m, k), out_dtype, (tm, tkk), lambda kk, i, jn: (i, kk), grid, ((1,), (1,)))


def mm_nt_j(name, g, wg, out_dtype):
    m, n = g.shape
    nj, ka, _ = wg.shape
    tm = _pick(m, TM_WIDE_CANDS)
    tn = _whole(n, lambda t: _nbytes((tm, t), g.dtype) + _nbytes((ka, t), wg.dtype) + _nbytes((tm, ka), out_dtype))
    grid = (nj, m // tm, n // tn)
    return _mm(name, g, wg, (tm, tn), lambda j, i, jn: (i, jn), (None, ka, tn), lambda j, i, jn: (j, 0, jn),
               (nj, m, ka), out_dtype, (None, tm, ka), lambda j, i, jn: (j, i, 0), grid, ((1,), (1,)))


def mm_nt_jsum(name, g, wg, out_dtype, folded, res=None, after=None):
    nj, k, n = wg.shape
    m = g.shape[0] if folded else g.shape[1]
    tm, tkk = _pick(m, TM_CANDS if res is not None else TM_WIDE_CANDS), _pick(k, TN_CANDS)
    grid = (m // tm, k // tkk, nj)
    if folded:
        g_bs, g_im = (tm, n), (lambda i, kk, j: (i, j))
    else:
        g_bs, g_im = (None, tm, n), (lambda i, kk, j: (j, i, 0))
    return _mm(name, g, wg, g_bs, g_im, (None, tkk, n), lambda i, kk, j: (j, kk, 0),
               (m, k), out_dtype, (tm, tkk), lambda i, kk, j: (i, kk), grid, ((1,), (1,)),
               res, (tm, tkk), lambda i, kk, j: (i, kk), after=after)


ROW_TILE_CANDS = (256, 128, 64, 32, 16, 8)


def _rstd(x):
    return lax.rsqrt(jnp.mean(x * x, axis=-1, keepdims=True) + RMS_EPS)


def _sigmoid(t):
    return 1.0 / (1.0 + jnp.exp(-t))


def rms_fwd(name, x, gain):
    s, d = x.shape
    ts = _pick(s, ROW_TILE_CANDS)

    def body(x_ref, g_ref, h_ref):
        xv = x_ref[...]
        h_ref[...] = ((xv * _rstd(xv)) * g_ref[...]).astype(h_ref.dtype)

    return pl.pallas_call(
        body, name=name, grid=(s // ts,),
        in_specs=[pl.BlockSpec((ts, d), lambda i: (i, 0)), pl.BlockSpec((1, d), lambda i: (0, 0))],
        out_specs=pl.BlockSpec((ts, d), lambda i: (i, 0)),
        out_shape=pltpu.HBM((s, d), BF16),
        compiler_params=_params(("parallel",), 3 * ts * d * 4))(x, gain)


def rms_bwd(name, x, gain, dh, dres, bf16_copy):
    s, d = x.shape
    ts = _pick(s, ROW_TILE_CANDS)

    def body(x_ref, g_ref, dh_ref, dr_ref, dx_ref, *rest):
        dxb_ref, dg_ref = rest if bf16_copy else (None, rest[0])
        xv = x_ref[...]
        r = _rstd(xv)
        xhat = xv * r
        dhv = dh_ref[...].astype(F32)
        dxhat = dhv * g_ref[...]
        dx = dr_ref[...] + r * (dxhat - xhat * jnp.mean(dxhat * xhat, axis=-1, keepdims=True))
        dx_ref[...] = dx
        if bf16_copy:
            dxb_ref[...] = dx.astype(dxb_ref.dtype)
        part = jnp.sum(dhv * xhat, axis=0, keepdims=True)

        @pl.when(pl.program_id(0) == 0)
        def _():
            dg_ref[...] = part

        @pl.when(pl.program_id(0) > 0)
        def _():
            dg_ref[...] += part

    row = pl.BlockSpec((ts, d), lambda i: (i, 0))
    vec = pl.BlockSpec((1, d), lambda i: (0, 0))
    copy_spec, copy_shape = ([row], [pltpu.HBM((s, d), BF16)]) if bf16_copy else ([], [])
    res = pl.pallas_call(
        body, name=name, grid=(s // ts,), in_specs=[row, vec, row, row], out_specs=[row] + copy_spec + [vec],
        out_shape=[pltpu.HBM((s, d), F32)] + copy_shape + [jax.ShapeDtypeStruct((1, d), F32)],
        compiler_params=_params(("arbitrary",), 7 * ts * d * 4))(x, gain, dh, dres)
    return (res[0], res[1], res[2]) if bf16_copy else (res[0], None, res[1])


def gate_merge_fwd(name, proj, ta, tb, d):
    s = proj.shape[0]
    ts = _pick(s, ROW_TILE_CANDS)
    cb = COL_GATES // d

    def body(ga_ref, gb_ref, ta_ref, tb_ref, o_ref):
        o_ref[...] = (_sigmoid(ga_ref[...]) * ta_ref[...] + _sigmoid(gb_ref[...]) * tb_ref[...]).astype(o_ref.dtype)

    row = pl.BlockSpec((ts, d), lambda i: (i, 0))
    return pl.pallas_call(
        body, name=name, grid=(s // ts,),
        in_specs=[pl.BlockSpec((ts, d), lambda i: (i, cb)), pl.BlockSpec((ts, d), lambda i: (i, cb + 1)), row, row],
        out_specs=row, out_shape=pltpu.HBM((s, d), BF16),
        compiler_params=_params(("parallel",), 5 * ts * d * 4))(proj, proj, ta, tb)


def gate_merge_bwd(name, dmerged, proj, ta, tb, d):
    s = proj.shape[0]
    ts = _pick(s, ROW_TILE_CANDS)
    cb = COL_GATES // d

    def body(dm_ref, ga_ref, gb_ref, ta_ref, tb_ref, dta_ref, dtb_ref, dga_ref, dgb_ref):
        dm = dm_ref[...]
        sa = _sigmoid(ga_ref[...])
        sb = _sigmoid(gb_ref[...])
        dta_ref[...] = (dm * sa).astype(dta_ref.dtype)
        dtb_ref[...] = (dm * sb).astype(dtb_ref.dtype)
        dga_ref[...] = (dm * ta_ref[...] * (sa * (1.0 - sa))).astype(dga_ref.dtype)
        dgb_ref[...] = (dm * tb_ref[...] * (sb * (1.0 - sb))).astype(dgb_ref.dtype)

    row = pl.BlockSpec((ts, d), lambda i: (i, 0))
    out = pltpu.HBM((s, d), BF16)
    return pl.pallas_call(
        body, name=name, grid=(s // ts,),
        in_specs=[row, pl.BlockSpec((ts, d), lambda i: (i, cb)), pl.BlockSpec((ts, d), lambda i: (i, cb + 1)), row, row],
        out_specs=[row, row, row, row], out_shape=[out, out, out, out],
        compiler_params=_params(("parallel",), 8 * ts * d * 4))(dmerged, proj, proj, ta, tb)


def tail_fwd_bwd(name, x2, lp, pp, gain, target):
    s, d = x2.shape
    ts = _pick(s, ROW_TILE_CANDS)

    def body(x2_ref, lp_ref, pp_ref, g_ref, t_ref, loss_ref, dx3_ref, dlp_ref, dpp_ref, dg_ref):
        gp = _sigmoid(lp_ref[...])
        ppv = pp_ref[...]
        x3 = x2_ref[...] + gp * ppv
        r = _rstd(x3)
        xhat = x3 * r
        gv = g_ref[...]
        err = xhat * gv - t_ref[...]
        loss = jnp.sum(err * err) * (0.5 / d)
        dy = err * (1.0 / d)
        dxhat = dy * gv
        dx3 = r * (dxhat - xhat * jnp.mean(dxhat * xhat, axis=-1, keepdims=True))
        dx3_ref[...] = dx3
        dlp_ref[...] = (dx3 * ppv * (gp * (1.0 - gp))).astype(dlp_ref.dtype)
        dpp_ref[...] = (dx3 * gp).astype(dpp_ref.dtype)
        part = jnp.sum(dy * xhat, axis=0, keepdims=True)
        lossv = jnp.full((1, LANES), loss, F32)

        @pl.when(pl.program_id(0) == 0)
        def _():
            dg_ref[...] = part
            loss_ref[...] = lossv

        @pl.when(pl.program_id(0) > 0)
        def _():
            dg_ref[...] += part
            loss_ref[...] += lossv

    row = pl.BlockSpec((ts, d), lambda i: (i, 0))
    vec = pl.BlockSpec((1, d), lambda i: (0, 0))
    return pl.pallas_call(
        body, name=name, grid=(s // ts,), in_specs=[row, row, row, vec, row],
        out_specs=[pl.BlockSpec((1, LANES), lambda i: (0, 0)), row, row, row, vec],
        out_shape=[jax.ShapeDtypeStruct((1, LANES), F32), pltpu.HBM((s, d), F32),
                   pltpu.HBM((s, d), BF16), pltpu.HBM((s, d), BF16),
                   jax.ShapeDtypeStruct((1, d), F32)],
        compiler_params=_params(("arbitrary",), 9 * ts * d * 4))(x2, lp, pp, gain, target)


HALO = SUBLANES
BF16_ROWS = 2 * SUBLANES


def _shift_rows(cur, prev_row, next_row):
    ts = cur.shape[0]
    rid = lax.broadcasted_iota(jnp.int32, cur.shape, 0)
    down = jnp.where(rid == 0, prev_row, pltpu.roll(cur, 1, 0))
    up = jnp.where(rid == ts - 1, next_row, pltpu.roll(cur, ts - 1, 0))
    return down, up


def _halo_specs(ts, s, nf, halo=HALO):
    nb = ts // halo
    last = s // halo - 1
    cur = pl.BlockSpec((None, ts, nf), lambda j, i: (j, i, 0))
    prev = pl.BlockSpec((None, halo, nf), lambda j, i: (j, jnp.maximum(i * nb - 1, 0), 0))
    nxt = pl.BlockSpec((None, halo, nf), lambda j, i: (j, jnp.minimum((i + 1) * nb, last), 0))
    return cur, prev, nxt


def _halo_rows(prev_ref, next_ref, n_tiles):
    i = pl.program_id(1)
    prev_row = jnp.where(i == 0, 0.0, prev_ref[HALO - 1:HALO, :].astype(F32))
    next_row = jnp.where(i == n_tiles - 1, 0.0, next_ref[0:1, :].astype(F32))
    return prev_row, next_row


def _gelu(g):
    t = jnp.tanh(GELU_C * (g + GELU_A * (g * g * g)))
    return 0.5 * g * (1.0 + t), t


def _conv(cur, down, up, cw_ref, cb_ref):
    return down * cw_ref[0:1, :] + cur * cw_ref[1:2, :] + up * cw_ref[2:3, :] + cb_ref[...]


def ffn_mid_fwd(name, gpre, u, cw, cb):
    nj, s, nf = gpre.shape
    ts = _pick(s, (512, 256, 128, 64, 32, 16, 8))
    n_tiles = s // ts
    cur, prev, nxt = _halo_specs(ts, s, nf)

    def body(g_ref, gp_ref, gn_ref, u_ref, cw_ref, cb_ref, z_ref):
        gv = g_ref[...]
        down, up = _shift_rows(gv, *_halo_rows(gp_ref, gn_ref, n_tiles))
        act, _ = _gelu(_conv(gv, down, up, cw_ref, cb_ref))
        z_ref[...] = (act * u_ref[...]).astype(z_ref.dtype)

    return pl.pallas_call(
        body, name=name, grid=(nj, n_tiles),
        in_specs=[cur, prev, nxt, cur, pl.BlockSpec((None, SUBLANES, nf), lambda j, i: (j, 0, 0)),
                  pl.BlockSpec((None, 1, nf), lambda j, i: (j, 0, 0))],
        out_specs=cur, out_shape=pltpu.HBM((nj, s, nf), BF16),
        compiler_params=_params(("parallel", "parallel"), 8 * ts * nf * 4))(gpre, gpre, gpre, u, cw, cb)


def _gelu_grad(g, t):
    return 0.5 * (1.0 + t) + 0.5 * g * (1.0 - t * t) * (GELU_C * (1.0 + 3.0 * GELU_A * (g * g)))


def ffn_mid_bwd(name, gpre, u, dz, cw, cb):
    nj, s, nf = gpre.shape
    ts = _pick(s, (512, 256, 128, 64, 32, 16, 8))
    n_tiles = s // ts
    cur, prev, nxt = _halo_specs(ts, s, nf)

    def body(g_ref, gp_ref, gn_ref, u_ref, up_ref, un_ref, dz_ref, dzp_ref, dzn_ref, cw_ref, cb_ref,
             du_ref, dgp_ref, dcw_ref):
        i = pl.program_id(1)
        w0, w1, w2, bias = cw_ref[0:1, :], cw_ref[1:2, :], cw_ref[2:3, :], cb_ref[...]
        gv = g_ref[...]
        down, up = _shift_rows(gv, *_halo_rows(gp_ref, gn_ref, n_tiles))
        gc = down * w0 + gv * w1 + up * w2 + bias
        act, t = _gelu(gc)
        dzv = dz_ref[...].astype(F32)
        du_ref[...] = (dzv * act).astype(du_ref.dtype)
        dg = dzv * u_ref[...] * _gelu_grad(gc, t)

        def edge_dg(g_before, g_at, g_after, u_at, dz_at):
            ge = g_before.astype(F32) * w0 + g_at.astype(F32) * w1 + g_after.astype(F32) * w2 + bias
            return dz_at.astype(F32) * u_at.astype(F32) * _gelu_grad(ge, _gelu(ge)[1])

        dz_before = dzp_ref[...].astype(F32)[BF16_ROWS - 1:BF16_ROWS, :]
        dz_after = dzn_ref[...].astype(F32)[0:1, :]
        dg_prev = jnp.where(i == 0, 0.0, edge_dg(gp_ref[HALO - 2:HALO - 1, :], gp_ref[HALO - 1:HALO, :], gv[0:1, :],
                                                 up_ref[HALO - 1:HALO, :], dz_before))
        dg_next = jnp.where(i == n_tiles - 1, 0.0, edge_dg(gv[ts - 1:ts, :], gn_ref[0:1, :], gn_ref[1:2, :],
                                                           un_ref[0:1, :], dz_after))
        dg_down, dg_up = _shift_rows(dg, dg_prev, dg_next)
        dgp_ref[...] = (dg_up * w0 + dg * w1 + dg_down * w2).astype(dgp_ref.dtype)
        rows = [jnp.sum(dg * down, axis=0, keepdims=True), jnp.sum(dg * gv, axis=0, keepdims=True),
                jnp.sum(dg * up, axis=0, keepdims=True), jnp.sum(dg, axis=0, keepdims=True)]
        part = jnp.concatenate(rows + [jnp.zeros((SUBLANES - len(rows), nf), F32)], axis=0)

        @pl.when(i == 0)
        def _():
            dcw_ref[...] = part

        @pl.when(i > 0)
        def _():
            dcw_ref[...] += part

    small = pl.BlockSpec((None, SUBLANES, nf), lambda j, i: (j, 0, 0))
    return pl.pallas_call(
        body, name=name, grid=(nj, n_tiles),
        in_specs=[cur, prev, nxt] * 2 + list(_halo_specs(ts, s, nf, BF16_ROWS))
        + [small, pl.BlockSpec((None, 1, nf), lambda j, i: (j, 0, 0))],
        out_specs=[cur, cur, small],
        out_shape=[pltpu.HBM((nj, s, nf), BF16), pltpu.HBM((nj, s, nf), BF16),
                   jax.ShapeDtypeStruct((nj, SUBLANES, nf), F32)],
        compiler_params=_params(("parallel", "arbitrary"), 14 * ts * nf * 4))(
            gpre, gpre, gpre, u, u, u, dz, dz, dz, cw, cb)


def _t5_bucket(rel):
    half = N_BUCKETS // 2
    max_exact = half // 2
    n = jnp.abs(rel)
    side = jnp.where(rel > 0, half, 0)
    nf = jnp.maximum(n, 1).astype(F32)
    large = max_exact + (jnp.log(nf / max_exact) / math.log(MAX_DISTANCE / max_exact)
                         * (half - max_exact)).astype(jnp.int32)
    large = jnp.minimum(large, half - 1)
    return side + jnp.where(n < max_exact, n, large)


def bucket_tile(rows, half, dil):
    rel = (jnp.arange(rows + 2 * half)[None, :] - half) - jnp.arange(rows)[:, None]
    return _t5_bucket(rel * dil).astype(jnp.int32)


def bias_build(name, table_t, bucket, h0, nh, half):
    blk, kw = bucket.shape

    def body(t_ref, b_ref, o_ref):
        h = pl.program_id(0)
        bv = b_ref[...]
        acc = jnp.zeros((blk, kw), F32)
        for b in range(N_BUCKETS):
            acc = jnp.where(bv == b, t_ref[h0 + h, b], acc)
        qi = lax.broadcasted_iota(jnp.int32, (blk, kw), 0)
        ci = lax.broadcasted_iota(jnp.int32, (blk, kw), 1)
        o_ref[...] = jnp.where(jnp.abs(ci - half - qi) <= half, acc, NEG_INF)

    return pl.pallas_call(
        body, name=name, grid=(nh,),
        in_specs=[pl.BlockSpec(memory_space=pltpu.SMEM), pl.BlockSpec((blk, kw), lambda h: (0, 0))],
        out_specs=pl.BlockSpec((None, blk, kw), lambda h: (h, 0, 0)),
        out_shape=jax.ShapeDtypeStruct((nh, blk, kw), F32),
        compiler_params=_params(("parallel",), 4 * blk * kw * 4))(table_t, bucket)


def table_grad(name, dbias, bucket):
    nh, blk, kw = dbias.shape

    def body(d_ref, b_ref, o_ref):
        bv = b_ref[...]
        dv = d_ref[...]
        lane = lax.broadcasted_iota(jnp.int32, (SUBLANES, LANES), 1)
        acc = jnp.zeros((SUBLANES, LANES), F32)
        for b in range(N_BUCKETS):
            acc = jnp.where(lane == b, jnp.sum(jnp.where(bv == b, dv, 0.0)), acc)
        o_ref[...] = acc

    return pl.pallas_call(
        body, name=name, grid=(nh,),
        in_specs=[pl.BlockSpec((None, blk, kw), lambda h: (h, 0, 0)), pl.BlockSpec((blk, kw), lambda h: (0, 0))],
        out_specs=pl.BlockSpec((None, SUBLANES, LANES), lambda h: (h, 0, 0)),
        out_shape=jax.ShapeDtypeStruct((nh, SUBLANES, LANES), F32),
        compiler_params=_params(("parallel",), 4 * blk * kw * 4))(dbias, bucket)


class _Band:
    def __init__(self, s, half, q_rows, n_chains, dil):
        self.s, self.half, self.dil, self.n_chains = s, half, dil, n_chains
        self.seg = s // dil
        self.q_rows = min(q_rows, self.seg)
        self.win = self.q_rows + 2 * half
        self.pad = self.seg + 2 * half
        self.nsb = self.seg // self.q_rows
        self.n_items = dil * self.nsb
        assert self.n_items % n_chains == 0 and self.seg % self.q_rows == 0
        self.staged = dil > 1

    def rows_of(self, r):
        return pl.ds(r, self.seg, stride=self.dil) if self.dil > 1 else slice(None)

    def stage_kv(self, dst, src_ref):
        zeros = jnp.zeros((self.half, HEAD_DIM), dst.dtype)
        for r in range(self.dil):
            base = r * self.pad
            dst[base:base + self.half, :] = zeros
            dst[base + self.half + self.seg:base + self.pad, :] = zeros
            dst[base + self.half:base + self.half + self.seg, :] = src_ref[self.rows_of(r), :].astype(dst.dtype)

    def stage(self, dst, src_ref):
        for r in range(self.dil):
            dst[r * self.seg:(r + 1) * self.seg, :] = src_ref[self.rows_of(r), :].astype(dst.dtype)

    def unstage(self, dst_ref, src, add=False):
        for r in range(self.dil):
            val = src[r * self.seg:(r + 1) * self.seg, :].astype(dst_ref.dtype)
            if add:
                val = val + dst_ref[self.rows_of(r), :]
            dst_ref[self.rows_of(r), :] = val

    def offsets(self, item):
        r, sb = item // self.nsb, item % self.nsb
        qoff = pl.multiple_of(r * self.seg + sb * self.q_rows, self.q_rows)
        koff = pl.multiple_of(r * self.pad + sb * self.q_rows, B_BLOCK)
        kpos = sb * self.q_rows - self.half + lax.broadcasted_iota(jnp.int32, (1, self.win), 1)
        edge = jnp.where((kpos >= 0) & (kpos < self.seg), 0.0, NEG_INF)
        return qoff, koff, edge


def band_attn_fwd(name, proj, bias, sink, *, half, q_rows, n_chains, dil, nh, group, cq, ck, cv):
    s, w = proj.shape
    g = _Band(s, half, q_rows, n_chains, dil)
    has_sink = sink is not None

    def body(*refs):
        q_ref, k_ref, v_ref, b_ref = refs[:4]
        s_ref = refs[4] if has_sink else None
        o_ref, l_ref, ks, vs = refs[4 + has_sink:8 + has_sink]
        qs, os_, ls = refs[8 + has_sink:] if g.staged else (None, o_ref, l_ref)
        g.stage_kv(ks, k_ref)
        g.stage_kv(vs, v_ref)
        if g.staged:
            g.stage(qs, q_ref)
        bias_v = b_ref[...]
        sk = s_ref[pl.program_id(0)] if has_sink else None

        def chain(item):
            qoff, koff, edge = g.offsets(item)
            rows = pl.ds(qoff, g.q_rows)
            qv = qs[rows, :] if g.staged else q_ref[rows, :].astype(BF16)
            kw_ = ks[pl.ds(koff, g.win), :]
            vw_ = vs[pl.ds(koff, g.win), :]
            sc = lax.dot_general(qv, kw_, (((1,), (1,)), ((), ())), preferred_element_type=F32) * ATTN_SCALE
            sc = sc + bias_v + edge
            m = jnp.max(sc, axis=-1, keepdims=True)
            if has_sink:
                m = jnp.maximum(m, sk)
            p = jnp.exp(sc - m)
            den = jnp.sum(p, axis=-1, keepdims=True)
            if has_sink:
                den = den + jnp.exp(sk - m)
            out = lax.dot_general(p.astype(BF16), vw_, (((1,), (0,)), ((), ())), preferred_element_type=F32)
            return rows, out / den, jnp.broadcast_to(m + jnp.log(den), (g.q_rows, HEAD_DIM))

        def step(i, carry):
            for rows, out, lse in [chain(i * n_chains + u) for u in range(n_chains)]:
                os_[rows, :] = out
                ls[rows, :] = lse
            return carry

        lax.fori_loop(0, g.n_items // n_chains, step, 0)
        if g.staged:
            g.unstage(o_ref, os_)
            g.unstage(l_ref, ls)

    def col(c0, per):
        return pl.BlockSpec((s, HEAD_DIM), lambda h: (0, c0 // LANES + h // per))

    in_specs = [col(cq, 1), col(ck, group), col(cv, group),
                pl.BlockSpec((None, g.q_rows, g.win), lambda h: (h, 0, 0))]
    args = [proj, proj, proj, bias]
    if has_sink:
        in_specs.append(pl.BlockSpec(memory_space=pltpu.SMEM))
        args.append(sink)
    shape = pltpu.HBM((s, nh * HEAD_DIM), F32)
    scratch = [pltpu.VMEM((dil * g.pad, HEAD_DIM), BF16), pltpu.VMEM((dil * g.pad, HEAD_DIM), BF16)]
    if g.staged:
        scratch += [pltpu.VMEM((s, HEAD_DIM), BF16), pltpu.VMEM((s, HEAD_DIM), F32), pltpu.VMEM((s, HEAD_DIM), F32)]
    return pl.pallas_call(
        body, name=name, grid=(nh,), in_specs=in_specs, out_specs=[col(0, 1), col(0, 1)], out_shape=[shape, shape],
        scratch_shapes=scratch, compiler_params=_params(("parallel",), 16 * s * HEAD_DIM * 4))(*args)


def band_attn_bwd(name, proj, bias, sink, dout, out, lse, dlse, *, half, q_rows, n_chains, dil, nh, group, cq, ck, cv):
    s, w = proj.shape
    g = _Band(s, half, q_rows, n_chains, dil)
    nkv = nh // group
    has_sink = sink is not None
    has_dl = dlse is not None
    n_in = 7 + int(has_sink) + int(has_dl)
    n_out = 4 + int(has_sink)

    def body(*refs):
        ins, outs, scr = refs[:n_in], refs[n_in:n_in + n_out], refs[n_in + n_out:]
        q_ref, k_ref, v_ref, b_ref, do_ref, o_ref, l_ref = ins[:7]
        s_ref = ins[7] if has_sink else None
        dl_ref = ins[n_in - 1] if has_dl else None
        dq_ref, dk_ref, dv_ref, db_ref = outs[:4]
        ks, vs, dks, dvs = scr[:4]
        scr = list(scr[4:])
        dsa = scr.pop(0) if has_sink else None
        if g.staged:
            qs, dos, os_, ls, dqs = scr[:5]
            dls = scr[5] if has_dl else None
            g.stage(qs, q_ref)
            g.stage(dos, do_ref)
            g.stage(os_, o_ref)
            g.stage(ls, l_ref)
            if has_dl:
                g.stage(dls, dl_ref)
        else:
            qs, dos, os_, ls, dqs, dls = None, do_ref, o_ref, l_ref, dq_ref, dl_ref
        h = pl.program_id(0)
        g.stage_kv(ks, k_ref)
        g.stage_kv(vs, v_ref)
        dks[...] = jnp.zeros_like(dks)
        dvs[...] = jnp.zeros_like(dvs)
        db_ref[...] = jnp.zeros_like(db_ref)
        bias_v = b_ref[...]
        if has_sink:
            sk = s_ref[h]
            dsa[...] = jnp.zeros_like(dsa)

        def chain(item):
            qoff, koff, edge = g.offsets(item)
            rows = pl.ds(qoff, g.q_rows)
            win = pl.ds(koff, g.win)
            qv = qs[rows, :] if g.staged else q_ref[rows, :].astype(BF16)
            kw_ = ks[win, :]
            vw_ = vs[win, :]
            sc = lax.dot_general(qv, kw_, (((1,), (1,)), ((), ())), preferred_element_type=F32) * ATTN_SCALE
            lv = ls[rows, :][:, 0:1]
            p = jnp.exp(sc + bias_v + edge - lv)
            dov = dos[rows, :]
            delta = jnp.sum(dov * os_[rows, :], axis=-1, keepdims=True)
            dob = dov.astype(BF16)
            dp = lax.dot_general(dob, vw_, (((1,), (1,)), ((), ())), preferred_element_type=F32)
            t = dp - delta
            if has_dl:
                t = t + dls[rows, :][:, 0:1]
            ds = p * t
            dsb = (ds * ATTN_SCALE).astype(BF16)
            dq = lax.dot_general(dsb, kw_, (((1,), (0,)), ((), ())), preferred_element_type=F32)
            dkc = lax.dot_general(dsb, qv, (((0,), (0,)), ((), ())), preferred_element_type=F32)
            dvc = lax.dot_general(p.astype(BF16), dob, (((0,), (0,)), ((), ())), preferred_element_type=F32)
            dsk = jnp.exp(sk - lv) * delta if has_sink else None
            return rows, win, dq, dkc, dvc, ds, dsk

        def step(i, carry):
            res = [chain(i * n_chains + u) for u in range(n_chains)]
            ds_sum = res[0][5]
            for rr in res[1:]:
                ds_sum = ds_sum + rr[5]
            db_ref[...] += ds_sum
            for rows, win, dq, dkc, dvc, ds, dsk in res:
                dqs[rows, :] = dq
                dks[win, :] += dkc
                dvs[win, :] += dvc
                if has_sink:
                    dsa[...] += dsk
            return carry

        lax.fori_loop(0, g.n_items // n_chains, step, 0)

        if g.staged:
            g.unstage(dq_ref, dqs)

        def emit_kv(add):
            for r in range(dil):
                lo = r * g.pad + half
                for dst_ref, src in ((dk_ref, dks), (dv_ref, dvs)):
                    val = src[lo:lo + g.seg, :]
                    if add:
                        val = val + dst_ref[g.rows_of(r), :]
                    dst_ref[g.rows_of(r), :] = val

        if group == 1:
            emit_kv(False)
        else:
            @pl.when(h % group == 0)
            def _():
                emit_kv(False)

            @pl.when(h % group != 0)
            def _():
                emit_kv(True)
        if has_sink:
            outs[4][...] = jnp.full((SUBLANES, LANES), -jnp.sum(dsa[...]), F32)

    def col(c0, per):
        return pl.BlockSpec((s, HEAD_DIM), lambda h: (0, c0 // LANES + h // per))

    b_spec = pl.BlockSpec((None, g.q_rows, g.win), lambda h: (h, 0, 0))
    in_specs = [col(cq, 1), col(ck, group), col(cv, group), b_spec, col(0, 1), col(0, 1), col(0, 1)]
    args = [proj, proj, proj, bias, dout, out, lse]
    if has_sink:
        in_specs.append(pl.BlockSpec(memory_space=pltpu.SMEM))
        args.append(sink)
    if has_dl:
        in_specs.append(col(0, 1))
        args.append(dlse)
    out_specs = [col(0, 1), col(0, group), col(0, group), b_spec]
    out_shape = [pltpu.HBM((s, nh * HEAD_DIM), F32), pltpu.HBM((s, nkv * HEAD_DIM), F32),
                 pltpu.HBM((s, nkv * HEAD_DIM), F32), jax.ShapeDtypeStruct((nh, g.q_rows, g.win), F32)]
    scratch = [pltpu.VMEM((dil * g.pad, HEAD_DIM), BF16), pltpu.VMEM((dil * g.pad, HEAD_DIM), BF16),
               pltpu.VMEM((dil * g.pad, HEAD_DIM), F32), pltpu.VMEM((dil * g.pad, HEAD_DIM), F32)]
    if has_sink:
        out_specs.append(pl.BlockSpec((None, SUBLANES, LANES), lambda h: (h, 0, 0)))
        out_shape.append(jax.ShapeDtypeStruct((nh, SUBLANES, LANES), F32))
        scratch.append(pltpu.VMEM((g.q_rows, 1), F32))
    if g.staged:
        scratch += [pltpu.VMEM((s, HEAD_DIM), BF16)] + [pltpu.VMEM((s, HEAD_DIM), F32)] * (4 + int(has_dl))
    res = pl.pallas_call(
        body, name=name, grid=(nh,), in_specs=in_specs, out_specs=out_specs, out_shape=out_shape,
        scratch_shapes=scratch, compiler_params=_params(("arbitrary",), 28 * s * HEAD_DIM * 4))(*args)
    return res[0], res[1], res[2], res[3], (res[4] if has_sink else None)


def dil_merge_fwd(name, outs, lses):
    s, w = outs[0].shape
    ts = _pick(s, ROW_TILE_CANDS)
    ng = len(outs)

    def body(*refs):
        o_refs, l_refs, y_ref = refs[:ng], refs[ng:2 * ng], refs[2 * ng]
        ls = [l[...] for l in l_refs]
        mx = ls[0]
        for l in ls[1:]:
            mx = jnp.maximum(mx, l)
        es = [jnp.exp(l - mx) for l in ls]
        tot = es[0]
        for e in es[1:]:
            tot = tot + e
        acc = (es[0] / tot) * o_refs[0][...]
        for e, o in zip(es[1:], o_refs[1:]):
            acc = acc + (e / tot) * o[...]
        y_ref[...] = acc.astype(y_ref.dtype)

    row = pl.BlockSpec((ts, w), lambda i: (i, 0))
    return pl.pallas_call(
        body, name=name, grid=(s // ts,), in_specs=[row] * (2 * ng), out_specs=row,
        out_shape=pltpu.HBM((s, w), BF16),
        compiler_params=_params(("parallel",), 10 * ts * w * 4))(*outs, *lses)


def dil_merge_bwd(name, dy, outs, lses):
    s, w = outs[0].shape
    ts = _pick(s, ROW_TILE_CANDS)
    ng = len(outs)
    nhead = w // HEAD_DIM

    def body(*refs):
        dy_ref = refs[0]
        o_refs, l_refs = refs[1:1 + ng], refs[1 + ng:1 + 2 * ng]
        do_refs, dl_refs = refs[1 + 2 * ng:1 + 3 * ng], refs[1 + 3 * ng:1 + 4 * ng]
        for hh in range(nhead):
            cols = slice(hh * HEAD_DIM, (hh + 1) * HEAD_DIM)
            dyv = dy_ref[:, cols]
            ls = [l[:, cols] for l in l_refs]
            mx = ls[0]
            for l in ls[1:]:
                mx = jnp.maximum(mx, l)
            es = [jnp.exp(l - mx) for l in ls]
            tot = es[0]
            for e in es[1:]:
                tot = tot + e
            alphas = [e / tot for e in es]
            dal = [jnp.broadcast_to(jnp.sum(dyv * o[:, cols], axis=-1, keepdims=True), dyv.shape) for o in o_refs]
            mean = alphas[0] * dal[0]
            for a, d in zip(alphas[1:], dal[1:]):
                mean = mean + a * d
            for g in range(ng):
                do_refs[g][:, cols] = alphas[g] * dyv
                dl_refs[g][:, cols] = alphas[g] * (dal[g] - mean)

    row = pl.BlockSpec((ts, w), lambda i: (i, 0))
    shape = pltpu.HBM((s, w), F32)
    res = pl.pallas_call(
        body, name=name, grid=(s // ts,), in_specs=[row] * (1 + 2 * ng), out_specs=[row] * (2 * ng),
        out_shape=[shape] * (2 * ng),
        compiler_params=_params(("parallel",), 16 * ts * w * 4))(dy, *outs, *lses)
    return res[:ng], res[ng:]


def _adamw(w, g, m, v):
    m = ADAM_B1 * m + (1.0 - ADAM_B1) * g
    v = ADAM_B2 * v + (1.0 - ADAM_B2) * (g * g)
    m_hat = m / (1.0 - ADAM_B1 ** ADAM_STEP)
    v_hat = v / (1.0 - ADAM_B2 ** ADAM_STEP)
    delta = -ADAM_LR * (m_hat / (jnp.sqrt(v_hat) + ADAM_EPS) + ADAM_WD * w)
    return delta, m, v


def _row_tile(r, c, budget=1 << 20):
    if r * c * 4 <= budget or r % SUBLANES:
        return r
    for t in (1024, 512, 256, 128, 64, 32, 16, 8):
        if r % t == 0 and t * c * 4 <= budget:
            return t
    return SUBLANES


def adam_small(name, g, w, m, v):
    def body(g_ref, w_ref, m_ref, v_ref, d_ref, nm_ref, nv_ref):
        d_ref[...], nm_ref[...], nv_ref[...] = _adamw(w_ref[...], g_ref[...], m_ref[...], v_ref[...])

    shape = jax.ShapeDtypeStruct(w.shape, F32)
    return pl.pallas_call(body, name=name, out_shape=[shape, shape, shape])(g, w, m, v)


def reduce_adam(name, mine, theirs, w, m, v):
    nq, r, c = mine.shape
    tr = _row_tile(r, c)

    def body(*refs):
        parts, (w_ref, m_ref, v_ref, g_ref, d_ref, nm_ref, nv_ref) = refs[:nq], refs[nq:]
        g = parts[0][...].astype(F32)
        for p_ref in parts[1:]:
            g = g + p_ref[...].astype(F32)
        g_ref[...] = g
        d_ref[...], nm_ref[...], nv_ref[...] = _adamw(w_ref[...], g, m_ref[...], v_ref[...])

    def slot(q):
        return pl.BlockSpec((None, tr, c), lambda i: (q, i, 0))

    row = pl.BlockSpec((tr, c), lambda i: (i, 0))
    shape = jax.ShapeDtypeStruct((r, c), F32)
    return pl.pallas_call(
        body, name=name, grid=(r // tr,), in_specs=[slot(q) for q in range(nq)] + [row, row, row],
        out_specs=[row] * 4, out_shape=[shape] * 4,
        compiler_params=_params(("parallel",), (nq * 2 + 7 * 4) * tr * c))(mine, *[theirs] * (nq - 1), *[_in_hbm(t) for t in (w, m, v)])


def _place():
    return lax.axis_index("x"), lax.axis_index("y"), lax.axis_index("c")


def _flip(pos, bits):
    return tuple((1 - p) if b else p for p, b in zip(pos, bits))


def _index(pos):
    return 4 * pos[0] + 2 * pos[1] + pos[2]


ANY = pl.BlockSpec(memory_space=pl.ANY)


HBM = pl.BlockSpec(memory_space=pltpu.HBM)
SEM = pl.BlockSpec(memory_space=pltpu.SEMAPHORE)
EFFECT = pltpu.SideEffectType.DATAFLOW_SIDE_EFFECTING
TO_SIBLING = (0, 0, 1)
TO_CHIPS = [(1, 0, 0), (0, 1, 0), (1, 1, 0)]


def _in_hbm(a):
    return pltpu.with_memory_space_constraint(a, pltpu.HBM)


def _token_value(token):
    return token[0, 0]


def _when(pred, fn):
    if pred is True:
        fn()
    elif pred is not False:
        pl.when(pred)(fn)


def _plan_copy(k, entry, ins, lnd, send_sems, recv_sems):
    a, src_a, sblk, lblk, to, send_if, recv_if = entry
    src = lnd[a] if src_a is None else ins[src_a]
    return pltpu.make_async_remote_copy(
        src_ref=src.at[sblk], dst_ref=lnd[a].at[lblk], send_sem=send_sems.at[k], recv_sem=recv_sems.at[k],
        device_id=to, device_id_type=MESH), send_if, recv_if


def split_start(name, srcs, lands, plan, after):
    ns, nl = len(srcs), len(lands)
    n_copies = len(plan((0, 0, 0)))

    def body(*refs):
        ins, lnd = refs[:ns], refs[ns:ns + nl]
        send_sems, recv_sems = refs[ns + nl + 1], refs[ns + nl + 2]
        token = refs[-1]
        for k, entry in enumerate(plan(_place())):
            cp, send_if, _ = _plan_copy(k, entry, ins, lnd, send_sems, recv_sems)
            _when(send_if, cp.start)
        token[...] = jnp.zeros_like(token)

    outs = pl.pallas_call(
        body, name=name,
        out_shape=(pltpu.SemaphoreType.DMA((n_copies,)), pltpu.SemaphoreType.DMA((n_copies,)),
                   *[pltpu.HBM(a.shape, a.dtype) for a in srcs], *[pltpu.HBM(a.shape, a.dtype) for a in lands],
                   jax.ShapeDtypeStruct((SUBLANES, LANES), F32)),
        in_specs=[HBM] * (ns + nl) + [ANY],
        out_specs=(SEM, SEM, *[HBM] * (ns + nl), pl.BlockSpec(memory_space=pltpu.VMEM)),
        input_output_aliases={i: 2 + i for i in range(ns + nl)},
        compiler_params=pltpu.CompilerParams(has_side_effects=EFFECT),
    )(*[_in_hbm(a) for a in srcs], *[_in_hbm(a) for a in lands], after)
    return outs[0], outs[1], list(outs[2:2 + ns]), list(outs[2 + ns:2 + ns + nl]), outs[-1]


def split_wait(name, send_sems, recv_sems, srcs, lands, plan, after):
    ns, nl = len(srcs), len(lands)

    def body(*refs):
        ins, lnd = refs[:ns], refs[ns:ns + nl]
        s_sems, r_sems = refs[ns + nl], refs[ns + nl + 1]
        for k, entry in enumerate(plan(_place())):
            cp, send_if, recv_if = _plan_copy(k, entry, ins, lnd, s_sems, r_sems)
            _when(send_if, cp.wait_send)
            _when(recv_if, cp.wait_recv)
        refs[-1][...] = jnp.zeros((SUBLANES, LANES), F32)

    outs = pl.pallas_call(
        body, name=name,
        out_shape=(*[pltpu.HBM(a.shape, a.dtype) for a in srcs], *[pltpu.HBM(a.shape, a.dtype) for a in lands],
                   jax.ShapeDtypeStruct((SUBLANES, LANES), F32)),
        in_specs=[HBM] * (ns + nl) + [SEM, SEM, ANY],
        out_specs=(*[HBM] * (ns + nl), pl.BlockSpec(memory_space=pltpu.VMEM)),
        input_output_aliases={i: i for i in range(ns + nl)},
        compiler_params=pltpu.CompilerParams(has_side_effects=EFFECT),
    )(*srcs, *lands, send_sems, recv_sems, after)
    return list(outs[:ns]), list(outs[ns:ns + nl]), outs[-1]


NORTH = 1


def ag_plan(n, rels=TO_CHIPS):
    def plan(me):
        x, y, c = me
        entries = []
        for a in range(n):
            for t in (NORTH, 1 - NORTH):
                blk = _index((x, y, t))
                for rel in rels:
                    entries.append((a, None, blk, blk, _flip((x, y, t), rel), c == NORTH, c == t))
        return entries
    return plan


TO_X, TO_Y = TO_CHIPS[0], TO_CHIPS[1]


def relay_plan(n):
    def plan(me):
        x, y, c = me
        entries = []
        for a in range(n):
            for t, came, goes in ((NORTH, TO_X, TO_Y), (1 - NORTH, TO_Y, TO_X)):
                blk = _index(_flip((x, y, t), came))
                entries.append((a, None, blk, blk, _flip((x, y, t), goes), c == t, c == t))
        return entries
    return plan


def ag_pair(name, lands, after):
    n = len(lands)

    def body(*refs):
        lnd = refs[n + 1:2 * n + 1]
        token = refs[2 * n + 1]
        send_sems, recv_sems = refs[2 * n + 2:]
        token[...] = jnp.zeros_like(token)
        me = _place()
        sibling = _flip(me, TO_SIBLING)
        copies = []
        for a in range(n):
            mine, theirs = lnd[a].at[_index(me)], lnd[a].at[_index(sibling)]
            cp = pltpu.make_async_remote_copy(src_ref=mine, dst_ref=mine, send_sem=send_sems.at[a],
                                              recv_sem=recv_sems.at[a], device_id=sibling, device_id_type=MESH)
            cp.start()
            copies.append((cp, pltpu.make_async_remote_copy(
                src_ref=mine, dst_ref=theirs, send_sem=send_sems.at[a], recv_sem=recv_sems.at[a], device_id=sibling,
                device_id_type=MESH)))
        for cp, arrival in copies:
            arrival.wait_recv()
        for cp, arrival in copies:
            cp.wait_send()

    outs = pl.pallas_call(
        body, name=name, in_specs=[ANY] * (n + 1), out_specs=[ANY] * n + [pl.BlockSpec(memory_space=pltpu.VMEM)],
        out_shape=[jax.ShapeDtypeStruct(l.shape, l.dtype) for l in lands]
        + [jax.ShapeDtypeStruct((SUBLANES, LANES), F32)],
        input_output_aliases={a: a for a in range(n)},
        scratch_shapes=[pltpu.SemaphoreType.DMA((n,)), pltpu.SemaphoreType.DMA((n,))],
    )(*lands, after)
    return list(outs[:n]), outs[n]


def pass_plan(n):
    def plan(me):
        sibling = _flip(me, TO_SIBLING)
        return [(a, None, _index(_flip(me, rel)), _index(_flip(me, rel)), sibling, True, True)
                for a in range(n) for rel in TO_CHIPS]
    return plan


def ag_finish(name, lands):
    n = len(lands)

    def body(*refs):
        lnd = refs[n:2 * n]
        send_sems, recv_sems = refs[2 * n:]
        me = _place()
        sibling = _flip(me, TO_SIBLING)
        copies = []
        for a in range(n):
            for j, rel in enumerate(TO_CHIPS):
                blk = lnd[a].at[_index(_flip(me, rel))]
                there = lnd[a].at[_index(_flip(sibling, rel))]
                cp = pltpu.make_async_remote_copy(
                    src_ref=blk, dst_ref=blk, send_sem=send_sems.at[a * 3 + j], recv_sem=recv_sems.at[a * 3 + j],
                    device_id=sibling, device_id_type=MESH)
                cp.start()
                copies.append((cp, pltpu.make_async_remote_copy(
                    src_ref=blk, dst_ref=there, send_sem=send_sems.at[a * 3 + j], recv_sem=recv_sems.at[a * 3 + j],
                    device_id=sibling, device_id_type=MESH)))
        for cp, arrival in copies:
            arrival.wait_recv()
        for cp, arrival in copies:
            cp.wait_send()

    return pl.pallas_call(
        body, name=name, in_specs=[ANY] * n, out_specs=[ANY] * n,
        out_shape=[jax.ShapeDtypeStruct(l.shape, l.dtype) for l in lands],
        input_output_aliases={a: a for a in range(n)},
        scratch_shapes=[pltpu.SemaphoreType.DMA((3 * n,)), pltpu.SemaphoreType.DMA((3 * n,))],
    )(*lands)


REL = [(b >> 2 & 1, b >> 1 & 1, b & 1) for b in range(N_DEV)]


CHIP_REL = [(0, 0, 0)] + TO_CHIPS
N_CHIPS = len(CHIP_REL)


def rs_pair(name, parts):
    n = len(parts)

    def body(*refs):
        ins, got = refs[:n], refs[n:2 * n]
        send_sems, recv_sems = refs[2 * n:]
        me = _place()
        sibling = _flip(me, TO_SIBLING)
        remote = []
        for a in range(n):
            for q, rel in enumerate(CHIP_REL):
                k = a * N_CHIPS + q
                cp = pltpu.make_async_remote_copy(
                    src_ref=ins[a].at[_index(_flip(sibling, rel))], dst_ref=got[a].at[q], send_sem=send_sems.at[k],
                    recv_sem=recv_sems.at[k], device_id=sibling, device_id_type=MESH)
                cp.start()
                remote.append(cp)
        for cp in remote:
            cp.wait_recv()
        for cp in remote:
            cp.wait_send()

    shapes = [jax.ShapeDtypeStruct((N_CHIPS,) + tuple(p.shape[1:]), p.dtype) for p in parts]
    res = pl.pallas_call(
        body, name=name, in_specs=[ANY] * n, out_specs=[ANY] * n, out_shape=shapes,
        scratch_shapes=[pltpu.SemaphoreType.DMA((N_CHIPS * n,)), pltpu.SemaphoreType.DMA((N_CHIPS * n,))],
    )(*parts)
    return list(res)


def own_blocks():
    me = _place()
    return jnp.stack([_index(_flip(me, rel)) for rel in CHIP_REL]).astype(jnp.int32)


def pair_add(name, blocks, parts, got):
    nq, r, c = got.shape
    tr = _row_tile(r, c, budget=6 << 20)

    def body(blk_ref, a_ref, b_ref, o_ref):
        o_ref[...] = (a_ref[...].astype(F32) + b_ref[...].astype(F32)).astype(o_ref.dtype)

    spec = pl.BlockSpec((None, tr, c), lambda q, i, blk: (q, i, 0))
    return pl.pallas_call(
        body, name=name,
        grid_spec=pltpu.PrefetchScalarGridSpec(
            num_scalar_prefetch=1, grid=(nq, r // tr),
            in_specs=[pl.BlockSpec((None, tr, c), lambda q, i, blk: (blk[q], i, 0)), spec], out_specs=spec),
        out_shape=pltpu.HBM(got.shape, got.dtype),
        compiler_params=_params(("arbitrary", "arbitrary"), 6 * tr * c * 2))(blocks, parts, got)


def rs_pair_plan(n):
    def plan(me):
        sibling = _flip(me, TO_SIBLING)
        return [(a, a, _index(_flip(sibling, rel)), q, sibling, True, True)
                for a in range(n) for q, rel in enumerate(CHIP_REL)]
    return plan


def rs_plan(n):
    def plan(me):
        return [(a, a, q, q, _flip(me, CHIP_REL[q]), True, True) for a in range(n) for q in range(1, N_CHIPS)]
    return plan


def rs_start(name, sums, after):
    lands = [lax.empty(t.shape, t.dtype) for t in sums]
    return split_start(name, sums, lands, rs_plan(len(sums)), after)


def allreduce_small(name, pack, after):
    rows, lanes = pack.shape

    def body(x_ref, after_ref, o_ref, land, send_sems, recv_sems):
        me = _place()
        idx = _index(me)
        land[idx] = x_ref[...]
        copies = []
        for r in range(1, N_DEV):
            peer = _flip(me, REL[r])
            cp = pltpu.make_async_remote_copy(
                src_ref=x_ref, dst_ref=land.at[idx], send_sem=send_sems.at[r - 1], recv_sem=recv_sems.at[r - 1],
                device_id=peer, device_id_type=MESH)
            cp.start()
            copies.append(cp)
        for cp in copies:
            cp.wait_recv()
        for cp in copies:
            cp.wait_send()
        acc = land[0]
        for i in range(1, N_DEV):
            acc = acc + land[i]
        o_ref[...] = acc

    return pl.pallas_call(
        body, name=name, in_specs=[pl.BlockSpec(memory_space=pltpu.VMEM), ANY],
        out_specs=pl.BlockSpec(memory_space=pltpu.VMEM), out_shape=jax.ShapeDtypeStruct((rows, lanes), F32),
        scratch_shapes=[pltpu.VMEM((N_DEV, rows, lanes), F32), pltpu.SemaphoreType.DMA((7,)),
                        pltpu.SemaphoreType.DMA((7,))],
    )(pack, after)


def _pad_rows(a, rows):
    return jnp.pad(a, ((0, rows - a.shape[0]), (0, 0)))


def _as_tiles(vec):
    n = vec.shape[0]
    rows = -(-n // LANES)
    rows = -(-rows // SUBLANES) * SUBLANES
    return jnp.pad(vec, (0, rows * LANES - n)).reshape(rows, LANES)


def kernel(x, p, rel_bias_table, attn_norm, w_in, sink_a, w_branch_a, w_branch_b, w_out, ffn_norm, w_ffn_gate, w_ffn_up, conv_w, conv_b, w_ffn_down, ple_norm, w_ple_gate, w_ple_proj, final_norm, loss_target, m_rel_bias_table, m_attn_norm, m_w_in, m_sink_a, m_w_branch_a, m_w_branch_b, m_w_out, m_ffn_norm, m_w_ffn_gate, m_w_ffn_up, m_conv_w, m_conv_b, m_w_ffn_down, m_ple_norm, m_w_ple_gate, m_w_ple_proj, m_final_norm, v_rel_bias_table, v_attn_norm, v_w_in, v_sink_a, v_w_branch_a, v_w_branch_b, v_w_out, v_ffn_norm, v_w_ffn_gate, v_w_ffn_up, v_conv_w, v_conv_b, v_w_ffn_down, v_ple_norm, v_w_ple_gate, v_w_ple_proj, v_final_norm):
    xs = x[0]
    s, d = xs.shape
    ps = p[0, 0]
    target = loss_target[0]
    me = 4 * lax.axis_index("x") + 2 * lax.axis_index("y") + lax.axis_index("c")

    big = dict(w_in=w_in[0], w_branch_a=w_branch_a[0], w_branch_b=w_branch_b[0], w_out=w_out[0],
               w_ffn_gate=w_ffn_gate[0], w_ffn_up=w_ffn_up[0], w_ffn_down=w_ffn_down[0],
               w_ple_gate=w_ple_gate[0], w_ple_proj=w_ple_proj[0])
    big_m = dict(w_in=m_w_in[0], w_branch_a=m_w_branch_a[0], w_branch_b=m_w_branch_b[0], w_out=m_w_out[0],
                 w_ffn_gate=m_w_ffn_gate[0], w_ffn_up=m_w_ffn_up[0], w_ffn_down=m_w_ffn_down[0],
                 w_ple_gate=m_w_ple_gate[0], w_ple_proj=m_w_ple_proj[0])
    big_v = dict(w_in=v_w_in[0], w_branch_a=v_w_branch_a[0], w_branch_b=v_w_branch_b[0], w_out=v_w_out[0],
                 w_ffn_gate=v_w_ffn_gate[0], w_ffn_up=v_w_ffn_up[0], w_ffn_down=v_w_ffn_down[0],
                 w_ple_gate=v_w_ple_gate[0], w_ple_proj=v_w_ple_proj[0])
    names = list(big)
    nf = big["w_ffn_gate"].shape[1]

    shards = {k: big[k].astype(BF16) for k in names}
    shards["conv_w"] = _pad_rows(conv_w[0], SUBLANES)
    flipped = ("w_ffn_gate", "w_ffn_up")
    for k in flipped:
        big[k], big_m[k], big_v[k] = big[k].T, big_m[k].T, big_v[k].T
    ag_groups = [["w_in"], ["w_branch_a", "w_branch_b", "w_out"], ["w_ffn_gate", "conv_w"], ["w_ffn_up"],
                 ["w_ffn_down"], ["w_ple_gate", "w_ple_proj"]]
    ag_started = {}
    wg = {}

    ag_paired, ag_passing = {}, {}

    def pair(gi, after):
        lands = [lax.dynamic_update_index_in_dim(lax.empty((N_DEV,) + shards[k].shape, shards[k].dtype), shards[k],
                                                 me, 0) for k in ag_groups[gi]]
        ag_paired[gi], token = ag_pair(f"ag_pair{gi}", lands, after)
        return token

    def copies(gi):
        return ag_plan(len(ag_groups[gi]), [TO_X, TO_Y] if gi == 0 else TO_CHIPS)

    def start(gi, after):
        s_sems, r_sems, _, lands, token = split_start(f"ag_start{gi}", [], ag_paired[gi], copies(gi), after)
        ag_started[gi] = (s_sems, r_sems, lands)
        return token

    def landed(gi, after):
        s_sems, r_sems, lands = ag_started[gi]
        return split_wait(f"ag_wait{gi}", s_sems, r_sems, [], lands, copies(gi), after)[1:]

    def relayed(lands, after, meanwhile):
        plan = relay_plan(len(lands))
        s_sems, r_sems, _, lands, token = split_start("ag_relay0", [], lands, plan, after)
        return split_wait("ag_relayed0", s_sems, r_sems, [], lands, plan, meanwhile + _token_value(token))[1:]

    def pass_on(gi, lands, after):
        s_sems, r_sems, _, lands, token = split_start(f"ag_pass{gi}", [], lands, pass_plan(len(lands)), after)
        ag_passing[gi] = (s_sems, r_sems, lands)
        return token

    def ready(gi, after):
        s_sems, r_sems, lands = ag_passing[gi]
        lands = split_wait(f"ag_ready{gi}", s_sems, r_sems, [], lands, pass_plan(len(lands)), after)[1]
        wg.update(zip(ag_groups[gi], lands))

    cb = conv_b.reshape(N_DEV, 1, nf)

    table_t = rel_bias_table.T
    geo_a = dict(half=A_BLOCK, q_rows=ATTN_Q_ROWS, n_chains=ATTN_CHAINS, dil=1, nh=A_Q_HEADS, group=A_GROUP,
                 cq=COL_QA, ck=COL_KA, cv=COL_VA)
    geo_b = [dict(half=B_BLOCK, q_rows=min(ATTN_Q_ROWS, s // dil), n_chains=ATTN_CHAINS, dil=dil,
                  nh=B_HEADS_PER_GROUP, group=1, cq=COL_QB + g * B_OUT_W, ck=COL_KB + g * B_OUT_W,
                  cv=COL_VB + g * B_OUT_W) for g, (_, dil) in enumerate(B_PATTERNS)]
    bucket_a = bucket_tile(geo_a["q_rows"], A_BLOCK, 1)
    bias_a = bias_build("bias_a", table_t, bucket_a, 0, A_Q_HEADS, A_BLOCK)
    buckets_b = [bucket_tile(gb["q_rows"], B_BLOCK, gb["dil"]) for gb in geo_b]
    biases_b = [bias_build(f"bias_b{g}", table_t, buckets_b[g], A_Q_HEADS + g * B_HEADS_PER_GROUP, B_HEADS_PER_GROUP,
                           B_BLOCK) for g in range(len(B_PATTERNS))]

    token = start(0, pair(0, xs))
    h = rms_fwd("rms_attn", xs, attn_norm + _token_value(token))
    lands0, token = landed(0, pair(5, pair(4, pair(3, pair(2, pair(1, h))))))
    bias_corner = bias_a[0, :1, :1] + sum(b[0, :1, :1] for b in biases_b)
    lands0, token = relayed(lands0, token, bias_corner)
    token = start(5, start(4, start(3, start(2, start(1, token)))))
    wg["w_in"] = ag_finish("ag_finish0", lands0)[0]
    proj = mm_cols("proj_in", h, wg["w_in"], F32, fold=True, after=token)
    token = pass_on(1, landed(1, proj)[0], proj)
    sink = sink_a[0] + _token_value(token)
    ya, lse_a = band_attn_fwd("attn_a_fwd", proj, bias_a, sink, **geo_a)
    outs_b, lses_b = [], []
    for g in range(len(B_PATTERNS)):
        o, l = band_attn_fwd(f"attn_b{g}_fwd", proj, biases_b[g], None, **geo_b[g])
        outs_b.append(o)
        lses_b.append(l)
    yb = dil_merge_fwd("dil_merge_fwd", outs_b, lses_b)
    ready(1, yb)
    token = pass_on(2, landed(2, yb)[0], yb)
    w_out_full = wg["w_out"].reshape(d, d)
    ta = mm_cols("branch_a", ya, wg["w_branch_a"], F32, fold=True, after=token)
    tb = mm_cols("branch_b", yb, wg["w_branch_b"], F32, fold=True)
    merged = gate_merge_fwd("gate_merge_fwd", proj, ta, tb, d)
    x1 = mm_plain("mix_out", merged, w_out_full, F32, res=xs)

    hf = rms_fwd("rms_ffn", x1, ffn_norm)
    ready(2, hf)
    token = pass_on(3, landed(3, hf)[0], hf)
    cw = wg["conv_w"]
    gpre = mm_cols("ffn_gate", hf, wg["w_ffn_gate"], F32, fold=False, after=token)
    ready(3, gpre)
    token = pass_on(4, landed(4, gpre)[0], gpre)
    u = mm_cols("ffn_up", hf, wg["w_ffn_up"], F32, fold=False, after=token)
    z = ffn_mid_fwd("ffn_mid_fwd", gpre, u, cw, cb)
    ready(4, z)
    token = pass_on(5, landed(5, z)[0], z)
    x2 = mm_jsum("ffn_down", z, wg["w_ffn_down"], F32, res=x1, after=token)

    hp = rms_fwd("rms_ple", x2, ple_norm)
    ready(5, hp)
    w_pg_full = wg["w_ple_gate"].reshape(d, d)
    lp = mm_plain("ple_gate", hp, w_pg_full, F32)
    pp = mm_cols("ple_proj", ps, wg["w_ple_proj"], F32, fold=True)
    loss_part, dx3, dlp, dpp, d_final = tail_fwd_bwd("tail", x2, lp, pp, final_norm.reshape(1, d), target)

    grads = {}
    rs_started = []
    blocks = own_blocks()

    exchanging = []

    def exchange(tag, keys):
        parts = [grads[k] for k in keys]
        lands = [lax.empty((N_CHIPS,) + tuple(p.shape[1:]), p.dtype) for p in parts]
        s_sems, r_sems, parts, lands, token = split_start(f"rs_pair_{tag}", parts, lands, rs_pair_plan(len(keys)), blocks)
        exchanging.append((tag, keys, s_sems, r_sems, parts, lands))
        return _token_value(token)

    def send(after):
        tag, keys, s_sems, r_sems, parts, lands = exchanging.pop(0)
        parts, got, _ = split_wait(f"rs_paired_{tag}", s_sems, r_sems, parts, lands, rs_pair_plan(len(keys)), after)
        return send_sums(tag, keys, parts, got)

    def send_sums(tag, keys, parts, got):
        sums = [pair_add(f"pair_add_{k}", blocks, p, g) for k, p, g in zip(keys, parts, got)]
        s_sems, r_sems, srcs, lands, token = rs_start(f"rs_start_{tag}", sums, blocks)
        rs_started.append((tag, keys, s_sems, r_sems, srcs, lands))
        return token

    grads["w_ple_proj"] = mm_tn_cols("d_w_ple_proj", ps, dpp, N_DEV, big["w_ple_proj"].shape[1], BF16, folded=True)
    grads["w_ple_gate"] = mm_tn_plain("d_w_ple_gate", hp, dlp, BF16).reshape(N_DEV, d // N_DEV, d)
    tok = exchange("ple", ["w_ple_proj", "w_ple_gate"])
    dhp = mm_nt_plain("d_hp", dlp, w_pg_full, F32)
    dx2, dx2_b, d_ple = rms_bwd("rms_ple_bwd", x2, ple_norm + tok, dhp, dx3, True)

    dz = mm_nt_j("d_z", dx2_b, wg["w_ffn_down"], BF16)
    grads["w_ffn_down"] = mm_tn_j("d_w_ffn_down", z, dx2_b, BF16)
    tok = _token_value(send(dz)) + exchange("down", ["w_ffn_down"])
    du, dgpre, dcw = ffn_mid_bwd("ffn_mid_bwd", gpre, u, dz, cw, cb + tok)
    grads["w_ffn_up"] = mm_tn_j("d_w_ffn_up", du, hf, BF16)
    grads["w_ffn_gate"] = mm_tn_j("d_w_ffn_gate", dgpre, hf, BF16)
    dhf = mm_nt_jsum("d_hf_up", du, wg["w_ffn_up"], F32, folded=False)
    dhf = mm_nt_jsum("d_hf_gate", dgpre, wg["w_ffn_gate"], F32, folded=False, res=dhf)
    tok = _token_value(send(dhf)) + exchange("upgate", ["w_ffn_up", "w_ffn_gate"])
    dx1, dx1_b, d_ffn = rms_bwd("rms_ffn_bwd", x1, ffn_norm + tok, dhf, dx2, True)

    dmerged = mm_nt_plain("d_merged", dx1_b, w_out_full, F32)
    grads["w_out"] = mm_tn_plain("d_w_out", merged, dx1_b, BF16).reshape(N_DEV, d // N_DEV, d)
    dta, dtb, dga, dgb = gate_merge_bwd("gate_merge_bwd", dmerged, proj, ta, tb, d)
    grads["w_branch_a"] = mm_tn_cols("d_w_branch_a", ya, dta, N_DEV, big["w_branch_a"].shape[1], BF16, folded=True)
    grads["w_branch_b"] = mm_tn_cols("d_w_branch_b", yb, dtb, N_DEV, big["w_branch_b"].shape[1], BF16, folded=True)
    dya = mm_nt_jsum("d_ya", dta, wg["w_branch_a"], F32, folded=True)
    dyb = mm_nt_jsum("d_yb", dtb, wg["w_branch_b"], F32, folded=True)
    tok = _token_value(send(dyb)) + exchange("mix", ["w_out", "w_branch_a", "w_branch_b"])
    dqa, dka, dva, dbias_a, dsink = band_attn_bwd("attn_a_bwd", proj, bias_a, sink + tok, dya, ya, lse_a, None, **geo_a)
    douts_b, dlses_b = dil_merge_bwd("dil_merge_bwd", dyb, outs_b, lses_b)
    dq_b, dk_b, dv_b, dbias_b = [], [], [], []
    for g in range(len(B_PATTERNS)):
        dq, dk, dv, db, _ = band_attn_bwd(f"attn_b{g}_bwd", proj, biases_b[g], None, douts_b[g], outs_b[g], lses_b[g],
                                          dlses_b[g], **geo_b[g])
        dq_b.append(dq)
        dk_b.append(dk)
        dv_b.append(dv)
        dbias_b.append(db)
    dproj = jnp.concatenate([t.astype(BF16) for t in [dqa, dka, dva] + dq_b + dk_b + dv_b + [dga, dgb]], axis=1)
    token = send(dproj)
    grads["w_in"] = mm_tn_cols("d_w_in", h, dproj, N_DEV, big["w_in"].shape[1], BF16, folded=True, after=token)
    token = send_sums("in", ["w_in"], [grads["w_in"]], rs_pair("rs_pair_in", [grads["w_in"]]))
    dh = mm_nt_jsum("d_h", dproj, wg["w_in"], F32, folded=True, after=token)
    grad_x, _, d_attn = rms_bwd("rms_attn_bwd", xs, attn_norm, dh, dx1, False)

    dt_a = table_grad("table_grad_a", dbias_a, bucket_a)[:, 0, :N_BUCKETS]
    dt_b = [table_grad(f"table_grad_b{g}", dbias_b[g], buckets_b[g])[:, 0, :N_BUCKETS] for g in range(len(B_PATTERNS))]
    d_table_part = jnp.concatenate([dt_a] + dt_b, axis=0).T

    pieces = [
        ("loss", loss_part[0, :1]),
        ("table", d_table_part.reshape(-1)),
        ("attn_norm", d_attn.reshape(-1)),
        ("sink", dsink[:, 0, 0]),
        ("ffn_norm", d_ffn.reshape(-1)),
        ("conv_w", dcw[:, 0:3, :].reshape(-1)),
        ("conv_b", dcw[:, 3, :].reshape(-1)),
        ("ple_norm", d_ple.reshape(-1)),
        ("final_norm", d_final.reshape(-1)),
    ]
    tiles = [_as_tiles(v) for _, v in pieces]
    pack = jnp.concatenate(tiles, axis=0)

    out_g, out_d, out_m, out_v = {}, {}, {}, {}

    def finish(group, after):
        tag, keys, s_sems, r_sems, srcs, lands = group
        srcs, lands, _ = split_wait(f"rs_wait_{tag}", s_sems, r_sems, srcs, lands, rs_plan(len(keys)), after)
        for k, mine, theirs in zip(keys, srcs, lands):
            res = reduce_adam("adam_" + k, mine, theirs, big[k], big_m[k], big_v[k])
            after = res[1]
            out_g[k], out_d[k], out_m[k], out_v[k] = [(t.T if k in flipped else t)[None] for t in res]
        return after

    after = pack
    for group in rs_started[:-1]:
        after = finish(group, after)
    total = allreduce_small("allreduce_small", pack, after)
    finish(rs_started[-1], total)
    small = {}
    row = 0
    for (nm, v), t in zip(pieces, tiles):
        small[nm] = total[row:row + t.shape[0]].reshape(-1)[:v.shape[0]]
        row += t.shape[0]
    loss = small["loss"][0]
    g_small = dict(
        rel_bias_table=small["table"].reshape(rel_bias_table.shape),
        attn_norm=small["attn_norm"].reshape(attn_norm.shape),
        sink_a=small["sink"].reshape(sink_a.shape),
        ffn_norm=small["ffn_norm"].reshape(ffn_norm.shape),
        conv_w=lax.dynamic_index_in_dim(small["conv_w"].reshape(N_DEV, 3, nf), me, 0, keepdims=False)[None],
        conv_b=small["conv_b"].reshape(conv_b.shape),
        ple_norm=small["ple_norm"].reshape(ple_norm.shape),
        final_norm=small["final_norm"].reshape(1, d),
    )
    w_small = dict(rel_bias_table=(rel_bias_table, m_rel_bias_table, v_rel_bias_table),
                   attn_norm=(attn_norm, m_attn_norm, v_attn_norm), sink_a=(sink_a, m_sink_a, v_sink_a),
                   ffn_norm=(ffn_norm, m_ffn_norm, v_ffn_norm), conv_w=(conv_w, m_conv_w, v_conv_w),
                   conv_b=(conv_b, m_conv_b, v_conv_b), ple_norm=(ple_norm, m_ple_norm, v_ple_norm),
                   final_norm=(final_norm, m_final_norm, v_final_norm))

    for k, (wv, mv, vv) in w_small.items():
        shape = wv.shape
        two_d = (1, shape[0]) if len(shape) == 1 else ((shape[0] * shape[1], shape[2]) if len(shape) == 3 else shape)
        gk = g_small[k].reshape(two_d)
        dl, nm, nv = adam_small("adam_" + k, gk, wv.reshape(two_d), mv.reshape(two_d), vv.reshape(two_d))
        out_g[k], out_d[k], out_m[k], out_v[k] = gk.reshape(shape), dl.reshape(shape), nm.reshape(shape), nv.reshape(shape)

    order = ["rel_bias_table", "attn_norm", "w_in", "sink_a", "w_branch_a", "w_branch_b", "w_out", "ffn_norm",
             "w_ffn_gate", "w_ffn_up", "conv_w", "conv_b", "w_ffn_down", "ple_norm", "w_ple_gate", "w_ple_proj",
             "final_norm"]
    return (loss, grad_x[None], *[out_g[k] for k in order], *[out_d[k] for k in order],
            *[out_m[k] for k in order], *[out_v[k] for k in order])
```

```python
import math

import jax
import jax.numpy as jnp
from jax import lax
from jax.experimental import pallas as pl
from jax.experimental.pallas import tpu as pltpu

F32 = jnp.float32
BF16 = jnp.bfloat16
MESH = pl.DeviceIdType.MESH
N_DEV = 8

HEAD_DIM = 128
A_Q_HEADS = 8
A_KV_HEADS = 2
A_GROUP = A_Q_HEADS // A_KV_HEADS
A_BLOCK = 128
B_PATTERNS = ((128, 1), (512, 4), (2048, 16))
B_HEADS_PER_GROUP = 4
B_HEADS = len(B_PATTERNS) * B_HEADS_PER_GROUP
B_BLOCK = 64
N_BUCKETS = 32
MAX_DISTANCE = 1024
A_Q_W = A_Q_HEADS * HEAD_DIM
A_KV_W = A_KV_HEADS * HEAD_DIM
B_W = B_HEADS * HEAD_DIM
B_OUT_W = B_HEADS_PER_GROUP * HEAD_DIM
COL_QA = 0
COL_KA = COL_QA + A_Q_W
COL_VA = COL_KA + A_KV_W
COL_QB = COL_VA + A_KV_W
COL_KB = COL_QB + B_W
COL_VB = COL_KB + B_W
COL_GATES = COL_VB + B_W
RMS_EPS = 1e-6
NEG_INF = -1e30
ATTN_SCALE = HEAD_DIM ** -0.5
ATTN_Q_ROWS = 256
ATTN_CHAINS = 4

ADAM_LR = 0.001
ADAM_B1 = 0.9
ADAM_B2 = 0.999
ADAM_EPS = 1e-08
ADAM_WD = 0.01
ADAM_STEP = 10

GELU_C = math.sqrt(2.0 / math.pi)
GELU_A = 0.044715

V7X_VMEM_BYTES = 64 * 1024 * 1024
VMEM_CEILING = V7X_VMEM_BYTES - 8 * 1024 * 1024
LANES = 128
SUBLANES = 8


def _pick(n, cands):
    for c in cands:
        if n % c == 0:
            return c
    return n


def _nbytes(shape, dtype):
    n = 1
    for d in shape:
        if d is not None:
            n *= d
    return n * jnp.dtype(dtype).itemsize


def _params(sem, est_bytes):
    limit = int(min(VMEM_CEILING, max(32 * 1024 * 1024, 2 * est_bytes + (8 << 20))))
    return pltpu.CompilerParams(dimension_semantics=sem, vmem_limit_bytes=limit)


def _mm(name, a, b, a_bs, a_im, b_bs, b_im, out_shape, out_dtype, o_bs, o_im, grid, dims,
        res=None, r_bs=None, r_im=None, after=None):
    nk = grid[-1]
    nax = len(grid)
    has_res = res is not None
    has_after = after is not None
    o_tile = tuple(d for d in o_bs if d is not None)

    def body(*refs):
        a_ref, b_ref = refs[:2]
        r_ref = refs[2] if has_res else None
        n_in = 2 + has_res + has_after
        o_ref = refs[n_in]
        rest = refs[n_in + 1:]

        def prod():
            return lax.dot_general(a_ref[...].astype(BF16), b_ref[...].astype(BF16), (dims, ((), ())),
                                   preferred_element_type=F32)

        def finish(r):
            if r_ref is not None:
                r = r + r_ref[...].astype(F32)
            o_ref[...] = r.astype(o_ref.dtype)

        if nk == 1:
            finish(prod())
        else:
            acc = rest[0]
            k = pl.program_id(nax - 1)

            @pl.when(k == 0)
            def _():
                acc[...] = prod()

            @pl.when(k > 0)
            def _():
                acc[...] += prod()

            @pl.when(k == nk - 1)
            def _():
                finish(acc[...])

    in_specs = [pl.BlockSpec(a_bs, a_im), pl.BlockSpec(b_bs, b_im)]
    args = [a, b]
    est = _nbytes(a_bs, a.dtype) + _nbytes(b_bs, b.dtype) + _nbytes(o_bs, out_dtype) + 2 * _nbytes(o_tile, F32)
    if has_res:
        in_specs.append(pl.BlockSpec(r_bs, r_im))
        args.append(res)
        est += _nbytes(r_bs, res.dtype)
    if has_after:
        in_specs.append(pl.BlockSpec(memory_space=pl.ANY))
        args.append(after)
    scratch = [] if nk == 1 else [pltpu.VMEM(o_tile, F32)]
    sem = ("parallel",) * (nax - 1) + ("arbitrary",)
    return pl.pallas_call(
        body, name=name, grid=grid, in_specs=in_specs, out_specs=pl.BlockSpec(o_bs, o_im),
        out_shape=pltpu.HBM(out_shape, out_dtype), scratch_shapes=scratch,
        compiler_params=_params(sem, est))(*args)


TM_CANDS = (1024, 512, 256, 128, 64, 32, 16, 8)
TM_WIDE_CANDS = (2048,) + TM_CANDS
MM_WHOLE_K_BYTES = 42 * 1024 * 1024


def _whole(k, tile_bytes):
    return k if 2 * tile_bytes(k) <= MM_WHOLE_K_BYTES else _pick(k, TK_CANDS)
TK_CANDS = (1024, 512, 256, 128)
TN_CANDS = (1024, 512, 256, 128)


def mm_cols(name, a, wg, out_dtype, fold, after=None):
    m, k = a.shape
    nj, _, n = wg.shape
    tm = _pick(m, TM_WIDE_CANDS)
    tk = _whole(k, lambda t: _nbytes((tm, t), a.dtype) + _nbytes((t, n), wg.dtype) + _nbytes((tm, n), out_dtype))
    grid = (nj, m // tm, k // tk)
    if fold:
        shape, o_bs, o_im = (m, nj * n), (tm, n), (lambda j, i, kk: (i, j))
    else:
        shape, o_bs, o_im = (nj, m, n), (None, tm, n), (lambda j, i, kk: (j, i, 0))
    return _mm(name, a, wg, (tm, tk), lambda j, i, kk: (i, kk), (None, tk, n), lambda j, i, kk: (j, kk, 0),
               shape, out_dtype, o_bs, o_im, grid, ((1,), (0,)), after=after)


def mm_plain(name, a, w, out_dtype, res=None):
    m, k = a.shape
    n = w.shape[1]
    tm, tk, tn = _pick(m, TM_CANDS if res is not None else TM_WIDE_CANDS), _pick(k, TK_CANDS), _pick(n, TN_CANDS)
    if res is None:
        tk = _whole(k, lambda t: _nbytes((tm, t), a.dtype) + _nbytes((t, tn), w.dtype) + _nbytes((tm, tn), out_dtype))
    grid = (n // tn, m // tm, k // tk)
    return _mm(name, a, w, (tm, tk), lambda j, i, kk: (i, kk), (tk, tn), lambda j, i, kk: (kk, j),
               (m, n), out_dtype, (tm, tn), lambda j, i, kk: (i, j), grid, ((1,), (0,)),
               res, (tm, tn), lambda j, i, kk: (i, j))


def mm_jsum(name, aj, wg, out_dtype, res=None, after=None):
    nj, m, ka = aj.shape
    n = wg.shape[2]
    tm, tn = _pick(m, TM_CANDS if res is not None else TM_WIDE_CANDS), _pick(n, TN_CANDS)
    grid = (m // tm, n // tn, nj)
    return _mm(name, aj, wg, (None, tm, ka), lambda i, jn, j: (j, i, 0), (None, ka, tn), lambda i, jn, j: (j, 0, jn),
               (m, n), out_dtype, (tm, tn), lambda i, jn, j: (i, jn), grid, ((1,), (0,)),
               res, (tm, tn), lambda i, jn, j: (i, jn), after=after)


def mm_tn_cols(name, a, g, nj, n, out_dtype, folded, after=None):
    s, kw = a.shape
    tkw = _pick(kw, TM_WIDE_CANDS)
    ts = _whole(s, lambda t: _nbytes((t, tkw), a.dtype) + _nbytes((t, n), g.dtype) + _nbytes((tkw, n), out_dtype))
    grid = (nj, kw // tkw, s // ts)
    if folded:
        g_bs, g_im = (ts, n), (lambda j, i, ss: (ss, j))
    else:
        g_bs, g_im = (None, ts, n), (lambda j, i, ss: (j, ss, 0))
    return _mm(name, a, g, (ts, tkw), lambda j, i, ss: (ss, i), g_bs, g_im,
               (nj, kw, n), out_dtype, (None, tkw, n), lambda j, i, ss: (j, i, 0), grid, ((0,), (0,)), after=after)


def mm_tn_plain(name, a, g, out_dtype):
    s, kw = a.shape
    n = g.shape[1]
    tkw, tn = _pick(kw, TM_WIDE_CANDS), _pick(n, TN_CANDS)
    ts = _whole(s, lambda t: _nbytes((t, tkw), a.dtype) + _nbytes((t, tn), g.dtype) + _nbytes((tkw, tn), out_dtype))
    grid = (kw // tkw, n // tn, s // ts)
    return _mm(name, a, g, (ts, tkw), lambda i, jn, ss: (ss, i), (ts, tn), lambda i, jn, ss: (ss, jn),
               (kw, n), out_dtype, (tkw, tn), lambda i, jn, ss: (i, jn), grid, ((0,), (0,)))


def mm_tn_j(name, aj, g, out_dtype):
    nj, s, ka = aj.shape
    n = g.shape[1]
    tn = _pick(n, TM_WIDE_CANDS)
    ts = _whole(s, lambda t: _nbytes((t, ka), aj.dtype) + _nbytes((t, tn), g.dtype) + _nbytes((ka, tn), out_dtype))
    grid = (nj, n // tn, s // ts)
    return _mm(name, aj, g, (None, ts, ka), lambda j, jn, ss: (j, ss, 0), (ts, tn), lambda j, jn, ss: (ss, jn),
               (nj, ka, n), out_dtype, (None, ka, tn), lambda j, jn, ss: (j, 0, jn), grid, ((0,), (0,)))


def mm_nt_plain(name, g, w, out_dtype):
    m, n = g.shape
    k = w.shape[0]
    tm, tkk = _pick(m, TM_WIDE_CANDS), _pick(k, TN_CANDS)
    tn = _whole(n, lambda t: _nbytes((tm, t), g.dtype) + _nbytes((tkk, t), w.dtype) + _nbytes((tm, tkk), out_dtype))
    grid = (k // tkk, m // tm, n // tn)
    return _mm(name, g, w, (tm, tn), lambda kk, i, jn: (i, jn), (tkk, tn), lambda kk, i, jn: (kk, jn),
               (m, k), out_dtype, (tm, tkk), lambda kk, i, jn: (i, kk), grid, ((1,), (1,)))


def mm_nt_j(name, g, wg, out_dtype):
    m, n = g.shape
    nj, ka, _ = wg.shape
    tm = _pick(m, TM_WIDE_CANDS)
    tn = _whole(n, lambda t: _nbytes((tm, t), g.dtype) + _nbytes((ka, t), wg.dtype) + _nbytes((tm, ka), out_dtype))
    grid = (nj, m // tm, n // tn)
    return _mm(name, g, wg, (tm, tn), lambda j, i, jn: (i, jn), (None, ka, tn), lambda j, i, jn: (j, 0, jn),
               (nj, m, ka), out_dtype, (None, tm, ka), lambda j, i, jn: (j, i, 0), grid, ((1,), (1,)))


def mm_nt_jsum(name, g, wg, out_dtype, folded, res=None, after=None):
    nj, k, n = wg.shape
    m = g.shape[0] if folded else g.shape[1]
    tm, tkk = _pick(m, TM_CANDS if res is not None else TM_WIDE_CANDS), _pick(k, TN_CANDS)
    grid = (m // tm, k // tkk, nj)
    if folded:
        g_bs, g_im = (tm, n), (lambda i, kk, j: (i, j))
    else:
        g_bs, g_im = (None, tm, n), (lambda i, kk, j: (j, i, 0))
    return _mm(name, g, wg, g_bs, g_im, (None, tkk, n), lambda i, kk, j: (j, kk, 0),
               (m, k), out_dtype, (tm, tkk), lambda i, kk, j: (i, kk), grid, ((1,), (1,)),
               res, (tm, tkk), lambda i, kk, j: (i, kk), after=after)


ROW_TILE_CANDS = (256, 128, 64, 32, 16, 8)


def _rstd(x):
    return lax.rsqrt(jnp.mean(x * x, axis=-1, keepdims=True) + RMS_EPS)


def _sigmoid(t):
    return 1.0 / (1.0 + jnp.exp(-t))


def rms_fwd(name, x, gain):
    s, d = x.shape
    ts = _pick(s, ROW_TILE_CANDS)

    def body(x_ref, g_ref, h_ref):
        xv = x_ref[...]
        h_ref[...] = ((xv * _rstd(xv)) * g_ref[...]).astype(h_ref.dtype)

    return pl.pallas_call(
        body, name=name, grid=(s // ts,),
        in_specs=[pl.BlockSpec((ts, d), lambda i: (i, 0)), pl.BlockSpec((1, d), lambda i: (0, 0))],
        out_specs=pl.BlockSpec((ts, d), lambda i: (i, 0)),
        out_shape=pltpu.HBM((s, d), BF16),
        compiler_params=_params(("parallel",), 3 * ts * d * 4))(x, gain)


def rms_bwd(name, x, gain, dh, dres, bf16_copy):
    s, d = x.shape
    ts = _pick(s, ROW_TILE_CANDS)

    def body(x_ref, g_ref, dh_ref, dr_ref, dx_ref, *rest):
        dxb_ref, dg_ref = rest if bf16_copy else (None, rest[0])
        xv = x_ref[...]
        r = _rstd(xv)
        xhat = xv * r
        dhv = dh_ref[...].astype(F32)
        dxhat = dhv * g_ref[...]
        dx = dr_ref[...] + r * (dxhat - xhat * jnp.mean(dxhat * xhat, axis=-1, keepdims=True))
        dx_ref[...] = dx
        if bf16_copy:
            dxb_ref[...] = dx.astype(dxb_ref.dtype)
        part = jnp.sum(dhv * xhat, axis=0, keepdims=True)

        @pl.when(pl.program_id(0) == 0)
        def _():
            dg_ref[...] = part

        @pl.when(pl.program_id(0) > 0)
        def _():
            dg_ref[...] += part

    row = pl.BlockSpec((ts, d), lambda i: (i, 0))
    vec = pl.BlockSpec((1, d), lambda i: (0, 0))
    copy_spec, copy_shape = ([row], [pltpu.HBM((s, d), BF16)]) if bf16_copy else ([], [])
    res = pl.pallas_call(
        body, name=name, grid=(s // ts,), in_specs=[row, vec, row, row], out_specs=[row] + copy_spec + [vec],
        out_shape=[pltpu.HBM((s, d), F32)] + copy_shape + [jax.ShapeDtypeStruct((1, d), F32)],
        compiler_params=_params(("arbitrary",), 7 * ts * d * 4))(x, gain, dh, dres)
    return (res[0], res[1], res[2]) if bf16_copy else (res[0], None, res[1])


def gate_merge_fwd(name, proj, ta, tb, d):
    s = proj.shape[0]
    ts = _pick(s, ROW_TILE_CANDS)
    cb = COL_GATES // d

    def body(ga_ref, gb_ref, ta_ref, tb_ref, o_ref):
        o_ref[...] = (_sigmoid(ga_ref[...]) * ta_ref[...] + _sigmoid(gb_ref[...]) * tb_ref[...]).astype(o_ref.dtype)

    row = pl.BlockSpec((ts, d), lambda i: (i, 0))
    return pl.pallas_call(
        body, name=name, grid=(s // ts,),
        in_specs=[pl.BlockSpec((ts, d), lambda i: (i, cb)), pl.BlockSpec((ts, d), lambda i: (i, cb + 1)), row, row],
        out_specs=row, out_shape=pltpu.HBM((s, d), BF16),
        compiler_params=_params(("parallel",), 5 * ts * d * 4))(proj, proj, ta, tb)


def gate_merge_bwd(name, dmerged, proj, ta, tb, d):
    s = proj.shape[0]
    ts = _pick(s, ROW_TILE_CANDS)
    cb = COL_GATES // d

    def body(dm_ref, ga_ref, gb_ref, ta_ref, tb_ref, dta_ref, dtb_ref, dga_ref, dgb_ref):
        dm = dm_ref[...]
        sa = _sigmoid(ga_ref[...])
        sb = _sigmoid(gb_ref[...])
        dta_ref[...] = (dm * sa).astype(dta_ref.dtype)
        dtb_ref[...] = (dm * sb).astype(dtb_ref.dtype)
        dga_ref[...] = (dm * ta_ref[...] * (sa * (1.0 - sa))).astype(dga_ref.dtype)
        dgb_ref[...] = (dm * tb_ref[...] * (sb * (1.0 - sb))).astype(dgb_ref.dtype)

    row = pl.BlockSpec((ts, d), lambda i: (i, 0))
    out = pltpu.HBM((s, d), BF16)
    return pl.pallas_call(
        body, name=name, grid=(s // ts,),
        in_specs=[row, pl.BlockSpec((ts, d), lambda i: (i, cb)), pl.BlockSpec((ts, d), lambda i: (i, cb + 1)), row, row],
        out_specs=[row, row, row, row], out_shape=[out, out, out, out],
        compiler_params=_params(("parallel",), 8 * ts * d * 4))(dmerged, proj, proj, ta, tb)


def tail_fwd_bwd(name, x2, lp, pp, gain, target):
    s, d = x2.shape
    ts = _pick(s, ROW_TILE_CANDS)

    def body(x2_ref, lp_ref, pp_ref, g_ref, t_ref, loss_ref, dx3_ref, dlp_ref, dpp_ref, dg_ref):
        gp = _sigmoid(lp_ref[...])
        ppv = pp_ref[...]
        x3 = x2_ref[...] + gp * ppv
        r = _rstd(x3)
        xhat = x3 * r
        gv = g_ref[...]
        err = xhat * gv - t_ref[...]
        loss = jnp.sum(err * err) * (0.5 / d)
        dy = err * (1.0 / d)
        dxhat = dy * gv
        dx3 = r * (dxhat - xhat * jnp.mean(dxhat * xhat, axis=-1, keepdims=True))
        dx3_ref[...] = dx3
        dlp_ref[...] = (dx3 * ppv * (gp * (1.0 - gp))).astype(dlp_ref.dtype)
        dpp_ref[...] = (dx3 * gp).astype(dpp_ref.dtype)
        part = jnp.sum(dy * xhat, axis=0, keepdims=True)
        lossv = jnp.full((1, LANES), loss, F32)

        @pl.when(pl.program_id(0) == 0)
        def _():
            dg_ref[...] = part
            loss_ref[...] = lossv

        @pl.when(pl.program_id(0) > 0)
        def _():
            dg_ref[...] += part
            loss_ref[...] += lossv

    row = pl.BlockSpec((ts, d), lambda i: (i, 0))
    vec = pl.BlockSpec((1, d), lambda i: (0, 0))
    return pl.pallas_call(
        body, name=name, grid=(s // ts,), in_specs=[row, row, row, vec, row],
        out_specs=[pl.BlockSpec((1, LANES), lambda i: (0, 0)), row, row, row, vec],
        out_shape=[jax.ShapeDtypeStruct((1, LANES), F32), pltpu.HBM((s, d), F32),
                   pltpu.HBM((s, d), BF16), pltpu.HBM((s, d), BF16),
                   jax.ShapeDtypeStruct((1, d), F32)],
        compiler_params=_params(("arbitrary",), 9 * ts * d * 4))(x2, lp, pp, gain, target)


HALO = SUBLANES
BF16_ROWS = 2 * SUBLANES


def _shift_rows(cur, prev_row, next_row):
    ts = cur.shape[0]
    rid = lax.broadcasted_iota(jnp.int32, cur.shape, 0)
    down = jnp.where(rid == 0, prev_row, pltpu.roll(cur, 1, 0))
    up = jnp.where(rid == ts - 1, next_row, pltpu.roll(cur, ts - 1, 0))
    return down, up


def _halo_specs(ts, s, nf, halo=HALO):
    nb = ts // halo
    last = s // halo - 1
    cur = pl.BlockSpec((None, ts, nf), lambda j, i: (j, i, 0))
    prev = pl.BlockSpec((None, halo, nf), lambda j, i: (j, jnp.maximum(i * nb - 1, 0), 0))
    nxt = pl.BlockSpec((None, halo, nf), lambda j, i: (j, jnp.minimum((i + 1) * nb, last), 0))
    return cur, prev, nxt


def _halo_rows(prev_ref, next_ref, n_tiles):
    i = pl.program_id(1)
    prev_row = jnp.where(i == 0, 0.0, prev_ref[HALO - 1:HALO, :].astype(F32))
    next_row = jnp.where(i == n_tiles - 1, 0.0, next_ref[0:1, :].astype(F32))
    return prev_row, next_row


def _gelu(g):
    t = jnp.tanh(GELU_C * (g + GELU_A * (g * g * g)))
    return 0.5 * g * (1.0 + t), t


def _conv(cur, down, up, cw_ref, cb_ref):
    return down * cw_ref[0:1, :] + cur * cw_ref[1:2, :] + up * cw_ref[2:3, :] + cb_ref[...]


def ffn_mid_fwd(name, gpre, u, cw, cb):
    nj, s, nf = gpre.shape
    ts = _pick(s, (512, 256, 128, 64, 32, 16, 8))
    n_tiles = s // ts
    cur, prev, nxt = _halo_specs(ts, s, nf)

    def body(g_ref, gp_ref, gn_ref, u_ref, cw_ref, cb_ref, z_ref):
        gv = g_ref[...]
        down, up = _shift_rows(gv, *_halo_rows(gp_ref, gn_ref, n_tiles))
        act, _ = _gelu(_conv(gv, down, up, cw_ref, cb_ref))
        z_ref[...] = (act * u_ref[...]).astype(z_ref.dtype)

    return pl.pallas_call(
        body, name=name, grid=(nj, n_tiles),
        in_specs=[cur, prev, nxt, cur, pl.BlockSpec((None, SUBLANES, nf), lambda j, i: (j, 0, 0)),
                  pl.BlockSpec((None, 1, nf), lambda j, i: (j, 0, 0))],
        out_specs=cur, out_shape=pltpu.HBM((nj, s, nf), BF16),
        compiler_params=_params(("parallel", "parallel"), 8 * ts * nf * 4))(gpre, gpre, gpre, u, cw, cb)


def _gelu_grad(g, t):
    return 0.5 * (1.0 + t) + 0.5 * g * (1.0 - t * t) * (GELU_C * (1.0 + 3.0 * GELU_A * (g * g)))


def ffn_mid_bwd(name, gpre, u, dz, cw, cb):
    nj, s, nf = gpre.shape
    ts = _pick(s, (512, 256, 128, 64, 32, 16, 8))
    n_tiles = s // ts
    cur, prev, nxt = _halo_specs(ts, s, nf)

    def body(g_ref, gp_ref, gn_ref, u_ref, up_ref, un_ref, dz_ref, dzp_ref, dzn_ref, cw_ref, cb_ref,
             du_ref, dgp_ref, dcw_ref):
        i = pl.program_id(1)
        w0, w1, w2, bias = cw_ref[0:1, :], cw_ref[1:2, :], cw_ref[2:3, :], cb_ref[...]
        gv = g_ref[...]
        down, up = _shift_rows(gv, *_halo_rows(gp_ref, gn_ref, n_tiles))
        gc = down * w0 + gv * w1 + up * w2 + bias
        act, t = _gelu(gc)
        dzv = dz_ref[...].astype(F32)
        du_ref[...] = (dzv * act).astype(du_ref.dtype)
        dg = dzv * u_ref[...] * _gelu_grad(gc, t)

        def edge_dg(g_before, g_at, g_after, u_at, dz_at):
            ge = g_before.astype(F32) * w0 + g_at.astype(F32) * w1 + g_after.astype(F32) * w2 + bias
            return dz_at.astype(F32) * u_at.astype(F32) * _gelu_grad(ge, _gelu(ge)[1])

        dz_before = dzp_ref[...].astype(F32)[BF16_ROWS - 1:BF16_ROWS, :]
        dz_after = dzn_ref[...].astype(F32)[0:1, :]
        dg_prev = jnp.where(i == 0, 0.0, edge_dg(gp_ref[HALO - 2:HALO - 1, :], gp_ref[HALO - 1:HALO, :], gv[0:1, :],
                                                 up_ref[HALO - 1:HALO, :], dz_before))
        dg_next = jnp.where(i == n_tiles - 1, 0.0, edge_dg(gv[ts - 1:ts, :], gn_ref[0:1, :], gn_ref[1:2, :],
                                                           un_ref[0:1, :], dz_after))
        dg_down, dg_up = _shift_rows(dg, dg_prev, dg_next)
        dgp_ref[...] = (dg_up * w0 + dg * w1 + dg_down * w2).astype(dgp_ref.dtype)
        rows = [jnp.sum(dg * down, axis=0, keepdims=True), jnp.sum(dg * gv, axis=0, keepdims=True),
                jnp.sum(dg * up, axis=0, keepdims=True), jnp.sum(dg, axis=0, keepdims=True)]
        part = jnp.concatenate(rows + [jnp.zeros((SUBLANES - len(rows), nf), F32)], axis=0)

        @pl.when(i == 0)
        def _():
            dcw_ref[...] = part

        @pl.when(i > 0)
        def _():
            dcw_ref[...] += part

    small = pl.BlockSpec((None, SUBLANES, nf), lambda j, i: (j, 0, 0))
    return pl.pallas_call(
        body, name=name, grid=(nj, n_tiles),
        in_specs=[cur, prev, nxt] * 2 + list(_halo_specs(ts, s, nf, BF16_ROWS))
        + [small, pl.BlockSpec((None, 1, nf), lambda j, i: (j, 0, 0))],
        out_specs=[cur, cur, small],
        out_shape=[pltpu.HBM((nj, s, nf), BF16), pltpu.HBM((nj, s, nf), BF16),
                   jax.ShapeDtypeStruct((nj, SUBLANES, nf), F32)],
        compiler_params=_params(("parallel", "arbitrary"), 14 * ts * nf * 4))(
            gpre, gpre, gpre, u, u, u, dz, dz, dz, cw, cb)


def _t5_bucket(rel):
    half = N_BUCKETS // 2
    max_exact = half // 2
    n = jnp.abs(rel)
    side = jnp.where(rel > 0, half, 0)
    nf = jnp.maximum(n, 1).astype(F32)
    large = max_exact + (jnp.log(nf / max_exact) / math.log(MAX_DISTANCE / max_exact)
                         * (half - max_exact)).astype(jnp.int32)
    large = jnp.minimum(large, half - 1)
    return side + jnp.where(n < max_exact, n, large)


def bucket_tile(rows, half, dil):
    rel = (jnp.arange(rows + 2 * half)[None, :] - half) - jnp.arange(rows)[:, None]
    return _t5_bucket(rel * dil).astype(jnp.int32)


def bias_build(name, table_t, bucket, h0, nh, half):
    blk, kw = bucket.shape

    def body(t_ref, b_ref, o_ref):
        h = pl.program_id(0)
        bv = b_ref[...]
        acc = jnp.zeros((blk, kw), F32)
        for b in range(N_BUCKETS):
            acc = jnp.where(bv == b, t_ref[h0 + h, b], acc)
        qi = lax.broadcasted_iota(jnp.int32, (blk, kw), 0)
        ci = lax.broadcasted_iota(jnp.int32, (blk, kw), 1)
        o_ref[...] = jnp.where(jnp.abs(ci - half - qi) <= half, acc, NEG_INF)

    return pl.pallas_call(
        body, name=name, grid=(nh,),
        in_specs=[pl.BlockSpec(memory_space=pltpu.SMEM), pl.BlockSpec((blk, kw), lambda h: (0, 0))],
        out_specs=pl.BlockSpec((None, blk, kw), lambda h: (h, 0, 0)),
        out_shape=jax.ShapeDtypeStruct((nh, blk, kw), F32),
        compiler_params=_params(("parallel",), 4 * blk * kw * 4))(table_t, bucket)


def table_grad(name, dbias, bucket):
    nh, blk, kw = dbias.shape

    def body(d_ref, b_ref, o_ref):
        bv = b_ref[...]
        dv = d_ref[...]
        lane = lax.broadcasted_iota(jnp.int32, (SUBLANES, LANES), 1)
        acc = jnp.zeros((SUBLANES, LANES), F32)
        for b in range(N_BUCKETS):
            acc = jnp.where(lane == b, jnp.sum(jnp.where(bv == b, dv, 0.0)), acc)
        o_ref[...] = acc

    return pl.pallas_call(
        body, name=name, grid=(nh,),
        in_specs=[pl.BlockSpec((None, blk, kw), lambda h: (h, 0, 0)), pl.BlockSpec((blk, kw), lambda h: (0, 0))],
        out_specs=pl.BlockSpec((None, SUBLANES, LANES), lambda h: (h, 0, 0)),
        out_shape=jax.ShapeDtypeStruct((nh, SUBLANES, LANES), F32),
        compiler_params=_params(("parallel",), 4 * blk * kw * 4))(dbias, bucket)


class _Band:
    def __init__(self, s, half, q_rows, n_chains, dil):
        self.s, self.half, self.dil, self.n_chains = s, half, dil, n_chains
        self.seg = s // dil
        self.q_rows = min(q_rows, self.seg)
        self.win = self.q_rows + 2 * half
        self.pad = self.seg + 2 * half
        self.nsb = self.seg // self.q_rows
        self.n_items = dil * self.nsb
        assert self.n_items % n_chains == 0 and self.seg % self.q_rows == 0
        self.staged = dil > 1

    def rows_of(self, r):
        return pl.ds(r, self.seg, stride=self.dil) if self.dil > 1 else slice(None)

    def stage_kv(self, dst, src_ref):
        zeros = jnp.zeros((self.half, HEAD_DIM), dst.dtype)
        for r in range(self.dil):
            base = r * self.pad
            dst[base:base + self.half, :] = zeros
            dst[base + self.half + self.seg:base + self.pad, :] = zeros
            dst[base + self.half:base + self.half + self.seg, :] = src_ref[self.rows_of(r), :].astype(dst.dtype)

    def stage(self, dst, src_ref):
        for r in range(self.dil):
            dst[r * self.seg:(r + 1) * self.seg, :] = src_ref[self.rows_of(r), :].astype(dst.dtype)

    def unstage(self, dst_ref, src, add=False):
        for r in range(self.dil):
            val = src[r * self.seg:(r + 1) * self.seg, :].astype(dst_ref.dtype)
            if add:
                val = val + dst_ref[self.rows_of(r), :]
            dst_ref[self.rows_of(r), :] = val

    def offsets(self, item):
        r, sb = item // self.nsb, item % self.nsb
        qoff = pl.multiple_of(r * self.seg + sb * self.q_rows, self.q_rows)
        koff = pl.multiple_of(r * self.pad + sb * self.q_rows, B_BLOCK)
        kpos = sb * self.q_rows - self.half + lax.broadcasted_iota(jnp.int32, (1, self.win), 1)
        edge = jnp.where((kpos >= 0) & (kpos < self.seg), 0.0, NEG_INF)
        return qoff, koff, edge


def band_attn_fwd(name, proj, bias, sink, *, half, q_rows, n_chains, dil, nh, group, cq, ck, cv):
    s, w = proj.shape
    g = _Band(s, half, q_rows, n_chains, dil)
    has_sink = sink is not None

    def body(*refs):
        q_ref, k_ref, v_ref, b_ref = refs[:4]
        s_ref = refs[4] if has_sink else None
        o_ref, l_ref, ks, vs = refs[4 + has_sink:8 + has_sink]
        qs, os_, ls = refs[8 + has_sink:] if g.staged else (None, o_ref, l_ref)
        g.stage_kv(ks, k_ref)
        g.stage_kv(vs, v_ref)
        if g.staged:
            g.stage(qs, q_ref)
        bias_v = b_ref[...]
        sk = s_ref[pl.program_id(0)] if has_sink else None

        def chain(item):
            qoff, koff, edge = g.offsets(item)
            rows = pl.ds(qoff, g.q_rows)
            qv = qs[rows, :] if g.staged else q_ref[rows, :].astype(BF16)
            kw_ = ks[pl.ds(koff, g.win), :]
            vw_ = vs[pl.ds(koff, g.win), :]
            sc = lax.dot_general(qv, kw_, (((1,), (1,)), ((), ())), preferred_element_type=F32) * ATTN_SCALE
            sc = sc + bias_v + edge
            m = jnp.max(sc, axis=-1, keepdims=True)
            if has_sink:
                m = jnp.maximum(m, sk)
            p = jnp.exp(sc - m)
            den = jnp.sum(p, axis=-1, keepdims=True)
            if has_sink:
                den = den + jnp.exp(sk - m)
            out = lax.dot_general(p.astype(BF16), vw_, (((1,), (0,)), ((), ())), preferred_element_type=F32)
            return rows, out / den, jnp.broadcast_to(m + jnp.log(den), (g.q_rows, HEAD_DIM))

        def step(i, carry):
            for rows, out, lse in [chain(i * n_chains + u) for u in range(n_chains)]:
                os_[rows, :] = out
                ls[rows, :] = lse
            return carry

        lax.fori_loop(0, g.n_items // n_chains, step, 0)
        if g.staged:
            g.unstage(o_ref, os_)
            g.unstage(l_ref, ls)

    def col(c0, per):
        return pl.BlockSpec((s, HEAD_DIM), lambda h: (0, c0 // LANES + h // per))

    in_specs = [col(cq, 1), col(ck, group), col(cv, group),
                pl.BlockSpec((None, g.q_rows, g.win), lambda h: (h, 0, 0))]
    args = [proj, proj, proj, bias]
    if has_sink:
        in_specs.append(pl.BlockSpec(memory_space=pltpu.SMEM))
        args.append(sink)
    shape = pltpu.HBM((s, nh * HEAD_DIM), F32)
    scratch = [pltpu.VMEM((dil * g.pad, HEAD_DIM), BF16), pltpu.VMEM((dil * g.pad, HEAD_DIM), BF16)]
    if g.staged:
        scratch += [pltpu.VMEM((s, HEAD_DIM), BF16), pltpu.VMEM((s, HEAD_DIM), F32), pltpu.VMEM((s, HEAD_DIM), F32)]
    return pl.pallas_call(
        body, name=name, grid=(nh,), in_specs=in_specs, out_specs=[col(0, 1), col(0, 1)], out_shape=[shape, shape],
        scratch_shapes=scratch, compiler_params=_params(("parallel",), 16 * s * HEAD_DIM * 4))(*args)


def band_attn_bwd(name, proj, bias, sink, dout, out, lse, dlse, *, half, q_rows, n_chains, dil, nh, group, cq, ck, cv):
    s, w = proj.shape
    g = _Band(s, half, q_rows, n_chains, dil)
    nkv = nh // group
    has_sink = sink is not None
    has_dl = dlse is not None
    n_in = 7 + int(has_sink) + int(has_dl)
    n_out = 4 + int(has_sink)

    def body(*refs):
        ins, outs, scr = refs[:n_in], refs[n_in:n_in + n_out], refs[n_in + n_out:]
        q_ref, k_ref, v_ref, b_ref, do_ref, o_ref, l_ref = ins[:7]
        s_ref = ins[7] if has_sink else None
        dl_ref = ins[n_in - 1] if has_dl else None
        dq_ref, dk_ref, dv_ref, db_ref = outs[:4]
        ks, vs, dks, dvs = scr[:4]
        scr = list(scr[4:])
        dsa = scr.pop(0) if has_sink else None
        if g.staged:
            qs, dos, os_, ls, dqs = scr[:5]
            dls = scr[5] if has_dl else None
            g.stage(qs, q_ref)
            g.stage(dos, do_ref)
            g.stage(os_, o_ref)
            g.stage(ls, l_ref)
            if has_dl:
                g.stage(dls, dl_ref)
        else:
            qs, dos, os_, ls, dqs, dls = None, do_ref, o_ref, l_ref, dq_ref, dl_ref
        h = pl.program_id(0)
        g.stage_kv(ks, k_ref)
        g.stage_kv(vs, v_ref)
        dks[...] = jnp.zeros_like(dks)
        dvs[...] = jnp.zeros_like(dvs)
        db_ref[...] = jnp.zeros_like(db_ref)
        bias_v = b_ref[...]
        if has_sink:
            sk = s_ref[h]
            dsa[...] = jnp.zeros_like(dsa)

        def chain(item):
            qoff, koff, edge = g.offsets(item)
            rows = pl.ds(qoff, g.q_rows)
            win = pl.ds(koff, g.win)
            qv = qs[rows, :] if g.staged else q_ref[rows, :].astype(BF16)
            kw_ = ks[win, :]
            vw_ = vs[win, :]
            sc = lax.dot_general(qv, kw_, (((1,), (1,)), ((), ())), preferred_element_type=F32) * ATTN_SCALE
            lv = ls[rows, :][:, 0:1]
            p = jnp.exp(sc + bias_v + edge - lv)
            dov = dos[rows, :]
            delta = jnp.sum(dov * os_[rows, :], axis=-1, keepdims=True)
            dob = dov.astype(BF16)
            dp = lax.dot_general(dob, vw_, (((1,), (1,)), ((), ())), preferred_element_type=F32)
            t = dp - delta
            if has_dl:
                t = t + dls[rows, :][:, 0:1]
            ds = p * t
            dsb = (ds * ATTN_SCALE).astype(BF16)
            dq = lax.dot_general(dsb, kw_, (((1,), (0,)), ((), ())), preferred_element_type=F32)
            dkc = lax.dot_general(dsb, qv, (((0,), (0,)), ((), ())), preferred_element_type=F32)
            dvc = lax.dot_general(p.astype(BF16), dob, (((0,), (0,)), ((), ())), preferred_element_type=F32)
            dsk = jnp.exp(sk - lv) * delta if has_sink else None
            return rows, win, dq, dkc, dvc, ds, dsk

        def step(i, carry):
            res = [chain(i * n_chains + u) for u in range(n_chains)]
            ds_sum = res[0][5]
            for rr in res[1:]:
                ds_sum = ds_sum + rr[5]
            db_ref[...] += ds_sum
            for rows, win, dq, dkc, dvc, ds, dsk in res:
                dqs[rows, :] = dq
                dks[win, :] += dkc
                dvs[win, :] += dvc
                if has_sink:
                    dsa[...] += dsk
            return carry

        lax.fori_loop(0, g.n_items // n_chains, step, 0)

        if g.staged:
            g.unstage(dq_ref, dqs)

        def emit_kv(add):
            for r in range(dil):
                lo = r * g.pad + half
                for dst_ref, src in ((dk_ref, dks), (dv_ref, dvs)):
                    val = src[lo:lo + g.seg, :]
                    if add:
                        val = val + dst_ref[g.rows_of(r), :]
                    dst_ref[g.rows_of(r), :] = val

        if group == 1:
            emit_kv(False)
        else:
            @pl.when(h % group == 0)
            def _():
                emit_kv(False)

            @pl.when(h % group != 0)
            def _():
                emit_kv(True)
        if has_sink:
            outs[4][...] = jnp.full((SUBLANES, LANES), -jnp.sum(dsa[...]), F32)

    def col(c0, per):
        return pl.BlockSpec((s, HEAD_DIM), lambda h: (0, c0 // LANES + h // per))

    b_spec = pl.BlockSpec((None, g.q_rows, g.win), lambda h: (h, 0, 0))
    in_specs = [col(cq, 1), col(ck, group), col(cv, group), b_spec, col(0, 1), col(0, 1), col(0, 1)]
    args = [proj, proj, proj, bias, dout, out, lse]
    if has_sink:
        in_specs.append(pl.BlockSpec(memory_space=pltpu.SMEM))
        args.append(sink)
    if has_dl:
        in_specs.append(col(0, 1))
        args.append(dlse)
    out_specs = [col(0, 1), col(0, group), col(0, group), b_spec]
    out_shape = [pltpu.HBM((s, nh * HEAD_DIM), F32), pltpu.HBM((s, nkv * HEAD_DIM), F32),
                 pltpu.HBM((s, nkv * HEAD_DIM), F32), jax.ShapeDtypeStruct((nh, g.q_rows, g.win), F32)]
    scratch = [pltpu.VMEM((dil * g.pad, HEAD_DIM), BF16), pltpu.VMEM((dil * g.pad, HEAD_DIM), BF16),
               pltpu.VMEM((dil * g.pad, HEAD_DIM), F32), pltpu.VMEM((dil * g.pad, HEAD_DIM), F32)]
    if has_sink:
        out_specs.append(pl.BlockSpec((None, SUBLANES, LANES), lambda h: (h, 0, 0)))
        out_shape.append(jax.ShapeDtypeStruct((nh, SUBLANES, LANES), F32))
        scratch.append(pltpu.VMEM((g.q_rows, 1), F32))
    if g.staged:
        scratch += [pltpu.VMEM((s, HEAD_DIM), BF16)] + [pltpu.VMEM((s, HEAD_DIM), F32)] * (4 + int(has_dl))
    res = pl.pallas_call(
        body, name=name, grid=(nh,), in_specs=in_specs, out_specs=out_specs, out_shape=out_shape,
        scratch_shapes=scratch, compiler_params=_params(("arbitrary",), 28 * s * HEAD_DIM * 4))(*args)
    return res[0], res[1], res[2], res[3], (res[4] if has_sink else None)


def dil_merge_fwd(name, outs, lses):
    s, w = outs[0].shape
    ts = _pick(s, ROW_TILE_CANDS)
    ng = len(outs)

    def body(*refs):
        o_refs, l_refs, y_ref = refs[:ng], refs[ng:2 * ng], refs[2 * ng]
        ls = [l[...] for l in l_refs]
        mx = ls[0]
        for l in ls[1:]:
            mx = jnp.maximum(mx, l)
        es = [jnp.exp(l - mx) for l in ls]
        tot = es[0]
        for e in es[1:]:
            tot = tot + e
        acc = (es[0] / tot) * o_refs[0][...]
        for e, o in zip(es[1:], o_refs[1:]):
            acc = acc + (e / tot) * o[...]
        y_ref[...] = acc.astype(y_ref.dtype)

    row = pl.BlockSpec((ts, w), lambda i: (i, 0))
    return pl.pallas_call(
        body, name=name, grid=(s // ts,), in_specs=[row] * (2 * ng), out_specs=row,
        out_shape=pltpu.HBM((s, w), BF16),
        compiler_params=_params(("parallel",), 10 * ts * w * 4))(*outs, *lses)


def dil_merge_bwd(name, dy, outs, lses):
    s, w = outs[0].shape
    ts = _pick(s, ROW_TILE_CANDS)
    ng = len(outs)
    nhead = w // HEAD_DIM

    def body(*refs):
        dy_ref = refs[0]
        o_refs, l_refs = refs[1:1 + ng], refs[1 + ng:1 + 2 * ng]
        do_refs, dl_refs = refs[1 + 2 * ng:1 + 3 * ng], refs[1 + 3 * ng:1 + 4 * ng]
        for hh in range(nhead):
            cols = slice(hh * HEAD_DIM, (hh + 1) * HEAD_DIM)
            dyv = dy_ref[:, cols]
            ls = [l[:, cols] for l in l_refs]
            mx = ls[0]
            for l in ls[1:]:
                mx = jnp.maximum(mx, l)
            es = [jnp.exp(l - mx) for l in ls]
            tot = es[0]
            for e in es[1:]:
                tot = tot + e
            alphas = [e / tot for e in es]
            dal = [jnp.broadcast_to(jnp.sum(dyv * o[:, cols], axis=-1, keepdims=True), dyv.shape) for o in o_refs]
            mean = alphas[0] * dal[0]
            for a, d in zip(alphas[1:], dal[1:]):
                mean = mean + a * d
            for g in range(ng):
                do_refs[g][:, cols] = alphas[g] * dyv
                dl_refs[g][:, cols] = alphas[g] * (dal[g] - mean)

    row = pl.BlockSpec((ts, w), lambda i: (i, 0))
    shape = pltpu.HBM((s, w), F32)
    res = pl.pallas_call(
        body, name=name, grid=(s // ts,), in_specs=[row] * (1 + 2 * ng), out_specs=[row] * (2 * ng),
        out_shape=[shape] * (2 * ng),
        compiler_params=_params(("parallel",), 16 * ts * w * 4))(dy, *outs, *lses)
    return res[:ng], res[ng:]


def _adamw(w, g, m, v):
    m = ADAM_B1 * m + (1.0 - ADAM_B1) * g
    v = ADAM_B2 * v + (1.0 - ADAM_B2) * (g * g)
    m_hat = m / (1.0 - ADAM_B1 ** ADAM_STEP)
    v_hat = v / (1.0 - ADAM_B2 ** ADAM_STEP)
    delta = -ADAM_LR * (m_hat / (jnp.sqrt(v_hat) + ADAM_EPS) + ADAM_WD * w)
    return delta, m, v


def _row_tile(r, c, budget=1 << 20):
    if r * c * 4 <= budget or r % SUBLANES:
        return r
    for t in (1024, 512, 256, 128, 64, 32, 16, 8):
        if r % t == 0 and t * c * 4 <= budget:
            return t
    return SUBLANES


def adam_small(name, g, w, m, v):
    def body(g_ref, w_ref, m_ref, v_ref, d_ref, nm_ref, nv_ref):
        d_ref[...], nm_ref[...], nv_ref[...] = _adamw(w_ref[...], g_ref[...], m_ref[...], v_ref[...])

    shape = jax.ShapeDtypeStruct(w.shape, F32)
    return pl.pallas_call(body, name=name, out_shape=[shape, shape, shape])(g, w, m, v)


def reduce_adam(name, mine, theirs, w, m, v):
    nq, r, c = mine.shape
    tr = _row_tile(r, c)

    def body(*refs):
        parts, (w_ref, m_ref, v_ref, g_ref, d_ref, nm_ref, nv_ref) = refs[:nq], refs[nq:]
        g = parts[0][...].astype(F32)
        for p_ref in parts[1:]:
            g = g + p_ref[...].astype(F32)
        g_ref[...] = g
        d_ref[...], nm_ref[...], nv_ref[...] = _adamw(w_ref[...], g, m_ref[...], v_ref[...])

    def slot(q):
        return pl.BlockSpec((None, tr, c), lambda i: (q, i, 0))

    row = pl.BlockSpec((tr, c), lambda i: (i, 0))
    shape = jax.ShapeDtypeStruct((r, c), F32)
    return pl.pallas_call(
        body, name=name, grid=(r // tr,), in_specs=[slot(q) for q in range(nq)] + [row, row, row],
        out_specs=[row] * 4, out_shape=[shape] * 4,
        compiler_params=_params(("parallel",), (nq * 2 + 7 * 4) * tr * c))(mine, *[theirs] * (nq - 1), *[_in_hbm(t) for t in (w, m, v)])


def _place():
    return lax.axis_index("x"), lax.axis_index("y"), lax.axis_index("c")


def _flip(pos, bits):
    return tuple((1 - p) if b else p for p, b in zip(pos, bits))


def _index(pos):
    return 4 * pos[0] + 2 * pos[1] + pos[2]


ANY = pl.BlockSpec(memory_space=pl.ANY)


HBM = pl.BlockSpec(memory_space=pltpu.HBM)
SEM = pl.BlockSpec(memory_space=pltpu.SEMAPHORE)
EFFECT = pltpu.SideEffectType.DATAFLOW_SIDE_EFFECTING
TO_SIBLING = (0, 0, 1)
TO_CHIPS = [(1, 0, 0), (0, 1, 0), (1, 1, 0)]


def _in_hbm(a):
    return pltpu.with_memory_space_constraint(a, pltpu.HBM)


def _token_value(token):
    return token[0, 0]


def _when(pred, fn):
    if pred is True:
        fn()
    elif pred is not False:
        pl.when(pred)(fn)


def _plan_copy(k, entry, ins, lnd, send_sems, recv_sems):
    a, src_a, sblk, lblk, to, send_if, recv_if = entry
    src = lnd[a] if src_a is None else ins[src_a]
    return pltpu.make_async_remote_copy(
        src_ref=src.at[sblk], dst_ref=lnd[a].at[lblk], send_sem=send_sems.at[k], recv_sem=recv_sems.at[k],
        device_id=to, device_id_type=MESH), send_if, recv_if


def split_start(name, srcs, lands, plan, after):
    ns, nl = len(srcs), len(lands)
    n_copies = len(plan((0, 0, 0)))

    def body(*refs):
        ins, lnd = refs[:ns], refs[ns:ns + nl]
        send_sems, recv_sems = refs[ns + nl + 1], refs[ns + nl + 2]
        token = refs[-1]
        for k, entry in enumerate(plan(_place())):
            cp, send_if, _ = _plan_copy(k, entry, ins, lnd, send_sems, recv_sems)
            _when(send_if, cp.start)
        token[...] = jnp.zeros_like(token)

    outs = pl.pallas_call(
        body, name=name,
        out_shape=(pltpu.SemaphoreType.DMA((n_copies,)), pltpu.SemaphoreType.DMA((n_copies,)),
                   *[pltpu.HBM(a.shape, a.dtype) for a in srcs], *[pltpu.HBM(a.shape, a.dtype) for a in lands],
                   jax.ShapeDtypeStruct((SUBLANES, LANES), F32)),
        in_specs=[HBM] * (ns + nl) + [ANY],
        out_specs=(SEM, SEM, *[HBM] * (ns + nl), pl.BlockSpec(memory_space=pltpu.VMEM)),
        input_output_aliases={i: 2 + i for i in range(ns + nl)},
        compiler_params=pltpu.CompilerParams(has_side_effects=EFFECT),
    )(*[_in_hbm(a) for a in srcs], *[_in_hbm(a) for a in lands], after)
    return outs[0], outs[1], list(outs[2:2 + ns]), list(outs[2 + ns:2 + ns + nl]), outs[-1]


def split_wait(name, send_sems, recv_sems, srcs, lands, plan, after):
    ns, nl = len(srcs), len(lands)

    def body(*refs):
        ins, lnd = refs[:ns], refs[ns:ns + nl]
        s_sems, r_sems = refs[ns + nl], refs[ns + nl + 1]
        for k, entry in enumerate(plan(_place())):
            cp, send_if, recv_if = _plan_copy(k, entry, ins, lnd, s_sems, r_sems)
            _when(send_if, cp.wait_send)
            _when(recv_if, cp.wait_recv)
        refs[-1][...] = jnp.zeros((SUBLANES, LANES), F32)

    outs = pl.pallas_call(
        body, name=name,
        out_shape=(*[pltpu.HBM(a.shape, a.dtype) for a in srcs], *[pltpu.HBM(a.shape, a.dtype) for a in lands],
                   jax.ShapeDtypeStruct((SUBLANES, LANES), F32)),
        in_specs=[HBM] * (ns + nl) + [SEM, SEM, ANY],
        out_specs=(*[HBM] * (ns + nl), pl.BlockSpec(memory_space=pltpu.VMEM)),
        input_output_aliases={i: i for i in range(ns + nl)},
        compiler_params=pltpu.CompilerParams(has_side_effects=EFFECT),
    )(*srcs, *lands, send_sems, recv_sems, after)
    return list(outs[:ns]), list(outs[ns:ns + nl]), outs[-1]


NORTH = 1


def ag_plan(n, rels=TO_CHIPS):
    def plan(me):
        x, y, c = me
        entries = []
        for a in range(n):
            for t in (NORTH, 1 - NORTH):
                blk = _index((x, y, t))
                for rel in rels:
                    entries.append((a, None, blk, blk, _flip((x, y, t), rel), c == NORTH, c == t))
        return entries
    return plan


TO_X, TO_Y = TO_CHIPS[0], TO_CHIPS[1]


def relay_plan(n):
    def plan(me):
        x, y, c = me
        entries = []
        for a in range(n):
            for t, came, goes in ((NORTH, TO_X, TO_Y), (1 - NORTH, TO_Y, TO_X)):
                blk = _index(_flip((x, y, t), came))
                entries.append((a, None, blk, blk, _flip((x, y, t), goes), c == t, c == t))
        return entries
    return plan


def ag_pair(name, lands, after):
    n = len(lands)

    def body(*refs):
        lnd = refs[n + 1:2 * n + 1]
        token = refs[2 * n + 1]
        send_sems, recv_sems = refs[2 * n + 2:]
        token[...] = jnp.zeros_like(token)
        me = _place()
        sibling = _flip(me, TO_SIBLING)
        copies = []
        for a in range(n):
            mine, theirs = lnd[a].at[_index(me)], lnd[a].at[_index(sibling)]
            cp = pltpu.make_async_remote_copy(src_ref=mine, dst_ref=mine, send_sem=send_sems.at[a],
                                              recv_sem=recv_sems.at[a], device_id=sibling, device_id_type=MESH)
            cp.start()
            copies.append((cp, pltpu.make_async_remote_copy(
                src_ref=mine, dst_ref=theirs, send_sem=send_sems.at[a], recv_sem=recv_sems.at[a], device_id=sibling,
                device_id_type=MESH)))
        for cp, arrival in copies:
            arrival.wait_recv()
        for cp, arrival in copies:
            cp.wait_send()

    outs = pl.pallas_call(
        body, name=name, in_specs=[ANY] * (n + 1), out_specs=[ANY] * n + [pl.BlockSpec(memory_space=pltpu.VMEM)],
        out_shape=[jax.ShapeDtypeStruct(l.shape, l.dtype) for l in lands]
        + [jax.ShapeDtypeStruct((SUBLANES, LANES), F32)],
        input_output_aliases={a: a for a in range(n)},
        scratch_shapes=[pltpu.SemaphoreType.DMA((n,)), pltpu.SemaphoreType.DMA((n,))],
    )(*lands, after)
    return list(outs[:n]), outs[n]


def pass_plan(n):
    def plan(me):
        sibling = _flip(me, TO_SIBLING)
        return [(a, None, _index(_flip(me, rel)), _index(_flip(me, rel)), sibling, True, True)
                for a in range(n) for rel in TO_CHIPS]
    return plan


def ag_finish(name, lands):
    n = len(lands)

    def body(*refs):
        lnd = refs[n:2 * n]
        send_sems, recv_sems = refs[2 * n:]
        me = _place()
        sibling = _flip(me, TO_SIBLING)
        copies = []
        for a in range(n):
            for j, rel in enumerate(TO_CHIPS):
                blk = lnd[a].at[_index(_flip(me, rel))]
                there = lnd[a].at[_index(_flip(sibling, rel))]
                cp = pltpu.make_async_remote_copy(
                    src_ref=blk, dst_ref=blk, send_sem=send_sems.at[a * 3 + j], recv_sem=recv_sems.at[a * 3 + j],
                    device_id=sibling, device_id_type=MESH)
                cp.start()
                copies.append((cp, pltpu.make_async_remote_copy(
                    src_ref=blk, dst_ref=there, send_sem=send_sems.at[a * 3 + j], recv_sem=recv_sems.at[a * 3 + j],
                    device_id=sibling, device_id_type=MESH)))
        for cp, arrival in copies:
            arrival.wait_recv()
        for cp, arrival in copies:
            cp.wait_send()

    return pl.pallas_call(
        body, name=name, in_specs=[ANY] * n, out_specs=[ANY] * n,
        out_shape=[jax.ShapeDtypeStruct(l.shape, l.dtype) for l in lands],
        input_output_aliases={a: a for a in range(n)},
        scratch_shapes=[pltpu.SemaphoreType.DMA((3 * n,)), pltpu.SemaphoreType.DMA((3 * n,))],
    )(*lands)


REL = [(b >> 2 & 1, b >> 1 & 1, b & 1) for b in range(N_DEV)]


CHIP_REL = [(0, 0, 0)] + TO_CHIPS
N_CHIPS = len(CHIP_REL)


def rs_pair(name, parts):
    n = len(parts)

    def body(*refs):
        ins, got = refs[:n], refs[n:2 * n]
        send_sems, recv_sems = refs[2 * n:]
        me = _place()
        sibling = _flip(me, TO_SIBLING)
        remote = []
        for a in range(n):
            for q, rel in enumerate(CHIP_REL):
                k = a * N_CHIPS + q
                cp = pltpu.make_async_remote_copy(
                    src_ref=ins[a].at[_index(_flip(sibling, rel))], dst_ref=got[a].at[q], send_sem=send_sems.at[k],
                    recv_sem=recv_sems.at[k], device_id=sibling, device_id_type=MESH)
                cp.start()
                remote.append(cp)
        for cp in remote:
            cp.wait_recv()
        for cp in remote:
            cp.wait_send()

    shapes = [jax.ShapeDtypeStruct((N_CHIPS,) + tuple(p.shape[1:]), p.dtype) for p in parts]
    res = pl.pallas_call(
        body, name=name, in_specs=[ANY] * n, out_specs=[ANY] * n, out_shape=shapes,
        scratch_shapes=[pltpu.SemaphoreType.DMA((N_CHIPS * n,)), pltpu.SemaphoreType.DMA((N_CHIPS * n,))],
    )(*parts)
    return list(res)


def own_blocks():
    me = _place()
    return jnp.stack([_index(_flip(me, rel)) for rel in CHIP_REL]).astype(jnp.int32)


def pair_add(name, blocks, parts, got):
    nq, r, c = got.shape
    tr = _row_tile(r, c, budget=6 << 20)

    def body(blk_ref, a_ref, b_ref, o_ref):
        o_ref[...] = (a_ref[...].astype(F32) + b_ref[...].astype(F32)).astype(o_ref.dtype)

    spec = pl.BlockSpec((None, tr, c), lambda q, i, blk: (q, i, 0))
    return pl.pallas_call(
        body, name=name,
        grid_spec=pltpu.PrefetchScalarGridSpec(
            num_scalar_prefetch=1, grid=(nq, r // tr),
            in_specs=[pl.BlockSpec((None, tr, c), lambda q, i, blk: (blk[q], i, 0)), spec], out_specs=spec),
        out_shape=pltpu.HBM(got.shape, got.dtype),
        compiler_params=_params(("arbitrary", "arbitrary"), 6 * tr * c * 2))(blocks, parts, got)


def rs_pair_plan(n):
    def plan(me):
        sibling = _flip(me, TO_SIBLING)
        return [(a, a, _index(_flip(sibling, rel)), q, sibling, True, True)
                for a in range(n) for q, rel in enumerate(CHIP_REL)]
    return plan


def rs_plan(n):
    def plan(me):
        return [(a, a, q, q, _flip(me, CHIP_REL[q]), True, True) for a in range(n) for q in range(1, N_CHIPS)]
    return plan


def rs_start(name, sums, after):
    lands = [lax.empty(t.shape, t.dtype) for t in sums]
    return split_start(name, sums, lands, rs_plan(len(sums)), after)


def allreduce_small(name, pack, after):
    rows, lanes = pack.shape

    def body(x_ref, after_ref, o_ref, land, send_sems, recv_sems):
        me = _place()
        idx = _index(me)
        land[idx] = x_ref[...]
        copies = []
        for r in range(1, N_DEV):
            peer = _flip(me, REL[r])
            cp = pltpu.make_async_remote_copy(
                src_ref=x_ref, dst_ref=land.at[idx], send_sem=send_sems.at[r - 1], recv_sem=recv_sems.at[r - 1],
                device_id=peer, device_id_type=MESH)
            cp.start()
            copies.append(cp)
        for cp in copies:
            cp.wait_recv()
        for cp in copies:
            cp.wait_send()
        acc = land[0]
        for i in range(1, N_DEV):
            acc = acc + land[i]
        o_ref[...] = acc

    return pl.pallas_call(
        body, name=name, in_specs=[pl.BlockSpec(memory_space=pltpu.VMEM), ANY],
        out_specs=pl.BlockSpec(memory_space=pltpu.VMEM), out_shape=jax.ShapeDtypeStruct((rows, lanes), F32),
        scratch_shapes=[pltpu.VMEM((N_DEV, rows, lanes), F32), pltpu.SemaphoreType.DMA((7,)),
                        pltpu.SemaphoreType.DMA((7,))],
    )(pack, after)


def _pad_rows(a, rows):
    return jnp.pad(a, ((0, rows - a.shape[0]), (0, 0)))


def _as_tiles(vec):
    n = vec.shape[0]
    rows = -(-n // LANES)
    rows = -(-rows // SUBLANES) * SUBLANES
    return jnp.pad(vec, (0, rows * LANES - n)).reshape(rows, LANES)


def kernel(x, p, rel_bias_table, attn_norm, w_in, sink_a, w_branch_a, w_branch_b, w_out, ffn_norm, w_ffn_gate, w_ffn_up, conv_w, conv_b, w_ffn_down, ple_norm, w_ple_gate, w_ple_proj, final_norm, loss_target, m_rel_bias_table, m_attn_norm, m_w_in, m_sink_a, m_w_branch_a, m_w_branch_b, m_w_out, m_ffn_norm, m_w_ffn_gate, m_w_ffn_up, m_conv_w, m_conv_b, m_w_ffn_down, m_ple_norm, m_w_ple_gate, m_w_ple_proj, m_final_norm, v_rel_bias_table, v_attn_norm, v_w_in, v_sink_a, v_w_branch_a, v_w_branch_b, v_w_out, v_ffn_norm, v_w_ffn_gate, v_w_ffn_up, v_conv_w, v_conv_b, v_w_ffn_down, v_ple_norm, v_w_ple_gate, v_w_ple_proj, v_final_norm):
    xs = x[0]
    s, d = xs.shape
    ps = p[0, 0]
    target = loss_target[0]
    me = 4 * lax.axis_index("x") + 2 * lax.axis_index("y") + lax.axis_index("c")

    big = dict(w_in=w_in[0], w_branch_a=w_branch_a[0], w_branch_b=w_branch_b[0], w_out=w_out[0],
               w_ffn_gate=w_ffn_gate[0], w_ffn_up=w_ffn_up[0], w_ffn_down=w_ffn_down[0],
               w_ple_gate=w_ple_gate[0], w_ple_proj=w_ple_proj[0])
    big_m = dict(w_in=m_w_in[0], w_branch_a=m_w_branch_a[0], w_branch_b=m_w_branch_b[0], w_out=m_w_out[0],
                 w_ffn_gate=m_w_ffn_gate[0], w_ffn_up=m_w_ffn_up[0], w_ffn_down=m_w_ffn_down[0],
                 w_ple_gate=m_w_ple_gate[0], w_ple_proj=m_w_ple_proj[0])
    big_v = dict(w_in=v_w_in[0], w_branch_a=v_w_branch_a[0], w_branch_b=v_w_branch_b[0], w_out=v_w_out[0],
                 w_ffn_gate=v_w_ffn_gate[0], w_ffn_up=v_w_ffn_up[0], w_ffn_down=v_w_ffn_down[0],
                 w_ple_gate=v_w_ple_gate[0], w_ple_proj=v_w_ple_proj[0])
    names = list(big)
    nf = big["w_ffn_gate"].shape[1]

    shards = {k: big[k].astype(BF16) for k in names}
    shards["conv_w"] = _pad_rows(conv_w[0], SUBLANES)
    flipped = ("w_ffn_gate", "w_ffn_up")
    for k in flipped:
        big[k], big_m[k], big_v[k] = big[k].T, big_m[k].T, big_v[k].T
    ag_groups = [["w_in"], ["w_branch_a", "w_branch_b", "w_out"], ["w_ffn_gate", "conv_w"], ["w_ffn_up"],
                 ["w_ffn_down"], ["w_ple_gate", "w_ple_proj"]]
    ag_started = {}
    wg = {}

    ag_paired, ag_passing = {}, {}

    def pair(gi, after):
        lands = [lax.dynamic_update_index_in_dim(lax.empty((N_DEV,) + shards[k].shape, shards[k].dtype), shards[k],
                                                 me, 0) for k in ag_groups[gi]]
        ag_paired[gi], token = ag_pair(f"ag_pair{gi}", lands, after)
        return token

    def copies(gi):
        return ag_plan(len(ag_groups[gi]), [TO_X, TO_Y] if gi == 0 else TO_CHIPS)

    def start(gi, after):
        s_sems, r_sems, _, lands, token = split_start(f"ag_start{gi}", [], ag_paired[gi], copies(gi), after)
        ag_started[gi] = (s_sems, r_sems, lands)
        return token

    def landed(gi, after):
        s_sems, r_sems, lands = ag_started[gi]
        return split_wait(f"ag_wait{gi}", s_sems, r_sems, [], lands, copies(gi), after)[1:]

    def relayed(lands, after, meanwhile):
        plan = relay_plan(len(lands))
        s_sems, r_sems, _, lands, token = split_start("ag_relay0", [], lands, plan, after)
        return split_wait("ag_relayed0", s_sems, r_sems, [], lands, plan, meanwhile + _token_value(token))[1:]

    def pass_on(gi, lands, after):
        s_sems, r_sems, _, lands, token = split_start(f"ag_pass{gi}", [], lands, pass_plan(len(lands)), after)
        ag_passing[gi] = (s_sems, r_sems, lands)
        return token

    def ready(gi, after):
        s_sems, r_sems, lands = ag_passing[gi]
        lands = split_wait(f"ag_ready{gi}", s_sems, r_sems, [], lands, pass_plan(len(lands)), after)[1]
        wg.update(zip(ag_groups[gi], lands))

    cb = conv_b.reshape(N_DEV, 1, nf)

    table_t = rel_bias_table.T
    geo_a = dict(half=A_BLOCK, q_rows=ATTN_Q_ROWS, n_chains=ATTN_CHAINS, dil=1, nh=A_Q_HEADS, group=A_GROUP,
                 cq=COL_QA, ck=COL_KA, cv=COL_VA)
    geo_b = [dict(half=B_BLOCK, q_rows=min(ATTN_Q_ROWS, s // dil), n_chains=ATTN_CHAINS, dil=dil,
                  nh=B_HEADS_PER_GROUP, group=1, cq=COL_QB + g * B_OUT_W, ck=COL_KB + g * B_OUT_W,
                  cv=COL_VB + g * B_OUT_W) for g, (_, dil) in enumerate(B_PATTERNS)]
    bucket_a = bucket_tile(geo_a["q_rows"], A_BLOCK, 1)
    bias_a = bias_build("bias_a", table_t, bucket_a, 0, A_Q_HEADS, A_BLOCK)
    buckets_b = [bucket_tile(gb["q_rows"], B_BLOCK, gb["dil"]) for gb in geo_b]
    biases_b = [bias_build(f"bias_b{g}", table_t, buckets_b[g], A_Q_HEADS + g * B_HEADS_PER_GROUP, B_HEADS_PER_GROUP,
                           B_BLOCK) for g in range(len(B_PATTERNS))]

    token = start(0, pair(0, xs))
    h = rms_fwd("rms_attn", xs, attn_norm + _token_value(token))
    lands0, token = landed(0, pair(5, pair(4, pair(3, pair(2, pair(1, h))))))
    bias_corner = bias_a[0, :1, :1] + sum(b[0, :1, :1] for b in biases_b)
    lands0, token = relayed(lands0, token, bias_corner)
    token = start(5, start(4, start(3, start(2, start(1, token)))))
    wg["w_in"] = ag_finish("ag_finish0", lands0)[0]
    proj = mm_cols("proj_in", h, wg["w_in"], F32, fold=True, after=token)
    token = pass_on(1, landed(1, proj)[0], proj)
    sink = sink_a[0] + _token_value(token)
    ya, lse_a = band_attn_fwd("attn_a_fwd", proj, bias_a, sink, **geo_a)
    outs_b, lses_b = [], []
    for g in range(len(B_PATTERNS)):
        o, l = band_attn_fwd(f"attn_b{g}_fwd", proj, biases_b[g], None, **geo_b[g])
        outs_b.append(o)
        lses_b.append(l)
    yb = dil_merge_fwd("dil_merge_fwd", outs_b, lses_b)
    ready(1, yb)
    token = pass_on(2, landed(2, yb)[0], yb)
    w_out_full = wg["w_out"].reshape(d, d)
    ta = mm_cols("branch_a", ya, wg["w_branch_a"], F32, fold=True, after=token)
    tb = mm_cols("branch_b", yb, wg["w_branch_b"], F32, fold=True)
    merged = gate_merge_fwd("gate_merge_fwd", proj, ta, tb, d)
    x1 = mm_plain("mix_out", merged, w_out_full, F32, res=xs)

    hf = rms_fwd("rms_ffn", x1, ffn_norm)
    ready(2, hf)
    token = pass_on(3, landed(3, hf)[0], hf)
    cw = wg["conv_w"]
    gpre = mm_cols("ffn_gate", hf, wg["w_ffn_gate"], F32, fold=False, after=token)
    ready(3, gpre)
    token = pass_on(4, landed(4, gpre)[0], gpre)
    u = mm_cols("ffn_up", hf, wg["w_ffn_up"], F32, fold=False, after=token)
    z = ffn_mid_fwd("ffn_mid_fwd", gpre, u, cw, cb)
    ready(4, z)
    token = pass_on(5, landed(5, z)[0], z)
    x2 = mm_jsum("ffn_down", z, wg["w_ffn_down"], F32, res=x1, after=token)

    hp = rms_fwd("rms_ple", x2, ple_norm)
    ready(5, hp)
    w_pg_full = wg["w_ple_gate"].reshape(d, d)
    lp = mm_plain("ple_gate", hp, w_pg_full, F32)
    pp = mm_cols("ple_proj", ps, wg["w_ple_proj"], F32, fold=True)
    loss_part, dx3, dlp, dpp, d_final = tail_fwd_bwd("tail", x2, lp, pp, final_norm.reshape(1, d), target)

    grads = {}
    rs_started = []
    blocks = own_blocks()

    exchanging = []

    def exchange(tag, keys):
        parts = [grads[k] for k in keys]
        lands = [lax.empty((N_CHIPS,) + tuple(p.shape[1:]), p.dtype) for p in parts]
        s_sems, r_sems, parts, lands, token = split_start(f"rs_pair_{tag}", parts, lands, rs_pair_plan(len(keys)), blocks)
        exchanging.append((tag, keys, s_sems, r_sems, parts, lands))
        return _token_value(token)

    def send(after):
        tag, keys, s_sems, r_sems, parts, lands = exchanging.pop(0)
        parts, got, _ = split_wait(f"rs_paired_{tag}", s_sems, r_sems, parts, lands, rs_pair_plan(len(keys)), after)
        return send_sums(tag, keys, parts, got)

    def send_sums(tag, keys, parts, got):
        sums = [pair_add(f"pair_add_{k}", blocks, p, g) for k, p, g in zip(keys, parts, got)]
        s_sems, r_sems, srcs, lands, token = rs_start(f"rs_start_{tag}", sums, blocks)
        rs_started.append((tag, keys, s_sems, r_sems, srcs, lands))
        return token

    grads["w_ple_proj"] = mm_tn_cols("d_w_ple_proj", ps, dpp, N_DEV, big["w_ple_proj"].shape[1], BF16, folded=True)
    grads["w_ple_gate"] = mm_tn_plain("d_w_ple_gate", hp, dlp, BF16).reshape(N_DEV, d // N_DEV, d)
    tok = exchange("ple", ["w_ple_proj", "w_ple_gate"])
    dhp = mm_nt_plain("d_hp", dlp, w_pg_full, F32)
    dx2, dx2_b, d_ple = rms_bwd("rms_ple_bwd", x2, ple_norm + tok, dhp, dx3, True)

    dz = mm_nt_j("d_z", dx2_b, wg["w_ffn_down"], BF16)
    grads["w_ffn_down"] = mm_tn_j("d_w_ffn_down", z, dx2_b, BF16)
    tok = _token_value(send(dz)) + exchange("down", ["w_ffn_down"])
    du, dgpre, dcw = ffn_mid_bwd("ffn_mid_bwd", gpre, u, dz, cw, cb + tok)
    grads["w_ffn_up"] = mm_tn_j("d_w_ffn_up", du, hf, BF16)
    grads["w_ffn_gate"] = mm_tn_j("d_w_ffn_gate", dgpre, hf, BF16)
    dhf = mm_nt_jsum("d_hf_up", du, wg["w_ffn_up"], F32, folded=False)
    dhf = mm_nt_jsum("d_hf_gate", dgpre, wg["w_ffn_gate"], F32, folded=False, res=dhf)
    tok = _token_value(send(dhf)) + exchange("upgate", ["w_ffn_up", "w_ffn_gate"])
    dx1, dx1_b, d_ffn = rms_bwd("rms_ffn_bwd", x1, ffn_norm + tok, dhf, dx2, True)

    dmerged = mm_nt_plain("d_merged", dx1_b, w_out_full, F32)
    grads["w_out"] = mm_tn_plain("d_w_out", merged, dx1_b, BF16).reshape(N_DEV, d // N_DEV, d)
    dta, dtb, dga, dgb = gate_merge_bwd("gate_merge_bwd", dmerged, proj, ta, tb, d)
    grads["w_branch_a"] = mm_tn_cols("d_w_branch_a", ya, dta, N_DEV, big["w_branch_a"].shape[1], BF16, folded=True)
    grads["w_branch_b"] = mm_tn_cols("d_w_branch_b", yb, dtb, N_DEV, big["w_branch_b"].shape[1], BF16, folded=True)
    dya = mm_nt_jsum("d_ya", dta, wg["w_branch_a"], F32, folded=True)
    dyb = mm_nt_jsum("d_yb", dtb, wg["w_branch_b"], F32, folded=True)
    tok = _token_value(send(dyb)) + exchange("mix", ["w_out", "w_branch_a", "w_branch_b"])
    dqa, dka, dva, dbias_a, dsink = band_attn_bwd("attn_a_bwd", proj, bias_a, sink + tok, dya, ya, lse_a, None, **geo_a)
    douts_b, dlses_b = dil_merge_bwd("dil_merge_bwd", dyb, outs_b, lses_b)
    dq_b, dk_b, dv_b, dbias_b = [], [], [], []
    for g in range(len(B_PATTERNS)):
        dq, dk, dv, db, _ = band_attn_bwd(f"attn_b{g}_bwd", proj, biases_b[g], None, douts_b[g], outs_b[g], lses_b[g],
                                          dlses_b[g], **geo_b[g])
        dq_b.append(dq)
        dk_b.append(dk)
        dv_b.append(dv)
        dbias_b.append(db)
    dproj = jnp.concatenate([t.astype(BF16) for t in [dqa, dka, dva] + dq_b + dk_b + dv_b + [dga, dgb]], axis=1)
    token = send(dproj)
    grads["w_in"] = mm_tn_cols("d_w_in", h, dproj, N_DEV, big["w_in"].shape[1], BF16, folded=True, after=token)
    token = send_sums("in", ["w_in"], [grads["w_in"]], rs_pair("rs_pair_in", [grads["w_in"]]))
    dh = mm_nt_jsum("d_h", dproj, wg["w_in"], F32, folded=True, after=token)
    grad_x, _, d_attn = rms_bwd("rms_attn_bwd", xs, attn_norm, dh, dx1, False)

    dt_a = table_grad("table_grad_a", dbias_a, bucket_a)[:, 0, :N_BUCKETS]
    dt_b = [table_grad(f"table_grad_b{g}", dbias_b[g], buckets_b[g])[:, 0, :N_BUCKETS] for g in range(len(B_PATTERNS))]
    d_table_part = jnp.concatenate([dt_a] + dt_b, axis=0).T

    pieces = [
        ("loss", loss_part[0, :1]),
        ("table", d_table_part.reshape(-1)),
        ("attn_norm", d_attn.reshape(-1)),
        ("sink", dsink[:, 0, 0]),
        ("ffn_norm", d_ffn.reshape(-1)),
        ("conv_w", dcw[:, 0:3, :].reshape(-1)),
        ("conv_b", dcw[:, 3, :].reshape(-1)),
        ("ple_norm", d_ple.reshape(-1)),
        ("final_norm", d_final.reshape(-1)),
    ]
    tiles = [_as_tiles(v) for _, v in pieces]
    pack = jnp.concatenate(tiles, axis=0)

    out_g, out_d, out_m, out_v = {}, {}, {}, {}

    def finish(group, after):
        tag, keys, s_sems, r_sems, srcs, lands = group
        srcs, lands, _ = split_wait(f"rs_wait_{tag}", s_sems, r_sems, srcs, lands, rs_plan(len(keys)), after)
        for k, mine, theirs in zip(keys, srcs, lands):
            res = reduce_adam("adam_" + k, mine, theirs, big[k], big_m[k], big_v[k])
            after = res[1]
            out_g[k], out_d[k], out_m[k], out_v[k] = [(t.T if k in flipped else t)[None] for t in res]
        return after

    after = pack
    for group in rs_started[:-1]:
        after = finish(group, after)
    total = allreduce_small("allreduce_small", pack, after)
    finish(rs_started[-1], total)
    small = {}
    row = 0
    for (nm, v), t in zip(pieces, tiles):
        small[nm] = total[row:row + t.shape[0]].reshape(-1)[:v.shape[0]]
        row += t.shape[0]
    loss = small["loss"][0]
    g_small = dict(
        rel_bias_table=small["table"].reshape(rel_bias_table.shape),
        attn_norm=small["attn_norm"].reshape(attn_norm.shape),
        sink_a=small["sink"].reshape(sink_a.shape),
        ffn_norm=small["ffn_norm"].reshape(ffn_norm.shape),
        conv_w=lax.dynamic_index_in_dim(small["conv_w"].reshape(N_DEV, 3, nf), me, 0, keepdims=False)[None],
        conv_b=small["conv_b"].reshape(conv_b.shape),
        ple_norm=small["ple_norm"].reshape(ple_norm.shape),
        final_norm=small["final_norm"].reshape(1, d),
    )
    w_small = dict(rel_bias_table=(rel_bias_table, m_rel_bias_table, v_rel_bias_table),
                   attn_norm=(attn_norm, m_attn_norm, v_attn_norm), sink_a=(sink_a, m_sink_a, v_sink_a),
                   ffn_norm=(ffn_norm, m_ffn_norm, v_ffn_norm), conv_w=(conv_w, m_conv_w, v_conv_w),
                   conv_b=(conv_b, m_conv_b, v_conv_b), ple_norm=(ple_norm, m_ple_norm, v_ple_norm),
                   final_norm=(final_norm, m_final_norm, v_final_norm))

    for k, (wv, mv, vv) in w_small.items():
        shape = wv.shape
        two_d = (1, shape[0]) if len(shape) == 1 else ((shape[0] * shape[1], shape[2]) if len(shape) == 3 else shape)
        gk = g_small[k].reshape(two_d)
        dl, nm, nv = adam_small("adam_" + k, gk, wv.reshape(two_d), mv.reshape(two_d), vv.reshape(two_d))
        out_g[k], out_d[k], out_m[k], out_v[k] = gk.reshape(shape), dl.reshape(shape), nm.reshape(shape), nv.reshape(shape)

    order = ["rel_bias_table", "attn_norm", "w_in", "sink_a", "w_branch_a", "w_branch_b", "w_out", "ffn_norm",
             "w_ffn_gate", "w_ffn_up", "conv_w", "conv_b", "w_ffn_down", "ple_norm", "w_ple_gate", "w_ple_proj",
             "final_norm"]
    return (loss, grad_x[None], *[out_g[k] for k in order], *[out_d[k] for k in order],
            *[out_m[k] for k in order], *[out_v[k] for k in order])
```

```python
import math

import jax
import jax.numpy as jnp
from jax import lax
from jax.experimental import pallas as pl
from jax.experimental.pallas import tpu as pltpu

F32 = jnp.float32
BF16 = jnp.bfloat16
MESH = pl.DeviceIdType.MESH
N_DEV = 8

HEAD_DIM = 128
A_Q_HEADS = 8
A_KV_HEADS = 2
A_GROUP = A_Q_HEADS // A_KV_HEADS
A_BLOCK = 128
B_PATTERNS = ((128, 1), (512, 4), (2048, 16))
B_HEADS_PER_GROUP = 4
B_HEADS = len(B_PATTERNS) * B_HEADS_PER_GROUP
B_BLOCK = 64
N_BUCKETS = 32
MAX_DISTANCE = 1024
A_Q_W = A_Q_HEADS * HEAD_DIM
A_KV_W = A_KV_HEADS * HEAD_DIM
B_W = B_HEADS * HEAD_DIM
B_OUT_W = B_HEADS_PER_GROUP * HEAD_DIM
COL_QA = 0
COL_KA = COL_QA + A_Q_W
COL_VA = COL_KA + A_KV_W
COL_QB = COL_VA + A_KV_W
COL_KB = COL_QB + B_W
COL_VB = COL_KB + B_W
COL_GATES = COL_VB + B_W
RMS_EPS = 1e-6
NEG_INF = -1e30
ATTN_SCALE = HEAD_DIM ** -0.5
ATTN_Q_ROWS = 256
ATTN_CHAINS = 4

ADAM_LR = 0.001
ADAM_B1 = 0.9
ADAM_B2 = 0.999
ADAM_EPS = 1e-08
ADAM_WD = 0.01
ADAM_STEP = 10

GELU_C = math.sqrt(2.0 / math.pi)
GELU_A = 0.044715

V7X_VMEM_BYTES = 64 * 1024 * 1024
VMEM_CEILING = V7X_VMEM_BYTES - 8 * 1024 * 1024
LANES = 128
SUBLANES = 8


def _pick(n, cands):
    for c in cands:
        if n % c == 0:
            return c
    return n


def _nbytes(shape, dtype):
    n = 1
    for d in shape:
        if d is not None:
            n *= d
    return n * jnp.dtype(dtype).itemsize


def _params(sem, est_bytes):
    limit = int(min(VMEM_CEILING, max(32 * 1024 * 1024, 2 * est_bytes + (8 << 20))))
    return pltpu.CompilerParams(dimension_semantics=sem, vmem_limit_bytes=limit)


def _mm(name, a, b, a_bs, a_im, b_bs, b_im, out_shape, out_dtype, o_bs, o_im, grid, dims,
        res=None, r_bs=None, r_im=None, after=None):
    nk = grid[-1]
    nax = len(grid)
    has_res = res is not None
    has_after = after is not None
    o_tile = tuple(d for d in o_bs if d is not None)

    def body(*refs):
        a_ref, b_ref = refs[:2]
        r_ref = refs[2] if has_res else None
        n_in = 2 + has_res + has_after
        o_ref = refs[n_in]
        rest = refs[n_in + 1:]

        def prod():
            return lax.dot_general(a_ref[...].astype(BF16), b_ref[...].astype(BF16), (dims, ((), ())),
                                   preferred_element_type=F32)

        def finish(r):
            if r_ref is not None:
                r = r + r_ref[...].astype(F32)
            o_ref[...] = r.astype(o_ref.dtype)

        if nk == 1:
            finish(prod())
        else:
            acc = rest[0]
            k = pl.program_id(nax - 1)

            @pl.when(k == 0)
            def _():
                acc[...] = prod()

            @pl.when(k > 0)
            def _():
                acc[...] += prod()

            @pl.when(k == nk - 1)
            def _():
                finish(acc[...])

    in_specs = [pl.BlockSpec(a_bs, a_im), pl.BlockSpec(b_bs, b_im)]
    args = [a, b]
    est = _nbytes(a_bs, a.dtype) + _nbytes(b_bs, b.dtype) + _nbytes(o_bs, out_dtype) + 2 * _nbytes(o_tile, F32)
    if has_res:
        in_specs.append(pl.BlockSpec(r_bs, r_im))
        args.append(res)
        est += _nbytes(r_bs, res.dtype)
    if has_after:
        in_specs.append(pl.BlockSpec(memory_space=pl.ANY))
        args.append(after)
    scratch = [] if nk == 1 else [pltpu.VMEM(o_tile, F32)]
    sem = ("parallel",) * (nax - 1) + ("arbitrary",)
    return pl.pallas_call(
        body, name=name, grid=grid, in_specs=in_specs, out_specs=pl.BlockSpec(o_bs, o_im),
        out_shape=pltpu.HBM(out_shape, out_dtype), scratch_shapes=scratch,
        compiler_params=_params(sem, est))(*args)


TM_CANDS = (1024, 512, 256, 128, 64, 32, 16, 8)
TM_WIDE_CANDS = (2048,) + TM_CANDS
MM_WHOLE_K_BYTES = 48 * 1024 * 1024


def _whole(k, tile_bytes):
    return k if 2 * tile_bytes(k) <= MM_WHOLE_K_BYTES else _pick(k, TK_CANDS)
TK_CANDS = (1024, 512, 256, 128)
TN_CANDS = (1024, 512, 256, 128)


def mm_cols(name, a, wg, out_dtype, fold, after=None):
    m, k = a.shape
    nj, _, n = wg.shape
    tm = _pick(m, TM_WIDE_CANDS)
    tk = _whole(k, lambda t: _nbytes((tm, t), a.dtype) + _nbytes((t, n), wg.dtype) + _nbytes((tm, n), out_dtype))
    grid = (nj, m // tm, k // tk)
    if fold:
        shape, o_bs, o_im = (m, nj * n), (tm, n), (lambda j, i, kk: (i, j))
    else:
        shape, o_bs, o_im = (nj, m, n), (None, tm, n), (lambda j, i, kk: (j, i, 0))
    return _mm(name, a, wg, (tm, tk), lambda j, i, kk: (i, kk), (None, tk, n), lambda j, i, kk: (j, kk, 0),
               shape, out_dtype, o_bs, o_im, grid, ((1,), (0,)), after=after)


def mm_plain(name, a, w, out_dtype, res=None):
    m, k = a.shape
    n = w.shape[1]
    tm, tk, tn = _pick(m, TM_CANDS if res is not None else TM_WIDE_CANDS), _pick(k, TK_CANDS), _pick(n, TN_CANDS)
    if res is None:
        tk = _whole(k, lambda t: _nbytes((tm, t), a.dtype) + _nbytes((t, tn), w.dtype) + _nbytes((tm, tn), out_dtype))
    grid = (n // tn, m // tm, k // tk)
    return _mm(name, a, w, (tm, tk), lambda j, i, kk: (i, kk), (tk, tn), lambda j, i, kk: (kk, j),
               (m, n), out_dtype, (tm, tn), lambda j, i, kk: (i, j), grid, ((1,), (0,)),
               res, (tm, tn), lambda j, i, kk: (i, j))


def mm_jsum(name, aj, wg, out_dtype, res=None, after=None):
    nj, m, ka = aj.shape
    n = wg.shape[2]
    tm, tn = _pick(m, TM_CANDS if res is not None else TM_WIDE_CANDS), _pick(n, TN_CANDS)
    grid = (m // tm, n // tn, nj)
    return _mm(name, aj, wg, (None, tm, ka), lambda i, jn, j: (j, i, 0), (None, ka, tn), lambda i, jn, j: (j, 0, jn),
               (m, n), out_dtype, (tm, tn), lambda i, jn, j: (i, jn), grid, ((1,), (0,)),
               res, (tm, tn), lambda i, jn, j: (i, jn), after=after)


def mm_tn_cols(name, a, g, nj, n, out_dtype, folded, after=None):
    s, kw = a.shape
    tkw = _pick(kw, TM_WIDE_CANDS)
    ts = _whole(s, lambda t: _nbytes((t, tkw), a.dtype) + _nbytes((t, n), g.dtype) + _nbytes((tkw, n), out_dtype))
    grid = (nj, kw // tkw, s // ts)
    if folded:
        g_bs, g_im = (ts, n), (lambda j, i, ss: (ss, j))
    else:
        g_bs, g_im = (None, ts, n), (lambda j, i, ss: (j, ss, 0))
    return _mm(name, a, g, (ts, tkw), lambda j, i, ss: (ss, i), g_bs, g_im,
               (nj, kw, n), out_dtype, (None, tkw, n), lambda j, i, ss: (j, i, 0), grid, ((0,), (0,)), after=after)


def mm_tn_plain(name, a, g, out_dtype):
    s, kw = a.shape
    n = g.shape[1]
    tkw, tn = _pick(kw, TM_WIDE_CANDS), _pick(n, TN_CANDS)
    ts = _whole(s, lambda t: _nbytes((t, tkw), a.dtype) + _nbytes((t, tn), g.dtype) + _nbytes((tkw, tn), out_dtype))
    grid = (kw // tkw, n // tn, s // ts)
    return _mm(name, a, g, (ts, tkw), lambda i, jn, ss: (ss, i), (ts, tn), lambda i, jn, ss: (ss, jn),
               (kw, n), out_dtype, (tkw, tn), lambda i, jn, ss: (i, jn), grid, ((0,), (0,)))


def mm_tn_j(name, aj, g, out_dtype):
    nj, s, ka = aj.shape
    n = g.shape[1]
    tn = _pick(n, TM_WIDE_CANDS)
    ts = _whole(s, lambda t: _nbytes((t, ka), aj.dtype) + _nbytes((t, tn), g.dtype) + _nbytes((ka, tn), out_dtype))
    grid = (nj, n // tn, s // ts)
    return _mm(name, aj, g, (None, ts, ka), lambda j, jn, ss: (j, ss, 0), (ts, tn), lambda j, jn, ss: (ss, jn),
               (nj, ka, n), out_dtype, (None, ka, tn), lambda j, jn, ss: (j, 0, jn), grid, ((0,), (0,)))


def mm_nt_plain(name, g, w, out_dtype):
    m, n = g.shape
    k = w.shape[0]
    tm, tkk = _pick(m, TM_WIDE_CANDS), _pick(k, TN_CANDS)
    tn = _whole(n, lambda t: _nbytes((tm, t), g.dtype) + _nbytes((tkk, t), w.dtype) + _nbytes((tm, tkk), out_dtype))
    grid = (k // tkk, m // tm, n // tn)
    return _mm(name, g, w, (tm, tn), lambda kk, i, jn: (i, jn), (tkk, tn), lambda kk, i, jn: (kk, jn),
               (m, k), out_dtype, (tm, tkk), lambda kk, i, jn: (i, kk), grid, ((1,), (1,)))


def mm_nt_j(name, g, wg, out_dtype):
    m, n = g.shape
    nj, ka, _ = wg.shape
    tm = _pick(m, TM_WIDE_CANDS)
    tn = _whole(n, lambda t: _nbytes((tm, t), g.dtype) + _nbytes((ka, t), wg.dtype) + _nbytes((tm, ka), out_dtype))
    grid = (nj, m // tm, n // tn)
    return _mm(name, g, wg, (tm, tn), lambda j, i, jn: (i, jn), (None, ka, tn), lambda j, i, jn: (j, 0, jn),
               (nj, m, ka), out_dtype, (None, tm, ka), lambda j, i, jn: (j, i, 0), grid, ((1,), (1,)))


def mm_nt_jsum(name, g, wg, out_dtype, folded, res=None, after=None):
    nj, k, n = wg.shape
    m = g.shape[0] if folded else g.shape[1]
    tm, tkk = _pick(m, TM_CANDS if res is not None else TM_WIDE_CANDS), _pick(k, TN_CANDS)
    grid = (m // tm, k // tkk, nj)
    if folded:
        g_bs, g_im = (tm, n), (lambda i, kk, j: (i, j))
    else:
        g_bs, g_im = (None, tm, n), (lambda i, kk, j: (j, i, 0))
    return _mm(name, g, wg, g_bs, g_im, (None, tkk, n), lambda i, kk, j: (j, kk, 0),
               (m, k), out_dtype, (tm, tkk), lambda i, kk, j: (i, kk), grid, ((1,), (1,)),
               res, (tm, tkk), lambda i, kk, j: (i, kk), after=after)


ROW_TILE_CANDS = (256, 128, 64, 32, 16, 8)


def _rstd(x):
    return lax.rsqrt(jnp.mean(x * x, axis=-1, keepdims=True) + RMS_EPS)


def _sigmoid(t):
    return 1.0 / (1.0 + jnp.exp(-t))


def rms_fwd(name, x, gain):
    s, d = x.shape
    ts = _pick(s, ROW_TILE_CANDS)

    def body(x_ref, g_ref, h_ref):
        xv = x_ref[...]
        h_ref[...] = ((xv * _rstd(xv)) * g_ref[...]).astype(h_ref.dtype)

    return pl.pallas_call(
        body, name=name, grid=(s // ts,),
        in_specs=[pl.BlockSpec((ts, d), lambda i: (i, 0)), pl.BlockSpec((1, d), lambda i: (0, 0))],
        out_specs=pl.BlockSpec((ts, d), lambda i: (i, 0)),
        out_shape=pltpu.HBM((s, d), BF16),
        compiler_params=_params(("parallel",), 3 * ts * d * 4))(x, gain)


def rms_bwd(name, x, gain, dh, dres, bf16_copy):
    s, d = x.shape
    ts = _pick(s, ROW_TILE_CANDS)

    def body(x_ref, g_ref, dh_ref, dr_ref, dx_ref, *rest):
        dxb_ref, dg_ref = rest if bf16_copy else (None, rest[0])
        xv = x_ref[...]
        r = _rstd(xv)
        xhat = xv * r
        dhv = dh_ref[...].astype(F32)
        dxhat = dhv * g_ref[...]
        dx = dr_ref[...] + r * (dxhat - xhat * jnp.mean(dxhat * xhat, axis=-1, keepdims=True))
        dx_ref[...] = dx
        if bf16_copy:
            dxb_ref[...] = dx.astype(dxb_ref.dtype)
        part = jnp.sum(dhv * xhat, axis=0, keepdims=True)

        @pl.when(pl.program_id(0) == 0)
        def _():
            dg_ref[...] = part

        @pl.when(pl.program_id(0) > 0)
        def _():
            dg_ref[...] += part

    row = pl.BlockSpec((ts, d), lambda i: (i, 0))
    vec = pl.BlockSpec((1, d), lambda i: (0, 0))
    copy_spec, copy_shape = ([row], [pltpu.HBM((s, d), BF16)]) if bf16_copy else ([], [])
    res = pl.pallas_call(
        body, name=name, grid=(s // ts,), in_specs=[row, vec, row, row], out_specs=[row] + copy_spec + [vec],
        out_shape=[pltpu.HBM((s, d), F32)] + copy_shape + [jax.ShapeDtypeStruct((1, d), F32)],
        compiler_params=_params(("arbitrary",), 7 * ts * d * 4))(x, gain, dh, dres)
    return (res[0], res[1], res[2]) if bf16_copy else (res[0], None, res[1])


def gate_merge_fwd(name, proj, ta, tb, d):
    s = proj.shape[0]
    ts = _pick(s, ROW_TILE_CANDS)
    cb = COL_GATES // d

    def body(ga_ref, gb_ref, ta_ref, tb_ref, o_ref):
        o_ref[...] = (_sigmoid(ga_ref[...]) * ta_ref[...] + _sigmoid(gb_ref[...]) * tb_ref[...]).astype(o_ref.dtype)

    row = pl.BlockSpec((ts, d), lambda i: (i, 0))
    return pl.pallas_call(
        body, name=name, grid=(s // ts,),
        in_specs=[pl.BlockSpec((ts, d), lambda i: (i, cb)), pl.BlockSpec((ts, d), lambda i: (i, cb + 1)), row, row],
        out_specs=row, out_shape=pltpu.HBM((s, d), BF16),
        compiler_params=_params(("parallel",), 5 * ts * d * 4))(proj, proj, ta, tb)


def gate_merge_bwd(name, dmerged, proj, ta, tb, d):
    s = proj.shape[0]
    ts = _pick(s, ROW_TILE_CANDS)
    cb = COL_GATES // d

    def body(dm_ref, ga_ref, gb_ref, ta_ref, tb_ref, dta_ref, dtb_ref, dga_ref, dgb_ref):
        dm = dm_ref[...]
        sa = _sigmoid(ga_ref[...])
        sb = _sigmoid(gb_ref[...])
        dta_ref[...] = (dm * sa).astype(dta_ref.dtype)
        dtb_ref[...] = (dm * sb).astype(dtb_ref.dtype)
        dga_ref[...] = (dm * ta_ref[...] * (sa * (1.0 - sa))).astype(dga_ref.dtype)
        dgb_ref[...] = (dm * tb_ref[...] * (sb * (1.0 - sb))).astype(dgb_ref.dtype)

    row = pl.BlockSpec((ts, d), lambda i: (i, 0))
    out = pltpu.HBM((s, d), BF16)
    return pl.pallas_call(
        body, name=name, grid=(s // ts,),
        in_specs=[row, pl.BlockSpec((ts, d), lambda i: (i, cb)), pl.BlockSpec((ts, d), lambda i: (i, cb + 1)), row, row],
        out_specs=[row, row, row, row], out_shape=[out, out, out, out],
        compiler_params=_params(("parallel",), 8 * ts * d * 4))(dmerged, proj, proj, ta, tb)


def tail_fwd_bwd(name, x2, lp, pp, gain, target):
    s, d = x2.shape
    ts = _pick(s, ROW_TILE_CANDS)

    def body(x2_ref, lp_ref, pp_ref, g_ref, t_ref, loss_ref, dx3_ref, dlp_ref, dpp_ref, dg_ref):
        gp = _sigmoid(lp_ref[...])
        ppv = pp_ref[...]
        x3 = x2_ref[...] + gp * ppv
        r = _rstd(x3)
        xhat = x3 * r
        gv = g_ref[...]
        err = xhat * gv - t_ref[...]
        loss = jnp.sum(err * err) * (0.5 / d)
        dy = err * (1.0 / d)
        dxhat = dy * gv
        dx3 = r * (dxhat - xhat * jnp.mean(dxhat * xhat, axis=-1, keepdims=True))
        dx3_ref[...] = dx3
        dlp_ref[...] = (dx3 * ppv * (gp * (1.0 - gp))).astype(dlp_ref.dtype)
        dpp_ref[...] = (dx3 * gp).astype(dpp_ref.dtype)
        part = jnp.sum(dy * xhat, axis=0, keepdims=True)
        lossv = jnp.full((1, LANES), loss, F32)

        @pl.when(pl.program_id(0) == 0)
        def _():
            dg_ref[...] = part
            loss_ref[...] = lossv

        @pl.when(pl.program_id(0) > 0)
        def _():
            dg_ref[...] += part
            loss_ref[...] += lossv

    row = pl.BlockSpec((ts, d), lambda i: (i, 0))
    vec = pl.BlockSpec((1, d), lambda i: (0, 0))
    return pl.pallas_call(
        body, name=name, grid=(s // ts,), in_specs=[row, row, row, vec, row],
        out_specs=[pl.BlockSpec((1, LANES), lambda i: (0, 0)), row, row, row, vec],
        out_shape=[jax.ShapeDtypeStruct((1, LANES), F32), pltpu.HBM((s, d), F32),
                   pltpu.HBM((s, d), BF16), pltpu.HBM((s, d), BF16),
                   jax.ShapeDtypeStruct((1, d), F32)],
        compiler_params=_params(("arbitrary",), 9 * ts * d * 4))(x2, lp, pp, gain, target)


HALO = SUBLANES
BF16_ROWS = 2 * SUBLANES


def _shift_rows(cur, prev_row, next_row):
    ts = cur.shape[0]
    rid = lax.broadcasted_iota(jnp.int32, cur.shape, 0)
    down = jnp.where(rid == 0, prev_row, pltpu.roll(cur, 1, 0))
    up = jnp.where(rid == ts - 1, next_row, pltpu.roll(cur, ts - 1, 0))
    return down, up


def _halo_specs(ts, s, nf, halo=HALO):
    nb = ts // halo
    last = s // halo - 1
    cur = pl.BlockSpec((None, ts, nf), lambda j, i: (j, i, 0))
    prev = pl.BlockSpec((None, halo, nf), lambda j, i: (j, jnp.maximum(i * nb - 1, 0), 0))
    nxt = pl.BlockSpec((None, halo, nf), lambda j, i: (j, jnp.minimum((i + 1) * nb, last), 0))
    return cur, prev, nxt


def _halo_rows(prev_ref, next_ref, n_tiles):
    i = pl.program_id(1)
    prev_row = jnp.where(i == 0, 0.0, prev_ref[HALO - 1:HALO, :].astype(F32))
    next_row = jnp.where(i == n_tiles - 1, 0.0, next_ref[0:1, :].astype(F32))
    return prev_row, next_row


def _gelu(g):
    t = jnp.tanh(GELU_C * (g + GELU_A * (g * g * g)))
    return 0.5 * g * (1.0 + t), t


def _conv(cur, down, up, cw_ref, cb_ref):
    return down * cw_ref[0:1, :] + cur * cw_ref[1:2, :] + up * cw_ref[2:3, :] + cb_ref[...]


def ffn_mid_fwd(name, gpre, u, cw, cb):
    nj, s, nf = gpre.shape
    ts = _pick(s, (512, 256, 128, 64, 32, 16, 8))
    n_tiles = s // ts
    cur, prev, nxt = _halo_specs(ts, s, nf)

    def body(g_ref, gp_ref, gn_ref, u_ref, cw_ref, cb_ref, z_ref):
        gv = g_ref[...]
        down, up = _shift_rows(gv, *_halo_rows(gp_ref, gn_ref, n_tiles))
        act, _ = _gelu(_conv(gv, down, up, cw_ref, cb_ref))
        z_ref[...] = (act * u_ref[...]).astype(z_ref.dtype)

    return pl.pallas_call(
        body, name=name, grid=(nj, n_tiles),
        in_specs=[cur, prev, nxt, cur, pl.BlockSpec((None, SUBLANES, nf), lambda j, i: (j, 0, 0)),
                  pl.BlockSpec((None, 1, nf), lambda j, i: (j, 0, 0))],
        out_specs=cur, out_shape=pltpu.HBM((nj, s, nf), BF16),
        compiler_params=_params(("parallel", "parallel"), 8 * ts * nf * 4))(gpre, gpre, gpre, u, cw, cb)


def _gelu_grad(g, t):
    return 0.5 * (1.0 + t) + 0.5 * g * (1.0 - t * t) * (GELU_C * (1.0 + 3.0 * GELU_A * (g * g)))


def ffn_mid_bwd(name, gpre, u, dz, cw, cb):
    nj, s, nf = gpre.shape
    ts = _pick(s, (512, 256, 128, 64, 32, 16, 8))
    n_tiles = s // ts
    cur, prev, nxt = _halo_specs(ts, s, nf)

    def body(g_ref, gp_ref, gn_ref, u_ref, up_ref, un_ref, dz_ref, dzp_ref, dzn_ref, cw_ref, cb_ref,
             du_ref, dgp_ref, dcw_ref):
        i = pl.program_id(1)
        w0, w1, w2, bias = cw_ref[0:1, :], cw_ref[1:2, :], cw_ref[2:3, :], cb_ref[...]
        gv = g_ref[...]
        down, up = _shift_rows(gv, *_halo_rows(gp_ref, gn_ref, n_tiles))
        gc = down * w0 + gv * w1 + up * w2 + bias
        act, t = _gelu(gc)
        dzv = dz_ref[...].astype(F32)
        du_ref[...] = (dzv * act).astype(du_ref.dtype)
        dg = dzv * u_ref[...] * _gelu_grad(gc, t)

        def edge_dg(g_before, g_at, g_after, u_at, dz_at):
            ge = g_before.astype(F32) * w0 + g_at.astype(F32) * w1 + g_after.astype(F32) * w2 + bias
            return dz_at.astype(F32) * u_at.astype(F32) * _gelu_grad(ge, _gelu(ge)[1])

        dz_before = dzp_ref[...].astype(F32)[BF16_ROWS - 1:BF16_ROWS, :]
        dz_after = dzn_ref[...].astype(F32)[0:1, :]
        dg_prev = jnp.where(i == 0, 0.0, edge_dg(gp_ref[HALO - 2:HALO - 1, :], gp_ref[HALO - 1:HALO, :], gv[0:1, :],
                                                 up_ref[HALO - 1:HALO, :], dz_before))
        dg_next = jnp.where(i == n_tiles - 1, 0.0, edge_dg(gv[ts - 1:ts, :], gn_ref[0:1, :], gn_ref[1:2, :],
                                                           un_ref[0:1, :], dz_after))
        dg_down, dg_up = _shift_rows(dg, dg_prev, dg_next)
        dgp_ref[...] = (dg_up * w0 + dg * w1 + dg_down * w2).astype(dgp_ref.dtype)
        rows = [jnp.sum(dg * down, axis=0, keepdims=True), jnp.sum(dg * gv, axis=0, keepdims=True),
                jnp.sum(dg * up, axis=0, keepdims=True), jnp.sum(dg, axis=0, keepdims=True)]
        part = jnp.concatenate(rows + [jnp.zeros((SUBLANES - len(rows), nf), F32)], axis=0)

        @pl.when(i == 0)
        def _():
            dcw_ref[...] = part

        @pl.when(i > 0)
        def _():
            dcw_ref[...] += part

    small = pl.BlockSpec((None, SUBLANES, nf), lambda j, i: (j, 0, 0))
    return pl.pallas_call(
        body, name=name, grid=(nj, n_tiles),
        in_specs=[cur, prev, nxt] * 2 + list(_halo_specs(ts, s, nf, BF16_ROWS))
        + [small, pl.BlockSpec((None, 1, nf), lambda j, i: (j, 0, 0))],
        out_specs=[cur, cur, small],
        out_shape=[pltpu.HBM((nj, s, nf), BF16), pltpu.HBM((nj, s, nf), BF16),
                   jax.ShapeDtypeStruct((nj, SUBLANES, nf), F32)],
        compiler_params=_params(("parallel", "arbitrary"), 14 * ts * nf * 4))(
            gpre, gpre, gpre, u, u, u, dz, dz, dz, cw, cb)


def _t5_bucket(rel):
    half = N_BUCKETS // 2
    max_exact = half // 2
    n = jnp.abs(rel)
    side = jnp.where(rel > 0, half, 0)
    nf = jnp.maximum(n, 1).astype(F32)
    large = max_exact + (jnp.log(nf / max_exact) / math.log(MAX_DISTANCE / max_exact)
                         * (half - max_exact)).astype(jnp.int32)
    large = jnp.minimum(large, half - 1)
    return side + jnp.where(n < max_exact, n, large)


def bucket_tile(rows, half, dil):
    rel = (jnp.arange(rows + 2 * half)[None, :] - half) - jnp.arange(rows)[:, None]
    return _t5_bucket(rel * dil).astype(jnp.int32)


def bias_build(name, table_t, bucket, h0, nh, half):
    blk, kw = bucket.shape

    def body(t_ref, b_ref, o_ref):
        h = pl.program_id(0)
        bv = b_ref[...]
        acc = jnp.zeros((blk, kw), F32)
        for b in range(N_BUCKETS):
            acc = jnp.where(bv == b, t_ref[h0 + h, b], acc)
        qi = lax.broadcasted_iota(jnp.int32, (blk, kw), 0)
        ci = lax.broadcasted_iota(jnp.int32, (blk, kw), 1)
        o_ref[...] = jnp.where(jnp.abs(ci - half - qi) <= half, acc, NEG_INF)

    return pl.pallas_call(
        body, name=name, grid=(nh,),
        in_specs=[pl.BlockSpec(memory_space=pltpu.SMEM), pl.BlockSpec((blk, kw), lambda h: (0, 0))],
        out_specs=pl.BlockSpec((None, blk, kw), lambda h: (h, 0, 0)),
        out_shape=jax.ShapeDtypeStruct((nh, blk, kw), F32),
        compiler_params=_params(("parallel",), 4 * blk * kw * 4))(table_t, bucket)


def table_grad(name, dbias, bucket):
    nh, blk, kw = dbias.shape

    def body(d_ref, b_ref, o_ref):
        bv = b_ref[...]
        dv = d_ref[...]
        lane = lax.broadcasted_iota(jnp.int32, (SUBLANES, LANES), 1)
        acc = jnp.zeros((SUBLANES, LANES), F32)
        for b in range(N_BUCKETS):
            acc = jnp.where(lane == b, jnp.sum(jnp.where(bv == b, dv, 0.0)), acc)
        o_ref[...] = acc

    return pl.pallas_call(
        body, name=name, grid=(nh,),
        in_specs=[pl.BlockSpec((None, blk, kw), lambda h: (h, 0, 0)), pl.BlockSpec((blk, kw), lambda h: (0, 0))],
        out_specs=pl.BlockSpec((None, SUBLANES, LANES), lambda h: (h, 0, 0)),
        out_shape=jax.ShapeDtypeStruct((nh, SUBLANES, LANES), F32),
        compiler_params=_params(("parallel",), 4 * blk * kw * 4))(dbias, bucket)


class _Band:
    def __init__(self, s, half, q_rows, n_chains, dil):
        self.s, self.half, self.dil, self.n_chains = s, half, dil, n_chains
        self.seg = s // dil
        self.q_rows = min(q_rows, self.seg)
        self.win = self.q_rows + 2 * half
        self.pad = self.seg + 2 * half
        self.nsb = self.seg // self.q_rows
        self.n_items = dil * self.nsb
        assert self.n_items % n_chains == 0 and self.seg % self.q_rows == 0
        self.staged = dil > 1

    def rows_of(self, r):
        return pl.ds(r, self.seg, stride=self.dil) if self.dil > 1 else slice(None)

    def stage_kv(self, dst, src_ref):
        zeros = jnp.zeros((self.half, HEAD_DIM), dst.dtype)
        for r in range(self.dil):
            base = r * self.pad
            dst[base:base + self.half, :] = zeros
            dst[base + self.half + self.seg:base + self.pad, :] = zeros
            dst[base + self.half:base + self.half + self.seg, :] = src_ref[self.rows_of(r), :].astype(dst.dtype)

    def stage(self, dst, src_ref):
        for r in range(self.dil):
            dst[r * self.seg:(r + 1) * self.seg, :] = src_ref[self.rows_of(r), :].astype(dst.dtype)

    def unstage(self, dst_ref, src, add=False):
        for r in range(self.dil):
            val = src[r * self.seg:(r + 1) * self.seg, :].astype(dst_ref.dtype)
            if add:
                val = val + dst_ref[self.rows_of(r), :]
            dst_ref[self.rows_of(r), :] = val

    def offsets(self, item):
        r, sb = item // self.nsb, item % self.nsb
        qoff = pl.multiple_of(r * self.seg + sb * self.q_rows, self.q_rows)
        koff = pl.multiple_of(r * self.pad + sb * self.q_rows, B_BLOCK)
        kpos = sb * self.q_rows - self.half + lax.broadcasted_iota(jnp.int32, (1, self.win), 1)
        edge = jnp.where((kpos >= 0) & (kpos < self.seg), 0.0, NEG_INF)
        return qoff, koff, edge


def band_attn_fwd(name, proj, bias, sink, *, half, q_rows, n_chains, dil, nh, group, cq, ck, cv):
    s, w = proj.shape
    g = _Band(s, half, q_rows, n_chains, dil)
    has_sink = sink is not None

    def body(*refs):
        q_ref, k_ref, v_ref, b_ref = refs[:4]
        s_ref = refs[4] if has_sink else None
        o_ref, l_ref, ks, vs = refs[4 + has_sink:8 + has_sink]
        qs, os_, ls = refs[8 + has_sink:] if g.staged else (None, o_ref, l_ref)
        g.stage_kv(ks, k_ref)
        g.stage_kv(vs, v_ref)
        if g.staged:
            g.stage(qs, q_ref)
        bias_v = b_ref[...]
        sk = s_ref[pl.program_id(0)] if has_sink else None

        def chain(item):
            qoff, koff, edge = g.offsets(item)
            rows = pl.ds(qoff, g.q_rows)
            qv = qs[rows, :] if g.staged else q_ref[rows, :].astype(BF16)
            kw_ = ks[pl.ds(koff, g.win), :]
            vw_ = vs[pl.ds(koff, g.win), :]
            sc = lax.dot_general(qv, kw_, (((1,), (1,)), ((), ())), preferred_element_type=F32) * ATTN_SCALE
            sc = sc + bias_v + edge
            m = jnp.max(sc, axis=-1, keepdims=True)
            if has_sink:
                m = jnp.maximum(m, sk)
            p = jnp.exp(sc - m)
            den = jnp.sum(p, axis=-1, keepdims=True)
            if has_sink:
                den = den + jnp.exp(sk - m)
            out = lax.dot_general(p.astype(BF16), vw_, (((1,), (0,)), ((), ())), preferred_element_type=F32)
            return rows, out / den, jnp.broadcast_to(m + jnp.log(den), (g.q_rows, HEAD_DIM))

        def step(i, carry):
            for rows, out, lse in [chain(i * n_chains + u) for u in range(n_chains)]:
                os_[rows, :] = out
                ls[rows, :] = lse
            return carry

        lax.fori_loop(0, g.n_items // n_chains, step, 0)
        if g.staged:
            g.unstage(o_ref, os_)
            g.unstage(l_ref, ls)

    def col(c0, per):
        return pl.BlockSpec((s, HEAD_DIM), lambda h: (0, c0 // LANES + h // per))

    in_specs = [col(cq, 1), col(ck, group), col(cv, group),
                pl.BlockSpec((None, g.q_rows, g.win), lambda h: (h, 0, 0))]
    args = [proj, proj, proj, bias]
    if has_sink:
        in_specs.append(pl.BlockSpec(memory_space=pltpu.SMEM))
        args.append(sink)
    shape = pltpu.HBM((s, nh * HEAD_DIM), F32)
    scratch = [pltpu.VMEM((dil * g.pad, HEAD_DIM), BF16), pltpu.VMEM((dil * g.pad, HEAD_DIM), BF16)]
    if g.staged:
        scratch += [pltpu.VMEM((s, HEAD_DIM), BF16), pltpu.VMEM((s, HEAD_DIM), F32), pltpu.VMEM((s, HEAD_DIM), F32)]
    return pl.pallas_call(
        body, name=name, grid=(nh,), in_specs=in_specs, out_specs=[col(0, 1), col(0, 1)], out_shape=[shape, shape],
        scratch_shapes=scratch, compiler_params=_params(("parallel",), 16 * s * HEAD_DIM * 4))(*args)


def band_attn_bwd(name, proj, bias, sink, dout, out, lse, dlse, *, half, q_rows, n_chains, dil, nh, group, cq, ck, cv):
    s, w = proj.shape
    g = _Band(s, half, q_rows, n_chains, dil)
    nkv = nh // group
    has_sink = sink is not None
    has_dl = dlse is not None
    n_in = 7 + int(has_sink) + int(has_dl)
    n_out = 4 + int(has_sink)

    def body(*refs):
        ins, outs, scr = refs[:n_in], refs[n_in:n_in + n_out], refs[n_in + n_out:]
        q_ref, k_ref, v_ref, b_ref, do_ref, o_ref, l_ref = ins[:7]
        s_ref = ins[7] if has_sink else None
        dl_ref = ins[n_in - 1] if has_dl else None
        dq_ref, dk_ref, dv_ref, db_ref = outs[:4]
        ks, vs, dks, dvs = scr[:4]
        scr = list(scr[4:])
        dsa = scr.pop(0) if has_sink else None
        if g.staged:
            qs, dos, os_, ls, dqs = scr[:5]
            dls = scr[5] if has_dl else None
            g.stage(qs, q_ref)
            g.stage(dos, do_ref)
            g.stage(os_, o_ref)
            g.stage(ls, l_ref)
            if has_dl:
                g.stage(dls, dl_ref)
        else:
            qs, dos, os_, ls, dqs, dls = None, do_ref, o_ref, l_ref, dq_ref, dl_ref
        h = pl.program_id(0)
        g.stage_kv(ks, k_ref)
        g.stage_kv(vs, v_ref)
        dks[...] = jnp.zeros_like(dks)
        dvs[...] = jnp.zeros_like(dvs)
        db_ref[...] = jnp.zeros_like(db_ref)
        bias_v = b_ref[...]
        if has_sink:
            sk = s_ref[h]
            dsa[...] = jnp.zeros_like(dsa)

        def chain(item):
            qoff, koff, edge = g.offsets(item)
            rows = pl.ds(qoff, g.q_rows)
            win = pl.ds(koff, g.win)
            qv = qs[rows, :] if g.staged else q_ref[rows, :].astype(BF16)
            kw_ = ks[win, :]
            vw_ = vs[win, :]
            sc = lax.dot_general(qv, kw_, (((1,), (1,)), ((), ())), preferred_element_type=F32) * ATTN_SCALE
            lv = ls[rows, :][:, 0:1]
            p = jnp.exp(sc + bias_v + edge - lv)
            dov = dos[rows, :]
            delta = jnp.sum(dov * os_[rows, :], axis=-1, keepdims=True)
            dob = dov.astype(BF16)
            dp = lax.dot_general(dob, vw_, (((1,), (1,)), ((), ())), preferred_element_type=F32)
            t = dp - delta
            if has_dl:
                t = t + dls[rows, :][:, 0:1]
            ds = p * t
            dsb = (ds * ATTN_SCALE).astype(BF16)
            dq = lax.dot_general(dsb, kw_, (((1,), (0,)), ((), ())), preferred_element_type=F32)
            dkc = lax.dot_general(dsb, qv, (((0,), (0,)), ((), ())), preferred_element_type=F32)
            dvc = lax.dot_general(p.astype(BF16), dob, (((0,), (0,)), ((), ())), preferred_element_type=F32)
            dsk = jnp.exp(sk - lv) * delta if has_sink else None
            return rows, win, dq, dkc, dvc, ds, dsk

        def step(i, carry):
            res = [chain(i * n_chains + u) for u in range(n_chains)]
            ds_sum = res[0][5]
            for rr in res[1:]:
                ds_sum = ds_sum + rr[5]
            db_ref[...] += ds_sum
            for rows, win, dq, dkc, dvc, ds, dsk in res:
                dqs[rows, :] = dq
                dks[win, :] += dkc
                dvs[win, :] += dvc
                if has_sink:
                    dsa[...] += dsk
            return carry

        lax.fori_loop(0, g.n_items // n_chains, step, 0)

        if g.staged:
            g.unstage(dq_ref, dqs)

        def emit_kv(add):
            for r in range(dil):
                lo = r * g.pad + half
                for dst_ref, src in ((dk_ref, dks), (dv_ref, dvs)):
                    val = src[lo:lo + g.seg, :]
                    if add:
                        val = val + dst_ref[g.rows_of(r), :]
                    dst_ref[g.rows_of(r), :] = val

        if group == 1:
            emit_kv(False)
        else:
            @pl.when(h % group == 0)
            def _():
                emit_kv(False)

            @pl.when(h % group != 0)
            def _():
                emit_kv(True)
        if has_sink:
            outs[4][...] = jnp.full((SUBLANES, LANES), -jnp.sum(dsa[...]), F32)

    def col(c0, per):
        return pl.BlockSpec((s, HEAD_DIM), lambda h: (0, c0 // LANES + h // per))

    b_spec = pl.BlockSpec((None, g.q_rows, g.win), lambda h: (h, 0, 0))
    in_specs = [col(cq, 1), col(ck, group), col(cv, group), b_spec, col(0, 1), col(0, 1), col(0, 1)]
    args = [proj, proj, proj, bias, dout, out, lse]
    if has_sink:
        in_specs.append(pl.BlockSpec(memory_space=pltpu.SMEM))
        args.append(sink)
    if has_dl:
        in_specs.append(col(0, 1))
        args.append(dlse)
    out_specs = [col(0, 1), col(0, group), col(0, group), b_spec]
    out_shape = [pltpu.HBM((s, nh * HEAD_DIM), F32), pltpu.HBM((s, nkv * HEAD_DIM), F32),
                 pltpu.HBM((s, nkv * HEAD_DIM), F32), jax.ShapeDtypeStruct((nh, g.q_rows, g.win), F32)]
    scratch = [pltpu.VMEM((dil * g.pad, HEAD_DIM), BF16), pltpu.VMEM((dil * g.pad, HEAD_DIM), BF16),
               pltpu.VMEM((dil * g.pad, HEAD_DIM), F32), pltpu.VMEM((dil * g.pad, HEAD_DIM), F32)]
    if has_sink:
        out_specs.append(pl.BlockSpec((None, SUBLANES, LANES), lambda h: (h, 0, 0)))
        out_shape.append(jax.ShapeDtypeStruct((nh, SUBLANES, LANES), F32))
        scratch.append(pltpu.VMEM((g.q_rows, 1), F32))
    if g.staged:
        scratch += [pltpu.VMEM((s, HEAD_DIM), BF16)] + [pltpu.VMEM((s, HEAD_DIM), F32)] * (4 + int(has_dl))
    res = pl.pallas_call(
        body, name=name, grid=(nh,), in_specs=in_specs, out_specs=out_specs, out_shape=out_shape,
        scratch_shapes=scratch, compiler_params=_params(("arbitrary",), 28 * s * HEAD_DIM * 4))(*args)
    return res[0], res[1], res[2], res[3], (res[4] if has_sink else None)


def dil_merge_fwd(name, outs, lses):
    s, w = outs[0].shape
    ts = _pick(s, ROW_TILE_CANDS)
    ng = len(outs)

    def body(*refs):
        o_refs, l_refs, y_ref = refs[:ng], refs[ng:2 * ng], refs[2 * ng]
        ls = [l[...] for l in l_refs]
        mx = ls[0]
        for l in ls[1:]:
            mx = jnp.maximum(mx, l)
        es = [jnp.exp(l - mx) for l in ls]
        tot = es[0]
        for e in es[1:]:
            tot = tot + e
        acc = (es[0] / tot) * o_refs[0][...]
        for e, o in zip(es[1:], o_refs[1:]):
            acc = acc + (e / tot) * o[...]
        y_ref[...] = acc.astype(y_ref.dtype)

    row = pl.BlockSpec((ts, w), lambda i: (i, 0))
    return pl.pallas_call(
        body, name=name, grid=(s // ts,), in_specs=[row] * (2 * ng), out_specs=row,
        out_shape=pltpu.HBM((s, w), BF16),
        compiler_params=_params(("parallel",), 10 * ts * w * 4))(*outs, *lses)


def dil_merge_bwd(name, dy, outs, lses):
    s, w = outs[0].shape
    ts = _pick(s, ROW_TILE_CANDS)
    ng = len(outs)
    nhead = w // HEAD_DIM

    def body(*refs):
        dy_ref = refs[0]
        o_refs, l_refs = refs[1:1 + ng], refs[1 + ng:1 + 2 * ng]
        do_refs, dl_refs = refs[1 + 2 * ng:1 + 3 * ng], refs[1 + 3 * ng:1 + 4 * ng]
        for hh in range(nhead):
            cols = slice(hh * HEAD_DIM, (hh + 1) * HEAD_DIM)
            dyv = dy_ref[:, cols]
            ls = [l[:, cols] for l in l_refs]
            mx = ls[0]
            for l in ls[1:]:
                mx = jnp.maximum(mx, l)
            es = [jnp.exp(l - mx) for l in ls]
            tot = es[0]
            for e in es[1:]:
                tot = tot + e
            alphas = [e / tot for e in es]
            dal = [jnp.broadcast_to(jnp.sum(dyv * o[:, cols], axis=-1, keepdims=True), dyv.shape) for o in o_refs]
            mean = alphas[0] * dal[0]
            for a, d in zip(alphas[1:], dal[1:]):
                mean = mean + a * d
            for g in range(ng):
                do_refs[g][:, cols] = alphas[g] * dyv
                dl_refs[g][:, cols] = alphas[g] * (dal[g] - mean)

    row = pl.BlockSpec((ts, w), lambda i: (i, 0))
    shape = pltpu.HBM((s, w), F32)
    res = pl.pallas_call(
        body, name=name, grid=(s // ts,), in_specs=[row] * (1 + 2 * ng), out_specs=[row] * (2 * ng),
        out_shape=[shape] * (2 * ng),
        compiler_params=_params(("parallel",), 16 * ts * w * 4))(dy, *outs, *lses)
    return res[:ng], res[ng:]


def _adamw(w, g, m, v):
    m = ADAM_B1 * m + (1.0 - ADAM_B1) * g
    v = ADAM_B2 * v + (1.0 - ADAM_B2) * (g * g)
    m_hat = m / (1.0 - ADAM_B1 ** ADAM_STEP)
    v_hat = v / (1.0 - ADAM_B2 ** ADAM_STEP)
    delta = -ADAM_LR * (m_hat / (jnp.sqrt(v_hat) + ADAM_EPS) + ADAM_WD * w)
    return delta, m, v


def _row_tile(r, c, budget=1 << 20):
    if r * c * 4 <= budget or r % SUBLANES:
        return r
    for t in (1024, 512, 256, 128, 64, 32, 16, 8):
        if r % t == 0 and t * c * 4 <= budget:
            return t
    return SUBLANES


def adam_small(name, g, w, m, v):
    def body(g_ref, w_ref, m_ref, v_ref, d_ref, nm_ref, nv_ref):
        d_ref[...], nm_ref[...], nv_ref[...] = _adamw(w_ref[...], g_ref[...], m_ref[...], v_ref[...])

    shape = jax.ShapeDtypeStruct(w.shape, F32)
    return pl.pallas_call(body, name=name, out_shape=[shape, shape, shape])(g, w, m, v)


def reduce_adam(name, mine, theirs, w, m, v):
    nq, r, c = mine.shape
    tr = _row_tile(r, c)

    def body(*refs):
        parts, (w_ref, m_ref, v_ref, g_ref, d_ref, nm_ref, nv_ref) = refs[:nq], refs[nq:]
        g = parts[0][...].astype(F32)
        for p_ref in parts[1:]:
            g = g + p_ref[...].astype(F32)
        g_ref[...] = g
        d_ref[...], nm_ref[...], nv_ref[...] = _adamw(w_ref[...], g, m_ref[...], v_ref[...])

    def slot(q):
        return pl.BlockSpec((None, tr, c), lambda i: (q, i, 0))

    row = pl.BlockSpec((tr, c), lambda i: (i, 0))
    shape = jax.ShapeDtypeStruct((r, c), F32)
    return pl.pallas_call(
        body, name=name, grid=(r // tr,), in_specs=[slot(q) for q in range(nq)] + [row, row, row],
        out_specs=[row] * 4, out_shape=[shape] * 4,
        compiler_params=_params(("parallel",), (nq * 2 + 7 * 4) * tr * c))(mine, *[theirs] * (nq - 1), *[_in_hbm(t) for t in (w, m, v)])


def _place():
    return lax.axis_index("x"), lax.axis_index("y"), lax.axis_index("c")


def _flip(pos, bits):
    return tuple((1 - p) if b else p for p, b in zip(pos, bits))


def _index(pos):
    return 4 * pos[0] + 2 * pos[1] + pos[2]


ANY = pl.BlockSpec(memory_space=pl.ANY)


HBM = pl.BlockSpec(memory_space=pltpu.HBM)
SEM = pl.BlockSpec(memory_space=pltpu.SEMAPHORE)
EFFECT = pltpu.SideEffectType.DATAFLOW_SIDE_EFFECTING
TO_SIBLING = (0, 0, 1)
TO_CHIPS = [(1, 0, 0), (0, 1, 0), (1, 1, 0)]


def _in_hbm(a):
    return pltpu.with_memory_space_constraint(a, pltpu.HBM)


def _token_value(token):
    return token[0, 0]


def _when(pred, fn):
    if pred is True:
        fn()
    elif pred is not False:
        pl.when(pred)(fn)


def _plan_copy(k, entry, ins, lnd, send_sems, recv_sems):
    a, src_a, sblk, lblk, to, send_if, recv_if = entry
    src = lnd[a] if src_a is None else ins[src_a]
    return pltpu.make_async_remote_copy(
        src_ref=src.at[sblk], dst_ref=lnd[a].at[lblk], send_sem=send_sems.at[k], recv_sem=recv_sems.at[k],
        device_id=to, device_id_type=MESH), send_if, recv_if


def split_start(name, srcs, lands, plan, after):
    ns, nl = len(srcs), len(lands)
    n_copies = len(plan((0, 0, 0)))

    def body(*refs):
        ins, lnd = refs[:ns], refs[ns:ns + nl]
        send_sems, recv_sems = refs[ns + nl + 1], refs[ns + nl + 2]
        token = refs[-1]
        for k, entry in enumerate(plan(_place())):
            cp, send_if, _ = _plan_copy(k, entry, ins, lnd, send_sems, recv_sems)
            _when(send_if, cp.start)
        token[...] = jnp.zeros_like(token)

    outs = pl.pallas_call(
        body, name=name,
        out_shape=(pltpu.SemaphoreType.DMA((n_copies,)), pltpu.SemaphoreType.DMA((n_copies,)),
                   *[pltpu.HBM(a.shape, a.dtype) for a in srcs], *[pltpu.HBM(a.shape, a.dtype) for a in lands],
                   jax.ShapeDtypeStruct((SUBLANES, LANES), F32)),
        in_specs=[HBM] * (ns + nl) + [ANY],
        out_specs=(SEM, SEM, *[HBM] * (ns + nl), pl.BlockSpec(memory_space=pltpu.VMEM)),
        input_output_aliases={i: 2 + i for i in range(ns + nl)},
        compiler_params=pltpu.CompilerParams(has_side_effects=EFFECT),
    )(*[_in_hbm(a) for a in srcs], *[_in_hbm(a) for a in lands], after)
    return outs[0], outs[1], list(outs[2:2 + ns]), list(outs[2 + ns:2 + ns + nl]), outs[-1]


def split_wait(name, send_sems, recv_sems, srcs, lands, plan, after):
    ns, nl = len(srcs), len(lands)

    def body(*refs):
        ins, lnd = refs[:ns], refs[ns:ns + nl]
        s_sems, r_sems = refs[ns + nl], refs[ns + nl + 1]
        for k, entry in enumerate(plan(_place())):
            cp, send_if, recv_if = _plan_copy(k, entry, ins, lnd, s_sems, r_sems)
            _when(send_if, cp.wait_send)
            _when(recv_if, cp.wait_recv)
        refs[-1][...] = jnp.zeros((SUBLANES, LANES), F32)

    outs = pl.pallas_call(
        body, name=name,
        out_shape=(*[pltpu.HBM(a.shape, a.dtype) for a in srcs], *[pltpu.HBM(a.shape, a.dtype) for a in lands],
                   jax.ShapeDtypeStruct((SUBLANES, LANES), F32)),
        in_specs=[HBM] * (ns + nl) + [SEM, SEM, ANY],
        out_specs=(*[HBM] * (ns + nl), pl.BlockSpec(memory_space=pltpu.VMEM)),
        input_output_aliases={i: i for i in range(ns + nl)},
        compiler_params=pltpu.CompilerParams(has_side_effects=EFFECT),
    )(*srcs, *lands, send_sems, recv_sems, after)
    return list(outs[:ns]), list(outs[ns:ns + nl]), outs[-1]


NORTH = 1


def ag_plan(n, rels=TO_CHIPS):
    def plan(me):
        x, y, c = me
        entries = []
        for a in range(n):
            for t in (NORTH, 1 - NORTH):
                blk = _index((x, y, t))
                for rel in rels:
                    entries.append((a, None, blk, blk, _flip((x, y, t), rel), c == NORTH, c == t))
        return entries
    return plan


TO_X, TO_Y = TO_CHIPS[0], TO_CHIPS[1]


def relay_plan(n):
    def plan(me):
        x, y, c = me
        entries = []
        for a in range(n):
            for t, came, goes in ((NORTH, TO_X, TO_Y), (1 - NORTH, TO_Y, TO_X)):
                blk = _index(_flip((x, y, t), came))
                entries.append((a, None, blk, blk, _flip((x, y, t), goes), c == t, c == t))
        return entries
    return plan


def ag_pair(name, lands, after):
    n = len(lands)

    def body(*refs):
        lnd = refs[n + 1:2 * n + 1]
        token = refs[2 * n + 1]
        send_sems, recv_sems = refs[2 * n + 2:]
        token[...] = jnp.zeros_like(token)
        me = _place()
        sibling = _flip(me, TO_SIBLING)
        copies = []
        for a in range(n):
            mine, theirs = lnd[a].at[_index(me)], lnd[a].at[_index(sibling)]
            cp = pltpu.make_async_remote_copy(src_ref=mine, dst_ref=mine, send_sem=send_sems.at[a],
                                              recv_sem=recv_sems.at[a], device_id=sibling, device_id_type=MESH)
            cp.start()
            copies.append((cp, pltpu.make_async_remote_copy(
                src_ref=mine, dst_ref=theirs, send_sem=send_sems.at[a], recv_sem=recv_sems.at[a], device_id=sibling,
                device_id_type=MESH)))
        for cp, arrival in copies:
            arrival.wait_recv()
        for cp, arrival in copies:
            cp.wait_send()

    outs = pl.pallas_call(
        body, name=name, in_specs=[ANY] * (n + 1), out_specs=[ANY] * n + [pl.BlockSpec(memory_space=pltpu.VMEM)],
        out_shape=[jax.ShapeDtypeStruct(l.shape, l.dtype) for l in lands]
        + [jax.ShapeDtypeStruct((SUBLANES, LANES), F32)],
        input_output_aliases={a: a for a in range(n)},
        scratch_shapes=[pltpu.SemaphoreType.DMA((n,)), pltpu.SemaphoreType.DMA((n,))],
    )(*lands, after)
    return list(outs[:n]), outs[n]


def pass_plan(n):
    def plan(me):
        sibling = _flip(me, TO_SIBLING)
        return [(a, None, _index(_flip(me, rel)), _index(_flip(me, rel)), sibling, True, True)
                for a in range(n) for rel in TO_CHIPS]
    return plan


def ag_finish(name, lands):
    n = len(lands)

    def body(*refs):
        lnd = refs[n:2 * n]
        send_sems, recv_sems = refs[2 * n:]
        me = _place()
        sibling = _flip(me, TO_SIBLING)
        copies = []
        for a in range(n):
            for j, rel in enumerate(TO_CHIPS):
                blk = lnd[a].at[_index(_flip(me, rel))]
                there = lnd[a].at[_index(_flip(sibling, rel))]
                cp = pltpu.make_async_remote_copy(
                    src_ref=blk, dst_ref=blk, send_sem=send_sems.at[a * 3 + j], recv_sem=recv_sems.at[a * 3 + j],
                    device_id=sibling, device_id_type=MESH)
                cp.start()
                copies.append((cp, pltpu.make_async_remote_copy(
                    src_ref=blk, dst_ref=there, send_sem=send_sems.at[a * 3 + j], recv_sem=recv_sems.at[a * 3 + j],
                    device_id=sibling, device_id_type=MESH)))
        for cp, arrival in copies:
            arrival.wait_recv()
        for cp, arrival in copies:
            cp.wait_send()

    return pl.pallas_call(
        body, name=name, in_specs=[ANY] * n, out_specs=[ANY] * n,
        out_shape=[jax.ShapeDtypeStruct(l.shape, l.dtype) for l in lands],
        input_output_aliases={a: a for a in range(n)},
        scratch_shapes=[pltpu.SemaphoreType.DMA((3 * n,)), pltpu.SemaphoreType.DMA((3 * n,))],
    )(*lands)


REL = [(b >> 2 & 1, b >> 1 & 1, b & 1) for b in range(N_DEV)]


CHIP_REL = [(0, 0, 0)] + TO_CHIPS
N_CHIPS = len(CHIP_REL)


def rs_pair(name, parts):
    n = len(parts)

    def body(*refs):
        ins, got = refs[:n], refs[n:2 * n]
        send_sems, recv_sems = refs[2 * n:]
        me = _place()
        sibling = _flip(me, TO_SIBLING)
        remote = []
        for a in range(n):
            for q, rel in enumerate(CHIP_REL):
                k = a * N_CHIPS + q
                cp = pltpu.make_async_remote_copy(
                    src_ref=ins[a].at[_index(_flip(sibling, rel))], dst_ref=got[a].at[q], send_sem=send_sems.at[k],
                    recv_sem=recv_sems.at[k], device_id=sibling, device_id_type=MESH)
                cp.start()
                remote.append(cp)
        for cp in remote:
            cp.wait_recv()
        for cp in remote:
            cp.wait_send()

    shapes = [jax.ShapeDtypeStruct((N_CHIPS,) + tuple(p.shape[1:]), p.dtype) for p in parts]
    res = pl.pallas_call(
        body, name=name, in_specs=[ANY] * n, out_specs=[ANY] * n, out_shape=shapes,
        scratch_shapes=[pltpu.SemaphoreType.DMA((N_CHIPS * n,)), pltpu.SemaphoreType.DMA((N_CHIPS * n,))],
    )(*parts)
    return list(res)


def own_blocks():
    me = _place()
    return jnp.stack([_index(_flip(me, rel)) for rel in CHIP_REL]).astype(jnp.int32)


def pair_add(name, blocks, parts, got):
    nq, r, c = got.shape
    tr = _row_tile(r, c, budget=6 << 20)

    def body(blk_ref, a_ref, b_ref, o_ref):
        o_ref[...] = (a_ref[...].astype(F32) + b_ref[...].astype(F32)).astype(o_ref.dtype)

    spec = pl.BlockSpec((None, tr, c), lambda q, i, blk: (q, i, 0))
    return pl.pallas_call(
        body, name=name,
        grid_spec=pltpu.PrefetchScalarGridSpec(
            num_scalar_prefetch=1, grid=(nq, r // tr),
            in_specs=[pl.BlockSpec((None, tr, c), lambda q, i, blk: (blk[q], i, 0)), spec], out_specs=spec),
        out_shape=pltpu.HBM(got.shape, got.dtype),
        compiler_params=_params(("arbitrary", "arbitrary"), 6 * tr * c * 2))(blocks, parts, got)


def rs_pair_plan(n):
    def plan(me):
        sibling = _flip(me, TO_SIBLING)
        return [(a, a, _index(_flip(sibling, rel)), q, sibling, True, True)
                for a in range(n) for q, rel in enumerate(CHIP_REL)]
    return plan


def rs_plan(n):
    def plan(me):
        return [(a, a, q, q, _flip(me, CHIP_REL[q]), True, True) for a in range(n) for q in range(1, N_CHIPS)]
    return plan


def rs_start(name, sums, after):
    lands = [lax.empty(t.shape, t.dtype) for t in sums]
    return split_start(name, sums, lands, rs_plan(len(sums)), after)


def allreduce_small(name, pack, after):
    rows, lanes = pack.shape

    def body(x_ref, after_ref, o_ref, land, send_sems, recv_sems):
        me = _place()
        idx = _index(me)
        land[idx] = x_ref[...]
        copies = []
        for r in range(1, N_DEV):
            peer = _flip(me, REL[r])
            cp = pltpu.make_async_remote_copy(
                src_ref=x_ref, dst_ref=land.at[idx], send_sem=send_sems.at[r - 1], recv_sem=recv_sems.at[r - 1],
                device_id=peer, device_id_type=MESH)
            cp.start()
            copies.append(cp)
        for cp in copies:
            cp.wait_recv()
        for cp in copies:
            cp.wait_send()
        acc = land[0]
        for i in range(1, N_DEV):
            acc = acc + land[i]
        o_ref[...] = acc

    return pl.pallas_call(
        body, name=name, in_specs=[pl.BlockSpec(memory_space=pltpu.VMEM), ANY],
        out_specs=pl.BlockSpec(memory_space=pltpu.VMEM), out_shape=jax.ShapeDtypeStruct((rows, lanes), F32),
        scratch_shapes=[pltpu.VMEM((N_DEV, rows, lanes), F32), pltpu.SemaphoreType.DMA((7,)),
                        pltpu.SemaphoreType.DMA((7,))],
    )(pack, after)


def _pad_rows(a, rows):
    return jnp.pad(a, ((0, rows - a.shape[0]), (0, 0)))


def _as_tiles(vec):
    n = vec.shape[0]
    rows = -(-n // LANES)
    rows = -(-rows // SUBLANES) * SUBLANES
    return jnp.pad(vec, (0, rows * LANES - n)).reshape(rows, LANES)


def kernel(x, p, rel_bias_table, attn_norm, w_in, sink_a, w_branch_a, w_branch_b, w_out, ffn_norm, w_ffn_gate, w_ffn_up, conv_w, conv_b, w_ffn_down, ple_norm, w_ple_gate, w_ple_proj, final_norm, loss_target, m_rel_bias_table, m_attn_norm, m_w_in, m_sink_a, m_w_branch_a, m_w_branch_b, m_w_out, m_ffn_norm, m_w_ffn_gate, m_w_ffn_up, m_conv_w, m_conv_b, m_w_ffn_down, m_ple_norm, m_w_ple_gate, m_w_ple_proj, m_final_norm, v_rel_bias_table, v_attn_norm, v_w_in, v_sink_a, v_w_branch_a, v_w_branch_b, v_w_out, v_ffn_norm, v_w_ffn_gate, v_w_ffn_up, v_conv_w, v_conv_b, v_w_ffn_down, v_ple_norm, v_w_ple_gate, v_w_ple_proj, v_final_norm):
    xs = x[0]
    s, d = xs.shape
    ps = p[0, 0]
    target = loss_target[0]
    me = 4 * lax.axis_index("x") + 2 * lax.axis_index("y") + lax.axis_index("c")

    big = dict(w_in=w_in[0], w_branch_a=w_branch_a[0], w_branch_b=w_branch_b[0], w_out=w_out[0],
               w_ffn_gate=w_ffn_gate[0], w_ffn_up=w_ffn_up[0], w_ffn_down=w_ffn_down[0],
               w_ple_gate=w_ple_gate[0], w_ple_proj=w_ple_proj[0])
    big_m = dict(w_in=m_w_in[0], w_branch_a=m_w_branch_a[0], w_branch_b=m_w_branch_b[0], w_out=m_w_out[0],
                 w_ffn_gate=m_w_ffn_gate[0], w_ffn_up=m_w_ffn_up[0], w_ffn_down=m_w_ffn_down[0],
                 w_ple_gate=m_w_ple_gate[0], w_ple_proj=m_w_ple_proj[0])
    big_v = dict(w_in=v_w_in[0], w_branch_a=v_w_branch_a[0], w_branch_b=v_w_branch_b[0], w_out=v_w_out[0],
                 w_ffn_gate=v_w_ffn_gate[0], w_ffn_up=v_w_ffn_up[0], w_ffn_down=v_w_ffn_down[0],
                 w_ple_gate=v_w_ple_gate[0], w_ple_proj=v_w_ple_proj[0])
    names = list(big)
    nf = big["w_ffn_gate"].shape[1]

    shards = {k: big[k].astype(BF16) for k in names}
    shards["conv_w"] = _pad_rows(conv_w[0], SUBLANES)
    flipped = ("w_ffn_gate", "w_ffn_up")
    for k in flipped:
        big[k], big_m[k], big_v[k] = big[k].T, big_m[k].T, big_v[k].T
    ag_groups = [["w_in"], ["w_branch_a", "w_branch_b", "w_out"], ["w_ffn_gate", "conv_w"], ["w_ffn_up"],
                 ["w_ffn_down"], ["w_ple_gate", "w_ple_proj"]]
    ag_started = {}
    wg = {}

    ag_paired, ag_passing = {}, {}

    def pair(gi, after):
        lands = [lax.dynamic_update_index_in_dim(lax.empty((N_DEV,) + shards[k].shape, shards[k].dtype), shards[k],
                                                 me, 0) for k in ag_groups[gi]]
        ag_paired[gi], token = ag_pair(f"ag_pair{gi}", lands, after)
        return token

    def copies(gi):
        return ag_plan(len(ag_groups[gi]), [TO_X, TO_Y] if gi == 0 else TO_CHIPS)

    def start(gi, after):
        s_sems, r_sems, _, lands, token = split_start(f"ag_start{gi}", [], ag_paired[gi], copies(gi), after)
        ag_started[gi] = (s_sems, r_sems, lands)
        return token

    def landed(gi, after):
        s_sems, r_sems, lands = ag_started[gi]
        return split_wait(f"ag_wait{gi}", s_sems, r_sems, [], lands, copies(gi), after)[1:]

    def relayed(lands, after, meanwhile):
        plan = relay_plan(len(lands))
        s_sems, r_sems, _, lands, token = split_start("ag_relay0", [], lands, plan, after)
        return split_wait("ag_relayed0", s_sems, r_sems, [], lands, plan, meanwhile + _token_value(token))[1:]

    def pass_on(gi, lands, after):
        s_sems, r_sems, _, lands, token = split_start(f"ag_pass{gi}", [], lands, pass_plan(len(lands)), after)
        ag_passing[gi] = (s_sems, r_sems, lands)
        return token

    def ready(gi, after):
        s_sems, r_sems, lands = ag_passing[gi]
        lands = split_wait(f"ag_ready{gi}", s_sems, r_sems, [], lands, pass_plan(len(lands)), after)[1]
        wg.update(zip(ag_groups[gi], lands))

    cb = conv_b.reshape(N_DEV, 1, nf)

    table_t = rel_bias_table.T
    geo_a = dict(half=A_BLOCK, q_rows=ATTN_Q_ROWS, n_chains=ATTN_CHAINS, dil=1, nh=A_Q_HEADS, group=A_GROUP,
                 cq=COL_QA, ck=COL_KA, cv=COL_VA)
    geo_b = [dict(half=B_BLOCK, q_rows=min(ATTN_Q_ROWS, s // dil), n_chains=ATTN_CHAINS, dil=dil,
                  nh=B_HEADS_PER_GROUP, group=1, cq=COL_QB + g * B_OUT_W, ck=COL_KB + g * B_OUT_W,
                  cv=COL_VB + g * B_OUT_W) for g, (_, dil) in enumerate(B_PATTERNS)]
    bucket_a = bucket_tile(geo_a["q_rows"], A_BLOCK, 1)
    bias_a = bias_build("bias_a", table_t, bucket_a, 0, A_Q_HEADS, A_BLOCK)
    buckets_b = [bucket_tile(gb["q_rows"], B_BLOCK, gb["dil"]) for gb in geo_b]
    biases_b = [bias_build(f"bias_b{g}", table_t, buckets_b[g], A_Q_HEADS + g * B_HEADS_PER_GROUP, B_HEADS_PER_GROUP,
                           B_BLOCK) for g in range(len(B_PATTERNS))]

    token = start(0, pair(0, xs))
    h = rms_fwd("rms_attn", xs, attn_norm + _token_value(token))
    lands0, token = landed(0, pair(5, pair(4, pair(3, pair(2, pair(1, h))))))
    bias_corner = bias_a[0, :1, :1] + sum(b[0, :1, :1] for b in biases_b)
    lands0, token = relayed(lands0, token, bias_corner)
    token = start(5, start(4, start(3, start(2, start(1, token)))))
    wg["w_in"] = ag_finish("ag_finish0", lands0)[0]
    proj = mm_cols("proj_in", h, wg["w_in"], F32, fold=True, after=token)
    token = pass_on(1, landed(1, proj)[0], proj)
    sink = sink_a[0] + _token_value(token)
    ya, lse_a = band_attn_fwd("attn_a_fwd", proj, bias_a, sink, **geo_a)
    outs_b, lses_b = [], []
    for g in range(len(B_PATTERNS)):
        o, l = band_attn_fwd(f"attn_b{g}_fwd", proj, biases_b[g], None, **geo_b[g])
        outs_b.append(o)
        lses_b.append(l)
    yb = dil_merge_fwd("dil_merge_fwd", outs_b, lses_b)
    ready(1, yb)
    token = pass_on(2, landed(2, yb)[0], yb)
    w_out_full = wg["w_out"].reshape(d, d)
    ta = mm_cols("branch_a", ya, wg["w_branch_a"], F32, fold=True, after=token)
    tb = mm_cols("branch_b", yb, wg["w_branch_b"], F32, fold=True)
    merged = gate_merge_fwd("gate_merge_fwd", proj, ta, tb, d)
    x1 = mm_plain("mix_out", merged, w_out_full, F32, res=xs)

    hf = rms_fwd("rms_ffn", x1, ffn_norm)
    ready(2, hf)
    token = pass_on(3, landed(3, hf)[0], hf)
    cw = wg["conv_w"]
    gpre = mm_cols("ffn_gate", hf, wg["w_ffn_gate"], F32, fold=False, after=token)
    ready(3, gpre)
    token = pass_on(4, landed(4, gpre)[0], gpre)
    u = mm_cols("ffn_up", hf, wg["w_ffn_up"], F32, fold=False, after=token)
    z = ffn_mid_fwd("ffn_mid_fwd", gpre, u, cw, cb)
    ready(4, z)
    token = pass_on(5, landed(5, z)[0], z)
    x2 = mm_jsum("ffn_down", z, wg["w_ffn_down"], F32, res=x1, after=token)

    hp = rms_fwd("rms_ple", x2, ple_norm)
    ready(5, hp)
    w_pg_full = wg["w_ple_gate"].reshape(d, d)
    lp = mm_plain("ple_gate", hp, w_pg_full, F32)
    pp = mm_cols("ple_proj", ps, wg["w_ple_proj"], F32, fold=True)
    loss_part, dx3, dlp, dpp, d_final = tail_fwd_bwd("tail", x2, lp, pp, final_norm.reshape(1, d), target)

    grads = {}
    rs_started = []
    blocks = own_blocks()

    exchanging = []

    def exchange(tag, keys):
        parts = [grads[k] for k in keys]
        lands = [lax.empty((N_CHIPS,) + tuple(p.shape[1:]), p.dtype) for p in parts]
        s_sems, r_sems, parts, lands, token = split_start(f"rs_pair_{tag}", parts, lands, rs_pair_plan(len(keys)), blocks)
        exchanging.append((tag, keys, s_sems, r_sems, parts, lands))
        return _token_value(token)

    def send(after):
        tag, keys, s_sems, r_sems, parts, lands = exchanging.pop(0)
        parts, got, _ = split_wait(f"rs_paired_{tag}", s_sems, r_sems, parts, lands, rs_pair_plan(len(keys)), after)
        return send_sums(tag, keys, parts, got)

    def send_sums(tag, keys, parts, got):
        sums = [pair_add(f"pair_add_{k}", blocks, p, g) for k, p, g in zip(keys, parts, got)]
        s_sems, r_sems, srcs, lands, token = rs_start(f"rs_start_{tag}", sums, blocks)
        rs_started.append((tag, keys, s_sems, r_sems, srcs, lands))
        return token

    grads["w_ple_proj"] = mm_tn_cols("d_w_ple_proj", ps, dpp, N_DEV, big["w_ple_proj"].shape[1], BF16, folded=True)
    grads["w_ple_gate"] = mm_tn_plain("d_w_ple_gate", hp, dlp, BF16).reshape(N_DEV, d // N_DEV, d)
    tok = exchange("ple", ["w_ple_proj", "w_ple_gate"])
    dhp = mm_nt_plain("d_hp", dlp, w_pg_full, F32)
    dx2, dx2_b, d_ple = rms_bwd("rms_ple_bwd", x2, ple_norm + tok, dhp, dx3, True)

    dz = mm_nt_j("d_z", dx2_b, wg["w_ffn_down"], BF16)
    grads["w_ffn_down"] = mm_tn_j("d_w_ffn_down", z, dx2_b, BF16)
    tok = _token_value(send(dz)) + exchange("down", ["w_ffn_down"])
    du, dgpre, dcw = ffn_mid_bwd("ffn_mid_bwd", gpre, u, dz, cw, cb + tok)
    grads["w_ffn_up"] = mm_tn_j("d_w_ffn_up", du, hf, BF16)
    grads["w_ffn_gate"] = mm_tn_j("d_w_ffn_gate", dgpre, hf, BF16)
    dhf = mm_nt_jsum("d_hf_up", du, wg["w_ffn_up"], F32, folded=False)
    dhf = mm_nt_jsum("d_hf_gate", dgpre, wg["w_ffn_gate"], F32, folded=False, res=dhf)
    tok = _token_value(send(dhf)) + exchange("upgate", ["w_ffn_up", "w_ffn_gate"])
    dx1, dx1_b, d_ffn = rms_bwd("rms_ffn_bwd", x1, ffn_norm + tok, dhf, dx2, True)

    dmerged = mm_nt_plain("d_merged", dx1_b, w_out_full, F32)
    grads["w_out"] = mm_tn_plain("d_w_out", merged, dx1_b, BF16).reshape(N_DEV, d // N_DEV, d)
    dta, dtb, dga, dgb = gate_merge_bwd("gate_merge_bwd", dmerged, proj, ta, tb, d)
    grads["w_branch_a"] = mm_tn_cols("d_w_branch_a", ya, dta, N_DEV, big["w_branch_a"].shape[1], BF16, folded=True)
    grads["w_branch_b"] = mm_tn_cols("d_w_branch_b", yb, dtb, N_DEV, big["w_branch_b"].shape[1], BF16, folded=True)
    dya = mm_nt_jsum("d_ya", dta, wg["w_branch_a"], F32, folded=True)
    dyb = mm_nt_jsum("d_yb", dtb, wg["w_branch_b"], F32, folded=True)
    tok = _token_value(send(dyb)) + exchange("mix", ["w_out", "w_branch_a", "w_branch_b"])
    dqa, dka, dva, dbias_a, dsink = band_attn_bwd("attn_a_bwd", proj, bias_a, sink + tok, dya, ya, lse_a, None, **geo_a)
    douts_b, dlses_b = dil_merge_bwd("dil_merge_bwd", dyb, outs_b, lses_b)
    dq_b, dk_b, dv_b, dbias_b = [], [], [], []
    for g in range(len(B_PATTERNS)):
        dq, dk, dv, db, _ = band_attn_bwd(f"attn_b{g}_bwd", proj, biases_b[g], None, douts_b[g], outs_b[g], lses_b[g],
                                          dlses_b[g], **geo_b[g])
        dq_b.append(dq)
        dk_b.append(dk)
        dv_b.append(dv)
        dbias_b.append(db)
    dproj = jnp.concatenate([t.astype(BF16) for t in [dqa, dka, dva] + dq_b + dk_b + dv_b + [dga, dgb]], axis=1)
    token = send(dproj)
    grads["w_in"] = mm_tn_cols("d_w_in", h, dproj, N_DEV, big["w_in"].shape[1], BF16, folded=True, after=token)
    token = send_sums("in", ["w_in"], [grads["w_in"]], rs_pair("rs_pair_in", [grads["w_in"]]))
    dh = mm_nt_jsum("d_h", dproj, wg["w_in"], F32, folded=True, after=token)
    grad_x, _, d_attn = rms_bwd("rms_attn_bwd", xs, attn_norm, dh, dx1, False)

    dt_a = table_grad("table_grad_a", dbias_a, bucket_a)[:, 0, :N_BUCKETS]
    dt_b = [table_grad(f"table_grad_b{g}", dbias_b[g], buckets_b[g])[:, 0, :N_BUCKETS] for g in range(len(B_PATTERNS))]
    d_table_part = jnp.concatenate([dt_a] + dt_b, axis=0).T

    pieces = [
        ("loss", loss_part[0, :1]),
        ("table", d_table_part.reshape(-1)),
        ("attn_norm", d_attn.reshape(-1)),
        ("sink", dsink[:, 0, 0]),
        ("ffn_norm", d_ffn.reshape(-1)),
        ("conv_w", dcw[:, 0:3, :].reshape(-1)),
        ("conv_b", dcw[:, 3, :].reshape(-1)),
        ("ple_norm", d_ple.reshape(-1)),
        ("final_norm", d_final.reshape(-1)),
    ]
    tiles = [_as_tiles(v) for _, v in pieces]
    pack = jnp.concatenate(tiles, axis=0)

    out_g, out_d, out_m, out_v = {}, {}, {}, {}

    def finish(group, after):
        tag, keys, s_sems, r_sems, srcs, lands = group
        srcs, lands, _ = split_wait(f"rs_wait_{tag}", s_sems, r_sems, srcs, lands, rs_plan(len(keys)), after)
        for k, mine, theirs in zip(keys, srcs, lands):
            res = reduce_adam("adam_" + k, mine, theirs, big[k], big_m[k], big_v[k])
            after = res[1]
            out_g[k], out_d[k], out_m[k], out_v[k] = [(t.T if k in flipped else t)[None] for t in res]
        return after

    after = pack
    for group in rs_started[:-1]:
        after = finish(group, after)
    total = allreduce_small("allreduce_small", pack, after)
    finish(rs_started[-1], total)
    small = {}
    row = 0
    for (nm, v), t in zip(pieces, tiles):
        small[nm] = total[row:row + t.shape[0]].reshape(-1)[:v.shape[0]]
        row += t.shape[0]
    loss = small["loss"][0]
    g_small = dict(
        rel_bias_table=small["table"].reshape(rel_bias_table.shape),
        attn_norm=small["attn_norm"].reshape(attn_norm.shape),
        sink_a=small["sink"].reshape(sink_a.shape),
        ffn_norm=small["ffn_norm"].reshape(ffn_norm.shape),
        conv_w=lax.dynamic_index_in_dim(small["conv_w"].reshape(N_DEV, 3, nf), me, 0, keepdims=False)[None],
        conv_b=small["conv_b"].reshape(conv_b.shape),
        ple_norm=small["ple_norm"].reshape(ple_norm.shape),
        final_norm=small["final_norm"].reshape(1, d),
    )
    w_small = dict(rel_bias_table=(rel_bias_table, m_rel_bias_table, v_rel_bias_table),
                   attn_norm=(attn_norm, m_attn_norm, v_attn_norm), sink_a=(sink_a, m_sink_a, v_sink_a),
                   ffn_norm=(ffn_norm, m_ffn_norm, v_ffn_norm), conv_w=(conv_w, m_conv_w, v_conv_w),
                   conv_b=(conv_b, m_conv_b, v_conv_b), ple_norm=(ple_norm, m_ple_norm, v_ple_norm),
                   final_norm=(final_norm, m_final_norm, v_final_norm))

    for k, (wv, mv, vv) in w_small.items():
        shape = wv.shape
        two_d = (1, shape[0]) if len(shape) == 1 else ((shape[0] * shape[1], shape[2]) if len(shape) == 3 else shape)
        gk = g_small[k].reshape(two_d)
        dl, nm, nv = adam_small("adam_" + k, gk, wv.reshape(two_d), mv.reshape(two_d), vv.reshape(two_d))
        out_g[k], out_d[k], out_m[k], out_v[k] = gk.reshape(shape), dl.reshape(shape), nm.reshape(shape), nv.reshape(shape)

    order = ["rel_bias_table", "attn_norm", "w_in", "sink_a", "w_branch_a", "w_branch_b", "w_out", "ffn_norm",
             "w_ffn_gate", "w_ffn_up", "conv_w", "conv_b", "w_ffn_down", "ple_norm", "w_ple_gate", "w_ple_proj",
             "final_norm"]
    return (loss, grad_x[None], *[out_g[k] for k in order], *[out_d[k] for k in order],
            *[out_m[k] for k in order], *[out_v[k] for k in order])
```

```python
import math

import jax
import jax.numpy as jnp
from jax import lax
from jax.experimental import pallas as pl
from jax.experimental.pallas import tpu as pltpu

F32 = jnp.float32
BF16 = jnp.bfloat16
MESH = pl.DeviceIdType.MESH
N_DEV = 8

HEAD_DIM = 128
A_Q_HEADS = 8
A_KV_HEADS = 2
A_GROUP = A_Q_HEADS // A_KV_HEADS
A_BLOCK = 128
B_PATTERNS = ((128, 1), (512, 4), (2048, 16))
B_HEADS_PER_GROUP = 4
B_HEADS = len(B_PATTERNS) * B_HEADS_PER_GROUP
B_BLOCK = 64
N_BUCKETS = 32
MAX_DISTANCE = 1024
A_Q_W = A_Q_HEADS * HEAD_DIM
A_KV_W = A_KV_HEADS * HEAD_DIM
B_W = B_HEADS * HEAD_DIM
B_OUT_W = B_HEADS_PER_GROUP * HEAD_DIM
COL_QA = 0
COL_KA = COL_QA + A_Q_W
COL_VA = COL_KA + A_KV_W
COL_QB = COL_VA + A_KV_W
COL_KB = COL_QB + B_W
COL_VB = COL_KB + B_W
COL_GATES = COL_VB + B_W
RMS_EPS = 1e-6
NEG_INF = -1e30
ATTN_SCALE = HEAD_DIM ** -0.5
ATTN_Q_ROWS = 256
ATTN_CHAINS = 4

ADAM_LR = 0.001
ADAM_B1 = 0.9
ADAM_B2 = 0.999
ADAM_EPS = 1e-08
ADAM_WD = 0.01
ADAM_STEP = 10

GELU_C = math.sqrt(2.0 / math.pi)
GELU_A = 0.044715

V7X_VMEM_BYTES = 64 * 1024 * 1024
VMEM_CEILING = V7X_VMEM_BYTES - 8 * 1024 * 1024
LANES = 128
SUBLANES = 8


def _pick(n, cands):
    for c in cands:
        if n % c == 0:
            return c
    return n


def _nbytes(shape, dtype):
    n = 1
    for d in shape:
        if d is not None:
            n *= d
    return n * jnp.dtype(dtype).itemsize


def _params(sem, est_bytes):
    limit = int(min(VMEM_CEILING, max(32 * 1024 * 1024, 2 * est_bytes + (8 << 20))))
    return pltpu.CompilerParams(dimension_semantics=sem, vmem_limit_bytes=limit)


def _mm(name, a, b, a_bs, a_im, b_bs, b_im, out_shape, out_dtype, o_bs, o_im, grid, dims,
        res=None, r_bs=None, r_im=None, after=None):
    nk = grid[-1]
    nax = len(grid)
    has_res = res is not None
    has_after = after is not None
    o_tile = tuple(d for d in o_bs if d is not None)

    def body(*refs):
        a_ref, b_ref = refs[:2]
        r_ref = refs[2] if has_res else None
        n_in = 2 + has_res + has_after
        o_ref = refs[n_in]
        rest = refs[n_in + 1:]

        def prod():
            return lax.dot_general(a_ref[...].astype(BF16), b_ref[...].astype(BF16), (dims, ((), ())),
                                   preferred_element_type=F32)

        def finish(r):
            if r_ref is not None:
                r = r + r_ref[...].astype(F32)
            o_ref[...] = r.astype(o_ref.dtype)

        if nk == 1:
            finish(prod())
        else:
            acc = rest[0]
            k = pl.program_id(nax - 1)

            @pl.when(k == 0)
            def _():
                acc[...] = prod()

            @pl.when(k > 0)
            def _():
                acc[...] += prod()

            @pl.when(k == nk - 1)
            def _():
                finish(acc[...])

    in_specs = [pl.BlockSpec(a_bs, a_im), pl.BlockSpec(b_bs, b_im)]
    args = [a, b]
    est = _nbytes(a_bs, a.dtype) + _nbytes(b_bs, b.dtype) + _nbytes(o_bs, out_dtype) + 2 * _nbytes(o_tile, F32)
    if has_res:
        in_specs.append(pl.BlockSpec(r_bs, r_im))
        args.append(res)
        est += _nbytes(r_bs, res.dtype)
    if has_after:
        in_specs.append(pl.BlockSpec(memory_space=pl.ANY))
        args.append(after)
    scratch = [] if nk == 1 else [pltpu.VMEM(o_tile, F32)]
    sem = ("parallel",) * (nax - 1) + ("arbitrary",)
    return pl.pallas_call(
        body, name=name, grid=grid, in_specs=in_specs, out_specs=pl.BlockSpec(o_bs, o_im),
        out_shape=pltpu.HBM(out_shape, out_dtype), scratch_shapes=scratch,
        compiler_params=_params(sem, est))(*args)


TM_CANDS = (1024, 512, 256, 128, 64, 32, 16, 8)
TM_WIDE_CANDS = (2048,) + TM_CANDS
MM_WHOLE_K_BYTES = 42 * 1024 * 1024


def _whole(k, tile_bytes):
    return k if 2 * tile_bytes(k) <= MM_WHOLE_K_BYTES else _pick(k, TK_CANDS)
TK_CANDS = (1024, 512, 256, 128)
TN_CANDS = (1024, 512, 256, 128)


def mm_cols(name, a, wg, out_dtype, fold, after=None):
    m, k = a.shape
    nj, _, n = wg.shape
    tm = _pick(m, TM_WIDE_CANDS)
    tk = _whole(k, lambda t: _nbytes((tm, t), a.dtype) + _nbytes((t, n), wg.dtype) + _nbytes((tm, n), out_dtype))
    grid = (nj, m // tm, k // tk)
    if fold:
        shape, o_bs, o_im = (m, nj * n), (tm, n), (lambda j, i, kk: (i, j))
    else:
        shape, o_bs, o_im = (nj, m, n), (None, tm, n), (lambda j, i, kk: (j, i, 0))
    return _mm(name, a, wg, (tm, tk), lambda j, i, kk: (i, kk), (None, tk, n), lambda j, i, kk: (j, kk, 0),
               shape, out_dtype, o_bs, o_im, grid, ((1,), (0,)), after=after)


def mm_plain(name, a, w, out_dtype, res=None):
    m, k = a.shape
    n = w.shape[1]
    tm, tk, tn = _pick(m, TM_CANDS if res is not None else TM_WIDE_CANDS), _pick(k, TK_CANDS), _pick(n, TN_CANDS)
    if res is None:
        tk = _whole(k, lambda t: _nbytes((tm, t), a.dtype) + _nbytes((t, tn), w.dtype) + _nbytes((tm, tn), out_dtype))
    grid = (n // tn, m // tm, k // tk)
    return _mm(name, a, w, (tm, tk), lambda j, i, kk: (i, kk), (tk, tn), lambda j, i, kk: (kk, j),
               (m, n), out_dtype, (tm, tn), lambda j, i, kk: (i, j), grid, ((1,), (0,)),
               res, (tm, tn), lambda j, i, kk: (i, j))


def mm_jsum(name, aj, wg, out_dtype, res=None, after=None):
    nj, m, ka = aj.shape
    n = wg.shape[2]
    tm, tn = _pick(m, TM_CANDS if res is not None else TM_WIDE_CANDS), _pick(n, TN_CANDS)
    grid = (m // tm, n // tn, nj)
    return _mm(name, aj, wg, (None, tm, ka), lambda i, jn, j: (j, i, 0), (None, ka, tn), lambda i, jn, j: (j, 0, jn),
               (m, n), out_dtype, (tm, tn), lambda i, jn, j: (i, jn), grid, ((1,), (0,)),
               res, (tm, tn), lambda i, jn, j: (i, jn), after=after)


def mm_tn_cols(name, a, g, nj, n, out_dtype, folded, after=None):
    s, kw = a.shape
    tkw = _pick(kw, TM_WIDE_CANDS)
    ts = _whole(s, lambda t: _nbytes((t, tkw), a.dtype) + _nbytes((t, n), g.dtype) + _nbytes((tkw, n), out_dtype))
    grid = (nj, kw // tkw, s // ts)
    if folded:
        g_bs, g_im = (ts, n), (lambda j, i, ss: (ss, j))
    else:
        g_bs, g_im = (None, ts, n), (lambda j, i, ss: (j, ss, 0))
    return _mm(name, a, g, (ts, tkw), lambda j, i, ss: (ss, i), g_bs, g_im,
               (nj, kw, n), out_dtype, (None, tkw, n), lambda j, i, ss: (j, i, 0), grid, ((0,), (0,)), after=after)


def mm_tn_plain(name, a, g, out_dtype):
    s, kw = a.shape
    n = g.shape[1]
    tkw, tn = _pick(kw, TM_WIDE_CANDS), _pick(n, TN_CANDS)
    ts = _whole(s, lambda t: _nbytes((t, tkw), a.dtype) + _nbytes((t, tn), g.dtype) + _nbytes((tkw, tn), out_dtype))
    grid = (kw // tkw, n // tn, s // ts)
    return _mm(name, a, g, (ts, tkw), lambda i, jn, ss: (ss, i), (ts, tn), lambda i, jn, ss: (ss, jn),
               (kw, n), out_dtype, (tkw, tn), lambda i, jn, ss: (i, jn), grid, ((0,), (0,)))


def mm_tn_j(name, aj, g, out_dtype):
    nj, s, ka = aj.shape
    n = g.shape[1]
    tn = _pick(n, TM_WIDE_CANDS)
    ts = _whole(s, lambda t: _nbytes((t, ka), aj.dtype) + _nbytes((t, tn), g.dtype) + _nbytes((ka, tn), out_dtype))
    grid = (nj, n // tn, s // ts)
    return _mm(name, aj, g, (None, ts, ka), lambda j, jn, ss: (j, ss, 0), (ts, tn), lambda j, jn, ss: (ss, jn),
               (nj, ka, n), out_dtype, (None, ka, tn), lambda j, jn, ss: (j, 0, jn), grid, ((0,), (0,)))


def mm_nt_plain(name, g, w, out_dtype):
    m, n = g.shape
    k = w.shape[0]
    tm, tkk = _pick(m, TM_WIDE_CANDS), _pick(k, TN_CANDS)
    tn = _whole(n, lambda t: _nbytes((tm, t), g.dtype) + _nbytes((tkk, t), w.dtype) + _nbytes((tm, tkk), out_dtype))
    grid = (k // tkk, m // tm, n // tn)
    return _mm(name, g, w, (tm, tn), lambda kk, i, jn: (i, jn), (tkk, tn), lambda kk, i, jn: (kk, jn),
               (m, k), out_dtype, (tm, tkk), lambda kk, i, jn: (i, kk), grid, ((1,), (1,)))


def mm_nt_j(name, g, wg, out_dtype):
    m, n = g.shape
    nj, ka, _ = wg.shape
    tm = _pick(m, TM_WIDE_CANDS)
    tn = _whole(n, lambda t: _nbytes((tm, t), g.dtype) + _nbytes((ka, t), wg.dtype) + _nbytes((tm, ka), out_dtype))
    grid = (nj, m // tm, n // tn)
    return _mm(name, g, wg, (tm, tn), lambda j, i, jn: (i, jn), (None, ka, tn), lambda j, i, jn: (j, 0, jn),
               (nj, m, ka), out_dtype, (None, tm, ka), lambda j, i, jn: (j, i, 0), grid, ((1,), (1,)))


def mm_nt_jsum(name, g, wg, out_dtype, folded, res=None, after=None):
    nj, k, n = wg.shape
    m = g.shape[0] if folded else g.shape[1]
    tm, tkk = _pick(m, TM_CANDS if res is not None else TM_WIDE_CANDS), _pick(k, TN_CANDS)
    grid = (m // tm, k // tkk, nj)
    if folded:
        g_bs, g_im = (tm, n), (lambda i, kk, j: (i, j))
    else:
        g_bs, g_im = (None, tm, n), (lambda i, kk, j: (j, i, 0))
    return _mm(name, g, wg, g_bs, g_im, (None, tkk, n), lambda i, kk, j: (j, kk, 0),
               (m, k), out_dtype, (tm, tkk), lambda i, kk, j: (i, kk), grid, ((1,), (1,)),
               res, (tm, tkk), lambda i, kk, j: (i, kk), after=after)


ROW_TILE_CANDS = (256, 128, 64, 32, 16, 8)


def _rstd(x):
    return lax.rsqrt(jnp.mean(x * x, axis=-1, keepdims=True) + RMS_EPS)


def _sigmoid(t):
    return 1.0 / (1.0 + jnp.exp(-t))


def rms_fwd(name, x, gain):
    s, d = x.shape
    ts = _pick(s, ROW_TILE_CANDS)

    def body(x_ref, g_ref, h_ref):
        xv = x_ref[...]
        h_ref[...] = ((xv * _rstd(xv)) * g_ref[...]).astype(h_ref.dtype)

    return pl.pallas_call(
        body, name=name, grid=(s // ts,),
        in_specs=[pl.BlockSpec((ts, d), lambda i: (i, 0)), pl.BlockSpec((1, d), lambda i: (0, 0))],
        out_specs=pl.BlockSpec((ts, d), lambda i: (i, 0)),
        out_shape=pltpu.HBM((s, d), BF16),
        compiler_params=_params(("parallel",), 3 * ts * d * 4))(x, gain)


def rms_bwd(name, x, gain, dh, dres, bf16_copy):
    s, d = x.shape
    ts = _pick(s, ROW_TILE_CANDS)

    def body(x_ref, g_ref, dh_ref, dr_ref, dx_ref, *rest):
        dxb_ref, dg_ref = rest if bf16_copy else (None, rest[0])
        xv = x_ref[...]
        r = _rstd(xv)
        xhat = xv * r
        dhv = dh_ref[...].astype(F32)
        dxhat = dhv * g_ref[...]
        dx = dr_ref[...] + r * (dxhat - xhat * jnp.mean(dxhat * xhat, axis=-1, keepdims=True))
        dx_ref[...] = dx
        if bf16_copy:
            dxb_ref[...] = dx.astype(dxb_ref.dtype)
        part = jnp.sum(dhv * xhat, axis=0, keepdims=True)

        @pl.when(pl.program_id(0) == 0)
        def _():
            dg_ref[...] = part

        @pl.when(pl.program_id(0) > 0)
        def _():
            dg_ref[...] += part

    row = pl.BlockSpec((ts, d), lambda i: (i, 0))
    vec = pl.BlockSpec((1, d), lambda i: (0, 0))
    copy_spec, copy_shape = ([row], [pltpu.HBM((s, d), BF16)]) if bf16_copy else ([], [])
    res = pl.pallas_call(
        body, name=name, grid=(s // ts,), in_specs=[row, vec, row, row], out_specs=[row] + copy_spec + [vec],
        out_shape=[pltpu.HBM((s, d), F32)] + copy_shape + [jax.ShapeDtypeStruct((1, d), F32)],
        compiler_params=_params(("arbitrary",), 7 * ts * d * 4))(x, gain, dh, dres)
    return (res[0], res[1], res[2]) if bf16_copy else (res[0], None, res[1])


def gate_merge_fwd(name, proj, ta, tb, d):
    s = proj.shape[0]
    ts = _pick(s, ROW_TILE_CANDS)
    cb = COL_GATES // d

    def body(ga_ref, gb_ref, ta_ref, tb_ref, o_ref):
        o_ref[...] = (_sigmoid(ga_ref[...]) * ta_ref[...] + _sigmoid(gb_ref[...]) * tb_ref[...]).astype(o_ref.dtype)

    row = pl.BlockSpec((ts, d), lambda i: (i, 0))
    return pl.pallas_call(
        body, name=name, grid=(s // ts,),
        in_specs=[pl.BlockSpec((ts, d), lambda i: (i, cb)), pl.BlockSpec((ts, d), lambda i: (i, cb + 1)), row, row],
        out_specs=row, out_shape=pltpu.HBM((s, d), BF16),
        compiler_params=_params(("parallel",), 5 * ts * d * 4))(proj, proj, ta, tb)


def gate_merge_bwd(name, dmerged, proj, ta, tb, d):
    s = proj.shape[0]
    ts = _pick(s, ROW_TILE_CANDS)
    cb = COL_GATES // d

    def body(dm_ref, ga_ref, gb_ref, ta_ref, tb_ref, dta_ref, dtb_ref, dga_ref, dgb_ref):
        dm = dm_ref[...]
        sa = _sigmoid(ga_ref[...])
        sb = _sigmoid(gb_ref[...])
        dta_ref[...] = (dm * sa).astype(dta_ref.dtype)
        dtb_ref[...] = (dm * sb).astype(dtb_ref.dtype)
        dga_ref[...] = (dm * ta_ref[...] * (sa * (1.0 - sa))).astype(dga_ref.dtype)
        dgb_ref[...] = (dm * tb_ref[...] * (sb * (1.0 - sb))).astype(dgb_ref.dtype)

    row = pl.BlockSpec((ts, d), lambda i: (i, 0))
    out = pltpu.HBM((s, d), BF16)
    return pl.pallas_call(
        body, name=name, grid=(s // ts,),
        in_specs=[row, pl.BlockSpec((ts, d), lambda i: (i, cb)), pl.BlockSpec((ts, d), lambda i: (i, cb + 1)), row, row],
        out_specs=[row, row, row, row], out_shape=[out, out, out, out],
        compiler_params=_params(("parallel",), 8 * ts * d * 4))(dmerged, proj, proj, ta, tb)


def tail_fwd_bwd(name, x2, lp, pp, gain, target):
    s, d = x2.shape
    ts = _pick(s, ROW_TILE_CANDS)

    def body(x2_ref, lp_ref, pp_ref, g_ref, t_ref, loss_ref, dx3_ref, dlp_ref, dpp_ref, dg_ref):
        gp = _sigmoid(lp_ref[...])
        ppv = pp_ref[...]
        x3 = x2_ref[...] + gp * ppv
        r = _rstd(x3)
        xhat = x3 * r
        gv = g_ref[...]
        err = xhat * gv - t_ref[...]
        loss = jnp.sum(err * err) * (0.5 / d)
        dy = err * (1.0 / d)
        dxhat = dy * gv
        dx3 = r * (dxhat - xhat * jnp.mean(dxhat * xhat, axis=-1, keepdims=True))
        dx3_ref[...] = dx3
        dlp_ref[...] = (dx3 * ppv * (gp * (1.0 - gp))).astype(dlp_ref.dtype)
        dpp_ref[...] = (dx3 * gp).astype(dpp_ref.dtype)
        part = jnp.sum(dy * xhat, axis=0, keepdims=True)
        lossv = jnp.full((1, LANES), loss, F32)

        @pl.when(pl.program_id(0) == 0)
        def _():
            dg_ref[...] = part
            loss_ref[...] = lossv

        @pl.when(pl.program_id(0) > 0)
        def _():
            dg_ref[...] += part
            loss_ref[...] += lossv

    row = pl.BlockSpec((ts, d), lambda i: (i, 0))
    vec = pl.BlockSpec((1, d), lambda i: (0, 0))
    return pl.pallas_call(
        body, name=name, grid=(s // ts,), in_specs=[row, row, row, vec, row],
        out_specs=[pl.BlockSpec((1, LANES), lambda i: (0, 0)), row, row, row, vec],
        out_shape=[jax.ShapeDtypeStruct((1, LANES), F32), pltpu.HBM((s, d), F32),
                   pltpu.HBM((s, d), BF16), pltpu.HBM((s, d), BF16),
                   jax.ShapeDtypeStruct((1, d), F32)],
        compiler_params=_params(("arbitrary",), 9 * ts * d * 4))(x2, lp, pp, gain, target)


HALO = SUBLANES
BF16_ROWS = 2 * SUBLANES


def _shift_rows(cur, prev_row, next_row):
    ts = cur.shape[0]
    rid = lax.broadcasted_iota(jnp.int32, cur.shape, 0)
    down = jnp.where(rid == 0, prev_row, pltpu.roll(cur, 1, 0))
    up = jnp.where(rid == ts - 1, next_row, pltpu.roll(cur, ts - 1, 0))
    return down, up


def _halo_specs(ts, s, nf, halo=HALO):
    nb = ts // halo
    last = s // halo - 1
    cur = pl.BlockSpec((None, ts, nf), lambda j, i: (j, i, 0))
    prev = pl.BlockSpec((None, halo, nf), lambda j, i: (j, jnp.maximum(i * nb - 1, 0), 0))
    nxt = pl.BlockSpec((None, halo, nf), lambda j, i: (j, jnp.minimum((i + 1) * nb, last), 0))
    return cur, prev, nxt


def _halo_rows(prev_ref, next_ref, n_tiles):
    i = pl.program_id(1)
    prev_row = jnp.where(i == 0, 0.0, prev_ref[HALO - 1:HALO, :].astype(F32))
    next_row = jnp.where(i == n_tiles - 1, 0.0, next_ref[0:1, :].astype(F32))
    return prev_row, next_row


def _gelu(g):
    t = jnp.tanh(GELU_C * (g + GELU_A * (g * g * g)))
    return 0.5 * g * (1.0 + t), t


def _conv(cur, down, up, cw_ref, cb_ref):
    return down * cw_ref[0:1, :] + cur * cw_ref[1:2, :] + up * cw_ref[2:3, :] + cb_ref[...]


def ffn_mid_fwd(name, gpre, u, cw, cb):
    nj, s, nf = gpre.shape
    ts = _pick(s, (512, 256, 128, 64, 32, 16, 8))
    n_tiles = s // ts
    cur, prev, nxt = _halo_specs(ts, s, nf)

    def body(g_ref, gp_ref, gn_ref, u_ref, cw_ref, cb_ref, z_ref):
        gv = g_ref[...]
        down, up = _shift_rows(gv, *_halo_rows(gp_ref, gn_ref, n_tiles))
        act, _ = _gelu(_conv(gv, down, up, cw_ref, cb_ref))
        z_ref[...] = (act * u_ref[...]).astype(z_ref.dtype)

    return pl.pallas_call(
        body, name=name, grid=(nj, n_tiles),
        in_specs=[cur, prev, nxt, cur, pl.BlockSpec((None, SUBLANES, nf), lambda j, i: (j, 0, 0)),
                  pl.BlockSpec((None, 1, nf), lambda j, i: (j, 0, 0))],
        out_specs=cur, out_shape=pltpu.HBM((nj, s, nf), BF16),
        compiler_params=_params(("parallel", "parallel"), 8 * ts * nf * 4))(gpre, gpre, gpre, u, cw, cb)


def _gelu_grad(g, t):
    return 0.5 * (1.0 + t) + 0.5 * g * (1.0 - t * t) * (GELU_C * (1.0 + 3.0 * GELU_A * (g * g)))


def ffn_mid_bwd(name, gpre, u, dz, cw, cb):
    nj, s, nf = gpre.shape
    ts = _pick(s, (512, 256, 128, 64, 32, 16, 8))
    n_tiles = s // ts
    cur, prev, nxt = _halo_specs(ts, s, nf)

    def body(g_ref, gp_ref, gn_ref, u_ref, up_ref, un_ref, dz_ref, dzp_ref, dzn_ref, cw_ref, cb_ref,
             du_ref, dgp_ref, dcw_ref):
        i = pl.program_id(1)
        w0, w1, w2, bias = cw_ref[0:1, :], cw_ref[1:2, :], cw_ref[2:3, :], cb_ref[...]
        gv = g_ref[...]
        down, up = _shift_rows(gv, *_halo_rows(gp_ref, gn_ref, n_tiles))
        gc = down * w0 + gv * w1 + up * w2 + bias
        act, t = _gelu(gc)
        dzv = dz_ref[...].astype(F32)
        du_ref[...] = (dzv * act).astype(du_ref.dtype)
        dg = dzv * u_ref[...] * _gelu_grad(gc, t)

        def edge_dg(g_before, g_at, g_after, u_at, dz_at):
            ge = g_before.astype(F32) * w0 + g_at.astype(F32) * w1 + g_after.astype(F32) * w2 + bias
            return dz_at.astype(F32) * u_at.astype(F32) * _gelu_grad(ge, _gelu(ge)[1])

        dz_before = dzp_ref[...].astype(F32)[BF16_ROWS - 1:BF16_ROWS, :]
        dz_after = dzn_ref[...].astype(F32)[0:1, :]
        dg_prev = jnp.where(i == 0, 0.0, edge_dg(gp_ref[HALO - 2:HALO - 1, :], gp_ref[HALO - 1:HALO, :], gv[0:1, :],
                                                 up_ref[HALO - 1:HALO, :], dz_before))
        dg_next = jnp.where(i == n_tiles - 1, 0.0, edge_dg(gv[ts - 1:ts, :], gn_ref[0:1, :], gn_ref[1:2, :],
                                                           un_ref[0:1, :], dz_after))
        dg_down, dg_up = _shift_rows(dg, dg_prev, dg_next)
        dgp_ref[...] = (dg_up * w0 + dg * w1 + dg_down * w2).astype(dgp_ref.dtype)
        rows = [jnp.sum(dg * down, axis=0, keepdims=True), jnp.sum(dg * gv, axis=0, keepdims=True),
                jnp.sum(dg * up, axis=0, keepdims=True), jnp.sum(dg, axis=0, keepdims=True)]
        part = jnp.concatenate(rows + [jnp.zeros((SUBLANES - len(rows), nf), F32)], axis=0)

        @pl.when(i == 0)
        def _():
            dcw_ref[...] = part

        @pl.when(i > 0)
        def _():
            dcw_ref[...] += part

    small = pl.BlockSpec((None, SUBLANES, nf), lambda j, i: (j, 0, 0))
    return pl.pallas_call(
        body, name=name, grid=(nj, n_tiles),
        in_specs=[cur, prev, nxt] * 2 + list(_halo_specs(ts, s, nf, BF16_ROWS))
        + [small, pl.BlockSpec((None, 1, nf), lambda j, i: (j, 0, 0))],
        out_specs=[cur, cur, small],
        out_shape=[pltpu.HBM((nj, s, nf), BF16), pltpu.HBM((nj, s, nf), BF16),
                   jax.ShapeDtypeStruct((nj, SUBLANES, nf), F32)],
        compiler_params=_params(("parallel", "arbitrary"), 14 * ts * nf * 4))(
            gpre, gpre, gpre, u, u, u, dz, dz, dz, cw, cb)


def _t5_bucket(rel):
    half = N_BUCKETS // 2
    max_exact = half // 2
    n = jnp.abs(rel)
    side = jnp.where(rel > 0, half, 0)
    nf = jnp.maximum(n, 1).astype(F32)
    large = max_exact + (jnp.log(nf / max_exact) / math.log(MAX_DISTANCE / max_exact)
                         * (half - max_exact)).astype(jnp.int32)
    large = jnp.minimum(large, half - 1)
    return side + jnp.where(n < max_exact, n, large)


def bucket_tile(rows, half, dil):
    rel = (jnp.arange(rows + 2 * half)[None, :] - half) - jnp.arange(rows)[:, None]
    return _t5_bucket(rel * dil).astype(jnp.int32)


def bias_build(name, table_t, bucket, h0, nh, half):
    blk, kw = bucket.shape

    def body(t_ref, b_ref, o_ref):
        h = pl.program_id(0)
        bv = b_ref[...]
        acc = jnp.zeros((blk, kw), F32)
        for b in range(N_BUCKETS):
            acc = jnp.where(bv == b, t_ref[h0 + h, b], acc)
        qi = lax.broadcasted_iota(jnp.int32, (blk, kw), 0)
        ci = lax.broadcasted_iota(jnp.int32, (blk, kw), 1)
        o_ref[...] = jnp.where(jnp.abs(ci - half - qi) <= half, acc, NEG_INF)

    return pl.pallas_call(
        body, name=name, grid=(nh,),
        in_specs=[pl.BlockSpec(memory_space=pltpu.SMEM), pl.BlockSpec((blk, kw), lambda h: (0, 0))],
        out_specs=pl.BlockSpec((None, blk, kw), lambda h: (h, 0, 0)),
        out_shape=jax.ShapeDtypeStruct((nh, blk, kw), F32),
        compiler_params=_params(("parallel",), 4 * blk * kw * 4))(table_t, bucket)


def table_grad(name, dbias, bucket):
    nh, blk, kw = dbias.shape

    def body(d_ref, b_ref, o_ref):
        bv = b_ref[...]
        dv = d_ref[...]
        lane = lax.broadcasted_iota(jnp.int32, (SUBLANES, LANES), 1)
        acc = jnp.zeros((SUBLANES, LANES), F32)
        for b in range(N_BUCKETS):
            acc = jnp.where(lane == b, jnp.sum(jnp.where(bv == b, dv, 0.0)), acc)
        o_ref[...] = acc

    return pl.pallas_call(
        body, name=name, grid=(nh,),
        in_specs=[pl.BlockSpec((None, blk, kw), lambda h: (h, 0, 0)), pl.BlockSpec((blk, kw), lambda h: (0, 0))],
        out_specs=pl.BlockSpec((None, SUBLANES, LANES), lambda h: (h, 0, 0)),
        out_shape=jax.ShapeDtypeStruct((nh, SUBLANES, LANES), F32),
        compiler_params=_params(("parallel",), 4 * blk * kw * 4))(dbias, bucket)


class _Band:
    def __init__(self, s, half, q_rows, n_chains, dil):
        self.s, self.half, self.dil, self.n_chains = s, half, dil, n_chains
        self.seg = s // dil
        self.q_rows = min(q_rows, self.seg)
        self.win = self.q_rows + 2 * half
        self.pad = self.seg + 2 * half
        self.nsb = self.seg // self.q_rows
        self.n_items = dil * self.nsb
        assert self.n_items % n_chains == 0 and self.seg % self.q_rows == 0
        self.staged = dil > 1

    def rows_of(self, r):
        return pl.ds(r, self.seg, stride=self.dil) if self.dil > 1 else slice(None)

    def stage_kv(self, dst, src_ref):
        zeros = jnp.zeros((self.half, HEAD_DIM), dst.dtype)
        for r in range(self.dil):
            base = r * self.pad
            dst[base:base + self.half, :] = zeros
            dst[base + self.half + self.seg:base + self.pad, :] = zeros
            dst[base + self.half:base + self.half + self.seg, :] = src_ref[self.rows_of(r), :].astype(dst.dtype)

    def stage(self, dst, src_ref):
        for r in range(self.dil):
            dst[r * self.seg:(r + 1) * self.seg, :] = src_ref[self.rows_of(r), :].astype(dst.dtype)

    def unstage(self, dst_ref, src, add=False):
        for r in range(self.dil):
            val = src[r * self.seg:(r + 1) * self.seg, :].astype(dst_ref.dtype)
            if add:
                val = val + dst_ref[self.rows_of(r), :]
            dst_ref[self.rows_of(r), :] = val

    def offsets(self, item):
        r, sb = item // self.nsb, item % self.nsb
        qoff = pl.multiple_of(r * self.seg + sb * self.q_rows, self.q_rows)
        koff = pl.multiple_of(r * self.pad + sb * self.q_rows, B_BLOCK)
        kpos = sb * self.q_rows - self.half + lax.broadcasted_iota(jnp.int32, (1, self.win), 1)
        edge = jnp.where((kpos >= 0) & (kpos < self.seg), 0.0, NEG_INF)
        return qoff, koff, edge


def band_attn_fwd(name, proj, bias, sink, *, half, q_rows, n_chains, dil, nh, group, cq, ck, cv):
    s, w = proj.shape
    g = _Band(s, half, q_rows, n_chains, dil)
    has_sink = sink is not None

    def body(*refs):
        q_ref, k_ref, v_ref, b_ref = refs[:4]
        s_ref = refs[4] if has_sink else None
        o_ref, l_ref, ks, vs = refs[4 + has_sink:8 + has_sink]
        qs, os_, ls = refs[8 + has_sink:] if g.staged else (None, o_ref, l_ref)
        g.stage_kv(ks, k_ref)
        g.stage_kv(vs, v_ref)
        if g.staged:
            g.stage(qs, q_ref)
        bias_v = b_ref[...]
        sk = s_ref[pl.program_id(0)] if has_sink else None

        def chain(item):
            qoff, koff, edge = g.offsets(item)
            rows = pl.ds(qoff, g.q_rows)
            qv = qs[rows, :] if g.staged else q_ref[rows, :].astype(BF16)
            kw_ = ks[pl.ds(koff, g.win), :]
            vw_ = vs[pl.ds(koff, g.win), :]
            sc = lax.dot_general(qv, kw_, (((1,), (1,)), ((), ())), preferred_element_type=F32) * ATTN_SCALE
            sc = sc + bias_v + edge
            m = jnp.max(sc, axis=-1, keepdims=True)
            if has_sink:
                m = jnp.maximum(m, sk)
            p = jnp.exp(sc - m)
            den = jnp.sum(p, axis=-1, keepdims=True)
            if has_sink:
                den = den + jnp.exp(sk - m)
            out = lax.dot_general(p.astype(BF16), vw_, (((1,), (0,)), ((), ())), preferred_element_type=F32)
            return rows, out / den, jnp.broadcast_to(m + jnp.log(den), (g.q_rows, HEAD_DIM))

        def step(i, carry):
            for rows, out, lse in [chain(i * n_chains + u) for u in range(n_chains)]:
                os_[rows, :] = out
                ls[rows, :] = lse
            return carry

        lax.fori_loop(0, g.n_items // n_chains, step, 0)
        if g.staged:
            g.unstage(o_ref, os_)
            g.unstage(l_ref, ls)

    def col(c0, per):
        return pl.BlockSpec((s, HEAD_DIM), lambda h: (0, c0 // LANES + h // per))

    in_specs = [col(cq, 1), col(ck, group), col(cv, group),
                pl.BlockSpec((None, g.q_rows, g.win), lambda h: (h, 0, 0))]
    args = [proj, proj, proj, bias]
    if has_sink:
        in_specs.append(pl.BlockSpec(memory_space=pltpu.SMEM))
        args.append(sink)
    shape = pltpu.HBM((s, nh * HEAD_DIM), F32)
    scratch = [pltpu.VMEM((dil * g.pad, HEAD_DIM), BF16), pltpu.VMEM((dil * g.pad, HEAD_DIM), BF16)]
    if g.staged:
        scratch += [pltpu.VMEM((s, HEAD_DIM), BF16), pltpu.VMEM((s, HEAD_DIM), F32), pltpu.VMEM((s, HEAD_DIM), F32)]
    return pl.pallas_call(
        body, name=name, grid=(nh,), in_specs=in_specs, out_specs=[col(0, 1), col(0, 1)], out_shape=[shape, shape],
        scratch_shapes=scratch, compiler_params=_params(("parallel",), 16 * s * HEAD_DIM * 4))(*args)


def band_attn_bwd(name, proj, bias, sink, dout, out, lse, dlse, *, half, q_rows, n_chains, dil, nh, group, cq, ck, cv):
    s, w = proj.shape
    g = _Band(s, half, q_rows, n_chains, dil)
    nkv = nh // group
    has_sink = sink is not None
    has_dl = dlse is not None
    n_in = 7 + int(has_sink) + int(has_dl)
    n_out = 4 + int(has_sink)

    def body(*refs):
        ins, outs, scr = refs[:n_in], refs[n_in:n_in + n_out], refs[n_in + n_out:]
        q_ref, k_ref, v_ref, b_ref, do_ref, o_ref, l_ref = ins[:7]
        s_ref = ins[7] if has_sink else None
        dl_ref = ins[n_in - 1] if has_dl else None
        dq_ref, dk_ref, dv_ref, db_ref = outs[:4]
        ks, vs, dks, dvs = scr[:4]
        scr = list(scr[4:])
        dsa = scr.pop(0) if has_sink else None
        if g.staged:
            qs, dos, os_, ls, dqs = scr[:5]
            dls = scr[5] if has_dl else None
            g.stage(qs, q_ref)
            g.stage(dos, do_ref)
            g.stage(os_, o_ref)
            g.stage(ls, l_ref)
            if has_dl:
                g.stage(dls, dl_ref)
        else:
            qs, dos, os_, ls, dqs, dls = None, do_ref, o_ref, l_ref, dq_ref, dl_ref
        h = pl.program_id(0)
        g.stage_kv(ks, k_ref)
        g.stage_kv(vs, v_ref)
        dks[...] = jnp.zeros_like(dks)
        dvs[...] = jnp.zeros_like(dvs)
        db_ref[...] = jnp.zeros_like(db_ref)
        bias_v = b_ref[...]
        if has_sink:
            sk = s_ref[h]
            dsa[...] = jnp.zeros_like(dsa)

        def chain(item):
            qoff, koff, edge = g.offsets(item)
            rows = pl.ds(qoff, g.q_rows)
            win = pl.ds(koff, g.win)
            qv = qs[rows, :] if g.staged else q_ref[rows, :].astype(BF16)
            kw_ = ks[win, :]
            vw_ = vs[win, :]
            sc = lax.dot_general(qv, kw_, (((1,), (1,)), ((), ())), preferred_element_type=F32) * ATTN_SCALE
            lv = ls[rows, :][:, 0:1]
            p = jnp.exp(sc + bias_v + edge - lv)
            dov = dos[rows, :]
            delta = jnp.sum(dov * os_[rows, :], axis=-1, keepdims=True)
            dob = dov.astype(BF16)
            dp = lax.dot_general(dob, vw_, (((1,), (1,)), ((), ())), preferred_element_type=F32)
            t = dp - delta
            if has_dl:
                t = t + dls[rows, :][:, 0:1]
            ds = p * t
            dsb = (ds * ATTN_SCALE).astype(BF16)
            dq = lax.dot_general(dsb, kw_, (((1,), (0,)), ((), ())), preferred_element_type=F32)
            dkc = lax.dot_general(dsb, qv, (((0,), (0,)), ((), ())), preferred_element_type=F32)
            dvc = lax.dot_general(p.astype(BF16), dob, (((0,), (0,)), ((), ())), preferred_element_type=F32)
            dsk = jnp.exp(sk - lv) * delta if has_sink else None
            return rows, win, dq, dkc, dvc, ds, dsk

        def step(i, carry):
            res = [chain(i * n_chains + u) for u in range(n_chains)]
            ds_sum = res[0][5]
            for rr in res[1:]:
                ds_sum = ds_sum + rr[5]
            db_ref[...] += ds_sum
            for rows, win, dq, dkc, dvc, ds, dsk in res:
                dqs[rows, :] = dq
                dks[win, :] += dkc
                dvs[win, :] += dvc
                if has_sink:
                    dsa[...] += dsk
            return carry

        lax.fori_loop(0, g.n_items // n_chains, step, 0)

        if g.staged:
            g.unstage(dq_ref, dqs)

        def emit_kv(add):
            for r in range(dil):
                lo = r * g.pad + half
                for dst_ref, src in ((dk_ref, dks), (dv_ref, dvs)):
                    val = src[lo:lo + g.seg, :]
                    if add:
                        val = val + dst_ref[g.rows_of(r), :]
                    dst_ref[g.rows_of(r), :] = val

        if group == 1:
            emit_kv(False)
        else:
            @pl.when(h % group == 0)
            def _():
                emit_kv(False)

            @pl.when(h % group != 0)
            def _():
                emit_kv(True)
        if has_sink:
            outs[4][...] = jnp.full((SUBLANES, LANES), -jnp.sum(dsa[...]), F32)

    def col(c0, per):
        return pl.BlockSpec((s, HEAD_DIM), lambda h: (0, c0 // LANES + h // per))

    b_spec = pl.BlockSpec((None, g.q_rows, g.win), lambda h: (h, 0, 0))
    in_specs = [col(cq, 1), col(ck, group), col(cv, group), b_spec, col(0, 1), col(0, 1), col(0, 1)]
    args = [proj, proj, proj, bias, dout, out, lse]
    if has_sink:
        in_specs.append(pl.BlockSpec(memory_space=pltpu.SMEM))
        args.append(sink)
    if has_dl:
        in_specs.append(col(0, 1))
        args.append(dlse)
    out_specs = [col(0, 1), col(0, group), col(0, group), b_spec]
    out_shape = [pltpu.HBM((s, nh * HEAD_DIM), F32), pltpu.HBM((s, nkv * HEAD_DIM), F32),
                 pltpu.HBM((s, nkv * HEAD_DIM), F32), jax.ShapeDtypeStruct((nh, g.q_rows, g.win), F32)]
    scratch = [pltpu.VMEM((dil * g.pad, HEAD_DIM), BF16), pltpu.VMEM((dil * g.pad, HEAD_DIM), BF16),
               pltpu.VMEM((dil * g.pad, HEAD_DIM), F32), pltpu.VMEM((dil * g.pad, HEAD_DIM), F32)]
    if has_sink:
        out_specs.append(pl.BlockSpec((None, SUBLANES, LANES), lambda h: (h, 0, 0)))
        out_shape.append(jax.ShapeDtypeStruct((nh, SUBLANES, LANES), F32))
        scratch.append(pltpu.VMEM((g.q_rows, 1), F32))
    if g.staged:
        scratch += [pltpu.VMEM((s, HEAD_DIM), BF16)] + [pltpu.VMEM((s, HEAD_DIM), F32)] * (4 + int(has_dl))
    res = pl.pallas_call(
        body, name=name, grid=(nh,), in_specs=in_specs, out_specs=out_specs, out_shape=out_shape,
        scratch_shapes=scratch, compiler_params=_params(("arbitrary",), 28 * s * HEAD_DIM * 4))(*args)
    return res[0], res[1], res[2], res[3], (res[4] if has_sink else None)


def dil_merge_fwd(name, outs, lses):
    s, w = outs[0].shape
    ts = _pick(s, ROW_TILE_CANDS)
    ng = len(outs)

    def body(*refs):
        o_refs, l_refs, y_ref = refs[:ng], refs[ng:2 * ng], refs[2 * ng]
        ls = [l[...] for l in l_refs]
        mx = ls[0]
        for l in ls[1:]:
            mx = jnp.maximum(mx, l)
        es = [jnp.exp(l - mx) for l in ls]
        tot = es[0]
        for e in es[1:]:
            tot = tot + e
        acc = (es[0] / tot) * o_refs[0][...]
        for e, o in zip(es[1:], o_refs[1:]):
            acc = acc + (e / tot) * o[...]
        y_ref[...] = acc.astype(y_ref.dtype)

    row = pl.BlockSpec((ts, w), lambda i: (i, 0))
    return pl.pallas_call(
        body, name=name, grid=(s // ts,), in_specs=[row] * (2 * ng), out_specs=row,
        out_shape=pltpu.HBM((s, w), BF16),
        compiler_params=_params(("parallel",), 10 * ts * w * 4))(*outs, *lses)


def dil_merge_bwd(name, dy, outs, lses):
    s, w = outs[0].shape
    ts = _pick(s, ROW_TILE_CANDS)
    ng = len(outs)
    nhead = w // HEAD_DIM

    def body(*refs):
        dy_ref = refs[0]
        o_refs, l_refs = refs[1:1 + ng], refs[1 + ng:1 + 2 * ng]
        do_refs, dl_refs = refs[1 + 2 * ng:1 + 3 * ng], refs[1 + 3 * ng:1 + 4 * ng]
        for hh in range(nhead):
            cols = slice(hh * HEAD_DIM, (hh + 1) * HEAD_DIM)
            dyv = dy_ref[:, cols]
            ls = [l[:, cols] for l in l_refs]
            mx = ls[0]
            for l in ls[1:]:
                mx = jnp.maximum(mx, l)
            es = [jnp.exp(l - mx) for l in ls]
            tot = es[0]
            for e in es[1:]:
                tot = tot + e
            alphas = [e / tot for e in es]
            dal = [jnp.broadcast_to(jnp.sum(dyv * o[:, cols], axis=-1, keepdims=True), dyv.shape) for o in o_refs]
            mean = alphas[0] * dal[0]
            for a, d in zip(alphas[1:], dal[1:]):
                mean = mean + a * d
            for g in range(ng):
                do_refs[g][:, cols] = alphas[g] * dyv
                dl_refs[g][:, cols] = alphas[g] * (dal[g] - mean)

    row = pl.BlockSpec((ts, w), lambda i: (i, 0))
    shape = pltpu.HBM((s, w), F32)
    res = pl.pallas_call(
        body, name=name, grid=(s // ts,), in_specs=[row] * (1 + 2 * ng), out_specs=[row] * (2 * ng),
        out_shape=[shape] * (2 * ng),
        compiler_params=_params(("parallel",), 16 * ts * w * 4))(dy, *outs, *lses)
    return res[:ng], res[ng:]


def _adamw(w, g, m, v):
    m = ADAM_B1 * m + (1.0 - ADAM_B1) * g
    v = ADAM_B2 * v + (1.0 - ADAM_B2) * (g * g)
    m_hat = m / (1.0 - ADAM_B1 ** ADAM_STEP)
    v_hat = v / (1.0 - ADAM_B2 ** ADAM_STEP)
    delta = -ADAM_LR * (m_hat / (jnp.sqrt(v_hat) + ADAM_EPS) + ADAM_WD * w)
    return delta, m, v


def _row_tile(r, c, budget=1 << 20):
    if r * c * 4 <= budget or r % SUBLANES:
        return r
    for t in (1024, 512, 256, 128, 64, 32, 16, 8):
        if r % t == 0 and t * c * 4 <= budget:
            return t
    return SUBLANES


def adam_small(name, g, w, m, v):
    def body(g_ref, w_ref, m_ref, v_ref, d_ref, nm_ref, nv_ref):
        d_ref[...], nm_ref[...], nv_ref[...] = _adamw(w_ref[...], g_ref[...], m_ref[...], v_ref[...])

    shape = jax.ShapeDtypeStruct(w.shape, F32)
    return pl.pallas_call(body, name=name, out_shape=[shape, shape, shape])(g, w, m, v)


def reduce_adam(name, mine, theirs, w, m, v):
    nq, r, c = mine.shape
    tr = _row_tile(r, c)

    def body(*refs):
        parts, (w_ref, m_ref, v_ref, g_ref, d_ref, nm_ref, nv_ref) = refs[:nq], refs[nq:]
        g = parts[0][...].astype(F32)
        for p_ref in parts[1:]:
            g = g + p_ref[...].astype(F32)
        g_ref[...] = g
        d_ref[...], nm_ref[...], nv_ref[...] = _adamw(w_ref[...], g, m_ref[...], v_ref[...])

    def slot(q):
        return pl.BlockSpec((None, tr, c), lambda i: (q, i, 0))

    row = pl.BlockSpec((tr, c), lambda i: (i, 0))
    shape = jax.ShapeDtypeStruct((r, c), F32)
    return pl.pallas_call(
        body, name=name, grid=(r // tr,), in_specs=[slot(q) for q in range(nq)] + [row, row, row],
        out_specs=[row] * 4, out_shape=[shape] * 4,
        compiler_params=_params(("parallel",), (nq * 2 + 7 * 4) * tr * c))(mine, *[theirs] * (nq - 1), *[_in_hbm(t) for t in (w, m, v)])


def _place():
    return lax.axis_index("x"), lax.axis_index("y"), lax.axis_index("c")


def _flip(pos, bits):
    return tuple((1 - p) if b else p for p, b in zip(pos, bits))


def _index(pos):
    return 4 * pos[0] + 2 * pos[1] + pos[2]


ANY = pl.BlockSpec(memory_space=pl.ANY)


HBM = pl.BlockSpec(memory_space=pltpu.HBM)
SEM = pl.BlockSpec(memory_space=pltpu.SEMAPHORE)
EFFECT = pltpu.SideEffectType.DATAFLOW_SIDE_EFFECTING
TO_SIBLING = (0, 0, 1)
TO_CHIPS = [(1, 0, 0), (0, 1, 0), (1, 1, 0)]


def _in_hbm(a):
    return pltpu.with_memory_space_constraint(a, pltpu.HBM)


def _token_value(token):
    return token[0, 0]


def _when(pred, fn):
    if pred is True:
        fn()
    elif pred is not False:
        pl.when(pred)(fn)


def _plan_copy(k, entry, ins, lnd, send_sems, recv_sems):
    a, src_a, sblk, lblk, to, send_if, recv_if = entry
    src = lnd[a] if src_a is None else ins[src_a]
    return pltpu.make_async_remote_copy(
        src_ref=src.at[sblk], dst_ref=lnd[a].at[lblk], send_sem=send_sems.at[k], recv_sem=recv_sems.at[k],
        device_id=to, device_id_type=MESH), send_if, recv_if


def split_start(name, srcs, lands, plan, after):
    ns, nl = len(srcs), len(lands)
    n_copies = len(plan((0, 0, 0)))

    def body(*refs):
        ins, lnd = refs[:ns], refs[ns:ns + nl]
        send_sems, recv_sems = refs[ns + nl + 1], refs[ns + nl + 2]
        token = refs[-1]
        for k, entry in enumerate(plan(_place())):
            cp, send_if, _ = _plan_copy(k, entry, ins, lnd, send_sems, recv_sems)
            _when(send_if, cp.start)
        token[...] = jnp.zeros_like(token)

    outs = pl.pallas_call(
        body, name=name,
        out_shape=(pltpu.SemaphoreType.DMA((n_copies,)), pltpu.SemaphoreType.DMA((n_copies,)),
                   *[pltpu.HBM(a.shape, a.dtype) for a in srcs], *[pltpu.HBM(a.shape, a.dtype) for a in lands],
                   jax.ShapeDtypeStruct((SUBLANES, LANES), F32)),
        in_specs=[HBM] * (ns + nl) + [ANY],
        out_specs=(SEM, SEM, *[HBM] * (ns + nl), pl.BlockSpec(memory_space=pltpu.VMEM)),
        input_output_aliases={i: 2 + i for i in range(ns + nl)},
        compiler_params=pltpu.CompilerParams(has_side_effects=EFFECT),
    )(*[_in_hbm(a) for a in srcs], *[_in_hbm(a) for a in lands], after)
    return outs[0], outs[1], list(outs[2:2 + ns]), list(outs[2 + ns:2 + ns + nl]), outs[-1]


def split_wait(name, send_sems, recv_sems, srcs, lands, plan, after):
    ns, nl = len(srcs), len(lands)

    def body(*refs):
        ins, lnd = refs[:ns], refs[ns:ns + nl]
        s_sems, r_sems = refs[ns + nl], refs[ns + nl + 1]
        for k, entry in enumerate(plan(_place())):
            cp, send_if, recv_if = _plan_copy(k, entry, ins, lnd, s_sems, r_sems)
            _when(send_if, cp.wait_send)
            _when(recv_if, cp.wait_recv)
        refs[-1][...] = jnp.zeros((SUBLANES, LANES), F32)

    outs = pl.pallas_call(
        body, name=name,
        out_shape=(*[pltpu.HBM(a.shape, a.dtype) for a in srcs], *[pltpu.HBM(a.shape, a.dtype) for a in lands],
                   jax.ShapeDtypeStruct((SUBLANES, LANES), F32)),
        in_specs=[HBM] * (ns + nl) + [SEM, SEM, ANY],
        out_specs=(*[HBM] * (ns + nl), pl.BlockSpec(memory_space=pltpu.VMEM)),
        input_output_aliases={i: i for i in range(ns + nl)},
        compiler_params=pltpu.CompilerParams(has_side_effects=EFFECT),
    )(*srcs, *lands, send_sems, recv_sems, after)
    return list(outs[:ns]), list(outs[ns:ns + nl]), outs[-1]


NORTH = 1


def ag_plan(n, rels=TO_CHIPS):
    def plan(me):
        x, y, c = me
        entries = []
        for a in range(n):
            for t in (NORTH, 1 - NORTH):
                blk = _index((x, y, t))
                for rel in rels:
                    entries.append((a, None, blk, blk, _flip((x, y, t), rel), c == NORTH, c == t))
        return entries
    return plan


TO_X, TO_Y = TO_CHIPS[0], TO_CHIPS[1]


def relay_plan(n):
    def plan(me):
        x, y, c = me
        entries = []
        for a in range(n):
            for t, came, goes in ((NORTH, TO_X, TO_Y), (1 - NORTH, TO_Y, TO_X)):
                blk = _index(_flip((x, y, t), came))
                entries.append((a, None, blk, blk, _flip((x, y, t), goes), c == t, c == t))
        return entries
    return plan


def ag_pair(name, lands, after):
    n = len(lands)

    def body(*refs):
        lnd = refs[n + 1:2 * n + 1]
        token = refs[2 * n + 1]
        send_sems, recv_sems = refs[2 * n + 2:]
        token[...] = jnp.zeros_like(token)
        me = _place()
        sibling = _flip(me, TO_SIBLING)
        copies = []
        for a in range(n):
            mine, theirs = lnd[a].at[_index(me)], lnd[a].at[_index(sibling)]
            cp = pltpu.make_async_remote_copy(src_ref=mine, dst_ref=mine, send_sem=send_sems.at[a],
                                              recv_sem=recv_sems.at[a], device_id=sibling, device_id_type=MESH)
            cp.start()
            copies.append((cp, pltpu.make_async_remote_copy(
                src_ref=mine, dst_ref=theirs, send_sem=send_sems.at[a], recv_sem=recv_sems.at[a], device_id=sibling,
                device_id_type=MESH)))
        for cp, arrival in copies:
            arrival.wait_recv()
        for cp, arrival in copies:
            cp.wait_send()

    outs = pl.pallas_call(
        body, name=name, in_specs=[ANY] * (n + 1), out_specs=[ANY] * n + [pl.BlockSpec(memory_space=pltpu.VMEM)],
        out_shape=[jax.ShapeDtypeStruct(l.shape, l.dtype) for l in lands]
        + [jax.ShapeDtypeStruct((SUBLANES, LANES), F32)],
        input_output_aliases={a: a for a in range(n)},
        scratch_shapes=[pltpu.SemaphoreType.DMA((n,)), pltpu.SemaphoreType.DMA((n,))],
    )(*lands, after)
    return list(outs[:n]), outs[n]


def pass_plan(n):
    def plan(me):
        sibling = _flip(me, TO_SIBLING)
        return [(a, None, _index(_flip(me, rel)), _index(_flip(me, rel)), sibling, True, True)
                for a in range(n) for rel in TO_CHIPS]
    return plan


def ag_finish(name, lands):
    n = len(lands)

    def body(*refs):
        lnd = refs[n:2 * n]
        send_sems, recv_sems = refs[2 * n:]
        me = _place()
        sibling = _flip(me, TO_SIBLING)
        copies = []
        for a in range(n):
            for j, rel in enumerate(TO_CHIPS):
                blk = lnd[a].at[_index(_flip(me, rel))]
                there = lnd[a].at[_index(_flip(sibling, rel))]
                cp = pltpu.make_async_remote_copy(
                    src_ref=blk, dst_ref=blk, send_sem=send_sems.at[a * 3 + j], recv_sem=recv_sems.at[a * 3 + j],
                    device_id=sibling, device_id_type=MESH)
                cp.start()
                copies.append((cp, pltpu.make_async_remote_copy(
                    src_ref=blk, dst_ref=there, send_sem=send_sems.at[a * 3 + j], recv_sem=recv_sems.at[a * 3 + j],
                    device_id=sibling, device_id_type=MESH)))
        for cp, arrival in copies:
            arrival.wait_recv()
        for cp, arrival in copies:
            cp.wait_send()

    return pl.pallas_call(
        body, name=name, in_specs=[ANY] * n, out_specs=[ANY] * n,
        out_shape=[jax.ShapeDtypeStruct(l.shape, l.dtype) for l in lands],
        input_output_aliases={a: a for a in range(n)},
        scratch_shapes=[pltpu.SemaphoreType.DMA((3 * n,)), pltpu.SemaphoreType.DMA((3 * n,))],
    )(*lands)


REL = [(b >> 2 & 1, b >> 1 & 1, b & 1) for b in range(N_DEV)]


CHIP_REL = [(0, 0, 0)] + TO_CHIPS
N_CHIPS = len(CHIP_REL)


def rs_pair(name, parts):
    n = len(parts)

    def body(*refs):
        ins, got = refs[:n], refs[n:2 * n]
        send_sems, recv_sems = refs[2 * n:]
        me = _place()
        sibling = _flip(me, TO_SIBLING)
        remote = []
        for a in range(n):
            for q, rel in enumerate(CHIP_REL):
                k = a * N_CHIPS + q
                cp = pltpu.make_async_remote_copy(
                    src_ref=ins[a].at[_index(_flip(sibling, rel))], dst_ref=got[a].at[q], send_sem=send_sems.at[k],
                    recv_sem=recv_sems.at[k], device_id=sibling, device_id_type=MESH)
                cp.start()
                remote.append(cp)
        for cp in remote:
            cp.wait_recv()
        for cp in remote:
            cp.wait_send()

    shapes = [jax.ShapeDtypeStruct((N_CHIPS,) + tuple(p.shape[1:]), p.dtype) for p in parts]
    res = pl.pallas_call(
        body, name=name, in_specs=[ANY] * n, out_specs=[ANY] * n, out_shape=shapes,
        scratch_shapes=[pltpu.SemaphoreType.DMA((N_CHIPS * n,)), pltpu.SemaphoreType.DMA((N_CHIPS * n,))],
    )(*parts)
    return list(res)


def own_blocks():
    me = _place()
    return jnp.stack([_index(_flip(me, rel)) for rel in CHIP_REL]).astype(jnp.int32)


def pair_add(name, blocks, parts, got):
    nq, r, c = got.shape
    tr = _row_tile(r, c, budget=6 << 20)

    def body(blk_ref, a_ref, b_ref, o_ref):
        o_ref[...] = (a_ref[...].astype(F32) + b_ref[...].astype(F32)).astype(o_ref.dtype)

    spec = pl.BlockSpec((None, tr, c), lambda q, i, blk: (q, i, 0))
    return pl.pallas_call(
        body, name=name,
        grid_spec=pltpu.PrefetchScalarGridSpec(
            num_scalar_prefetch=1, grid=(nq, r // tr),
            in_specs=[pl.BlockSpec((None, tr, c), lambda q, i, blk: (blk[q], i, 0)), spec], out_specs=spec),
        out_shape=pltpu.HBM(got.shape, got.dtype),
        compiler_params=_params(("arbitrary", "arbitrary"), 6 * tr * c * 2))(blocks, parts, got)


def rs_pair_plan(n):
    def plan(me):
        sibling = _flip(me, TO_SIBLING)
        return [(a, a, _index(_flip(sibling, rel)), q, sibling, True, True)
                for a in range(n) for q, rel in enumerate(CHIP_REL)]
    return plan


def rs_plan(n):
    def plan(me):
        return [(a, a, q, q, _flip(me, CHIP_REL[q]), True, True) for a in range(n) for q in range(1, N_CHIPS)]
    return plan


def rs_start(name, sums, after):
    lands = [lax.empty(t.shape, t.dtype) for t in sums]
    return split_start(name, sums, lands, rs_plan(len(sums)), after)


def allreduce_small(name, pack, after):
    rows, lanes = pack.shape

    def body(x_ref, after_ref, o_ref, land, send_sems, recv_sems):
        me = _place()
        idx = _index(me)
        land[idx] = x_ref[...]
        copies = []
        for r in range(1, N_DEV):
            peer = _flip(me, REL[r])
            cp = pltpu.make_async_remote_copy(
                src_ref=x_ref, dst_ref=land.at[idx], send_sem=send_sems.at[r - 1], recv_sem=recv_sems.at[r - 1],
                device_id=peer, device_id_type=MESH)
            cp.start()
            copies.append(cp)
        for cp in copies:
            cp.wait_recv()
        for cp in copies:
            cp.wait_send()
        acc = land[0]
        for i in range(1, N_DEV):
            acc = acc + land[i]
        o_ref[...] = acc

    return pl.pallas_call(
        body, name=name, in_specs=[pl.BlockSpec(memory_space=pltpu.VMEM), ANY],
        out_specs=pl.BlockSpec(memory_space=pltpu.VMEM), out_shape=jax.ShapeDtypeStruct((rows, lanes), F32),
        scratch_shapes=[pltpu.VMEM((N_DEV, rows, lanes), F32), pltpu.SemaphoreType.DMA((7,)),
                        pltpu.SemaphoreType.DMA((7,))],
    )(pack, after)


def _pad_rows(a, rows):
    return jnp.pad(a, ((0, rows - a.shape[0]), (0, 0)))


def _as_tiles(vec):
    n = vec.shape[0]
    rows = -(-n // LANES)
    rows = -(-rows // SUBLANES) * SUBLANES
    return jnp.pad(vec, (0, rows * LANES - n)).reshape(rows, LANES)


def kernel(x, p, rel_bias_table, attn_norm, w_in, sink_a, w_branch_a, w_branch_b, w_out, ffn_norm, w_ffn_gate, w_ffn_up, conv_w, conv_b, w_ffn_down, ple_norm, w_ple_gate, w_ple_proj, final_norm, loss_target, m_rel_bias_table, m_attn_norm, m_w_in, m_sink_a, m_w_branch_a, m_w_branch_b, m_w_out, m_ffn_norm, m_w_ffn_gate, m_w_ffn_up, m_conv_w, m_conv_b, m_w_ffn_down, m_ple_norm, m_w_ple_gate, m_w_ple_proj, m_final_norm, v_rel_bias_table, v_attn_norm, v_w_in, v_sink_a, v_w_branch_a, v_w_branch_b, v_w_out, v_ffn_norm, v_w_ffn_gate, v_w_ffn_up, v_conv_w, v_conv_b, v_w_ffn_down, v_ple_norm, v_w_ple_gate, v_w_ple_proj, v_final_norm):
    xs = x[0]
    s, d = xs.shape
    ps = p[0, 0]
    target = loss_target[0]
    me = 4 * lax.axis_index("x") + 2 * lax.axis_index("y") + lax.axis_index("c")

    big = dict(w_in=w_in[0], w_branch_a=w_branch_a[0], w_branch_b=w_branch_b[0], w_out=w_out[0],
               w_ffn_gate=w_ffn_gate[0], w_ffn_up=w_ffn_up[0], w_ffn_down=w_ffn_down[0],
               w_ple_gate=w_ple_gate[0], w_ple_proj=w_ple_proj[0])
    big_m = dict(w_in=m_w_in[0], w_branch_a=m_w_branch_a[0], w_branch_b=m_w_branch_b[0], w_out=m_w_out[0],
                 w_ffn_gate=m_w_ffn_gate[0], w_ffn_up=m_w_ffn_up[0], w_ffn_down=m_w_ffn_down[0],
                 w_ple_gate=m_w_ple_gate[0], w_ple_proj=m_w_ple_proj[0])
    big_v = dict(w_in=v_w_in[0], w_branch_a=v_w_branch_a[0], w_branch_b=v_w_branch_b[0], w_out=v_w_out[0],
                 w_ffn_gate=v_w_ffn_gate[0], w_ffn_up=v_w_ffn_up[0], w_ffn_down=v_w_ffn_down[0],
                 w_ple_gate=v_w_ple_gate[0], w_ple_proj=v_w_ple_proj[0])
    names = list(big)
    nf = big["w_ffn_gate"].shape[1]

    shards = {k: big[k].astype(BF16) for k in names}
    shards["conv_w"] = _pad_rows(conv_w[0], SUBLANES)
    flipped = ("w_ffn_gate", "w_ffn_up")
    for k in flipped:
        big[k], big_m[k], big_v[k] = big[k].T, big_m[k].T, big_v[k].T
    ag_groups = [["w_in"], ["w_branch_a", "w_branch_b", "w_out"], ["w_ffn_gate", "conv_w"], ["w_ffn_up"],
                 ["w_ffn_down"], ["w_ple_gate", "w_ple_proj"]]
    ag_started = {}
    wg = {}

    ag_paired, ag_passing = {}, {}

    def pair(gi, after):
        lands = [lax.dynamic_update_index_in_dim(lax.empty((N_DEV,) + shards[k].shape, shards[k].dtype), shards[k],
                                                 me, 0) for k in ag_groups[gi]]
        ag_paired[gi], token = ag_pair(f"ag_pair{gi}", lands, after)
        return token

    two_hop = (0, 4)

    def copies(gi):
        return ag_plan(len(ag_groups[gi]), [TO_X, TO_Y] if gi in two_hop else TO_CHIPS)

    def start(gi, after):
        s_sems, r_sems, _, lands, token = split_start(f"ag_start{gi}", [], ag_paired[gi], copies(gi), after)
        ag_started[gi] = (s_sems, r_sems, lands)
        return token

    def landed(gi, after):
        s_sems, r_sems, lands = ag_started[gi]
        return split_wait(f"ag_wait{gi}", s_sems, r_sems, [], lands, copies(gi), after)[1:]

    ag_relaying = {}

    def relay(gi, lands, after):
        s_sems, r_sems, _, lands, token = split_start(f"ag_relay{gi}", [], lands, relay_plan(len(lands)), after)
        ag_relaying[gi] = (s_sems, r_sems, lands)
        return token

    def relayed(gi, meanwhile):
        s_sems, r_sems, lands = ag_relaying[gi]
        return split_wait(f"ag_relayed{gi}", s_sems, r_sems, [], lands, relay_plan(len(lands)), meanwhile)[1:]

    def pass_on(gi, lands, after):
        s_sems, r_sems, _, lands, token = split_start(f"ag_pass{gi}", [], lands, pass_plan(len(lands)), after)
        ag_passing[gi] = (s_sems, r_sems, lands)
        return token

    def ready(gi, after):
        s_sems, r_sems, lands = ag_passing[gi]
        lands = split_wait(f"ag_ready{gi}", s_sems, r_sems, [], lands, pass_plan(len(lands)), after)[1]
        wg.update(zip(ag_groups[gi], lands))

    cb = conv_b.reshape(N_DEV, 1, nf)

    table_t = rel_bias_table.T
    geo_a = dict(half=A_BLOCK, q_rows=ATTN_Q_ROWS, n_chains=ATTN_CHAINS, dil=1, nh=A_Q_HEADS, group=A_GROUP,
                 cq=COL_QA, ck=COL_KA, cv=COL_VA)
    geo_b = [dict(half=B_BLOCK, q_rows=min(ATTN_Q_ROWS, s // dil), n_chains=ATTN_CHAINS, dil=dil,
                  nh=B_HEADS_PER_GROUP, group=1, cq=COL_QB + g * B_OUT_W, ck=COL_KB + g * B_OUT_W,
                  cv=COL_VB + g * B_OUT_W) for g, (_, dil) in enumerate(B_PATTERNS)]
    bucket_a = bucket_tile(geo_a["q_rows"], A_BLOCK, 1)
    bias_a = bias_build("bias_a", table_t, bucket_a, 0, A_Q_HEADS, A_BLOCK)
    buckets_b = [bucket_tile(gb["q_rows"], B_BLOCK, gb["dil"]) for gb in geo_b]
    biases_b = [bias_build(f"bias_b{g}", table_t, buckets_b[g], A_Q_HEADS + g * B_HEADS_PER_GROUP, B_HEADS_PER_GROUP,
                           B_BLOCK) for g in range(len(B_PATTERNS))]

    token = start(0, pair(0, xs))
    h = rms_fwd("rms_attn", xs, attn_norm + _token_value(token))
    lands0, token = landed(0, pair(5, pair(4, pair(3, pair(2, pair(1, h))))))
    bias_corner = bias_a[0, :1, :1] + sum(b[0, :1, :1] for b in biases_b)
    lands0, token = relayed(0, bias_corner + _token_value(relay(0, lands0, token)))
    token = start(5, start(4, start(3, start(2, start(1, token)))))
    wg["w_in"] = ag_finish("ag_finish0", lands0)[0]
    proj = mm_cols("proj_in", h, wg["w_in"], F32, fold=True, after=token)
    token = pass_on(1, landed(1, proj)[0], proj)
    sink = sink_a[0] + _token_value(token)
    ya, lse_a = band_attn_fwd("attn_a_fwd", proj, bias_a, sink, **geo_a)
    outs_b, lses_b = [], []
    for g in range(len(B_PATTERNS)):
        o, l = band_attn_fwd(f"attn_b{g}_fwd", proj, biases_b[g], None, **geo_b[g])
        outs_b.append(o)
        lses_b.append(l)
    yb = dil_merge_fwd("dil_merge_fwd", outs_b, lses_b)
    ready(1, yb)
    token = pass_on(2, landed(2, yb)[0], yb)
    w_out_full = wg["w_out"].reshape(d, d)
    ta = mm_cols("branch_a", ya, wg["w_branch_a"], F32, fold=True, after=token)
    tb = mm_cols("branch_b", yb, wg["w_branch_b"], F32, fold=True)
    merged = gate_merge_fwd("gate_merge_fwd", proj, ta, tb, d)
    x1 = mm_plain("mix_out", merged, w_out_full, F32, res=xs)

    hf = rms_fwd("rms_ffn", x1, ffn_norm)
    ready(2, hf)
    token = pass_on(3, landed(3, hf)[0], hf)
    cw = wg["conv_w"]
    gpre = mm_cols("ffn_gate", hf, wg["w_ffn_gate"], F32, fold=False, after=token)
    ready(3, gpre)
    token = relay(4, landed(4, gpre)[0], gpre)
    u = mm_cols("ffn_up", hf, wg["w_ffn_up"], F32, fold=False, after=token)
    token = pass_on(4, relayed(4, u)[0], u)
    z = ffn_mid_fwd("ffn_mid_fwd", gpre, u, cw, cb + _token_value(token))
    ready(4, z)
    token = pass_on(5, landed(5, z)[0], z)
    x2 = mm_jsum("ffn_down", z, wg["w_ffn_down"], F32, res=x1, after=token)

    hp = rms_fwd("rms_ple", x2, ple_norm)
    ready(5, hp)
    w_pg_full = wg["w_ple_gate"].reshape(d, d)
    lp = mm_plain("ple_gate", hp, w_pg_full, F32)
    pp = mm_cols("ple_proj", ps, wg["w_ple_proj"], F32, fold=True)
    loss_part, dx3, dlp, dpp, d_final = tail_fwd_bwd("tail", x2, lp, pp, final_norm.reshape(1, d), target)

    grads = {}
    rs_started = []
    blocks = own_blocks()

    exchanging = []

    def exchange(tag, keys):
        parts = [grads[k] for k in keys]
        lands = [lax.empty((N_CHIPS,) + tuple(p.shape[1:]), p.dtype) for p in parts]
        s_sems, r_sems, parts, lands, token = split_start(f"rs_pair_{tag}", parts, lands, rs_pair_plan(len(keys)), blocks)
        exchanging.append((tag, keys, s_sems, r_sems, parts, lands))
        return _token_value(token)

    def send(after):
        tag, keys, s_sems, r_sems, parts, lands = exchanging.pop(0)
        parts, got, _ = split_wait(f"rs_paired_{tag}", s_sems, r_sems, parts, lands, rs_pair_plan(len(keys)), after)
        return send_sums(tag, keys, parts, got)

    def send_sums(tag, keys, parts, got):
        sums = [pair_add(f"pair_add_{k}", blocks, p, g) for k, p, g in zip(keys, parts, got)]
        s_sems, r_sems, srcs, lands, token = rs_start(f"rs_start_{tag}", sums, blocks)
        rs_started.append((tag, keys, s_sems, r_sems, srcs, lands))
        return token

    grads["w_ple_proj"] = mm_tn_cols("d_w_ple_proj", ps, dpp, N_DEV, big["w_ple_proj"].shape[1], BF16, folded=True)
    grads["w_ple_gate"] = mm_tn_plain("d_w_ple_gate", hp, dlp, BF16).reshape(N_DEV, d // N_DEV, d)
    tok = exchange("ple", ["w_ple_proj", "w_ple_gate"])
    dhp = mm_nt_plain("d_hp", dlp, w_pg_full, F32)
    dx2, dx2_b, d_ple = rms_bwd("rms_ple_bwd", x2, ple_norm + tok, dhp, dx3, True)

    dz = mm_nt_j("d_z", dx2_b, wg["w_ffn_down"], BF16)
    grads["w_ffn_down"] = mm_tn_j("d_w_ffn_down", z, dx2_b, BF16)
    tok = _token_value(send(dz)) + exchange("down", ["w_ffn_down"])
    du, dgpre, dcw = ffn_mid_bwd("ffn_mid_bwd", gpre, u, dz, cw, cb + tok)
    grads["w_ffn_up"] = mm_tn_j("d_w_ffn_up", du, hf, BF16)
    grads["w_ffn_gate"] = mm_tn_j("d_w_ffn_gate", dgpre, hf, BF16)
    dhf = mm_nt_jsum("d_hf_up", du, wg["w_ffn_up"], F32, folded=False)
    dhf = mm_nt_jsum("d_hf_gate", dgpre, wg["w_ffn_gate"], F32, folded=False, res=dhf)
    tok = _token_value(send(dhf)) + exchange("upgate", ["w_ffn_up", "w_ffn_gate"])
    dx1, dx1_b, d_ffn = rms_bwd("rms_ffn_bwd", x1, ffn_norm + tok, dhf, dx2, True)

    dmerged = mm_nt_plain("d_merged", dx1_b, w_out_full, F32)
    grads["w_out"] = mm_tn_plain("d_w_out", merged, dx1_b, BF16).reshape(N_DEV, d // N_DEV, d)
    dta, dtb, dga, dgb = gate_merge_bwd("gate_merge_bwd", dmerged, proj, ta, tb, d)
    grads["w_branch_a"] = mm_tn_cols("d_w_branch_a", ya, dta, N_DEV, big["w_branch_a"].shape[1], BF16, folded=True)
    grads["w_branch_b"] = mm_tn_cols("d_w_branch_b", yb, dtb, N_DEV, big["w_branch_b"].shape[1], BF16, folded=True)
    dya = mm_nt_jsum("d_ya", dta, wg["w_branch_a"], F32, folded=True)
    dyb = mm_nt_jsum("d_yb", dtb, wg["w_branch_b"], F32, folded=True)
    tok = _token_value(send(dyb)) + exchange("mix", ["w_out", "w_branch_a", "w_branch_b"])
    dqa, dka, dva, dbias_a, dsink = band_attn_bwd("attn_a_bwd", proj, bias_a, sink + tok, dya, ya, lse_a, None, **geo_a)
    douts_b, dlses_b = dil_merge_bwd("dil_merge_bwd", dyb, outs_b, lses_b)
    dq_b, dk_b, dv_b, dbias_b = [], [], [], []
    for g in range(len(B_PATTERNS)):
        dq, dk, dv, db, _ = band_attn_bwd(f"attn_b{g}_bwd", proj, biases_b[g], None, douts_b[g], outs_b[g], lses_b[g],
                                          dlses_b[g], **geo_b[g])
        dq_b.append(dq)
        dk_b.append(dk)
        dv_b.append(dv)
        dbias_b.append(db)
    dproj = jnp.concatenate([t.astype(BF16) for t in [dqa, dka, dva] + dq_b + dk_b + dv_b + [dga, dgb]], axis=1)
    token = send(dproj)
    grads["w_in"] = mm_tn_cols("d_w_in", h, dproj, N_DEV, big["w_in"].shape[1], BF16, folded=True, after=token)
    token = send_sums("in", ["w_in"], [grads["w_in"]], rs_pair("rs_pair_in", [grads["w_in"]]))
    dh = mm_nt_jsum("d_h", dproj, wg["w_in"], F32, folded=True, after=token)
    grad_x, _, d_attn = rms_bwd("rms_attn_bwd", xs, attn_norm, dh, dx1, False)

    dt_a = table_grad("table_grad_a", dbias_a, bucket_a)[:, 0, :N_BUCKETS]
    dt_b = [table_grad(f"table_grad_b{g}", dbias_b[g], buckets_b[g])[:, 0, :N_BUCKETS] for g in range(len(B_PATTERNS))]
    d_table_part = jnp.concatenate([dt_a] + dt_b, axis=0).T

    pieces = [
        ("loss", loss_part[0, :1]),
        ("table", d_table_part.reshape(-1)),
        ("attn_norm", d_attn.reshape(-1)),
        ("sink", dsink[:, 0, 0]),
        ("ffn_norm", d_ffn.reshape(-1)),
        ("conv_w", dcw[:, 0:3, :].reshape(-1)),
        ("conv_b", dcw[:, 3, :].reshape(-1)),
        ("ple_norm", d_ple.reshape(-1)),
        ("final_norm", d_final.reshape(-1)),
    ]
    tiles = [_as_tiles(v) for _, v in pieces]
    pack = jnp.concatenate(tiles, axis=0)

    out_g, out_d, out_m, out_v = {}, {}, {}, {}

    def finish(group, after):
        tag, keys, s_sems, r_sems, srcs, lands = group
        srcs, lands, _ = split_wait(f"rs_wait_{tag}", s_sems, r_sems, srcs, lands, rs_plan(len(keys)), after)
        for k, mine, theirs in zip(keys, srcs, lands):
            res = reduce_adam("adam_" + k, mine, theirs, big[k], big_m[k], big_v[k])
            after = res[1]
            out_g[k], out_d[k], out_m[k], out_v[k] = [(t.T if k in flipped else t)[None] for t in res]
        return after

    after = pack
    for group in rs_started[:-1]:
        after = finish(group, after)
    total = allreduce_small("allreduce_small", pack, after)
    finish(rs_started[-1], total)
    small = {}
    row = 0
    for (nm, v), t in zip(pieces, tiles):
        small[nm] = total[row:row + t.shape[0]].reshape(-1)[:v.shape[0]]
        row += t.shape[0]
    loss = small["loss"][0]
    g_small = dict(
        rel_bias_table=small["table"].reshape(rel_bias_table.shape),
        attn_norm=small["attn_norm"].reshape(attn_norm.shape),
        sink_a=small["sink"].reshape(sink_a.shape),
        ffn_norm=small["ffn_norm"].reshape(ffn_norm.shape),
        conv_w=lax.dynamic_index_in_dim(small["conv_w"].reshape(N_DEV, 3, nf), me, 0, keepdims=False)[None],
        conv_b=small["conv_b"].reshape(conv_b.shape),
        ple_norm=small["ple_norm"].reshape(ple_norm.shape),
        final_norm=small["final_norm"].reshape(1, d),
    )
    w_small = dict(rel_bias_table=(rel_bias_table, m_rel_bias_table, v_rel_bias_table),
                   attn_norm=(attn_norm, m_attn_norm, v_attn_norm), sink_a=(sink_a, m_sink_a, v_sink_a),
                   ffn_norm=(ffn_norm, m_ffn_norm, v_ffn_norm), conv_w=(conv_w, m_conv_w, v_conv_w),
                   conv_b=(conv_b, m_conv_b, v_conv_b), ple_norm=(ple_norm, m_ple_norm, v_ple_norm),
                   final_norm=(final_norm, m_final_norm, v_final_norm))

    for k, (wv, mv, vv) in w_small.items():
        shape = wv.shape
        two_d = (1, shape[0]) if len(shape) == 1 else ((shape[0] * shape[1], shape[2]) if len(shape) == 3 else shape)
        gk = g_small[k].reshape(two_d)
        dl, nm, nv = adam_small("adam_" + k, gk, wv.reshape(two_d), mv.reshape(two_d), vv.reshape(two_d))
        out_g[k], out_d[k], out_m[k], out_v[k] = gk.reshape(shape), dl.reshape(shape), nm.reshape(shape), nv.reshape(shape)

    order = ["rel_bias_table", "attn_norm", "w_in", "sink_a", "w_branch_a", "w_branch_b", "w_out", "ffn_norm",
             "w_ffn_gate", "w_ffn_up", "conv_w", "conv_b", "w_ffn_down", "ple_norm", "w_ple_gate", "w_ple_proj",
             "final_norm"]
    return (loss, grad_x[None], *[out_g[k] for k in order], *[out_d[k] for k in order],
            *[out_m[k] for k in order], *[out_v[k] for k in order])
```

```python
import math

import jax
import jax.numpy as jnp
from jax import lax
from jax.experimental import pallas as pl
from jax.experimental.pallas import tpu as pltpu

F32 = jnp.float32
BF16 = jnp.bfloat16
MESH = pl.DeviceIdType.MESH
N_DEV = 8

HEAD_DIM = 128
A_Q_HEADS = 8
A_KV_HEADS = 2
A_GROUP = A_Q_HEADS // A_KV_HEADS
A_BLOCK = 128
B_PATTERNS = ((128, 1), (512, 4), (2048, 16))
B_HEADS_PER_GROUP = 4
B_HEADS = len(B_PATTERNS) * B_HEADS_PER_GROUP
B_BLOCK = 64
N_BUCKETS = 32
MAX_DISTANCE = 1024
A_Q_W = A_Q_HEADS * HEAD_DIM
A_KV_W = A_KV_HEADS * HEAD_DIM
B_W = B_HEADS * HEAD_DIM
B_OUT_W = B_HEADS_PER_GROUP * HEAD_DIM
COL_QA = 0
COL_KA = COL_QA + A_Q_W
COL_VA = COL_KA + A_KV_W
COL_QB = COL_VA + A_KV_W
COL_KB = COL_QB + B_W
COL_VB = COL_KB + B_W
COL_GATES = COL_VB + B_W
RMS_EPS = 1e-6
NEG_INF = -1e30
ATTN_SCALE = HEAD_DIM ** -0.5
ATTN_Q_ROWS = 256
ATTN_CHAINS = 4

ADAM_LR = 0.001
ADAM_B1 = 0.9
ADAM_B2 = 0.999
ADAM_EPS = 1e-08
ADAM_WD = 0.01
ADAM_STEP = 10

GELU_C = math.sqrt(2.0 / math.pi)
GELU_A = 0.044715

V7X_VMEM_BYTES = 64 * 1024 * 1024
VMEM_CEILING = V7X_VMEM_BYTES - 8 * 1024 * 1024
LANES = 128
SUBLANES = 8


def _pick(n, cands):
    for c in cands:
        if n % c == 0:
            return c
    return n


def _nbytes(shape, dtype):
    n = 1
    for d in shape:
        if d is not None:
            n *= d
    return n * jnp.dtype(dtype).itemsize


def _params(sem, est_bytes):
    limit = int(min(VMEM_CEILING, max(32 * 1024 * 1024, 2 * est_bytes + (8 << 20))))
    return pltpu.CompilerParams(dimension_semantics=sem, vmem_limit_bytes=limit)


def _mm(name, a, b, a_bs, a_im, b_bs, b_im, out_shape, out_dtype, o_bs, o_im, grid, dims,
        res=None, r_bs=None, r_im=None, after=None):
    nk = grid[-1]
    nax = len(grid)
    has_res = res is not None
    has_after = after is not None
    o_tile = tuple(d for d in o_bs if d is not None)

    def body(*refs):
        a_ref, b_ref = refs[:2]
        r_ref = refs[2] if has_res else None
        n_in = 2 + has_res + has_after
        o_ref = refs[n_in]
        rest = refs[n_in + 1:]

        def prod():
            return lax.dot_general(a_ref[...].astype(BF16), b_ref[...].astype(BF16), (dims, ((), ())),
                                   preferred_element_type=F32)

        def finish(r):
            if r_ref is not None:
                r = r + r_ref[...].astype(F32)
            o_ref[...] = r.astype(o_ref.dtype)

        if nk == 1:
            finish(prod())
        else:
            acc = rest[0]
            k = pl.program_id(nax - 1)

            @pl.when(k == 0)
            def _():
                acc[...] = prod()

            @pl.when(k > 0)
            def _():
                acc[...] += prod()

            @pl.when(k == nk - 1)
            def _():
                finish(acc[...])

    in_specs = [pl.BlockSpec(a_bs, a_im), pl.BlockSpec(b_bs, b_im)]
    args = [a, b]
    est = _nbytes(a_bs, a.dtype) + _nbytes(b_bs, b.dtype) + _nbytes(o_bs, out_dtype) + 2 * _nbytes(o_tile, F32)
    if has_res:
        in_specs.append(pl.BlockSpec(r_bs, r_im))
        args.append(res)
        est += _nbytes(r_bs, res.dtype)
    if has_after:
        in_specs.append(pl.BlockSpec(memory_space=pl.ANY))
        args.append(after)
    scratch = [] if nk == 1 else [pltpu.VMEM(o_tile, F32)]
    sem = ("parallel",) * (nax - 1) + ("arbitrary",)
    return pl.pallas_call(
        body, name=name, grid=grid, in_specs=in_specs, out_specs=pl.BlockSpec(o_bs, o_im),
        out_shape=pltpu.HBM(out_shape, out_dtype), scratch_shapes=scratch,
        compiler_params=_params(sem, est))(*args)


TM_CANDS = (1024, 512, 256, 128, 64, 32, 16, 8)
TM_WIDE_CANDS = (2048,) + TM_CANDS
MM_WHOLE_K_BYTES = 48 * 1024 * 1024


def _whole(k, tile_bytes):
    return k if 2 * tile_bytes(k) <= MM_WHOLE_K_BYTES else _pick(k, TK_CANDS)
TK_CANDS = (1024, 512, 256, 128)
TN_CANDS = (1024, 512, 256, 128)


def mm_cols(name, a, wg, out_dtype, fold, after=None):
    m, k = a.shape
    nj, _, n = wg.shape
    tm = _pick(m, TM_WIDE_CANDS)
    tk = _whole(k, lambda t: _nbytes((tm, t), a.dtype) + _nbytes((t, n), wg.dtype) + _nbytes((tm, n), out_dtype))
    grid = (nj, m // tm, k // tk)
    if fold:
        shape, o_bs, o_im = (m, nj * n), (tm, n), (lambda j, i, kk: (i, j))
    else:
        shape, o_bs, o_im = (nj, m, n), (None, tm, n), (lambda j, i, kk: (j, i, 0))
    return _mm(name, a, wg, (tm, tk), lambda j, i, kk: (i, kk), (None, tk, n), lambda j, i, kk: (j, kk, 0),
               shape, out_dtype, o_bs, o_im, grid, ((1,), (0,)), after=after)


def mm_plain(name, a, w, out_dtype, res=None):
    m, k = a.shape
    n = w.shape[1]
    tm, tk, tn = _pick(m, TM_CANDS if res is not None else TM_WIDE_CANDS), _pick(k, TK_CANDS), _pick(n, TN_CANDS)
    if res is None:
        tk = _whole(k, lambda t: _nbytes((tm, t), a.dtype) + _nbytes((t, tn), w.dtype) + _nbytes((tm, tn), out_dtype))
    grid = (n // tn, m // tm, k // tk)
    return _mm(name, a, w, (tm, tk), lambda j, i, kk: (i, kk), (tk, tn), lambda j, i, kk: (kk, j),
               (m, n), out_dtype, (tm, tn), lambda j, i, kk: (i, j), grid, ((1,), (0,)),
               res, (tm, tn), lambda j, i, kk: (i, j))


def mm_jsum(name, aj, wg, out_dtype, res=None, after=None):
    nj, m, ka = aj.shape
    n = wg.shape[2]
    tm, tn = _pick(m, TM_CANDS if res is not None else TM_WIDE_CANDS), _pick(n, TN_CANDS)
    grid = (m // tm, n // tn, nj)
    return _mm(name, aj, wg, (None, tm, ka), lambda i, jn, j: (j, i, 0), (None, ka, tn), lambda i, jn, j: (j, 0, jn),
               (m, n), out_dtype, (tm, tn), lambda i, jn, j: (i, jn), grid, ((1,), (0,)),
               res, (tm, tn), lambda i, jn, j: (i, jn), after=after)


def mm_tn_cols(name, a, g, nj, n, out_dtype, folded, after=None):
    s, kw = a.shape
    tkw = _pick(kw, TM_WIDE_CANDS)
    ts = _whole(s, lambda t: _nbytes((t, tkw), a.dtype) + _nbytes((t, n), g.dtype) + _nbytes((tkw, n), out_dtype))
    grid = (nj, kw // tkw, s // ts)
    if folded:
        g_bs, g_im = (ts, n), (lambda j, i, ss: (ss, j))
    else:
        g_bs, g_im = (None, ts, n), (lambda j, i, ss: (j, ss, 0))
    return _mm(name, a, g, (ts, tkw), lambda j, i, ss: (ss, i), g_bs, g_im,
               (nj, kw, n), out_dtype, (None, tkw, n), lambda j, i, ss: (j, i, 0), grid, ((0,), (0,)), after=after)


def mm_tn_plain(name, a, g, out_dtype):
    s, kw = a.shape
    n = g.shape[1]
    tkw, tn = _pick(kw, TM_WIDE_CANDS), _pick(n, TN_CANDS)
    ts = _whole(s, lambda t: _nbytes((t, tkw), a.dtype) + _nbytes((t, tn), g.dtype) + _nbytes((tkw, tn), out_dtype))
    grid = (kw // tkw, n // tn, s // ts)
    return _mm(name, a, g, (ts, tkw), lambda i, jn, ss: (ss, i), (ts, tn), lambda i, jn, ss: (ss, jn),
               (kw, n), out_dtype, (tkw, tn), lambda i, jn, ss: (i, jn), grid, ((0,), (0,)))


def mm_tn_j(name, aj, g, out_dtype):
    nj, s, ka = aj.shape
    n = g.shape[1]
    tn = _pick(n, TM_WIDE_CANDS)
    ts = _whole(s, lambda t: _nbytes((t, ka), aj.dtype) + _nbytes((t, tn), g.dtype) + _nbytes((ka, tn), out_dtype))
    grid = (nj, n // tn, s // ts)
    return _mm(name, aj, g, (None, ts, ka), lambda j, jn, ss: (j, ss, 0), (ts, tn), lambda j, jn, ss: (ss, jn),
               (nj, ka, n), out_dtype, (None, ka, tn), lambda j, jn, ss: (j, 0, jn), grid, ((0,), (0,)))


def mm_nt_plain(name, g, w, out_dtype):
    m, n = g.shape
    k = w.shape[0]
    tm, tkk = _pick(m, TM_WIDE_CANDS), _pick(k, TN_CANDS)
    tn = _whole(n, lambda t: _nbytes((tm, t), g.dtype) + _nbytes((tkk, t), w.dtype) + _nbytes((tm, tkk), out_dtype))
    grid = (k // tkk, m // tm, n // tn)
    return _mm(name, g, w, (tm, tn), lambda kk, i, jn: (i, jn), (tkk, tn), lambda kk, i, jn: (kk, jn),
               (m, k), out_dtype, (tm, tkk), lambda kk, i, jn: (i, kk), grid, ((1,), (1,)))


def mm_nt_j(name, g, wg, out_dtype):
    m, n = g.shape
    nj, ka, _ = wg.shape
    tm = _pick(m, TM_WIDE_CANDS)
    tn = _whole(n, lambda t: _nbytes((tm, t), g.dtype) + _nbytes((ka, t), wg.dtype) + _nbytes((tm, ka), out_dtype))
    grid = (nj, m // tm, n // tn)
    return _mm(name, g, wg, (tm, tn), lambda j, i, jn: (i, jn), (None, ka, tn), lambda j, i, jn: (j, 0, jn),
               (nj, m, ka), out_dtype, (None, tm, ka), lambda j, i, jn: (j, i, 0), grid, ((1,), (1,)))


def mm_nt_jsum(name, g, wg, out_dtype, folded, res=None, after=None):
    nj, k, n = wg.shape
    m = g.shape[0] if folded else g.shape[1]
    tm, tkk = _pick(m, TM_CANDS if res is not None else TM_WIDE_CANDS), _pick(k, TN_CANDS)
    grid = (m // tm, k // tkk, nj)
    if folded:
        g_bs, g_im = (tm, n), (lambda i, kk, j: (i, j))
    else:
        g_bs, g_im = (None, tm, n), (lambda i, kk, j: (j, i, 0))
    return _mm(name, g, wg, g_bs, g_im, (None, tkk, n), lambda i, kk, j: (j, kk, 0),
               (m, k), out_dtype, (tm, tkk), lambda i, kk, j: (i, kk), grid, ((1,), (1,)),
               res, (tm, tkk), lambda i, kk, j: (i, kk), after=after)


ROW_TILE_CANDS = (256, 128, 64, 32, 16, 8)


def _rstd(x):
    return lax.rsqrt(jnp.mean(x * x, axis=-1, keepdims=True) + RMS_EPS)


def _sigmoid(t):
    return 1.0 / (1.0 + jnp.exp(-t))


def rms_fwd(name, x, gain):
    s, d = x.shape
    ts = _pick(s, ROW_TILE_CANDS)

    def body(x_ref, g_ref, h_ref):
        xv = x_ref[...]
        h_ref[...] = ((xv * _rstd(xv)) * g_ref[...]).astype(h_ref.dtype)

    return pl.pallas_call(
        body, name=name, grid=(s // ts,),
        in_specs=[pl.BlockSpec((ts, d), lambda i: (i, 0)), pl.BlockSpec((1, d), lambda i: (0, 0))],
        out_specs=pl.BlockSpec((ts, d), lambda i: (i, 0)),
        out_shape=pltpu.HBM((s, d), BF16),
        compiler_params=_params(("parallel",), 3 * ts * d * 4))(x, gain)


def rms_bwd(name, x, gain, dh, dres, bf16_copy):
    s, d = x.shape
    ts = _pick(s, ROW_TILE_CANDS)

    def body(x_ref, g_ref, dh_ref, dr_ref, dx_ref, *rest):
        dxb_ref, dg_ref = rest if bf16_copy else (None, rest[0])
        xv = x_ref[...]
        r = _rstd(xv)
        xhat = xv * r
        dhv = dh_ref[...].astype(F32)
        dxhat = dhv * g_ref[...]
        dx = dr_ref[...] + r * (dxhat - xhat * jnp.mean(dxhat * xhat, axis=-1, keepdims=True))
        dx_ref[...] = dx
        if bf16_copy:
            dxb_ref[...] = dx.astype(dxb_ref.dtype)
        part = jnp.sum(dhv * xhat, axis=0, keepdims=True)

        @pl.when(pl.program_id(0) == 0)
        def _():
            dg_ref[...] = part

        @pl.when(pl.program_id(0) > 0)
        def _():
            dg_ref[...] += part

    row = pl.BlockSpec((ts, d), lambda i: (i, 0))
    vec = pl.BlockSpec((1, d), lambda i: (0, 0))
    copy_spec, copy_shape = ([row], [pltpu.HBM((s, d), BF16)]) if bf16_copy else ([], [])
    res = pl.pallas_call(
        body, name=name, grid=(s // ts,), in_specs=[row, vec, row, row], out_specs=[row] + copy_spec + [vec],
        out_shape=[pltpu.HBM((s, d), F32)] + copy_shape + [jax.ShapeDtypeStruct((1, d), F32)],
        compiler_params=_params(("arbitrary",), 7 * ts * d * 4))(x, gain, dh, dres)
    return (res[0], res[1], res[2]) if bf16_copy else (res[0], None, res[1])


def gate_merge_fwd(name, proj, ta, tb, d):
    s = proj.shape[0]
    ts = _pick(s, ROW_TILE_CANDS)
    cb = COL_GATES // d

    def body(ga_ref, gb_ref, ta_ref, tb_ref, o_ref):
        o_ref[...] = (_sigmoid(ga_ref[...]) * ta_ref[...] + _sigmoid(gb_ref[...]) * tb_ref[...]).astype(o_ref.dtype)

    row = pl.BlockSpec((ts, d), lambda i: (i, 0))
    return pl.pallas_call(
        body, name=name, grid=(s // ts,),
        in_specs=[pl.BlockSpec((ts, d), lambda i: (i, cb)), pl.BlockSpec((ts, d), lambda i: (i, cb + 1)), row, row],
        out_specs=row, out_shape=pltpu.HBM((s, d), BF16),
        compiler_params=_params(("parallel",), 5 * ts * d * 4))(proj, proj, ta, tb)


def gate_merge_bwd(name, dmerged, proj, ta, tb, d):
    s = proj.shape[0]
    ts = _pick(s, ROW_TILE_CANDS)
    cb = COL_GATES // d

    def body(dm_ref, ga_ref, gb_ref, ta_ref, tb_ref, dta_ref, dtb_ref, dga_ref, dgb_ref):
        dm = dm_ref[...]
        sa = _sigmoid(ga_ref[...])
        sb = _sigmoid(gb_ref[...])
        dta_ref[...] = (dm * sa).astype(dta_ref.dtype)
        dtb_ref[...] = (dm * sb).astype(dtb_ref.dtype)
        dga_ref[...] = (dm * ta_ref[...] * (sa * (1.0 - sa))).astype(dga_ref.dtype)
        dgb_ref[...] = (dm * tb_ref[...] * (sb * (1.0 - sb))).astype(dgb_ref.dtype)

    row = pl.BlockSpec((ts, d), lambda i: (i, 0))
    out = pltpu.HBM((s, d), BF16)
    return pl.pallas_call(
        body, name=name, grid=(s // ts,),
        in_specs=[row, pl.BlockSpec((ts, d), lambda i: (i, cb)), pl.BlockSpec((ts, d), lambda i: (i, cb + 1)), row, row],
        out_specs=[row, row, row, row], out_shape=[out, out, out, out],
        compiler_params=_params(("parallel",), 8 * ts * d * 4))(dmerged, proj, proj, ta, tb)


def tail_fwd_bwd(name, x2, lp, pp, gain, target):
    s, d = x2.shape
    ts = _pick(s, ROW_TILE_CANDS)

    def body(x2_ref, lp_ref, pp_ref, g_ref, t_ref, loss_ref, dx3_ref, dlp_ref, dpp_ref, dg_ref):
        gp = _sigmoid(lp_ref[...])
        ppv = pp_ref[...]
        x3 = x2_ref[...] + gp * ppv
        r = _rstd(x3)
        xhat = x3 * r
        gv = g_ref[...]
        err = xhat * gv - t_ref[...]
        loss = jnp.sum(err * err) * (0.5 / d)
        dy = err * (1.0 / d)
        dxhat = dy * gv
        dx3 = r * (dxhat - xhat * jnp.mean(dxhat * xhat, axis=-1, keepdims=True))
        dx3_ref[...] = dx3
        dlp_ref[...] = (dx3 * ppv * (gp * (1.0 - gp))).astype(dlp_ref.dtype)
        dpp_ref[...] = (dx3 * gp).astype(dpp_ref.dtype)
        part = jnp.sum(dy * xhat, axis=0, keepdims=True)
        lossv = jnp.full((1, LANES), loss, F32)

        @pl.when(pl.program_id(0) == 0)
        def _():
            dg_ref[...] = part
            loss_ref[...] = lossv

        @pl.when(pl.program_id(0) > 0)
        def _():
            dg_ref[...] += part
            loss_ref[...] += lossv

    row = pl.BlockSpec((ts, d), lambda i: (i, 0))
    vec = pl.BlockSpec((1, d), lambda i: (0, 0))
    return pl.pallas_call(
        body, name=name, grid=(s // ts,), in_specs=[row, row, row, vec, row],
        out_specs=[pl.BlockSpec((1, LANES), lambda i: (0, 0)), row, row, row, vec],
        out_shape=[jax.ShapeDtypeStruct((1, LANES), F32), pltpu.HBM((s, d), F32),
                   pltpu.HBM((s, d), BF16), pltpu.HBM((s, d), BF16),
                   jax.ShapeDtypeStruct((1, d), F32)],
        compiler_params=_params(("arbitrary",), 9 * ts * d * 4))(x2, lp, pp, gain, target)


HALO = SUBLANES
BF16_ROWS = 2 * SUBLANES


def _shift_rows(cur, prev_row, next_row):
    ts = cur.shape[0]
    rid = lax.broadcasted_iota(jnp.int32, cur.shape, 0)
    down = jnp.where(rid == 0, prev_row, pltpu.roll(cur, 1, 0))
    up = jnp.where(rid == ts - 1, next_row, pltpu.roll(cur, ts - 1, 0))
    return down, up


def _halo_specs(ts, s, nf, halo=HALO):
    nb = ts // halo
    last = s // halo - 1
    cur = pl.BlockSpec((None, ts, nf), lambda j, i: (j, i, 0))
    prev = pl.BlockSpec((None, halo, nf), lambda j, i: (j, jnp.maximum(i * nb - 1, 0), 0))
    nxt = pl.BlockSpec((None, halo, nf), lambda j, i: (j, jnp.minimum((i + 1) * nb, last), 0))
    return cur, prev, nxt


def _halo_rows(prev_ref, next_ref, n_tiles):
    i = pl.program_id(1)
    prev_row = jnp.where(i == 0, 0.0, prev_ref[HALO - 1:HALO, :].astype(F32))
    next_row = jnp.where(i == n_tiles - 1, 0.0, next_ref[0:1, :].astype(F32))
    return prev_row, next_row


def _gelu(g):
    t = jnp.tanh(GELU_C * (g + GELU_A * (g * g * g)))
    return 0.5 * g * (1.0 + t), t


def _conv(cur, down, up, cw_ref, cb_ref):
    return down * cw_ref[0:1, :] + cur * cw_ref[1:2, :] + up * cw_ref[2:3, :] + cb_ref[...]


def ffn_mid_fwd(name, gpre, u, cw, cb):
    nj, s, nf = gpre.shape
    ts = _pick(s, (512, 256, 128, 64, 32, 16, 8))
    n_tiles = s // ts
    cur, prev, nxt = _halo_specs(ts, s, nf)

    def body(g_ref, gp_ref, gn_ref, u_ref, cw_ref, cb_ref, z_ref):
        gv = g_ref[...]
        down, up = _shift_rows(gv, *_halo_rows(gp_ref, gn_ref, n_tiles))
        act, _ = _gelu(_conv(gv, down, up, cw_ref, cb_ref))
        z_ref[...] = (act * u_ref[...]).astype(z_ref.dtype)

    return pl.pallas_call(
        body, name=name, grid=(nj, n_tiles),
        in_specs=[cur, prev, nxt, cur, pl.BlockSpec((None, SUBLANES, nf), lambda j, i: (j, 0, 0)),
                  pl.BlockSpec((None, 1, nf), lambda j, i: (j, 0, 0))],
        out_specs=cur, out_shape=pltpu.HBM((nj, s, nf), BF16),
        compiler_params=_params(("parallel", "parallel"), 8 * ts * nf * 4))(gpre, gpre, gpre, u, cw, cb)


def _gelu_grad(g, t):
    return 0.5 * (1.0 + t) + 0.5 * g * (1.0 - t * t) * (GELU_C * (1.0 + 3.0 * GELU_A * (g * g)))


def ffn_mid_bwd(name, gpre, u, dz, cw, cb):
    nj, s, nf = gpre.shape
    ts = _pick(s, (512, 256, 128, 64, 32, 16, 8))
    n_tiles = s // ts
    cur, prev, nxt = _halo_specs(ts, s, nf)

    def body(g_ref, gp_ref, gn_ref, u_ref, up_ref, un_ref, dz_ref, dzp_ref, dzn_ref, cw_ref, cb_ref,
             du_ref, dgp_ref, dcw_ref):
        i = pl.program_id(1)
        w0, w1, w2, bias = cw_ref[0:1, :], cw_ref[1:2, :], cw_ref[2:3, :], cb_ref[...]
        gv = g_ref[...]
        down, up = _shift_rows(gv, *_halo_rows(gp_ref, gn_ref, n_tiles))
        gc = down * w0 + gv * w1 + up * w2 + bias
        act, t = _gelu(gc)
        dzv = dz_ref[...].astype(F32)
        du_ref[...] = (dzv * act).astype(du_ref.dtype)
        dg = dzv * u_ref[...] * _gelu_grad(gc, t)

        def edge_dg(g_before, g_at, g_after, u_at, dz_at):
            ge = g_before.astype(F32) * w0 + g_at.astype(F32) * w1 + g_after.astype(F32) * w2 + bias
            return dz_at.astype(F32) * u_at.astype(F32) * _gelu_grad(ge, _gelu(ge)[1])

        dz_before = dzp_ref[...].astype(F32)[BF16_ROWS - 1:BF16_ROWS, :]
        dz_after = dzn_ref[...].astype(F32)[0:1, :]
        dg_prev = jnp.where(i == 0, 0.0, edge_dg(gp_ref[HALO - 2:HALO - 1, :], gp_ref[HALO - 1:HALO, :], gv[0:1, :],
                                                 up_ref[HALO - 1:HALO, :], dz_before))
        dg_next = jnp.where(i == n_tiles - 1, 0.0, edge_dg(gv[ts - 1:ts, :], gn_ref[0:1, :], gn_ref[1:2, :],
                                                           un_ref[0:1, :], dz_after))
        dg_down, dg_up = _shift_rows(dg, dg_prev, dg_next)
        dgp_ref[...] = (dg_up * w0 + dg * w1 + dg_down * w2).astype(dgp_ref.dtype)
        rows = [jnp.sum(dg * down, axis=0, keepdims=True), jnp.sum(dg * gv, axis=0, keepdims=True),
                jnp.sum(dg * up, axis=0, keepdims=True), jnp.sum(dg, axis=0, keepdims=True)]
        part = jnp.concatenate(rows + [jnp.zeros((SUBLANES - len(rows), nf), F32)], axis=0)

        @pl.when(i == 0)
        def _():
            dcw_ref[...] = part

        @pl.when(i > 0)
        def _():
            dcw_ref[...] += part

    small = pl.BlockSpec((None, SUBLANES, nf), lambda j, i: (j, 0, 0))
    return pl.pallas_call(
        body, name=name, grid=(nj, n_tiles),
        in_specs=[cur, prev, nxt] * 2 + list(_halo_specs(ts, s, nf, BF16_ROWS))
        + [small, pl.BlockSpec((None, 1, nf), lambda j, i: (j, 0, 0))],
        out_specs=[cur, cur, small],
        out_shape=[pltpu.HBM((nj, s, nf), BF16), pltpu.HBM((nj, s, nf), BF16),
                   jax.ShapeDtypeStruct((nj, SUBLANES, nf), F32)],
        compiler_params=_params(("parallel", "arbitrary"), 14 * ts * nf * 4))(
            gpre, gpre, gpre, u, u, u, dz, dz, dz, cw, cb)


def _t5_bucket(rel):
    half = N_BUCKETS // 2
    max_exact = half // 2
    n = jnp.abs(rel)
    side = jnp.where(rel > 0, half, 0)
    nf = jnp.maximum(n, 1).astype(F32)
    large = max_exact + (jnp.log(nf / max_exact) / math.log(MAX_DISTANCE / max_exact)
                         * (half - max_exact)).astype(jnp.int32)
    large = jnp.minimum(large, half - 1)
    return side + jnp.where(n < max_exact, n, large)


def bucket_tile(rows, half, dil):
    rel = (jnp.arange(rows + 2 * half)[None, :] - half) - jnp.arange(rows)[:, None]
    return _t5_bucket(rel * dil).astype(jnp.int32)


def bias_build(name, table_t, bucket, h0, nh, half):
    blk, kw = bucket.shape

    def body(t_ref, b_ref, o_ref):
        h = pl.program_id(0)
        bv = b_ref[...]
        acc = jnp.zeros((blk, kw), F32)
        for b in range(N_BUCKETS):
            acc = jnp.where(bv == b, t_ref[h0 + h, b], acc)
        qi = lax.broadcasted_iota(jnp.int32, (blk, kw), 0)
        ci = lax.broadcasted_iota(jnp.int32, (blk, kw), 1)
        o_ref[...] = jnp.where(jnp.abs(ci - half - qi) <= half, acc, NEG_INF)

    return pl.pallas_call(
        body, name=name, grid=(nh,),
        in_specs=[pl.BlockSpec(memory_space=pltpu.SMEM), pl.BlockSpec((blk, kw), lambda h: (0, 0))],
        out_specs=pl.BlockSpec((None, blk, kw), lambda h: (h, 0, 0)),
        out_shape=jax.ShapeDtypeStruct((nh, blk, kw), F32),
        compiler_params=_params(("parallel",), 4 * blk * kw * 4))(table_t, bucket)


def table_grad(name, dbias, bucket):
    nh, blk, kw = dbias.shape

    def body(d_ref, b_ref, o_ref):
        bv = b_ref[...]
        dv = d_ref[...]
        lane = lax.broadcasted_iota(jnp.int32, (SUBLANES, LANES), 1)
        acc = jnp.zeros((SUBLANES, LANES), F32)
        for b in range(N_BUCKETS):
            acc = jnp.where(lane == b, jnp.sum(jnp.where(bv == b, dv, 0.0)), acc)
        o_ref[...] = acc

    return pl.pallas_call(
        body, name=name, grid=(nh,),
        in_specs=[pl.BlockSpec((None, blk, kw), lambda h: (h, 0, 0)), pl.BlockSpec((blk, kw), lambda h: (0, 0))],
        out_specs=pl.BlockSpec((None, SUBLANES, LANES), lambda h: (h, 0, 0)),
        out_shape=jax.ShapeDtypeStruct((nh, SUBLANES, LANES), F32),
        compiler_params=_params(("parallel",), 4 * blk * kw * 4))(dbias, bucket)


class _Band:
    def __init__(self, s, half, q_rows, n_chains, dil):
        self.s, self.half, self.dil, self.n_chains = s, half, dil, n_chains
        self.seg = s // dil
        self.q_rows = min(q_rows, self.seg)
        self.win = self.q_rows + 2 * half
        self.pad = self.seg + 2 * half
        self.nsb = self.seg // self.q_rows
        self.n_items = dil * self.nsb
        assert self.n_items % n_chains == 0 and self.seg % self.q_rows == 0
        self.staged = dil > 1

    def rows_of(self, r):
        return pl.ds(r, self.seg, stride=self.dil) if self.dil > 1 else slice(None)

    def stage_kv(self, dst, src_ref):
        zeros = jnp.zeros((self.half, HEAD_DIM), dst.dtype)
        for r in range(self.dil):
            base = r * self.pad
            dst[base:base + self.half, :] = zeros
            dst[base + self.half + self.seg:base + self.pad, :] = zeros
            dst[base + self.half:base + self.half + self.seg, :] = src_ref[self.rows_of(r), :].astype(dst.dtype)

    def stage(self, dst, src_ref):
        for r in range(self.dil):
            dst[r * self.seg:(r + 1) * self.seg, :] = src_ref[self.rows_of(r), :].astype(dst.dtype)

    def unstage(self, dst_ref, src, add=False):
        for r in range(self.dil):
            val = src[r * self.seg:(r + 1) * self.seg, :].astype(dst_ref.dtype)
            if add:
                val = val + dst_ref[self.rows_of(r), :]
            dst_ref[self.rows_of(r), :] = val

    def offsets(self, item):
        r, sb = item // self.nsb, item % self.nsb
        qoff = pl.multiple_of(r * self.seg + sb * self.q_rows, self.q_rows)
        koff = pl.multiple_of(r * self.pad + sb * self.q_rows, B_BLOCK)
        kpos = sb * self.q_rows - self.half + lax.broadcasted_iota(jnp.int32, (1, self.win), 1)
        edge = jnp.where((kpos >= 0) & (kpos < self.seg), 0.0, NEG_INF)
        return qoff, koff, edge


def band_attn_fwd(name, proj, bias, sink, *, half, q_rows, n_chains, dil, nh, group, cq, ck, cv):
    s, w = proj.shape
    g = _Band(s, half, q_rows, n_chains, dil)
    has_sink = sink is not None

    def body(*refs):
        q_ref, k_ref, v_ref, b_ref = refs[:4]
        s_ref = refs[4] if has_sink else None
        o_ref, l_ref, ks, vs = refs[4 + has_sink:8 + has_sink]
        qs, os_, ls = refs[8 + has_sink:] if g.staged else (None, o_ref, l_ref)
        g.stage_kv(ks, k_ref)
        g.stage_kv(vs, v_ref)
        if g.staged:
            g.stage(qs, q_ref)
        bias_v = b_ref[...]
        sk = s_ref[pl.program_id(0)] if has_sink else None

        def chain(item):
            qoff, koff, edge = g.offsets(item)
            rows = pl.ds(qoff, g.q_rows)
            qv = qs[rows, :] if g.staged else q_ref[rows, :].astype(BF16)
            kw_ = ks[pl.ds(koff, g.win), :]
            vw_ = vs[pl.ds(koff, g.win), :]
            sc = lax.dot_general(qv, kw_, (((1,), (1,)), ((), ())), preferred_element_type=F32) * ATTN_SCALE
            sc = sc + bias_v + edge
            m = jnp.max(sc, axis=-1, keepdims=True)
            if has_sink:
                m = jnp.maximum(m, sk)
            p = jnp.exp(sc - m)
            den = jnp.sum(p, axis=-1, keepdims=True)
            if has_sink:
                den = den + jnp.exp(sk - m)
            out = lax.dot_general(p.astype(BF16), vw_, (((1,), (0,)), ((), ())), preferred_element_type=F32)
            return rows, out / den, jnp.broadcast_to(m + jnp.log(den), (g.q_rows, HEAD_DIM))

        def step(i, carry):
            for rows, out, lse in [chain(i * n_chains + u) for u in range(n_chains)]:
                os_[rows, :] = out
                ls[rows, :] = lse
            return carry

        lax.fori_loop(0, g.n_items // n_chains, step, 0)
        if g.staged:
            g.unstage(o_ref, os_)
            g.unstage(l_ref, ls)

    def col(c0, per):
        return pl.BlockSpec((s, HEAD_DIM), lambda h: (0, c0 // LANES + h // per))

    in_specs = [col(cq, 1), col(ck, group), col(cv, group),
                pl.BlockSpec((None, g.q_rows, g.win), lambda h: (h, 0, 0))]
    args = [proj, proj, proj, bias]
    if has_sink:
        in_specs.append(pl.BlockSpec(memory_space=pltpu.SMEM))
        args.append(sink)
    shape = pltpu.HBM((s, nh * HEAD_DIM), F32)
    scratch = [pltpu.VMEM((dil * g.pad, HEAD_DIM), BF16), pltpu.VMEM((dil * g.pad, HEAD_DIM), BF16)]
    if g.staged:
        scratch += [pltpu.VMEM((s, HEAD_DIM), BF16), pltpu.VMEM((s, HEAD_DIM), F32), pltpu.VMEM((s, HEAD_DIM), F32)]
    return pl.pallas_call(
        body, name=name, grid=(nh,), in_specs=in_specs, out_specs=[col(0, 1), col(0, 1)], out_shape=[shape, shape],
        scratch_shapes=scratch, compiler_params=_params(("parallel",), 16 * s * HEAD_DIM * 4))(*args)


def band_attn_bwd(name, proj, bias, sink, dout, out, lse, dlse, *, half, q_rows, n_chains, dil, nh, group, cq, ck, cv):
    s, w = proj.shape
    g = _Band(s, half, q_rows, n_chains, dil)
    nkv = nh // group
    has_sink = sink is not None
    has_dl = dlse is not None
    n_in = 7 + int(has_sink) + int(has_dl)
    n_out = 4 + int(has_sink)

    def body(*refs):
        ins, outs, scr = refs[:n_in], refs[n_in:n_in + n_out], refs[n_in + n_out:]
        q_ref, k_ref, v_ref, b_ref, do_ref, o_ref, l_ref = ins[:7]
        s_ref = ins[7] if has_sink else None
        dl_ref = ins[n_in - 1] if has_dl else None
        dq_ref, dk_ref, dv_ref, db_ref = outs[:4]
        ks, vs, dks, dvs = scr[:4]
        scr = list(scr[4:])
        dsa = scr.pop(0) if has_sink else None
        if g.staged:
            qs, dos, os_, ls, dqs = scr[:5]
            dls = scr[5] if has_dl else None
            g.stage(qs, q_ref)
            g.stage(dos, do_ref)
            g.stage(os_, o_ref)
            g.stage(ls, l_ref)
            if has_dl:
                g.stage(dls, dl_ref)
        else:
            qs, dos, os_, ls, dqs, dls = None, do_ref, o_ref, l_ref, dq_ref, dl_ref
        h = pl.program_id(0)
        g.stage_kv(ks, k_ref)
        g.stage_kv(vs, v_ref)
        dks[...] = jnp.zeros_like(dks)
        dvs[...] = jnp.zeros_like(dvs)
        db_ref[...] = jnp.zeros_like(db_ref)
        bias_v = b_ref[...]
        if has_sink:
            sk = s_ref[h]
            dsa[...] = jnp.zeros_like(dsa)

        def chain(item):
            qoff, koff, edge = g.offsets(item)
            rows = pl.ds(qoff, g.q_rows)
            win = pl.ds(koff, g.win)
            qv = qs[rows, :] if g.staged else q_ref[rows, :].astype(BF16)
            kw_ = ks[win, :]
            vw_ = vs[win, :]
            sc = lax.dot_general(qv, kw_, (((1,), (1,)), ((), ())), preferred_element_type=F32) * ATTN_SCALE
            lv = ls[rows, :][:, 0:1]
            p = jnp.exp(sc + bias_v + edge - lv)
            dov = dos[rows, :]
            delta = jnp.sum(dov * os_[rows, :], axis=-1, keepdims=True)
            dob = dov.astype(BF16)
            dp = lax.dot_general(dob, vw_, (((1,), (1,)), ((), ())), preferred_element_type=F32)
            t = dp - delta
            if has_dl:
                t = t + dls[rows, :][:, 0:1]
            ds = p * t
            dsb = (ds * ATTN_SCALE).astype(BF16)
            dq = lax.dot_general(dsb, kw_, (((1,), (0,)), ((), ())), preferred_element_type=F32)
            dkc = lax.dot_general(dsb, qv, (((0,), (0,)), ((), ())), preferred_element_type=F32)
            dvc = lax.dot_general(p.astype(BF16), dob, (((0,), (0,)), ((), ())), preferred_element_type=F32)
            dsk = jnp.exp(sk - lv) * delta if has_sink else None
            return rows, win, dq, dkc, dvc, ds, dsk

        def step(i, carry):
            res = [chain(i * n_chains + u) for u in range(n_chains)]
            ds_sum = res[0][5]
            for rr in res[1:]:
                ds_sum = ds_sum + rr[5]
            db_ref[...] += ds_sum
            for rows, win, dq, dkc, dvc, ds, dsk in res:
                dqs[rows, :] = dq
                dks[win, :] += dkc
                dvs[win, :] += dvc
                if has_sink:
                    dsa[...] += dsk
            return carry

        lax.fori_loop(0, g.n_items // n_chains, step, 0)

        if g.staged:
            g.unstage(dq_ref, dqs)

        def emit_kv(add):
            for r in range(dil):
                lo = r * g.pad + half
                for dst_ref, src in ((dk_ref, dks), (dv_ref, dvs)):
                    val = src[lo:lo + g.seg, :]
                    if add:
                        val = val + dst_ref[g.rows_of(r), :]
                    dst_ref[g.rows_of(r), :] = val

        if group == 1:
            emit_kv(False)
        else:
            @pl.when(h % group == 0)
            def _():
                emit_kv(False)

            @pl.when(h % group != 0)
            def _():
                emit_kv(True)
        if has_sink:
            outs[4][...] = jnp.full((SUBLANES, LANES), -jnp.sum(dsa[...]), F32)

    def col(c0, per):
        return pl.BlockSpec((s, HEAD_DIM), lambda h: (0, c0 // LANES + h // per))

    b_spec = pl.BlockSpec((None, g.q_rows, g.win), lambda h: (h, 0, 0))
    in_specs = [col(cq, 1), col(ck, group), col(cv, group), b_spec, col(0, 1), col(0, 1), col(0, 1)]
    args = [proj, proj, proj, bias, dout, out, lse]
    if has_sink:
        in_specs.append(pl.BlockSpec(memory_space=pltpu.SMEM))
        args.append(sink)
    if has_dl:
        in_specs.append(col(0, 1))
        args.append(dlse)
    out_specs = [col(0, 1), col(0, group), col(0, group), b_spec]
    out_shape = [pltpu.HBM((s, nh * HEAD_DIM), F32), pltpu.HBM((s, nkv * HEAD_DIM), F32),
                 pltpu.HBM((s, nkv * HEAD_DIM), F32), jax.ShapeDtypeStruct((nh, g.q_rows, g.win), F32)]
    scratch = [pltpu.VMEM((dil * g.pad, HEAD_DIM), BF16), pltpu.VMEM((dil * g.pad, HEAD_DIM), BF16),
               pltpu.VMEM((dil * g.pad, HEAD_DIM), F32), pltpu.VMEM((dil * g.pad, HEAD_DIM), F32)]
    if has_sink:
        out_specs.append(pl.BlockSpec((None, SUBLANES, LANES), lambda h: (h, 0, 0)))
        out_shape.append(jax.ShapeDtypeStruct((nh, SUBLANES, LANES), F32))
        scratch.append(pltpu.VMEM((g.q_rows, 1), F32))
    if g.staged:
        scratch += [pltpu.VMEM((s, HEAD_DIM), BF16)] + [pltpu.VMEM((s, HEAD_DIM), F32)] * (4 + int(has_dl))
    res = pl.pallas_call(
        body, name=name, grid=(nh,), in_specs=in_specs, out_specs=out_specs, out_shape=out_shape,
        scratch_shapes=scratch, compiler_params=_params(("arbitrary",), 28 * s * HEAD_DIM * 4))(*args)
    return res[0], res[1], res[2], res[3], (res[4] if has_sink else None)


def dil_merge_fwd(name, outs, lses):
    s, w = outs[0].shape
    ts = _pick(s, ROW_TILE_CANDS)
    ng = len(outs)

    def body(*refs):
        o_refs, l_refs, y_ref = refs[:ng], refs[ng:2 * ng], refs[2 * ng]
        ls = [l[...] for l in l_refs]
        mx = ls[0]
        for l in ls[1:]:
            mx = jnp.maximum(mx, l)
        es = [jnp.exp(l - mx) for l in ls]
        tot = es[0]
        for e in es[1:]:
            tot = tot + e
        acc = (es[0] / tot) * o_refs[0][...]
        for e, o in zip(es[1:], o_refs[1:]):
            acc = acc + (e / tot) * o[...]
        y_ref[...] = acc.astype(y_ref.dtype)

    row = pl.BlockSpec((ts, w), lambda i: (i, 0))
    return pl.pallas_call(
        body, name=name, grid=(s // ts,), in_specs=[row] * (2 * ng), out_specs=row,
        out_shape=pltpu.HBM((s, w), BF16),
        compiler_params=_params(("parallel",), 10 * ts * w * 4))(*outs, *lses)


def dil_merge_bwd(name, dy, outs, lses):
    s, w = outs[0].shape
    ts = _pick(s, ROW_TILE_CANDS)
    ng = len(outs)
    nhead = w // HEAD_DIM

    def body(*refs):
        dy_ref = refs[0]
        o_refs, l_refs = refs[1:1 + ng], refs[1 + ng:1 + 2 * ng]
        do_refs, dl_refs = refs[1 + 2 * ng:1 + 3 * ng], refs[1 + 3 * ng:1 + 4 * ng]
        for hh in range(nhead):
            cols = slice(hh * HEAD_DIM, (hh + 1) * HEAD_DIM)
            dyv = dy_ref[:, cols]
            ls = [l[:, cols] for l in l_refs]
            mx = ls[0]
            for l in ls[1:]:
                mx = jnp.maximum(mx, l)
            es = [jnp.exp(l - mx) for l in ls]
            tot = es[0]
            for e in es[1:]:
                tot = tot + e
            alphas = [e / tot for e in es]
            dal = [jnp.broadcast_to(jnp.sum(dyv * o[:, cols], axis=-1, keepdims=True), dyv.shape) for o in o_refs]
            mean = alphas[0] * dal[0]
            for a, d in zip(alphas[1:], dal[1:]):
                mean = mean + a * d
            for g in range(ng):
                do_refs[g][:, cols] = alphas[g] * dyv
                dl_refs[g][:, cols] = alphas[g] * (dal[g] - mean)

    row = pl.BlockSpec((ts, w), lambda i: (i, 0))
    shape = pltpu.HBM((s, w), F32)
    res = pl.pallas_call(
        body, name=name, grid=(s // ts,), in_specs=[row] * (1 + 2 * ng), out_specs=[row] * (2 * ng),
        out_shape=[shape] * (2 * ng),
        compiler_params=_params(("parallel",), 16 * ts * w * 4))(dy, *outs, *lses)
    return res[:ng], res[ng:]


def _adamw(w, g, m, v):
    m = ADAM_B1 * m + (1.0 - ADAM_B1) * g
    v = ADAM_B2 * v + (1.0 - ADAM_B2) * (g * g)
    m_hat = m / (1.0 - ADAM_B1 ** ADAM_STEP)
    v_hat = v / (1.0 - ADAM_B2 ** ADAM_STEP)
    delta = -ADAM_LR * (m_hat / (jnp.sqrt(v_hat) + ADAM_EPS) + ADAM_WD * w)
    return delta, m, v


def _row_tile(r, c, budget=1 << 20):
    if r * c * 4 <= budget or r % SUBLANES:
        return r
    for t in (1024, 512, 256, 128, 64, 32, 16, 8):
        if r % t == 0 and t * c * 4 <= budget:
            return t
    return SUBLANES


def adam_small(name, g, w, m, v):
    def body(g_ref, w_ref, m_ref, v_ref, d_ref, nm_ref, nv_ref):
        d_ref[...], nm_ref[...], nv_ref[...] = _adamw(w_ref[...], g_ref[...], m_ref[...], v_ref[...])

    shape = jax.ShapeDtypeStruct(w.shape, F32)
    return pl.pallas_call(body, name=name, out_shape=[shape, shape, shape])(g, w, m, v)


def reduce_adam(name, mine, theirs, w, m, v):
    nq, r, c = mine.shape
    tr = _row_tile(r, c)

    def body(*refs):
        parts, (w_ref, m_ref, v_ref, g_ref, d_ref, nm_ref, nv_ref) = refs[:nq], refs[nq:]
        g = parts[0][...].astype(F32)
        for p_ref in parts[1:]:
            g = g + p_ref[...].astype(F32)
        g_ref[...] = g
        d_ref[...], nm_ref[...], nv_ref[...] = _adamw(w_ref[...], g, m_ref[...], v_ref[...])

    def slot(q):
        return pl.BlockSpec((None, tr, c), lambda i: (q, i, 0))

    row = pl.BlockSpec((tr, c), lambda i: (i, 0))
    shape = jax.ShapeDtypeStruct((r, c), F32)
    return pl.pallas_call(
        body, name=name, grid=(r // tr,), in_specs=[slot(q) for q in range(nq)] + [row, row, row],
        out_specs=[row] * 4, out_shape=[shape] * 4,
        compiler_params=_params(("parallel",), (nq * 2 + 7 * 4) * tr * c))(mine, *[theirs] * (nq - 1), *[_in_hbm(t) for t in (w, m, v)])


def _place():
    return lax.axis_index("x"), lax.axis_index("y"), lax.axis_index("c")


def _flip(pos, bits):
    return tuple((1 - p) if b else p for p, b in zip(pos, bits))


def _index(pos):
    return 4 * pos[0] + 2 * pos[1] + pos[2]


ANY = pl.BlockSpec(memory_space=pl.ANY)


HBM = pl.BlockSpec(memory_space=pltpu.HBM)
SEM = pl.BlockSpec(memory_space=pltpu.SEMAPHORE)
EFFECT = pltpu.SideEffectType.DATAFLOW_SIDE_EFFECTING
TO_SIBLING = (0, 0, 1)
TO_CHIPS = [(1, 0, 0), (0, 1, 0), (1, 1, 0)]


def _in_hbm(a):
    return pltpu.with_memory_space_constraint(a, pltpu.HBM)


def _token_value(token):
    return token[0, 0]


def _when(pred, fn):
    if pred is True:
        fn()
    elif pred is not False:
        pl.when(pred)(fn)


def _plan_copy(k, entry, ins, lnd, send_sems, recv_sems):
    a, src_a, sblk, lblk, to, send_if, recv_if = entry
    src = lnd[a] if src_a is None else ins[src_a]
    return pltpu.make_async_remote_copy(
        src_ref=src.at[sblk], dst_ref=lnd[a].at[lblk], send_sem=send_sems.at[k], recv_sem=recv_sems.at[k],
        device_id=to, device_id_type=MESH), send_if, recv_if


def split_start(name, srcs, lands, plan, after):
    ns, nl = len(srcs), len(lands)
    n_copies = len(plan((0, 0, 0)))

    def body(*refs):
        ins, lnd = refs[:ns], refs[ns:ns + nl]
        send_sems, recv_sems = refs[ns + nl + 1], refs[ns + nl + 2]
        token = refs[-1]
        for k, entry in enumerate(plan(_place())):
            cp, send_if, _ = _plan_copy(k, entry, ins, lnd, send_sems, recv_sems)
            _when(send_if, cp.start)
        token[...] = jnp.zeros_like(token)

    outs = pl.pallas_call(
        body, name=name,
        out_shape=(pltpu.SemaphoreType.DMA((n_copies,)), pltpu.SemaphoreType.DMA((n_copies,)),
                   *[pltpu.HBM(a.shape, a.dtype) for a in srcs], *[pltpu.HBM(a.shape, a.dtype) for a in lands],
                   jax.ShapeDtypeStruct((SUBLANES, LANES), F32)),
        in_specs=[HBM] * (ns + nl) + [ANY],
        out_specs=(SEM, SEM, *[HBM] * (ns + nl), pl.BlockSpec(memory_space=pltpu.VMEM)),
        input_output_aliases={i: 2 + i for i in range(ns + nl)},
        compiler_params=pltpu.CompilerParams(has_side_effects=EFFECT),
    )(*[_in_hbm(a) for a in srcs], *[_in_hbm(a) for a in lands], after)
    return outs[0], outs[1], list(outs[2:2 + ns]), list(outs[2 + ns:2 + ns + nl]), outs[-1]


def split_wait(name, send_sems, recv_sems, srcs, lands, plan, after):
    ns, nl = len(srcs), len(lands)

    def body(*refs):
        ins, lnd = refs[:ns], refs[ns:ns + nl]
        s_sems, r_sems = refs[ns + nl], refs[ns + nl + 1]
        for k, entry in enumerate(plan(_place())):
            cp, send_if, recv_if = _plan_copy(k, entry, ins, lnd, s_sems, r_sems)
            _when(send_if, cp.wait_send)
            _when(recv_if, cp.wait_recv)
        refs[-1][...] = jnp.zeros((SUBLANES, LANES), F32)

    outs = pl.pallas_call(
        body, name=name,
        out_shape=(*[pltpu.HBM(a.shape, a.dtype) for a in srcs], *[pltpu.HBM(a.shape, a.dtype) for a in lands],
                   jax.ShapeDtypeStruct((SUBLANES, LANES), F32)),
        in_specs=[HBM] * (ns + nl) + [SEM, SEM, ANY],
        out_specs=(*[HBM] * (ns + nl), pl.BlockSpec(memory_space=pltpu.VMEM)),
        input_output_aliases={i: i for i in range(ns + nl)},
        compiler_params=pltpu.CompilerParams(has_side_effects=EFFECT),
    )(*srcs, *lands, send_sems, recv_sems, after)
    return list(outs[:ns]), list(outs[ns:ns + nl]), outs[-1]


NORTH = 1


def ag_plan(n, rels=TO_CHIPS):
    def plan(me):
        x, y, c = me
        entries = []
        for a in range(n):
            for t in (NORTH, 1 - NORTH):
                blk = _index((x, y, t))
                for rel in rels:
                    entries.append((a, None, blk, blk, _flip((x, y, t), rel), c == NORTH, c == t))
        return entries
    return plan


TO_X, TO_Y = TO_CHIPS[0], TO_CHIPS[1]


def relay_plan(n):
    def plan(me):
        x, y, c = me
        entries = []
        for a in range(n):
            for t, came, goes in ((NORTH, TO_X, TO_Y), (1 - NORTH, TO_Y, TO_X)):
                blk = _index(_flip((x, y, t), came))
                entries.append((a, None, blk, blk, _flip((x, y, t), goes), c == t, c == t))
        return entries
    return plan


def ag_pair(name, lands, after):
    n = len(lands)

    def body(*refs):
        lnd = refs[n + 1:2 * n + 1]
        token = refs[2 * n + 1]
        send_sems, recv_sems = refs[2 * n + 2:]
        token[...] = jnp.zeros_like(token)
        me = _place()
        sibling = _flip(me, TO_SIBLING)
        copies = []
        for a in range(n):
            mine, theirs = lnd[a].at[_index(me)], lnd[a].at[_index(sibling)]
            cp = pltpu.make_async_remote_copy(src_ref=mine, dst_ref=mine, send_sem=send_sems.at[a],
                                              recv_sem=recv_sems.at[a], device_id=sibling, device_id_type=MESH)
            cp.start()
            copies.append((cp, pltpu.make_async_remote_copy(
                src_ref=mine, dst_ref=theirs, send_sem=send_sems.at[a], recv_sem=recv_sems.at[a], device_id=sibling,
                device_id_type=MESH)))
        for cp, arrival in copies:
            arrival.wait_recv()
        for cp, arrival in copies:
            cp.wait_send()

    outs = pl.pallas_call(
        body, name=name, in_specs=[ANY] * (n + 1), out_specs=[ANY] * n + [pl.BlockSpec(memory_space=pltpu.VMEM)],
        out_shape=[jax.ShapeDtypeStruct(l.shape, l.dtype) for l in lands]
        + [jax.ShapeDtypeStruct((SUBLANES, LANES), F32)],
        input_output_aliases={a: a for a in range(n)},
        scratch_shapes=[pltpu.SemaphoreType.DMA((n,)), pltpu.SemaphoreType.DMA((n,))],
    )(*lands, after)
    return list(outs[:n]), outs[n]


def pass_plan(n):
    def plan(me):
        sibling = _flip(me, TO_SIBLING)
        return [(a, None, _index(_flip(me, rel)), _index(_flip(me, rel)), sibling, True, True)
                for a in range(n) for rel in TO_CHIPS]
    return plan


def ag_finish(name, lands):
    n = len(lands)

    def body(*refs):
        lnd = refs[n:2 * n]
        send_sems, recv_sems = refs[2 * n:]
        me = _place()
        sibling = _flip(me, TO_SIBLING)
        copies = []
        for a in range(n):
            for j, rel in enumerate(TO_CHIPS):
                blk = lnd[a].at[_index(_flip(me, rel))]
                there = lnd[a].at[_index(_flip(sibling, rel))]
                cp = pltpu.make_async_remote_copy(
                    src_ref=blk, dst_ref=blk, send_sem=send_sems.at[a * 3 + j], recv_sem=recv_sems.at[a * 3 + j],
                    device_id=sibling, device_id_type=MESH)
                cp.start()
                copies.append((cp, pltpu.make_async_remote_copy(
                    src_ref=blk, dst_ref=there, send_sem=send_sems.at[a * 3 + j], recv_sem=recv_sems.at[a * 3 + j],
                    device_id=sibling, device_id_type=MESH)))
        for cp, arrival in copies:
            arrival.wait_recv()
        for cp, arrival in copies:
            cp.wait_send()

    return pl.pallas_call(
        body, name=name, in_specs=[ANY] * n, out_specs=[ANY] * n,
        out_shape=[jax.ShapeDtypeStruct(l.shape, l.dtype) for l in lands],
        input_output_aliases={a: a for a in range(n)},
        scratch_shapes=[pltpu.SemaphoreType.DMA((3 * n,)), pltpu.SemaphoreType.DMA((3 * n,))],
    )(*lands)


REL = [(b >> 2 & 1, b >> 1 & 1, b & 1) for b in range(N_DEV)]


CHIP_REL = [(0, 0, 0)] + TO_CHIPS
N_CHIPS = len(CHIP_REL)


def rs_pair(name, parts):
    n = len(parts)

    def body(*refs):
        ins, got = refs[:n], refs[n:2 * n]
        send_sems, recv_sems = refs[2 * n:]
        me = _place()
        sibling = _flip(me, TO_SIBLING)
        remote = []
        for a in range(n):
            for q, rel in enumerate(CHIP_REL):
                k = a * N_CHIPS + q
                cp = pltpu.make_async_remote_copy(
                    src_ref=ins[a].at[_index(_flip(sibling, rel))], dst_ref=got[a].at[q], send_sem=send_sems.at[k],
                    recv_sem=recv_sems.at[k], device_id=sibling, device_id_type=MESH)
                cp.start()
                remote.append(cp)
        for cp in remote:
            cp.wait_recv()
        for cp in remote:
            cp.wait_send()

    shapes = [jax.ShapeDtypeStruct((N_CHIPS,) + tuple(p.shape[1:]), p.dtype) for p in parts]
    res = pl.pallas_call(
        body, name=name, in_specs=[ANY] * n, out_specs=[ANY] * n, out_shape=shapes,
        scratch_shapes=[pltpu.SemaphoreType.DMA((N_CHIPS * n,)), pltpu.SemaphoreType.DMA((N_CHIPS * n,))],
    )(*parts)
    return list(res)


def own_blocks():
    me = _place()
    return jnp.stack([_index(_flip(me, rel)) for rel in CHIP_REL]).astype(jnp.int32)


def pair_add(name, blocks, parts, got):
    nq, r, c = got.shape
    tr = _row_tile(r, c, budget=6 << 20)

    def body(blk_ref, a_ref, b_ref, o_ref):
        o_ref[...] = (a_ref[...].astype(F32) + b_ref[...].astype(F32)).astype(o_ref.dtype)

    spec = pl.BlockSpec((None, tr, c), lambda q, i, blk: (q, i, 0))
    return pl.pallas_call(
        body, name=name,
        grid_spec=pltpu.PrefetchScalarGridSpec(
            num_scalar_prefetch=1, grid=(nq, r // tr),
            in_specs=[pl.BlockSpec((None, tr, c), lambda q, i, blk: (blk[q], i, 0)), spec], out_specs=spec),
        out_shape=pltpu.HBM(got.shape, got.dtype),
        compiler_params=_params(("arbitrary", "arbitrary"), 6 * tr * c * 2))(blocks, parts, got)


def rs_pair_plan(n):
    def plan(me):
        sibling = _flip(me, TO_SIBLING)
        return [(a, a, _index(_flip(sibling, rel)), q, sibling, True, True)
                for a in range(n) for q, rel in enumerate(CHIP_REL)]
    return plan


def rs_plan(n):
    def plan(me):
        return [(a, a, q, q, _flip(me, CHIP_REL[q]), True, True) for a in range(n) for q in range(1, N_CHIPS)]
    return plan


def rs_start(name, sums, after):
    lands = [lax.empty(t.shape, t.dtype) for t in sums]
    return split_start(name, sums, lands, rs_plan(len(sums)), after)


def allreduce_small(name, pack, after):
    rows, lanes = pack.shape

    def body(x_ref, after_ref, o_ref, land, send_sems, recv_sems):
        me = _place()
        idx = _index(me)
        land[idx] = x_ref[...]
        copies = []
        for r in range(1, N_DEV):
            peer = _flip(me, REL[r])
            cp = pltpu.make_async_remote_copy(
                src_ref=x_ref, dst_ref=land.at[idx], send_sem=send_sems.at[r - 1], recv_sem=recv_sems.at[r - 1],
                device_id=peer, device_id_type=MESH)
            cp.start()
            copies.append(cp)
        for cp in copies:
            cp.wait_recv()
        for cp in copies:
            cp.wait_send()
        acc = land[0]
        for i in range(1, N_DEV):
            acc = acc + land[i]
        o_ref[...] = acc

    return pl.pallas_call(
        body, name=name, in_specs=[pl.BlockSpec(memory_space=pltpu.VMEM), ANY],
        out_specs=pl.BlockSpec(memory_space=pltpu.VMEM), out_shape=jax.ShapeDtypeStruct((rows, lanes), F32),
        scratch_shapes=[pltpu.VMEM((N_DEV, rows, lanes), F32), pltpu.SemaphoreType.DMA((7,)),
                        pltpu.SemaphoreType.DMA((7,))],
    )(pack, after)


def _pad_rows(a, rows):
    return jnp.pad(a, ((0, rows - a.shape[0]), (0, 0)))


def _as_tiles(vec):
    n = vec.shape[0]
    rows = -(-n // LANES)
    rows = -(-rows // SUBLANES) * SUBLANES
    return jnp.pad(vec, (0, rows * LANES - n)).reshape(rows, LANES)


def kernel(x, p, rel_bias_table, attn_norm, w_in, sink_a, w_branch_a, w_branch_b, w_out, ffn_norm, w_ffn_gate, w_ffn_up, conv_w, conv_b, w_ffn_down, ple_norm, w_ple_gate, w_ple_proj, final_norm, loss_target, m_rel_bias_table, m_attn_norm, m_w_in, m_sink_a, m_w_branch_a, m_w_branch_b, m_w_out, m_ffn_norm, m_w_ffn_gate, m_w_ffn_up, m_conv_w, m_conv_b, m_w_ffn_down, m_ple_norm, m_w_ple_gate, m_w_ple_proj, m_final_norm, v_rel_bias_table, v_attn_norm, v_w_in, v_sink_a, v_w_branch_a, v_w_branch_b, v_w_out, v_ffn_norm, v_w_ffn_gate, v_w_ffn_up, v_conv_w, v_conv_b, v_w_ffn_down, v_ple_norm, v_w_ple_gate, v_w_ple_proj, v_final_norm):
    xs = x[0]
    s, d = xs.shape
    ps = p[0, 0]
    target = loss_target[0]
    me = 4 * lax.axis_index("x") + 2 * lax.axis_index("y") + lax.axis_index("c")

    big = dict(w_in=w_in[0], w_branch_a=w_branch_a[0], w_branch_b=w_branch_b[0], w_out=w_out[0],
               w_ffn_gate=w_ffn_gate[0], w_ffn_up=w_ffn_up[0], w_ffn_down=w_ffn_down[0],
               w_ple_gate=w_ple_gate[0], w_ple_proj=w_ple_proj[0])
    big_m = dict(w_in=m_w_in[0], w_branch_a=m_w_branch_a[0], w_branch_b=m_w_branch_b[0], w_out=m_w_out[0],
                 w_ffn_gate=m_w_ffn_gate[0], w_ffn_up=m_w_ffn_up[0], w_ffn_down=m_w_ffn_down[0],
                 w_ple_gate=m_w_ple_gate[0], w_ple_proj=m_w_ple_proj[0])
    big_v = dict(w_in=v_w_in[0], w_branch_a=v_w_branch_a[0], w_branch_b=v_w_branch_b[0], w_out=v_w_out[0],
                 w_ffn_gate=v_w_ffn_gate[0], w_ffn_up=v_w_ffn_up[0], w_ffn_down=v_w_ffn_down[0],
                 w_ple_gate=v_w_ple_gate[0], w_ple_proj=v_w_ple_proj[0])
    names = list(big)
    nf = big["w_ffn_gate"].shape[1]

    shards = {k: big[k].astype(BF16) for k in names}
    shards["conv_w"] = _pad_rows(conv_w[0], SUBLANES)
    flipped = ("w_ffn_gate", "w_ffn_up")
    for k in flipped:
        big[k], big_m[k], big_v[k] = big[k].T, big_m[k].T, big_v[k].T
    ag_groups = [["w_in"], ["w_branch_a", "w_branch_b", "w_out"], ["w_ffn_gate", "conv_w"], ["w_ffn_up"],
                 ["w_ffn_down"], ["w_ple_gate", "w_ple_proj"]]
    ag_started = {}
    wg = {}

    ag_paired, ag_passing = {}, {}

    def pair(gi, after):
        lands = [lax.dynamic_update_index_in_dim(lax.empty((N_DEV,) + shards[k].shape, shards[k].dtype), shards[k],
                                                 me, 0) for k in ag_groups[gi]]
        ag_paired[gi], token = ag_pair(f"ag_pair{gi}", lands, after)
        return token

    two_hop = (0, 4)

    def copies(gi):
        return ag_plan(len(ag_groups[gi]), [TO_X, TO_Y] if gi in two_hop else TO_CHIPS)

    def start(gi, after):
        s_sems, r_sems, _, lands, token = split_start(f"ag_start{gi}", [], ag_paired[gi], copies(gi), after)
        ag_started[gi] = (s_sems, r_sems, lands)
        return token

    def landed(gi, after):
        s_sems, r_sems, lands = ag_started[gi]
        return split_wait(f"ag_wait{gi}", s_sems, r_sems, [], lands, copies(gi), after)[1:]

    ag_relaying = {}

    def relay(gi, lands, after):
        s_sems, r_sems, _, lands, token = split_start(f"ag_relay{gi}", [], lands, relay_plan(len(lands)), after)
        ag_relaying[gi] = (s_sems, r_sems, lands)
        return token

    def relayed(gi, meanwhile):
        s_sems, r_sems, lands = ag_relaying[gi]
        return split_wait(f"ag_relayed{gi}", s_sems, r_sems, [], lands, relay_plan(len(lands)), meanwhile)[1:]

    def pass_on(gi, lands, after):
        s_sems, r_sems, _, lands, token = split_start(f"ag_pass{gi}", [], lands, pass_plan(len(lands)), after)
        ag_passing[gi] = (s_sems, r_sems, lands)
        return token

    def ready(gi, after):
        s_sems, r_sems, lands = ag_passing[gi]
        lands = split_wait(f"ag_ready{gi}", s_sems, r_sems, [], lands, pass_plan(len(lands)), after)[1]
        wg.update(zip(ag_groups[gi], lands))

    cb = conv_b.reshape(N_DEV, 1, nf)

    table_t = rel_bias_table.T
    geo_a = dict(half=A_BLOCK, q_rows=ATTN_Q_ROWS, n_chains=ATTN_CHAINS, dil=1, nh=A_Q_HEADS, group=A_GROUP,
                 cq=COL_QA, ck=COL_KA, cv=COL_VA)
    geo_b = [dict(half=B_BLOCK, q_rows=min(ATTN_Q_ROWS, s // dil), n_chains=ATTN_CHAINS, dil=dil,
                  nh=B_HEADS_PER_GROUP, group=1, cq=COL_QB + g * B_OUT_W, ck=COL_KB + g * B_OUT_W,
                  cv=COL_VB + g * B_OUT_W) for g, (_, dil) in enumerate(B_PATTERNS)]
    bucket_a = bucket_tile(geo_a["q_rows"], A_BLOCK, 1)
    bias_a = bias_build("bias_a", table_t, bucket_a, 0, A_Q_HEADS, A_BLOCK)
    buckets_b = [bucket_tile(gb["q_rows"], B_BLOCK, gb["dil"]) for gb in geo_b]
    biases_b = [bias_build(f"bias_b{g}", table_t, buckets_b[g], A_Q_HEADS + g * B_HEADS_PER_GROUP, B_HEADS_PER_GROUP,
                           B_BLOCK) for g in range(len(B_PATTERNS))]

    token = start(0, pair(0, xs))
    h = rms_fwd("rms_attn", xs, attn_norm + _token_value(token))
    lands0, token = landed(0, pair(5, pair(4, pair(3, pair(2, pair(1, h))))))
    bias_corner = bias_a[0, :1, :1] + sum(b[0, :1, :1] for b in biases_b)
    lands0, token = relayed(0, bias_corner + _token_value(relay(0, lands0, token)))
    token = start(5, start(4, start(3, start(2, start(1, token)))))
    wg["w_in"] = ag_finish("ag_finish0", lands0)[0]
    proj = mm_cols("proj_in", h, wg["w_in"], F32, fold=True, after=token)
    token = pass_on(1, landed(1, proj)[0], proj)
    sink = sink_a[0] + _token_value(token)
    ya, lse_a = band_attn_fwd("attn_a_fwd", proj, bias_a, sink, **geo_a)
    outs_b, lses_b = [], []
    for g in range(len(B_PATTERNS)):
        o, l = band_attn_fwd(f"attn_b{g}_fwd", proj, biases_b[g], None, **geo_b[g])
        outs_b.append(o)
        lses_b.append(l)
    yb = dil_merge_fwd("dil_merge_fwd", outs_b, lses_b)
    ready(1, yb)
    token = pass_on(2, landed(2, yb)[0], yb)
    w_out_full = wg["w_out"].reshape(d, d)
    ta = mm_cols("branch_a", ya, wg["w_branch_a"], F32, fold=True, after=token)
    tb = mm_cols("branch_b", yb, wg["w_branch_b"], F32, fold=True)
    merged = gate_merge_fwd("gate_merge_fwd", proj, ta, tb, d)
    x1 = mm_plain("mix_out", merged, w_out_full, F32, res=xs)

    hf = rms_fwd("rms_ffn", x1, ffn_norm)
    ready(2, hf)
    token = pass_on(3, landed(3, hf)[0], hf)
    cw = wg["conv_w"]
    gpre = mm_cols("ffn_gate", hf, wg["w_ffn_gate"], F32, fold=False, after=token)
    ready(3, gpre)
    token = relay(4, landed(4, gpre)[0], gpre)
    u = mm_cols("ffn_up", hf, wg["w_ffn_up"], F32, fold=False, after=token)
    token = pass_on(4, relayed(4, u)[0], u)
    z = ffn_mid_fwd("ffn_mid_fwd", gpre, u, cw, cb + _token_value(token))
    ready(4, z)
    token = pass_on(5, landed(5, z)[0], z)
    x2 = mm_jsum("ffn_down", z, wg["w_ffn_down"], F32, res=x1, after=token)

    hp = rms_fwd("rms_ple", x2, ple_norm)
    ready(5, hp)
    w_pg_full = wg["w_ple_gate"].reshape(d, d)
    lp = mm_plain("ple_gate", hp, w_pg_full, F32)
    pp = mm_cols("ple_proj", ps, wg["w_ple_proj"], F32, fold=True)
    loss_part, dx3, dlp, dpp, d_final = tail_fwd_bwd("tail", x2, lp, pp, final_norm.reshape(1, d), target)

    grads = {}
    rs_started = []
    blocks = own_blocks()

    exchanging = []

    def exchange(tag, keys):
        parts = [grads[k] for k in keys]
        lands = [lax.empty((N_CHIPS,) + tuple(p.shape[1:]), p.dtype) for p in parts]
        s_sems, r_sems, parts, lands, token = split_start(f"rs_pair_{tag}", parts, lands, rs_pair_plan(len(keys)), blocks)
        exchanging.append((tag, keys, s_sems, r_sems, parts, lands))
        return _token_value(token)

    def send(after):
        tag, keys, s_sems, r_sems, parts, lands = exchanging.pop(0)
        parts, got, _ = split_wait(f"rs_paired_{tag}", s_sems, r_sems, parts, lands, rs_pair_plan(len(keys)), after)
        return send_sums(tag, keys, parts, got)

    def send_sums(tag, keys, parts, got):
        sums = [pair_add(f"pair_add_{k}", blocks, p, g) for k, p, g in zip(keys, parts, got)]
        s_sems, r_sems, srcs, lands, token = rs_start(f"rs_start_{tag}", sums, blocks)
        rs_started.append((tag, keys, s_sems, r_sems, srcs, lands))
        return token

    grads["w_ple_proj"] = mm_tn_cols("d_w_ple_proj", ps, dpp, N_DEV, big["w_ple_proj"].shape[1], BF16, folded=True)
    grads["w_ple_gate"] = mm_tn_plain("d_w_ple_gate", hp, dlp, BF16).reshape(N_DEV, d // N_DEV, d)
    tok = exchange("ple", ["w_ple_proj", "w_ple_gate"])
    dhp = mm_nt_plain("d_hp", dlp, w_pg_full, F32)
    dx2, dx2_b, d_ple = rms_bwd("rms_ple_bwd", x2, ple_norm + tok, dhp, dx3, True)

    dz = mm_nt_j("d_z", dx2_b, wg["w_ffn_down"], BF16)
    grads["w_ffn_down"] = mm_tn_j("d_w_ffn_down", z, dx2_b, BF16)
    tok = _token_value(send(dz)) + exchange("down", ["w_ffn_down"])
    du, dgpre, dcw = ffn_mid_bwd("ffn_mid_bwd", gpre, u, dz, cw, cb + tok)
    grads["w_ffn_up"] = mm_tn_j("d_w_ffn_up", du, hf, BF16)
    grads["w_ffn_gate"] = mm_tn_j("d_w_ffn_gate", dgpre, hf, BF16)
    dhf = mm_nt_jsum("d_hf_up", du, wg["w_ffn_up"], F32, folded=False)
    dhf = mm_nt_jsum("d_hf_gate", dgpre, wg["w_ffn_gate"], F32, folded=False, res=dhf)
    tok = _token_value(send(dhf)) + exchange("upgate", ["w_ffn_up", "w_ffn_gate"])
    dx1, dx1_b, d_ffn = rms_bwd("rms_ffn_bwd", x1, ffn_norm + tok, dhf, dx2, True)

    dmerged = mm_nt_plain("d_merged", dx1_b, w_out_full, F32)
    grads["w_out"] = mm_tn_plain("d_w_out", merged, dx1_b, BF16).reshape(N_DEV, d // N_DEV, d)
    dta, dtb, dga, dgb = gate_merge_bwd("gate_merge_bwd", dmerged, proj, ta, tb, d)
    grads["w_branch_a"] = mm_tn_cols("d_w_branch_a", ya, dta, N_DEV, big["w_branch_a"].shape[1], BF16, folded=True)
    grads["w_branch_b"] = mm_tn_cols("d_w_branch_b", yb, dtb, N_DEV, big["w_branch_b"].shape[1], BF16, folded=True)
    dya = mm_nt_jsum("d_ya", dta, wg["w_branch_a"], F32, folded=True)
    dyb = mm_nt_jsum("d_yb", dtb, wg["w_branch_b"], F32, folded=True)
    tok = _token_value(send(dyb)) + exchange("mix", ["w_out", "w_branch_a", "w_branch_b"])
    dqa, dka, dva, dbias_a, dsink = band_attn_bwd("attn_a_bwd", proj, bias_a, sink + tok, dya, ya, lse_a, None, **geo_a)
    douts_b, dlses_b = dil_merge_bwd("dil_merge_bwd", dyb, outs_b, lses_b)
    dq_b, dk_b, dv_b, dbias_b = [], [], [], []
    for g in range(len(B_PATTERNS)):
        dq, dk, dv, db, _ = band_attn_bwd(f"attn_b{g}_bwd", proj, biases_b[g], None, douts_b[g], outs_b[g], lses_b[g],
                                          dlses_b[g], **geo_b[g])
        dq_b.append(dq)
        dk_b.append(dk)
        dv_b.append(dv)
        dbias_b.append(db)
    dproj = jnp.concatenate([t.astype(BF16) for t in [dqa, dka, dva] + dq_b + dk_b + dv_b + [dga, dgb]], axis=1)
    token = send(dproj)
    grads["w_in"] = mm_tn_cols("d_w_in", h, dproj, N_DEV, big["w_in"].shape[1], BF16, folded=True, after=token)
    token = send_sums("in", ["w_in"], [grads["w_in"]], rs_pair("rs_pair_in", [grads["w_in"]]))
    dh = mm_nt_jsum("d_h", dproj, wg["w_in"], F32, folded=True, after=token)
    grad_x, _, d_attn = rms_bwd("rms_attn_bwd", xs, attn_norm, dh, dx1, False)

    dt_a = table_grad("table_grad_a", dbias_a, bucket_a)[:, 0, :N_BUCKETS]
    dt_b = [table_grad(f"table_grad_b{g}", dbias_b[g], buckets_b[g])[:, 0, :N_BUCKETS] for g in range(len(B_PATTERNS))]
    d_table_part = jnp.concatenate([dt_a] + dt_b, axis=0).T

    pieces = [
        ("loss", loss_part[0, :1]),
        ("table", d_table_part.reshape(-1)),
        ("attn_norm", d_attn.reshape(-1)),
        ("sink", dsink[:, 0, 0]),
        ("ffn_norm", d_ffn.reshape(-1)),
        ("conv_w", dcw[:, 0:3, :].reshape(-1)),
        ("conv_b", dcw[:, 3, :].reshape(-1)),
        ("ple_norm", d_ple.reshape(-1)),
        ("final_norm", d_final.reshape(-1)),
    ]
    tiles = [_as_tiles(v) for _, v in pieces]
    pack = jnp.concatenate(tiles, axis=0)

    out_g, out_d, out_m, out_v = {}, {}, {}, {}

    def finish(group, after):
        tag, keys, s_sems, r_sems, srcs, lands = group
        srcs, lands, _ = split_wait(f"rs_wait_{tag}", s_sems, r_sems, srcs, lands, rs_plan(len(keys)), after)
        for k, mine, theirs in zip(keys, srcs, lands):
            res = reduce_adam("adam_" + k, mine, theirs, big[k], big_m[k], big_v[k])
            after = res[1]
            out_g[k], out_d[k], out_m[k], out_v[k] = [(t.T if k in flipped else t)[None] for t in res]
        return after

    after = pack
    for group in rs_started[:-1]:
        after = finish(group, after)
    total = allreduce_small("allreduce_small", pack, after)
    finish(rs_started[-1], total)
    small = {}
    row = 0
    for (nm, v), t in zip(pieces, tiles):
        small[nm] = total[row:row + t.shape[0]].reshape(-1)[:v.shape[0]]
        row += t.shape[0]
    loss = small["loss"][0]
    g_small = dict(
        rel_bias_table=small["table"].reshape(rel_bias_table.shape),
        attn_norm=small["attn_norm"].reshape(attn_norm.shape),
        sink_a=small["sink"].reshape(sink_a.shape),
        ffn_norm=small["ffn_norm"].reshape(ffn_norm.shape),
        conv_w=lax.dynamic_index_in_dim(small["conv_w"].reshape(N_DEV, 3, nf), me, 0, keepdims=False)[None],
        conv_b=small["conv_b"].reshape(conv_b.shape),
        ple_norm=small["ple_norm"].reshape(ple_norm.shape),
        final_norm=small["final_norm"].reshape(1, d),
    )
    w_small = dict(rel_bias_table=(rel_bias_table, m_rel_bias_table, v_rel_bias_table),
                   attn_norm=(attn_norm, m_attn_norm, v_attn_norm), sink_a=(sink_a, m_sink_a, v_sink_a),
                   ffn_norm=(ffn_norm, m_ffn_norm, v_ffn_norm), conv_w=(conv_w, m_conv_w, v_conv_w),
                   conv_b=(conv_b, m_conv_b, v_conv_b), ple_norm=(ple_norm, m_ple_norm, v_ple_norm),
                   final_norm=(final_norm, m_final_norm, v_final_norm))

    for k, (wv, mv, vv) in w_small.items():
        shape = wv.shape
        two_d = (1, shape[0]) if len(shape) == 1 else ((shape[0] * shape[1], shape[2]) if len(shape) == 3 else shape)
        gk = g_small[k].reshape(two_d)
        dl, nm, nv = adam_small("adam_" + k, gk, wv.reshape(two_d), mv.reshape(two_d), vv.reshape(two_d))
        out_g[k], out_d[k], out_m[k], out_v[k] = gk.reshape(shape), dl.reshape(shape), nm.reshape(shape), nv.reshape(shape)

    order = ["rel_bias_table", "attn_norm", "w_in", "sink_a", "w_branch_a", "w_branch_b", "w_out", "ffn_norm",
             "w_ffn_gate", "w_ffn_up", "conv_w", "conv_b", "w_ffn_down", "ple_norm", "w_ple_gate", "w_ple_proj",
             "final_norm"]
    return (loss, grad_x[None], *[out_g[k] for k in order], *[out_d[k] for k in order],
            *[out_m[k] for k in order], *[out_v[k] for k in order])
```
